```python
import jax
import jax.numpy as jnp
from jax import lax
import numpy as np

D_MODEL = 1024
BATCH = 8
SEQ = 8192
DEPTH = 1

GRID_W = 64
CTX_LEN = 256
EPS = 1e-6

A_WIDTH = 512
A_GROUPS = 4
A_GROUP_DIM = A_WIDTH // A_GROUPS
A_CHUNK = 128
A_ROW_GROUPS = 2

B_HEADS = 4
B_DK = 64
B_DV = 128
B_KEY_WIDTH = B_HEADS * B_DK
B_VAL_WIDTH = B_HEADS * B_DV
B_GATE_RANK = 16
B_GATE_TAU = 16.0
B_CHUNK = 64

Q0 = 0
K0 = Q0 + B_KEY_WIDTH
V0 = K0 + B_KEY_WIDTH
LR0 = V0 + B_VAL_WIDTH
ZB0 = LR0 + 2 * B_GATE_RANK
UA0 = ZB0 + B_VAL_WIDTH
VA0 = UA0 + A_WIDTH
ZA0 = VA0 + A_WIDTH
G0 = ZA0 + A_WIDTH
IN_WIDTH = G0 + 2 * D_MODEL

kernel_name = "hybrid_gmlp_gla_prefix_block"


def rmsnorm(x, g):
    xf = x.astype(jnp.float32)
    y = xf * lax.rsqrt(jnp.mean(xf * xf, axis=-1, keepdims=True) + EPS)
    return (y * g.astype(jnp.float32)).astype(x.dtype)


def layernorm(x, g, b):
    xf = x.astype(jnp.float32)
    xc = xf - jnp.mean(xf, axis=-1, keepdims=True)
    y = xc * lax.rsqrt(jnp.mean(xc * xc, axis=-1, keepdims=True) + EPS)
    return (y * g.astype(jnp.float32) + b.astype(jnp.float32)).astype(x.dtype)


def spatial_mix(vg, ws, bs):
    b, l, g, dg = vg.shape
    vr = vg.reshape(b, l // A_CHUNK, A_CHUNK, g, dg)
    s = jnp.einsum('gij,bnjgc->bnigc', ws, vr) + bs.T[None, None, :, :, None]
    return s.reshape(b, l, g, dg)


def to_colmajor(t, rows):
    b, l, g, dg = t.shape
    return t.reshape(b, rows, GRID_W, g, dg).transpose(0, 2, 1, 3, 4).reshape(b, l, g, dg)


def from_colmajor(t, rows):
    b, l, g, dg = t.shape
    return t.reshape(b, GRID_W, rows, g, dg).transpose(0, 2, 1, 3, 4).reshape(b, l, g, dg)


def chunk_mlp_branch(p, rows, ln_g, ln_b, ws, bs, w_proj):
    b, l, _ = p.shape
    u = p[..., UA0:VA0]
    z = p[..., ZA0:G0]
    vn = layernorm(p[..., VA0:ZA0], ln_g, ln_b).reshape(b, l, A_GROUPS, A_GROUP_DIM)
    if rows is None:
        sv = spatial_mix(vn, ws, bs)
    else:
        r = A_ROW_GROUPS
        sv_row = spatial_mix(vn[:, :, :r], ws[:r], bs[:r])
        sv_col = from_colmajor(spatial_mix(to_colmajor(vn[:, :, r:], rows), ws[r:], bs[r:]), rows)
        sv = jnp.concatenate([sv_row, sv_col], axis=2)
    return (u * sv.reshape(b, l, A_WIDTH) * jax.nn.silu(z)) @ w_proj


def to_chunks(t):
    b, l, h, d = t.shape
    return t.reshape(b, l // B_CHUNK, B_CHUNK, h, d).transpose(0, 3, 1, 2, 4)


def from_chunks(t):
    b, h, n, c, d = t.shape
    return t.transpose(0, 2, 3, 1, 4).reshape(b, n * c, h, d)


def gla_states(kc, vc, ac, s0):
    cum = jnp.cumsum(ac, axis=3)
    cum_last = cum[:, :, :, -1]
    k_dec = kc * jnp.exp(cum_last[:, :, :, None] - cum)
    kv = jnp.einsum('bhncd,bhnce->bhnde', k_dec, vc)

    def step(s, inp):
        decay, upd = inp
        return decay[..., None] * s + upd, s

    s_final, s_before = lax.scan(step, s0, (jnp.moveaxis(jnp.exp(cum_last), 2, 0), jnp.moveaxis(kv, 2, 0)))
    return s_final, jnp.moveaxis(s_before, 0, 2), cum


def gla_chunk_outputs(qc, kc, vc, cum, s_before):
    qd = qc * jnp.exp(cum)
    kd = kc * jnp.exp(-cum)
    scores = jnp.einsum('bhnid,bhnjd->bhnij', qd, kd)
    mask = jnp.tril(jnp.ones((B_CHUNK, B_CHUNK), dtype=bool))
    scores = jnp.where(mask, scores, 0.0)
    return jnp.einsum('bhnij,bhnje->bhnie', scores, vc) + jnp.einsum('bhnid,bhnde->bhnie', qd, s_before)


def gla_direction(q, k, v, log_a, s0, reverse):
    if reverse:
        k, v, log_a = jnp.flip(k, 1), jnp.flip(v, 1), jnp.flip(log_a, 1)
        if q is not None:
            q = jnp.flip(q, 1)
    kc, vc, ac = to_chunks(k), to_chunks(v), to_chunks(log_a)
    s_final, s_before, cum = gla_states(kc, vc, ac, s0)
    if q is None:
        return None, s_final
    o = from_chunks(gla_chunk_outputs(to_chunks(q), kc, vc, cum, s_before))
    if reverse:
        o = jnp.flip(o, 1)
    return o, s_final


def gla_q(p):
    b, l, _ = p.shape
    return p[..., Q0:K0].reshape(b, l, B_HEADS, B_DK).astype(jnp.float32) * (B_DK ** -0.5)


def gla_kva(p, base, w2, gb):
    b, l, _ = p.shape
    k = p[..., K0 - base:V0 - base].reshape(b, l, B_HEADS, B_DK).astype(jnp.float32)
    v = p[..., V0 - base:LR0 - base].reshape(b, l, B_HEADS, B_DV).astype(jnp.float32)
    lr = p[..., LR0 - base:ZB0 - base].reshape(b, l, 2, B_GATE_RANK)
    logits = jnp.einsum('blrk,rkd->blrd', lr, w2) + gb
    log_a = (jax.nn.log_sigmoid(logits.astype(jnp.float32)) / B_GATE_TAU).reshape(b, l, 2, B_HEADS, B_DK)
    return k, v, log_a[:, :, 0], log_a[:, :, 1]


def gla_branch_out(o, z, g, w_proj):
    b, l, _, _ = o.shape
    on = o * lax.rsqrt(jnp.mean(o * o, axis=-1, keepdims=True) + EPS) * g.reshape(B_HEADS, B_DV).astype(jnp.float32)
    on = on.reshape(b, l, B_VAL_WIDTH).astype(z.dtype)
    return (on * jax.nn.silu(z)) @ w_proj


def merge_branches(p, ya, yb, w_out):
    g = jax.nn.sigmoid(p[..., G0:])
    return (g[..., :D_MODEL] * ya + g[..., D_MODEL:] * yb) @ w_out


def _fwd_setup_inputs(seed: int = 0) -> dict:
    key = jax.random.key(seed)
    ks = jax.random.split(key, 20)

    def nrm(k, shape, s):
        return jax.random.normal(k, shape, jnp.float32) * s

    return {
        "x": nrm(ks[0], (BATCH, SEQ, D_MODEL), 1.0),
        "c": nrm(ks[1], (BATCH, D_MODEL), 1.0),
        "ctx": nrm(ks[2], (BATCH, CTX_LEN, D_MODEL), 1.0),
        "c_ctx": nrm(ks[3], (D_MODEL,), 1.0),
        "w_mod": nrm(ks[4], (DEPTH, D_MODEL, 3 * D_MODEL), D_MODEL ** -0.5),
        "b_mod": nrm(ks[5], (DEPTH, 3 * D_MODEL), 0.01),
        "norm_g": 1.0 + nrm(ks[6], (DEPTH, D_MODEL), 0.01),
        "w_in": nrm(ks[7], (DEPTH, D_MODEL, IN_WIDTH), D_MODEL ** -0.5),
        "a_ln_g": 1.0 + nrm(ks[8], (DEPTH, A_WIDTH), 0.01),
        "a_ln_b": nrm(ks[9], (DEPTH, A_WIDTH), 0.01),
        "a_ws": nrm(ks[10], (DEPTH, A_GROUPS, A_CHUNK, A_CHUNK), A_CHUNK ** -0.5),
        "a_bs": 1.0 + nrm(ks[11], (DEPTH, A_GROUPS, A_CHUNK), 0.01),
        "b_gate_w2": nrm(ks[12], (DEPTH, 2, B_GATE_RANK, B_KEY_WIDTH), B_GATE_RANK ** -0.5),
        "b_gate_b": nrm(ks[13], (DEPTH, 2, B_KEY_WIDTH), 0.1),
        "b_norm_g": 1.0 + nrm(ks[14], (DEPTH, B_VAL_WIDTH), 0.01),
        "w_proj_a": nrm(ks[15], (DEPTH, A_WIDTH, D_MODEL), A_WIDTH ** -0.5),
        "w_proj_b": nrm(ks[16], (DEPTH, B_VAL_WIDTH, D_MODEL), B_VAL_WIDTH ** -0.5),
        "w_out": nrm(ks[17], (DEPTH, D_MODEL, D_MODEL), D_MODEL ** -0.5),
        "final_norm_g": 1.0 + nrm(ks[18], (D_MODEL,), 0.01),
    }


def _fwd_reference(x, c, ctx, c_ctx, w_mod, b_mod, norm_g, w_in, a_ln_g, a_ln_b, a_ws, a_bs,
              b_gate_w2, b_gate_b, b_norm_g, w_proj_a, w_proj_b, w_out, final_norm_g):
    rows = x.shape[1] // GRID_W
    xc = ctx
    for layer in range(DEPTH):
        last = layer == DEPTH - 1
        wm, bm, wi = w_mod[layer], b_mod[layer], w_in[layer]

        mod = jax.nn.silu(c) @ wm + bm
        shift, scale, gate = jnp.split(mod, 3, axis=-1)
        h = rmsnorm(x, norm_g[layer]) * (1 + scale[:, None]) + shift[:, None]
        p = h @ wi

        n_mod = (2 if last else 3) * D_MODEL
        mod_c = jax.nn.silu(c_ctx) @ wm[:, :n_mod] + bm[:n_mod]
        hc = rmsnorm(xc, norm_g[layer]) * (1 + mod_c[D_MODEL:2 * D_MODEL]) + mod_c[:D_MODEL]
        base_c = K0 if last else 0
        pc = hc @ (wi[:, K0:ZB0] if last else wi)

        kc, vc, ac_f, ac_b = gla_kva(pc, base_c, b_gate_w2[layer], b_gate_b[layer])
        qc = None if last else gla_q(pc)
        s0 = jnp.zeros((xc.shape[0], B_HEADS, B_DK, B_DV), jnp.float32)
        oc_f, sc_f = gla_direction(qc, kc, vc, ac_f, s0, False)
        oc_b, sc_b = gla_direction(qc, kc, vc, ac_b, s0, True)

        k, v, a_f, a_b = gla_kva(p, 0, b_gate_w2[layer], b_gate_b[layer])
        q = gla_q(p)
        o_f, _ = gla_direction(q, k, v, a_f, sc_f, False)
        o_b, _ = gla_direction(q, k, v, a_b, sc_b, True)
        yb = gla_branch_out(o_f + o_b, p[..., ZB0:UA0], b_norm_g[layer], w_proj_b[layer])

        ya = chunk_mlp_branch(p, rows, a_ln_g[layer], a_ln_b[layer], a_ws[layer], a_bs[layer], w_proj_a[layer])

        x = x + gate[:, None] * merge_branches(p, ya, yb, w_out[layer])

        if not last:
            ybc = gla_branch_out(oc_f + oc_b, pc[..., ZB0:UA0], b_norm_g[layer], w_proj_b[layer])
            yac = chunk_mlp_branch(pc, None, a_ln_g[layer], a_ln_b[layer], a_ws[layer], a_bs[layer], w_proj_a[layer])
            xc = xc + mod_c[2 * D_MODEL:] * merge_branches(pc, yac, ybc, w_out[layer])
    return rmsnorm(x, final_norm_g)


import jax as _jax
import jax.numpy as _jnp

TWIN_FORMAT = 'train_step'
FWD_PARAMS = ['x', 'c', 'ctx', 'c_ctx', 'w_mod', 'b_mod', 'norm_g', 'w_in', 'a_ln_g', 'a_ln_b', 'a_ws', 'a_bs', 'b_gate_w2', 'b_gate_b', 'b_norm_g', 'w_proj_a', 'w_proj_b', 'w_out', 'final_norm_g']
TWIN_WEIGHTS = ['c_ctx', 'w_mod', 'b_mod', 'norm_g', 'w_in', 'a_ln_g', 'a_ln_b', 'a_ws', 'a_bs', 'b_gate_w2', 'b_gate_b', 'b_norm_g', 'w_proj_a', 'w_proj_b', 'w_out', 'final_norm_g']
TWIN_DIFF_INPUT = 'x'
TWIN_INPUTS = ['x', 'c', 'ctx', 'c_ctx', 'w_mod', 'b_mod', 'norm_g', 'w_in', 'a_ln_g', 'a_ln_b', 'a_ws', 'a_bs', 'b_gate_w2', 'b_gate_b', 'b_norm_g', 'w_proj_a', 'w_proj_b', 'w_out', 'final_norm_g', 'loss_target', 'm_c_ctx', 'm_w_mod', 'm_b_mod', 'm_norm_g', 'm_w_in', 'm_a_ln_g', 'm_a_ln_b', 'm_a_ws', 'm_a_bs', 'm_b_gate_w2', 'm_b_gate_b', 'm_b_norm_g', 'm_w_proj_a', 'm_w_proj_b', 'm_w_out', 'm_final_norm_g', 'v_c_ctx', 'v_w_mod', 'v_b_mod', 'v_norm_g', 'v_w_in', 'v_a_ln_g', 'v_a_ln_b', 'v_a_ws', 'v_a_bs', 'v_b_gate_w2', 'v_b_gate_b', 'v_b_norm_g', 'v_w_proj_a', 'v_w_proj_b', 'v_w_out', 'v_final_norm_g']
TWIN_OUTPUTS = ['loss', 'grad_x', 'grad_c_ctx', 'grad_w_mod', 'grad_b_mod', 'grad_norm_g', 'grad_w_in', 'grad_a_ln_g', 'grad_a_ln_b', 'grad_a_ws', 'grad_a_bs', 'grad_b_gate_w2', 'grad_b_gate_b', 'grad_b_norm_g', 'grad_w_proj_a', 'grad_w_proj_b', 'grad_w_out', 'grad_final_norm_g', 'delta_c_ctx', 'delta_w_mod', 'delta_b_mod', 'delta_norm_g', 'delta_w_in', 'delta_a_ln_g', 'delta_a_ln_b', 'delta_a_ws', 'delta_a_bs', 'delta_b_gate_w2', 'delta_b_gate_b', 'delta_b_norm_g', 'delta_w_proj_a', 'delta_w_proj_b', 'delta_w_out', 'delta_final_norm_g', 'new_m_c_ctx', 'new_m_w_mod', 'new_m_b_mod', 'new_m_norm_g', 'new_m_w_in', 'new_m_a_ln_g', 'new_m_a_ln_b', 'new_m_a_ws', 'new_m_a_bs', 'new_m_b_gate_w2', 'new_m_b_gate_b', 'new_m_b_norm_g', 'new_m_w_proj_a', 'new_m_w_proj_b', 'new_m_w_out', 'new_m_final_norm_g', 'new_v_c_ctx', 'new_v_w_mod', 'new_v_b_mod', 'new_v_norm_g', 'new_v_w_in', 'new_v_a_ln_g', 'new_v_a_ln_b', 'new_v_a_ws', 'new_v_a_bs', 'new_v_b_gate_w2', 'new_v_b_gate_b', 'new_v_b_norm_g', 'new_v_w_proj_a', 'new_v_w_proj_b', 'new_v_w_out', 'new_v_final_norm_g']
TWIN_LEAF_KINDS = {'loss': 'loss', 'grad_x': 'grad_x', 'grad_c_ctx': 'grad_w', 'grad_w_mod': 'grad_w', 'grad_b_mod': 'grad_w', 'grad_norm_g': 'grad_w', 'grad_w_in': 'grad_w', 'grad_a_ln_g': 'grad_w', 'grad_a_ln_b': 'grad_w', 'grad_a_ws': 'grad_w', 'grad_a_bs': 'grad_w', 'grad_b_gate_w2': 'grad_w', 'grad_b_gate_b': 'grad_w', 'grad_b_norm_g': 'grad_w', 'grad_w_proj_a': 'grad_w', 'grad_w_proj_b': 'grad_w', 'grad_w_out': 'grad_w', 'grad_final_norm_g': 'grad_w', 'delta_c_ctx': 'delta_w', 'delta_w_mod': 'delta_w', 'delta_b_mod': 'delta_w', 'delta_norm_g': 'delta_w', 'delta_w_in': 'delta_w', 'delta_a_ln_g': 'delta_w', 'delta_a_ln_b': 'delta_w', 'delta_a_ws': 'delta_w', 'delta_a_bs': 'delta_w', 'delta_b_gate_w2': 'delta_w', 'delta_b_gate_b': 'delta_w', 'delta_b_norm_g': 'delta_w', 'delta_w_proj_a': 'delta_w', 'delta_w_proj_b': 'delta_w', 'delta_w_out': 'delta_w', 'delta_final_norm_g': 'delta_w', 'new_m_c_ctx': 'new_m', 'new_m_w_mod': 'new_m', 'new_m_b_mod': 'new_m', 'new_m_norm_g': 'new_m', 'new_m_w_in': 'new_m', 'new_m_a_ln_g': 'new_m', 'new_m_a_ln_b': 'new_m', 'new_m_a_ws': 'new_m', 'new_m_a_bs': 'new_m', 'new_m_b_gate_w2': 'new_m', 'new_m_b_gate_b': 'new_m', 'new_m_b_norm_g': 'new_m', 'new_m_w_proj_a': 'new_m', 'new_m_w_proj_b': 'new_m', 'new_m_w_out': 'new_m', 'new_m_final_norm_g': 'new_m', 'new_v_c_ctx': 'new_v', 'new_v_w_mod': 'new_v', 'new_v_b_mod': 'new_v', 'new_v_norm_g': 'new_v', 'new_v_w_in': 'new_v', 'new_v_a_ln_g': 'new_v', 'new_v_a_ln_b': 'new_v', 'new_v_a_ws': 'new_v', 'new_v_a_bs': 'new_v', 'new_v_b_gate_w2': 'new_v', 'new_v_b_gate_b': 'new_v', 'new_v_b_norm_g': 'new_v', 'new_v_w_proj_a': 'new_v', 'new_v_w_proj_b': 'new_v', 'new_v_w_out': 'new_v', 'new_v_final_norm_g': 'new_v'}


def _forward(args):
    return _fwd_reference(*[args[k] for k in FWD_PARAMS])


def _output_shape():
    def fwd():
        inp = _fwd_setup_inputs(0)
        return _fwd_reference(*[inp[k] for k in FWD_PARAMS])
    out = _jax.eval_shape(fwd)
    return out.shape, out.dtype

N_MICROBATCH = 1
ADAM_LR = 0.001
ADAM_B1 = 0.9
ADAM_B2 = 0.999
ADAM_EPS = 1e-08
ADAM_WD = 0.01
ADAM_STEP = 10
PER_EXAMPLE_BATCH_AXIS = {'x': 0, 'c': 0, 'ctx': 0, 'loss_target': 0}
SHARED_INPUTS = []
_WEIGHT_DTYPES = {'c_ctx': _jnp.float32, 'w_mod': _jnp.float32, 'b_mod': _jnp.float32, 'norm_g': _jnp.float32, 'w_in': _jnp.float32, 'a_ln_g': _jnp.float32, 'a_ln_b': _jnp.float32, 'a_ws': _jnp.float32, 'a_bs': _jnp.float32, 'b_gate_w2': _jnp.float32, 'b_gate_b': _jnp.float32, 'b_norm_g': _jnp.float32, 'w_proj_a': _jnp.float32, 'w_proj_b': _jnp.float32, 'w_out': _jnp.float32, 'final_norm_g': _jnp.float32}
MOMENT_SCALE = {'c_ctx': 1.824999e-02, 'w_mod': 1.255127e-01, 'b_mod': 2.142666e-01, 'norm_g': 2.129774e-01, 'w_in': 1.050376e-01, 'a_ln_g': 1.191627e-01, 'a_ln_b': 1.195377e-01, 'a_ws': 1.125684e-01, 'a_bs': 1.147648e-01, 'b_gate_w2': 3.254732e-02, 'b_gate_b': 5.181008e-02, 'b_norm_g': 8.865972e-02, 'w_proj_a': 1.110963e-01, 'w_proj_b': 6.080496e-02, 'w_out': 1.268385e-01, 'final_norm_g': 6.458310e+01}


def _to_microbatches(a, axis):
    t = _jnp.moveaxis(a, axis, 0)
    t = t.reshape((N_MICROBATCH, t.shape[0] // N_MICROBATCH) + t.shape[1:])
    return _jnp.moveaxis(t, 1, axis + 1)


def setup_inputs(seed: int = 0) -> dict:
    inp = _fwd_setup_inputs(seed)
    key = _jax.random.fold_in(_jax.random.key(seed), 7919)
    shape, _ = _output_shape()
    out = dict(inp)
    out["loss_target"] = _jax.random.normal(_jax.random.fold_in(key, 0), shape, _jnp.float32)
    for i, name in enumerate(TWIN_WEIGHTS):
        w = inp[name].astype(_jnp.float32)
        if MOMENT_SCALE is None:
            s = _jnp.sqrt(_jnp.mean(_jnp.square(w)) + 1e-30)
        else:
            s = MOMENT_SCALE[name]
        km, kv = _jax.random.split(_jax.random.fold_in(key, i + 1))
        out[name] = w
        out["m_" + name] = s * _jax.random.normal(km, w.shape, _jnp.float32)
        out["v_" + name] = (s * s) * _jax.random.uniform(kv, w.shape, _jnp.float32, 0.5, 1.5)
    if N_MICROBATCH > 1:
        for name, axis in PER_EXAMPLE_BATCH_AXIS.items():
            out[name] = _to_microbatches(out[name], axis)
    return {'x': out['x'], 'c': out['c'], 'ctx': out['ctx'], 'c_ctx': out['c_ctx'], 'w_mod': out['w_mod'], 'b_mod': out['b_mod'], 'norm_g': out['norm_g'], 'w_in': out['w_in'], 'a_ln_g': out['a_ln_g'], 'a_ln_b': out['a_ln_b'], 'a_ws': out['a_ws'], 'a_bs': out['a_bs'], 'b_gate_w2': out['b_gate_w2'], 'b_gate_b': out['b_gate_b'], 'b_norm_g': out['b_norm_g'], 'w_proj_a': out['w_proj_a'], 'w_proj_b': out['w_proj_b'], 'w_out': out['w_out'], 'final_norm_g': out['final_norm_g'], 'loss_target': out['loss_target'], 'm_c_ctx': out['m_c_ctx'], 'm_w_mod': out['m_w_mod'], 'm_b_mod': out['m_b_mod'], 'm_norm_g': out['m_norm_g'], 'm_w_in': out['m_w_in'], 'm_a_ln_g': out['m_a_ln_g'], 'm_a_ln_b': out['m_a_ln_b'], 'm_a_ws': out['m_a_ws'], 'm_a_bs': out['m_a_bs'], 'm_b_gate_w2': out['m_b_gate_w2'], 'm_b_gate_b': out['m_b_gate_b'], 'm_b_norm_g': out['m_b_norm_g'], 'm_w_proj_a': out['m_w_proj_a'], 'm_w_proj_b': out['m_w_proj_b'], 'm_w_out': out['m_w_out'], 'm_final_norm_g': out['m_final_norm_g'], 'v_c_ctx': out['v_c_ctx'], 'v_w_mod': out['v_w_mod'], 'v_b_mod': out['v_b_mod'], 'v_norm_g': out['v_norm_g'], 'v_w_in': out['v_w_in'], 'v_a_ln_g': out['v_a_ln_g'], 'v_a_ln_b': out['v_a_ln_b'], 'v_a_ws': out['v_a_ws'], 'v_a_bs': out['v_a_bs'], 'v_b_gate_w2': out['v_b_gate_w2'], 'v_b_gate_b': out['v_b_gate_b'], 'v_b_norm_g': out['v_b_norm_g'], 'v_w_proj_a': out['v_w_proj_a'], 'v_w_proj_b': out['v_w_proj_b'], 'v_w_out': out['v_w_out'], 'v_final_norm_g': out['v_final_norm_g']}


def _loss(weights, diff, rest, loss_target):
    with _jax.named_scope("forward"):
        args = {**rest, TWIN_DIFF_INPUT: diff, **{k: w.astype(_WEIGHT_DTYPES[k]) for k, w in weights.items()}}
        y = _forward(args)
    with _jax.named_scope("loss_head"):
        err = _jnp.square(y.astype(_jnp.float32) - loss_target)
        return 0.5 * _jnp.sum(_jnp.mean(err, axis=-1)) if err.ndim else 0.5 * err


def _adamw(w, g, m, v):
    m = ADAM_B1 * m + (1.0 - ADAM_B1) * g
    v = ADAM_B2 * v + (1.0 - ADAM_B2) * _jnp.square(g)
    m_hat = m / (1.0 - ADAM_B1 ** ADAM_STEP)
    v_hat = v / (1.0 - ADAM_B2 ** ADAM_STEP)
    delta = -ADAM_LR * (m_hat / (_jnp.sqrt(v_hat) + ADAM_EPS) + ADAM_WD * w)
    return delta, m, v


def reference(x, c, ctx, c_ctx, w_mod, b_mod, norm_g, w_in, a_ln_g, a_ln_b, a_ws, a_bs, b_gate_w2, b_gate_b, b_norm_g, w_proj_a, w_proj_b, w_out, final_norm_g, loss_target, m_c_ctx, m_w_mod, m_b_mod, m_norm_g, m_w_in, m_a_ln_g, m_a_ln_b, m_a_ws, m_a_bs, m_b_gate_w2, m_b_gate_b, m_b_norm_g, m_w_proj_a, m_w_proj_b, m_w_out, m_final_norm_g, v_c_ctx, v_w_mod, v_b_mod, v_norm_g, v_w_in, v_a_ln_g, v_a_ln_b, v_a_ws, v_a_bs, v_b_gate_w2, v_b_gate_b, v_b_norm_g, v_w_proj_a, v_w_proj_b, v_w_out, v_final_norm_g):
    given = dict(x=x, c=c, ctx=ctx, c_ctx=c_ctx, w_mod=w_mod, b_mod=b_mod, norm_g=norm_g, w_in=w_in, a_ln_g=a_ln_g, a_ln_b=a_ln_b, a_ws=a_ws, a_bs=a_bs, b_gate_w2=b_gate_w2, b_gate_b=b_gate_b, b_norm_g=b_norm_g, w_proj_a=w_proj_a, w_proj_b=w_proj_b, w_out=w_out, final_norm_g=final_norm_g, loss_target=loss_target, m_c_ctx=m_c_ctx, m_w_mod=m_w_mod, m_b_mod=m_b_mod, m_norm_g=m_norm_g, m_w_in=m_w_in, m_a_ln_g=m_a_ln_g, m_a_ln_b=m_a_ln_b, m_a_ws=m_a_ws, m_a_bs=m_a_bs, m_b_gate_w2=m_b_gate_w2, m_b_gate_b=m_b_gate_b, m_b_norm_g=m_b_norm_g, m_w_proj_a=m_w_proj_a, m_w_proj_b=m_w_proj_b, m_w_out=m_w_out, m_final_norm_g=m_final_norm_g, v_c_ctx=v_c_ctx, v_w_mod=v_w_mod, v_b_mod=v_b_mod, v_norm_g=v_norm_g, v_w_in=v_w_in, v_a_ln_g=v_a_ln_g, v_a_ln_b=v_a_ln_b, v_a_ws=v_a_ws, v_a_bs=v_a_bs, v_b_gate_w2=v_b_gate_w2, v_b_gate_b=v_b_gate_b, v_b_norm_g=v_b_norm_g, v_w_proj_a=v_w_proj_a, v_w_proj_b=v_w_proj_b, v_w_out=v_w_out, v_final_norm_g=v_final_norm_g)
    weights = {n: given[n] for n in TWIN_WEIGHTS}
    shared = {n: given[n] for n in SHARED_INPUTS}
    per_example = {n: given[n] for n in ['x', 'c', 'ctx']}
    grad_fn = _jax.value_and_grad(_loss, argnums=(0, 1))

    def one_microbatch(ex, loss_target):
        ex = dict(ex)
        diff = ex.pop(TWIN_DIFF_INPUT)
        return grad_fn(weights, diff, {**shared, **ex}, loss_target)

    if N_MICROBATCH == 1:
        loss, (grad_w, grad_x) = one_microbatch(per_example, given["loss_target"])
    else:
        def body(carry, xs):
            loss_sum, grad_sum = carry
            l_k, (gw_k, gx_k) = one_microbatch(xs[0], xs[1])
            with _jax.named_scope("update"):
                return (loss_sum + l_k, _jax.tree.map(_jnp.add, grad_sum, gw_k)), gx_k

        init = (_jnp.zeros((), _jnp.float32), _jax.tree.map(_jnp.zeros_like, weights))
        (loss, grad_w), grad_x = _jax.lax.scan(body, init, (per_example, given["loss_target"]))
    with _jax.named_scope("update"):
        delta_w, new_m, new_v = {}, {}, {}
        for n in TWIN_WEIGHTS:
            delta_w[n], new_m[n], new_v[n] = _adamw(weights[n], grad_w[n], given["m_" + n], given["v_" + n])
    return (loss, grad_x, *[grad_w[n] for n in TWIN_WEIGHTS], *[delta_w[n] for n in TWIN_WEIGHTS],
            *[new_m[n] for n in TWIN_WEIGHTS], *[new_v[n] for n in TWIN_WEIGHTS])
```

```python
import functools

import jax
import jax.numpy as jnp
from jax import lax
from jax.experimental import pallas as pl
from jax.experimental.pallas import tpu as pltpu

F32 = jnp.float32
BF16 = jnp.bfloat16
SDS = jax.ShapeDtypeStruct

D = 1024
NP = 5120
LRW = 128
CH = 64
AC = 128
EPS = 1e-6
TOK = 256
GLA_TB = 256
VMEM_BIG = 48 * 1024 * 1024

ADAM_LR, ADAM_B1, ADAM_B2, ADAM_EPS, ADAM_WD, ADAM_STEP = 0.001, 0.9, 0.999, 1e-08, 0.01, 10

_pcall = pl.pallas_call
MESH = pl.DeviceIdType.MESH


def _cp(sem=None, vmem=None):
    kw = {}
    if sem is not None:
        kw["dimension_semantics"] = sem
    if vmem is not None:
        kw["vmem_limit_bytes"] = vmem
    return pltpu.CompilerParams(**kw)


def _silu(x):
    return x * jax.nn.sigmoid(x)


def _dsilu(x):
    s = jax.nn.sigmoid(x)
    return s * (1.0 + x * (1.0 - s))


def _logsig(x):
    return jnp.minimum(x, 0.0) - jnp.log1p(jnp.exp(-jnp.abs(x)))


def _nt(a, b):
    return lax.dot_general(a, b, (((1,), (1,)), ((), ())), preferred_element_type=F32)


def _tn(a, b):
    return lax.dot_general(a, b, (((0,), (0,)), ((), ())), preferred_element_type=F32)


def _nn(a, b):
    return jnp.dot(a, b, preferred_element_type=F32)


def _full(shape):
    return pl.BlockSpec(shape, lambda *_: (0,) * len(shape))


def _mm(a, b, *, tm, tn, tk, out_dtype, name, acc=None):
    m, k = a.shape
    k2, n = b.shape
    assert k == k2 and m % tm == 0 and n % tn == 0 and k % tk == 0, (a.shape, b.shape, tm, tn, tk)
    nk = k // tk
    has_acc = acc is not None

    def body(*refs):
        if has_acc:
            a_ref, b_ref, c_ref, o_ref, acc_ref = refs
        else:
            a_ref, b_ref, o_ref, acc_ref = refs
        kk = pl.program_id(2)
        part = _nn(a_ref[...].astype(BF16), b_ref[...].astype(BF16))

        @pl.when(kk == 0)
        def _():
            if has_acc:
                acc_ref[...] = c_ref[...] + part
            else:
                acc_ref[...] = part

        @pl.when(kk > 0)
        def _():
            acc_ref[...] += part

        @pl.when(kk == nk - 1)
        def _():
            o_ref[...] = acc_ref[...].astype(out_dtype)

    in_specs = [pl.BlockSpec((tm, tk), lambda i, j, kk: (i, kk)), pl.BlockSpec((tk, tn), lambda i, j, kk: (kk, j))]
    args = [a, b]
    if has_acc:
        in_specs.append(pl.BlockSpec((tm, tn), lambda i, j, kk: (i, j)))
        args.append(acc)
    return _pcall(
        body, name=name, grid=(m // tm, n // tn, nk), in_specs=in_specs,
        out_specs=pl.BlockSpec((tm, tn), lambda i, j, kk: (i, j)),
        out_shape=SDS((m, n), out_dtype), scratch_shapes=[pltpu.VMEM((tm, tn), F32)],
        compiler_params=_cp(("parallel", "parallel", "arbitrary"), VMEM_BIG),
    )(*args)


def _modvec(cc, wm, bm):
    def body(c_ref, w_ref, b_ref, o_ref):
        o_ref[...] = _nn(_silu(c_ref[...]).astype(BF16), w_ref[...]) + b_ref[...]

    return _pcall(body, name="modvec", out_shape=SDS((8, 3 * D), F32), compiler_params=_cp(None, VMEM_BIG))(cc, wm, bm)


def _dcctx(dmodc, wm):
    def body(d_ref, w_ref, o_ref):
        o_ref[...] = _nt(d_ref[...].astype(BF16), w_ref[...])

    return _pcall(
        body, name="dcctx", grid=(1,), in_specs=[_full((8, 2 * D)), pl.BlockSpec((D, 2 * D), lambda i: (0, 0))],
        out_specs=_full((8, D)), out_shape=SDS((8, D), F32), compiler_params=_cp(("arbitrary",), VMEM_BIG),
    )(dmodc, wm)


def _prep_h(x, ng, scale, shift, name):
    m = x.shape[0]

    def body(x_ref, g_ref, sc_ref, sh_ref, h_ref):
        xf = x_ref[...]
        r = lax.rsqrt(jnp.mean(xf * xf, axis=-1, keepdims=True) + EPS)
        y = (xf * r) * g_ref[...]
        h_ref[...] = (y * (1.0 + sc_ref[...]) + sh_ref[...]).astype(BF16)

    row = pl.BlockSpec((TOK, D), lambda i: (i, 0))
    return _pcall(
        body, name=name, grid=(m // TOK,), in_specs=[row, _full((1, D)), _full((1, D)), _full((1, D))],
        out_specs=row, out_shape=SDS((m, D), BF16), compiler_params=_cp(("parallel",)),
    )(x, ng, scale, shift)


def _prep_bwd(x, dh, dx1, ng, scale, name):
    m = x.shape[0]
    has_res = dx1 is not None

    def body(*refs):
        if has_res:
            x_ref, dh_ref, r_ref, g_ref, sc_ref, dx_ref, dg_ref, dsc_ref, dsh_ref = refs
        else:
            x_ref, dh_ref, g_ref, sc_ref, dx_ref, dg_ref, dsc_ref, dsh_ref = refs
        i = pl.program_id(0)

        @pl.when(i == 0)
        def _():
            dg_ref[...] = jnp.zeros_like(dg_ref)
            dsc_ref[...] = jnp.zeros_like(dsc_ref)
            dsh_ref[...] = jnp.zeros_like(dsh_ref)

        xf = x_ref[...]
        dh_ = dh_ref[...]
        r = lax.rsqrt(jnp.mean(xf * xf, axis=-1, keepdims=True) + EPS)
        xh = xf * r
        y = xh * g_ref[...]
        dsh_ref[...] += jnp.sum(dh_, axis=0, keepdims=True)
        dsc_ref[...] += jnp.sum(dh_ * y, axis=0, keepdims=True)
        dy = dh_ * (1.0 + sc_ref[...])
        dg_ref[...] += jnp.sum(dy * xh, axis=0, keepdims=True)
        dxh = dy * g_ref[...]
        dx = r * (dxh - xh * jnp.mean(dxh * xh, axis=-1, keepdims=True))
        if has_res:
            dx = dx + r_ref[...]
        dx_ref[...] = dx

    row = pl.BlockSpec((TOK, D), lambda i: (i, 0))
    vec = _full((1, D))
    in_specs = [row, row] + ([row] if has_res else []) + [vec, vec]
    args = [x, dh] + ([dx1] if has_res else []) + [ng, scale]
    return _pcall(
        body, name=name, grid=(m // TOK,), in_specs=in_specs, out_specs=[row, vec, vec, vec],
        out_shape=[SDS((m, D), F32), SDS((1, D), F32), SDS((1, D), F32), SDS((1, D), F32)],
        compiler_params=_cp(("arbitrary",)),
    )(*args)


def _ln_fwd(p, ln_g, ln_b):
    m = p.shape[0]

    def body(va_ref, g_ref, b_ref, vr_ref, vc_ref):
        xf = va_ref[...]
        xc = xf - jnp.mean(xf, axis=-1, keepdims=True)
        y = xc * lax.rsqrt(jnp.mean(xc * xc, axis=-1, keepdims=True) + EPS)
        vn = y * g_ref[...] + b_ref[...]
        vr_ref[...] = vn[:, 0:256].astype(BF16)
        vc_ref[0] = vn[:, 256:384].astype(BF16)
        vc_ref[1] = vn[:, 384:512].astype(BF16)

    return _pcall(
        body, name="ln_fwd", grid=(m // TOK,),
        in_specs=[pl.BlockSpec((TOK, 512), lambda i: (i, 9)), _full((1, 512)), _full((1, 512))],
        out_specs=[pl.BlockSpec((TOK, 256), lambda i: (i, 0)), pl.BlockSpec((2, TOK, 128), lambda i: (0, i, 0))],
        out_shape=[SDS((m, 256), BF16), SDS((2, m, 128), BF16)], compiler_params=_cp(("parallel",)),
    )(p, ln_g, ln_b)


COLB = 2048


def _colmix_fwd(vnc, ws23, bs23):
    rows = vnc.shape[2] // COLB

    def body(v_ref, w_ref, b_ref, o_ref):
        o_ref[0] = _nn(w_ref[0], v_ref[0]) + b_ref[0]

    return _pcall(
        body, name="colmix_fwd", grid=(2, rows),
        in_specs=[pl.BlockSpec((1, AC, COLB), lambda g, j: (g, 0, j)), pl.BlockSpec((1, AC, AC), lambda g, j: (g, 0, 0)),
                  pl.BlockSpec((1, AC, 1), lambda g, j: (g, 0, 0))],
        out_specs=pl.BlockSpec((1, AC, COLB), lambda g, j: (g, 0, j)),
        out_shape=SDS(vnc.shape, F32), compiler_params=_cp(("parallel", "parallel")),
    )(vnc, ws23, bs23)


def _colmix_bwd(dsvc, vnc, ws23t):
    rows = vnc.shape[2] // COLB

    def body(d_ref, v_ref, wt_ref, dv_ref, dw_ref, db_ref):
        j = pl.program_id(1)

        @pl.when(j == 0)
        def _():
            dw_ref[...] = jnp.zeros_like(dw_ref)
            db_ref[...] = jnp.zeros_like(db_ref)

        d = d_ref[0]
        d16 = d.astype(BF16)
        dv_ref[0] = _nn(wt_ref[0], d16)
        dw_ref[0] += _nt(d16, v_ref[0])
        db_ref[0] += jnp.sum(d, axis=1, keepdims=True)

    blk = pl.BlockSpec((1, AC, COLB), lambda g, j: (g, 0, j))
    return _pcall(
        body, name="colmix_bwd", grid=(2, rows),
        in_specs=[blk, blk, pl.BlockSpec((1, AC, AC), lambda g, j: (g, 0, 0))],
        out_specs=[blk, pl.BlockSpec((1, AC, AC), lambda g, j: (g, 0, 0)), pl.BlockSpec((1, AC, 1), lambda g, j: (g, 0, 0))],
        out_shape=[SDS(vnc.shape, F32), SDS((2, AC, AC), F32), SDS((2, AC, 1), F32)],
        compiler_params=_cp(("parallel", "arbitrary")),
    )(dsvc, vnc, ws23t)


def _head_norm(o, gbn):
    out = []
    for h in range(4):
        oh = o[:, 128 * h:128 * h + 128]
        r = lax.rsqrt(jnp.mean(oh * oh, axis=-1, keepdims=True) + EPS)
        out.append((r, oh * r))
    return out


def _mid_fwd(o_f, o_b, p, vnr, svc, ws01, bs01, gbn):
    m = p.shape[0]

    def body(of_ref, ob_ref, zb_ref, ua_ref, za_ref, vnr_ref, svc_ref, w_ref, b_ref, g_ref, ya_ref, yb_ref, svr_ref):
        o = of_ref[...] + ob_ref[...]
        zb = zb_ref[...]
        parts = []
        for h, (r, xh) in enumerate(_head_norm(o, None)):
            parts.append(xh * g_ref[:, 128 * h:128 * h + 128])
        on = jnp.concatenate(parts, axis=1)
        yb_ref[...] = (on * _silu(zb)).astype(BF16)
        for j in range(TOK // AC):
            for g in range(2):
                sv = _nn(w_ref[g], vnr_ref[AC * j:AC * j + AC, AC * g:AC * g + AC]) + b_ref[g]
                svr_ref[AC * j:AC * j + AC, AC * g:AC * g + AC] = sv
        sz = _silu(za_ref[...])
        u = ua_ref[...]
        ya_ref[:, 0:256] = ((u[:, 0:256] * svr_ref[...]) * sz[:, 0:256]).astype(BF16)
        ya_ref[:, 256:384] = ((u[:, 256:384] * svc_ref[0]) * sz[:, 256:384]).astype(BF16)
        ya_ref[:, 384:512] = ((u[:, 384:512] * svc_ref[1]) * sz[:, 384:512]).astype(BF16)

    r512 = pl.BlockSpec((TOK, 512), lambda i: (i, 0))
    return _pcall(
        body, name="mid_fwd", grid=(m // TOK,),
        in_specs=[r512, r512, pl.BlockSpec((TOK, 512), lambda i: (i, 6)), pl.BlockSpec((TOK, 512), lambda i: (i, 7)),
                  pl.BlockSpec((TOK, 512), lambda i: (i, 8)), pl.BlockSpec((TOK, 256), lambda i: (i, 0)),
                  pl.BlockSpec((2, TOK, 128), lambda i: (0, i, 0)), _full((2, AC, AC)), _full((2, AC, 1)), _full((1, 512))],
        out_specs=[r512, r512, pl.BlockSpec((TOK, 256), lambda i: (i, 0))],
        out_shape=[SDS((m, 512), BF16), SDS((m, 512), BF16), SDS((m, 256), F32)],
        compiler_params=_cp(("parallel",)),
    )(o_f, o_b, p, p, p, vnr, svc, ws01, bs01, gbn)


def _merge_fwd(p, ya, yb):
    m = p.shape[0]

    def body(ga_ref, gb_ref, ya_ref, yb_ref, m_ref):
        m_ref[...] = (jax.nn.sigmoid(ga_ref[...]) * ya_ref[...] + jax.nn.sigmoid(gb_ref[...]) * yb_ref[...]).astype(BF16)

    row = pl.BlockSpec((TOK, D), lambda i: (i, 0))
    return _pcall(
        body, name="merge_fwd", grid=(m // TOK,),
        in_specs=[row, pl.BlockSpec((TOK, D), lambda i: (i, 1)), row, row], out_specs=row,
        out_shape=SDS((m, D), BF16), compiler_params=_cp(("parallel",)),
    )(p, p, ya, yb)


def _loss_head(x, out, tgt, gate, gf):
    m = x.shape[0]

    def body(x_ref, o_ref, t_ref, gate_ref, gf_ref, dx1_ref, dout_ref, loss_ref, dgate_ref, dgf_ref):
        i = pl.program_id(0)

        @pl.when(i == 0)
        def _():
            loss_ref[...] = jnp.zeros_like(loss_ref)
            dgate_ref[...] = jnp.zeros_like(dgate_ref)
            dgf_ref[...] = jnp.zeros_like(dgf_ref)

        out_ = o_ref[...]
        x1 = x_ref[...] + gate_ref[...] * out_
        r = lax.rsqrt(jnp.mean(x1 * x1, axis=-1, keepdims=True) + EPS)
        xh = x1 * r
        err = xh * gf_ref[...] - t_ref[...]
        loss_ref[...] += 0.5 * jnp.sum(jnp.mean(err * err, axis=-1, keepdims=True), axis=0, keepdims=True)
        dy = err * (1.0 / D)
        dgf_ref[...] += jnp.sum(dy * xh, axis=0, keepdims=True)
        dxh = dy * gf_ref[...]
        dx1 = r * (dxh - xh * jnp.mean(dxh * xh, axis=-1, keepdims=True))
        dx1_ref[...] = dx1
        dout_ref[...] = (gate_ref[...] * dx1).astype(BF16)
        dgate_ref[...] += jnp.sum(dx1 * out_, axis=0, keepdims=True)

    row = pl.BlockSpec((TOK, D), lambda i: (i, 0))
    vec = _full((1, D))
    return _pcall(
        body, name="loss_head", grid=(m // TOK,), in_specs=[row, row, row, vec, vec],
        out_specs=[row, row, _full((1, 128)), vec, vec],
        out_shape=[SDS((m, D), F32), SDS((m, D), BF16), SDS((1, 128), F32), SDS((1, D), F32), SDS((1, D), F32)],
        compiler_params=_cp(("arbitrary",)),
    )(x, out, tgt, gate, gf)


def _merge_bwd(dm, ya, yb, p):
    m = p.shape[0]

    def body(dm_ref, ya_ref, yb_ref, ga_ref, gb_ref, dya_ref, dyb_ref, dp_ref):
        dm_ = dm_ref[...]
        sa = jax.nn.sigmoid(ga_ref[...])
        sb = jax.nn.sigmoid(gb_ref[...])
        dya_ref[...] = (dm_ * sa).astype(BF16)
        dyb_ref[...] = (dm_ * sb).astype(BF16)
        dp_ref[:, 0:D] = dm_ * ya_ref[...] * (sa * (1.0 - sa))
        dp_ref[:, D:2 * D] = dm_ * yb_ref[...] * (sb * (1.0 - sb))

    row = pl.BlockSpec((TOK, D), lambda i: (i, 0))
    return _pcall(
        body, name="merge_bwd", grid=(m // TOK,),
        in_specs=[row, row, row, row, pl.BlockSpec((TOK, D), lambda i: (i, 1))],
        out_specs=[row, row, pl.BlockSpec((TOK, 2 * D), lambda i: (i, 0))],
        out_shape=[SDS((m, D), BF16), SDS((m, D), BF16), SDS((m, NP), F32)],
        compiler_params=_cp(("parallel",)),
    )(dm, ya, yb, p, p)


def _mid_bwd(dya_in, dyb_in, p, svr, svc, o_f, o_b, gbn, dp):
    m = p.shape[0]

    def body(dya_ref, dyb_ref, zb_ref, ua_ref, za_ref, svr_ref, svc_ref, of_ref, ob_ref, g_ref, dpi_ref,
             dp_ref, dsr_ref, dsc_ref, do_ref, dg_ref):
        i = pl.program_id(0)

        @pl.when(i == 0)
        def _():
            dg_ref[...] = jnp.zeros_like(dg_ref)

        dya = dya_ref[...]
        u = ua_ref[...]
        za = za_ref[...]
        sz = _silu(za)
        sv = jnp.concatenate([svr_ref[...], svc_ref[0], svc_ref[1]], axis=1)
        dp_ref[:, 512:1024] = dya * sv * sz
        dsv = dya * u * sz
        dsr_ref[...] = dsv[:, 0:256]
        dsc_ref[0] = dsv[:, 256:384]
        dsc_ref[1] = dsv[:, 384:512]
        dp_ref[:, 1024:1536] = dya * u * sv * _dsilu(za)

        dyb = dyb_ref[...]
        zb = zb_ref[...]
        o = of_ref[...] + ob_ref[...]
        szb = _silu(zb)
        dszb = _dsilu(zb)
        for h, (r, xh) in enumerate(_head_norm(o, None)):
            sl = slice(128 * h, 128 * h + 128)
            gh = g_ref[:, sl]
            don = dyb[:, sl] * szb[:, sl]
            dp_ref[:, sl] = dyb[:, sl] * (xh * gh) * dszb[:, sl]
            dg_ref[:, sl] += jnp.sum(don * xh, axis=0, keepdims=True)
            dxh = don * gh
            do_ref[:, sl] = r * (dxh - xh * jnp.mean(dxh * xh, axis=-1, keepdims=True))

    r512 = pl.BlockSpec((TOK, 512), lambda i: (i, 0))
    return _pcall(
        body, name="mid_bwd", grid=(m // TOK,),
        in_specs=[r512, r512, pl.BlockSpec((TOK, 512), lambda i: (i, 6)), pl.BlockSpec((TOK, 512), lambda i: (i, 7)),
                  pl.BlockSpec((TOK, 512), lambda i: (i, 8)), pl.BlockSpec((TOK, 256), lambda i: (i, 0)),
                  pl.BlockSpec((2, TOK, 128), lambda i: (0, i, 0)), r512, r512, _full((1, 512)),
                  pl.BlockSpec(memory_space=pl.ANY)],
        out_specs=[pl.BlockSpec((TOK, 1536), lambda i: (i, 2)), pl.BlockSpec((TOK, 256), lambda i: (i, 0)),
                   pl.BlockSpec((2, TOK, 128), lambda i: (0, i, 0)), r512, _full((1, 512))],
        out_shape=[SDS((m, NP), F32), SDS((m, 256), F32), SDS((2, m, 128), F32), SDS((m, 512), F32), SDS((1, 512), F32)],
        input_output_aliases={10: 0}, compiler_params=_cp(("arbitrary",)),
    )(dya_in, dyb_in, p, p, p, svr, svc, o_f, o_b, gbn, dp)


def _ln_bwd(dsr, vnr, dvnc, p, ws01t, ln_g, dp):
    m = p.shape[0]

    def body(dsr_ref, vnr_ref, dvc_ref, va_ref, wt_ref, g_ref, dpi_ref, dp_ref, dw_ref, db_ref, dlg_ref, dlb_ref, dvn_ref):
        i = pl.program_id(0)

        @pl.when(i == 0)
        def _():
            dw_ref[...] = jnp.zeros_like(dw_ref)
            db_ref[...] = jnp.zeros_like(db_ref)
            dlg_ref[...] = jnp.zeros_like(dlg_ref)
            dlb_ref[...] = jnp.zeros_like(dlb_ref)

        for j in range(TOK // AC):
            for g in range(2):
                d = dsr_ref[AC * j:AC * j + AC, AC * g:AC * g + AC]
                d16 = d.astype(BF16)
                dvn_ref[AC * j:AC * j + AC, AC * g:AC * g + AC] = _nn(wt_ref[g], d16)
                dw_ref[g] += _nt(d16, vnr_ref[AC * j:AC * j + AC, AC * g:AC * g + AC])
                db_ref[g] += jnp.sum(d, axis=1, keepdims=True)
        dvn_ref[:, 256:384] = dvc_ref[0]
        dvn_ref[:, 384:512] = dvc_ref[1]
        dvn = dvn_ref[...]
        xf = va_ref[...]
        xc = xf - jnp.mean(xf, axis=-1, keepdims=True)
        rs = lax.rsqrt(jnp.mean(xc * xc, axis=-1, keepdims=True) + EPS)
        xh = xc * rs
        dlg_ref[...] += jnp.sum(dvn * xh, axis=0, keepdims=True)
        dlb_ref[...] += jnp.sum(dvn, axis=0, keepdims=True)
        dxh = dvn * g_ref[...]
        dp_ref[...] = rs * (dxh - jnp.mean(dxh, axis=-1, keepdims=True) - xh * jnp.mean(dxh * xh, axis=-1, keepdims=True))

    return _pcall(
        body, name="ln_bwd", grid=(m // TOK,),
        in_specs=[pl.BlockSpec((TOK, 256), lambda i: (i, 0)), pl.BlockSpec((TOK, 256), lambda i: (i, 0)),
                  pl.BlockSpec((2, TOK, 128), lambda i: (0, i, 0)), pl.BlockSpec((TOK, 512), lambda i: (i, 9)),
                  _full((2, AC, AC)), _full((1, 512)), pl.BlockSpec(memory_space=pl.ANY)],
        out_specs=[pl.BlockSpec((TOK, 512), lambda i: (i, 9)), _full((2, AC, AC)), _full((2, AC, 1)), _full((1, 512)), _full((1, 512))],
        out_shape=[SDS((m, NP), F32), SDS((2, AC, AC), F32), SDS((2, AC, 1), F32), SDS((1, 512), F32), SDS((1, 512), F32)],
        scratch_shapes=[pltpu.VMEM((TOK, 512), F32)],
        input_output_aliases={6: 0}, compiler_params=_cp(("arbitrary",)),
    )(dsr, vnr, dvnc, p, ws01t, ln_g, dp)


def _tri_mm(tri, a):
    a1 = a.astype(BF16)
    r1 = a - a1.astype(F32)
    a2 = r1.astype(BF16)
    a3 = (r1 - a2.astype(F32)).astype(BF16)
    return _nn(tri, a1) + _nn(tri, a2) + _nn(tri, a3)


def _gla_masks(reverse):
    ri = lax.broadcasted_iota(jnp.int32, (CH, CH), 0)
    ci = lax.broadcasted_iota(jnp.int32, (CH, CH), 1)
    vis = (ci >= ri) if reverse else (ci <= ri)
    vis_t = (ci <= ri) if reverse else (ci >= ri)
    r4 = lax.broadcasted_iota(jnp.int32, (4 * CH, CH), 0) & (CH - 1)
    c4 = lax.broadcasted_iota(jnp.int32, (4 * CH, CH), 1)
    vis4 = (c4 >= r4) if reverse else (c4 <= r4)
    vis4_t = (c4 <= r4) if reverse else (c4 >= r4)
    lane = lax.broadcasted_iota(jnp.int32, (1, 256), 1)
    hm = [(lane >= CH * h) & (lane < CH * h + CH) for h in range(4)]
    return vis, vis_t, vis4, vis4_t, hm


def _stack_heads(x, hm):
    return jnp.concatenate([jnp.where(hm[h], x, 0.0).astype(BF16) for h in range(4)], axis=0)


def _diag_heads(full, hm):
    acc = jnp.where(hm[0], full[0:128], 0.0)
    for h in range(1, 4):
        acc = acc + jnp.where(hm[h], full[128 * h:128 * h + 128], 0.0)
    return acc


def _gla_fwd(p, qkv_blk, lr, lrw, gbias, s0, *, reverse, name):
    m = p.shape[0]
    nb = m // GLA_TB
    nc = GLA_TB // CH
    rmap = (lambda i: nb - 1 - i) if reverse else (lambda i: i)

    def body(qkv_ref, lr_ref, lrw_ref, gb_ref, s0_ref, o_ref, sb_ref, sfin_ref, st_ref):
        i = pl.program_id(0)

        @pl.when(i == 0)
        def _():
            st_ref[...] = s0_ref[...]

        vis, _, vis4, _, hm = _gla_masks(reverse)
        tri = vis.astype(F32).astype(BF16)
        logits = _nn(lr_ref[...].astype(BF16), lrw_ref[...]) + gb_ref[...]
        a_all = _logsig(logits) * (1.0 / 16.0)
        st = st_ref[...]
        for c in (range(nc - 1, -1, -1) if reverse else range(nc)):
            rows = slice(CH * c, CH * c + CH)
            b = _tri_mm(tri, a_all[rows])
            bl = b[0:1] if reverse else b[CH - 1:CH]
            q = qkv_ref[rows, 0:256] * 0.125
            k = qkv_ref[rows, 256:512]
            v16 = qkv_ref[rows, 512:1024].astype(BF16)
            qd = q * jnp.exp(b)
            kd16 = (k * jnp.exp(-b)).astype(BF16)
            kdec16 = (k * jnp.exp(bl - b)).astype(BF16)
            qstack = _stack_heads(qd, hm)
            sc = jnp.where(vis4, _nt(qstack, kd16), 0.0).astype(BF16)
            inter = _nt(qstack, st.astype(BF16))
            for h in range(4):
                o_ref[rows, 128 * h:128 * h + 128] = (
                    _nn(sc[CH * h:CH * h + CH], v16[:, 128 * h:128 * h + 128]) + inter[CH * h:CH * h + CH])
            sb_ref[c] = st
            st = st * jnp.exp(bl) + _diag_heads(_tn(v16, kdec16), hm)
        st_ref[...] = st

        @pl.when(i == nb - 1)
        def _():
            sfin_ref[...] = st

    return _pcall(
        body, name=name, grid=(nb,),
        in_specs=[pl.BlockSpec((GLA_TB, 1024), lambda i: (rmap(i), qkv_blk)), pl.BlockSpec((GLA_TB, LRW), lambda i: (rmap(i), 0)),
                  _full((LRW, 256)), _full((1, 256)), _full((128, 256))],
        out_specs=[pl.BlockSpec((GLA_TB, 512), lambda i: (rmap(i), 0)), pl.BlockSpec((nc, 128, 256), lambda i: (rmap(i), 0, 0)),
                   _full((128, 256))],
        out_shape=[SDS((m, 512), F32), SDS((m // CH, 128, 256), F32), SDS((128, 256), F32)],
        scratch_shapes=[pltpu.VMEM((128, 256), F32)], compiler_params=_cp(("arbitrary",)),
    )(p, lr, lrw, gbias, s0)


def _gla_bwd(p, qkv_blk, lr, lrw, lrwt, gbias, sb, dsfin, do, prev, dp, *, reverse, name):
    m = p.shape[0]
    nb = m // GLA_TB
    nc = GLA_TB // CH
    rmap = (lambda i: i) if reverse else (lambda i: nb - 1 - i)
    has_prev = prev is not None
    has_dp = dp is not None

    def body(*refs):
        refs = list(refs)
        qkv_ref, lr_ref, lrw_ref, lrwt_ref, gb_ref, sb_ref, dsfin_ref, do_ref = refs[:8]
        refs = refs[8:]
        if has_prev:
            pq_ref, plr_ref = refs[:2]
            refs = refs[2:]
        if has_dp:
            refs = refs[1:]
        dqkv_ref, dlr_ref, dw2_ref, dgb_ref, ds0_ref, dst_ref, dlog_ref = refs
        i = pl.program_id(0)

        @pl.when(i == 0)
        def _():
            dst_ref[...] = dsfin_ref[...]
            dw2_ref[...] = jnp.zeros_like(dw2_ref)
            dgb_ref[...] = jnp.zeros_like(dgb_ref)

        vis, vis_t, vis4, vis4_t, hm = _gla_masks(reverse)
        tri = vis.astype(F32).astype(BF16)
        tri_t = vis_t.astype(F32).astype(BF16)
        lr16 = lr_ref[...].astype(BF16)
        logits = _nn(lr16, lrw_ref[...]) + gb_ref[...]
        a_all = _logsig(logits) * (1.0 / 16.0)
        dsig = (1.0 - jax.nn.sigmoid(logits)) * (1.0 / 16.0)
        dst = dst_ref[...]
        for c in (range(nc) if reverse else range(nc - 1, -1, -1)):
            rows = slice(CH * c, CH * c + CH)
            b = _tri_mm(tri, a_all[rows])
            bl = b[0:1] if reverse else b[CH - 1:CH]
            eb = jnp.exp(b)
            enb = jnp.exp(-b)
            ebl = jnp.exp(bl - b)
            el = jnp.exp(bl)
            q = qkv_ref[rows, 0:256] * 0.125
            k = qkv_ref[rows, 256:512]
            v16 = qkv_ref[rows, 512:1024].astype(BF16)
            do16 = do_ref[rows, :].astype(BF16)
            qd = q * eb
            kd = k * enb
            kdec = k * ebl
            st = sb_ref[c]
            st16 = st.astype(BF16)
            dst16 = dst.astype(BF16)
            qd16 = qd.astype(BF16)
            kd16 = kd.astype(BF16)
            qstack = _stack_heads(qd, hm)
            kstack = _stack_heads(kd, hm)
            kdecstack = _stack_heads(kdec, hm)
            pt = jnp.where(vis4_t, _nt(kstack, qd16), 0.0).astype(BF16)
            dvinter = _nt(kdecstack, dst16)
            dqd = jnp.zeros((CH, 256), F32)
            dkd = jnp.zeros((CH, 256), F32)
            dkdec = jnp.zeros((CH, 256), F32)
            for h in range(4):
                hs = slice(128 * h, 128 * h + 128)
                rh = slice(CH * h, CH * h + CH)
                doh = do16[:, hs]
                vh = v16[:, hs]
                dpm = jnp.where(vis, _nt(doh, vh), 0.0).astype(BF16)
                dpt = jnp.where(vis_t, _nt(vh, doh), 0.0).astype(BF16)
                dv_h = _nn(pt[rh], doh) + dvinter[rh]
                if has_prev:
                    dv_h = dv_h + pq_ref[rows, 512 + 128 * h:512 + 128 * h + 128]
                dqkv_ref[rows, 512 + 128 * h:512 + 128 * h + 128] = dv_h
                dqd = dqd + _nn(dpm, kstack[rh]) + jnp.where(hm[h], _nn(doh, st16), 0.0)
                dkd = dkd + _nn(dpt, qstack[rh])
                dkdec = dkdec + jnp.where(hm[h], _nn(vh, dst16), 0.0)
            dq = dqd * eb * 0.125
            dk = dkd * enb + dkdec * ebl
            if has_prev:
                dq = dq + pq_ref[rows, 0:256]
                dk = dk + pq_ref[rows, 256:512]
            dqkv_ref[rows, 0:256] = dq
            dqkv_ref[rows, 256:512] = dk
            g_kdec = dkdec * kdec
            db = dqd * qd - dkd * kd - g_kdec
            dbl = jnp.sum(g_kdec, axis=0, keepdims=True) + jnp.sum(st * dst, axis=0, keepdims=True) * el
            da = _tri_mm(tri_t, db) + dbl
            dlog_ref[rows, :] = da * dsig[rows]
            dst = dst * el + _diag_heads(_tn(do16, qd16), hm)
        dst_ref[...] = dst
        dlog = dlog_ref[...]
        dlog16 = dlog.astype(BF16)
        dlr = _nn(dlog16, lrwt_ref[...])
        if has_prev:
            dlr = dlr + plr_ref[...]
        dlr_ref[...] = dlr
        dw2_ref[...] += _tn(lr16, dlog16)
        dgb_ref[...] += jnp.sum(dlog, axis=0, keepdims=True)

        @pl.when(i == nb - 1)
        def _():
            ds0_ref[...] = dst

    in_specs = [pl.BlockSpec((GLA_TB, 1024), lambda i: (rmap(i), qkv_blk)), pl.BlockSpec((GLA_TB, LRW), lambda i: (rmap(i), 0)),
                _full((LRW, 256)), _full((256, LRW)), _full((1, 256)), pl.BlockSpec((nc, 128, 256), lambda i: (rmap(i), 0, 0)),
                _full((128, 256)), pl.BlockSpec((GLA_TB, 512), lambda i: (rmap(i), 0))]
    args = [p, lr, lrw, lrwt, gbias, sb, dsfin, do]
    if has_prev:
        in_specs += [pl.BlockSpec((GLA_TB, 1024), lambda i: (rmap(i), 0)), pl.BlockSpec((GLA_TB, LRW), lambda i: (rmap(i), 0))]
        args += list(prev)
    aliases = {}
    if has_dp:
        in_specs.append(pl.BlockSpec(memory_space=pl.ANY))
        aliases = {len(args): 0}
        args.append(dp)
        dq_spec = pl.BlockSpec((GLA_TB, 1024), lambda i: (rmap(i), qkv_blk))
        dq_shape = SDS(dp.shape, F32)
    else:
        dq_spec = pl.BlockSpec((GLA_TB, 1024), lambda i: (rmap(i), 0))
        dq_shape = SDS((m, 1024), F32)
    return _pcall(
        body, name=name, grid=(nb,), in_specs=in_specs,
        out_specs=[dq_spec, pl.BlockSpec((GLA_TB, LRW), lambda i: (rmap(i), 0)), _full((LRW, 256)), _full((1, 256)), _full((128, 256))],
        out_shape=[dq_shape, SDS((m, LRW), F32), SDS((LRW, 256), F32), SDS((1, 256), F32), SDS((128, 256), F32)],
        scratch_shapes=[pltpu.VMEM((128, 256), F32), pltpu.VMEM((GLA_TB, 256), F32)],
        input_output_aliases=aliases, compiler_params=_cp(("arbitrary",)),
    )(*args)


def _device_step(x, c, ctx, c_ctx, tgt, wm, bm, ng, wi, wlr, ln_g, ln_b, ws, bs, w2, gb2, gbn, wpa, wpb, wo, gf):
    L = x.shape[0]
    wit = wi.T
    wlrt = wlr.T
    ws16 = ws.astype(BF16)
    wst16 = jnp.swapaxes(ws, 1, 2).astype(BF16)
    bscol = bs[:, :, None]
    lrw = [jnp.zeros((LRW, 256), F32).at[16 * r:16 * r + 16].set(w2[r]).astype(BF16) for r in range(2)]
    lrwt = [w.T for w in lrw]
    gbias = [gb2[r:r + 1] for r in range(2)]

    cc = jnp.zeros((8, D), F32).at[0:1].set(c).at[1:2].set(c_ctx)
    mod = _modvec(cc, wm, bm)
    shift, scale, gate = mod[0:1, 0:D], mod[0:1, D:2 * D], mod[0:1, 2 * D:3 * D]
    shift_c, scale_c = mod[1:2, 0:D], mod[1:2, D:2 * D]

    hc = _prep_h(ctx, ng, scale_c, shift_c, "prep_hc")
    pc = _mm(hc, wi[:, 2048:3072], tm=256, tn=1024, tk=D, out_dtype=F32, name="mm_pc")
    plrc = _mm(hc, wlr, tm=256, tn=LRW, tk=D, out_dtype=F32, name="mm_plrc")
    zero_s = jnp.zeros((128, 256), F32)
    _, sbc_f, sc_f = _gla_fwd(pc, 0, plrc, lrw[0], gbias[0], zero_s, reverse=False, name="gla_fwd_cf")
    _, sbc_b, sc_b = _gla_fwd(pc, 0, plrc, lrw[1], gbias[1], zero_s, reverse=True, name="gla_fwd_cb")

    h = _prep_h(x, ng, scale, shift, "prep_h")
    p = _mm(h, wi, tm=512, tn=1024, tk=D, out_dtype=F32, name="mm_p")
    plr = _mm(h, wlr, tm=1024, tn=LRW, tk=D, out_dtype=F32, name="mm_plr")
    o_f, sb_f, _ = _gla_fwd(p, 2, plr, lrw[0], gbias[0], sc_f, reverse=False, name="gla_fwd_f")
    o_b, sb_b, _ = _gla_fwd(p, 2, plr, lrw[1], gbias[1], sc_b, reverse=True, name="gla_fwd_b")
    vnr, vnc = _ln_fwd(p, ln_g, ln_b)
    svc = _colmix_fwd(vnc.reshape(2, AC, L), ws16[2:4], bscol[2:4]).reshape(2, L, 128)
    ya_in, yb_in, svr = _mid_fwd(o_f, o_b, p, vnr, svc, ws16[0:2], bscol[0:2], gbn)
    ya = _mm(ya_in, wpa, tm=1024, tn=D, tk=512, out_dtype=F32, name="mm_ya")
    yb = _mm(yb_in, wpb, tm=1024, tn=D, tk=512, out_dtype=F32, name="mm_yb")
    mrg = _merge_fwd(p, ya, yb)
    out = _mm(mrg, wo, tm=1024, tn=D, tk=D, out_dtype=F32, name="mm_out")
    dx1, dout, loss, dgate, dgf = _loss_head(x, out, tgt, gate, gf)

    dm = _mm(dout, wo.T, tm=1024, tn=D, tk=D, out_dtype=F32, name="mm_dm")
    dwo = _mm(mrg.T, dout, tm=D, tn=D, tk=1024, out_dtype=F32, name="mm_dwo")
    dya, dyb, dp = _merge_bwd(dm, ya, yb, p)
    dya_in = _mm(dya, wpa.T, tm=1024, tn=512, tk=D, out_dtype=F32, name="mm_dya_in")
    dyb_in = _mm(dyb, wpb.T, tm=1024, tn=512, tk=D, out_dtype=F32, name="mm_dyb_in")
    dwpa = _mm(ya_in.T, dya, tm=512, tn=D, tk=1024, out_dtype=F32, name="mm_dwpa")
    dwpb = _mm(yb_in.T, dyb, tm=512, tn=D, tk=1024, out_dtype=F32, name="mm_dwpb")
    dp, dsr, dsc, do, dgbn = _mid_bwd(dya_in, dyb_in, p, svr, svc, o_f, o_b, gbn, dp)
    dvnc, dws23, dbs23 = _colmix_bwd(dsc.reshape(2, AC, L), vnc.reshape(2, AC, L), wst16[2:4])
    dp, dws01, dbs01, dlng, dlnb = _ln_bwd(dsr, vnr, dvnc.reshape(2, L, 128), p, wst16[0:2], ln_g, dp)
    zero_ds = jnp.zeros((128, 256), F32)
    dqkv_f, dlr_f, dw2_f, dgb_f, ds0_f = _gla_bwd(p, 2, plr, lrw[0], lrwt[0], gbias[0], sb_f, zero_ds, do, None, None,
                                                  reverse=False, name="gla_bwd_f")
    dp, dlr, dw2_b, dgb_b, ds0_b = _gla_bwd(p, 2, plr, lrw[1], lrwt[1], gbias[1], sb_b, zero_ds, do, (dqkv_f, dlr_f), dp,
                                            reverse=True, name="gla_bwd_b")
    zero_do = jnp.zeros((ctx.shape[0], 512), F32)
    dqkvc_f, dlrc_f, dw2c_f, dgbc_f, _ = _gla_bwd(pc, 0, plrc, lrw[0], lrwt[0], gbias[0], sbc_f, ds0_f, zero_do, None, None,
                                                  reverse=False, name="gla_bwd_cf")
    dqkvc, dlrc, dw2c_b, dgbc_b, _ = _gla_bwd(pc, 0, plrc, lrw[1], lrwt[1], gbias[1], sbc_b, ds0_b, zero_do,
                                              (dqkvc_f, dlrc_f), None, reverse=True, name="gla_bwd_cb")
    dhc = _mm(dqkvc, wit[2048:3072], tm=256, tn=D, tk=1024, out_dtype=F32, name="mm_dhc")
    dhc = _mm(dlrc, wlrt, tm=256, tn=D, tk=LRW, out_dtype=F32, name="mm_dhc_lr", acc=dhc)
    _, dng_c, dscale_c, dshift_c = _prep_bwd(ctx, dhc, None, ng, scale_c, "prep_bwd_c")

    dh = _mm(dp, wit, tm=1024, tn=D, tk=1024, out_dtype=F32, name="mm_dh")
    dh = _mm(dlr, wlrt, tm=1024, tn=D, tk=LRW, out_dtype=F32, name="mm_dh_lr", acc=dh)
    ht = h.T
    hct = hc.T
    dwi = _mm(ht, dp, tm=D, tn=1024, tk=1024, out_dtype=F32, name="mm_dwi")
    dwi_qkv = _mm(hct, dqkvc, tm=D, tn=1024, tk=256, out_dtype=F32, name="mm_dwi_c", acc=dwi[:, 2048:3072])
    dwlr = _mm(ht, dlr, tm=D, tn=LRW, tk=1024, out_dtype=F32, name="mm_dwlr")
    dwlr = _mm(hct, dlrc, tm=D, tn=LRW, tk=256, out_dtype=F32, name="mm_dwlr_c", acc=dwlr)
    dx, dng, dscale, dshift = _prep_bwd(x, dh, dx1, ng, scale, "prep_bwd")

    dmodc = jnp.concatenate([dshift_c, dscale_c], axis=1)
    dscc = _dcctx(jnp.zeros((8, 2 * D), F32).at[0:1].set(dmodc), wm)[0:1]
    dw2p = dw2_f + dw2c_f, dw2_b + dw2c_b
    return dict(
        loss=loss[0, 0], dx=dx, dwi=dwi, dwi_qkv=dwi_qkv, dwlr=dwlr, dwpa=dwpa, dwpb=dwpb, dwo=dwo,
        dmod=jnp.concatenate([dshift, dscale, dgate], axis=1), dmodc=dmodc, dscc=dscc, dng=dng + dng_c,
        dlng=dlng, dlnb=dlnb, dws=jnp.concatenate([dws01, dws23], axis=0),
        dbs=jnp.concatenate([dbs01, dbs23], axis=0)[:, :, 0], dgbn=dgbn, dgf=dgf,
        dw2=jnp.stack([dw2p[0][0:16], dw2p[1][16:32]]), dgb2=jnp.concatenate([dgb_f + dgbc_f, dgb_b + dgbc_b], axis=0),
    )


ANY = pl.BlockSpec(memory_space=pl.ANY)


def _coords():
    return lax.axis_index("x"), lax.axis_index("y"), lax.axis_index("c")


def _flip(v, bit):
    return 1 - v if bit else v


def _gather_xy(arrs, name):
    n = len(arrs)

    def body(*refs):
        ins, outs = refs[:n], refs[n:2 * n]
        send_sems, recv_sems, loc_sems = refs[2 * n:]
        x, y, c = _coords()
        me = 2 * x + y
        peers = [(1 - x, y), (x, 1 - y), (1 - x, 1 - y)]
        locs, sends = [], []
        for k in range(n):
            lc = pltpu.make_async_copy(ins[k], outs[k].at[me], loc_sems.at[k])
            lc.start()
            locs.append(lc)
            for j, (px, py) in enumerate(peers):
                rc = pltpu.make_async_remote_copy(
                    src_ref=ins[k], dst_ref=outs[k].at[me], send_sem=send_sems.at[3 * k + j], recv_sem=recv_sems.at[3 * k + j],
                    device_id=(px, py, c), device_id_type=MESH)
                rc.start()
                sends.append(rc)
        for k in range(n):
            for j, (px, py) in enumerate(peers):
                pltpu.make_async_remote_copy(
                    src_ref=ins[k], dst_ref=outs[k].at[2 * px + py], send_sem=send_sems.at[3 * k + j],
                    recv_sem=recv_sems.at[3 * k + j], device_id=(px, py, c), device_id_type=MESH).wait_recv()
        for rc in sends:
            rc.wait_send()
        for lc in locs:
            lc.wait()

    return _pcall(
        body, name=name, in_specs=[ANY] * n, out_specs=[ANY] * n,
        out_shape=[SDS((4,) + a.shape, a.dtype) for a in arrs],
        scratch_shapes=[pltpu.SemaphoreType.DMA((3 * n,)), pltpu.SemaphoreType.DMA((3 * n,)), pltpu.SemaphoreType.DMA((n,))],
    )(*arrs)


def _gather_all(a, name):
    masks = [(mx, my, mc) for mx in range(2) for my in range(2) for mc in range(2)][1:]

    def body(in_ref, out_ref, send_sems, recv_sems, loc_sem):
        x, y, c = _coords()
        me = 4 * x + 2 * y + c
        lc = pltpu.make_async_copy(in_ref, out_ref.at[me], loc_sem)
        lc.start()
        sends = []
        for j, (mx, my, mc) in enumerate(masks):
            peer = (_flip(x, mx), _flip(y, my), _flip(c, mc))
            rc = pltpu.make_async_remote_copy(
                src_ref=in_ref, dst_ref=out_ref.at[me], send_sem=send_sems.at[j], recv_sem=recv_sems.at[j],
                device_id=peer, device_id_type=MESH)
            rc.start()
            sends.append(rc)
        for j, (mx, my, mc) in enumerate(masks):
            px, py, pc = _flip(x, mx), _flip(y, my), _flip(c, mc)
            pltpu.make_async_remote_copy(
                src_ref=in_ref, dst_ref=out_ref.at[4 * px + 2 * py + pc], send_sem=send_sems.at[j], recv_sem=recv_sems.at[j],
                device_id=(px, py, pc), device_id_type=MESH).wait_recv()
        for rc in sends:
            rc.wait_send()
        lc.wait()

    return _pcall(
        body, name=name, in_specs=[ANY], out_specs=ANY, out_shape=SDS((8,) + a.shape, a.dtype),
        scratch_shapes=[pltpu.SemaphoreType.DMA((7,)), pltpu.SemaphoreType.DMA((7,)), pltpu.SemaphoreType.DMA(())],
    )(a)


def _swap_c(a, name):
    def body(in_ref, out_ref, send_sem, recv_sem):
        x, y, c = _coords()
        rc = pltpu.make_async_remote_copy(src_ref=in_ref, dst_ref=out_ref, send_sem=send_sem, recv_sem=recv_sem,
                                          device_id=(x, y, 1 - c), device_id_type=MESH)
        rc.start()
        rc.wait()

    return _pcall(
        body, name=name, in_specs=[ANY], out_specs=ANY, out_shape=SDS(a.shape, a.dtype),
        scratch_shapes=[pltpu.SemaphoreType.DMA(()), pltpu.SemaphoreType.DMA(())],
    )(a)


def _a2a_xy(a, name):
    def body(in_ref, out_ref, send_sems, recv_sems, loc_sem):
        x, y, c = _coords()
        me = 2 * x + y
        peers = [(1 - x, y), (x, 1 - y), (1 - x, 1 - y)]
        lc = pltpu.make_async_copy(in_ref.at[me], out_ref.at[me], loc_sem)
        lc.start()
        sends = []
        for j, (px, py) in enumerate(peers):
            rc = pltpu.make_async_remote_copy(
                src_ref=in_ref.at[2 * px + py], dst_ref=out_ref.at[me], send_sem=send_sems.at[j], recv_sem=recv_sems.at[j],
                device_id=(px, py, c), device_id_type=MESH)
            rc.start()
            sends.append(rc)
        for j, (px, py) in enumerate(peers):
            pltpu.make_async_remote_copy(
                src_ref=in_ref.at[me], dst_ref=out_ref.at[2 * px + py], send_sem=send_sems.at[j], recv_sem=recv_sems.at[j],
                device_id=(px, py, c), device_id_type=MESH).wait_recv()
        for rc in sends:
            rc.wait_send()
        lc.wait()

    return _pcall(
        body, name=name, in_specs=[ANY], out_specs=ANY, out_shape=SDS(a.shape, a.dtype),
        scratch_shapes=[pltpu.SemaphoreType.DMA((3,)), pltpu.SemaphoreType.DMA((3,)), pltpu.SemaphoreType.DMA(())],
    )(a)


def _sum_slots(a, name, rows):
    s, n, _ = a.shape

    def body(a_ref, o_ref):
        acc = a_ref[0]
        for k in range(1, s):
            acc = acc + a_ref[k]
        o_ref[...] = acc

    return _pcall(
        body, name=name, grid=(n // rows,), in_specs=[pl.BlockSpec((s, rows, 128), lambda i: (0, i, 0))],
        out_specs=pl.BlockSpec((rows, 128), lambda i: (i, 0)), out_shape=SDS((n, 128), F32),
        compiler_params=_cp(("parallel",)),
    )(a)


def _adamw(w, g, m, v, name, rows):
    r, cdim = w.shape

    def body(w_ref, g_ref, m_ref, v_ref, d_ref, nm_ref, nv_ref):
        g_ = g_ref[...]
        nm = ADAM_B1 * m_ref[...] + (1.0 - ADAM_B1) * g_
        nv = ADAM_B2 * v_ref[...] + (1.0 - ADAM_B2) * (g_ * g_)
        m_hat = nm / (1.0 - ADAM_B1 ** ADAM_STEP)
        v_hat = nv / (1.0 - ADAM_B2 ** ADAM_STEP)
        d_ref[...] = -ADAM_LR * (m_hat / (jnp.sqrt(v_hat) + ADAM_EPS) + ADAM_WD * w_ref[...])
        nm_ref[...] = nm
        nv_ref[...] = nv

    blk = pl.BlockSpec((rows, cdim), lambda i: (i, 0))
    return _pcall(
        body, name=name, grid=(r // rows,), in_specs=[blk] * 4, out_specs=[blk] * 3,
        out_shape=[SDS(w.shape, F32)] * 3, compiler_params=_cp(("parallel",)),
    )(w, g, m, v)


def _pack(pieces, rows):
    flat = jnp.concatenate([p.reshape(-1) for p in pieces])
    return jnp.pad(flat, (0, rows * 128 - flat.shape[0])).reshape(rows, 128)


def _unpack(buf, shapes):
    flat = buf.reshape(-1)
    out, off = [], 0
    for shp in shapes:
        size = 1
        for s in shp:
            size *= s
        out.append(flat[off:off + size].reshape(shp))
        off += size
    return out


def _perm_cols(w):
    perm = jnp.concatenate([w[..., 3104:5152], w[..., 0:1024], w[..., 1056:1568], w[..., 1568:2080], w[..., 2592:3104],
                            w[..., 2080:2592]], axis=-1)
    return perm, w[..., 1024:1056]


def _unperm_cols(perm, lr32):
    return jnp.concatenate([perm[..., 2048:3072], lr32, perm[..., 3072:3584], perm[..., 3584:4096], perm[..., 4608:5120],
                            perm[..., 4096:4608], perm[..., 0:2048]], axis=-1)


SMALL_ROWS = 672
HALF_ROWS = 7200


def kernel(x, c, ctx, c_ctx, w_mod, b_mod, norm_g, w_in, a_ln_g, a_ln_b, a_ws, a_bs, b_gate_w2, b_gate_b, b_norm_g, w_proj_a, w_proj_b, w_out, final_norm_g, loss_target, m_c_ctx, m_w_mod, m_b_mod, m_norm_g, m_w_in, m_a_ln_g, m_a_ln_b, m_a_ws, m_a_bs, m_b_gate_w2, m_b_gate_b, m_b_norm_g, m_w_proj_a, m_w_proj_b, m_w_out, m_final_norm_g, v_c_ctx, v_w_mod, v_b_mod, v_norm_g, v_w_in, v_a_ln_g, v_a_ln_b, v_a_ws, v_a_bs, v_b_gate_w2, v_b_gate_b, v_b_norm_g, v_w_proj_a, v_w_proj_b, v_w_out, v_final_norm_g):
    xi, yi, ci = _coords()
    me_xy = 2 * xi + yi

    gate_pack = _pack([b_gate_w2[0], b_gate_b[0]], 24)
    g_wi, g_wm, g_wpa, g_wpb, g_wo, g_gate = _gather_xy(
        [w_in[0].astype(BF16), w_mod[0].astype(BF16), w_proj_a[0].astype(BF16), w_proj_b[0].astype(BF16),
         w_out[0].astype(BF16), gate_pack], "gather_weights")
    wi_full = jnp.swapaxes(g_wi, 0, 1).reshape(D, 4 * 1288)
    wi, wlr32 = _perm_cols(wi_full)
    wlr = jnp.pad(wlr32, ((0, 0), (0, LRW - 32)))
    wm = jnp.swapaxes(g_wm, 0, 1).reshape(D, 3 * D)
    wpa = jnp.swapaxes(g_wpa, 0, 1).reshape(512, D)
    wpb = jnp.swapaxes(g_wpb, 0, 1).reshape(512, D)
    wo = g_wo.reshape(D, D)
    gflat = g_gate.reshape(4, 24 * 128)
    w2 = jnp.swapaxes(gflat[:, 0:2048].reshape(4, 2, 16, 64), 0, 2)
    w2 = jnp.swapaxes(w2, 0, 1).reshape(2, 16, 256)
    gb2 = jnp.swapaxes(gflat[:, 2048:2176].reshape(4, 2, 64), 0, 1).reshape(2, 256)

    r = _device_step(x[0], c, ctx[0], c_ctx[None], loss_target[0], wm, b_mod, norm_g, wi, wlr, a_ln_g, a_ln_b, a_ws[0], a_bs[0],
                     w2, gb2, b_norm_g, wpa, wpb, wo, final_norm_g[None])

    small = _pack([r["dmod"], c, r["dmodc"], r["dscc"], r["dng"], r["dlng"], r["dlnb"], r["dws"], r["dbs"], r["dgbn"], r["dgf"],
                   r["dw2"], r["dgb2"], jnp.broadcast_to(r["loss"], (128,))], SMALL_ROWS)
    small_all = _gather_all(small, "gather_small")
    small_sum = _sum_slots(small_all, "sum_small", SMALL_ROWS // 4)
    (s_dmod, _, s_dmodc, s_dscc, s_dng, s_dlng, s_dlnb, s_dws, s_dbs, s_dgbn, s_dgf, s_dw2, s_dgb2, s_loss) = _unpack(
        small_sum, [(1, 3 * D), (1, D), (1, 2 * D), (D,), (1, D), (1, 512), (1, 512), (1, 4, 128, 128), (1, 4, 128), (1, 512),
                    (D,), (2, 16, 256), (2, 256), (128,)])
    loss = s_loss[0]
    s_dmodc_p = jnp.pad(s_dmodc, ((0, 0), (0, D)))
    g_b_mod = s_dmod + s_dmodc_p
    sg = jax.nn.sigmoid(c_ctx)
    g_c_ctx = s_dscc * (sg * (1.0 + c_ctx * (1.0 - sg)))
    g_w2 = lax.dynamic_slice_in_dim(s_dw2, 64 * me_xy, 64, axis=2)[None]
    g_gb2 = lax.dynamic_slice_in_dim(s_dgb2, 64 * me_xy, 64, axis=1)[None]

    flat_all = small_all.reshape(8, SMALL_ROWS * 128)
    dmod_all = flat_all[:, 0:3 * D]
    c_all = flat_all[:, 3 * D:4 * D]
    lhs = jnp.concatenate([_silu(c_all), _silu(c_ctx)[None], jnp.zeros((7, D), F32)], axis=0)
    rhs = jnp.concatenate([dmod_all, s_dmodc_p, jnp.zeros((7, 3 * D), F32)], axis=0)
    rhs = lax.dynamic_slice_in_dim(rhs, 768 * me_xy, 768, axis=1)
    g_w_mod = _mm(lhs.T.astype(BF16), rhs.astype(BF16), tm=D, tn=768, tk=16, out_dtype=F32, name="mm_dwm")

    dwi_perm = r["dwi"].at[:, 2048:3072].set(r["dwi_qkv"])
    dwi_full = _unperm_cols(dwi_perm, r["dwlr"][:, 0:32])
    big = jnp.concatenate([
        jnp.swapaxes(dwi_full.reshape(D, 4, 1288), 0, 1).reshape(4, -1),
        jnp.swapaxes(r["dwpa"].reshape(512, 4, 256), 0, 1).reshape(4, -1),
        jnp.swapaxes(r["dwpb"].reshape(512, 4, 256), 0, 1).reshape(4, -1),
        r["dwo"].reshape(4, -1)], axis=1).reshape(4, 2 * HALF_ROWS, 128)
    mine = lax.dynamic_slice_in_dim(big, ci * HALF_ROWS, HALF_ROWS, axis=1)
    theirs = lax.dynamic_slice_in_dim(big, (1 - ci) * HALF_ROWS, HALF_ROWS, axis=1)
    got = _swap_c(theirs, "swap_half_in")
    pair = _sum_slots(jnp.stack([mine, got]).reshape(2, 4 * HALF_ROWS, 128), "sum_pair", 1800).reshape(4, HALF_ROWS, 128)
    parts = _a2a_xy(pair, "a2a_grads")
    half = _sum_slots(parts, "sum_chips", 1800)
    other = _swap_c(half, "swap_half_out")
    lo = jnp.where(ci == 0, half, other)
    hi = jnp.where(ci == 0, other, half)
    g_w_in, g_wpa, g_wpb, g_wo = _unpack(jnp.concatenate([lo, hi], axis=0), [(D, 1288), (512, 256), (512, 256), (256, D)])

    d_w_in, nm_w_in, nv_w_in = _adamw(w_in[0], g_w_in, m_w_in[0], v_w_in[0], "adamw_w_in", 256)
    d_w_mod, nm_w_mod, nv_w_mod = _adamw(w_mod[0], g_w_mod, m_w_mod[0], v_w_mod[0], "adamw_w_mod", 256)
    d_wpa, nm_wpa, nv_wpa = _adamw(w_proj_a[0], g_wpa, m_w_proj_a[0], v_w_proj_a[0], "adamw_wpa", 256)
    d_wpb, nm_wpb, nv_wpb = _adamw(w_proj_b[0], g_wpb, m_w_proj_b[0], v_w_proj_b[0], "adamw_wpb", 256)
    d_wo, nm_wo, nv_wo = _adamw(w_out[0], g_wo, m_w_out[0], v_w_out[0], "adamw_wo", 256)

    names = ["c_ctx", "b_mod", "norm_g", "a_ln_g", "a_ln_b", "a_ws", "a_bs", "b_gate_w2", "b_gate_b", "b_norm_g", "final_norm_g"]
    ws_ = [c_ctx, b_mod, norm_g, a_ln_g, a_ln_b, a_ws, a_bs, b_gate_w2, b_gate_b, b_norm_g, final_norm_g]
    gs_ = [g_c_ctx, g_b_mod, s_dng, s_dlng, s_dlnb, s_dws, s_dbs, g_w2, g_gb2, s_dgbn, s_dgf]
    ms_ = [m_c_ctx, m_b_mod, m_norm_g, m_a_ln_g, m_a_ln_b, m_a_ws, m_a_bs, m_b_gate_w2, m_b_gate_b, m_b_norm_g, m_final_norm_g]
    vs_ = [v_c_ctx, v_b_mod, v_norm_g, v_a_ln_g, v_a_ln_b, v_a_ws, v_a_bs, v_b_gate_w2, v_b_gate_b, v_b_norm_g, v_final_norm_g]
    shapes = [w.shape for w in ws_]
    gs_ = [g.reshape(s) for g, s in zip(gs_, shapes)]
    d_s, nm_s, nv_s = _adamw(_pack(ws_, 600), _pack(gs_, 600), _pack(ms_, 600), _pack(vs_, 600), "adamw_small", 600)
    d_small = dict(zip(names, _unpack(d_s, shapes)))
    nm_small = dict(zip(names, _unpack(nm_s, shapes)))
    nv_small = dict(zip(names, _unpack(nv_s, shapes)))
    g_small = dict(zip(names, gs_))

    order = ["c_ctx", "w_mod", "b_mod", "norm_g", "w_in", "a_ln_g", "a_ln_b", "a_ws", "a_bs", "b_gate_w2", "b_gate_b", "b_norm_g",
             "w_proj_a", "w_proj_b", "w_out", "final_norm_g"]
    big_g = dict(w_mod=g_w_mod[None], w_in=g_w_in[None], w_proj_a=g_wpa[None], w_proj_b=g_wpb[None], w_out=g_wo[None])
    big_d = dict(w_mod=d_w_mod[None], w_in=d_w_in[None], w_proj_a=d_wpa[None], w_proj_b=d_wpb[None], w_out=d_wo[None])
    big_m = dict(w_mod=nm_w_mod[None], w_in=nm_w_in[None], w_proj_a=nm_wpa[None], w_proj_b=nm_wpb[None], w_out=nm_wo[None])
    big_v = dict(w_mod=nv_w_mod[None], w_in=nv_w_in[None], w_proj_a=nv_wpa[None], w_proj_b=nv_wpb[None], w_out=nv_wo[None])
    grads = [big_g[n] if n in big_g else g_small[n] for n in order]
    deltas = [big_d[n] if n in big_d else d_small[n] for n in order]
    new_m = [big_m[n] if n in big_m else nm_small[n] for n in order]
    new_v = [big_v[n] if n in big_v else nv_small[n] for n in order]
    return (loss, r["dx"][None], *grads, *deltas, *new_m, *new_v)
```

```python
import functools

import jax
import jax.numpy as jnp
from jax import lax
from jax.experimental import pallas as pl
from jax.experimental.pallas import tpu as pltpu

F32 = jnp.float32
BF16 = jnp.bfloat16
SDS = jax.ShapeDtypeStruct

D = 1024
NP = 5120
LRW = 128
CH = 64
AC = 128
EPS = 1e-6
TOK = 256
GLA_TB = 256
VMEM_BIG = 48 * 1024 * 1024

ADAM_LR, ADAM_B1, ADAM_B2, ADAM_EPS, ADAM_WD, ADAM_STEP = 0.001, 0.9, 0.999, 1e-08, 0.01, 10

_pcall = pl.pallas_call
MESH = pl.DeviceIdType.MESH


def _cp(sem=None, vmem=None):
    kw = {}
    if sem is not None:
        kw["dimension_semantics"] = sem
    if vmem is not None:
        kw["vmem_limit_bytes"] = vmem
    return pltpu.CompilerParams(**kw)


def _silu(x):
    return x * jax.nn.sigmoid(x)


def _dsilu(x):
    s = jax.nn.sigmoid(x)
    return s * (1.0 + x * (1.0 - s))


def _logsig(x):
    return jnp.minimum(x, 0.0) - jnp.log1p(jnp.exp(-jnp.abs(x)))


def _nt(a, b):
    return lax.dot_general(a, b, (((1,), (1,)), ((), ())), preferred_element_type=F32)


def _tn(a, b):
    return lax.dot_general(a, b, (((0,), (0,)), ((), ())), preferred_element_type=F32)


def _nn(a, b):
    return jnp.dot(a, b, preferred_element_type=F32)


def _full(shape):
    return pl.BlockSpec(shape, lambda *_: (0,) * len(shape))


def _mm(a, b, *, tm, tn, tk, out_dtype, name, acc=None):
    m, k = a.shape
    k2, n = b.shape
    assert k == k2 and m % tm == 0 and n % tn == 0 and k % tk == 0, (a.shape, b.shape, tm, tn, tk)
    nk = k // tk
    has_acc = acc is not None

    def body(*refs):
        if has_acc:
            a_ref, b_ref, c_ref, o_ref, acc_ref = refs
        else:
            a_ref, b_ref, o_ref, acc_ref = refs
        kk = pl.program_id(2)
        part = _nn(a_ref[...].astype(BF16), b_ref[...].astype(BF16))

        @pl.when(kk == 0)
        def _():
            if has_acc:
                acc_ref[...] = c_ref[...] + part
            else:
                acc_ref[...] = part

        @pl.when(kk > 0)
        def _():
            acc_ref[...] += part

        @pl.when(kk == nk - 1)
        def _():
            o_ref[...] = acc_ref[...].astype(out_dtype)

    in_specs = [pl.BlockSpec((tm, tk), lambda i, j, kk: (i, kk)), pl.BlockSpec((tk, tn), lambda i, j, kk: (kk, j))]
    args = [a, b]
    if has_acc:
        in_specs.append(pl.BlockSpec((tm, tn), lambda i, j, kk: (i, j)))
        args.append(acc)
    return _pcall(
        body, name=name, grid=(m // tm, n // tn, nk), in_specs=in_specs,
        out_specs=pl.BlockSpec((tm, tn), lambda i, j, kk: (i, j)),
        out_shape=SDS((m, n), out_dtype), scratch_shapes=[pltpu.VMEM((tm, tn), F32)],
        compiler_params=_cp(("parallel", "parallel", "arbitrary"), VMEM_BIG),
    )(*args)


def _modvec(cc, wm, bm):
    def body(c_ref, w_ref, b_ref, o_ref):
        o_ref[...] = _nn(_silu(c_ref[...]).astype(BF16), w_ref[...]) + b_ref[...]

    return _pcall(body, name="modvec", out_shape=SDS((8, 3 * D), F32), compiler_params=_cp(None, VMEM_BIG))(cc, wm, bm)


def _dcctx(dmodc, wm):
    def body(d_ref, w_ref, o_ref):
        o_ref[...] = _nt(d_ref[...].astype(BF16), w_ref[...])

    return _pcall(
        body, name="dcctx", grid=(1,), in_specs=[_full((8, 2 * D)), pl.BlockSpec((D, 2 * D), lambda i: (0, 0))],
        out_specs=_full((8, D)), out_shape=SDS((8, D), F32), compiler_params=_cp(("arbitrary",), VMEM_BIG),
    )(dmodc, wm)


def _prep_h(x, ng, scale, shift, name):
    m = x.shape[0]

    def body(x_ref, g_ref, sc_ref, sh_ref, h_ref):
        xf = x_ref[...]
        r = lax.rsqrt(jnp.mean(xf * xf, axis=-1, keepdims=True) + EPS)
        y = (xf * r) * g_ref[...]
        h_ref[...] = (y * (1.0 + sc_ref[...]) + sh_ref[...]).astype(BF16)

    row = pl.BlockSpec((TOK, D), lambda i: (i, 0))
    return _pcall(
        body, name=name, grid=(m // TOK,), in_specs=[row, _full((1, D)), _full((1, D)), _full((1, D))],
        out_specs=row, out_shape=SDS((m, D), BF16), compiler_params=_cp(("parallel",)),
    )(x, ng, scale, shift)


def _prep_bwd(x, dh, dx1, ng, scale, name):
    m = x.shape[0]
    has_res = dx1 is not None

    def body(*refs):
        if has_res:
            x_ref, dh_ref, r_ref, g_ref, sc_ref, dx_ref, dg_ref, dsc_ref, dsh_ref = refs
        else:
            x_ref, dh_ref, g_ref, sc_ref, dx_ref, dg_ref, dsc_ref, dsh_ref = refs
        i = pl.program_id(0)

        @pl.when(i == 0)
        def _():
            dg_ref[...] = jnp.zeros_like(dg_ref)
            dsc_ref[...] = jnp.zeros_like(dsc_ref)
            dsh_ref[...] = jnp.zeros_like(dsh_ref)

        xf = x_ref[...]
        dh_ = dh_ref[...]
        r = lax.rsqrt(jnp.mean(xf * xf, axis=-1, keepdims=True) + EPS)
        xh = xf * r
        y = xh * g_ref[...]
        dsh_ref[...] += jnp.sum(dh_, axis=0, keepdims=True)
        dsc_ref[...] += jnp.sum(dh_ * y, axis=0, keepdims=True)
        dy = dh_ * (1.0 + sc_ref[...])
        dg_ref[...] += jnp.sum(dy * xh, axis=0, keepdims=True)
        dxh = dy * g_ref[...]
        dx = r * (dxh - xh * jnp.mean(dxh * xh, axis=-1, keepdims=True))
        if has_res:
            dx = dx + r_ref[...]
        dx_ref[...] = dx

    row = pl.BlockSpec((TOK, D), lambda i: (i, 0))
    vec = _full((1, D))
    in_specs = [row, row] + ([row] if has_res else []) + [vec, vec]
    args = [x, dh] + ([dx1] if has_res else []) + [ng, scale]
    return _pcall(
        body, name=name, grid=(m // TOK,), in_specs=in_specs, out_specs=[row, vec, vec, vec],
        out_shape=[SDS((m, D), F32), SDS((1, D), F32), SDS((1, D), F32), SDS((1, D), F32)],
        compiler_params=_cp(("arbitrary",)),
    )(*args)


def _ln_fwd(p, ln_g, ln_b):
    m = p.shape[0]

    def body(va_ref, g_ref, b_ref, vr_ref, vc_ref):
        xf = va_ref[...]
        xc = xf - jnp.mean(xf, axis=-1, keepdims=True)
        y = xc * lax.rsqrt(jnp.mean(xc * xc, axis=-1, keepdims=True) + EPS)
        vn = y * g_ref[...] + b_ref[...]
        vr_ref[...] = vn[:, 0:256].astype(BF16)
        vc_ref[0] = vn[:, 256:384].astype(BF16)
        vc_ref[1] = vn[:, 384:512].astype(BF16)

    return _pcall(
        body, name="ln_fwd", grid=(m // TOK,),
        in_specs=[pl.BlockSpec((TOK, 512), lambda i: (i, 9)), _full((1, 512)), _full((1, 512))],
        out_specs=[pl.BlockSpec((TOK, 256), lambda i: (i, 0)), pl.BlockSpec((2, TOK, 128), lambda i: (0, i, 0))],
        out_shape=[SDS((m, 256), BF16), SDS((2, m, 128), BF16)], compiler_params=_cp(("parallel",)),
    )(p, ln_g, ln_b)


COLB = 2048


def _colmix_fwd(vnc, ws23, bs23):
    rows = vnc.shape[2] // COLB

    def body(v_ref, w_ref, b_ref, o_ref):
        o_ref[0] = _nn(w_ref[0], v_ref[0]) + b_ref[0]

    return _pcall(
        body, name="colmix_fwd", grid=(2, rows),
        in_specs=[pl.BlockSpec((1, AC, COLB), lambda g, j: (g, 0, j)), pl.BlockSpec((1, AC, AC), lambda g, j: (g, 0, 0)),
                  pl.BlockSpec((1, AC, 1), lambda g, j: (g, 0, 0))],
        out_specs=pl.BlockSpec((1, AC, COLB), lambda g, j: (g, 0, j)),
        out_shape=SDS(vnc.shape, F32), compiler_params=_cp(("parallel", "parallel")),
    )(vnc, ws23, bs23)


def _colmix_bwd(dsvc, vnc, ws23t):
    rows = vnc.shape[2] // COLB

    def body(d_ref, v_ref, wt_ref, dv_ref, dw_ref, db_ref):
        j = pl.program_id(1)

        @pl.when(j == 0)
        def _():
            dw_ref[...] = jnp.zeros_like(dw_ref)
            db_ref[...] = jnp.zeros_like(db_ref)

        d = d_ref[0]
        d16 = d.astype(BF16)
        dv_ref[0] = _nn(wt_ref[0], d16)
        dw_ref[0] += _nt(d16, v_ref[0])
        db_ref[0] += jnp.sum(d, axis=1, keepdims=True)

    blk = pl.BlockSpec((1, AC, COLB), lambda g, j: (g, 0, j))
    return _pcall(
        body, name="colmix_bwd", grid=(2, rows),
        in_specs=[blk, blk, pl.BlockSpec((1, AC, AC), lambda g, j: (g, 0, 0))],
        out_specs=[blk, pl.BlockSpec((1, AC, AC), lambda g, j: (g, 0, 0)), pl.BlockSpec((1, AC, 1), lambda g, j: (g, 0, 0))],
        out_shape=[SDS(vnc.shape, F32), SDS((2, AC, AC), F32), SDS((2, AC, 1), F32)],
        compiler_params=_cp(("parallel", "arbitrary")),
    )(dsvc, vnc, ws23t)


def _head_norm(o, gbn):
    out = []
    for h in range(4):
        oh = o[:, 128 * h:128 * h + 128]
        r = lax.rsqrt(jnp.mean(oh * oh, axis=-1, keepdims=True) + EPS)
        out.append((r, oh * r))
    return out


def _mid_fwd(o_f, o_b, p, vnr, svc, ws01, bs01, gbn):
    m = p.shape[0]

    def body(of_ref, ob_ref, zb_ref, ua_ref, za_ref, vnr_ref, svc_ref, w_ref, b_ref, g_ref, ya_ref, yb_ref, svr_ref):
        o = of_ref[...] + ob_ref[...]
        zb = zb_ref[...]
        parts = []
        for h, (r, xh) in enumerate(_head_norm(o, None)):
            parts.append(xh * g_ref[:, 128 * h:128 * h + 128])
        on = jnp.concatenate(parts, axis=1)
        yb_ref[...] = (on * _silu(zb)).astype(BF16)
        for j in range(TOK // AC):
            for g in range(2):
                sv = _nn(w_ref[g], vnr_ref[AC * j:AC * j + AC, AC * g:AC * g + AC]) + b_ref[g]
                svr_ref[AC * j:AC * j + AC, AC * g:AC * g + AC] = sv
        sz = _silu(za_ref[...])
        u = ua_ref[...]
        ya_ref[:, 0:256] = ((u[:, 0:256] * svr_ref[...]) * sz[:, 0:256]).astype(BF16)
        ya_ref[:, 256:384] = ((u[:, 256:384] * svc_ref[0]) * sz[:, 256:384]).astype(BF16)
        ya_ref[:, 384:512] = ((u[:, 384:512] * svc_ref[1]) * sz[:, 384:512]).astype(BF16)

    r512 = pl.BlockSpec((TOK, 512), lambda i: (i, 0))
    return _pcall(
        body, name="mid_fwd", grid=(m // TOK,),
        in_specs=[r512, r512, pl.BlockSpec((TOK, 512), lambda i: (i, 6)), pl.BlockSpec((TOK, 512), lambda i: (i, 7)),
                  pl.BlockSpec((TOK, 512), lambda i: (i, 8)), pl.BlockSpec((TOK, 256), lambda i: (i, 0)),
                  pl.BlockSpec((2, TOK, 128), lambda i: (0, i, 0)), _full((2, AC, AC)), _full((2, AC, 1)), _full((1, 512))],
        out_specs=[r512, r512, pl.BlockSpec((TOK, 256), lambda i: (i, 0))],
        out_shape=[SDS((m, 512), BF16), SDS((m, 512), BF16), SDS((m, 256), F32)],
        compiler_params=_cp(("parallel",)),
    )(o_f, o_b, p, p, p, vnr, svc, ws01, bs01, gbn)


def _merge_fwd(p, ya, yb):
    m = p.shape[0]

    def body(ga_ref, gb_ref, ya_ref, yb_ref, m_ref):
        m_ref[...] = (jax.nn.sigmoid(ga_ref[...]) * ya_ref[...] + jax.nn.sigmoid(gb_ref[...]) * yb_ref[...]).astype(BF16)

    row = pl.BlockSpec((TOK, D), lambda i: (i, 0))
    return _pcall(
        body, name="merge_fwd", grid=(m // TOK,),
        in_specs=[row, pl.BlockSpec((TOK, D), lambda i: (i, 1)), row, row], out_specs=row,
        out_shape=SDS((m, D), BF16), compiler_params=_cp(("parallel",)),
    )(p, p, ya, yb)


def _loss_head(x, out, tgt, gate, gf):
    m = x.shape[0]

    def body(x_ref, o_ref, t_ref, gate_ref, gf_ref, dx1_ref, dout_ref, loss_ref, dgate_ref, dgf_ref):
        i = pl.program_id(0)

        @pl.when(i == 0)
        def _():
            loss_ref[...] = jnp.zeros_like(loss_ref)
            dgate_ref[...] = jnp.zeros_like(dgate_ref)
            dgf_ref[...] = jnp.zeros_like(dgf_ref)

        out_ = o_ref[...]
        x1 = x_ref[...] + gate_ref[...] * out_
        r = lax.rsqrt(jnp.mean(x1 * x1, axis=-1, keepdims=True) + EPS)
        xh = x1 * r
        err = xh * gf_ref[...] - t_ref[...]
        loss_ref[...] += 0.5 * jnp.sum(jnp.mean(err * err, axis=-1, keepdims=True), axis=0, keepdims=True)
        dy = err * (1.0 / D)
        dgf_ref[...] += jnp.sum(dy * xh, axis=0, keepdims=True)
        dxh = dy * gf_ref[...]
        dx1 = r * (dxh - xh * jnp.mean(dxh * xh, axis=-1, keepdims=True))
        dx1_ref[...] = dx1
        dout_ref[...] = (gate_ref[...] * dx1).astype(BF16)
        dgate_ref[...] += jnp.sum(dx1 * out_, axis=0, keepdims=True)

    row = pl.BlockSpec((TOK, D), lambda i: (i, 0))
    vec = _full((1, D))
    return _pcall(
        body, name="loss_head", grid=(m // TOK,), in_specs=[row, row, row, vec, vec],
        out_specs=[row, row, _full((1, 128)), vec, vec],
        out_shape=[SDS((m, D), F32), SDS((m, D), BF16), SDS((1, 128), F32), SDS((1, D), F32), SDS((1, D), F32)],
        compiler_params=_cp(("arbitrary",)),
    )(x, out, tgt, gate, gf)


def _merge_bwd(dm, ya, yb, p):
    m = p.shape[0]

    def body(dm_ref, ya_ref, yb_ref, ga_ref, gb_ref, dya_ref, dyb_ref, dp_ref):
        dm_ = dm_ref[...]
        sa = jax.nn.sigmoid(ga_ref[...])
        sb = jax.nn.sigmoid(gb_ref[...])
        dya_ref[...] = (dm_ * sa).astype(BF16)
        dyb_ref[...] = (dm_ * sb).astype(BF16)
        dp_ref[:, 0:D] = dm_ * ya_ref[...] * (sa * (1.0 - sa))
        dp_ref[:, D:2 * D] = dm_ * yb_ref[...] * (sb * (1.0 - sb))

    row = pl.BlockSpec((TOK, D), lambda i: (i, 0))
    return _pcall(
        body, name="merge_bwd", grid=(m // TOK,),
        in_specs=[row, row, row, row, pl.BlockSpec((TOK, D), lambda i: (i, 1))],
        out_specs=[row, row, pl.BlockSpec((TOK, 2 * D), lambda i: (i, 0))],
        out_shape=[SDS((m, D), BF16), SDS((m, D), BF16), SDS((m, NP), F32)],
        compiler_params=_cp(("parallel",)),
    )(dm, ya, yb, p, p)


def _mid_bwd(dya_in, dyb_in, p, svr, svc, o_f, o_b, gbn, dp):
    m = p.shape[0]

    def body(dya_ref, dyb_ref, zb_ref, ua_ref, za_ref, svr_ref, svc_ref, of_ref, ob_ref, g_ref, dpi_ref,
             dp_ref, dsr_ref, dsc_ref, do_ref, dg_ref):
        i = pl.program_id(0)

        @pl.when(i == 0)
        def _():
            dg_ref[...] = jnp.zeros_like(dg_ref)

        dya = dya_ref[...]
        u = ua_ref[...]
        za = za_ref[...]
        sz = _silu(za)
        sv = jnp.concatenate([svr_ref[...], svc_ref[0], svc_ref[1]], axis=1)
        dp_ref[:, 512:1024] = dya * sv * sz
        dsv = dya * u * sz
        dsr_ref[...] = dsv[:, 0:256]
        dsc_ref[0] = dsv[:, 256:384]
        dsc_ref[1] = dsv[:, 384:512]
        dp_ref[:, 1024:1536] = dya * u * sv * _dsilu(za)

        dyb = dyb_ref[...]
        zb = zb_ref[...]
        o = of_ref[...] + ob_ref[...]
        szb = _silu(zb)
        dszb = _dsilu(zb)
        for h, (r, xh) in enumerate(_head_norm(o, None)):
            sl = slice(128 * h, 128 * h + 128)
            gh = g_ref[:, sl]
            don = dyb[:, sl] * szb[:, sl]
            dp_ref[:, sl] = dyb[:, sl] * (xh * gh) * dszb[:, sl]
            dg_ref[:, sl] += jnp.sum(don * xh, axis=0, keepdims=True)
            dxh = don * gh
            do_ref[:, sl] = r * (dxh - xh * jnp.mean(dxh * xh, axis=-1, keepdims=True))

    r512 = pl.BlockSpec((TOK, 512), lambda i: (i, 0))
    return _pcall(
        body, name="mid_bwd", grid=(m // TOK,),
        in_specs=[r512, r512, pl.BlockSpec((TOK, 512), lambda i: (i, 6)), pl.BlockSpec((TOK, 512), lambda i: (i, 7)),
                  pl.BlockSpec((TOK, 512), lambda i: (i, 8)), pl.BlockSpec((TOK, 256), lambda i: (i, 0)),
                  pl.BlockSpec((2, TOK, 128), lambda i: (0, i, 0)), r512, r512, _full((1, 512)),
                  pl.BlockSpec(memory_space=pl.ANY)],
        out_specs=[pl.BlockSpec((TOK, 1536), lambda i: (i, 2)), pl.BlockSpec((TOK, 256), lambda i: (i, 0)),
                   pl.BlockSpec((2, TOK, 128), lambda i: (0, i, 0)), r512, _full((1, 512))],
        out_shape=[SDS((m, NP), F32), SDS((m, 256), F32), SDS((2, m, 128), F32), SDS((m, 512), F32), SDS((1, 512), F32)],
        input_output_aliases={10: 0}, compiler_params=_cp(("arbitrary",)),
    )(dya_in, dyb_in, p, p, p, svr, svc, o_f, o_b, gbn, dp)


def _ln_bwd(dsr, vnr, dvnc, p, ws01t, ln_g, dp):
    m = p.shape[0]

    def body(dsr_ref, vnr_ref, dvc_ref, va_ref, wt_ref, g_ref, dpi_ref, dp_ref, dw_ref, db_ref, dlg_ref, dlb_ref, dvn_ref):
        i = pl.program_id(0)

        @pl.when(i == 0)
        def _():
            dw_ref[...] = jnp.zeros_like(dw_ref)
            db_ref[...] = jnp.zeros_like(db_ref)
            dlg_ref[...] = jnp.zeros_like(dlg_ref)
            dlb_ref[...] = jnp.zeros_like(dlb_ref)

        for j in range(TOK // AC):
            for g in range(2):
                d = dsr_ref[AC * j:AC * j + AC, AC * g:AC * g + AC]
                d16 = d.astype(BF16)
                dvn_ref[AC * j:AC * j + AC, AC * g:AC * g + AC] = _nn(wt_ref[g], d16)
                dw_ref[g] += _nt(d16, vnr_ref[AC * j:AC * j + AC, AC * g:AC * g + AC])
                db_ref[g] += jnp.sum(d, axis=1, keepdims=True)
        dvn_ref[:, 256:384] = dvc_ref[0]
        dvn_ref[:, 384:512] = dvc_ref[1]
        dvn = dvn_ref[...]
        xf = va_ref[...]
        xc = xf - jnp.mean(xf, axis=-1, keepdims=True)
        rs = lax.rsqrt(jnp.mean(xc * xc, axis=-1, keepdims=True) + EPS)
        xh = xc * rs
        dlg_ref[...] += jnp.sum(dvn * xh, axis=0, keepdims=True)
        dlb_ref[...] += jnp.sum(dvn, axis=0, keepdims=True)
        dxh = dvn * g_ref[...]
        dp_ref[...] = rs * (dxh - jnp.mean(dxh, axis=-1, keepdims=True) - xh * jnp.mean(dxh * xh, axis=-1, keepdims=True))

    return _pcall(
        body, name="ln_bwd", grid=(m // TOK,),
        in_specs=[pl.BlockSpec((TOK, 256), lambda i: (i, 0)), pl.BlockSpec((TOK, 256), lambda i: (i, 0)),
                  pl.BlockSpec((2, TOK, 128), lambda i: (0, i, 0)), pl.BlockSpec((TOK, 512), lambda i: (i, 9)),
                  _full((2, AC, AC)), _full((1, 512)), pl.BlockSpec(memory_space=pl.ANY)],
        out_specs=[pl.BlockSpec((TOK, 512), lambda i: (i, 9)), _full((2, AC, AC)), _full((2, AC, 1)), _full((1, 512)), _full((1, 512))],
        out_shape=[SDS((m, NP), F32), SDS((2, AC, AC), F32), SDS((2, AC, 1), F32), SDS((1, 512), F32), SDS((1, 512), F32)],
        scratch_shapes=[pltpu.VMEM((TOK, 512), F32)],
        input_output_aliases={6: 0}, compiler_params=_cp(("arbitrary",)),
    )(dsr, vnr, dvnc, p, ws01t, ln_g, dp)


def _tri_mm(tri, a):
    a1 = a.astype(BF16)
    r1 = a - a1.astype(F32)
    a2 = r1.astype(BF16)
    a3 = (r1 - a2.astype(F32)).astype(BF16)
    return _nn(tri, a1) + _nn(tri, a2) + _nn(tri, a3)


def _gla_masks(reverse):
    ri = lax.broadcasted_iota(jnp.int32, (CH, CH), 0)
    ci = lax.broadcasted_iota(jnp.int32, (CH, CH), 1)
    vis = (ci >= ri) if reverse else (ci <= ri)
    vis_t = (ci <= ri) if reverse else (ci >= ri)
    r4 = lax.broadcasted_iota(jnp.int32, (4 * CH, CH), 0) & (CH - 1)
    c4 = lax.broadcasted_iota(jnp.int32, (4 * CH, CH), 1)
    vis4 = (c4 >= r4) if reverse else (c4 <= r4)
    vis4_t = (c4 <= r4) if reverse else (c4 >= r4)
    lane = lax.broadcasted_iota(jnp.int32, (1, 256), 1)
    hm = [(lane >= CH * h) & (lane < CH * h + CH) for h in range(4)]
    return vis, vis_t, vis4, vis4_t, hm


def _stack_heads(x, hm):
    return jnp.concatenate([jnp.where(hm[h], x, 0.0).astype(BF16) for h in range(4)], axis=0)


def _diag_heads(full, hm):
    acc = jnp.where(hm[0], full[0:128], 0.0)
    for h in range(1, 4):
        acc = acc + jnp.where(hm[h], full[128 * h:128 * h + 128], 0.0)
    return acc


def _gla_fwd(p, qkv_blk, lr, lrw, gbias, s0, *, reverse, name):
    m = p.shape[0]
    nb = m // GLA_TB
    nc = GLA_TB // CH
    rmap = (lambda i: nb - 1 - i) if reverse else (lambda i: i)

    def body(qkv_ref, lr_ref, lrw_ref, gb_ref, s0_ref, o_ref, sb_ref, sfin_ref, st_ref):
        i = pl.program_id(0)

        @pl.when(i == 0)
        def _():
            st_ref[...] = s0_ref[...]

        vis, _, vis4, _, hm = _gla_masks(reverse)
        tri = vis.astype(F32).astype(BF16)
        logits = _nn(lr_ref[...].astype(BF16), lrw_ref[...]) + gb_ref[...]
        a_all = _logsig(logits) * (1.0 / 16.0)
        st = st_ref[...]
        for c in (range(nc - 1, -1, -1) if reverse else range(nc)):
            rows = slice(CH * c, CH * c + CH)
            b = _tri_mm(tri, a_all[rows])
            bl = b[0:1] if reverse else b[CH - 1:CH]
            q = qkv_ref[rows, 0:256] * 0.125
            k = qkv_ref[rows, 256:512]
            v16 = qkv_ref[rows, 512:1024].astype(BF16)
            qd = q * jnp.exp(b)
            kd16 = (k * jnp.exp(-b)).astype(BF16)
            kdec16 = (k * jnp.exp(bl - b)).astype(BF16)
            qstack = _stack_heads(qd, hm)
            sc = jnp.where(vis4, _nt(qstack, kd16), 0.0).astype(BF16)
            inter = _nt(qstack, st.astype(BF16))
            for h in range(4):
                o_ref[rows, 128 * h:128 * h + 128] = (
                    _nn(sc[CH * h:CH * h + CH], v16[:, 128 * h:128 * h + 128]) + inter[CH * h:CH * h + CH])
            sb_ref[c] = st
            st = st * jnp.exp(bl) + _diag_heads(_tn(v16, kdec16), hm)
        st_ref[...] = st

        @pl.when(i == nb - 1)
        def _():
            sfin_ref[...] = st

    return _pcall(
        body, name=name, grid=(nb,),
        in_specs=[pl.BlockSpec((GLA_TB, 1024), lambda i: (rmap(i), qkv_blk)), pl.BlockSpec((GLA_TB, LRW), lambda i: (rmap(i), 0)),
                  _full((LRW, 256)), _full((1, 256)), _full((128, 256))],
        out_specs=[pl.BlockSpec((GLA_TB, 512), lambda i: (rmap(i), 0)), pl.BlockSpec((nc, 128, 256), lambda i: (rmap(i), 0, 0)),
                   _full((128, 256))],
        out_shape=[SDS((m, 512), F32), SDS((m // CH, 128, 256), F32), SDS((128, 256), F32)],
        scratch_shapes=[pltpu.VMEM((128, 256), F32)], compiler_params=_cp(("arbitrary",)),
    )(p, lr, lrw, gbias, s0)


def _gla_bwd(p, qkv_blk, lr, lrw, lrwt, gbias, sb, dsfin, do, prev, dp, *, reverse, name):
    m = p.shape[0]
    nb = m // GLA_TB
    nc = GLA_TB // CH
    rmap = (lambda i: i) if reverse else (lambda i: nb - 1 - i)
    has_prev = prev is not None
    has_dp = dp is not None

    def body(*refs):
        refs = list(refs)
        qkv_ref, lr_ref, lrw_ref, lrwt_ref, gb_ref, sb_ref, dsfin_ref, do_ref = refs[:8]
        refs = refs[8:]
        if has_prev:
            pq_ref, plr_ref = refs[:2]
            refs = refs[2:]
        if has_dp:
            refs = refs[1:]
        dqkv_ref, dlr_ref, dw2_ref, dgb_ref, ds0_ref, dst_ref, dlog_ref = refs
        i = pl.program_id(0)

        @pl.when(i == 0)
        def _():
            dst_ref[...] = dsfin_ref[...]
            dw2_ref[...] = jnp.zeros_like(dw2_ref)
            dgb_ref[...] = jnp.zeros_like(dgb_ref)

        vis, vis_t, vis4, vis4_t, hm = _gla_masks(reverse)
        tri = vis.astype(F32).astype(BF16)
        tri_t = vis_t.astype(F32).astype(BF16)
        lr16 = lr_ref[...].astype(BF16)
        logits = _nn(lr16, lrw_ref[...]) + gb_ref[...]
        a_all = _logsig(logits) * (1.0 / 16.0)
        dsig = (1.0 - jax.nn.sigmoid(logits)) * (1.0 / 16.0)
        dst = dst_ref[...]
        for c in (range(nc) if reverse else range(nc - 1, -1, -1)):
            rows = slice(CH * c, CH * c + CH)
            b = _tri_mm(tri, a_all[rows])
            bl = b[0:1] if reverse else b[CH - 1:CH]
            eb = jnp.exp(b)
            enb = jnp.exp(-b)
            ebl = jnp.exp(bl - b)
            el = jnp.exp(bl)
            q = qkv_ref[rows, 0:256] * 0.125
            k = qkv_ref[rows, 256:512]
            v16 = qkv_ref[rows, 512:1024].astype(BF16)
            do16 = do_ref[rows, :].astype(BF16)
            qd = q * eb
            kd = k * enb
            kdec = k * ebl
            st = sb_ref[c]
            st16 = st.astype(BF16)
            dst16 = dst.astype(BF16)
            qd16 = qd.astype(BF16)
            kd16 = kd.astype(BF16)
            qstack = _stack_heads(qd, hm)
            kstack = _stack_heads(kd, hm)
            kdecstack = _stack_heads(kdec, hm)
            pt = jnp.where(vis4_t, _nt(kstack, qd16), 0.0).astype(BF16)
            dvinter = _nt(kdecstack, dst16)
            dqd = jnp.zeros((CH, 256), F32)
            dkd = jnp.zeros((CH, 256), F32)
            dkdec = jnp.zeros((CH, 256), F32)
            for h in range(4):
                hs = slice(128 * h, 128 * h + 128)
                rh = slice(CH * h, CH * h + CH)
                doh = do16[:, hs]
                vh = v16[:, hs]
                dpm = jnp.where(vis, _nt(doh, vh), 0.0).astype(BF16)
                dpt = jnp.where(vis_t, _nt(vh, doh), 0.0).astype(BF16)
                dv_h = _nn(pt[rh], doh) + dvinter[rh]
                if has_prev:
                    dv_h = dv_h + pq_ref[rows, 512 + 128 * h:512 + 128 * h + 128]
                dqkv_ref[rows, 512 + 128 * h:512 + 128 * h + 128] = dv_h
                dqd = dqd + _nn(dpm, kstack[rh]) + jnp.where(hm[h], _nn(doh, st16), 0.0)
                dkd = dkd + _nn(dpt, qstack[rh])
                dkdec = dkdec + jnp.where(hm[h], _nn(vh, dst16), 0.0)
            dq = dqd * eb * 0.125
            dk = dkd * enb + dkdec * ebl
            if has_prev:
                dq = dq + pq_ref[rows, 0:256]
                dk = dk + pq_ref[rows, 256:512]
            dqkv_ref[rows, 0:256] = dq
            dqkv_ref[rows, 256:512] = dk
            g_kdec = dkdec * kdec
            db = dqd * qd - dkd * kd - g_kdec
            dbl = jnp.sum(g_kdec, axis=0, keepdims=True) + jnp.sum(st * dst, axis=0, keepdims=True) * el
            da = _tri_mm(tri_t, db) + dbl
            dlog_ref[rows, :] = da * dsig[rows]
            dst = dst * el + _diag_heads(_tn(do16, qd16), hm)
        dst_ref[...] = dst
        dlog = dlog_ref[...]
        dlog16 = dlog.astype(BF16)
        dlr = _nn(dlog16, lrwt_ref[...])
        if has_prev:
            dlr = dlr + plr_ref[...]
        dlr_ref[...] = dlr
        dw2_ref[...] += _tn(lr16, dlog16)
        dgb_ref[...] += jnp.sum(dlog, axis=0, keepdims=True)

        @pl.when(i == nb - 1)
        def _():
            ds0_ref[...] = dst

    in_specs = [pl.BlockSpec((GLA_TB, 1024), lambda i: (rmap(i), qkv_blk)), pl.BlockSpec((GLA_TB, LRW), lambda i: (rmap(i), 0)),
                _full((LRW, 256)), _full((256, LRW)), _full((1, 256)), pl.BlockSpec((nc, 128, 256), lambda i: (rmap(i), 0, 0)),
                _full((128, 256)), pl.BlockSpec((GLA_TB, 512), lambda i: (rmap(i), 0))]
    args = [p, lr, lrw, lrwt, gbias, sb, dsfin, do]
    if has_prev:
        in_specs += [pl.BlockSpec((GLA_TB, 1024), lambda i: (rmap(i), 0)), pl.BlockSpec((GLA_TB, LRW), lambda i: (rmap(i), 0))]
        args += list(prev)
    aliases = {}
    if has_dp:
        in_specs.append(pl.BlockSpec(memory_space=pl.ANY))
        aliases = {len(args): 0}
        args.append(dp)
        dq_spec = pl.BlockSpec((GLA_TB, 1024), lambda i: (rmap(i), qkv_blk))
        dq_shape = SDS(dp.shape, F32)
    else:
        dq_spec = pl.BlockSpec((GLA_TB, 1024), lambda i: (rmap(i), 0))
        dq_shape = SDS((m, 1024), F32)
    return _pcall(
        body, name=name, grid=(nb,), in_specs=in_specs,
        out_specs=[dq_spec, pl.BlockSpec((GLA_TB, LRW), lambda i: (rmap(i), 0)), _full((LRW, 256)), _full((1, 256)), _full((128, 256))],
        out_shape=[dq_shape, SDS((m, LRW), F32), SDS((LRW, 256), F32), SDS((1, 256), F32), SDS((128, 256), F32)],
        scratch_shapes=[pltpu.VMEM((128, 256), F32), pltpu.VMEM((GLA_TB, 256), F32)],
        input_output_aliases=aliases, compiler_params=_cp(("arbitrary",)),
    )(*args)


def _device_step(x, c, ctx, c_ctx, tgt, wm, bm, ng, wi, wlr, ln_g, ln_b, ws, bs, w2, gb2, gbn, wpa, wpb, wo, gf):
    L = x.shape[0]
    wit = wi.T
    wlrt = wlr.T
    ws16 = ws.astype(BF16)
    wst16 = jnp.swapaxes(ws, 1, 2).astype(BF16)
    bscol = bs[:, :, None]
    lrw = [jnp.zeros((LRW, 256), F32).at[16 * r:16 * r + 16].set(w2[r]).astype(BF16) for r in range(2)]
    lrwt = [w.T for w in lrw]
    gbias = [gb2[r:r + 1] for r in range(2)]

    cc = jnp.zeros((8, D), F32).at[0:1].set(c).at[1:2].set(c_ctx)
    mod = _modvec(cc, wm, bm)
    shift, scale, gate = mod[0:1, 0:D], mod[0:1, D:2 * D], mod[0:1, 2 * D:3 * D]
    shift_c, scale_c = mod[1:2, 0:D], mod[1:2, D:2 * D]

    hc = _prep_h(ctx, ng, scale_c, shift_c, "prep_hc")
    pc = _mm(hc, wi[:, 2048:3072], tm=256, tn=1024, tk=D, out_dtype=F32, name="mm_pc")
    plrc = _mm(hc, wlr, tm=256, tn=LRW, tk=D, out_dtype=F32, name="mm_plrc")
    zero_s = jnp.zeros((128, 256), F32)
    _, sbc_f, sc_f = _gla_fwd(pc, 0, plrc, lrw[0], gbias[0], zero_s, reverse=False, name="gla_fwd_cf")
    _, sbc_b, sc_b = _gla_fwd(pc, 0, plrc, lrw[1], gbias[1], zero_s, reverse=True, name="gla_fwd_cb")

    h = _prep_h(x, ng, scale, shift, "prep_h")
    p = _mm(h, wi, tm=512, tn=1024, tk=D, out_dtype=F32, name="mm_p")
    plr = _mm(h, wlr, tm=1024, tn=LRW, tk=D, out_dtype=F32, name="mm_plr")
    o_f, sb_f, _ = _gla_fwd(p, 2, plr, lrw[0], gbias[0], sc_f, reverse=False, name="gla_fwd_f")
    o_b, sb_b, _ = _gla_fwd(p, 2, plr, lrw[1], gbias[1], sc_b, reverse=True, name="gla_fwd_b")
    vnr, vnc = _ln_fwd(p, ln_g, ln_b)
    svc = _colmix_fwd(vnc.reshape(2, AC, L), ws16[2:4], bscol[2:4]).reshape(2, L, 128)
    ya_in, yb_in, svr = _mid_fwd(o_f, o_b, p, vnr, svc, ws16[0:2], bscol[0:2], gbn)
    ya = _mm(ya_in, wpa, tm=1024, tn=D, tk=512, out_dtype=F32, name="mm_ya")
    yb = _mm(yb_in, wpb, tm=1024, tn=D, tk=512, out_dtype=F32, name="mm_yb")
    mrg = _merge_fwd(p, ya, yb)
    out = _mm(mrg, wo, tm=1024, tn=D, tk=D, out_dtype=F32, name="mm_out")
    dx1, dout, loss, dgate, dgf = _loss_head(x, out, tgt, gate, gf)

    dm = _mm(dout, wo.T, tm=1024, tn=D, tk=D, out_dtype=F32, name="mm_dm")
    dwo = _mm(mrg.T, dout, tm=D, tn=D, tk=1024, out_dtype=F32, name="mm_dwo")
    dya, dyb, dp = _merge_bwd(dm, ya, yb, p)
    dya_in = _mm(dya, wpa.T, tm=1024, tn=512, tk=D, out_dtype=F32, name="mm_dya_in")
    dyb_in = _mm(dyb, wpb.T, tm=1024, tn=512, tk=D, out_dtype=F32, name="mm_dyb_in")
    dwpa = _mm(ya_in.T, dya, tm=512, tn=D, tk=1024, out_dtype=F32, name="mm_dwpa")
    dwpb = _mm(yb_in.T, dyb, tm=512, tn=D, tk=1024, out_dtype=F32, name="mm_dwpb")
    dp, dsr, dsc, do, dgbn = _mid_bwd(dya_in, dyb_in, p, svr, svc, o_f, o_b, gbn, dp)
    dvnc, dws23, dbs23 = _colmix_bwd(dsc.reshape(2, AC, L), vnc.reshape(2, AC, L), wst16[2:4])
    dp, dws01, dbs01, dlng, dlnb = _ln_bwd(dsr, vnr, dvnc.reshape(2, L, 128), p, wst16[0:2], ln_g, dp)
    zero_ds = jnp.zeros((128, 256), F32)
    dqkv_f, dlr_f, dw2_f, dgb_f, ds0_f = _gla_bwd(p, 2, plr, lrw[0], lrwt[0], gbias[0], sb_f, zero_ds, do, None, None,
                                                  reverse=False, name="gla_bwd_f")
    dp, dlr, dw2_b, dgb_b, ds0_b = _gla_bwd(p, 2, plr, lrw[1], lrwt[1], gbias[1], sb_b, zero_ds, do, (dqkv_f, dlr_f), dp,
                                            reverse=True, name="gla_bwd_b")
    zero_do = jnp.zeros((ctx.shape[0], 512), F32)
    dqkvc_f, dlrc_f, dw2c_f, dgbc_f, _ = _gla_bwd(pc, 0, plrc, lrw[0], lrwt[0], gbias[0], sbc_f, ds0_f, zero_do, None, None,
                                                  reverse=False, name="gla_bwd_cf")
    dqkvc, dlrc, dw2c_b, dgbc_b, _ = _gla_bwd(pc, 0, plrc, lrw[1], lrwt[1], gbias[1], sbc_b, ds0_b, zero_do,
                                              (dqkvc_f, dlrc_f), None, reverse=True, name="gla_bwd_cb")
    dhc = _mm(dqkvc, wit[2048:3072], tm=256, tn=D, tk=1024, out_dtype=F32, name="mm_dhc")
    dhc = _mm(dlrc, wlrt, tm=256, tn=D, tk=LRW, out_dtype=F32, name="mm_dhc_lr", acc=dhc)
    _, dng_c, dscale_c, dshift_c = _prep_bwd(ctx, dhc, None, ng, scale_c, "prep_bwd_c")

    dh = _mm(dp, wit, tm=1024, tn=D, tk=1024, out_dtype=F32, name="mm_dh")
    dh = _mm(dlr, wlrt, tm=1024, tn=D, tk=LRW, out_dtype=F32, name="mm_dh_lr", acc=dh)
    ht = h.T
    hct = hc.T
    dwi = _mm(ht, dp, tm=D, tn=1024, tk=1024, out_dtype=F32, name="mm_dwi")
    dwi_qkv = _mm(hct, dqkvc, tm=D, tn=1024, tk=256, out_dtype=F32, name="mm_dwi_c", acc=dwi[:, 2048:3072])
    dwlr = _mm(ht, dlr, tm=D, tn=LRW, tk=1024, out_dtype=F32, name="mm_dwlr")
    dwlr = _mm(hct, dlrc, tm=D, tn=LRW, tk=256, out_dtype=F32, name="mm_dwlr_c", acc=dwlr)
    dx, dng, dscale, dshift = _prep_bwd(x, dh, dx1, ng, scale, "prep_bwd")

    dmodc = jnp.concatenate([dshift_c, dscale_c], axis=1)
    dscc = _dcctx(jnp.zeros((8, 2 * D), F32).at[0:1].set(dmodc), wm)[0:1]
    dw2p = dw2_f + dw2c_f, dw2_b + dw2c_b
    return dict(
        loss=loss[0, 0], dx=dx, dwi=dwi, dwi_qkv=dwi_qkv, dwlr=dwlr, dwpa=dwpa, dwpb=dwpb, dwo=dwo,
        dmod=jnp.concatenate([dshift, dscale, dgate], axis=1), dmodc=dmodc, dscc=dscc, dng=dng + dng_c,
        dlng=dlng, dlnb=dlnb, dws=jnp.concatenate([dws01, dws23], axis=0),
        dbs=jnp.concatenate([dbs01, dbs23], axis=0)[:, :, 0], dgbn=dgbn, dgf=dgf,
        dw2=jnp.stack([dw2p[0][0:16], dw2p[1][16:32]]), dgb2=jnp.concatenate([dgb_f + dgbc_f, dgb_b + dgbc_b], axis=0),
    )


ANY = pl.BlockSpec(memory_space=pl.ANY)


def _coords():
    return lax.axis_index("x"), lax.axis_index("y"), lax.axis_index("c")


def _flip(v, bit):
    return 1 - v if bit else v


def _remote(src, dst, send_sem, recv_sem, dev):
    return pltpu.make_async_remote_copy(src_ref=src, dst_ref=dst, send_sem=send_sem, recv_sem=recv_sem,
                                        device_id=dev, device_id_type=MESH)


def _gather_weights(split, whole, name):
    ns, nw = len(split), len(whole)
    n = ns + nw

    def body(*refs):
        ins, outs = refs[:n], refs[n:2 * n]
        a_send, a_recv, b_send, b_recv, loc_sems = refs[2 * n:]
        x, y, c = _coords()
        me = 2 * x + y
        sib = (x, y, 1 - c)
        peers = [(1 - x, y), (x, 1 - y), (1 - x, 1 - y)]
        locs, sends = [], []
        for k in range(n):
            lc = pltpu.make_async_copy(ins[k], outs[k].at[me], loc_sems.at[k])
            lc.start()
            locs.append(lc)
            for j, (px, py) in enumerate(peers):
                if k < ns:
                    h = split[k].shape[0] // 2
                    rc = _remote(ins[k].at[pl.ds(c * h, h)], outs[k].at[me, pl.ds(c * h, h)], a_send.at[3 * k + j],
                                 a_recv.at[3 * k + j], (px, py, c))
                else:
                    rc = _remote(ins[k], outs[k].at[me], a_send.at[3 * k + j], a_recv.at[3 * k + j], (px, py, c))
                rc.start()
                sends.append(rc)
        for k in range(ns):
            h = split[k].shape[0] // 2
            for j, (px, py) in enumerate(peers):
                landed = outs[k].at[2 * px + py, pl.ds(c * h, h)]
                _remote(landed, landed, a_send.at[3 * k + j], a_recv.at[3 * k + j], (px, py, c)).wait_recv()
                fw = _remote(landed, landed, b_send.at[3 * k + j], b_recv.at[3 * k + j], sib)
                fw.start()
                sends.append(fw)
        for k in range(ns, n):
            for j, (px, py) in enumerate(peers):
                landed = outs[k].at[2 * px + py]
                _remote(landed, landed, a_send.at[3 * k + j], a_recv.at[3 * k + j], (px, py, c)).wait_recv()
        for k in range(ns):
            h = split[k].shape[0] // 2
            for j, (px, py) in enumerate(peers):
                passed = outs[k].at[2 * px + py, pl.ds((1 - c) * h, h)]
                _remote(passed, passed, b_send.at[3 * k + j], b_recv.at[3 * k + j], sib).wait_recv()
        for rc in sends:
            rc.wait_send()
        for lc in locs:
            lc.wait()

    arrs = list(split) + list(whole)
    return _pcall(
        body, name=name, in_specs=[ANY] * n, out_specs=[ANY] * n,
        out_shape=[SDS((4,) + a.shape, a.dtype) for a in arrs],
        scratch_shapes=[pltpu.SemaphoreType.DMA((3 * n,)), pltpu.SemaphoreType.DMA((3 * n,)), pltpu.SemaphoreType.DMA((3 * ns,)),
                        pltpu.SemaphoreType.DMA((3 * ns,)), pltpu.SemaphoreType.DMA((n,))],
    )(*arrs)


def _gather_all(a, name):
    masks = [(mx, my, mc) for mx in range(2) for my in range(2) for mc in range(2)][1:]

    def body(in_ref, out_ref, send_sems, recv_sems, loc_sem):
        x, y, c = _coords()
        me = 4 * x + 2 * y + c
        lc = pltpu.make_async_copy(in_ref, out_ref.at[me], loc_sem)
        lc.start()
        sends = []
        for j, (mx, my, mc) in enumerate(masks):
            peer = (_flip(x, mx), _flip(y, my), _flip(c, mc))
            rc = pltpu.make_async_remote_copy(
                src_ref=in_ref, dst_ref=out_ref.at[me], send_sem=send_sems.at[j], recv_sem=recv_sems.at[j],
                device_id=peer, device_id_type=MESH)
            rc.start()
            sends.append(rc)
        for j, (mx, my, mc) in enumerate(masks):
            px, py, pc = _flip(x, mx), _flip(y, my), _flip(c, mc)
            pltpu.make_async_remote_copy(
                src_ref=in_ref, dst_ref=out_ref.at[4 * px + 2 * py + pc], send_sem=send_sems.at[j], recv_sem=recv_sems.at[j],
                device_id=(px, py, pc), device_id_type=MESH).wait_recv()
        for rc in sends:
            rc.wait_send()
        lc.wait()

    return _pcall(
        body, name=name, in_specs=[ANY], out_specs=ANY, out_shape=SDS((8,) + a.shape, a.dtype),
        scratch_shapes=[pltpu.SemaphoreType.DMA((7,)), pltpu.SemaphoreType.DMA((7,)), pltpu.SemaphoreType.DMA(())],
    )(a)


def _swap_rows_c(arrs, name):
    n = len(arrs)

    def body(*refs):
        ins, outs = refs[:n], refs[n:2 * n]
        send_sems, recv_sems = refs[2 * n:]
        x, y, c = _coords()
        sends = []
        for k in range(n):
            h = arrs[k].shape[1] // 2
            rc = _remote(ins[k].at[pl.ds(0, 4), pl.ds((1 - c) * h, h)], outs[k], send_sems.at[k], recv_sems.at[k], (x, y, 1 - c))
            rc.start()
            sends.append(rc)
        for rc in sends:
            rc.wait()

    return _pcall(
        body, name=name, in_specs=[ANY] * n, out_specs=[ANY] * n,
        out_shape=[SDS((4, a.shape[1] // 2, a.shape[2]), a.dtype) for a in arrs],
        scratch_shapes=[pltpu.SemaphoreType.DMA((n,)), pltpu.SemaphoreType.DMA((n,))],
    )(*arrs)


def _a2a_xy(arrs, name):
    n = len(arrs)

    def body(*refs):
        ins, outs = refs[:n], refs[n:2 * n]
        send_sems, recv_sems, loc_sems = refs[2 * n:]
        x, y, c = _coords()
        me = 2 * x + y
        peers = [(1 - x, y), (x, 1 - y), (1 - x, 1 - y)]
        locs, sends = [], []
        for k in range(n):
            lc = pltpu.make_async_copy(ins[k].at[me], outs[k].at[me], loc_sems.at[k])
            lc.start()
            locs.append(lc)
            for j, (px, py) in enumerate(peers):
                rc = _remote(ins[k].at[2 * px + py], outs[k].at[me], send_sems.at[3 * k + j], recv_sems.at[3 * k + j], (px, py, c))
                rc.start()
                sends.append(rc)
        for k in range(n):
            for j, (px, py) in enumerate(peers):
                landed = outs[k].at[2 * px + py]
                _remote(landed, landed, send_sems.at[3 * k + j], recv_sems.at[3 * k + j], (px, py, c)).wait_recv()
        for rc in sends:
            rc.wait_send()
        for lc in locs:
            lc.wait()

    return _pcall(
        body, name=name, in_specs=[ANY] * n, out_specs=[ANY] * n, out_shape=[SDS(a.shape, a.dtype) for a in arrs],
        scratch_shapes=[pltpu.SemaphoreType.DMA((3 * n,)), pltpu.SemaphoreType.DMA((3 * n,)), pltpu.SemaphoreType.DMA((n,))],
    )(*arrs)


def _join_halves(halves, name):
    n = len(halves)

    def body(*refs):
        ins, outs = refs[:n], refs[n:2 * n]
        send_sems, recv_sems, loc_sems = refs[2 * n:]
        x, y, c = _coords()
        locs, sends = [], []
        for k in range(n):
            h = halves[k].shape[0]
            lc = pltpu.make_async_copy(ins[k], outs[k].at[pl.ds(c * h, h)], loc_sems.at[k])
            lc.start()
            locs.append(lc)
            rc = _remote(ins[k], outs[k].at[pl.ds(c * h, h)], send_sems.at[k], recv_sems.at[k], (x, y, 1 - c))
            rc.start()
            sends.append(rc)
        for k in range(n):
            h = halves[k].shape[0]
            landed = outs[k].at[pl.ds((1 - c) * h, h)]
            _remote(landed, landed, send_sems.at[k], recv_sems.at[k], (x, y, 1 - c)).wait_recv()
        for rc in sends:
            rc.wait_send()
        for lc in locs:
            lc.wait()

    return _pcall(
        body, name=name, in_specs=[ANY] * n, out_specs=[ANY] * n,
        out_shape=[SDS((2 * a.shape[0], a.shape[1]), a.dtype) for a in halves],
        scratch_shapes=[pltpu.SemaphoreType.DMA((n,)), pltpu.SemaphoreType.DMA((n,)), pltpu.SemaphoreType.DMA((n,))],
    )(*halves)


def _pair_sum(a, got, cidx, name):
    _, r, cdim = a.shape
    h = r // 2
    tr = min(h, 256)
    nj = h // tr

    def body(c_ref, a_ref, g_ref, o_ref):
        o_ref[...] = (a_ref[...] + g_ref[...]).astype(BF16)

    blk = pl.BlockSpec((1, tr, cdim), lambda s, j, c: (s, j, 0))
    return _pcall(
        body, name=name, out_shape=SDS((4, h, cdim), BF16),
        grid_spec=pltpu.PrefetchScalarGridSpec(
            num_scalar_prefetch=1, grid=(4, nj),
            in_specs=[pl.BlockSpec((1, tr, cdim), lambda s, j, c: (s, c[0] * nj + j, 0)), blk], out_specs=blk),
        compiler_params=_cp(("parallel", "parallel")),
    )(cidx, a, got)


def _sum_chips(parts, name):
    _, h, cdim = parts.shape
    tr = min(h, 256)

    def body(p_ref, o_ref):
        acc = p_ref[0].astype(F32)
        for k in range(1, 4):
            acc = acc + p_ref[k].astype(F32)
        o_ref[...] = acc

    return _pcall(
        body, name=name, grid=(h // tr,), in_specs=[pl.BlockSpec((4, tr, cdim), lambda i: (0, i, 0))],
        out_specs=pl.BlockSpec((tr, cdim), lambda i: (i, 0)), out_shape=SDS((h, cdim), F32), compiler_params=_cp(("parallel",)),
    )(parts)


def _sum_slots(a, name, rows):
    s, n, _ = a.shape

    def body(a_ref, o_ref):
        acc = a_ref[0]
        for k in range(1, s):
            acc = acc + a_ref[k]
        o_ref[...] = acc

    return _pcall(
        body, name=name, grid=(n // rows,), in_specs=[pl.BlockSpec((s, rows, 128), lambda i: (0, i, 0))],
        out_specs=pl.BlockSpec((rows, 128), lambda i: (i, 0)), out_shape=SDS((n, 128), F32),
        compiler_params=_cp(("parallel",)),
    )(a)


def _adamw(w, g, m, v, name, rows):
    r, cdim = w.shape

    def body(w_ref, g_ref, m_ref, v_ref, d_ref, nm_ref, nv_ref):
        g_ = g_ref[...]
        nm = ADAM_B1 * m_ref[...] + (1.0 - ADAM_B1) * g_
        nv = ADAM_B2 * v_ref[...] + (1.0 - ADAM_B2) * (g_ * g_)
        m_hat = nm / (1.0 - ADAM_B1 ** ADAM_STEP)
        v_hat = nv / (1.0 - ADAM_B2 ** ADAM_STEP)
        d_ref[...] = -ADAM_LR * (m_hat / (jnp.sqrt(v_hat) + ADAM_EPS) + ADAM_WD * w_ref[...])
        nm_ref[...] = nm
        nv_ref[...] = nv

    blk = pl.BlockSpec((rows, cdim), lambda i: (i, 0))
    return _pcall(
        body, name=name, grid=(r // rows,), in_specs=[blk] * 4, out_specs=[blk] * 3,
        out_shape=[SDS(w.shape, F32)] * 3, compiler_params=_cp(("parallel",)),
    )(w, g, m, v)


def _pack(pieces, rows):
    flat = jnp.concatenate([p.reshape(-1) for p in pieces])
    return jnp.pad(flat, (0, rows * 128 - flat.shape[0])).reshape(rows, 128)


def _unpack(buf, shapes):
    flat = buf.reshape(-1)
    out, off = [], 0
    for shp in shapes:
        size = 1
        for s in shp:
            size *= s
        out.append(flat[off:off + size].reshape(shp))
        off += size
    return out


def _perm_cols(w):
    perm = jnp.concatenate([w[..., 3104:5152], w[..., 0:1024], w[..., 1056:1568], w[..., 1568:2080], w[..., 2592:3104],
                            w[..., 2080:2592]], axis=-1)
    return perm, w[..., 1024:1056]


def _unperm_cols(perm, lr32):
    return jnp.concatenate([perm[..., 2048:3072], lr32, perm[..., 3072:3584], perm[..., 3584:4096], perm[..., 4608:5120],
                            perm[..., 4096:4608], perm[..., 0:2048]], axis=-1)


SMALL_ROWS = 672
HALF_ROWS = 7200


def kernel(x, c, ctx, c_ctx, w_mod, b_mod, norm_g, w_in, a_ln_g, a_ln_b, a_ws, a_bs, b_gate_w2, b_gate_b, b_norm_g, w_proj_a, w_proj_b, w_out, final_norm_g, loss_target, m_c_ctx, m_w_mod, m_b_mod, m_norm_g, m_w_in, m_a_ln_g, m_a_ln_b, m_a_ws, m_a_bs, m_b_gate_w2, m_b_gate_b, m_b_norm_g, m_w_proj_a, m_w_proj_b, m_w_out, m_final_norm_g, v_c_ctx, v_w_mod, v_b_mod, v_norm_g, v_w_in, v_a_ln_g, v_a_ln_b, v_a_ws, v_a_bs, v_b_gate_w2, v_b_gate_b, v_b_norm_g, v_w_proj_a, v_w_proj_b, v_w_out, v_final_norm_g):
    xi, yi, ci = _coords()
    me_xy = 2 * xi + yi

    gate_pack = _pack([b_gate_w2[0], b_gate_b[0]], 24)
    g_wi, g_wm, g_wpa, g_wpb, g_wo, g_gate = _gather_weights(
        [w_in[0].astype(BF16), w_mod[0].astype(BF16), w_proj_a[0].astype(BF16), w_proj_b[0].astype(BF16),
         w_out[0].astype(BF16)], [gate_pack], "gather_weights")
    wi_full = jnp.swapaxes(g_wi, 0, 1).reshape(D, 4 * 1288)
    wi, wlr32 = _perm_cols(wi_full)
    wlr = jnp.pad(wlr32, ((0, 0), (0, LRW - 32)))
    wm = jnp.swapaxes(g_wm, 0, 1).reshape(D, 3 * D)
    wpa = jnp.swapaxes(g_wpa, 0, 1).reshape(512, D)
    wpb = jnp.swapaxes(g_wpb, 0, 1).reshape(512, D)
    wo = g_wo.reshape(D, D)
    gflat = g_gate.reshape(4, 24 * 128)
    w2 = jnp.swapaxes(gflat[:, 0:2048].reshape(4, 2, 16, 64), 0, 2)
    w2 = jnp.swapaxes(w2, 0, 1).reshape(2, 16, 256)
    gb2 = jnp.swapaxes(gflat[:, 2048:2176].reshape(4, 2, 64), 0, 1).reshape(2, 256)

    r = _device_step(x[0], c, ctx[0], c_ctx[None], loss_target[0], wm, b_mod, norm_g, wi, wlr, a_ln_g, a_ln_b, a_ws[0], a_bs[0],
                     w2, gb2, b_norm_g, wpa, wpb, wo, final_norm_g[None])

    small = _pack([r["dmod"], c, r["dmodc"], r["dscc"], r["dng"], r["dlng"], r["dlnb"], r["dws"], r["dbs"], r["dgbn"], r["dgf"],
                   r["dw2"], r["dgb2"], jnp.broadcast_to(r["loss"], (128,))], SMALL_ROWS)
    small_all = _gather_all(small, "gather_small")
    small_sum = _sum_slots(small_all, "sum_small", SMALL_ROWS // 4)
    (s_dmod, _, s_dmodc, s_dscc, s_dng, s_dlng, s_dlnb, s_dws, s_dbs, s_dgbn, s_dgf, s_dw2, s_dgb2, s_loss) = _unpack(
        small_sum, [(1, 3 * D), (1, D), (1, 2 * D), (D,), (1, D), (1, 512), (1, 512), (1, 4, 128, 128), (1, 4, 128), (1, 512),
                    (D,), (2, 16, 256), (2, 256), (128,)])
    loss = s_loss[0]
    s_dmodc_p = jnp.pad(s_dmodc, ((0, 0), (0, D)))
    g_b_mod = s_dmod + s_dmodc_p
    sg = jax.nn.sigmoid(c_ctx)
    g_c_ctx = s_dscc * (sg * (1.0 + c_ctx * (1.0 - sg)))
    g_w2 = lax.dynamic_slice_in_dim(s_dw2, 64 * me_xy, 64, axis=2)[None]
    g_gb2 = lax.dynamic_slice_in_dim(s_dgb2, 64 * me_xy, 64, axis=1)[None]

    flat_all = small_all.reshape(8, SMALL_ROWS * 128)
    dmod_all = flat_all[:, 0:3 * D]
    c_all = flat_all[:, 3 * D:4 * D]
    lhs = jnp.concatenate([_silu(c_all), _silu(c_ctx)[None], jnp.zeros((7, D), F32)], axis=0)
    rhs = jnp.concatenate([dmod_all, s_dmodc_p, jnp.zeros((7, 3 * D), F32)], axis=0)
    rhs = lax.dynamic_slice_in_dim(rhs, 768 * me_xy, 768, axis=1)
    g_w_mod = _mm(lhs.T.astype(BF16), rhs.astype(BF16), tm=D, tn=768, tk=16, out_dtype=F32, name="mm_dwm")

    dwi = r["dwi"]
    dwi_full = jnp.concatenate([r["dwi_qkv"], r["dwlr"][:, 0:32], dwi[:, 3072:3584], dwi[:, 3584:4096], dwi[:, 4608:5120],
                                dwi[:, 4096:4608], dwi[:, 0:2048]], axis=1)
    big = [jnp.swapaxes(dwi_full.reshape(D, 4, 1288), 0, 1), jnp.swapaxes(r["dwpa"].reshape(512, 4, 256), 0, 1),
           jnp.swapaxes(r["dwpb"].reshape(512, 4, 256), 0, 1), r["dwo"].reshape(4, 256, D)]
    tags = ["wi", "wpa", "wpb", "wo"]
    got = _swap_rows_c(big, "swap_half_in")
    cidx = jnp.reshape(ci, (1,)).astype(jnp.int32)
    pair = [_pair_sum(a, g, cidx, "sum_pair_" + t) for a, g, t in zip(big, got, tags)]
    parts = _a2a_xy(pair, "a2a_grads")
    halves = [_sum_chips(p_, "sum_chips_" + t) for p_, t in zip(parts, tags)]
    g_w_in, g_wpa, g_wpb, g_wo = _join_halves(halves, "swap_half_out")

    d_w_in, nm_w_in, nv_w_in = _adamw(w_in[0], g_w_in, m_w_in[0], v_w_in[0], "adamw_w_in", 256)
    d_w_mod, nm_w_mod, nv_w_mod = _adamw(w_mod[0], g_w_mod, m_w_mod[0], v_w_mod[0], "adamw_w_mod", 256)
    d_wpa, nm_wpa, nv_wpa = _adamw(w_proj_a[0], g_wpa, m_w_proj_a[0], v_w_proj_a[0], "adamw_wpa", 256)
    d_wpb, nm_wpb, nv_wpb = _adamw(w_proj_b[0], g_wpb, m_w_proj_b[0], v_w_proj_b[0], "adamw_wpb", 256)
    d_wo, nm_wo, nv_wo = _adamw(w_out[0], g_wo, m_w_out[0], v_w_out[0], "adamw_wo", 256)

    names = ["c_ctx", "b_mod", "norm_g", "a_ln_g", "a_ln_b", "a_ws", "a_bs", "b_gate_w2", "b_gate_b", "b_norm_g", "final_norm_g"]
    ws_ = [c_ctx, b_mod, norm_g, a_ln_g, a_ln_b, a_ws, a_bs, b_gate_w2, b_gate_b, b_norm_g, final_norm_g]
    gs_ = [g_c_ctx, g_b_mod, s_dng, s_dlng, s_dlnb, s_dws, s_dbs, g_w2, g_gb2, s_dgbn, s_dgf]
    ms_ = [m_c_ctx, m_b_mod, m_norm_g, m_a_ln_g, m_a_ln_b, m_a_ws, m_a_bs, m_b_gate_w2, m_b_gate_b, m_b_norm_g, m_final_norm_g]
    vs_ = [v_c_ctx, v_b_mod, v_norm_g, v_a_ln_g, v_a_ln_b, v_a_ws, v_a_bs, v_b_gate_w2, v_b_gate_b, v_b_norm_g, v_final_norm_g]
    shapes = [w.shape for w in ws_]
    gs_ = [g.reshape(s) for g, s in zip(gs_, shapes)]
    d_s, nm_s, nv_s = _adamw(_pack(ws_, 600), _pack(gs_, 600), _pack(ms_, 600), _pack(vs_, 600), "adamw_small", 600)
    d_small = dict(zip(names, _unpack(d_s, shapes)))
    nm_small = dict(zip(names, _unpack(nm_s, shapes)))
    nv_small = dict(zip(names, _unpack(nv_s, shapes)))
    g_small = dict(zip(names, gs_))

    order = ["c_ctx", "w_mod", "b_mod", "norm_g", "w_in", "a_ln_g", "a_ln_b", "a_ws", "a_bs", "b_gate_w2", "b_gate_b", "b_norm_g",
             "w_proj_a", "w_proj_b", "w_out", "final_norm_g"]
    big_g = dict(w_mod=g_w_mod[None], w_in=g_w_in[None], w_proj_a=g_wpa[None], w_proj_b=g_wpb[None], w_out=g_wo[None])
    big_d = dict(w_mod=d_w_mod[None], w_in=d_w_in[None], w_proj_a=d_wpa[None], w_proj_b=d_wpb[None], w_out=d_wo[None])
    big_m = dict(w_mod=nm_w_mod[None], w_in=nm_w_in[None], w_proj_a=nm_wpa[None], w_proj_b=nm_wpb[None], w_out=nm_wo[None])
    big_v = dict(w_mod=nv_w_mod[None], w_in=nv_w_in[None], w_proj_a=nv_wpa[None], w_proj_b=nv_wpb[None], w_out=nv_wo[None])
    grads = [big_g[n] if n in big_g else g_small[n] for n in order]
    deltas = [big_d[n] if n in big_d else d_small[n] for n in order]
    new_m = [big_m[n] if n in big_m else nm_small[n] for n in order]
    new_v = [big_v[n] if n in big_v else nv_small[n] for n in order]
    return (loss, r["dx"][None], *grads, *deltas, *new_m, *new_v)
```

```python
import functools

import jax
import jax.numpy as jnp
from jax import lax
from jax.experimental import pallas as pl
from jax.experimental.pallas import tpu as pltpu

F32 = jnp.float32
BF16 = jnp.bfloat16
SDS = jax.ShapeDtypeStruct

D = 1024
NP = 5120
LRW = 128
CH = 64
AC = 128
EPS = 1e-6
TOK = 256
GLA_TB = 256
VMEM_BIG = 48 * 1024 * 1024

ADAM_LR, ADAM_B1, ADAM_B2, ADAM_EPS, ADAM_WD, ADAM_STEP = 0.001, 0.9, 0.999, 1e-08, 0.01, 10

_pcall = pl.pallas_call
MESH = pl.DeviceIdType.MESH


def _cp(sem=None, vmem=None):
    kw = {}
    if sem is not None:
        kw["dimension_semantics"] = sem
    if vmem is not None:
        kw["vmem_limit_bytes"] = vmem
    return pltpu.CompilerParams(**kw)


def _silu(x):
    return x * jax.nn.sigmoid(x)


def _dsilu(x):
    s = jax.nn.sigmoid(x)
    return s * (1.0 + x * (1.0 - s))


def _logsig(x):
    return jnp.minimum(x, 0.0) - jnp.log1p(jnp.exp(-jnp.abs(x)))


def _nt(a, b):
    return lax.dot_general(a, b, (((1,), (1,)), ((), ())), preferred_element_type=F32)


def _tn(a, b):
    return lax.dot_general(a, b, (((0,), (0,)), ((), ())), preferred_element_type=F32)


def _nn(a, b):
    return jnp.dot(a, b, preferred_element_type=F32)


def _full(shape):
    return pl.BlockSpec(shape, lambda *_: (0,) * len(shape))


def _mm(a, b, *, tm, tn, tk, out_dtype, name, acc=None):
    m, k = a.shape
    k2, n = b.shape
    assert k == k2 and m % tm == 0 and n % tn == 0 and k % tk == 0, (a.shape, b.shape, tm, tn, tk)
    nk = k // tk
    has_acc = acc is not None

    def body(*refs):
        if has_acc:
            a_ref, b_ref, c_ref, o_ref, acc_ref = refs
        else:
            a_ref, b_ref, o_ref, acc_ref = refs
        kk = pl.program_id(2)
        part = _nn(a_ref[...].astype(BF16), b_ref[...].astype(BF16))

        @pl.when(kk == 0)
        def _():
            if has_acc:
                acc_ref[...] = c_ref[...] + part
            else:
                acc_ref[...] = part

        @pl.when(kk > 0)
        def _():
            acc_ref[...] += part

        @pl.when(kk == nk - 1)
        def _():
            o_ref[...] = acc_ref[...].astype(out_dtype)

    in_specs = [pl.BlockSpec((tm, tk), lambda i, j, kk: (i, kk)), pl.BlockSpec((tk, tn), lambda i, j, kk: (kk, j))]
    args = [a, b]
    if has_acc:
        in_specs.append(pl.BlockSpec((tm, tn), lambda i, j, kk: (i, j)))
        args.append(acc)
    return _pcall(
        body, name=name, grid=(m // tm, n // tn, nk), in_specs=in_specs,
        out_specs=pl.BlockSpec((tm, tn), lambda i, j, kk: (i, j)),
        out_shape=SDS((m, n), out_dtype), scratch_shapes=[pltpu.VMEM((tm, tn), F32)],
        compiler_params=_cp(("parallel", "parallel", "arbitrary"), VMEM_BIG),
    )(*args)


def _modvec(cc, wm, bm):
    def body(c_ref, w_ref, b_ref, o_ref):
        o_ref[...] = _nn(_silu(c_ref[...]).astype(BF16), w_ref[...]) + b_ref[...]

    return _pcall(body, name="modvec", out_shape=SDS((8, 3 * D), F32), compiler_params=_cp(None, VMEM_BIG))(cc, wm, bm)


def _dcctx(dmodc, wm):
    def body(d_ref, w_ref, o_ref):
        o_ref[...] = _nt(d_ref[...].astype(BF16), w_ref[...])

    return _pcall(
        body, name="dcctx", grid=(1,), in_specs=[_full((8, 2 * D)), pl.BlockSpec((D, 2 * D), lambda i: (0, 0))],
        out_specs=_full((8, D)), out_shape=SDS((8, D), F32), compiler_params=_cp(("arbitrary",), VMEM_BIG),
    )(dmodc, wm)


def _prep_h(x, ng, scale, shift, name):
    m = x.shape[0]

    def body(x_ref, g_ref, sc_ref, sh_ref, h_ref):
        xf = x_ref[...]
        r = lax.rsqrt(jnp.mean(xf * xf, axis=-1, keepdims=True) + EPS)
        y = (xf * r) * g_ref[...]
        h_ref[...] = (y * (1.0 + sc_ref[...]) + sh_ref[...]).astype(BF16)

    row = pl.BlockSpec((TOK, D), lambda i: (i, 0))
    return _pcall(
        body, name=name, grid=(m // TOK,), in_specs=[row, _full((1, D)), _full((1, D)), _full((1, D))],
        out_specs=row, out_shape=SDS((m, D), BF16), compiler_params=_cp(("parallel",)),
    )(x, ng, scale, shift)


def _prep_bwd(x, dh, dx1, ng, scale, name):
    m = x.shape[0]
    has_res = dx1 is not None

    def body(*refs):
        if has_res:
            x_ref, dh_ref, r_ref, g_ref, sc_ref, dx_ref, dg_ref, dsc_ref, dsh_ref = refs
        else:
            x_ref, dh_ref, g_ref, sc_ref, dx_ref, dg_ref, dsc_ref, dsh_ref = refs
        i = pl.program_id(0)

        @pl.when(i == 0)
        def _():
            dg_ref[...] = jnp.zeros_like(dg_ref)
            dsc_ref[...] = jnp.zeros_like(dsc_ref)
            dsh_ref[...] = jnp.zeros_like(dsh_ref)

        xf = x_ref[...]
        dh_ = dh_ref[...]
        r = lax.rsqrt(jnp.mean(xf * xf, axis=-1, keepdims=True) + EPS)
        xh = xf * r
        y = xh * g_ref[...]
        dsh_ref[...] += jnp.sum(dh_, axis=0, keepdims=True)
        dsc_ref[...] += jnp.sum(dh_ * y, axis=0, keepdims=True)
        dy = dh_ * (1.0 + sc_ref[...])
        dg_ref[...] += jnp.sum(dy * xh, axis=0, keepdims=True)
        dxh = dy * g_ref[...]
        dx = r * (dxh - xh * jnp.mean(dxh * xh, axis=-1, keepdims=True))
        if has_res:
            dx = dx + r_ref[...]
        dx_ref[...] = dx

    row = pl.BlockSpec((TOK, D), lambda i: (i, 0))
    vec = _full((1, D))
    in_specs = [row, row] + ([row] if has_res else []) + [vec, vec]
    args = [x, dh] + ([dx1] if has_res else []) + [ng, scale]
    return _pcall(
        body, name=name, grid=(m // TOK,), in_specs=in_specs, out_specs=[row, vec, vec, vec],
        out_shape=[SDS((m, D), F32), SDS((1, D), F32), SDS((1, D), F32), SDS((1, D), F32)],
        compiler_params=_cp(("arbitrary",)),
    )(*args)


def _ln_fwd(p, ln_g, ln_b):
    m = p.shape[0]

    def body(va_ref, g_ref, b_ref, vr_ref, vc_ref):
        xf = va_ref[...]
        xc = xf - jnp.mean(xf, axis=-1, keepdims=True)
        y = xc * lax.rsqrt(jnp.mean(xc * xc, axis=-1, keepdims=True) + EPS)
        vn = y * g_ref[...] + b_ref[...]
        vr_ref[...] = vn[:, 0:256].astype(BF16)
        vc_ref[0] = vn[:, 256:384].astype(BF16)
        vc_ref[1] = vn[:, 384:512].astype(BF16)

    return _pcall(
        body, name="ln_fwd", grid=(m // TOK,),
        in_specs=[pl.BlockSpec((TOK, 512), lambda i: (i, 9)), _full((1, 512)), _full((1, 512))],
        out_specs=[pl.BlockSpec((TOK, 256), lambda i: (i, 0)), pl.BlockSpec((2, TOK, 128), lambda i: (0, i, 0))],
        out_shape=[SDS((m, 256), BF16), SDS((2, m, 128), BF16)], compiler_params=_cp(("parallel",)),
    )(p, ln_g, ln_b)


COLB = 2048


def _colmix_fwd(vnc, ws23, bs23):
    rows = vnc.shape[2] // COLB

    def body(v_ref, w_ref, b_ref, o_ref):
        o_ref[0] = _nn(w_ref[0], v_ref[0]) + b_ref[0]

    return _pcall(
        body, name="colmix_fwd", grid=(2, rows),
        in_specs=[pl.BlockSpec((1, AC, COLB), lambda g, j: (g, 0, j)), pl.BlockSpec((1, AC, AC), lambda g, j: (g, 0, 0)),
                  pl.BlockSpec((1, AC, 1), lambda g, j: (g, 0, 0))],
        out_specs=pl.BlockSpec((1, AC, COLB), lambda g, j: (g, 0, j)),
        out_shape=SDS(vnc.shape, F32), compiler_params=_cp(("parallel", "parallel")),
    )(vnc, ws23, bs23)


def _colmix_bwd(dsvc, vnc, ws23t):
    rows = vnc.shape[2] // COLB

    def body(d_ref, v_ref, wt_ref, dv_ref, dw_ref, db_ref):
        j = pl.program_id(1)

        @pl.when(j == 0)
        def _():
            dw_ref[...] = jnp.zeros_like(dw_ref)
            db_ref[...] = jnp.zeros_like(db_ref)

        d = d_ref[0]
        d16 = d.astype(BF16)
        dv_ref[0] = _nn(wt_ref[0], d16)
        dw_ref[0] += _nt(d16, v_ref[0])
        db_ref[0] += jnp.sum(d, axis=1, keepdims=True)

    blk = pl.BlockSpec((1, AC, COLB), lambda g, j: (g, 0, j))
    return _pcall(
        body, name="colmix_bwd", grid=(2, rows),
        in_specs=[blk, blk, pl.BlockSpec((1, AC, AC), lambda g, j: (g, 0, 0))],
        out_specs=[blk, pl.BlockSpec((1, AC, AC), lambda g, j: (g, 0, 0)), pl.BlockSpec((1, AC, 1), lambda g, j: (g, 0, 0))],
        out_shape=[SDS(vnc.shape, F32), SDS((2, AC, AC), F32), SDS((2, AC, 1), F32)],
        compiler_params=_cp(("parallel", "arbitrary")),
    )(dsvc, vnc, ws23t)


def _head_norm(o, gbn):
    out = []
    for h in range(4):
        oh = o[:, 128 * h:128 * h + 128]
        r = lax.rsqrt(jnp.mean(oh * oh, axis=-1, keepdims=True) + EPS)
        out.append((r, oh * r))
    return out


def _mid_fwd(o_f, o_b, p, vnr, svc, ws01, bs01, gbn):
    m = p.shape[0]

    def body(of_ref, ob_ref, zb_ref, ua_ref, za_ref, vnr_ref, svc_ref, w_ref, b_ref, g_ref, ya_ref, yb_ref, svr_ref):
        o = of_ref[...] + ob_ref[...]
        zb = zb_ref[...]
        parts = []
        for h, (r, xh) in enumerate(_head_norm(o, None)):
            parts.append(xh * g_ref[:, 128 * h:128 * h + 128])
        on = jnp.concatenate(parts, axis=1)
        yb_ref[...] = (on * _silu(zb)).astype(BF16)
        for j in range(TOK // AC):
            for g in range(2):
                sv = _nn(w_ref[g], vnr_ref[AC * j:AC * j + AC, AC * g:AC * g + AC]) + b_ref[g]
                svr_ref[AC * j:AC * j + AC, AC * g:AC * g + AC] = sv
        sz = _silu(za_ref[...])
        u = ua_ref[...]
        ya_ref[:, 0:256] = ((u[:, 0:256] * svr_ref[...]) * sz[:, 0:256]).astype(BF16)
        ya_ref[:, 256:384] = ((u[:, 256:384] * svc_ref[0]) * sz[:, 256:384]).astype(BF16)
        ya_ref[:, 384:512] = ((u[:, 384:512] * svc_ref[1]) * sz[:, 384:512]).astype(BF16)

    r512 = pl.BlockSpec((TOK, 512), lambda i: (i, 0))
    return _pcall(
        body, name="mid_fwd", grid=(m // TOK,),
        in_specs=[r512, r512, pl.BlockSpec((TOK, 512), lambda i: (i, 6)), pl.BlockSpec((TOK, 512), lambda i: (i, 7)),
                  pl.BlockSpec((TOK, 512), lambda i: (i, 8)), pl.BlockSpec((TOK, 256), lambda i: (i, 0)),
                  pl.BlockSpec((2, TOK, 128), lambda i: (0, i, 0)), _full((2, AC, AC)), _full((2, AC, 1)), _full((1, 512))],
        out_specs=[r512, r512, pl.BlockSpec((TOK, 256), lambda i: (i, 0))],
        out_shape=[SDS((m, 512), BF16), SDS((m, 512), BF16), SDS((m, 256), F32)],
        compiler_params=_cp(("parallel",)),
    )(o_f, o_b, p, p, p, vnr, svc, ws01, bs01, gbn)


def _merge_fwd(p, ya, yb):
    m = p.shape[0]

    def body(ga_ref, gb_ref, ya_ref, yb_ref, m_ref):
        m_ref[...] = (jax.nn.sigmoid(ga_ref[...]) * ya_ref[...] + jax.nn.sigmoid(gb_ref[...]) * yb_ref[...]).astype(BF16)

    row = pl.BlockSpec((TOK, D), lambda i: (i, 0))
    return _pcall(
        body, name="merge_fwd", grid=(m // TOK,),
        in_specs=[row, pl.BlockSpec((TOK, D), lambda i: (i, 1)), row, row], out_specs=row,
        out_shape=SDS((m, D), BF16), compiler_params=_cp(("parallel",)),
    )(p, p, ya, yb)


def _loss_head(x, out, tgt, gate, gf):
    m = x.shape[0]

    def body(x_ref, o_ref, t_ref, gate_ref, gf_ref, dx1_ref, dout_ref, loss_ref, dgate_ref, dgf_ref):
        i = pl.program_id(0)

        @pl.when(i == 0)
        def _():
            loss_ref[...] = jnp.zeros_like(loss_ref)
            dgate_ref[...] = jnp.zeros_like(dgate_ref)
            dgf_ref[...] = jnp.zeros_like(dgf_ref)

        out_ = o_ref[...]
        x1 = x_ref[...] + gate_ref[...] * out_
        r = lax.rsqrt(jnp.mean(x1 * x1, axis=-1, keepdims=True) + EPS)
        xh = x1 * r
        err = xh * gf_ref[...] - t_ref[...]
        loss_ref[...] += 0.5 * jnp.sum(jnp.mean(err * err, axis=-1, keepdims=True), axis=0, keepdims=True)
        dy = err * (1.0 / D)
        dgf_ref[...] += jnp.sum(dy * xh, axis=0, keepdims=True)
        dxh = dy * gf_ref[...]
        dx1 = r * (dxh - xh * jnp.mean(dxh * xh, axis=-1, keepdims=True))
        dx1_ref[...] = dx1
        dout_ref[...] = (gate_ref[...] * dx1).astype(BF16)
        dgate_ref[...] += jnp.sum(dx1 * out_, axis=0, keepdims=True)

    row = pl.BlockSpec((TOK, D), lambda i: (i, 0))
    vec = _full((1, D))
    return _pcall(
        body, name="loss_head", grid=(m // TOK,), in_specs=[row, row, row, vec, vec],
        out_specs=[row, row, _full((1, 128)), vec, vec],
        out_shape=[SDS((m, D), F32), SDS((m, D), BF16), SDS((1, 128), F32), SDS((1, D), F32), SDS((1, D), F32)],
        compiler_params=_cp(("arbitrary",)),
    )(x, out, tgt, gate, gf)


def _merge_bwd(dm, ya, yb, p):
    m = p.shape[0]

    def body(dm_ref, ya_ref, yb_ref, ga_ref, gb_ref, dya_ref, dyb_ref, dp_ref):
        dm_ = dm_ref[...]
        sa = jax.nn.sigmoid(ga_ref[...])
        sb = jax.nn.sigmoid(gb_ref[...])
        dya_ref[...] = (dm_ * sa).astype(BF16)
        dyb_ref[...] = (dm_ * sb).astype(BF16)
        dp_ref[:, 0:D] = (dm_ * ya_ref[...] * (sa * (1.0 - sa))).astype(BF16)
        dp_ref[:, D:2 * D] = (dm_ * yb_ref[...] * (sb * (1.0 - sb))).astype(BF16)

    row = pl.BlockSpec((TOK, D), lambda i: (i, 0))
    return _pcall(
        body, name="merge_bwd", grid=(m // TOK,),
        in_specs=[row, row, row, row, pl.BlockSpec((TOK, D), lambda i: (i, 1))],
        out_specs=[row, row, pl.BlockSpec((TOK, 2 * D), lambda i: (i, 0))],
        out_shape=[SDS((m, D), BF16), SDS((m, D), BF16), SDS((m, NP), BF16)],
        compiler_params=_cp(("parallel",)),
    )(dm, ya, yb, p, p)


def _mid_bwd(dya_in, dyb_in, p, svr, svc, o_f, o_b, gbn, dp):
    m = p.shape[0]

    def body(dya_ref, dyb_ref, zb_ref, ua_ref, za_ref, svr_ref, svc_ref, of_ref, ob_ref, g_ref, dpi_ref,
             dp_ref, dsr_ref, dsc_ref, do_ref, dg_ref):
        i = pl.program_id(0)

        @pl.when(i == 0)
        def _():
            dg_ref[...] = jnp.zeros_like(dg_ref)

        dya = dya_ref[...]
        u = ua_ref[...]
        za = za_ref[...]
        sz = _silu(za)
        sv = jnp.concatenate([svr_ref[...], svc_ref[0], svc_ref[1]], axis=1)
        dp_ref[:, 512:1024] = (dya * sv * sz).astype(BF16)
        dsv = dya * u * sz
        dsr_ref[...] = dsv[:, 0:256]
        dsc_ref[0] = dsv[:, 256:384]
        dsc_ref[1] = dsv[:, 384:512]
        dp_ref[:, 1024:1536] = (dya * u * sv * _dsilu(za)).astype(BF16)

        dyb = dyb_ref[...]
        zb = zb_ref[...]
        o = of_ref[...] + ob_ref[...]
        szb = _silu(zb)
        dszb = _dsilu(zb)
        for h, (r, xh) in enumerate(_head_norm(o, None)):
            sl = slice(128 * h, 128 * h + 128)
            gh = g_ref[:, sl]
            don = dyb[:, sl] * szb[:, sl]
            dp_ref[:, sl] = (dyb[:, sl] * (xh * gh) * dszb[:, sl]).astype(BF16)
            dg_ref[:, sl] += jnp.sum(don * xh, axis=0, keepdims=True)
            dxh = don * gh
            do_ref[:, sl] = r * (dxh - xh * jnp.mean(dxh * xh, axis=-1, keepdims=True))

    r512 = pl.BlockSpec((TOK, 512), lambda i: (i, 0))
    return _pcall(
        body, name="mid_bwd", grid=(m // TOK,),
        in_specs=[r512, r512, pl.BlockSpec((TOK, 512), lambda i: (i, 6)), pl.BlockSpec((TOK, 512), lambda i: (i, 7)),
                  pl.BlockSpec((TOK, 512), lambda i: (i, 8)), pl.BlockSpec((TOK, 256), lambda i: (i, 0)),
                  pl.BlockSpec((2, TOK, 128), lambda i: (0, i, 0)), r512, r512, _full((1, 512)),
                  pl.BlockSpec(memory_space=pl.ANY)],
        out_specs=[pl.BlockSpec((TOK, 1536), lambda i: (i, 2)), pl.BlockSpec((TOK, 256), lambda i: (i, 0)),
                   pl.BlockSpec((2, TOK, 128), lambda i: (0, i, 0)), r512, _full((1, 512))],
        out_shape=[SDS((m, NP), BF16), SDS((m, 256), F32), SDS((2, m, 128), F32), SDS((m, 512), F32), SDS((1, 512), F32)],
        input_output_aliases={10: 0}, compiler_params=_cp(("arbitrary",)),
    )(dya_in, dyb_in, p, p, p, svr, svc, o_f, o_b, gbn, dp)


def _ln_bwd(dsr, vnr, dvnc, p, ws01t, ln_g, dp):
    m = p.shape[0]

    def body(dsr_ref, vnr_ref, dvc_ref, va_ref, wt_ref, g_ref, dpi_ref, dp_ref, dw_ref, db_ref, dlg_ref, dlb_ref, dvn_ref):
        i = pl.program_id(0)

        @pl.when(i == 0)
        def _():
            dw_ref[...] = jnp.zeros_like(dw_ref)
            db_ref[...] = jnp.zeros_like(db_ref)
            dlg_ref[...] = jnp.zeros_like(dlg_ref)
            dlb_ref[...] = jnp.zeros_like(dlb_ref)

        for j in range(TOK // AC):
            for g in range(2):
                d = dsr_ref[AC * j:AC * j + AC, AC * g:AC * g + AC]
                d16 = d.astype(BF16)
                dvn_ref[AC * j:AC * j + AC, AC * g:AC * g + AC] = _nn(wt_ref[g], d16)
                dw_ref[g] += _nt(d16, vnr_ref[AC * j:AC * j + AC, AC * g:AC * g + AC])
                db_ref[g] += jnp.sum(d, axis=1, keepdims=True)
        dvn_ref[:, 256:384] = dvc_ref[0]
        dvn_ref[:, 384:512] = dvc_ref[1]
        dvn = dvn_ref[...]
        xf = va_ref[...]
        xc = xf - jnp.mean(xf, axis=-1, keepdims=True)
        rs = lax.rsqrt(jnp.mean(xc * xc, axis=-1, keepdims=True) + EPS)
        xh = xc * rs
        dlg_ref[...] += jnp.sum(dvn * xh, axis=0, keepdims=True)
        dlb_ref[...] += jnp.sum(dvn, axis=0, keepdims=True)
        dxh = dvn * g_ref[...]
        dva = rs * (dxh - jnp.mean(dxh, axis=-1, keepdims=True) - xh * jnp.mean(dxh * xh, axis=-1, keepdims=True))
        dp_ref[...] = dva.astype(BF16)

    return _pcall(
        body, name="ln_bwd", grid=(m // TOK,),
        in_specs=[pl.BlockSpec((TOK, 256), lambda i: (i, 0)), pl.BlockSpec((TOK, 256), lambda i: (i, 0)),
                  pl.BlockSpec((2, TOK, 128), lambda i: (0, i, 0)), pl.BlockSpec((TOK, 512), lambda i: (i, 9)),
                  _full((2, AC, AC)), _full((1, 512)), pl.BlockSpec(memory_space=pl.ANY)],
        out_specs=[pl.BlockSpec((TOK, 512), lambda i: (i, 9)), _full((2, AC, AC)), _full((2, AC, 1)), _full((1, 512)), _full((1, 512))],
        out_shape=[SDS((m, NP), BF16), SDS((2, AC, AC), F32), SDS((2, AC, 1), F32), SDS((1, 512), F32), SDS((1, 512), F32)],
        scratch_shapes=[pltpu.VMEM((TOK, 512), F32)],
        input_output_aliases={6: 0}, compiler_params=_cp(("arbitrary",)),
    )(dsr, vnr, dvnc, p, ws01t, ln_g, dp)


def _tri_mm(tri, a):
    a1 = a.astype(BF16)
    r1 = a - a1.astype(F32)
    a2 = r1.astype(BF16)
    a3 = (r1 - a2.astype(F32)).astype(BF16)
    return _nn(tri, a1) + _nn(tri, a2) + _nn(tri, a3)


def _gla_masks(reverse):
    ri = lax.broadcasted_iota(jnp.int32, (CH, CH), 0)
    ci = lax.broadcasted_iota(jnp.int32, (CH, CH), 1)
    vis = (ci >= ri) if reverse else (ci <= ri)
    vis_t = (ci <= ri) if reverse else (ci >= ri)
    r4 = lax.broadcasted_iota(jnp.int32, (4 * CH, CH), 0) & (CH - 1)
    c4 = lax.broadcasted_iota(jnp.int32, (4 * CH, CH), 1)
    vis4 = (c4 >= r4) if reverse else (c4 <= r4)
    vis4_t = (c4 <= r4) if reverse else (c4 >= r4)
    lane = lax.broadcasted_iota(jnp.int32, (1, 256), 1)
    hm = [(lane >= CH * h) & (lane < CH * h + CH) for h in range(4)]
    return vis, vis_t, vis4, vis4_t, hm


def _stack_heads(x, hm):
    return jnp.concatenate([jnp.where(hm[h], x, 0.0).astype(BF16) for h in range(4)], axis=0)


def _diag_heads(full, hm):
    acc = jnp.where(hm[0], full[0:128], 0.0)
    for h in range(1, 4):
        acc = acc + jnp.where(hm[h], full[128 * h:128 * h + 128], 0.0)
    return acc


def _gla_fwd(p, qkv_blk, lr, lrw, gbias, s0, *, reverse, name):
    m = p.shape[0]
    nb = m // GLA_TB
    nc = GLA_TB // CH
    rmap = (lambda i: nb - 1 - i) if reverse else (lambda i: i)

    def body(qkv_ref, lr_ref, lrw_ref, gb_ref, s0_ref, o_ref, sb_ref, sfin_ref, st_ref):
        i = pl.program_id(0)

        @pl.when(i == 0)
        def _():
            st_ref[...] = s0_ref[...]

        vis, _, vis4, _, hm = _gla_masks(reverse)
        tri = vis.astype(F32).astype(BF16)
        logits = _nn(lr_ref[...].astype(BF16), lrw_ref[...]) + gb_ref[...]
        a_all = _logsig(logits) * (1.0 / 16.0)
        st = st_ref[...]
        for c in (range(nc - 1, -1, -1) if reverse else range(nc)):
            rows = slice(CH * c, CH * c + CH)
            b = _tri_mm(tri, a_all[rows])
            bl = b[0:1] if reverse else b[CH - 1:CH]
            q = qkv_ref[rows, 0:256] * 0.125
            k = qkv_ref[rows, 256:512]
            v16 = qkv_ref[rows, 512:1024].astype(BF16)
            qd = q * jnp.exp(b)
            kd16 = (k * jnp.exp(-b)).astype(BF16)
            kdec16 = (k * jnp.exp(bl - b)).astype(BF16)
            qstack = _stack_heads(qd, hm)
            sc = jnp.where(vis4, _nt(qstack, kd16), 0.0).astype(BF16)
            inter = _nt(qstack, st.astype(BF16))
            for h in range(4):
                o_ref[rows, 128 * h:128 * h + 128] = (
                    _nn(sc[CH * h:CH * h + CH], v16[:, 128 * h:128 * h + 128]) + inter[CH * h:CH * h + CH])
            sb_ref[c] = st
            st = st * jnp.exp(bl) + _diag_heads(_tn(v16, kdec16), hm)
        st_ref[...] = st

        @pl.when(i == nb - 1)
        def _():
            sfin_ref[...] = st

    return _pcall(
        body, name=name, grid=(nb,),
        in_specs=[pl.BlockSpec((GLA_TB, 1024), lambda i: (rmap(i), qkv_blk)), pl.BlockSpec((GLA_TB, LRW), lambda i: (rmap(i), 0)),
                  _full((LRW, 256)), _full((1, 256)), _full((128, 256))],
        out_specs=[pl.BlockSpec((GLA_TB, 512), lambda i: (rmap(i), 0)), pl.BlockSpec((nc, 128, 256), lambda i: (rmap(i), 0, 0)),
                   _full((128, 256))],
        out_shape=[SDS((m, 512), F32), SDS((m // CH, 128, 256), F32), SDS((128, 256), F32)],
        scratch_shapes=[pltpu.VMEM((128, 256), F32)], compiler_params=_cp(("arbitrary",)),
    )(p, lr, lrw, gbias, s0)


def _gla_bwd(p, qkv_blk, lr, lrw, lrwt, gbias, sb, dsfin, do, prev, dp, *, reverse, name):
    m = p.shape[0]
    nb = m // GLA_TB
    nc = GLA_TB // CH
    rmap = (lambda i: i) if reverse else (lambda i: nb - 1 - i)
    has_prev = prev is not None
    has_dp = dp is not None

    def body(*refs):
        refs = list(refs)
        qkv_ref, lr_ref, lrw_ref, lrwt_ref, gb_ref, sb_ref, dsfin_ref, do_ref = refs[:8]
        refs = refs[8:]
        if has_prev:
            pq_ref, plr_ref = refs[:2]
            refs = refs[2:]
        if has_dp:
            refs = refs[1:]
        dqkv_ref, dlr_ref, dw2_ref, dgb_ref, ds0_ref, dst_ref, dlog_ref = refs
        i = pl.program_id(0)

        @pl.when(i == 0)
        def _():
            dst_ref[...] = dsfin_ref[...]
            dw2_ref[...] = jnp.zeros_like(dw2_ref)
            dgb_ref[...] = jnp.zeros_like(dgb_ref)

        vis, vis_t, vis4, vis4_t, hm = _gla_masks(reverse)
        tri = vis.astype(F32).astype(BF16)
        tri_t = vis_t.astype(F32).astype(BF16)
        lr16 = lr_ref[...].astype(BF16)
        logits = _nn(lr16, lrw_ref[...]) + gb_ref[...]
        a_all = _logsig(logits) * (1.0 / 16.0)
        dsig = (1.0 - jax.nn.sigmoid(logits)) * (1.0 / 16.0)
        dst = dst_ref[...]
        for c in (range(nc) if reverse else range(nc - 1, -1, -1)):
            rows = slice(CH * c, CH * c + CH)
            b = _tri_mm(tri, a_all[rows])
            bl = b[0:1] if reverse else b[CH - 1:CH]
            eb = jnp.exp(b)
            enb = jnp.exp(-b)
            ebl = jnp.exp(bl - b)
            el = jnp.exp(bl)
            q = qkv_ref[rows, 0:256] * 0.125
            k = qkv_ref[rows, 256:512]
            v16 = qkv_ref[rows, 512:1024].astype(BF16)
            do16 = do_ref[rows, :].astype(BF16)
            qd = q * eb
            kd = k * enb
            kdec = k * ebl
            st = sb_ref[c]
            st16 = st.astype(BF16)
            dst16 = dst.astype(BF16)
            qd16 = qd.astype(BF16)
            kd16 = kd.astype(BF16)
            qstack = _stack_heads(qd, hm)
            kstack = _stack_heads(kd, hm)
            kdecstack = _stack_heads(kdec, hm)
            pt = jnp.where(vis4_t, _nt(kstack, qd16), 0.0).astype(BF16)
            dvinter = _nt(kdecstack, dst16)
            dqd = jnp.zeros((CH, 256), F32)
            dkd = jnp.zeros((CH, 256), F32)
            dkdec = jnp.zeros((CH, 256), F32)
            for h in range(4):
                hs = slice(128 * h, 128 * h + 128)
                rh = slice(CH * h, CH * h + CH)
                doh = do16[:, hs]
                vh = v16[:, hs]
                dpm = jnp.where(vis, _nt(doh, vh), 0.0).astype(BF16)
                dpt = jnp.where(vis_t, _nt(vh, doh), 0.0).astype(BF16)
                dv_h = _nn(pt[rh], doh) + dvinter[rh]
                if has_prev:
                    dv_h = dv_h + pq_ref[rows, 512 + 128 * h:512 + 128 * h + 128]
                dqkv_ref[rows, 512 + 128 * h:512 + 128 * h + 128] = dv_h.astype(dqkv_ref.dtype)
                dqd = dqd + _nn(dpm, kstack[rh]) + jnp.where(hm[h], _nn(doh, st16), 0.0)
                dkd = dkd + _nn(dpt, qstack[rh])
                dkdec = dkdec + jnp.where(hm[h], _nn(vh, dst16), 0.0)
            dq = dqd * eb * 0.125
            dk = dkd * enb + dkdec * ebl
            if has_prev:
                dq = dq + pq_ref[rows, 0:256]
                dk = dk + pq_ref[rows, 256:512]
            dqkv_ref[rows, 0:256] = dq.astype(dqkv_ref.dtype)
            dqkv_ref[rows, 256:512] = dk.astype(dqkv_ref.dtype)
            g_kdec = dkdec * kdec
            db = dqd * qd - dkd * kd - g_kdec
            dbl = jnp.sum(g_kdec, axis=0, keepdims=True) + jnp.sum(st * dst, axis=0, keepdims=True) * el
            da = _tri_mm(tri_t, db) + dbl
            dlog_ref[rows, :] = da * dsig[rows]
            dst = dst * el + _diag_heads(_tn(do16, qd16), hm)
        dst_ref[...] = dst
        dlog = dlog_ref[...]
        dlog16 = dlog.astype(BF16)
        dlr = _nn(dlog16, lrwt_ref[...])
        if has_prev:
            dlr = dlr + plr_ref[...]
        dlr_ref[...] = dlr
        dw2_ref[...] += _tn(lr16, dlog16)
        dgb_ref[...] += jnp.sum(dlog, axis=0, keepdims=True)

        @pl.when(i == nb - 1)
        def _():
            ds0_ref[...] = dst

    in_specs = [pl.BlockSpec((GLA_TB, 1024), lambda i: (rmap(i), qkv_blk)), pl.BlockSpec((GLA_TB, LRW), lambda i: (rmap(i), 0)),
                _full((LRW, 256)), _full((256, LRW)), _full((1, 256)), pl.BlockSpec((nc, 128, 256), lambda i: (rmap(i), 0, 0)),
                _full((128, 256)), pl.BlockSpec((GLA_TB, 512), lambda i: (rmap(i), 0))]
    args = [p, lr, lrw, lrwt, gbias, sb, dsfin, do]
    if has_prev:
        in_specs += [pl.BlockSpec((GLA_TB, 1024), lambda i: (rmap(i), 0)), pl.BlockSpec((GLA_TB, LRW), lambda i: (rmap(i), 0))]
        args += list(prev)
    aliases = {}
    if has_dp:
        in_specs.append(pl.BlockSpec(memory_space=pl.ANY))
        aliases = {len(args): 0}
        args.append(dp)
        dq_spec = pl.BlockSpec((GLA_TB, 1024), lambda i: (rmap(i), qkv_blk))
        dq_shape = SDS(dp.shape, dp.dtype)
    else:
        dq_spec = pl.BlockSpec((GLA_TB, 1024), lambda i: (rmap(i), 0))
        dq_shape = SDS((m, 1024), F32)
    return _pcall(
        body, name=name, grid=(nb,), in_specs=in_specs,
        out_specs=[dq_spec, pl.BlockSpec((GLA_TB, LRW), lambda i: (rmap(i), 0)), _full((LRW, 256)), _full((1, 256)), _full((128, 256))],
        out_shape=[dq_shape, SDS((m, LRW), F32), SDS((LRW, 256), F32), SDS((1, 256), F32), SDS((128, 256), F32)],
        scratch_shapes=[pltpu.VMEM((128, 256), F32), pltpu.VMEM((GLA_TB, 256), F32)],
        input_output_aliases=aliases, compiler_params=_cp(("arbitrary",)),
    )(*args)


def _device_step(x, c, ctx, c_ctx, tgt, wm, bm, ng, wi, wlr, ln_g, ln_b, ws, bs, w2, gb2, gbn, wpa, wpb, wo, gf):
    L = x.shape[0]
    wit = wi.T
    wlrt = wlr.T
    ws16 = ws.astype(BF16)
    wst16 = jnp.swapaxes(ws, 1, 2).astype(BF16)
    bscol = bs[:, :, None]
    lrw = [jnp.zeros((LRW, 256), F32).at[16 * r:16 * r + 16].set(w2[r]).astype(BF16) for r in range(2)]
    lrwt = [w.T for w in lrw]
    gbias = [gb2[r:r + 1] for r in range(2)]

    cc = jnp.zeros((8, D), F32).at[0:1].set(c).at[1:2].set(c_ctx)
    mod = _modvec(cc, wm, bm)
    shift, scale, gate = mod[0:1, 0:D], mod[0:1, D:2 * D], mod[0:1, 2 * D:3 * D]
    shift_c, scale_c = mod[1:2, 0:D], mod[1:2, D:2 * D]

    hc = _prep_h(ctx, ng, scale_c, shift_c, "prep_hc")
    pc = _mm(hc, wi[:, 2048:3072], tm=256, tn=1024, tk=D, out_dtype=F32, name="mm_pc")
    plrc = _mm(hc, wlr, tm=256, tn=LRW, tk=D, out_dtype=F32, name="mm_plrc")
    zero_s = jnp.zeros((128, 256), F32)
    _, sbc_f, sc_f = _gla_fwd(pc, 0, plrc, lrw[0], gbias[0], zero_s, reverse=False, name="gla_fwd_cf")
    _, sbc_b, sc_b = _gla_fwd(pc, 0, plrc, lrw[1], gbias[1], zero_s, reverse=True, name="gla_fwd_cb")

    h = _prep_h(x, ng, scale, shift, "prep_h")
    p = _mm(h, wi, tm=512, tn=1024, tk=D, out_dtype=F32, name="mm_p")
    plr = _mm(h, wlr, tm=1024, tn=LRW, tk=D, out_dtype=F32, name="mm_plr")
    o_f, sb_f, _ = _gla_fwd(p, 2, plr, lrw[0], gbias[0], sc_f, reverse=False, name="gla_fwd_f")
    o_b, sb_b, _ = _gla_fwd(p, 2, plr, lrw[1], gbias[1], sc_b, reverse=True, name="gla_fwd_b")
    vnr, vnc = _ln_fwd(p, ln_g, ln_b)
    svc = _colmix_fwd(vnc.reshape(2, AC, L), ws16[2:4], bscol[2:4]).reshape(2, L, 128)
    ya_in, yb_in, svr = _mid_fwd(o_f, o_b, p, vnr, svc, ws16[0:2], bscol[0:2], gbn)
    ya = _mm(ya_in, wpa, tm=1024, tn=D, tk=512, out_dtype=F32, name="mm_ya")
    yb = _mm(yb_in, wpb, tm=1024, tn=D, tk=512, out_dtype=F32, name="mm_yb")
    mrg = _merge_fwd(p, ya, yb)
    out = _mm(mrg, wo, tm=1024, tn=D, tk=D, out_dtype=F32, name="mm_out")
    dx1, dout, loss, dgate, dgf = _loss_head(x, out, tgt, gate, gf)

    dm = _mm(dout, wo.T, tm=1024, tn=D, tk=D, out_dtype=F32, name="mm_dm")
    dwo = _mm(mrg.T, dout, tm=D, tn=D, tk=1024, out_dtype=F32, name="mm_dwo")
    dya, dyb, dp = _merge_bwd(dm, ya, yb, p)
    dya_in = _mm(dya, wpa.T, tm=1024, tn=512, tk=D, out_dtype=F32, name="mm_dya_in")
    dyb_in = _mm(dyb, wpb.T, tm=1024, tn=512, tk=D, out_dtype=F32, name="mm_dyb_in")
    dwpa = _mm(ya_in.T, dya, tm=512, tn=D, tk=1024, out_dtype=F32, name="mm_dwpa")
    dwpb = _mm(yb_in.T, dyb, tm=512, tn=D, tk=1024, out_dtype=F32, name="mm_dwpb")
    dp, dsr, dsc, do, dgbn = _mid_bwd(dya_in, dyb_in, p, svr, svc, o_f, o_b, gbn, dp)
    dvnc, dws23, dbs23 = _colmix_bwd(dsc.reshape(2, AC, L), vnc.reshape(2, AC, L), wst16[2:4])
    dp, dws01, dbs01, dlng, dlnb = _ln_bwd(dsr, vnr, dvnc.reshape(2, L, 128), p, wst16[0:2], ln_g, dp)
    zero_ds = jnp.zeros((128, 256), F32)
    dqkv_f, dlr_f, dw2_f, dgb_f, ds0_f = _gla_bwd(p, 2, plr, lrw[0], lrwt[0], gbias[0], sb_f, zero_ds, do, None, None,
                                                  reverse=False, name="gla_bwd_f")
    dp, dlr, dw2_b, dgb_b, ds0_b = _gla_bwd(p, 2, plr, lrw[1], lrwt[1], gbias[1], sb_b, zero_ds, do, (dqkv_f, dlr_f), dp,
                                            reverse=True, name="gla_bwd_b")
    zero_do = jnp.zeros((ctx.shape[0], 512), F32)
    dqkvc_f, dlrc_f, dw2c_f, dgbc_f, _ = _gla_bwd(pc, 0, plrc, lrw[0], lrwt[0], gbias[0], sbc_f, ds0_f, zero_do, None, None,
                                                  reverse=False, name="gla_bwd_cf")
    dqkvc, dlrc, dw2c_b, dgbc_b, _ = _gla_bwd(pc, 0, plrc, lrw[1], lrwt[1], gbias[1], sbc_b, ds0_b, zero_do,
                                              (dqkvc_f, dlrc_f), None, reverse=True, name="gla_bwd_cb")
    dhc = _mm(dqkvc, wit[2048:3072], tm=256, tn=D, tk=1024, out_dtype=F32, name="mm_dhc")
    dhc = _mm(dlrc, wlrt, tm=256, tn=D, tk=LRW, out_dtype=F32, name="mm_dhc_lr", acc=dhc)
    _, dng_c, dscale_c, dshift_c = _prep_bwd(ctx, dhc, None, ng, scale_c, "prep_bwd_c")

    dh = _mm(dp, wit, tm=1024, tn=D, tk=1024, out_dtype=F32, name="mm_dh")
    dh = _mm(dlr, wlrt, tm=1024, tn=D, tk=LRW, out_dtype=F32, name="mm_dh_lr", acc=dh)
    ht = h.T
    hct = hc.T
    dwi = _mm(ht, dp, tm=D, tn=1024, tk=1024, out_dtype=F32, name="mm_dwi")
    dwi_qkv = _mm(hct, dqkvc, tm=D, tn=1024, tk=256, out_dtype=F32, name="mm_dwi_c", acc=dwi[:, 2048:3072])
    dwlr = _mm(ht, dlr, tm=D, tn=LRW, tk=1024, out_dtype=F32, name="mm_dwlr")
    dwlr = _mm(hct, dlrc, tm=D, tn=LRW, tk=256, out_dtype=F32, name="mm_dwlr_c", acc=dwlr)
    dx, dng, dscale, dshift = _prep_bwd(x, dh, dx1, ng, scale, "prep_bwd")

    dmodc = jnp.concatenate([dshift_c, dscale_c], axis=1)
    dscc = _dcctx(jnp.zeros((8, 2 * D), F32).at[0:1].set(dmodc), wm)[0:1]
    dw2p = dw2_f + dw2c_f, dw2_b + dw2c_b
    return dict(
        loss=loss[0, 0], dx=dx, dwi=dwi, dwi_qkv=dwi_qkv, dwlr=dwlr, dwpa=dwpa, dwpb=dwpb, dwo=dwo,
        dmod=jnp.concatenate([dshift, dscale, dgate], axis=1), dmodc=dmodc, dscc=dscc, dng=dng + dng_c,
        dlng=dlng, dlnb=dlnb, dws=jnp.concatenate([dws01, dws23], axis=0),
        dbs=jnp.concatenate([dbs01, dbs23], axis=0)[:, :, 0], dgbn=dgbn, dgf=dgf,
        dw2=jnp.stack([dw2p[0][0:16], dw2p[1][16:32]]), dgb2=jnp.concatenate([dgb_f + dgbc_f, dgb_b + dgbc_b], axis=0),
    )


ANY = pl.BlockSpec(memory_space=pl.ANY)


def _coords():
    return lax.axis_index("x"), lax.axis_index("y"), lax.axis_index("c")


def _flip(v, bit):
    return 1 - v if bit else v


def _remote(src, dst, send_sem, recv_sem, dev):
    return pltpu.make_async_remote_copy(src_ref=src, dst_ref=dst, send_sem=send_sem, recv_sem=recv_sem,
                                        device_id=dev, device_id_type=MESH)


def _own(out, block, idx):
    return lax.dynamic_update_slice_in_dim(out, block[None], idx, axis=0)


def _gather_weights(split, whole, name):
    ns, nw = len(split), len(whole)
    n = ns + nw

    def body(*refs):
        ins, outs = refs[:n], refs[n:2 * n]
        a_send, a_recv, b_send, b_recv = refs[2 * n:]
        x, y, c = _coords()
        me = 2 * x + y
        sib = (x, y, 1 - c)
        peers = [(1 - x, y), (x, 1 - y), (1 - x, 1 - y)]
        sends = []
        for k in range(n):
            for j, (px, py) in enumerate(peers):
                if k < ns:
                    h = split[k].shape[0] // 2
                    rc = _remote(ins[k].at[pl.ds(c * h, h)], outs[k].at[me, pl.ds(c * h, h)], a_send.at[3 * k + j],
                                 a_recv.at[3 * k + j], (px, py, c))
                else:
                    rc = _remote(ins[k], outs[k].at[me], a_send.at[3 * k + j], a_recv.at[3 * k + j], (px, py, c))
                rc.start()
                sends.append(rc)
        for k in range(ns):
            h = split[k].shape[0] // 2
            for j, (px, py) in enumerate(peers):
                landed = outs[k].at[2 * px + py, pl.ds(c * h, h)]
                _remote(landed, landed, a_send.at[3 * k + j], a_recv.at[3 * k + j], (px, py, c)).wait_recv()
                fw = _remote(landed, landed, b_send.at[3 * k + j], b_recv.at[3 * k + j], sib)
                fw.start()
                sends.append(fw)
        for k in range(ns, n):
            for j, (px, py) in enumerate(peers):
                landed = outs[k].at[2 * px + py]
                _remote(landed, landed, a_send.at[3 * k + j], a_recv.at[3 * k + j], (px, py, c)).wait_recv()
        for k in range(ns):
            h = split[k].shape[0] // 2
            for j, (px, py) in enumerate(peers):
                passed = outs[k].at[2 * px + py, pl.ds((1 - c) * h, h)]
                _remote(passed, passed, b_send.at[3 * k + j], b_recv.at[3 * k + j], sib).wait_recv()
        for rc in sends:
            rc.wait_send()

    arrs = list(split) + list(whole)
    outs = _pcall(
        body, name=name, in_specs=[ANY] * n, out_specs=[ANY] * n,
        out_shape=[SDS((4,) + a.shape, a.dtype) for a in arrs],
        scratch_shapes=[pltpu.SemaphoreType.DMA((3 * n,)), pltpu.SemaphoreType.DMA((3 * n,)), pltpu.SemaphoreType.DMA((3 * ns,)),
                        pltpu.SemaphoreType.DMA((3 * ns,))],
    )(*arrs)
    me_xy = 2 * lax.axis_index("x") + lax.axis_index("y")
    return [_own(o, a, me_xy) for o, a in zip(outs, arrs)]


def _gather_all(a, name):
    masks = [(mx, my, mc) for mx in range(2) for my in range(2) for mc in range(2)][1:]

    def body(in_ref, out_ref, send_sems, recv_sems):
        x, y, c = _coords()
        me = 4 * x + 2 * y + c
        sends = []
        for j, (mx, my, mc) in enumerate(masks):
            peer = (_flip(x, mx), _flip(y, my), _flip(c, mc))
            rc = pltpu.make_async_remote_copy(
                src_ref=in_ref, dst_ref=out_ref.at[me], send_sem=send_sems.at[j], recv_sem=recv_sems.at[j],
                device_id=peer, device_id_type=MESH)
            rc.start()
            sends.append(rc)
        for j, (mx, my, mc) in enumerate(masks):
            px, py, pc = _flip(x, mx), _flip(y, my), _flip(c, mc)
            pltpu.make_async_remote_copy(
                src_ref=in_ref, dst_ref=out_ref.at[4 * px + 2 * py + pc], send_sem=send_sems.at[j], recv_sem=recv_sems.at[j],
                device_id=(px, py, pc), device_id_type=MESH).wait_recv()
        for rc in sends:
            rc.wait_send()

    out = _pcall(
        body, name=name, in_specs=[ANY], out_specs=ANY, out_shape=SDS((8,) + a.shape, a.dtype),
        scratch_shapes=[pltpu.SemaphoreType.DMA((7,)), pltpu.SemaphoreType.DMA((7,))],
    )(a)
    return _own(out, a, 4 * lax.axis_index("x") + 2 * lax.axis_index("y") + lax.axis_index("c"))


def _swap_rows_c(arrs, name):
    n = len(arrs)

    def body(*refs):
        ins, outs = refs[:n], refs[n:2 * n]
        send_sems, recv_sems = refs[2 * n:]
        x, y, c = _coords()
        sends = []
        for k in range(n):
            h = arrs[k].shape[1] // 2
            rc = _remote(ins[k].at[pl.ds(0, 4), pl.ds((1 - c) * h, h)], outs[k], send_sems.at[k], recv_sems.at[k], (x, y, 1 - c))
            rc.start()
            sends.append(rc)
        for rc in sends:
            rc.wait()

    return _pcall(
        body, name=name, in_specs=[ANY] * n, out_specs=[ANY] * n,
        out_shape=[SDS((4, a.shape[1] // 2, a.shape[2]), a.dtype) for a in arrs],
        scratch_shapes=[pltpu.SemaphoreType.DMA((n,)), pltpu.SemaphoreType.DMA((n,))],
    )(*arrs)


def _a2a_xy(arrs, name):
    n = len(arrs)

    def body(*refs):
        ins, outs = refs[:n], refs[n:2 * n]
        send_sems, recv_sems = refs[2 * n:]
        x, y, c = _coords()
        me = 2 * x + y
        peers = [(1 - x, y), (x, 1 - y), (1 - x, 1 - y)]
        sends = []
        for k in range(n):
            for j, (px, py) in enumerate(peers):
                rc = _remote(ins[k].at[2 * px + py], outs[k].at[me], send_sems.at[3 * k + j], recv_sems.at[3 * k + j], (px, py, c))
                rc.start()
                sends.append(rc)
        for k in range(n):
            for j, (px, py) in enumerate(peers):
                landed = outs[k].at[2 * px + py]
                _remote(landed, landed, send_sems.at[3 * k + j], recv_sems.at[3 * k + j], (px, py, c)).wait_recv()
        for rc in sends:
            rc.wait_send()

    outs = _pcall(
        body, name=name, in_specs=[ANY] * n, out_specs=[ANY] * n, out_shape=[SDS(a.shape, a.dtype) for a in arrs],
        scratch_shapes=[pltpu.SemaphoreType.DMA((3 * n,)), pltpu.SemaphoreType.DMA((3 * n,))],
    )(*arrs)
    me_xy = 2 * lax.axis_index("x") + lax.axis_index("y")
    return [_own(o, lax.dynamic_index_in_dim(a, me_xy, axis=0, keepdims=False), me_xy) for o, a in zip(outs, arrs)]


def _join_halves(halves, name):
    n = len(halves)

    def body(*refs):
        ins, outs = refs[:n], refs[n:2 * n]
        send_sems, recv_sems = refs[2 * n:]
        x, y, c = _coords()
        sends = []
        for k in range(n):
            h = halves[k].shape[0]
            rc = _remote(ins[k], outs[k].at[pl.ds(c * h, h)], send_sems.at[k], recv_sems.at[k], (x, y, 1 - c))
            rc.start()
            sends.append(rc)
        for k in range(n):
            h = halves[k].shape[0]
            landed = outs[k].at[pl.ds((1 - c) * h, h)]
            _remote(landed, landed, send_sems.at[k], recv_sems.at[k], (x, y, 1 - c)).wait_recv()
        for rc in sends:
            rc.wait_send()

    outs = _pcall(
        body, name=name, in_specs=[ANY] * n, out_specs=[ANY] * n,
        out_shape=[SDS((2 * a.shape[0], a.shape[1]), a.dtype) for a in halves],
        scratch_shapes=[pltpu.SemaphoreType.DMA((n,)), pltpu.SemaphoreType.DMA((n,))],
    )(*halves)
    ci = lax.axis_index("c")
    return [lax.dynamic_update_slice_in_dim(o, a, ci * a.shape[0], axis=0) for o, a in zip(outs, halves)]


def _pair_sum(a, got, cidx, name):
    _, r, cdim = a.shape
    h = r // 2
    tr = min(h, 256)
    nj = h // tr

    def body(c_ref, a_ref, g_ref, o_ref):
        o_ref[...] = (a_ref[...] + g_ref[...]).astype(BF16)

    blk = pl.BlockSpec((1, tr, cdim), lambda s, j, c: (s, j, 0))
    return _pcall(
        body, name=name, out_shape=SDS((4, h, cdim), BF16),
        grid_spec=pltpu.PrefetchScalarGridSpec(
            num_scalar_prefetch=1, grid=(4, nj),
            in_specs=[pl.BlockSpec((1, tr, cdim), lambda s, j, c: (s, c[0] * nj + j, 0)), blk], out_specs=blk),
        compiler_params=_cp(("parallel", "parallel")),
    )(cidx, a, got)


def _sum_chips(parts, name):
    _, h, cdim = parts.shape
    tr = min(h, 256)

    def body(p_ref, o_ref):
        acc = p_ref[0].astype(F32)
        for k in range(1, 4):
            acc = acc + p_ref[k].astype(F32)
        o_ref[...] = acc

    return _pcall(
        body, name=name, grid=(h // tr,), in_specs=[pl.BlockSpec((4, tr, cdim), lambda i: (0, i, 0))],
        out_specs=pl.BlockSpec((tr, cdim), lambda i: (i, 0)), out_shape=SDS((h, cdim), F32), compiler_params=_cp(("parallel",)),
    )(parts)


def _sum_slots(a, name, rows):
    s, n, _ = a.shape

    def body(a_ref, o_ref):
        acc = a_ref[0]
        for k in range(1, s):
            acc = acc + a_ref[k]
        o_ref[...] = acc

    return _pcall(
        body, name=name, grid=(n // rows,), in_specs=[pl.BlockSpec((s, rows, 128), lambda i: (0, i, 0))],
        out_specs=pl.BlockSpec((rows, 128), lambda i: (i, 0)), out_shape=SDS((n, 128), F32),
        compiler_params=_cp(("parallel",)),
    )(a)


def _adamw(w, g, m, v, name, rows):
    r, cdim = w.shape

    def body(w_ref, g_ref, m_ref, v_ref, d_ref, nm_ref, nv_ref):
        g_ = g_ref[...]
        nm = ADAM_B1 * m_ref[...] + (1.0 - ADAM_B1) * g_
        nv = ADAM_B2 * v_ref[...] + (1.0 - ADAM_B2) * (g_ * g_)
        m_hat = nm / (1.0 - ADAM_B1 ** ADAM_STEP)
        v_hat = nv / (1.0 - ADAM_B2 ** ADAM_STEP)
        d_ref[...] = -ADAM_LR * (m_hat / (jnp.sqrt(v_hat) + ADAM_EPS) + ADAM_WD * w_ref[...])
        nm_ref[...] = nm
        nv_ref[...] = nv

    blk = pl.BlockSpec((rows, cdim), lambda i: (i, 0))
    return _pcall(
        body, name=name, grid=(r // rows,), in_specs=[blk] * 4, out_specs=[blk] * 3,
        out_shape=[SDS(w.shape, F32)] * 3, compiler_params=_cp(("parallel",)),
    )(w, g, m, v)


def _pack(pieces, rows):
    flat = jnp.concatenate([p.reshape(-1) for p in pieces])
    return jnp.pad(flat, (0, rows * 128 - flat.shape[0])).reshape(rows, 128)


def _unpack(buf, shapes):
    flat = buf.reshape(-1)
    out, off = [], 0
    for shp in shapes:
        size = 1
        for s in shp:
            size *= s
        out.append(flat[off:off + size].reshape(shp))
        off += size
    return out


def _perm_cols(w):
    perm = jnp.concatenate([w[..., 3104:5152], w[..., 0:1024], w[..., 1056:1568], w[..., 1568:2080], w[..., 2592:3104],
                            w[..., 2080:2592]], axis=-1)
    return perm, w[..., 1024:1056]


def _unperm_cols(perm, lr32):
    return jnp.concatenate([perm[..., 2048:3072], lr32, perm[..., 3072:3584], perm[..., 3584:4096], perm[..., 4608:5120],
                            perm[..., 4096:4608], perm[..., 0:2048]], axis=-1)


SMALL_ROWS = 672
HALF_ROWS = 7200


def kernel(x, c, ctx, c_ctx, w_mod, b_mod, norm_g, w_in, a_ln_g, a_ln_b, a_ws, a_bs, b_gate_w2, b_gate_b, b_norm_g, w_proj_a, w_proj_b, w_out, final_norm_g, loss_target, m_c_ctx, m_w_mod, m_b_mod, m_norm_g, m_w_in, m_a_ln_g, m_a_ln_b, m_a_ws, m_a_bs, m_b_gate_w2, m_b_gate_b, m_b_norm_g, m_w_proj_a, m_w_proj_b, m_w_out, m_final_norm_g, v_c_ctx, v_w_mod, v_b_mod, v_norm_g, v_w_in, v_a_ln_g, v_a_ln_b, v_a_ws, v_a_bs, v_b_gate_w2, v_b_gate_b, v_b_norm_g, v_w_proj_a, v_w_proj_b, v_w_out, v_final_norm_g):
    xi, yi, ci = _coords()
    me_xy = 2 * xi + yi

    gate_pack = _pack([b_gate_w2[0], b_gate_b[0]], 24)
    g_wi, g_wm, g_wpa, g_wpb, g_wo, g_gate = _gather_weights(
        [w_in[0].astype(BF16), w_mod[0].astype(BF16), w_proj_a[0].astype(BF16), w_proj_b[0].astype(BF16),
         w_out[0].astype(BF16)], [gate_pack], "gather_weights")
    wi_full = jnp.swapaxes(g_wi, 0, 1).reshape(D, 4 * 1288)
    wi, wlr32 = _perm_cols(wi_full)
    wlr = jnp.pad(wlr32, ((0, 0), (0, LRW - 32)))
    wm = jnp.swapaxes(g_wm, 0, 1).reshape(D, 3 * D)
    wpa = jnp.swapaxes(g_wpa, 0, 1).reshape(512, D)
    wpb = jnp.swapaxes(g_wpb, 0, 1).reshape(512, D)
    wo = g_wo.reshape(D, D)
    gflat = g_gate.reshape(4, 24 * 128)
    w2 = jnp.swapaxes(gflat[:, 0:2048].reshape(4, 2, 16, 64), 0, 2)
    w2 = jnp.swapaxes(w2, 0, 1).reshape(2, 16, 256)
    gb2 = jnp.swapaxes(gflat[:, 2048:2176].reshape(4, 2, 64), 0, 1).reshape(2, 256)

    r = _device_step(x[0], c, ctx[0], c_ctx[None], loss_target[0], wm, b_mod, norm_g, wi, wlr, a_ln_g, a_ln_b, a_ws[0], a_bs[0],
                     w2, gb2, b_norm_g, wpa, wpb, wo, final_norm_g[None])

    small = _pack([r["dmod"], c, r["dmodc"], r["dscc"], r["dng"], r["dlng"], r["dlnb"], r["dws"], r["dbs"], r["dgbn"], r["dgf"],
                   r["dw2"], r["dgb2"], jnp.broadcast_to(r["loss"], (128,))], SMALL_ROWS)
    small_all = _gather_all(small, "gather_small")
    small_sum = _sum_slots(small_all, "sum_small", SMALL_ROWS // 4)
    (s_dmod, _, s_dmodc, s_dscc, s_dng, s_dlng, s_dlnb, s_dws, s_dbs, s_dgbn, s_dgf, s_dw2, s_dgb2, s_loss) = _unpack(
        small_sum, [(1, 3 * D), (1, D), (1, 2 * D), (D,), (1, D), (1, 512), (1, 512), (1, 4, 128, 128), (1, 4, 128), (1, 512),
                    (D,), (2, 16, 256), (2, 256), (128,)])
    loss = s_loss[0]
    s_dmodc_p = jnp.pad(s_dmodc, ((0, 0), (0, D)))
    g_b_mod = s_dmod + s_dmodc_p
    sg = jax.nn.sigmoid(c_ctx)
    g_c_ctx = s_dscc * (sg * (1.0 + c_ctx * (1.0 - sg)))
    g_w2 = lax.dynamic_slice_in_dim(s_dw2, 64 * me_xy, 64, axis=2)[None]
    g_gb2 = lax.dynamic_slice_in_dim(s_dgb2, 64 * me_xy, 64, axis=1)[None]

    flat_all = small_all.reshape(8, SMALL_ROWS * 128)
    dmod_all = flat_all[:, 0:3 * D]
    c_all = flat_all[:, 3 * D:4 * D]
    lhs = jnp.concatenate([_silu(c_all), _silu(c_ctx)[None], jnp.zeros((7, D), F32)], axis=0)
    rhs = jnp.concatenate([dmod_all, s_dmodc_p, jnp.zeros((7, 3 * D), F32)], axis=0)
    rhs = lax.dynamic_slice_in_dim(rhs, 768 * me_xy, 768, axis=1)
    g_w_mod = _mm(lhs.T.astype(BF16), rhs.astype(BF16), tm=D, tn=768, tk=16, out_dtype=F32, name="mm_dwm")

    dwi = r["dwi"]
    dwi_full = jnp.concatenate([r["dwi_qkv"], r["dwlr"][:, 0:32], dwi[:, 3072:3584], dwi[:, 3584:4096], dwi[:, 4608:5120],
                                dwi[:, 4096:4608], dwi[:, 0:2048]], axis=1)
    big = [jnp.swapaxes(dwi_full.reshape(D, 4, 1288), 0, 1), jnp.swapaxes(r["dwpa"].reshape(512, 4, 256), 0, 1),
           jnp.swapaxes(r["dwpb"].reshape(512, 4, 256), 0, 1), r["dwo"].reshape(4, 256, D)]
    tags = ["wi", "wpa", "wpb", "wo"]
    got = _swap_rows_c(big, "swap_half_in")
    cidx = jnp.reshape(ci, (1,)).astype(jnp.int32)
    pair = [_pair_sum(a, g, cidx, "sum_pair_" + t) for a, g, t in zip(big, got, tags)]
    parts = _a2a_xy(pair, "a2a_grads")
    halves = [_sum_chips(p_, "sum_chips_" + t) for p_, t in zip(parts, tags)]
    g_w_in, g_wpa, g_wpb, g_wo = _join_halves(halves, "swap_half_out")

    d_w_in, nm_w_in, nv_w_in = _adamw(w_in[0], g_w_in, m_w_in[0], v_w_in[0], "adamw_w_in", 256)
    d_w_mod, nm_w_mod, nv_w_mod = _adamw(w_mod[0], g_w_mod, m_w_mod[0], v_w_mod[0], "adamw_w_mod", 256)
    d_wpa, nm_wpa, nv_wpa = _adamw(w_proj_a[0], g_wpa, m_w_proj_a[0], v_w_proj_a[0], "adamw_wpa", 256)
    d_wpb, nm_wpb, nv_wpb = _adamw(w_proj_b[0], g_wpb, m_w_proj_b[0], v_w_proj_b[0], "adamw_wpb", 256)
    d_wo, nm_wo, nv_wo = _adamw(w_out[0], g_wo, m_w_out[0], v_w_out[0], "adamw_wo", 256)

    names = ["c_ctx", "b_mod", "norm_g", "a_ln_g", "a_ln_b", "a_ws", "a_bs", "b_gate_w2", "b_gate_b", "b_norm_g", "final_norm_g"]
    ws_ = [c_ctx, b_mod, norm_g, a_ln_g, a_ln_b, a_ws, a_bs, b_gate_w2, b_gate_b, b_norm_g, final_norm_g]
    gs_ = [g_c_ctx, g_b_mod, s_dng, s_dlng, s_dlnb, s_dws, s_dbs, g_w2, g_gb2, s_dgbn, s_dgf]
    ms_ = [m_c_ctx, m_b_mod, m_norm_g, m_a_ln_g, m_a_ln_b, m_a_ws, m_a_bs, m_b_gate_w2, m_b_gate_b, m_b_norm_g, m_final_norm_g]
    vs_ = [v_c_ctx, v_b_mod, v_norm_g, v_a_ln_g, v_a_ln_b, v_a_ws, v_a_bs, v_b_gate_w2, v_b_gate_b, v_b_norm_g, v_final_norm_g]
    shapes = [w.shape for w in ws_]
    gs_ = [g.reshape(s) for g, s in zip(gs_, shapes)]
    d_s, nm_s, nv_s = _adamw(_pack(ws_, 600), _pack(gs_, 600), _pack(ms_, 600), _pack(vs_, 600), "adamw_small", 600)
    d_small = dict(zip(names, _unpack(d_s, shapes)))
    nm_small = dict(zip(names, _unpack(nm_s, shapes)))
    nv_small = dict(zip(names, _unpack(nv_s, shapes)))
    g_small = dict(zip(names, gs_))

    order = ["c_ctx", "w_mod", "b_mod", "norm_g", "w_in", "a_ln_g", "a_ln_b", "a_ws", "a_bs", "b_gate_w2", "b_gate_b", "b_norm_g",
             "w_proj_a", "w_proj_b", "w_out", "final_norm_g"]
    big_g = dict(w_mod=g_w_mod[None], w_in=g_w_in[None], w_proj_a=g_wpa[None], w_proj_b=g_wpb[None], w_out=g_wo[None])
    big_d = dict(w_mod=d_w_mod[None], w_in=d_w_in[None], w_proj_a=d_wpa[None], w_proj_b=d_wpb[None], w_out=d_wo[None])
    big_m = dict(w_mod=nm_w_mod[None], w_in=nm_w_in[None], w_proj_a=nm_wpa[None], w_proj_b=nm_wpb[None], w_out=nm_wo[None])
    big_v = dict(w_mod=nv_w_mod[None], w_in=nv_w_in[None], w_proj_a=nv_wpa[None], w_proj_b=nv_wpb[None], w_out=nv_wo[None])
    grads = [big_g[n] if n in big_g else g_small[n] for n in order]
    deltas = [big_d[n] if n in big_d else d_small[n] for n in order]
    new_m = [big_m[n] if n in big_m else nm_small[n] for n in order]
    new_v = [big_v[n] if n in big_v else nv_small[n] for n in order]
    return (loss, r["dx"][None], *grads, *deltas, *new_m, *new_v)
```

```python
import functools

import jax
import jax.numpy as jnp
from jax import lax
from jax.experimental import pallas as pl
from jax.experimental.pallas import tpu as pltpu

F32 = jnp.float32
BF16 = jnp.bfloat16
SDS = jax.ShapeDtypeStruct

D = 1024
NP = 5120
LRW = 128
CH = 64
AC = 128
EPS = 1e-6
TOK = 256
GLA_TB = 256
VMEM_BIG = 48 * 1024 * 1024

ADAM_LR, ADAM_B1, ADAM_B2, ADAM_EPS, ADAM_WD, ADAM_STEP = 0.001, 0.9, 0.999, 1e-08, 0.01, 10

_pcall = pl.pallas_call
MESH = pl.DeviceIdType.MESH


def _cp(sem=None, vmem=None):
    kw = {}
    if sem is not None:
        kw["dimension_semantics"] = sem
    if vmem is not None:
        kw["vmem_limit_bytes"] = vmem
    return pltpu.CompilerParams(**kw)


def _silu(x):
    return x * jax.nn.sigmoid(x)


def _dsilu(x):
    s = jax.nn.sigmoid(x)
    return s * (1.0 + x * (1.0 - s))


def _logsig(x):
    return jnp.minimum(x, 0.0) - jnp.log1p(jnp.exp(-jnp.abs(x)))


def _nt(a, b):
    return lax.dot_general(a, b, (((1,), (1,)), ((), ())), preferred_element_type=F32)


def _tn(a, b):
    return lax.dot_general(a, b, (((0,), (0,)), ((), ())), preferred_element_type=F32)


def _nn(a, b):
    return jnp.dot(a, b, preferred_element_type=F32)


def _full(shape):
    return pl.BlockSpec(shape, lambda *_: (0,) * len(shape))


def _mm(a, b, *, tm, tn, tk, out_dtype, name, acc=None):
    m, k = a.shape
    k2, n = b.shape
    assert k == k2 and m % tm == 0 and n % tn == 0 and k % tk == 0, (a.shape, b.shape, tm, tn, tk)
    nk = k // tk
    has_acc = acc is not None

    def body(*refs):
        if has_acc:
            a_ref, b_ref, c_ref, o_ref, acc_ref = refs
        else:
            a_ref, b_ref, o_ref, acc_ref = refs
        kk = pl.program_id(2)
        part = _nn(a_ref[...].astype(BF16), b_ref[...].astype(BF16))

        @pl.when(kk == 0)
        def _():
            if has_acc:
                acc_ref[...] = c_ref[...] + part
            else:
                acc_ref[...] = part

        @pl.when(kk > 0)
        def _():
            acc_ref[...] += part

        @pl.when(kk == nk - 1)
        def _():
            o_ref[...] = acc_ref[...].astype(out_dtype)

    in_specs = [pl.BlockSpec((tm, tk), lambda i, j, kk: (i, kk)), pl.BlockSpec((tk, tn), lambda i, j, kk: (kk, j))]
    args = [a, b]
    if has_acc:
        in_specs.append(pl.BlockSpec((tm, tn), lambda i, j, kk: (i, j)))
        args.append(acc)
    return _pcall(
        body, name=name, grid=(m // tm, n // tn, nk), in_specs=in_specs,
        out_specs=pl.BlockSpec((tm, tn), lambda i, j, kk: (i, j)),
        out_shape=SDS((m, n), out_dtype), scratch_shapes=[pltpu.VMEM((tm, tn), F32)],
        compiler_params=_cp(("parallel", "parallel", "arbitrary"), VMEM_BIG),
    )(*args)


def _modvec(cc, wm, bm):
    def body(c_ref, w_ref, b_ref, o_ref):
        o_ref[...] = _nn(_silu(c_ref[...]).astype(BF16), w_ref[...]) + b_ref[...]

    return _pcall(body, name="modvec", out_shape=SDS((8, 3 * D), F32), compiler_params=_cp(None, VMEM_BIG))(cc, wm, bm)


def _dcctx(dmodc, wm):
    def body(d_ref, w_ref, o_ref):
        o_ref[...] = _nt(d_ref[...].astype(BF16), w_ref[...])

    return _pcall(
        body, name="dcctx", grid=(1,), in_specs=[_full((8, 2 * D)), pl.BlockSpec((D, 2 * D), lambda i: (0, 0))],
        out_specs=_full((8, D)), out_shape=SDS((8, D), F32), compiler_params=_cp(("arbitrary",), VMEM_BIG),
    )(dmodc, wm)


def _prep_h(x, ng, scale, shift, name):
    m = x.shape[0]

    def body(x_ref, g_ref, sc_ref, sh_ref, h_ref):
        xf = x_ref[...]
        r = lax.rsqrt(jnp.mean(xf * xf, axis=-1, keepdims=True) + EPS)
        y = (xf * r) * g_ref[...]
        h_ref[...] = (y * (1.0 + sc_ref[...]) + sh_ref[...]).astype(BF16)

    row = pl.BlockSpec((TOK, D), lambda i: (i, 0))
    return _pcall(
        body, name=name, grid=(m // TOK,), in_specs=[row, _full((1, D)), _full((1, D)), _full((1, D))],
        out_specs=row, out_shape=SDS((m, D), BF16), compiler_params=_cp(("parallel",)),
    )(x, ng, scale, shift)


def _prep_bwd(x, dh, dx1, ng, scale, name):
    m = x.shape[0]
    has_res = dx1 is not None

    def body(*refs):
        if has_res:
            x_ref, dh_ref, r_ref, g_ref, sc_ref, dx_ref, dg_ref, dsc_ref, dsh_ref = refs
        else:
            x_ref, dh_ref, g_ref, sc_ref, dx_ref, dg_ref, dsc_ref, dsh_ref = refs
        i = pl.program_id(0)

        @pl.when(i == 0)
        def _():
            dg_ref[...] = jnp.zeros_like(dg_ref)
            dsc_ref[...] = jnp.zeros_like(dsc_ref)
            dsh_ref[...] = jnp.zeros_like(dsh_ref)

        xf = x_ref[...]
        dh_ = dh_ref[...]
        r = lax.rsqrt(jnp.mean(xf * xf, axis=-1, keepdims=True) + EPS)
        xh = xf * r
        y = xh * g_ref[...]
        dsh_ref[...] += jnp.sum(dh_, axis=0, keepdims=True)
        dsc_ref[...] += jnp.sum(dh_ * y, axis=0, keepdims=True)
        dy = dh_ * (1.0 + sc_ref[...])
        dg_ref[...] += jnp.sum(dy * xh, axis=0, keepdims=True)
        dxh = dy * g_ref[...]
        dx = r * (dxh - xh * jnp.mean(dxh * xh, axis=-1, keepdims=True))
        if has_res:
            dx = dx + r_ref[...]
        dx_ref[...] = dx

    row = pl.BlockSpec((TOK, D), lambda i: (i, 0))
    vec = _full((1, D))
    in_specs = [row, row] + ([row] if has_res else []) + [vec, vec]
    args = [x, dh] + ([dx1] if has_res else []) + [ng, scale]
    return _pcall(
        body, name=name, grid=(m // TOK,), in_specs=in_specs, out_specs=[row, vec, vec, vec],
        out_shape=[SDS((m, D), F32), SDS((1, D), F32), SDS((1, D), F32), SDS((1, D), F32)],
        compiler_params=_cp(("arbitrary",)),
    )(*args)


def _ln_fwd(p, ln_g, ln_b):
    m = p.shape[0]

    def body(va_ref, g_ref, b_ref, vr_ref, vc_ref):
        xf = va_ref[...]
        xc = xf - jnp.mean(xf, axis=-1, keepdims=True)
        y = xc * lax.rsqrt(jnp.mean(xc * xc, axis=-1, keepdims=True) + EPS)
        vn = y * g_ref[...] + b_ref[...]
        vr_ref[...] = vn[:, 0:256].astype(BF16)
        vc_ref[0] = vn[:, 256:384].astype(BF16)
        vc_ref[1] = vn[:, 384:512].astype(BF16)

    return _pcall(
        body, name="ln_fwd", grid=(m // TOK,),
        in_specs=[pl.BlockSpec((TOK, 512), lambda i: (i, 9)), _full((1, 512)), _full((1, 512))],
        out_specs=[pl.BlockSpec((TOK, 256), lambda i: (i, 0)), pl.BlockSpec((2, TOK, 128), lambda i: (0, i, 0))],
        out_shape=[SDS((m, 256), BF16), SDS((2, m, 128), BF16)], compiler_params=_cp(("parallel",)),
    )(p, ln_g, ln_b)


COLB = 2048


def _colmix_fwd(vnc, ws23, bs23):
    rows = vnc.shape[2] // COLB

    def body(v_ref, w_ref, b_ref, o_ref):
        o_ref[0] = _nn(w_ref[0], v_ref[0]) + b_ref[0]

    return _pcall(
        body, name="colmix_fwd", grid=(2, rows),
        in_specs=[pl.BlockSpec((1, AC, COLB), lambda g, j: (g, 0, j)), pl.BlockSpec((1, AC, AC), lambda g, j: (g, 0, 0)),
                  pl.BlockSpec((1, AC, 1), lambda g, j: (g, 0, 0))],
        out_specs=pl.BlockSpec((1, AC, COLB), lambda g, j: (g, 0, j)),
        out_shape=SDS(vnc.shape, F32), compiler_params=_cp(("parallel", "parallel")),
    )(vnc, ws23, bs23)


def _colmix_bwd(dsvc, vnc, ws23t):
    rows = vnc.shape[2] // COLB

    def body(d_ref, v_ref, wt_ref, dv_ref, dw_ref, db_ref):
        j = pl.program_id(1)

        @pl.when(j == 0)
        def _():
            dw_ref[...] = jnp.zeros_like(dw_ref)
            db_ref[...] = jnp.zeros_like(db_ref)

        d = d_ref[0]
        d16 = d.astype(BF16)
        dv_ref[0] = _nn(wt_ref[0], d16)
        dw_ref[0] += _nt(d16, v_ref[0])
        db_ref[0] += jnp.sum(d, axis=1, keepdims=True)

    blk = pl.BlockSpec((1, AC, COLB), lambda g, j: (g, 0, j))
    return _pcall(
        body, name="colmix_bwd", grid=(2, rows),
        in_specs=[blk, blk, pl.BlockSpec((1, AC, AC), lambda g, j: (g, 0, 0))],
        out_specs=[blk, pl.BlockSpec((1, AC, AC), lambda g, j: (g, 0, 0)), pl.BlockSpec((1, AC, 1), lambda g, j: (g, 0, 0))],
        out_shape=[SDS(vnc.shape, F32), SDS((2, AC, AC), F32), SDS((2, AC, 1), F32)],
        compiler_params=_cp(("parallel", "arbitrary")),
    )(dsvc, vnc, ws23t)


def _head_norm(o, gbn):
    out = []
    for h in range(4):
        oh = o[:, 128 * h:128 * h + 128]
        r = lax.rsqrt(jnp.mean(oh * oh, axis=-1, keepdims=True) + EPS)
        out.append((r, oh * r))
    return out


def _mid_fwd(o_f, o_b, p, vnr, svc, ws01, bs01, gbn):
    m = p.shape[0]

    def body(of_ref, ob_ref, zb_ref, ua_ref, za_ref, vnr_ref, svc_ref, w_ref, b_ref, g_ref, ya_ref, yb_ref, svr_ref):
        o = of_ref[...] + ob_ref[...]
        zb = zb_ref[...]
        parts = []
        for h, (r, xh) in enumerate(_head_norm(o, None)):
            parts.append(xh * g_ref[:, 128 * h:128 * h + 128])
        on = jnp.concatenate(parts, axis=1)
        yb_ref[...] = (on * _silu(zb)).astype(BF16)
        for j in range(TOK // AC):
            for g in range(2):
                sv = _nn(w_ref[g], vnr_ref[AC * j:AC * j + AC, AC * g:AC * g + AC]) + b_ref[g]
                svr_ref[AC * j:AC * j + AC, AC * g:AC * g + AC] = sv
        sz = _silu(za_ref[...])
        u = ua_ref[...]
        ya_ref[:, 0:256] = ((u[:, 0:256] * svr_ref[...]) * sz[:, 0:256]).astype(BF16)
        ya_ref[:, 256:384] = ((u[:, 256:384] * svc_ref[0]) * sz[:, 256:384]).astype(BF16)
        ya_ref[:, 384:512] = ((u[:, 384:512] * svc_ref[1]) * sz[:, 384:512]).astype(BF16)

    r512 = pl.BlockSpec((TOK, 512), lambda i: (i, 0))
    return _pcall(
        body, name="mid_fwd", grid=(m // TOK,),
        in_specs=[r512, r512, pl.BlockSpec((TOK, 512), lambda i: (i, 6)), pl.BlockSpec((TOK, 512), lambda i: (i, 7)),
                  pl.BlockSpec((TOK, 512), lambda i: (i, 8)), pl.BlockSpec((TOK, 256), lambda i: (i, 0)),
                  pl.BlockSpec((2, TOK, 128), lambda i: (0, i, 0)), _full((2, AC, AC)), _full((2, AC, 1)), _full((1, 512))],
        out_specs=[r512, r512, pl.BlockSpec((TOK, 256), lambda i: (i, 0))],
        out_shape=[SDS((m, 512), BF16), SDS((m, 512), BF16), SDS((m, 256), F32)],
        compiler_params=_cp(("parallel",)),
    )(o_f, o_b, p, p, p, vnr, svc, ws01, bs01, gbn)


def _merge_fwd(p, ya, yb):
    m = p.shape[0]

    def body(ga_ref, gb_ref, ya_ref, yb_ref, m_ref):
        m_ref[...] = (jax.nn.sigmoid(ga_ref[...]) * ya_ref[...] + jax.nn.sigmoid(gb_ref[...]) * yb_ref[...]).astype(BF16)

    row = pl.BlockSpec((TOK, D), lambda i: (i, 0))
    return _pcall(
        body, name="merge_fwd", grid=(m // TOK,),
        in_specs=[row, pl.BlockSpec((TOK, D), lambda i: (i, 1)), row, row], out_specs=row,
        out_shape=SDS((m, D), BF16), compiler_params=_cp(("parallel",)),
    )(p, p, ya, yb)


def _loss_head(x, out, tgt, gate, gf):
    m = x.shape[0]

    def body(x_ref, o_ref, t_ref, gate_ref, gf_ref, dx1_ref, dout_ref, loss_ref, dgate_ref, dgf_ref):
        i = pl.program_id(0)

        @pl.when(i == 0)
        def _():
            loss_ref[...] = jnp.zeros_like(loss_ref)
            dgate_ref[...] = jnp.zeros_like(dgate_ref)
            dgf_ref[...] = jnp.zeros_like(dgf_ref)

        out_ = o_ref[...]
        x1 = x_ref[...] + gate_ref[...] * out_
        r = lax.rsqrt(jnp.mean(x1 * x1, axis=-1, keepdims=True) + EPS)
        xh = x1 * r
        err = xh * gf_ref[...] - t_ref[...]
        loss_ref[...] += 0.5 * jnp.sum(jnp.mean(err * err, axis=-1, keepdims=True), axis=0, keepdims=True)
        dy = err * (1.0 / D)
        dgf_ref[...] += jnp.sum(dy * xh, axis=0, keepdims=True)
        dxh = dy * gf_ref[...]
        dx1 = r * (dxh - xh * jnp.mean(dxh * xh, axis=-1, keepdims=True))
        dx1_ref[...] = dx1
        dout_ref[...] = (gate_ref[...] * dx1).astype(BF16)
        dgate_ref[...] += jnp.sum(dx1 * out_, axis=0, keepdims=True)

    row = pl.BlockSpec((TOK, D), lambda i: (i, 0))
    vec = _full((1, D))
    return _pcall(
        body, name="loss_head", grid=(m // TOK,), in_specs=[row, row, row, vec, vec],
        out_specs=[row, row, _full((1, 128)), vec, vec],
        out_shape=[SDS((m, D), F32), SDS((m, D), BF16), SDS((1, 128), F32), SDS((1, D), F32), SDS((1, D), F32)],
        compiler_params=_cp(("arbitrary",)),
    )(x, out, tgt, gate, gf)


def _merge_bwd(dm, ya, yb, p):
    m = p.shape[0]

    def body(dm_ref, ya_ref, yb_ref, ga_ref, gb_ref, dya_ref, dyb_ref, dp_ref):
        dm_ = dm_ref[...]
        sa = jax.nn.sigmoid(ga_ref[...])
        sb = jax.nn.sigmoid(gb_ref[...])
        dya_ref[...] = (dm_ * sa).astype(BF16)
        dyb_ref[...] = (dm_ * sb).astype(BF16)
        dp_ref[:, 0:D] = (dm_ * ya_ref[...] * (sa * (1.0 - sa))).astype(BF16)
        dp_ref[:, D:2 * D] = (dm_ * yb_ref[...] * (sb * (1.0 - sb))).astype(BF16)

    row = pl.BlockSpec((TOK, D), lambda i: (i, 0))
    return _pcall(
        body, name="merge_bwd", grid=(m // TOK,),
        in_specs=[row, row, row, row, pl.BlockSpec((TOK, D), lambda i: (i, 1))],
        out_specs=[row, row, pl.BlockSpec((TOK, 2 * D), lambda i: (i, 0))],
        out_shape=[SDS((m, D), BF16), SDS((m, D), BF16), SDS((m, NP), BF16)],
        compiler_params=_cp(("parallel",)),
    )(dm, ya, yb, p, p)


def _mid_bwd(dya_in, dyb_in, p, svr, svc, o_f, o_b, gbn, dp):
    m = p.shape[0]

    def body(dya_ref, dyb_ref, zb_ref, ua_ref, za_ref, svr_ref, svc_ref, of_ref, ob_ref, g_ref, dpi_ref,
             dp_ref, dsr_ref, dsc_ref, do_ref, dg_ref):
        i = pl.program_id(0)

        @pl.when(i == 0)
        def _():
            dg_ref[...] = jnp.zeros_like(dg_ref)

        dya = dya_ref[...]
        u = ua_ref[...]
        za = za_ref[...]
        sz = _silu(za)
        sv = jnp.concatenate([svr_ref[...], svc_ref[0], svc_ref[1]], axis=1)
        dp_ref[:, 512:1024] = (dya * sv * sz).astype(BF16)
        dsv = dya * u * sz
        dsr_ref[...] = dsv[:, 0:256]
        dsc_ref[0] = dsv[:, 256:384]
        dsc_ref[1] = dsv[:, 384:512]
        dp_ref[:, 1024:1536] = (dya * u * sv * _dsilu(za)).astype(BF16)

        dyb = dyb_ref[...]
        zb = zb_ref[...]
        o = of_ref[...] + ob_ref[...]
        szb = _silu(zb)
        dszb = _dsilu(zb)
        for h, (r, xh) in enumerate(_head_norm(o, None)):
            sl = slice(128 * h, 128 * h + 128)
            gh = g_ref[:, sl]
            don = dyb[:, sl] * szb[:, sl]
            dp_ref[:, sl] = (dyb[:, sl] * (xh * gh) * dszb[:, sl]).astype(BF16)
            dg_ref[:, sl] += jnp.sum(don * xh, axis=0, keepdims=True)
            dxh = don * gh
            do_ref[:, sl] = r * (dxh - xh * jnp.mean(dxh * xh, axis=-1, keepdims=True))

    r512 = pl.BlockSpec((TOK, 512), lambda i: (i, 0))
    return _pcall(
        body, name="mid_bwd", grid=(m // TOK,),
        in_specs=[r512, r512, pl.BlockSpec((TOK, 512), lambda i: (i, 6)), pl.BlockSpec((TOK, 512), lambda i: (i, 7)),
                  pl.BlockSpec((TOK, 512), lambda i: (i, 8)), pl.BlockSpec((TOK, 256), lambda i: (i, 0)),
                  pl.BlockSpec((2, TOK, 128), lambda i: (0, i, 0)), r512, r512, _full((1, 512)),
                  pl.BlockSpec(memory_space=pl.ANY)],
        out_specs=[pl.BlockSpec((TOK, 1536), lambda i: (i, 2)), pl.BlockSpec((TOK, 256), lambda i: (i, 0)),
                   pl.BlockSpec((2, TOK, 128), lambda i: (0, i, 0)), r512, _full((1, 512))],
        out_shape=[SDS((m, NP), BF16), SDS((m, 256), F32), SDS((2, m, 128), F32), SDS((m, 512), F32), SDS((1, 512), F32)],
        input_output_aliases={10: 0}, compiler_params=_cp(("arbitrary",)),
    )(dya_in, dyb_in, p, p, p, svr, svc, o_f, o_b, gbn, dp)


def _tail_fwd(o_f, o_b, p, vnr, svc, x, tgt, ws01, bs01, gbn, wpa, wpb, wo, gate, gf):
    m = p.shape[0]

    def body(of_ref, ob_ref, zb_ref, ua_ref, za_ref, ga_ref, gb_ref, vnr_ref, svc_ref, x_ref, t_ref, w_ref, b_ref, g_ref,
             wpa_ref, wpb_ref, wo_ref, gate_ref, gf_ref,
             ya_ref, yb_ref, svr_ref, m_ref, dx1_ref, dout_ref, loss_ref, dgate_ref, dgf_ref):
        i = pl.program_id(0)

        @pl.when(i == 0)
        def _():
            loss_ref[...] = jnp.zeros_like(loss_ref)
            dgate_ref[...] = jnp.zeros_like(dgate_ref)
            dgf_ref[...] = jnp.zeros_like(dgf_ref)

        o = of_ref[...] + ob_ref[...]
        zb = zb_ref[...]
        for h, (r, xh) in enumerate(_head_norm(o, None)):
            sl = slice(128 * h, 128 * h + 128)
            yb_ref[:, sl] = ((xh * g_ref[:, sl]) * _silu(zb[:, sl])).astype(BF16)
        for j in range(TOK // AC):
            for g in range(2):
                sv = _nn(w_ref[g], vnr_ref[AC * j:AC * j + AC, AC * g:AC * g + AC]) + b_ref[g]
                svr_ref[AC * j:AC * j + AC, AC * g:AC * g + AC] = sv
        sz = _silu(za_ref[...])
        u = ua_ref[...]
        ya_ref[:, 0:256] = ((u[:, 0:256] * svr_ref[...]) * sz[:, 0:256]).astype(BF16)
        ya_ref[:, 256:384] = ((u[:, 256:384] * svc_ref[0]) * sz[:, 256:384]).astype(BF16)
        ya_ref[:, 384:512] = ((u[:, 384:512] * svc_ref[1]) * sz[:, 384:512]).astype(BF16)
        ya = _nn(ya_ref[...], wpa_ref[...])
        yb = _nn(yb_ref[...], wpb_ref[...])
        mg = (jax.nn.sigmoid(ga_ref[...]) * ya + jax.nn.sigmoid(gb_ref[...]) * yb).astype(BF16)
        m_ref[...] = mg
        out_ = _nn(mg, wo_ref[...])
        x1 = x_ref[...] + gate_ref[...] * out_
        r = lax.rsqrt(jnp.mean(x1 * x1, axis=-1, keepdims=True) + EPS)
        xh = x1 * r
        err = xh * gf_ref[...] - t_ref[...]
        loss_ref[...] += 0.5 * jnp.sum(jnp.mean(err * err, axis=-1, keepdims=True), axis=0, keepdims=True)
        dy = err * (1.0 / D)
        dgf_ref[...] += jnp.sum(dy * xh, axis=0, keepdims=True)
        dxh = dy * gf_ref[...]
        dx1 = r * (dxh - xh * jnp.mean(dxh * xh, axis=-1, keepdims=True))
        dx1_ref[...] = dx1
        dout_ref[...] = (gate_ref[...] * dx1).astype(BF16)
        dgate_ref[...] += jnp.sum(dx1 * out_, axis=0, keepdims=True)

    r512 = pl.BlockSpec((TOK, 512), lambda i: (i, 0))
    row = pl.BlockSpec((TOK, D), lambda i: (i, 0))
    vec = _full((1, D))
    return _pcall(
        body, name="tail_fwd", grid=(m // TOK,),
        in_specs=[r512, r512, pl.BlockSpec((TOK, 512), lambda i: (i, 6)), pl.BlockSpec((TOK, 512), lambda i: (i, 7)),
                  pl.BlockSpec((TOK, 512), lambda i: (i, 8)), row, pl.BlockSpec((TOK, D), lambda i: (i, 1)),
                  pl.BlockSpec((TOK, 256), lambda i: (i, 0)), pl.BlockSpec((2, TOK, 128), lambda i: (0, i, 0)), row, row,
                  _full((2, AC, AC)), _full((2, AC, 1)), _full((1, 512)), _full((512, D)), _full((512, D)), _full((D, D)), vec, vec],
        out_specs=[r512, r512, pl.BlockSpec((TOK, 256), lambda i: (i, 0)), row, row, row, _full((1, 128)), vec, vec],
        out_shape=[SDS((m, 512), BF16), SDS((m, 512), BF16), SDS((m, 256), F32), SDS((m, D), BF16), SDS((m, D), F32),
                   SDS((m, D), BF16), SDS((1, 128), F32), SDS((1, D), F32), SDS((1, D), F32)],
        compiler_params=_cp(("arbitrary",), VMEM_BIG),
    )(o_f, o_b, p, p, p, p, p, vnr, svc, x, tgt, ws01, bs01, gbn, wpa, wpb, wo, gate, gf)


DPR = 3072


def _tail_bwd(dout, ya_in, yb_in, p, svr, svc, o_f, o_b, gbn, wot, wpa, wpb, wpat, wpbt):
    m = p.shape[0]

    def body(dout_ref, ya_ref, yb_ref, ga_ref, gb_ref, zb_ref, ua_ref, za_ref, svr_ref, svc_ref, of_ref, ob_ref, g_ref,
             wot_ref, wpa_ref, wpb_ref, wpat_ref, wpbt_ref,
             dya_ref, dyb_ref, dpg_ref, dpr_ref, dsr_ref, dsc_ref, do_ref, dg_ref):
        i = pl.program_id(0)

        @pl.when(i == 0)
        def _():
            dg_ref[...] = jnp.zeros_like(dg_ref)

        dm_ = _nn(dout_ref[...], wot_ref[...])
        ya = _nn(ya_ref[...], wpa_ref[...])
        yb = _nn(yb_ref[...], wpb_ref[...])
        sa = jax.nn.sigmoid(ga_ref[...])
        sb = jax.nn.sigmoid(gb_ref[...])
        dya16 = (dm_ * sa).astype(BF16)
        dyb16 = (dm_ * sb).astype(BF16)
        dya_ref[...] = dya16
        dyb_ref[...] = dyb16
        dpg_ref[:, 0:D] = (dm_ * ya * (sa * (1.0 - sa))).astype(BF16)
        dpg_ref[:, D:2 * D] = (dm_ * yb * (sb * (1.0 - sb))).astype(BF16)
        dya = _nn(dya16, wpat_ref[...])
        dyb = _nn(dyb16, wpbt_ref[...])

        u = ua_ref[...]
        za = za_ref[...]
        sz = _silu(za)
        sv = jnp.concatenate([svr_ref[...], svc_ref[0], svc_ref[1]], axis=1)
        dpr_ref[:, 512:1024] = (dya * sv * sz).astype(BF16)
        dsv = dya * u * sz
        dsr_ref[...] = dsv[:, 0:256]
        dsc_ref[0] = dsv[:, 256:384]
        dsc_ref[1] = dsv[:, 384:512]
        dpr_ref[:, 1024:1536] = (dya * u * sv * _dsilu(za)).astype(BF16)

        zb = zb_ref[...]
        o = of_ref[...] + ob_ref[...]
        szb = _silu(zb)
        dszb = _dsilu(zb)
        for h, (r, xh) in enumerate(_head_norm(o, None)):
            sl = slice(128 * h, 128 * h + 128)
            gh = g_ref[:, sl]
            don = dyb[:, sl] * szb[:, sl]
            dpr_ref[:, sl] = (dyb[:, sl] * (xh * gh) * dszb[:, sl]).astype(BF16)
            dg_ref[:, sl] += jnp.sum(don * xh, axis=0, keepdims=True)
            dxh = don * gh
            do_ref[:, sl] = r * (dxh - xh * jnp.mean(dxh * xh, axis=-1, keepdims=True))

    r512 = pl.BlockSpec((TOK, 512), lambda i: (i, 0))
    row = pl.BlockSpec((TOK, D), lambda i: (i, 0))
    return _pcall(
        body, name="tail_bwd", grid=(m // TOK,),
        in_specs=[row, r512, r512, row, pl.BlockSpec((TOK, D), lambda i: (i, 1)), pl.BlockSpec((TOK, 512), lambda i: (i, 6)),
                  pl.BlockSpec((TOK, 512), lambda i: (i, 7)), pl.BlockSpec((TOK, 512), lambda i: (i, 8)),
                  pl.BlockSpec((TOK, 256), lambda i: (i, 0)), pl.BlockSpec((2, TOK, 128), lambda i: (0, i, 0)), r512, r512,
                  _full((1, 512)), _full((D, D)), _full((512, D)), _full((512, D)), _full((D, 512)), _full((D, 512))],
        out_specs=[row, row, pl.BlockSpec((TOK, 2 * D), lambda i: (i, 0)), pl.BlockSpec((TOK, 1536), lambda i: (i, 0)),
                   pl.BlockSpec((TOK, 256), lambda i: (i, 0)), pl.BlockSpec((2, TOK, 128), lambda i: (0, i, 0)), r512, _full((1, 512))],
        out_shape=[SDS((m, D), BF16), SDS((m, D), BF16), SDS((m, 2 * D), BF16), SDS((m, DPR), BF16), SDS((m, 256), F32),
                   SDS((2, m, 128), F32), SDS((m, 512), F32), SDS((1, 512), F32)],
        compiler_params=_cp(("arbitrary",), VMEM_BIG),
    )(dout, ya_in, yb_in, p, p, p, p, p, svr, svc, o_f, o_b, gbn, wot, wpa, wpb, wpat, wpbt)


def _mm_multi(pairs, *, tm, tn, out_dtype, name):
    m = pairs[0][0].shape[0]
    n = pairs[0][1].shape[1]
    nks = [a.shape[1] // tk for a, _, tk in pairs]
    starts = [sum(nks[:i]) for i in range(len(pairs))]
    total = sum(nks)

    def body(*refs):
        o_ref, acc_ref = refs[-2], refs[-1]
        kk = pl.program_id(2)
        for idx in range(len(pairs)):
            a_ref, b_ref = refs[2 * idx], refs[2 * idx + 1]

            @pl.when((kk >= starts[idx]) & (kk < starts[idx] + nks[idx]))
            def _(a_ref=a_ref, b_ref=b_ref, first=(idx == 0)):
                part = _nn(a_ref[...].astype(BF16), b_ref[...].astype(BF16))
                if first:
                    @pl.when(kk == 0)
                    def _():
                        acc_ref[...] = part

                    @pl.when(kk > 0)
                    def _():
                        acc_ref[...] += part
                else:
                    acc_ref[...] += part

        @pl.when(kk == total - 1)
        def _():
            o_ref[...] = acc_ref[...].astype(out_dtype)

    in_specs, args = [], []
    for (a, b, tk), st, nk in zip(pairs, starts, nks):
        in_specs.append(pl.BlockSpec((tm, tk), lambda i, j, kk, st=st, nk=nk: (i, jnp.clip(kk - st, 0, nk - 1))))
        in_specs.append(pl.BlockSpec((tk, tn), lambda i, j, kk, st=st, nk=nk: (jnp.clip(kk - st, 0, nk - 1), j)))
        args += [a, b]
    return _pcall(
        body, name=name, grid=(m // tm, n // tn, total), in_specs=in_specs,
        out_specs=pl.BlockSpec((tm, tn), lambda i, j, kk: (i, j)), out_shape=SDS((m, n), out_dtype),
        scratch_shapes=[pltpu.VMEM((tm, tn), F32)], compiler_params=_cp(("parallel", "parallel", "arbitrary"), VMEM_BIG),
    )(*args)


def _ln_bwd(dsr, vnr, dvnc, p, ws01t, ln_g, dp):
    m = p.shape[0]

    def body(dsr_ref, vnr_ref, dvc_ref, va_ref, wt_ref, g_ref, dpi_ref, dp_ref, dw_ref, db_ref, dlg_ref, dlb_ref, dvn_ref):
        i = pl.program_id(0)

        @pl.when(i == 0)
        def _():
            dw_ref[...] = jnp.zeros_like(dw_ref)
            db_ref[...] = jnp.zeros_like(db_ref)
            dlg_ref[...] = jnp.zeros_like(dlg_ref)
            dlb_ref[...] = jnp.zeros_like(dlb_ref)

        for j in range(TOK // AC):
            for g in range(2):
                d = dsr_ref[AC * j:AC * j + AC, AC * g:AC * g + AC]
                d16 = d.astype(BF16)
                dvn_ref[AC * j:AC * j + AC, AC * g:AC * g + AC] = _nn(wt_ref[g], d16)
                dw_ref[g] += _nt(d16, vnr_ref[AC * j:AC * j + AC, AC * g:AC * g + AC])
                db_ref[g] += jnp.sum(d, axis=1, keepdims=True)
        dvn_ref[:, 256:384] = dvc_ref[0]
        dvn_ref[:, 384:512] = dvc_ref[1]
        dvn = dvn_ref[...]
        xf = va_ref[...]
        xc = xf - jnp.mean(xf, axis=-1, keepdims=True)
        rs = lax.rsqrt(jnp.mean(xc * xc, axis=-1, keepdims=True) + EPS)
        xh = xc * rs
        dlg_ref[...] += jnp.sum(dvn * xh, axis=0, keepdims=True)
        dlb_ref[...] += jnp.sum(dvn, axis=0, keepdims=True)
        dxh = dvn * g_ref[...]
        dva = rs * (dxh - jnp.mean(dxh, axis=-1, keepdims=True) - xh * jnp.mean(dxh * xh, axis=-1, keepdims=True))
        dp_ref[...] = dva.astype(BF16)

    return _pcall(
        body, name="ln_bwd", grid=(m // TOK,),
        in_specs=[pl.BlockSpec((TOK, 256), lambda i: (i, 0)), pl.BlockSpec((TOK, 256), lambda i: (i, 0)),
                  pl.BlockSpec((2, TOK, 128), lambda i: (0, i, 0)), pl.BlockSpec((TOK, 512), lambda i: (i, 9)),
                  _full((2, AC, AC)), _full((1, 512)), pl.BlockSpec(memory_space=pl.ANY)],
        out_specs=[pl.BlockSpec((TOK, 512), lambda i: (i, 3)), _full((2, AC, AC)), _full((2, AC, 1)), _full((1, 512)), _full((1, 512))],
        out_shape=[SDS((m, DPR), BF16), SDS((2, AC, AC), F32), SDS((2, AC, 1), F32), SDS((1, 512), F32), SDS((1, 512), F32)],
        scratch_shapes=[pltpu.VMEM((TOK, 512), F32)],
        input_output_aliases={6: 0}, compiler_params=_cp(("arbitrary",)),
    )(dsr, vnr, dvnc, p, ws01t, ln_g, dp)


def _tri_mm(tri, a):
    a1 = a.astype(BF16)
    r1 = a - a1.astype(F32)
    a2 = r1.astype(BF16)
    a3 = (r1 - a2.astype(F32)).astype(BF16)
    n = a.shape[1]
    r = _nn(tri, jnp.concatenate([a1, a2, a3], axis=1))
    return r[:, 0:n] + r[:, n:2 * n] + r[:, 2 * n:3 * n]


def _gla_masks(reverse):
    ri = lax.broadcasted_iota(jnp.int32, (CH, CH), 0)
    ci = lax.broadcasted_iota(jnp.int32, (CH, CH), 1)
    vis = (ci >= ri) if reverse else (ci <= ri)
    vis_t = (ci <= ri) if reverse else (ci >= ri)
    r4 = lax.broadcasted_iota(jnp.int32, (4 * CH, CH), 0) & (CH - 1)
    c4 = lax.broadcasted_iota(jnp.int32, (4 * CH, CH), 1)
    vis4 = (c4 >= r4) if reverse else (c4 <= r4)
    vis4_t = (c4 <= r4) if reverse else (c4 >= r4)
    lane = lax.broadcasted_iota(jnp.int32, (1, 256), 1)
    hm = [(lane >= CH * h) & (lane < CH * h + CH) for h in range(4)]
    return vis, vis_t, vis4, vis4_t, hm


def _stack_heads(x, hm):
    return jnp.concatenate([jnp.where(hm[h], x, 0.0).astype(BF16) for h in range(4)], axis=0)


def _diag_heads(full, hm):
    r = full.shape[0] // 4
    acc = jnp.where(hm[0], full[0:r], 0.0)
    for h in range(1, 4):
        acc = acc + jnp.where(hm[h], full[r * h:r * h + r], 0.0)
    return acc


def _rows_of_heads(x):
    return jnp.concatenate([x[:, 128 * h:128 * h + 128] for h in range(4)], axis=0)


def _lane_vis(reverse, transpose):
    ri = lax.broadcasted_iota(jnp.int32, (CH, 4 * CH), 0)
    ci = lax.broadcasted_iota(jnp.int32, (CH, 4 * CH), 1) & (CH - 1)
    return (ci >= ri) if (reverse != transpose) else (ci <= ri)


def _gla_fwd(p, qkv_blk, lr, lrw, gbias, s0, *, reverse, name):
    m = p.shape[0]
    nb = m // GLA_TB
    nc = GLA_TB // CH
    rmap = (lambda i: nb - 1 - i) if reverse else (lambda i: i)

    def body(qkv_ref, lr_ref, lrw_ref, gb_ref, s0_ref, o_ref, sb_ref, sfin_ref, st_ref):
        i = pl.program_id(0)

        @pl.when(i == 0)
        def _():
            st_ref[...] = s0_ref[...]

        vis, _, vis4, _, hm = _gla_masks(reverse)
        tri = vis.astype(F32).astype(BF16)
        logits = _nn(lr_ref[...].astype(BF16), lrw_ref[...]) + gb_ref[...]
        a_all = _logsig(logits) * (1.0 / 16.0)
        st = st_ref[...]
        for c in (range(nc - 1, -1, -1) if reverse else range(nc)):
            rows = slice(CH * c, CH * c + CH)
            b = _tri_mm(tri, a_all[rows])
            bl = b[0:1] if reverse else b[CH - 1:CH]
            q = qkv_ref[rows, 0:256] * 0.125
            k = qkv_ref[rows, 256:512]
            v16 = qkv_ref[rows, 512:1024].astype(BF16)
            qd = q * jnp.exp(b)
            kd16 = (k * jnp.exp(-b)).astype(BF16)
            kdec16 = (k * jnp.exp(bl - b)).astype(BF16)
            qstack = _stack_heads(qd, hm)
            sc = jnp.where(vis4, _nt(qstack, kd16), 0.0).astype(BF16)
            inter = _nt(qstack, st.astype(BF16))
            for h in range(4):
                o_ref[rows, 128 * h:128 * h + 128] = (
                    _nn(sc[CH * h:CH * h + CH], v16[:, 128 * h:128 * h + 128]) + inter[CH * h:CH * h + CH])
            sb_ref[c] = st
            st = st * jnp.exp(bl) + _diag_heads(_tn(v16, kdec16), hm)
        st_ref[...] = st

        @pl.when(i == nb - 1)
        def _():
            sfin_ref[...] = st

    return _pcall(
        body, name=name, grid=(nb,),
        in_specs=[pl.BlockSpec((GLA_TB, 1024), lambda i: (rmap(i), qkv_blk)), pl.BlockSpec((GLA_TB, LRW), lambda i: (rmap(i), 0)),
                  _full((LRW, 256)), _full((1, 256)), _full((128, 256))],
        out_specs=[pl.BlockSpec((GLA_TB, 512), lambda i: (rmap(i), 0)), pl.BlockSpec((nc, 128, 256), lambda i: (rmap(i), 0, 0)),
                   _full((128, 256))],
        out_shape=[SDS((m, 512), F32), SDS((m // CH, 128, 256), F32), SDS((128, 256), F32)],
        scratch_shapes=[pltpu.VMEM((128, 256), F32)], compiler_params=_cp(("arbitrary",)),
    )(p, lr, lrw, gbias, s0)


def _gla_bwd(p, qkv_blk, lr, lrw, lrwt, gbias, sb, dsfin, do, prev, dp, *, reverse, name):
    m = p.shape[0]
    nb = m // GLA_TB
    nc = GLA_TB // CH
    rmap = (lambda i: i) if reverse else (lambda i: nb - 1 - i)
    has_prev = prev is not None
    has_dp = dp is not None

    def body(*refs):
        refs = list(refs)
        qkv_ref, lr_ref, lrw_ref, lrwt_ref, gb_ref, sb_ref, dsfin_ref, do_ref = refs[:8]
        refs = refs[8:]
        if has_prev:
            pq_ref, plr_ref = refs[:2]
            refs = refs[2:]
        if has_dp:
            refs = refs[1:]
        dqkv_ref, dlr_ref, dw2_ref, dgb_ref, ds0_ref, dst_ref, dlog_ref = refs
        i = pl.program_id(0)

        @pl.when(i == 0)
        def _():
            dst_ref[...] = dsfin_ref[...]
            dw2_ref[...] = jnp.zeros_like(dw2_ref)
            dgb_ref[...] = jnp.zeros_like(dgb_ref)

        vis, vis_t, vis4, vis4_t, hm = _gla_masks(reverse)
        tri = vis.astype(F32).astype(BF16)
        tri_t = vis_t.astype(F32).astype(BF16)
        lane_vis = _lane_vis(reverse, False)
        lane_vis_t = _lane_vis(reverse, True)
        lr16 = lr_ref[...].astype(BF16)
        logits = _nn(lr16, lrw_ref[...]) + gb_ref[...]
        a_all = _logsig(logits) * (1.0 / 16.0)
        dsig = (1.0 - jax.nn.sigmoid(logits)) * (1.0 / 16.0)
        dst = dst_ref[...]
        for c in (range(nc) if reverse else range(nc - 1, -1, -1)):
            rows = slice(CH * c, CH * c + CH)
            b = _tri_mm(tri, a_all[rows])
            bl = b[0:1] if reverse else b[CH - 1:CH]
            eb = jnp.exp(b)
            enb = jnp.exp(-b)
            ebl = jnp.exp(bl - b)
            el = jnp.exp(bl)
            q = qkv_ref[rows, 0:256] * 0.125
            k = qkv_ref[rows, 256:512]
            v16 = qkv_ref[rows, 512:1024].astype(BF16)
            do16 = do_ref[rows, :].astype(BF16)
            qd = q * eb
            kd = k * enb
            kdec = k * ebl
            st = sb_ref[c]
            st16 = st.astype(BF16)
            dst16 = dst.astype(BF16)
            qd16 = qd.astype(BF16)
            kd16 = kd.astype(BF16)
            qstack = _stack_heads(qd, hm)
            kstack = _stack_heads(kd, hm)
            kdecstack = _stack_heads(kdec, hm)
            pt = jnp.where(vis4_t, _nt(kstack, qd16), 0.0).astype(BF16)
            dvinter = _nt(kdecstack, dst16)
            do_rows = _rows_of_heads(do16)
            v_rows = _rows_of_heads(v16)
            dp_cat = jnp.where(lane_vis, _diag_heads(_nt(do_rows, v_rows), hm), 0.0).astype(BF16)
            dpt_cat = jnp.where(lane_vis_t, _diag_heads(_nt(v_rows, do_rows), hm), 0.0).astype(BF16)
            dqd = _nn(dp_cat, kstack) + _diag_heads(_nn(do_rows, st16), hm)
            dkd = _nn(dpt_cat, qstack)
            dkdec = _diag_heads(_nn(v_rows, dst16), hm)
            for h in range(4):
                rh = slice(CH * h, CH * h + CH)
                dv_h = _nn(pt[rh], do_rows[rh]) + dvinter[rh]
                if has_prev:
                    dv_h = dv_h + pq_ref[rows, 512 + 128 * h:512 + 128 * h + 128]
                dqkv_ref[rows, 512 + 128 * h:512 + 128 * h + 128] = dv_h.astype(dqkv_ref.dtype)
            dq = dqd * eb * 0.125
            dk = dkd * enb + dkdec * ebl
            if has_prev:
                dq = dq + pq_ref[rows, 0:256]
                dk = dk + pq_ref[rows, 256:512]
            dqkv_ref[rows, 0:256] = dq.astype(dqkv_ref.dtype)
            dqkv_ref[rows, 256:512] = dk.astype(dqkv_ref.dtype)
            g_kdec = dkdec * kdec
            db = dqd * qd - dkd * kd - g_kdec
            dbl = jnp.sum(g_kdec, axis=0, keepdims=True) + jnp.sum(st * dst, axis=0, keepdims=True) * el
            da = _tri_mm(tri_t, db) + dbl
            dlog_ref[rows, :] = da * dsig[rows]
            dst = dst * el + _diag_heads(_tn(do16, qd16), hm)
        dst_ref[...] = dst
        dlog = dlog_ref[...]
        dlog16 = dlog.astype(BF16)
        dlr = _nn(dlog16, lrwt_ref[...])
        if has_prev:
            dlr = dlr + plr_ref[...]
        dlr_ref[...] = dlr
        dw2_ref[...] += _tn(lr16, dlog16)
        dgb_ref[...] += jnp.sum(dlog, axis=0, keepdims=True)

        @pl.when(i == nb - 1)
        def _():
            ds0_ref[...] = dst

    in_specs = [pl.BlockSpec((GLA_TB, 1024), lambda i: (rmap(i), qkv_blk)), pl.BlockSpec((GLA_TB, LRW), lambda i: (rmap(i), 0)),
                _full((LRW, 256)), _full((256, LRW)), _full((1, 256)), pl.BlockSpec((nc, 128, 256), lambda i: (rmap(i), 0, 0)),
                _full((128, 256)), pl.BlockSpec((GLA_TB, 512), lambda i: (rmap(i), 0))]
    args = [p, lr, lrw, lrwt, gbias, sb, dsfin, do]
    if has_prev:
        in_specs += [pl.BlockSpec((GLA_TB, 1024), lambda i: (rmap(i), 0)), pl.BlockSpec((GLA_TB, LRW), lambda i: (rmap(i), 0))]
        args += list(prev)
    aliases = {}
    if has_dp:
        in_specs.append(pl.BlockSpec(memory_space=pl.ANY))
        aliases = {len(args): 0}
        args.append(dp)
        dq_spec = pl.BlockSpec((GLA_TB, 1024), lambda i: (rmap(i), 2))
        dq_shape = SDS(dp.shape, dp.dtype)
    else:
        dq_spec = pl.BlockSpec((GLA_TB, 1024), lambda i: (rmap(i), 0))
        dq_shape = SDS((m, 1024), F32)
    return _pcall(
        body, name=name, grid=(nb,), in_specs=in_specs,
        out_specs=[dq_spec, pl.BlockSpec((GLA_TB, LRW), lambda i: (rmap(i), 0)), _full((LRW, 256)), _full((1, 256)), _full((128, 256))],
        out_shape=[dq_shape, SDS((m, LRW), F32), SDS((LRW, 256), F32), SDS((1, 256), F32), SDS((128, 256), F32)],
        scratch_shapes=[pltpu.VMEM((128, 256), F32), pltpu.VMEM((GLA_TB, 256), F32)],
        input_output_aliases=aliases, compiler_params=_cp(("arbitrary",)),
    )(*args)


def _device_step(x, c, ctx, c_ctx, tgt, wm, bm, ng, wi, wlr, ln_g, ln_b, ws, bs, w2, gb2, gbn, wpa, wpb, wo, gf):
    L = x.shape[0]
    wit = wi.T
    wlrt = wlr.T
    ws16 = ws.astype(BF16)
    wst16 = jnp.swapaxes(ws, 1, 2).astype(BF16)
    bscol = bs[:, :, None]
    lrw = [jnp.zeros((LRW, 256), F32).at[16 * r:16 * r + 16].set(w2[r]).astype(BF16) for r in range(2)]
    lrwt = [w.T for w in lrw]
    gbias = [gb2[r:r + 1] for r in range(2)]

    cc = jnp.zeros((8, D), F32).at[0:1].set(c).at[1:2].set(c_ctx)
    mod = _modvec(cc, wm, bm)
    shift, scale, gate = mod[0:1, 0:D], mod[0:1, D:2 * D], mod[0:1, 2 * D:3 * D]
    shift_c, scale_c = mod[1:2, 0:D], mod[1:2, D:2 * D]

    hc = _prep_h(ctx, ng, scale_c, shift_c, "prep_hc")
    pc = _mm(hc, wi[:, 2048:3072], tm=256, tn=1024, tk=D, out_dtype=F32, name="mm_pc")
    plrc = _mm(hc, wlr, tm=256, tn=LRW, tk=D, out_dtype=F32, name="mm_plrc")
    zero_s = jnp.zeros((128, 256), F32)
    _, sbc_f, sc_f = _gla_fwd(pc, 0, plrc, lrw[0], gbias[0], zero_s, reverse=False, name="gla_fwd_cf")
    _, sbc_b, sc_b = _gla_fwd(pc, 0, plrc, lrw[1], gbias[1], zero_s, reverse=True, name="gla_fwd_cb")

    h = _prep_h(x, ng, scale, shift, "prep_h")
    p = _mm(h, wi, tm=512, tn=1024, tk=D, out_dtype=F32, name="mm_p")
    plr = _mm(h, wlr, tm=1024, tn=LRW, tk=D, out_dtype=F32, name="mm_plr")
    o_f, sb_f, _ = _gla_fwd(p, 2, plr, lrw[0], gbias[0], sc_f, reverse=False, name="gla_fwd_f")
    o_b, sb_b, _ = _gla_fwd(p, 2, plr, lrw[1], gbias[1], sc_b, reverse=True, name="gla_fwd_b")
    vnr, vnc = _ln_fwd(p, ln_g, ln_b)
    svc = _colmix_fwd(vnc.reshape(2, AC, L), ws16[2:4], bscol[2:4]).reshape(2, L, 128)
    ya_in, yb_in, svr, mrg, dx1, dout, loss, dgate, dgf = _tail_fwd(
        o_f, o_b, p, vnr, svc, x, tgt, ws16[0:2], bscol[0:2], gbn, wpa, wpb, wo, gate, gf)

    dya, dyb, dp_g, dp, dsr, dsc, do, dgbn = _tail_bwd(dout, ya_in, yb_in, p, svr, svc, o_f, o_b, gbn, wo.T, wpa, wpb, wpa.T, wpb.T)
    dwo = _mm(mrg.T, dout, tm=D, tn=D, tk=1024, out_dtype=F32, name="mm_dwo")
    dwpa = _mm(ya_in.T, dya, tm=512, tn=D, tk=1024, out_dtype=F32, name="mm_dwpa")
    dwpb = _mm(yb_in.T, dyb, tm=512, tn=D, tk=1024, out_dtype=F32, name="mm_dwpb")
    dvnc, dws23, dbs23 = _colmix_bwd(dsc.reshape(2, AC, L), vnc.reshape(2, AC, L), wst16[2:4])
    dp, dws01, dbs01, dlng, dlnb = _ln_bwd(dsr, vnr, dvnc.reshape(2, L, 128), p, wst16[0:2], ln_g, dp)
    zero_ds = jnp.zeros((128, 256), F32)
    dqkv_f, dlr_f, dw2_f, dgb_f, ds0_f = _gla_bwd(p, 2, plr, lrw[0], lrwt[0], gbias[0], sb_f, zero_ds, do, None, None,
                                                  reverse=False, name="gla_bwd_f")
    dp, dlr, dw2_b, dgb_b, ds0_b = _gla_bwd(p, 2, plr, lrw[1], lrwt[1], gbias[1], sb_b, zero_ds, do, (dqkv_f, dlr_f), dp,
                                            reverse=True, name="gla_bwd_b")
    zero_do = jnp.zeros((ctx.shape[0], 512), F32)
    dqkvc_f, dlrc_f, dw2c_f, dgbc_f, _ = _gla_bwd(pc, 0, plrc, lrw[0], lrwt[0], gbias[0], sbc_f, ds0_f, zero_do, None, None,
                                                  reverse=False, name="gla_bwd_cf")
    dqkvc, dlrc, dw2c_b, dgbc_b, _ = _gla_bwd(pc, 0, plrc, lrw[1], lrwt[1], gbias[1], sbc_b, ds0_b, zero_do,
                                              (dqkvc_f, dlrc_f), None, reverse=True, name="gla_bwd_cb")
    dhc = _mm(dqkvc, wit[2048:3072], tm=256, tn=D, tk=1024, out_dtype=F32, name="mm_dhc")
    dhc = _mm(dlrc, wlrt, tm=256, tn=D, tk=LRW, out_dtype=F32, name="mm_dhc_lr", acc=dhc)
    _, dng_c, dscale_c, dshift_c = _prep_bwd(ctx, dhc, None, ng, scale_c, "prep_bwd_c")

    wit_r = jnp.concatenate([wit[3072:5120], wit[2048:3072]], axis=0)
    dh = _mm_multi([(dp_g, wit[0:2048], 1024), (dp, wit_r, 1024), (dlr, wlrt, LRW)], tm=1024, tn=D, out_dtype=F32, name="mm_dh")
    ht = h.T
    hct = hc.T
    dwi_g = _mm(ht, dp_g, tm=D, tn=1024, tk=1024, out_dtype=F32, name="mm_dwi_g")
    dwi_r = _mm(ht, dp, tm=D, tn=1024, tk=1024, out_dtype=F32, name="mm_dwi_r")
    dwi_qkv = _mm(hct, dqkvc, tm=D, tn=1024, tk=256, out_dtype=F32, name="mm_dwi_c", acc=dwi_r[:, 2048:3072])
    dwlr = _mm(ht, dlr, tm=D, tn=LRW, tk=1024, out_dtype=F32, name="mm_dwlr")
    dwlr = _mm(hct, dlrc, tm=D, tn=LRW, tk=256, out_dtype=F32, name="mm_dwlr_c", acc=dwlr)
    dx, dng, dscale, dshift = _prep_bwd(x, dh, dx1, ng, scale, "prep_bwd")

    dmodc = jnp.concatenate([dshift_c, dscale_c], axis=1)
    dscc = _dcctx(jnp.zeros((8, 2 * D), F32).at[0:1].set(dmodc), wm)[0:1]
    dw2p = dw2_f + dw2c_f, dw2_b + dw2c_b
    return dict(
        loss=loss[0, 0], dx=dx, dwi_g=dwi_g, dwi_r=dwi_r, dwi_qkv=dwi_qkv, dwlr=dwlr, dwpa=dwpa, dwpb=dwpb, dwo=dwo,
        dmod=jnp.concatenate([dshift, dscale, dgate], axis=1), dmodc=dmodc, dscc=dscc, dng=dng + dng_c,
        dlng=dlng, dlnb=dlnb, dws=jnp.concatenate([dws01, dws23], axis=0),
        dbs=jnp.concatenate([dbs01, dbs23], axis=0)[:, :, 0], dgbn=dgbn, dgf=dgf,
        dw2=jnp.stack([dw2p[0][0:16], dw2p[1][16:32]]), dgb2=jnp.concatenate([dgb_f + dgbc_f, dgb_b + dgbc_b], axis=0),
    )


ANY = pl.BlockSpec(memory_space=pl.ANY)


def _coords():
    return lax.axis_index("x"), lax.axis_index("y"), lax.axis_index("c")


def _flip(v, bit):
    return 1 - v if bit else v


def _remote(src, dst, send_sem, recv_sem, dev):
    return pltpu.make_async_remote_copy(src_ref=src, dst_ref=dst, send_sem=send_sem, recv_sem=recv_sem,
                                        device_id=dev, device_id_type=MESH)


def _own(out, block, idx):
    return lax.dynamic_update_slice_in_dim(out, block[None], idx, axis=0)


def _gather_weights(split, whole, name):
    ns, nw = len(split), len(whole)
    n = ns + nw

    def body(*refs):
        ins, outs = refs[:n], refs[n:2 * n]
        a_send, a_recv, b_send, b_recv = refs[2 * n:]
        x, y, c = _coords()
        me = 2 * x + y
        sib = (x, y, 1 - c)
        peers = [(1 - x, y), (x, 1 - y), (1 - x, 1 - y)]
        sends = []
        for k in range(n):
            for j, (px, py) in enumerate(peers):
                if k < ns:
                    h = split[k].shape[0] // 2
                    rc = _remote(ins[k].at[pl.ds(c * h, h)], outs[k].at[me, pl.ds(c * h, h)], a_send.at[3 * k + j],
                                 a_recv.at[3 * k + j], (px, py, c))
                else:
                    rc = _remote(ins[k], outs[k].at[me], a_send.at[3 * k + j], a_recv.at[3 * k + j], (px, py, c))
                rc.start()
                sends.append(rc)
        for k in range(ns):
            h = split[k].shape[0] // 2
            for j, (px, py) in enumerate(peers):
                landed = outs[k].at[2 * px + py, pl.ds(c * h, h)]
                _remote(landed, landed, a_send.at[3 * k + j], a_recv.at[3 * k + j], (px, py, c)).wait_recv()
                fw = _remote(landed, landed, b_send.at[3 * k + j], b_recv.at[3 * k + j], sib)
                fw.start()
                sends.append(fw)
        for k in range(ns, n):
            for j, (px, py) in enumerate(peers):
                landed = outs[k].at[2 * px + py]
                _remote(landed, landed, a_send.at[3 * k + j], a_recv.at[3 * k + j], (px, py, c)).wait_recv()
        for k in range(ns):
            h = split[k].shape[0] // 2
            for j, (px, py) in enumerate(peers):
                passed = outs[k].at[2 * px + py, pl.ds((1 - c) * h, h)]
                _remote(passed, passed, b_send.at[3 * k + j], b_recv.at[3 * k + j], sib).wait_recv()
        for rc in sends:
            rc.wait_send()

    arrs = list(split) + list(whole)
    outs = _pcall(
        body, name=name, in_specs=[ANY] * n, out_specs=[ANY] * n,
        out_shape=[SDS((4,) + a.shape, a.dtype) for a in arrs],
        scratch_shapes=[pltpu.SemaphoreType.DMA((3 * n,)), pltpu.SemaphoreType.DMA((3 * n,)), pltpu.SemaphoreType.DMA((3 * ns,)),
                        pltpu.SemaphoreType.DMA((3 * ns,))],
    )(*arrs)
    me_xy = 2 * lax.axis_index("x") + lax.axis_index("y")
    return [_own(o, a, me_xy) for o, a in zip(outs, arrs)]


def _gather_all(a, name):
    masks = [(mx, my, mc) for mx in range(2) for my in range(2) for mc in range(2)][1:]

    def body(in_ref, out_ref, send_sems, recv_sems):
        x, y, c = _coords()
        me = 4 * x + 2 * y + c
        sends = []
        for j, (mx, my, mc) in enumerate(masks):
            peer = (_flip(x, mx), _flip(y, my), _flip(c, mc))
            rc = pltpu.make_async_remote_copy(
                src_ref=in_ref, dst_ref=out_ref.at[me], send_sem=send_sems.at[j], recv_sem=recv_sems.at[j],
                device_id=peer, device_id_type=MESH)
            rc.start()
            sends.append(rc)
        for j, (mx, my, mc) in enumerate(masks):
            px, py, pc = _flip(x, mx), _flip(y, my), _flip(c, mc)
            pltpu.make_async_remote_copy(
                src_ref=in_ref, dst_ref=out_ref.at[4 * px + 2 * py + pc], send_sem=send_sems.at[j], recv_sem=recv_sems.at[j],
                device_id=(px, py, pc), device_id_type=MESH).wait_recv()
        for rc in sends:
            rc.wait_send()

    out = _pcall(
        body, name=name, in_specs=[ANY], out_specs=ANY, out_shape=SDS((8,) + a.shape, a.dtype),
        scratch_shapes=[pltpu.SemaphoreType.DMA((7,)), pltpu.SemaphoreType.DMA((7,))],
    )(a)
    return _own(out, a, 4 * lax.axis_index("x") + 2 * lax.axis_index("y") + lax.axis_index("c"))


def _swap_rows_c(arrs, name):
    n = len(arrs)

    def body(*refs):
        ins, outs = refs[:n], refs[n:2 * n]
        send_sems, recv_sems = refs[2 * n:]
        x, y, c = _coords()
        sends = []
        for k in range(n):
            h = arrs[k].shape[1] // 2
            rc = _remote(ins[k].at[pl.ds(0, 4), pl.ds((1 - c) * h, h)], outs[k], send_sems.at[k], recv_sems.at[k], (x, y, 1 - c))
            rc.start()
            sends.append(rc)
        for rc in sends:
            rc.wait()

    return _pcall(
        body, name=name, in_specs=[ANY] * n, out_specs=[ANY] * n,
        out_shape=[SDS((4, a.shape[1] // 2, a.shape[2]), a.dtype) for a in arrs],
        scratch_shapes=[pltpu.SemaphoreType.DMA((n,)), pltpu.SemaphoreType.DMA((n,))],
    )(*arrs)


def _a2a_xy(arrs, name):
    n = len(arrs)

    def body(*refs):
        ins, outs = refs[:n], refs[n:2 * n]
        send_sems, recv_sems = refs[2 * n:]
        x, y, c = _coords()
        me = 2 * x + y
        peers = [(1 - x, y), (x, 1 - y), (1 - x, 1 - y)]
        sends = []
        for k in range(n):
            for j, (px, py) in enumerate(peers):
                rc = _remote(ins[k].at[2 * px + py], outs[k].at[me], send_sems.at[3 * k + j], recv_sems.at[3 * k + j], (px, py, c))
                rc.start()
                sends.append(rc)
        for k in range(n):
            for j, (px, py) in enumerate(peers):
                landed = outs[k].at[2 * px + py]
                _remote(landed, landed, send_sems.at[3 * k + j], recv_sems.at[3 * k + j], (px, py, c)).wait_recv()
        for rc in sends:
            rc.wait_send()

    outs = _pcall(
        body, name=name, in_specs=[ANY] * n, out_specs=[ANY] * n, out_shape=[SDS(a.shape, a.dtype) for a in arrs],
        scratch_shapes=[pltpu.SemaphoreType.DMA((3 * n,)), pltpu.SemaphoreType.DMA((3 * n,))],
    )(*arrs)
    me_xy = 2 * lax.axis_index("x") + lax.axis_index("y")
    return [_own(o, lax.dynamic_index_in_dim(a, me_xy, axis=0, keepdims=False), me_xy) for o, a in zip(outs, arrs)]


def _join_halves(halves, name):
    n = len(halves)

    def body(*refs):
        ins, outs = refs[:n], refs[n:2 * n]
        send_sems, recv_sems = refs[2 * n:]
        x, y, c = _coords()
        sends = []
        for k in range(n):
            h = halves[k].shape[0]
            rc = _remote(ins[k], outs[k].at[pl.ds(c * h, h)], send_sems.at[k], recv_sems.at[k], (x, y, 1 - c))
            rc.start()
            sends.append(rc)
        for k in range(n):
            h = halves[k].shape[0]
            landed = outs[k].at[pl.ds((1 - c) * h, h)]
            _remote(landed, landed, send_sems.at[k], recv_sems.at[k], (x, y, 1 - c)).wait_recv()
        for rc in sends:
            rc.wait_send()

    outs = _pcall(
        body, name=name, in_specs=[ANY] * n, out_specs=[ANY] * n,
        out_shape=[SDS((2 * a.shape[0], a.shape[1]), a.dtype) for a in halves],
        scratch_shapes=[pltpu.SemaphoreType.DMA((n,)), pltpu.SemaphoreType.DMA((n,))],
    )(*halves)
    ci = lax.axis_index("c")
    return [lax.dynamic_update_slice_in_dim(o, a, ci * a.shape[0], axis=0) for o, a in zip(outs, halves)]


def _pair_sum(a, got, cidx, name):
    _, r, cdim = a.shape
    h = r // 2
    tr = min(h, 256)
    nj = h // tr

    def body(c_ref, a_ref, g_ref, o_ref):
        o_ref[...] = (a_ref[...] + g_ref[...]).astype(BF16)

    blk = pl.BlockSpec((1, tr, cdim), lambda s, j, c: (s, j, 0))
    return _pcall(
        body, name=name, out_shape=SDS((4, h, cdim), BF16),
        grid_spec=pltpu.PrefetchScalarGridSpec(
            num_scalar_prefetch=1, grid=(4, nj),
            in_specs=[pl.BlockSpec((1, tr, cdim), lambda s, j, c: (s, c[0] * nj + j, 0)), blk], out_specs=blk),
        compiler_params=_cp(("parallel", "parallel")),
    )(cidx, a, got)


def _sum_chips(parts, name):
    _, h, cdim = parts.shape
    tr = min(h, 256)

    def body(p_ref, o_ref):
        acc = p_ref[0].astype(F32)
        for k in range(1, 4):
            acc = acc + p_ref[k].astype(F32)
        o_ref[...] = acc

    return _pcall(
        body, name=name, grid=(h // tr,), in_specs=[pl.BlockSpec((4, tr, cdim), lambda i: (0, i, 0))],
        out_specs=pl.BlockSpec((tr, cdim), lambda i: (i, 0)), out_shape=SDS((h, cdim), F32), compiler_params=_cp(("parallel",)),
    )(parts)


def _sum_slots(a, name, rows):
    s, n, _ = a.shape

    def body(a_ref, o_ref):
        acc = a_ref[0]
        for k in range(1, s):
            acc = acc + a_ref[k]
        o_ref[...] = acc

    return _pcall(
        body, name=name, grid=(n // rows,), in_specs=[pl.BlockSpec((s, rows, 128), lambda i: (0, i, 0))],
        out_specs=pl.BlockSpec((rows, 128), lambda i: (i, 0)), out_shape=SDS((n, 128), F32),
        compiler_params=_cp(("parallel",)),
    )(a)


def _adamw(w, g, m, v, name, rows):
    r, cdim = w.shape

    def body(w_ref, g_ref, m_ref, v_ref, d_ref, nm_ref, nv_ref):
        g_ = g_ref[...]
        nm = ADAM_B1 * m_ref[...] + (1.0 - ADAM_B1) * g_
        nv = ADAM_B2 * v_ref[...] + (1.0 - ADAM_B2) * (g_ * g_)
        m_hat = nm / (1.0 - ADAM_B1 ** ADAM_STEP)
        v_hat = nv / (1.0 - ADAM_B2 ** ADAM_STEP)
        d_ref[...] = -ADAM_LR * (m_hat / (jnp.sqrt(v_hat) + ADAM_EPS) + ADAM_WD * w_ref[...])
        nm_ref[...] = nm
        nv_ref[...] = nv

    blk = pl.BlockSpec((rows, cdim), lambda i: (i, 0))
    return _pcall(
        body, name=name, grid=(r // rows,), in_specs=[blk] * 4, out_specs=[blk] * 3,
        out_shape=[SDS(w.shape, F32)] * 3, compiler_params=_cp(("parallel",)),
    )(w, g, m, v)


def _pack(pieces, rows):
    flat = jnp.concatenate([p.reshape(-1) for p in pieces])
    return jnp.pad(flat, (0, rows * 128 - flat.shape[0])).reshape(rows, 128)


def _unpack(buf, shapes):
    flat = buf.reshape(-1)
    out, off = [], 0
    for shp in shapes:
        size = 1
        for s in shp:
            size *= s
        out.append(flat[off:off + size].reshape(shp))
        off += size
    return out


def _perm_cols(w):
    perm = jnp.concatenate([w[..., 3104:5152], w[..., 0:1024], w[..., 1056:1568], w[..., 1568:2080], w[..., 2592:3104],
                            w[..., 2080:2592]], axis=-1)
    return perm, w[..., 1024:1056]


def _unperm_cols(perm, lr32):
    return jnp.concatenate([perm[..., 2048:3072], lr32, perm[..., 3072:3584], perm[..., 3584:4096], perm[..., 4608:5120],
                            perm[..., 4096:4608], perm[..., 0:2048]], axis=-1)


SMALL_ROWS = 672
HALF_ROWS = 7200


def kernel(x, c, ctx, c_ctx, w_mod, b_mod, norm_g, w_in, a_ln_g, a_ln_b, a_ws, a_bs, b_gate_w2, b_gate_b, b_norm_g, w_proj_a, w_proj_b, w_out, final_norm_g, loss_target, m_c_ctx, m_w_mod, m_b_mod, m_norm_g, m_w_in, m_a_ln_g, m_a_ln_b, m_a_ws, m_a_bs, m_b_gate_w2, m_b_gate_b, m_b_norm_g, m_w_proj_a, m_w_proj_b, m_w_out, m_final_norm_g, v_c_ctx, v_w_mod, v_b_mod, v_norm_g, v_w_in, v_a_ln_g, v_a_ln_b, v_a_ws, v_a_bs, v_b_gate_w2, v_b_gate_b, v_b_norm_g, v_w_proj_a, v_w_proj_b, v_w_out, v_final_norm_g):
    xi, yi, ci = _coords()
    me_xy = 2 * xi + yi

    gate_pack = _pack([b_gate_w2[0], b_gate_b[0]], 24)
    g_wi, g_wm, g_wpa, g_wpb, g_wo, g_gate = _gather_weights(
        [w_in[0].astype(BF16), w_mod[0].astype(BF16), w_proj_a[0].astype(BF16), w_proj_b[0].astype(BF16),
         w_out[0].astype(BF16)], [gate_pack], "gather_weights")
    wi_full = jnp.swapaxes(g_wi, 0, 1).reshape(D, 4 * 1288)
    wi, wlr32 = _perm_cols(wi_full)
    wlr = jnp.pad(wlr32, ((0, 0), (0, LRW - 32)))
    wm = jnp.swapaxes(g_wm, 0, 1).reshape(D, 3 * D)
    wpa = jnp.swapaxes(g_wpa, 0, 1).reshape(512, D)
    wpb = jnp.swapaxes(g_wpb, 0, 1).reshape(512, D)
    wo = g_wo.reshape(D, D)
    gflat = g_gate.reshape(4, 24 * 128)
    w2 = jnp.swapaxes(gflat[:, 0:2048].reshape(4, 2, 16, 64), 0, 2)
    w2 = jnp.swapaxes(w2, 0, 1).reshape(2, 16, 256)
    gb2 = jnp.swapaxes(gflat[:, 2048:2176].reshape(4, 2, 64), 0, 1).reshape(2, 256)

    r = _device_step(x[0], c, ctx[0], c_ctx[None], loss_target[0], wm, b_mod, norm_g, wi, wlr, a_ln_g, a_ln_b, a_ws[0], a_bs[0],
                     w2, gb2, b_norm_g, wpa, wpb, wo, final_norm_g[None])

    small = _pack([r["dmod"], c, r["dmodc"], r["dscc"], r["dng"], r["dlng"], r["dlnb"], r["dws"], r["dbs"], r["dgbn"], r["dgf"],
                   r["dw2"], r["dgb2"], jnp.broadcast_to(r["loss"], (128,))], SMALL_ROWS)
    small_all = _gather_all(small, "gather_small")
    small_sum = _sum_slots(small_all, "sum_small", SMALL_ROWS // 4)
    (s_dmod, _, s_dmodc, s_dscc, s_dng, s_dlng, s_dlnb, s_dws, s_dbs, s_dgbn, s_dgf, s_dw2, s_dgb2, s_loss) = _unpack(
        small_sum, [(1, 3 * D), (1, D), (1, 2 * D), (D,), (1, D), (1, 512), (1, 512), (1, 4, 128, 128), (1, 4, 128), (1, 512),
                    (D,), (2, 16, 256), (2, 256), (128,)])
    loss = s_loss[0]
    s_dmodc_p = jnp.pad(s_dmodc, ((0, 0), (0, D)))
    g_b_mod = s_dmod + s_dmodc_p
    sg = jax.nn.sigmoid(c_ctx)
    g_c_ctx = s_dscc * (sg * (1.0 + c_ctx * (1.0 - sg)))
    g_w2 = lax.dynamic_slice_in_dim(s_dw2, 64 * me_xy, 64, axis=2)[None]
    g_gb2 = lax.dynamic_slice_in_dim(s_dgb2, 64 * me_xy, 64, axis=1)[None]

    flat_all = small_all.reshape(8, SMALL_ROWS * 128)
    dmod_all = flat_all[:, 0:3 * D]
    c_all = flat_all[:, 3 * D:4 * D]
    lhs = jnp.concatenate([_silu(c_all), _silu(c_ctx)[None], jnp.zeros((7, D), F32)], axis=0)
    rhs = jnp.concatenate([dmod_all, s_dmodc_p, jnp.zeros((7, 3 * D), F32)], axis=0)
    rhs = lax.dynamic_slice_in_dim(rhs, 768 * me_xy, 768, axis=1)
    g_w_mod = _mm(lhs.T.astype(BF16), rhs.astype(BF16), tm=D, tn=768, tk=16, out_dtype=F32, name="mm_dwm")

    dwr = r["dwi_r"]
    dwi_full = jnp.concatenate([r["dwi_qkv"], r["dwlr"][:, 0:32], dwr[:, 0:512], dwr[:, 512:1024], dwr[:, 1536:2048],
                                dwr[:, 1024:1536], r["dwi_g"]], axis=1)
    big = [jnp.swapaxes(dwi_full.reshape(D, 4, 1288), 0, 1), jnp.swapaxes(r["dwpa"].reshape(512, 4, 256), 0, 1),
           jnp.swapaxes(r["dwpb"].reshape(512, 4, 256), 0, 1), r["dwo"].reshape(4, 256, D)]
    tags = ["wi", "wpa", "wpb", "wo"]
    got = _swap_rows_c(big, "swap_half_in")
    cidx = jnp.reshape(ci, (1,)).astype(jnp.int32)
    pair = [_pair_sum(a, g, cidx, "sum_pair_" + t) for a, g, t in zip(big, got, tags)]
    parts = _a2a_xy(pair, "a2a_grads")
    halves = [_sum_chips(p_, "sum_chips_" + t) for p_, t in zip(parts, tags)]
    g_w_in, g_wpa, g_wpb, g_wo = _join_halves(halves, "swap_half_out")

    d_w_in, nm_w_in, nv_w_in = _adamw(w_in[0], g_w_in, m_w_in[0], v_w_in[0], "adamw_w_in", 256)
    d_w_mod, nm_w_mod, nv_w_mod = _adamw(w_mod[0], g_w_mod, m_w_mod[0], v_w_mod[0], "adamw_w_mod", 256)
    d_wpa, nm_wpa, nv_wpa = _adamw(w_proj_a[0], g_wpa, m_w_proj_a[0], v_w_proj_a[0], "adamw_wpa", 256)
    d_wpb, nm_wpb, nv_wpb = _adamw(w_proj_b[0], g_wpb, m_w_proj_b[0], v_w_proj_b[0], "adamw_wpb", 256)
    d_wo, nm_wo, nv_wo = _adamw(w_out[0], g_wo, m_w_out[0], v_w_out[0], "adamw_wo", 256)

    names = ["c_ctx", "b_mod", "norm_g", "a_ln_g", "a_ln_b", "a_ws", "a_bs", "b_gate_w2", "b_gate_b", "b_norm_g", "final_norm_g"]
    ws_ = [c_ctx, b_mod, norm_g, a_ln_g, a_ln_b, a_ws, a_bs, b_gate_w2, b_gate_b, b_norm_g, final_norm_g]
    gs_ = [g_c_ctx, g_b_mod, s_dng, s_dlng, s_dlnb, s_dws, s_dbs, g_w2, g_gb2, s_dgbn, s_dgf]
    ms_ = [m_c_ctx, m_b_mod, m_norm_g, m_a_ln_g, m_a_ln_b, m_a_ws, m_a_bs, m_b_gate_w2, m_b_gate_b, m_b_norm_g, m_final_norm_g]
    vs_ = [v_c_ctx, v_b_mod, v_norm_g, v_a_ln_g, v_a_ln_b, v_a_ws, v_a_bs, v_b_gate_w2, v_b_gate_b, v_b_norm_g, v_final_norm_g]
    shapes = [w.shape for w in ws_]
    gs_ = [g.reshape(s) for g, s in zip(gs_, shapes)]
    d_s, nm_s, nv_s = _adamw(_pack(ws_, 600), _pack(gs_, 600), _pack(ms_, 600), _pack(vs_, 600), "adamw_small", 600)
    d_small = dict(zip(names, _unpack(d_s, shapes)))
    nm_small = dict(zip(names, _unpack(nm_s, shapes)))
    nv_small = dict(zip(names, _unpack(nv_s, shapes)))
    g_small = dict(zip(names, gs_))

    order = ["c_ctx", "w_mod", "b_mod", "norm_g", "w_in", "a_ln_g", "a_ln_b", "a_ws", "a_bs", "b_gate_w2", "b_gate_b", "b_norm_g",
             "w_proj_a", "w_proj_b", "w_out", "final_norm_g"]
    big_g = dict(w_mod=g_w_mod[None], w_in=g_w_in[None], w_proj_a=g_wpa[None], w_proj_b=g_wpb[None], w_out=g_wo[None])
    big_d = dict(w_mod=d_w_mod[None], w_in=d_w_in[None], w_proj_a=d_wpa[None], w_proj_b=d_wpb[None], w_out=d_wo[None])
    big_m = dict(w_mod=nm_w_mod[None], w_in=nm_w_in[None], w_proj_a=nm_wpa[None], w_proj_b=nm_wpb[None], w_out=nm_wo[None])
    big_v = dict(w_mod=nv_w_mod[None], w_in=nv_w_in[None], w_proj_a=nv_wpa[None], w_proj_b=nv_wpb[None], w_out=nv_wo[None])
    grads = [big_g[n] if n in big_g else g_small[n] for n in order]
    deltas = [big_d[n] if n in big_d else d_small[n] for n in order]
    new_m = [big_m[n] if n in big_m else nm_small[n] for n in order]
    new_v = [big_v[n] if n in big_v else nv_small[n] for n in order]
    return (loss, r["dx"][None], *grads, *deltas, *new_m, *new_v)
```

```python
import functools

import jax
import jax.numpy as jnp
from jax import lax
from jax.experimental import pallas as pl
from jax.experimental.pallas import tpu as pltpu

F32 = jnp.float32
BF16 = jnp.bfloat16
SDS = jax.ShapeDtypeStruct

D = 1024
NP = 5120
LRW = 128
CH = 64
AC = 128
EPS = 1e-6
TOK = 256
GLA_TB = 256
VMEM_BIG = 48 * 1024 * 1024

ADAM_LR, ADAM_B1, ADAM_B2, ADAM_EPS, ADAM_WD, ADAM_STEP = 0.001, 0.9, 0.999, 1e-08, 0.01, 10

_pcall = pl.pallas_call
MESH = pl.DeviceIdType.MESH


def _cp(sem=None, vmem=None):
    kw = {}
    if sem is not None:
        kw["dimension_semantics"] = sem
    if vmem is not None:
        kw["vmem_limit_bytes"] = vmem
    return pltpu.CompilerParams(**kw)


def _silu(x):
    return x * jax.nn.sigmoid(x)


def _dsilu(x):
    s = jax.nn.sigmoid(x)
    return s * (1.0 + x * (1.0 - s))


def _logsig(x):
    return jnp.minimum(x, 0.0) - jnp.log1p(jnp.exp(-jnp.abs(x)))


def _nt(a, b):
    return lax.dot_general(a, b, (((1,), (1,)), ((), ())), preferred_element_type=F32)


def _tn(a, b):
    return lax.dot_general(a, b, (((0,), (0,)), ((), ())), preferred_element_type=F32)


def _nn(a, b):
    return jnp.dot(a, b, preferred_element_type=F32)


def _full(shape):
    return pl.BlockSpec(shape, lambda *_: (0,) * len(shape))


def _mm(a, b, *, tm, tn, tk, out_dtype, name, acc=None, n_outer=False):
    m, k = a.shape
    k2, n = b.shape
    assert k == k2 and m % tm == 0 and n % tn == 0 and k % tk == 0, (a.shape, b.shape, tm, tn, tk)
    nk = k // tk
    has_acc = acc is not None

    def body(*refs):
        if has_acc:
            a_ref, b_ref, c_ref, o_ref = refs[:4]
        else:
            a_ref, b_ref, o_ref = refs[:3]
        part = _nn(a_ref[...].astype(BF16), b_ref[...].astype(BF16))
        if nk == 1:
            o_ref[...] = ((c_ref[...] + part) if has_acc else part).astype(out_dtype)
            return
        acc_ref = refs[-1]
        kk = pl.program_id(2)

        @pl.when(kk == 0)
        def _():
            if has_acc:
                acc_ref[...] = c_ref[...] + part
            else:
                acc_ref[...] = part

        @pl.when(kk > 0)
        def _():
            acc_ref[...] += part

        @pl.when(kk == nk - 1)
        def _():
            o_ref[...] = acc_ref[...].astype(out_dtype)

    if n_outer:
        ij = lambda g0, g1: (g1, g0)
        grid = (n // tn, m // tm, nk)
    else:
        ij = lambda g0, g1: (g0, g1)
        grid = (m // tm, n // tn, nk)
    in_specs = [pl.BlockSpec((tm, tk), lambda g0, g1, kk: (ij(g0, g1)[0], kk)),
                pl.BlockSpec((tk, tn), lambda g0, g1, kk: (kk, ij(g0, g1)[1]))]
    args = [a, b]
    if has_acc:
        in_specs.append(pl.BlockSpec((tm, tn), lambda g0, g1, kk: ij(g0, g1)))
        args.append(acc)
    return _pcall(
        body, name=name, grid=grid, in_specs=in_specs,
        out_specs=pl.BlockSpec((tm, tn), lambda g0, g1, kk: ij(g0, g1)),
        out_shape=SDS((m, n), out_dtype), scratch_shapes=([pltpu.VMEM((tm, tn), F32)] if nk > 1 else []),
        compiler_params=_cp(("parallel", "parallel", "arbitrary"), VMEM_BIG),
    )(*args)


def _mm_tn(a, b, *, ta, tn, tk, name, acc=None):
    m, ka = a.shape
    m2, n = b.shape
    assert m == m2 and ka % ta == 0 and n % tn == 0 and m % tk == 0, (a.shape, b.shape, ta, tn, tk)
    nk = m // tk
    has_acc = acc is not None

    def body(*refs):
        if has_acc:
            a_ref, b_ref, c_ref, o_ref = refs
        else:
            a_ref, b_ref, o_ref = refs
        kk = pl.program_id(2)
        part = _tn(a_ref[...].astype(BF16), b_ref[...].astype(BF16))

        @pl.when(kk == 0)
        def _():
            if has_acc:
                o_ref[...] = c_ref[...] + part
            else:
                o_ref[...] = part

        @pl.when(kk > 0)
        def _():
            o_ref[...] += part

    in_specs = [pl.BlockSpec((tk, ta), lambda i, j, kk: (kk, i)), pl.BlockSpec((tk, tn), lambda i, j, kk: (kk, j))]
    args = [a, b]
    if has_acc:
        in_specs.append(pl.BlockSpec((ta, tn), lambda i, j, kk: (i, j)))
        args.append(acc)
    return _pcall(
        body, name=name, grid=(ka // ta, n // tn, nk), in_specs=in_specs,
        out_specs=pl.BlockSpec((ta, tn), lambda i, j, kk: (i, j)), out_shape=SDS((ka, n), F32),
        compiler_params=_cp(("parallel", "parallel", "arbitrary"), VMEM_BIG),
    )(*args)


def _modvec(cc, wm, bm):
    def body(c_ref, w_ref, b_ref, o_ref):
        o_ref[...] = _nn(_silu(c_ref[...]).astype(BF16), w_ref[...]) + b_ref[...]

    return _pcall(body, name="modvec", out_shape=SDS((8, 3 * D), F32), compiler_params=_cp(None, VMEM_BIG))(cc, wm, bm)


def _dcctx(dmodc, wm):
    def body(d_ref, w_ref, o_ref):
        o_ref[...] = _nt(d_ref[...].astype(BF16), w_ref[...])

    return _pcall(
        body, name="dcctx", grid=(1,), in_specs=[_full((8, 2 * D)), pl.BlockSpec((D, 2 * D), lambda i: (0, 0))],
        out_specs=_full((8, D)), out_shape=SDS((8, D), F32), compiler_params=_cp(("arbitrary",), VMEM_BIG),
    )(dmodc, wm)


def _prep_h(x, ng, scale, shift, name):
    m = x.shape[0]

    def body(x_ref, g_ref, sc_ref, sh_ref, h_ref):
        xf = x_ref[...]
        r = lax.rsqrt(jnp.mean(xf * xf, axis=-1, keepdims=True) + EPS)
        y = (xf * r) * g_ref[...]
        h_ref[...] = (y * (1.0 + sc_ref[...]) + sh_ref[...]).astype(BF16)

    row = pl.BlockSpec((TOK, D), lambda i: (i, 0))
    return _pcall(
        body, name=name, grid=(m // TOK,), in_specs=[row, _full((1, D)), _full((1, D)), _full((1, D))],
        out_specs=row, out_shape=SDS((m, D), BF16), compiler_params=_cp(("parallel",)),
    )(x, ng, scale, shift)


def _resident(shape):
    return pl.BlockSpec(shape, lambda *_: (0,) * len(shape), pipeline_mode=pl.Buffered(1))


PROJ_TM = 512


def _proj_fwd(x, ng, scale, shift, wi, wlr):
    m = x.shape[0]
    n = wi.shape[1]

    def body(x_ref, g_ref, sc_ref, sh_ref, wi_ref, wlr_ref, h_ref, p_ref, plr_ref):
        xf = x_ref[...]
        r = lax.rsqrt(jnp.mean(xf * xf, axis=-1, keepdims=True) + EPS)
        y = (xf * r) * g_ref[...]
        h = (y * (1.0 + sc_ref[...]) + sh_ref[...]).astype(BF16)
        h_ref[...] = h
        for j in range(n // D):
            p_ref[:, D * j:D * j + D] = _nn(h, wi_ref[:, D * j:D * j + D]).astype(BF16)
        plr_ref[...] = _nn(h, wlr_ref[...])

    row = pl.BlockSpec((PROJ_TM, D), lambda i: (i, 0))
    vec = _full((1, D))
    return _pcall(
        body, name="proj_fwd", grid=(m // PROJ_TM,),
        in_specs=[row, vec, vec, vec, _resident((D, n)), _resident((D, LRW))],
        out_specs=[row, pl.BlockSpec((PROJ_TM, n), lambda i: (i, 0)), pl.BlockSpec((PROJ_TM, LRW), lambda i: (i, 0))],
        out_shape=[SDS((m, D), BF16), SDS((m, n), BF16), SDS((m, LRW), F32)],
        compiler_params=_cp(("parallel",), VMEM_BIG),
    )(x, ng, scale, shift, wi, wlr)


def _proj_bwd(dp_g, dp_r, dlr, wit_g, wit_r, wlrt, x, dx1, ng, scale):
    m = x.shape[0]

    def body(dpg_ref, dpr_ref, dlr_ref, wg_ref, wr_ref, wl_ref, x_ref, r_ref, g_ref, sc_ref, dx_ref, dg_ref, dsc_ref, dsh_ref):
        i = pl.program_id(0)

        @pl.when(i == 0)
        def _():
            dg_ref[...] = jnp.zeros_like(dg_ref)
            dsc_ref[...] = jnp.zeros_like(dsc_ref)
            dsh_ref[...] = jnp.zeros_like(dsh_ref)

        dh_ = (_nn(dpg_ref[...], wg_ref[...]) + _nn(dpr_ref[...], wr_ref[...])
               + _nn(dlr_ref[...].astype(BF16), wl_ref[...]))
        xf = x_ref[...]
        r = lax.rsqrt(jnp.mean(xf * xf, axis=-1, keepdims=True) + EPS)
        xh = xf * r
        y = xh * g_ref[...]
        dsh_ref[...] += jnp.sum(dh_, axis=0, keepdims=True)
        dsc_ref[...] += jnp.sum(dh_ * y, axis=0, keepdims=True)
        dy = dh_ * (1.0 + sc_ref[...])
        dg_ref[...] += jnp.sum(dy * xh, axis=0, keepdims=True)
        dxh = dy * g_ref[...]
        dx_ref[...] = r * (dxh - xh * jnp.mean(dxh * xh, axis=-1, keepdims=True)) + r_ref[...]

    row = pl.BlockSpec((PROJ_TM, D), lambda i: (i, 0))
    vec = _full((1, D))
    kg, kr = dp_g.shape[1], dp_r.shape[1]
    return _pcall(
        body, name="proj_bwd", grid=(m // PROJ_TM,),
        in_specs=[pl.BlockSpec((PROJ_TM, kg), lambda i: (i, 0)), pl.BlockSpec((PROJ_TM, kr), lambda i: (i, 0)),
                  pl.BlockSpec((PROJ_TM, LRW), lambda i: (i, 0)), _resident((kg, D)), _resident((kr, D)), _resident((LRW, D)),
                  row, row, vec, vec],
        out_specs=[row, vec, vec, vec],
        out_shape=[SDS((m, D), F32), SDS((1, D), F32), SDS((1, D), F32), SDS((1, D), F32)],
        compiler_params=_cp(("arbitrary",), VMEM_BIG),
    )(dp_g, dp_r, dlr, wit_g, wit_r, wlrt, x, dx1, ng, scale)


def _prep_bwd(x, dh, dx1, ng, scale, name):
    m = x.shape[0]
    has_res = dx1 is not None

    def body(*refs):
        if has_res:
            x_ref, dh_ref, r_ref, g_ref, sc_ref, dx_ref, dg_ref, dsc_ref, dsh_ref = refs
        else:
            x_ref, dh_ref, g_ref, sc_ref, dx_ref, dg_ref, dsc_ref, dsh_ref = refs
        i = pl.program_id(0)

        @pl.when(i == 0)
        def _():
            dg_ref[...] = jnp.zeros_like(dg_ref)
            dsc_ref[...] = jnp.zeros_like(dsc_ref)
            dsh_ref[...] = jnp.zeros_like(dsh_ref)

        xf = x_ref[...]
        dh_ = dh_ref[...]
        r = lax.rsqrt(jnp.mean(xf * xf, axis=-1, keepdims=True) + EPS)
        xh = xf * r
        y = xh * g_ref[...]
        dsh_ref[...] += jnp.sum(dh_, axis=0, keepdims=True)
        dsc_ref[...] += jnp.sum(dh_ * y, axis=0, keepdims=True)
        dy = dh_ * (1.0 + sc_ref[...])
        dg_ref[...] += jnp.sum(dy * xh, axis=0, keepdims=True)
        dxh = dy * g_ref[...]
        dx = r * (dxh - xh * jnp.mean(dxh * xh, axis=-1, keepdims=True))
        if has_res:
            dx = dx + r_ref[...]
        dx_ref[...] = dx

    row = pl.BlockSpec((TOK, D), lambda i: (i, 0))
    vec = _full((1, D))
    in_specs = [row, row] + ([row] if has_res else []) + [vec, vec]
    args = [x, dh] + ([dx1] if has_res else []) + [ng, scale]
    return _pcall(
        body, name=name, grid=(m // TOK,), in_specs=in_specs, out_specs=[row, vec, vec, vec],
        out_shape=[SDS((m, D), F32), SDS((1, D), F32), SDS((1, D), F32), SDS((1, D), F32)],
        compiler_params=_cp(("arbitrary",)),
    )(*args)


def _ln_fwd(p, ln_g, ln_b):
    m = p.shape[0]

    def body(va_ref, g_ref, b_ref, vr_ref, vc_ref):
        xf = va_ref[...].astype(F32)
        xc = xf - jnp.mean(xf, axis=-1, keepdims=True)
        y = xc * lax.rsqrt(jnp.mean(xc * xc, axis=-1, keepdims=True) + EPS)
        vn = y * g_ref[...] + b_ref[...]
        vr_ref[...] = vn[:, 0:256].astype(BF16)
        vc_ref[0] = vn[:, 256:384].astype(BF16)
        vc_ref[1] = vn[:, 384:512].astype(BF16)

    return _pcall(
        body, name="ln_fwd", grid=(m // TOK,),
        in_specs=[pl.BlockSpec((TOK, 512), lambda i: (i, 9)), _full((1, 512)), _full((1, 512))],
        out_specs=[pl.BlockSpec((TOK, 256), lambda i: (i, 0)), pl.BlockSpec((2, TOK, 128), lambda i: (0, i, 0))],
        out_shape=[SDS((m, 256), BF16), SDS((2, m, 128), BF16)], compiler_params=_cp(("parallel",)),
    )(p, ln_g, ln_b)


COLB = 2048


def _colmix_fwd(vnc, ws23, bs23):
    rows = vnc.shape[2] // COLB

    def body(v_ref, w_ref, b_ref, o_ref):
        o_ref[0] = _nn(w_ref[0], v_ref[0]) + b_ref[0]

    return _pcall(
        body, name="colmix_fwd", grid=(2, rows),
        in_specs=[pl.BlockSpec((1, AC, COLB), lambda g, j: (g, 0, j)), pl.BlockSpec((1, AC, AC), lambda g, j: (g, 0, 0)),
                  pl.BlockSpec((1, AC, 1), lambda g, j: (g, 0, 0))],
        out_specs=pl.BlockSpec((1, AC, COLB), lambda g, j: (g, 0, j)),
        out_shape=SDS(vnc.shape, F32), compiler_params=_cp(("parallel", "parallel")),
    )(vnc, ws23, bs23)


def _colmix_bwd(dsvc, vnc, ws23t):
    rows = vnc.shape[2] // COLB

    def body(d_ref, v_ref, wt_ref, dv_ref, dw_ref, db_ref):
        j = pl.program_id(1)

        @pl.when(j == 0)
        def _():
            dw_ref[...] = jnp.zeros_like(dw_ref)
            db_ref[...] = jnp.zeros_like(db_ref)

        d = d_ref[0]
        d16 = d.astype(BF16)
        dv_ref[0] = _nn(wt_ref[0], d16)
        dw_ref[0] += _nt(d16, v_ref[0])
        db_ref[0] += jnp.sum(d, axis=1, keepdims=True)

    blk = pl.BlockSpec((1, AC, COLB), lambda g, j: (g, 0, j))
    return _pcall(
        body, name="colmix_bwd", grid=(2, rows),
        in_specs=[blk, blk, pl.BlockSpec((1, AC, AC), lambda g, j: (g, 0, 0))],
        out_specs=[blk, pl.BlockSpec((1, AC, AC), lambda g, j: (g, 0, 0)), pl.BlockSpec((1, AC, 1), lambda g, j: (g, 0, 0))],
        out_shape=[SDS(vnc.shape, F32), SDS((2, AC, AC), F32), SDS((2, AC, 1), F32)],
        compiler_params=_cp(("parallel", "arbitrary")),
    )(dsvc, vnc, ws23t)


def _head_norm(o, gbn):
    out = []
    for h in range(4):
        oh = o[:, 128 * h:128 * h + 128]
        r = lax.rsqrt(jnp.mean(oh * oh, axis=-1, keepdims=True) + EPS)
        out.append((r, oh * r))
    return out


def _mid_fwd(o_f, o_b, p, vnr, svc, ws01, bs01, gbn):
    m = p.shape[0]

    def body(of_ref, ob_ref, zb_ref, ua_ref, za_ref, vnr_ref, svc_ref, w_ref, b_ref, g_ref, ya_ref, yb_ref, svr_ref):
        o = of_ref[...] + ob_ref[...]
        zb = zb_ref[...]
        parts = []
        for h, (r, xh) in enumerate(_head_norm(o, None)):
            parts.append(xh * g_ref[:, 128 * h:128 * h + 128])
        on = jnp.concatenate(parts, axis=1)
        yb_ref[...] = (on * _silu(zb)).astype(BF16)
        for j in range(TOK // AC):
            for g in range(2):
                sv = _nn(w_ref[g], vnr_ref[AC * j:AC * j + AC, AC * g:AC * g + AC]) + b_ref[g]
                svr_ref[AC * j:AC * j + AC, AC * g:AC * g + AC] = sv
        sz = _silu(za_ref[...])
        u = ua_ref[...]
        ya_ref[:, 0:256] = ((u[:, 0:256] * svr_ref[...]) * sz[:, 0:256]).astype(BF16)
        ya_ref[:, 256:384] = ((u[:, 256:384] * svc_ref[0]) * sz[:, 256:384]).astype(BF16)
        ya_ref[:, 384:512] = ((u[:, 384:512] * svc_ref[1]) * sz[:, 384:512]).astype(BF16)

    r512 = pl.BlockSpec((TOK, 512), lambda i: (i, 0))
    return _pcall(
        body, name="mid_fwd", grid=(m // TOK,),
        in_specs=[r512, r512, pl.BlockSpec((TOK, 512), lambda i: (i, 6)), pl.BlockSpec((TOK, 512), lambda i: (i, 7)),
                  pl.BlockSpec((TOK, 512), lambda i: (i, 8)), pl.BlockSpec((TOK, 256), lambda i: (i, 0)),
                  pl.BlockSpec((2, TOK, 128), lambda i: (0, i, 0)), _full((2, AC, AC)), _full((2, AC, 1)), _full((1, 512))],
        out_specs=[r512, r512, pl.BlockSpec((TOK, 256), lambda i: (i, 0))],
        out_shape=[SDS((m, 512), BF16), SDS((m, 512), BF16), SDS((m, 256), F32)],
        compiler_params=_cp(("parallel",)),
    )(o_f, o_b, p, p, p, vnr, svc, ws01, bs01, gbn)


def _merge_fwd(p, ya, yb):
    m = p.shape[0]

    def body(ga_ref, gb_ref, ya_ref, yb_ref, m_ref):
        m_ref[...] = (jax.nn.sigmoid(ga_ref[...]) * ya_ref[...] + jax.nn.sigmoid(gb_ref[...]) * yb_ref[...]).astype(BF16)

    row = pl.BlockSpec((TOK, D), lambda i: (i, 0))
    return _pcall(
        body, name="merge_fwd", grid=(m // TOK,),
        in_specs=[row, pl.BlockSpec((TOK, D), lambda i: (i, 1)), row, row], out_specs=row,
        out_shape=SDS((m, D), BF16), compiler_params=_cp(("parallel",)),
    )(p, p, ya, yb)


def _loss_head(x, out, tgt, gate, gf):
    m = x.shape[0]

    def body(x_ref, o_ref, t_ref, gate_ref, gf_ref, dx1_ref, dout_ref, loss_ref, dgate_ref, dgf_ref):
        i = pl.program_id(0)

        @pl.when(i == 0)
        def _():
            loss_ref[...] = jnp.zeros_like(loss_ref)
            dgate_ref[...] = jnp.zeros_like(dgate_ref)
            dgf_ref[...] = jnp.zeros_like(dgf_ref)

        out_ = o_ref[...]
        x1 = x_ref[...] + gate_ref[...] * out_
        r = lax.rsqrt(jnp.mean(x1 * x1, axis=-1, keepdims=True) + EPS)
        xh = x1 * r
        err = xh * gf_ref[...] - t_ref[...]
        loss_ref[...] += 0.5 * jnp.sum(jnp.mean(err * err, axis=-1, keepdims=True), axis=0, keepdims=True)
        dy = err * (1.0 / D)
        dgf_ref[...] += jnp.sum(dy * xh, axis=0, keepdims=True)
        dxh = dy * gf_ref[...]
        dx1 = r * (dxh - xh * jnp.mean(dxh * xh, axis=-1, keepdims=True))
        dx1_ref[...] = dx1
        dout_ref[...] = (gate_ref[...] * dx1).astype(BF16)
        dgate_ref[...] += jnp.sum(dx1 * out_, axis=0, keepdims=True)

    row = pl.BlockSpec((TOK, D), lambda i: (i, 0))
    vec = _full((1, D))
    return _pcall(
        body, name="loss_head", grid=(m // TOK,), in_specs=[row, row, row, vec, vec],
        out_specs=[row, row, _full((1, 128)), vec, vec],
        out_shape=[SDS((m, D), F32), SDS((m, D), BF16), SDS((1, 128), F32), SDS((1, D), F32), SDS((1, D), F32)],
        compiler_params=_cp(("arbitrary",)),
    )(x, out, tgt, gate, gf)


def _merge_bwd(dm, ya, yb, p):
    m = p.shape[0]

    def body(dm_ref, ya_ref, yb_ref, ga_ref, gb_ref, dya_ref, dyb_ref, dp_ref):
        dm_ = dm_ref[...]
        sa = jax.nn.sigmoid(ga_ref[...])
        sb = jax.nn.sigmoid(gb_ref[...])
        dya_ref[...] = (dm_ * sa).astype(BF16)
        dyb_ref[...] = (dm_ * sb).astype(BF16)
        dp_ref[:, 0:D] = (dm_ * ya_ref[...] * (sa * (1.0 - sa))).astype(BF16)
        dp_ref[:, D:2 * D] = (dm_ * yb_ref[...] * (sb * (1.0 - sb))).astype(BF16)

    row = pl.BlockSpec((TOK, D), lambda i: (i, 0))
    return _pcall(
        body, name="merge_bwd", grid=(m // TOK,),
        in_specs=[row, row, row, row, pl.BlockSpec((TOK, D), lambda i: (i, 1))],
        out_specs=[row, row, pl.BlockSpec((TOK, 2 * D), lambda i: (i, 0))],
        out_shape=[SDS((m, D), BF16), SDS((m, D), BF16), SDS((m, NP), BF16)],
        compiler_params=_cp(("parallel",)),
    )(dm, ya, yb, p, p)


def _mid_bwd(dya_in, dyb_in, p, svr, svc, o_f, o_b, gbn, dp):
    m = p.shape[0]

    def body(dya_ref, dyb_ref, zb_ref, ua_ref, za_ref, svr_ref, svc_ref, of_ref, ob_ref, g_ref, dpi_ref,
             dp_ref, dsr_ref, dsc_ref, do_ref, dg_ref):
        i = pl.program_id(0)

        @pl.when(i == 0)
        def _():
            dg_ref[...] = jnp.zeros_like(dg_ref)

        dya = dya_ref[...]
        u = ua_ref[...]
        za = za_ref[...]
        sz = _silu(za)
        sv = jnp.concatenate([svr_ref[...], svc_ref[0], svc_ref[1]], axis=1)
        dp_ref[:, 512:1024] = (dya * sv * sz).astype(BF16)
        dsv = dya * u * sz
        dsr_ref[...] = dsv[:, 0:256]
        dsc_ref[0] = dsv[:, 256:384]
        dsc_ref[1] = dsv[:, 384:512]
        dp_ref[:, 1024:1536] = (dya * u * sv * _dsilu(za)).astype(BF16)

        dyb = dyb_ref[...]
        zb = zb_ref[...]
        o = of_ref[...] + ob_ref[...]
        szb = _silu(zb)
        dszb = _dsilu(zb)
        for h, (r, xh) in enumerate(_head_norm(o, None)):
            sl = slice(128 * h, 128 * h + 128)
            gh = g_ref[:, sl]
            don = dyb[:, sl] * szb[:, sl]
            dp_ref[:, sl] = (dyb[:, sl] * (xh * gh) * dszb[:, sl]).astype(BF16)
            dg_ref[:, sl] += jnp.sum(don * xh, axis=0, keepdims=True)
            dxh = don * gh
            do_ref[:, sl] = r * (dxh - xh * jnp.mean(dxh * xh, axis=-1, keepdims=True))

    r512 = pl.BlockSpec((TOK, 512), lambda i: (i, 0))
    return _pcall(
        body, name="mid_bwd", grid=(m // TOK,),
        in_specs=[r512, r512, pl.BlockSpec((TOK, 512), lambda i: (i, 6)), pl.BlockSpec((TOK, 512), lambda i: (i, 7)),
                  pl.BlockSpec((TOK, 512), lambda i: (i, 8)), pl.BlockSpec((TOK, 256), lambda i: (i, 0)),
                  pl.BlockSpec((2, TOK, 128), lambda i: (0, i, 0)), r512, r512, _full((1, 512)),
                  pl.BlockSpec(memory_space=pl.ANY)],
        out_specs=[pl.BlockSpec((TOK, 1536), lambda i: (i, 2)), pl.BlockSpec((TOK, 256), lambda i: (i, 0)),
                   pl.BlockSpec((2, TOK, 128), lambda i: (0, i, 0)), r512, _full((1, 512))],
        out_shape=[SDS((m, NP), BF16), SDS((m, 256), F32), SDS((2, m, 128), F32), SDS((m, 512), F32), SDS((1, 512), F32)],
        input_output_aliases={10: 0}, compiler_params=_cp(("arbitrary",)),
    )(dya_in, dyb_in, p, p, p, svr, svc, o_f, o_b, gbn, dp)


def _tail_fwd(o_f, o_b, p, vnr, svc, x, tgt, ws01, bs01, gbn, wpa, wpb, wo, gate, gf):
    m = p.shape[0]

    def body(of_ref, ob_ref, zb_ref, ua_ref, za_ref, ga_ref, gb_ref, vnr_ref, svc_ref, x_ref, t_ref, w_ref, b_ref, g_ref,
             wpa_ref, wpb_ref, wo_ref, gate_ref, gf_ref,
             ya_ref, yb_ref, svr_ref, m_ref, dx1_ref, dout_ref, loss_ref, dgate_ref, dgf_ref):
        i = pl.program_id(0)

        @pl.when(i == 0)
        def _():
            loss_ref[...] = jnp.zeros_like(loss_ref)
            dgate_ref[...] = jnp.zeros_like(dgate_ref)
            dgf_ref[...] = jnp.zeros_like(dgf_ref)

        o = of_ref[...] + ob_ref[...]
        zb = zb_ref[...].astype(F32)
        for h, (r, xh) in enumerate(_head_norm(o, None)):
            sl = slice(128 * h, 128 * h + 128)
            yb_ref[:, sl] = ((xh * g_ref[:, sl]) * _silu(zb[:, sl])).astype(BF16)
        for j in range(TOK // AC):
            for g in range(2):
                sv = _nn(w_ref[g], vnr_ref[AC * j:AC * j + AC, AC * g:AC * g + AC]) + b_ref[g]
                svr_ref[AC * j:AC * j + AC, AC * g:AC * g + AC] = sv
        sz = _silu(za_ref[...].astype(F32))
        u = ua_ref[...].astype(F32)
        ya_ref[:, 0:256] = ((u[:, 0:256] * svr_ref[...]) * sz[:, 0:256]).astype(BF16)
        ya_ref[:, 256:384] = ((u[:, 256:384] * svc_ref[0]) * sz[:, 256:384]).astype(BF16)
        ya_ref[:, 384:512] = ((u[:, 384:512] * svc_ref[1]) * sz[:, 384:512]).astype(BF16)
        ya = _nn(ya_ref[...], wpa_ref[...])
        yb = _nn(yb_ref[...], wpb_ref[...])
        mg = (jax.nn.sigmoid(ga_ref[...].astype(F32)) * ya + jax.nn.sigmoid(gb_ref[...].astype(F32)) * yb).astype(BF16)
        m_ref[...] = mg
        out_ = _nn(mg, wo_ref[...])
        x1 = x_ref[...] + gate_ref[...] * out_
        r = lax.rsqrt(jnp.mean(x1 * x1, axis=-1, keepdims=True) + EPS)
        xh = x1 * r
        err = xh * gf_ref[...] - t_ref[...]
        loss_ref[...] += 0.5 * jnp.sum(jnp.mean(err * err, axis=-1, keepdims=True), axis=0, keepdims=True)
        dy = err * (1.0 / D)
        dgf_ref[...] += jnp.sum(dy * xh, axis=0, keepdims=True)
        dxh = dy * gf_ref[...]
        dx1 = r * (dxh - xh * jnp.mean(dxh * xh, axis=-1, keepdims=True))
        dx1_ref[...] = dx1
        dout_ref[...] = (gate_ref[...] * dx1).astype(BF16)
        dgate_ref[...] += jnp.sum(dx1 * out_, axis=0, keepdims=True)

    r512 = pl.BlockSpec((TOK, 512), lambda i: (i, 0))
    row = pl.BlockSpec((TOK, D), lambda i: (i, 0))
    vec = _full((1, D))
    return _pcall(
        body, name="tail_fwd", grid=(m // TOK,),
        in_specs=[r512, r512, pl.BlockSpec((TOK, 512), lambda i: (i, 6)), pl.BlockSpec((TOK, 512), lambda i: (i, 7)),
                  pl.BlockSpec((TOK, 512), lambda i: (i, 8)), row, pl.BlockSpec((TOK, D), lambda i: (i, 1)),
                  pl.BlockSpec((TOK, 256), lambda i: (i, 0)), pl.BlockSpec((2, TOK, 128), lambda i: (0, i, 0)), row, row,
                  _full((2, AC, AC)), _full((2, AC, 1)), _full((1, 512)), _full((512, D)), _full((512, D)), _full((D, D)), vec, vec],
        out_specs=[r512, r512, pl.BlockSpec((TOK, 256), lambda i: (i, 0)), row, row, row, _full((1, 128)), vec, vec],
        out_shape=[SDS((m, 512), BF16), SDS((m, 512), BF16), SDS((m, 256), F32), SDS((m, D), BF16), SDS((m, D), F32),
                   SDS((m, D), BF16), SDS((1, 128), F32), SDS((1, D), F32), SDS((1, D), F32)],
        compiler_params=_cp(("arbitrary",), VMEM_BIG),
    )(o_f, o_b, p, p, p, p, p, vnr, svc, x, tgt, ws01, bs01, gbn, wpa, wpb, wo, gate, gf)


DPR = 3072


def _tail_bwd(dout, ya_in, yb_in, p, svr, svc, o_f, o_b, gbn, wot, wpa, wpb, wpat, wpbt):
    m = p.shape[0]

    def body(dout_ref, ya_ref, yb_ref, ga_ref, gb_ref, zb_ref, ua_ref, za_ref, svr_ref, svc_ref, of_ref, ob_ref, g_ref,
             wot_ref, wpa_ref, wpb_ref, wpat_ref, wpbt_ref,
             dya_ref, dyb_ref, dpg_ref, dpr_ref, dsr_ref, dsc_ref, do_ref, dg_ref):
        i = pl.program_id(0)

        @pl.when(i == 0)
        def _():
            dg_ref[...] = jnp.zeros_like(dg_ref)

        dm_ = _nn(dout_ref[...], wot_ref[...])
        ya = _nn(ya_ref[...], wpa_ref[...])
        yb = _nn(yb_ref[...], wpb_ref[...])
        sa = jax.nn.sigmoid(ga_ref[...].astype(F32))
        sb = jax.nn.sigmoid(gb_ref[...].astype(F32))
        dya16 = (dm_ * sa).astype(BF16)
        dyb16 = (dm_ * sb).astype(BF16)
        dya_ref[...] = dya16
        dyb_ref[...] = dyb16
        dpg_ref[:, 0:D] = (dm_ * ya * (sa * (1.0 - sa))).astype(BF16)
        dpg_ref[:, D:2 * D] = (dm_ * yb * (sb * (1.0 - sb))).astype(BF16)
        dya = _nn(dya16, wpat_ref[...])
        dyb = _nn(dyb16, wpbt_ref[...])

        u = ua_ref[...].astype(F32)
        za = za_ref[...].astype(F32)
        sz = _silu(za)
        sv = jnp.concatenate([svr_ref[...], svc_ref[0], svc_ref[1]], axis=1)
        dpr_ref[:, 512:1024] = (dya * sv * sz).astype(BF16)
        dsv = dya * u * sz
        dsr_ref[...] = dsv[:, 0:256]
        dsc_ref[0] = dsv[:, 256:384]
        dsc_ref[1] = dsv[:, 384:512]
        dpr_ref[:, 1024:1536] = (dya * u * sv * _dsilu(za)).astype(BF16)

        zb = zb_ref[...].astype(F32)
        o = of_ref[...] + ob_ref[...]
        szb = _silu(zb)
        dszb = _dsilu(zb)
        for h, (r, xh) in enumerate(_head_norm(o, None)):
            sl = slice(128 * h, 128 * h + 128)
            gh = g_ref[:, sl]
            don = dyb[:, sl] * szb[:, sl]
            dpr_ref[:, sl] = (dyb[:, sl] * (xh * gh) * dszb[:, sl]).astype(BF16)
            dg_ref[:, sl] += jnp.sum(don * xh, axis=0, keepdims=True)
            dxh = don * gh
            do_ref[:, sl] = r * (dxh - xh * jnp.mean(dxh * xh, axis=-1, keepdims=True))

    r512 = pl.BlockSpec((TOK, 512), lambda i: (i, 0))
    row = pl.BlockSpec((TOK, D), lambda i: (i, 0))
    return _pcall(
        body, name="tail_bwd", grid=(m // TOK,),
        in_specs=[row, r512, r512, row, pl.BlockSpec((TOK, D), lambda i: (i, 1)), pl.BlockSpec((TOK, 512), lambda i: (i, 6)),
                  pl.BlockSpec((TOK, 512), lambda i: (i, 7)), pl.BlockSpec((TOK, 512), lambda i: (i, 8)),
                  pl.BlockSpec((TOK, 256), lambda i: (i, 0)), pl.BlockSpec((2, TOK, 128), lambda i: (0, i, 0)), r512, r512,
                  _full((1, 512)), _full((D, D)), _full((512, D)), _full((512, D)), _full((D, 512)), _full((D, 512))],
        out_specs=[row, row, pl.BlockSpec((TOK, 2 * D), lambda i: (i, 0)), pl.BlockSpec((TOK, 1536), lambda i: (i, 0)),
                   pl.BlockSpec((TOK, 256), lambda i: (i, 0)), pl.BlockSpec((2, TOK, 128), lambda i: (0, i, 0)), r512, _full((1, 512))],
        out_shape=[SDS((m, D), BF16), SDS((m, D), BF16), SDS((m, 2 * D), BF16), SDS((m, DPR), BF16), SDS((m, 256), F32),
                   SDS((2, m, 128), F32), SDS((m, 512), F32), SDS((1, 512), F32)],
        compiler_params=_cp(("arbitrary",), VMEM_BIG),
    )(dout, ya_in, yb_in, p, p, p, p, p, svr, svc, o_f, o_b, gbn, wot, wpa, wpb, wpat, wpbt)


def _mm_multi(pairs, *, tm, tn, out_dtype, name):
    m = pairs[0][0].shape[0]
    n = pairs[0][1].shape[1]
    nks = [a.shape[1] // tk for a, _, tk in pairs]
    starts = [sum(nks[:i]) for i in range(len(pairs))]
    total = sum(nks)

    def body(*refs):
        o_ref, acc_ref = refs[-2], refs[-1]
        kk = pl.program_id(2)
        for idx in range(len(pairs)):
            a_ref, b_ref = refs[2 * idx], refs[2 * idx + 1]

            @pl.when((kk >= starts[idx]) & (kk < starts[idx] + nks[idx]))
            def _(a_ref=a_ref, b_ref=b_ref, first=(idx == 0)):
                part = _nn(a_ref[...].astype(BF16), b_ref[...].astype(BF16))
                if first:
                    @pl.when(kk == 0)
                    def _():
                        acc_ref[...] = part

                    @pl.when(kk > 0)
                    def _():
                        acc_ref[...] += part
                else:
                    acc_ref[...] += part

        @pl.when(kk == total - 1)
        def _():
            o_ref[...] = acc_ref[...].astype(out_dtype)

    in_specs, args = [], []
    for (a, b, tk), st, nk in zip(pairs, starts, nks):
        in_specs.append(pl.BlockSpec((tm, tk), lambda i, j, kk, st=st, nk=nk: (i, jnp.clip(kk - st, 0, nk - 1))))
        in_specs.append(pl.BlockSpec((tk, tn), lambda i, j, kk, st=st, nk=nk: (jnp.clip(kk - st, 0, nk - 1), j)))
        args += [a, b]
    return _pcall(
        body, name=name, grid=(m // tm, n // tn, total), in_specs=in_specs,
        out_specs=pl.BlockSpec((tm, tn), lambda i, j, kk: (i, j)), out_shape=SDS((m, n), out_dtype),
        scratch_shapes=[pltpu.VMEM((tm, tn), F32)], compiler_params=_cp(("parallel", "parallel", "arbitrary"), VMEM_BIG),
    )(*args)


def _ln_bwd(dsr, vnr, dvnc, p, ws01t, ln_g, dp):
    m = p.shape[0]

    def body(dsr_ref, vnr_ref, dvc_ref, va_ref, wt_ref, g_ref, dpi_ref, dp_ref, dw_ref, db_ref, dlg_ref, dlb_ref, dvn_ref):
        i = pl.program_id(0)

        @pl.when(i == 0)
        def _():
            dw_ref[...] = jnp.zeros_like(dw_ref)
            db_ref[...] = jnp.zeros_like(db_ref)
            dlg_ref[...] = jnp.zeros_like(dlg_ref)
            dlb_ref[...] = jnp.zeros_like(dlb_ref)

        for j in range(TOK // AC):
            for g in range(2):
                d = dsr_ref[AC * j:AC * j + AC, AC * g:AC * g + AC]
                d16 = d.astype(BF16)
                dvn_ref[AC * j:AC * j + AC, AC * g:AC * g + AC] = _nn(wt_ref[g], d16)
                dw_ref[g] += _nt(d16, vnr_ref[AC * j:AC * j + AC, AC * g:AC * g + AC])
                db_ref[g] += jnp.sum(d, axis=1, keepdims=True)
        dvn_ref[:, 256:384] = dvc_ref[0]
        dvn_ref[:, 384:512] = dvc_ref[1]
        dvn = dvn_ref[...]
        xf = va_ref[...].astype(F32)
        xc = xf - jnp.mean(xf, axis=-1, keepdims=True)
        rs = lax.rsqrt(jnp.mean(xc * xc, axis=-1, keepdims=True) + EPS)
        xh = xc * rs
        dlg_ref[...] += jnp.sum(dvn * xh, axis=0, keepdims=True)
        dlb_ref[...] += jnp.sum(dvn, axis=0, keepdims=True)
        dxh = dvn * g_ref[...]
        dva = rs * (dxh - jnp.mean(dxh, axis=-1, keepdims=True) - xh * jnp.mean(dxh * xh, axis=-1, keepdims=True))
        dp_ref[...] = dva.astype(BF16)

    return _pcall(
        body, name="ln_bwd", grid=(m // TOK,),
        in_specs=[pl.BlockSpec((TOK, 256), lambda i: (i, 0)), pl.BlockSpec((TOK, 256), lambda i: (i, 0)),
                  pl.BlockSpec((2, TOK, 128), lambda i: (0, i, 0)), pl.BlockSpec((TOK, 512), lambda i: (i, 9)),
                  _full((2, AC, AC)), _full((1, 512)), pl.BlockSpec(memory_space=pl.ANY)],
        out_specs=[pl.BlockSpec((TOK, 512), lambda i: (i, 3)), _full((2, AC, AC)), _full((2, AC, 1)), _full((1, 512)), _full((1, 512))],
        out_shape=[SDS((m, DPR), BF16), SDS((2, AC, AC), F32), SDS((2, AC, 1), F32), SDS((1, 512), F32), SDS((1, 512), F32)],
        scratch_shapes=[pltpu.VMEM((TOK, 512), F32)],
        input_output_aliases={6: 0}, compiler_params=_cp(("arbitrary",)),
    )(dsr, vnr, dvnc, p, ws01t, ln_g, dp)


def _tri_mm(tri, a):
    a1 = a.astype(BF16)
    r1 = a - a1.astype(F32)
    a2 = r1.astype(BF16)
    a3 = (r1 - a2.astype(F32)).astype(BF16)
    n = a.shape[1]
    r = _nn(tri, jnp.concatenate([a1, a2, a3], axis=1))
    return r[:, 0:n] + r[:, n:2 * n] + r[:, 2 * n:3 * n]


def _gla_masks(reverse):
    ri = lax.broadcasted_iota(jnp.int32, (CH, CH), 0)
    ci = lax.broadcasted_iota(jnp.int32, (CH, CH), 1)
    vis = (ci >= ri) if reverse else (ci <= ri)
    vis_t = (ci <= ri) if reverse else (ci >= ri)
    r4 = lax.broadcasted_iota(jnp.int32, (4 * CH, CH), 0) & (CH - 1)
    c4 = lax.broadcasted_iota(jnp.int32, (4 * CH, CH), 1)
    vis4 = (c4 >= r4) if reverse else (c4 <= r4)
    vis4_t = (c4 <= r4) if reverse else (c4 >= r4)
    lane = lax.broadcasted_iota(jnp.int32, (1, 256), 1)
    hm = [(lane >= CH * h) & (lane < CH * h + CH) for h in range(4)]
    return vis, vis_t, vis4, vis4_t, hm


def _stack_heads(x, hm):
    return jnp.concatenate([jnp.where(hm[h], x, 0.0).astype(BF16) for h in range(4)], axis=0)


def _diag_heads(full, hm):
    r = full.shape[0] // 4
    acc = jnp.where(hm[0], full[0:r], 0.0)
    for h in range(1, 4):
        acc = acc + jnp.where(hm[h], full[r * h:r * h + r], 0.0)
    return acc


def _rows_of_heads(x):
    return jnp.concatenate([x[:, 128 * h:128 * h + 128] for h in range(4)], axis=0)


def _lane_vis(reverse, transpose):
    ri = lax.broadcasted_iota(jnp.int32, (CH, 4 * CH), 0)
    ci = lax.broadcasted_iota(jnp.int32, (CH, 4 * CH), 1) & (CH - 1)
    return (ci >= ri) if (reverse != transpose) else (ci <= ri)


def _gla_fwd(p, qkv_blk, lr, lrw, gbias, s0, *, reverse, name):
    m = p.shape[0]
    nb = m // GLA_TB
    nc = GLA_TB // CH
    rmap = (lambda i: nb - 1 - i) if reverse else (lambda i: i)

    def body(qkv_ref, lr_ref, lrw_ref, gb_ref, s0_ref, o_ref, sb_ref, sfin_ref, st_ref):
        i = pl.program_id(0)

        @pl.when(i == 0)
        def _():
            st_ref[...] = s0_ref[...]

        vis, _, vis4, _, hm = _gla_masks(reverse)
        tri = vis.astype(F32).astype(BF16)
        logits = _nn(lr_ref[...].astype(BF16), lrw_ref[...]) + gb_ref[...]
        a_all = _logsig(logits) * (1.0 / 16.0)
        st = st_ref[...]
        for c in (range(nc - 1, -1, -1) if reverse else range(nc)):
            rows = slice(CH * c, CH * c + CH)
            b = _tri_mm(tri, a_all[rows])
            bl = b[0:1] if reverse else b[CH - 1:CH]
            q = qkv_ref[rows, 0:256].astype(F32) * 0.125
            k = qkv_ref[rows, 256:512].astype(F32)
            v16 = qkv_ref[rows, 512:1024].astype(BF16)
            qd = q * jnp.exp(b)
            kd16 = (k * jnp.exp(-b)).astype(BF16)
            kdec16 = (k * jnp.exp(bl - b)).astype(BF16)
            qstack = _stack_heads(qd, hm)
            sc = jnp.where(vis4, _nt(qstack, kd16), 0.0).astype(BF16)
            inter = _nt(qstack, st.astype(BF16))
            for h in range(4):
                o_ref[rows, 128 * h:128 * h + 128] = (
                    _nn(sc[CH * h:CH * h + CH], v16[:, 128 * h:128 * h + 128]) + inter[CH * h:CH * h + CH])
            sb_ref[c] = st
            st = st * jnp.exp(bl) + _diag_heads(_tn(v16, kdec16), hm)
        st_ref[...] = st

        @pl.when(i == nb - 1)
        def _():
            sfin_ref[...] = st

    return _pcall(
        body, name=name, grid=(nb,),
        in_specs=[pl.BlockSpec((GLA_TB, 1024), lambda i: (rmap(i), qkv_blk)), pl.BlockSpec((GLA_TB, LRW), lambda i: (rmap(i), 0)),
                  _full((LRW, 256)), _full((1, 256)), _full((128, 256))],
        out_specs=[pl.BlockSpec((GLA_TB, 512), lambda i: (rmap(i), 0)), pl.BlockSpec((nc, 128, 256), lambda i: (rmap(i), 0, 0)),
                   _full((128, 256))],
        out_shape=[SDS((m, 512), F32), SDS((m // CH, 128, 256), F32), SDS((128, 256), F32)],
        scratch_shapes=[pltpu.VMEM((128, 256), F32)], compiler_params=_cp(("arbitrary",)),
    )(p, lr, lrw, gbias, s0)


def _gla_bwd(p, qkv_blk, lr, lrw, lrwt, gbias, sb, dsfin, do, prev, dp, *, reverse, name):
    m = p.shape[0]
    nb = m // GLA_TB
    nc = GLA_TB // CH
    rmap = (lambda i: i) if reverse else (lambda i: nb - 1 - i)
    has_prev = prev is not None
    has_dp = dp is not None

    def body(*refs):
        refs = list(refs)
        qkv_ref, lr_ref, lrw_ref, lrwt_ref, gb_ref, sb_ref, dsfin_ref, do_ref = refs[:8]
        refs = refs[8:]
        if has_prev:
            pq_ref, plr_ref = refs[:2]
            refs = refs[2:]
        if has_dp:
            refs = refs[1:]
        dqkv_ref, dlr_ref, dw2_ref, dgb_ref, ds0_ref, dst_ref, dlog_ref = refs
        i = pl.program_id(0)

        @pl.when(i == 0)
        def _():
            dst_ref[...] = dsfin_ref[...]
            dw2_ref[...] = jnp.zeros_like(dw2_ref)
            dgb_ref[...] = jnp.zeros_like(dgb_ref)

        vis, vis_t, vis4, vis4_t, hm = _gla_masks(reverse)
        tri = vis.astype(F32).astype(BF16)
        tri_t = vis_t.astype(F32).astype(BF16)
        lane_vis = _lane_vis(reverse, False)
        lane_vis_t = _lane_vis(reverse, True)
        lr16 = lr_ref[...].astype(BF16)
        logits = _nn(lr16, lrw_ref[...]) + gb_ref[...]
        a_all = _logsig(logits) * (1.0 / 16.0)
        dsig = (1.0 - jax.nn.sigmoid(logits)) * (1.0 / 16.0)
        dst = dst_ref[...]
        for c in (range(nc) if reverse else range(nc - 1, -1, -1)):
            rows = slice(CH * c, CH * c + CH)
            b = _tri_mm(tri, a_all[rows])
            bl = b[0:1] if reverse else b[CH - 1:CH]
            eb = jnp.exp(b)
            enb = jnp.exp(-b)
            ebl = jnp.exp(bl - b)
            el = jnp.exp(bl)
            q = qkv_ref[rows, 0:256].astype(F32) * 0.125
            k = qkv_ref[rows, 256:512].astype(F32)
            v16 = qkv_ref[rows, 512:1024].astype(BF16)
            do16 = do_ref[rows, :].astype(BF16)
            qd = q * eb
            kd = k * enb
            kdec = k * ebl
            st = sb_ref[c]
            st16 = st.astype(BF16)
            dst16 = dst.astype(BF16)
            qd16 = qd.astype(BF16)
            kd16 = kd.astype(BF16)
            qstack = _stack_heads(qd, hm)
            kstack = _stack_heads(kd, hm)
            kdecstack = _stack_heads(kdec, hm)
            pt = jnp.where(vis4_t, _nt(kstack, qd16), 0.0).astype(BF16)
            dvinter = _nt(kdecstack, dst16)
            do_rows = _rows_of_heads(do16)
            v_rows = _rows_of_heads(v16)
            dp_cat = jnp.where(lane_vis, _diag_heads(_nt(do_rows, v_rows), hm), 0.0).astype(BF16)
            dpt_cat = jnp.where(lane_vis_t, _diag_heads(_nt(v_rows, do_rows), hm), 0.0).astype(BF16)
            dqd = _nn(dp_cat, kstack) + _diag_heads(_nn(do_rows, st16), hm)
            dkd = _nn(dpt_cat, qstack)
            dkdec = _diag_heads(_nn(v_rows, dst16), hm)
            for h in range(4):
                rh = slice(CH * h, CH * h + CH)
                dv_h = _nn(pt[rh], do_rows[rh]) + dvinter[rh]
                if has_prev:
                    dv_h = dv_h + pq_ref[rows, 512 + 128 * h:512 + 128 * h + 128]
                dqkv_ref[rows, 512 + 128 * h:512 + 128 * h + 128] = dv_h.astype(dqkv_ref.dtype)
            dq = dqd * eb * 0.125
            dk = dkd * enb + dkdec * ebl
            if has_prev:
                dq = dq + pq_ref[rows, 0:256]
                dk = dk + pq_ref[rows, 256:512]
            dqkv_ref[rows, 0:256] = dq.astype(dqkv_ref.dtype)
            dqkv_ref[rows, 256:512] = dk.astype(dqkv_ref.dtype)
            g_kdec = dkdec * kdec
            db = dqd * qd - dkd * kd - g_kdec
            dbl = jnp.sum(g_kdec, axis=0, keepdims=True) + jnp.sum(st * dst, axis=0, keepdims=True) * el
            da = _tri_mm(tri_t, db) + dbl
            dlog_ref[rows, :] = da * dsig[rows]
            dst = dst * el + _diag_heads(_tn(do16, qd16), hm)
        dst_ref[...] = dst
        dlog = dlog_ref[...]
        dlog16 = dlog.astype(BF16)
        dlr = _nn(dlog16, lrwt_ref[...])
        if has_prev:
            dlr = dlr + plr_ref[...]
        dlr_ref[...] = dlr
        dw2_ref[...] += _tn(lr16, dlog16)
        dgb_ref[...] += jnp.sum(dlog, axis=0, keepdims=True)

        @pl.when(i == nb - 1)
        def _():
            ds0_ref[...] = dst

    in_specs = [pl.BlockSpec((GLA_TB, 1024), lambda i: (rmap(i), qkv_blk)), pl.BlockSpec((GLA_TB, LRW), lambda i: (rmap(i), 0)),
                _full((LRW, 256)), _full((256, LRW)), _full((1, 256)), pl.BlockSpec((nc, 128, 256), lambda i: (rmap(i), 0, 0)),
                _full((128, 256)), pl.BlockSpec((GLA_TB, 512), lambda i: (rmap(i), 0))]
    args = [p, lr, lrw, lrwt, gbias, sb, dsfin, do]
    if has_prev:
        in_specs += [pl.BlockSpec((GLA_TB, 1024), lambda i: (rmap(i), 0)), pl.BlockSpec((GLA_TB, LRW), lambda i: (rmap(i), 0))]
        args += list(prev)
    aliases = {}
    if has_dp:
        in_specs.append(pl.BlockSpec(memory_space=pl.ANY))
        aliases = {len(args): 0}
        args.append(dp)
        dq_spec = pl.BlockSpec((GLA_TB, 1024), lambda i: (rmap(i), 2))
        dq_shape = SDS(dp.shape, dp.dtype)
    else:
        dq_spec = pl.BlockSpec((GLA_TB, 1024), lambda i: (rmap(i), 0))
        dq_shape = SDS((m, 1024), F32)
    return _pcall(
        body, name=name, grid=(nb,), in_specs=in_specs,
        out_specs=[dq_spec, pl.BlockSpec((GLA_TB, LRW), lambda i: (rmap(i), 0)), _full((LRW, 256)), _full((1, 256)), _full((128, 256))],
        out_shape=[dq_shape, SDS((m, LRW), F32), SDS((LRW, 256), F32), SDS((1, 256), F32), SDS((128, 256), F32)],
        scratch_shapes=[pltpu.VMEM((128, 256), F32), pltpu.VMEM((GLA_TB, 256), F32)],
        input_output_aliases=aliases, compiler_params=_cp(("arbitrary",)),
    )(*args)


def _device_step(x, c, ctx, c_ctx, tgt, wm, bm, ng, wi, wlr, ln_g, ln_b, ws, bs, w2, gb2, gbn, wpa, wpb, wo, gf):
    L = x.shape[0]
    wit = wi.T
    wlrt = wlr.T
    ws16 = ws.astype(BF16)
    wst16 = jnp.swapaxes(ws, 1, 2).astype(BF16)
    bscol = bs[:, :, None]
    lrw = [jnp.zeros((LRW, 256), F32).at[16 * r:16 * r + 16].set(w2[r]).astype(BF16) for r in range(2)]
    lrwt = [w.T for w in lrw]
    gbias = [gb2[r:r + 1] for r in range(2)]

    cc = jnp.zeros((8, D), F32).at[0:1].set(c).at[1:2].set(c_ctx)
    mod = _modvec(cc, wm, bm)
    shift, scale, gate = mod[0:1, 0:D], mod[0:1, D:2 * D], mod[0:1, 2 * D:3 * D]
    shift_c, scale_c = mod[1:2, 0:D], mod[1:2, D:2 * D]

    hc = _prep_h(ctx, ng, scale_c, shift_c, "prep_hc")
    pc = _mm(hc, wi[:, 2048:3072], tm=256, tn=1024, tk=D, out_dtype=F32, name="mm_pc")
    plrc = _mm(hc, wlr, tm=256, tn=LRW, tk=D, out_dtype=F32, name="mm_plrc")
    zero_s = jnp.zeros((128, 256), F32)
    _, sbc_f, sc_f = _gla_fwd(pc, 0, plrc, lrw[0], gbias[0], zero_s, reverse=False, name="gla_fwd_cf")
    _, sbc_b, sc_b = _gla_fwd(pc, 0, plrc, lrw[1], gbias[1], zero_s, reverse=True, name="gla_fwd_cb")

    h, p, plr = _proj_fwd(x, ng, scale, shift, wi, wlr)
    o_f, sb_f, _ = _gla_fwd(p, 2, plr, lrw[0], gbias[0], sc_f, reverse=False, name="gla_fwd_f")
    o_b, sb_b, _ = _gla_fwd(p, 2, plr, lrw[1], gbias[1], sc_b, reverse=True, name="gla_fwd_b")
    vnr, vnc = _ln_fwd(p, ln_g, ln_b)
    svc = _colmix_fwd(vnc.reshape(2, AC, L), ws16[2:4], bscol[2:4]).reshape(2, L, 128)
    ya_in, yb_in, svr, mrg, dx1, dout, loss, dgate, dgf = _tail_fwd(
        o_f, o_b, p, vnr, svc, x, tgt, ws16[0:2], bscol[0:2], gbn, wpa, wpb, wo, gate, gf)

    dya, dyb, dp_g, dp, dsr, dsc, do, dgbn = _tail_bwd(dout, ya_in, yb_in, p, svr, svc, o_f, o_b, gbn, wo.T, wpa, wpb, wpa.T, wpb.T)
    dwo = _mm_tn(mrg, dout, ta=D, tn=D, tk=1024, name="mm_dwo")
    dwpa = _mm_tn(ya_in, dya, ta=512, tn=D, tk=1024, name="mm_dwpa")
    dwpb = _mm_tn(yb_in, dyb, ta=512, tn=D, tk=1024, name="mm_dwpb")
    dvnc, dws23, dbs23 = _colmix_bwd(dsc.reshape(2, AC, L), vnc.reshape(2, AC, L), wst16[2:4])
    dp, dws01, dbs01, dlng, dlnb = _ln_bwd(dsr, vnr, dvnc.reshape(2, L, 128), p, wst16[0:2], ln_g, dp)
    zero_ds = jnp.zeros((128, 256), F32)
    dqkv_f, dlr_f, dw2_f, dgb_f, ds0_f = _gla_bwd(p, 2, plr, lrw[0], lrwt[0], gbias[0], sb_f, zero_ds, do, None, None,
                                                  reverse=False, name="gla_bwd_f")
    dp, dlr, dw2_b, dgb_b, ds0_b = _gla_bwd(p, 2, plr, lrw[1], lrwt[1], gbias[1], sb_b, zero_ds, do, (dqkv_f, dlr_f), dp,
                                            reverse=True, name="gla_bwd_b")
    zero_do = jnp.zeros((ctx.shape[0], 512), F32)
    dqkvc_f, dlrc_f, dw2c_f, dgbc_f, _ = _gla_bwd(pc, 0, plrc, lrw[0], lrwt[0], gbias[0], sbc_f, ds0_f, zero_do, None, None,
                                                  reverse=False, name="gla_bwd_cf")
    dqkvc, dlrc, dw2c_b, dgbc_b, _ = _gla_bwd(pc, 0, plrc, lrw[1], lrwt[1], gbias[1], sbc_b, ds0_b, zero_do,
                                              (dqkvc_f, dlrc_f), None, reverse=True, name="gla_bwd_cb")
    dhc = _mm(dqkvc, wit[2048:3072], tm=256, tn=D, tk=1024, out_dtype=F32, name="mm_dhc")
    dhc = _mm(dlrc, wlrt, tm=256, tn=D, tk=LRW, out_dtype=F32, name="mm_dhc_lr", acc=dhc)
    _, dng_c, dscale_c, dshift_c = _prep_bwd(ctx, dhc, None, ng, scale_c, "prep_bwd_c")

    wit_r = jnp.concatenate([wit[3072:5120], wit[2048:3072]], axis=0)
    dx, dng, dscale, dshift = _proj_bwd(dp_g, dp, dlr, wit[0:2048], wit_r, wlrt, x, dx1, ng, scale)
    dwi_g = _mm_tn(h, dp_g, ta=D, tn=1024, tk=1024, name="mm_dwi_g")
    dwi_r = _mm_tn(h, dp, ta=D, tn=1024, tk=1024, name="mm_dwi_r")
    dwi_qkv = _mm_tn(hc, dqkvc, ta=D, tn=1024, tk=256, name="mm_dwi_c", acc=dwi_r[:, 2048:3072])
    dwlr = _mm_tn(h, dlr, ta=D, tn=LRW, tk=1024, name="mm_dwlr")
    dwlr = _mm_tn(hc, dlrc, ta=D, tn=LRW, tk=256, name="mm_dwlr_c", acc=dwlr)

    dmodc = jnp.concatenate([dshift_c, dscale_c], axis=1)
    dscc = _dcctx(jnp.zeros((8, 2 * D), F32).at[0:1].set(dmodc), wm)[0:1]
    dw2p = dw2_f + dw2c_f, dw2_b + dw2c_b
    return dict(
        loss=loss[0, 0], dx=dx, dwi_g=dwi_g, dwi_r=dwi_r, dwi_qkv=dwi_qkv, dwlr=dwlr, dwpa=dwpa, dwpb=dwpb, dwo=dwo,
        dmod=jnp.concatenate([dshift, dscale, dgate], axis=1), dmodc=dmodc, dscc=dscc, dng=dng + dng_c,
        dlng=dlng, dlnb=dlnb, dws=jnp.concatenate([dws01, dws23], axis=0),
        dbs=jnp.concatenate([dbs01, dbs23], axis=0)[:, :, 0], dgbn=dgbn, dgf=dgf,
        dw2=jnp.stack([dw2p[0][0:16], dw2p[1][16:32]]), dgb2=jnp.concatenate([dgb_f + dgbc_f, dgb_b + dgbc_b], axis=0),
    )


ANY = pl.BlockSpec(memory_space=pl.ANY)


def _coords():
    return lax.axis_index("x"), lax.axis_index("y"), lax.axis_index("c")


def _flip(v, bit):
    return 1 - v if bit else v


def _remote(src, dst, send_sem, recv_sem, dev):
    return pltpu.make_async_remote_copy(src_ref=src, dst_ref=dst, send_sem=send_sem, recv_sem=recv_sem,
                                        device_id=dev, device_id_type=MESH)


def _own(out, block, idx):
    return lax.dynamic_update_slice_in_dim(out, block[None], idx, axis=0)


def _gather_weights(split, whole, name):
    ns, nw = len(split), len(whole)
    n = ns + nw

    def body(*refs):
        ins, outs = refs[:n], refs[n:2 * n]
        a_send, a_recv, b_send, b_recv = refs[2 * n:]
        x, y, c = _coords()
        me = 2 * x + y
        sib = (x, y, 1 - c)
        peers = [(1 - x, y), (x, 1 - y), (1 - x, 1 - y)]
        sends = []
        for k in range(n):
            for j, (px, py) in enumerate(peers):
                if k < ns:
                    h = split[k].shape[0] // 2
                    rc = _remote(ins[k].at[pl.ds(c * h, h)], outs[k].at[me, pl.ds(c * h, h)], a_send.at[3 * k + j],
                                 a_recv.at[3 * k + j], (px, py, c))
                else:
                    rc = _remote(ins[k], outs[k].at[me], a_send.at[3 * k + j], a_recv.at[3 * k + j], (px, py, c))
                rc.start()
                sends.append(rc)
        for k in range(ns):
            h = split[k].shape[0] // 2
            for j, (px, py) in enumerate(peers):
                landed = outs[k].at[2 * px + py, pl.ds(c * h, h)]
                _remote(landed, landed, a_send.at[3 * k + j], a_recv.at[3 * k + j], (px, py, c)).wait_recv()
                fw = _remote(landed, landed, b_send.at[3 * k + j], b_recv.at[3 * k + j], sib)
                fw.start()
                sends.append(fw)
        for k in range(ns, n):
            for j, (px, py) in enumerate(peers):
                landed = outs[k].at[2 * px + py]
                _remote(landed, landed, a_send.at[3 * k + j], a_recv.at[3 * k + j], (px, py, c)).wait_recv()
        for k in range(ns):
            h = split[k].shape[0] // 2
            for j, (px, py) in enumerate(peers):
                passed = outs[k].at[2 * px + py, pl.ds((1 - c) * h, h)]
                _remote(passed, passed, b_send.at[3 * k + j], b_recv.at[3 * k + j], sib).wait_recv()
        for rc in sends:
            rc.wait_send()

    arrs = list(split) + list(whole)
    outs = _pcall(
        body, name=name, in_specs=[ANY] * n, out_specs=[ANY] * n,
        out_shape=[SDS((4,) + a.shape, a.dtype) for a in arrs],
        scratch_shapes=[pltpu.SemaphoreType.DMA((3 * n,)), pltpu.SemaphoreType.DMA((3 * n,)), pltpu.SemaphoreType.DMA((3 * ns,)),
                        pltpu.SemaphoreType.DMA((3 * ns,))],
    )(*arrs)
    me_xy = 2 * lax.axis_index("x") + lax.axis_index("y")
    return [_own(o, a, me_xy) for o, a in zip(outs, arrs)]


def _gather_all(a, name):
    masks = [(mx, my, mc) for mx in range(2) for my in range(2) for mc in range(2)][1:]

    def body(in_ref, out_ref, send_sems, recv_sems):
        x, y, c = _coords()
        me = 4 * x + 2 * y + c
        sends = []
        for j, (mx, my, mc) in enumerate(masks):
            peer = (_flip(x, mx), _flip(y, my), _flip(c, mc))
            rc = pltpu.make_async_remote_copy(
                src_ref=in_ref, dst_ref=out_ref.at[me], send_sem=send_sems.at[j], recv_sem=recv_sems.at[j],
                device_id=peer, device_id_type=MESH)
            rc.start()
            sends.append(rc)
        for j, (mx, my, mc) in enumerate(masks):
            px, py, pc = _flip(x, mx), _flip(y, my), _flip(c, mc)
            pltpu.make_async_remote_copy(
                src_ref=in_ref, dst_ref=out_ref.at[4 * px + 2 * py + pc], send_sem=send_sems.at[j], recv_sem=recv_sems.at[j],
                device_id=(px, py, pc), device_id_type=MESH).wait_recv()
        for rc in sends:
            rc.wait_send()

    out = _pcall(
        body, name=name, in_specs=[ANY], out_specs=ANY, out_shape=SDS((8,) + a.shape, a.dtype),
        scratch_shapes=[pltpu.SemaphoreType.DMA((7,)), pltpu.SemaphoreType.DMA((7,))],
    )(a)
    return _own(out, a, 4 * lax.axis_index("x") + 2 * lax.axis_index("y") + lax.axis_index("c"))


def _swap_rows_c(arrs, name):
    n = len(arrs)

    def body(*refs):
        ins, outs = refs[:n], refs[n:2 * n]
        send_sems, recv_sems = refs[2 * n:]
        x, y, c = _coords()
        sends = []
        for k in range(n):
            h = arrs[k].shape[1] // 2
            rc = _remote(ins[k].at[pl.ds(0, 4), pl.ds((1 - c) * h, h)], outs[k], send_sems.at[k], recv_sems.at[k], (x, y, 1 - c))
            rc.start()
            sends.append(rc)
        for rc in sends:
            rc.wait()

    return _pcall(
        body, name=name, in_specs=[ANY] * n, out_specs=[ANY] * n,
        out_shape=[SDS((4, a.shape[1] // 2, a.shape[2]), a.dtype) for a in arrs],
        scratch_shapes=[pltpu.SemaphoreType.DMA((n,)), pltpu.SemaphoreType.DMA((n,))],
    )(*arrs)


def _a2a_xy(arrs, name):
    n = len(arrs)

    def body(*refs):
        ins, outs = refs[:n], refs[n:2 * n]
        send_sems, recv_sems = refs[2 * n:]
        x, y, c = _coords()
        me = 2 * x + y
        peers = [(1 - x, y), (x, 1 - y), (1 - x, 1 - y)]
        sends = []
        for k in range(n):
            for j, (px, py) in enumerate(peers):
                rc = _remote(ins[k].at[2 * px + py], outs[k].at[me], send_sems.at[3 * k + j], recv_sems.at[3 * k + j], (px, py, c))
                rc.start()
                sends.append(rc)
        for k in range(n):
            for j, (px, py) in enumerate(peers):
                landed = outs[k].at[2 * px + py]
                _remote(landed, landed, send_sems.at[3 * k + j], recv_sems.at[3 * k + j], (px, py, c)).wait_recv()
        for rc in sends:
            rc.wait_send()

    outs = _pcall(
        body, name=name, in_specs=[ANY] * n, out_specs=[ANY] * n, out_shape=[SDS(a.shape, a.dtype) for a in arrs],
        scratch_shapes=[pltpu.SemaphoreType.DMA((3 * n,)), pltpu.SemaphoreType.DMA((3 * n,))],
    )(*arrs)
    me_xy = 2 * lax.axis_index("x") + lax.axis_index("y")
    return [_own(o, lax.dynamic_index_in_dim(a, me_xy, axis=0, keepdims=False), me_xy) for o, a in zip(outs, arrs)]


def _join_halves(halves, name):
    n = len(halves)

    def body(*refs):
        ins, outs = refs[:n], refs[n:2 * n]
        send_sems, recv_sems = refs[2 * n:]
        x, y, c = _coords()
        sends = []
        for k in range(n):
            h = halves[k].shape[0]
            rc = _remote(ins[k], outs[k].at[pl.ds(c * h, h)], send_sems.at[k], recv_sems.at[k], (x, y, 1 - c))
            rc.start()
            sends.append(rc)
        for k in range(n):
            h = halves[k].shape[0]
            landed = outs[k].at[pl.ds((1 - c) * h, h)]
            _remote(landed, landed, send_sems.at[k], recv_sems.at[k], (x, y, 1 - c)).wait_recv()
        for rc in sends:
            rc.wait_send()

    outs = _pcall(
        body, name=name, in_specs=[ANY] * n, out_specs=[ANY] * n,
        out_shape=[SDS((2 * a.shape[0], a.shape[1]), a.dtype) for a in halves],
        scratch_shapes=[pltpu.SemaphoreType.DMA((n,)), pltpu.SemaphoreType.DMA((n,))],
    )(*halves)
    ci = lax.axis_index("c")
    return [lax.dynamic_update_slice_in_dim(o, a, ci * a.shape[0], axis=0) for o, a in zip(outs, halves)]


def _pair_sum(a, got, cidx, name):
    _, r, cdim = a.shape
    h = r // 2
    tr = min(h, 256)
    nj = h // tr

    def body(c_ref, a_ref, g_ref, o_ref):
        o_ref[...] = (a_ref[...] + g_ref[...]).astype(BF16)

    blk = pl.BlockSpec((1, tr, cdim), lambda s, j, c: (s, j, 0))
    return _pcall(
        body, name=name, out_shape=SDS((4, h, cdim), BF16),
        grid_spec=pltpu.PrefetchScalarGridSpec(
            num_scalar_prefetch=1, grid=(4, nj),
            in_specs=[pl.BlockSpec((1, tr, cdim), lambda s, j, c: (s, c[0] * nj + j, 0)), blk], out_specs=blk),
        compiler_params=_cp(("parallel", "parallel")),
    )(cidx, a, got)


def _sum_chips(parts, name):
    _, h, cdim = parts.shape
    tr = min(h, 256)

    def body(p_ref, o_ref):
        acc = p_ref[0].astype(F32)
        for k in range(1, 4):
            acc = acc + p_ref[k].astype(F32)
        o_ref[...] = acc

    return _pcall(
        body, name=name, grid=(h // tr,), in_specs=[pl.BlockSpec((4, tr, cdim), lambda i: (0, i, 0))],
        out_specs=pl.BlockSpec((tr, cdim), lambda i: (i, 0)), out_shape=SDS((h, cdim), F32), compiler_params=_cp(("parallel",)),
    )(parts)


def _sum_slots(a, name, rows):
    s, n, _ = a.shape

    def body(a_ref, o_ref):
        acc = a_ref[0]
        for k in range(1, s):
            acc = acc + a_ref[k]
        o_ref[...] = acc

    return _pcall(
        body, name=name, grid=(n // rows,), in_specs=[pl.BlockSpec((s, rows, 128), lambda i: (0, i, 0))],
        out_specs=pl.BlockSpec((rows, 128), lambda i: (i, 0)), out_shape=SDS((n, 128), F32),
        compiler_params=_cp(("parallel",)),
    )(a)


def _adamw(w, g, m, v, name, rows):
    r, cdim = w.shape

    def body(w_ref, g_ref, m_ref, v_ref, d_ref, nm_ref, nv_ref):
        g_ = g_ref[...]
        nm = ADAM_B1 * m_ref[...] + (1.0 - ADAM_B1) * g_
        nv = ADAM_B2 * v_ref[...] + (1.0 - ADAM_B2) * (g_ * g_)
        m_hat = nm / (1.0 - ADAM_B1 ** ADAM_STEP)
        v_hat = nv / (1.0 - ADAM_B2 ** ADAM_STEP)
        d_ref[...] = -ADAM_LR * (m_hat / (jnp.sqrt(v_hat) + ADAM_EPS) + ADAM_WD * w_ref[...])
        nm_ref[...] = nm
        nv_ref[...] = nv

    blk = pl.BlockSpec((rows, cdim), lambda i: (i, 0))
    return _pcall(
        body, name=name, grid=(r // rows,), in_specs=[blk] * 4, out_specs=[blk] * 3,
        out_shape=[SDS(w.shape, F32)] * 3, compiler_params=_cp(("parallel",)),
    )(w, g, m, v)


def _pack(pieces, rows):
    flat = jnp.concatenate([p.reshape(-1) for p in pieces])
    return jnp.pad(flat, (0, rows * 128 - flat.shape[0])).reshape(rows, 128)


def _unpack(buf, shapes):
    flat = buf.reshape(-1)
    out, off = [], 0
    for shp in shapes:
        size = 1
        for s in shp:
            size *= s
        out.append(flat[off:off + size].reshape(shp))
        off += size
    return out


def _perm_cols(w):
    perm = jnp.concatenate([w[..., 3104:5152], w[..., 0:1024], w[..., 1056:1568], w[..., 1568:2080], w[..., 2592:3104],
                            w[..., 2080:2592]], axis=-1)
    return perm, w[..., 1024:1056]


def _unperm_cols(perm, lr32):
    return jnp.concatenate([perm[..., 2048:3072], lr32, perm[..., 3072:3584], perm[..., 3584:4096], perm[..., 4608:5120],
                            perm[..., 4096:4608], perm[..., 0:2048]], axis=-1)


SMALL_ROWS = 672
HALF_ROWS = 7200


def kernel(x, c, ctx, c_ctx, w_mod, b_mod, norm_g, w_in, a_ln_g, a_ln_b, a_ws, a_bs, b_gate_w2, b_gate_b, b_norm_g, w_proj_a, w_proj_b, w_out, final_norm_g, loss_target, m_c_ctx, m_w_mod, m_b_mod, m_norm_g, m_w_in, m_a_ln_g, m_a_ln_b, m_a_ws, m_a_bs, m_b_gate_w2, m_b_gate_b, m_b_norm_g, m_w_proj_a, m_w_proj_b, m_w_out, m_final_norm_g, v_c_ctx, v_w_mod, v_b_mod, v_norm_g, v_w_in, v_a_ln_g, v_a_ln_b, v_a_ws, v_a_bs, v_b_gate_w2, v_b_gate_b, v_b_norm_g, v_w_proj_a, v_w_proj_b, v_w_out, v_final_norm_g):
    xi, yi, ci = _coords()
    me_xy = 2 * xi + yi

    gate_pack = _pack([b_gate_w2[0], b_gate_b[0]], 24)
    g_wi, g_wm, g_wpa, g_wpb, g_wo, g_gate = _gather_weights(
        [w_in[0].astype(BF16), w_mod[0].astype(BF16), w_proj_a[0].astype(BF16), w_proj_b[0].astype(BF16),
         w_out[0].astype(BF16)], [gate_pack], "gather_weights")
    wi_full = jnp.swapaxes(g_wi, 0, 1).reshape(D, 4 * 1288)
    wi, wlr32 = _perm_cols(wi_full)
    wlr = jnp.pad(wlr32, ((0, 0), (0, LRW - 32)))
    wm = jnp.swapaxes(g_wm, 0, 1).reshape(D, 3 * D)
    wpa = jnp.swapaxes(g_wpa, 0, 1).reshape(512, D)
    wpb = jnp.swapaxes(g_wpb, 0, 1).reshape(512, D)
    wo = g_wo.reshape(D, D)
    gflat = g_gate.reshape(4, 24 * 128)
    w2 = jnp.swapaxes(gflat[:, 0:2048].reshape(4, 2, 16, 64), 0, 2)
    w2 = jnp.swapaxes(w2, 0, 1).reshape(2, 16, 256)
    gb2 = jnp.swapaxes(gflat[:, 2048:2176].reshape(4, 2, 64), 0, 1).reshape(2, 256)

    r = _device_step(x[0], c, ctx[0], c_ctx[None], loss_target[0], wm, b_mod, norm_g, wi, wlr, a_ln_g, a_ln_b, a_ws[0], a_bs[0],
                     w2, gb2, b_norm_g, wpa, wpb, wo, final_norm_g[None])

    small = _pack([r["dmod"], c, r["dmodc"], r["dscc"], r["dng"], r["dlng"], r["dlnb"], r["dws"], r["dbs"], r["dgbn"], r["dgf"],
                   r["dw2"], r["dgb2"], jnp.broadcast_to(r["loss"], (128,))], SMALL_ROWS)
    small_all = _gather_all(small, "gather_small")
    small_sum = _sum_slots(small_all, "sum_small", SMALL_ROWS // 4)
    (s_dmod, _, s_dmodc, s_dscc, s_dng, s_dlng, s_dlnb, s_dws, s_dbs, s_dgbn, s_dgf, s_dw2, s_dgb2, s_loss) = _unpack(
        small_sum, [(1, 3 * D), (1, D), (1, 2 * D), (D,), (1, D), (1, 512), (1, 512), (1, 4, 128, 128), (1, 4, 128), (1, 512),
                    (D,), (2, 16, 256), (2, 256), (128,)])
    loss = s_loss[0]
    s_dmodc_p = jnp.pad(s_dmodc, ((0, 0), (0, D)))
    g_b_mod = s_dmod + s_dmodc_p
    sg = jax.nn.sigmoid(c_ctx)
    g_c_ctx = s_dscc * (sg * (1.0 + c_ctx * (1.0 - sg)))
    g_w2 = lax.dynamic_slice_in_dim(s_dw2, 64 * me_xy, 64, axis=2)[None]
    g_gb2 = lax.dynamic_slice_in_dim(s_dgb2, 64 * me_xy, 64, axis=1)[None]

    flat_all = small_all.reshape(8, SMALL_ROWS * 128)
    dmod_all = flat_all[:, 0:3 * D]
    c_all = flat_all[:, 3 * D:4 * D]
    lhs = jnp.concatenate([_silu(c_all), _silu(c_ctx)[None], jnp.zeros((7, D), F32)], axis=0)
    rhs = jnp.concatenate([dmod_all, s_dmodc_p, jnp.zeros((7, 3 * D), F32)], axis=0)
    rhs = lax.dynamic_slice_in_dim(rhs, 768 * me_xy, 768, axis=1)
    g_w_mod = _mm(lhs.T.astype(BF16), rhs.astype(BF16), tm=D, tn=768, tk=16, out_dtype=F32, name="mm_dwm")

    dwr = r["dwi_r"]
    dwi_full = jnp.concatenate([r["dwi_qkv"], r["dwlr"][:, 0:32], dwr[:, 0:512], dwr[:, 512:1024], dwr[:, 1536:2048],
                                dwr[:, 1024:1536], r["dwi_g"]], axis=1)
    big = [jnp.swapaxes(dwi_full.reshape(D, 4, 1288), 0, 1), jnp.swapaxes(r["dwpa"].reshape(512, 4, 256), 0, 1),
           jnp.swapaxes(r["dwpb"].reshape(512, 4, 256), 0, 1), r["dwo"].reshape(4, 256, D)]
    tags = ["wi", "wpa", "wpb", "wo"]
    got = _swap_rows_c(big, "swap_half_in")
    cidx = jnp.reshape(ci, (1,)).astype(jnp.int32)
    pair = [_pair_sum(a, g, cidx, "sum_pair_" + t) for a, g, t in zip(big, got, tags)]
    parts = _a2a_xy(pair, "a2a_grads")
    halves = [_sum_chips(p_, "sum_chips_" + t) for p_, t in zip(parts, tags)]
    g_w_in, g_wpa, g_wpb, g_wo = _join_halves(halves, "swap_half_out")

    d_w_in, nm_w_in, nv_w_in = _adamw(w_in[0], g_w_in, m_w_in[0], v_w_in[0], "adamw_w_in", 256)
    d_w_mod, nm_w_mod, nv_w_mod = _adamw(w_mod[0], g_w_mod, m_w_mod[0], v_w_mod[0], "adamw_w_mod", 256)
    d_wpa, nm_wpa, nv_wpa = _adamw(w_proj_a[0], g_wpa, m_w_proj_a[0], v_w_proj_a[0], "adamw_wpa", 256)
    d_wpb, nm_wpb, nv_wpb = _adamw(w_proj_b[0], g_wpb, m_w_proj_b[0], v_w_proj_b[0], "adamw_wpb", 256)
    d_wo, nm_wo, nv_wo = _adamw(w_out[0], g_wo, m_w_out[0], v_w_out[0], "adamw_wo", 256)

    names = ["c_ctx", "b_mod", "norm_g", "a_ln_g", "a_ln_b", "a_ws", "a_bs", "b_gate_w2", "b_gate_b", "b_norm_g", "final_norm_g"]
    ws_ = [c_ctx, b_mod, norm_g, a_ln_g, a_ln_b, a_ws, a_bs, b_gate_w2, b_gate_b, b_norm_g, final_norm_g]
    gs_ = [g_c_ctx, g_b_mod, s_dng, s_dlng, s_dlnb, s_dws, s_dbs, g_w2, g_gb2, s_dgbn, s_dgf]
    ms_ = [m_c_ctx, m_b_mod, m_norm_g, m_a_ln_g, m_a_ln_b, m_a_ws, m_a_bs, m_b_gate_w2, m_b_gate_b, m_b_norm_g, m_final_norm_g]
    vs_ = [v_c_ctx, v_b_mod, v_norm_g, v_a_ln_g, v_a_ln_b, v_a_ws, v_a_bs, v_b_gate_w2, v_b_gate_b, v_b_norm_g, v_final_norm_g]
    shapes = [w.shape for w in ws_]
    gs_ = [g.reshape(s) for g, s in zip(gs_, shapes)]
    d_s, nm_s, nv_s = _adamw(_pack(ws_, 600), _pack(gs_, 600), _pack(ms_, 600), _pack(vs_, 600), "adamw_small", 600)
    d_small = dict(zip(names, _unpack(d_s, shapes)))
    nm_small = dict(zip(names, _unpack(nm_s, shapes)))
    nv_small = dict(zip(names, _unpack(nv_s, shapes)))
    g_small = dict(zip(names, gs_))

    order = ["c_ctx", "w_mod", "b_mod", "norm_g", "w_in", "a_ln_g", "a_ln_b", "a_ws", "a_bs", "b_gate_w2", "b_gate_b", "b_norm_g",
             "w_proj_a", "w_proj_b", "w_out", "final_norm_g"]
    big_g = dict(w_mod=g_w_mod[None], w_in=g_w_in[None], w_proj_a=g_wpa[None], w_proj_b=g_wpb[None], w_out=g_wo[None])
    big_d = dict(w_mod=d_w_mod[None], w_in=d_w_in[None], w_proj_a=d_wpa[None], w_proj_b=d_wpb[None], w_out=d_wo[None])
    big_m = dict(w_mod=nm_w_mod[None], w_in=nm_w_in[None], w_proj_a=nm_wpa[None], w_proj_b=nm_wpb[None], w_out=nm_wo[None])
    big_v = dict(w_mod=nv_w_mod[None], w_in=nv_w_in[None], w_proj_a=nv_wpa[None], w_proj_b=nv_wpb[None], w_out=nv_wo[None])
    grads = [big_g[n] if n in big_g else g_small[n] for n in order]
    deltas = [big_d[n] if n in big_d else d_small[n] for n in order]
    new_m = [big_m[n] if n in big_m else nm_small[n] for n in order]
    new_v = [big_v[n] if n in big_v else nv_small[n] for n in order]
    return (loss, r["dx"][None], *grads, *deltas, *new_m, *new_v)
```

```python
import functools

import jax
import jax.numpy as jnp
from jax import lax
from jax.experimental import pallas as pl
from jax.experimental.pallas import tpu as pltpu

F32 = jnp.float32
BF16 = jnp.bfloat16
SDS = jax.ShapeDtypeStruct

D = 1024
NP = 5120
LRW = 128
CH = 64
AC = 128
EPS = 1e-6
TOK = 256
GLA_TB = 256
VMEM_BIG = 48 * 1024 * 1024

ADAM_LR, ADAM_B1, ADAM_B2, ADAM_EPS, ADAM_WD, ADAM_STEP = 0.001, 0.9, 0.999, 1e-08, 0.01, 10

_pcall = pl.pallas_call
MESH = pl.DeviceIdType.MESH


def _cp(sem=None, vmem=None):
    kw = {}
    if sem is not None:
        kw["dimension_semantics"] = sem
    if vmem is not None:
        kw["vmem_limit_bytes"] = vmem
    return pltpu.CompilerParams(**kw)


def _silu(x):
    return x * jax.nn.sigmoid(x)


def _dsilu(x):
    s = jax.nn.sigmoid(x)
    return s * (1.0 + x * (1.0 - s))


def _logsig(x):
    return jnp.minimum(x, 0.0) - jnp.log1p(jnp.exp(-jnp.abs(x)))


def _nt(a, b):
    return lax.dot_general(a, b, (((1,), (1,)), ((), ())), preferred_element_type=F32)


def _tn(a, b):
    return lax.dot_general(a, b, (((0,), (0,)), ((), ())), preferred_element_type=F32)


def _nn(a, b):
    return jnp.dot(a, b, preferred_element_type=F32)


def _full(shape):
    return pl.BlockSpec(shape, lambda *_: (0,) * len(shape))


def _mm(a, b, *, tm, tn, tk, out_dtype, name, acc=None, n_outer=False):
    m, k = a.shape
    k2, n = b.shape
    assert k == k2 and m % tm == 0 and n % tn == 0 and k % tk == 0, (a.shape, b.shape, tm, tn, tk)
    nk = k // tk
    has_acc = acc is not None

    def body(*refs):
        if has_acc:
            a_ref, b_ref, c_ref, o_ref = refs[:4]
        else:
            a_ref, b_ref, o_ref = refs[:3]
        part = _nn(a_ref[...].astype(BF16), b_ref[...].astype(BF16))
        if nk == 1:
            o_ref[...] = ((c_ref[...] + part) if has_acc else part).astype(out_dtype)
            return
        acc_ref = refs[-1]
        kk = pl.program_id(2)

        @pl.when(kk == 0)
        def _():
            if has_acc:
                acc_ref[...] = c_ref[...] + part
            else:
                acc_ref[...] = part

        @pl.when(kk > 0)
        def _():
            acc_ref[...] += part

        @pl.when(kk == nk - 1)
        def _():
            o_ref[...] = acc_ref[...].astype(out_dtype)

    if n_outer:
        ij = lambda g0, g1: (g1, g0)
        grid = (n // tn, m // tm, nk)
    else:
        ij = lambda g0, g1: (g0, g1)
        grid = (m // tm, n // tn, nk)
    in_specs = [pl.BlockSpec((tm, tk), lambda g0, g1, kk: (ij(g0, g1)[0], kk)),
                pl.BlockSpec((tk, tn), lambda g0, g1, kk: (kk, ij(g0, g1)[1]))]
    args = [a, b]
    if has_acc:
        in_specs.append(pl.BlockSpec((tm, tn), lambda g0, g1, kk: ij(g0, g1)))
        args.append(acc)
    return _pcall(
        body, name=name, grid=grid, in_specs=in_specs,
        out_specs=pl.BlockSpec((tm, tn), lambda g0, g1, kk: ij(g0, g1)),
        out_shape=SDS((m, n), out_dtype), scratch_shapes=([pltpu.VMEM((tm, tn), F32)] if nk > 1 else []),
        compiler_params=_cp(("parallel", "parallel", "arbitrary"), VMEM_BIG),
    )(*args)


def _mm_tn(a, b, *, ta, tn, tk, name, acc=None):
    m, ka = a.shape
    m2, n = b.shape
    assert m == m2 and ka % ta == 0 and n % tn == 0 and m % tk == 0, (a.shape, b.shape, ta, tn, tk)
    nk = m // tk
    has_acc = acc is not None

    def body(*refs):
        if has_acc:
            a_ref, b_ref, c_ref, o_ref = refs
        else:
            a_ref, b_ref, o_ref = refs
        kk = pl.program_id(2)
        part = _tn(a_ref[...].astype(BF16), b_ref[...].astype(BF16))

        @pl.when(kk == 0)
        def _():
            if has_acc:
                o_ref[...] = c_ref[...] + part
            else:
                o_ref[...] = part

        @pl.when(kk > 0)
        def _():
            o_ref[...] += part

    in_specs = [pl.BlockSpec((tk, ta), lambda i, j, kk: (kk, i)), pl.BlockSpec((tk, tn), lambda i, j, kk: (kk, j))]
    args = [a, b]
    if has_acc:
        in_specs.append(pl.BlockSpec((ta, tn), lambda i, j, kk: (i, j)))
        args.append(acc)
    return _pcall(
        body, name=name, grid=(ka // ta, n // tn, nk), in_specs=in_specs,
        out_specs=pl.BlockSpec((ta, tn), lambda i, j, kk: (i, j)), out_shape=SDS((ka, n), F32),
        compiler_params=_cp(("parallel", "parallel", "arbitrary"), VMEM_BIG),
    )(*args)


def _modvec(cc, wm, bm):
    def body(c_ref, w_ref, b_ref, o_ref):
        o_ref[...] = _nn(_silu(c_ref[...]).astype(BF16), w_ref[...]) + b_ref[...]

    return _pcall(body, name="modvec", out_shape=SDS((8, 3 * D), F32), compiler_params=_cp(None, VMEM_BIG))(cc, wm, bm)


def _dcctx(dmodc, wm):
    def body(d_ref, w_ref, o_ref):
        o_ref[...] = _nt(d_ref[...].astype(BF16), w_ref[...])

    return _pcall(
        body, name="dcctx", grid=(1,), in_specs=[_full((8, 2 * D)), pl.BlockSpec((D, 2 * D), lambda i: (0, 0))],
        out_specs=_full((8, D)), out_shape=SDS((8, D), F32), compiler_params=_cp(("arbitrary",), VMEM_BIG),
    )(dmodc, wm)


def _prep_h(x, ng, scale, shift, name):
    m = x.shape[0]

    def body(x_ref, g_ref, sc_ref, sh_ref, h_ref):
        xf = x_ref[...]
        r = lax.rsqrt(jnp.mean(xf * xf, axis=-1, keepdims=True) + EPS)
        y = (xf * r) * g_ref[...]
        h_ref[...] = (y * (1.0 + sc_ref[...]) + sh_ref[...]).astype(BF16)

    row = pl.BlockSpec((TOK, D), lambda i: (i, 0))
    return _pcall(
        body, name=name, grid=(m // TOK,), in_specs=[row, _full((1, D)), _full((1, D)), _full((1, D))],
        out_specs=row, out_shape=SDS((m, D), BF16), compiler_params=_cp(("parallel",)),
    )(x, ng, scale, shift)


def _resident(shape):
    return pl.BlockSpec(shape, lambda *_: (0,) * len(shape), pipeline_mode=pl.Buffered(1))


PROJ_TM = 512


def _proj_fwd(x, ng, scale, shift, wi, wlr):
    m = x.shape[0]
    n = wi.shape[1]

    def body(x_ref, g_ref, sc_ref, sh_ref, wi_ref, wlr_ref, h_ref, p_ref, plr_ref):
        xf = x_ref[...]
        r = lax.rsqrt(jnp.mean(xf * xf, axis=-1, keepdims=True) + EPS)
        y = (xf * r) * g_ref[...]
        h = (y * (1.0 + sc_ref[...]) + sh_ref[...]).astype(BF16)
        h_ref[...] = h
        for j in range(n // D):
            p_ref[:, D * j:D * j + D] = _nn(h, wi_ref[:, D * j:D * j + D]).astype(BF16)
        plr_ref[...] = _nn(h, wlr_ref[...])

    row = pl.BlockSpec((PROJ_TM, D), lambda i: (i, 0))
    vec = _full((1, D))
    return _pcall(
        body, name="proj_fwd", grid=(m // PROJ_TM,),
        in_specs=[row, vec, vec, vec, _resident((D, n)), _resident((D, LRW))],
        out_specs=[row, pl.BlockSpec((PROJ_TM, n), lambda i: (i, 0)), pl.BlockSpec((PROJ_TM, LRW), lambda i: (i, 0))],
        out_shape=[SDS((m, D), BF16), SDS((m, n), BF16), SDS((m, LRW), F32)],
        compiler_params=_cp(("parallel",), VMEM_BIG),
    )(x, ng, scale, shift, wi, wlr)


def _proj_bwd(dp_g, dp_r, dlr, wit_g, wit_r, wlrt, x, dx1, ng, scale, send=()):
    m = x.shape[0]
    ns = len(send)
    steps = m // PROJ_TM

    def body(*refs):
        (dpg_ref, dpr_ref, dlr_ref, wg_ref, wr_ref, wl_ref, x_ref, r_ref, g_ref, sc_ref) = refs[:10]
        send_refs = refs[10:10 + ns]
        dx_ref, dg_ref, dsc_ref, dsh_ref = refs[10 + ns:14 + ns]
        got_refs = refs[14 + ns:14 + 2 * ns]
        sems = refs[14 + 2 * ns:]
        i = pl.program_id(0)

        def copies():
            cx, cy, cc = _coords()
            me = 2 * cx + cy
            peers = [(1 - cx, cy), (cx, 1 - cy), (1 - cx, 1 - cy)]
            out, back = [], []
            for k in range(ns):
                for j, (px, py) in enumerate(peers):
                    out.append(_remote(send_refs[k].at[2 * px + py], got_refs[k].at[me], sems[0].at[3 * k + j],
                                       sems[1].at[3 * k + j], (px, py, cc)))
                    landed = got_refs[k].at[2 * px + py]
                    back.append(_remote(landed, landed, sems[0].at[3 * k + j], sems[1].at[3 * k + j], (px, py, cc)))
            return out, back

        @pl.when(i == 0)
        def _():
            dg_ref[...] = jnp.zeros_like(dg_ref)
            dsc_ref[...] = jnp.zeros_like(dsc_ref)
            dsh_ref[...] = jnp.zeros_like(dsh_ref)
            if ns:
                for rc in copies()[0]:
                    rc.start()

        dh_ = (_nn(dpg_ref[...], wg_ref[...]) + _nn(dpr_ref[...], wr_ref[...])
               + _nn(dlr_ref[...].astype(BF16), wl_ref[...]))
        xf = x_ref[...]
        r = lax.rsqrt(jnp.mean(xf * xf, axis=-1, keepdims=True) + EPS)
        xh = xf * r
        y = xh * g_ref[...]
        dsh_ref[...] += jnp.sum(dh_, axis=0, keepdims=True)
        dsc_ref[...] += jnp.sum(dh_ * y, axis=0, keepdims=True)
        dy = dh_ * (1.0 + sc_ref[...])
        dg_ref[...] += jnp.sum(dy * xh, axis=0, keepdims=True)
        dxh = dy * g_ref[...]
        dx_ref[...] = r * (dxh - xh * jnp.mean(dxh * xh, axis=-1, keepdims=True)) + r_ref[...]

        if ns:
            @pl.when(i == steps - 1)
            def _():
                out, back = copies()
                for rc in back:
                    rc.wait_recv()
                for rc in out:
                    rc.wait_send()

    row = pl.BlockSpec((PROJ_TM, D), lambda i: (i, 0))
    vec = _full((1, D))
    kg, kr = dp_g.shape[1], dp_r.shape[1]
    res = _pcall(
        body, name="proj_bwd", grid=(steps,),
        in_specs=[pl.BlockSpec((PROJ_TM, kg), lambda i: (i, 0)), pl.BlockSpec((PROJ_TM, kr), lambda i: (i, 0)),
                  pl.BlockSpec((PROJ_TM, LRW), lambda i: (i, 0)), _resident((kg, D)), _resident((kr, D)), _resident((LRW, D)),
                  row, row, vec, vec] + [ANY] * ns,
        out_specs=[row, vec, vec, vec] + [ANY] * ns,
        out_shape=[SDS((m, D), F32), SDS((1, D), F32), SDS((1, D), F32), SDS((1, D), F32)] + [SDS(a.shape, a.dtype) for a in send],
        scratch_shapes=([pltpu.SemaphoreType.DMA((3 * ns,)), pltpu.SemaphoreType.DMA((3 * ns,))] if ns else []),
        compiler_params=_cp(("arbitrary",), VMEM_BIG),
    )(dp_g, dp_r, dlr, wit_g, wit_r, wlrt, x, dx1, ng, scale, *send)
    return tuple(res[:4]), list(res[4:])


def _prep_bwd(x, dh, dx1, ng, scale, name):
    m = x.shape[0]
    has_res = dx1 is not None

    def body(*refs):
        if has_res:
            x_ref, dh_ref, r_ref, g_ref, sc_ref, dx_ref, dg_ref, dsc_ref, dsh_ref = refs
        else:
            x_ref, dh_ref, g_ref, sc_ref, dx_ref, dg_ref, dsc_ref, dsh_ref = refs
        i = pl.program_id(0)

        @pl.when(i == 0)
        def _():
            dg_ref[...] = jnp.zeros_like(dg_ref)
            dsc_ref[...] = jnp.zeros_like(dsc_ref)
            dsh_ref[...] = jnp.zeros_like(dsh_ref)

        xf = x_ref[...]
        dh_ = dh_ref[...]
        r = lax.rsqrt(jnp.mean(xf * xf, axis=-1, keepdims=True) + EPS)
        xh = xf * r
        y = xh * g_ref[...]
        dsh_ref[...] += jnp.sum(dh_, axis=0, keepdims=True)
        dsc_ref[...] += jnp.sum(dh_ * y, axis=0, keepdims=True)
        dy = dh_ * (1.0 + sc_ref[...])
        dg_ref[...] += jnp.sum(dy * xh, axis=0, keepdims=True)
        dxh = dy * g_ref[...]
        dx = r * (dxh - xh * jnp.mean(dxh * xh, axis=-1, keepdims=True))
        if has_res:
            dx = dx + r_ref[...]
        dx_ref[...] = dx

    row = pl.BlockSpec((TOK, D), lambda i: (i, 0))
    vec = _full((1, D))
    in_specs = [row, row] + ([row] if has_res else []) + [vec, vec]
    args = [x, dh] + ([dx1] if has_res else []) + [ng, scale]
    return _pcall(
        body, name=name, grid=(m // TOK,), in_specs=in_specs, out_specs=[row, vec, vec, vec],
        out_shape=[SDS((m, D), F32), SDS((1, D), F32), SDS((1, D), F32), SDS((1, D), F32)],
        compiler_params=_cp(("arbitrary",)),
    )(*args)


def _ln_fwd(p, ln_g, ln_b):
    m = p.shape[0]

    def body(va_ref, g_ref, b_ref, vr_ref, vc_ref):
        xf = va_ref[...].astype(F32)
        xc = xf - jnp.mean(xf, axis=-1, keepdims=True)
        y = xc * lax.rsqrt(jnp.mean(xc * xc, axis=-1, keepdims=True) + EPS)
        vn = y * g_ref[...] + b_ref[...]
        vr_ref[...] = vn[:, 0:256].astype(BF16)
        vc_ref[0] = vn[:, 256:384].astype(BF16)
        vc_ref[1] = vn[:, 384:512].astype(BF16)

    return _pcall(
        body, name="ln_fwd", grid=(m // TOK,),
        in_specs=[pl.BlockSpec((TOK, 512), lambda i: (i, 9)), _full((1, 512)), _full((1, 512))],
        out_specs=[pl.BlockSpec((TOK, 256), lambda i: (i, 0)), pl.BlockSpec((2, TOK, 128), lambda i: (0, i, 0))],
        out_shape=[SDS((m, 256), BF16), SDS((2, m, 128), BF16)], compiler_params=_cp(("parallel",)),
    )(p, ln_g, ln_b)


COLB = 2048


def _colmix_fwd(vnc, ws23, bs23):
    rows = vnc.shape[2] // COLB

    def body(v_ref, w_ref, b_ref, o_ref):
        o_ref[0] = _nn(w_ref[0], v_ref[0]) + b_ref[0]

    return _pcall(
        body, name="colmix_fwd", grid=(2, rows),
        in_specs=[pl.BlockSpec((1, AC, COLB), lambda g, j: (g, 0, j)), pl.BlockSpec((1, AC, AC), lambda g, j: (g, 0, 0)),
                  pl.BlockSpec((1, AC, 1), lambda g, j: (g, 0, 0))],
        out_specs=pl.BlockSpec((1, AC, COLB), lambda g, j: (g, 0, j)),
        out_shape=SDS(vnc.shape, F32), compiler_params=_cp(("parallel", "parallel")),
    )(vnc, ws23, bs23)


def _colmix_bwd(dsvc, vnc, ws23t):
    rows = vnc.shape[2] // COLB

    def body(d_ref, v_ref, wt_ref, dv_ref, dw_ref, db_ref):
        j = pl.program_id(1)

        @pl.when(j == 0)
        def _():
            dw_ref[...] = jnp.zeros_like(dw_ref)
            db_ref[...] = jnp.zeros_like(db_ref)

        d = d_ref[0]
        d16 = d.astype(BF16)
        dv_ref[0] = _nn(wt_ref[0], d16)
        dw_ref[0] += _nt(d16, v_ref[0])
        db_ref[0] += jnp.sum(d, axis=1, keepdims=True)

    blk = pl.BlockSpec((1, AC, COLB), lambda g, j: (g, 0, j))
    return _pcall(
        body, name="colmix_bwd", grid=(2, rows),
        in_specs=[blk, blk, pl.BlockSpec((1, AC, AC), lambda g, j: (g, 0, 0))],
        out_specs=[blk, pl.BlockSpec((1, AC, AC), lambda g, j: (g, 0, 0)), pl.BlockSpec((1, AC, 1), lambda g, j: (g, 0, 0))],
        out_shape=[SDS(vnc.shape, F32), SDS((2, AC, AC), F32), SDS((2, AC, 1), F32)],
        compiler_params=_cp(("parallel", "arbitrary")),
    )(dsvc, vnc, ws23t)


def _head_norm(o, gbn):
    out = []
    for h in range(4):
        oh = o[:, 128 * h:128 * h + 128]
        r = lax.rsqrt(jnp.mean(oh * oh, axis=-1, keepdims=True) + EPS)
        out.append((r, oh * r))
    return out


def _mid_fwd(o_f, o_b, p, vnr, svc, ws01, bs01, gbn):
    m = p.shape[0]

    def body(of_ref, ob_ref, zb_ref, ua_ref, za_ref, vnr_ref, svc_ref, w_ref, b_ref, g_ref, ya_ref, yb_ref, svr_ref):
        o = of_ref[...] + ob_ref[...]
        zb = zb_ref[...]
        parts = []
        for h, (r, xh) in enumerate(_head_norm(o, None)):
            parts.append(xh * g_ref[:, 128 * h:128 * h + 128])
        on = jnp.concatenate(parts, axis=1)
        yb_ref[...] = (on * _silu(zb)).astype(BF16)
        for j in range(TOK // AC):
            for g in range(2):
                sv = _nn(w_ref[g], vnr_ref[AC * j:AC * j + AC, AC * g:AC * g + AC]) + b_ref[g]
                svr_ref[AC * j:AC * j + AC, AC * g:AC * g + AC] = sv
        sz = _silu(za_ref[...])
        u = ua_ref[...]
        ya_ref[:, 0:256] = ((u[:, 0:256] * svr_ref[...]) * sz[:, 0:256]).astype(BF16)
        ya_ref[:, 256:384] = ((u[:, 256:384] * svc_ref[0]) * sz[:, 256:384]).astype(BF16)
        ya_ref[:, 384:512] = ((u[:, 384:512] * svc_ref[1]) * sz[:, 384:512]).astype(BF16)

    r512 = pl.BlockSpec((TOK, 512), lambda i: (i, 0))
    return _pcall(
        body, name="mid_fwd", grid=(m // TOK,),
        in_specs=[r512, r512, pl.BlockSpec((TOK, 512), lambda i: (i, 6)), pl.BlockSpec((TOK, 512), lambda i: (i, 7)),
                  pl.BlockSpec((TOK, 512), lambda i: (i, 8)), pl.BlockSpec((TOK, 256), lambda i: (i, 0)),
                  pl.BlockSpec((2, TOK, 128), lambda i: (0, i, 0)), _full((2, AC, AC)), _full((2, AC, 1)), _full((1, 512))],
        out_specs=[r512, r512, pl.BlockSpec((TOK, 256), lambda i: (i, 0))],
        out_shape=[SDS((m, 512), BF16), SDS((m, 512), BF16), SDS((m, 256), F32)],
        compiler_params=_cp(("parallel",)),
    )(o_f, o_b, p, p, p, vnr, svc, ws01, bs01, gbn)


def _merge_fwd(p, ya, yb):
    m = p.shape[0]

    def body(ga_ref, gb_ref, ya_ref, yb_ref, m_ref):
        m_ref[...] = (jax.nn.sigmoid(ga_ref[...]) * ya_ref[...] + jax.nn.sigmoid(gb_ref[...]) * yb_ref[...]).astype(BF16)

    row = pl.BlockSpec((TOK, D), lambda i: (i, 0))
    return _pcall(
        body, name="merge_fwd", grid=(m // TOK,),
        in_specs=[row, pl.BlockSpec((TOK, D), lambda i: (i, 1)), row, row], out_specs=row,
        out_shape=SDS((m, D), BF16), compiler_params=_cp(("parallel",)),
    )(p, p, ya, yb)


def _loss_head(x, out, tgt, gate, gf):
    m = x.shape[0]

    def body(x_ref, o_ref, t_ref, gate_ref, gf_ref, dx1_ref, dout_ref, loss_ref, dgate_ref, dgf_ref):
        i = pl.program_id(0)

        @pl.when(i == 0)
        def _():
            loss_ref[...] = jnp.zeros_like(loss_ref)
            dgate_ref[...] = jnp.zeros_like(dgate_ref)
            dgf_ref[...] = jnp.zeros_like(dgf_ref)

        out_ = o_ref[...]
        x1 = x_ref[...] + gate_ref[...] * out_
        r = lax.rsqrt(jnp.mean(x1 * x1, axis=-1, keepdims=True) + EPS)
        xh = x1 * r
        err = xh * gf_ref[...] - t_ref[...]
        loss_ref[...] += 0.5 * jnp.sum(jnp.mean(err * err, axis=-1, keepdims=True), axis=0, keepdims=True)
        dy = err * (1.0 / D)
        dgf_ref[...] += jnp.sum(dy * xh, axis=0, keepdims=True)
        dxh = dy * gf_ref[...]
        dx1 = r * (dxh - xh * jnp.mean(dxh * xh, axis=-1, keepdims=True))
        dx1_ref[...] = dx1
        dout_ref[...] = (gate_ref[...] * dx1).astype(BF16)
        dgate_ref[...] += jnp.sum(dx1 * out_, axis=0, keepdims=True)

    row = pl.BlockSpec((TOK, D), lambda i: (i, 0))
    vec = _full((1, D))
    return _pcall(
        body, name="loss_head", grid=(m // TOK,), in_specs=[row, row, row, vec, vec],
        out_specs=[row, row, _full((1, 128)), vec, vec],
        out_shape=[SDS((m, D), F32), SDS((m, D), BF16), SDS((1, 128), F32), SDS((1, D), F32), SDS((1, D), F32)],
        compiler_params=_cp(("arbitrary",)),
    )(x, out, tgt, gate, gf)


def _merge_bwd(dm, ya, yb, p):
    m = p.shape[0]

    def body(dm_ref, ya_ref, yb_ref, ga_ref, gb_ref, dya_ref, dyb_ref, dp_ref):
        dm_ = dm_ref[...]
        sa = jax.nn.sigmoid(ga_ref[...])
        sb = jax.nn.sigmoid(gb_ref[...])
        dya_ref[...] = (dm_ * sa).astype(BF16)
        dyb_ref[...] = (dm_ * sb).astype(BF16)
        dp_ref[:, 0:D] = (dm_ * ya_ref[...] * (sa * (1.0 - sa))).astype(BF16)
        dp_ref[:, D:2 * D] = (dm_ * yb_ref[...] * (sb * (1.0 - sb))).astype(BF16)

    row = pl.BlockSpec((TOK, D), lambda i: (i, 0))
    return _pcall(
        body, name="merge_bwd", grid=(m // TOK,),
        in_specs=[row, row, row, row, pl.BlockSpec((TOK, D), lambda i: (i, 1))],
        out_specs=[row, row, pl.BlockSpec((TOK, 2 * D), lambda i: (i, 0))],
        out_shape=[SDS((m, D), BF16), SDS((m, D), BF16), SDS((m, NP), BF16)],
        compiler_params=_cp(("parallel",)),
    )(dm, ya, yb, p, p)


def _mid_bwd(dya_in, dyb_in, p, svr, svc, o_f, o_b, gbn, dp):
    m = p.shape[0]

    def body(dya_ref, dyb_ref, zb_ref, ua_ref, za_ref, svr_ref, svc_ref, of_ref, ob_ref, g_ref, dpi_ref,
             dp_ref, dsr_ref, dsc_ref, do_ref, dg_ref):
        i = pl.program_id(0)

        @pl.when(i == 0)
        def _():
            dg_ref[...] = jnp.zeros_like(dg_ref)

        dya = dya_ref[...]
        u = ua_ref[...]
        za = za_ref[...]
        sz = _silu(za)
        sv = jnp.concatenate([svr_ref[...], svc_ref[0], svc_ref[1]], axis=1)
        dp_ref[:, 512:1024] = (dya * sv * sz).astype(BF16)
        dsv = dya * u * sz
        dsr_ref[...] = dsv[:, 0:256]
        dsc_ref[0] = dsv[:, 256:384]
        dsc_ref[1] = dsv[:, 384:512]
        dp_ref[:, 1024:1536] = (dya * u * sv * _dsilu(za)).astype(BF16)

        dyb = dyb_ref[...]
        zb = zb_ref[...]
        o = of_ref[...] + ob_ref[...]
        szb = _silu(zb)
        dszb = _dsilu(zb)
        for h, (r, xh) in enumerate(_head_norm(o, None)):
            sl = slice(128 * h, 128 * h + 128)
            gh = g_ref[:, sl]
            don = dyb[:, sl] * szb[:, sl]
            dp_ref[:, sl] = (dyb[:, sl] * (xh * gh) * dszb[:, sl]).astype(BF16)
            dg_ref[:, sl] += jnp.sum(don * xh, axis=0, keepdims=True)
            dxh = don * gh
            do_ref[:, sl] = r * (dxh - xh * jnp.mean(dxh * xh, axis=-1, keepdims=True))

    r512 = pl.BlockSpec((TOK, 512), lambda i: (i, 0))
    return _pcall(
        body, name="mid_bwd", grid=(m // TOK,),
        in_specs=[r512, r512, pl.BlockSpec((TOK, 512), lambda i: (i, 6)), pl.BlockSpec((TOK, 512), lambda i: (i, 7)),
                  pl.BlockSpec((TOK, 512), lambda i: (i, 8)), pl.BlockSpec((TOK, 256), lambda i: (i, 0)),
                  pl.BlockSpec((2, TOK, 128), lambda i: (0, i, 0)), r512, r512, _full((1, 512)),
                  pl.BlockSpec(memory_space=pl.ANY)],
        out_specs=[pl.BlockSpec((TOK, 1536), lambda i: (i, 2)), pl.BlockSpec((TOK, 256), lambda i: (i, 0)),
                   pl.BlockSpec((2, TOK, 128), lambda i: (0, i, 0)), r512, _full((1, 512))],
        out_shape=[SDS((m, NP), BF16), SDS((m, 256), F32), SDS((2, m, 128), F32), SDS((m, 512), F32), SDS((1, 512), F32)],
        input_output_aliases={10: 0}, compiler_params=_cp(("arbitrary",)),
    )(dya_in, dyb_in, p, p, p, svr, svc, o_f, o_b, gbn, dp)


def _tail_fwd(o_f, o_b, p, vnr, svc, x, tgt, ws01, bs01, gbn, wpa, wpb, wo, gate, gf):
    m = p.shape[0]

    def body(of_ref, ob_ref, zb_ref, ua_ref, za_ref, ga_ref, gb_ref, vnr_ref, svc_ref, x_ref, t_ref, w_ref, b_ref, g_ref,
             wpa_ref, wpb_ref, wo_ref, gate_ref, gf_ref,
             ya_ref, yb_ref, svr_ref, m_ref, dx1_ref, dout_ref, loss_ref, dgate_ref, dgf_ref):
        i = pl.program_id(0)

        @pl.when(i == 0)
        def _():
            loss_ref[...] = jnp.zeros_like(loss_ref)
            dgate_ref[...] = jnp.zeros_like(dgate_ref)
            dgf_ref[...] = jnp.zeros_like(dgf_ref)

        o = of_ref[...] + ob_ref[...]
        zb = zb_ref[...].astype(F32)
        for h, (r, xh) in enumerate(_head_norm(o, None)):
            sl = slice(128 * h, 128 * h + 128)
            yb_ref[:, sl] = ((xh * g_ref[:, sl]) * _silu(zb[:, sl])).astype(BF16)
        for j in range(TOK // AC):
            for g in range(2):
                sv = _nn(w_ref[g], vnr_ref[AC * j:AC * j + AC, AC * g:AC * g + AC]) + b_ref[g]
                svr_ref[AC * j:AC * j + AC, AC * g:AC * g + AC] = sv
        sz = _silu(za_ref[...].astype(F32))
        u = ua_ref[...].astype(F32)
        ya_ref[:, 0:256] = ((u[:, 0:256] * svr_ref[...]) * sz[:, 0:256]).astype(BF16)
        ya_ref[:, 256:384] = ((u[:, 256:384] * svc_ref[0]) * sz[:, 256:384]).astype(BF16)
        ya_ref[:, 384:512] = ((u[:, 384:512] * svc_ref[1]) * sz[:, 384:512]).astype(BF16)
        ya = _nn(ya_ref[...], wpa_ref[...])
        yb = _nn(yb_ref[...], wpb_ref[...])
        mg = (jax.nn.sigmoid(ga_ref[...].astype(F32)) * ya + jax.nn.sigmoid(gb_ref[...].astype(F32)) * yb).astype(BF16)
        m_ref[...] = mg
        out_ = _nn(mg, wo_ref[...])
        x1 = x_ref[...] + gate_ref[...] * out_
        r = lax.rsqrt(jnp.mean(x1 * x1, axis=-1, keepdims=True) + EPS)
        xh = x1 * r
        err = xh * gf_ref[...] - t_ref[...]
        loss_ref[...] += 0.5 * jnp.sum(jnp.mean(err * err, axis=-1, keepdims=True), axis=0, keepdims=True)
        dy = err * (1.0 / D)
        dgf_ref[...] += jnp.sum(dy * xh, axis=0, keepdims=True)
        dxh = dy * gf_ref[...]
        dx1 = r * (dxh - xh * jnp.mean(dxh * xh, axis=-1, keepdims=True))
        dx1_ref[...] = dx1
        dout_ref[...] = (gate_ref[...] * dx1).astype(BF16)
        dgate_ref[...] += jnp.sum(dx1 * out_, axis=0, keepdims=True)

    r512 = pl.BlockSpec((TOK, 512), lambda i: (i, 0))
    row = pl.BlockSpec((TOK, D), lambda i: (i, 0))
    vec = _full((1, D))
    return _pcall(
        body, name="tail_fwd", grid=(m // TOK,),
        in_specs=[r512, r512, pl.BlockSpec((TOK, 512), lambda i: (i, 6)), pl.BlockSpec((TOK, 512), lambda i: (i, 7)),
                  pl.BlockSpec((TOK, 512), lambda i: (i, 8)), row, pl.BlockSpec((TOK, D), lambda i: (i, 1)),
                  pl.BlockSpec((TOK, 256), lambda i: (i, 0)), pl.BlockSpec((2, TOK, 128), lambda i: (0, i, 0)), row, row,
                  _full((2, AC, AC)), _full((2, AC, 1)), _full((1, 512)), _full((512, D)), _full((512, D)), _full((D, D)), vec, vec],
        out_specs=[r512, r512, pl.BlockSpec((TOK, 256), lambda i: (i, 0)), row, row, row, _full((1, 128)), vec, vec],
        out_shape=[SDS((m, 512), BF16), SDS((m, 512), BF16), SDS((m, 256), F32), SDS((m, D), BF16), SDS((m, D), F32),
                   SDS((m, D), BF16), SDS((1, 128), F32), SDS((1, D), F32), SDS((1, D), F32)],
        compiler_params=_cp(("arbitrary",), VMEM_BIG),
    )(o_f, o_b, p, p, p, p, p, vnr, svc, x, tgt, ws01, bs01, gbn, wpa, wpb, wo, gate, gf)


DPR = 3072


def _tail_bwd(dout, ya_in, yb_in, p, svr, svc, o_f, o_b, gbn, wot, wpa, wpb, wpat, wpbt):
    m = p.shape[0]

    def body(dout_ref, ya_ref, yb_ref, ga_ref, gb_ref, zb_ref, ua_ref, za_ref, svr_ref, svc_ref, of_ref, ob_ref, g_ref,
             wot_ref, wpa_ref, wpb_ref, wpat_ref, wpbt_ref,
             dya_ref, dyb_ref, dpg_ref, dpr_ref, dsr_ref, dsc_ref, do_ref, dg_ref):
        i = pl.program_id(0)

        @pl.when(i == 0)
        def _():
            dg_ref[...] = jnp.zeros_like(dg_ref)

        dm_ = _nn(dout_ref[...], wot_ref[...])
        ya = _nn(ya_ref[...], wpa_ref[...])
        yb = _nn(yb_ref[...], wpb_ref[...])
        sa = jax.nn.sigmoid(ga_ref[...].astype(F32))
        sb = jax.nn.sigmoid(gb_ref[...].astype(F32))
        dya16 = (dm_ * sa).astype(BF16)
        dyb16 = (dm_ * sb).astype(BF16)
        dya_ref[...] = dya16
        dyb_ref[...] = dyb16
        dpg_ref[:, 0:D] = (dm_ * ya * (sa * (1.0 - sa))).astype(BF16)
        dpg_ref[:, D:2 * D] = (dm_ * yb * (sb * (1.0 - sb))).astype(BF16)
        dya = _nn(dya16, wpat_ref[...])
        dyb = _nn(dyb16, wpbt_ref[...])

        u = ua_ref[...].astype(F32)
        za = za_ref[...].astype(F32)
        sz = _silu(za)
        sv = jnp.concatenate([svr_ref[...], svc_ref[0], svc_ref[1]], axis=1)
        dpr_ref[:, 512:1024] = (dya * sv * sz).astype(BF16)
        dsv = dya * u * sz
        dsr_ref[...] = dsv[:, 0:256]
        dsc_ref[0] = dsv[:, 256:384]
        dsc_ref[1] = dsv[:, 384:512]
        dpr_ref[:, 1024:1536] = (dya * u * sv * _dsilu(za)).astype(BF16)

        zb = zb_ref[...].astype(F32)
        o = of_ref[...] + ob_ref[...]
        szb = _silu(zb)
        dszb = _dsilu(zb)
        for h, (r, xh) in enumerate(_head_norm(o, None)):
            sl = slice(128 * h, 128 * h + 128)
            gh = g_ref[:, sl]
            don = dyb[:, sl] * szb[:, sl]
            dpr_ref[:, sl] = (dyb[:, sl] * (xh * gh) * dszb[:, sl]).astype(BF16)
            dg_ref[:, sl] += jnp.sum(don * xh, axis=0, keepdims=True)
            dxh = don * gh
            do_ref[:, sl] = r * (dxh - xh * jnp.mean(dxh * xh, axis=-1, keepdims=True))

    r512 = pl.BlockSpec((TOK, 512), lambda i: (i, 0))
    row = pl.BlockSpec((TOK, D), lambda i: (i, 0))
    return _pcall(
        body, name="tail_bwd", grid=(m // TOK,),
        in_specs=[row, r512, r512, row, pl.BlockSpec((TOK, D), lambda i: (i, 1)), pl.BlockSpec((TOK, 512), lambda i: (i, 6)),
                  pl.BlockSpec((TOK, 512), lambda i: (i, 7)), pl.BlockSpec((TOK, 512), lambda i: (i, 8)),
                  pl.BlockSpec((TOK, 256), lambda i: (i, 0)), pl.BlockSpec((2, TOK, 128), lambda i: (0, i, 0)), r512, r512,
                  _full((1, 512)), _full((D, D)), _full((512, D)), _full((512, D)), _full((D, 512)), _full((D, 512))],
        out_specs=[row, row, pl.BlockSpec((TOK, 2 * D), lambda i: (i, 0)), pl.BlockSpec((TOK, 1536), lambda i: (i, 0)),
                   pl.BlockSpec((TOK, 256), lambda i: (i, 0)), pl.BlockSpec((2, TOK, 128), lambda i: (0, i, 0)), r512, _full((1, 512))],
        out_shape=[SDS((m, D), BF16), SDS((m, D), BF16), SDS((m, 2 * D), BF16), SDS((m, DPR), BF16), SDS((m, 256), F32),
                   SDS((2, m, 128), F32), SDS((m, 512), F32), SDS((1, 512), F32)],
        compiler_params=_cp(("arbitrary",), VMEM_BIG),
    )(dout, ya_in, yb_in, p, p, p, p, p, svr, svc, o_f, o_b, gbn, wot, wpa, wpb, wpat, wpbt)


def _mm_multi(pairs, *, tm, tn, out_dtype, name):
    m = pairs[0][0].shape[0]
    n = pairs[0][1].shape[1]
    nks = [a.shape[1] // tk for a, _, tk in pairs]
    starts = [sum(nks[:i]) for i in range(len(pairs))]
    total = sum(nks)

    def body(*refs):
        o_ref, acc_ref = refs[-2], refs[-1]
        kk = pl.program_id(2)
        for idx in range(len(pairs)):
            a_ref, b_ref = refs[2 * idx], refs[2 * idx + 1]

            @pl.when((kk >= starts[idx]) & (kk < starts[idx] + nks[idx]))
            def _(a_ref=a_ref, b_ref=b_ref, first=(idx == 0)):
                part = _nn(a_ref[...].astype(BF16), b_ref[...].astype(BF16))
                if first:
                    @pl.when(kk == 0)
                    def _():
                        acc_ref[...] = part

                    @pl.when(kk > 0)
                    def _():
                        acc_ref[...] += part
                else:
                    acc_ref[...] += part

        @pl.when(kk == total - 1)
        def _():
            o_ref[...] = acc_ref[...].astype(out_dtype)

    in_specs, args = [], []
    for (a, b, tk), st, nk in zip(pairs, starts, nks):
        in_specs.append(pl.BlockSpec((tm, tk), lambda i, j, kk, st=st, nk=nk: (i, jnp.clip(kk - st, 0, nk - 1))))
        in_specs.append(pl.BlockSpec((tk, tn), lambda i, j, kk, st=st, nk=nk: (jnp.clip(kk - st, 0, nk - 1), j)))
        args += [a, b]
    return _pcall(
        body, name=name, grid=(m // tm, n // tn, total), in_specs=in_specs,
        out_specs=pl.BlockSpec((tm, tn), lambda i, j, kk: (i, j)), out_shape=SDS((m, n), out_dtype),
        scratch_shapes=[pltpu.VMEM((tm, tn), F32)], compiler_params=_cp(("parallel", "parallel", "arbitrary"), VMEM_BIG),
    )(*args)


def _ln_bwd(dsr, vnr, dvnc, p, ws01t, ln_g, dp):
    m = p.shape[0]

    def body(dsr_ref, vnr_ref, dvc_ref, va_ref, wt_ref, g_ref, dpi_ref, dp_ref, dw_ref, db_ref, dlg_ref, dlb_ref, dvn_ref):
        i = pl.program_id(0)

        @pl.when(i == 0)
        def _():
            dw_ref[...] = jnp.zeros_like(dw_ref)
            db_ref[...] = jnp.zeros_like(db_ref)
            dlg_ref[...] = jnp.zeros_like(dlg_ref)
            dlb_ref[...] = jnp.zeros_like(dlb_ref)

        for j in range(TOK // AC):
            for g in range(2):
                d = dsr_ref[AC * j:AC * j + AC, AC * g:AC * g + AC]
                d16 = d.astype(BF16)
                dvn_ref[AC * j:AC * j + AC, AC * g:AC * g + AC] = _nn(wt_ref[g], d16)
                dw_ref[g] += _nt(d16, vnr_ref[AC * j:AC * j + AC, AC * g:AC * g + AC])
                db_ref[g] += jnp.sum(d, axis=1, keepdims=True)
        dvn_ref[:, 256:384] = dvc_ref[0]
        dvn_ref[:, 384:512] = dvc_ref[1]
        dvn = dvn_ref[...]
        xf = va_ref[...].astype(F32)
        xc = xf - jnp.mean(xf, axis=-1, keepdims=True)
        rs = lax.rsqrt(jnp.mean(xc * xc, axis=-1, keepdims=True) + EPS)
        xh = xc * rs
        dlg_ref[...] += jnp.sum(dvn * xh, axis=0, keepdims=True)
        dlb_ref[...] += jnp.sum(dvn, axis=0, keepdims=True)
        dxh = dvn * g_ref[...]
        dva = rs * (dxh - jnp.mean(dxh, axis=-1, keepdims=True) - xh * jnp.mean(dxh * xh, axis=-1, keepdims=True))
        dp_ref[...] = dva.astype(BF16)

    return _pcall(
        body, name="ln_bwd", grid=(m // TOK,),
        in_specs=[pl.BlockSpec((TOK, 256), lambda i: (i, 0)), pl.BlockSpec((TOK, 256), lambda i: (i, 0)),
                  pl.BlockSpec((2, TOK, 128), lambda i: (0, i, 0)), pl.BlockSpec((TOK, 512), lambda i: (i, 9)),
                  _full((2, AC, AC)), _full((1, 512)), pl.BlockSpec(memory_space=pl.ANY)],
        out_specs=[pl.BlockSpec((TOK, 512), lambda i: (i, 3)), _full((2, AC, AC)), _full((2, AC, 1)), _full((1, 512)), _full((1, 512))],
        out_shape=[SDS((m, DPR), BF16), SDS((2, AC, AC), F32), SDS((2, AC, 1), F32), SDS((1, 512), F32), SDS((1, 512), F32)],
        scratch_shapes=[pltpu.VMEM((TOK, 512), F32)],
        input_output_aliases={6: 0}, compiler_params=_cp(("arbitrary",)),
    )(dsr, vnr, dvnc, p, ws01t, ln_g, dp)


def _tri_mm(tri, a):
    a1 = a.astype(BF16)
    r1 = a - a1.astype(F32)
    a2 = r1.astype(BF16)
    a3 = (r1 - a2.astype(F32)).astype(BF16)
    n = a.shape[1]
    r = _nn(tri, jnp.concatenate([a1, a2, a3], axis=1))
    return r[:, 0:n] + r[:, n:2 * n] + r[:, 2 * n:3 * n]


def _gla_masks(reverse):
    ri = lax.broadcasted_iota(jnp.int32, (CH, CH), 0)
    ci = lax.broadcasted_iota(jnp.int32, (CH, CH), 1)
    vis = (ci >= ri) if reverse else (ci <= ri)
    vis_t = (ci <= ri) if reverse else (ci >= ri)
    r4 = lax.broadcasted_iota(jnp.int32, (4 * CH, CH), 0) & (CH - 1)
    c4 = lax.broadcasted_iota(jnp.int32, (4 * CH, CH), 1)
    vis4 = (c4 >= r4) if reverse else (c4 <= r4)
    vis4_t = (c4 <= r4) if reverse else (c4 >= r4)
    lane = lax.broadcasted_iota(jnp.int32, (1, 256), 1)
    hm = [(lane >= CH * h) & (lane < CH * h + CH) for h in range(4)]
    return vis, vis_t, vis4, vis4_t, hm


def _stack_heads(x, hm):
    return jnp.concatenate([jnp.where(hm[h], x, 0.0).astype(BF16) for h in range(4)], axis=0)


def _diag_heads(full, hm):
    r = full.shape[0] // 4
    acc = jnp.where(hm[0], full[0:r], 0.0)
    for h in range(1, 4):
        acc = acc + jnp.where(hm[h], full[r * h:r * h + r], 0.0)
    return acc


def _rows_of_heads(x):
    return jnp.concatenate([x[:, 128 * h:128 * h + 128] for h in range(4)], axis=0)


def _lane_vis(reverse, transpose):
    ri = lax.broadcasted_iota(jnp.int32, (CH, 4 * CH), 0)
    ci = lax.broadcasted_iota(jnp.int32, (CH, 4 * CH), 1) & (CH - 1)
    return (ci >= ri) if (reverse != transpose) else (ci <= ri)


def _gla_fwd(p, qkv_blk, lr, lrw, gbias, s0, *, reverse, name):
    m = p.shape[0]
    nb = m // GLA_TB
    nc = GLA_TB // CH
    rmap = (lambda i: nb - 1 - i) if reverse else (lambda i: i)

    def body(qkv_ref, lr_ref, lrw_ref, gb_ref, s0_ref, o_ref, sb_ref, sfin_ref, st_ref):
        i = pl.program_id(0)

        @pl.when(i == 0)
        def _():
            st_ref[...] = s0_ref[...]

        vis, _, vis4, _, hm = _gla_masks(reverse)
        tri = vis.astype(F32).astype(BF16)
        logits = _nn(lr_ref[...].astype(BF16), lrw_ref[...]) + gb_ref[...]
        a_all = _logsig(logits) * (1.0 / 16.0)
        st = st_ref[...]
        for c in (range(nc - 1, -1, -1) if reverse else range(nc)):
            rows = slice(CH * c, CH * c + CH)
            b = _tri_mm(tri, a_all[rows])
            bl = b[0:1] if reverse else b[CH - 1:CH]
            q = qkv_ref[rows, 0:256].astype(F32) * 0.125
            k = qkv_ref[rows, 256:512].astype(F32)
            v16 = qkv_ref[rows, 512:1024].astype(BF16)
            qd = q * jnp.exp(b)
            kd16 = (k * jnp.exp(-b)).astype(BF16)
            kdec16 = (k * jnp.exp(bl - b)).astype(BF16)
            qstack = _stack_heads(qd, hm)
            sc = jnp.where(vis4, _nt(qstack, kd16), 0.0).astype(BF16)
            inter = _nt(qstack, st.astype(BF16))
            for h in range(4):
                o_ref[rows, 128 * h:128 * h + 128] = (
                    _nn(sc[CH * h:CH * h + CH], v16[:, 128 * h:128 * h + 128]) + inter[CH * h:CH * h + CH])
            sb_ref[c] = st
            st = st * jnp.exp(bl) + _diag_heads(_tn(v16, kdec16), hm)
        st_ref[...] = st

        @pl.when(i == nb - 1)
        def _():
            sfin_ref[...] = st

    return _pcall(
        body, name=name, grid=(nb,),
        in_specs=[pl.BlockSpec((GLA_TB, 1024), lambda i: (rmap(i), qkv_blk)), pl.BlockSpec((GLA_TB, LRW), lambda i: (rmap(i), 0)),
                  _full((LRW, 256)), _full((1, 256)), _full((128, 256))],
        out_specs=[pl.BlockSpec((GLA_TB, 512), lambda i: (rmap(i), 0)), pl.BlockSpec((nc, 128, 256), lambda i: (rmap(i), 0, 0)),
                   _full((128, 256))],
        out_shape=[SDS((m, 512), F32), SDS((m // CH, 128, 256), F32), SDS((128, 256), F32)],
        scratch_shapes=[pltpu.VMEM((128, 256), F32)], compiler_params=_cp(("arbitrary",)),
    )(p, lr, lrw, gbias, s0)


def _gla_bwd(p, qkv_blk, lr, lrw, lrwt, gbias, sb, dsfin, do, prev, dp, *, reverse, name):
    m = p.shape[0]
    nb = m // GLA_TB
    nc = GLA_TB // CH
    rmap = (lambda i: i) if reverse else (lambda i: nb - 1 - i)
    has_prev = prev is not None
    has_dp = dp is not None

    def body(*refs):
        refs = list(refs)
        qkv_ref, lr_ref, lrw_ref, lrwt_ref, gb_ref, sb_ref, dsfin_ref, do_ref = refs[:8]
        refs = refs[8:]
        if has_prev:
            pq_ref, plr_ref = refs[:2]
            refs = refs[2:]
        if has_dp:
            refs = refs[1:]
        dqkv_ref, dlr_ref, dw2_ref, dgb_ref, ds0_ref, dst_ref, dlog_ref = refs
        i = pl.program_id(0)

        @pl.when(i == 0)
        def _():
            dst_ref[...] = dsfin_ref[...]
            dw2_ref[...] = jnp.zeros_like(dw2_ref)
            dgb_ref[...] = jnp.zeros_like(dgb_ref)

        vis, vis_t, vis4, vis4_t, hm = _gla_masks(reverse)
        tri = vis.astype(F32).astype(BF16)
        tri_t = vis_t.astype(F32).astype(BF16)
        lane_vis = _lane_vis(reverse, False)
        lane_vis_t = _lane_vis(reverse, True)
        lr16 = lr_ref[...].astype(BF16)
        logits = _nn(lr16, lrw_ref[...]) + gb_ref[...]
        a_all = _logsig(logits) * (1.0 / 16.0)
        dsig = (1.0 - jax.nn.sigmoid(logits)) * (1.0 / 16.0)
        dst = dst_ref[...]
        for c in (range(nc) if reverse else range(nc - 1, -1, -1)):
            rows = slice(CH * c, CH * c + CH)
            b = _tri_mm(tri, a_all[rows])
            bl = b[0:1] if reverse else b[CH - 1:CH]
            eb = jnp.exp(b)
            enb = jnp.exp(-b)
            ebl = jnp.exp(bl - b)
            el = jnp.exp(bl)
            q = qkv_ref[rows, 0:256].astype(F32) * 0.125
            k = qkv_ref[rows, 256:512].astype(F32)
            v16 = qkv_ref[rows, 512:1024].astype(BF16)
            do16 = do_ref[rows, :].astype(BF16)
            qd = q * eb
            kd = k * enb
            kdec = k * ebl
            st = sb_ref[c]
            st16 = st.astype(BF16)
            dst16 = dst.astype(BF16)
            qd16 = qd.astype(BF16)
            kd16 = kd.astype(BF16)
            qstack = _stack_heads(qd, hm)
            kstack = _stack_heads(kd, hm)
            kdecstack = _stack_heads(kdec, hm)
            pt = jnp.where(vis4_t, _nt(kstack, qd16), 0.0).astype(BF16)
            dvinter = _nt(kdecstack, dst16)
            do_rows = _rows_of_heads(do16)
            v_rows = _rows_of_heads(v16)
            dp_cat = jnp.where(lane_vis, _diag_heads(_nt(do_rows, v_rows), hm), 0.0).astype(BF16)
            dpt_cat = jnp.where(lane_vis_t, _diag_heads(_nt(v_rows, do_rows), hm), 0.0).astype(BF16)
            dqd = _nn(dp_cat, kstack) + _diag_heads(_nn(do_rows, st16), hm)
            dkd = _nn(dpt_cat, qstack)
            dkdec = _diag_heads(_nn(v_rows, dst16), hm)
            for h in range(4):
                rh = slice(CH * h, CH * h + CH)
                dv_h = _nn(pt[rh], do_rows[rh]) + dvinter[rh]
                if has_prev:
                    dv_h = dv_h + pq_ref[rows, 512 + 128 * h:512 + 128 * h + 128]
                dqkv_ref[rows, 512 + 128 * h:512 + 128 * h + 128] = dv_h.astype(dqkv_ref.dtype)
            dq = dqd * eb * 0.125
            dk = dkd * enb + dkdec * ebl
            if has_prev:
                dq = dq + pq_ref[rows, 0:256]
                dk = dk + pq_ref[rows, 256:512]
            dqkv_ref[rows, 0:256] = dq.astype(dqkv_ref.dtype)
            dqkv_ref[rows, 256:512] = dk.astype(dqkv_ref.dtype)
            g_kdec = dkdec * kdec
            db = dqd * qd - dkd * kd - g_kdec
            dbl = jnp.sum(g_kdec, axis=0, keepdims=True) + jnp.sum(st * dst, axis=0, keepdims=True) * el
            da = _tri_mm(tri_t, db) + dbl
            dlog_ref[rows, :] = da * dsig[rows]
            dst = dst * el + _diag_heads(_tn(do16, qd16), hm)
        dst_ref[...] = dst
        dlog = dlog_ref[...]
        dlog16 = dlog.astype(BF16)
        dlr = _nn(dlog16, lrwt_ref[...])
        if has_prev:
            dlr = dlr + plr_ref[...]
        dlr_ref[...] = dlr
        dw2_ref[...] += _tn(lr16, dlog16)
        dgb_ref[...] += jnp.sum(dlog, axis=0, keepdims=True)

        @pl.when(i == nb - 1)
        def _():
            ds0_ref[...] = dst

    in_specs = [pl.BlockSpec((GLA_TB, 1024), lambda i: (rmap(i), qkv_blk)), pl.BlockSpec((GLA_TB, LRW), lambda i: (rmap(i), 0)),
                _full((LRW, 256)), _full((256, LRW)), _full((1, 256)), pl.BlockSpec((nc, 128, 256), lambda i: (rmap(i), 0, 0)),
                _full((128, 256)), pl.BlockSpec((GLA_TB, 512), lambda i: (rmap(i), 0))]
    args = [p, lr, lrw, lrwt, gbias, sb, dsfin, do]
    if has_prev:
        in_specs += [pl.BlockSpec((GLA_TB, 1024), lambda i: (rmap(i), 0)), pl.BlockSpec((GLA_TB, LRW), lambda i: (rmap(i), 0))]
        args += list(prev)
    aliases = {}
    if has_dp:
        in_specs.append(pl.BlockSpec(memory_space=pl.ANY))
        aliases = {len(args): 0}
        args.append(dp)
        dq_spec = pl.BlockSpec((GLA_TB, 1024), lambda i: (rmap(i), 2))
        dq_shape = SDS(dp.shape, dp.dtype)
    else:
        dq_spec = pl.BlockSpec((GLA_TB, 1024), lambda i: (rmap(i), 0))
        dq_shape = SDS((m, 1024), F32)
    return _pcall(
        body, name=name, grid=(nb,), in_specs=in_specs,
        out_specs=[dq_spec, pl.BlockSpec((GLA_TB, LRW), lambda i: (rmap(i), 0)), _full((LRW, 256)), _full((1, 256)), _full((128, 256))],
        out_shape=[dq_shape, SDS((m, LRW), F32), SDS((LRW, 256), F32), SDS((1, 256), F32), SDS((128, 256), F32)],
        scratch_shapes=[pltpu.VMEM((128, 256), F32), pltpu.VMEM((GLA_TB, 256), F32)],
        input_output_aliases=aliases, compiler_params=_cp(("arbitrary",)),
    )(*args)


def _device_step(x, c, ctx, c_ctx, tgt, wm, bm, ng, wi, wlr, ln_g, ln_b, ws, bs, w2, gb2, gbn, wpa, wpb, wo, gf, exchange=None):
    L = x.shape[0]
    wit = wi.T
    wlrt = wlr.T
    ws16 = ws.astype(BF16)
    wst16 = jnp.swapaxes(ws, 1, 2).astype(BF16)
    bscol = bs[:, :, None]
    lrw = [jnp.zeros((LRW, 256), F32).at[16 * r:16 * r + 16].set(w2[r]).astype(BF16) for r in range(2)]
    lrwt = [w.T for w in lrw]
    gbias = [gb2[r:r + 1] for r in range(2)]

    cc = jnp.zeros((8, D), F32).at[0:1].set(c).at[1:2].set(c_ctx)
    mod = _modvec(cc, wm, bm)
    shift, scale, gate = mod[0:1, 0:D], mod[0:1, D:2 * D], mod[0:1, 2 * D:3 * D]
    shift_c, scale_c = mod[1:2, 0:D], mod[1:2, D:2 * D]

    hc = _prep_h(ctx, ng, scale_c, shift_c, "prep_hc")
    pc = _mm(hc, wi[:, 2048:3072], tm=256, tn=1024, tk=D, out_dtype=F32, name="mm_pc")
    plrc = _mm(hc, wlr, tm=256, tn=LRW, tk=D, out_dtype=F32, name="mm_plrc")
    zero_s = jnp.zeros((128, 256), F32)
    _, sbc_f, sc_f = _gla_fwd(pc, 0, plrc, lrw[0], gbias[0], zero_s, reverse=False, name="gla_fwd_cf")
    _, sbc_b, sc_b = _gla_fwd(pc, 0, plrc, lrw[1], gbias[1], zero_s, reverse=True, name="gla_fwd_cb")

    h, p, plr = _proj_fwd(x, ng, scale, shift, wi, wlr)
    o_f, sb_f, _ = _gla_fwd(p, 2, plr, lrw[0], gbias[0], sc_f, reverse=False, name="gla_fwd_f")
    o_b, sb_b, _ = _gla_fwd(p, 2, plr, lrw[1], gbias[1], sc_b, reverse=True, name="gla_fwd_b")
    vnr, vnc = _ln_fwd(p, ln_g, ln_b)
    svc = _colmix_fwd(vnc.reshape(2, AC, L), ws16[2:4], bscol[2:4]).reshape(2, L, 128)
    ya_in, yb_in, svr, mrg, dx1, dout, loss, dgate, dgf = _tail_fwd(
        o_f, o_b, p, vnr, svc, x, tgt, ws16[0:2], bscol[0:2], gbn, wpa, wpb, wo, gate, gf)

    dya, dyb, dp_g, dp, dsr, dsc, do, dgbn = _tail_bwd(dout, ya_in, yb_in, p, svr, svc, o_f, o_b, gbn, wo.T, wpa, wpb, wpa.T, wpb.T)
    dwo = _mm_tn(mrg, dout, ta=D, tn=D, tk=1024, name="mm_dwo")
    dwpa = _mm_tn(ya_in, dya, ta=512, tn=D, tk=1024, name="mm_dwpa")
    dwpb = _mm_tn(yb_in, dyb, ta=512, tn=D, tk=1024, name="mm_dwpb")
    dvnc, dws23, dbs23 = _colmix_bwd(dsc.reshape(2, AC, L), vnc.reshape(2, AC, L), wst16[2:4])
    dp, dws01, dbs01, dlng, dlnb = _ln_bwd(dsr, vnr, dvnc.reshape(2, L, 128), p, wst16[0:2], ln_g, dp)
    zero_ds = jnp.zeros((128, 256), F32)
    dqkv_f, dlr_f, dw2_f, dgb_f, ds0_f = _gla_bwd(p, 2, plr, lrw[0], lrwt[0], gbias[0], sb_f, zero_ds, do, None, None,
                                                  reverse=False, name="gla_bwd_f")
    dp, dlr, dw2_b, dgb_b, ds0_b = _gla_bwd(p, 2, plr, lrw[1], lrwt[1], gbias[1], sb_b, zero_ds, do, (dqkv_f, dlr_f), dp,
                                            reverse=True, name="gla_bwd_b")
    zero_do = jnp.zeros((ctx.shape[0], 512), F32)
    dqkvc_f, dlrc_f, dw2c_f, dgbc_f, _ = _gla_bwd(pc, 0, plrc, lrw[0], lrwt[0], gbias[0], sbc_f, ds0_f, zero_do, None, None,
                                                  reverse=False, name="gla_bwd_cf")
    dqkvc, dlrc, dw2c_b, dgbc_b, _ = _gla_bwd(pc, 0, plrc, lrw[1], lrwt[1], gbias[1], sbc_b, ds0_b, zero_do,
                                              (dqkvc_f, dlrc_f), None, reverse=True, name="gla_bwd_cb")
    dhc = _mm(dqkvc, wit[2048:3072], tm=256, tn=D, tk=1024, out_dtype=F32, name="mm_dhc")
    dhc = _mm(dlrc, wlrt, tm=256, tn=D, tk=LRW, out_dtype=F32, name="mm_dhc_lr", acc=dhc)
    _, dng_c, dscale_c, dshift_c = _prep_bwd(ctx, dhc, None, ng, scale_c, "prep_bwd_c")

    dwi_g = _mm_tn(h, dp_g, ta=D, tn=1024, tk=2048, name="mm_dwi_g")
    dwi_r = _mm_tn(h, dp, ta=D, tn=1024, tk=2048, name="mm_dwi_r")
    dwi_qkv = _mm_tn(hc, dqkvc, ta=D, tn=1024, tk=256, name="mm_dwi_c", acc=dwi_r[:, 2048:3072])
    dwlr = _mm_tn(h, dlr, ta=D, tn=LRW, tk=2048, name="mm_dwlr")
    dwlr = _mm_tn(hc, dlrc, ta=D, tn=LRW, tk=256, name="mm_dwlr_c", acc=dwlr)
    big = dict(dwi_g=dwi_g, dwi_r=dwi_r, dwi_qkv=dwi_qkv, dwlr=dwlr, dwpa=dwpa, dwpb=dwpb, dwo=dwo)

    send = exchange(big) if exchange is not None else ()
    wit_r = jnp.concatenate([wit[3072:5120], wit[2048:3072]], axis=0)
    (dx, dng, dscale, dshift), got = _proj_bwd(dp_g, dp, dlr, wit[0:2048], wit_r, wlrt, x, dx1, ng, scale, send)

    dmodc = jnp.concatenate([dshift_c, dscale_c], axis=1)
    dscc = _dcctx(jnp.zeros((8, 2 * D), F32).at[0:1].set(dmodc), wm)[0:1]
    dw2p = dw2_f + dw2c_f, dw2_b + dw2c_b
    return dict(
        loss=loss[0, 0], dx=dx, got=got, **big,
        dmod=jnp.concatenate([dshift, dscale, dgate], axis=1), dmodc=dmodc, dscc=dscc, dng=dng + dng_c,
        dlng=dlng, dlnb=dlnb, dws=jnp.concatenate([dws01, dws23], axis=0),
        dbs=jnp.concatenate([dbs01, dbs23], axis=0)[:, :, 0], dgbn=dgbn, dgf=dgf,
        dw2=jnp.stack([dw2p[0][0:16], dw2p[1][16:32]]), dgb2=jnp.concatenate([dgb_f + dgbc_f, dgb_b + dgbc_b], axis=0),
    )


ANY = pl.BlockSpec(memory_space=pl.ANY)


def _coords():
    return lax.axis_index("x"), lax.axis_index("y"), lax.axis_index("c")


def _flip(v, bit):
    return 1 - v if bit else v


def _remote(src, dst, send_sem, recv_sem, dev):
    return pltpu.make_async_remote_copy(src_ref=src, dst_ref=dst, send_sem=send_sem, recv_sem=recv_sem,
                                        device_id=dev, device_id_type=MESH)


def _own(out, block, idx):
    return lax.dynamic_update_slice_in_dim(out, block[None], idx, axis=0)


def _gather_weights(split, whole, name):
    ns, nw = len(split), len(whole)
    n = ns + nw

    def body(*refs):
        ins, outs = refs[:n], refs[n:2 * n]
        a_send, a_recv, b_send, b_recv = refs[2 * n:]
        x, y, c = _coords()
        me = 2 * x + y
        sib = (x, y, 1 - c)
        peers = [(1 - x, y), (x, 1 - y), (1 - x, 1 - y)]
        sends = []
        for k in range(n):
            for j, (px, py) in enumerate(peers):
                if k < ns:
                    h = split[k].shape[0] // 2
                    rc = _remote(ins[k].at[pl.ds(c * h, h)], outs[k].at[me, pl.ds(c * h, h)], a_send.at[3 * k + j],
                                 a_recv.at[3 * k + j], (px, py, c))
                else:
                    rc = _remote(ins[k], outs[k].at[me], a_send.at[3 * k + j], a_recv.at[3 * k + j], (px, py, c))
                rc.start()
                sends.append(rc)
        for k in range(ns):
            h = split[k].shape[0] // 2
            for j, (px, py) in enumerate(peers):
                landed = outs[k].at[2 * px + py, pl.ds(c * h, h)]
                _remote(landed, landed, a_send.at[3 * k + j], a_recv.at[3 * k + j], (px, py, c)).wait_recv()
                fw = _remote(landed, landed, b_send.at[3 * k + j], b_recv.at[3 * k + j], sib)
                fw.start()
                sends.append(fw)
        for k in range(ns, n):
            for j, (px, py) in enumerate(peers):
                landed = outs[k].at[2 * px + py]
                _remote(landed, landed, a_send.at[3 * k + j], a_recv.at[3 * k + j], (px, py, c)).wait_recv()
        for k in range(ns):
            h = split[k].shape[0] // 2
            for j, (px, py) in enumerate(peers):
                passed = outs[k].at[2 * px + py, pl.ds((1 - c) * h, h)]
                _remote(passed, passed, b_send.at[3 * k + j], b_recv.at[3 * k + j], sib).wait_recv()
        for rc in sends:
            rc.wait_send()

    arrs = list(split) + list(whole)
    outs = _pcall(
        body, name=name, in_specs=[ANY] * n, out_specs=[ANY] * n,
        out_shape=[SDS((4,) + a.shape, a.dtype) for a in arrs],
        scratch_shapes=[pltpu.SemaphoreType.DMA((3 * n,)), pltpu.SemaphoreType.DMA((3 * n,)), pltpu.SemaphoreType.DMA((3 * ns,)),
                        pltpu.SemaphoreType.DMA((3 * ns,))],
    )(*arrs)
    me_xy = 2 * lax.axis_index("x") + lax.axis_index("y")
    return [_own(o, a, me_xy) for o, a in zip(outs, arrs)]


def _gather_all(a, name):
    masks = [(mx, my, mc) for mx in range(2) for my in range(2) for mc in range(2)][1:]

    def body(in_ref, out_ref, send_sems, recv_sems):
        x, y, c = _coords()
        me = 4 * x + 2 * y + c
        sends = []
        for j, (mx, my, mc) in enumerate(masks):
            peer = (_flip(x, mx), _flip(y, my), _flip(c, mc))
            rc = pltpu.make_async_remote_copy(
                src_ref=in_ref, dst_ref=out_ref.at[me], send_sem=send_sems.at[j], recv_sem=recv_sems.at[j],
                device_id=peer, device_id_type=MESH)
            rc.start()
            sends.append(rc)
        for j, (mx, my, mc) in enumerate(masks):
            px, py, pc = _flip(x, mx), _flip(y, my), _flip(c, mc)
            pltpu.make_async_remote_copy(
                src_ref=in_ref, dst_ref=out_ref.at[4 * px + 2 * py + pc], send_sem=send_sems.at[j], recv_sem=recv_sems.at[j],
                device_id=(px, py, pc), device_id_type=MESH).wait_recv()
        for rc in sends:
            rc.wait_send()

    out = _pcall(
        body, name=name, in_specs=[ANY], out_specs=ANY, out_shape=SDS((8,) + a.shape, a.dtype),
        scratch_shapes=[pltpu.SemaphoreType.DMA((7,)), pltpu.SemaphoreType.DMA((7,))],
    )(a)
    return _own(out, a, 4 * lax.axis_index("x") + 2 * lax.axis_index("y") + lax.axis_index("c"))


def _swap_rows_c(arrs, name):
    n = len(arrs)

    def body(*refs):
        ins, outs = refs[:n], refs[n:2 * n]
        send_sems, recv_sems = refs[2 * n:]
        x, y, c = _coords()
        sends = []
        for k in range(n):
            h = arrs[k].shape[1] // 2
            rc = _remote(ins[k].at[pl.ds(0, 4), pl.ds((1 - c) * h, h)], outs[k], send_sems.at[k], recv_sems.at[k], (x, y, 1 - c))
            rc.start()
            sends.append(rc)
        for rc in sends:
            rc.wait()

    return _pcall(
        body, name=name, in_specs=[ANY] * n, out_specs=[ANY] * n,
        out_shape=[SDS((4, a.shape[1] // 2, a.shape[2]), a.dtype) for a in arrs],
        scratch_shapes=[pltpu.SemaphoreType.DMA((n,)), pltpu.SemaphoreType.DMA((n,))],
    )(*arrs)


def _a2a_xy(arrs, name):
    n = len(arrs)

    def body(*refs):
        ins, outs = refs[:n], refs[n:2 * n]
        send_sems, recv_sems = refs[2 * n:]
        x, y, c = _coords()
        me = 2 * x + y
        peers = [(1 - x, y), (x, 1 - y), (1 - x, 1 - y)]
        sends = []
        for k in range(n):
            for j, (px, py) in enumerate(peers):
                rc = _remote(ins[k].at[2 * px + py], outs[k].at[me], send_sems.at[3 * k + j], recv_sems.at[3 * k + j], (px, py, c))
                rc.start()
                sends.append(rc)
        for k in range(n):
            for j, (px, py) in enumerate(peers):
                landed = outs[k].at[2 * px + py]
                _remote(landed, landed, send_sems.at[3 * k + j], recv_sems.at[3 * k + j], (px, py, c)).wait_recv()
        for rc in sends:
            rc.wait_send()

    outs = _pcall(
        body, name=name, in_specs=[ANY] * n, out_specs=[ANY] * n, out_shape=[SDS(a.shape, a.dtype) for a in arrs],
        scratch_shapes=[pltpu.SemaphoreType.DMA((3 * n,)), pltpu.SemaphoreType.DMA((3 * n,))],
    )(*arrs)
    me_xy = 2 * lax.axis_index("x") + lax.axis_index("y")
    return [_own(o, lax.dynamic_index_in_dim(a, me_xy, axis=0, keepdims=False), me_xy) for o, a in zip(outs, arrs)]


def _join_halves(halves, name):
    n = len(halves)

    def body(*refs):
        ins, outs = refs[:n], refs[n:2 * n]
        send_sems, recv_sems = refs[2 * n:]
        x, y, c = _coords()
        sends = []
        for k in range(n):
            h = halves[k].shape[0]
            rc = _remote(ins[k], outs[k].at[pl.ds(c * h, h)], send_sems.at[k], recv_sems.at[k], (x, y, 1 - c))
            rc.start()
            sends.append(rc)
        for k in range(n):
            h = halves[k].shape[0]
            landed = outs[k].at[pl.ds((1 - c) * h, h)]
            _remote(landed, landed, send_sems.at[k], recv_sems.at[k], (x, y, 1 - c)).wait_recv()
        for rc in sends:
            rc.wait_send()

    outs = _pcall(
        body, name=name, in_specs=[ANY] * n, out_specs=[ANY] * n,
        out_shape=[SDS((2 * a.shape[0], a.shape[1]), a.dtype) for a in halves],
        scratch_shapes=[pltpu.SemaphoreType.DMA((n,)), pltpu.SemaphoreType.DMA((n,))],
    )(*halves)
    ci = lax.axis_index("c")
    return [lax.dynamic_update_slice_in_dim(o, a, ci * a.shape[0], axis=0) for o, a in zip(outs, halves)]


def _pair_sum(a, got, cidx, name):
    _, r, cdim = a.shape
    h = r // 2
    tr = min(h, 256)
    nj = h // tr

    def body(c_ref, a_ref, g_ref, o_ref):
        o_ref[...] = (a_ref[...] + g_ref[...]).astype(BF16)

    blk = pl.BlockSpec((1, tr, cdim), lambda s, j, c: (s, j, 0))
    return _pcall(
        body, name=name, out_shape=SDS((4, h, cdim), BF16),
        grid_spec=pltpu.PrefetchScalarGridSpec(
            num_scalar_prefetch=1, grid=(4, nj),
            in_specs=[pl.BlockSpec((1, tr, cdim), lambda s, j, c: (s, c[0] * nj + j, 0)), blk], out_specs=blk),
        compiler_params=_cp(("parallel", "parallel")),
    )(cidx, a, got)


def _sum_chips(parts, name):
    _, h, cdim = parts.shape
    tr = min(h, 256)

    def body(p_ref, o_ref):
        acc = p_ref[0].astype(F32)
        for k in range(1, 4):
            acc = acc + p_ref[k].astype(F32)
        o_ref[...] = acc

    return _pcall(
        body, name=name, grid=(h // tr,), in_specs=[pl.BlockSpec((4, tr, cdim), lambda i: (0, i, 0))],
        out_specs=pl.BlockSpec((tr, cdim), lambda i: (i, 0)), out_shape=SDS((h, cdim), F32), compiler_params=_cp(("parallel",)),
    )(parts)


def _sum_slots(a, name, rows):
    s, n, _ = a.shape

    def body(a_ref, o_ref):
        acc = a_ref[0]
        for k in range(1, s):
            acc = acc + a_ref[k]
        o_ref[...] = acc

    return _pcall(
        body, name=name, grid=(n // rows,), in_specs=[pl.BlockSpec((s, rows, 128), lambda i: (0, i, 0))],
        out_specs=pl.BlockSpec((rows, 128), lambda i: (i, 0)), out_shape=SDS((n, 128), F32),
        compiler_params=_cp(("parallel",)),
    )(a)


def _adamw(w, g, m, v, name, rows):
    r, cdim = w.shape

    def body(w_ref, g_ref, m_ref, v_ref, d_ref, nm_ref, nv_ref):
        g_ = g_ref[...]
        nm = ADAM_B1 * m_ref[...] + (1.0 - ADAM_B1) * g_
        nv = ADAM_B2 * v_ref[...] + (1.0 - ADAM_B2) * (g_ * g_)
        m_hat = nm / (1.0 - ADAM_B1 ** ADAM_STEP)
        v_hat = nv / (1.0 - ADAM_B2 ** ADAM_STEP)
        d_ref[...] = -ADAM_LR * (m_hat / (jnp.sqrt(v_hat) + ADAM_EPS) + ADAM_WD * w_ref[...])
        nm_ref[...] = nm
        nv_ref[...] = nv

    blk = pl.BlockSpec((rows, cdim), lambda i: (i, 0))
    return _pcall(
        body, name=name, grid=(r // rows,), in_specs=[blk] * 4, out_specs=[blk] * 3,
        out_shape=[SDS(w.shape, F32)] * 3, compiler_params=_cp(("parallel",)),
    )(w, g, m, v)


def _pack(pieces, rows):
    flat = jnp.concatenate([p.reshape(-1) for p in pieces])
    return jnp.pad(flat, (0, rows * 128 - flat.shape[0])).reshape(rows, 128)


def _unpack(buf, shapes):
    flat = buf.reshape(-1)
    out, off = [], 0
    for shp in shapes:
        size = 1
        for s in shp:
            size *= s
        out.append(flat[off:off + size].reshape(shp))
        off += size
    return out


def _perm_cols(w):
    perm = jnp.concatenate([w[..., 3104:5152], w[..., 0:1024], w[..., 1056:1568], w[..., 1568:2080], w[..., 2592:3104],
                            w[..., 2080:2592]], axis=-1)
    return perm, w[..., 1024:1056]


def _unperm_cols(perm, lr32):
    return jnp.concatenate([perm[..., 2048:3072], lr32, perm[..., 3072:3584], perm[..., 3584:4096], perm[..., 4608:5120],
                            perm[..., 4096:4608], perm[..., 0:2048]], axis=-1)


SMALL_ROWS = 672
HALF_ROWS = 7200


def kernel(x, c, ctx, c_ctx, w_mod, b_mod, norm_g, w_in, a_ln_g, a_ln_b, a_ws, a_bs, b_gate_w2, b_gate_b, b_norm_g, w_proj_a, w_proj_b, w_out, final_norm_g, loss_target, m_c_ctx, m_w_mod, m_b_mod, m_norm_g, m_w_in, m_a_ln_g, m_a_ln_b, m_a_ws, m_a_bs, m_b_gate_w2, m_b_gate_b, m_b_norm_g, m_w_proj_a, m_w_proj_b, m_w_out, m_final_norm_g, v_c_ctx, v_w_mod, v_b_mod, v_norm_g, v_w_in, v_a_ln_g, v_a_ln_b, v_a_ws, v_a_bs, v_b_gate_w2, v_b_gate_b, v_b_norm_g, v_w_proj_a, v_w_proj_b, v_w_out, v_final_norm_g):
    xi, yi, ci = _coords()
    me_xy = 2 * xi + yi

    gate_pack = _pack([b_gate_w2[0], b_gate_b[0]], 24)
    g_wi, g_wm, g_wpa, g_wpb, g_wo, g_gate = _gather_weights(
        [w_in[0].astype(BF16), w_mod[0].astype(BF16), w_proj_a[0].astype(BF16), w_proj_b[0].astype(BF16),
         w_out[0].astype(BF16)], [gate_pack], "gather_weights")
    wi_full = jnp.swapaxes(g_wi, 0, 1).reshape(D, 4 * 1288)
    wi, wlr32 = _perm_cols(wi_full)
    wlr = jnp.pad(wlr32, ((0, 0), (0, LRW - 32)))
    wm = jnp.swapaxes(g_wm, 0, 1).reshape(D, 3 * D)
    wpa = jnp.swapaxes(g_wpa, 0, 1).reshape(512, D)
    wpb = jnp.swapaxes(g_wpb, 0, 1).reshape(512, D)
    wo = g_wo.reshape(D, D)
    gflat = g_gate.reshape(4, 24 * 128)
    w2 = jnp.swapaxes(gflat[:, 0:2048].reshape(4, 2, 16, 64), 0, 2)
    w2 = jnp.swapaxes(w2, 0, 1).reshape(2, 16, 256)
    gb2 = jnp.swapaxes(gflat[:, 2048:2176].reshape(4, 2, 64), 0, 1).reshape(2, 256)

    tags = ["wi", "wpa", "wpb", "wo"]
    sent = []

    def exchange(g):
        dwr = g["dwi_r"]
        dwi_full = jnp.concatenate([g["dwi_qkv"], g["dwlr"][:, 0:32], dwr[:, 0:512], dwr[:, 512:1024], dwr[:, 1536:2048],
                                    dwr[:, 1024:1536], g["dwi_g"]], axis=1)
        big = [jnp.swapaxes(dwi_full.reshape(D, 4, 1288), 0, 1), jnp.swapaxes(g["dwpa"].reshape(512, 4, 256), 0, 1),
               jnp.swapaxes(g["dwpb"].reshape(512, 4, 256), 0, 1), g["dwo"].reshape(4, 256, D)]
        other = _swap_rows_c(big, "swap_half_in")
        cidx = jnp.reshape(ci, (1,)).astype(jnp.int32)
        sent.extend(_pair_sum(a, o, cidx, "sum_pair_" + t) for a, o, t in zip(big, other, tags))
        return sent

    r = _device_step(x[0], c, ctx[0], c_ctx[None], loss_target[0], wm, b_mod, norm_g, wi, wlr, a_ln_g, a_ln_b, a_ws[0], a_bs[0],
                     w2, gb2, b_norm_g, wpa, wpb, wo, final_norm_g[None], exchange)

    small = _pack([r["dmod"], c, r["dmodc"], r["dscc"], r["dng"], r["dlng"], r["dlnb"], r["dws"], r["dbs"], r["dgbn"], r["dgf"],
                   r["dw2"], r["dgb2"], jnp.broadcast_to(r["loss"], (128,))], SMALL_ROWS)
    small_all = _gather_all(small, "gather_small")
    small_sum = _sum_slots(small_all, "sum_small", SMALL_ROWS // 4)
    (s_dmod, _, s_dmodc, s_dscc, s_dng, s_dlng, s_dlnb, s_dws, s_dbs, s_dgbn, s_dgf, s_dw2, s_dgb2, s_loss) = _unpack(
        small_sum, [(1, 3 * D), (1, D), (1, 2 * D), (D,), (1, D), (1, 512), (1, 512), (1, 4, 128, 128), (1, 4, 128), (1, 512),
                    (D,), (2, 16, 256), (2, 256), (128,)])
    loss = s_loss[0]
    s_dmodc_p = jnp.pad(s_dmodc, ((0, 0), (0, D)))
    g_b_mod = s_dmod + s_dmodc_p
    sg = jax.nn.sigmoid(c_ctx)
    g_c_ctx = s_dscc * (sg * (1.0 + c_ctx * (1.0 - sg)))
    g_w2 = lax.dynamic_slice_in_dim(s_dw2, 64 * me_xy, 64, axis=2)[None]
    g_gb2 = lax.dynamic_slice_in_dim(s_dgb2, 64 * me_xy, 64, axis=1)[None]

    flat_all = small_all.reshape(8, SMALL_ROWS * 128)
    dmod_all = flat_all[:, 0:3 * D]
    c_all = flat_all[:, 3 * D:4 * D]
    lhs = jnp.concatenate([_silu(c_all), _silu(c_ctx)[None], jnp.zeros((7, D), F32)], axis=0)
    rhs = jnp.concatenate([dmod_all, s_dmodc_p, jnp.zeros((7, 3 * D), F32)], axis=0)
    rhs = lax.dynamic_slice_in_dim(rhs, 768 * me_xy, 768, axis=1)
    g_w_mod = _mm(lhs.T.astype(BF16), rhs.astype(BF16), tm=D, tn=768, tk=16, out_dtype=F32, name="mm_dwm")

    parts = [_own(g, lax.dynamic_index_in_dim(s_, me_xy, axis=0, keepdims=False), me_xy) for g, s_ in zip(r["got"], sent)]
    halves = [_sum_chips(p_, "sum_chips_" + t) for p_, t in zip(parts, tags)]
    g_w_in, g_wpa, g_wpb, g_wo = _join_halves(halves, "swap_half_out")

    d_w_in, nm_w_in, nv_w_in = _adamw(w_in[0], g_w_in, m_w_in[0], v_w_in[0], "adamw_w_in", 256)
    d_w_mod, nm_w_mod, nv_w_mod = _adamw(w_mod[0], g_w_mod, m_w_mod[0], v_w_mod[0], "adamw_w_mod", 256)
    d_wpa, nm_wpa, nv_wpa = _adamw(w_proj_a[0], g_wpa, m_w_proj_a[0], v_w_proj_a[0], "adamw_wpa", 256)
    d_wpb, nm_wpb, nv_wpb = _adamw(w_proj_b[0], g_wpb, m_w_proj_b[0], v_w_proj_b[0], "adamw_wpb", 256)
    d_wo, nm_wo, nv_wo = _adamw(w_out[0], g_wo, m_w_out[0], v_w_out[0], "adamw_wo", 256)

    names = ["c_ctx", "b_mod", "norm_g", "a_ln_g", "a_ln_b", "a_ws", "a_bs", "b_gate_w2", "b_gate_b", "b_norm_g", "final_norm_g"]
    ws_ = [c_ctx, b_mod, norm_g, a_ln_g, a_ln_b, a_ws, a_bs, b_gate_w2, b_gate_b, b_norm_g, final_norm_g]
    gs_ = [g_c_ctx, g_b_mod, s_dng, s_dlng, s_dlnb, s_dws, s_dbs, g_w2, g_gb2, s_dgbn, s_dgf]
    ms_ = [m_c_ctx, m_b_mod, m_norm_g, m_a_ln_g, m_a_ln_b, m_a_ws, m_a_bs, m_b_gate_w2, m_b_gate_b, m_b_norm_g, m_final_norm_g]
    vs_ = [v_c_ctx, v_b_mod, v_norm_g, v_a_ln_g, v_a_ln_b, v_a_ws, v_a_bs, v_b_gate_w2, v_b_gate_b, v_b_norm_g, v_final_norm_g]
    shapes = [w.shape for w in ws_]
    gs_ = [g.reshape(s) for g, s in zip(gs_, shapes)]
    d_s, nm_s, nv_s = _adamw(_pack(ws_, 600), _pack(gs_, 600), _pack(ms_, 600), _pack(vs_, 600), "adamw_small", 600)
    d_small = dict(zip(names, _unpack(d_s, shapes)))
    nm_small = dict(zip(names, _unpack(nm_s, shapes)))
    nv_small = dict(zip(names, _unpack(nv_s, shapes)))
    g_small = dict(zip(names, gs_))

    order = ["c_ctx", "w_mod", "b_mod", "norm_g", "w_in", "a_ln_g", "a_ln_b", "a_ws", "a_bs", "b_gate_w2", "b_gate_b", "b_norm_g",
             "w_proj_a", "w_proj_b", "w_out", "final_norm_g"]
    big_g = dict(w_mod=g_w_mod[None], w_in=g_w_in[None], w_proj_a=g_wpa[None], w_proj_b=g_wpb[None], w_out=g_wo[None])
    big_d = dict(w_mod=d_w_mod[None], w_in=d_w_in[None], w_proj_a=d_wpa[None], w_proj_b=d_wpb[None], w_out=d_wo[None])
    big_m = dict(w_mod=nm_w_mod[None], w_in=nm_w_in[None], w_proj_a=nm_wpa[None], w_proj_b=nm_wpb[None], w_out=nm_wo[None])
    big_v = dict(w_mod=nv_w_mod[None], w_in=nv_w_in[None], w_proj_a=nv_wpa[None], w_proj_b=nv_wpb[None], w_out=nv_wo[None])
    grads = [big_g[n] if n in big_g else g_small[n] for n in order]
    deltas = [big_d[n] if n in big_d else d_small[n] for n in order]
    new_m = [big_m[n] if n in big_m else nm_small[n] for n in order]
    new_v = [big_v[n] if n in big_v else nv_small[n] for n in order]
    return (loss, r["dx"][None], *grads, *deltas, *new_m, *new_v)
```

```python
import functools

import jax
import jax.numpy as jnp
from jax import lax
from jax.experimental import pallas as pl
from jax.experimental.pallas import tpu as pltpu

F32 = jnp.float32
BF16 = jnp.bfloat16
SDS = jax.ShapeDtypeStruct

D = 1024
NP = 5120
LRW = 128
CH = 64
AC = 128
EPS = 1e-6
TOK = 256
GLA_TB = 256
VMEM_BIG = 48 * 1024 * 1024

ADAM_LR, ADAM_B1, ADAM_B2, ADAM_EPS, ADAM_WD, ADAM_STEP = 0.001, 0.9, 0.999, 1e-08, 0.01, 10

_pcall = pl.pallas_call
MESH = pl.DeviceIdType.MESH


def _cp(sem=None, vmem=None):
    kw = {}
    if sem is not None:
        kw["dimension_semantics"] = sem
    if vmem is not None:
        kw["vmem_limit_bytes"] = vmem
    return pltpu.CompilerParams(**kw)


def _silu(x):
    return x * jax.nn.sigmoid(x)


def _dsilu(x):
    s = jax.nn.sigmoid(x)
    return s * (1.0 + x * (1.0 - s))


def _logsig(x):
    return jnp.minimum(x, 0.0) - jnp.log1p(jnp.exp(-jnp.abs(x)))


def _nt(a, b):
    return lax.dot_general(a, b, (((1,), (1,)), ((), ())), preferred_element_type=F32)


def _tn(a, b):
    return lax.dot_general(a, b, (((0,), (0,)), ((), ())), preferred_element_type=F32)


def _nn(a, b):
    return jnp.dot(a, b, preferred_element_type=F32)


def _full(shape):
    return pl.BlockSpec(shape, lambda *_: (0,) * len(shape))


def _mm(a, b, *, tm, tn, tk, out_dtype, name, acc=None, n_outer=False):
    m, k = a.shape
    k2, n = b.shape
    assert k == k2 and m % tm == 0 and n % tn == 0 and k % tk == 0, (a.shape, b.shape, tm, tn, tk)
    nk = k // tk
    has_acc = acc is not None

    def body(*refs):
        if has_acc:
            a_ref, b_ref, c_ref, o_ref = refs[:4]
        else:
            a_ref, b_ref, o_ref = refs[:3]
        part = _nn(a_ref[...].astype(BF16), b_ref[...].astype(BF16))
        if nk == 1:
            o_ref[...] = ((c_ref[...] + part) if has_acc else part).astype(out_dtype)
            return
        acc_ref = refs[-1]
        kk = pl.program_id(2)

        @pl.when(kk == 0)
        def _():
            if has_acc:
                acc_ref[...] = c_ref[...] + part
            else:
                acc_ref[...] = part

        @pl.when(kk > 0)
        def _():
            acc_ref[...] += part

        @pl.when(kk == nk - 1)
        def _():
            o_ref[...] = acc_ref[...].astype(out_dtype)

    if n_outer:
        ij = lambda g0, g1: (g1, g0)
        grid = (n // tn, m // tm, nk)
    else:
        ij = lambda g0, g1: (g0, g1)
        grid = (m // tm, n // tn, nk)
    in_specs = [pl.BlockSpec((tm, tk), lambda g0, g1, kk: (ij(g0, g1)[0], kk)),
                pl.BlockSpec((tk, tn), lambda g0, g1, kk: (kk, ij(g0, g1)[1]))]
    args = [a, b]
    if has_acc:
        in_specs.append(pl.BlockSpec((tm, tn), lambda g0, g1, kk: ij(g0, g1)))
        args.append(acc)
    return _pcall(
        body, name=name, grid=grid, in_specs=in_specs,
        out_specs=pl.BlockSpec((tm, tn), lambda g0, g1, kk: ij(g0, g1)),
        out_shape=SDS((m, n), out_dtype), scratch_shapes=([pltpu.VMEM((tm, tn), F32)] if nk > 1 else []),
        compiler_params=_cp(("parallel", "parallel", "arbitrary"), VMEM_BIG),
    )(*args)


def _mm_tn(a, b, *, ta, tn, tk, name, acc=None):
    m, ka = a.shape
    m2, n = b.shape
    assert m == m2 and ka % ta == 0 and n % tn == 0 and m % tk == 0, (a.shape, b.shape, ta, tn, tk)
    nk = m // tk
    has_acc = acc is not None

    def body(*refs):
        if has_acc:
            a_ref, b_ref, c_ref, o_ref = refs
        else:
            a_ref, b_ref, o_ref = refs
        kk = pl.program_id(2)
        part = _tn(a_ref[...].astype(BF16), b_ref[...].astype(BF16))

        @pl.when(kk == 0)
        def _():
            if has_acc:
                o_ref[...] = c_ref[...] + part
            else:
                o_ref[...] = part

        @pl.when(kk > 0)
        def _():
            o_ref[...] += part

    in_specs = [pl.BlockSpec((tk, ta), lambda i, j, kk: (kk, i)), pl.BlockSpec((tk, tn), lambda i, j, kk: (kk, j))]
    args = [a, b]
    if has_acc:
        in_specs.append(pl.BlockSpec((ta, tn), lambda i, j, kk: (i, j)))
        args.append(acc)
    return _pcall(
        body, name=name, grid=(ka // ta, n // tn, nk), in_specs=in_specs,
        out_specs=pl.BlockSpec((ta, tn), lambda i, j, kk: (i, j)), out_shape=SDS((ka, n), F32),
        compiler_params=_cp(("parallel", "parallel", "arbitrary"), VMEM_BIG),
    )(*args)


def _modvec(cc, wm, bm):
    def body(c_ref, w_ref, b_ref, o_ref):
        o_ref[...] = _nn(_silu(c_ref[...]).astype(BF16), w_ref[...]) + b_ref[...]

    return _pcall(body, name="modvec", out_shape=SDS((8, 3 * D), F32), compiler_params=_cp(None, VMEM_BIG))(cc, wm, bm)


def _dcctx(dmodc, wm):
    def body(d_ref, w_ref, o_ref):
        o_ref[...] = _nt(d_ref[...].astype(BF16), w_ref[...])

    return _pcall(
        body, name="dcctx", grid=(1,), in_specs=[_full((8, 2 * D)), pl.BlockSpec((D, 2 * D), lambda i: (0, 0))],
        out_specs=_full((8, D)), out_shape=SDS((8, D), F32), compiler_params=_cp(("arbitrary",), VMEM_BIG),
    )(dmodc, wm)


def _prep_h(x, ng, scale, shift, name):
    m = x.shape[0]

    def body(x_ref, g_ref, sc_ref, sh_ref, h_ref):
        xf = x_ref[...]
        r = lax.rsqrt(jnp.mean(xf * xf, axis=-1, keepdims=True) + EPS)
        y = (xf * r) * g_ref[...]
        h_ref[...] = (y * (1.0 + sc_ref[...]) + sh_ref[...]).astype(BF16)

    row = pl.BlockSpec((TOK, D), lambda i: (i, 0))
    return _pcall(
        body, name=name, grid=(m // TOK,), in_specs=[row, _full((1, D)), _full((1, D)), _full((1, D))],
        out_specs=row, out_shape=SDS((m, D), BF16), compiler_params=_cp(("parallel",)),
    )(x, ng, scale, shift)


def _resident(shape):
    return pl.BlockSpec(shape, lambda *_: (0,) * len(shape), pipeline_mode=pl.Buffered(1))


PROJ_TM = 512


def _proj_fwd(x, ng, scale, shift, wi, wlr):
    m = x.shape[0]
    n = wi.shape[1]

    def body(x_ref, g_ref, sc_ref, sh_ref, wi_ref, wlr_ref, h_ref, p_ref, plr_ref):
        xf = x_ref[...]
        r = lax.rsqrt(jnp.mean(xf * xf, axis=-1, keepdims=True) + EPS)
        y = (xf * r) * g_ref[...]
        h = (y * (1.0 + sc_ref[...]) + sh_ref[...]).astype(BF16)
        h_ref[...] = h
        for j in range(n // D):
            p_ref[:, D * j:D * j + D] = _nn(h, wi_ref[:, D * j:D * j + D]).astype(BF16)
        plr_ref[...] = _nn(h, wlr_ref[...])

    row = pl.BlockSpec((PROJ_TM, D), lambda i: (i, 0))
    vec = _full((1, D))
    return _pcall(
        body, name="proj_fwd", grid=(m // PROJ_TM,),
        in_specs=[row, vec, vec, vec, _resident((D, n)), _resident((D, LRW))],
        out_specs=[row, pl.BlockSpec((PROJ_TM, n), lambda i: (i, 0)), pl.BlockSpec((PROJ_TM, LRW), lambda i: (i, 0))],
        out_shape=[SDS((m, D), BF16), SDS((m, n), BF16), SDS((m, LRW), F32)],
        compiler_params=_cp(("parallel",), VMEM_BIG),
    )(x, ng, scale, shift, wi, wlr)


def _proj_bwd(dp_g, dp_r, dlr, wit_g, wit_r, wlrt, x, dx1, ng, scale, send=()):
    m = x.shape[0]
    ns = len(send)
    steps = m // PROJ_TM

    def body(*refs):
        (dpg_ref, dpr_ref, dlr_ref, wg_ref, wr_ref, wl_ref, x_ref, r_ref, g_ref, sc_ref) = refs[:10]
        send_refs = refs[10:10 + ns]
        dx_ref, dg_ref, dsc_ref, dsh_ref = refs[10 + ns:14 + ns]
        got_refs = refs[14 + ns:14 + 2 * ns]
        sems = refs[14 + 2 * ns:]
        i = pl.program_id(0)

        def copies():
            cx, cy, cc = _coords()
            me = 2 * cx + cy
            peers = [(1 - cx, cy), (cx, 1 - cy), (1 - cx, 1 - cy)]
            out, back = [], []
            for k in range(ns):
                for j, (px, py) in enumerate(peers):
                    out.append(_remote(send_refs[k].at[2 * px + py], got_refs[k].at[me], sems[0].at[3 * k + j],
                                       sems[1].at[3 * k + j], (px, py, cc)))
                    landed = got_refs[k].at[2 * px + py]
                    back.append(_remote(landed, landed, sems[0].at[3 * k + j], sems[1].at[3 * k + j], (px, py, cc)))
            return out, back

        @pl.when(i == 0)
        def _():
            dg_ref[...] = jnp.zeros_like(dg_ref)
            dsc_ref[...] = jnp.zeros_like(dsc_ref)
            dsh_ref[...] = jnp.zeros_like(dsh_ref)
            if ns:
                for rc in copies()[0]:
                    rc.start()

        dh_ = (_nn(dpg_ref[...], wg_ref[...]) + _nn(dpr_ref[...], wr_ref[...])
               + _nn(dlr_ref[...].astype(BF16), wl_ref[...]))
        xf = x_ref[...]
        r = lax.rsqrt(jnp.mean(xf * xf, axis=-1, keepdims=True) + EPS)
        xh = xf * r
        y = xh * g_ref[...]
        dsh_ref[...] += jnp.sum(dh_, axis=0, keepdims=True)
        dsc_ref[...] += jnp.sum(dh_ * y, axis=0, keepdims=True)
        dy = dh_ * (1.0 + sc_ref[...])
        dg_ref[...] += jnp.sum(dy * xh, axis=0, keepdims=True)
        dxh = dy * g_ref[...]
        dx_ref[...] = r * (dxh - xh * jnp.mean(dxh * xh, axis=-1, keepdims=True)) + r_ref[...]

        if ns:
            @pl.when(i == steps - 1)
            def _():
                out, back = copies()
                for rc in back:
                    rc.wait_recv()
                for rc in out:
                    rc.wait_send()

    row = pl.BlockSpec((PROJ_TM, D), lambda i: (i, 0))
    vec = _full((1, D))
    kg, kr = dp_g.shape[1], dp_r.shape[1]
    res = _pcall(
        body, name="proj_bwd", grid=(steps,),
        in_specs=[pl.BlockSpec((PROJ_TM, kg), lambda i: (i, 0)), pl.BlockSpec((PROJ_TM, kr), lambda i: (i, 0)),
                  pl.BlockSpec((PROJ_TM, LRW), lambda i: (i, 0)), _resident((kg, D)), _resident((kr, D)), _resident((LRW, D)),
                  row, row, vec, vec] + [ANY] * ns,
        out_specs=[row, vec, vec, vec] + [ANY] * ns,
        out_shape=[SDS((m, D), F32), SDS((1, D), F32), SDS((1, D), F32), SDS((1, D), F32)] + [SDS(a.shape, a.dtype) for a in send],
        scratch_shapes=([pltpu.SemaphoreType.DMA((3 * ns,)), pltpu.SemaphoreType.DMA((3 * ns,))] if ns else []),
        compiler_params=_cp(("arbitrary",), VMEM_BIG),
    )(dp_g, dp_r, dlr, wit_g, wit_r, wlrt, x, dx1, ng, scale, *send)
    return tuple(res[:4]), list(res[4:])


def _prep_bwd(x, dh, dx1, ng, scale, name):
    m = x.shape[0]
    has_res = dx1 is not None

    def body(*refs):
        if has_res:
            x_ref, dh_ref, r_ref, g_ref, sc_ref, dx_ref, dg_ref, dsc_ref, dsh_ref = refs
        else:
            x_ref, dh_ref, g_ref, sc_ref, dx_ref, dg_ref, dsc_ref, dsh_ref = refs
        i = pl.program_id(0)

        @pl.when(i == 0)
        def _():
            dg_ref[...] = jnp.zeros_like(dg_ref)
            dsc_ref[...] = jnp.zeros_like(dsc_ref)
            dsh_ref[...] = jnp.zeros_like(dsh_ref)

        xf = x_ref[...]
        dh_ = dh_ref[...]
        r = lax.rsqrt(jnp.mean(xf * xf, axis=-1, keepdims=True) + EPS)
        xh = xf * r
        y = xh * g_ref[...]
        dsh_ref[...] += jnp.sum(dh_, axis=0, keepdims=True)
        dsc_ref[...] += jnp.sum(dh_ * y, axis=0, keepdims=True)
        dy = dh_ * (1.0 + sc_ref[...])
        dg_ref[...] += jnp.sum(dy * xh, axis=0, keepdims=True)
        dxh = dy * g_ref[...]
        dx = r * (dxh - xh * jnp.mean(dxh * xh, axis=-1, keepdims=True))
        if has_res:
            dx = dx + r_ref[...]
        dx_ref[...] = dx

    row = pl.BlockSpec((TOK, D), lambda i: (i, 0))
    vec = _full((1, D))
    in_specs = [row, row] + ([row] if has_res else []) + [vec, vec]
    args = [x, dh] + ([dx1] if has_res else []) + [ng, scale]
    return _pcall(
        body, name=name, grid=(m // TOK,), in_specs=in_specs, out_specs=[row, vec, vec, vec],
        out_shape=[SDS((m, D), F32), SDS((1, D), F32), SDS((1, D), F32), SDS((1, D), F32)],
        compiler_params=_cp(("arbitrary",)),
    )(*args)


def _ln_fwd(p, ln_g, ln_b):
    m = p.shape[0]

    def body(va_ref, g_ref, b_ref, vr_ref, vc_ref):
        xf = va_ref[...].astype(F32)
        xc = xf - jnp.mean(xf, axis=-1, keepdims=True)
        y = xc * lax.rsqrt(jnp.mean(xc * xc, axis=-1, keepdims=True) + EPS)
        vn = y * g_ref[...] + b_ref[...]
        vr_ref[...] = vn[:, 0:256].astype(BF16)
        vc_ref[0] = vn[:, 256:384].astype(BF16)
        vc_ref[1] = vn[:, 384:512].astype(BF16)

    return _pcall(
        body, name="ln_fwd", grid=(m // TOK,),
        in_specs=[pl.BlockSpec((TOK, 512), lambda i: (i, 9)), _full((1, 512)), _full((1, 512))],
        out_specs=[pl.BlockSpec((TOK, 256), lambda i: (i, 0)), pl.BlockSpec((2, TOK, 128), lambda i: (0, i, 0))],
        out_shape=[SDS((m, 256), BF16), SDS((2, m, 128), BF16)], compiler_params=_cp(("parallel",)),
    )(p, ln_g, ln_b)


COLB = 2048


def _colmix_fwd(vnc, ws23, bs23):
    rows = vnc.shape[2] // COLB

    def body(v_ref, w_ref, b_ref, o_ref):
        o_ref[0] = _nn(w_ref[0], v_ref[0]) + b_ref[0]

    return _pcall(
        body, name="colmix_fwd", grid=(2, rows),
        in_specs=[pl.BlockSpec((1, AC, COLB), lambda g, j: (g, 0, j)), pl.BlockSpec((1, AC, AC), lambda g, j: (g, 0, 0)),
                  pl.BlockSpec((1, AC, 1), lambda g, j: (g, 0, 0))],
        out_specs=pl.BlockSpec((1, AC, COLB), lambda g, j: (g, 0, j)),
        out_shape=SDS(vnc.shape, F32), compiler_params=_cp(("parallel", "parallel")),
    )(vnc, ws23, bs23)


def _colmix_bwd(dsvc, vnc, ws23t):
    rows = vnc.shape[2] // COLB

    def body(d_ref, v_ref, wt_ref, dv_ref, dw_ref, db_ref):
        j = pl.program_id(1)

        @pl.when(j == 0)
        def _():
            dw_ref[...] = jnp.zeros_like(dw_ref)
            db_ref[...] = jnp.zeros_like(db_ref)

        d = d_ref[0]
        d16 = d.astype(BF16)
        dv_ref[0] = _nn(wt_ref[0], d16)
        dw_ref[0] += _nt(d16, v_ref[0])
        db_ref[0] += jnp.sum(d, axis=1, keepdims=True)

    blk = pl.BlockSpec((1, AC, COLB), lambda g, j: (g, 0, j))
    return _pcall(
        body, name="colmix_bwd", grid=(2, rows),
        in_specs=[blk, blk, pl.BlockSpec((1, AC, AC), lambda g, j: (g, 0, 0))],
        out_specs=[blk, pl.BlockSpec((1, AC, AC), lambda g, j: (g, 0, 0)), pl.BlockSpec((1, AC, 1), lambda g, j: (g, 0, 0))],
        out_shape=[SDS(vnc.shape, F32), SDS((2, AC, AC), F32), SDS((2, AC, 1), F32)],
        compiler_params=_cp(("parallel", "arbitrary")),
    )(dsvc, vnc, ws23t)


def _head_norm(o, gbn):
    out = []
    for h in range(4):
        oh = o[:, 128 * h:128 * h + 128]
        r = lax.rsqrt(jnp.mean(oh * oh, axis=-1, keepdims=True) + EPS)
        out.append((r, oh * r))
    return out


def _mid_fwd(o_f, o_b, p, vnr, svc, ws01, bs01, gbn):
    m = p.shape[0]

    def body(of_ref, ob_ref, zb_ref, ua_ref, za_ref, vnr_ref, svc_ref, w_ref, b_ref, g_ref, ya_ref, yb_ref, svr_ref):
        o = of_ref[...] + ob_ref[...]
        zb = zb_ref[...]
        parts = []
        for h, (r, xh) in enumerate(_head_norm(o, None)):
            parts.append(xh * g_ref[:, 128 * h:128 * h + 128])
        on = jnp.concatenate(parts, axis=1)
        yb_ref[...] = (on * _silu(zb)).astype(BF16)
        for j in range(TOK // AC):
            for g in range(2):
                sv = _nn(w_ref[g], vnr_ref[AC * j:AC * j + AC, AC * g:AC * g + AC]) + b_ref[g]
                svr_ref[AC * j:AC * j + AC, AC * g:AC * g + AC] = sv
        sz = _silu(za_ref[...])
        u = ua_ref[...]
        ya_ref[:, 0:256] = ((u[:, 0:256] * svr_ref[...]) * sz[:, 0:256]).astype(BF16)
        ya_ref[:, 256:384] = ((u[:, 256:384] * svc_ref[0]) * sz[:, 256:384]).astype(BF16)
        ya_ref[:, 384:512] = ((u[:, 384:512] * svc_ref[1]) * sz[:, 384:512]).astype(BF16)

    r512 = pl.BlockSpec((TOK, 512), lambda i: (i, 0))
    return _pcall(
        body, name="mid_fwd", grid=(m // TOK,),
        in_specs=[r512, r512, pl.BlockSpec((TOK, 512), lambda i: (i, 6)), pl.BlockSpec((TOK, 512), lambda i: (i, 7)),
                  pl.BlockSpec((TOK, 512), lambda i: (i, 8)), pl.BlockSpec((TOK, 256), lambda i: (i, 0)),
                  pl.BlockSpec((2, TOK, 128), lambda i: (0, i, 0)), _full((2, AC, AC)), _full((2, AC, 1)), _full((1, 512))],
        out_specs=[r512, r512, pl.BlockSpec((TOK, 256), lambda i: (i, 0))],
        out_shape=[SDS((m, 512), BF16), SDS((m, 512), BF16), SDS((m, 256), F32)],
        compiler_params=_cp(("parallel",)),
    )(o_f, o_b, p, p, p, vnr, svc, ws01, bs01, gbn)


def _merge_fwd(p, ya, yb):
    m = p.shape[0]

    def body(ga_ref, gb_ref, ya_ref, yb_ref, m_ref):
        m_ref[...] = (jax.nn.sigmoid(ga_ref[...]) * ya_ref[...] + jax.nn.sigmoid(gb_ref[...]) * yb_ref[...]).astype(BF16)

    row = pl.BlockSpec((TOK, D), lambda i: (i, 0))
    return _pcall(
        body, name="merge_fwd", grid=(m // TOK,),
        in_specs=[row, pl.BlockSpec((TOK, D), lambda i: (i, 1)), row, row], out_specs=row,
        out_shape=SDS((m, D), BF16), compiler_params=_cp(("parallel",)),
    )(p, p, ya, yb)


def _loss_head(x, out, tgt, gate, gf):
    m = x.shape[0]

    def body(x_ref, o_ref, t_ref, gate_ref, gf_ref, dx1_ref, dout_ref, loss_ref, dgate_ref, dgf_ref):
        i = pl.program_id(0)

        @pl.when(i == 0)
        def _():
            loss_ref[...] = jnp.zeros_like(loss_ref)
            dgate_ref[...] = jnp.zeros_like(dgate_ref)
            dgf_ref[...] = jnp.zeros_like(dgf_ref)

        out_ = o_ref[...]
        x1 = x_ref[...] + gate_ref[...] * out_
        r = lax.rsqrt(jnp.mean(x1 * x1, axis=-1, keepdims=True) + EPS)
        xh = x1 * r
        err = xh * gf_ref[...] - t_ref[...]
        loss_ref[...] += 0.5 * jnp.sum(jnp.mean(err * err, axis=-1, keepdims=True), axis=0, keepdims=True)
        dy = err * (1.0 / D)
        dgf_ref[...] += jnp.sum(dy * xh, axis=0, keepdims=True)
        dxh = dy * gf_ref[...]
        dx1 = r * (dxh - xh * jnp.mean(dxh * xh, axis=-1, keepdims=True))
        dx1_ref[...] = dx1
        dout_ref[...] = (gate_ref[...] * dx1).astype(BF16)
        dgate_ref[...] += jnp.sum(dx1 * out_, axis=0, keepdims=True)

    row = pl.BlockSpec((TOK, D), lambda i: (i, 0))
    vec = _full((1, D))
    return _pcall(
        body, name="loss_head", grid=(m // TOK,), in_specs=[row, row, row, vec, vec],
        out_specs=[row, row, _full((1, 128)), vec, vec],
        out_shape=[SDS((m, D), F32), SDS((m, D), BF16), SDS((1, 128), F32), SDS((1, D), F32), SDS((1, D), F32)],
        compiler_params=_cp(("arbitrary",)),
    )(x, out, tgt, gate, gf)


def _merge_bwd(dm, ya, yb, p):
    m = p.shape[0]

    def body(dm_ref, ya_ref, yb_ref, ga_ref, gb_ref, dya_ref, dyb_ref, dp_ref):
        dm_ = dm_ref[...]
        sa = jax.nn.sigmoid(ga_ref[...])
        sb = jax.nn.sigmoid(gb_ref[...])
        dya_ref[...] = (dm_ * sa).astype(BF16)
        dyb_ref[...] = (dm_ * sb).astype(BF16)
        dp_ref[:, 0:D] = (dm_ * ya_ref[...] * (sa * (1.0 - sa))).astype(BF16)
        dp_ref[:, D:2 * D] = (dm_ * yb_ref[...] * (sb * (1.0 - sb))).astype(BF16)

    row = pl.BlockSpec((TOK, D), lambda i: (i, 0))
    return _pcall(
        body, name="merge_bwd", grid=(m // TOK,),
        in_specs=[row, row, row, row, pl.BlockSpec((TOK, D), lambda i: (i, 1))],
        out_specs=[row, row, pl.BlockSpec((TOK, 2 * D), lambda i: (i, 0))],
        out_shape=[SDS((m, D), BF16), SDS((m, D), BF16), SDS((m, NP), BF16)],
        compiler_params=_cp(("parallel",)),
    )(dm, ya, yb, p, p)


def _mid_bwd(dya_in, dyb_in, p, svr, svc, o_f, o_b, gbn, dp):
    m = p.shape[0]

    def body(dya_ref, dyb_ref, zb_ref, ua_ref, za_ref, svr_ref, svc_ref, of_ref, ob_ref, g_ref, dpi_ref,
             dp_ref, dsr_ref, dsc_ref, do_ref, dg_ref):
        i = pl.program_id(0)

        @pl.when(i == 0)
        def _():
            dg_ref[...] = jnp.zeros_like(dg_ref)

        dya = dya_ref[...]
        u = ua_ref[...]
        za = za_ref[...]
        sz = _silu(za)
        sv = jnp.concatenate([svr_ref[...], svc_ref[0], svc_ref[1]], axis=1)
        dp_ref[:, 512:1024] = (dya * sv * sz).astype(BF16)
        dsv = dya * u * sz
        dsr_ref[...] = dsv[:, 0:256]
        dsc_ref[0] = dsv[:, 256:384]
        dsc_ref[1] = dsv[:, 384:512]
        dp_ref[:, 1024:1536] = (dya * u * sv * _dsilu(za)).astype(BF16)

        dyb = dyb_ref[...]
        zb = zb_ref[...]
        o = of_ref[...] + ob_ref[...]
        szb = _silu(zb)
        dszb = _dsilu(zb)
        for h, (r, xh) in enumerate(_head_norm(o, None)):
            sl = slice(128 * h, 128 * h + 128)
            gh = g_ref[:, sl]
            don = dyb[:, sl] * szb[:, sl]
            dp_ref[:, sl] = (dyb[:, sl] * (xh * gh) * dszb[:, sl]).astype(BF16)
            dg_ref[:, sl] += jnp.sum(don * xh, axis=0, keepdims=True)
            dxh = don * gh
            do_ref[:, sl] = r * (dxh - xh * jnp.mean(dxh * xh, axis=-1, keepdims=True))

    r512 = pl.BlockSpec((TOK, 512), lambda i: (i, 0))
    return _pcall(
        body, name="mid_bwd", grid=(m // TOK,),
        in_specs=[r512, r512, pl.BlockSpec((TOK, 512), lambda i: (i, 6)), pl.BlockSpec((TOK, 512), lambda i: (i, 7)),
                  pl.BlockSpec((TOK, 512), lambda i: (i, 8)), pl.BlockSpec((TOK, 256), lambda i: (i, 0)),
                  pl.BlockSpec((2, TOK, 128), lambda i: (0, i, 0)), r512, r512, _full((1, 512)),
                  pl.BlockSpec(memory_space=pl.ANY)],
        out_specs=[pl.BlockSpec((TOK, 1536), lambda i: (i, 2)), pl.BlockSpec((TOK, 256), lambda i: (i, 0)),
                   pl.BlockSpec((2, TOK, 128), lambda i: (0, i, 0)), r512, _full((1, 512))],
        out_shape=[SDS((m, NP), BF16), SDS((m, 256), F32), SDS((2, m, 128), F32), SDS((m, 512), F32), SDS((1, 512), F32)],
        input_output_aliases={10: 0}, compiler_params=_cp(("arbitrary",)),
    )(dya_in, dyb_in, p, p, p, svr, svc, o_f, o_b, gbn, dp)


def _tail_fwd(o_f, o_b, p, vnr, svc, x, tgt, ws01, bs01, gbn, wpa, wpb, wo, gate, gf):
    m = p.shape[0]

    def body(of_ref, ob_ref, zb_ref, ua_ref, za_ref, ga_ref, gb_ref, vnr_ref, svc_ref, x_ref, t_ref, w_ref, b_ref, g_ref,
             wpa_ref, wpb_ref, wo_ref, gate_ref, gf_ref,
             ya_ref, yb_ref, svr_ref, m_ref, dx1_ref, dout_ref, loss_ref, dgate_ref, dgf_ref):
        i = pl.program_id(0)

        @pl.when(i == 0)
        def _():
            loss_ref[...] = jnp.zeros_like(loss_ref)
            dgate_ref[...] = jnp.zeros_like(dgate_ref)
            dgf_ref[...] = jnp.zeros_like(dgf_ref)

        o = of_ref[...] + ob_ref[...]
        zb = zb_ref[...].astype(F32)
        for h, (r, xh) in enumerate(_head_norm(o, None)):
            sl = slice(128 * h, 128 * h + 128)
            yb_ref[:, sl] = ((xh * g_ref[:, sl]) * _silu(zb[:, sl])).astype(BF16)
        for j in range(TOK // AC):
            for g in range(2):
                sv = _nn(w_ref[g], vnr_ref[AC * j:AC * j + AC, AC * g:AC * g + AC]) + b_ref[g]
                svr_ref[AC * j:AC * j + AC, AC * g:AC * g + AC] = sv
        sz = _silu(za_ref[...].astype(F32))
        u = ua_ref[...].astype(F32)
        ya_ref[:, 0:256] = ((u[:, 0:256] * svr_ref[...]) * sz[:, 0:256]).astype(BF16)
        ya_ref[:, 256:384] = ((u[:, 256:384] * svc_ref[0]) * sz[:, 256:384]).astype(BF16)
        ya_ref[:, 384:512] = ((u[:, 384:512] * svc_ref[1]) * sz[:, 384:512]).astype(BF16)
        ya = _nn(ya_ref[...], wpa_ref[...])
        yb = _nn(yb_ref[...], wpb_ref[...])
        mg = (jax.nn.sigmoid(ga_ref[...].astype(F32)) * ya + jax.nn.sigmoid(gb_ref[...].astype(F32)) * yb).astype(BF16)
        m_ref[...] = mg
        out_ = _nn(mg, wo_ref[...])
        x1 = x_ref[...] + gate_ref[...] * out_
        r = lax.rsqrt(jnp.mean(x1 * x1, axis=-1, keepdims=True) + EPS)
        xh = x1 * r
        err = xh * gf_ref[...] - t_ref[...]
        loss_ref[...] += 0.5 * jnp.sum(jnp.mean(err * err, axis=-1, keepdims=True), axis=0, keepdims=True)
        dy = err * (1.0 / D)
        dgf_ref[...] += jnp.sum(dy * xh, axis=0, keepdims=True)
        dxh = dy * gf_ref[...]
        dx1 = r * (dxh - xh * jnp.mean(dxh * xh, axis=-1, keepdims=True))
        dx1_ref[...] = dx1
        dout_ref[...] = (gate_ref[...] * dx1).astype(BF16)
        dgate_ref[...] += jnp.sum(dx1 * out_, axis=0, keepdims=True)

    r512 = pl.BlockSpec((TOK, 512), lambda i: (i, 0))
    row = pl.BlockSpec((TOK, D), lambda i: (i, 0))
    vec = _full((1, D))
    return _pcall(
        body, name="tail_fwd", grid=(m // TOK,),
        in_specs=[r512, r512, pl.BlockSpec((TOK, 512), lambda i: (i, 6)), pl.BlockSpec((TOK, 512), lambda i: (i, 7)),
                  pl.BlockSpec((TOK, 512), lambda i: (i, 8)), row, pl.BlockSpec((TOK, D), lambda i: (i, 1)),
                  pl.BlockSpec((TOK, 256), lambda i: (i, 0)), pl.BlockSpec((2, TOK, 128), lambda i: (0, i, 0)), row, row,
                  _full((2, AC, AC)), _full((2, AC, 1)), _full((1, 512)), _full((512, D)), _full((512, D)), _full((D, D)), vec, vec],
        out_specs=[r512, r512, pl.BlockSpec((TOK, 256), lambda i: (i, 0)), row, row, row, _full((1, 128)), vec, vec],
        out_shape=[SDS((m, 512), BF16), SDS((m, 512), BF16), SDS((m, 256), F32), SDS((m, D), BF16), SDS((m, D), F32),
                   SDS((m, D), BF16), SDS((1, 128), F32), SDS((1, D), F32), SDS((1, D), F32)],
        compiler_params=_cp(("arbitrary",), VMEM_BIG),
    )(o_f, o_b, p, p, p, p, p, vnr, svc, x, tgt, ws01, bs01, gbn, wpa, wpb, wo, gate, gf)


DPR = 3072


def _tail_bwd(dout, ya_in, yb_in, p, svr, svc, o_f, o_b, gbn, wot, wpa, wpb, wpat, wpbt):
    m = p.shape[0]

    def body(dout_ref, ya_ref, yb_ref, ga_ref, gb_ref, zb_ref, ua_ref, za_ref, svr_ref, svc_ref, of_ref, ob_ref, g_ref,
             wot_ref, wpa_ref, wpb_ref, wpat_ref, wpbt_ref,
             dya_ref, dyb_ref, dpg_ref, dpr_ref, dsr_ref, dsc_ref, do_ref, dg_ref):
        i = pl.program_id(0)

        @pl.when(i == 0)
        def _():
            dg_ref[...] = jnp.zeros_like(dg_ref)

        dm_ = _nn(dout_ref[...], wot_ref[...])
        ya = _nn(ya_ref[...], wpa_ref[...])
        yb = _nn(yb_ref[...], wpb_ref[...])
        sa = jax.nn.sigmoid(ga_ref[...].astype(F32))
        sb = jax.nn.sigmoid(gb_ref[...].astype(F32))
        dya16 = (dm_ * sa).astype(BF16)
        dyb16 = (dm_ * sb).astype(BF16)
        dya_ref[...] = dya16
        dyb_ref[...] = dyb16
        dpg_ref[:, 0:D] = (dm_ * ya * (sa * (1.0 - sa))).astype(BF16)
        dpg_ref[:, D:2 * D] = (dm_ * yb * (sb * (1.0 - sb))).astype(BF16)
        dya = _nn(dya16, wpat_ref[...])
        dyb = _nn(dyb16, wpbt_ref[...])

        u = ua_ref[...].astype(F32)
        za = za_ref[...].astype(F32)
        sz = _silu(za)
        sv = jnp.concatenate([svr_ref[...], svc_ref[0], svc_ref[1]], axis=1)
        dpr_ref[:, 512:1024] = (dya * sv * sz).astype(BF16)
        dsv = dya * u * sz
        dsr_ref[...] = dsv[:, 0:256]
        dsc_ref[0] = dsv[:, 256:384]
        dsc_ref[1] = dsv[:, 384:512]
        dpr_ref[:, 1024:1536] = (dya * u * sv * _dsilu(za)).astype(BF16)

        zb = zb_ref[...].astype(F32)
        o = of_ref[...] + ob_ref[...]
        szb = _silu(zb)
        dszb = _dsilu(zb)
        for h, (r, xh) in enumerate(_head_norm(o, None)):
            sl = slice(128 * h, 128 * h + 128)
            gh = g_ref[:, sl]
            don = dyb[:, sl] * szb[:, sl]
            dpr_ref[:, sl] = (dyb[:, sl] * (xh * gh) * dszb[:, sl]).astype(BF16)
            dg_ref[:, sl] += jnp.sum(don * xh, axis=0, keepdims=True)
            dxh = don * gh
            do_ref[:, sl] = r * (dxh - xh * jnp.mean(dxh * xh, axis=-1, keepdims=True))

    r512 = pl.BlockSpec((TOK, 512), lambda i: (i, 0))
    row = pl.BlockSpec((TOK, D), lambda i: (i, 0))
    return _pcall(
        body, name="tail_bwd", grid=(m // TOK,),
        in_specs=[row, r512, r512, row, pl.BlockSpec((TOK, D), lambda i: (i, 1)), pl.BlockSpec((TOK, 512), lambda i: (i, 6)),
                  pl.BlockSpec((TOK, 512), lambda i: (i, 7)), pl.BlockSpec((TOK, 512), lambda i: (i, 8)),
                  pl.BlockSpec((TOK, 256), lambda i: (i, 0)), pl.BlockSpec((2, TOK, 128), lambda i: (0, i, 0)), r512, r512,
                  _full((1, 512)), _full((D, D)), _full((512, D)), _full((512, D)), _full((D, 512)), _full((D, 512))],
        out_specs=[row, row, pl.BlockSpec((TOK, 2 * D), lambda i: (i, 0)), pl.BlockSpec((TOK, 1536), lambda i: (i, 0)),
                   pl.BlockSpec((TOK, 256), lambda i: (i, 0)), pl.BlockSpec((2, TOK, 128), lambda i: (0, i, 0)), r512, _full((1, 512))],
        out_shape=[SDS((m, D), BF16), SDS((m, D), BF16), SDS((m, 2 * D), BF16), SDS((m, DPR), BF16), SDS((m, 256), F32),
                   SDS((2, m, 128), F32), SDS((m, 512), F32), SDS((1, 512), F32)],
        compiler_params=_cp(("arbitrary",), VMEM_BIG),
    )(dout, ya_in, yb_in, p, p, p, p, p, svr, svc, o_f, o_b, gbn, wot, wpa, wpb, wpat, wpbt)


def _mm_multi(pairs, *, tm, tn, out_dtype, name):
    m = pairs[0][0].shape[0]
    n = pairs[0][1].shape[1]
    nks = [a.shape[1] // tk for a, _, tk in pairs]
    starts = [sum(nks[:i]) for i in range(len(pairs))]
    total = sum(nks)

    def body(*refs):
        o_ref, acc_ref = refs[-2], refs[-1]
        kk = pl.program_id(2)
        for idx in range(len(pairs)):
            a_ref, b_ref = refs[2 * idx], refs[2 * idx + 1]

            @pl.when((kk >= starts[idx]) & (kk < starts[idx] + nks[idx]))
            def _(a_ref=a_ref, b_ref=b_ref, first=(idx == 0)):
                part = _nn(a_ref[...].astype(BF16), b_ref[...].astype(BF16))
                if first:
                    @pl.when(kk == 0)
                    def _():
                        acc_ref[...] = part

                    @pl.when(kk > 0)
                    def _():
                        acc_ref[...] += part
                else:
                    acc_ref[...] += part

        @pl.when(kk == total - 1)
        def _():
            o_ref[...] = acc_ref[...].astype(out_dtype)

    in_specs, args = [], []
    for (a, b, tk), st, nk in zip(pairs, starts, nks):
        in_specs.append(pl.BlockSpec((tm, tk), lambda i, j, kk, st=st, nk=nk: (i, jnp.clip(kk - st, 0, nk - 1))))
        in_specs.append(pl.BlockSpec((tk, tn), lambda i, j, kk, st=st, nk=nk: (jnp.clip(kk - st, 0, nk - 1), j)))
        args += [a, b]
    return _pcall(
        body, name=name, grid=(m // tm, n // tn, total), in_specs=in_specs,
        out_specs=pl.BlockSpec((tm, tn), lambda i, j, kk: (i, j)), out_shape=SDS((m, n), out_dtype),
        scratch_shapes=[pltpu.VMEM((tm, tn), F32)], compiler_params=_cp(("parallel", "parallel", "arbitrary"), VMEM_BIG),
    )(*args)


def _ln_bwd(dsr, vnr, dvnc, p, ws01t, ln_g, dp):
    m = p.shape[0]

    def body(dsr_ref, vnr_ref, dvc_ref, va_ref, wt_ref, g_ref, dpi_ref, dp_ref, dw_ref, db_ref, dlg_ref, dlb_ref, dvn_ref):
        i = pl.program_id(0)

        @pl.when(i == 0)
        def _():
            dw_ref[...] = jnp.zeros_like(dw_ref)
            db_ref[...] = jnp.zeros_like(db_ref)
            dlg_ref[...] = jnp.zeros_like(dlg_ref)
            dlb_ref[...] = jnp.zeros_like(dlb_ref)

        for j in range(TOK // AC):
            for g in range(2):
                d = dsr_ref[AC * j:AC * j + AC, AC * g:AC * g + AC]
                d16 = d.astype(BF16)
                dvn_ref[AC * j:AC * j + AC, AC * g:AC * g + AC] = _nn(wt_ref[g], d16)
                dw_ref[g] += _nt(d16, vnr_ref[AC * j:AC * j + AC, AC * g:AC * g + AC])
                db_ref[g] += jnp.sum(d, axis=1, keepdims=True)
        dvn_ref[:, 256:384] = dvc_ref[0]
        dvn_ref[:, 384:512] = dvc_ref[1]
        dvn = dvn_ref[...]
        xf = va_ref[...].astype(F32)
        xc = xf - jnp.mean(xf, axis=-1, keepdims=True)
        rs = lax.rsqrt(jnp.mean(xc * xc, axis=-1, keepdims=True) + EPS)
        xh = xc * rs
        dlg_ref[...] += jnp.sum(dvn * xh, axis=0, keepdims=True)
        dlb_ref[...] += jnp.sum(dvn, axis=0, keepdims=True)
        dxh = dvn * g_ref[...]
        dva = rs * (dxh - jnp.mean(dxh, axis=-1, keepdims=True) - xh * jnp.mean(dxh * xh, axis=-1, keepdims=True))
        dp_ref[...] = dva.astype(BF16)

    return _pcall(
        body, name="ln_bwd", grid=(m // TOK,),
        in_specs=[pl.BlockSpec((TOK, 256), lambda i: (i, 0)), pl.BlockSpec((TOK, 256), lambda i: (i, 0)),
                  pl.BlockSpec((2, TOK, 128), lambda i: (0, i, 0)), pl.BlockSpec((TOK, 512), lambda i: (i, 9)),
                  _full((2, AC, AC)), _full((1, 512)), pl.BlockSpec(memory_space=pl.ANY)],
        out_specs=[pl.BlockSpec((TOK, 512), lambda i: (i, 3)), _full((2, AC, AC)), _full((2, AC, 1)), _full((1, 512)), _full((1, 512))],
        out_shape=[SDS((m, DPR), BF16), SDS((2, AC, AC), F32), SDS((2, AC, 1), F32), SDS((1, 512), F32), SDS((1, 512), F32)],
        scratch_shapes=[pltpu.VMEM((TOK, 512), F32)],
        input_output_aliases={6: 0}, compiler_params=_cp(("arbitrary",)),
    )(dsr, vnr, dvnc, p, ws01t, ln_g, dp)


def _tri_mm(tri, a):
    a1 = a.astype(BF16)
    r1 = a - a1.astype(F32)
    a2 = r1.astype(BF16)
    a3 = (r1 - a2.astype(F32)).astype(BF16)
    n = a.shape[1]
    r = _nn(tri, jnp.concatenate([a1, a2, a3], axis=1))
    return r[:, 0:n] + r[:, n:2 * n] + r[:, 2 * n:3 * n]


def _gla_masks(reverse):
    ri = lax.broadcasted_iota(jnp.int32, (CH, CH), 0)
    ci = lax.broadcasted_iota(jnp.int32, (CH, CH), 1)
    vis = (ci >= ri) if reverse else (ci <= ri)
    vis_t = (ci <= ri) if reverse else (ci >= ri)
    r4 = lax.broadcasted_iota(jnp.int32, (4 * CH, CH), 0) & (CH - 1)
    c4 = lax.broadcasted_iota(jnp.int32, (4 * CH, CH), 1)
    vis4 = (c4 >= r4) if reverse else (c4 <= r4)
    vis4_t = (c4 <= r4) if reverse else (c4 >= r4)
    lane = lax.broadcasted_iota(jnp.int32, (1, 256), 1)
    hm = [(lane >= CH * h) & (lane < CH * h + CH) for h in range(4)]
    return vis, vis_t, vis4, vis4_t, hm


def _stack_heads(x, hm):
    return jnp.concatenate([jnp.where(hm[h], x, 0.0).astype(BF16) for h in range(4)], axis=0)


def _diag_heads(full, hm):
    r = full.shape[0] // 4
    acc = jnp.where(hm[0], full[0:r], 0.0)
    for h in range(1, 4):
        acc = acc + jnp.where(hm[h], full[r * h:r * h + r], 0.0)
    return acc


def _rows_of_heads(x):
    return jnp.concatenate([x[:, 128 * h:128 * h + 128] for h in range(4)], axis=0)


def _lane_vis(reverse, transpose):
    ri = lax.broadcasted_iota(jnp.int32, (CH, 4 * CH), 0)
    ci = lax.broadcasted_iota(jnp.int32, (CH, 4 * CH), 1) & (CH - 1)
    return (ci >= ri) if (reverse != transpose) else (ci <= ri)


def _gla_fwd(p, qkv_blk, lr, lrw, gbias, s0, *, reverse, name):
    m = p.shape[0]
    nb = m // GLA_TB
    nc = GLA_TB // CH
    rmap = (lambda i: nb - 1 - i) if reverse else (lambda i: i)

    def body(qkv_ref, lr_ref, lrw_ref, gb_ref, s0_ref, o_ref, sb_ref, sfin_ref, st_ref):
        i = pl.program_id(0)

        @pl.when(i == 0)
        def _():
            st_ref[...] = s0_ref[...]

        vis, _, vis4, _, hm = _gla_masks(reverse)
        tri = vis.astype(F32).astype(BF16)
        logits = _nn(lr_ref[...].astype(BF16), lrw_ref[...]) + gb_ref[...]
        a_all = _logsig(logits) * (1.0 / 16.0)
        st = st_ref[...]
        for c in (range(nc - 1, -1, -1) if reverse else range(nc)):
            rows = slice(CH * c, CH * c + CH)
            b = _tri_mm(tri, a_all[rows])
            bl = b[0:1] if reverse else b[CH - 1:CH]
            q = qkv_ref[rows, 0:256].astype(F32) * 0.125
            k = qkv_ref[rows, 256:512].astype(F32)
            v16 = qkv_ref[rows, 512:1024].astype(BF16)
            qd = q * jnp.exp(b)
            kd16 = (k * jnp.exp(-b)).astype(BF16)
            kdec16 = (k * jnp.exp(bl - b)).astype(BF16)
            qstack = _stack_heads(qd, hm)
            sc = jnp.where(vis4, _nt(qstack, kd16), 0.0).astype(BF16)
            inter = _nt(qstack, st.astype(BF16))
            for h in range(4):
                o_ref[rows, 128 * h:128 * h + 128] = (
                    _nn(sc[CH * h:CH * h + CH], v16[:, 128 * h:128 * h + 128]) + inter[CH * h:CH * h + CH])
            sb_ref[c] = st
            st = st * jnp.exp(bl) + _diag_heads(_tn(v16, kdec16), hm)
        st_ref[...] = st

        @pl.when(i == nb - 1)
        def _():
            sfin_ref[...] = st

    return _pcall(
        body, name=name, grid=(nb,),
        in_specs=[pl.BlockSpec((GLA_TB, 1024), lambda i: (rmap(i), qkv_blk)), pl.BlockSpec((GLA_TB, LRW), lambda i: (rmap(i), 0)),
                  _full((LRW, 256)), _full((1, 256)), _full((128, 256))],
        out_specs=[pl.BlockSpec((GLA_TB, 512), lambda i: (rmap(i), 0)), pl.BlockSpec((nc, 128, 256), lambda i: (rmap(i), 0, 0)),
                   _full((128, 256))],
        out_shape=[SDS((m, 512), F32), SDS((m // CH, 128, 256), F32), SDS((128, 256), F32)],
        scratch_shapes=[pltpu.VMEM((128, 256), F32)], compiler_params=_cp(("arbitrary",)),
    )(p, lr, lrw, gbias, s0)


def _gla_bwd(p, qkv_blk, lr, lrw, lrwt, gbias, sb, dsfin, do, prev, dp, *, reverse, name):
    m = p.shape[0]
    nb = m // GLA_TB
    nc = GLA_TB // CH
    rmap = (lambda i: i) if reverse else (lambda i: nb - 1 - i)
    has_prev = prev is not None
    has_dp = dp is not None

    def body(*refs):
        refs = list(refs)
        qkv_ref, lr_ref, lrw_ref, lrwt_ref, gb_ref, sb_ref, dsfin_ref, do_ref = refs[:8]
        refs = refs[8:]
        if has_prev:
            pq_ref, plr_ref = refs[:2]
            refs = refs[2:]
        if has_dp:
            refs = refs[1:]
        dqkv_ref, dlr_ref, dw2_ref, dgb_ref, ds0_ref, dst_ref, dlog_ref = refs
        i = pl.program_id(0)

        @pl.when(i == 0)
        def _():
            dst_ref[...] = dsfin_ref[...]
            dw2_ref[...] = jnp.zeros_like(dw2_ref)
            dgb_ref[...] = jnp.zeros_like(dgb_ref)

        vis, vis_t, vis4, vis4_t, hm = _gla_masks(reverse)
        tri = vis.astype(F32).astype(BF16)
        tri_t = vis_t.astype(F32).astype(BF16)
        lane_vis = _lane_vis(reverse, False)
        lane_vis_t = _lane_vis(reverse, True)
        lr16 = lr_ref[...].astype(BF16)
        logits = _nn(lr16, lrw_ref[...]) + gb_ref[...]
        a_all = _logsig(logits) * (1.0 / 16.0)
        dsig = (1.0 - jax.nn.sigmoid(logits)) * (1.0 / 16.0)
        dst = dst_ref[...]
        for c in (range(nc) if reverse else range(nc - 1, -1, -1)):
            rows = slice(CH * c, CH * c + CH)
            b = _tri_mm(tri, a_all[rows])
            bl = b[0:1] if reverse else b[CH - 1:CH]
            eb = jnp.exp(b)
            enb = jnp.exp(-b)
            ebl = jnp.exp(bl - b)
            el = jnp.exp(bl)
            q = qkv_ref[rows, 0:256].astype(F32) * 0.125
            k = qkv_ref[rows, 256:512].astype(F32)
            v16 = qkv_ref[rows, 512:1024].astype(BF16)
            do16 = do_ref[rows, :].astype(BF16)
            qd = q * eb
            kd = k * enb
            kdec = k * ebl
            st = sb_ref[c]
            st16 = st.astype(BF16)
            dst16 = dst.astype(BF16)
            qd16 = qd.astype(BF16)
            kd16 = kd.astype(BF16)
            qstack = _stack_heads(qd, hm)
            kstack = _stack_heads(kd, hm)
            kdecstack = _stack_heads(kdec, hm)
            pt = jnp.where(vis4_t, _nt(kstack, qd16), 0.0).astype(BF16)
            dvinter = _nt(kdecstack, dst16)
            do_rows = _rows_of_heads(do16)
            v_rows = _rows_of_heads(v16)
            dp_cat = jnp.where(lane_vis, _diag_heads(_nt(do_rows, v_rows), hm), 0.0).astype(BF16)
            dpt_cat = jnp.where(lane_vis_t, _diag_heads(_nt(v_rows, do_rows), hm), 0.0).astype(BF16)
            dqd = _nn(dp_cat, kstack) + _diag_heads(_nn(do_rows, st16), hm)
            dkd = _nn(dpt_cat, qstack)
            dkdec = _diag_heads(_nn(v_rows, dst16), hm)
            for h in range(4):
                rh = slice(CH * h, CH * h + CH)
                dv_h = _nn(pt[rh], do_rows[rh]) + dvinter[rh]
                if has_prev:
                    dv_h = dv_h + pq_ref[rows, 512 + 128 * h:512 + 128 * h + 128]
                dqkv_ref[rows, 512 + 128 * h:512 + 128 * h + 128] = dv_h.astype(dqkv_ref.dtype)
            dq = dqd * eb * 0.125
            dk = dkd * enb + dkdec * ebl
            if has_prev:
                dq = dq + pq_ref[rows, 0:256]
                dk = dk + pq_ref[rows, 256:512]
            dqkv_ref[rows, 0:256] = dq.astype(dqkv_ref.dtype)
            dqkv_ref[rows, 256:512] = dk.astype(dqkv_ref.dtype)
            g_kdec = dkdec * kdec
            db = dqd * qd - dkd * kd - g_kdec
            dbl = jnp.sum(g_kdec, axis=0, keepdims=True) + jnp.sum(st * dst, axis=0, keepdims=True) * el
            da = _tri_mm(tri_t, db) + dbl
            dlog_ref[rows, :] = da * dsig[rows]
            dst = dst * el + _diag_heads(_tn(do16, qd16), hm)
        dst_ref[...] = dst
        dlog = dlog_ref[...]
        dlog16 = dlog.astype(BF16)
        dlr = _nn(dlog16, lrwt_ref[...])
        if has_prev:
            dlr = dlr + plr_ref[...]
        dlr_ref[...] = dlr
        dw2_ref[...] += _tn(lr16, dlog16)
        dgb_ref[...] += jnp.sum(dlog, axis=0, keepdims=True)

        @pl.when(i == nb - 1)
        def _():
            ds0_ref[...] = dst

    in_specs = [pl.BlockSpec((GLA_TB, 1024), lambda i: (rmap(i), qkv_blk)), pl.BlockSpec((GLA_TB, LRW), lambda i: (rmap(i), 0)),
                _full((LRW, 256)), _full((256, LRW)), _full((1, 256)), pl.BlockSpec((nc, 128, 256), lambda i: (rmap(i), 0, 0)),
                _full((128, 256)), pl.BlockSpec((GLA_TB, 512), lambda i: (rmap(i), 0))]
    args = [p, lr, lrw, lrwt, gbias, sb, dsfin, do]
    if has_prev:
        in_specs += [pl.BlockSpec((GLA_TB, 1024), lambda i: (rmap(i), 0)), pl.BlockSpec((GLA_TB, LRW), lambda i: (rmap(i), 0))]
        args += list(prev)
    aliases = {}
    if has_dp:
        in_specs.append(pl.BlockSpec(memory_space=pl.ANY))
        aliases = {len(args): 0}
        args.append(dp)
        dq_spec = pl.BlockSpec((GLA_TB, 1024), lambda i: (rmap(i), 2))
        dq_shape = SDS(dp.shape, dp.dtype)
    else:
        dq_spec = pl.BlockSpec((GLA_TB, 1024), lambda i: (rmap(i), 0))
        dq_shape = SDS((m, 1024), F32)
    return _pcall(
        body, name=name, grid=(nb,), in_specs=in_specs,
        out_specs=[dq_spec, pl.BlockSpec((GLA_TB, LRW), lambda i: (rmap(i), 0)), _full((LRW, 256)), _full((1, 256)), _full((128, 256))],
        out_shape=[dq_shape, SDS((m, LRW), F32), SDS((LRW, 256), F32), SDS((1, 256), F32), SDS((128, 256), F32)],
        scratch_shapes=[pltpu.VMEM((128, 256), F32), pltpu.VMEM((GLA_TB, 256), F32)],
        input_output_aliases=aliases, compiler_params=_cp(("arbitrary",)),
    )(*args)


def _device_step(x, c, ctx, c_ctx, tgt, wm, bm, ng, wit_g, wit_r, wlrt, ln_g, ln_b, ws, bs, w2, gb2, gbn, wpa, wpb, wo, gf,
                 exchange=None):
    L = x.shape[0]
    wit_qkv = wit_r[2048:3072]
    wi = jnp.concatenate([wit_g, wit_qkv, wit_r[0:2048]], axis=0).T
    wlr = wlrt.T
    ws16 = ws.astype(BF16)
    wst16 = jnp.swapaxes(ws, 1, 2).astype(BF16)
    bscol = bs[:, :, None]
    lrw = [jnp.zeros((LRW, 256), F32).at[16 * r:16 * r + 16].set(w2[r]).astype(BF16) for r in range(2)]
    lrwt = [w.T for w in lrw]
    gbias = [gb2[r:r + 1] for r in range(2)]

    cc = jnp.zeros((8, D), F32).at[0:1].set(c).at[1:2].set(c_ctx)
    mod = _modvec(cc, wm, bm)
    shift, scale, gate = mod[0:1, 0:D], mod[0:1, D:2 * D], mod[0:1, 2 * D:3 * D]
    shift_c, scale_c = mod[1:2, 0:D], mod[1:2, D:2 * D]

    hc = _prep_h(ctx, ng, scale_c, shift_c, "prep_hc")
    pc = _mm(hc, wi[:, 2048:3072], tm=256, tn=1024, tk=D, out_dtype=F32, name="mm_pc")
    plrc = _mm(hc, wlr, tm=256, tn=LRW, tk=D, out_dtype=F32, name="mm_plrc")
    zero_s = jnp.zeros((128, 256), F32)
    _, sbc_f, sc_f = _gla_fwd(pc, 0, plrc, lrw[0], gbias[0], zero_s, reverse=False, name="gla_fwd_cf")
    _, sbc_b, sc_b = _gla_fwd(pc, 0, plrc, lrw[1], gbias[1], zero_s, reverse=True, name="gla_fwd_cb")

    h, p, plr = _proj_fwd(x, ng, scale, shift, wi, wlr)
    o_f, sb_f, _ = _gla_fwd(p, 2, plr, lrw[0], gbias[0], sc_f, reverse=False, name="gla_fwd_f")
    o_b, sb_b, _ = _gla_fwd(p, 2, plr, lrw[1], gbias[1], sc_b, reverse=True, name="gla_fwd_b")
    vnr, vnc = _ln_fwd(p, ln_g, ln_b)
    svc = _colmix_fwd(vnc.reshape(2, AC, L), ws16[2:4], bscol[2:4]).reshape(2, L, 128)
    ya_in, yb_in, svr, mrg, dx1, dout, loss, dgate, dgf = _tail_fwd(
        o_f, o_b, p, vnr, svc, x, tgt, ws16[0:2], bscol[0:2], gbn, wpa, wpb, wo, gate, gf)

    dya, dyb, dp_g, dp, dsr, dsc, do, dgbn = _tail_bwd(dout, ya_in, yb_in, p, svr, svc, o_f, o_b, gbn, wo.T, wpa, wpb, wpa.T, wpb.T)
    dwo = _mm_tn(mrg, dout, ta=D, tn=D, tk=1024, name="mm_dwo")
    dwpa = _mm_tn(ya_in, dya, ta=512, tn=D, tk=1024, name="mm_dwpa")
    dwpb = _mm_tn(yb_in, dyb, ta=512, tn=D, tk=1024, name="mm_dwpb")
    dvnc, dws23, dbs23 = _colmix_bwd(dsc.reshape(2, AC, L), vnc.reshape(2, AC, L), wst16[2:4])
    dp, dws01, dbs01, dlng, dlnb = _ln_bwd(dsr, vnr, dvnc.reshape(2, L, 128), p, wst16[0:2], ln_g, dp)
    zero_ds = jnp.zeros((128, 256), F32)
    dqkv_f, dlr_f, dw2_f, dgb_f, ds0_f = _gla_bwd(p, 2, plr, lrw[0], lrwt[0], gbias[0], sb_f, zero_ds, do, None, None,
                                                  reverse=False, name="gla_bwd_f")
    dp, dlr, dw2_b, dgb_b, ds0_b = _gla_bwd(p, 2, plr, lrw[1], lrwt[1], gbias[1], sb_b, zero_ds, do, (dqkv_f, dlr_f), dp,
                                            reverse=True, name="gla_bwd_b")
    zero_do = jnp.zeros((ctx.shape[0], 512), F32)
    dqkvc_f, dlrc_f, dw2c_f, dgbc_f, _ = _gla_bwd(pc, 0, plrc, lrw[0], lrwt[0], gbias[0], sbc_f, ds0_f, zero_do, None, None,
                                                  reverse=False, name="gla_bwd_cf")
    dqkvc, dlrc, dw2c_b, dgbc_b, _ = _gla_bwd(pc, 0, plrc, lrw[1], lrwt[1], gbias[1], sbc_b, ds0_b, zero_do,
                                              (dqkvc_f, dlrc_f), None, reverse=True, name="gla_bwd_cb")
    dhc = _mm(dqkvc, wit_qkv, tm=256, tn=D, tk=1024, out_dtype=F32, name="mm_dhc")
    dhc = _mm(dlrc, wlrt, tm=256, tn=D, tk=LRW, out_dtype=F32, name="mm_dhc_lr", acc=dhc)
    _, dng_c, dscale_c, dshift_c = _prep_bwd(ctx, dhc, None, ng, scale_c, "prep_bwd_c")

    dwit_g = _mm_tn(dp_g, h, ta=1024, tn=D, tk=2048, name="mm_dwi_g")
    dwit_r = _mm_tn(dp, h, ta=1024, tn=D, tk=2048, name="mm_dwi_r")
    dwit_qkv = _mm_tn(dqkvc, hc, ta=1024, tn=D, tk=256, name="mm_dwi_c", acc=dwit_r[2048:3072])
    dwlrt = _mm_tn(dlr, h, ta=LRW, tn=D, tk=2048, name="mm_dwlr")
    dwlrt = _mm_tn(dlrc, hc, ta=LRW, tn=D, tk=256, name="mm_dwlr_c", acc=dwlrt)
    big = dict(dwit_g=dwit_g, dwit_r=dwit_r, dwit_qkv=dwit_qkv, dwlrt=dwlrt, dwpa=dwpa, dwpb=dwpb, dwo=dwo)

    send = exchange(big) if exchange is not None else ()
    (dx, dng, dscale, dshift), got = _proj_bwd(dp_g, dp, dlr, wit_g, wit_r, wlrt, x, dx1, ng, scale, send)

    dmodc = jnp.concatenate([dshift_c, dscale_c], axis=1)
    dscc = _dcctx(jnp.zeros((8, 2 * D), F32).at[0:1].set(dmodc), wm)[0:1]
    dw2p = dw2_f + dw2c_f, dw2_b + dw2c_b
    return dict(
        loss=loss[0, 0], dx=dx, got=got, **big,
        dmod=jnp.concatenate([dshift, dscale, dgate], axis=1), dmodc=dmodc, dscc=dscc, dng=dng + dng_c,
        dlng=dlng, dlnb=dlnb, dws=jnp.concatenate([dws01, dws23], axis=0),
        dbs=jnp.concatenate([dbs01, dbs23], axis=0)[:, :, 0], dgbn=dgbn, dgf=dgf,
        dw2=jnp.stack([dw2p[0][0:16], dw2p[1][16:32]]), dgb2=jnp.concatenate([dgb_f + dgbc_f, dgb_b + dgbc_b], axis=0),
    )


ANY = pl.BlockSpec(memory_space=pl.ANY)


def _coords():
    return lax.axis_index("x"), lax.axis_index("y"), lax.axis_index("c")


def _flip(v, bit):
    return 1 - v if bit else v


def _remote(src, dst, send_sem, recv_sem, dev):
    return pltpu.make_async_remote_copy(src_ref=src, dst_ref=dst, send_sem=send_sem, recv_sem=recv_sem,
                                        device_id=dev, device_id_type=MESH)


def _own(out, block, idx):
    return lax.dynamic_update_slice_in_dim(out, block[None], idx, axis=0)


def _half_idx(shape, axis, which, lead=()):
    idx = [pl.ds(0, d) for d in shape]
    h = shape[axis] // 2
    idx[axis] = pl.ds(which * h, h)
    return tuple(lead) + tuple(idx)


def _gather_weights(split, whole, name):
    ns, nw = len(split), len(whole)
    n = ns + nw
    arrs = [a for a, _ in split] + list(whole)

    def body(*refs):
        ins, outs = refs[:n], refs[n:2 * n]
        a_send, a_recv, b_send, b_recv = refs[2 * n:]
        x, y, c = _coords()
        me = 2 * x + y
        sib = (x, y, 1 - c)
        peers = [(1 - x, y), (x, 1 - y), (1 - x, 1 - y)]

        def half(k, slot, which):
            return outs[k].at[_half_idx(arrs[k].shape, split[k][1], which, lead=(slot,))]

        sends = []
        for k in range(n):
            for j, (px, py) in enumerate(peers):
                if k < ns:
                    rc = _remote(ins[k].at[_half_idx(arrs[k].shape, split[k][1], c)], half(k, me, c), a_send.at[3 * k + j],
                                 a_recv.at[3 * k + j], (px, py, c))
                else:
                    rc = _remote(ins[k], outs[k].at[me], a_send.at[3 * k + j], a_recv.at[3 * k + j], (px, py, c))
                rc.start()
                sends.append(rc)
        for k in range(ns):
            for j, (px, py) in enumerate(peers):
                landed = half(k, 2 * px + py, c)
                _remote(landed, landed, a_send.at[3 * k + j], a_recv.at[3 * k + j], (px, py, c)).wait_recv()
                fw = _remote(landed, landed, b_send.at[3 * k + j], b_recv.at[3 * k + j], sib)
                fw.start()
                sends.append(fw)
        for k in range(ns, n):
            for j, (px, py) in enumerate(peers):
                landed = outs[k].at[2 * px + py]
                _remote(landed, landed, a_send.at[3 * k + j], a_recv.at[3 * k + j], (px, py, c)).wait_recv()
        for k in range(ns):
            for j, (px, py) in enumerate(peers):
                passed = half(k, 2 * px + py, 1 - c)
                _remote(passed, passed, b_send.at[3 * k + j], b_recv.at[3 * k + j], sib).wait_recv()
        for rc in sends:
            rc.wait_send()

    outs = _pcall(
        body, name=name, in_specs=[ANY] * n, out_specs=[ANY] * n,
        out_shape=[SDS((4,) + a.shape, a.dtype) for a in arrs],
        scratch_shapes=[pltpu.SemaphoreType.DMA((3 * n,)), pltpu.SemaphoreType.DMA((3 * n,)), pltpu.SemaphoreType.DMA((3 * ns,)),
                        pltpu.SemaphoreType.DMA((3 * ns,))],
    )(*arrs)
    me_xy = 2 * lax.axis_index("x") + lax.axis_index("y")
    return [_own(o, a, me_xy) for o, a in zip(outs, arrs)]


def _gather_all(a, name):
    masks = [(mx, my, mc) for mx in range(2) for my in range(2) for mc in range(2)][1:]

    def body(in_ref, out_ref, send_sems, recv_sems):
        x, y, c = _coords()
        me = 4 * x + 2 * y + c
        sends = []
        for j, (mx, my, mc) in enumerate(masks):
            peer = (_flip(x, mx), _flip(y, my), _flip(c, mc))
            rc = pltpu.make_async_remote_copy(
                src_ref=in_ref, dst_ref=out_ref.at[me], send_sem=send_sems.at[j], recv_sem=recv_sems.at[j],
                device_id=peer, device_id_type=MESH)
            rc.start()
            sends.append(rc)
        for j, (mx, my, mc) in enumerate(masks):
            px, py, pc = _flip(x, mx), _flip(y, my), _flip(c, mc)
            pltpu.make_async_remote_copy(
                src_ref=in_ref, dst_ref=out_ref.at[4 * px + 2 * py + pc], send_sem=send_sems.at[j], recv_sem=recv_sems.at[j],
                device_id=(px, py, pc), device_id_type=MESH).wait_recv()
        for rc in sends:
            rc.wait_send()

    out = _pcall(
        body, name=name, in_specs=[ANY], out_specs=ANY, out_shape=SDS((8,) + a.shape, a.dtype),
        scratch_shapes=[pltpu.SemaphoreType.DMA((7,)), pltpu.SemaphoreType.DMA((7,))],
    )(a)
    return _own(out, a, 4 * lax.axis_index("x") + 2 * lax.axis_index("y") + lax.axis_index("c"))


def _half_shape(shape, axis):
    return tuple(d // 2 if i == axis else d for i, d in enumerate(shape))


def _swap_half_c(arrs, axes, name):
    n = len(arrs)

    def body(*refs):
        ins, outs = refs[:n], refs[n:2 * n]
        send_sems, recv_sems = refs[2 * n:]
        x, y, c = _coords()
        sends = []
        for k in range(n):
            rc = _remote(ins[k].at[_half_idx(arrs[k].shape, axes[k], 1 - c)], outs[k], send_sems.at[k], recv_sems.at[k],
                         (x, y, 1 - c))
            rc.start()
            sends.append(rc)
        for rc in sends:
            rc.wait()

    return _pcall(
        body, name=name, in_specs=[ANY] * n, out_specs=[ANY] * n,
        out_shape=[SDS(_half_shape(a.shape, ax), a.dtype) for a, ax in zip(arrs, axes)],
        scratch_shapes=[pltpu.SemaphoreType.DMA((n,)), pltpu.SemaphoreType.DMA((n,))],
    )(*arrs)


def _a2a_xy(arrs, name):
    n = len(arrs)

    def body(*refs):
        ins, outs = refs[:n], refs[n:2 * n]
        send_sems, recv_sems = refs[2 * n:]
        x, y, c = _coords()
        me = 2 * x + y
        peers = [(1 - x, y), (x, 1 - y), (1 - x, 1 - y)]
        sends = []
        for k in range(n):
            for j, (px, py) in enumerate(peers):
                rc = _remote(ins[k].at[2 * px + py], outs[k].at[me], send_sems.at[3 * k + j], recv_sems.at[3 * k + j], (px, py, c))
                rc.start()
                sends.append(rc)
        for k in range(n):
            for j, (px, py) in enumerate(peers):
                landed = outs[k].at[2 * px + py]
                _remote(landed, landed, send_sems.at[3 * k + j], recv_sems.at[3 * k + j], (px, py, c)).wait_recv()
        for rc in sends:
            rc.wait_send()

    outs = _pcall(
        body, name=name, in_specs=[ANY] * n, out_specs=[ANY] * n, out_shape=[SDS(a.shape, a.dtype) for a in arrs],
        scratch_shapes=[pltpu.SemaphoreType.DMA((3 * n,)), pltpu.SemaphoreType.DMA((3 * n,))],
    )(*arrs)
    me_xy = 2 * lax.axis_index("x") + lax.axis_index("y")
    return [_own(o, lax.dynamic_index_in_dim(a, me_xy, axis=0, keepdims=False), me_xy) for o, a in zip(outs, arrs)]


def _join_halves(halves, axes, name):
    n = len(halves)
    full = [tuple(2 * d if i == ax else d for i, d in enumerate(a.shape)) for a, ax in zip(halves, axes)]

    def body(*refs):
        ins, outs = refs[:n], refs[n:2 * n]
        send_sems, recv_sems = refs[2 * n:]
        x, y, c = _coords()
        sends = []
        for k in range(n):
            rc = _remote(ins[k], outs[k].at[_half_idx(full[k], axes[k], c)], send_sems.at[k], recv_sems.at[k], (x, y, 1 - c))
            rc.start()
            sends.append(rc)
        for k in range(n):
            landed = outs[k].at[_half_idx(full[k], axes[k], 1 - c)]
            _remote(landed, landed, send_sems.at[k], recv_sems.at[k], (x, y, 1 - c)).wait_recv()
        for rc in sends:
            rc.wait_send()

    outs = _pcall(
        body, name=name, in_specs=[ANY] * n, out_specs=[ANY] * n,
        out_shape=[SDS(f, a.dtype) for f, a in zip(full, halves)],
        scratch_shapes=[pltpu.SemaphoreType.DMA((n,)), pltpu.SemaphoreType.DMA((n,))],
    )(*halves)
    ci = lax.axis_index("c")
    return [lax.dynamic_update_slice_in_dim(o, a, ci * a.shape[ax], axis=ax) for o, a, ax in zip(outs, halves, axes)]


def _pair_sum(a, got, cidx, axis, name):
    _, r, cdim = a.shape
    hshape = _half_shape(a.shape, axis)

    def body(c_ref, a_ref, g_ref, o_ref):
        o_ref[...] = (a_ref[...] + g_ref[...]).astype(BF16)

    if axis == 1:
        tr = min(r // 2, 256)
        nj = (r // 2) // tr
        blk = pl.BlockSpec((1, tr, cdim), lambda s, j, c: (s, j, 0))
        a_spec = pl.BlockSpec((1, tr, cdim), lambda s, j, c: (s, c[0] * nj + j, 0))
    else:
        nj = (cdim // 2) // 128
        blk = pl.BlockSpec((1, r, 128), lambda s, j, c: (s, 0, j))
        a_spec = pl.BlockSpec((1, r, 128), lambda s, j, c: (s, 0, c[0] * nj + j))
    return _pcall(
        body, name=name, out_shape=SDS(hshape, BF16),
        grid_spec=pltpu.PrefetchScalarGridSpec(num_scalar_prefetch=1, grid=(4, nj), in_specs=[a_spec, blk], out_specs=blk),
        compiler_params=_cp(("parallel", "parallel")),
    )(cidx, a, got)


def _sum_chips(parts, name):
    _, h, cdim = parts.shape

    def body(p_ref, o_ref):
        acc = p_ref[0].astype(F32)
        for k in range(1, 4):
            acc = acc + p_ref[k].astype(F32)
        o_ref[...] = acc

    if h % 256 == 0 or h in (128,):
        tr = min(h, 256)
        grid, in_spec, out_spec = (h // tr,), pl.BlockSpec((4, tr, cdim), lambda i: (0, i, 0)), pl.BlockSpec((tr, cdim), lambda i: (i, 0))
    else:
        grid, in_spec, out_spec = (cdim // 128,), pl.BlockSpec((4, h, 128), lambda i: (0, 0, i)), pl.BlockSpec((h, 128), lambda i: (0, i))
    return _pcall(
        body, name=name, grid=grid, in_specs=[in_spec], out_specs=out_spec, out_shape=SDS((h, cdim), F32),
        compiler_params=_cp(("parallel",)),
    )(parts)


def _sum_slots(a, name, rows):
    s, n, _ = a.shape

    def body(a_ref, o_ref):
        acc = a_ref[0]
        for k in range(1, s):
            acc = acc + a_ref[k]
        o_ref[...] = acc

    return _pcall(
        body, name=name, grid=(n // rows,), in_specs=[pl.BlockSpec((s, rows, 128), lambda i: (0, i, 0))],
        out_specs=pl.BlockSpec((rows, 128), lambda i: (i, 0)), out_shape=SDS((n, 128), F32),
        compiler_params=_cp(("parallel",)),
    )(a)


def _adamw(w, g, m, v, name, rows):
    r, cdim = w.shape

    def body(w_ref, g_ref, m_ref, v_ref, d_ref, nm_ref, nv_ref):
        g_ = g_ref[...]
        nm = ADAM_B1 * m_ref[...] + (1.0 - ADAM_B1) * g_
        nv = ADAM_B2 * v_ref[...] + (1.0 - ADAM_B2) * (g_ * g_)
        m_hat = nm / (1.0 - ADAM_B1 ** ADAM_STEP)
        v_hat = nv / (1.0 - ADAM_B2 ** ADAM_STEP)
        d_ref[...] = -ADAM_LR * (m_hat / (jnp.sqrt(v_hat) + ADAM_EPS) + ADAM_WD * w_ref[...])
        nm_ref[...] = nm
        nv_ref[...] = nv

    blk = pl.BlockSpec((rows, cdim), lambda i: (i, 0))
    return _pcall(
        body, name=name, grid=(r // rows,), in_specs=[blk] * 4, out_specs=[blk] * 3,
        out_shape=[SDS(w.shape, F32)] * 3, compiler_params=_cp(("parallel",)),
    )(w, g, m, v)


def _pack(pieces, rows):
    flat = jnp.concatenate([p.reshape(-1) for p in pieces])
    return jnp.pad(flat, (0, rows * 128 - flat.shape[0])).reshape(rows, 128)


def _unpack(buf, shapes):
    flat = buf.reshape(-1)
    out, off = [], 0
    for shp in shapes:
        size = 1
        for s in shp:
            size *= s
        out.append(flat[off:off + size].reshape(shp))
        off += size
    return out


def _perm_cols(w):
    perm = jnp.concatenate([w[..., 3104:5152], w[..., 0:1024], w[..., 1056:1568], w[..., 1568:2080], w[..., 2592:3104],
                            w[..., 2080:2592]], axis=-1)
    return perm, w[..., 1024:1056]


def _unperm_cols(perm, lr32):
    return jnp.concatenate([perm[..., 2048:3072], lr32, perm[..., 3072:3584], perm[..., 3584:4096], perm[..., 4608:5120],
                            perm[..., 4096:4608], perm[..., 0:2048]], axis=-1)


SMALL_ROWS = 672
HALF_ROWS = 7200


def kernel(x, c, ctx, c_ctx, w_mod, b_mod, norm_g, w_in, a_ln_g, a_ln_b, a_ws, a_bs, b_gate_w2, b_gate_b, b_norm_g, w_proj_a, w_proj_b, w_out, final_norm_g, loss_target, m_c_ctx, m_w_mod, m_b_mod, m_norm_g, m_w_in, m_a_ln_g, m_a_ln_b, m_a_ws, m_a_bs, m_b_gate_w2, m_b_gate_b, m_b_norm_g, m_w_proj_a, m_w_proj_b, m_w_out, m_final_norm_g, v_c_ctx, v_w_mod, v_b_mod, v_norm_g, v_w_in, v_a_ln_g, v_a_ln_b, v_a_ws, v_a_bs, v_b_gate_w2, v_b_gate_b, v_b_norm_g, v_w_proj_a, v_w_proj_b, v_w_out, v_final_norm_g):
    xi, yi, ci = _coords()
    me_xy = 2 * xi + yi

    gate_pack = _pack([b_gate_w2[0], b_gate_b[0]], 24)
    w_in_t, m_w_in_t, v_w_in_t = (jnp.swapaxes(a[0], 0, 1) for a in (w_in, m_w_in, v_w_in))
    g_wit, g_wm, g_wpa, g_wpb, g_wo, g_gate = _gather_weights(
        [(w_in_t.astype(BF16), 1), (w_mod[0].astype(BF16), 0), (w_proj_a[0].astype(BF16), 0), (w_proj_b[0].astype(BF16), 0),
         (w_out[0].astype(BF16), 0)], [gate_pack], "gather_weights")
    wit_u = g_wit.reshape(4 * 1288, D)
    wit_g = wit_u[3104:5152]
    wit_r = jnp.concatenate([wit_u[1056:1568], wit_u[1568:2080], wit_u[2592:3104], wit_u[2080:2592], wit_u[0:1024]], axis=0)
    wlrt = jnp.pad(wit_u[1024:1056], ((0, LRW - 32), (0, 0)))
    wm = jnp.swapaxes(g_wm, 0, 1).reshape(D, 3 * D)
    wpa = jnp.swapaxes(g_wpa, 0, 1).reshape(512, D)
    wpb = jnp.swapaxes(g_wpb, 0, 1).reshape(512, D)
    wo = g_wo.reshape(D, D)
    gflat = g_gate.reshape(4, 24 * 128)
    w2 = jnp.swapaxes(gflat[:, 0:2048].reshape(4, 2, 16, 64), 0, 2)
    w2 = jnp.swapaxes(w2, 0, 1).reshape(2, 16, 256)
    gb2 = jnp.swapaxes(gflat[:, 2048:2176].reshape(4, 2, 64), 0, 1).reshape(2, 256)

    tags = ["wi", "wpa", "wpb", "wo"]
    half_axes = [2, 1, 1, 1]
    sent = []

    def exchange(g):
        dwr = g["dwit_r"]
        dwit_u = jnp.concatenate([g["dwit_qkv"], g["dwlrt"][0:32], dwr[0:512], dwr[512:1024], dwr[1536:2048], dwr[1024:1536],
                                  g["dwit_g"]], axis=0)
        big = [dwit_u.reshape(4, 1288, D), jnp.swapaxes(g["dwpa"].reshape(512, 4, 256), 0, 1),
               jnp.swapaxes(g["dwpb"].reshape(512, 4, 256), 0, 1), g["dwo"].reshape(4, 256, D)]
        other = _swap_half_c(big, half_axes, "swap_half_in")
        cidx = jnp.reshape(ci, (1,)).astype(jnp.int32)
        sent.extend(_pair_sum(a, o, cidx, ax, "sum_pair_" + t) for a, o, ax, t in zip(big, other, half_axes, tags))
        return sent

    r = _device_step(x[0], c, ctx[0], c_ctx[None], loss_target[0], wm, b_mod, norm_g, wit_g, wit_r, wlrt, a_ln_g, a_ln_b,
                     a_ws[0], a_bs[0], w2, gb2, b_norm_g, wpa, wpb, wo, final_norm_g[None], exchange)

    small = _pack([r["dmod"], c, r["dmodc"], r["dscc"], r["dng"], r["dlng"], r["dlnb"], r["dws"], r["dbs"], r["dgbn"], r["dgf"],
                   r["dw2"], r["dgb2"], jnp.broadcast_to(r["loss"], (128,))], SMALL_ROWS)
    small_all = _gather_all(small, "gather_small")
    small_sum = _sum_slots(small_all, "sum_small", SMALL_ROWS // 4)
    (s_dmod, _, s_dmodc, s_dscc, s_dng, s_dlng, s_dlnb, s_dws, s_dbs, s_dgbn, s_dgf, s_dw2, s_dgb2, s_loss) = _unpack(
        small_sum, [(1, 3 * D), (1, D), (1, 2 * D), (D,), (1, D), (1, 512), (1, 512), (1, 4, 128, 128), (1, 4, 128), (1, 512),
                    (D,), (2, 16, 256), (2, 256), (128,)])
    loss = s_loss[0]
    s_dmodc_p = jnp.pad(s_dmodc, ((0, 0), (0, D)))
    g_b_mod = s_dmod + s_dmodc_p
    sg = jax.nn.sigmoid(c_ctx)
    g_c_ctx = s_dscc * (sg * (1.0 + c_ctx * (1.0 - sg)))
    g_w2 = lax.dynamic_slice_in_dim(s_dw2, 64 * me_xy, 64, axis=2)[None]
    g_gb2 = lax.dynamic_slice_in_dim(s_dgb2, 64 * me_xy, 64, axis=1)[None]

    flat_all = small_all.reshape(8, SMALL_ROWS * 128)
    dmod_all = flat_all[:, 0:3 * D]
    c_all = flat_all[:, 3 * D:4 * D]
    lhs = jnp.concatenate([_silu(c_all), _silu(c_ctx)[None], jnp.zeros((7, D), F32)], axis=0)
    rhs = jnp.concatenate([dmod_all, s_dmodc_p, jnp.zeros((7, 3 * D), F32)], axis=0)
    rhs = lax.dynamic_slice_in_dim(rhs, 768 * me_xy, 768, axis=1)
    g_w_mod = _mm(lhs.T.astype(BF16), rhs.astype(BF16), tm=D, tn=768, tk=16, out_dtype=F32, name="mm_dwm")

    parts = [_own(g, lax.dynamic_index_in_dim(s_, me_xy, axis=0, keepdims=False), me_xy) for g, s_ in zip(r["got"], sent)]
    halves = [_sum_chips(p_, "sum_chips_" + t) for p_, t in zip(parts, tags)]
    g_w_in_t, g_wpa, g_wpb, g_wo = _join_halves(halves, [1, 0, 0, 0], "swap_half_out")

    d_w_in_t, nm_w_in_t, nv_w_in_t = _adamw(w_in_t, g_w_in_t, m_w_in_t, v_w_in_t, "adamw_w_in", 184)
    g_w_in, d_w_in, nm_w_in, nv_w_in = (jnp.swapaxes(a, 0, 1) for a in (g_w_in_t, d_w_in_t, nm_w_in_t, nv_w_in_t))
    d_w_mod, nm_w_mod, nv_w_mod = _adamw(w_mod[0], g_w_mod, m_w_mod[0], v_w_mod[0], "adamw_w_mod", 256)
    d_wpa, nm_wpa, nv_wpa = _adamw(w_proj_a[0], g_wpa, m_w_proj_a[0], v_w_proj_a[0], "adamw_wpa", 256)
    d_wpb, nm_wpb, nv_wpb = _adamw(w_proj_b[0], g_wpb, m_w_proj_b[0], v_w_proj_b[0], "adamw_wpb", 256)
    d_wo, nm_wo, nv_wo = _adamw(w_out[0], g_wo, m_w_out[0], v_w_out[0], "adamw_wo", 256)

    names = ["c_ctx", "b_mod", "norm_g", "a_ln_g", "a_ln_b", "a_ws", "a_bs", "b_gate_w2", "b_gate_b", "b_norm_g", "final_norm_g"]
    ws_ = [c_ctx, b_mod, norm_g, a_ln_g, a_ln_b, a_ws, a_bs, b_gate_w2, b_gate_b, b_norm_g, final_norm_g]
    gs_ = [g_c_ctx, g_b_mod, s_dng, s_dlng, s_dlnb, s_dws, s_dbs, g_w2, g_gb2, s_dgbn, s_dgf]
    ms_ = [m_c_ctx, m_b_mod, m_norm_g, m_a_ln_g, m_a_ln_b, m_a_ws, m_a_bs, m_b_gate_w2, m_b_gate_b, m_b_norm_g, m_final_norm_g]
    vs_ = [v_c_ctx, v_b_mod, v_norm_g, v_a_ln_g, v_a_ln_b, v_a_ws, v_a_bs, v_b_gate_w2, v_b_gate_b, v_b_norm_g, v_final_norm_g]
    shapes = [w.shape for w in ws_]
    gs_ = [g.reshape(s) for g, s in zip(gs_, shapes)]
    d_s, nm_s, nv_s = _adamw(_pack(ws_, 600), _pack(gs_, 600), _pack(ms_, 600), _pack(vs_, 600), "adamw_small", 600)
    d_small = dict(zip(names, _unpack(d_s, shapes)))
    nm_small = dict(zip(names, _unpack(nm_s, shapes)))
    nv_small = dict(zip(names, _unpack(nv_s, shapes)))
    g_small = dict(zip(names, gs_))

    order = ["c_ctx", "w_mod", "b_mod", "norm_g", "w_in", "a_ln_g", "a_ln_b", "a_ws", "a_bs", "b_gate_w2", "b_gate_b", "b_norm_g",
             "w_proj_a", "w_proj_b", "w_out", "final_norm_g"]
    big_g = dict(w_mod=g_w_mod[None], w_in=g_w_in[None], w_proj_a=g_wpa[None], w_proj_b=g_wpb[None], w_out=g_wo[None])
    big_d = dict(w_mod=d_w_mod[None], w_in=d_w_in[None], w_proj_a=d_wpa[None], w_proj_b=d_wpb[None], w_out=d_wo[None])
    big_m = dict(w_mod=nm_w_mod[None], w_in=nm_w_in[None], w_proj_a=nm_wpa[None], w_proj_b=nm_wpb[None], w_out=nm_wo[None])
    big_v = dict(w_mod=nv_w_mod[None], w_in=nv_w_in[None], w_proj_a=nv_wpa[None], w_proj_b=nv_wpb[None], w_out=nv_wo[None])
    grads = [big_g[n] if n in big_g else g_small[n] for n in order]
    deltas = [big_d[n] if n in big_d else d_small[n] for n in order]
    new_m = [big_m[n] if n in big_m else nm_small[n] for n in order]
    new_v = [big_v[n] if n in big_v else nv_small[n] for n in order]
    return (loss, r["dx"][None], *grads, *deltas, *new_m, *new_v)
```

```python
import functools

import jax
import jax.numpy as jnp
from jax import lax
from jax.experimental import pallas as pl
from jax.experimental.pallas import tpu as pltpu

F32 = jnp.float32
BF16 = jnp.bfloat16
SDS = jax.ShapeDtypeStruct

D = 1024
NP = 5120
LRW = 128
CH = 64
AC = 128
EPS = 1e-6
TOK = 256
GLA_TB = 256
VMEM_BIG = 48 * 1024 * 1024

ADAM_LR, ADAM_B1, ADAM_B2, ADAM_EPS, ADAM_WD, ADAM_STEP = 0.001, 0.9, 0.999, 1e-08, 0.01, 10

_pcall = pl.pallas_call
MESH = pl.DeviceIdType.MESH


def _cp(sem=None, vmem=None):
    kw = {}
    if sem is not None:
        kw["dimension_semantics"] = sem
    if vmem is not None:
        kw["vmem_limit_bytes"] = vmem
    return pltpu.CompilerParams(**kw)


def _silu(x):
    return x * jax.nn.sigmoid(x)


def _dsilu(x):
    s = jax.nn.sigmoid(x)
    return s * (1.0 + x * (1.0 - s))


def _logsig(x):
    return jnp.minimum(x, 0.0) - jnp.log1p(jnp.exp(-jnp.abs(x)))


def _nt(a, b):
    return lax.dot_general(a, b, (((1,), (1,)), ((), ())), preferred_element_type=F32)


def _tn(a, b):
    return lax.dot_general(a, b, (((0,), (0,)), ((), ())), preferred_element_type=F32)


def _nn(a, b):
    return jnp.dot(a, b, preferred_element_type=F32)


def _full(shape):
    return pl.BlockSpec(shape, lambda *_: (0,) * len(shape))


def _mm(a, b, *, tm, tn, tk, out_dtype, name, acc=None, n_outer=False, b_t=False):
    m, k = a.shape
    n, k2 = (b.shape if b_t else b.shape[::-1])
    assert k == k2 and m % tm == 0 and n % tn == 0 and k % tk == 0, (a.shape, b.shape, tm, tn, tk)
    nk = k // tk
    has_acc = acc is not None

    def body(*refs):
        if has_acc:
            a_ref, b_ref, c_ref, o_ref = refs[:4]
        else:
            a_ref, b_ref, o_ref = refs[:3]
        part = (_nt if b_t else _nn)(a_ref[...].astype(BF16), b_ref[...].astype(BF16))
        if nk == 1:
            o_ref[...] = ((c_ref[...] + part) if has_acc else part).astype(out_dtype)
            return
        acc_ref = refs[-1]
        kk = pl.program_id(2)

        @pl.when(kk == 0)
        def _():
            if has_acc:
                acc_ref[...] = c_ref[...] + part
            else:
                acc_ref[...] = part

        @pl.when(kk > 0)
        def _():
            acc_ref[...] += part

        @pl.when(kk == nk - 1)
        def _():
            o_ref[...] = acc_ref[...].astype(out_dtype)

    if n_outer:
        ij = lambda g0, g1: (g1, g0)
        grid = (n // tn, m // tm, nk)
    else:
        ij = lambda g0, g1: (g0, g1)
        grid = (m // tm, n // tn, nk)
    b_spec = (pl.BlockSpec((tn, tk), lambda g0, g1, kk: (ij(g0, g1)[1], kk)) if b_t
              else pl.BlockSpec((tk, tn), lambda g0, g1, kk: (kk, ij(g0, g1)[1])))
    in_specs = [pl.BlockSpec((tm, tk), lambda g0, g1, kk: (ij(g0, g1)[0], kk)), b_spec]
    args = [a, b]
    if has_acc:
        in_specs.append(pl.BlockSpec((tm, tn), lambda g0, g1, kk: ij(g0, g1)))
        args.append(acc)
    return _pcall(
        body, name=name, grid=grid, in_specs=in_specs,
        out_specs=pl.BlockSpec((tm, tn), lambda g0, g1, kk: ij(g0, g1)),
        out_shape=SDS((m, n), out_dtype), scratch_shapes=([pltpu.VMEM((tm, tn), F32)] if nk > 1 else []),
        compiler_params=_cp(("parallel", "parallel", "arbitrary"), VMEM_BIG),
    )(*args)


def _mm_tn(a, b, *, ta, tn, tk, name, acc=None):
    m, ka = a.shape
    m2, n = b.shape
    assert m == m2 and ka % ta == 0 and n % tn == 0 and m % tk == 0, (a.shape, b.shape, ta, tn, tk)
    nk = m // tk
    has_acc = acc is not None

    def body(*refs):
        if has_acc:
            a_ref, b_ref, c_ref, o_ref = refs
        else:
            a_ref, b_ref, o_ref = refs
        kk = pl.program_id(2)
        part = _tn(a_ref[...].astype(BF16), b_ref[...].astype(BF16))

        @pl.when(kk == 0)
        def _():
            if has_acc:
                o_ref[...] = c_ref[...] + part
            else:
                o_ref[...] = part

        @pl.when(kk > 0)
        def _():
            o_ref[...] += part

    in_specs = [pl.BlockSpec((tk, ta), lambda i, j, kk: (kk, i)), pl.BlockSpec((tk, tn), lambda i, j, kk: (kk, j))]
    args = [a, b]
    if has_acc:
        in_specs.append(pl.BlockSpec((ta, tn), lambda i, j, kk: (i, j)))
        args.append(acc)
    return _pcall(
        body, name=name, grid=(ka // ta, n // tn, nk), in_specs=in_specs,
        out_specs=pl.BlockSpec((ta, tn), lambda i, j, kk: (i, j)), out_shape=SDS((ka, n), F32),
        compiler_params=_cp(("parallel", "parallel", "arbitrary"), VMEM_BIG),
    )(*args)


def _modvec(cc, wm, bm):
    def body(c_ref, w_ref, b_ref, o_ref):
        o_ref[...] = _nn(_silu(c_ref[...]).astype(BF16), w_ref[...]) + b_ref[...]

    return _pcall(body, name="modvec", out_shape=SDS((8, 3 * D), F32), compiler_params=_cp(None, VMEM_BIG))(cc, wm, bm)


def _dcctx(dmodc, wm):
    def body(d_ref, w_ref, o_ref):
        o_ref[...] = _nt(d_ref[...].astype(BF16), w_ref[...])

    return _pcall(
        body, name="dcctx", grid=(1,), in_specs=[_full((8, 2 * D)), pl.BlockSpec((D, 2 * D), lambda i: (0, 0))],
        out_specs=_full((8, D)), out_shape=SDS((8, D), F32), compiler_params=_cp(("arbitrary",), VMEM_BIG),
    )(dmodc, wm)


def _prep_h(x, ng, scale, shift, name):
    m = x.shape[0]

    def body(x_ref, g_ref, sc_ref, sh_ref, h_ref):
        xf = x_ref[...]
        r = lax.rsqrt(jnp.mean(xf * xf, axis=-1, keepdims=True) + EPS)
        y = (xf * r) * g_ref[...]
        h_ref[...] = (y * (1.0 + sc_ref[...]) + sh_ref[...]).astype(BF16)

    row = pl.BlockSpec((TOK, D), lambda i: (i, 0))
    return _pcall(
        body, name=name, grid=(m // TOK,), in_specs=[row, _full((1, D)), _full((1, D)), _full((1, D))],
        out_specs=row, out_shape=SDS((m, D), BF16), compiler_params=_cp(("parallel",)),
    )(x, ng, scale, shift)


def _resident(shape):
    return pl.BlockSpec(shape, lambda *_: (0,) * len(shape), pipeline_mode=pl.Buffered(1))


PROJ_TM = 512


def _proj_fwd(x, ng, scale, shift, wit_g, wit_r, wlrt):
    m = x.shape[0]
    src = [(0, 0), (0, D), (1, 2 * D), (1, 0), (1, D)]

    def body(x_ref, g_ref, sc_ref, sh_ref, wg_ref, wr_ref, wl_ref, h_ref, p_ref, plr_ref):
        xf = x_ref[...]
        r = lax.rsqrt(jnp.mean(xf * xf, axis=-1, keepdims=True) + EPS)
        y = (xf * r) * g_ref[...]
        h = (y * (1.0 + sc_ref[...]) + sh_ref[...]).astype(BF16)
        h_ref[...] = h
        for j, (which, r0) in enumerate(src):
            w_ref = wr_ref if which else wg_ref
            p_ref[:, D * j:D * j + D] = _nt(h, w_ref[r0:r0 + D, :]).astype(BF16)
        plr_ref[...] = _nt(h, wl_ref[...])

    row = pl.BlockSpec((PROJ_TM, D), lambda i: (i, 0))
    vec = _full((1, D))
    return _pcall(
        body, name="proj_fwd", grid=(m // PROJ_TM,),
        in_specs=[row, vec, vec, vec, _resident((2 * D, D)), _resident((3 * D, D)), _resident((LRW, D))],
        out_specs=[row, pl.BlockSpec((PROJ_TM, NP), lambda i: (i, 0)), pl.BlockSpec((PROJ_TM, LRW), lambda i: (i, 0))],
        out_shape=[SDS((m, D), BF16), SDS((m, NP), BF16), SDS((m, LRW), F32)],
        compiler_params=_cp(("parallel",), VMEM_BIG),
    )(x, ng, scale, shift, wit_g, wit_r, wlrt)


def _proj_bwd(dp_g, dp_r, dlr, wit_g, wit_r, wlrt, x, dx1, ng, scale, send=()):
    m = x.shape[0]
    ns = len(send)
    steps = m // PROJ_TM

    def body(*refs):
        (dpg_ref, dpr_ref, dlr_ref, wg_ref, wr_ref, wl_ref, x_ref, r_ref, g_ref, sc_ref) = refs[:10]
        send_refs = refs[10:10 + ns]
        dx_ref, dg_ref, dsc_ref, dsh_ref = refs[10 + ns:14 + ns]
        got_refs = refs[14 + ns:14 + 2 * ns]
        sems = refs[14 + 2 * ns:]
        i = pl.program_id(0)

        def copies():
            cx, cy, cc = _coords()
            me = 2 * cx + cy
            peers = [(1 - cx, cy), (cx, 1 - cy), (1 - cx, 1 - cy)]
            out, back = [], []
            for k in range(ns):
                for j, (px, py) in enumerate(peers):
                    out.append(_remote(send_refs[k].at[2 * px + py], got_refs[k].at[me], sems[0].at[3 * k + j],
                                       sems[1].at[3 * k + j], (px, py, cc)))
                    landed = got_refs[k].at[2 * px + py]
                    back.append(_remote(landed, landed, sems[0].at[3 * k + j], sems[1].at[3 * k + j], (px, py, cc)))
            return out, back

        @pl.when(i == 0)
        def _():
            dg_ref[...] = jnp.zeros_like(dg_ref)
            dsc_ref[...] = jnp.zeros_like(dsc_ref)
            dsh_ref[...] = jnp.zeros_like(dsh_ref)
            if ns:
                for rc in copies()[0]:
                    rc.start()

        dh_ = (_nn(dpg_ref[...], wg_ref[...]) + _nn(dpr_ref[...], wr_ref[...])
               + _nn(dlr_ref[...].astype(BF16), wl_ref[...]))
        xf = x_ref[...]
        r = lax.rsqrt(jnp.mean(xf * xf, axis=-1, keepdims=True) + EPS)
        xh = xf * r
        y = xh * g_ref[...]
        dsh_ref[...] += jnp.sum(dh_, axis=0, keepdims=True)
        dsc_ref[...] += jnp.sum(dh_ * y, axis=0, keepdims=True)
        dy = dh_ * (1.0 + sc_ref[...])
        dg_ref[...] += jnp.sum(dy * xh, axis=0, keepdims=True)
        dxh = dy * g_ref[...]
        dx_ref[...] = r * (dxh - xh * jnp.mean(dxh * xh, axis=-1, keepdims=True)) + r_ref[...]

        if ns:
            @pl.when(i == steps - 1)
            def _():
                out, back = copies()
                for rc in back:
                    rc.wait_recv()
                for rc in out:
                    rc.wait_send()

    row = pl.BlockSpec((PROJ_TM, D), lambda i: (i, 0))
    vec = _full((1, D))
    kg, kr = dp_g.shape[1], dp_r.shape[1]
    res = _pcall(
        body, name="proj_bwd", grid=(steps,),
        in_specs=[pl.BlockSpec((PROJ_TM, kg), lambda i: (i, 0)), pl.BlockSpec((PROJ_TM, kr), lambda i: (i, 0)),
                  pl.BlockSpec((PROJ_TM, LRW), lambda i: (i, 0)), _resident((kg, D)), _resident((kr, D)), _resident((LRW, D)),
                  row, row, vec, vec] + [ANY] * ns,
        out_specs=[row, vec, vec, vec] + [ANY] * ns,
        out_shape=[SDS((m, D), F32), SDS((1, D), F32), SDS((1, D), F32), SDS((1, D), F32)] + [SDS(a.shape, a.dtype) for a in send],
        scratch_shapes=([pltpu.SemaphoreType.DMA((3 * ns,)), pltpu.SemaphoreType.DMA((3 * ns,))] if ns else []),
        compiler_params=_cp(("arbitrary",), VMEM_BIG),
    )(dp_g, dp_r, dlr, wit_g, wit_r, wlrt, x, dx1, ng, scale, *send)
    return tuple(res[:4]), list(res[4:])


def _prep_bwd(x, dh, dx1, ng, scale, name):
    m = x.shape[0]
    has_res = dx1 is not None

    def body(*refs):
        if has_res:
            x_ref, dh_ref, r_ref, g_ref, sc_ref, dx_ref, dg_ref, dsc_ref, dsh_ref = refs
        else:
            x_ref, dh_ref, g_ref, sc_ref, dx_ref, dg_ref, dsc_ref, dsh_ref = refs
        i = pl.program_id(0)

        @pl.when(i == 0)
        def _():
            dg_ref[...] = jnp.zeros_like(dg_ref)
            dsc_ref[...] = jnp.zeros_like(dsc_ref)
            dsh_ref[...] = jnp.zeros_like(dsh_ref)

        xf = x_ref[...]
        dh_ = dh_ref[...]
        r = lax.rsqrt(jnp.mean(xf * xf, axis=-1, keepdims=True) + EPS)
        xh = xf * r
        y = xh * g_ref[...]
        dsh_ref[...] += jnp.sum(dh_, axis=0, keepdims=True)
        dsc_ref[...] += jnp.sum(dh_ * y, axis=0, keepdims=True)
        dy = dh_ * (1.0 + sc_ref[...])
        dg_ref[...] += jnp.sum(dy * xh, axis=0, keepdims=True)
        dxh = dy * g_ref[...]
        dx = r * (dxh - xh * jnp.mean(dxh * xh, axis=-1, keepdims=True))
        if has_res:
            dx = dx + r_ref[...]
        dx_ref[...] = dx

    row = pl.BlockSpec((TOK, D), lambda i: (i, 0))
    vec = _full((1, D))
    in_specs = [row, row] + ([row] if has_res else []) + [vec, vec]
    args = [x, dh] + ([dx1] if has_res else []) + [ng, scale]
    return _pcall(
        body, name=name, grid=(m // TOK,), in_specs=in_specs, out_specs=[row, vec, vec, vec],
        out_shape=[SDS((m, D), F32), SDS((1, D), F32), SDS((1, D), F32), SDS((1, D), F32)],
        compiler_params=_cp(("arbitrary",)),
    )(*args)


def _ln_fwd(p, ln_g, ln_b):
    m = p.shape[0]

    def body(va_ref, g_ref, b_ref, vr_ref, vc_ref):
        xf = va_ref[...].astype(F32)
        xc = xf - jnp.mean(xf, axis=-1, keepdims=True)
        y = xc * lax.rsqrt(jnp.mean(xc * xc, axis=-1, keepdims=True) + EPS)
        vn = y * g_ref[...] + b_ref[...]
        vr_ref[...] = vn[:, 0:256].astype(BF16)
        vc_ref[0] = vn[:, 256:384].astype(BF16)
        vc_ref[1] = vn[:, 384:512].astype(BF16)

    return _pcall(
        body, name="ln_fwd", grid=(m // TOK,),
        in_specs=[pl.BlockSpec((TOK, 512), lambda i: (i, 9)), _full((1, 512)), _full((1, 512))],
        out_specs=[pl.BlockSpec((TOK, 256), lambda i: (i, 0)), pl.BlockSpec((2, TOK, 128), lambda i: (0, i, 0))],
        out_shape=[SDS((m, 256), BF16), SDS((2, m, 128), BF16)], compiler_params=_cp(("parallel",)),
    )(p, ln_g, ln_b)


COLB = 2048


def _colmix_fwd(vnc, ws23, bs23):
    rows = vnc.shape[2] // COLB

    def body(v_ref, w_ref, b_ref, o_ref):
        o_ref[0] = _nn(w_ref[0], v_ref[0]) + b_ref[0]

    return _pcall(
        body, name="colmix_fwd", grid=(2, rows),
        in_specs=[pl.BlockSpec((1, AC, COLB), lambda g, j: (g, 0, j)), pl.BlockSpec((1, AC, AC), lambda g, j: (g, 0, 0)),
                  pl.BlockSpec((1, AC, 1), lambda g, j: (g, 0, 0))],
        out_specs=pl.BlockSpec((1, AC, COLB), lambda g, j: (g, 0, j)),
        out_shape=SDS(vnc.shape, F32), compiler_params=_cp(("parallel", "parallel")),
    )(vnc, ws23, bs23)


def _colmix_bwd(dsvc, vnc, ws23t):
    rows = vnc.shape[2] // COLB

    def body(d_ref, v_ref, wt_ref, dv_ref, dw_ref, db_ref):
        j = pl.program_id(1)

        @pl.when(j == 0)
        def _():
            dw_ref[...] = jnp.zeros_like(dw_ref)
            db_ref[...] = jnp.zeros_like(db_ref)

        d = d_ref[0]
        d16 = d.astype(BF16)
        dv_ref[0] = _nn(wt_ref[0], d16)
        dw_ref[0] += _nt(d16, v_ref[0])
        db_ref[0] += jnp.sum(d, axis=1, keepdims=True)

    blk = pl.BlockSpec((1, AC, COLB), lambda g, j: (g, 0, j))
    return _pcall(
        body, name="colmix_bwd", grid=(2, rows),
        in_specs=[blk, blk, pl.BlockSpec((1, AC, AC), lambda g, j: (g, 0, 0))],
        out_specs=[blk, pl.BlockSpec((1, AC, AC), lambda g, j: (g, 0, 0)), pl.BlockSpec((1, AC, 1), lambda g, j: (g, 0, 0))],
        out_shape=[SDS(vnc.shape, F32), SDS((2, AC, AC), F32), SDS((2, AC, 1), F32)],
        compiler_params=_cp(("parallel", "arbitrary")),
    )(dsvc, vnc, ws23t)


def _head_norm(o, gbn):
    out = []
    for h in range(4):
        oh = o[:, 128 * h:128 * h + 128]
        r = lax.rsqrt(jnp.mean(oh * oh, axis=-1, keepdims=True) + EPS)
        out.append((r, oh * r))
    return out


def _mid_fwd(o_f, o_b, p, vnr, svc, ws01, bs01, gbn):
    m = p.shape[0]

    def body(of_ref, ob_ref, zb_ref, ua_ref, za_ref, vnr_ref, svc_ref, w_ref, b_ref, g_ref, ya_ref, yb_ref, svr_ref):
        o = of_ref[...] + ob_ref[...]
        zb = zb_ref[...]
        parts = []
        for h, (r, xh) in enumerate(_head_norm(o, None)):
            parts.append(xh * g_ref[:, 128 * h:128 * h + 128])
        on = jnp.concatenate(parts, axis=1)
        yb_ref[...] = (on * _silu(zb)).astype(BF16)
        for j in range(TOK // AC):
            for g in range(2):
                sv = _nn(w_ref[g], vnr_ref[AC * j:AC * j + AC, AC * g:AC * g + AC]) + b_ref[g]
                svr_ref[AC * j:AC * j + AC, AC * g:AC * g + AC] = sv
        sz = _silu(za_ref[...])
        u = ua_ref[...]
        ya_ref[:, 0:256] = ((u[:, 0:256] * svr_ref[...]) * sz[:, 0:256]).astype(BF16)
        ya_ref[:, 256:384] = ((u[:, 256:384] * svc_ref[0]) * sz[:, 256:384]).astype(BF16)
        ya_ref[:, 384:512] = ((u[:, 384:512] * svc_ref[1]) * sz[:, 384:512]).astype(BF16)

    r512 = pl.BlockSpec((TOK, 512), lambda i: (i, 0))
    return _pcall(
        body, name="mid_fwd", grid=(m // TOK,),
        in_specs=[r512, r512, pl.BlockSpec((TOK, 512), lambda i: (i, 6)), pl.BlockSpec((TOK, 512), lambda i: (i, 7)),
                  pl.BlockSpec((TOK, 512), lambda i: (i, 8)), pl.BlockSpec((TOK, 256), lambda i: (i, 0)),
                  pl.BlockSpec((2, TOK, 128), lambda i: (0, i, 0)), _full((2, AC, AC)), _full((2, AC, 1)), _full((1, 512))],
        out_specs=[r512, r512, pl.BlockSpec((TOK, 256), lambda i: (i, 0))],
        out_shape=[SDS((m, 512), BF16), SDS((m, 512), BF16), SDS((m, 256), F32)],
        compiler_params=_cp(("parallel",)),
    )(o_f, o_b, p, p, p, vnr, svc, ws01, bs01, gbn)


def _merge_fwd(p, ya, yb):
    m = p.shape[0]

    def body(ga_ref, gb_ref, ya_ref, yb_ref, m_ref):
        m_ref[...] = (jax.nn.sigmoid(ga_ref[...]) * ya_ref[...] + jax.nn.sigmoid(gb_ref[...]) * yb_ref[...]).astype(BF16)

    row = pl.BlockSpec((TOK, D), lambda i: (i, 0))
    return _pcall(
        body, name="merge_fwd", grid=(m // TOK,),
        in_specs=[row, pl.BlockSpec((TOK, D), lambda i: (i, 1)), row, row], out_specs=row,
        out_shape=SDS((m, D), BF16), compiler_params=_cp(("parallel",)),
    )(p, p, ya, yb)


def _loss_head(x, out, tgt, gate, gf):
    m = x.shape[0]

    def body(x_ref, o_ref, t_ref, gate_ref, gf_ref, dx1_ref, dout_ref, loss_ref, dgate_ref, dgf_ref):
        i = pl.program_id(0)

        @pl.when(i == 0)
        def _():
            loss_ref[...] = jnp.zeros_like(loss_ref)
            dgate_ref[...] = jnp.zeros_like(dgate_ref)
            dgf_ref[...] = jnp.zeros_like(dgf_ref)

        out_ = o_ref[...]
        x1 = x_ref[...] + gate_ref[...] * out_
        r = lax.rsqrt(jnp.mean(x1 * x1, axis=-1, keepdims=True) + EPS)
        xh = x1 * r
        err = xh * gf_ref[...] - t_ref[...]
        loss_ref[...] += 0.5 * jnp.sum(jnp.mean(err * err, axis=-1, keepdims=True), axis=0, keepdims=True)
        dy = err * (1.0 / D)
        dgf_ref[...] += jnp.sum(dy * xh, axis=0, keepdims=True)
        dxh = dy * gf_ref[...]
        dx1 = r * (dxh - xh * jnp.mean(dxh * xh, axis=-1, keepdims=True))
        dx1_ref[...] = dx1
        dout_ref[...] = (gate_ref[...] * dx1).astype(BF16)
        dgate_ref[...] += jnp.sum(dx1 * out_, axis=0, keepdims=True)

    row = pl.BlockSpec((TOK, D), lambda i: (i, 0))
    vec = _full((1, D))
    return _pcall(
        body, name="loss_head", grid=(m // TOK,), in_specs=[row, row, row, vec, vec],
        out_specs=[row, row, _full((1, 128)), vec, vec],
        out_shape=[SDS((m, D), F32), SDS((m, D), BF16), SDS((1, 128), F32), SDS((1, D), F32), SDS((1, D), F32)],
        compiler_params=_cp(("arbitrary",)),
    )(x, out, tgt, gate, gf)


def _merge_bwd(dm, ya, yb, p):
    m = p.shape[0]

    def body(dm_ref, ya_ref, yb_ref, ga_ref, gb_ref, dya_ref, dyb_ref, dp_ref):
        dm_ = dm_ref[...]
        sa = jax.nn.sigmoid(ga_ref[...])
        sb = jax.nn.sigmoid(gb_ref[...])
        dya_ref[...] = (dm_ * sa).astype(BF16)
        dyb_ref[...] = (dm_ * sb).astype(BF16)
        dp_ref[:, 0:D] = (dm_ * ya_ref[...] * (sa * (1.0 - sa))).astype(BF16)
        dp_ref[:, D:2 * D] = (dm_ * yb_ref[...] * (sb * (1.0 - sb))).astype(BF16)

    row = pl.BlockSpec((TOK, D), lambda i: (i, 0))
    return _pcall(
        body, name="merge_bwd", grid=(m // TOK,),
        in_specs=[row, row, row, row, pl.BlockSpec((TOK, D), lambda i: (i, 1))],
        out_specs=[row, row, pl.BlockSpec((TOK, 2 * D), lambda i: (i, 0))],
        out_shape=[SDS((m, D), BF16), SDS((m, D), BF16), SDS((m, NP), BF16)],
        compiler_params=_cp(("parallel",)),
    )(dm, ya, yb, p, p)


def _mid_bwd(dya_in, dyb_in, p, svr, svc, o_f, o_b, gbn, dp):
    m = p.shape[0]

    def body(dya_ref, dyb_ref, zb_ref, ua_ref, za_ref, svr_ref, svc_ref, of_ref, ob_ref, g_ref, dpi_ref,
             dp_ref, dsr_ref, dsc_ref, do_ref, dg_ref):
        i = pl.program_id(0)

        @pl.when(i == 0)
        def _():
            dg_ref[...] = jnp.zeros_like(dg_ref)

        dya = dya_ref[...]
        u = ua_ref[...]
        za = za_ref[...]
        sz = _silu(za)
        sv = jnp.concatenate([svr_ref[...], svc_ref[0], svc_ref[1]], axis=1)
        dp_ref[:, 512:1024] = (dya * sv * sz).astype(BF16)
        dsv = dya * u * sz
        dsr_ref[...] = dsv[:, 0:256]
        dsc_ref[0] = dsv[:, 256:384]
        dsc_ref[1] = dsv[:, 384:512]
        dp_ref[:, 1024:1536] = (dya * u * sv * _dsilu(za)).astype(BF16)

        dyb = dyb_ref[...]
        zb = zb_ref[...]
        o = of_ref[...] + ob_ref[...]
        szb = _silu(zb)
        dszb = _dsilu(zb)
        for h, (r, xh) in enumerate(_head_norm(o, None)):
            sl = slice(128 * h, 128 * h + 128)
            gh = g_ref[:, sl]
            don = dyb[:, sl] * szb[:, sl]
            dp_ref[:, sl] = (dyb[:, sl] * (xh * gh) * dszb[:, sl]).astype(BF16)
            dg_ref[:, sl] += jnp.sum(don * xh, axis=0, keepdims=True)
            dxh = don * gh
            do_ref[:, sl] = r * (dxh - xh * jnp.mean(dxh * xh, axis=-1, keepdims=True))

    r512 = pl.BlockSpec((TOK, 512), lambda i: (i, 0))
    return _pcall(
        body, name="mid_bwd", grid=(m // TOK,),
        in_specs=[r512, r512, pl.BlockSpec((TOK, 512), lambda i: (i, 6)), pl.BlockSpec((TOK, 512), lambda i: (i, 7)),
                  pl.BlockSpec((TOK, 512), lambda i: (i, 8)), pl.BlockSpec((TOK, 256), lambda i: (i, 0)),
                  pl.BlockSpec((2, TOK, 128), lambda i: (0, i, 0)), r512, r512, _full((1, 512)),
                  pl.BlockSpec(memory_space=pl.ANY)],
        out_specs=[pl.BlockSpec((TOK, 1536), lambda i: (i, 2)), pl.BlockSpec((TOK, 256), lambda i: (i, 0)),
                   pl.BlockSpec((2, TOK, 128), lambda i: (0, i, 0)), r512, _full((1, 512))],
        out_shape=[SDS((m, NP), BF16), SDS((m, 256), F32), SDS((2, m, 128), F32), SDS((m, 512), F32), SDS((1, 512), F32)],
        input_output_aliases={10: 0}, compiler_params=_cp(("arbitrary",)),
    )(dya_in, dyb_in, p, p, p, svr, svc, o_f, o_b, gbn, dp)


def _tail_fwd(o_f, o_b, p, vnr, svc, x, tgt, ws01, bs01, gbn, wpa, wpb, wo, gate, gf):
    m = p.shape[0]

    def body(of_ref, ob_ref, zb_ref, ua_ref, za_ref, ga_ref, gb_ref, vnr_ref, svc_ref, x_ref, t_ref, w_ref, b_ref, g_ref,
             wpa_ref, wpb_ref, wo_ref, gate_ref, gf_ref,
             ya_ref, yb_ref, svr_ref, m_ref, dx1_ref, dout_ref, loss_ref, dgate_ref, dgf_ref):
        i = pl.program_id(0)

        @pl.when(i == 0)
        def _():
            loss_ref[...] = jnp.zeros_like(loss_ref)
            dgate_ref[...] = jnp.zeros_like(dgate_ref)
            dgf_ref[...] = jnp.zeros_like(dgf_ref)

        o = of_ref[...] + ob_ref[...]
        zb = zb_ref[...].astype(F32)
        for h, (r, xh) in enumerate(_head_norm(o, None)):
            sl = slice(128 * h, 128 * h + 128)
            yb_ref[:, sl] = ((xh * g_ref[:, sl]) * _silu(zb[:, sl])).astype(BF16)
        for j in range(TOK // AC):
            for g in range(2):
                sv = _nn(w_ref[g], vnr_ref[AC * j:AC * j + AC, AC * g:AC * g + AC]) + b_ref[g]
                svr_ref[AC * j:AC * j + AC, AC * g:AC * g + AC] = sv
        sz = _silu(za_ref[...].astype(F32))
        u = ua_ref[...].astype(F32)
        ya_ref[:, 0:256] = ((u[:, 0:256] * svr_ref[...]) * sz[:, 0:256]).astype(BF16)
        ya_ref[:, 256:384] = ((u[:, 256:384] * svc_ref[0]) * sz[:, 256:384]).astype(BF16)
        ya_ref[:, 384:512] = ((u[:, 384:512] * svc_ref[1]) * sz[:, 384:512]).astype(BF16)
        ya = _nn(ya_ref[...], wpa_ref[...])
        yb = _nn(yb_ref[...], wpb_ref[...])
        mg = (jax.nn.sigmoid(ga_ref[...].astype(F32)) * ya + jax.nn.sigmoid(gb_ref[...].astype(F32)) * yb).astype(BF16)
        m_ref[...] = mg
        out_ = _nn(mg, wo_ref[...])
        x1 = x_ref[...] + gate_ref[...] * out_
        r = lax.rsqrt(jnp.mean(x1 * x1, axis=-1, keepdims=True) + EPS)
        xh = x1 * r
        err = xh * gf_ref[...] - t_ref[...]
        loss_ref[...] += 0.5 * jnp.sum(jnp.mean(err * err, axis=-1, keepdims=True), axis=0, keepdims=True)
        dy = err * (1.0 / D)
        dgf_ref[...] += jnp.sum(dy * xh, axis=0, keepdims=True)
        dxh = dy * gf_ref[...]
        dx1 = r * (dxh - xh * jnp.mean(dxh * xh, axis=-1, keepdims=True))
        dx1_ref[...] = dx1
        dout_ref[...] = (gate_ref[...] * dx1).astype(BF16)
        dgate_ref[...] += jnp.sum(dx1 * out_, axis=0, keepdims=True)

    r512 = pl.BlockSpec((TOK, 512), lambda i: (i, 0))
    row = pl.BlockSpec((TOK, D), lambda i: (i, 0))
    vec = _full((1, D))
    return _pcall(
        body, name="tail_fwd", grid=(m // TOK,),
        in_specs=[r512, r512, pl.BlockSpec((TOK, 512), lambda i: (i, 6)), pl.BlockSpec((TOK, 512), lambda i: (i, 7)),
                  pl.BlockSpec((TOK, 512), lambda i: (i, 8)), row, pl.BlockSpec((TOK, D), lambda i: (i, 1)),
                  pl.BlockSpec((TOK, 256), lambda i: (i, 0)), pl.BlockSpec((2, TOK, 128), lambda i: (0, i, 0)), row, row,
                  _full((2, AC, AC)), _full((2, AC, 1)), _full((1, 512)), _resident((512, D)), _resident((512, D)),
                  _resident((D, D)), vec, vec],
        out_specs=[r512, r512, pl.BlockSpec((TOK, 256), lambda i: (i, 0)), row, row, row, _full((1, 128)), vec, vec],
        out_shape=[SDS((m, 512), BF16), SDS((m, 512), BF16), SDS((m, 256), F32), SDS((m, D), BF16), SDS((m, D), F32),
                   SDS((m, D), BF16), SDS((1, 128), F32), SDS((1, D), F32), SDS((1, D), F32)],
        compiler_params=_cp(("arbitrary",), VMEM_BIG),
    )(o_f, o_b, p, p, p, p, p, vnr, svc, x, tgt, ws01, bs01, gbn, wpa, wpb, wo, gate, gf)


DPR = 3072


def _tail_bwd(dout, ya_in, yb_in, p, svr, svc, o_f, o_b, gbn, wo, wpa, wpb):
    m = p.shape[0]

    def body(dout_ref, ya_ref, yb_ref, ga_ref, gb_ref, zb_ref, ua_ref, za_ref, svr_ref, svc_ref, of_ref, ob_ref, g_ref,
             wo_ref, wpa_ref, wpb_ref,
             dya_ref, dyb_ref, dpg_ref, dpr_ref, dsr_ref, dsc_ref, do_ref, dg_ref):
        i = pl.program_id(0)

        @pl.when(i == 0)
        def _():
            dg_ref[...] = jnp.zeros_like(dg_ref)

        dm_ = _nt(dout_ref[...], wo_ref[...])
        ya = _nn(ya_ref[...], wpa_ref[...])
        yb = _nn(yb_ref[...], wpb_ref[...])
        sa = jax.nn.sigmoid(ga_ref[...].astype(F32))
        sb = jax.nn.sigmoid(gb_ref[...].astype(F32))
        dya16 = (dm_ * sa).astype(BF16)
        dyb16 = (dm_ * sb).astype(BF16)
        dya_ref[...] = dya16
        dyb_ref[...] = dyb16
        dpg_ref[:, 0:D] = (dm_ * ya * (sa * (1.0 - sa))).astype(BF16)
        dpg_ref[:, D:2 * D] = (dm_ * yb * (sb * (1.0 - sb))).astype(BF16)
        dya = _nt(dya16, wpa_ref[...])
        dyb = _nt(dyb16, wpb_ref[...])

        u = ua_ref[...].astype(F32)
        za = za_ref[...].astype(F32)
        sz = _silu(za)
        sv = jnp.concatenate([svr_ref[...], svc_ref[0], svc_ref[1]], axis=1)
        dpr_ref[:, 512:1024] = (dya * sv * sz).astype(BF16)
        dsv = dya * u * sz
        dsr_ref[...] = dsv[:, 0:256]
        dsc_ref[0] = dsv[:, 256:384]
        dsc_ref[1] = dsv[:, 384:512]
        dpr_ref[:, 1024:1536] = (dya * u * sv * _dsilu(za)).astype(BF16)

        zb = zb_ref[...].astype(F32)
        o = of_ref[...] + ob_ref[...]
        szb = _silu(zb)
        dszb = _dsilu(zb)
        for h, (r, xh) in enumerate(_head_norm(o, None)):
            sl = slice(128 * h, 128 * h + 128)
            gh = g_ref[:, sl]
            don = dyb[:, sl] * szb[:, sl]
            dpr_ref[:, sl] = (dyb[:, sl] * (xh * gh) * dszb[:, sl]).astype(BF16)
            dg_ref[:, sl] += jnp.sum(don * xh, axis=0, keepdims=True)
            dxh = don * gh
            do_ref[:, sl] = r * (dxh - xh * jnp.mean(dxh * xh, axis=-1, keepdims=True))

    r512 = pl.BlockSpec((TOK, 512), lambda i: (i, 0))
    row = pl.BlockSpec((TOK, D), lambda i: (i, 0))
    return _pcall(
        body, name="tail_bwd", grid=(m // TOK,),
        in_specs=[row, r512, r512, row, pl.BlockSpec((TOK, D), lambda i: (i, 1)), pl.BlockSpec((TOK, 512), lambda i: (i, 6)),
                  pl.BlockSpec((TOK, 512), lambda i: (i, 7)), pl.BlockSpec((TOK, 512), lambda i: (i, 8)),
                  pl.BlockSpec((TOK, 256), lambda i: (i, 0)), pl.BlockSpec((2, TOK, 128), lambda i: (0, i, 0)), r512, r512,
                  _full((1, 512)), _resident((D, D)), _resident((512, D)), _resident((512, D))],
        out_specs=[row, row, pl.BlockSpec((TOK, 2 * D), lambda i: (i, 0)), pl.BlockSpec((TOK, 1536), lambda i: (i, 0)),
                   pl.BlockSpec((TOK, 256), lambda i: (i, 0)), pl.BlockSpec((2, TOK, 128), lambda i: (0, i, 0)), r512, _full((1, 512))],
        out_shape=[SDS((m, D), BF16), SDS((m, D), BF16), SDS((m, 2 * D), BF16), SDS((m, DPR), BF16), SDS((m, 256), F32),
                   SDS((2, m, 128), F32), SDS((m, 512), F32), SDS((1, 512), F32)],
        compiler_params=_cp(("arbitrary",), VMEM_BIG),
    )(dout, ya_in, yb_in, p, p, p, p, p, svr, svc, o_f, o_b, gbn, wo, wpa, wpb)


def _mm_multi(pairs, *, tm, tn, out_dtype, name):
    m = pairs[0][0].shape[0]
    n = pairs[0][1].shape[1]
    nks = [a.shape[1] // tk for a, _, tk in pairs]
    starts = [sum(nks[:i]) for i in range(len(pairs))]
    total = sum(nks)

    def body(*refs):
        o_ref, acc_ref = refs[-2], refs[-1]
        kk = pl.program_id(2)
        for idx in range(len(pairs)):
            a_ref, b_ref = refs[2 * idx], refs[2 * idx + 1]

            @pl.when((kk >= starts[idx]) & (kk < starts[idx] + nks[idx]))
            def _(a_ref=a_ref, b_ref=b_ref, first=(idx == 0)):
                part = _nn(a_ref[...].astype(BF16), b_ref[...].astype(BF16))
                if first:
                    @pl.when(kk == 0)
                    def _():
                        acc_ref[...] = part

                    @pl.when(kk > 0)
                    def _():
                        acc_ref[...] += part
                else:
                    acc_ref[...] += part

        @pl.when(kk == total - 1)
        def _():
            o_ref[...] = acc_ref[...].astype(out_dtype)

    in_specs, args = [], []
    for (a, b, tk), st, nk in zip(pairs, starts, nks):
        in_specs.append(pl.BlockSpec((tm, tk), lambda i, j, kk, st=st, nk=nk: (i, jnp.clip(kk - st, 0, nk - 1))))
        in_specs.append(pl.BlockSpec((tk, tn), lambda i, j, kk, st=st, nk=nk: (jnp.clip(kk - st, 0, nk - 1), j)))
        args += [a, b]
    return _pcall(
        body, name=name, grid=(m // tm, n // tn, total), in_specs=in_specs,
        out_specs=pl.BlockSpec((tm, tn), lambda i, j, kk: (i, j)), out_shape=SDS((m, n), out_dtype),
        scratch_shapes=[pltpu.VMEM((tm, tn), F32)], compiler_params=_cp(("parallel", "parallel", "arbitrary"), VMEM_BIG),
    )(*args)


def _ln_bwd(dsr, vnr, dvnc, p, ws01t, ln_g, dp):
    m = p.shape[0]

    def body(dsr_ref, vnr_ref, dvc_ref, va_ref, wt_ref, g_ref, dpi_ref, dp_ref, dw_ref, db_ref, dlg_ref, dlb_ref, dvn_ref):
        i = pl.program_id(0)

        @pl.when(i == 0)
        def _():
            dw_ref[...] = jnp.zeros_like(dw_ref)
            db_ref[...] = jnp.zeros_like(db_ref)
            dlg_ref[...] = jnp.zeros_like(dlg_ref)
            dlb_ref[...] = jnp.zeros_like(dlb_ref)

        for j in range(TOK // AC):
            for g in range(2):
                d = dsr_ref[AC * j:AC * j + AC, AC * g:AC * g + AC]
                d16 = d.astype(BF16)
                dvn_ref[AC * j:AC * j + AC, AC * g:AC * g + AC] = _nn(wt_ref[g], d16)
                dw_ref[g] += _nt(d16, vnr_ref[AC * j:AC * j + AC, AC * g:AC * g + AC])
                db_ref[g] += jnp.sum(d, axis=1, keepdims=True)
        dvn_ref[:, 256:384] = dvc_ref[0]
        dvn_ref[:, 384:512] = dvc_ref[1]
        dvn = dvn_ref[...]
        xf = va_ref[...].astype(F32)
        xc = xf - jnp.mean(xf, axis=-1, keepdims=True)
        rs = lax.rsqrt(jnp.mean(xc * xc, axis=-1, keepdims=True) + EPS)
        xh = xc * rs
        dlg_ref[...] += jnp.sum(dvn * xh, axis=0, keepdims=True)
        dlb_ref[...] += jnp.sum(dvn, axis=0, keepdims=True)
        dxh = dvn * g_ref[...]
        dva = rs * (dxh - jnp.mean(dxh, axis=-1, keepdims=True) - xh * jnp.mean(dxh * xh, axis=-1, keepdims=True))
        dp_ref[...] = dva.astype(BF16)

    return _pcall(
        body, name="ln_bwd", grid=(m // TOK,),
        in_specs=[pl.BlockSpec((TOK, 256), lambda i: (i, 0)), pl.BlockSpec((TOK, 256), lambda i: (i, 0)),
                  pl.BlockSpec((2, TOK, 128), lambda i: (0, i, 0)), pl.BlockSpec((TOK, 512), lambda i: (i, 9)),
                  _full((2, AC, AC)), _full((1, 512)), pl.BlockSpec(memory_space=pl.ANY)],
        out_specs=[pl.BlockSpec((TOK, 512), lambda i: (i, 3)), _full((2, AC, AC)), _full((2, AC, 1)), _full((1, 512)), _full((1, 512))],
        out_shape=[SDS((m, DPR), BF16), SDS((2, AC, AC), F32), SDS((2, AC, 1), F32), SDS((1, 512), F32), SDS((1, 512), F32)],
        scratch_shapes=[pltpu.VMEM((TOK, 512), F32)],
        input_output_aliases={6: 0}, compiler_params=_cp(("arbitrary",)),
    )(dsr, vnr, dvnc, p, ws01t, ln_g, dp)


def _tri_mm(tri, a):
    a1 = a.astype(BF16)
    r1 = a - a1.astype(F32)
    a2 = r1.astype(BF16)
    a3 = (r1 - a2.astype(F32)).astype(BF16)
    n = a.shape[1]
    r = _nn(tri, jnp.concatenate([a1, a2, a3], axis=1))
    return r[:, 0:n] + r[:, n:2 * n] + r[:, 2 * n:3 * n]


def _gla_masks(reverse):
    ri = lax.broadcasted_iota(jnp.int32, (CH, CH), 0)
    ci = lax.broadcasted_iota(jnp.int32, (CH, CH), 1)
    vis = (ci >= ri) if reverse else (ci <= ri)
    vis_t = (ci <= ri) if reverse else (ci >= ri)
    r4 = lax.broadcasted_iota(jnp.int32, (4 * CH, CH), 0) & (CH - 1)
    c4 = lax.broadcasted_iota(jnp.int32, (4 * CH, CH), 1)
    vis4 = (c4 >= r4) if reverse else (c4 <= r4)
    vis4_t = (c4 <= r4) if reverse else (c4 >= r4)
    lane = lax.broadcasted_iota(jnp.int32, (1, 256), 1)
    hm = [(lane >= CH * h) & (lane < CH * h + CH) for h in range(4)]
    return vis, vis_t, vis4, vis4_t, hm


def _stack_heads(x, hm):
    return jnp.concatenate([jnp.where(hm[h], x, 0.0).astype(BF16) for h in range(4)], axis=0)


def _diag_heads(full, hm):
    r = full.shape[0] // 4
    acc = jnp.where(hm[0], full[0:r], 0.0)
    for h in range(1, 4):
        acc = acc + jnp.where(hm[h], full[r * h:r * h + r], 0.0)
    return acc


def _rows_of_heads(x):
    return jnp.concatenate([x[:, 128 * h:128 * h + 128] for h in range(4)], axis=0)


def _lane_vis(reverse, transpose):
    ri = lax.broadcasted_iota(jnp.int32, (CH, 4 * CH), 0)
    ci = lax.broadcasted_iota(jnp.int32, (CH, 4 * CH), 1) & (CH - 1)
    return (ci >= ri) if (reverse != transpose) else (ci <= ri)


def _gla_fwd(p, qkv_blk, lr, lrw, gbias, s0, *, reverse, name):
    m = p.shape[0]
    nb = m // GLA_TB
    nc = GLA_TB // CH
    rmap = (lambda i: nb - 1 - i) if reverse else (lambda i: i)

    def body(qkv_ref, lr_ref, lrw_ref, gb_ref, s0_ref, o_ref, sb_ref, sfin_ref, st_ref):
        i = pl.program_id(0)

        @pl.when(i == 0)
        def _():
            st_ref[...] = s0_ref[...]

        vis, _, vis4, _, hm = _gla_masks(reverse)
        tri = vis.astype(F32).astype(BF16)
        logits = _nn(lr_ref[...].astype(BF16), lrw_ref[...]) + gb_ref[...]
        a_all = _logsig(logits) * (1.0 / 16.0)
        st = st_ref[...]
        for c in (range(nc - 1, -1, -1) if reverse else range(nc)):
            rows = slice(CH * c, CH * c + CH)
            b = _tri_mm(tri, a_all[rows])
            bl = b[0:1] if reverse else b[CH - 1:CH]
            q = qkv_ref[rows, 0:256].astype(F32) * 0.125
            k = qkv_ref[rows, 256:512].astype(F32)
            v16 = qkv_ref[rows, 512:1024].astype(BF16)
            qd = q * jnp.exp(b)
            kd16 = (k * jnp.exp(-b)).astype(BF16)
            kdec16 = (k * jnp.exp(bl - b)).astype(BF16)
            qstack = _stack_heads(qd, hm)
            sc = jnp.where(vis4, _nt(qstack, kd16), 0.0).astype(BF16)
            inter = _nt(qstack, st.astype(BF16))
            for h in range(4):
                o_ref[rows, 128 * h:128 * h + 128] = (
                    _nn(sc[CH * h:CH * h + CH], v16[:, 128 * h:128 * h + 128]) + inter[CH * h:CH * h + CH])
            sb_ref[c] = st
            st = st * jnp.exp(bl) + _diag_heads(_tn(v16, kdec16), hm)
        st_ref[...] = st

        @pl.when(i == nb - 1)
        def _():
            sfin_ref[...] = st

    return _pcall(
        body, name=name, grid=(nb,),
        in_specs=[pl.BlockSpec((GLA_TB, 1024), lambda i: (rmap(i), qkv_blk)), pl.BlockSpec((GLA_TB, LRW), lambda i: (rmap(i), 0)),
                  _full((LRW, 256)), _full((1, 256)), _full((128, 256))],
        out_specs=[pl.BlockSpec((GLA_TB, 512), lambda i: (rmap(i), 0)), pl.BlockSpec((nc, 128, 256), lambda i: (rmap(i), 0, 0)),
                   _full((128, 256))],
        out_shape=[SDS((m, 512), F32), SDS((m // CH, 128, 256), F32), SDS((128, 256), F32)],
        scratch_shapes=[pltpu.VMEM((128, 256), F32)], compiler_params=_cp(("arbitrary",)),
    )(p, lr, lrw, gbias, s0)


def _gla_bwd(p, qkv_blk, lr, lrw, lrwt, gbias, sb, dsfin, do, prev, dp, *, reverse, name):
    m = p.shape[0]
    nb = m // GLA_TB
    nc = GLA_TB // CH
    rmap = (lambda i: i) if reverse else (lambda i: nb - 1 - i)
    has_prev = prev is not None
    has_dp = dp is not None

    def body(*refs):
        refs = list(refs)
        qkv_ref, lr_ref, lrw_ref, lrwt_ref, gb_ref, sb_ref, dsfin_ref, do_ref = refs[:8]
        refs = refs[8:]
        if has_prev:
            pq_ref, plr_ref = refs[:2]
            refs = refs[2:]
        if has_dp:
            refs = refs[1:]
        dqkv_ref, dlr_ref, dw2_ref, dgb_ref, ds0_ref, dst_ref, dlog_ref = refs
        i = pl.program_id(0)

        @pl.when(i == 0)
        def _():
            dst_ref[...] = dsfin_ref[...]
            dw2_ref[...] = jnp.zeros_like(dw2_ref)
            dgb_ref[...] = jnp.zeros_like(dgb_ref)

        vis, vis_t, vis4, vis4_t, hm = _gla_masks(reverse)
        tri = vis.astype(F32).astype(BF16)
        tri_t = vis_t.astype(F32).astype(BF16)
        lane_vis = _lane_vis(reverse, False)
        lane_vis_t = _lane_vis(reverse, True)
        lr16 = lr_ref[...].astype(BF16)
        logits = _nn(lr16, lrw_ref[...]) + gb_ref[...]
        a_all = _logsig(logits) * (1.0 / 16.0)
        dsig = (1.0 - jax.nn.sigmoid(logits)) * (1.0 / 16.0)
        dst = dst_ref[...]
        for c in (range(nc) if reverse else range(nc - 1, -1, -1)):
            rows = slice(CH * c, CH * c + CH)
            b = _tri_mm(tri, a_all[rows])
            bl = b[0:1] if reverse else b[CH - 1:CH]
            eb = jnp.exp(b)
            enb = jnp.exp(-b)
            ebl = jnp.exp(bl - b)
            el = jnp.exp(bl)
            q = qkv_ref[rows, 0:256].astype(F32) * 0.125
            k = qkv_ref[rows, 256:512].astype(F32)
            v16 = qkv_ref[rows, 512:1024].astype(BF16)
            do16 = do_ref[rows, :].astype(BF16)
            qd = q * eb
            kd = k * enb
            kdec = k * ebl
            st = sb_ref[c]
            st16 = st.astype(BF16)
            dst16 = dst.astype(BF16)
            qd16 = qd.astype(BF16)
            kd16 = kd.astype(BF16)
            qstack = _stack_heads(qd, hm)
            kstack = _stack_heads(kd, hm)
            kdecstack = _stack_heads(kdec, hm)
            pt = jnp.where(vis4_t, _nt(kstack, qd16), 0.0).astype(BF16)
            dvinter = _nt(kdecstack, dst16)
            do_rows = _rows_of_heads(do16)
            v_rows = _rows_of_heads(v16)
            dp_cat = jnp.where(lane_vis, _diag_heads(_nt(do_rows, v_rows), hm), 0.0).astype(BF16)
            dpt_cat = jnp.where(lane_vis_t, _diag_heads(_nt(v_rows, do_rows), hm), 0.0).astype(BF16)
            dqd = _nn(dp_cat, kstack) + _diag_heads(_nn(do_rows, st16), hm)
            dkd = _nn(dpt_cat, qstack)
            dkdec = _diag_heads(_nn(v_rows, dst16), hm)
            for h in range(4):
                rh = slice(CH * h, CH * h + CH)
                dv_h = _nn(pt[rh], do_rows[rh]) + dvinter[rh]
                if has_prev:
                    dv_h = dv_h + pq_ref[rows, 512 + 128 * h:512 + 128 * h + 128]
                dqkv_ref[rows, 512 + 128 * h:512 + 128 * h + 128] = dv_h.astype(dqkv_ref.dtype)
            dq = dqd * eb * 0.125
            dk = dkd * enb + dkdec * ebl
            if has_prev:
                dq = dq + pq_ref[rows, 0:256]
                dk = dk + pq_ref[rows, 256:512]
            dqkv_ref[rows, 0:256] = dq.astype(dqkv_ref.dtype)
            dqkv_ref[rows, 256:512] = dk.astype(dqkv_ref.dtype)
            g_kdec = dkdec * kdec
            db = dqd * qd - dkd * kd - g_kdec
            dbl = jnp.sum(g_kdec, axis=0, keepdims=True) + jnp.sum(st * dst, axis=0, keepdims=True) * el
            da = _tri_mm(tri_t, db) + dbl
            dlog_ref[rows, :] = da * dsig[rows]
            dst = dst * el + _diag_heads(_tn(do16, qd16), hm)
        dst_ref[...] = dst
        dlog = dlog_ref[...]
        dlog16 = dlog.astype(BF16)
        dlr = _nn(dlog16, lrwt_ref[...])
        if has_prev:
            dlr = dlr + plr_ref[...]
        dlr_ref[...] = dlr
        dw2_ref[...] += _tn(lr16, dlog16)
        dgb_ref[...] += jnp.sum(dlog, axis=0, keepdims=True)

        @pl.when(i == nb - 1)
        def _():
            ds0_ref[...] = dst

    in_specs = [pl.BlockSpec((GLA_TB, 1024), lambda i: (rmap(i), qkv_blk)), pl.BlockSpec((GLA_TB, LRW), lambda i: (rmap(i), 0)),
                _full((LRW, 256)), _full((256, LRW)), _full((1, 256)), pl.BlockSpec((nc, 128, 256), lambda i: (rmap(i), 0, 0)),
                _full((128, 256)), pl.BlockSpec((GLA_TB, 512), lambda i: (rmap(i), 0))]
    args = [p, lr, lrw, lrwt, gbias, sb, dsfin, do]
    if has_prev:
        in_specs += [pl.BlockSpec((GLA_TB, 1024), lambda i: (rmap(i), 0)), pl.BlockSpec((GLA_TB, LRW), lambda i: (rmap(i), 0))]
        args += list(prev)
    aliases = {}
    if has_dp:
        in_specs.append(pl.BlockSpec(memory_space=pl.ANY))
        aliases = {len(args): 0}
        args.append(dp)
        dq_spec = pl.BlockSpec((GLA_TB, 1024), lambda i: (rmap(i), 2))
        dq_shape = SDS(dp.shape, dp.dtype)
    else:
        dq_spec = pl.BlockSpec((GLA_TB, 1024), lambda i: (rmap(i), 0))
        dq_shape = SDS((m, 1024), F32)
    return _pcall(
        body, name=name, grid=(nb,), in_specs=in_specs,
        out_specs=[dq_spec, pl.BlockSpec((GLA_TB, LRW), lambda i: (rmap(i), 0)), _full((LRW, 256)), _full((1, 256)), _full((128, 256))],
        out_shape=[dq_shape, SDS((m, LRW), F32), SDS((LRW, 256), F32), SDS((1, 256), F32), SDS((128, 256), F32)],
        scratch_shapes=[pltpu.VMEM((128, 256), F32), pltpu.VMEM((GLA_TB, 256), F32)],
        input_output_aliases=aliases, compiler_params=_cp(("arbitrary",)),
    )(*args)


def _device_step(x, c, ctx, c_ctx, tgt, wm, bm, ng, wit_g, wit_r, wlrt, ln_g, ln_b, ws, bs, w2, gb2, gbn, wpa, wpb, wo, gf,
                 exchange=None):
    L = x.shape[0]
    wit_qkv = wit_r[2048:3072]
    ws16 = ws.astype(BF16)
    wst16 = jnp.swapaxes(ws, 1, 2).astype(BF16)
    bscol = bs[:, :, None]
    lrw = [jnp.zeros((LRW, 256), F32).at[16 * r:16 * r + 16].set(w2[r]).astype(BF16) for r in range(2)]
    lrwt = [w.T for w in lrw]
    gbias = [gb2[r:r + 1] for r in range(2)]

    cc = jnp.zeros((8, D), F32).at[0:1].set(c).at[1:2].set(c_ctx)
    mod = _modvec(cc, wm, bm)
    shift, scale, gate = mod[0:1, 0:D], mod[0:1, D:2 * D], mod[0:1, 2 * D:3 * D]
    shift_c, scale_c = mod[1:2, 0:D], mod[1:2, D:2 * D]

    hc = _prep_h(ctx, ng, scale_c, shift_c, "prep_hc")
    pc = _mm(hc, wit_qkv, tm=256, tn=1024, tk=D, out_dtype=F32, name="mm_pc", b_t=True)
    plrc = _mm(hc, wlrt, tm=256, tn=LRW, tk=D, out_dtype=F32, name="mm_plrc", b_t=True)
    zero_s = jnp.zeros((128, 256), F32)
    _, sbc_f, sc_f = _gla_fwd(pc, 0, plrc, lrw[0], gbias[0], zero_s, reverse=False, name="gla_fwd_cf")
    _, sbc_b, sc_b = _gla_fwd(pc, 0, plrc, lrw[1], gbias[1], zero_s, reverse=True, name="gla_fwd_cb")

    h, p, plr = _proj_fwd(x, ng, scale, shift, wit_g, wit_r, wlrt)
    o_f, sb_f, _ = _gla_fwd(p, 2, plr, lrw[0], gbias[0], sc_f, reverse=False, name="gla_fwd_f")
    o_b, sb_b, _ = _gla_fwd(p, 2, plr, lrw[1], gbias[1], sc_b, reverse=True, name="gla_fwd_b")
    vnr, vnc = _ln_fwd(p, ln_g, ln_b)
    svc = _colmix_fwd(vnc.reshape(2, AC, L), ws16[2:4], bscol[2:4]).reshape(2, L, 128)
    ya_in, yb_in, svr, mrg, dx1, dout, loss, dgate, dgf = _tail_fwd(
        o_f, o_b, p, vnr, svc, x, tgt, ws16[0:2], bscol[0:2], gbn, wpa, wpb, wo, gate, gf)

    dya, dyb, dp_g, dp, dsr, dsc, do, dgbn = _tail_bwd(dout, ya_in, yb_in, p, svr, svc, o_f, o_b, gbn, wo, wpa, wpb)
    dwo = _mm_tn(mrg, dout, ta=D, tn=D, tk=1024, name="mm_dwo")
    dwpa = _mm_tn(ya_in, dya, ta=512, tn=D, tk=1024, name="mm_dwpa")
    dwpb = _mm_tn(yb_in, dyb, ta=512, tn=D, tk=1024, name="mm_dwpb")
    dvnc, dws23, dbs23 = _colmix_bwd(dsc.reshape(2, AC, L), vnc.reshape(2, AC, L), wst16[2:4])
    dp, dws01, dbs01, dlng, dlnb = _ln_bwd(dsr, vnr, dvnc.reshape(2, L, 128), p, wst16[0:2], ln_g, dp)
    zero_ds = jnp.zeros((128, 256), F32)
    dqkv_f, dlr_f, dw2_f, dgb_f, ds0_f = _gla_bwd(p, 2, plr, lrw[0], lrwt[0], gbias[0], sb_f, zero_ds, do, None, None,
                                                  reverse=False, name="gla_bwd_f")
    dp, dlr, dw2_b, dgb_b, ds0_b = _gla_bwd(p, 2, plr, lrw[1], lrwt[1], gbias[1], sb_b, zero_ds, do, (dqkv_f, dlr_f), dp,
                                            reverse=True, name="gla_bwd_b")
    zero_do = jnp.zeros((ctx.shape[0], 512), F32)
    dqkvc_f, dlrc_f, dw2c_f, dgbc_f, _ = _gla_bwd(pc, 0, plrc, lrw[0], lrwt[0], gbias[0], sbc_f, ds0_f, zero_do, None, None,
                                                  reverse=False, name="gla_bwd_cf")
    dqkvc, dlrc, dw2c_b, dgbc_b, _ = _gla_bwd(pc, 0, plrc, lrw[1], lrwt[1], gbias[1], sbc_b, ds0_b, zero_do,
                                              (dqkvc_f, dlrc_f), None, reverse=True, name="gla_bwd_cb")
    dhc = _mm(dqkvc, wit_qkv, tm=256, tn=D, tk=1024, out_dtype=F32, name="mm_dhc")
    dhc = _mm(dlrc, wlrt, tm=256, tn=D, tk=LRW, out_dtype=F32, name="mm_dhc_lr", acc=dhc)
    _, dng_c, dscale_c, dshift_c = _prep_bwd(ctx, dhc, None, ng, scale_c, "prep_bwd_c")

    dwit_g = _mm_tn(dp_g, h, ta=1024, tn=D, tk=2048, name="mm_dwi_g")
    dwit_r = _mm_tn(dp, h, ta=1024, tn=D, tk=2048, name="mm_dwi_r")
    dwit_qkv = _mm_tn(dqkvc, hc, ta=1024, tn=D, tk=256, name="mm_dwi_c", acc=dwit_r[2048:3072])
    dwlrt = _mm_tn(dlr, h, ta=LRW, tn=D, tk=2048, name="mm_dwlr")
    dwlrt = _mm_tn(dlrc, hc, ta=LRW, tn=D, tk=256, name="mm_dwlr_c", acc=dwlrt)
    big = dict(dwit_g=dwit_g, dwit_r=dwit_r, dwit_qkv=dwit_qkv, dwlrt=dwlrt, dwpa=dwpa, dwpb=dwpb, dwo=dwo)

    send = exchange(big) if exchange is not None else ()
    (dx, dng, dscale, dshift), got = _proj_bwd(dp_g, dp, dlr, wit_g, wit_r, wlrt, x, dx1, ng, scale, send)

    dmodc = jnp.concatenate([dshift_c, dscale_c], axis=1)
    dscc = _dcctx(jnp.zeros((8, 2 * D), F32).at[0:1].set(dmodc), wm)[0:1]
    dw2p = dw2_f + dw2c_f, dw2_b + dw2c_b
    return dict(
        loss=loss[0, 0], dx=dx, got=got, **big,
        dmod=jnp.concatenate([dshift, dscale, dgate], axis=1), dmodc=dmodc, dscc=dscc, dng=dng + dng_c,
        dlng=dlng, dlnb=dlnb, dws=jnp.concatenate([dws01, dws23], axis=0),
        dbs=jnp.concatenate([dbs01, dbs23], axis=0)[:, :, 0], dgbn=dgbn, dgf=dgf,
        dw2=jnp.stack([dw2p[0][0:16], dw2p[1][16:32]]), dgb2=jnp.concatenate([dgb_f + dgbc_f, dgb_b + dgbc_b], axis=0),
    )


ANY = pl.BlockSpec(memory_space=pl.ANY)


def _coords():
    return lax.axis_index("x"), lax.axis_index("y"), lax.axis_index("c")


def _flip(v, bit):
    return 1 - v if bit else v


def _remote(src, dst, send_sem, recv_sem, dev):
    return pltpu.make_async_remote_copy(src_ref=src, dst_ref=dst, send_sem=send_sem, recv_sem=recv_sem,
                                        device_id=dev, device_id_type=MESH)


def _own(out, block, idx):
    return lax.dynamic_update_slice_in_dim(out, block[None], idx, axis=0)


def _half_idx(shape, axis, which, lead=()):
    idx = [pl.ds(0, d) for d in shape]
    h = shape[axis] // 2
    idx[axis] = pl.ds(which * h, h)
    return tuple(lead) + tuple(idx)


def _gather_weights(split, whole, name):
    ns, nw = len(split), len(whole)
    n = ns + nw
    arrs = [a for a, _ in split] + list(whole)

    def body(*refs):
        ins, outs = refs[:n], refs[n:2 * n]
        a_send, a_recv, b_send, b_recv = refs[2 * n:]
        x, y, c = _coords()
        me = 2 * x + y
        sib = (x, y, 1 - c)
        peers = [(1 - x, y), (x, 1 - y), (1 - x, 1 - y)]

        def half(k, slot, which):
            return outs[k].at[_half_idx(arrs[k].shape, split[k][1], which, lead=(slot,))]

        sends = []
        for k in range(n):
            for j, (px, py) in enumerate(peers):
                if k < ns:
                    rc = _remote(ins[k].at[_half_idx(arrs[k].shape, split[k][1], c)], half(k, me, c), a_send.at[3 * k + j],
                                 a_recv.at[3 * k + j], (px, py, c))
                else:
                    rc = _remote(ins[k], outs[k].at[me], a_send.at[3 * k + j], a_recv.at[3 * k + j], (px, py, c))
                rc.start()
                sends.append(rc)
        for k in range(ns):
            for j, (px, py) in enumerate(peers):
                landed = half(k, 2 * px + py, c)
                _remote(landed, landed, a_send.at[3 * k + j], a_recv.at[3 * k + j], (px, py, c)).wait_recv()
                fw = _remote(landed, landed, b_send.at[3 * k + j], b_recv.at[3 * k + j], sib)
                fw.start()
                sends.append(fw)
        for k in range(ns, n):
            for j, (px, py) in enumerate(peers):
                landed = outs[k].at[2 * px + py]
                _remote(landed, landed, a_send.at[3 * k + j], a_recv.at[3 * k + j], (px, py, c)).wait_recv()
        for k in range(ns):
            for j, (px, py) in enumerate(peers):
                passed = half(k, 2 * px + py, 1 - c)
                _remote(passed, passed, b_send.at[3 * k + j], b_recv.at[3 * k + j], sib).wait_recv()
        for rc in sends:
            rc.wait_send()

    outs = _pcall(
        body, name=name, in_specs=[ANY] * n, out_specs=[ANY] * n,
        out_shape=[SDS((4,) + a.shape, a.dtype) for a in arrs],
        scratch_shapes=[pltpu.SemaphoreType.DMA((3 * n,)), pltpu.SemaphoreType.DMA((3 * n,)), pltpu.SemaphoreType.DMA((3 * ns,)),
                        pltpu.SemaphoreType.DMA((3 * ns,))],
    )(*arrs)
    me_xy = 2 * lax.axis_index("x") + lax.axis_index("y")
    return [_own(o, a, me_xy) for o, a in zip(outs, arrs)]


def _gather_all(a, name):
    masks = [(mx, my, mc) for mx in range(2) for my in range(2) for mc in range(2)][1:]

    def body(in_ref, out_ref, send_sems, recv_sems):
        x, y, c = _coords()
        me = 4 * x + 2 * y + c
        sends = []
        for j, (mx, my, mc) in enumerate(masks):
            peer = (_flip(x, mx), _flip(y, my), _flip(c, mc))
            rc = pltpu.make_async_remote_copy(
                src_ref=in_ref, dst_ref=out_ref.at[me], send_sem=send_sems.at[j], recv_sem=recv_sems.at[j],
                device_id=peer, device_id_type=MESH)
            rc.start()
            sends.append(rc)
        for j, (mx, my, mc) in enumerate(masks):
            px, py, pc = _flip(x, mx), _flip(y, my), _flip(c, mc)
            pltpu.make_async_remote_copy(
                src_ref=in_ref, dst_ref=out_ref.at[4 * px + 2 * py + pc], send_sem=send_sems.at[j], recv_sem=recv_sems.at[j],
                device_id=(px, py, pc), device_id_type=MESH).wait_recv()
        for rc in sends:
            rc.wait_send()

    out = _pcall(
        body, name=name, in_specs=[ANY], out_specs=ANY, out_shape=SDS((8,) + a.shape, a.dtype),
        scratch_shapes=[pltpu.SemaphoreType.DMA((7,)), pltpu.SemaphoreType.DMA((7,))],
    )(a)
    return _own(out, a, 4 * lax.axis_index("x") + 2 * lax.axis_index("y") + lax.axis_index("c"))


def _half_shape(shape, axis):
    return tuple(d // 2 if i == axis else d for i, d in enumerate(shape))


def _swap_half_c(arrs, axes, name):
    n = len(arrs)

    def body(*refs):
        ins, outs = refs[:n], refs[n:2 * n]
        send_sems, recv_sems = refs[2 * n:]
        x, y, c = _coords()
        sends = []
        for k in range(n):
            rc = _remote(ins[k].at[_half_idx(arrs[k].shape, axes[k], 1 - c)], outs[k], send_sems.at[k], recv_sems.at[k],
                         (x, y, 1 - c))
            rc.start()
            sends.append(rc)
        for rc in sends:
            rc.wait()

    return _pcall(
        body, name=name, in_specs=[ANY] * n, out_specs=[ANY] * n,
        out_shape=[SDS(_half_shape(a.shape, ax), a.dtype) for a, ax in zip(arrs, axes)],
        scratch_shapes=[pltpu.SemaphoreType.DMA((n,)), pltpu.SemaphoreType.DMA((n,))],
    )(*arrs)


def _a2a_xy(arrs, name):
    n = len(arrs)

    def body(*refs):
        ins, outs = refs[:n], refs[n:2 * n]
        send_sems, recv_sems = refs[2 * n:]
        x, y, c = _coords()
        me = 2 * x + y
        peers = [(1 - x, y), (x, 1 - y), (1 - x, 1 - y)]
        sends = []
        for k in range(n):
            for j, (px, py) in enumerate(peers):
                rc = _remote(ins[k].at[2 * px + py], outs[k].at[me], send_sems.at[3 * k + j], recv_sems.at[3 * k + j], (px, py, c))
                rc.start()
                sends.append(rc)
        for k in range(n):
            for j, (px, py) in enumerate(peers):
                landed = outs[k].at[2 * px + py]
                _remote(landed, landed, send_sems.at[3 * k + j], recv_sems.at[3 * k + j], (px, py, c)).wait_recv()
        for rc in sends:
            rc.wait_send()

    outs = _pcall(
        body, name=name, in_specs=[ANY] * n, out_specs=[ANY] * n, out_shape=[SDS(a.shape, a.dtype) for a in arrs],
        scratch_shapes=[pltpu.SemaphoreType.DMA((3 * n,)), pltpu.SemaphoreType.DMA((3 * n,))],
    )(*arrs)
    me_xy = 2 * lax.axis_index("x") + lax.axis_index("y")
    return [_own(o, lax.dynamic_index_in_dim(a, me_xy, axis=0, keepdims=False), me_xy) for o, a in zip(outs, arrs)]


def _exchange_c(arrs, name):
    n = len(arrs)

    def body(*refs):
        ins, outs = refs[:n], refs[n:2 * n]
        send_sems, recv_sems = refs[2 * n:]
        x, y, c = _coords()
        sends = []
        for k in range(n):
            rc = _remote(ins[k], outs[k], send_sems.at[k], recv_sems.at[k], (x, y, 1 - c))
            rc.start()
            sends.append(rc)
        for rc in sends:
            rc.wait()

    return _pcall(
        body, name=name, in_specs=[ANY] * n, out_specs=[ANY] * n, out_shape=[SDS(a.shape, a.dtype) for a in arrs],
        scratch_shapes=[pltpu.SemaphoreType.DMA((n,)), pltpu.SemaphoreType.DMA((n,))],
    )(*arrs)


def _join_halves(halves, axes, name):
    n = len(halves)
    full = [tuple(2 * d if i == ax else d for i, d in enumerate(a.shape)) for a, ax in zip(halves, axes)]

    def body(*refs):
        ins, outs = refs[:n], refs[n:2 * n]
        send_sems, recv_sems = refs[2 * n:]
        x, y, c = _coords()
        sends = []
        for k in range(n):
            rc = _remote(ins[k], outs[k].at[_half_idx(full[k], axes[k], c)], send_sems.at[k], recv_sems.at[k], (x, y, 1 - c))
            rc.start()
            sends.append(rc)
        for k in range(n):
            landed = outs[k].at[_half_idx(full[k], axes[k], 1 - c)]
            _remote(landed, landed, send_sems.at[k], recv_sems.at[k], (x, y, 1 - c)).wait_recv()
        for rc in sends:
            rc.wait_send()

    outs = _pcall(
        body, name=name, in_specs=[ANY] * n, out_specs=[ANY] * n,
        out_shape=[SDS(f, a.dtype) for f, a in zip(full, halves)],
        scratch_shapes=[pltpu.SemaphoreType.DMA((n,)), pltpu.SemaphoreType.DMA((n,))],
    )(*halves)
    ci = lax.axis_index("c")
    return [lax.dynamic_update_slice_in_dim(o, a, ci * a.shape[ax], axis=ax) for o, a, ax in zip(outs, halves, axes)]


def _pair_sum(a, got, cidx, axis, name):
    _, r, cdim = a.shape
    hshape = _half_shape(a.shape, axis)

    def body(c_ref, a_ref, g_ref, o_ref):
        o_ref[...] = (a_ref[...] + g_ref[...]).astype(BF16)

    if axis == 1:
        tr = min(r // 2, 256)
        nj = (r // 2) // tr
        blk = pl.BlockSpec((1, tr, cdim), lambda s, j, c: (s, j, 0))
        a_spec = pl.BlockSpec((1, tr, cdim), lambda s, j, c: (s, c[0] * nj + j, 0))
    else:
        nj = (cdim // 2) // 128
        blk = pl.BlockSpec((1, r, 128), lambda s, j, c: (s, 0, j))
        a_spec = pl.BlockSpec((1, r, 128), lambda s, j, c: (s, 0, c[0] * nj + j))
    return _pcall(
        body, name=name, out_shape=SDS(hshape, BF16),
        grid_spec=pltpu.PrefetchScalarGridSpec(num_scalar_prefetch=1, grid=(4, nj), in_specs=[a_spec, blk], out_specs=blk),
        compiler_params=_cp(("parallel", "parallel")),
    )(cidx, a, got)


def _sum_chips(parts, name):
    _, h, cdim = parts.shape

    def body(p_ref, o_ref):
        acc = p_ref[0].astype(F32)
        for k in range(1, 4):
            acc = acc + p_ref[k].astype(F32)
        o_ref[...] = acc

    if h % 256 == 0 or h in (128,):
        tr = min(h, 256)
        grid, in_spec, out_spec = (h // tr,), pl.BlockSpec((4, tr, cdim), lambda i: (0, i, 0)), pl.BlockSpec((tr, cdim), lambda i: (i, 0))
    else:
        grid, in_spec, out_spec = (cdim // 128,), pl.BlockSpec((4, h, 128), lambda i: (0, 0, i)), pl.BlockSpec((h, 128), lambda i: (0, i))
    return _pcall(
        body, name=name, grid=grid, in_specs=[in_spec], out_specs=out_spec, out_shape=SDS((h, cdim), F32),
        compiler_params=_cp(("parallel",)),
    )(parts)


def _sum_slots(a, name, rows):
    s, n, _ = a.shape

    def body(a_ref, o_ref):
        acc = a_ref[0]
        for k in range(1, s):
            acc = acc + a_ref[k]
        o_ref[...] = acc

    return _pcall(
        body, name=name, grid=(n // rows,), in_specs=[pl.BlockSpec((s, rows, 128), lambda i: (0, i, 0))],
        out_specs=pl.BlockSpec((rows, 128), lambda i: (i, 0)), out_shape=SDS((n, 128), F32),
        compiler_params=_cp(("parallel",)),
    )(a)


def _adam_math(w, g, m, v):
    nm = ADAM_B1 * m + (1.0 - ADAM_B1) * g
    nv = ADAM_B2 * v + (1.0 - ADAM_B2) * (g * g)
    m_hat = nm / (1.0 - ADAM_B1 ** ADAM_STEP)
    v_hat = nv / (1.0 - ADAM_B2 ** ADAM_STEP)
    return -ADAM_LR * (m_hat / (jnp.sqrt(v_hat) + ADAM_EPS) + ADAM_WD * w), nm, nv


def _adamw(w, g, m, v, name, rows):
    r, cdim = w.shape

    def body(w_ref, g_ref, m_ref, v_ref, d_ref, nm_ref, nv_ref):
        d_ref[...], nm_ref[...], nv_ref[...] = _adam_math(w_ref[...], g_ref[...], m_ref[...], v_ref[...])

    blk = pl.BlockSpec((rows, cdim), lambda i: (i, 0))
    return _pcall(
        body, name=name, grid=(r // rows,), in_specs=[blk] * 4, out_specs=[blk] * 3,
        out_shape=[SDS(w.shape, F32)] * 3, compiler_params=_cp(("parallel",)),
    )(w, g, m, v)


def _adamw_joined(w, mine, other, m, v, cidx, axis, name, rows):
    r, cdim = w.shape
    if axis == 0:
        rows = r

    def body(c_ref, w_ref, a_ref, b_ref, m_ref, v_ref, g_ref, d_ref, nm_ref, nv_ref):
        a, b = a_ref[...], b_ref[...]
        g = jnp.where(c_ref[0] == 0, jnp.concatenate([a, b], axis=axis), jnp.concatenate([b, a], axis=axis))
        g_ref[...] = g
        d_ref[...], nm_ref[...], nv_ref[...] = _adam_math(w_ref[...], g, m_ref[...], v_ref[...])

    blk = pl.BlockSpec((rows, cdim), lambda i, c: (i, 0))
    hshape = (rows // 2, cdim) if axis == 0 else (rows, cdim // 2)
    hblk = pl.BlockSpec(hshape, lambda i, c: (i, 0))
    return _pcall(
        body, name=name, out_shape=[SDS(w.shape, F32)] * 4,
        grid_spec=pltpu.PrefetchScalarGridSpec(num_scalar_prefetch=1, grid=(r // rows,), in_specs=[blk, hblk, hblk, blk, blk],
                                               out_specs=[blk] * 4),
        compiler_params=_cp(("parallel",)),
    )(cidx, w, mine, other, m, v)


def _adamw_many(ws, gs, ms, vs, name):
    n = len(ws)

    def body(*refs):
        outs = refs[4 * n:]
        for k in range(n):
            d, nm, nv = _adam_math(refs[k][...], refs[n + k][...], refs[2 * n + k][...], refs[3 * n + k][...])
            outs[k][...] = d
            outs[n + k][...] = nm
            outs[2 * n + k][...] = nv

    res = _pcall(body, name=name, out_shape=[SDS(w.shape, F32) for w in ws] * 3)(*ws, *gs, *ms, *vs)
    return res[:n], res[n:2 * n], res[2 * n:]


def _pack(pieces, rows):
    flat = jnp.concatenate([p.reshape(-1) for p in pieces])
    return jnp.pad(flat, (0, rows * 128 - flat.shape[0])).reshape(rows, 128)


def _unpack(buf, shapes):
    flat = buf.reshape(-1)
    out, off = [], 0
    for shp in shapes:
        size = 1
        for s in shp:
            size *= s
        out.append(flat[off:off + size].reshape(shp))
        off += size
    return out


def _perm_cols(w):
    perm = jnp.concatenate([w[..., 3104:5152], w[..., 0:1024], w[..., 1056:1568], w[..., 1568:2080], w[..., 2592:3104],
                            w[..., 2080:2592]], axis=-1)
    return perm, w[..., 1024:1056]


def _unperm_cols(perm, lr32):
    return jnp.concatenate([perm[..., 2048:3072], lr32, perm[..., 3072:3584], perm[..., 3584:4096], perm[..., 4608:5120],
                            perm[..., 4096:4608], perm[..., 0:2048]], axis=-1)


SMALL_ROWS = 672
HALF_ROWS = 7200


def kernel(x, c, ctx, c_ctx, w_mod, b_mod, norm_g, w_in, a_ln_g, a_ln_b, a_ws, a_bs, b_gate_w2, b_gate_b, b_norm_g, w_proj_a, w_proj_b, w_out, final_norm_g, loss_target, m_c_ctx, m_w_mod, m_b_mod, m_norm_g, m_w_in, m_a_ln_g, m_a_ln_b, m_a_ws, m_a_bs, m_b_gate_w2, m_b_gate_b, m_b_norm_g, m_w_proj_a, m_w_proj_b, m_w_out, m_final_norm_g, v_c_ctx, v_w_mod, v_b_mod, v_norm_g, v_w_in, v_a_ln_g, v_a_ln_b, v_a_ws, v_a_bs, v_b_gate_w2, v_b_gate_b, v_b_norm_g, v_w_proj_a, v_w_proj_b, v_w_out, v_final_norm_g):
    xi, yi, ci = _coords()
    me_xy = 2 * xi + yi

    gate_pack = _pack([b_gate_w2[0], b_gate_b[0]], 24)
    w_in_t, m_w_in_t, v_w_in_t = (jnp.swapaxes(a[0], 0, 1) for a in (w_in, m_w_in, v_w_in))
    g_wit, g_wm, g_wpa, g_wpb, g_wo, g_gate = _gather_weights(
        [(w_in_t.astype(BF16), 1), (w_mod[0].astype(BF16), 0), (w_proj_a[0].astype(BF16), 0), (w_proj_b[0].astype(BF16), 0),
         (w_out[0].astype(BF16), 0)], [gate_pack], "gather_weights")
    wit_u = g_wit.reshape(4 * 1288, D)
    wit_g = wit_u[3104:5152]
    wit_r = jnp.concatenate([wit_u[1056:1568], wit_u[1568:2080], wit_u[2592:3104], wit_u[2080:2592], wit_u[0:1024]], axis=0)
    wlrt = jnp.pad(wit_u[1024:1056], ((0, LRW - 32), (0, 0)))
    wm = jnp.swapaxes(g_wm, 0, 1).reshape(D, 3 * D)
    wpa = jnp.swapaxes(g_wpa, 0, 1).reshape(512, D)
    wpb = jnp.swapaxes(g_wpb, 0, 1).reshape(512, D)
    wo = g_wo.reshape(D, D)
    gflat = g_gate.reshape(4, 24 * 128)
    w2 = jnp.swapaxes(gflat[:, 0:2048].reshape(4, 2, 16, 64), 0, 2)
    w2 = jnp.swapaxes(w2, 0, 1).reshape(2, 16, 256)
    gb2 = jnp.swapaxes(gflat[:, 2048:2176].reshape(4, 2, 64), 0, 1).reshape(2, 256)

    tags = ["wi", "wpa", "wpb", "wo"]
    half_axes = [2, 1, 1, 1]
    sent = []

    def exchange(g):
        dwr = g["dwit_r"]
        dwit_u = jnp.concatenate([g["dwit_qkv"], g["dwlrt"][0:32], dwr[0:512], dwr[512:1024], dwr[1536:2048], dwr[1024:1536],
                                  g["dwit_g"]], axis=0)
        big = [dwit_u.reshape(4, 1288, D), jnp.swapaxes(g["dwpa"].reshape(512, 4, 256), 0, 1),
               jnp.swapaxes(g["dwpb"].reshape(512, 4, 256), 0, 1), g["dwo"].reshape(4, 256, D)]
        other = _swap_half_c(big, half_axes, "swap_half_in")
        cidx = jnp.reshape(ci, (1,)).astype(jnp.int32)
        sent.extend(_pair_sum(a, o, cidx, ax, "sum_pair_" + t) for a, o, ax, t in zip(big, other, half_axes, tags))
        return sent

    r = _device_step(x[0], c, ctx[0], c_ctx[None], loss_target[0], wm, b_mod, norm_g, wit_g, wit_r, wlrt, a_ln_g, a_ln_b,
                     a_ws[0], a_bs[0], w2, gb2, b_norm_g, wpa, wpb, wo, final_norm_g[None], exchange)

    small = _pack([r["dmod"], c, r["dmodc"], r["dscc"], r["dng"], r["dlng"], r["dlnb"], r["dws"], r["dbs"], r["dgbn"], r["dgf"],
                   r["dw2"], r["dgb2"], jnp.broadcast_to(r["loss"], (128,))], SMALL_ROWS)
    small_all = _gather_all(small, "gather_small")
    small_sum = _sum_slots(small_all, "sum_small", SMALL_ROWS // 4)
    (s_dmod, _, s_dmodc, s_dscc, s_dng, s_dlng, s_dlnb, s_dws, s_dbs, s_dgbn, s_dgf, s_dw2, s_dgb2, s_loss) = _unpack(
        small_sum, [(1, 3 * D), (1, D), (1, 2 * D), (D,), (1, D), (1, 512), (1, 512), (1, 4, 128, 128), (1, 4, 128), (1, 512),
                    (D,), (2, 16, 256), (2, 256), (128,)])
    loss = s_loss[0]
    s_dmodc_p = jnp.pad(s_dmodc, ((0, 0), (0, D)))
    g_b_mod = s_dmod + s_dmodc_p
    sg = jax.nn.sigmoid(c_ctx)
    g_c_ctx = s_dscc * (sg * (1.0 + c_ctx * (1.0 - sg)))
    g_w2 = lax.dynamic_slice_in_dim(s_dw2, 64 * me_xy, 64, axis=2)[None]
    g_gb2 = lax.dynamic_slice_in_dim(s_dgb2, 64 * me_xy, 64, axis=1)[None]

    flat_all = small_all.reshape(8, SMALL_ROWS * 128)
    dmod_all = flat_all[:, 0:3 * D]
    c_all = flat_all[:, 3 * D:4 * D]
    lhs = jnp.concatenate([_silu(c_all), _silu(c_ctx)[None], jnp.zeros((7, D), F32)], axis=0)
    rhs = jnp.concatenate([dmod_all, s_dmodc_p, jnp.zeros((7, 3 * D), F32)], axis=0)
    rhs = lax.dynamic_slice_in_dim(rhs, 768 * me_xy, 768, axis=1)
    g_w_mod = _mm(lhs.T.astype(BF16), rhs.astype(BF16), tm=D, tn=768, tk=16, out_dtype=F32, name="mm_dwm")

    parts = [_own(g, lax.dynamic_index_in_dim(s_, me_xy, axis=0, keepdims=False), me_xy) for g, s_ in zip(r["got"], sent)]
    halves = [_sum_chips(p_, "sum_chips_" + t) for p_, t in zip(parts, tags)]
    others = _exchange_c(halves, "swap_half_out")

    cidx = jnp.reshape(ci, (1,)).astype(jnp.int32)
    g_w_in_t, d_w_in_t, nm_w_in_t, nv_w_in_t = _adamw_joined(w_in_t, halves[0], others[0], m_w_in_t, v_w_in_t, cidx, 1,
                                                             "adamw_w_in", 184)
    g_w_in, d_w_in, nm_w_in, nv_w_in = (jnp.swapaxes(a, 0, 1) for a in (g_w_in_t, d_w_in_t, nm_w_in_t, nv_w_in_t))
    g_wpa, d_wpa, nm_wpa, nv_wpa = _adamw_joined(w_proj_a[0], halves[1], others[1], m_w_proj_a[0], v_w_proj_a[0], cidx, 0,
                                                 "adamw_wpa", 0)
    g_wpb, d_wpb, nm_wpb, nv_wpb = _adamw_joined(w_proj_b[0], halves[2], others[2], m_w_proj_b[0], v_w_proj_b[0], cidx, 0,
                                                 "adamw_wpb", 0)
    g_wo, d_wo, nm_wo, nv_wo = _adamw_joined(w_out[0], halves[3], others[3], m_w_out[0], v_w_out[0], cidx, 0, "adamw_wo", 0)
    d_w_mod, nm_w_mod, nv_w_mod = _adamw(w_mod[0], g_w_mod, m_w_mod[0], v_w_mod[0], "adamw_w_mod", 256)

    names = ["c_ctx", "b_mod", "norm_g", "a_ln_g", "a_ln_b", "a_ws", "a_bs", "b_gate_w2", "b_gate_b", "b_norm_g", "final_norm_g"]
    ws_ = [c_ctx, b_mod, norm_g, a_ln_g, a_ln_b, a_ws, a_bs, b_gate_w2, b_gate_b, b_norm_g, final_norm_g]
    gs_ = [g_c_ctx, g_b_mod, s_dng, s_dlng, s_dlnb, s_dws, s_dbs, g_w2, g_gb2, s_dgbn, s_dgf]
    ms_ = [m_c_ctx, m_b_mod, m_norm_g, m_a_ln_g, m_a_ln_b, m_a_ws, m_a_bs, m_b_gate_w2, m_b_gate_b, m_b_norm_g, m_final_norm_g]
    vs_ = [v_c_ctx, v_b_mod, v_norm_g, v_a_ln_g, v_a_ln_b, v_a_ws, v_a_bs, v_b_gate_w2, v_b_gate_b, v_b_norm_g, v_final_norm_g]
    shapes = [w.shape for w in ws_]
    flat2 = [(1, 1024), (1, 3072), (1, 1024), (1, 512), (1, 512), (512, 128), (4, 128), (32, 64), (2, 64), (1, 512), (1, 1024)]
    as2d = lambda arrs: [a.reshape(s) for a, s in zip(arrs, flat2)]
    d_s, nm_s, nv_s = _adamw_many(as2d(ws_), as2d(gs_), as2d(ms_), as2d(vs_), "adamw_small")
    d_small = {n: a.reshape(s) for n, a, s in zip(names, d_s, shapes)}
    nm_small = {n: a.reshape(s) for n, a, s in zip(names, nm_s, shapes)}
    nv_small = {n: a.reshape(s) for n, a, s in zip(names, nv_s, shapes)}
    g_small = {n: g.reshape(s) for n, g, s in zip(names, gs_, shapes)}

    order = ["c_ctx", "w_mod", "b_mod", "norm_g", "w_in", "a_ln_g", "a_ln_b", "a_ws", "a_bs", "b_gate_w2", "b_gate_b", "b_norm_g",
             "w_proj_a", "w_proj_b", "w_out", "final_norm_g"]
    big_g = dict(w_mod=g_w_mod[None], w_in=g_w_in[None], w_proj_a=g_wpa[None], w_proj_b=g_wpb[None], w_out=g_wo[None])
    big_d = dict(w_mod=d_w_mod[None], w_in=d_w_in[None], w_proj_a=d_wpa[None], w_proj_b=d_wpb[None], w_out=d_wo[None])
    big_m = dict(w_mod=nm_w_mod[None], w_in=nm_w_in[None], w_proj_a=nm_wpa[None], w_proj_b=nm_wpb[None], w_out=nm_wo[None])
    big_v = dict(w_mod=nv_w_mod[None], w_in=nv_w_in[None], w_proj_a=nv_wpa[None], w_proj_b=nv_wpb[None], w_out=nv_wo[None])
    grads = [big_g[n] if n in big_g else g_small[n] for n in order]
    deltas = [big_d[n] if n in big_d else d_small[n] for n in order]
    new_m = [big_m[n] if n in big_m else nm_small[n] for n in order]
    new_v = [big_v[n] if n in big_v else nv_small[n] for n in order]
    return (loss, r["dx"][None], *grads, *deltas, *new_m, *new_v)
```

```python
import functools

import jax
import jax.numpy as jnp
from jax import lax
from jax.experimental import pallas as pl
from jax.experimental.pallas import tpu as pltpu

F32 = jnp.float32
BF16 = jnp.bfloat16
SDS = jax.ShapeDtypeStruct

D = 1024
NP = 5120
LRW = 128
CH = 64
AC = 128
EPS = 1e-6
TOK = 256
GLA_TB = 1024
VMEM_BIG = 48 * 1024 * 1024

ADAM_LR, ADAM_B1, ADAM_B2, ADAM_EPS, ADAM_WD, ADAM_STEP = 0.001, 0.9, 0.999, 1e-08, 0.01, 10

_pcall = pl.pallas_call
MESH = pl.DeviceIdType.MESH


def _cp(sem=None, vmem=None):
    kw = {}
    if sem is not None:
        kw["dimension_semantics"] = sem
    if vmem is not None:
        kw["vmem_limit_bytes"] = vmem
    return pltpu.CompilerParams(**kw)


def _silu(x):
    return x * jax.nn.sigmoid(x)


def _dsilu(x):
    s = jax.nn.sigmoid(x)
    return s * (1.0 + x * (1.0 - s))


def _logsig(x):
    return jnp.minimum(x, 0.0) - jnp.log1p(jnp.exp(-jnp.abs(x)))


def _nt(a, b):
    return lax.dot_general(a, b, (((1,), (1,)), ((), ())), preferred_element_type=F32)


def _tn(a, b):
    return lax.dot_general(a, b, (((0,), (0,)), ((), ())), preferred_element_type=F32)


def _nn(a, b):
    return jnp.dot(a, b, preferred_element_type=F32)


def _full(shape):
    return pl.BlockSpec(shape, lambda *_: (0,) * len(shape))


def _mm(a, b, *, tm, tn, tk, out_dtype, name, acc=None, n_outer=False, b_t=False):
    m, k = a.shape
    n, k2 = (b.shape if b_t else b.shape[::-1])
    assert k == k2 and m % tm == 0 and n % tn == 0 and k % tk == 0, (a.shape, b.shape, tm, tn, tk)
    nk = k // tk
    has_acc = acc is not None

    def body(*refs):
        if has_acc:
            a_ref, b_ref, c_ref, o_ref = refs[:4]
        else:
            a_ref, b_ref, o_ref = refs[:3]
        part = (_nt if b_t else _nn)(a_ref[...].astype(BF16), b_ref[...].astype(BF16))
        if nk == 1:
            o_ref[...] = ((c_ref[...] + part) if has_acc else part).astype(out_dtype)
            return
        acc_ref = refs[-1]
        kk = pl.program_id(2)

        @pl.when(kk == 0)
        def _():
            if has_acc:
                acc_ref[...] = c_ref[...] + part
            else:
                acc_ref[...] = part

        @pl.when(kk > 0)
        def _():
            acc_ref[...] += part

        @pl.when(kk == nk - 1)
        def _():
            o_ref[...] = acc_ref[...].astype(out_dtype)

    if n_outer:
        ij = lambda g0, g1: (g1, g0)
        grid = (n // tn, m // tm, nk)
    else:
        ij = lambda g0, g1: (g0, g1)
        grid = (m // tm, n // tn, nk)
    b_spec = (pl.BlockSpec((tn, tk), lambda g0, g1, kk: (ij(g0, g1)[1], kk)) if b_t
              else pl.BlockSpec((tk, tn), lambda g0, g1, kk: (kk, ij(g0, g1)[1])))
    in_specs = [pl.BlockSpec((tm, tk), lambda g0, g1, kk: (ij(g0, g1)[0], kk)), b_spec]
    args = [a, b]
    if has_acc:
        in_specs.append(pl.BlockSpec((tm, tn), lambda g0, g1, kk: ij(g0, g1)))
        args.append(acc)
    return _pcall(
        body, name=name, grid=grid, in_specs=in_specs,
        out_specs=pl.BlockSpec((tm, tn), lambda g0, g1, kk: ij(g0, g1)),
        out_shape=SDS((m, n), out_dtype), scratch_shapes=([pltpu.VMEM((tm, tn), F32)] if nk > 1 else []),
        compiler_params=_cp(("parallel", "parallel", "arbitrary"), VMEM_BIG),
    )(*args)


def _mm_tn(a, b, *, ta, tn, tk, name, acc=None):
    m, ka = a.shape
    m2, n = b.shape
    assert m == m2 and ka % ta == 0 and n % tn == 0 and m % tk == 0, (a.shape, b.shape, ta, tn, tk)
    nk = m // tk
    has_acc = acc is not None

    def body(*refs):
        if has_acc:
            a_ref, b_ref, c_ref, o_ref = refs
        else:
            a_ref, b_ref, o_ref = refs
        kk = pl.program_id(2)
        part = _tn(a_ref[...].astype(BF16), b_ref[...].astype(BF16))

        @pl.when(kk == 0)
        def _():
            if has_acc:
                o_ref[...] = c_ref[...] + part
            else:
                o_ref[...] = part

        @pl.when(kk > 0)
        def _():
            o_ref[...] += part

    in_specs = [pl.BlockSpec((tk, ta), lambda i, j, kk: (kk, i)), pl.BlockSpec((tk, tn), lambda i, j, kk: (kk, j))]
    args = [a, b]
    if has_acc:
        in_specs.append(pl.BlockSpec((ta, tn), lambda i, j, kk: (i, j)))
        args.append(acc)
    return _pcall(
        body, name=name, grid=(ka // ta, n // tn, nk), in_specs=in_specs,
        out_specs=pl.BlockSpec((ta, tn), lambda i, j, kk: (i, j)), out_shape=SDS((ka, n), F32),
        compiler_params=_cp(("parallel", "parallel", "arbitrary"), VMEM_BIG),
    )(*args)


def _modvec(cc, wm, bm):
    def body(c_ref, w_ref, b_ref, o_ref):
        o_ref[...] = _nn(_silu(c_ref[...]).astype(BF16), w_ref[...]) + b_ref[...]

    return _pcall(body, name="modvec", out_shape=SDS((8, 3 * D), F32), compiler_params=_cp(None, VMEM_BIG))(cc, wm, bm)


def _dcctx(dmodc, wm):
    def body(d_ref, w_ref, o_ref):
        o_ref[...] = _nt(d_ref[...].astype(BF16), w_ref[...])

    return _pcall(
        body, name="dcctx", grid=(1,), in_specs=[_full((8, 2 * D)), pl.BlockSpec((D, 2 * D), lambda i: (0, 0))],
        out_specs=_full((8, D)), out_shape=SDS((8, D), F32), compiler_params=_cp(("arbitrary",), VMEM_BIG),
    )(dmodc, wm)


def _prep_h(x, ng, scale, shift, name):
    m = x.shape[0]

    def body(x_ref, g_ref, sc_ref, sh_ref, h_ref):
        xf = x_ref[...]
        r = lax.rsqrt(jnp.mean(xf * xf, axis=-1, keepdims=True) + EPS)
        y = (xf * r) * g_ref[...]
        h_ref[...] = (y * (1.0 + sc_ref[...]) + sh_ref[...]).astype(BF16)

    row = pl.BlockSpec((TOK, D), lambda i: (i, 0))
    return _pcall(
        body, name=name, grid=(m // TOK,), in_specs=[row, _full((1, D)), _full((1, D)), _full((1, D))],
        out_specs=row, out_shape=SDS((m, D), BF16), compiler_params=_cp(("parallel",)),
    )(x, ng, scale, shift)


def _resident(shape):
    return pl.BlockSpec(shape, lambda *_: (0,) * len(shape), pipeline_mode=pl.Buffered(1))


PROJ_TM = 512


def _proj_fwd(x, ng, scale, shift, wit_g, wit_r, wlrt, share=()):
    m = x.shape[0]
    ns = len(share)
    steps = m // PROJ_TM
    src = [(0, 0), (0, D), (1, 2 * D), (1, 0), (1, D)]

    def body(*refs):
        x_ref, g_ref, sc_ref, sh_ref, wg_ref, wr_ref, wl_ref = refs[:7]
        share_refs = refs[7:7 + ns]
        h_ref, p_ref, plr_ref = refs[7 + ns:10 + ns]
        got_refs = refs[10 + ns:10 + 2 * ns]
        sems = refs[10 + 2 * ns:]

        def copies():
            cx, cy, cc = _coords()
            me = 2 * cx + cy
            peers = [(1 - cx, cy), (cx, 1 - cy), (1 - cx, 1 - cy)]
            out, back = [], []
            for k in range(ns):
                for j, (px, py) in enumerate(peers):
                    out.append(_remote(share_refs[k], got_refs[k].at[me], sems[0].at[3 * k + j], sems[1].at[3 * k + j], (px, py, cc)))
                    landed = got_refs[k].at[2 * px + py]
                    back.append(_remote(landed, landed, sems[0].at[3 * k + j], sems[1].at[3 * k + j], (px, py, cc)))
            return out, back

        if ns:
            @pl.when(pl.program_id(0) == 0)
            def _():
                for rc in copies()[0]:
                    rc.start()

            @pl.when(pl.program_id(0) == steps - 1)
            def _():
                out, back = copies()
                for rc in back:
                    rc.wait_recv()
                for rc in out:
                    rc.wait_send()

        xf = x_ref[...]
        r = lax.rsqrt(jnp.mean(xf * xf, axis=-1, keepdims=True) + EPS)
        y = (xf * r) * g_ref[...]
        h = (y * (1.0 + sc_ref[...]) + sh_ref[...]).astype(BF16)
        h_ref[...] = h
        for j, (which, r0) in enumerate(src):
            w_ref = wr_ref if which else wg_ref
            p_ref[:, D * j:D * j + D] = _nt(h, w_ref[r0:r0 + D, :]).astype(BF16)
        plr_ref[...] = _nt(h, wl_ref[...])

    row = pl.BlockSpec((PROJ_TM, D), lambda i: (i, 0))
    vec = _full((1, D))
    res = _pcall(
        body, name="proj_fwd", grid=(steps,),
        in_specs=[row, vec, vec, vec, _resident((2 * D, D)), _resident((3 * D, D)), _resident((LRW, D))] + [ANY] * ns,
        out_specs=[row, pl.BlockSpec((PROJ_TM, NP), lambda i: (i, 0)), pl.BlockSpec((PROJ_TM, LRW), lambda i: (i, 0))] + [ANY] * ns,
        out_shape=[SDS((m, D), BF16), SDS((m, NP), BF16), SDS((m, LRW), F32)] + [SDS((4,) + a.shape, a.dtype) for a in share],
        scratch_shapes=([pltpu.SemaphoreType.DMA((3 * ns,)), pltpu.SemaphoreType.DMA((3 * ns,))] if ns else []),
        compiler_params=_cp(("arbitrary",), VMEM_BIG),
    )(x, ng, scale, shift, wit_g, wit_r, wlrt, *share)
    return res[0], res[1], res[2], list(res[3:])


def _proj_bwd(dp_g, dp_r, dlr, wit_g, wit_r, wlrt, x, dx1, ng, scale, send=()):
    m = x.shape[0]
    ns = len(send)
    steps = m // PROJ_TM

    def body(*refs):
        (dpg_ref, dpr_ref, dlr_ref, wg_ref, wr_ref, wl_ref, x_ref, r_ref, g_ref, sc_ref) = refs[:10]
        send_refs = refs[10:10 + ns]
        dx_ref, dg_ref, dsc_ref, dsh_ref = refs[10 + ns:14 + ns]
        got_refs = refs[14 + ns:14 + 2 * ns]
        sems = refs[14 + 2 * ns:]
        i = pl.program_id(0)

        def copies():
            cx, cy, cc = _coords()
            me = 2 * cx + cy
            peers = [(1 - cx, cy), (cx, 1 - cy), (1 - cx, 1 - cy)]
            out, back = [], []
            for k in range(ns):
                for j, (px, py) in enumerate(peers):
                    out.append(_remote(send_refs[k].at[2 * px + py], got_refs[k].at[me], sems[0].at[3 * k + j],
                                       sems[1].at[3 * k + j], (px, py, cc)))
                    landed = got_refs[k].at[2 * px + py]
                    back.append(_remote(landed, landed, sems[0].at[3 * k + j], sems[1].at[3 * k + j], (px, py, cc)))
            return out, back

        @pl.when(i == 0)
        def _():
            dg_ref[...] = jnp.zeros_like(dg_ref)
            dsc_ref[...] = jnp.zeros_like(dsc_ref)
            dsh_ref[...] = jnp.zeros_like(dsh_ref)
            if ns:
                for rc in copies()[0]:
                    rc.start()

        dh_ = (_nn(dpg_ref[...], wg_ref[...]) + _nn(dpr_ref[...], wr_ref[...])
               + _nn(dlr_ref[...].astype(BF16), wl_ref[...]))
        xf = x_ref[...]
        r = lax.rsqrt(jnp.mean(xf * xf, axis=-1, keepdims=True) + EPS)
        xh = xf * r
        y = xh * g_ref[...]
        dsh_ref[...] += jnp.sum(dh_, axis=0, keepdims=True)
        dsc_ref[...] += jnp.sum(dh_ * y, axis=0, keepdims=True)
        dy = dh_ * (1.0 + sc_ref[...])
        dg_ref[...] += jnp.sum(dy * xh, axis=0, keepdims=True)
        dxh = dy * g_ref[...]
        dx_ref[...] = r * (dxh - xh * jnp.mean(dxh * xh, axis=-1, keepdims=True)) + r_ref[...]

        if ns:
            @pl.when(i == steps - 1)
            def _():
                out, back = copies()
                for rc in back:
                    rc.wait_recv()
                for rc in out:
                    rc.wait_send()

    row = pl.BlockSpec((PROJ_TM, D), lambda i: (i, 0))
    vec = _full((1, D))
    kg, kr = dp_g.shape[1], dp_r.shape[1]
    res = _pcall(
        body, name="proj_bwd", grid=(steps,),
        in_specs=[pl.BlockSpec((PROJ_TM, kg), lambda i: (i, 0)), pl.BlockSpec((PROJ_TM, kr), lambda i: (i, 0)),
                  pl.BlockSpec((PROJ_TM, LRW), lambda i: (i, 0)), _resident((kg, D)), _resident((kr, D)), _resident((LRW, D)),
                  row, row, vec, vec] + [ANY] * ns,
        out_specs=[row, vec, vec, vec] + [ANY] * ns,
        out_shape=[SDS((m, D), F32), SDS((1, D), F32), SDS((1, D), F32), SDS((1, D), F32)] + [SDS(a.shape, a.dtype) for a in send],
        scratch_shapes=([pltpu.SemaphoreType.DMA((3 * ns,)), pltpu.SemaphoreType.DMA((3 * ns,))] if ns else []),
        compiler_params=_cp(("arbitrary",), VMEM_BIG),
    )(dp_g, dp_r, dlr, wit_g, wit_r, wlrt, x, dx1, ng, scale, *send)
    return tuple(res[:4]), list(res[4:])


def _prep_bwd(x, dh, dx1, ng, scale, name):
    m = x.shape[0]
    has_res = dx1 is not None

    def body(*refs):
        if has_res:
            x_ref, dh_ref, r_ref, g_ref, sc_ref, dx_ref, dg_ref, dsc_ref, dsh_ref = refs
        else:
            x_ref, dh_ref, g_ref, sc_ref, dx_ref, dg_ref, dsc_ref, dsh_ref = refs
        i = pl.program_id(0)

        @pl.when(i == 0)
        def _():
            dg_ref[...] = jnp.zeros_like(dg_ref)
            dsc_ref[...] = jnp.zeros_like(dsc_ref)
            dsh_ref[...] = jnp.zeros_like(dsh_ref)

        xf = x_ref[...]
        dh_ = dh_ref[...]
        r = lax.rsqrt(jnp.mean(xf * xf, axis=-1, keepdims=True) + EPS)
        xh = xf * r
        y = xh * g_ref[...]
        dsh_ref[...] += jnp.sum(dh_, axis=0, keepdims=True)
        dsc_ref[...] += jnp.sum(dh_ * y, axis=0, keepdims=True)
        dy = dh_ * (1.0 + sc_ref[...])
        dg_ref[...] += jnp.sum(dy * xh, axis=0, keepdims=True)
        dxh = dy * g_ref[...]
        dx = r * (dxh - xh * jnp.mean(dxh * xh, axis=-1, keepdims=True))
        if has_res:
            dx = dx + r_ref[...]
        dx_ref[...] = dx

    row = pl.BlockSpec((TOK, D), lambda i: (i, 0))
    vec = _full((1, D))
    in_specs = [row, row] + ([row] if has_res else []) + [vec, vec]
    args = [x, dh] + ([dx1] if has_res else []) + [ng, scale]
    return _pcall(
        body, name=name, grid=(m // TOK,), in_specs=in_specs, out_specs=[row, vec, vec, vec],
        out_shape=[SDS((m, D), F32), SDS((1, D), F32), SDS((1, D), F32), SDS((1, D), F32)],
        compiler_params=_cp(("arbitrary",)),
    )(*args)


def _ln_fwd(p, ln_g, ln_b):
    m = p.shape[0]

    def body(va_ref, g_ref, b_ref, vr_ref, vc_ref):
        xf = va_ref[...].astype(F32)
        xc = xf - jnp.mean(xf, axis=-1, keepdims=True)
        y = xc * lax.rsqrt(jnp.mean(xc * xc, axis=-1, keepdims=True) + EPS)
        vn = y * g_ref[...] + b_ref[...]
        vr_ref[...] = vn[:, 0:256].astype(BF16)
        vc_ref[0] = vn[:, 256:384].astype(BF16)
        vc_ref[1] = vn[:, 384:512].astype(BF16)

    return _pcall(
        body, name="ln_fwd", grid=(m // TOK,),
        in_specs=[pl.BlockSpec((TOK, 512), lambda i: (i, 9)), _full((1, 512)), _full((1, 512))],
        out_specs=[pl.BlockSpec((TOK, 256), lambda i: (i, 0)), pl.BlockSpec((2, TOK, 128), lambda i: (0, i, 0))],
        out_shape=[SDS((m, 256), BF16), SDS((2, m, 128), BF16)], compiler_params=_cp(("parallel",)),
    )(p, ln_g, ln_b)


COLB = 2048


def _colmix_fwd(vnc, ws23, bs23):
    rows = vnc.shape[2] // COLB

    def body(v_ref, w_ref, b_ref, o_ref):
        o_ref[0] = _nn(w_ref[0], v_ref[0]) + b_ref[0]

    return _pcall(
        body, name="colmix_fwd", grid=(2, rows),
        in_specs=[pl.BlockSpec((1, AC, COLB), lambda g, j: (g, 0, j)), pl.BlockSpec((1, AC, AC), lambda g, j: (g, 0, 0)),
                  pl.BlockSpec((1, AC, 1), lambda g, j: (g, 0, 0))],
        out_specs=pl.BlockSpec((1, AC, COLB), lambda g, j: (g, 0, j)),
        out_shape=SDS(vnc.shape, F32), compiler_params=_cp(("parallel", "parallel")),
    )(vnc, ws23, bs23)


def _colmix_bwd(dsvc, vnc, ws23t):
    rows = vnc.shape[2] // COLB

    def body(d_ref, v_ref, wt_ref, dv_ref, dw_ref, db_ref):
        j = pl.program_id(1)

        @pl.when(j == 0)
        def _():
            dw_ref[...] = jnp.zeros_like(dw_ref)
            db_ref[...] = jnp.zeros_like(db_ref)

        d = d_ref[0]
        d16 = d.astype(BF16)
        dv_ref[0] = _nn(wt_ref[0], d16)
        dw_ref[0] += _nt(d16, v_ref[0])
        db_ref[0] += jnp.sum(d, axis=1, keepdims=True)

    blk = pl.BlockSpec((1, AC, COLB), lambda g, j: (g, 0, j))
    return _pcall(
        body, name="colmix_bwd", grid=(2, rows),
        in_specs=[blk, blk, pl.BlockSpec((1, AC, AC), lambda g, j: (g, 0, 0))],
        out_specs=[blk, pl.BlockSpec((1, AC, AC), lambda g, j: (g, 0, 0)), pl.BlockSpec((1, AC, 1), lambda g, j: (g, 0, 0))],
        out_shape=[SDS(vnc.shape, F32), SDS((2, AC, AC), F32), SDS((2, AC, 1), F32)],
        compiler_params=_cp(("parallel", "arbitrary")),
    )(dsvc, vnc, ws23t)


def _head_norm(o, gbn):
    out = []
    for h in range(4):
        oh = o[:, 128 * h:128 * h + 128]
        r = lax.rsqrt(jnp.mean(oh * oh, axis=-1, keepdims=True) + EPS)
        out.append((r, oh * r))
    return out


def _mid_fwd(o_f, o_b, p, vnr, svc, ws01, bs01, gbn):
    m = p.shape[0]

    def body(of_ref, ob_ref, zb_ref, ua_ref, za_ref, vnr_ref, svc_ref, w_ref, b_ref, g_ref, ya_ref, yb_ref, svr_ref):
        o = of_ref[...] + ob_ref[...]
        zb = zb_ref[...]
        parts = []
        for h, (r, xh) in enumerate(_head_norm(o, None)):
            parts.append(xh * g_ref[:, 128 * h:128 * h + 128])
        on = jnp.concatenate(parts, axis=1)
        yb_ref[...] = (on * _silu(zb)).astype(BF16)
        for j in range(TOK // AC):
            for g in range(2):
                sv = _nn(w_ref[g], vnr_ref[AC * j:AC * j + AC, AC * g:AC * g + AC]) + b_ref[g]
                svr_ref[AC * j:AC * j + AC, AC * g:AC * g + AC] = sv
        sz = _silu(za_ref[...])
        u = ua_ref[...]
        ya_ref[:, 0:256] = ((u[:, 0:256] * svr_ref[...]) * sz[:, 0:256]).astype(BF16)
        ya_ref[:, 256:384] = ((u[:, 256:384] * svc_ref[0]) * sz[:, 256:384]).astype(BF16)
        ya_ref[:, 384:512] = ((u[:, 384:512] * svc_ref[1]) * sz[:, 384:512]).astype(BF16)

    r512 = pl.BlockSpec((TOK, 512), lambda i: (i, 0))
    return _pcall(
        body, name="mid_fwd", grid=(m // TOK,),
        in_specs=[r512, r512, pl.BlockSpec((TOK, 512), lambda i: (i, 6)), pl.BlockSpec((TOK, 512), lambda i: (i, 7)),
                  pl.BlockSpec((TOK, 512), lambda i: (i, 8)), pl.BlockSpec((TOK, 256), lambda i: (i, 0)),
                  pl.BlockSpec((2, TOK, 128), lambda i: (0, i, 0)), _full((2, AC, AC)), _full((2, AC, 1)), _full((1, 512))],
        out_specs=[r512, r512, pl.BlockSpec((TOK, 256), lambda i: (i, 0))],
        out_shape=[SDS((m, 512), BF16), SDS((m, 512), BF16), SDS((m, 256), F32)],
        compiler_params=_cp(("parallel",)),
    )(o_f, o_b, p, p, p, vnr, svc, ws01, bs01, gbn)


def _merge_fwd(p, ya, yb):
    m = p.shape[0]

    def body(ga_ref, gb_ref, ya_ref, yb_ref, m_ref):
        m_ref[...] = (jax.nn.sigmoid(ga_ref[...]) * ya_ref[...] + jax.nn.sigmoid(gb_ref[...]) * yb_ref[...]).astype(BF16)

    row = pl.BlockSpec((TOK, D), lambda i: (i, 0))
    return _pcall(
        body, name="merge_fwd", grid=(m // TOK,),
        in_specs=[row, pl.BlockSpec((TOK, D), lambda i: (i, 1)), row, row], out_specs=row,
        out_shape=SDS((m, D), BF16), compiler_params=_cp(("parallel",)),
    )(p, p, ya, yb)


def _loss_head(x, out, tgt, gate, gf):
    m = x.shape[0]

    def body(x_ref, o_ref, t_ref, gate_ref, gf_ref, dx1_ref, dout_ref, loss_ref, dgate_ref, dgf_ref):
        i = pl.program_id(0)

        @pl.when(i == 0)
        def _():
            loss_ref[...] = jnp.zeros_like(loss_ref)
            dgate_ref[...] = jnp.zeros_like(dgate_ref)
            dgf_ref[...] = jnp.zeros_like(dgf_ref)

        out_ = o_ref[...]
        x1 = x_ref[...] + gate_ref[...] * out_
        r = lax.rsqrt(jnp.mean(x1 * x1, axis=-1, keepdims=True) + EPS)
        xh = x1 * r
        err = xh * gf_ref[...] - t_ref[...]
        loss_ref[...] += 0.5 * jnp.sum(jnp.mean(err * err, axis=-1, keepdims=True), axis=0, keepdims=True)
        dy = err * (1.0 / D)
        dgf_ref[...] += jnp.sum(dy * xh, axis=0, keepdims=True)
        dxh = dy * gf_ref[...]
        dx1 = r * (dxh - xh * jnp.mean(dxh * xh, axis=-1, keepdims=True))
        dx1_ref[...] = dx1
        dout_ref[...] = (gate_ref[...] * dx1).astype(BF16)
        dgate_ref[...] += jnp.sum(dx1 * out_, axis=0, keepdims=True)

    row = pl.BlockSpec((TOK, D), lambda i: (i, 0))
    vec = _full((1, D))
    return _pcall(
        body, name="loss_head", grid=(m // TOK,), in_specs=[row, row, row, vec, vec],
        out_specs=[row, row, _full((1, 128)), vec, vec],
        out_shape=[SDS((m, D), F32), SDS((m, D), BF16), SDS((1, 128), F32), SDS((1, D), F32), SDS((1, D), F32)],
        compiler_params=_cp(("arbitrary",)),
    )(x, out, tgt, gate, gf)


def _merge_bwd(dm, ya, yb, p):
    m = p.shape[0]

    def body(dm_ref, ya_ref, yb_ref, ga_ref, gb_ref, dya_ref, dyb_ref, dp_ref):
        dm_ = dm_ref[...]
        sa = jax.nn.sigmoid(ga_ref[...])
        sb = jax.nn.sigmoid(gb_ref[...])
        dya_ref[...] = (dm_ * sa).astype(BF16)
        dyb_ref[...] = (dm_ * sb).astype(BF16)
        dp_ref[:, 0:D] = (dm_ * ya_ref[...] * (sa * (1.0 - sa))).astype(BF16)
        dp_ref[:, D:2 * D] = (dm_ * yb_ref[...] * (sb * (1.0 - sb))).astype(BF16)

    row = pl.BlockSpec((TOK, D), lambda i: (i, 0))
    return _pcall(
        body, name="merge_bwd", grid=(m // TOK,),
        in_specs=[row, row, row, row, pl.BlockSpec((TOK, D), lambda i: (i, 1))],
        out_specs=[row, row, pl.BlockSpec((TOK, 2 * D), lambda i: (i, 0))],
        out_shape=[SDS((m, D), BF16), SDS((m, D), BF16), SDS((m, NP), BF16)],
        compiler_params=_cp(("parallel",)),
    )(dm, ya, yb, p, p)


def _mid_bwd(dya_in, dyb_in, p, svr, svc, o_f, o_b, gbn, dp):
    m = p.shape[0]

    def body(dya_ref, dyb_ref, zb_ref, ua_ref, za_ref, svr_ref, svc_ref, of_ref, ob_ref, g_ref, dpi_ref,
             dp_ref, dsr_ref, dsc_ref, do_ref, dg_ref):
        i = pl.program_id(0)

        @pl.when(i == 0)
        def _():
            dg_ref[...] = jnp.zeros_like(dg_ref)

        dya = dya_ref[...]
        u = ua_ref[...]
        za = za_ref[...]
        sz = _silu(za)
        sv = jnp.concatenate([svr_ref[...], svc_ref[0], svc_ref[1]], axis=1)
        dp_ref[:, 512:1024] = (dya * sv * sz).astype(BF16)
        dsv = dya * u * sz
        dsr_ref[...] = dsv[:, 0:256]
        dsc_ref[0] = dsv[:, 256:384]
        dsc_ref[1] = dsv[:, 384:512]
        dp_ref[:, 1024:1536] = (dya * u * sv * _dsilu(za)).astype(BF16)

        dyb = dyb_ref[...]
        zb = zb_ref[...]
        o = of_ref[...] + ob_ref[...]
        szb = _silu(zb)
        dszb = _dsilu(zb)
        for h, (r, xh) in enumerate(_head_norm(o, None)):
            sl = slice(128 * h, 128 * h + 128)
            gh = g_ref[:, sl]
            don = dyb[:, sl] * szb[:, sl]
            dp_ref[:, sl] = (dyb[:, sl] * (xh * gh) * dszb[:, sl]).astype(BF16)
            dg_ref[:, sl] += jnp.sum(don * xh, axis=0, keepdims=True)
            dxh = don * gh
            do_ref[:, sl] = r * (dxh - xh * jnp.mean(dxh * xh, axis=-1, keepdims=True))

    r512 = pl.BlockSpec((TOK, 512), lambda i: (i, 0))
    return _pcall(
        body, name="mid_bwd", grid=(m // TOK,),
        in_specs=[r512, r512, pl.BlockSpec((TOK, 512), lambda i: (i, 6)), pl.BlockSpec((TOK, 512), lambda i: (i, 7)),
                  pl.BlockSpec((TOK, 512), lambda i: (i, 8)), pl.BlockSpec((TOK, 256), lambda i: (i, 0)),
                  pl.BlockSpec((2, TOK, 128), lambda i: (0, i, 0)), r512, r512, _full((1, 512)),
                  pl.BlockSpec(memory_space=pl.ANY)],
        out_specs=[pl.BlockSpec((TOK, 1536), lambda i: (i, 2)), pl.BlockSpec((TOK, 256), lambda i: (i, 0)),
                   pl.BlockSpec((2, TOK, 128), lambda i: (0, i, 0)), r512, _full((1, 512))],
        out_shape=[SDS((m, NP), BF16), SDS((m, 256), F32), SDS((2, m, 128), F32), SDS((m, 512), F32), SDS((1, 512), F32)],
        input_output_aliases={10: 0}, compiler_params=_cp(("arbitrary",)),
    )(dya_in, dyb_in, p, p, p, svr, svc, o_f, o_b, gbn, dp)


def _tail_fwd(o_f, o_b, p, vnr, svc, x, tgt, ws01, bs01, gbn, wpa, wpb, wo, gate, gf):
    m = p.shape[0]

    def body(of_ref, ob_ref, zb_ref, ua_ref, za_ref, ga_ref, gb_ref, vnr_ref, svc_ref, x_ref, t_ref, w_ref, b_ref, g_ref,
             wpa_ref, wpb_ref, wo_ref, gate_ref, gf_ref,
             ya_ref, yb_ref, svr_ref, m_ref, dx1_ref, dout_ref, loss_ref, dgate_ref, dgf_ref):
        i = pl.program_id(0)

        @pl.when(i == 0)
        def _():
            loss_ref[...] = jnp.zeros_like(loss_ref)
            dgate_ref[...] = jnp.zeros_like(dgate_ref)
            dgf_ref[...] = jnp.zeros_like(dgf_ref)

        o = of_ref[...] + ob_ref[...]
        zb = zb_ref[...].astype(F32)
        for h, (r, xh) in enumerate(_head_norm(o, None)):
            sl = slice(128 * h, 128 * h + 128)
            yb_ref[:, sl] = ((xh * g_ref[:, sl]) * _silu(zb[:, sl])).astype(BF16)
        for j in range(TOK // AC):
            for g in range(2):
                sv = _nn(w_ref[g], vnr_ref[AC * j:AC * j + AC, AC * g:AC * g + AC]) + b_ref[g]
                svr_ref[AC * j:AC * j + AC, AC * g:AC * g + AC] = sv
        sz = _silu(za_ref[...].astype(F32))
        u = ua_ref[...].astype(F32)
        ya_ref[:, 0:256] = ((u[:, 0:256] * svr_ref[...]) * sz[:, 0:256]).astype(BF16)
        ya_ref[:, 256:384] = ((u[:, 256:384] * svc_ref[0]) * sz[:, 256:384]).astype(BF16)
        ya_ref[:, 384:512] = ((u[:, 384:512] * svc_ref[1]) * sz[:, 384:512]).astype(BF16)
        ya = _nn(ya_ref[...], wpa_ref[...])
        yb = _nn(yb_ref[...], wpb_ref[...])
        mg = (jax.nn.sigmoid(ga_ref[...].astype(F32)) * ya + jax.nn.sigmoid(gb_ref[...].astype(F32)) * yb).astype(BF16)
        m_ref[...] = mg
        out_ = _nn(mg, wo_ref[...])
        x1 = x_ref[...] + gate_ref[...] * out_
        r = lax.rsqrt(jnp.mean(x1 * x1, axis=-1, keepdims=True) + EPS)
        xh = x1 * r
        err = xh * gf_ref[...] - t_ref[...]
        loss_ref[...] += 0.5 * jnp.sum(jnp.mean(err * err, axis=-1, keepdims=True), axis=0, keepdims=True)
        dy = err * (1.0 / D)
        dgf_ref[...] += jnp.sum(dy * xh, axis=0, keepdims=True)
        dxh = dy * gf_ref[...]
        dx1 = r * (dxh - xh * jnp.mean(dxh * xh, axis=-1, keepdims=True))
        dx1_ref[...] = dx1
        dout_ref[...] = (gate_ref[...] * dx1).astype(BF16)
        dgate_ref[...] += jnp.sum(dx1 * out_, axis=0, keepdims=True)

    r512 = pl.BlockSpec((TOK, 512), lambda i: (i, 0))
    row = pl.BlockSpec((TOK, D), lambda i: (i, 0))
    vec = _full((1, D))
    return _pcall(
        body, name="tail_fwd", grid=(m // TOK,),
        in_specs=[r512, r512, pl.BlockSpec((TOK, 512), lambda i: (i, 6)), pl.BlockSpec((TOK, 512), lambda i: (i, 7)),
                  pl.BlockSpec((TOK, 512), lambda i: (i, 8)), row, pl.BlockSpec((TOK, D), lambda i: (i, 1)),
                  pl.BlockSpec((TOK, 256), lambda i: (i, 0)), pl.BlockSpec((2, TOK, 128), lambda i: (0, i, 0)), row, row,
                  _full((2, AC, AC)), _full((2, AC, 1)), _full((1, 512)), _resident((512, D)), _resident((512, D)),
                  _resident((D, D)), vec, vec],
        out_specs=[r512, r512, pl.BlockSpec((TOK, 256), lambda i: (i, 0)), row, row, row, _full((1, 128)), vec, vec],
        out_shape=[SDS((m, 512), BF16), SDS((m, 512), BF16), SDS((m, 256), F32), SDS((m, D), BF16), SDS((m, D), F32),
                   SDS((m, D), BF16), SDS((1, 128), F32), SDS((1, D), F32), SDS((1, D), F32)],
        compiler_params=_cp(("arbitrary",), VMEM_BIG),
    )(o_f, o_b, p, p, p, p, p, vnr, svc, x, tgt, ws01, bs01, gbn, wpa, wpb, wo, gate, gf)


DPR = 3072


def _tail_bwd(dout, ya_in, yb_in, p, svr, svc, o_f, o_b, gbn, wo, wpa, wpb):
    m = p.shape[0]

    def body(dout_ref, ya_ref, yb_ref, ga_ref, gb_ref, zb_ref, ua_ref, za_ref, svr_ref, svc_ref, of_ref, ob_ref, g_ref,
             wo_ref, wpa_ref, wpb_ref,
             dya_ref, dyb_ref, dpg_ref, dpr_ref, dsr_ref, dsc_ref, do_ref, dg_ref):
        i = pl.program_id(0)

        @pl.when(i == 0)
        def _():
            dg_ref[...] = jnp.zeros_like(dg_ref)

        dm_ = _nt(dout_ref[...], wo_ref[...])
        ya = _nn(ya_ref[...], wpa_ref[...])
        yb = _nn(yb_ref[...], wpb_ref[...])
        sa = jax.nn.sigmoid(ga_ref[...].astype(F32))
        sb = jax.nn.sigmoid(gb_ref[...].astype(F32))
        dya16 = (dm_ * sa).astype(BF16)
        dyb16 = (dm_ * sb).astype(BF16)
        dya_ref[...] = dya16
        dyb_ref[...] = dyb16
        dpg_ref[:, 0:D] = (dm_ * ya * (sa * (1.0 - sa))).astype(BF16)
        dpg_ref[:, D:2 * D] = (dm_ * yb * (sb * (1.0 - sb))).astype(BF16)
        dya = _nt(dya16, wpa_ref[...])
        dyb = _nt(dyb16, wpb_ref[...])

        u = ua_ref[...].astype(F32)
        za = za_ref[...].astype(F32)
        sz = _silu(za)
        sv = jnp.concatenate([svr_ref[...], svc_ref[0], svc_ref[1]], axis=1)
        dpr_ref[:, 512:1024] = (dya * sv * sz).astype(BF16)
        dsv = dya * u * sz
        dsr_ref[...] = dsv[:, 0:256]
        dsc_ref[0] = dsv[:, 256:384]
        dsc_ref[1] = dsv[:, 384:512]
        dpr_ref[:, 1024:1536] = (dya * u * sv * _dsilu(za)).astype(BF16)

        zb = zb_ref[...].astype(F32)
        o = of_ref[...] + ob_ref[...]
        szb = _silu(zb)
        dszb = _dsilu(zb)
        for h, (r, xh) in enumerate(_head_norm(o, None)):
            sl = slice(128 * h, 128 * h + 128)
            gh = g_ref[:, sl]
            don = dyb[:, sl] * szb[:, sl]
            dpr_ref[:, sl] = (dyb[:, sl] * (xh * gh) * dszb[:, sl]).astype(BF16)
            dg_ref[:, sl] += jnp.sum(don * xh, axis=0, keepdims=True)
            dxh = don * gh
            do_ref[:, sl] = r * (dxh - xh * jnp.mean(dxh * xh, axis=-1, keepdims=True))

    r512 = pl.BlockSpec((TOK, 512), lambda i: (i, 0))
    row = pl.BlockSpec((TOK, D), lambda i: (i, 0))
    return _pcall(
        body, name="tail_bwd", grid=(m // TOK,),
        in_specs=[row, r512, r512, row, pl.BlockSpec((TOK, D), lambda i: (i, 1)), pl.BlockSpec((TOK, 512), lambda i: (i, 6)),
                  pl.BlockSpec((TOK, 512), lambda i: (i, 7)), pl.BlockSpec((TOK, 512), lambda i: (i, 8)),
                  pl.BlockSpec((TOK, 256), lambda i: (i, 0)), pl.BlockSpec((2, TOK, 128), lambda i: (0, i, 0)), r512, r512,
                  _full((1, 512)), _resident((D, D)), _resident((512, D)), _resident((512, D))],
        out_specs=[row, row, pl.BlockSpec((TOK, 2 * D), lambda i: (i, 0)), pl.BlockSpec((TOK, 1536), lambda i: (i, 0)),
                   pl.BlockSpec((TOK, 256), lambda i: (i, 0)), pl.BlockSpec((2, TOK, 128), lambda i: (0, i, 0)), r512, _full((1, 512))],
        out_shape=[SDS((m, D), BF16), SDS((m, D), BF16), SDS((m, 2 * D), BF16), SDS((m, DPR), BF16), SDS((m, 256), F32),
                   SDS((2, m, 128), F32), SDS((m, 512), F32), SDS((1, 512), F32)],
        compiler_params=_cp(("arbitrary",), VMEM_BIG),
    )(dout, ya_in, yb_in, p, p, p, p, p, svr, svc, o_f, o_b, gbn, wo, wpa, wpb)


def _mm_multi(pairs, *, tm, tn, out_dtype, name):
    m = pairs[0][0].shape[0]
    n = pairs[0][1].shape[1]
    nks = [a.shape[1] // tk for a, _, tk in pairs]
    starts = [sum(nks[:i]) for i in range(len(pairs))]
    total = sum(nks)

    def body(*refs):
        o_ref, acc_ref = refs[-2], refs[-1]
        kk = pl.program_id(2)
        for idx in range(len(pairs)):
            a_ref, b_ref = refs[2 * idx], refs[2 * idx + 1]

            @pl.when((kk >= starts[idx]) & (kk < starts[idx] + nks[idx]))
            def _(a_ref=a_ref, b_ref=b_ref, first=(idx == 0)):
                part = _nn(a_ref[...].astype(BF16), b_ref[...].astype(BF16))
                if first:
                    @pl.when(kk == 0)
                    def _():
                        acc_ref[...] = part

                    @pl.when(kk > 0)
                    def _():
                        acc_ref[...] += part
                else:
                    acc_ref[...] += part

        @pl.when(kk == total - 1)
        def _():
            o_ref[...] = acc_ref[...].astype(out_dtype)

    in_specs, args = [], []
    for (a, b, tk), st, nk in zip(pairs, starts, nks):
        in_specs.append(pl.BlockSpec((tm, tk), lambda i, j, kk, st=st, nk=nk: (i, jnp.clip(kk - st, 0, nk - 1))))
        in_specs.append(pl.BlockSpec((tk, tn), lambda i, j, kk, st=st, nk=nk: (jnp.clip(kk - st, 0, nk - 1), j)))
        args += [a, b]
    return _pcall(
        body, name=name, grid=(m // tm, n // tn, total), in_specs=in_specs,
        out_specs=pl.BlockSpec((tm, tn), lambda i, j, kk: (i, j)), out_shape=SDS((m, n), out_dtype),
        scratch_shapes=[pltpu.VMEM((tm, tn), F32)], compiler_params=_cp(("parallel", "parallel", "arbitrary"), VMEM_BIG),
    )(*args)


def _ln_bwd(dsr, vnr, dvnc, p, ws01t, ln_g, dp):
    m = p.shape[0]

    def body(dsr_ref, vnr_ref, dvc_ref, va_ref, wt_ref, g_ref, dpi_ref, dp_ref, dw_ref, db_ref, dlg_ref, dlb_ref, dvn_ref):
        i = pl.program_id(0)

        @pl.when(i == 0)
        def _():
            dw_ref[...] = jnp.zeros_like(dw_ref)
            db_ref[...] = jnp.zeros_like(db_ref)
            dlg_ref[...] = jnp.zeros_like(dlg_ref)
            dlb_ref[...] = jnp.zeros_like(dlb_ref)

        for j in range(TOK // AC):
            for g in range(2):
                d = dsr_ref[AC * j:AC * j + AC, AC * g:AC * g + AC]
                d16 = d.astype(BF16)
                dvn_ref[AC * j:AC * j + AC, AC * g:AC * g + AC] = _nn(wt_ref[g], d16)
                dw_ref[g] += _nt(d16, vnr_ref[AC * j:AC * j + AC, AC * g:AC * g + AC])
                db_ref[g] += jnp.sum(d, axis=1, keepdims=True)
        dvn_ref[:, 256:384] = dvc_ref[0]
        dvn_ref[:, 384:512] = dvc_ref[1]
        dvn = dvn_ref[...]
        xf = va_ref[...].astype(F32)
        xc = xf - jnp.mean(xf, axis=-1, keepdims=True)
        rs = lax.rsqrt(jnp.mean(xc * xc, axis=-1, keepdims=True) + EPS)
        xh = xc * rs
        dlg_ref[...] += jnp.sum(dvn * xh, axis=0, keepdims=True)
        dlb_ref[...] += jnp.sum(dvn, axis=0, keepdims=True)
        dxh = dvn * g_ref[...]
        dva = rs * (dxh - jnp.mean(dxh, axis=-1, keepdims=True) - xh * jnp.mean(dxh * xh, axis=-1, keepdims=True))
        dp_ref[...] = dva.astype(BF16)

    return _pcall(
        body, name="ln_bwd", grid=(m // TOK,),
        in_specs=[pl.BlockSpec((TOK, 256), lambda i: (i, 0)), pl.BlockSpec((TOK, 256), lambda i: (i, 0)),
                  pl.BlockSpec((2, TOK, 128), lambda i: (0, i, 0)), pl.BlockSpec((TOK, 512), lambda i: (i, 9)),
                  _full((2, AC, AC)), _full((1, 512)), pl.BlockSpec(memory_space=pl.ANY)],
        out_specs=[pl.BlockSpec((TOK, 512), lambda i: (i, 3)), _full((2, AC, AC)), _full((2, AC, 1)), _full((1, 512)), _full((1, 512))],
        out_shape=[SDS((m, DPR), BF16), SDS((2, AC, AC), F32), SDS((2, AC, 1), F32), SDS((1, 512), F32), SDS((1, 512), F32)],
        scratch_shapes=[pltpu.VMEM((TOK, 512), F32)],
        input_output_aliases={6: 0}, compiler_params=_cp(("arbitrary",)),
    )(dsr, vnr, dvnc, p, ws01t, ln_g, dp)


def _tri_mm(tri, a):
    a1 = a.astype(BF16)
    r1 = a - a1.astype(F32)
    a2 = r1.astype(BF16)
    a3 = (r1 - a2.astype(F32)).astype(BF16)
    n = a.shape[1]
    r = _nn(tri, jnp.concatenate([a1, a2, a3], axis=1))
    return r[:, 0:n] + r[:, n:2 * n] + r[:, 2 * n:3 * n]


def _gla_masks(reverse):
    ri = lax.broadcasted_iota(jnp.int32, (CH, CH), 0)
    ci = lax.broadcasted_iota(jnp.int32, (CH, CH), 1)
    vis = (ci >= ri) if reverse else (ci <= ri)
    vis_t = (ci <= ri) if reverse else (ci >= ri)
    r4 = lax.broadcasted_iota(jnp.int32, (4 * CH, CH), 0) & (CH - 1)
    c4 = lax.broadcasted_iota(jnp.int32, (4 * CH, CH), 1)
    vis4 = (c4 >= r4) if reverse else (c4 <= r4)
    vis4_t = (c4 <= r4) if reverse else (c4 >= r4)
    lane = lax.broadcasted_iota(jnp.int32, (1, 256), 1)
    hm = [(lane >= CH * h) & (lane < CH * h + CH) for h in range(4)]
    return vis, vis_t, vis4, vis4_t, hm


def _stack_heads(x, hm):
    return jnp.concatenate([jnp.where(hm[h], x, 0.0).astype(BF16) for h in range(4)], axis=0)


def _diag_heads(full, hm):
    r = full.shape[0] // 4
    acc = jnp.where(hm[0], full[0:r], 0.0)
    for h in range(1, 4):
        acc = acc + jnp.where(hm[h], full[r * h:r * h + r], 0.0)
    return acc


def _rows_of_heads(x):
    return jnp.concatenate([x[:, 128 * h:128 * h + 128] for h in range(4)], axis=0)


def _lane_vis(reverse, transpose):
    ri = lax.broadcasted_iota(jnp.int32, (CH, 4 * CH), 0)
    ci = lax.broadcasted_iota(jnp.int32, (CH, 4 * CH), 1) & (CH - 1)
    return (ci >= ri) if (reverse != transpose) else (ci <= ri)


def _gla_fwd(p, qkv_blk, lr, lrw, gbias, s0, *, reverse, name):
    m = p.shape[0]
    tb = min(GLA_TB, m)
    nb = m // tb
    nc = tb // CH
    rmap = (lambda i: nb - 1 - i) if reverse else (lambda i: i)

    def body(qkv_ref, lr_ref, lrw_ref, gb_ref, s0_ref, o_ref, sb_ref, sfin_ref, st_ref):
        i = pl.program_id(0)

        @pl.when(i == 0)
        def _():
            st_ref[...] = s0_ref[...]

        vis, _, vis4, _, hm = _gla_masks(reverse)
        tri = vis.astype(F32).astype(BF16)
        logits = _nn(lr_ref[...].astype(BF16), lrw_ref[...]) + gb_ref[...]
        a_all = _logsig(logits) * (1.0 / 16.0)
        st = st_ref[...]
        for c in (range(nc - 1, -1, -1) if reverse else range(nc)):
            rows = slice(CH * c, CH * c + CH)
            b = _tri_mm(tri, a_all[rows])
            bl = b[0:1] if reverse else b[CH - 1:CH]
            q = qkv_ref[rows, 0:256].astype(F32) * 0.125
            k = qkv_ref[rows, 256:512].astype(F32)
            v16 = qkv_ref[rows, 512:1024].astype(BF16)
            qd = q * jnp.exp(b)
            kd16 = (k * jnp.exp(-b)).astype(BF16)
            kdec16 = (k * jnp.exp(bl - b)).astype(BF16)
            qstack = _stack_heads(qd, hm)
            sc = jnp.where(vis4, _nt(qstack, kd16), 0.0).astype(BF16)
            inter = _nt(qstack, st.astype(BF16))
            for h in range(4):
                o_ref[rows, 128 * h:128 * h + 128] = (
                    _nn(sc[CH * h:CH * h + CH], v16[:, 128 * h:128 * h + 128]) + inter[CH * h:CH * h + CH])
            sb_ref[c] = st
            st = st * jnp.exp(bl) + _diag_heads(_tn(v16, kdec16), hm)
        st_ref[...] = st

        @pl.when(i == nb - 1)
        def _():
            sfin_ref[...] = st

    return _pcall(
        body, name=name, grid=(nb,),
        in_specs=[pl.BlockSpec((tb,1024), lambda i: (rmap(i), qkv_blk)), pl.BlockSpec((tb,LRW), lambda i: (rmap(i), 0)),
                  _full((LRW, 256)), _full((1, 256)), _full((128, 256))],
        out_specs=[pl.BlockSpec((tb,512), lambda i: (rmap(i), 0)), pl.BlockSpec((nc, 128, 256), lambda i: (rmap(i), 0, 0)),
                   _full((128, 256))],
        out_shape=[SDS((m, 512), F32), SDS((m // CH, 128, 256), F32), SDS((128, 256), F32)],
        scratch_shapes=[pltpu.VMEM((128, 256), F32)], compiler_params=_cp(("arbitrary",)),
    )(p, lr, lrw, gbias, s0)


def _gla_bwd(p, qkv_blk, lr, lrw, lrwt, gbias, sb, dsfin, do, prev, dp, *, reverse, name):
    m = p.shape[0]
    tb = min(GLA_TB, m)
    nb = m // tb
    nc = tb // CH
    rmap = (lambda i: i) if reverse else (lambda i: nb - 1 - i)
    has_prev = prev is not None
    has_dp = dp is not None

    def body(*refs):
        refs = list(refs)
        qkv_ref, lr_ref, lrw_ref, lrwt_ref, gb_ref, sb_ref, dsfin_ref, do_ref = refs[:8]
        refs = refs[8:]
        if has_prev:
            pq_ref, plr_ref = refs[:2]
            refs = refs[2:]
        if has_dp:
            refs = refs[1:]
        dqkv_ref, dlr_ref, dw2_ref, dgb_ref, ds0_ref, dst_ref, dlog_ref = refs
        i = pl.program_id(0)

        @pl.when(i == 0)
        def _():
            dst_ref[...] = dsfin_ref[...]
            dw2_ref[...] = jnp.zeros_like(dw2_ref)
            dgb_ref[...] = jnp.zeros_like(dgb_ref)

        vis, vis_t, vis4, vis4_t, hm = _gla_masks(reverse)
        tri = vis.astype(F32).astype(BF16)
        tri_t = vis_t.astype(F32).astype(BF16)
        lane_vis = _lane_vis(reverse, False)
        lane_vis_t = _lane_vis(reverse, True)
        lr16 = lr_ref[...].astype(BF16)
        logits = _nn(lr16, lrw_ref[...]) + gb_ref[...]
        a_all = _logsig(logits) * (1.0 / 16.0)
        dsig = (1.0 - jax.nn.sigmoid(logits)) * (1.0 / 16.0)
        dst = dst_ref[...]
        for c in (range(nc) if reverse else range(nc - 1, -1, -1)):
            rows = slice(CH * c, CH * c + CH)
            b = _tri_mm(tri, a_all[rows])
            bl = b[0:1] if reverse else b[CH - 1:CH]
            eb = jnp.exp(b)
            enb = jnp.exp(-b)
            ebl = jnp.exp(bl - b)
            el = jnp.exp(bl)
            q = qkv_ref[rows, 0:256].astype(F32) * 0.125
            k = qkv_ref[rows, 256:512].astype(F32)
            v16 = qkv_ref[rows, 512:1024].astype(BF16)
            do16 = do_ref[rows, :].astype(BF16)
            qd = q * eb
            kd = k * enb
            kdec = k * ebl
            st = sb_ref[c]
            st16 = st.astype(BF16)
            dst16 = dst.astype(BF16)
            qd16 = qd.astype(BF16)
            kd16 = kd.astype(BF16)
            qstack = _stack_heads(qd, hm)
            kstack = _stack_heads(kd, hm)
            kdecstack = _stack_heads(kdec, hm)
            pt = jnp.where(vis4_t, _nt(kstack, qd16), 0.0).astype(BF16)
            dvinter = _nt(kdecstack, dst16)
            do_rows = _rows_of_heads(do16)
            v_rows = _rows_of_heads(v16)
            dp_cat = jnp.where(lane_vis, _diag_heads(_nt(do_rows, v_rows), hm), 0.0).astype(BF16)
            dpt_cat = jnp.where(lane_vis_t, _diag_heads(_nt(v_rows, do_rows), hm), 0.0).astype(BF16)
            dqd = _nn(dp_cat, kstack) + _diag_heads(_nn(do_rows, st16), hm)
            dkd = _nn(dpt_cat, qstack)
            dkdec = _diag_heads(_nn(v_rows, dst16), hm)
            for h in range(4):
                rh = slice(CH * h, CH * h + CH)
                dv_h = _nn(pt[rh], do_rows[rh]) + dvinter[rh]
                if has_prev:
                    dv_h = dv_h + pq_ref[rows, 512 + 128 * h:512 + 128 * h + 128]
                dqkv_ref[rows, 512 + 128 * h:512 + 128 * h + 128] = dv_h.astype(dqkv_ref.dtype)
            dq = dqd * eb * 0.125
            dk = dkd * enb + dkdec * ebl
            if has_prev:
                dq = dq + pq_ref[rows, 0:256]
                dk = dk + pq_ref[rows, 256:512]
            dqkv_ref[rows, 0:256] = dq.astype(dqkv_ref.dtype)
            dqkv_ref[rows, 256:512] = dk.astype(dqkv_ref.dtype)
            g_kdec = dkdec * kdec
            db = dqd * qd - dkd * kd - g_kdec
            dbl = jnp.sum(g_kdec, axis=0, keepdims=True) + jnp.sum(st * dst, axis=0, keepdims=True) * el
            da = _tri_mm(tri_t, db) + dbl
            dlog_ref[rows, :] = da * dsig[rows]
            dst = dst * el + _diag_heads(_tn(do16, qd16), hm)
        dst_ref[...] = dst
        dlog = dlog_ref[...]
        dlog16 = dlog.astype(BF16)
        dlr = _nn(dlog16, lrwt_ref[...])
        if has_prev:
            dlr = dlr + plr_ref[...]
        dlr_ref[...] = dlr
        dw2_ref[...] += _tn(lr16, dlog16)
        dgb_ref[...] += jnp.sum(dlog, axis=0, keepdims=True)

        @pl.when(i == nb - 1)
        def _():
            ds0_ref[...] = dst

    in_specs = [pl.BlockSpec((tb,1024), lambda i: (rmap(i), qkv_blk)), pl.BlockSpec((tb,LRW), lambda i: (rmap(i), 0)),
                _full((LRW, 256)), _full((256, LRW)), _full((1, 256)), pl.BlockSpec((nc, 128, 256), lambda i: (rmap(i), 0, 0)),
                _full((128, 256)), pl.BlockSpec((tb,512), lambda i: (rmap(i), 0))]
    args = [p, lr, lrw, lrwt, gbias, sb, dsfin, do]
    if has_prev:
        in_specs += [pl.BlockSpec((tb,1024), lambda i: (rmap(i), 0)), pl.BlockSpec((tb,LRW), lambda i: (rmap(i), 0))]
        args += list(prev)
    aliases = {}
    if has_dp:
        in_specs.append(pl.BlockSpec(memory_space=pl.ANY))
        aliases = {len(args): 0}
        args.append(dp)
        dq_spec = pl.BlockSpec((tb,1024), lambda i: (rmap(i), 2))
        dq_shape = SDS(dp.shape, dp.dtype)
    else:
        dq_spec = pl.BlockSpec((tb,1024), lambda i: (rmap(i), 0))
        dq_shape = SDS((m, 1024), F32)
    return _pcall(
        body, name=name, grid=(nb,), in_specs=in_specs,
        out_specs=[dq_spec, pl.BlockSpec((tb,LRW), lambda i: (rmap(i), 0)), _full((LRW, 256)), _full((1, 256)), _full((128, 256))],
        out_shape=[dq_shape, SDS((m, LRW), F32), SDS((LRW, 256), F32), SDS((1, 256), F32), SDS((128, 256), F32)],
        scratch_shapes=[pltpu.VMEM((128, 256), F32), pltpu.VMEM((tb,256), F32)],
        input_output_aliases=aliases, compiler_params=_cp(("arbitrary",)),
    )(*args)


def _device_step(x, c, ctx, c_ctx, tgt, wm, bm, ng, wit_g, wit_r, wlrt, ln_g, ln_b, ws, bs, w2, gb2, gbn, wpa, wpb, wo, gf,
                 exchange=None, shards=None):
    L = x.shape[0]
    wit_qkv = wit_r[2048:3072]
    ws16 = ws.astype(BF16)
    wst16 = jnp.swapaxes(ws, 1, 2).astype(BF16)
    bscol = bs[:, :, None]
    lrw = [jnp.zeros((LRW, 256), F32).at[16 * r:16 * r + 16].set(w2[r]).astype(BF16) for r in range(2)]
    lrwt = [w.T for w in lrw]
    gbias = [gb2[r:r + 1] for r in range(2)]

    cc = jnp.zeros((8, D), F32).at[0:1].set(c).at[1:2].set(c_ctx)
    mod = _modvec(cc, wm, bm)
    shift, scale, gate = mod[0:1, 0:D], mod[0:1, D:2 * D], mod[0:1, 2 * D:3 * D]
    shift_c, scale_c = mod[1:2, 0:D], mod[1:2, D:2 * D]

    hc = _prep_h(ctx, ng, scale_c, shift_c, "prep_hc")
    pc = _mm(hc, wit_qkv, tm=256, tn=1024, tk=D, out_dtype=F32, name="mm_pc", b_t=True)
    plrc = _mm(hc, wlrt, tm=256, tn=LRW, tk=D, out_dtype=F32, name="mm_plrc", b_t=True)
    zero_s = jnp.zeros((128, 256), F32)
    _, sbc_f, sc_f = _gla_fwd(pc, 0, plrc, lrw[0], gbias[0], zero_s, reverse=False, name="gla_fwd_cf")
    _, sbc_b, sc_b = _gla_fwd(pc, 0, plrc, lrw[1], gbias[1], zero_s, reverse=True, name="gla_fwd_cb")

    h, p, plr, late = _proj_fwd(x, ng, scale, shift, wit_g, wit_r, wlrt, shards if shards is not None else ())
    if shards is not None:
        me_xy = 2 * lax.axis_index("x") + lax.axis_index("y")
        g_wpa, g_wpb, g_wo = (_own(g, s_, me_xy) for g, s_ in zip(late, shards))
        wpa = jnp.swapaxes(g_wpa, 0, 1).reshape(512, D)
        wpb = jnp.swapaxes(g_wpb, 0, 1).reshape(512, D)
        wo = g_wo.reshape(D, D)
    o_f, sb_f, _ = _gla_fwd(p, 2, plr, lrw[0], gbias[0], sc_f, reverse=False, name="gla_fwd_f")
    o_b, sb_b, _ = _gla_fwd(p, 2, plr, lrw[1], gbias[1], sc_b, reverse=True, name="gla_fwd_b")
    vnr, vnc = _ln_fwd(p, ln_g, ln_b)
    svc = _colmix_fwd(vnc.reshape(2, AC, L), ws16[2:4], bscol[2:4]).reshape(2, L, 128)
    ya_in, yb_in, svr, mrg, dx1, dout, loss, dgate, dgf = _tail_fwd(
        o_f, o_b, p, vnr, svc, x, tgt, ws16[0:2], bscol[0:2], gbn, wpa, wpb, wo, gate, gf)

    dya, dyb, dp_g, dp, dsr, dsc, do, dgbn = _tail_bwd(dout, ya_in, yb_in, p, svr, svc, o_f, o_b, gbn, wo, wpa, wpb)
    dwo = _mm_tn(mrg, dout, ta=D, tn=D, tk=1024, name="mm_dwo")
    dwpa = _mm_tn(ya_in, dya, ta=512, tn=D, tk=1024, name="mm_dwpa")
    dwpb = _mm_tn(yb_in, dyb, ta=512, tn=D, tk=1024, name="mm_dwpb")
    dvnc, dws23, dbs23 = _colmix_bwd(dsc.reshape(2, AC, L), vnc.reshape(2, AC, L), wst16[2:4])
    dp, dws01, dbs01, dlng, dlnb = _ln_bwd(dsr, vnr, dvnc.reshape(2, L, 128), p, wst16[0:2], ln_g, dp)
    zero_ds = jnp.zeros((128, 256), F32)
    dqkv_f, dlr_f, dw2_f, dgb_f, ds0_f = _gla_bwd(p, 2, plr, lrw[0], lrwt[0], gbias[0], sb_f, zero_ds, do, None, None,
                                                  reverse=False, name="gla_bwd_f")
    dp, dlr, dw2_b, dgb_b, ds0_b = _gla_bwd(p, 2, plr, lrw[1], lrwt[1], gbias[1], sb_b, zero_ds, do, (dqkv_f, dlr_f), dp,
                                            reverse=True, name="gla_bwd_b")
    zero_do = jnp.zeros((ctx.shape[0], 512), F32)
    dqkvc_f, dlrc_f, dw2c_f, dgbc_f, _ = _gla_bwd(pc, 0, plrc, lrw[0], lrwt[0], gbias[0], sbc_f, ds0_f, zero_do, None, None,
                                                  reverse=False, name="gla_bwd_cf")
    dqkvc, dlrc, dw2c_b, dgbc_b, _ = _gla_bwd(pc, 0, plrc, lrw[1], lrwt[1], gbias[1], sbc_b, ds0_b, zero_do,
                                              (dqkvc_f, dlrc_f), None, reverse=True, name="gla_bwd_cb")
    dhc = _mm(dqkvc, wit_qkv, tm=256, tn=D, tk=1024, out_dtype=F32, name="mm_dhc")
    dhc = _mm(dlrc, wlrt, tm=256, tn=D, tk=LRW, out_dtype=F32, name="mm_dhc_lr", acc=dhc)
    _, dng_c, dscale_c, dshift_c = _prep_bwd(ctx, dhc, None, ng, scale_c, "prep_bwd_c")

    dwit_g = _mm_tn(dp_g, h, ta=1024, tn=D, tk=2048, name="mm_dwi_g")
    dwit_r = _mm_tn(dp, h, ta=1024, tn=D, tk=2048, name="mm_dwi_r")
    dwit_qkv = _mm_tn(dqkvc, hc, ta=1024, tn=D, tk=256, name="mm_dwi_c", acc=dwit_r[2048:3072])
    dwlrt = _mm_tn(dlr, h, ta=LRW, tn=D, tk=2048, name="mm_dwlr")
    dwlrt = _mm_tn(dlrc, hc, ta=LRW, tn=D, tk=256, name="mm_dwlr_c", acc=dwlrt)
    big = dict(dwit_g=dwit_g, dwit_r=dwit_r, dwit_qkv=dwit_qkv, dwlrt=dwlrt, dwpa=dwpa, dwpb=dwpb, dwo=dwo)

    send = exchange(big) if exchange is not None else ()
    (dx, dng, dscale, dshift), got = _proj_bwd(dp_g, dp, dlr, wit_g, wit_r, wlrt, x, dx1, ng, scale, send)

    dmodc = jnp.concatenate([dshift_c, dscale_c], axis=1)
    dscc = _dcctx(jnp.zeros((8, 2 * D), F32).at[0:1].set(dmodc), wm)[0:1]
    dw2p = dw2_f + dw2c_f, dw2_b + dw2c_b
    return dict(
        loss=loss[0, 0], dx=dx, got=got, **big,
        dmod=jnp.concatenate([dshift, dscale, dgate], axis=1), dmodc=dmodc, dscc=dscc, dng=dng + dng_c,
        dlng=dlng, dlnb=dlnb, dws=jnp.concatenate([dws01, dws23], axis=0),
        dbs=jnp.concatenate([dbs01, dbs23], axis=0)[:, :, 0], dgbn=dgbn, dgf=dgf,
        dw2=jnp.stack([dw2p[0][0:16], dw2p[1][16:32]]), dgb2=jnp.concatenate([dgb_f + dgbc_f, dgb_b + dgbc_b], axis=0),
    )


ANY = pl.BlockSpec(memory_space=pl.ANY)


def _coords():
    return lax.axis_index("x"), lax.axis_index("y"), lax.axis_index("c")


def _flip(v, bit):
    return 1 - v if bit else v


def _remote(src, dst, send_sem, recv_sem, dev):
    return pltpu.make_async_remote_copy(src_ref=src, dst_ref=dst, send_sem=send_sem, recv_sem=recv_sem,
                                        device_id=dev, device_id_type=MESH)


def _own(out, block, idx):
    return lax.dynamic_update_slice_in_dim(out, block[None], idx, axis=0)


def _half_idx(shape, axis, which, lead=()):
    idx = [pl.ds(0, d) for d in shape]
    h = shape[axis] // 2
    idx[axis] = pl.ds(which * h, h)
    return tuple(lead) + tuple(idx)


def _gather_weights(split, whole, name):
    ns, nw = len(split), len(whole)
    n = ns + nw
    arrs = [a for a, _ in split] + list(whole)

    def body(*refs):
        ins, outs = refs[:n], refs[n:2 * n]
        a_send, a_recv, b_send, b_recv = refs[2 * n:]
        x, y, c = _coords()
        me = 2 * x + y
        sib = (x, y, 1 - c)
        peers = [(1 - x, y), (x, 1 - y), (1 - x, 1 - y)]

        def half(k, slot, which):
            return outs[k].at[_half_idx(arrs[k].shape, split[k][1], which, lead=(slot,))]

        sends = []
        for k in range(n):
            for j, (px, py) in enumerate(peers):
                if k < ns:
                    rc = _remote(ins[k].at[_half_idx(arrs[k].shape, split[k][1], c)], half(k, me, c), a_send.at[3 * k + j],
                                 a_recv.at[3 * k + j], (px, py, c))
                else:
                    rc = _remote(ins[k], outs[k].at[me], a_send.at[3 * k + j], a_recv.at[3 * k + j], (px, py, c))
                rc.start()
                sends.append(rc)
        for k in range(ns):
            for j, (px, py) in enumerate(peers):
                landed = half(k, 2 * px + py, c)
                _remote(landed, landed, a_send.at[3 * k + j], a_recv.at[3 * k + j], (px, py, c)).wait_recv()
                fw = _remote(landed, landed, b_send.at[3 * k + j], b_recv.at[3 * k + j], sib)
                fw.start()
                sends.append(fw)
        for k in range(ns, n):
            for j, (px, py) in enumerate(peers):
                landed = outs[k].at[2 * px + py]
                _remote(landed, landed, a_send.at[3 * k + j], a_recv.at[3 * k + j], (px, py, c)).wait_recv()
        for k in range(ns):
            for j, (px, py) in enumerate(peers):
                passed = half(k, 2 * px + py, 1 - c)
                _remote(passed, passed, b_send.at[3 * k + j], b_recv.at[3 * k + j], sib).wait_recv()
        for rc in sends:
            rc.wait_send()

    outs = _pcall(
        body, name=name, in_specs=[ANY] * n, out_specs=[ANY] * n,
        out_shape=[SDS((4,) + a.shape, a.dtype) for a in arrs],
        scratch_shapes=[pltpu.SemaphoreType.DMA((3 * n,)), pltpu.SemaphoreType.DMA((3 * n,)), pltpu.SemaphoreType.DMA((3 * ns,)),
                        pltpu.SemaphoreType.DMA((3 * ns,))],
    )(*arrs)
    me_xy = 2 * lax.axis_index("x") + lax.axis_index("y")
    return [_own(o, a, me_xy) for o, a in zip(outs, arrs)]


def _gather_all(a, name):
    masks = [(mx, my, mc) for mx in range(2) for my in range(2) for mc in range(2)][1:]

    def body(in_ref, out_ref, send_sems, recv_sems):
        x, y, c = _coords()
        me = 4 * x + 2 * y + c
        sends = []
        for j, (mx, my, mc) in enumerate(masks):
            peer = (_flip(x, mx), _flip(y, my), _flip(c, mc))
            rc = pltpu.make_async_remote_copy(
                src_ref=in_ref, dst_ref=out_ref.at[me], send_sem=send_sems.at[j], recv_sem=recv_sems.at[j],
                device_id=peer, device_id_type=MESH)
            rc.start()
            sends.append(rc)
        for j, (mx, my, mc) in enumerate(masks):
            px, py, pc = _flip(x, mx), _flip(y, my), _flip(c, mc)
            pltpu.make_async_remote_copy(
                src_ref=in_ref, dst_ref=out_ref.at[4 * px + 2 * py + pc], send_sem=send_sems.at[j], recv_sem=recv_sems.at[j],
                device_id=(px, py, pc), device_id_type=MESH).wait_recv()
        for rc in sends:
            rc.wait_send()

    out = _pcall(
        body, name=name, in_specs=[ANY], out_specs=ANY, out_shape=SDS((8,) + a.shape, a.dtype),
        scratch_shapes=[pltpu.SemaphoreType.DMA((7,)), pltpu.SemaphoreType.DMA((7,))],
    )(a)
    return _own(out, a, 4 * lax.axis_index("x") + 2 * lax.axis_index("y") + lax.axis_index("c"))


def _half_shape(shape, axis):
    return tuple(d // 2 if i == axis else d for i, d in enumerate(shape))


def _swap_half_c(arrs, axes, name):
    n = len(arrs)

    def body(*refs):
        ins, outs = refs[:n], refs[n:2 * n]
        send_sems, recv_sems = refs[2 * n:]
        x, y, c = _coords()
        sends = []
        for k in range(n):
            rc = _remote(ins[k].at[_half_idx(arrs[k].shape, axes[k], 1 - c)], outs[k], send_sems.at[k], recv_sems.at[k],
                         (x, y, 1 - c))
            rc.start()
            sends.append(rc)
        for rc in sends:
            rc.wait()

    return _pcall(
        body, name=name, in_specs=[ANY] * n, out_specs=[ANY] * n,
        out_shape=[SDS(_half_shape(a.shape, ax), a.dtype) for a, ax in zip(arrs, axes)],
        scratch_shapes=[pltpu.SemaphoreType.DMA((n,)), pltpu.SemaphoreType.DMA((n,))],
    )(*arrs)


def _a2a_xy(arrs, name):
    n = len(arrs)

    def body(*refs):
        ins, outs = refs[:n], refs[n:2 * n]
        send_sems, recv_sems = refs[2 * n:]
        x, y, c = _coords()
        me = 2 * x + y
        peers = [(1 - x, y), (x, 1 - y), (1 - x, 1 - y)]
        sends = []
        for k in range(n):
            for j, (px, py) in enumerate(peers):
                rc = _remote(ins[k].at[2 * px + py], outs[k].at[me], send_sems.at[3 * k + j], recv_sems.at[3 * k + j], (px, py, c))
                rc.start()
                sends.append(rc)
        for k in range(n):
            for j, (px, py) in enumerate(peers):
                landed = outs[k].at[2 * px + py]
                _remote(landed, landed, send_sems.at[3 * k + j], recv_sems.at[3 * k + j], (px, py, c)).wait_recv()
        for rc in sends:
            rc.wait_send()

    outs = _pcall(
        body, name=name, in_specs=[ANY] * n, out_specs=[ANY] * n, out_shape=[SDS(a.shape, a.dtype) for a in arrs],
        scratch_shapes=[pltpu.SemaphoreType.DMA((3 * n,)), pltpu.SemaphoreType.DMA((3 * n,))],
    )(*arrs)
    me_xy = 2 * lax.axis_index("x") + lax.axis_index("y")
    return [_own(o, lax.dynamic_index_in_dim(a, me_xy, axis=0, keepdims=False), me_xy) for o, a in zip(outs, arrs)]


def _exchange_c(arrs, name):
    n = len(arrs)

    def body(*refs):
        ins, outs = refs[:n], refs[n:2 * n]
        send_sems, recv_sems = refs[2 * n:]
        x, y, c = _coords()
        sends = []
        for k in range(n):
            rc = _remote(ins[k], outs[k], send_sems.at[k], recv_sems.at[k], (x, y, 1 - c))
            rc.start()
            sends.append(rc)
        for rc in sends:
            rc.wait()

    return _pcall(
        body, name=name, in_specs=[ANY] * n, out_specs=[ANY] * n, out_shape=[SDS(a.shape, a.dtype) for a in arrs],
        scratch_shapes=[pltpu.SemaphoreType.DMA((n,)), pltpu.SemaphoreType.DMA((n,))],
    )(*arrs)


def _join_halves(halves, axes, name):
    n = len(halves)
    full = [tuple(2 * d if i == ax else d for i, d in enumerate(a.shape)) for a, ax in zip(halves, axes)]

    def body(*refs):
        ins, outs = refs[:n], refs[n:2 * n]
        send_sems, recv_sems = refs[2 * n:]
        x, y, c = _coords()
        sends = []
        for k in range(n):
            rc = _remote(ins[k], outs[k].at[_half_idx(full[k], axes[k], c)], send_sems.at[k], recv_sems.at[k], (x, y, 1 - c))
            rc.start()
            sends.append(rc)
        for k in range(n):
            landed = outs[k].at[_half_idx(full[k], axes[k], 1 - c)]
            _remote(landed, landed, send_sems.at[k], recv_sems.at[k], (x, y, 1 - c)).wait_recv()
        for rc in sends:
            rc.wait_send()

    outs = _pcall(
        body, name=name, in_specs=[ANY] * n, out_specs=[ANY] * n,
        out_shape=[SDS(f, a.dtype) for f, a in zip(full, halves)],
        scratch_shapes=[pltpu.SemaphoreType.DMA((n,)), pltpu.SemaphoreType.DMA((n,))],
    )(*halves)
    ci = lax.axis_index("c")
    return [lax.dynamic_update_slice_in_dim(o, a, ci * a.shape[ax], axis=ax) for o, a, ax in zip(outs, halves, axes)]


def _pair_sum(a, got, cidx, axis, name):
    _, r, cdim = a.shape
    hshape = _half_shape(a.shape, axis)

    def body(c_ref, a_ref, g_ref, o_ref):
        o_ref[...] = (a_ref[...] + g_ref[...]).astype(BF16)

    if axis == 1:
        tr = min(r // 2, 256)
        nj = (r // 2) // tr
        blk = pl.BlockSpec((1, tr, cdim), lambda s, j, c: (s, j, 0))
        a_spec = pl.BlockSpec((1, tr, cdim), lambda s, j, c: (s, c[0] * nj + j, 0))
    else:
        nj = (cdim // 2) // 128
        blk = pl.BlockSpec((1, r, 128), lambda s, j, c: (s, 0, j))
        a_spec = pl.BlockSpec((1, r, 128), lambda s, j, c: (s, 0, c[0] * nj + j))
    return _pcall(
        body, name=name, out_shape=SDS(hshape, BF16),
        grid_spec=pltpu.PrefetchScalarGridSpec(num_scalar_prefetch=1, grid=(4, nj), in_specs=[a_spec, blk], out_specs=blk),
        compiler_params=_cp(("parallel", "parallel")),
    )(cidx, a, got)


def _sum_chips(parts, name):
    _, h, cdim = parts.shape

    def body(p_ref, o_ref):
        acc = p_ref[0].astype(F32)
        for k in range(1, 4):
            acc = acc + p_ref[k].astype(F32)
        o_ref[...] = acc

    if h % 256 == 0 or h in (128,):
        tr = min(h, 256)
        grid, in_spec, out_spec = (h // tr,), pl.BlockSpec((4, tr, cdim), lambda i: (0, i, 0)), pl.BlockSpec((tr, cdim), lambda i: (i, 0))
    else:
        grid, in_spec, out_spec = (cdim // 128,), pl.BlockSpec((4, h, 128), lambda i: (0, 0, i)), pl.BlockSpec((h, 128), lambda i: (0, i))
    return _pcall(
        body, name=name, grid=grid, in_specs=[in_spec], out_specs=out_spec, out_shape=SDS((h, cdim), F32),
        compiler_params=_cp(("parallel",)),
    )(parts)


def _sum_slots(a, name, rows):
    s, n, _ = a.shape

    def body(a_ref, o_ref):
        acc = a_ref[0]
        for k in range(1, s):
            acc = acc + a_ref[k]
        o_ref[...] = acc

    return _pcall(
        body, name=name, grid=(n // rows,), in_specs=[pl.BlockSpec((s, rows, 128), lambda i: (0, i, 0))],
        out_specs=pl.BlockSpec((rows, 128), lambda i: (i, 0)), out_shape=SDS((n, 128), F32),
        compiler_params=_cp(("parallel",)),
    )(a)


def _adam_math(w, g, m, v):
    nm = ADAM_B1 * m + (1.0 - ADAM_B1) * g
    nv = ADAM_B2 * v + (1.0 - ADAM_B2) * (g * g)
    m_hat = nm / (1.0 - ADAM_B1 ** ADAM_STEP)
    v_hat = nv / (1.0 - ADAM_B2 ** ADAM_STEP)
    return -ADAM_LR * (m_hat / (jnp.sqrt(v_hat) + ADAM_EPS) + ADAM_WD * w), nm, nv


def _adamw(w, g, m, v, name, rows):
    r, cdim = w.shape

    def body(w_ref, g_ref, m_ref, v_ref, d_ref, nm_ref, nv_ref):
        d_ref[...], nm_ref[...], nv_ref[...] = _adam_math(w_ref[...], g_ref[...], m_ref[...], v_ref[...])

    blk = pl.BlockSpec((rows, cdim), lambda i: (i, 0))
    return _pcall(
        body, name=name, grid=(r // rows,), in_specs=[blk] * 4, out_specs=[blk] * 3,
        out_shape=[SDS(w.shape, F32)] * 3, compiler_params=_cp(("parallel",)),
    )(w, g, m, v)


def _adamw_joined(w, mine, other, m, v, cidx, axis, name, rows):
    r, cdim = w.shape
    if axis == 0:
        rows = r

    def body(c_ref, w_ref, a_ref, b_ref, m_ref, v_ref, g_ref, d_ref, nm_ref, nv_ref):
        a, b = a_ref[...], b_ref[...]
        g = jnp.where(c_ref[0] == 0, jnp.concatenate([a, b], axis=axis), jnp.concatenate([b, a], axis=axis))
        g_ref[...] = g
        d_ref[...], nm_ref[...], nv_ref[...] = _adam_math(w_ref[...], g, m_ref[...], v_ref[...])

    blk = pl.BlockSpec((rows, cdim), lambda i, c: (i, 0))
    hshape = (rows // 2, cdim) if axis == 0 else (rows, cdim // 2)
    hblk = pl.BlockSpec(hshape, lambda i, c: (i, 0))
    return _pcall(
        body, name=name, out_shape=[SDS(w.shape, F32)] * 4,
        grid_spec=pltpu.PrefetchScalarGridSpec(num_scalar_prefetch=1, grid=(r // rows,), in_specs=[blk, hblk, hblk, blk, blk],
                                               out_specs=[blk] * 4),
        compiler_params=_cp(("parallel",)),
    )(cidx, w, mine, other, m, v)


def _adamw_many(ws, gs, ms, vs, name):
    n = len(ws)

    def body(*refs):
        outs = refs[4 * n:]
        for k in range(n):
            d, nm, nv = _adam_math(refs[k][...], refs[n + k][...], refs[2 * n + k][...], refs[3 * n + k][...])
            outs[k][...] = d
            outs[n + k][...] = nm
            outs[2 * n + k][...] = nv

    res = _pcall(body, name=name, out_shape=[SDS(w.shape, F32) for w in ws] * 3)(*ws, *gs, *ms, *vs)
    return res[:n], res[n:2 * n], res[2 * n:]


def _pack(pieces, rows):
    flat = jnp.concatenate([p.reshape(-1) for p in pieces])
    return jnp.pad(flat, (0, rows * 128 - flat.shape[0])).reshape(rows, 128)


def _unpack(buf, shapes):
    flat = buf.reshape(-1)
    out, off = [], 0
    for shp in shapes:
        size = 1
        for s in shp:
            size *= s
        out.append(flat[off:off + size].reshape(shp))
        off += size
    return out


def _perm_cols(w):
    perm = jnp.concatenate([w[..., 3104:5152], w[..., 0:1024], w[..., 1056:1568], w[..., 1568:2080], w[..., 2592:3104],
                            w[..., 2080:2592]], axis=-1)
    return perm, w[..., 1024:1056]


def _unperm_cols(perm, lr32):
    return jnp.concatenate([perm[..., 2048:3072], lr32, perm[..., 3072:3584], perm[..., 3584:4096], perm[..., 4608:5120],
                            perm[..., 4096:4608], perm[..., 0:2048]], axis=-1)


SMALL_ROWS = 672
HALF_ROWS = 7200


def kernel(x, c, ctx, c_ctx, w_mod, b_mod, norm_g, w_in, a_ln_g, a_ln_b, a_ws, a_bs, b_gate_w2, b_gate_b, b_norm_g, w_proj_a, w_proj_b, w_out, final_norm_g, loss_target, m_c_ctx, m_w_mod, m_b_mod, m_norm_g, m_w_in, m_a_ln_g, m_a_ln_b, m_a_ws, m_a_bs, m_b_gate_w2, m_b_gate_b, m_b_norm_g, m_w_proj_a, m_w_proj_b, m_w_out, m_final_norm_g, v_c_ctx, v_w_mod, v_b_mod, v_norm_g, v_w_in, v_a_ln_g, v_a_ln_b, v_a_ws, v_a_bs, v_b_gate_w2, v_b_gate_b, v_b_norm_g, v_w_proj_a, v_w_proj_b, v_w_out, v_final_norm_g):
    xi, yi, ci = _coords()
    me_xy = 2 * xi + yi

    gate_pack = _pack([b_gate_w2[0], b_gate_b[0]], 24)
    w_in_t, m_w_in_t, v_w_in_t = (jnp.swapaxes(a[0], 0, 1) for a in (w_in, m_w_in, v_w_in))
    g_wit, g_wm, g_gate = _gather_weights([(w_in_t.astype(BF16), 1), (w_mod[0].astype(BF16), 0)], [gate_pack], "gather_weights")
    late_shards = (w_proj_a[0].astype(BF16), w_proj_b[0].astype(BF16), w_out[0].astype(BF16))
    wit_u = g_wit.reshape(4 * 1288, D)
    wit_g = wit_u[3104:5152]
    wit_r = jnp.concatenate([wit_u[1056:1568], wit_u[1568:2080], wit_u[2592:3104], wit_u[2080:2592], wit_u[0:1024]], axis=0)
    wlrt = jnp.pad(wit_u[1024:1056], ((0, LRW - 32), (0, 0)))
    wm = jnp.swapaxes(g_wm, 0, 1).reshape(D, 3 * D)
    gflat = g_gate.reshape(4, 24 * 128)
    w2 = jnp.swapaxes(gflat[:, 0:2048].reshape(4, 2, 16, 64), 0, 2)
    w2 = jnp.swapaxes(w2, 0, 1).reshape(2, 16, 256)
    gb2 = jnp.swapaxes(gflat[:, 2048:2176].reshape(4, 2, 64), 0, 1).reshape(2, 256)

    tags = ["wi", "wpa", "wpb", "wo"]
    half_axes = [2, 1, 1, 1]
    sent = []

    def exchange(g):
        dwr = g["dwit_r"]
        dwit_u = jnp.concatenate([g["dwit_qkv"], g["dwlrt"][0:32], dwr[0:512], dwr[512:1024], dwr[1536:2048], dwr[1024:1536],
                                  g["dwit_g"]], axis=0)
        big = [dwit_u.reshape(4, 1288, D), jnp.swapaxes(g["dwpa"].reshape(512, 4, 256), 0, 1),
               jnp.swapaxes(g["dwpb"].reshape(512, 4, 256), 0, 1), g["dwo"].reshape(4, 256, D)]
        other = _swap_half_c(big, half_axes, "swap_half_in")
        cidx = jnp.reshape(ci, (1,)).astype(jnp.int32)
        sent.extend(_pair_sum(a, o, cidx, ax, "sum_pair_" + t) for a, o, ax, t in zip(big, other, half_axes, tags))
        return sent

    r = _device_step(x[0], c, ctx[0], c_ctx[None], loss_target[0], wm, b_mod, norm_g, wit_g, wit_r, wlrt, a_ln_g, a_ln_b,
                     a_ws[0], a_bs[0], w2, gb2, b_norm_g, None, None, None, final_norm_g[None], exchange, late_shards)

    small = _pack([r["dmod"], c, r["dmodc"], r["dscc"], r["dng"], r["dlng"], r["dlnb"], r["dws"], r["dbs"], r["dgbn"], r["dgf"],
                   r["dw2"], r["dgb2"], jnp.broadcast_to(r["loss"], (128,))], SMALL_ROWS)
    small_all = _gather_all(small, "gather_small")
    small_sum = _sum_slots(small_all, "sum_small", SMALL_ROWS // 4)
    (s_dmod, _, s_dmodc, s_dscc, s_dng, s_dlng, s_dlnb, s_dws, s_dbs, s_dgbn, s_dgf, s_dw2, s_dgb2, s_loss) = _unpack(
        small_sum, [(1, 3 * D), (1, D), (1, 2 * D), (D,), (1, D), (1, 512), (1, 512), (1, 4, 128, 128), (1, 4, 128), (1, 512),
                    (D,), (2, 16, 256), (2, 256), (128,)])
    loss = s_loss[0]
    s_dmodc_p = jnp.pad(s_dmodc, ((0, 0), (0, D)))
    g_b_mod = s_dmod + s_dmodc_p
    sg = jax.nn.sigmoid(c_ctx)
    g_c_ctx = s_dscc * (sg * (1.0 + c_ctx * (1.0 - sg)))
    g_w2 = lax.dynamic_slice_in_dim(s_dw2, 64 * me_xy, 64, axis=2)[None]
    g_gb2 = lax.dynamic_slice_in_dim(s_dgb2, 64 * me_xy, 64, axis=1)[None]

    flat_all = small_all.reshape(8, SMALL_ROWS * 128)
    dmod_all = flat_all[:, 0:3 * D]
    c_all = flat_all[:, 3 * D:4 * D]
    lhs = jnp.concatenate([_silu(c_all), _silu(c_ctx)[None], jnp.zeros((7, D), F32)], axis=0)
    rhs = jnp.concatenate([dmod_all, s_dmodc_p, jnp.zeros((7, 3 * D), F32)], axis=0)
    rhs = lax.dynamic_slice_in_dim(rhs, 768 * me_xy, 768, axis=1)
    g_w_mod = _mm(lhs.T.astype(BF16), rhs.astype(BF16), tm=D, tn=768, tk=16, out_dtype=F32, name="mm_dwm")

    parts = [_own(g, lax.dynamic_index_in_dim(s_, me_xy, axis=0, keepdims=False), me_xy) for g, s_ in zip(r["got"], sent)]
    halves = [_sum_chips(p_, "sum_chips_" + t) for p_, t in zip(parts, tags)]
    others = _exchange_c(halves, "swap_half_out")

    cidx = jnp.reshape(ci, (1,)).astype(jnp.int32)
    g_w_in_t, d_w_in_t, nm_w_in_t, nv_w_in_t = _adamw_joined(w_in_t, halves[0], others[0], m_w_in_t, v_w_in_t, cidx, 1,
                                                             "adamw_w_in", 184)
    g_w_in, d_w_in, nm_w_in, nv_w_in = (jnp.swapaxes(a, 0, 1) for a in (g_w_in_t, d_w_in_t, nm_w_in_t, nv_w_in_t))
    g_wpa, d_wpa, nm_wpa, nv_wpa = _adamw_joined(w_proj_a[0], halves[1], others[1], m_w_proj_a[0], v_w_proj_a[0], cidx, 0,
                                                 "adamw_wpa", 0)
    g_wpb, d_wpb, nm_wpb, nv_wpb = _adamw_joined(w_proj_b[0], halves[2], others[2], m_w_proj_b[0], v_w_proj_b[0], cidx, 0,
                                                 "adamw_wpb", 0)
    g_wo, d_wo, nm_wo, nv_wo = _adamw_joined(w_out[0], halves[3], others[3], m_w_out[0], v_w_out[0], cidx, 0, "adamw_wo", 0)
    d_w_mod, nm_w_mod, nv_w_mod = _adamw(w_mod[0], g_w_mod, m_w_mod[0], v_w_mod[0], "adamw_w_mod", 256)

    names = ["c_ctx", "b_mod", "norm_g", "a_ln_g", "a_ln_b", "a_ws", "a_bs", "b_gate_w2", "b_gate_b", "b_norm_g", "final_norm_g"]
    ws_ = [c_ctx, b_mod, norm_g, a_ln_g, a_ln_b, a_ws, a_bs, b_gate_w2, b_gate_b, b_norm_g, final_norm_g]
    gs_ = [g_c_ctx, g_b_mod, s_dng, s_dlng, s_dlnb, s_dws, s_dbs, g_w2, g_gb2, s_dgbn, s_dgf]
    ms_ = [m_c_ctx, m_b_mod, m_norm_g, m_a_ln_g, m_a_ln_b, m_a_ws, m_a_bs, m_b_gate_w2, m_b_gate_b, m_b_norm_g, m_final_norm_g]
    vs_ = [v_c_ctx, v_b_mod, v_norm_g, v_a_ln_g, v_a_ln_b, v_a_ws, v_a_bs, v_b_gate_w2, v_b_gate_b, v_b_norm_g, v_final_norm_g]
    shapes = [w.shape for w in ws_]
    flat2 = [(1, 1024), (1, 3072), (1, 1024), (1, 512), (1, 512), (512, 128), (4, 128), (32, 64), (2, 64), (1, 512), (1, 1024)]
    as2d = lambda arrs: [a.reshape(s) for a, s in zip(arrs, flat2)]
    d_s, nm_s, nv_s = _adamw_many(as2d(ws_), as2d(gs_), as2d(ms_), as2d(vs_), "adamw_small")
    d_small = {n: a.reshape(s) for n, a, s in zip(names, d_s, shapes)}
    nm_small = {n: a.reshape(s) for n, a, s in zip(names, nm_s, shapes)}
    nv_small = {n: a.reshape(s) for n, a, s in zip(names, nv_s, shapes)}
    g_small = {n: g.reshape(s) for n, g, s in zip(names, gs_, shapes)}

    order = ["c_ctx", "w_mod", "b_mod", "norm_g", "w_in", "a_ln_g", "a_ln_b", "a_ws", "a_bs", "b_gate_w2", "b_gate_b", "b_norm_g",
             "w_proj_a", "w_proj_b", "w_out", "final_norm_g"]
    big_g = dict(w_mod=g_w_mod[None], w_in=g_w_in[None], w_proj_a=g_wpa[None], w_proj_b=g_wpb[None], w_out=g_wo[None])
    big_d = dict(w_mod=d_w_mod[None], w_in=d_w_in[None], w_proj_a=d_wpa[None], w_proj_b=d_wpb[None], w_out=d_wo[None])
    big_m = dict(w_mod=nm_w_mod[None], w_in=nm_w_in[None], w_proj_a=nm_wpa[None], w_proj_b=nm_wpb[None], w_out=nm_wo[None])
    big_v = dict(w_mod=nv_w_mod[None], w_in=nv_w_in[None], w_proj_a=nv_wpa[None], w_proj_b=nv_wpb[None], w_out=nv_wo[None])
    grads = [big_g[n] if n in big_g else g_small[n] for n in order]
    deltas = [big_d[n] if n in big_d else d_small[n] for n in order]
    new_m = [big_m[n] if n in big_m else nm_small[n] for n in order]
    new_v = [big_v[n] if n in big_v else nv_small[n] for n in order]
    return (loss, r["dx"][None], *grads, *deltas, *new_m, *new_v)
```

```python
import functools

import jax
import jax.numpy as jnp
from jax import lax
from jax.experimental import pallas as pl
from jax.experimental.pallas import tpu as pltpu

F32 = jnp.float32
BF16 = jnp.bfloat16
SDS = jax.ShapeDtypeStruct

D = 1024
NP = 5120
LRW = 128
CH = 64
AC = 128
EPS = 1e-6
TOK = 256
GLA_TB = 1024
VMEM_BIG = 48 * 1024 * 1024

ADAM_LR, ADAM_B1, ADAM_B2, ADAM_EPS, ADAM_WD, ADAM_STEP = 0.001, 0.9, 0.999, 1e-08, 0.01, 10

_pcall = pl.pallas_call
MESH = pl.DeviceIdType.MESH


def _cp(sem=None, vmem=None):
    kw = {}
    if sem is not None:
        kw["dimension_semantics"] = sem
    if vmem is not None:
        kw["vmem_limit_bytes"] = vmem
    return pltpu.CompilerParams(**kw)


def _silu(x):
    return x * jax.nn.sigmoid(x)


def _dsilu(x):
    s = jax.nn.sigmoid(x)
    return s * (1.0 + x * (1.0 - s))


def _silu_and_grad(x):
    s = jax.nn.sigmoid(x)
    return x * s, s * (1.0 + x * (1.0 - s))


def _logsig(x):
    return jnp.minimum(x, 0.0) - jnp.log1p(jnp.exp(-jnp.abs(x)))


def _nt(a, b):
    return lax.dot_general(a, b, (((1,), (1,)), ((), ())), preferred_element_type=F32)


def _tn(a, b):
    return lax.dot_general(a, b, (((0,), (0,)), ((), ())), preferred_element_type=F32)


def _nn(a, b):
    return jnp.dot(a, b, preferred_element_type=F32)


def _full(shape):
    return pl.BlockSpec(shape, lambda *_: (0,) * len(shape))


def _mm(a, b, *, tm, tn, tk, out_dtype, name, acc=None, n_outer=False, b_t=False):
    m, k = a.shape
    n, k2 = (b.shape if b_t else b.shape[::-1])
    assert k == k2 and m % tm == 0 and n % tn == 0 and k % tk == 0, (a.shape, b.shape, tm, tn, tk)
    nk = k // tk
    has_acc = acc is not None

    def body(*refs):
        if has_acc:
            a_ref, b_ref, c_ref, o_ref = refs[:4]
        else:
            a_ref, b_ref, o_ref = refs[:3]
        part = (_nt if b_t else _nn)(a_ref[...].astype(BF16), b_ref[...].astype(BF16))
        if nk == 1:
            o_ref[...] = ((c_ref[...] + part) if has_acc else part).astype(out_dtype)
            return
        acc_ref = refs[-1]
        kk = pl.program_id(2)

        @pl.when(kk == 0)
        def _():
            if has_acc:
                acc_ref[...] = c_ref[...] + part
            else:
                acc_ref[...] = part

        @pl.when(kk > 0)
        def _():
            acc_ref[...] += part

        @pl.when(kk == nk - 1)
        def _():
            o_ref[...] = acc_ref[...].astype(out_dtype)

    if n_outer:
        ij = lambda g0, g1: (g1, g0)
        grid = (n // tn, m // tm, nk)
    else:
        ij = lambda g0, g1: (g0, g1)
        grid = (m // tm, n // tn, nk)
    b_spec = (pl.BlockSpec((tn, tk), lambda g0, g1, kk: (ij(g0, g1)[1], kk)) if b_t
              else pl.BlockSpec((tk, tn), lambda g0, g1, kk: (kk, ij(g0, g1)[1])))
    in_specs = [pl.BlockSpec((tm, tk), lambda g0, g1, kk: (ij(g0, g1)[0], kk)), b_spec]
    args = [a, b]
    if has_acc:
        in_specs.append(pl.BlockSpec((tm, tn), lambda g0, g1, kk: ij(g0, g1)))
        args.append(acc)
    return _pcall(
        body, name=name, grid=grid, in_specs=in_specs,
        out_specs=pl.BlockSpec((tm, tn), lambda g0, g1, kk: ij(g0, g1)),
        out_shape=SDS((m, n), out_dtype), scratch_shapes=([pltpu.VMEM((tm, tn), F32)] if nk > 1 else []),
        compiler_params=_cp(("parallel", "parallel", "arbitrary"), VMEM_BIG),
    )(*args)


def _mm_tn(a, b, *, ta, tn, tk, name, acc=None):
    m, ka = a.shape
    m2, n = b.shape
    assert m == m2 and ka % ta == 0 and n % tn == 0 and m % tk == 0, (a.shape, b.shape, ta, tn, tk)
    nk = m // tk
    has_acc = acc is not None

    def body(*refs):
        if has_acc:
            a_ref, b_ref, c_ref, o_ref = refs
        else:
            a_ref, b_ref, o_ref = refs
        kk = pl.program_id(2)
        part = _tn(a_ref[...].astype(BF16), b_ref[...].astype(BF16))

        @pl.when(kk == 0)
        def _():
            if has_acc:
                o_ref[...] = c_ref[...] + part
            else:
                o_ref[...] = part

        @pl.when(kk > 0)
        def _():
            o_ref[...] += part

    in_specs = [pl.BlockSpec((tk, ta), lambda i, j, kk: (kk, i)), pl.BlockSpec((tk, tn), lambda i, j, kk: (kk, j))]
    args = [a, b]
    if has_acc:
        in_specs.append(pl.BlockSpec((ta, tn), lambda i, j, kk: (i, j)))
        args.append(acc)
    return _pcall(
        body, name=name, grid=(ka // ta, n // tn, nk), in_specs=in_specs,
        out_specs=pl.BlockSpec((ta, tn), lambda i, j, kk: (i, j)), out_shape=SDS((ka, n), F32),
        compiler_params=_cp(("parallel", "parallel", "arbitrary"), VMEM_BIG),
    )(*args)


def _modvec(cc, wm, bm):
    def body(c_ref, w_ref, b_ref, o_ref):
        o_ref[...] = _nn(_silu(c_ref[...]).astype(BF16), w_ref[...]) + b_ref[...]

    return _pcall(body, name="modvec", out_shape=SDS((8, 3 * D), F32), compiler_params=_cp(None, VMEM_BIG))(cc, wm, bm)


def _dcctx(dmodc, wm):
    def body(d_ref, w_ref, o_ref):
        o_ref[...] = _nt(d_ref[...].astype(BF16), w_ref[...])

    return _pcall(
        body, name="dcctx", grid=(1,), in_specs=[_full((8, 2 * D)), pl.BlockSpec((D, 2 * D), lambda i: (0, 0))],
        out_specs=_full((8, D)), out_shape=SDS((8, D), F32), compiler_params=_cp(("arbitrary",), VMEM_BIG),
    )(dmodc, wm)


def _prep_h(x, ng, scale, shift, name):
    m = x.shape[0]

    def body(x_ref, g_ref, sc_ref, sh_ref, h_ref):
        xf = x_ref[...]
        r = lax.rsqrt(jnp.mean(xf * xf, axis=-1, keepdims=True) + EPS)
        y = (xf * r) * g_ref[...]
        h_ref[...] = (y * (1.0 + sc_ref[...]) + sh_ref[...]).astype(BF16)

    row = pl.BlockSpec((TOK, D), lambda i: (i, 0))
    return _pcall(
        body, name=name, grid=(m // TOK,), in_specs=[row, _full((1, D)), _full((1, D)), _full((1, D))],
        out_specs=row, out_shape=SDS((m, D), BF16), compiler_params=_cp(("parallel",)),
    )(x, ng, scale, shift)


def _resident(shape):
    return pl.BlockSpec(shape, lambda *_: (0,) * len(shape), pipeline_mode=pl.Buffered(1))


PROJ_TM = 512


def _proj_fwd(x, ng, scale, shift, wit_g, wit_r, wlrt, ln_g, ln_b, share=()):
    m = x.shape[0]
    ns = len(share)
    steps = m // PROJ_TM
    src = [(0, 0), (0, D), (1, 2 * D), (1, 0), (1, D)]

    def body(*refs):
        x_ref, g_ref, sc_ref, sh_ref, wg_ref, wr_ref, wl_ref, lg_ref, lb_ref = refs[:9]
        share_refs = refs[9:9 + ns]
        h_ref, p_ref, plr_ref, vr_ref, vc_ref = refs[9 + ns:14 + ns]
        got_refs = refs[14 + ns:14 + 2 * ns]
        sems = refs[14 + 2 * ns:]

        def copies():
            cx, cy, cc = _coords()
            me = 2 * cx + cy
            peers = [(1 - cx, cy), (cx, 1 - cy), (1 - cx, 1 - cy)]
            out, back = [], []
            for k in range(ns):
                for j, (px, py) in enumerate(peers):
                    out.append(_remote(share_refs[k], got_refs[k].at[me], sems[0].at[3 * k + j], sems[1].at[3 * k + j], (px, py, cc)))
                    landed = got_refs[k].at[2 * px + py]
                    back.append(_remote(landed, landed, sems[0].at[3 * k + j], sems[1].at[3 * k + j], (px, py, cc)))
            return out, back

        if ns:
            @pl.when(pl.program_id(0) == 0)
            def _():
                for rc in copies()[0]:
                    rc.start()

            @pl.when(pl.program_id(0) == steps - 1)
            def _():
                out, back = copies()
                for rc in back:
                    rc.wait_recv()
                for rc in out:
                    rc.wait_send()

        xf = x_ref[...]
        r = lax.rsqrt(jnp.mean(xf * xf, axis=-1, keepdims=True) + EPS)
        y = (xf * r) * g_ref[...]
        h = (y * (1.0 + sc_ref[...]) + sh_ref[...]).astype(BF16)
        h_ref[...] = h
        for j, (which, r0) in enumerate(src):
            w_ref = wr_ref if which else wg_ref
            blk = _nt(h, w_ref[r0:r0 + D, :]).astype(BF16)
            p_ref[:, D * j:D * j + D] = blk
            if j == 4:
                xf = blk[:, 512:1024].astype(F32)
                xc = xf - jnp.mean(xf, axis=-1, keepdims=True)
                vn = (xc * lax.rsqrt(jnp.mean(xc * xc, axis=-1, keepdims=True) + EPS)) * lg_ref[...] + lb_ref[...]
                vr_ref[...] = vn[:, 0:256].astype(BF16)
                vc_ref[0] = vn[:, 256:384].astype(BF16)
                vc_ref[1] = vn[:, 384:512].astype(BF16)
        plr_ref[...] = _nt(h, wl_ref[...])

    row = pl.BlockSpec((PROJ_TM, D), lambda i: (i, 0))
    vec = _full((1, D))
    res = _pcall(
        body, name="proj_fwd", grid=(steps,),
        in_specs=[row, vec, vec, vec, _resident((2 * D, D)), _resident((3 * D, D)), _resident((LRW, D)), _full((1, 512)),
                  _full((1, 512))] + [ANY] * ns,
        out_specs=[row, pl.BlockSpec((PROJ_TM, NP), lambda i: (i, 0)), pl.BlockSpec((PROJ_TM, LRW), lambda i: (i, 0)),
                   pl.BlockSpec((PROJ_TM, 256), lambda i: (i, 0)), pl.BlockSpec((2, PROJ_TM, 128), lambda i: (0, i, 0))] + [ANY] * ns,
        out_shape=[SDS((m, D), BF16), SDS((m, NP), BF16), SDS((m, LRW), F32), SDS((m, 256), BF16), SDS((2, m, 128), BF16)]
        + [SDS((4,) + a.shape, a.dtype) for a in share],
        scratch_shapes=([pltpu.SemaphoreType.DMA((3 * ns,)), pltpu.SemaphoreType.DMA((3 * ns,))] if ns else []),
        compiler_params=_cp(("arbitrary",), VMEM_BIG),
    )(x, ng, scale, shift, wit_g, wit_r, wlrt, ln_g, ln_b, *share)
    return res[0], res[1], res[2], res[3], res[4], list(res[5:])


def _proj_bwd(dp_g, dp_r, dlr, wit_g, wit_r, wlrt, x, dx1, ng, scale, send=()):
    m = x.shape[0]
    ns = len(send)
    steps = m // PROJ_TM

    def body(*refs):
        (dpg_ref, dpr_ref, dlr_ref, wg_ref, wr_ref, wl_ref, x_ref, r_ref, g_ref, sc_ref) = refs[:10]
        send_refs = refs[10:10 + ns]
        dx_ref, dg_ref, dsc_ref, dsh_ref = refs[10 + ns:14 + ns]
        got_refs = refs[14 + ns:14 + 2 * ns]
        sems = refs[14 + 2 * ns:]
        i = pl.program_id(0)

        def copies():
            cx, cy, cc = _coords()
            me = 2 * cx + cy
            peers = [(1 - cx, cy), (cx, 1 - cy), (1 - cx, 1 - cy)]
            out, back = [], []
            for k in range(ns):
                for j, (px, py) in enumerate(peers):
                    out.append(_remote(send_refs[k].at[2 * px + py], got_refs[k].at[me], sems[0].at[3 * k + j],
                                       sems[1].at[3 * k + j], (px, py, cc)))
                    landed = got_refs[k].at[2 * px + py]
                    back.append(_remote(landed, landed, sems[0].at[3 * k + j], sems[1].at[3 * k + j], (px, py, cc)))
            return out, back

        @pl.when(i == 0)
        def _():
            dg_ref[...] = jnp.zeros_like(dg_ref)
            dsc_ref[...] = jnp.zeros_like(dsc_ref)
            dsh_ref[...] = jnp.zeros_like(dsh_ref)
            if ns:
                for rc in copies()[0]:
                    rc.start()

        dh_ = (_nn(dpg_ref[...], wg_ref[...]) + _nn(dpr_ref[...], wr_ref[...])
               + _nn(dlr_ref[...].astype(BF16), wl_ref[...]))
        xf = x_ref[...]
        r = lax.rsqrt(jnp.mean(xf * xf, axis=-1, keepdims=True) + EPS)
        xh = xf * r
        y = xh * g_ref[...]
        dsh_ref[...] += jnp.sum(dh_, axis=0, keepdims=True)
        dsc_ref[...] += jnp.sum(dh_ * y, axis=0, keepdims=True)
        dy = dh_ * (1.0 + sc_ref[...])
        dg_ref[...] += jnp.sum(dy * xh, axis=0, keepdims=True)
        dxh = dy * g_ref[...]
        dx_ref[...] = r * (dxh - xh * jnp.mean(dxh * xh, axis=-1, keepdims=True)) + r_ref[...]

        if ns:
            @pl.when(i == steps - 1)
            def _():
                out, back = copies()
                for rc in back:
                    rc.wait_recv()
                for rc in out:
                    rc.wait_send()

    row = pl.BlockSpec((PROJ_TM, D), lambda i: (i, 0))
    vec = _full((1, D))
    kg, kr = dp_g.shape[1], dp_r.shape[1]
    res = _pcall(
        body, name="proj_bwd", grid=(steps,),
        in_specs=[pl.BlockSpec((PROJ_TM, kg), lambda i: (i, 0)), pl.BlockSpec((PROJ_TM, kr), lambda i: (i, 0)),
                  pl.BlockSpec((PROJ_TM, LRW), lambda i: (i, 0)), _resident((kg, D)), _resident((kr, D)), _resident((LRW, D)),
                  row, row, vec, vec] + [ANY] * ns,
        out_specs=[row, vec, vec, vec] + [ANY] * ns,
        out_shape=[SDS((m, D), F32), SDS((1, D), F32), SDS((1, D), F32), SDS((1, D), F32)] + [SDS(a.shape, a.dtype) for a in send],
        scratch_shapes=([pltpu.SemaphoreType.DMA((3 * ns,)), pltpu.SemaphoreType.DMA((3 * ns,))] if ns else []),
        compiler_params=_cp(("arbitrary",), VMEM_BIG),
    )(dp_g, dp_r, dlr, wit_g, wit_r, wlrt, x, dx1, ng, scale, *send)
    return tuple(res[:4]), list(res[4:])


def _prep_bwd(x, dh, dx1, ng, scale, name):
    m = x.shape[0]
    has_res = dx1 is not None

    def body(*refs):
        if has_res:
            x_ref, dh_ref, r_ref, g_ref, sc_ref, dx_ref, dg_ref, dsc_ref, dsh_ref = refs
        else:
            x_ref, dh_ref, g_ref, sc_ref, dx_ref, dg_ref, dsc_ref, dsh_ref = refs
        i = pl.program_id(0)

        @pl.when(i == 0)
        def _():
            dg_ref[...] = jnp.zeros_like(dg_ref)
            dsc_ref[...] = jnp.zeros_like(dsc_ref)
            dsh_ref[...] = jnp.zeros_like(dsh_ref)

        xf = x_ref[...]
        dh_ = dh_ref[...]
        r = lax.rsqrt(jnp.mean(xf * xf, axis=-1, keepdims=True) + EPS)
        xh = xf * r
        y = xh * g_ref[...]
        dsh_ref[...] += jnp.sum(dh_, axis=0, keepdims=True)
        dsc_ref[...] += jnp.sum(dh_ * y, axis=0, keepdims=True)
        dy = dh_ * (1.0 + sc_ref[...])
        dg_ref[...] += jnp.sum(dy * xh, axis=0, keepdims=True)
        dxh = dy * g_ref[...]
        dx = r * (dxh - xh * jnp.mean(dxh * xh, axis=-1, keepdims=True))
        if has_res:
            dx = dx + r_ref[...]
        dx_ref[...] = dx

    row = pl.BlockSpec((TOK, D), lambda i: (i, 0))
    vec = _full((1, D))
    in_specs = [row, row] + ([row] if has_res else []) + [vec, vec]
    args = [x, dh] + ([dx1] if has_res else []) + [ng, scale]
    return _pcall(
        body, name=name, grid=(m // TOK,), in_specs=in_specs, out_specs=[row, vec, vec, vec],
        out_shape=[SDS((m, D), F32), SDS((1, D), F32), SDS((1, D), F32), SDS((1, D), F32)],
        compiler_params=_cp(("arbitrary",)),
    )(*args)


def _ln_fwd(p, ln_g, ln_b):
    m = p.shape[0]

    def body(va_ref, g_ref, b_ref, vr_ref, vc_ref):
        xf = va_ref[...].astype(F32)
        xc = xf - jnp.mean(xf, axis=-1, keepdims=True)
        y = xc * lax.rsqrt(jnp.mean(xc * xc, axis=-1, keepdims=True) + EPS)
        vn = y * g_ref[...] + b_ref[...]
        vr_ref[...] = vn[:, 0:256].astype(BF16)
        vc_ref[0] = vn[:, 256:384].astype(BF16)
        vc_ref[1] = vn[:, 384:512].astype(BF16)

    return _pcall(
        body, name="ln_fwd", grid=(m // TOK,),
        in_specs=[pl.BlockSpec((TOK, 512), lambda i: (i, 9)), _full((1, 512)), _full((1, 512))],
        out_specs=[pl.BlockSpec((TOK, 256), lambda i: (i, 0)), pl.BlockSpec((2, TOK, 128), lambda i: (0, i, 0))],
        out_shape=[SDS((m, 256), BF16), SDS((2, m, 128), BF16)], compiler_params=_cp(("parallel",)),
    )(p, ln_g, ln_b)


COLB = 2048


def _colmix_fwd(vnc, ws23, bs23):
    rows = vnc.shape[2] // COLB

    def body(v_ref, w_ref, b_ref, o_ref):
        o_ref[0] = _nn(w_ref[0], v_ref[0]) + b_ref[0]

    return _pcall(
        body, name="colmix_fwd", grid=(2, rows),
        in_specs=[pl.BlockSpec((1, AC, COLB), lambda g, j: (g, 0, j)), pl.BlockSpec((1, AC, AC), lambda g, j: (g, 0, 0)),
                  pl.BlockSpec((1, AC, 1), lambda g, j: (g, 0, 0))],
        out_specs=pl.BlockSpec((1, AC, COLB), lambda g, j: (g, 0, j)),
        out_shape=SDS(vnc.shape, F32), compiler_params=_cp(("parallel", "parallel")),
    )(vnc, ws23, bs23)


def _colmix_bwd(dsvc, vnc, ws23t):
    rows = vnc.shape[2] // COLB

    def body(d_ref, v_ref, wt_ref, dv_ref, dw_ref, db_ref):
        j = pl.program_id(1)

        @pl.when(j == 0)
        def _():
            dw_ref[...] = jnp.zeros_like(dw_ref)
            db_ref[...] = jnp.zeros_like(db_ref)

        d = d_ref[0]
        d16 = d.astype(BF16)
        dv_ref[0] = _nn(wt_ref[0], d16)
        dw_ref[0] += _nt(d16, v_ref[0])
        db_ref[0] += jnp.sum(d, axis=1, keepdims=True)

    blk = pl.BlockSpec((1, AC, COLB), lambda g, j: (g, 0, j))
    return _pcall(
        body, name="colmix_bwd", grid=(2, rows),
        in_specs=[blk, blk, pl.BlockSpec((1, AC, AC), lambda g, j: (g, 0, 0))],
        out_specs=[blk, pl.BlockSpec((1, AC, AC), lambda g, j: (g, 0, 0)), pl.BlockSpec((1, AC, 1), lambda g, j: (g, 0, 0))],
        out_shape=[SDS(vnc.shape, F32), SDS((2, AC, AC), F32), SDS((2, AC, 1), F32)],
        compiler_params=_cp(("parallel", "arbitrary")),
    )(dsvc, vnc, ws23t)


def _head_norm(o, gbn):
    out = []
    for h in range(4):
        oh = o[:, 128 * h:128 * h + 128]
        r = lax.rsqrt(jnp.mean(oh * oh, axis=-1, keepdims=True) + EPS)
        out.append((r, oh * r))
    return out


def _mid_fwd(o_f, o_b, p, vnr, svc, ws01, bs01, gbn):
    m = p.shape[0]

    def body(of_ref, ob_ref, zb_ref, ua_ref, za_ref, vnr_ref, svc_ref, w_ref, b_ref, g_ref, ya_ref, yb_ref, svr_ref):
        o = of_ref[...] + ob_ref[...]
        zb = zb_ref[...]
        parts = []
        for h, (r, xh) in enumerate(_head_norm(o, None)):
            parts.append(xh * g_ref[:, 128 * h:128 * h + 128])
        on = jnp.concatenate(parts, axis=1)
        yb_ref[...] = (on * _silu(zb)).astype(BF16)
        for j in range(TOK // AC):
            for g in range(2):
                sv = _nn(w_ref[g], vnr_ref[AC * j:AC * j + AC, AC * g:AC * g + AC]) + b_ref[g]
                svr_ref[AC * j:AC * j + AC, AC * g:AC * g + AC] = sv
        sz = _silu(za_ref[...])
        u = ua_ref[...]
        ya_ref[:, 0:256] = ((u[:, 0:256] * svr_ref[...]) * sz[:, 0:256]).astype(BF16)
        ya_ref[:, 256:384] = ((u[:, 256:384] * svc_ref[0]) * sz[:, 256:384]).astype(BF16)
        ya_ref[:, 384:512] = ((u[:, 384:512] * svc_ref[1]) * sz[:, 384:512]).astype(BF16)

    r512 = pl.BlockSpec((TOK, 512), lambda i: (i, 0))
    return _pcall(
        body, name="mid_fwd", grid=(m // TOK,),
        in_specs=[r512, r512, pl.BlockSpec((TOK, 512), lambda i: (i, 6)), pl.BlockSpec((TOK, 512), lambda i: (i, 7)),
                  pl.BlockSpec((TOK, 512), lambda i: (i, 8)), pl.BlockSpec((TOK, 256), lambda i: (i, 0)),
                  pl.BlockSpec((2, TOK, 128), lambda i: (0, i, 0)), _full((2, AC, AC)), _full((2, AC, 1)), _full((1, 512))],
        out_specs=[r512, r512, pl.BlockSpec((TOK, 256), lambda i: (i, 0))],
        out_shape=[SDS((m, 512), BF16), SDS((m, 512), BF16), SDS((m, 256), F32)],
        compiler_params=_cp(("parallel",)),
    )(o_f, o_b, p, p, p, vnr, svc, ws01, bs01, gbn)


def _merge_fwd(p, ya, yb):
    m = p.shape[0]

    def body(ga_ref, gb_ref, ya_ref, yb_ref, m_ref):
        m_ref[...] = (jax.nn.sigmoid(ga_ref[...]) * ya_ref[...] + jax.nn.sigmoid(gb_ref[...]) * yb_ref[...]).astype(BF16)

    row = pl.BlockSpec((TOK, D), lambda i: (i, 0))
    return _pcall(
        body, name="merge_fwd", grid=(m // TOK,),
        in_specs=[row, pl.BlockSpec((TOK, D), lambda i: (i, 1)), row, row], out_specs=row,
        out_shape=SDS((m, D), BF16), compiler_params=_cp(("parallel",)),
    )(p, p, ya, yb)


def _loss_head(x, out, tgt, gate, gf):
    m = x.shape[0]

    def body(x_ref, o_ref, t_ref, gate_ref, gf_ref, dx1_ref, dout_ref, loss_ref, dgate_ref, dgf_ref):
        i = pl.program_id(0)

        @pl.when(i == 0)
        def _():
            loss_ref[...] = jnp.zeros_like(loss_ref)
            dgate_ref[...] = jnp.zeros_like(dgate_ref)
            dgf_ref[...] = jnp.zeros_like(dgf_ref)

        out_ = o_ref[...]
        x1 = x_ref[...] + gate_ref[...] * out_
        r = lax.rsqrt(jnp.mean(x1 * x1, axis=-1, keepdims=True) + EPS)
        xh = x1 * r
        err = xh * gf_ref[...] - t_ref[...]
        loss_ref[...] += 0.5 * jnp.sum(jnp.mean(err * err, axis=-1, keepdims=True), axis=0, keepdims=True)
        dy = err * (1.0 / D)
        dgf_ref[...] += jnp.sum(dy * xh, axis=0, keepdims=True)
        dxh = dy * gf_ref[...]
        dx1 = r * (dxh - xh * jnp.mean(dxh * xh, axis=-1, keepdims=True))
        dx1_ref[...] = dx1
        dout_ref[...] = (gate_ref[...] * dx1).astype(BF16)
        dgate_ref[...] += jnp.sum(dx1 * out_, axis=0, keepdims=True)

    row = pl.BlockSpec((TOK, D), lambda i: (i, 0))
    vec = _full((1, D))
    return _pcall(
        body, name="loss_head", grid=(m // TOK,), in_specs=[row, row, row, vec, vec],
        out_specs=[row, row, _full((1, 128)), vec, vec],
        out_shape=[SDS((m, D), F32), SDS((m, D), BF16), SDS((1, 128), F32), SDS((1, D), F32), SDS((1, D), F32)],
        compiler_params=_cp(("arbitrary",)),
    )(x, out, tgt, gate, gf)


def _merge_bwd(dm, ya, yb, p):
    m = p.shape[0]

    def body(dm_ref, ya_ref, yb_ref, ga_ref, gb_ref, dya_ref, dyb_ref, dp_ref):
        dm_ = dm_ref[...]
        sa = jax.nn.sigmoid(ga_ref[...])
        sb = jax.nn.sigmoid(gb_ref[...])
        dya_ref[...] = (dm_ * sa).astype(BF16)
        dyb_ref[...] = (dm_ * sb).astype(BF16)
        dp_ref[:, 0:D] = (dm_ * ya_ref[...] * (sa * (1.0 - sa))).astype(BF16)
        dp_ref[:, D:2 * D] = (dm_ * yb_ref[...] * (sb * (1.0 - sb))).astype(BF16)

    row = pl.BlockSpec((TOK, D), lambda i: (i, 0))
    return _pcall(
        body, name="merge_bwd", grid=(m // TOK,),
        in_specs=[row, row, row, row, pl.BlockSpec((TOK, D), lambda i: (i, 1))],
        out_specs=[row, row, pl.BlockSpec((TOK, 2 * D), lambda i: (i, 0))],
        out_shape=[SDS((m, D), BF16), SDS((m, D), BF16), SDS((m, NP), BF16)],
        compiler_params=_cp(("parallel",)),
    )(dm, ya, yb, p, p)


def _mid_bwd(dya_in, dyb_in, p, svr, svc, o_f, o_b, gbn, dp):
    m = p.shape[0]

    def body(dya_ref, dyb_ref, zb_ref, ua_ref, za_ref, svr_ref, svc_ref, of_ref, ob_ref, g_ref, dpi_ref,
             dp_ref, dsr_ref, dsc_ref, do_ref, dg_ref):
        i = pl.program_id(0)

        @pl.when(i == 0)
        def _():
            dg_ref[...] = jnp.zeros_like(dg_ref)

        dya = dya_ref[...]
        u = ua_ref[...]
        za = za_ref[...]
        sz = _silu(za)
        sv = jnp.concatenate([svr_ref[...], svc_ref[0], svc_ref[1]], axis=1)
        dp_ref[:, 512:1024] = (dya * sv * sz).astype(BF16)
        dsv = dya * u * sz
        dsr_ref[...] = dsv[:, 0:256]
        dsc_ref[0] = dsv[:, 256:384]
        dsc_ref[1] = dsv[:, 384:512]
        dp_ref[:, 1024:1536] = (dya * u * sv * _dsilu(za)).astype(BF16)

        dyb = dyb_ref[...]
        zb = zb_ref[...]
        o = of_ref[...] + ob_ref[...]
        szb = _silu(zb)
        dszb = _dsilu(zb)
        for h, (r, xh) in enumerate(_head_norm(o, None)):
            sl = slice(128 * h, 128 * h + 128)
            gh = g_ref[:, sl]
            don = dyb[:, sl] * szb[:, sl]
            dp_ref[:, sl] = (dyb[:, sl] * (xh * gh) * dszb[:, sl]).astype(BF16)
            dg_ref[:, sl] += jnp.sum(don * xh, axis=0, keepdims=True)
            dxh = don * gh
            do_ref[:, sl] = r * (dxh - xh * jnp.mean(dxh * xh, axis=-1, keepdims=True))

    r512 = pl.BlockSpec((TOK, 512), lambda i: (i, 0))
    return _pcall(
        body, name="mid_bwd", grid=(m // TOK,),
        in_specs=[r512, r512, pl.BlockSpec((TOK, 512), lambda i: (i, 6)), pl.BlockSpec((TOK, 512), lambda i: (i, 7)),
                  pl.BlockSpec((TOK, 512), lambda i: (i, 8)), pl.BlockSpec((TOK, 256), lambda i: (i, 0)),
                  pl.BlockSpec((2, TOK, 128), lambda i: (0, i, 0)), r512, r512, _full((1, 512)),
                  pl.BlockSpec(memory_space=pl.ANY)],
        out_specs=[pl.BlockSpec((TOK, 1536), lambda i: (i, 2)), pl.BlockSpec((TOK, 256), lambda i: (i, 0)),
                   pl.BlockSpec((2, TOK, 128), lambda i: (0, i, 0)), r512, _full((1, 512))],
        out_shape=[SDS((m, NP), BF16), SDS((m, 256), F32), SDS((2, m, 128), F32), SDS((m, 512), F32), SDS((1, 512), F32)],
        input_output_aliases={10: 0}, compiler_params=_cp(("arbitrary",)),
    )(dya_in, dyb_in, p, p, p, svr, svc, o_f, o_b, gbn, dp)


def _tail_fwd(o_f, o_b, p, vnr, svc, x, tgt, ws01, bs01, gbn, wpa, wpb, wo, gate, gf):
    m = p.shape[0]

    def body(of_ref, ob_ref, zb_ref, ua_ref, za_ref, ga_ref, gb_ref, vnr_ref, svc_ref, x_ref, t_ref, w_ref, b_ref, g_ref,
             wpa_ref, wpb_ref, wo_ref, gate_ref, gf_ref,
             ya_ref, yb_ref, svr_ref, m_ref, dx1_ref, dout_ref, loss_ref, dgate_ref, dgf_ref):
        i = pl.program_id(0)

        @pl.when(i == 0)
        def _():
            loss_ref[...] = jnp.zeros_like(loss_ref)
            dgate_ref[...] = jnp.zeros_like(dgate_ref)
            dgf_ref[...] = jnp.zeros_like(dgf_ref)

        o = of_ref[...] + ob_ref[...]
        zb = zb_ref[...].astype(F32)
        for h, (r, xh) in enumerate(_head_norm(o, None)):
            sl = slice(128 * h, 128 * h + 128)
            yb_ref[:, sl] = ((xh * g_ref[:, sl]) * _silu(zb[:, sl])).astype(BF16)
        for j in range(TOK // AC):
            for g in range(2):
                sv = _nn(w_ref[g], vnr_ref[AC * j:AC * j + AC, AC * g:AC * g + AC]) + b_ref[g]
                svr_ref[AC * j:AC * j + AC, AC * g:AC * g + AC] = sv
        sz = _silu(za_ref[...].astype(F32))
        u = ua_ref[...].astype(F32)
        ya_ref[:, 0:256] = ((u[:, 0:256] * svr_ref[...]) * sz[:, 0:256]).astype(BF16)
        ya_ref[:, 256:384] = ((u[:, 256:384] * svc_ref[0]) * sz[:, 256:384]).astype(BF16)
        ya_ref[:, 384:512] = ((u[:, 384:512] * svc_ref[1]) * sz[:, 384:512]).astype(BF16)
        ya = _nn(ya_ref[...], wpa_ref[...])
        yb = _nn(yb_ref[...], wpb_ref[...])
        mg = (jax.nn.sigmoid(ga_ref[...].astype(F32)) * ya + jax.nn.sigmoid(gb_ref[...].astype(F32)) * yb).astype(BF16)
        m_ref[...] = mg
        out_ = _nn(mg, wo_ref[...])
        x1 = x_ref[...] + gate_ref[...] * out_
        r = lax.rsqrt(jnp.mean(x1 * x1, axis=-1, keepdims=True) + EPS)
        xh = x1 * r
        err = xh * gf_ref[...] - t_ref[...]
        loss_ref[...] += 0.5 * jnp.sum(jnp.mean(err * err, axis=-1, keepdims=True), axis=0, keepdims=True)
        dy = err * (1.0 / D)
        dgf_ref[...] += jnp.sum(dy * xh, axis=0, keepdims=True)
        dxh = dy * gf_ref[...]
        dx1 = r * (dxh - xh * jnp.mean(dxh * xh, axis=-1, keepdims=True))
        dx1_ref[...] = dx1
        dout_ref[...] = (gate_ref[...] * dx1).astype(BF16)
        dgate_ref[...] += jnp.sum(dx1 * out_, axis=0, keepdims=True)

    r512 = pl.BlockSpec((TOK, 512), lambda i: (i, 0))
    row = pl.BlockSpec((TOK, D), lambda i: (i, 0))
    vec = _full((1, D))
    return _pcall(
        body, name="tail_fwd", grid=(m // TOK,),
        in_specs=[r512, r512, pl.BlockSpec((TOK, 512), lambda i: (i, 6)), pl.BlockSpec((TOK, 512), lambda i: (i, 7)),
                  pl.BlockSpec((TOK, 512), lambda i: (i, 8)), row, pl.BlockSpec((TOK, D), lambda i: (i, 1)),
                  pl.BlockSpec((TOK, 256), lambda i: (i, 0)), pl.BlockSpec((2, TOK, 128), lambda i: (0, i, 0)), row, row,
                  _full((2, AC, AC)), _full((2, AC, 1)), _full((1, 512)), _resident((512, D)), _resident((512, D)),
                  _resident((D, D)), vec, vec],
        out_specs=[r512, r512, pl.BlockSpec((TOK, 256), lambda i: (i, 0)), row, row, row, _full((1, 128)), vec, vec],
        out_shape=[SDS((m, 512), BF16), SDS((m, 512), BF16), SDS((m, 256), F32), SDS((m, D), BF16), SDS((m, D), F32),
                   SDS((m, D), BF16), SDS((1, 128), F32), SDS((1, D), F32), SDS((1, D), F32)],
        compiler_params=_cp(("arbitrary",), VMEM_BIG),
    )(o_f, o_b, p, p, p, p, p, vnr, svc, x, tgt, ws01, bs01, gbn, wpa, wpb, wo, gate, gf)


DPR = 3072


def _tail_bwd(dout, ya_in, yb_in, p, svr, svc, o_f, o_b, gbn, wo, wpa, wpb):
    m = p.shape[0]

    def body(dout_ref, ya_ref, yb_ref, ga_ref, gb_ref, zb_ref, ua_ref, za_ref, svr_ref, svc_ref, of_ref, ob_ref, g_ref,
             wo_ref, wpa_ref, wpb_ref,
             dya_ref, dyb_ref, dpg_ref, dpr_ref, dsr_ref, dsc_ref, do_ref, dg_ref):
        i = pl.program_id(0)

        @pl.when(i == 0)
        def _():
            dg_ref[...] = jnp.zeros_like(dg_ref)

        dm_ = _nt(dout_ref[...], wo_ref[...])
        ya = _nn(ya_ref[...], wpa_ref[...])
        yb = _nn(yb_ref[...], wpb_ref[...])
        sa = jax.nn.sigmoid(ga_ref[...].astype(F32))
        sb = jax.nn.sigmoid(gb_ref[...].astype(F32))
        dya16 = (dm_ * sa).astype(BF16)
        dyb16 = (dm_ * sb).astype(BF16)
        dya_ref[...] = dya16
        dyb_ref[...] = dyb16
        dpg_ref[:, 0:D] = (dm_ * ya * (sa * (1.0 - sa))).astype(BF16)
        dpg_ref[:, D:2 * D] = (dm_ * yb * (sb * (1.0 - sb))).astype(BF16)
        dya = _nt(dya16, wpa_ref[...])
        dyb = _nt(dyb16, wpb_ref[...])

        u = ua_ref[...].astype(F32)
        za = za_ref[...].astype(F32)
        sz, dsz = _silu_and_grad(za)
        sv = jnp.concatenate([svr_ref[...], svc_ref[0], svc_ref[1]], axis=1)
        dpr_ref[:, 512:1024] = (dya * sv * sz).astype(BF16)
        dsv = dya * u * sz
        dsr_ref[...] = dsv[:, 0:256]
        dsc_ref[0] = dsv[:, 256:384]
        dsc_ref[1] = dsv[:, 384:512]
        dpr_ref[:, 1024:1536] = (dya * u * sv * dsz).astype(BF16)

        zb = zb_ref[...].astype(F32)
        o = of_ref[...] + ob_ref[...]
        szb, dszb = _silu_and_grad(zb)
        for h, (r, xh) in enumerate(_head_norm(o, None)):
            sl = slice(128 * h, 128 * h + 128)
            gh = g_ref[:, sl]
            don = dyb[:, sl] * szb[:, sl]
            dpr_ref[:, sl] = (dyb[:, sl] * (xh * gh) * dszb[:, sl]).astype(BF16)
            dg_ref[:, sl] += jnp.sum(don * xh, axis=0, keepdims=True)
            dxh = don * gh
            do_ref[:, sl] = r * (dxh - xh * jnp.mean(dxh * xh, axis=-1, keepdims=True))

    r512 = pl.BlockSpec((TOK, 512), lambda i: (i, 0))
    row = pl.BlockSpec((TOK, D), lambda i: (i, 0))
    return _pcall(
        body, name="tail_bwd", grid=(m // TOK,),
        in_specs=[row, r512, r512, row, pl.BlockSpec((TOK, D), lambda i: (i, 1)), pl.BlockSpec((TOK, 512), lambda i: (i, 6)),
                  pl.BlockSpec((TOK, 512), lambda i: (i, 7)), pl.BlockSpec((TOK, 512), lambda i: (i, 8)),
                  pl.BlockSpec((TOK, 256), lambda i: (i, 0)), pl.BlockSpec((2, TOK, 128), lambda i: (0, i, 0)), r512, r512,
                  _full((1, 512)), _resident((D, D)), _resident((512, D)), _resident((512, D))],
        out_specs=[row, row, pl.BlockSpec((TOK, 2 * D), lambda i: (i, 0)), pl.BlockSpec((TOK, 1536), lambda i: (i, 0)),
                   pl.BlockSpec((TOK, 256), lambda i: (i, 0)), pl.BlockSpec((2, TOK, 128), lambda i: (0, i, 0)), r512, _full((1, 512))],
        out_shape=[SDS((m, D), BF16), SDS((m, D), BF16), SDS((m, 2 * D), BF16), SDS((m, DPR), BF16), SDS((m, 256), F32),
                   SDS((2, m, 128), F32), SDS((m, 512), F32), SDS((1, 512), F32)],
        compiler_params=_cp(("arbitrary",), VMEM_BIG),
    )(dout, ya_in, yb_in, p, p, p, p, p, svr, svc, o_f, o_b, gbn, wo, wpa, wpb)


def _mm_multi(pairs, *, tm, tn, out_dtype, name):
    m = pairs[0][0].shape[0]
    n = pairs[0][1].shape[1]
    nks = [a.shape[1] // tk for a, _, tk in pairs]
    starts = [sum(nks[:i]) for i in range(len(pairs))]
    total = sum(nks)

    def body(*refs):
        o_ref, acc_ref = refs[-2], refs[-1]
        kk = pl.program_id(2)
        for idx in range(len(pairs)):
            a_ref, b_ref = refs[2 * idx], refs[2 * idx + 1]

            @pl.when((kk >= starts[idx]) & (kk < starts[idx] + nks[idx]))
            def _(a_ref=a_ref, b_ref=b_ref, first=(idx == 0)):
                part = _nn(a_ref[...].astype(BF16), b_ref[...].astype(BF16))
                if first:
                    @pl.when(kk == 0)
                    def _():
                        acc_ref[...] = part

                    @pl.when(kk > 0)
                    def _():
                        acc_ref[...] += part
                else:
                    acc_ref[...] += part

        @pl.when(kk == total - 1)
        def _():
            o_ref[...] = acc_ref[...].astype(out_dtype)

    in_specs, args = [], []
    for (a, b, tk), st, nk in zip(pairs, starts, nks):
        in_specs.append(pl.BlockSpec((tm, tk), lambda i, j, kk, st=st, nk=nk: (i, jnp.clip(kk - st, 0, nk - 1))))
        in_specs.append(pl.BlockSpec((tk, tn), lambda i, j, kk, st=st, nk=nk: (jnp.clip(kk - st, 0, nk - 1), j)))
        args += [a, b]
    return _pcall(
        body, name=name, grid=(m // tm, n // tn, total), in_specs=in_specs,
        out_specs=pl.BlockSpec((tm, tn), lambda i, j, kk: (i, j)), out_shape=SDS((m, n), out_dtype),
        scratch_shapes=[pltpu.VMEM((tm, tn), F32)], compiler_params=_cp(("parallel", "parallel", "arbitrary"), VMEM_BIG),
    )(*args)


def _ln_bwd(dsr, vnr, dvnc, p, ws01t, ln_g, dp):
    m = p.shape[0]

    def body(dsr_ref, vnr_ref, dvc_ref, va_ref, wt_ref, g_ref, dpi_ref, dp_ref, dw_ref, db_ref, dlg_ref, dlb_ref, dvn_ref):
        i = pl.program_id(0)

        @pl.when(i == 0)
        def _():
            dw_ref[...] = jnp.zeros_like(dw_ref)
            db_ref[...] = jnp.zeros_like(db_ref)
            dlg_ref[...] = jnp.zeros_like(dlg_ref)
            dlb_ref[...] = jnp.zeros_like(dlb_ref)

        for j in range(TOK // AC):
            for g in range(2):
                d = dsr_ref[AC * j:AC * j + AC, AC * g:AC * g + AC]
                d16 = d.astype(BF16)
                dvn_ref[AC * j:AC * j + AC, AC * g:AC * g + AC] = _nn(wt_ref[g], d16)
                dw_ref[g] += _nt(d16, vnr_ref[AC * j:AC * j + AC, AC * g:AC * g + AC])
                db_ref[g] += jnp.sum(d, axis=1, keepdims=True)
        dvn_ref[:, 256:384] = dvc_ref[0]
        dvn_ref[:, 384:512] = dvc_ref[1]
        dvn = dvn_ref[...]
        xf = va_ref[...].astype(F32)
        xc = xf - jnp.mean(xf, axis=-1, keepdims=True)
        rs = lax.rsqrt(jnp.mean(xc * xc, axis=-1, keepdims=True) + EPS)
        xh = xc * rs
        dlg_ref[...] += jnp.sum(dvn * xh, axis=0, keepdims=True)
        dlb_ref[...] += jnp.sum(dvn, axis=0, keepdims=True)
        dxh = dvn * g_ref[...]
        dva = rs * (dxh - jnp.mean(dxh, axis=-1, keepdims=True) - xh * jnp.mean(dxh * xh, axis=-1, keepdims=True))
        dp_ref[...] = dva.astype(BF16)

    return _pcall(
        body, name="ln_bwd", grid=(m // TOK,),
        in_specs=[pl.BlockSpec((TOK, 256), lambda i: (i, 0)), pl.BlockSpec((TOK, 256), lambda i: (i, 0)),
                  pl.BlockSpec((2, TOK, 128), lambda i: (0, i, 0)), pl.BlockSpec((TOK, 512), lambda i: (i, 9)),
                  _full((2, AC, AC)), _full((1, 512)), pl.BlockSpec(memory_space=pl.ANY)],
        out_specs=[pl.BlockSpec((TOK, 512), lambda i: (i, 3)), _full((2, AC, AC)), _full((2, AC, 1)), _full((1, 512)), _full((1, 512))],
        out_shape=[SDS((m, DPR), BF16), SDS((2, AC, AC), F32), SDS((2, AC, 1), F32), SDS((1, 512), F32), SDS((1, 512), F32)],
        scratch_shapes=[pltpu.VMEM((TOK, 512), F32)],
        input_output_aliases={6: 0}, compiler_params=_cp(("arbitrary",)),
    )(dsr, vnr, dvnc, p, ws01t, ln_g, dp)


def _tri_mm(tri, a):
    a1 = a.astype(BF16)
    r1 = a - a1.astype(F32)
    a2 = r1.astype(BF16)
    a3 = (r1 - a2.astype(F32)).astype(BF16)
    n = a.shape[1]
    r = _nn(tri, jnp.concatenate([a1, a2, a3], axis=1))
    return r[:, 0:n] + r[:, n:2 * n] + r[:, 2 * n:3 * n]


def _gla_masks(reverse):
    ri = lax.broadcasted_iota(jnp.int32, (CH, CH), 0)
    ci = lax.broadcasted_iota(jnp.int32, (CH, CH), 1)
    vis = (ci >= ri) if reverse else (ci <= ri)
    vis_t = (ci <= ri) if reverse else (ci >= ri)
    r4 = lax.broadcasted_iota(jnp.int32, (4 * CH, CH), 0) & (CH - 1)
    c4 = lax.broadcasted_iota(jnp.int32, (4 * CH, CH), 1)
    vis4 = (c4 >= r4) if reverse else (c4 <= r4)
    vis4_t = (c4 <= r4) if reverse else (c4 >= r4)
    lane = lax.broadcasted_iota(jnp.int32, (1, 256), 1)
    hm = [(lane >= CH * h) & (lane < CH * h + CH) for h in range(4)]
    return vis, vis_t, vis4, vis4_t, hm


def _stack_heads(x, hm):
    return jnp.concatenate([jnp.where(hm[h], x, 0.0).astype(BF16) for h in range(4)], axis=0)


def _diag_heads(full, hm):
    r = full.shape[0] // 4
    acc = jnp.where(hm[0], full[0:r], 0.0)
    for h in range(1, 4):
        acc = acc + jnp.where(hm[h], full[r * h:r * h + r], 0.0)
    return acc


def _rows_of_heads(x):
    return jnp.concatenate([x[:, 128 * h:128 * h + 128] for h in range(4)], axis=0)


def _lane_vis(reverse, transpose):
    ri = lax.broadcasted_iota(jnp.int32, (CH, 4 * CH), 0)
    ci = lax.broadcasted_iota(jnp.int32, (CH, 4 * CH), 1) & (CH - 1)
    return (ci >= ri) if (reverse != transpose) else (ci <= ri)


def _gla_fwd(p, qkv_blk, lr, lrw, gbias, s0, *, reverse, name):
    m = p.shape[0]
    tb = min(GLA_TB, m)
    nb = m // tb
    nc = tb // CH
    rmap = (lambda i: nb - 1 - i) if reverse else (lambda i: i)

    def body(qkv_ref, lr_ref, lrw_ref, gb_ref, s0_ref, o_ref, sb_ref, sfin_ref, st_ref):
        i = pl.program_id(0)

        @pl.when(i == 0)
        def _():
            st_ref[...] = s0_ref[...]

        vis, _, vis4, _, hm = _gla_masks(reverse)
        tri = vis.astype(F32).astype(BF16)
        logits = _nn(lr_ref[...].astype(BF16), lrw_ref[...]) + gb_ref[...]
        a_all = _logsig(logits) * (1.0 / 16.0)
        st = st_ref[...]
        for c in (range(nc - 1, -1, -1) if reverse else range(nc)):
            rows = slice(CH * c, CH * c + CH)
            b = _tri_mm(tri, a_all[rows])
            bl = b[0:1] if reverse else b[CH - 1:CH]
            q = qkv_ref[rows, 0:256].astype(F32) * 0.125
            k = qkv_ref[rows, 256:512].astype(F32)
            v16 = qkv_ref[rows, 512:1024].astype(BF16)
            qd = q * jnp.exp(b)
            kd16 = (k * jnp.exp(-b)).astype(BF16)
            kdec16 = (k * jnp.exp(bl - b)).astype(BF16)
            qstack = _stack_heads(qd, hm)
            sc = jnp.where(vis4, _nt(qstack, kd16), 0.0).astype(BF16)
            inter = _nt(qstack, st.astype(BF16))
            for h in range(4):
                o_ref[rows, 128 * h:128 * h + 128] = (
                    _nn(sc[CH * h:CH * h + CH], v16[:, 128 * h:128 * h + 128]) + inter[CH * h:CH * h + CH])
            sb_ref[c] = st
            st = st * jnp.exp(bl) + _diag_heads(_tn(v16, kdec16), hm)
        st_ref[...] = st

        @pl.when(i == nb - 1)
        def _():
            sfin_ref[...] = st

    return _pcall(
        body, name=name, grid=(nb,),
        in_specs=[pl.BlockSpec((tb,1024), lambda i: (rmap(i), qkv_blk)), pl.BlockSpec((tb,LRW), lambda i: (rmap(i), 0)),
                  _full((LRW, 256)), _full((1, 256)), _full((128, 256))],
        out_specs=[pl.BlockSpec((tb,512), lambda i: (rmap(i), 0)), pl.BlockSpec((nc, 128, 256), lambda i: (rmap(i), 0, 0)),
                   _full((128, 256))],
        out_shape=[SDS((m, 512), F32), SDS((m // CH, 128, 256), F32), SDS((128, 256), F32)],
        scratch_shapes=[pltpu.VMEM((128, 256), F32)], compiler_params=_cp(("arbitrary",)),
    )(p, lr, lrw, gbias, s0)


def _gla_bwd(p, qkv_blk, lr, lrw, lrwt, gbias, sb, dsfin, do, prev, dp, *, reverse, name):
    m = p.shape[0]
    tb = min(GLA_TB, m)
    nb = m // tb
    nc = tb // CH
    rmap = (lambda i: i) if reverse else (lambda i: nb - 1 - i)
    has_prev = prev is not None
    has_dp = dp is not None

    def body(*refs):
        refs = list(refs)
        qkv_ref, lr_ref, lrw_ref, lrwt_ref, gb_ref, sb_ref, dsfin_ref, do_ref = refs[:8]
        refs = refs[8:]
        if has_prev:
            pq_ref, plr_ref = refs[:2]
            refs = refs[2:]
        if has_dp:
            refs = refs[1:]
        dqkv_ref, dlr_ref, dw2_ref, dgb_ref, ds0_ref, dst_ref, dlog_ref = refs
        i = pl.program_id(0)

        @pl.when(i == 0)
        def _():
            dst_ref[...] = dsfin_ref[...]
            dw2_ref[...] = jnp.zeros_like(dw2_ref)
            dgb_ref[...] = jnp.zeros_like(dgb_ref)

        vis, vis_t, vis4, vis4_t, hm = _gla_masks(reverse)
        tri = vis.astype(F32).astype(BF16)
        tri_t = vis_t.astype(F32).astype(BF16)
        lane_vis = _lane_vis(reverse, False)
        lane_vis_t = _lane_vis(reverse, True)
        lr16 = lr_ref[...].astype(BF16)
        logits = _nn(lr16, lrw_ref[...]) + gb_ref[...]
        a_all = _logsig(logits) * (1.0 / 16.0)
        dsig = (1.0 - jax.nn.sigmoid(logits)) * (1.0 / 16.0)
        dst = dst_ref[...]
        for c in (range(nc) if reverse else range(nc - 1, -1, -1)):
            rows = slice(CH * c, CH * c + CH)
            b = _tri_mm(tri, a_all[rows])
            bl = b[0:1] if reverse else b[CH - 1:CH]
            eb = jnp.exp(b)
            enb = jnp.exp(-b)
            ebl = jnp.exp(bl - b)
            el = jnp.exp(bl)
            q = qkv_ref[rows, 0:256].astype(F32) * 0.125
            k = qkv_ref[rows, 256:512].astype(F32)
            v16 = qkv_ref[rows, 512:1024].astype(BF16)
            do16 = do_ref[rows, :].astype(BF16)
            qd = q * eb
            kd = k * enb
            kdec = k * ebl
            st = sb_ref[c]
            st16 = st.astype(BF16)
            dst16 = dst.astype(BF16)
            qd16 = qd.astype(BF16)
            kd16 = kd.astype(BF16)
            qstack = _stack_heads(qd, hm)
            kstack = _stack_heads(kd, hm)
            kdecstack = _stack_heads(kdec, hm)
            pt = jnp.where(vis4_t, _nt(kstack, qd16), 0.0).astype(BF16)
            dvinter = _nt(kdecstack, dst16)
            do_rows = _rows_of_heads(do16)
            v_rows = _rows_of_heads(v16)
            dp_cat = jnp.where(lane_vis, _diag_heads(_nt(do_rows, v_rows), hm), 0.0).astype(BF16)
            dpt_cat = jnp.where(lane_vis_t, _diag_heads(_nt(v_rows, do_rows), hm), 0.0).astype(BF16)
            dqd = _nn(dp_cat, kstack) + _diag_heads(_nn(do_rows, st16), hm)
            dkd = _nn(dpt_cat, qstack)
            dkdec = _diag_heads(_nn(v_rows, dst16), hm)
            for h in range(4):
                rh = slice(CH * h, CH * h + CH)
                dv_h = _nn(pt[rh], do_rows[rh]) + dvinter[rh]
                if has_prev:
                    dv_h = dv_h + pq_ref[rows, 512 + 128 * h:512 + 128 * h + 128]
                dqkv_ref[rows, 512 + 128 * h:512 + 128 * h + 128] = dv_h.astype(dqkv_ref.dtype)
            dq = dqd * eb * 0.125
            dk = dkd * enb + dkdec * ebl
            if has_prev:
                dq = dq + pq_ref[rows, 0:256]
                dk = dk + pq_ref[rows, 256:512]
            dqkv_ref[rows, 0:256] = dq.astype(dqkv_ref.dtype)
            dqkv_ref[rows, 256:512] = dk.astype(dqkv_ref.dtype)
            g_kdec = dkdec * kdec
            db = dqd * qd - dkd * kd - g_kdec
            dbl = jnp.sum(g_kdec, axis=0, keepdims=True) + jnp.sum(st * dst, axis=0, keepdims=True) * el
            da = _tri_mm(tri_t, db) + dbl
            dlog_ref[rows, :] = da * dsig[rows]
            dst = dst * el + _diag_heads(_tn(do16, qd16), hm)
        dst_ref[...] = dst
        dlog = dlog_ref[...]
        dlog16 = dlog.astype(BF16)
        dlr = _nn(dlog16, lrwt_ref[...])
        if has_prev:
            dlr = dlr + plr_ref[...]
        dlr_ref[...] = dlr
        dw2_ref[...] += _tn(lr16, dlog16)
        dgb_ref[...] += jnp.sum(dlog, axis=0, keepdims=True)

        @pl.when(i == nb - 1)
        def _():
            ds0_ref[...] = dst

    in_specs = [pl.BlockSpec((tb,1024), lambda i: (rmap(i), qkv_blk)), pl.BlockSpec((tb,LRW), lambda i: (rmap(i), 0)),
                _full((LRW, 256)), _full((256, LRW)), _full((1, 256)), pl.BlockSpec((nc, 128, 256), lambda i: (rmap(i), 0, 0)),
                _full((128, 256)), pl.BlockSpec((tb,512), lambda i: (rmap(i), 0))]
    args = [p, lr, lrw, lrwt, gbias, sb, dsfin, do]
    if has_prev:
        in_specs += [pl.BlockSpec((tb,1024), lambda i: (rmap(i), 0)), pl.BlockSpec((tb,LRW), lambda i: (rmap(i), 0))]
        args += list(prev)
    aliases = {}
    if has_dp:
        in_specs.append(pl.BlockSpec(memory_space=pl.ANY))
        aliases = {len(args): 0}
        args.append(dp)
        dq_spec = pl.BlockSpec((tb,1024), lambda i: (rmap(i), 2))
        dq_shape = SDS(dp.shape, dp.dtype)
    else:
        dq_spec = pl.BlockSpec((tb,1024), lambda i: (rmap(i), 0))
        dq_shape = SDS((m, 1024), F32)
    return _pcall(
        body, name=name, grid=(nb,), in_specs=in_specs,
        out_specs=[dq_spec, pl.BlockSpec((tb,LRW), lambda i: (rmap(i), 0)), _full((LRW, 256)), _full((1, 256)), _full((128, 256))],
        out_shape=[dq_shape, SDS((m, LRW), F32), SDS((LRW, 256), F32), SDS((1, 256), F32), SDS((128, 256), F32)],
        scratch_shapes=[pltpu.VMEM((128, 256), F32), pltpu.VMEM((tb,256), F32)],
        input_output_aliases=aliases, compiler_params=_cp(("arbitrary",)),
    )(*args)


def _device_step(x, c, ctx, c_ctx, tgt, wm, bm, ng, wit_g, wit_r, wlrt, ln_g, ln_b, ws, bs, w2, gb2, gbn, wpa, wpb, wo, gf,
                 exchange=None, shards=None):
    L = x.shape[0]
    wit_qkv = wit_r[2048:3072]
    ws16 = ws.astype(BF16)
    wst16 = jnp.swapaxes(ws, 1, 2).astype(BF16)
    bscol = bs[:, :, None]
    lrw = [jnp.zeros((LRW, 256), F32).at[16 * r:16 * r + 16].set(w2[r]).astype(BF16) for r in range(2)]
    lrwt = [w.T for w in lrw]
    gbias = [gb2[r:r + 1] for r in range(2)]

    cc = jnp.zeros((8, D), F32).at[0:1].set(c).at[1:2].set(c_ctx)
    mod = _modvec(cc, wm, bm)
    shift, scale, gate = mod[0:1, 0:D], mod[0:1, D:2 * D], mod[0:1, 2 * D:3 * D]
    shift_c, scale_c = mod[1:2, 0:D], mod[1:2, D:2 * D]

    hc = _prep_h(ctx, ng, scale_c, shift_c, "prep_hc")
    pc = _mm(hc, wit_qkv, tm=256, tn=1024, tk=D, out_dtype=F32, name="mm_pc", b_t=True)
    plrc = _mm(hc, wlrt, tm=256, tn=LRW, tk=D, out_dtype=F32, name="mm_plrc", b_t=True)
    zero_s = jnp.zeros((128, 256), F32)
    _, sbc_f, sc_f = _gla_fwd(pc, 0, plrc, lrw[0], gbias[0], zero_s, reverse=False, name="gla_fwd_cf")
    _, sbc_b, sc_b = _gla_fwd(pc, 0, plrc, lrw[1], gbias[1], zero_s, reverse=True, name="gla_fwd_cb")

    h, p, plr, vnr, vnc, late = _proj_fwd(x, ng, scale, shift, wit_g, wit_r, wlrt, ln_g, ln_b,
                                          shards if shards is not None else ())
    if shards is not None:
        me_xy = 2 * lax.axis_index("x") + lax.axis_index("y")
        g_wpa, g_wpb, g_wo = (_own(g, s_, me_xy) for g, s_ in zip(late, shards))
        wpa = jnp.swapaxes(g_wpa, 0, 1).reshape(512, D)
        wpb = jnp.swapaxes(g_wpb, 0, 1).reshape(512, D)
        wo = g_wo.reshape(D, D)
    o_f, sb_f, _ = _gla_fwd(p, 2, plr, lrw[0], gbias[0], sc_f, reverse=False, name="gla_fwd_f")
    o_b, sb_b, _ = _gla_fwd(p, 2, plr, lrw[1], gbias[1], sc_b, reverse=True, name="gla_fwd_b")
    svc = _colmix_fwd(vnc.reshape(2, AC, L), ws16[2:4], bscol[2:4]).reshape(2, L, 128)
    ya_in, yb_in, svr, mrg, dx1, dout, loss, dgate, dgf = _tail_fwd(
        o_f, o_b, p, vnr, svc, x, tgt, ws16[0:2], bscol[0:2], gbn, wpa, wpb, wo, gate, gf)

    dya, dyb, dp_g, dp, dsr, dsc, do, dgbn = _tail_bwd(dout, ya_in, yb_in, p, svr, svc, o_f, o_b, gbn, wo, wpa, wpb)
    dwo = _mm_tn(mrg, dout, ta=D, tn=D, tk=1024, name="mm_dwo")
    dwpa = _mm_tn(ya_in, dya, ta=512, tn=D, tk=1024, name="mm_dwpa")
    dwpb = _mm_tn(yb_in, dyb, ta=512, tn=D, tk=1024, name="mm_dwpb")
    dvnc, dws23, dbs23 = _colmix_bwd(dsc.reshape(2, AC, L), vnc.reshape(2, AC, L), wst16[2:4])
    dp, dws01, dbs01, dlng, dlnb = _ln_bwd(dsr, vnr, dvnc.reshape(2, L, 128), p, wst16[0:2], ln_g, dp)
    zero_ds = jnp.zeros((128, 256), F32)
    dqkv_f, dlr_f, dw2_f, dgb_f, ds0_f = _gla_bwd(p, 2, plr, lrw[0], lrwt[0], gbias[0], sb_f, zero_ds, do, None, None,
                                                  reverse=False, name="gla_bwd_f")
    dp, dlr, dw2_b, dgb_b, ds0_b = _gla_bwd(p, 2, plr, lrw[1], lrwt[1], gbias[1], sb_b, zero_ds, do, (dqkv_f, dlr_f), dp,
                                            reverse=True, name="gla_bwd_b")
    zero_do = jnp.zeros((ctx.shape[0], 512), F32)
    dqkvc_f, dlrc_f, dw2c_f, dgbc_f, _ = _gla_bwd(pc, 0, plrc, lrw[0], lrwt[0], gbias[0], sbc_f, ds0_f, zero_do, None, None,
                                                  reverse=False, name="gla_bwd_cf")
    dqkvc, dlrc, dw2c_b, dgbc_b, _ = _gla_bwd(pc, 0, plrc, lrw[1], lrwt[1], gbias[1], sbc_b, ds0_b, zero_do,
                                              (dqkvc_f, dlrc_f), None, reverse=True, name="gla_bwd_cb")
    dhc = _mm(dqkvc, wit_qkv, tm=256, tn=D, tk=1024, out_dtype=F32, name="mm_dhc")
    dhc = _mm(dlrc, wlrt, tm=256, tn=D, tk=LRW, out_dtype=F32, name="mm_dhc_lr", acc=dhc)
    _, dng_c, dscale_c, dshift_c = _prep_bwd(ctx, dhc, None, ng, scale_c, "prep_bwd_c")

    dwit_g = _mm_tn(dp_g, h, ta=1024, tn=D, tk=2048, name="mm_dwi_g")
    dwit_r = _mm_tn(dp, h, ta=1024, tn=D, tk=2048, name="mm_dwi_r")
    dwit_qkv = _mm_tn(dqkvc, hc, ta=1024, tn=D, tk=256, name="mm_dwi_c", acc=dwit_r[2048:3072])
    dwlrt = _mm_tn(dlr, h, ta=LRW, tn=D, tk=2048, name="mm_dwlr")
    dwlrt = _mm_tn(dlrc, hc, ta=LRW, tn=D, tk=256, name="mm_dwlr_c", acc=dwlrt)
    big = dict(dwit_g=dwit_g, dwit_r=dwit_r, dwit_qkv=dwit_qkv, dwlrt=dwlrt, dwpa=dwpa, dwpb=dwpb, dwo=dwo)

    send = exchange(big) if exchange is not None else ()
    (dx, dng, dscale, dshift), got = _proj_bwd(dp_g, dp, dlr, wit_g, wit_r, wlrt, x, dx1, ng, scale, send)

    dmodc = jnp.concatenate([dshift_c, dscale_c], axis=1)
    dscc = _dcctx(jnp.zeros((8, 2 * D), F32).at[0:1].set(dmodc), wm)[0:1]
    dw2p = dw2_f + dw2c_f, dw2_b + dw2c_b
    return dict(
        loss=loss[0, 0], dx=dx, got=got, **big,
        dmod=jnp.concatenate([dshift, dscale, dgate], axis=1), dmodc=dmodc, dscc=dscc, dng=dng + dng_c,
        dlng=dlng, dlnb=dlnb, dws=jnp.concatenate([dws01, dws23], axis=0),
        dbs=jnp.concatenate([dbs01, dbs23], axis=0)[:, :, 0], dgbn=dgbn, dgf=dgf,
        dw2=jnp.stack([dw2p[0][0:16], dw2p[1][16:32]]), dgb2=jnp.concatenate([dgb_f + dgbc_f, dgb_b + dgbc_b], axis=0),
    )


ANY = pl.BlockSpec(memory_space=pl.ANY)


def _coords():
    return lax.axis_index("x"), lax.axis_index("y"), lax.axis_index("c")


def _flip(v, bit):
    return 1 - v if bit else v


def _remote(src, dst, send_sem, recv_sem, dev):
    return pltpu.make_async_remote_copy(src_ref=src, dst_ref=dst, send_sem=send_sem, recv_sem=recv_sem,
                                        device_id=dev, device_id_type=MESH)


def _own(out, block, idx):
    return lax.dynamic_update_slice_in_dim(out, block[None], idx, axis=0)


def _half_idx(shape, axis, which, lead=()):
    idx = [pl.ds(0, d) for d in shape]
    h = shape[axis] // 2
    idx[axis] = pl.ds(which * h, h)
    return tuple(lead) + tuple(idx)


def _gather_weights(split, whole, name):
    ns, nw = len(split), len(whole)
    n = ns + nw
    arrs = [a for a, _ in split] + list(whole)

    def body(*refs):
        ins, outs = refs[:n], refs[n:2 * n]
        a_send, a_recv, b_send, b_recv = refs[2 * n:]
        x, y, c = _coords()
        me = 2 * x + y
        sib = (x, y, 1 - c)
        peers = [(1 - x, y), (x, 1 - y), (1 - x, 1 - y)]

        def half(k, slot, which):
            return outs[k].at[_half_idx(arrs[k].shape, split[k][1], which, lead=(slot,))]

        sends = []
        for k in range(n):
            for j, (px, py) in enumerate(peers):
                if k < ns:
                    rc = _remote(ins[k].at[_half_idx(arrs[k].shape, split[k][1], c)], half(k, me, c), a_send.at[3 * k + j],
                                 a_recv.at[3 * k + j], (px, py, c))
                else:
                    rc = _remote(ins[k], outs[k].at[me], a_send.at[3 * k + j], a_recv.at[3 * k + j], (px, py, c))
                rc.start()
                sends.append(rc)
        for k in range(ns):
            for j, (px, py) in enumerate(peers):
                landed = half(k, 2 * px + py, c)
                _remote(landed, landed, a_send.at[3 * k + j], a_recv.at[3 * k + j], (px, py, c)).wait_recv()
                fw = _remote(landed, landed, b_send.at[3 * k + j], b_recv.at[3 * k + j], sib)
                fw.start()
                sends.append(fw)
        for k in range(ns, n):
            for j, (px, py) in enumerate(peers):
                landed = outs[k].at[2 * px + py]
                _remote(landed, landed, a_send.at[3 * k + j], a_recv.at[3 * k + j], (px, py, c)).wait_recv()
        for k in range(ns):
            for j, (px, py) in enumerate(peers):
                passed = half(k, 2 * px + py, 1 - c)
                _remote(passed, passed, b_send.at[3 * k + j], b_recv.at[3 * k + j], sib).wait_recv()
        for rc in sends:
            rc.wait_send()

    outs = _pcall(
        body, name=name, in_specs=[ANY] * n, out_specs=[ANY] * n,
        out_shape=[SDS((4,) + a.shape, a.dtype) for a in arrs],
        scratch_shapes=[pltpu.SemaphoreType.DMA((3 * n,)), pltpu.SemaphoreType.DMA((3 * n,)), pltpu.SemaphoreType.DMA((3 * ns,)),
                        pltpu.SemaphoreType.DMA((3 * ns,))],
    )(*arrs)
    me_xy = 2 * lax.axis_index("x") + lax.axis_index("y")
    return [_own(o, a, me_xy) for o, a in zip(outs, arrs)]


def _gather_all(a, swap, name):
    masks = [(mx, my, mc) for mx in range(2) for my in range(2) for mc in range(2)][1:]
    n = len(swap)

    def body(*refs):
        in_ref, sw_in = refs[0], refs[1:1 + n]
        out_ref, sw_out = refs[1 + n], refs[2 + n:2 + 2 * n]
        send_sems, recv_sems = refs[2 + 2 * n:]
        x, y, c = _coords()
        me = 4 * x + 2 * y + c
        sends = []
        for j, (mx, my, mc) in enumerate(masks):
            rc = _remote(in_ref, out_ref.at[me], send_sems.at[j], recv_sems.at[j], (_flip(x, mx), _flip(y, my), _flip(c, mc)))
            rc.start()
            sends.append(rc)
        for k in range(n):
            rc = _remote(sw_in[k], sw_out[k], send_sems.at[7 + k], recv_sems.at[7 + k], (x, y, 1 - c))
            rc.start()
            sends.append(rc)
        for j, (mx, my, mc) in enumerate(masks):
            px, py, pc = _flip(x, mx), _flip(y, my), _flip(c, mc)
            landed = out_ref.at[4 * px + 2 * py + pc]
            _remote(landed, landed, send_sems.at[j], recv_sems.at[j], (px, py, pc)).wait_recv()
        for k in range(n):
            _remote(sw_out[k], sw_out[k], send_sems.at[7 + k], recv_sems.at[7 + k], (x, y, 1 - c)).wait_recv()
        for rc in sends:
            rc.wait_send()

    res = _pcall(
        body, name=name, in_specs=[ANY] * (1 + n), out_specs=[ANY] * (1 + n),
        out_shape=[SDS((8,) + a.shape, a.dtype)] + [SDS(s_.shape, s_.dtype) for s_ in swap],
        scratch_shapes=[pltpu.SemaphoreType.DMA((7 + n,)), pltpu.SemaphoreType.DMA((7 + n,))],
    )(a, *swap)
    return _own(res[0], a, 4 * lax.axis_index("x") + 2 * lax.axis_index("y") + lax.axis_index("c")), list(res[1:])


def _half_shape(shape, axis):
    return tuple(d // 2 if i == axis else d for i, d in enumerate(shape))


def _swap_half_c(arrs, axes, name):
    n = len(arrs)

    def body(*refs):
        ins, outs = refs[:n], refs[n:2 * n]
        send_sems, recv_sems = refs[2 * n:]
        x, y, c = _coords()
        sends = []
        for k in range(n):
            rc = _remote(ins[k].at[_half_idx(arrs[k].shape, axes[k], 1 - c)], outs[k], send_sems.at[k], recv_sems.at[k],
                         (x, y, 1 - c))
            rc.start()
            sends.append(rc)
        for rc in sends:
            rc.wait()

    return _pcall(
        body, name=name, in_specs=[ANY] * n, out_specs=[ANY] * n,
        out_shape=[SDS(_half_shape(a.shape, ax), a.dtype) for a, ax in zip(arrs, axes)],
        scratch_shapes=[pltpu.SemaphoreType.DMA((n,)), pltpu.SemaphoreType.DMA((n,))],
    )(*arrs)


def _a2a_xy(arrs, name):
    n = len(arrs)

    def body(*refs):
        ins, outs = refs[:n], refs[n:2 * n]
        send_sems, recv_sems = refs[2 * n:]
        x, y, c = _coords()
        me = 2 * x + y
        peers = [(1 - x, y), (x, 1 - y), (1 - x, 1 - y)]
        sends = []
        for k in range(n):
            for j, (px, py) in enumerate(peers):
                rc = _remote(ins[k].at[2 * px + py], outs[k].at[me], send_sems.at[3 * k + j], recv_sems.at[3 * k + j], (px, py, c))
                rc.start()
                sends.append(rc)
        for k in range(n):
            for j, (px, py) in enumerate(peers):
                landed = outs[k].at[2 * px + py]
                _remote(landed, landed, send_sems.at[3 * k + j], recv_sems.at[3 * k + j], (px, py, c)).wait_recv()
        for rc in sends:
            rc.wait_send()

    outs = _pcall(
        body, name=name, in_specs=[ANY] * n, out_specs=[ANY] * n, out_shape=[SDS(a.shape, a.dtype) for a in arrs],
        scratch_shapes=[pltpu.SemaphoreType.DMA((3 * n,)), pltpu.SemaphoreType.DMA((3 * n,))],
    )(*arrs)
    me_xy = 2 * lax.axis_index("x") + lax.axis_index("y")
    return [_own(o, lax.dynamic_index_in_dim(a, me_xy, axis=0, keepdims=False), me_xy) for o, a in zip(outs, arrs)]


def _exchange_c(arrs, name):
    n = len(arrs)

    def body(*refs):
        ins, outs = refs[:n], refs[n:2 * n]
        send_sems, recv_sems = refs[2 * n:]
        x, y, c = _coords()
        sends = []
        for k in range(n):
            rc = _remote(ins[k], outs[k], send_sems.at[k], recv_sems.at[k], (x, y, 1 - c))
            rc.start()
            sends.append(rc)
        for rc in sends:
            rc.wait()

    return _pcall(
        body, name=name, in_specs=[ANY] * n, out_specs=[ANY] * n, out_shape=[SDS(a.shape, a.dtype) for a in arrs],
        scratch_shapes=[pltpu.SemaphoreType.DMA((n,)), pltpu.SemaphoreType.DMA((n,))],
    )(*arrs)


def _join_halves(halves, axes, name):
    n = len(halves)
    full = [tuple(2 * d if i == ax else d for i, d in enumerate(a.shape)) for a, ax in zip(halves, axes)]

    def body(*refs):
        ins, outs = refs[:n], refs[n:2 * n]
        send_sems, recv_sems = refs[2 * n:]
        x, y, c = _coords()
        sends = []
        for k in range(n):
            rc = _remote(ins[k], outs[k].at[_half_idx(full[k], axes[k], c)], send_sems.at[k], recv_sems.at[k], (x, y, 1 - c))
            rc.start()
            sends.append(rc)
        for k in range(n):
            landed = outs[k].at[_half_idx(full[k], axes[k], 1 - c)]
            _remote(landed, landed, send_sems.at[k], recv_sems.at[k], (x, y, 1 - c)).wait_recv()
        for rc in sends:
            rc.wait_send()

    outs = _pcall(
        body, name=name, in_specs=[ANY] * n, out_specs=[ANY] * n,
        out_shape=[SDS(f, a.dtype) for f, a in zip(full, halves)],
        scratch_shapes=[pltpu.SemaphoreType.DMA((n,)), pltpu.SemaphoreType.DMA((n,))],
    )(*halves)
    ci = lax.axis_index("c")
    return [lax.dynamic_update_slice_in_dim(o, a, ci * a.shape[ax], axis=ax) for o, a, ax in zip(outs, halves, axes)]


def _pair_sum(a, got, cidx, axis, name):
    _, r, cdim = a.shape
    hshape = _half_shape(a.shape, axis)

    def body(c_ref, a_ref, g_ref, o_ref):
        o_ref[...] = (a_ref[...] + g_ref[...]).astype(BF16)

    if axis == 1:
        tr = min(r // 2, 256)
        nj = (r // 2) // tr
        blk = pl.BlockSpec((1, tr, cdim), lambda s, j, c: (s, j, 0))
        a_spec = pl.BlockSpec((1, tr, cdim), lambda s, j, c: (s, c[0] * nj + j, 0))
    else:
        nj = (cdim // 2) // 128
        blk = pl.BlockSpec((1, r, 128), lambda s, j, c: (s, 0, j))
        a_spec = pl.BlockSpec((1, r, 128), lambda s, j, c: (s, 0, c[0] * nj + j))
    return _pcall(
        body, name=name, out_shape=SDS(hshape, BF16),
        grid_spec=pltpu.PrefetchScalarGridSpec(num_scalar_prefetch=1, grid=(4, nj), in_specs=[a_spec, blk], out_specs=blk),
        compiler_params=_cp(("parallel", "parallel")),
    )(cidx, a, got)


def _sum_chips(parts, name):
    _, h, cdim = parts.shape

    def body(p_ref, o_ref):
        acc = p_ref[0].astype(F32)
        for k in range(1, 4):
            acc = acc + p_ref[k].astype(F32)
        o_ref[...] = acc

    if h % 256 == 0 or h in (128,):
        tr = min(h, 256)
        grid, in_spec, out_spec = (h // tr,), pl.BlockSpec((4, tr, cdim), lambda i: (0, i, 0)), pl.BlockSpec((tr, cdim), lambda i: (i, 0))
    else:
        grid, in_spec, out_spec = (cdim // 128,), pl.BlockSpec((4, h, 128), lambda i: (0, 0, i)), pl.BlockSpec((h, 128), lambda i: (0, i))
    return _pcall(
        body, name=name, grid=grid, in_specs=[in_spec], out_specs=out_spec, out_shape=SDS((h, cdim), F32),
        compiler_params=_cp(("parallel",)),
    )(parts)


def _sum_slots(a, name, rows):
    s, n, _ = a.shape

    def body(a_ref, o_ref):
        acc = a_ref[0]
        for k in range(1, s):
            acc = acc + a_ref[k]
        o_ref[...] = acc

    return _pcall(
        body, name=name, grid=(n // rows,), in_specs=[pl.BlockSpec((s, rows, 128), lambda i: (0, i, 0))],
        out_specs=pl.BlockSpec((rows, 128), lambda i: (i, 0)), out_shape=SDS((n, 128), F32),
        compiler_params=_cp(("parallel",)),
    )(a)


def _adam_math(w, g, m, v):
    nm = ADAM_B1 * m + (1.0 - ADAM_B1) * g
    nv = ADAM_B2 * v + (1.0 - ADAM_B2) * (g * g)
    m_hat = nm / (1.0 - ADAM_B1 ** ADAM_STEP)
    v_hat = nv / (1.0 - ADAM_B2 ** ADAM_STEP)
    return -ADAM_LR * (m_hat / (jnp.sqrt(v_hat) + ADAM_EPS) + ADAM_WD * w), nm, nv


def _adamw(w, g, m, v, name, rows):
    r, cdim = w.shape

    def body(w_ref, g_ref, m_ref, v_ref, d_ref, nm_ref, nv_ref):
        d_ref[...], nm_ref[...], nv_ref[...] = _adam_math(w_ref[...], g_ref[...], m_ref[...], v_ref[...])

    blk = pl.BlockSpec((rows, cdim), lambda i: (i, 0))
    return _pcall(
        body, name=name, grid=(r // rows,), in_specs=[blk] * 4, out_specs=[blk] * 3,
        out_shape=[SDS(w.shape, F32)] * 3, compiler_params=_cp(("parallel",)),
    )(w, g, m, v)


def _adamw_joined(w, mine, other, m, v, cidx, axis, name, rows):
    r, cdim = w.shape
    if axis == 0:
        rows = r

    def body(c_ref, w_ref, a_ref, b_ref, m_ref, v_ref, g_ref, d_ref, nm_ref, nv_ref):
        a, b = a_ref[...], b_ref[...]
        g = jnp.where(c_ref[0] == 0, jnp.concatenate([a, b], axis=axis), jnp.concatenate([b, a], axis=axis))
        g_ref[...] = g
        d_ref[...], nm_ref[...], nv_ref[...] = _adam_math(w_ref[...], g, m_ref[...], v_ref[...])

    blk = pl.BlockSpec((rows, cdim), lambda i, c: (i, 0))
    hshape = (rows // 2, cdim) if axis == 0 else (rows, cdim // 2)
    hblk = pl.BlockSpec(hshape, lambda i, c: (i, 0))
    return _pcall(
        body, name=name, out_shape=[SDS(w.shape, F32)] * 4,
        grid_spec=pltpu.PrefetchScalarGridSpec(num_scalar_prefetch=1, grid=(r // rows,), in_specs=[blk, hblk, hblk, blk, blk],
                                               out_specs=[blk] * 4),
        compiler_params=_cp(("parallel",)),
    )(cidx, w, mine, other, m, v)


def _adamw_many(ws, gs, ms, vs, name):
    n = len(ws)

    def body(*refs):
        outs = refs[4 * n:]
        for k in range(n):
            d, nm, nv = _adam_math(refs[k][...], refs[n + k][...], refs[2 * n + k][...], refs[3 * n + k][...])
            outs[k][...] = d
            outs[n + k][...] = nm
            outs[2 * n + k][...] = nv

    res = _pcall(body, name=name, out_shape=[SDS(w.shape, F32) for w in ws] * 3)(*ws, *gs, *ms, *vs)
    return res[:n], res[n:2 * n], res[2 * n:]


def _pack(pieces, rows):
    flat = jnp.concatenate([p.reshape(-1) for p in pieces])
    return jnp.pad(flat, (0, rows * 128 - flat.shape[0])).reshape(rows, 128)


def _unpack(buf, shapes):
    flat = buf.reshape(-1)
    out, off = [], 0
    for shp in shapes:
        size = 1
        for s in shp:
            size *= s
        out.append(flat[off:off + size].reshape(shp))
        off += size
    return out


def _perm_cols(w):
    perm = jnp.concatenate([w[..., 3104:5152], w[..., 0:1024], w[..., 1056:1568], w[..., 1568:2080], w[..., 2592:3104],
                            w[..., 2080:2592]], axis=-1)
    return perm, w[..., 1024:1056]


def _unperm_cols(perm, lr32):
    return jnp.concatenate([perm[..., 2048:3072], lr32, perm[..., 3072:3584], perm[..., 3584:4096], perm[..., 4608:5120],
                            perm[..., 4096:4608], perm[..., 0:2048]], axis=-1)


SMALL_ROWS = 672
HALF_ROWS = 7200


def kernel(x, c, ctx, c_ctx, w_mod, b_mod, norm_g, w_in, a_ln_g, a_ln_b, a_ws, a_bs, b_gate_w2, b_gate_b, b_norm_g, w_proj_a, w_proj_b, w_out, final_norm_g, loss_target, m_c_ctx, m_w_mod, m_b_mod, m_norm_g, m_w_in, m_a_ln_g, m_a_ln_b, m_a_ws, m_a_bs, m_b_gate_w2, m_b_gate_b, m_b_norm_g, m_w_proj_a, m_w_proj_b, m_w_out, m_final_norm_g, v_c_ctx, v_w_mod, v_b_mod, v_norm_g, v_w_in, v_a_ln_g, v_a_ln_b, v_a_ws, v_a_bs, v_b_gate_w2, v_b_gate_b, v_b_norm_g, v_w_proj_a, v_w_proj_b, v_w_out, v_final_norm_g):
    xi, yi, ci = _coords()
    me_xy = 2 * xi + yi

    gate_pack = _pack([b_gate_w2[0], b_gate_b[0]], 24)
    w_in_t, m_w_in_t, v_w_in_t = (jnp.swapaxes(a[0], 0, 1) for a in (w_in, m_w_in, v_w_in))
    g_wit, g_wm, g_gate = _gather_weights([(w_in_t.astype(BF16), 1), (w_mod[0].astype(BF16), 0)], [gate_pack], "gather_weights")
    late_shards = (w_proj_a[0].astype(BF16), w_proj_b[0].astype(BF16), w_out[0].astype(BF16))
    wit_u = g_wit.reshape(4 * 1288, D)
    wit_g = wit_u[3104:5152]
    wit_r = jnp.concatenate([wit_u[1056:1568], wit_u[1568:2080], wit_u[2592:3104], wit_u[2080:2592], wit_u[0:1024]], axis=0)
    wlrt = jnp.pad(wit_u[1024:1056], ((0, LRW - 32), (0, 0)))
    wm = jnp.swapaxes(g_wm, 0, 1).reshape(D, 3 * D)
    gflat = g_gate.reshape(4, 24 * 128)
    w2 = jnp.swapaxes(gflat[:, 0:2048].reshape(4, 2, 16, 64), 0, 2)
    w2 = jnp.swapaxes(w2, 0, 1).reshape(2, 16, 256)
    gb2 = jnp.swapaxes(gflat[:, 2048:2176].reshape(4, 2, 64), 0, 1).reshape(2, 256)

    tags = ["wi", "wpa", "wpb", "wo"]
    half_axes = [2, 1, 1, 1]
    sent = []

    def exchange(g):
        dwr = g["dwit_r"]
        dwit_u = jnp.concatenate([g["dwit_qkv"], g["dwlrt"][0:32], dwr[0:512], dwr[512:1024], dwr[1536:2048], dwr[1024:1536],
                                  g["dwit_g"]], axis=0)
        big = [dwit_u.reshape(4, 1288, D), jnp.swapaxes(g["dwpa"].reshape(512, 4, 256), 0, 1),
               jnp.swapaxes(g["dwpb"].reshape(512, 4, 256), 0, 1), g["dwo"].reshape(4, 256, D)]
        other = _swap_half_c(big, half_axes, "swap_half_in")
        cidx = jnp.reshape(ci, (1,)).astype(jnp.int32)
        sent.extend(_pair_sum(a, o, cidx, ax, "sum_pair_" + t) for a, o, ax, t in zip(big, other, half_axes, tags))
        return sent

    r = _device_step(x[0], c, ctx[0], c_ctx[None], loss_target[0], wm, b_mod, norm_g, wit_g, wit_r, wlrt, a_ln_g, a_ln_b,
                     a_ws[0], a_bs[0], w2, gb2, b_norm_g, None, None, None, final_norm_g[None], exchange, late_shards)

    parts = [_own(g, lax.dynamic_index_in_dim(s_, me_xy, axis=0, keepdims=False), me_xy) for g, s_ in zip(r["got"], sent)]
    halves = [_sum_chips(p_, "sum_chips_" + t) for p_, t in zip(parts, tags)]

    small = _pack([r["dmod"], c, r["dmodc"], r["dscc"], r["dng"], r["dlng"], r["dlnb"], r["dws"], r["dbs"], r["dgbn"], r["dgf"],
                   r["dw2"], r["dgb2"], jnp.broadcast_to(r["loss"], (128,))], SMALL_ROWS)
    small_all, others = _gather_all(small, halves, "gather_small")
    small_sum = _sum_slots(small_all, "sum_small", SMALL_ROWS // 4)
    (s_dmod, _, s_dmodc, s_dscc, s_dng, s_dlng, s_dlnb, s_dws, s_dbs, s_dgbn, s_dgf, s_dw2, s_dgb2, s_loss) = _unpack(
        small_sum, [(1, 3 * D), (1, D), (1, 2 * D), (D,), (1, D), (1, 512), (1, 512), (1, 4, 128, 128), (1, 4, 128), (1, 512),
                    (D,), (2, 16, 256), (2, 256), (128,)])
    loss = s_loss[0]
    s_dmodc_p = jnp.pad(s_dmodc, ((0, 0), (0, D)))
    g_b_mod = s_dmod + s_dmodc_p
    sg = jax.nn.sigmoid(c_ctx)
    g_c_ctx = s_dscc * (sg * (1.0 + c_ctx * (1.0 - sg)))
    g_w2 = lax.dynamic_slice_in_dim(s_dw2, 64 * me_xy, 64, axis=2)[None]
    g_gb2 = lax.dynamic_slice_in_dim(s_dgb2, 64 * me_xy, 64, axis=1)[None]

    flat_all = small_all.reshape(8, SMALL_ROWS * 128)
    dmod_all = flat_all[:, 0:3 * D]
    c_all = flat_all[:, 3 * D:4 * D]
    lhs = jnp.concatenate([_silu(c_all), _silu(c_ctx)[None], jnp.zeros((7, D), F32)], axis=0)
    rhs = jnp.concatenate([dmod_all, s_dmodc_p, jnp.zeros((7, 3 * D), F32)], axis=0)
    rhs = lax.dynamic_slice_in_dim(rhs, 768 * me_xy, 768, axis=1)
    g_w_mod = _mm(lhs.T.astype(BF16), rhs.astype(BF16), tm=D, tn=768, tk=16, out_dtype=F32, name="mm_dwm")

    cidx = jnp.reshape(ci, (1,)).astype(jnp.int32)
    g_w_in_t, d_w_in_t, nm_w_in_t, nv_w_in_t = _adamw_joined(w_in_t, halves[0], others[0], m_w_in_t, v_w_in_t, cidx, 1,
                                                             "adamw_w_in", 184)
    g_w_in, d_w_in, nm_w_in, nv_w_in = (jnp.swapaxes(a, 0, 1) for a in (g_w_in_t, d_w_in_t, nm_w_in_t, nv_w_in_t))
    g_wpa, d_wpa, nm_wpa, nv_wpa = _adamw_joined(w_proj_a[0], halves[1], others[1], m_w_proj_a[0], v_w_proj_a[0], cidx, 0,
                                                 "adamw_wpa", 0)
    g_wpb, d_wpb, nm_wpb, nv_wpb = _adamw_joined(w_proj_b[0], halves[2], others[2], m_w_proj_b[0], v_w_proj_b[0], cidx, 0,
                                                 "adamw_wpb", 0)
    g_wo, d_wo, nm_wo, nv_wo = _adamw_joined(w_out[0], halves[3], others[3], m_w_out[0], v_w_out[0], cidx, 0, "adamw_wo", 0)
    d_w_mod, nm_w_mod, nv_w_mod = _adamw(w_mod[0], g_w_mod, m_w_mod[0], v_w_mod[0], "adamw_w_mod", 256)

    names = ["c_ctx", "b_mod", "norm_g", "a_ln_g", "a_ln_b", "a_ws", "a_bs", "b_gate_w2", "b_gate_b", "b_norm_g", "final_norm_g"]
    ws_ = [c_ctx, b_mod, norm_g, a_ln_g, a_ln_b, a_ws, a_bs, b_gate_w2, b_gate_b, b_norm_g, final_norm_g]
    gs_ = [g_c_ctx, g_b_mod, s_dng, s_dlng, s_dlnb, s_dws, s_dbs, g_w2, g_gb2, s_dgbn, s_dgf]
    ms_ = [m_c_ctx, m_b_mod, m_norm_g, m_a_ln_g, m_a_ln_b, m_a_ws, m_a_bs, m_b_gate_w2, m_b_gate_b, m_b_norm_g, m_final_norm_g]
    vs_ = [v_c_ctx, v_b_mod, v_norm_g, v_a_ln_g, v_a_ln_b, v_a_ws, v_a_bs, v_b_gate_w2, v_b_gate_b, v_b_norm_g, v_final_norm_g]
    shapes = [w.shape for w in ws_]
    flat2 = [(1, 1024), (1, 3072), (1, 1024), (1, 512), (1, 512), (512, 128), (4, 128), (32, 64), (2, 64), (1, 512), (1, 1024)]
    as2d = lambda arrs: [a.reshape(s) for a, s in zip(arrs, flat2)]
    d_s, nm_s, nv_s = _adamw_many(as2d(ws_), as2d(gs_), as2d(ms_), as2d(vs_), "adamw_small")
    d_small = {n: a.reshape(s) for n, a, s in zip(names, d_s, shapes)}
    nm_small = {n: a.reshape(s) for n, a, s in zip(names, nm_s, shapes)}
    nv_small = {n: a.reshape(s) for n, a, s in zip(names, nv_s, shapes)}
    g_small = {n: g.reshape(s) for n, g, s in zip(names, gs_, shapes)}

    order = ["c_ctx", "w_mod", "b_mod", "norm_g", "w_in", "a_ln_g", "a_ln_b", "a_ws", "a_bs", "b_gate_w2", "b_gate_b", "b_norm_g",
             "w_proj_a", "w_proj_b", "w_out", "final_norm_g"]
    big_g = dict(w_mod=g_w_mod[None], w_in=g_w_in[None], w_proj_a=g_wpa[None], w_proj_b=g_wpb[None], w_out=g_wo[None])
    big_d = dict(w_mod=d_w_mod[None], w_in=d_w_in[None], w_proj_a=d_wpa[None], w_proj_b=d_wpb[None], w_out=d_wo[None])
    big_m = dict(w_mod=nm_w_mod[None], w_in=nm_w_in[None], w_proj_a=nm_wpa[None], w_proj_b=nm_wpb[None], w_out=nm_wo[None])
    big_v = dict(w_mod=nv_w_mod[None], w_in=nv_w_in[None], w_proj_a=nv_wpa[None], w_proj_b=nv_wpb[None], w_out=nv_wo[None])
    grads = [big_g[n] if n in big_g else g_small[n] for n in order]
    deltas = [big_d[n] if n in big_d else d_small[n] for n in order]
    new_m = [big_m[n] if n in big_m else nm_small[n] for n in order]
    new_v = [big_v[n] if n in big_v else nv_small[n] for n in order]
    return (loss, r["dx"][None], *grads, *deltas, *new_m, *new_v)
```

```python
import functools

import jax
import jax.numpy as jnp
from jax import lax
from jax.experimental import pallas as pl
from jax.experimental.pallas import tpu as pltpu

F32 = jnp.float32
BF16 = jnp.bfloat16
SDS = jax.ShapeDtypeStruct

D = 1024
NP = 5120
LRW = 128
CH = 64
AC = 128
EPS = 1e-6
TOK = 512
GLA_TB = 1024
VMEM_BIG = 48 * 1024 * 1024

ADAM_LR, ADAM_B1, ADAM_B2, ADAM_EPS, ADAM_WD, ADAM_STEP = 0.001, 0.9, 0.999, 1e-08, 0.01, 10

_pcall = pl.pallas_call
MESH = pl.DeviceIdType.MESH


def _cp(sem=None, vmem=None):
    kw = {}
    if sem is not None:
        kw["dimension_semantics"] = sem
    if vmem is not None:
        kw["vmem_limit_bytes"] = vmem
    return pltpu.CompilerParams(**kw)


def _silu(x):
    return x * jax.nn.sigmoid(x)


def _dsilu(x):
    s = jax.nn.sigmoid(x)
    return s * (1.0 + x * (1.0 - s))


def _silu_and_grad(x):
    s = jax.nn.sigmoid(x)
    return x * s, s * (1.0 + x * (1.0 - s))


def _logsig(x):
    return jnp.minimum(x, 0.0) - jnp.log1p(jnp.exp(-jnp.abs(x)))


def _nt(a, b):
    return lax.dot_general(a, b, (((1,), (1,)), ((), ())), preferred_element_type=F32)


def _tn(a, b):
    return lax.dot_general(a, b, (((0,), (0,)), ((), ())), preferred_element_type=F32)


def _nn(a, b):
    return jnp.dot(a, b, preferred_element_type=F32)


def _full(shape):
    return pl.BlockSpec(shape, lambda *_: (0,) * len(shape))


def _mm(a, b, *, tm, tn, tk, out_dtype, name, acc=None, n_outer=False, b_t=False):
    m, k = a.shape
    n, k2 = (b.shape if b_t else b.shape[::-1])
    assert k == k2 and m % tm == 0 and n % tn == 0 and k % tk == 0, (a.shape, b.shape, tm, tn, tk)
    nk = k // tk
    has_acc = acc is not None

    def body(*refs):
        if has_acc:
            a_ref, b_ref, c_ref, o_ref = refs[:4]
        else:
            a_ref, b_ref, o_ref = refs[:3]
        part = (_nt if b_t else _nn)(a_ref[...].astype(BF16), b_ref[...].astype(BF16))
        if nk == 1:
            o_ref[...] = ((c_ref[...] + part) if has_acc else part).astype(out_dtype)
            return
        acc_ref = refs[-1]
        kk = pl.program_id(2)

        @pl.when(kk == 0)
        def _():
            if has_acc:
                acc_ref[...] = c_ref[...] + part
            else:
                acc_ref[...] = part

        @pl.when(kk > 0)
        def _():
            acc_ref[...] += part

        @pl.when(kk == nk - 1)
        def _():
            o_ref[...] = acc_ref[...].astype(out_dtype)

    if n_outer:
        ij = lambda g0, g1: (g1, g0)
        grid = (n // tn, m // tm, nk)
    else:
        ij = lambda g0, g1: (g0, g1)
        grid = (m // tm, n // tn, nk)
    b_spec = (pl.BlockSpec((tn, tk), lambda g0, g1, kk: (ij(g0, g1)[1], kk)) if b_t
              else pl.BlockSpec((tk, tn), lambda g0, g1, kk: (kk, ij(g0, g1)[1])))
    in_specs = [pl.BlockSpec((tm, tk), lambda g0, g1, kk: (ij(g0, g1)[0], kk)), b_spec]
    args = [a, b]
    if has_acc:
        in_specs.append(pl.BlockSpec((tm, tn), lambda g0, g1, kk: ij(g0, g1)))
        args.append(acc)
    return _pcall(
        body, name=name, grid=grid, in_specs=in_specs,
        out_specs=pl.BlockSpec((tm, tn), lambda g0, g1, kk: ij(g0, g1)),
        out_shape=SDS((m, n), out_dtype), scratch_shapes=([pltpu.VMEM((tm, tn), F32)] if nk > 1 else []),
        compiler_params=_cp(("parallel", "parallel", "arbitrary"), VMEM_BIG),
    )(*args)


def _mm_tn(a, b, *, ta, tn, tk, name, acc=None):
    m, ka = a.shape
    m2, n = b.shape
    assert m == m2 and ka % ta == 0 and n % tn == 0 and m % tk == 0, (a.shape, b.shape, ta, tn, tk)
    nk = m // tk
    has_acc = acc is not None

    def body(*refs):
        if has_acc:
            a_ref, b_ref, c_ref, o_ref = refs
        else:
            a_ref, b_ref, o_ref = refs
        kk = pl.program_id(2)
        part = _tn(a_ref[...].astype(BF16), b_ref[...].astype(BF16))

        @pl.when(kk == 0)
        def _():
            if has_acc:
                o_ref[...] = c_ref[...] + part
            else:
                o_ref[...] = part

        @pl.when(kk > 0)
        def _():
            o_ref[...] += part

    in_specs = [pl.BlockSpec((tk, ta), lambda i, j, kk: (kk, i)), pl.BlockSpec((tk, tn), lambda i, j, kk: (kk, j))]
    args = [a, b]
    if has_acc:
        in_specs.append(pl.BlockSpec((ta, tn), lambda i, j, kk: (i, j)))
        args.append(acc)
    return _pcall(
        body, name=name, grid=(ka // ta, n // tn, nk), in_specs=in_specs,
        out_specs=pl.BlockSpec((ta, tn), lambda i, j, kk: (i, j)), out_shape=SDS((ka, n), F32),
        compiler_params=_cp(("parallel", "parallel", "arbitrary"), VMEM_BIG),
    )(*args)


def _modvec(cc, wm, bm):
    def body(c_ref, w_ref, b_ref, o_ref):
        o_ref[...] = _nn(_silu(c_ref[...]).astype(BF16), w_ref[...]) + b_ref[...]

    return _pcall(body, name="modvec", out_shape=SDS((8, 3 * D), F32), compiler_params=_cp(None, VMEM_BIG))(cc, wm, bm)


def _dcctx(dmodc, wm):
    def body(d_ref, w_ref, o_ref):
        o_ref[...] = _nt(d_ref[...].astype(BF16), w_ref[...])

    return _pcall(
        body, name="dcctx", grid=(1,), in_specs=[_full((8, 2 * D)), pl.BlockSpec((D, 2 * D), lambda i: (0, 0))],
        out_specs=_full((8, D)), out_shape=SDS((8, D), F32), compiler_params=_cp(("arbitrary",), VMEM_BIG),
    )(dmodc, wm)


def _prep_h(x, ng, scale, shift, name):
    m = x.shape[0]

    def body(x_ref, g_ref, sc_ref, sh_ref, h_ref):
        xf = x_ref[...]
        r = lax.rsqrt(jnp.mean(xf * xf, axis=-1, keepdims=True) + EPS)
        y = (xf * r) * g_ref[...]
        h_ref[...] = (y * (1.0 + sc_ref[...]) + sh_ref[...]).astype(BF16)

    tok = min(TOK, m)
    row = pl.BlockSpec((tok, D), lambda i: (i, 0))
    return _pcall(
        body, name=name, grid=(m // tok,), in_specs=[row, _full((1, D)), _full((1, D)), _full((1, D))],
        out_specs=row, out_shape=SDS((m, D), BF16), compiler_params=_cp(("parallel",)),
    )(x, ng, scale, shift)


def _resident(shape):
    return pl.BlockSpec(shape, lambda *_: (0,) * len(shape), pipeline_mode=pl.Buffered(1))


PROJ_TM = 512


def _proj_fwd(x, ng, scale, shift, wit_g, wit_r, wlrt, ln_g, ln_b, share=()):
    m = x.shape[0]
    ns = len(share)
    steps = m // PROJ_TM
    src = [(0, 0), (0, D), (1, 2 * D), (1, 0), (1, D)]

    def body(*refs):
        x_ref, g_ref, sc_ref, sh_ref, wg_ref, wr_ref, wl_ref, lg_ref, lb_ref = refs[:9]
        share_refs = refs[9:9 + ns]
        h_ref, p_ref, plr_ref, vr_ref, vc_ref = refs[9 + ns:14 + ns]
        got_refs = refs[14 + ns:14 + 2 * ns]
        sems = refs[14 + 2 * ns:]

        def copies():
            cx, cy, cc = _coords()
            me = 2 * cx + cy
            peers = [(1 - cx, cy), (cx, 1 - cy), (1 - cx, 1 - cy)]
            out, back = [], []
            for k in range(ns):
                for j, (px, py) in enumerate(peers):
                    out.append(_remote(share_refs[k], got_refs[k].at[me], sems[0].at[3 * k + j], sems[1].at[3 * k + j], (px, py, cc)))
                    landed = got_refs[k].at[2 * px + py]
                    back.append(_remote(landed, landed, sems[0].at[3 * k + j], sems[1].at[3 * k + j], (px, py, cc)))
            return out, back

        if ns:
            @pl.when(pl.program_id(0) == 0)
            def _():
                for rc in copies()[0]:
                    rc.start()

            @pl.when(pl.program_id(0) == steps - 1)
            def _():
                out, back = copies()
                for rc in back:
                    rc.wait_recv()
                for rc in out:
                    rc.wait_send()

        xf = x_ref[...]
        r = lax.rsqrt(jnp.mean(xf * xf, axis=-1, keepdims=True) + EPS)
        y = (xf * r) * g_ref[...]
        h = (y * (1.0 + sc_ref[...]) + sh_ref[...]).astype(BF16)
        h_ref[...] = h
        for j, (which, r0) in enumerate(src):
            w_ref = wr_ref if which else wg_ref
            blk = _nt(h, w_ref[r0:r0 + D, :]).astype(BF16)
            p_ref[:, D * j:D * j + D] = blk
            if j == 4:
                xf = blk[:, 512:1024].astype(F32)
                xc = xf - jnp.mean(xf, axis=-1, keepdims=True)
                vn = (xc * lax.rsqrt(jnp.mean(xc * xc, axis=-1, keepdims=True) + EPS)) * lg_ref[...] + lb_ref[...]
                vr_ref[...] = vn[:, 0:256].astype(BF16)
                vc_ref[0] = vn[:, 256:384].astype(BF16)
                vc_ref[1] = vn[:, 384:512].astype(BF16)
        plr_ref[...] = _nt(h, wl_ref[...])

    row = pl.BlockSpec((PROJ_TM, D), lambda i: (i, 0))
    vec = _full((1, D))
    res = _pcall(
        body, name="proj_fwd", grid=(steps,),
        in_specs=[row, vec, vec, vec, _resident((2 * D, D)), _resident((3 * D, D)), _resident((LRW, D)), _full((1, 512)),
                  _full((1, 512))] + [ANY] * ns,
        out_specs=[row, pl.BlockSpec((PROJ_TM, NP), lambda i: (i, 0)), pl.BlockSpec((PROJ_TM, LRW), lambda i: (i, 0)),
                   pl.BlockSpec((PROJ_TM, 256), lambda i: (i, 0)), pl.BlockSpec((2, PROJ_TM, 128), lambda i: (0, i, 0))] + [ANY] * ns,
        out_shape=[SDS((m, D), BF16), SDS((m, NP), BF16), SDS((m, LRW), F32), SDS((m, 256), BF16), SDS((2, m, 128), BF16)]
        + [SDS((4,) + a.shape, a.dtype) for a in share],
        scratch_shapes=([pltpu.SemaphoreType.DMA((3 * ns,)), pltpu.SemaphoreType.DMA((3 * ns,))] if ns else []),
        compiler_params=_cp(("arbitrary",), VMEM_BIG),
    )(x, ng, scale, shift, wit_g, wit_r, wlrt, ln_g, ln_b, *share)
    return res[0], res[1], res[2], res[3], res[4], list(res[5:])


def _proj_bwd(dp_g, dp_r, dlr, wit_g, wit_r, wlrt, x, dx1, ng, scale, send=()):
    m = x.shape[0]
    ns = len(send)
    steps = m // PROJ_TM

    def body(*refs):
        (dpg_ref, dpr_ref, dlr_ref, wg_ref, wr_ref, wl_ref, x_ref, r_ref, g_ref, sc_ref) = refs[:10]
        send_refs = refs[10:10 + ns]
        dx_ref, dg_ref, dsc_ref, dsh_ref = refs[10 + ns:14 + ns]
        got_refs = refs[14 + ns:14 + 2 * ns]
        sems = refs[14 + 2 * ns:]
        i = pl.program_id(0)

        def copies():
            cx, cy, cc = _coords()
            me = 2 * cx + cy
            peers = [(1 - cx, cy), (cx, 1 - cy), (1 - cx, 1 - cy)]
            out, back = [], []
            for k in range(ns):
                for j, (px, py) in enumerate(peers):
                    out.append(_remote(send_refs[k].at[2 * px + py], got_refs[k].at[me], sems[0].at[3 * k + j],
                                       sems[1].at[3 * k + j], (px, py, cc)))
                    landed = got_refs[k].at[2 * px + py]
                    back.append(_remote(landed, landed, sems[0].at[3 * k + j], sems[1].at[3 * k + j], (px, py, cc)))
            return out, back

        @pl.when(i == 0)
        def _():
            dg_ref[...] = jnp.zeros_like(dg_ref)
            dsc_ref[...] = jnp.zeros_like(dsc_ref)
            dsh_ref[...] = jnp.zeros_like(dsh_ref)
            if ns:
                for rc in copies()[0]:
                    rc.start()

        dh_ = (_nn(dpg_ref[...], wg_ref[...]) + _nn(dpr_ref[...], wr_ref[...])
               + _nn(dlr_ref[...].astype(BF16), wl_ref[...]))
        xf = x_ref[...]
        r = lax.rsqrt(jnp.mean(xf * xf, axis=-1, keepdims=True) + EPS)
        xh = xf * r
        y = xh * g_ref[...]
        dsh_ref[...] += jnp.sum(dh_, axis=0, keepdims=True)
        dsc_ref[...] += jnp.sum(dh_ * y, axis=0, keepdims=True)
        dy = dh_ * (1.0 + sc_ref[...])
        dg_ref[...] += jnp.sum(dy * xh, axis=0, keepdims=True)
        dxh = dy * g_ref[...]
        dx_ref[...] = r * (dxh - xh * jnp.mean(dxh * xh, axis=-1, keepdims=True)) + r_ref[...]

        if ns:
            @pl.when(i == steps - 1)
            def _():
                out, back = copies()
                for rc in back:
                    rc.wait_recv()
                for rc in out:
                    rc.wait_send()

    row = pl.BlockSpec((PROJ_TM, D), lambda i: (i, 0))
    vec = _full((1, D))
    kg, kr = dp_g.shape[1], dp_r.shape[1]
    res = _pcall(
        body, name="proj_bwd", grid=(steps,),
        in_specs=[pl.BlockSpec((PROJ_TM, kg), lambda i: (i, 0)), pl.BlockSpec((PROJ_TM, kr), lambda i: (i, 0)),
                  pl.BlockSpec((PROJ_TM, LRW), lambda i: (i, 0)), _resident((kg, D)), _resident((kr, D)), _resident((LRW, D)),
                  row, row, vec, vec] + [ANY] * ns,
        out_specs=[row, vec, vec, vec] + [ANY] * ns,
        out_shape=[SDS((m, D), F32), SDS((1, D), F32), SDS((1, D), F32), SDS((1, D), F32)] + [SDS(a.shape, a.dtype) for a in send],
        scratch_shapes=([pltpu.SemaphoreType.DMA((3 * ns,)), pltpu.SemaphoreType.DMA((3 * ns,))] if ns else []),
        compiler_params=_cp(("arbitrary",), VMEM_BIG),
    )(dp_g, dp_r, dlr, wit_g, wit_r, wlrt, x, dx1, ng, scale, *send)
    return tuple(res[:4]), list(res[4:])


def _prep_bwd(x, dh, dx1, ng, scale, name):
    m = x.shape[0]
    has_res = dx1 is not None

    def body(*refs):
        if has_res:
            x_ref, dh_ref, r_ref, g_ref, sc_ref, dx_ref, dg_ref, dsc_ref, dsh_ref = refs
        else:
            x_ref, dh_ref, g_ref, sc_ref, dx_ref, dg_ref, dsc_ref, dsh_ref = refs
        i = pl.program_id(0)

        @pl.when(i == 0)
        def _():
            dg_ref[...] = jnp.zeros_like(dg_ref)
            dsc_ref[...] = jnp.zeros_like(dsc_ref)
            dsh_ref[...] = jnp.zeros_like(dsh_ref)

        xf = x_ref[...]
        dh_ = dh_ref[...]
        r = lax.rsqrt(jnp.mean(xf * xf, axis=-1, keepdims=True) + EPS)
        xh = xf * r
        y = xh * g_ref[...]
        dsh_ref[...] += jnp.sum(dh_, axis=0, keepdims=True)
        dsc_ref[...] += jnp.sum(dh_ * y, axis=0, keepdims=True)
        dy = dh_ * (1.0 + sc_ref[...])
        dg_ref[...] += jnp.sum(dy * xh, axis=0, keepdims=True)
        dxh = dy * g_ref[...]
        dx = r * (dxh - xh * jnp.mean(dxh * xh, axis=-1, keepdims=True))
        if has_res:
            dx = dx + r_ref[...]
        dx_ref[...] = dx

    tok = min(TOK, m)
    row = pl.BlockSpec((tok, D), lambda i: (i, 0))
    vec = _full((1, D))
    in_specs = [row, row] + ([row] if has_res else []) + [vec, vec]
    args = [x, dh] + ([dx1] if has_res else []) + [ng, scale]
    return _pcall(
        body, name=name, grid=(m // tok,), in_specs=in_specs, out_specs=[row, vec, vec, vec],
        out_shape=[SDS((m, D), F32), SDS((1, D), F32), SDS((1, D), F32), SDS((1, D), F32)],
        compiler_params=_cp(("arbitrary",)),
    )(*args)


def _ln_fwd(p, ln_g, ln_b):
    m = p.shape[0]

    def body(va_ref, g_ref, b_ref, vr_ref, vc_ref):
        xf = va_ref[...].astype(F32)
        xc = xf - jnp.mean(xf, axis=-1, keepdims=True)
        y = xc * lax.rsqrt(jnp.mean(xc * xc, axis=-1, keepdims=True) + EPS)
        vn = y * g_ref[...] + b_ref[...]
        vr_ref[...] = vn[:, 0:256].astype(BF16)
        vc_ref[0] = vn[:, 256:384].astype(BF16)
        vc_ref[1] = vn[:, 384:512].astype(BF16)

    return _pcall(
        body, name="ln_fwd", grid=(m // TOK,),
        in_specs=[pl.BlockSpec((TOK, 512), lambda i: (i, 9)), _full((1, 512)), _full((1, 512))],
        out_specs=[pl.BlockSpec((TOK, 256), lambda i: (i, 0)), pl.BlockSpec((2, TOK, 128), lambda i: (0, i, 0))],
        out_shape=[SDS((m, 256), BF16), SDS((2, m, 128), BF16)], compiler_params=_cp(("parallel",)),
    )(p, ln_g, ln_b)


COLB = 2048


def _colmix_fwd(vnc, ws23, bs23):
    rows = vnc.shape[2] // COLB

    def body(v_ref, w_ref, b_ref, o_ref):
        o_ref[0] = _nn(w_ref[0], v_ref[0]) + b_ref[0]

    return _pcall(
        body, name="colmix_fwd", grid=(2, rows),
        in_specs=[pl.BlockSpec((1, AC, COLB), lambda g, j: (g, 0, j)), pl.BlockSpec((1, AC, AC), lambda g, j: (g, 0, 0)),
                  pl.BlockSpec((1, AC, 1), lambda g, j: (g, 0, 0))],
        out_specs=pl.BlockSpec((1, AC, COLB), lambda g, j: (g, 0, j)),
        out_shape=SDS(vnc.shape, F32), compiler_params=_cp(("parallel", "parallel")),
    )(vnc, ws23, bs23)


def _colmix_bwd(dsvc, vnc, ws23t):
    rows = vnc.shape[2] // COLB

    def body(d_ref, v_ref, wt_ref, dv_ref, dw_ref, db_ref):
        j = pl.program_id(1)

        @pl.when(j == 0)
        def _():
            dw_ref[...] = jnp.zeros_like(dw_ref)
            db_ref[...] = jnp.zeros_like(db_ref)

        d = d_ref[0]
        d16 = d.astype(BF16)
        dv_ref[0] = _nn(wt_ref[0], d16)
        dw_ref[0] += _nt(d16, v_ref[0])
        db_ref[0] += jnp.sum(d, axis=1, keepdims=True)

    blk = pl.BlockSpec((1, AC, COLB), lambda g, j: (g, 0, j))
    return _pcall(
        body, name="colmix_bwd", grid=(2, rows),
        in_specs=[blk, blk, pl.BlockSpec((1, AC, AC), lambda g, j: (g, 0, 0))],
        out_specs=[blk, pl.BlockSpec((1, AC, AC), lambda g, j: (g, 0, 0)), pl.BlockSpec((1, AC, 1), lambda g, j: (g, 0, 0))],
        out_shape=[SDS(vnc.shape, F32), SDS((2, AC, AC), F32), SDS((2, AC, 1), F32)],
        compiler_params=_cp(("parallel", "arbitrary")),
    )(dsvc, vnc, ws23t)


def _head_norm(o, gbn):
    out = []
    for h in range(4):
        oh = o[:, 128 * h:128 * h + 128]
        r = lax.rsqrt(jnp.mean(oh * oh, axis=-1, keepdims=True) + EPS)
        out.append((r, oh * r))
    return out


def _mid_fwd(o_f, o_b, p, vnr, svc, ws01, bs01, gbn):
    m = p.shape[0]

    def body(of_ref, ob_ref, zb_ref, ua_ref, za_ref, vnr_ref, svc_ref, w_ref, b_ref, g_ref, ya_ref, yb_ref, svr_ref):
        o = of_ref[...] + ob_ref[...]
        zb = zb_ref[...]
        parts = []
        for h, (r, xh) in enumerate(_head_norm(o, None)):
            parts.append(xh * g_ref[:, 128 * h:128 * h + 128])
        on = jnp.concatenate(parts, axis=1)
        yb_ref[...] = (on * _silu(zb)).astype(BF16)
        for j in range(TOK // AC):
            for g in range(2):
                sv = _nn(w_ref[g], vnr_ref[AC * j:AC * j + AC, AC * g:AC * g + AC]) + b_ref[g]
                svr_ref[AC * j:AC * j + AC, AC * g:AC * g + AC] = sv
        sz = _silu(za_ref[...])
        u = ua_ref[...]
        ya_ref[:, 0:256] = ((u[:, 0:256] * svr_ref[...]) * sz[:, 0:256]).astype(BF16)
        ya_ref[:, 256:384] = ((u[:, 256:384] * svc_ref[0]) * sz[:, 256:384]).astype(BF16)
        ya_ref[:, 384:512] = ((u[:, 384:512] * svc_ref[1]) * sz[:, 384:512]).astype(BF16)

    r512 = pl.BlockSpec((TOK, 512), lambda i: (i, 0))
    return _pcall(
        body, name="mid_fwd", grid=(m // TOK,),
        in_specs=[r512, r512, pl.BlockSpec((TOK, 512), lambda i: (i, 6)), pl.BlockSpec((TOK, 512), lambda i: (i, 7)),
                  pl.BlockSpec((TOK, 512), lambda i: (i, 8)), pl.BlockSpec((TOK, 256), lambda i: (i, 0)),
                  pl.BlockSpec((2, TOK, 128), lambda i: (0, i, 0)), _full((2, AC, AC)), _full((2, AC, 1)), _full((1, 512))],
        out_specs=[r512, r512, pl.BlockSpec((TOK, 256), lambda i: (i, 0))],
        out_shape=[SDS((m, 512), BF16), SDS((m, 512), BF16), SDS((m, 256), F32)],
        compiler_params=_cp(("parallel",)),
    )(o_f, o_b, p, p, p, vnr, svc, ws01, bs01, gbn)


def _merge_fwd(p, ya, yb):
    m = p.shape[0]

    def body(ga_ref, gb_ref, ya_ref, yb_ref, m_ref):
        m_ref[...] = (jax.nn.sigmoid(ga_ref[...]) * ya_ref[...] + jax.nn.sigmoid(gb_ref[...]) * yb_ref[...]).astype(BF16)

    row = pl.BlockSpec((TOK, D), lambda i: (i, 0))
    return _pcall(
        body, name="merge_fwd", grid=(m // TOK,),
        in_specs=[row, pl.BlockSpec((TOK, D), lambda i: (i, 1)), row, row], out_specs=row,
        out_shape=SDS((m, D), BF16), compiler_params=_cp(("parallel",)),
    )(p, p, ya, yb)


def _loss_head(x, out, tgt, gate, gf):
    m = x.shape[0]

    def body(x_ref, o_ref, t_ref, gate_ref, gf_ref, dx1_ref, dout_ref, loss_ref, dgate_ref, dgf_ref):
        i = pl.program_id(0)

        @pl.when(i == 0)
        def _():
            loss_ref[...] = jnp.zeros_like(loss_ref)
            dgate_ref[...] = jnp.zeros_like(dgate_ref)
            dgf_ref[...] = jnp.zeros_like(dgf_ref)

        out_ = o_ref[...]
        x1 = x_ref[...] + gate_ref[...] * out_
        r = lax.rsqrt(jnp.mean(x1 * x1, axis=-1, keepdims=True) + EPS)
        xh = x1 * r
        err = xh * gf_ref[...] - t_ref[...]
        loss_ref[...] += 0.5 * jnp.sum(jnp.mean(err * err, axis=-1, keepdims=True), axis=0, keepdims=True)
        dy = err * (1.0 / D)
        dgf_ref[...] += jnp.sum(dy * xh, axis=0, keepdims=True)
        dxh = dy * gf_ref[...]
        dx1 = r * (dxh - xh * jnp.mean(dxh * xh, axis=-1, keepdims=True))
        dx1_ref[...] = dx1
        dout_ref[...] = (gate_ref[...] * dx1).astype(BF16)
        dgate_ref[...] += jnp.sum(dx1 * out_, axis=0, keepdims=True)

    row = pl.BlockSpec((TOK, D), lambda i: (i, 0))
    vec = _full((1, D))
    return _pcall(
        body, name="loss_head", grid=(m // TOK,), in_specs=[row, row, row, vec, vec],
        out_specs=[row, row, _full((1, 128)), vec, vec],
        out_shape=[SDS((m, D), F32), SDS((m, D), BF16), SDS((1, 128), F32), SDS((1, D), F32), SDS((1, D), F32)],
        compiler_params=_cp(("arbitrary",)),
    )(x, out, tgt, gate, gf)


def _merge_bwd(dm, ya, yb, p):
    m = p.shape[0]

    def body(dm_ref, ya_ref, yb_ref, ga_ref, gb_ref, dya_ref, dyb_ref, dp_ref):
        dm_ = dm_ref[...]
        sa = jax.nn.sigmoid(ga_ref[...])
        sb = jax.nn.sigmoid(gb_ref[...])
        dya_ref[...] = (dm_ * sa).astype(BF16)
        dyb_ref[...] = (dm_ * sb).astype(BF16)
        dp_ref[:, 0:D] = (dm_ * ya_ref[...] * (sa * (1.0 - sa))).astype(BF16)
        dp_ref[:, D:2 * D] = (dm_ * yb_ref[...] * (sb * (1.0 - sb))).astype(BF16)

    row = pl.BlockSpec((TOK, D), lambda i: (i, 0))
    return _pcall(
        body, name="merge_bwd", grid=(m // TOK,),
        in_specs=[row, row, row, row, pl.BlockSpec((TOK, D), lambda i: (i, 1))],
        out_specs=[row, row, pl.BlockSpec((TOK, 2 * D), lambda i: (i, 0))],
        out_shape=[SDS((m, D), BF16), SDS((m, D), BF16), SDS((m, NP), BF16)],
        compiler_params=_cp(("parallel",)),
    )(dm, ya, yb, p, p)


def _mid_bwd(dya_in, dyb_in, p, svr, svc, o_f, o_b, gbn, dp):
    m = p.shape[0]

    def body(dya_ref, dyb_ref, zb_ref, ua_ref, za_ref, svr_ref, svc_ref, of_ref, ob_ref, g_ref, dpi_ref,
             dp_ref, dsr_ref, dsc_ref, do_ref, dg_ref):
        i = pl.program_id(0)

        @pl.when(i == 0)
        def _():
            dg_ref[...] = jnp.zeros_like(dg_ref)

        dya = dya_ref[...]
        u = ua_ref[...]
        za = za_ref[...]
        sz = _silu(za)
        sv = jnp.concatenate([svr_ref[...], svc_ref[0], svc_ref[1]], axis=1)
        dp_ref[:, 512:1024] = (dya * sv * sz).astype(BF16)
        dsv = dya * u * sz
        dsr_ref[...] = dsv[:, 0:256]
        dsc_ref[0] = dsv[:, 256:384]
        dsc_ref[1] = dsv[:, 384:512]
        dp_ref[:, 1024:1536] = (dya * u * sv * _dsilu(za)).astype(BF16)

        dyb = dyb_ref[...]
        zb = zb_ref[...]
        o = of_ref[...] + ob_ref[...]
        szb = _silu(zb)
        dszb = _dsilu(zb)
        for h, (r, xh) in enumerate(_head_norm(o, None)):
            sl = slice(128 * h, 128 * h + 128)
            gh = g_ref[:, sl]
            don = dyb[:, sl] * szb[:, sl]
            dp_ref[:, sl] = (dyb[:, sl] * (xh * gh) * dszb[:, sl]).astype(BF16)
            dg_ref[:, sl] += jnp.sum(don * xh, axis=0, keepdims=True)
            dxh = don * gh
            do_ref[:, sl] = r * (dxh - xh * jnp.mean(dxh * xh, axis=-1, keepdims=True))

    r512 = pl.BlockSpec((TOK, 512), lambda i: (i, 0))
    return _pcall(
        body, name="mid_bwd", grid=(m // TOK,),
        in_specs=[r512, r512, pl.BlockSpec((TOK, 512), lambda i: (i, 6)), pl.BlockSpec((TOK, 512), lambda i: (i, 7)),
                  pl.BlockSpec((TOK, 512), lambda i: (i, 8)), pl.BlockSpec((TOK, 256), lambda i: (i, 0)),
                  pl.BlockSpec((2, TOK, 128), lambda i: (0, i, 0)), r512, r512, _full((1, 512)),
                  pl.BlockSpec(memory_space=pl.ANY)],
        out_specs=[pl.BlockSpec((TOK, 1536), lambda i: (i, 2)), pl.BlockSpec((TOK, 256), lambda i: (i, 0)),
                   pl.BlockSpec((2, TOK, 128), lambda i: (0, i, 0)), r512, _full((1, 512))],
        out_shape=[SDS((m, NP), BF16), SDS((m, 256), F32), SDS((2, m, 128), F32), SDS((m, 512), F32), SDS((1, 512), F32)],
        input_output_aliases={10: 0}, compiler_params=_cp(("arbitrary",)),
    )(dya_in, dyb_in, p, p, p, svr, svc, o_f, o_b, gbn, dp)


def _tail_fwd(o_f, o_b, p, vnr, svc, x, tgt, ws01, bs01, gbn, wpa, wpb, wo, gate, gf):
    m = p.shape[0]

    def body(of_ref, ob_ref, zb_ref, ua_ref, za_ref, ga_ref, gb_ref, vnr_ref, svc_ref, x_ref, t_ref, w_ref, b_ref, g_ref,
             wpa_ref, wpb_ref, wo_ref, gate_ref, gf_ref,
             ya_ref, yb_ref, svr_ref, m_ref, dx1_ref, dout_ref, loss_ref, dgate_ref, dgf_ref):
        i = pl.program_id(0)

        @pl.when(i == 0)
        def _():
            loss_ref[...] = jnp.zeros_like(loss_ref)
            dgate_ref[...] = jnp.zeros_like(dgate_ref)
            dgf_ref[...] = jnp.zeros_like(dgf_ref)

        o = of_ref[...] + ob_ref[...]
        zb = zb_ref[...].astype(F32)
        for h, (r, xh) in enumerate(_head_norm(o, None)):
            sl = slice(128 * h, 128 * h + 128)
            yb_ref[:, sl] = ((xh * g_ref[:, sl]) * _silu(zb[:, sl])).astype(BF16)
        for j in range(TOK // AC):
            for g in range(2):
                sv = _nn(w_ref[g], vnr_ref[AC * j:AC * j + AC, AC * g:AC * g + AC]) + b_ref[g]
                svr_ref[AC * j:AC * j + AC, AC * g:AC * g + AC] = sv
        sz = _silu(za_ref[...].astype(F32))
        u = ua_ref[...].astype(F32)
        ya_ref[:, 0:256] = ((u[:, 0:256] * svr_ref[...]) * sz[:, 0:256]).astype(BF16)
        ya_ref[:, 256:384] = ((u[:, 256:384] * svc_ref[0]) * sz[:, 256:384]).astype(BF16)
        ya_ref[:, 384:512] = ((u[:, 384:512] * svc_ref[1]) * sz[:, 384:512]).astype(BF16)
        ya = _nn(ya_ref[...], wpa_ref[...])
        yb = _nn(yb_ref[...], wpb_ref[...])
        mg = (jax.nn.sigmoid(ga_ref[...].astype(F32)) * ya + jax.nn.sigmoid(gb_ref[...].astype(F32)) * yb).astype(BF16)
        m_ref[...] = mg
        out_ = _nn(mg, wo_ref[...])
        x1 = x_ref[...] + gate_ref[...] * out_
        r = lax.rsqrt(jnp.mean(x1 * x1, axis=-1, keepdims=True) + EPS)
        xh = x1 * r
        err = xh * gf_ref[...] - t_ref[...]
        loss_ref[...] += 0.5 * jnp.sum(jnp.mean(err * err, axis=-1, keepdims=True), axis=0, keepdims=True)
        dy = err * (1.0 / D)
        dgf_ref[...] += jnp.sum(dy * xh, axis=0, keepdims=True)
        dxh = dy * gf_ref[...]
        dx1 = r * (dxh - xh * jnp.mean(dxh * xh, axis=-1, keepdims=True))
        dx1_ref[...] = dx1
        dout_ref[...] = (gate_ref[...] * dx1).astype(BF16)
        dgate_ref[...] += jnp.sum(dx1 * out_, axis=0, keepdims=True)

    r512 = pl.BlockSpec((TOK, 512), lambda i: (i, 0))
    row = pl.BlockSpec((TOK, D), lambda i: (i, 0))
    vec = _full((1, D))
    return _pcall(
        body, name="tail_fwd", grid=(m // TOK,),
        in_specs=[r512, r512, pl.BlockSpec((TOK, 512), lambda i: (i, 6)), pl.BlockSpec((TOK, 512), lambda i: (i, 7)),
                  pl.BlockSpec((TOK, 512), lambda i: (i, 8)), row, pl.BlockSpec((TOK, D), lambda i: (i, 1)),
                  pl.BlockSpec((TOK, 256), lambda i: (i, 0)), pl.BlockSpec((2, TOK, 128), lambda i: (0, i, 0)), row, row,
                  _full((2, AC, AC)), _full((2, AC, 1)), _full((1, 512)), _resident((512, D)), _resident((512, D)),
                  _resident((D, D)), vec, vec],
        out_specs=[r512, r512, pl.BlockSpec((TOK, 256), lambda i: (i, 0)), row, row, row, _full((1, 128)), vec, vec],
        out_shape=[SDS((m, 512), BF16), SDS((m, 512), BF16), SDS((m, 256), F32), SDS((m, D), BF16), SDS((m, D), F32),
                   SDS((m, D), BF16), SDS((1, 128), F32), SDS((1, D), F32), SDS((1, D), F32)],
        compiler_params=_cp(("arbitrary",), VMEM_BIG),
    )(o_f, o_b, p, p, p, p, p, vnr, svc, x, tgt, ws01, bs01, gbn, wpa, wpb, wo, gate, gf)


DPR = 3072


def _tail_bwd(dout, ya_in, yb_in, p, svr, svc, o_f, o_b, gbn, wo, wpa, wpb):
    m = p.shape[0]

    def body(dout_ref, ya_ref, yb_ref, ga_ref, gb_ref, zb_ref, ua_ref, za_ref, svr_ref, svc_ref, of_ref, ob_ref, g_ref,
             wo_ref, wpa_ref, wpb_ref,
             dya_ref, dyb_ref, dpg_ref, dpr_ref, dsr_ref, dsc_ref, do_ref, dg_ref):
        i = pl.program_id(0)

        @pl.when(i == 0)
        def _():
            dg_ref[...] = jnp.zeros_like(dg_ref)

        dm_ = _nt(dout_ref[...], wo_ref[...])
        ya = _nn(ya_ref[...], wpa_ref[...])
        yb = _nn(yb_ref[...], wpb_ref[...])
        sa = jax.nn.sigmoid(ga_ref[...].astype(F32))
        sb = jax.nn.sigmoid(gb_ref[...].astype(F32))
        dya16 = (dm_ * sa).astype(BF16)
        dyb16 = (dm_ * sb).astype(BF16)
        dya_ref[...] = dya16
        dyb_ref[...] = dyb16
        dpg_ref[:, 0:D] = (dm_ * ya * (sa * (1.0 - sa))).astype(BF16)
        dpg_ref[:, D:2 * D] = (dm_ * yb * (sb * (1.0 - sb))).astype(BF16)
        dya = _nt(dya16, wpa_ref[...])
        dyb = _nt(dyb16, wpb_ref[...])

        u = ua_ref[...].astype(F32)
        za = za_ref[...].astype(F32)
        sz, dsz = _silu_and_grad(za)
        sv = jnp.concatenate([svr_ref[...], svc_ref[0], svc_ref[1]], axis=1)
        dpr_ref[:, 512:1024] = (dya * sv * sz).astype(BF16)
        dsv = dya * u * sz
        dsr_ref[...] = dsv[:, 0:256]
        dsc_ref[0] = dsv[:, 256:384]
        dsc_ref[1] = dsv[:, 384:512]
        dpr_ref[:, 1024:1536] = (dya * u * sv * dsz).astype(BF16)

        zb = zb_ref[...].astype(F32)
        o = of_ref[...] + ob_ref[...]
        szb, dszb = _silu_and_grad(zb)
        for h, (r, xh) in enumerate(_head_norm(o, None)):
            sl = slice(128 * h, 128 * h + 128)
            gh = g_ref[:, sl]
            don = dyb[:, sl] * szb[:, sl]
            dpr_ref[:, sl] = (dyb[:, sl] * (xh * gh) * dszb[:, sl]).astype(BF16)
            dg_ref[:, sl] += jnp.sum(don * xh, axis=0, keepdims=True)
            dxh = don * gh
            do_ref[:, sl] = r * (dxh - xh * jnp.mean(dxh * xh, axis=-1, keepdims=True))

    r512 = pl.BlockSpec((TOK, 512), lambda i: (i, 0))
    row = pl.BlockSpec((TOK, D), lambda i: (i, 0))
    return _pcall(
        body, name="tail_bwd", grid=(m // TOK,),
        in_specs=[row, r512, r512, row, pl.BlockSpec((TOK, D), lambda i: (i, 1)), pl.BlockSpec((TOK, 512), lambda i: (i, 6)),
                  pl.BlockSpec((TOK, 512), lambda i: (i, 7)), pl.BlockSpec((TOK, 512), lambda i: (i, 8)),
                  pl.BlockSpec((TOK, 256), lambda i: (i, 0)), pl.BlockSpec((2, TOK, 128), lambda i: (0, i, 0)), r512, r512,
                  _full((1, 512)), _resident((D, D)), _resident((512, D)), _resident((512, D))],
        out_specs=[row, row, pl.BlockSpec((TOK, 2 * D), lambda i: (i, 0)), pl.BlockSpec((TOK, 1536), lambda i: (i, 0)),
                   pl.BlockSpec((TOK, 256), lambda i: (i, 0)), pl.BlockSpec((2, TOK, 128), lambda i: (0, i, 0)), r512, _full((1, 512))],
        out_shape=[SDS((m, D), BF16), SDS((m, D), BF16), SDS((m, 2 * D), BF16), SDS((m, DPR), BF16), SDS((m, 256), F32),
                   SDS((2, m, 128), F32), SDS((m, 512), F32), SDS((1, 512), F32)],
        compiler_params=_cp(("arbitrary",), VMEM_BIG),
    )(dout, ya_in, yb_in, p, p, p, p, p, svr, svc, o_f, o_b, gbn, wo, wpa, wpb)


def _mm_multi(pairs, *, tm, tn, out_dtype, name):
    m = pairs[0][0].shape[0]
    n = pairs[0][1].shape[1]
    nks = [a.shape[1] // tk for a, _, tk in pairs]
    starts = [sum(nks[:i]) for i in range(len(pairs))]
    total = sum(nks)

    def body(*refs):
        o_ref, acc_ref = refs[-2], refs[-1]
        kk = pl.program_id(2)
        for idx in range(len(pairs)):
            a_ref, b_ref = refs[2 * idx], refs[2 * idx + 1]

            @pl.when((kk >= starts[idx]) & (kk < starts[idx] + nks[idx]))
            def _(a_ref=a_ref, b_ref=b_ref, first=(idx == 0)):
                part = _nn(a_ref[...].astype(BF16), b_ref[...].astype(BF16))
                if first:
                    @pl.when(kk == 0)
                    def _():
                        acc_ref[...] = part

                    @pl.when(kk > 0)
                    def _():
                        acc_ref[...] += part
                else:
                    acc_ref[...] += part

        @pl.when(kk == total - 1)
        def _():
            o_ref[...] = acc_ref[...].astype(out_dtype)

    in_specs, args = [], []
    for (a, b, tk), st, nk in zip(pairs, starts, nks):
        in_specs.append(pl.BlockSpec((tm, tk), lambda i, j, kk, st=st, nk=nk: (i, jnp.clip(kk - st, 0, nk - 1))))
        in_specs.append(pl.BlockSpec((tk, tn), lambda i, j, kk, st=st, nk=nk: (jnp.clip(kk - st, 0, nk - 1), j)))
        args += [a, b]
    return _pcall(
        body, name=name, grid=(m // tm, n // tn, total), in_specs=in_specs,
        out_specs=pl.BlockSpec((tm, tn), lambda i, j, kk: (i, j)), out_shape=SDS((m, n), out_dtype),
        scratch_shapes=[pltpu.VMEM((tm, tn), F32)], compiler_params=_cp(("parallel", "parallel", "arbitrary"), VMEM_BIG),
    )(*args)


def _ln_bwd(dsr, vnr, dvnc, p, ws01t, ln_g, dp):
    m = p.shape[0]

    def body(dsr_ref, vnr_ref, dvc_ref, va_ref, wt_ref, g_ref, dpi_ref, dp_ref, dw_ref, db_ref, dlg_ref, dlb_ref, dvn_ref):
        i = pl.program_id(0)

        @pl.when(i == 0)
        def _():
            dw_ref[...] = jnp.zeros_like(dw_ref)
            db_ref[...] = jnp.zeros_like(db_ref)
            dlg_ref[...] = jnp.zeros_like(dlg_ref)
            dlb_ref[...] = jnp.zeros_like(dlb_ref)

        for j in range(TOK // AC):
            for g in range(2):
                d = dsr_ref[AC * j:AC * j + AC, AC * g:AC * g + AC]
                d16 = d.astype(BF16)
                dvn_ref[AC * j:AC * j + AC, AC * g:AC * g + AC] = _nn(wt_ref[g], d16)
                dw_ref[g] += _nt(d16, vnr_ref[AC * j:AC * j + AC, AC * g:AC * g + AC])
                db_ref[g] += jnp.sum(d, axis=1, keepdims=True)
        dvn_ref[:, 256:384] = dvc_ref[0]
        dvn_ref[:, 384:512] = dvc_ref[1]
        dvn = dvn_ref[...]
        xf = va_ref[...].astype(F32)
        xc = xf - jnp.mean(xf, axis=-1, keepdims=True)
        rs = lax.rsqrt(jnp.mean(xc * xc, axis=-1, keepdims=True) + EPS)
        xh = xc * rs
        dlg_ref[...] += jnp.sum(dvn * xh, axis=0, keepdims=True)
        dlb_ref[...] += jnp.sum(dvn, axis=0, keepdims=True)
        dxh = dvn * g_ref[...]
        dva = rs * (dxh - jnp.mean(dxh, axis=-1, keepdims=True) - xh * jnp.mean(dxh * xh, axis=-1, keepdims=True))
        dp_ref[...] = dva.astype(BF16)

    return _pcall(
        body, name="ln_bwd", grid=(m // TOK,),
        in_specs=[pl.BlockSpec((TOK, 256), lambda i: (i, 0)), pl.BlockSpec((TOK, 256), lambda i: (i, 0)),
                  pl.BlockSpec((2, TOK, 128), lambda i: (0, i, 0)), pl.BlockSpec((TOK, 512), lambda i: (i, 9)),
                  _full((2, AC, AC)), _full((1, 512)), pl.BlockSpec(memory_space=pl.ANY)],
        out_specs=[pl.BlockSpec((TOK, 512), lambda i: (i, 3)), _full((2, AC, AC)), _full((2, AC, 1)), _full((1, 512)), _full((1, 512))],
        out_shape=[SDS((m, DPR), BF16), SDS((2, AC, AC), F32), SDS((2, AC, 1), F32), SDS((1, 512), F32), SDS((1, 512), F32)],
        scratch_shapes=[pltpu.VMEM((TOK, 512), F32)],
        input_output_aliases={6: 0}, compiler_params=_cp(("arbitrary",)),
    )(dsr, vnr, dvnc, p, ws01t, ln_g, dp)


def _tri_mm(tri, a):
    a1 = a.astype(BF16)
    r1 = a - a1.astype(F32)
    a2 = r1.astype(BF16)
    a3 = (r1 - a2.astype(F32)).astype(BF16)
    n = a.shape[1]
    r = _nn(tri, jnp.concatenate([a1, a2, a3], axis=1))
    return r[:, 0:n] + r[:, n:2 * n] + r[:, 2 * n:3 * n]


def _gla_masks(reverse):
    ri = lax.broadcasted_iota(jnp.int32, (CH, CH), 0)
    ci = lax.broadcasted_iota(jnp.int32, (CH, CH), 1)
    vis = (ci >= ri) if reverse else (ci <= ri)
    vis_t = (ci <= ri) if reverse else (ci >= ri)
    r4 = lax.broadcasted_iota(jnp.int32, (4 * CH, CH), 0) & (CH - 1)
    c4 = lax.broadcasted_iota(jnp.int32, (4 * CH, CH), 1)
    vis4 = (c4 >= r4) if reverse else (c4 <= r4)
    vis4_t = (c4 <= r4) if reverse else (c4 >= r4)
    lane = lax.broadcasted_iota(jnp.int32, (1, 256), 1)
    hm = [(lane >= CH * h) & (lane < CH * h + CH) for h in range(4)]
    return vis, vis_t, vis4, vis4_t, hm


def _stack_heads(x, hm):
    return jnp.concatenate([jnp.where(hm[h], x, 0.0).astype(BF16) for h in range(4)], axis=0)


def _diag_heads(full, hm):
    r = full.shape[0] // 4
    acc = jnp.where(hm[0], full[0:r], 0.0)
    for h in range(1, 4):
        acc = acc + jnp.where(hm[h], full[r * h:r * h + r], 0.0)
    return acc


def _rows_of_heads(x):
    return jnp.concatenate([x[:, 128 * h:128 * h + 128] for h in range(4)], axis=0)


def _lane_vis(reverse, transpose):
    ri = lax.broadcasted_iota(jnp.int32, (CH, 4 * CH), 0)
    ci = lax.broadcasted_iota(jnp.int32, (CH, 4 * CH), 1) & (CH - 1)
    return (ci >= ri) if (reverse != transpose) else (ci <= ri)


def _gla_fwd(p, qkv_blk, lr, lrw, gbias, s0, *, reverse, name):
    m = p.shape[0]
    tb = min(GLA_TB, m)
    nb = m // tb
    nc = tb // CH
    rmap = (lambda i: nb - 1 - i) if reverse else (lambda i: i)

    def body(qkv_ref, lr_ref, lrw_ref, gb_ref, s0_ref, o_ref, sb_ref, sfin_ref, st_ref):
        i = pl.program_id(0)

        @pl.when(i == 0)
        def _():
            st_ref[...] = s0_ref[...]

        vis, _, vis4, _, hm = _gla_masks(reverse)
        tri = vis.astype(F32).astype(BF16)
        logits = _nn(lr_ref[...].astype(BF16), lrw_ref[...]) + gb_ref[...]
        a_all = _logsig(logits) * (1.0 / 16.0)
        st = st_ref[...]
        for c in (range(nc - 1, -1, -1) if reverse else range(nc)):
            rows = slice(CH * c, CH * c + CH)
            b = _tri_mm(tri, a_all[rows])
            bl = b[0:1] if reverse else b[CH - 1:CH]
            q = qkv_ref[rows, 0:256].astype(F32) * 0.125
            k = qkv_ref[rows, 256:512].astype(F32)
            v16 = qkv_ref[rows, 512:1024].astype(BF16)
            qd = q * jnp.exp(b)
            kd16 = (k * jnp.exp(-b)).astype(BF16)
            kdec16 = (k * jnp.exp(bl - b)).astype(BF16)
            qstack = _stack_heads(qd, hm)
            sc = jnp.where(vis4, _nt(qstack, kd16), 0.0).astype(BF16)
            inter = _nt(qstack, st.astype(BF16))
            for h in range(4):
                o_ref[rows, 128 * h:128 * h + 128] = (
                    _nn(sc[CH * h:CH * h + CH], v16[:, 128 * h:128 * h + 128]) + inter[CH * h:CH * h + CH])
            sb_ref[c] = st
            st = st * jnp.exp(bl) + _diag_heads(_tn(v16, kdec16), hm)
        st_ref[...] = st

        @pl.when(i == nb - 1)
        def _():
            sfin_ref[...] = st

    return _pcall(
        body, name=name, grid=(nb,),
        in_specs=[pl.BlockSpec((tb,1024), lambda i: (rmap(i), qkv_blk)), pl.BlockSpec((tb,LRW), lambda i: (rmap(i), 0)),
                  _full((LRW, 256)), _full((1, 256)), _full((128, 256))],
        out_specs=[pl.BlockSpec((tb,512), lambda i: (rmap(i), 0)), pl.BlockSpec((nc, 128, 256), lambda i: (rmap(i), 0, 0)),
                   _full((128, 256))],
        out_shape=[SDS((m, 512), F32), SDS((m // CH, 128, 256), F32), SDS((128, 256), F32)],
        scratch_shapes=[pltpu.VMEM((128, 256), F32)], compiler_params=_cp(("arbitrary",)),
    )(p, lr, lrw, gbias, s0)


def _gla_fwd2(p, qkv_blk, lr, lrws, gbiases, s0s, name):
    m = p.shape[0]
    tb = min(GLA_TB, m)
    nb = m // tb
    nc = tb // CH

    def body(qkv_f, lr_f, qkv_b, lr_b, lrw_f, lrw_b, gb_f, gb_b, s0_f, s0_b,
             o_f, sb_f, sfin_f, o_b, sb_b, sfin_b, st_f, st_b):
        i = pl.program_id(0)

        @pl.when(i == 0)
        def _():
            st_f[...] = s0_f[...]
            st_b[...] = s0_b[...]

        dirs = []
        for reverse, qkv_ref, lr_ref, lrw_ref, gb_ref, o_ref, sb_ref, st_ref in (
                (False, qkv_f, lr_f, lrw_f, gb_f, o_f, sb_f, st_f), (True, qkv_b, lr_b, lrw_b, gb_b, o_b, sb_b, st_b)):
            vis, _, vis4, _, hm = _gla_masks(reverse)
            logits = _nn(lr_ref[...].astype(BF16), lrw_ref[...]) + gb_ref[...]
            dirs.append(dict(reverse=reverse, qkv=qkv_ref, o=o_ref, sb=sb_ref, vis4=vis4, hm=hm,
                             tri=vis.astype(F32).astype(BF16), a=_logsig(logits) * (1.0 / 16.0), st=st_ref[...]))
        for step in range(nc):
            for d in dirs:
                c = nc - 1 - step if d["reverse"] else step
                rows = slice(CH * c, CH * c + CH)
                b = _tri_mm(d["tri"], d["a"][rows])
                bl = b[0:1] if d["reverse"] else b[CH - 1:CH]
                q = d["qkv"][rows, 0:256].astype(F32) * 0.125
                k = d["qkv"][rows, 256:512].astype(F32)
                v16 = d["qkv"][rows, 512:1024].astype(BF16)
                qd = q * jnp.exp(b)
                kd16 = (k * jnp.exp(-b)).astype(BF16)
                kdec16 = (k * jnp.exp(bl - b)).astype(BF16)
                qstack = _stack_heads(qd, d["hm"])
                sc = jnp.where(d["vis4"], _nt(qstack, kd16), 0.0).astype(BF16)
                inter = _nt(qstack, d["st"].astype(BF16))
                for h in range(4):
                    d["o"][rows, 128 * h:128 * h + 128] = (
                        _nn(sc[CH * h:CH * h + CH], v16[:, 128 * h:128 * h + 128]) + inter[CH * h:CH * h + CH])
                d["sb"][c] = d["st"]
                d["st"] = d["st"] * jnp.exp(bl) + _diag_heads(_tn(v16, kdec16), d["hm"])
        st_f[...] = dirs[0]["st"]
        st_b[...] = dirs[1]["st"]

        @pl.when(i == nb - 1)
        def _():
            sfin_f[...] = dirs[0]["st"]
            sfin_b[...] = dirs[1]["st"]

    fw = lambda i: i
    bw = lambda i: nb - 1 - i
    in_specs = []
    for rm in (fw, bw):
        in_specs += [pl.BlockSpec((tb, 1024), lambda i, rm=rm: (rm(i), qkv_blk)), pl.BlockSpec((tb, LRW), lambda i, rm=rm: (rm(i), 0))]
    in_specs += [_full((LRW, 256))] * 2 + [_full((1, 256))] * 2 + [_full((128, 256))] * 2
    out_specs, out_shape = [], []
    for rm in (fw, bw):
        out_specs += [pl.BlockSpec((tb, 512), lambda i, rm=rm: (rm(i), 0)), pl.BlockSpec((nc, 128, 256), lambda i, rm=rm: (rm(i), 0, 0)),
                      _full((128, 256))]
        out_shape += [SDS((m, 512), F32), SDS((m // CH, 128, 256), F32), SDS((128, 256), F32)]
    return _pcall(
        body, name=name, grid=(nb,), in_specs=in_specs, out_specs=out_specs, out_shape=out_shape,
        scratch_shapes=[pltpu.VMEM((128, 256), F32), pltpu.VMEM((128, 256), F32)], compiler_params=_cp(("arbitrary",), VMEM_BIG),
    )(p, lr, p, lr, lrws[0], lrws[1], gbiases[0], gbiases[1], s0s[0], s0s[1])


def _gla_bwd(p, qkv_blk, lr, lrw, lrwt, gbias, sb, dsfin, do, prev, dp, *, reverse, name):
    m = p.shape[0]
    tb = min(GLA_TB, m)
    nb = m // tb
    nc = tb // CH
    rmap = (lambda i: i) if reverse else (lambda i: nb - 1 - i)
    has_prev = prev is not None
    has_dp = dp is not None

    def body(*refs):
        refs = list(refs)
        qkv_ref, lr_ref, lrw_ref, lrwt_ref, gb_ref, sb_ref, dsfin_ref, do_ref = refs[:8]
        refs = refs[8:]
        if has_prev:
            pq_ref, plr_ref = refs[:2]
            refs = refs[2:]
        if has_dp:
            refs = refs[1:]
        dqkv_ref, dlr_ref, dw2_ref, dgb_ref, ds0_ref, dst_ref, dlog_ref = refs
        i = pl.program_id(0)

        @pl.when(i == 0)
        def _():
            dst_ref[...] = dsfin_ref[...]
            dw2_ref[...] = jnp.zeros_like(dw2_ref)
            dgb_ref[...] = jnp.zeros_like(dgb_ref)

        vis, vis_t, vis4, vis4_t, hm = _gla_masks(reverse)
        tri = vis.astype(F32).astype(BF16)
        tri_t = vis_t.astype(F32).astype(BF16)
        lane_vis = _lane_vis(reverse, False)
        lane_vis_t = _lane_vis(reverse, True)
        lr16 = lr_ref[...].astype(BF16)
        logits = _nn(lr16, lrw_ref[...]) + gb_ref[...]
        a_all = _logsig(logits) * (1.0 / 16.0)
        dsig = (1.0 - jax.nn.sigmoid(logits)) * (1.0 / 16.0)
        dst = dst_ref[...]
        for c in (range(nc) if reverse else range(nc - 1, -1, -1)):
            rows = slice(CH * c, CH * c + CH)
            b = _tri_mm(tri, a_all[rows])
            bl = b[0:1] if reverse else b[CH - 1:CH]
            eb = jnp.exp(b)
            enb = jnp.exp(-b)
            ebl = jnp.exp(bl - b)
            el = jnp.exp(bl)
            q = qkv_ref[rows, 0:256].astype(F32) * 0.125
            k = qkv_ref[rows, 256:512].astype(F32)
            v16 = qkv_ref[rows, 512:1024].astype(BF16)
            do16 = do_ref[rows, :].astype(BF16)
            qd = q * eb
            kd = k * enb
            kdec = k * ebl
            st = sb_ref[c]
            st16 = st.astype(BF16)
            dst16 = dst.astype(BF16)
            qd16 = qd.astype(BF16)
            kd16 = kd.astype(BF16)
            qstack = _stack_heads(qd, hm)
            kstack = _stack_heads(kd, hm)
            kdecstack = _stack_heads(kdec, hm)
            pt = jnp.where(vis4_t, _nt(kstack, qd16), 0.0).astype(BF16)
            dvinter = _nt(kdecstack, dst16)
            do_rows = _rows_of_heads(do16)
            v_rows = _rows_of_heads(v16)
            dp_cat = jnp.where(lane_vis, _diag_heads(_nt(do_rows, v_rows), hm), 0.0).astype(BF16)
            dpt_cat = jnp.where(lane_vis_t, _diag_heads(_nt(v_rows, do_rows), hm), 0.0).astype(BF16)
            dqd = _nn(dp_cat, kstack) + _diag_heads(_nn(do_rows, st16), hm)
            dkd = _nn(dpt_cat, qstack)
            dkdec = _diag_heads(_nn(v_rows, dst16), hm)
            for h in range(4):
                rh = slice(CH * h, CH * h + CH)
                dv_h = _nn(pt[rh], do_rows[rh]) + dvinter[rh]
                if has_prev:
                    dv_h = dv_h + pq_ref[rows, 512 + 128 * h:512 + 128 * h + 128]
                dqkv_ref[rows, 512 + 128 * h:512 + 128 * h + 128] = dv_h.astype(dqkv_ref.dtype)
            dq = dqd * eb * 0.125
            dk = dkd * enb + dkdec * ebl
            if has_prev:
                dq = dq + pq_ref[rows, 0:256]
                dk = dk + pq_ref[rows, 256:512]
            dqkv_ref[rows, 0:256] = dq.astype(dqkv_ref.dtype)
            dqkv_ref[rows, 256:512] = dk.astype(dqkv_ref.dtype)
            g_kdec = dkdec * kdec
            db = dqd * qd - dkd * kd - g_kdec
            dbl = jnp.sum(g_kdec, axis=0, keepdims=True) + jnp.sum(st * dst, axis=0, keepdims=True) * el
            da = _tri_mm(tri_t, db) + dbl
            dlog_ref[rows, :] = da * dsig[rows]
            dst = dst * el + _diag_heads(_tn(do16, qd16), hm)
        dst_ref[...] = dst
        dlog = dlog_ref[...]
        dlog16 = dlog.astype(BF16)
        dlr = _nn(dlog16, lrwt_ref[...])
        if has_prev:
            dlr = dlr + plr_ref[...]
        dlr_ref[...] = dlr
        dw2_ref[...] += _tn(lr16, dlog16)
        dgb_ref[...] += jnp.sum(dlog, axis=0, keepdims=True)

        @pl.when(i == nb - 1)
        def _():
            ds0_ref[...] = dst

    in_specs = [pl.BlockSpec((tb,1024), lambda i: (rmap(i), qkv_blk)), pl.BlockSpec((tb,LRW), lambda i: (rmap(i), 0)),
                _full((LRW, 256)), _full((256, LRW)), _full((1, 256)), pl.BlockSpec((nc, 128, 256), lambda i: (rmap(i), 0, 0)),
                _full((128, 256)), pl.BlockSpec((tb,512), lambda i: (rmap(i), 0))]
    args = [p, lr, lrw, lrwt, gbias, sb, dsfin, do]
    if has_prev:
        in_specs += [pl.BlockSpec((tb,1024), lambda i: (rmap(i), 0)), pl.BlockSpec((tb,LRW), lambda i: (rmap(i), 0))]
        args += list(prev)
    aliases = {}
    if has_dp:
        in_specs.append(pl.BlockSpec(memory_space=pl.ANY))
        aliases = {len(args): 0}
        args.append(dp)
        dq_spec = pl.BlockSpec((tb,1024), lambda i: (rmap(i), 2))
        dq_shape = SDS(dp.shape, dp.dtype)
    else:
        dq_spec = pl.BlockSpec((tb,1024), lambda i: (rmap(i), 0))
        dq_shape = SDS((m, 1024), F32)
    return _pcall(
        body, name=name, grid=(nb,), in_specs=in_specs,
        out_specs=[dq_spec, pl.BlockSpec((tb,LRW), lambda i: (rmap(i), 0)), _full((LRW, 256)), _full((1, 256)), _full((128, 256))],
        out_shape=[dq_shape, SDS((m, LRW), F32), SDS((LRW, 256), F32), SDS((1, 256), F32), SDS((128, 256), F32)],
        scratch_shapes=[pltpu.VMEM((128, 256), F32), pltpu.VMEM((tb,256), F32)],
        input_output_aliases=aliases, compiler_params=_cp(("arbitrary",)),
    )(*args)


def _device_step(x, c, ctx, c_ctx, tgt, wm, bm, ng, wit_g, wit_r, wlrt, ln_g, ln_b, ws, bs, w2, gb2, gbn, wpa, wpb, wo, gf,
                 exchange=None, shards=None):
    L = x.shape[0]
    wit_qkv = wit_r[2048:3072]
    ws16 = ws.astype(BF16)
    wst16 = jnp.swapaxes(ws, 1, 2).astype(BF16)
    bscol = bs[:, :, None]
    lrw = [jnp.zeros((LRW, 256), F32).at[16 * r:16 * r + 16].set(w2[r]).astype(BF16) for r in range(2)]
    lrwt = [w.T for w in lrw]
    gbias = [gb2[r:r + 1] for r in range(2)]

    cc = jnp.zeros((8, D), F32).at[0:1].set(c).at[1:2].set(c_ctx)
    mod = _modvec(cc, wm, bm)
    shift, scale, gate = mod[0:1, 0:D], mod[0:1, D:2 * D], mod[0:1, 2 * D:3 * D]
    shift_c, scale_c = mod[1:2, 0:D], mod[1:2, D:2 * D]

    hc = _prep_h(ctx, ng, scale_c, shift_c, "prep_hc")
    pc = _mm(hc, wit_qkv, tm=256, tn=1024, tk=D, out_dtype=F32, name="mm_pc", b_t=True)
    plrc = _mm(hc, wlrt, tm=256, tn=LRW, tk=D, out_dtype=F32, name="mm_plrc", b_t=True)
    zero_s = jnp.zeros((128, 256), F32)
    _, sbc_f, sc_f, _, sbc_b, sc_b = _gla_fwd2(pc, 0, plrc, lrw, gbias, (zero_s, zero_s), "gla_fwd_c")

    h, p, plr, vnr, vnc, late = _proj_fwd(x, ng, scale, shift, wit_g, wit_r, wlrt, ln_g, ln_b,
                                          shards if shards is not None else ())
    if shards is not None:
        me_xy = 2 * lax.axis_index("x") + lax.axis_index("y")
        g_wpa, g_wpb, g_wo = (_own(g, s_, me_xy) for g, s_ in zip(late, shards))
        wpa = jnp.swapaxes(g_wpa, 0, 1).reshape(512, D)
        wpb = jnp.swapaxes(g_wpb, 0, 1).reshape(512, D)
        wo = g_wo.reshape(D, D)
    o_f, sb_f, _, o_b, sb_b, _ = _gla_fwd2(p, 2, plr, lrw, gbias, (sc_f, sc_b), "gla_fwd")
    svc = _colmix_fwd(vnc.reshape(2, AC, L), ws16[2:4], bscol[2:4]).reshape(2, L, 128)
    ya_in, yb_in, svr, mrg, dx1, dout, loss, dgate, dgf = _tail_fwd(
        o_f, o_b, p, vnr, svc, x, tgt, ws16[0:2], bscol[0:2], gbn, wpa, wpb, wo, gate, gf)

    dya, dyb, dp_g, dp, dsr, dsc, do, dgbn = _tail_bwd(dout, ya_in, yb_in, p, svr, svc, o_f, o_b, gbn, wo, wpa, wpb)
    dwo = _mm_tn(mrg, dout, ta=D, tn=D, tk=1024, name="mm_dwo")
    dwpa = _mm_tn(ya_in, dya, ta=512, tn=D, tk=1024, name="mm_dwpa")
    dwpb = _mm_tn(yb_in, dyb, ta=512, tn=D, tk=1024, name="mm_dwpb")
    dvnc, dws23, dbs23 = _colmix_bwd(dsc.reshape(2, AC, L), vnc.reshape(2, AC, L), wst16[2:4])
    dp, dws01, dbs01, dlng, dlnb = _ln_bwd(dsr, vnr, dvnc.reshape(2, L, 128), p, wst16[0:2], ln_g, dp)
    zero_ds = jnp.zeros((128, 256), F32)
    dqkv_f, dlr_f, dw2_f, dgb_f, ds0_f = _gla_bwd(p, 2, plr, lrw[0], lrwt[0], gbias[0], sb_f, zero_ds, do, None, None,
                                                  reverse=False, name="gla_bwd_f")
    dp, dlr, dw2_b, dgb_b, ds0_b = _gla_bwd(p, 2, plr, lrw[1], lrwt[1], gbias[1], sb_b, zero_ds, do, (dqkv_f, dlr_f), dp,
                                            reverse=True, name="gla_bwd_b")
    zero_do = jnp.zeros((ctx.shape[0], 512), F32)
    dqkvc_f, dlrc_f, dw2c_f, dgbc_f, _ = _gla_bwd(pc, 0, plrc, lrw[0], lrwt[0], gbias[0], sbc_f, ds0_f, zero_do, None, None,
                                                  reverse=False, name="gla_bwd_cf")
    dqkvc, dlrc, dw2c_b, dgbc_b, _ = _gla_bwd(pc, 0, plrc, lrw[1], lrwt[1], gbias[1], sbc_b, ds0_b, zero_do,
                                              (dqkvc_f, dlrc_f), None, reverse=True, name="gla_bwd_cb")
    dhc = _mm(dqkvc, wit_qkv, tm=256, tn=D, tk=1024, out_dtype=F32, name="mm_dhc")
    dhc = _mm(dlrc, wlrt, tm=256, tn=D, tk=LRW, out_dtype=F32, name="mm_dhc_lr", acc=dhc)
    _, dng_c, dscale_c, dshift_c = _prep_bwd(ctx, dhc, None, ng, scale_c, "prep_bwd_c")

    dwit_g = _mm_tn(dp_g, h, ta=1024, tn=D, tk=2048, name="mm_dwi_g")
    dwit_r = _mm_tn(dp, h, ta=1024, tn=D, tk=2048, name="mm_dwi_r")
    dwit_qkv = _mm_tn(dqkvc, hc, ta=1024, tn=D, tk=256, name="mm_dwi_c", acc=dwit_r[2048:3072])
    dwlrt = _mm_tn(dlr, h, ta=LRW, tn=D, tk=2048, name="mm_dwlr")
    dwlrt = _mm_tn(dlrc, hc, ta=LRW, tn=D, tk=256, name="mm_dwlr_c", acc=dwlrt)
    big = dict(dwit_g=dwit_g, dwit_r=dwit_r, dwit_qkv=dwit_qkv, dwlrt=dwlrt, dwpa=dwpa, dwpb=dwpb, dwo=dwo)

    send = exchange(big) if exchange is not None else ()
    (dx, dng, dscale, dshift), got = _proj_bwd(dp_g, dp, dlr, wit_g, wit_r, wlrt, x, dx1, ng, scale, send)

    dmodc = jnp.concatenate([dshift_c, dscale_c], axis=1)
    dscc = _dcctx(jnp.zeros((8, 2 * D), F32).at[0:1].set(dmodc), wm)[0:1]
    dw2p = dw2_f + dw2c_f, dw2_b + dw2c_b
    return dict(
        loss=loss[0, 0], dx=dx, got=got, **big,
        dmod=jnp.concatenate([dshift, dscale, dgate], axis=1), dmodc=dmodc, dscc=dscc, dng=dng + dng_c,
        dlng=dlng, dlnb=dlnb, dws=jnp.concatenate([dws01, dws23], axis=0),
        dbs=jnp.concatenate([dbs01, dbs23], axis=0)[:, :, 0], dgbn=dgbn, dgf=dgf,
        dw2=jnp.stack([dw2p[0][0:16], dw2p[1][16:32]]), dgb2=jnp.concatenate([dgb_f + dgbc_f, dgb_b + dgbc_b], axis=0),
    )


ANY = pl.BlockSpec(memory_space=pl.ANY)


def _coords():
    return lax.axis_index("x"), lax.axis_index("y"), lax.axis_index("c")


def _flip(v, bit):
    return 1 - v if bit else v


def _remote(src, dst, send_sem, recv_sem, dev):
    return pltpu.make_async_remote_copy(src_ref=src, dst_ref=dst, send_sem=send_sem, recv_sem=recv_sem,
                                        device_id=dev, device_id_type=MESH)


def _own(out, block, idx):
    return lax.dynamic_update_slice_in_dim(out, block[None], idx, axis=0)


def _half_idx(shape, axis, which, lead=()):
    idx = [pl.ds(0, d) for d in shape]
    h = shape[axis] // 2
    idx[axis] = pl.ds(which * h, h)
    return tuple(lead) + tuple(idx)


def _gather_weights(split, whole, name):
    ns, nw = len(split), len(whole)
    n = ns + nw
    arrs = [a for a, _ in split] + list(whole)

    def body(*refs):
        ins, outs = refs[:n], refs[n:2 * n]
        a_send, a_recv, b_send, b_recv = refs[2 * n:]
        x, y, c = _coords()
        me = 2 * x + y
        sib = (x, y, 1 - c)
        peers = [(1 - x, y), (x, 1 - y), (1 - x, 1 - y)]

        def half(k, slot, which):
            return outs[k].at[_half_idx(arrs[k].shape, split[k][1], which, lead=(slot,))]

        sends = []
        for k in range(n):
            for j, (px, py) in enumerate(peers):
                if k < ns:
                    rc = _remote(ins[k].at[_half_idx(arrs[k].shape, split[k][1], c)], half(k, me, c), a_send.at[3 * k + j],
                                 a_recv.at[3 * k + j], (px, py, c))
                else:
                    rc = _remote(ins[k], outs[k].at[me], a_send.at[3 * k + j], a_recv.at[3 * k + j], (px, py, c))
                rc.start()
                sends.append(rc)
        for k in range(ns):
            for j, (px, py) in enumerate(peers):
                landed = half(k, 2 * px + py, c)
                _remote(landed, landed, a_send.at[3 * k + j], a_recv.at[3 * k + j], (px, py, c)).wait_recv()
                fw = _remote(landed, landed, b_send.at[3 * k + j], b_recv.at[3 * k + j], sib)
                fw.start()
                sends.append(fw)
        for k in range(ns, n):
            for j, (px, py) in enumerate(peers):
                landed = outs[k].at[2 * px + py]
                _remote(landed, landed, a_send.at[3 * k + j], a_recv.at[3 * k + j], (px, py, c)).wait_recv()
        for k in range(ns):
            for j, (px, py) in enumerate(peers):
                passed = half(k, 2 * px + py, 1 - c)
                _remote(passed, passed, b_send.at[3 * k + j], b_recv.at[3 * k + j], sib).wait_recv()
        for rc in sends:
            rc.wait_send()

    outs = _pcall(
        body, name=name, in_specs=[ANY] * n, out_specs=[ANY] * n,
        out_shape=[SDS((4,) + a.shape, a.dtype) for a in arrs],
        scratch_shapes=[pltpu.SemaphoreType.DMA((3 * n,)), pltpu.SemaphoreType.DMA((3 * n,)), pltpu.SemaphoreType.DMA((3 * ns,)),
                        pltpu.SemaphoreType.DMA((3 * ns,))],
    )(*arrs)
    me_xy = 2 * lax.axis_index("x") + lax.axis_index("y")
    return [_own(o, a, me_xy) for o, a in zip(outs, arrs)]


def _gather_all(a, swap, name):
    masks = [(mx, my, mc) for mx in range(2) for my in range(2) for mc in range(2)][1:]
    n = len(swap)

    def body(*refs):
        in_ref, sw_in = refs[0], refs[1:1 + n]
        out_ref, sw_out = refs[1 + n], refs[2 + n:2 + 2 * n]
        send_sems, recv_sems = refs[2 + 2 * n:]
        x, y, c = _coords()
        me = 4 * x + 2 * y + c
        sends = []
        for j, (mx, my, mc) in enumerate(masks):
            rc = _remote(in_ref, out_ref.at[me], send_sems.at[j], recv_sems.at[j], (_flip(x, mx), _flip(y, my), _flip(c, mc)))
            rc.start()
            sends.append(rc)
        for k in range(n):
            rc = _remote(sw_in[k], sw_out[k], send_sems.at[7 + k], recv_sems.at[7 + k], (x, y, 1 - c))
            rc.start()
            sends.append(rc)
        for j, (mx, my, mc) in enumerate(masks):
            px, py, pc = _flip(x, mx), _flip(y, my), _flip(c, mc)
            landed = out_ref.at[4 * px + 2 * py + pc]
            _remote(landed, landed, send_sems.at[j], recv_sems.at[j], (px, py, pc)).wait_recv()
        for k in range(n):
            _remote(sw_out[k], sw_out[k], send_sems.at[7 + k], recv_sems.at[7 + k], (x, y, 1 - c)).wait_recv()
        for rc in sends:
            rc.wait_send()

    res = _pcall(
        body, name=name, in_specs=[ANY] * (1 + n), out_specs=[ANY] * (1 + n),
        out_shape=[SDS((8,) + a.shape, a.dtype)] + [SDS(s_.shape, s_.dtype) for s_ in swap],
        scratch_shapes=[pltpu.SemaphoreType.DMA((7 + n,)), pltpu.SemaphoreType.DMA((7 + n,))],
    )(a, *swap)
    return _own(res[0], a, 4 * lax.axis_index("x") + 2 * lax.axis_index("y") + lax.axis_index("c")), list(res[1:])


def _half_shape(shape, axis):
    return tuple(d // 2 if i == axis else d for i, d in enumerate(shape))


def _swap_half_c(arrs, axes, name):
    n = len(arrs)

    def body(*refs):
        ins, outs = refs[:n], refs[n:2 * n]
        send_sems, recv_sems = refs[2 * n:]
        x, y, c = _coords()
        sends = []
        for k in range(n):
            rc = _remote(ins[k].at[_half_idx(arrs[k].shape, axes[k], 1 - c)], outs[k], send_sems.at[k], recv_sems.at[k],
                         (x, y, 1 - c))
            rc.start()
            sends.append(rc)
        for rc in sends:
            rc.wait()

    return _pcall(
        body, name=name, in_specs=[ANY] * n, out_specs=[ANY] * n,
        out_shape=[SDS(_half_shape(a.shape, ax), a.dtype) for a, ax in zip(arrs, axes)],
        scratch_shapes=[pltpu.SemaphoreType.DMA((n,)), pltpu.SemaphoreType.DMA((n,))],
    )(*arrs)


def _a2a_xy(arrs, name):
    n = len(arrs)

    def body(*refs):
        ins, outs = refs[:n], refs[n:2 * n]
        send_sems, recv_sems = refs[2 * n:]
        x, y, c = _coords()
        me = 2 * x + y
        peers = [(1 - x, y), (x, 1 - y), (1 - x, 1 - y)]
        sends = []
        for k in range(n):
            for j, (px, py) in enumerate(peers):
                rc = _remote(ins[k].at[2 * px + py], outs[k].at[me], send_sems.at[3 * k + j], recv_sems.at[3 * k + j], (px, py, c))
                rc.start()
                sends.append(rc)
        for k in range(n):
            for j, (px, py) in enumerate(peers):
                landed = outs[k].at[2 * px + py]
                _remote(landed, landed, send_sems.at[3 * k + j], recv_sems.at[3 * k + j], (px, py, c)).wait_recv()
        for rc in sends:
            rc.wait_send()

    outs = _pcall(
        body, name=name, in_specs=[ANY] * n, out_specs=[ANY] * n, out_shape=[SDS(a.shape, a.dtype) for a in arrs],
        scratch_shapes=[pltpu.SemaphoreType.DMA((3 * n,)), pltpu.SemaphoreType.DMA((3 * n,))],
    )(*arrs)
    me_xy = 2 * lax.axis_index("x") + lax.axis_index("y")
    return [_own(o, lax.dynamic_index_in_dim(a, me_xy, axis=0, keepdims=False), me_xy) for o, a in zip(outs, arrs)]


def _exchange_c(arrs, name):
    n = len(arrs)

    def body(*refs):
        ins, outs = refs[:n], refs[n:2 * n]
        send_sems, recv_sems = refs[2 * n:]
        x, y, c = _coords()
        sends = []
        for k in range(n):
            rc = _remote(ins[k], outs[k], send_sems.at[k], recv_sems.at[k], (x, y, 1 - c))
            rc.start()
            sends.append(rc)
        for rc in sends:
            rc.wait()

    return _pcall(
        body, name=name, in_specs=[ANY] * n, out_specs=[ANY] * n, out_shape=[SDS(a.shape, a.dtype) for a in arrs],
        scratch_shapes=[pltpu.SemaphoreType.DMA((n,)), pltpu.SemaphoreType.DMA((n,))],
    )(*arrs)


def _join_halves(halves, axes, name):
    n = len(halves)
    full = [tuple(2 * d if i == ax else d for i, d in enumerate(a.shape)) for a, ax in zip(halves, axes)]

    def body(*refs):
        ins, outs = refs[:n], refs[n:2 * n]
        send_sems, recv_sems = refs[2 * n:]
        x, y, c = _coords()
        sends = []
        for k in range(n):
            rc = _remote(ins[k], outs[k].at[_half_idx(full[k], axes[k], c)], send_sems.at[k], recv_sems.at[k], (x, y, 1 - c))
            rc.start()
            sends.append(rc)
        for k in range(n):
            landed = outs[k].at[_half_idx(full[k], axes[k], 1 - c)]
            _remote(landed, landed, send_sems.at[k], recv_sems.at[k], (x, y, 1 - c)).wait_recv()
        for rc in sends:
            rc.wait_send()

    outs = _pcall(
        body, name=name, in_specs=[ANY] * n, out_specs=[ANY] * n,
        out_shape=[SDS(f, a.dtype) for f, a in zip(full, halves)],
        scratch_shapes=[pltpu.SemaphoreType.DMA((n,)), pltpu.SemaphoreType.DMA((n,))],
    )(*halves)
    ci = lax.axis_index("c")
    return [lax.dynamic_update_slice_in_dim(o, a, ci * a.shape[ax], axis=ax) for o, a, ax in zip(outs, halves, axes)]


def _pair_sum(a, got, cidx, axis, name):
    _, r, cdim = a.shape
    hshape = _half_shape(a.shape, axis)

    def body(c_ref, a_ref, g_ref, o_ref):
        o_ref[...] = (a_ref[...] + g_ref[...]).astype(BF16)

    if axis == 1:
        tr = min(r // 2, 256)
        nj = (r // 2) // tr
        blk = pl.BlockSpec((1, tr, cdim), lambda s, j, c: (s, j, 0))
        a_spec = pl.BlockSpec((1, tr, cdim), lambda s, j, c: (s, c[0] * nj + j, 0))
    else:
        nj = (cdim // 2) // 128
        blk = pl.BlockSpec((1, r, 128), lambda s, j, c: (s, 0, j))
        a_spec = pl.BlockSpec((1, r, 128), lambda s, j, c: (s, 0, c[0] * nj + j))
    return _pcall(
        body, name=name, out_shape=SDS(hshape, BF16),
        grid_spec=pltpu.PrefetchScalarGridSpec(num_scalar_prefetch=1, grid=(4, nj), in_specs=[a_spec, blk], out_specs=blk),
        compiler_params=_cp(("parallel", "parallel")),
    )(cidx, a, got)


def _sum_chips(parts, name):
    _, h, cdim = parts.shape

    def body(p_ref, o_ref):
        acc = p_ref[0].astype(F32)
        for k in range(1, 4):
            acc = acc + p_ref[k].astype(F32)
        o_ref[...] = acc

    if h % 256 == 0 or h in (128,):
        tr = min(h, 256)
        grid, in_spec, out_spec = (h // tr,), pl.BlockSpec((4, tr, cdim), lambda i: (0, i, 0)), pl.BlockSpec((tr, cdim), lambda i: (i, 0))
    else:
        grid, in_spec, out_spec = (cdim // 128,), pl.BlockSpec((4, h, 128), lambda i: (0, 0, i)), pl.BlockSpec((h, 128), lambda i: (0, i))
    return _pcall(
        body, name=name, grid=grid, in_specs=[in_spec], out_specs=out_spec, out_shape=SDS((h, cdim), F32),
        compiler_params=_cp(("parallel",)),
    )(parts)


def _sum_slots(a, name, rows):
    s, n, _ = a.shape

    def body(a_ref, o_ref):
        acc = a_ref[0]
        for k in range(1, s):
            acc = acc + a_ref[k]
        o_ref[...] = acc

    return _pcall(
        body, name=name, grid=(n // rows,), in_specs=[pl.BlockSpec((s, rows, 128), lambda i: (0, i, 0))],
        out_specs=pl.BlockSpec((rows, 128), lambda i: (i, 0)), out_shape=SDS((n, 128), F32),
        compiler_params=_cp(("parallel",)),
    )(a)


def _adam_math(w, g, m, v):
    nm = ADAM_B1 * m + (1.0 - ADAM_B1) * g
    nv = ADAM_B2 * v + (1.0 - ADAM_B2) * (g * g)
    m_hat = nm / (1.0 - ADAM_B1 ** ADAM_STEP)
    v_hat = nv / (1.0 - ADAM_B2 ** ADAM_STEP)
    return -ADAM_LR * (m_hat / (jnp.sqrt(v_hat) + ADAM_EPS) + ADAM_WD * w), nm, nv


def _adamw(w, g, m, v, name, rows):
    r, cdim = w.shape

    def body(w_ref, g_ref, m_ref, v_ref, d_ref, nm_ref, nv_ref):
        d_ref[...], nm_ref[...], nv_ref[...] = _adam_math(w_ref[...], g_ref[...], m_ref[...], v_ref[...])

    blk = pl.BlockSpec((rows, cdim), lambda i: (i, 0))
    return _pcall(
        body, name=name, grid=(r // rows,), in_specs=[blk] * 4, out_specs=[blk] * 3,
        out_shape=[SDS(w.shape, F32)] * 3, compiler_params=_cp(("parallel",)),
    )(w, g, m, v)


def _adamw_joined(w, mine, other, m, v, cidx, axis, name, rows):
    r, cdim = w.shape
    if axis == 0:
        rows = r

    def body(c_ref, w_ref, a_ref, b_ref, m_ref, v_ref, g_ref, d_ref, nm_ref, nv_ref):
        a, b = a_ref[...], b_ref[...]
        g = jnp.where(c_ref[0] == 0, jnp.concatenate([a, b], axis=axis), jnp.concatenate([b, a], axis=axis))
        g_ref[...] = g
        d_ref[...], nm_ref[...], nv_ref[...] = _adam_math(w_ref[...], g, m_ref[...], v_ref[...])

    blk = pl.BlockSpec((rows, cdim), lambda i, c: (i, 0))
    hshape = (rows // 2, cdim) if axis == 0 else (rows, cdim // 2)
    hblk = pl.BlockSpec(hshape, lambda i, c: (i, 0))
    return _pcall(
        body, name=name, out_shape=[SDS(w.shape, F32)] * 4,
        grid_spec=pltpu.PrefetchScalarGridSpec(num_scalar_prefetch=1, grid=(r // rows,), in_specs=[blk, hblk, hblk, blk, blk],
                                               out_specs=[blk] * 4),
        compiler_params=_cp(("parallel",)),
    )(cidx, w, mine, other, m, v)


def _adamw_many(ws, gs, ms, vs, name):
    n = len(ws)

    def body(*refs):
        outs = refs[4 * n:]
        for k in range(n):
            d, nm, nv = _adam_math(refs[k][...], refs[n + k][...], refs[2 * n + k][...], refs[3 * n + k][...])
            outs[k][...] = d
            outs[n + k][...] = nm
            outs[2 * n + k][...] = nv

    res = _pcall(body, name=name, out_shape=[SDS(w.shape, F32) for w in ws] * 3)(*ws, *gs, *ms, *vs)
    return res[:n], res[n:2 * n], res[2 * n:]


def _pack(pieces, rows):
    flat = jnp.concatenate([p.reshape(-1) for p in pieces])
    return jnp.pad(flat, (0, rows * 128 - flat.shape[0])).reshape(rows, 128)


def _unpack(buf, shapes):
    flat = buf.reshape(-1)
    out, off = [], 0
    for shp in shapes:
        size = 1
        for s in shp:
            size *= s
        out.append(flat[off:off + size].reshape(shp))
        off += size
    return out


def _perm_cols(w):
    perm = jnp.concatenate([w[..., 3104:5152], w[..., 0:1024], w[..., 1056:1568], w[..., 1568:2080], w[..., 2592:3104],
                            w[..., 2080:2592]], axis=-1)
    return perm, w[..., 1024:1056]


def _unperm_cols(perm, lr32):
    return jnp.concatenate([perm[..., 2048:3072], lr32, perm[..., 3072:3584], perm[..., 3584:4096], perm[..., 4608:5120],
                            perm[..., 4096:4608], perm[..., 0:2048]], axis=-1)


SMALL_ROWS = 672
HALF_ROWS = 7200


def kernel(x, c, ctx, c_ctx, w_mod, b_mod, norm_g, w_in, a_ln_g, a_ln_b, a_ws, a_bs, b_gate_w2, b_gate_b, b_norm_g, w_proj_a, w_proj_b, w_out, final_norm_g, loss_target, m_c_ctx, m_w_mod, m_b_mod, m_norm_g, m_w_in, m_a_ln_g, m_a_ln_b, m_a_ws, m_a_bs, m_b_gate_w2, m_b_gate_b, m_b_norm_g, m_w_proj_a, m_w_proj_b, m_w_out, m_final_norm_g, v_c_ctx, v_w_mod, v_b_mod, v_norm_g, v_w_in, v_a_ln_g, v_a_ln_b, v_a_ws, v_a_bs, v_b_gate_w2, v_b_gate_b, v_b_norm_g, v_w_proj_a, v_w_proj_b, v_w_out, v_final_norm_g):
    xi, yi, ci = _coords()
    me_xy = 2 * xi + yi

    gate_pack = _pack([b_gate_w2[0], b_gate_b[0]], 24)
    w_in_t, m_w_in_t, v_w_in_t = (jnp.swapaxes(a[0], 0, 1) for a in (w_in, m_w_in, v_w_in))
    g_wit, g_wm, g_gate = _gather_weights([(w_in_t.astype(BF16), 1), (w_mod[0].astype(BF16), 0)], [gate_pack], "gather_weights")
    late_shards = (w_proj_a[0].astype(BF16), w_proj_b[0].astype(BF16), w_out[0].astype(BF16))
    wit_u = g_wit.reshape(4 * 1288, D)
    wit_g = wit_u[3104:5152]
    wit_r = jnp.concatenate([wit_u[1056:1568], wit_u[1568:2080], wit_u[2592:3104], wit_u[2080:2592], wit_u[0:1024]], axis=0)
    wlrt = jnp.pad(wit_u[1024:1056], ((0, LRW - 32), (0, 0)))
    wm = jnp.swapaxes(g_wm, 0, 1).reshape(D, 3 * D)
    gflat = g_gate.reshape(4, 24 * 128)
    w2 = jnp.swapaxes(gflat[:, 0:2048].reshape(4, 2, 16, 64), 0, 2)
    w2 = jnp.swapaxes(w2, 0, 1).reshape(2, 16, 256)
    gb2 = jnp.swapaxes(gflat[:, 2048:2176].reshape(4, 2, 64), 0, 1).reshape(2, 256)

    tags = ["wi", "wpa", "wpb", "wo"]
    half_axes = [2, 1, 1, 1]
    sent = []

    def exchange(g):
        dwr = g["dwit_r"]
        dwit_u = jnp.concatenate([g["dwit_qkv"], g["dwlrt"][0:32], dwr[0:512], dwr[512:1024], dwr[1536:2048], dwr[1024:1536],
                                  g["dwit_g"]], axis=0)
        big = [dwit_u.reshape(4, 1288, D), jnp.swapaxes(g["dwpa"].reshape(512, 4, 256), 0, 1),
               jnp.swapaxes(g["dwpb"].reshape(512, 4, 256), 0, 1), g["dwo"].reshape(4, 256, D)]
        other = _swap_half_c(big, half_axes, "swap_half_in")
        cidx = jnp.reshape(ci, (1,)).astype(jnp.int32)
        sent.extend(_pair_sum(a, o, cidx, ax, "sum_pair_" + t) for a, o, ax, t in zip(big, other, half_axes, tags))
        return sent

    r = _device_step(x[0], c, ctx[0], c_ctx[None], loss_target[0], wm, b_mod, norm_g, wit_g, wit_r, wlrt, a_ln_g, a_ln_b,
                     a_ws[0], a_bs[0], w2, gb2, b_norm_g, None, None, None, final_norm_g[None], exchange, late_shards)

    parts = [_own(g, lax.dynamic_index_in_dim(s_, me_xy, axis=0, keepdims=False), me_xy) for g, s_ in zip(r["got"], sent)]
    halves = [_sum_chips(p_, "sum_chips_" + t) for p_, t in zip(parts, tags)]

    small = _pack([r["dmod"], c, r["dmodc"], r["dscc"], r["dng"], r["dlng"], r["dlnb"], r["dws"], r["dbs"], r["dgbn"], r["dgf"],
                   r["dw2"], r["dgb2"], jnp.broadcast_to(r["loss"], (128,))], SMALL_ROWS)
    small_all, others = _gather_all(small, halves, "gather_small")
    small_sum = _sum_slots(small_all, "sum_small", SMALL_ROWS // 4)
    (s_dmod, _, s_dmodc, s_dscc, s_dng, s_dlng, s_dlnb, s_dws, s_dbs, s_dgbn, s_dgf, s_dw2, s_dgb2, s_loss) = _unpack(
        small_sum, [(1, 3 * D), (1, D), (1, 2 * D), (D,), (1, D), (1, 512), (1, 512), (1, 4, 128, 128), (1, 4, 128), (1, 512),
                    (D,), (2, 16, 256), (2, 256), (128,)])
    loss = s_loss[0]
    s_dmodc_p = jnp.pad(s_dmodc, ((0, 0), (0, D)))
    g_b_mod = s_dmod + s_dmodc_p
    sg = jax.nn.sigmoid(c_ctx)
    g_c_ctx = s_dscc * (sg * (1.0 + c_ctx * (1.0 - sg)))
    g_w2 = lax.dynamic_slice_in_dim(s_dw2, 64 * me_xy, 64, axis=2)[None]
    g_gb2 = lax.dynamic_slice_in_dim(s_dgb2, 64 * me_xy, 64, axis=1)[None]

    flat_all = small_all.reshape(8, SMALL_ROWS * 128)
    dmod_all = flat_all[:, 0:3 * D]
    c_all = flat_all[:, 3 * D:4 * D]
    lhs = jnp.concatenate([_silu(c_all), _silu(c_ctx)[None], jnp.zeros((7, D), F32)], axis=0)
    rhs = jnp.concatenate([dmod_all, s_dmodc_p, jnp.zeros((7, 3 * D), F32)], axis=0)
    rhs = lax.dynamic_slice_in_dim(rhs, 768 * me_xy, 768, axis=1)
    g_w_mod = _mm(lhs.T.astype(BF16), rhs.astype(BF16), tm=D, tn=768, tk=16, out_dtype=F32, name="mm_dwm")

    cidx = jnp.reshape(ci, (1,)).astype(jnp.int32)
    g_w_in_t, d_w_in_t, nm_w_in_t, nv_w_in_t = _adamw_joined(w_in_t, halves[0], others[0], m_w_in_t, v_w_in_t, cidx, 1,
                                                             "adamw_w_in", 184)
    g_w_in, d_w_in, nm_w_in, nv_w_in = (jnp.swapaxes(a, 0, 1) for a in (g_w_in_t, d_w_in_t, nm_w_in_t, nv_w_in_t))
    g_wpa, d_wpa, nm_wpa, nv_wpa = _adamw_joined(w_proj_a[0], halves[1], others[1], m_w_proj_a[0], v_w_proj_a[0], cidx, 0,
                                                 "adamw_wpa", 0)
    g_wpb, d_wpb, nm_wpb, nv_wpb = _adamw_joined(w_proj_b[0], halves[2], others[2], m_w_proj_b[0], v_w_proj_b[0], cidx, 0,
                                                 "adamw_wpb", 0)
    g_wo, d_wo, nm_wo, nv_wo = _adamw_joined(w_out[0], halves[3], others[3], m_w_out[0], v_w_out[0], cidx, 0, "adamw_wo", 0)
    d_w_mod, nm_w_mod, nv_w_mod = _adamw(w_mod[0], g_w_mod, m_w_mod[0], v_w_mod[0], "adamw_w_mod", 256)

    names = ["c_ctx", "b_mod", "norm_g", "a_ln_g", "a_ln_b", "a_ws", "a_bs", "b_gate_w2", "b_gate_b", "b_norm_g", "final_norm_g"]
    ws_ = [c_ctx, b_mod, norm_g, a_ln_g, a_ln_b, a_ws, a_bs, b_gate_w2, b_gate_b, b_norm_g, final_norm_g]
    gs_ = [g_c_ctx, g_b_mod, s_dng, s_dlng, s_dlnb, s_dws, s_dbs, g_w2, g_gb2, s_dgbn, s_dgf]
    ms_ = [m_c_ctx, m_b_mod, m_norm_g, m_a_ln_g, m_a_ln_b, m_a_ws, m_a_bs, m_b_gate_w2, m_b_gate_b, m_b_norm_g, m_final_norm_g]
    vs_ = [v_c_ctx, v_b_mod, v_norm_g, v_a_ln_g, v_a_ln_b, v_a_ws, v_a_bs, v_b_gate_w2, v_b_gate_b, v_b_norm_g, v_final_norm_g]
    shapes = [w.shape for w in ws_]
    flat2 = [(1, 1024), (1, 3072), (1, 1024), (1, 512), (1, 512), (512, 128), (4, 128), (32, 64), (2, 64), (1, 512), (1, 1024)]
    as2d = lambda arrs: [a.reshape(s) for a, s in zip(arrs, flat2)]
    d_s, nm_s, nv_s = _adamw_many(as2d(ws_), as2d(gs_), as2d(ms_), as2d(vs_), "adamw_small")
    d_small = {n: a.reshape(s) for n, a, s in zip(names, d_s, shapes)}
    nm_small = {n: a.reshape(s) for n, a, s in zip(names, nm_s, shapes)}
    nv_small = {n: a.reshape(s) for n, a, s in zip(names, nv_s, shapes)}
    g_small = {n: g.reshape(s) for n, g, s in zip(names, gs_, shapes)}

    order = ["c_ctx", "w_mod", "b_mod", "norm_g", "w_in", "a_ln_g", "a_ln_b", "a_ws", "a_bs", "b_gate_w2", "b_gate_b", "b_norm_g",
             "w_proj_a", "w_proj_b", "w_out", "final_norm_g"]
    big_g = dict(w_mod=g_w_mod[None], w_in=g_w_in[None], w_proj_a=g_wpa[None], w_proj_b=g_wpb[None], w_out=g_wo[None])
    big_d = dict(w_mod=d_w_mod[None], w_in=d_w_in[None], w_proj_a=d_wpa[None], w_proj_b=d_wpb[None], w_out=d_wo[None])
    big_m = dict(w_mod=nm_w_mod[None], w_in=nm_w_in[None], w_proj_a=nm_wpa[None], w_proj_b=nm_wpb[None], w_out=nm_wo[None])
    big_v = dict(w_mod=nv_w_mod[None], w_in=nv_w_in[None], w_proj_a=nv_wpa[None], w_proj_b=nv_wpb[None], w_out=nv_wo[None])
    grads = [big_g[n] if n in big_g else g_small[n] for n in order]
    deltas = [big_d[n] if n in big_d else d_small[n] for n in order]
    new_m = [big_m[n] if n in big_m else nm_small[n] for n in order]
    new_v = [big_v[n] if n in big_v else nv_small[n] for n in order]
    return (loss, r["dx"][None], *grads, *deltas, *new_m, *new_v)
```

```python
import functools

import jax
import jax.numpy as jnp
from jax import lax
from jax.experimental import pallas as pl
from jax.experimental.pallas import tpu as pltpu

F32 = jnp.float32
BF16 = jnp.bfloat16
SDS = jax.ShapeDtypeStruct

D = 1024
NP = 5120
LRW = 128
CH = 64
AC = 128
EPS = 1e-6
TOK = 512
GLA_TB = 1024
VMEM_BIG = 48 * 1024 * 1024

ADAM_LR, ADAM_B1, ADAM_B2, ADAM_EPS, ADAM_WD, ADAM_STEP = 0.001, 0.9, 0.999, 1e-08, 0.01, 10

_pcall = pl.pallas_call
MESH = pl.DeviceIdType.MESH


def _cp(sem=None, vmem=None):
    kw = {}
    if sem is not None:
        kw["dimension_semantics"] = sem
    if vmem is not None:
        kw["vmem_limit_bytes"] = vmem
    return pltpu.CompilerParams(**kw)


def _silu(x):
    return x * jax.nn.sigmoid(x)


def _dsilu(x):
    s = jax.nn.sigmoid(x)
    return s * (1.0 + x * (1.0 - s))


def _silu_and_grad(x):
    s = jax.nn.sigmoid(x)
    return x * s, s * (1.0 + x * (1.0 - s))


def _logsig(x):
    return jnp.minimum(x, 0.0) - jnp.log1p(jnp.exp(-jnp.abs(x)))


def _nt(a, b):
    return lax.dot_general(a, b, (((1,), (1,)), ((), ())), preferred_element_type=F32)


def _tn(a, b):
    return lax.dot_general(a, b, (((0,), (0,)), ((), ())), preferred_element_type=F32)


def _nn(a, b):
    return jnp.dot(a, b, preferred_element_type=F32)


def _full(shape):
    return pl.BlockSpec(shape, lambda *_: (0,) * len(shape))


def _mm(a, b, *, tm, tn, tk, out_dtype, name, acc=None, n_outer=False, b_t=False):
    m, k = a.shape
    n, k2 = (b.shape if b_t else b.shape[::-1])
    assert k == k2 and m % tm == 0 and n % tn == 0 and k % tk == 0, (a.shape, b.shape, tm, tn, tk)
    nk = k // tk
    has_acc = acc is not None

    def body(*refs):
        if has_acc:
            a_ref, b_ref, c_ref, o_ref = refs[:4]
        else:
            a_ref, b_ref, o_ref = refs[:3]
        part = (_nt if b_t else _nn)(a_ref[...].astype(BF16), b_ref[...].astype(BF16))
        if nk == 1:
            o_ref[...] = ((c_ref[...] + part) if has_acc else part).astype(out_dtype)
            return
        acc_ref = refs[-1]
        kk = pl.program_id(2)

        @pl.when(kk == 0)
        def _():
            if has_acc:
                acc_ref[...] = c_ref[...] + part
            else:
                acc_ref[...] = part

        @pl.when(kk > 0)
        def _():
            acc_ref[...] += part

        @pl.when(kk == nk - 1)
        def _():
            o_ref[...] = acc_ref[...].astype(out_dtype)

    if n_outer:
        ij = lambda g0, g1: (g1, g0)
        grid = (n // tn, m // tm, nk)
    else:
        ij = lambda g0, g1: (g0, g1)
        grid = (m // tm, n // tn, nk)
    b_spec = (pl.BlockSpec((tn, tk), lambda g0, g1, kk: (ij(g0, g1)[1], kk)) if b_t
              else pl.BlockSpec((tk, tn), lambda g0, g1, kk: (kk, ij(g0, g1)[1])))
    in_specs = [pl.BlockSpec((tm, tk), lambda g0, g1, kk: (ij(g0, g1)[0], kk)), b_spec]
    args = [a, b]
    if has_acc:
        in_specs.append(pl.BlockSpec((tm, tn), lambda g0, g1, kk: ij(g0, g1)))
        args.append(acc)
    return _pcall(
        body, name=name, grid=grid, in_specs=in_specs,
        out_specs=pl.BlockSpec((tm, tn), lambda g0, g1, kk: ij(g0, g1)),
        out_shape=SDS((m, n), out_dtype), scratch_shapes=([pltpu.VMEM((tm, tn), F32)] if nk > 1 else []),
        compiler_params=_cp(("parallel", "parallel", "arbitrary"), VMEM_BIG),
    )(*args)


def _mm_tn(a, b, *, ta, tn, tk, name, acc=None):
    m, ka = a.shape
    m2, n = b.shape
    assert m == m2 and ka % ta == 0 and n % tn == 0 and m % tk == 0, (a.shape, b.shape, ta, tn, tk)
    nk = m // tk
    has_acc = acc is not None

    def body(*refs):
        if has_acc:
            a_ref, b_ref, c_ref, o_ref = refs
        else:
            a_ref, b_ref, o_ref = refs
        kk = pl.program_id(2)
        part = _tn(a_ref[...].astype(BF16), b_ref[...].astype(BF16))

        @pl.when(kk == 0)
        def _():
            if has_acc:
                o_ref[...] = c_ref[...] + part
            else:
                o_ref[...] = part

        @pl.when(kk > 0)
        def _():
            o_ref[...] += part

    in_specs = [pl.BlockSpec((tk, ta), lambda i, j, kk: (kk, i)), pl.BlockSpec((tk, tn), lambda i, j, kk: (kk, j))]
    args = [a, b]
    if has_acc:
        in_specs.append(pl.BlockSpec((ta, tn), lambda i, j, kk: (i, j)))
        args.append(acc)
    return _pcall(
        body, name=name, grid=(ka // ta, n // tn, nk), in_specs=in_specs,
        out_specs=pl.BlockSpec((ta, tn), lambda i, j, kk: (i, j)), out_shape=SDS((ka, n), F32),
        compiler_params=_cp(("parallel", "parallel", "arbitrary"), VMEM_BIG),
    )(*args)


def _modvec(cc, wm, bm):
    def body(c_ref, w_ref, b_ref, o_ref):
        o_ref[...] = _nn(_silu(c_ref[...]).astype(BF16), w_ref[...]) + b_ref[...]

    return _pcall(body, name="modvec", out_shape=SDS((8, 3 * D), F32), compiler_params=_cp(None, VMEM_BIG))(cc, wm, bm)


def _dcctx(dmodc, wm):
    def body(d_ref, w_ref, o_ref):
        o_ref[...] = _nt(d_ref[...].astype(BF16), w_ref[...])

    return _pcall(
        body, name="dcctx", grid=(1,), in_specs=[_full((8, 2 * D)), pl.BlockSpec((D, 2 * D), lambda i: (0, 0))],
        out_specs=_full((8, D)), out_shape=SDS((8, D), F32), compiler_params=_cp(("arbitrary",), VMEM_BIG),
    )(dmodc, wm)


def _prep_h(x, ng, scale, shift, name):
    m = x.shape[0]

    def body(x_ref, g_ref, sc_ref, sh_ref, h_ref):
        xf = x_ref[...]
        r = lax.rsqrt(jnp.mean(xf * xf, axis=-1, keepdims=True) + EPS)
        y = (xf * r) * g_ref[...]
        h_ref[...] = (y * (1.0 + sc_ref[...]) + sh_ref[...]).astype(BF16)

    tok = min(TOK, m)
    row = pl.BlockSpec((tok, D), lambda i: (i, 0))
    return _pcall(
        body, name=name, grid=(m // tok,), in_specs=[row, _full((1, D)), _full((1, D)), _full((1, D))],
        out_specs=row, out_shape=SDS((m, D), BF16), compiler_params=_cp(("parallel",)),
    )(x, ng, scale, shift)


def _resident(shape):
    return pl.BlockSpec(shape, lambda *_: (0,) * len(shape), pipeline_mode=pl.Buffered(1))


PROJ_TM = 512


def _proj_fwd(x, ng, scale, shift, wit_g, wit_r, wlrt, ln_g, ln_b, share=()):
    m = x.shape[0]
    ns = len(share)
    steps = m // PROJ_TM
    src = [(0, 0), (0, D), (1, 2 * D), (1, 0), (1, D)]

    def body(*refs):
        x_ref, g_ref, sc_ref, sh_ref, wg_ref, wr_ref, wl_ref, lg_ref, lb_ref = refs[:9]
        share_refs = refs[9:9 + ns]
        h_ref, p_ref, plr_ref, vr_ref, vc_ref = refs[9 + ns:14 + ns]
        got_refs = refs[14 + ns:14 + 2 * ns]
        sems = refs[14 + 2 * ns:]

        def copies():
            cx, cy, cc = _coords()
            me = 2 * cx + cy
            peers = [(1 - cx, cy), (cx, 1 - cy), (1 - cx, 1 - cy)]
            out, back = [], []
            for k in range(ns):
                for j, (px, py) in enumerate(peers):
                    out.append(_remote(share_refs[k], got_refs[k].at[me], sems[0].at[3 * k + j], sems[1].at[3 * k + j], (px, py, cc)))
                    landed = got_refs[k].at[2 * px + py]
                    back.append(_remote(landed, landed, sems[0].at[3 * k + j], sems[1].at[3 * k + j], (px, py, cc)))
            return out, back

        if ns:
            @pl.when(pl.program_id(0) == 0)
            def _():
                for rc in copies()[0]:
                    rc.start()

            @pl.when(pl.program_id(0) == steps - 1)
            def _():
                out, back = copies()
                for rc in back:
                    rc.wait_recv()
                for rc in out:
                    rc.wait_send()

        xf = x_ref[...]
        r = lax.rsqrt(jnp.mean(xf * xf, axis=-1, keepdims=True) + EPS)
        y = (xf * r) * g_ref[...]
        h = (y * (1.0 + sc_ref[...]) + sh_ref[...]).astype(BF16)
        h_ref[...] = h
        for j, (which, r0) in enumerate(src):
            w_ref = wr_ref if which else wg_ref
            blk = _nt(h, w_ref[r0:r0 + D, :]).astype(BF16)
            p_ref[:, D * j:D * j + D] = blk
            if j == 4:
                xf = blk[:, 512:1024].astype(F32)
                xc = xf - jnp.mean(xf, axis=-1, keepdims=True)
                vn = (xc * lax.rsqrt(jnp.mean(xc * xc, axis=-1, keepdims=True) + EPS)) * lg_ref[...] + lb_ref[...]
                vr_ref[...] = vn[:, 0:256].astype(BF16)
                vc_ref[0] = vn[:, 256:384].astype(BF16)
                vc_ref[1] = vn[:, 384:512].astype(BF16)
        plr_ref[...] = _nt(h, wl_ref[...])

    row = pl.BlockSpec((PROJ_TM, D), lambda i: (i, 0))
    vec = _full((1, D))
    res = _pcall(
        body, name="proj_fwd", grid=(steps,),
        in_specs=[row, vec, vec, vec, _resident((2 * D, D)), _resident((3 * D, D)), _resident((LRW, D)), _full((1, 512)),
                  _full((1, 512))] + [ANY] * ns,
        out_specs=[row, pl.BlockSpec((PROJ_TM, NP), lambda i: (i, 0)), pl.BlockSpec((PROJ_TM, LRW), lambda i: (i, 0)),
                   pl.BlockSpec((PROJ_TM, 256), lambda i: (i, 0)), pl.BlockSpec((2, PROJ_TM, 128), lambda i: (0, i, 0))] + [ANY] * ns,
        out_shape=[SDS((m, D), BF16), SDS((m, NP), BF16), SDS((m, LRW), F32), SDS((m, 256), BF16), SDS((2, m, 128), BF16)]
        + [SDS((4,) + a.shape, a.dtype) for a in share],
        scratch_shapes=([pltpu.SemaphoreType.DMA((3 * ns,)), pltpu.SemaphoreType.DMA((3 * ns,))] if ns else []),
        compiler_params=_cp(("arbitrary",), VMEM_BIG),
    )(x, ng, scale, shift, wit_g, wit_r, wlrt, ln_g, ln_b, *share)
    return res[0], res[1], res[2], res[3], res[4], list(res[5:])


def _proj_bwd(dp_g, dp_r, dlr, wit_g, wit_r, wlrt, x, dx1, ng, scale, send=()):
    m = x.shape[0]
    ns = len(send)
    steps = m // PROJ_TM

    def body(*refs):
        (dpg_ref, dpr_ref, dlr_ref, wg_ref, wr_ref, wl_ref, x_ref, r_ref, g_ref, sc_ref) = refs[:10]
        send_refs = refs[10:10 + ns]
        dx_ref, dg_ref, dsc_ref, dsh_ref = refs[10 + ns:14 + ns]
        got_refs = refs[14 + ns:14 + 2 * ns]
        sems = refs[14 + 2 * ns:]
        i = pl.program_id(0)

        def copies():
            cx, cy, cc = _coords()
            me = 2 * cx + cy
            peers = [(1 - cx, cy), (cx, 1 - cy), (1 - cx, 1 - cy)]
            out, back = [], []
            for k in range(ns):
                for j, (px, py) in enumerate(peers):
                    out.append(_remote(send_refs[k].at[2 * px + py], got_refs[k].at[me], sems[0].at[3 * k + j],
                                       sems[1].at[3 * k + j], (px, py, cc)))
                    landed = got_refs[k].at[2 * px + py]
                    back.append(_remote(landed, landed, sems[0].at[3 * k + j], sems[1].at[3 * k + j], (px, py, cc)))
            return out, back

        @pl.when(i == 0)
        def _():
            dg_ref[...] = jnp.zeros_like(dg_ref)
            dsc_ref[...] = jnp.zeros_like(dsc_ref)
            dsh_ref[...] = jnp.zeros_like(dsh_ref)
            if ns:
                for rc in copies()[0]:
                    rc.start()

        dh_ = (_nn(dpg_ref[...], wg_ref[...]) + _nn(dpr_ref[...], wr_ref[...])
               + _nn(dlr_ref[...].astype(BF16), wl_ref[...]))
        xf = x_ref[...]
        r = lax.rsqrt(jnp.mean(xf * xf, axis=-1, keepdims=True) + EPS)
        xh = xf * r
        y = xh * g_ref[...]
        dsh_ref[...] += jnp.sum(dh_, axis=0, keepdims=True)
        dsc_ref[...] += jnp.sum(dh_ * y, axis=0, keepdims=True)
        dy = dh_ * (1.0 + sc_ref[...])
        dg_ref[...] += jnp.sum(dy * xh, axis=0, keepdims=True)
        dxh = dy * g_ref[...]
        dx_ref[...] = r * (dxh - xh * jnp.mean(dxh * xh, axis=-1, keepdims=True)) + r_ref[...]

        if ns:
            @pl.when(i == steps - 1)
            def _():
                out, back = copies()
                for rc in back:
                    rc.wait_recv()
                for rc in out:
                    rc.wait_send()

    row = pl.BlockSpec((PROJ_TM, D), lambda i: (i, 0))
    vec = _full((1, D))
    kg, kr = dp_g.shape[1], dp_r.shape[1]
    res = _pcall(
        body, name="proj_bwd", grid=(steps,),
        in_specs=[pl.BlockSpec((PROJ_TM, kg), lambda i: (i, 0)), pl.BlockSpec((PROJ_TM, kr), lambda i: (i, 0)),
                  pl.BlockSpec((PROJ_TM, LRW), lambda i: (i, 0)), _resident((kg, D)), _resident((kr, D)), _resident((LRW, D)),
                  row, row, vec, vec] + [ANY] * ns,
        out_specs=[row, vec, vec, vec] + [ANY] * ns,
        out_shape=[SDS((m, D), F32), SDS((1, D), F32), SDS((1, D), F32), SDS((1, D), F32)] + [SDS(a.shape, a.dtype) for a in send],
        scratch_shapes=([pltpu.SemaphoreType.DMA((3 * ns,)), pltpu.SemaphoreType.DMA((3 * ns,))] if ns else []),
        compiler_params=_cp(("arbitrary",), VMEM_BIG),
    )(dp_g, dp_r, dlr, wit_g, wit_r, wlrt, x, dx1, ng, scale, *send)
    return tuple(res[:4]), list(res[4:])


def _prep_bwd(x, dh, dx1, ng, scale, name):
    m = x.shape[0]
    has_res = dx1 is not None

    def body(*refs):
        if has_res:
            x_ref, dh_ref, r_ref, g_ref, sc_ref, dx_ref, dg_ref, dsc_ref, dsh_ref = refs
        else:
            x_ref, dh_ref, g_ref, sc_ref, dx_ref, dg_ref, dsc_ref, dsh_ref = refs
        i = pl.program_id(0)

        @pl.when(i == 0)
        def _():
            dg_ref[...] = jnp.zeros_like(dg_ref)
            dsc_ref[...] = jnp.zeros_like(dsc_ref)
            dsh_ref[...] = jnp.zeros_like(dsh_ref)

        xf = x_ref[...]
        dh_ = dh_ref[...]
        r = lax.rsqrt(jnp.mean(xf * xf, axis=-1, keepdims=True) + EPS)
        xh = xf * r
        y = xh * g_ref[...]
        dsh_ref[...] += jnp.sum(dh_, axis=0, keepdims=True)
        dsc_ref[...] += jnp.sum(dh_ * y, axis=0, keepdims=True)
        dy = dh_ * (1.0 + sc_ref[...])
        dg_ref[...] += jnp.sum(dy * xh, axis=0, keepdims=True)
        dxh = dy * g_ref[...]
        dx = r * (dxh - xh * jnp.mean(dxh * xh, axis=-1, keepdims=True))
        if has_res:
            dx = dx + r_ref[...]
        dx_ref[...] = dx

    tok = min(TOK, m)
    row = pl.BlockSpec((tok, D), lambda i: (i, 0))
    vec = _full((1, D))
    in_specs = [row, row] + ([row] if has_res else []) + [vec, vec]
    args = [x, dh] + ([dx1] if has_res else []) + [ng, scale]
    return _pcall(
        body, name=name, grid=(m // tok,), in_specs=in_specs, out_specs=[row, vec, vec, vec],
        out_shape=[SDS((m, D), F32), SDS((1, D), F32), SDS((1, D), F32), SDS((1, D), F32)],
        compiler_params=_cp(("arbitrary",)),
    )(*args)


def _ln_fwd(p, ln_g, ln_b):
    m = p.shape[0]

    def body(va_ref, g_ref, b_ref, vr_ref, vc_ref):
        xf = va_ref[...].astype(F32)
        xc = xf - jnp.mean(xf, axis=-1, keepdims=True)
        y = xc * lax.rsqrt(jnp.mean(xc * xc, axis=-1, keepdims=True) + EPS)
        vn = y * g_ref[...] + b_ref[...]
        vr_ref[...] = vn[:, 0:256].astype(BF16)
        vc_ref[0] = vn[:, 256:384].astype(BF16)
        vc_ref[1] = vn[:, 384:512].astype(BF16)

    return _pcall(
        body, name="ln_fwd", grid=(m // TOK,),
        in_specs=[pl.BlockSpec((TOK, 512), lambda i: (i, 9)), _full((1, 512)), _full((1, 512))],
        out_specs=[pl.BlockSpec((TOK, 256), lambda i: (i, 0)), pl.BlockSpec((2, TOK, 128), lambda i: (0, i, 0))],
        out_shape=[SDS((m, 256), BF16), SDS((2, m, 128), BF16)], compiler_params=_cp(("parallel",)),
    )(p, ln_g, ln_b)


COLB = 2048


def _colmix_fwd(vnc, ws23, bs23):
    rows = vnc.shape[2] // COLB

    def body(v_ref, w_ref, b_ref, o_ref):
        o_ref[0] = _nn(w_ref[0], v_ref[0]) + b_ref[0]

    return _pcall(
        body, name="colmix_fwd", grid=(2, rows),
        in_specs=[pl.BlockSpec((1, AC, COLB), lambda g, j: (g, 0, j)), pl.BlockSpec((1, AC, AC), lambda g, j: (g, 0, 0)),
                  pl.BlockSpec((1, AC, 1), lambda g, j: (g, 0, 0))],
        out_specs=pl.BlockSpec((1, AC, COLB), lambda g, j: (g, 0, j)),
        out_shape=SDS(vnc.shape, F32), compiler_params=_cp(("parallel", "parallel")),
    )(vnc, ws23, bs23)


def _colmix_bwd(dsvc, vnc, ws23t):
    rows = vnc.shape[2] // COLB

    def body(d_ref, v_ref, wt_ref, dv_ref, dw_ref, db_ref):
        j = pl.program_id(1)

        @pl.when(j == 0)
        def _():
            dw_ref[...] = jnp.zeros_like(dw_ref)
            db_ref[...] = jnp.zeros_like(db_ref)

        d = d_ref[0]
        d16 = d.astype(BF16)
        dv_ref[0] = _nn(wt_ref[0], d16)
        dw_ref[0] += _nt(d16, v_ref[0])
        db_ref[0] += jnp.sum(d, axis=1, keepdims=True)

    blk = pl.BlockSpec((1, AC, COLB), lambda g, j: (g, 0, j))
    return _pcall(
        body, name="colmix_bwd", grid=(2, rows),
        in_specs=[blk, blk, pl.BlockSpec((1, AC, AC), lambda g, j: (g, 0, 0))],
        out_specs=[blk, pl.BlockSpec((1, AC, AC), lambda g, j: (g, 0, 0)), pl.BlockSpec((1, AC, 1), lambda g, j: (g, 0, 0))],
        out_shape=[SDS(vnc.shape, F32), SDS((2, AC, AC), F32), SDS((2, AC, 1), F32)],
        compiler_params=_cp(("parallel", "arbitrary")),
    )(dsvc, vnc, ws23t)


def _head_norm(o, gbn):
    out = []
    for h in range(4):
        oh = o[:, 128 * h:128 * h + 128]
        r = lax.rsqrt(jnp.mean(oh * oh, axis=-1, keepdims=True) + EPS)
        out.append((r, oh * r))
    return out


def _mid_fwd(o_f, o_b, p, vnr, svc, ws01, bs01, gbn):
    m = p.shape[0]

    def body(of_ref, ob_ref, zb_ref, ua_ref, za_ref, vnr_ref, svc_ref, w_ref, b_ref, g_ref, ya_ref, yb_ref, svr_ref):
        o = of_ref[...] + ob_ref[...]
        zb = zb_ref[...]
        parts = []
        for h, (r, xh) in enumerate(_head_norm(o, None)):
            parts.append(xh * g_ref[:, 128 * h:128 * h + 128])
        on = jnp.concatenate(parts, axis=1)
        yb_ref[...] = (on * _silu(zb)).astype(BF16)
        for j in range(TOK // AC):
            for g in range(2):
                sv = _nn(w_ref[g], vnr_ref[AC * j:AC * j + AC, AC * g:AC * g + AC]) + b_ref[g]
                svr_ref[AC * j:AC * j + AC, AC * g:AC * g + AC] = sv
        sz = _silu(za_ref[...])
        u = ua_ref[...]
        ya_ref[:, 0:256] = ((u[:, 0:256] * svr_ref[...]) * sz[:, 0:256]).astype(BF16)
        ya_ref[:, 256:384] = ((u[:, 256:384] * svc_ref[0]) * sz[:, 256:384]).astype(BF16)
        ya_ref[:, 384:512] = ((u[:, 384:512] * svc_ref[1]) * sz[:, 384:512]).astype(BF16)

    r512 = pl.BlockSpec((TOK, 512), lambda i: (i, 0))
    return _pcall(
        body, name="mid_fwd", grid=(m // TOK,),
        in_specs=[r512, r512, pl.BlockSpec((TOK, 512), lambda i: (i, 6)), pl.BlockSpec((TOK, 512), lambda i: (i, 7)),
                  pl.BlockSpec((TOK, 512), lambda i: (i, 8)), pl.BlockSpec((TOK, 256), lambda i: (i, 0)),
                  pl.BlockSpec((2, TOK, 128), lambda i: (0, i, 0)), _full((2, AC, AC)), _full((2, AC, 1)), _full((1, 512))],
        out_specs=[r512, r512, pl.BlockSpec((TOK, 256), lambda i: (i, 0))],
        out_shape=[SDS((m, 512), BF16), SDS((m, 512), BF16), SDS((m, 256), F32)],
        compiler_params=_cp(("parallel",)),
    )(o_f, o_b, p, p, p, vnr, svc, ws01, bs01, gbn)


def _merge_fwd(p, ya, yb):
    m = p.shape[0]

    def body(ga_ref, gb_ref, ya_ref, yb_ref, m_ref):
        m_ref[...] = (jax.nn.sigmoid(ga_ref[...]) * ya_ref[...] + jax.nn.sigmoid(gb_ref[...]) * yb_ref[...]).astype(BF16)

    row = pl.BlockSpec((TOK, D), lambda i: (i, 0))
    return _pcall(
        body, name="merge_fwd", grid=(m // TOK,),
        in_specs=[row, pl.BlockSpec((TOK, D), lambda i: (i, 1)), row, row], out_specs=row,
        out_shape=SDS((m, D), BF16), compiler_params=_cp(("parallel",)),
    )(p, p, ya, yb)


def _loss_head(x, out, tgt, gate, gf):
    m = x.shape[0]

    def body(x_ref, o_ref, t_ref, gate_ref, gf_ref, dx1_ref, dout_ref, loss_ref, dgate_ref, dgf_ref):
        i = pl.program_id(0)

        @pl.when(i == 0)
        def _():
            loss_ref[...] = jnp.zeros_like(loss_ref)
            dgate_ref[...] = jnp.zeros_like(dgate_ref)
            dgf_ref[...] = jnp.zeros_like(dgf_ref)

        out_ = o_ref[...]
        x1 = x_ref[...] + gate_ref[...] * out_
        r = lax.rsqrt(jnp.mean(x1 * x1, axis=-1, keepdims=True) + EPS)
        xh = x1 * r
        err = xh * gf_ref[...] - t_ref[...]
        loss_ref[...] += 0.5 * jnp.sum(jnp.mean(err * err, axis=-1, keepdims=True), axis=0, keepdims=True)
        dy = err * (1.0 / D)
        dgf_ref[...] += jnp.sum(dy * xh, axis=0, keepdims=True)
        dxh = dy * gf_ref[...]
        dx1 = r * (dxh - xh * jnp.mean(dxh * xh, axis=-1, keepdims=True))
        dx1_ref[...] = dx1
        dout_ref[...] = (gate_ref[...] * dx1).astype(BF16)
        dgate_ref[...] += jnp.sum(dx1 * out_, axis=0, keepdims=True)

    row = pl.BlockSpec((TOK, D), lambda i: (i, 0))
    vec = _full((1, D))
    return _pcall(
        body, name="loss_head", grid=(m // TOK,), in_specs=[row, row, row, vec, vec],
        out_specs=[row, row, _full((1, 128)), vec, vec],
        out_shape=[SDS((m, D), F32), SDS((m, D), BF16), SDS((1, 128), F32), SDS((1, D), F32), SDS((1, D), F32)],
        compiler_params=_cp(("arbitrary",)),
    )(x, out, tgt, gate, gf)


def _merge_bwd(dm, ya, yb, p):
    m = p.shape[0]

    def body(dm_ref, ya_ref, yb_ref, ga_ref, gb_ref, dya_ref, dyb_ref, dp_ref):
        dm_ = dm_ref[...]
        sa = jax.nn.sigmoid(ga_ref[...])
        sb = jax.nn.sigmoid(gb_ref[...])
        dya_ref[...] = (dm_ * sa).astype(BF16)
        dyb_ref[...] = (dm_ * sb).astype(BF16)
        dp_ref[:, 0:D] = (dm_ * ya_ref[...] * (sa * (1.0 - sa))).astype(BF16)
        dp_ref[:, D:2 * D] = (dm_ * yb_ref[...] * (sb * (1.0 - sb))).astype(BF16)

    row = pl.BlockSpec((TOK, D), lambda i: (i, 0))
    return _pcall(
        body, name="merge_bwd", grid=(m // TOK,),
        in_specs=[row, row, row, row, pl.BlockSpec((TOK, D), lambda i: (i, 1))],
        out_specs=[row, row, pl.BlockSpec((TOK, 2 * D), lambda i: (i, 0))],
        out_shape=[SDS((m, D), BF16), SDS((m, D), BF16), SDS((m, NP), BF16)],
        compiler_params=_cp(("parallel",)),
    )(dm, ya, yb, p, p)


def _mid_bwd(dya_in, dyb_in, p, svr, svc, o_f, o_b, gbn, dp):
    m = p.shape[0]

    def body(dya_ref, dyb_ref, zb_ref, ua_ref, za_ref, svr_ref, svc_ref, of_ref, ob_ref, g_ref, dpi_ref,
             dp_ref, dsr_ref, dsc_ref, do_ref, dg_ref):
        i = pl.program_id(0)

        @pl.when(i == 0)
        def _():
            dg_ref[...] = jnp.zeros_like(dg_ref)

        dya = dya_ref[...]
        u = ua_ref[...]
        za = za_ref[...]
        sz = _silu(za)
        sv = jnp.concatenate([svr_ref[...], svc_ref[0], svc_ref[1]], axis=1)
        dp_ref[:, 512:1024] = (dya * sv * sz).astype(BF16)
        dsv = dya * u * sz
        dsr_ref[...] = dsv[:, 0:256]
        dsc_ref[0] = dsv[:, 256:384]
        dsc_ref[1] = dsv[:, 384:512]
        dp_ref[:, 1024:1536] = (dya * u * sv * _dsilu(za)).astype(BF16)

        dyb = dyb_ref[...]
        zb = zb_ref[...]
        o = of_ref[...] + ob_ref[...]
        szb = _silu(zb)
        dszb = _dsilu(zb)
        for h, (r, xh) in enumerate(_head_norm(o, None)):
            sl = slice(128 * h, 128 * h + 128)
            gh = g_ref[:, sl]
            don = dyb[:, sl] * szb[:, sl]
            dp_ref[:, sl] = (dyb[:, sl] * (xh * gh) * dszb[:, sl]).astype(BF16)
            dg_ref[:, sl] += jnp.sum(don * xh, axis=0, keepdims=True)
            dxh = don * gh
            do_ref[:, sl] = r * (dxh - xh * jnp.mean(dxh * xh, axis=-1, keepdims=True))

    r512 = pl.BlockSpec((TOK, 512), lambda i: (i, 0))
    return _pcall(
        body, name="mid_bwd", grid=(m // TOK,),
        in_specs=[r512, r512, pl.BlockSpec((TOK, 512), lambda i: (i, 6)), pl.BlockSpec((TOK, 512), lambda i: (i, 7)),
                  pl.BlockSpec((TOK, 512), lambda i: (i, 8)), pl.BlockSpec((TOK, 256), lambda i: (i, 0)),
                  pl.BlockSpec((2, TOK, 128), lambda i: (0, i, 0)), r512, r512, _full((1, 512)),
                  pl.BlockSpec(memory_space=pl.ANY)],
        out_specs=[pl.BlockSpec((TOK, 1536), lambda i: (i, 2)), pl.BlockSpec((TOK, 256), lambda i: (i, 0)),
                   pl.BlockSpec((2, TOK, 128), lambda i: (0, i, 0)), r512, _full((1, 512))],
        out_shape=[SDS((m, NP), BF16), SDS((m, 256), F32), SDS((2, m, 128), F32), SDS((m, 512), F32), SDS((1, 512), F32)],
        input_output_aliases={10: 0}, compiler_params=_cp(("arbitrary",)),
    )(dya_in, dyb_in, p, p, p, svr, svc, o_f, o_b, gbn, dp)


def _tail_fwd(o_f, o_b, p, vnr, svc, x, tgt, ws01, bs01, gbn, wpa, wpb, wo, gate, gf):
    m = p.shape[0]

    def body(of_ref, ob_ref, zb_ref, ua_ref, za_ref, ga_ref, gb_ref, vnr_ref, svc_ref, x_ref, t_ref, w_ref, b_ref, g_ref,
             wpa_ref, wpb_ref, wo_ref, gate_ref, gf_ref,
             ya_ref, yb_ref, svr_ref, dwo_ref, dx1_ref, dout_ref, loss_ref, dgate_ref, dgf_ref):
        i = pl.program_id(0)

        @pl.when(i == 0)
        def _():
            loss_ref[...] = jnp.zeros_like(loss_ref)
            dgate_ref[...] = jnp.zeros_like(dgate_ref)
            dgf_ref[...] = jnp.zeros_like(dgf_ref)
            dwo_ref[...] = jnp.zeros_like(dwo_ref)

        o = of_ref[...] + ob_ref[...]
        zb = zb_ref[...].astype(F32)
        for h, (r, xh) in enumerate(_head_norm(o, None)):
            sl = slice(128 * h, 128 * h + 128)
            yb_ref[:, sl] = ((xh * g_ref[:, sl]) * _silu(zb[:, sl])).astype(BF16)
        for j in range(TOK // AC):
            for g in range(2):
                sv = _nn(w_ref[g], vnr_ref[AC * j:AC * j + AC, AC * g:AC * g + AC]) + b_ref[g]
                svr_ref[AC * j:AC * j + AC, AC * g:AC * g + AC] = sv
        sz = _silu(za_ref[...].astype(F32))
        u = ua_ref[...].astype(F32)
        ya_ref[:, 0:256] = ((u[:, 0:256] * svr_ref[...]) * sz[:, 0:256]).astype(BF16)
        ya_ref[:, 256:384] = ((u[:, 256:384] * svc_ref[0]) * sz[:, 256:384]).astype(BF16)
        ya_ref[:, 384:512] = ((u[:, 384:512] * svc_ref[1]) * sz[:, 384:512]).astype(BF16)
        ya = _nn(ya_ref[...], wpa_ref[...])
        yb = _nn(yb_ref[...], wpb_ref[...])
        mg = (jax.nn.sigmoid(ga_ref[...].astype(F32)) * ya + jax.nn.sigmoid(gb_ref[...].astype(F32)) * yb).astype(BF16)
        out_ = _nn(mg, wo_ref[...])
        x1 = x_ref[...] + gate_ref[...] * out_
        r = lax.rsqrt(jnp.mean(x1 * x1, axis=-1, keepdims=True) + EPS)
        xh = x1 * r
        err = xh * gf_ref[...] - t_ref[...]
        loss_ref[...] += 0.5 * jnp.sum(jnp.mean(err * err, axis=-1, keepdims=True), axis=0, keepdims=True)
        dy = err * (1.0 / D)
        dgf_ref[...] += jnp.sum(dy * xh, axis=0, keepdims=True)
        dxh = dy * gf_ref[...]
        dx1 = r * (dxh - xh * jnp.mean(dxh * xh, axis=-1, keepdims=True))
        dx1_ref[...] = dx1
        dout16 = (gate_ref[...] * dx1).astype(BF16)
        dout_ref[...] = dout16
        dgate_ref[...] += jnp.sum(dx1 * out_, axis=0, keepdims=True)
        dwo_ref[...] += _tn(mg, dout16)

    r512 = pl.BlockSpec((TOK, 512), lambda i: (i, 0))
    row = pl.BlockSpec((TOK, D), lambda i: (i, 0))
    vec = _full((1, D))
    return _pcall(
        body, name="tail_fwd", grid=(m // TOK,),
        in_specs=[r512, r512, pl.BlockSpec((TOK, 512), lambda i: (i, 6)), pl.BlockSpec((TOK, 512), lambda i: (i, 7)),
                  pl.BlockSpec((TOK, 512), lambda i: (i, 8)), row, pl.BlockSpec((TOK, D), lambda i: (i, 1)),
                  pl.BlockSpec((TOK, 256), lambda i: (i, 0)), pl.BlockSpec((2, TOK, 128), lambda i: (0, i, 0)), row, row,
                  _full((2, AC, AC)), _full((2, AC, 1)), _full((1, 512)), _resident((512, D)), _resident((512, D)),
                  _resident((D, D)), vec, vec],
        out_specs=[r512, r512, pl.BlockSpec((TOK, 256), lambda i: (i, 0)), _resident((D, D)), row, row, _full((1, 128)), vec, vec],
        out_shape=[SDS((m, 512), BF16), SDS((m, 512), BF16), SDS((m, 256), F32), SDS((D, D), F32), SDS((m, D), F32),
                   SDS((m, D), BF16), SDS((1, 128), F32), SDS((1, D), F32), SDS((1, D), F32)],
        compiler_params=_cp(("arbitrary",), VMEM_BIG),
    )(o_f, o_b, p, p, p, p, p, vnr, svc, x, tgt, ws01, bs01, gbn, wpa, wpb, wo, gate, gf)


DPR = 3072


def _tail_bwd(dout, ya_in, yb_in, p, svr, svc, o_f, o_b, gbn, wo, wpa, wpb):
    m = p.shape[0]

    def body(dout_ref, ya_ref, yb_ref, ga_ref, gb_ref, zb_ref, ua_ref, za_ref, svr_ref, svc_ref, of_ref, ob_ref, g_ref,
             wo_ref, wpa_ref, wpb_ref,
             dwpa_ref, dwpb_ref, dpg_ref, dpr_ref, dsr_ref, dsc_ref, do_ref, dg_ref):
        i = pl.program_id(0)

        @pl.when(i == 0)
        def _():
            dg_ref[...] = jnp.zeros_like(dg_ref)
            dwpa_ref[...] = jnp.zeros_like(dwpa_ref)
            dwpb_ref[...] = jnp.zeros_like(dwpb_ref)

        dm_ = _nt(dout_ref[...], wo_ref[...])
        ya_in, yb_in = ya_ref[...], yb_ref[...]
        ya = _nn(ya_in, wpa_ref[...])
        yb = _nn(yb_in, wpb_ref[...])
        sa = jax.nn.sigmoid(ga_ref[...].astype(F32))
        sb = jax.nn.sigmoid(gb_ref[...].astype(F32))
        dya16 = (dm_ * sa).astype(BF16)
        dyb16 = (dm_ * sb).astype(BF16)
        dwpa_ref[...] += _tn(ya_in, dya16)
        dwpb_ref[...] += _tn(yb_in, dyb16)
        dpg_ref[:, 0:D] = (dm_ * ya * (sa * (1.0 - sa))).astype(BF16)
        dpg_ref[:, D:2 * D] = (dm_ * yb * (sb * (1.0 - sb))).astype(BF16)
        dya = _nt(dya16, wpa_ref[...])
        dyb = _nt(dyb16, wpb_ref[...])

        u = ua_ref[...].astype(F32)
        za = za_ref[...].astype(F32)
        sz, dsz = _silu_and_grad(za)
        sv = jnp.concatenate([svr_ref[...], svc_ref[0], svc_ref[1]], axis=1)
        dpr_ref[:, 512:1024] = (dya * sv * sz).astype(BF16)
        dsv = dya * u * sz
        dsr_ref[...] = dsv[:, 0:256]
        dsc_ref[0] = dsv[:, 256:384]
        dsc_ref[1] = dsv[:, 384:512]
        dpr_ref[:, 1024:1536] = (dya * u * sv * dsz).astype(BF16)

        zb = zb_ref[...].astype(F32)
        o = of_ref[...] + ob_ref[...]
        szb, dszb = _silu_and_grad(zb)
        for h, (r, xh) in enumerate(_head_norm(o, None)):
            sl = slice(128 * h, 128 * h + 128)
            gh = g_ref[:, sl]
            don = dyb[:, sl] * szb[:, sl]
            dpr_ref[:, sl] = (dyb[:, sl] * (xh * gh) * dszb[:, sl]).astype(BF16)
            dg_ref[:, sl] += jnp.sum(don * xh, axis=0, keepdims=True)
            dxh = don * gh
            do_ref[:, sl] = r * (dxh - xh * jnp.mean(dxh * xh, axis=-1, keepdims=True))

    r512 = pl.BlockSpec((TOK, 512), lambda i: (i, 0))
    row = pl.BlockSpec((TOK, D), lambda i: (i, 0))
    return _pcall(
        body, name="tail_bwd", grid=(m // TOK,),
        in_specs=[row, r512, r512, row, pl.BlockSpec((TOK, D), lambda i: (i, 1)), pl.BlockSpec((TOK, 512), lambda i: (i, 6)),
                  pl.BlockSpec((TOK, 512), lambda i: (i, 7)), pl.BlockSpec((TOK, 512), lambda i: (i, 8)),
                  pl.BlockSpec((TOK, 256), lambda i: (i, 0)), pl.BlockSpec((2, TOK, 128), lambda i: (0, i, 0)), r512, r512,
                  _full((1, 512)), _resident((D, D)), _resident((512, D)), _resident((512, D))],
        out_specs=[_resident((512, D)), _resident((512, D)), pl.BlockSpec((TOK, 2 * D), lambda i: (i, 0)),
                   pl.BlockSpec((TOK, 1536), lambda i: (i, 0)),
                   pl.BlockSpec((TOK, 256), lambda i: (i, 0)), pl.BlockSpec((2, TOK, 128), lambda i: (0, i, 0)), r512, _full((1, 512))],
        out_shape=[SDS((512, D), F32), SDS((512, D), F32), SDS((m, 2 * D), BF16), SDS((m, DPR), BF16), SDS((m, 256), F32),
                   SDS((2, m, 128), F32), SDS((m, 512), F32), SDS((1, 512), F32)],
        compiler_params=_cp(("arbitrary",), VMEM_BIG),
    )(dout, ya_in, yb_in, p, p, p, p, p, svr, svc, o_f, o_b, gbn, wo, wpa, wpb)


def _mm_multi(pairs, *, tm, tn, out_dtype, name):
    m = pairs[0][0].shape[0]
    n = pairs[0][1].shape[1]
    nks = [a.shape[1] // tk for a, _, tk in pairs]
    starts = [sum(nks[:i]) for i in range(len(pairs))]
    total = sum(nks)

    def body(*refs):
        o_ref, acc_ref = refs[-2], refs[-1]
        kk = pl.program_id(2)
        for idx in range(len(pairs)):
            a_ref, b_ref = refs[2 * idx], refs[2 * idx + 1]

            @pl.when((kk >= starts[idx]) & (kk < starts[idx] + nks[idx]))
            def _(a_ref=a_ref, b_ref=b_ref, first=(idx == 0)):
                part = _nn(a_ref[...].astype(BF16), b_ref[...].astype(BF16))
                if first:
                    @pl.when(kk == 0)
                    def _():
                        acc_ref[...] = part

                    @pl.when(kk > 0)
                    def _():
                        acc_ref[...] += part
                else:
                    acc_ref[...] += part

        @pl.when(kk == total - 1)
        def _():
            o_ref[...] = acc_ref[...].astype(out_dtype)

    in_specs, args = [], []
    for (a, b, tk), st, nk in zip(pairs, starts, nks):
        in_specs.append(pl.BlockSpec((tm, tk), lambda i, j, kk, st=st, nk=nk: (i, jnp.clip(kk - st, 0, nk - 1))))
        in_specs.append(pl.BlockSpec((tk, tn), lambda i, j, kk, st=st, nk=nk: (jnp.clip(kk - st, 0, nk - 1), j)))
        args += [a, b]
    return _pcall(
        body, name=name, grid=(m // tm, n // tn, total), in_specs=in_specs,
        out_specs=pl.BlockSpec((tm, tn), lambda i, j, kk: (i, j)), out_shape=SDS((m, n), out_dtype),
        scratch_shapes=[pltpu.VMEM((tm, tn), F32)], compiler_params=_cp(("parallel", "parallel", "arbitrary"), VMEM_BIG),
    )(*args)


def _ln_bwd(dsr, vnr, dvnc, p, ws01t, ln_g, dp):
    m = p.shape[0]

    def body(dsr_ref, vnr_ref, dvc_ref, va_ref, wt_ref, g_ref, dpi_ref, dp_ref, dw_ref, db_ref, dlg_ref, dlb_ref, dvn_ref):
        i = pl.program_id(0)

        @pl.when(i == 0)
        def _():
            dw_ref[...] = jnp.zeros_like(dw_ref)
            db_ref[...] = jnp.zeros_like(db_ref)
            dlg_ref[...] = jnp.zeros_like(dlg_ref)
            dlb_ref[...] = jnp.zeros_like(dlb_ref)

        for j in range(TOK // AC):
            for g in range(2):
                d = dsr_ref[AC * j:AC * j + AC, AC * g:AC * g + AC]
                d16 = d.astype(BF16)
                dvn_ref[AC * j:AC * j + AC, AC * g:AC * g + AC] = _nn(wt_ref[g], d16)
                dw_ref[g] += _nt(d16, vnr_ref[AC * j:AC * j + AC, AC * g:AC * g + AC])
                db_ref[g] += jnp.sum(d, axis=1, keepdims=True)
        dvn_ref[:, 256:384] = dvc_ref[0]
        dvn_ref[:, 384:512] = dvc_ref[1]
        dvn = dvn_ref[...]
        xf = va_ref[...].astype(F32)
        xc = xf - jnp.mean(xf, axis=-1, keepdims=True)
        rs = lax.rsqrt(jnp.mean(xc * xc, axis=-1, keepdims=True) + EPS)
        xh = xc * rs
        dlg_ref[...] += jnp.sum(dvn * xh, axis=0, keepdims=True)
        dlb_ref[...] += jnp.sum(dvn, axis=0, keepdims=True)
        dxh = dvn * g_ref[...]
        dva = rs * (dxh - jnp.mean(dxh, axis=-1, keepdims=True) - xh * jnp.mean(dxh * xh, axis=-1, keepdims=True))
        dp_ref[...] = dva.astype(BF16)

    return _pcall(
        body, name="ln_bwd", grid=(m // TOK,),
        in_specs=[pl.BlockSpec((TOK, 256), lambda i: (i, 0)), pl.BlockSpec((TOK, 256), lambda i: (i, 0)),
                  pl.BlockSpec((2, TOK, 128), lambda i: (0, i, 0)), pl.BlockSpec((TOK, 512), lambda i: (i, 9)),
                  _full((2, AC, AC)), _full((1, 512)), pl.BlockSpec(memory_space=pl.ANY)],
        out_specs=[pl.BlockSpec((TOK, 512), lambda i: (i, 3)), _full((2, AC, AC)), _full((2, AC, 1)), _full((1, 512)), _full((1, 512))],
        out_shape=[SDS((m, DPR), BF16), SDS((2, AC, AC), F32), SDS((2, AC, 1), F32), SDS((1, 512), F32), SDS((1, 512), F32)],
        scratch_shapes=[pltpu.VMEM((TOK, 512), F32)],
        input_output_aliases={6: 0}, compiler_params=_cp(("arbitrary",)),
    )(dsr, vnr, dvnc, p, ws01t, ln_g, dp)


def _tri_mm(tri, a):
    a1 = a.astype(BF16)
    r1 = a - a1.astype(F32)
    a2 = r1.astype(BF16)
    a3 = (r1 - a2.astype(F32)).astype(BF16)
    n = a.shape[1]
    r = _nn(tri, jnp.concatenate([a1, a2, a3], axis=1))
    return r[:, 0:n] + r[:, n:2 * n] + r[:, 2 * n:3 * n]


def _gla_masks(reverse):
    ri = lax.broadcasted_iota(jnp.int32, (CH, CH), 0)
    ci = lax.broadcasted_iota(jnp.int32, (CH, CH), 1)
    vis = (ci >= ri) if reverse else (ci <= ri)
    vis_t = (ci <= ri) if reverse else (ci >= ri)
    r4 = lax.broadcasted_iota(jnp.int32, (4 * CH, CH), 0) & (CH - 1)
    c4 = lax.broadcasted_iota(jnp.int32, (4 * CH, CH), 1)
    vis4 = (c4 >= r4) if reverse else (c4 <= r4)
    vis4_t = (c4 <= r4) if reverse else (c4 >= r4)
    lane = lax.broadcasted_iota(jnp.int32, (1, 256), 1)
    hm = [(lane >= CH * h) & (lane < CH * h + CH) for h in range(4)]
    return vis, vis_t, vis4, vis4_t, hm


def _stack_heads(x, hm):
    return jnp.concatenate([jnp.where(hm[h], x, 0.0).astype(BF16) for h in range(4)], axis=0)


def _diag_heads(full, hm):
    r = full.shape[0] // 4
    acc = jnp.where(hm[0], full[0:r], 0.0)
    for h in range(1, 4):
        acc = acc + jnp.where(hm[h], full[r * h:r * h + r], 0.0)
    return acc


def _rows_of_heads(x):
    return jnp.concatenate([x[:, 128 * h:128 * h + 128] for h in range(4)], axis=0)


def _lane_vis(reverse, transpose):
    ri = lax.broadcasted_iota(jnp.int32, (CH, 4 * CH), 0)
    ci = lax.broadcasted_iota(jnp.int32, (CH, 4 * CH), 1) & (CH - 1)
    return (ci >= ri) if (reverse != transpose) else (ci <= ri)


def _gla_fwd(p, qkv_blk, lr, lrw, gbias, s0, *, reverse, name):
    m = p.shape[0]
    tb = min(GLA_TB, m)
    nb = m // tb
    nc = tb // CH
    rmap = (lambda i: nb - 1 - i) if reverse else (lambda i: i)

    def body(qkv_ref, lr_ref, lrw_ref, gb_ref, s0_ref, o_ref, sb_ref, sfin_ref, st_ref):
        i = pl.program_id(0)

        @pl.when(i == 0)
        def _():
            st_ref[...] = s0_ref[...]

        vis, _, vis4, _, hm = _gla_masks(reverse)
        tri = vis.astype(F32).astype(BF16)
        logits = _nn(lr_ref[...].astype(BF16), lrw_ref[...]) + gb_ref[...]
        a_all = _logsig(logits) * (1.0 / 16.0)
        st = st_ref[...]
        for c in (range(nc - 1, -1, -1) if reverse else range(nc)):
            rows = slice(CH * c, CH * c + CH)
            b = _tri_mm(tri, a_all[rows])
            bl = b[0:1] if reverse else b[CH - 1:CH]
            q = qkv_ref[rows, 0:256].astype(F32) * 0.125
            k = qkv_ref[rows, 256:512].astype(F32)
            v16 = qkv_ref[rows, 512:1024].astype(BF16)
            qd = q * jnp.exp(b)
            kd16 = (k * jnp.exp(-b)).astype(BF16)
            kdec16 = (k * jnp.exp(bl - b)).astype(BF16)
            qstack = _stack_heads(qd, hm)
            sc = jnp.where(vis4, _nt(qstack, kd16), 0.0).astype(BF16)
            inter = _nt(qstack, st.astype(BF16))
            for h in range(4):
                o_ref[rows, 128 * h:128 * h + 128] = (
                    _nn(sc[CH * h:CH * h + CH], v16[:, 128 * h:128 * h + 128]) + inter[CH * h:CH * h + CH])
            sb_ref[c] = st
            st = st * jnp.exp(bl) + _diag_heads(_tn(v16, kdec16), hm)
        st_ref[...] = st

        @pl.when(i == nb - 1)
        def _():
            sfin_ref[...] = st

    return _pcall(
        body, name=name, grid=(nb,),
        in_specs=[pl.BlockSpec((tb,1024), lambda i: (rmap(i), qkv_blk)), pl.BlockSpec((tb,LRW), lambda i: (rmap(i), 0)),
                  _full((LRW, 256)), _full((1, 256)), _full((128, 256))],
        out_specs=[pl.BlockSpec((tb,512), lambda i: (rmap(i), 0)), pl.BlockSpec((nc, 128, 256), lambda i: (rmap(i), 0, 0)),
                   _full((128, 256))],
        out_shape=[SDS((m, 512), F32), SDS((m // CH, 128, 256), F32), SDS((128, 256), F32)],
        scratch_shapes=[pltpu.VMEM((128, 256), F32)], compiler_params=_cp(("arbitrary",)),
    )(p, lr, lrw, gbias, s0)


def _gla_fwd2(p, qkv_blk, lr, lrws, gbiases, s0s, name):
    m = p.shape[0]
    tb = min(GLA_TB, m)
    nb = m // tb
    nc = tb // CH

    def body(qkv_f, lr_f, qkv_b, lr_b, lrw_f, lrw_b, gb_f, gb_b, s0_f, s0_b,
             o_f, sb_f, sfin_f, o_b, sb_b, sfin_b, st_f, st_b):
        i = pl.program_id(0)

        @pl.when(i == 0)
        def _():
            st_f[...] = s0_f[...]
            st_b[...] = s0_b[...]

        dirs = []
        for reverse, qkv_ref, lr_ref, lrw_ref, gb_ref, o_ref, sb_ref, st_ref in (
                (False, qkv_f, lr_f, lrw_f, gb_f, o_f, sb_f, st_f), (True, qkv_b, lr_b, lrw_b, gb_b, o_b, sb_b, st_b)):
            vis, _, vis4, _, hm = _gla_masks(reverse)
            logits = _nn(lr_ref[...].astype(BF16), lrw_ref[...]) + gb_ref[...]
            dirs.append(dict(reverse=reverse, qkv=qkv_ref, o=o_ref, sb=sb_ref, vis4=vis4, hm=hm,
                             tri=vis.astype(F32).astype(BF16), a=_logsig(logits) * (1.0 / 16.0), st=st_ref[...]))
        for step in range(nc):
            for d in dirs:
                c = nc - 1 - step if d["reverse"] else step
                rows = slice(CH * c, CH * c + CH)
                b = _tri_mm(d["tri"], d["a"][rows])
                bl = b[0:1] if d["reverse"] else b[CH - 1:CH]
                q = d["qkv"][rows, 0:256].astype(F32) * 0.125
                k = d["qkv"][rows, 256:512].astype(F32)
                v16 = d["qkv"][rows, 512:1024].astype(BF16)
                qd = q * jnp.exp(b)
                kd16 = (k * jnp.exp(-b)).astype(BF16)
                kdec16 = (k * jnp.exp(bl - b)).astype(BF16)
                qstack = _stack_heads(qd, d["hm"])
                sc = jnp.where(d["vis4"], _nt(qstack, kd16), 0.0).astype(BF16)
                inter = _nt(qstack, d["st"].astype(BF16))
                for h in range(4):
                    d["o"][rows, 128 * h:128 * h + 128] = (
                        _nn(sc[CH * h:CH * h + CH], v16[:, 128 * h:128 * h + 128]) + inter[CH * h:CH * h + CH])
                d["sb"][c] = d["st"]
                d["st"] = d["st"] * jnp.exp(bl) + _diag_heads(_tn(v16, kdec16), d["hm"])
        st_f[...] = dirs[0]["st"]
        st_b[...] = dirs[1]["st"]

        @pl.when(i == nb - 1)
        def _():
            sfin_f[...] = dirs[0]["st"]
            sfin_b[...] = dirs[1]["st"]

    fw = lambda i: i
    bw = lambda i: nb - 1 - i
    in_specs = []
    for rm in (fw, bw):
        in_specs += [pl.BlockSpec((tb, 1024), lambda i, rm=rm: (rm(i), qkv_blk)), pl.BlockSpec((tb, LRW), lambda i, rm=rm: (rm(i), 0))]
    in_specs += [_full((LRW, 256))] * 2 + [_full((1, 256))] * 2 + [_full((128, 256))] * 2
    out_specs, out_shape = [], []
    for rm in (fw, bw):
        out_specs += [pl.BlockSpec((tb, 512), lambda i, rm=rm: (rm(i), 0)), pl.BlockSpec((nc, 128, 256), lambda i, rm=rm: (rm(i), 0, 0)),
                      _full((128, 256))]
        out_shape += [SDS((m, 512), F32), SDS((m // CH, 128, 256), F32), SDS((128, 256), F32)]
    return _pcall(
        body, name=name, grid=(nb,), in_specs=in_specs, out_specs=out_specs, out_shape=out_shape,
        scratch_shapes=[pltpu.VMEM((128, 256), F32), pltpu.VMEM((128, 256), F32)], compiler_params=_cp(("arbitrary",), VMEM_BIG),
    )(p, lr, p, lr, lrws[0], lrws[1], gbiases[0], gbiases[1], s0s[0], s0s[1])


def _gla_bwd(p, qkv_blk, lr, lrw, lrwt, gbias, sb, dsfin, do, prev, dp, *, reverse, name):
    m = p.shape[0]
    tb = min(GLA_TB, m)
    nb = m // tb
    nc = tb // CH
    rmap = (lambda i: i) if reverse else (lambda i: nb - 1 - i)
    has_prev = prev is not None
    has_dp = dp is not None

    def body(*refs):
        refs = list(refs)
        qkv_ref, lr_ref, lrw_ref, lrwt_ref, gb_ref, sb_ref, dsfin_ref, do_ref = refs[:8]
        refs = refs[8:]
        if has_prev:
            pq_ref, plr_ref = refs[:2]
            refs = refs[2:]
        if has_dp:
            refs = refs[1:]
        dqkv_ref, dlr_ref, dw2_ref, dgb_ref, ds0_ref, dst_ref, dlog_ref = refs
        i = pl.program_id(0)

        @pl.when(i == 0)
        def _():
            dst_ref[...] = dsfin_ref[...]
            dw2_ref[...] = jnp.zeros_like(dw2_ref)
            dgb_ref[...] = jnp.zeros_like(dgb_ref)

        vis, vis_t, vis4, vis4_t, hm = _gla_masks(reverse)
        tri = vis.astype(F32).astype(BF16)
        tri_t = vis_t.astype(F32).astype(BF16)
        lane_vis = _lane_vis(reverse, False)
        lane_vis_t = _lane_vis(reverse, True)
        lr16 = lr_ref[...].astype(BF16)
        logits = _nn(lr16, lrw_ref[...]) + gb_ref[...]
        a_all = _logsig(logits) * (1.0 / 16.0)
        dsig = (1.0 - jax.nn.sigmoid(logits)) * (1.0 / 16.0)
        dst = dst_ref[...]
        for c in (range(nc) if reverse else range(nc - 1, -1, -1)):
            rows = slice(CH * c, CH * c + CH)
            b = _tri_mm(tri, a_all[rows])
            bl = b[0:1] if reverse else b[CH - 1:CH]
            eb = jnp.exp(b)
            enb = jnp.exp(-b)
            ebl = jnp.exp(bl - b)
            el = jnp.exp(bl)
            q = qkv_ref[rows, 0:256].astype(F32) * 0.125
            k = qkv_ref[rows, 256:512].astype(F32)
            v16 = qkv_ref[rows, 512:1024].astype(BF16)
            do16 = do_ref[rows, :].astype(BF16)
            qd = q * eb
            kd = k * enb
            kdec = k * ebl
            st = sb_ref[c]
            st16 = st.astype(BF16)
            dst16 = dst.astype(BF16)
            qd16 = qd.astype(BF16)
            kd16 = kd.astype(BF16)
            qstack = _stack_heads(qd, hm)
            kstack = _stack_heads(kd, hm)
            kdecstack = _stack_heads(kdec, hm)
            pt = jnp.where(vis4_t, _nt(kstack, qd16), 0.0).astype(BF16)
            dvinter = _nt(kdecstack, dst16)
            do_rows = _rows_of_heads(do16)
            v_rows = _rows_of_heads(v16)
            dp_cat = jnp.where(lane_vis, _diag_heads(_nt(do_rows, v_rows), hm), 0.0).astype(BF16)
            dpt_cat = jnp.where(lane_vis_t, _diag_heads(_nt(v_rows, do_rows), hm), 0.0).astype(BF16)
            dqd = _nn(dp_cat, kstack) + _diag_heads(_nn(do_rows, st16), hm)
            dkd = _nn(dpt_cat, qstack)
            dkdec = _diag_heads(_nn(v_rows, dst16), hm)
            for h in range(4):
                rh = slice(CH * h, CH * h + CH)
                dv_h = _nn(pt[rh], do_rows[rh]) + dvinter[rh]
                if has_prev:
                    dv_h = dv_h + pq_ref[rows, 512 + 128 * h:512 + 128 * h + 128]
                dqkv_ref[rows, 512 + 128 * h:512 + 128 * h + 128] = dv_h.astype(dqkv_ref.dtype)
            dq = dqd * eb * 0.125
            dk = dkd * enb + dkdec * ebl
            if has_prev:
                dq = dq + pq_ref[rows, 0:256]
                dk = dk + pq_ref[rows, 256:512]
            dqkv_ref[rows, 0:256] = dq.astype(dqkv_ref.dtype)
            dqkv_ref[rows, 256:512] = dk.astype(dqkv_ref.dtype)
            g_kdec = dkdec * kdec
            db = dqd * qd - dkd * kd - g_kdec
            dbl = jnp.sum(g_kdec, axis=0, keepdims=True) + jnp.sum(st * dst, axis=0, keepdims=True) * el
            da = _tri_mm(tri_t, db) + dbl
            dlog_ref[rows, :] = da * dsig[rows]
            dst = dst * el + _diag_heads(_tn(do16, qd16), hm)
        dst_ref[...] = dst
        dlog = dlog_ref[...]
        dlog16 = dlog.astype(BF16)
        dlr = _nn(dlog16, lrwt_ref[...])
        if has_prev:
            dlr = dlr + plr_ref[...]
        dlr_ref[...] = dlr
        dw2_ref[...] += _tn(lr16, dlog16)
        dgb_ref[...] += jnp.sum(dlog, axis=0, keepdims=True)

        @pl.when(i == nb - 1)
        def _():
            ds0_ref[...] = dst

    in_specs = [pl.BlockSpec((tb,1024), lambda i: (rmap(i), qkv_blk)), pl.BlockSpec((tb,LRW), lambda i: (rmap(i), 0)),
                _full((LRW, 256)), _full((256, LRW)), _full((1, 256)), pl.BlockSpec((nc, 128, 256), lambda i: (rmap(i), 0, 0)),
                _full((128, 256)), pl.BlockSpec((tb,512), lambda i: (rmap(i), 0))]
    args = [p, lr, lrw, lrwt, gbias, sb, dsfin, do]
    if has_prev:
        in_specs += [pl.BlockSpec((tb,1024), lambda i: (rmap(i), 0)), pl.BlockSpec((tb,LRW), lambda i: (rmap(i), 0))]
        args += list(prev)
    aliases = {}
    if has_dp:
        in_specs.append(pl.BlockSpec(memory_space=pl.ANY))
        aliases = {len(args): 0}
        args.append(dp)
        dq_spec = pl.BlockSpec((tb,1024), lambda i: (rmap(i), 2))
        dq_shape = SDS(dp.shape, dp.dtype)
    else:
        dq_spec = pl.BlockSpec((tb,1024), lambda i: (rmap(i), 0))
        dq_shape = SDS((m, 1024), F32)
    return _pcall(
        body, name=name, grid=(nb,), in_specs=in_specs,
        out_specs=[dq_spec, pl.BlockSpec((tb,LRW), lambda i: (rmap(i), 0)), _full((LRW, 256)), _full((1, 256)), _full((128, 256))],
        out_shape=[dq_shape, SDS((m, LRW), F32), SDS((LRW, 256), F32), SDS((1, 256), F32), SDS((128, 256), F32)],
        scratch_shapes=[pltpu.VMEM((128, 256), F32), pltpu.VMEM((tb,256), F32)],
        input_output_aliases=aliases, compiler_params=_cp(("arbitrary",)),
    )(*args)


def _device_step(x, c, ctx, c_ctx, tgt, wm, bm, ng, wit_g, wit_r, wlrt, ln_g, ln_b, ws, bs, w2, gb2, gbn, wpa, wpb, wo, gf,
                 exchange=None, shards=None):
    L = x.shape[0]
    wit_qkv = wit_r[2048:3072]
    ws16 = ws.astype(BF16)
    wst16 = jnp.swapaxes(ws, 1, 2).astype(BF16)
    bscol = bs[:, :, None]
    lrw = [jnp.zeros((LRW, 256), F32).at[16 * r:16 * r + 16].set(w2[r]).astype(BF16) for r in range(2)]
    lrwt = [w.T for w in lrw]
    gbias = [gb2[r:r + 1] for r in range(2)]

    cc = jnp.zeros((8, D), F32).at[0:1].set(c).at[1:2].set(c_ctx)
    mod = _modvec(cc, wm, bm)
    shift, scale, gate = mod[0:1, 0:D], mod[0:1, D:2 * D], mod[0:1, 2 * D:3 * D]
    shift_c, scale_c = mod[1:2, 0:D], mod[1:2, D:2 * D]

    hc = _prep_h(ctx, ng, scale_c, shift_c, "prep_hc")
    pc = _mm(hc, wit_qkv, tm=256, tn=1024, tk=D, out_dtype=F32, name="mm_pc", b_t=True)
    plrc = _mm(hc, wlrt, tm=256, tn=LRW, tk=D, out_dtype=F32, name="mm_plrc", b_t=True)
    zero_s = jnp.zeros((128, 256), F32)
    _, sbc_f, sc_f, _, sbc_b, sc_b = _gla_fwd2(pc, 0, plrc, lrw, gbias, (zero_s, zero_s), "gla_fwd_c")

    h, p, plr, vnr, vnc, late = _proj_fwd(x, ng, scale, shift, wit_g, wit_r, wlrt, ln_g, ln_b,
                                          shards if shards is not None else ())
    if shards is not None:
        me_xy = 2 * lax.axis_index("x") + lax.axis_index("y")
        g_wpa, g_wpb, g_wo = (_own(g, s_, me_xy) for g, s_ in zip(late, shards))
        wpa = jnp.swapaxes(g_wpa, 0, 1).reshape(512, D)
        wpb = jnp.swapaxes(g_wpb, 0, 1).reshape(512, D)
        wo = g_wo.reshape(D, D)
    o_f, sb_f, _, o_b, sb_b, _ = _gla_fwd2(p, 2, plr, lrw, gbias, (sc_f, sc_b), "gla_fwd")
    svc = _colmix_fwd(vnc.reshape(2, AC, L), ws16[2:4], bscol[2:4]).reshape(2, L, 128)
    ya_in, yb_in, svr, dwo, dx1, dout, loss, dgate, dgf = _tail_fwd(
        o_f, o_b, p, vnr, svc, x, tgt, ws16[0:2], bscol[0:2], gbn, wpa, wpb, wo, gate, gf)

    dwpa, dwpb, dp_g, dp, dsr, dsc, do, dgbn = _tail_bwd(dout, ya_in, yb_in, p, svr, svc, o_f, o_b, gbn, wo, wpa, wpb)
    dvnc, dws23, dbs23 = _colmix_bwd(dsc.reshape(2, AC, L), vnc.reshape(2, AC, L), wst16[2:4])
    dp, dws01, dbs01, dlng, dlnb = _ln_bwd(dsr, vnr, dvnc.reshape(2, L, 128), p, wst16[0:2], ln_g, dp)
    zero_ds = jnp.zeros((128, 256), F32)
    dqkv_f, dlr_f, dw2_f, dgb_f, ds0_f = _gla_bwd(p, 2, plr, lrw[0], lrwt[0], gbias[0], sb_f, zero_ds, do, None, None,
                                                  reverse=False, name="gla_bwd_f")
    dp, dlr, dw2_b, dgb_b, ds0_b = _gla_bwd(p, 2, plr, lrw[1], lrwt[1], gbias[1], sb_b, zero_ds, do, (dqkv_f, dlr_f), dp,
                                            reverse=True, name="gla_bwd_b")
    zero_do = jnp.zeros((ctx.shape[0], 512), F32)
    dqkvc_f, dlrc_f, dw2c_f, dgbc_f, _ = _gla_bwd(pc, 0, plrc, lrw[0], lrwt[0], gbias[0], sbc_f, ds0_f, zero_do, None, None,
                                                  reverse=False, name="gla_bwd_cf")
    dqkvc, dlrc, dw2c_b, dgbc_b, _ = _gla_bwd(pc, 0, plrc, lrw[1], lrwt[1], gbias[1], sbc_b, ds0_b, zero_do,
                                              (dqkvc_f, dlrc_f), None, reverse=True, name="gla_bwd_cb")
    dhc = _mm(dqkvc, wit_qkv, tm=256, tn=D, tk=1024, out_dtype=F32, name="mm_dhc")
    dhc = _mm(dlrc, wlrt, tm=256, tn=D, tk=LRW, out_dtype=F32, name="mm_dhc_lr", acc=dhc)
    _, dng_c, dscale_c, dshift_c = _prep_bwd(ctx, dhc, None, ng, scale_c, "prep_bwd_c")

    dwit_g = _mm_tn(dp_g, h, ta=1024, tn=D, tk=2048, name="mm_dwi_g")
    dwit_r = _mm_tn(dp, h, ta=1024, tn=D, tk=2048, name="mm_dwi_r")
    dwit_qkv = _mm_tn(dqkvc, hc, ta=1024, tn=D, tk=256, name="mm_dwi_c", acc=dwit_r[2048:3072])
    dwlrt = _mm_tn(dlr, h, ta=LRW, tn=D, tk=2048, name="mm_dwlr")
    dwlrt = _mm_tn(dlrc, hc, ta=LRW, tn=D, tk=256, name="mm_dwlr_c", acc=dwlrt)
    big = dict(dwit_g=dwit_g, dwit_r=dwit_r, dwit_qkv=dwit_qkv, dwlrt=dwlrt, dwpa=dwpa, dwpb=dwpb, dwo=dwo)

    send = exchange(big) if exchange is not None else ()
    (dx, dng, dscale, dshift), got = _proj_bwd(dp_g, dp, dlr, wit_g, wit_r, wlrt, x, dx1, ng, scale, send)

    dmodc = jnp.concatenate([dshift_c, dscale_c], axis=1)
    dscc = _dcctx(jnp.zeros((8, 2 * D), F32).at[0:1].set(dmodc), wm)[0:1]
    dw2p = dw2_f + dw2c_f, dw2_b + dw2c_b
    return dict(
        loss=loss[0, 0], dx=dx, got=got, **big,
        dmod=jnp.concatenate([dshift, dscale, dgate], axis=1), dmodc=dmodc, dscc=dscc, dng=dng + dng_c,
        dlng=dlng, dlnb=dlnb, dws=jnp.concatenate([dws01, dws23], axis=0),
        dbs=jnp.concatenate([dbs01, dbs23], axis=0)[:, :, 0], dgbn=dgbn, dgf=dgf,
        dw2=jnp.stack([dw2p[0][0:16], dw2p[1][16:32]]), dgb2=jnp.concatenate([dgb_f + dgbc_f, dgb_b + dgbc_b], axis=0),
    )


ANY = pl.BlockSpec(memory_space=pl.ANY)


def _coords():
    return lax.axis_index("x"), lax.axis_index("y"), lax.axis_index("c")


def _flip(v, bit):
    return 1 - v if bit else v


def _remote(src, dst, send_sem, recv_sem, dev):
    return pltpu.make_async_remote_copy(src_ref=src, dst_ref=dst, send_sem=send_sem, recv_sem=recv_sem,
                                        device_id=dev, device_id_type=MESH)


def _own(out, block, idx):
    return lax.dynamic_update_slice_in_dim(out, block[None], idx, axis=0)


def _half_idx(shape, axis, which, lead=()):
    idx = [pl.ds(0, d) for d in shape]
    h = shape[axis] // 2
    idx[axis] = pl.ds(which * h, h)
    return tuple(lead) + tuple(idx)


def _gather_weights(split, whole, name):
    ns, nw = len(split), len(whole)
    n = ns + nw
    arrs = [a for a, _ in split] + list(whole)

    def body(*refs):
        ins, outs = refs[:n], refs[n:2 * n]
        a_send, a_recv, b_send, b_recv = refs[2 * n:]
        x, y, c = _coords()
        me = 2 * x + y
        sib = (x, y, 1 - c)
        peers = [(1 - x, y), (x, 1 - y), (1 - x, 1 - y)]

        def half(k, slot, which):
            return outs[k].at[_half_idx(arrs[k].shape, split[k][1], which, lead=(slot,))]

        sends = []
        for k in range(n):
            for j, (px, py) in enumerate(peers):
                if k < ns:
                    rc = _remote(ins[k].at[_half_idx(arrs[k].shape, split[k][1], c)], half(k, me, c), a_send.at[3 * k + j],
                                 a_recv.at[3 * k + j], (px, py, c))
                else:
                    rc = _remote(ins[k], outs[k].at[me], a_send.at[3 * k + j], a_recv.at[3 * k + j], (px, py, c))
                rc.start()
                sends.append(rc)
        for k in range(ns):
            for j, (px, py) in enumerate(peers):
                landed = half(k, 2 * px + py, c)
                _remote(landed, landed, a_send.at[3 * k + j], a_recv.at[3 * k + j], (px, py, c)).wait_recv()
                fw = _remote(landed, landed, b_send.at[3 * k + j], b_recv.at[3 * k + j], sib)
                fw.start()
                sends.append(fw)
        for k in range(ns, n):
            for j, (px, py) in enumerate(peers):
                landed = outs[k].at[2 * px + py]
                _remote(landed, landed, a_send.at[3 * k + j], a_recv.at[3 * k + j], (px, py, c)).wait_recv()
        for k in range(ns):
            for j, (px, py) in enumerate(peers):
                passed = half(k, 2 * px + py, 1 - c)
                _remote(passed, passed, b_send.at[3 * k + j], b_recv.at[3 * k + j], sib).wait_recv()
        for rc in sends:
            rc.wait_send()

    outs = _pcall(
        body, name=name, in_specs=[ANY] * n, out_specs=[ANY] * n,
        out_shape=[SDS((4,) + a.shape, a.dtype) for a in arrs],
        scratch_shapes=[pltpu.SemaphoreType.DMA((3 * n,)), pltpu.SemaphoreType.DMA((3 * n,)), pltpu.SemaphoreType.DMA((3 * ns,)),
                        pltpu.SemaphoreType.DMA((3 * ns,))],
    )(*arrs)
    me_xy = 2 * lax.axis_index("x") + lax.axis_index("y")
    return [_own(o, a, me_xy) for o, a in zip(outs, arrs)]


def _gather_all(a, swap, name):
    masks = [(mx, my, mc) for mx in range(2) for my in range(2) for mc in range(2)][1:]
    n = len(swap)

    def body(*refs):
        in_ref, sw_in = refs[0], refs[1:1 + n]
        out_ref, sw_out = refs[1 + n], refs[2 + n:2 + 2 * n]
        send_sems, recv_sems = refs[2 + 2 * n:]
        x, y, c = _coords()
        me = 4 * x + 2 * y + c
        sends = []
        for j, (mx, my, mc) in enumerate(masks):
            rc = _remote(in_ref, out_ref.at[me], send_sems.at[j], recv_sems.at[j], (_flip(x, mx), _flip(y, my), _flip(c, mc)))
            rc.start()
            sends.append(rc)
        for k in range(n):
            rc = _remote(sw_in[k], sw_out[k], send_sems.at[7 + k], recv_sems.at[7 + k], (x, y, 1 - c))
            rc.start()
            sends.append(rc)
        for j, (mx, my, mc) in enumerate(masks):
            px, py, pc = _flip(x, mx), _flip(y, my), _flip(c, mc)
            landed = out_ref.at[4 * px + 2 * py + pc]
            _remote(landed, landed, send_sems.at[j], recv_sems.at[j], (px, py, pc)).wait_recv()
        for k in range(n):
            _remote(sw_out[k], sw_out[k], send_sems.at[7 + k], recv_sems.at[7 + k], (x, y, 1 - c)).wait_recv()
        for rc in sends:
            rc.wait_send()

    res = _pcall(
        body, name=name, in_specs=[ANY] * (1 + n), out_specs=[ANY] * (1 + n),
        out_shape=[SDS((8,) + a.shape, a.dtype)] + [SDS(s_.shape, s_.dtype) for s_ in swap],
        scratch_shapes=[pltpu.SemaphoreType.DMA((7 + n,)), pltpu.SemaphoreType.DMA((7 + n,))],
    )(a, *swap)
    return _own(res[0], a, 4 * lax.axis_index("x") + 2 * lax.axis_index("y") + lax.axis_index("c")), list(res[1:])


def _half_shape(shape, axis):
    return tuple(d // 2 if i == axis else d for i, d in enumerate(shape))


def _swap_half_c(arrs, axes, name):
    n = len(arrs)

    def body(*refs):
        ins, outs = refs[:n], refs[n:2 * n]
        send_sems, recv_sems = refs[2 * n:]
        x, y, c = _coords()
        sends = []
        for k in range(n):
            rc = _remote(ins[k].at[_half_idx(arrs[k].shape, axes[k], 1 - c)], outs[k], send_sems.at[k], recv_sems.at[k],
                         (x, y, 1 - c))
            rc.start()
            sends.append(rc)
        for rc in sends:
            rc.wait()

    return _pcall(
        body, name=name, in_specs=[ANY] * n, out_specs=[ANY] * n,
        out_shape=[SDS(_half_shape(a.shape, ax), a.dtype) for a, ax in zip(arrs, axes)],
        scratch_shapes=[pltpu.SemaphoreType.DMA((n,)), pltpu.SemaphoreType.DMA((n,))],
    )(*arrs)


def _a2a_xy(arrs, name):
    n = len(arrs)

    def body(*refs):
        ins, outs = refs[:n], refs[n:2 * n]
        send_sems, recv_sems = refs[2 * n:]
        x, y, c = _coords()
        me = 2 * x + y
        peers = [(1 - x, y), (x, 1 - y), (1 - x, 1 - y)]
        sends = []
        for k in range(n):
            for j, (px, py) in enumerate(peers):
                rc = _remote(ins[k].at[2 * px + py], outs[k].at[me], send_sems.at[3 * k + j], recv_sems.at[3 * k + j], (px, py, c))
                rc.start()
                sends.append(rc)
        for k in range(n):
            for j, (px, py) in enumerate(peers):
                landed = outs[k].at[2 * px + py]
                _remote(landed, landed, send_sems.at[3 * k + j], recv_sems.at[3 * k + j], (px, py, c)).wait_recv()
        for rc in sends:
            rc.wait_send()

    outs = _pcall(
        body, name=name, in_specs=[ANY] * n, out_specs=[ANY] * n, out_shape=[SDS(a.shape, a.dtype) for a in arrs],
        scratch_shapes=[pltpu.SemaphoreType.DMA((3 * n,)), pltpu.SemaphoreType.DMA((3 * n,))],
    )(*arrs)
    me_xy = 2 * lax.axis_index("x") + lax.axis_index("y")
    return [_own(o, lax.dynamic_index_in_dim(a, me_xy, axis=0, keepdims=False), me_xy) for o, a in zip(outs, arrs)]


def _exchange_c(arrs, name):
    n = len(arrs)

    def body(*refs):
        ins, outs = refs[:n], refs[n:2 * n]
        send_sems, recv_sems = refs[2 * n:]
        x, y, c = _coords()
        sends = []
        for k in range(n):
            rc = _remote(ins[k], outs[k], send_sems.at[k], recv_sems.at[k], (x, y, 1 - c))
            rc.start()
            sends.append(rc)
        for rc in sends:
            rc.wait()

    return _pcall(
        body, name=name, in_specs=[ANY] * n, out_specs=[ANY] * n, out_shape=[SDS(a.shape, a.dtype) for a in arrs],
        scratch_shapes=[pltpu.SemaphoreType.DMA((n,)), pltpu.SemaphoreType.DMA((n,))],
    )(*arrs)


def _join_halves(halves, axes, name):
    n = len(halves)
    full = [tuple(2 * d if i == ax else d for i, d in enumerate(a.shape)) for a, ax in zip(halves, axes)]

    def body(*refs):
        ins, outs = refs[:n], refs[n:2 * n]
        send_sems, recv_sems = refs[2 * n:]
        x, y, c = _coords()
        sends = []
        for k in range(n):
            rc = _remote(ins[k], outs[k].at[_half_idx(full[k], axes[k], c)], send_sems.at[k], recv_sems.at[k], (x, y, 1 - c))
            rc.start()
            sends.append(rc)
        for k in range(n):
            landed = outs[k].at[_half_idx(full[k], axes[k], 1 - c)]
            _remote(landed, landed, send_sems.at[k], recv_sems.at[k], (x, y, 1 - c)).wait_recv()
        for rc in sends:
            rc.wait_send()

    outs = _pcall(
        body, name=name, in_specs=[ANY] * n, out_specs=[ANY] * n,
        out_shape=[SDS(f, a.dtype) for f, a in zip(full, halves)],
        scratch_shapes=[pltpu.SemaphoreType.DMA((n,)), pltpu.SemaphoreType.DMA((n,))],
    )(*halves)
    ci = lax.axis_index("c")
    return [lax.dynamic_update_slice_in_dim(o, a, ci * a.shape[ax], axis=ax) for o, a, ax in zip(outs, halves, axes)]


def _pair_sum(a, got, cidx, axis, name):
    _, r, cdim = a.shape
    hshape = _half_shape(a.shape, axis)

    def body(c_ref, a_ref, g_ref, o_ref):
        o_ref[...] = (a_ref[...] + g_ref[...]).astype(BF16)

    if axis == 1:
        tr = min(r // 2, 256)
        nj = (r // 2) // tr
        blk = pl.BlockSpec((1, tr, cdim), lambda s, j, c: (s, j, 0))
        a_spec = pl.BlockSpec((1, tr, cdim), lambda s, j, c: (s, c[0] * nj + j, 0))
    else:
        nj = (cdim // 2) // 128
        blk = pl.BlockSpec((1, r, 128), lambda s, j, c: (s, 0, j))
        a_spec = pl.BlockSpec((1, r, 128), lambda s, j, c: (s, 0, c[0] * nj + j))
    return _pcall(
        body, name=name, out_shape=SDS(hshape, BF16),
        grid_spec=pltpu.PrefetchScalarGridSpec(num_scalar_prefetch=1, grid=(4, nj), in_specs=[a_spec, blk], out_specs=blk),
        compiler_params=_cp(("parallel", "parallel")),
    )(cidx, a, got)


def _sum_chips(parts, name):
    _, h, cdim = parts.shape

    def body(p_ref, o_ref):
        acc = p_ref[0].astype(F32)
        for k in range(1, 4):
            acc = acc + p_ref[k].astype(F32)
        o_ref[...] = acc

    if h % 256 == 0 or h in (128,):
        tr = min(h, 256)
        grid, in_spec, out_spec = (h // tr,), pl.BlockSpec((4, tr, cdim), lambda i: (0, i, 0)), pl.BlockSpec((tr, cdim), lambda i: (i, 0))
    else:
        grid, in_spec, out_spec = (cdim // 128,), pl.BlockSpec((4, h, 128), lambda i: (0, 0, i)), pl.BlockSpec((h, 128), lambda i: (0, i))
    return _pcall(
        body, name=name, grid=grid, in_specs=[in_spec], out_specs=out_spec, out_shape=SDS((h, cdim), F32),
        compiler_params=_cp(("parallel",)),
    )(parts)


def _sum_slots(a, name, rows):
    s, n, _ = a.shape

    def body(a_ref, o_ref):
        acc = a_ref[0]
        for k in range(1, s):
            acc = acc + a_ref[k]
        o_ref[...] = acc

    return _pcall(
        body, name=name, grid=(n // rows,), in_specs=[pl.BlockSpec((s, rows, 128), lambda i: (0, i, 0))],
        out_specs=pl.BlockSpec((rows, 128), lambda i: (i, 0)), out_shape=SDS((n, 128), F32),
        compiler_params=_cp(("parallel",)),
    )(a)


def _adam_math(w, g, m, v):
    nm = ADAM_B1 * m + (1.0 - ADAM_B1) * g
    nv = ADAM_B2 * v + (1.0 - ADAM_B2) * (g * g)
    m_hat = nm / (1.0 - ADAM_B1 ** ADAM_STEP)
    v_hat = nv / (1.0 - ADAM_B2 ** ADAM_STEP)
    return -ADAM_LR * (m_hat / (jnp.sqrt(v_hat) + ADAM_EPS) + ADAM_WD * w), nm, nv


def _adamw(w, g, m, v, name, rows):
    r, cdim = w.shape

    def body(w_ref, g_ref, m_ref, v_ref, d_ref, nm_ref, nv_ref):
        d_ref[...], nm_ref[...], nv_ref[...] = _adam_math(w_ref[...], g_ref[...], m_ref[...], v_ref[...])

    blk = pl.BlockSpec((rows, cdim), lambda i: (i, 0))
    return _pcall(
        body, name=name, grid=(r // rows,), in_specs=[blk] * 4, out_specs=[blk] * 3,
        out_shape=[SDS(w.shape, F32)] * 3, compiler_params=_cp(("parallel",)),
    )(w, g, m, v)


def _adamw_joined(w, mine, other, m, v, cidx, axis, name, rows):
    r, cdim = w.shape
    if axis == 0:
        rows = r

    def body(c_ref, w_ref, a_ref, b_ref, m_ref, v_ref, g_ref, d_ref, nm_ref, nv_ref):
        a, b = a_ref[...], b_ref[...]
        g = jnp.where(c_ref[0] == 0, jnp.concatenate([a, b], axis=axis), jnp.concatenate([b, a], axis=axis))
        g_ref[...] = g
        d_ref[...], nm_ref[...], nv_ref[...] = _adam_math(w_ref[...], g, m_ref[...], v_ref[...])

    blk = pl.BlockSpec((rows, cdim), lambda i, c: (i, 0))
    hshape = (rows // 2, cdim) if axis == 0 else (rows, cdim // 2)
    hblk = pl.BlockSpec(hshape, lambda i, c: (i, 0))
    return _pcall(
        body, name=name, out_shape=[SDS(w.shape, F32)] * 4,
        grid_spec=pltpu.PrefetchScalarGridSpec(num_scalar_prefetch=1, grid=(r // rows,), in_specs=[blk, hblk, hblk, blk, blk],
                                               out_specs=[blk] * 4),
        compiler_params=_cp(("parallel",)),
    )(cidx, w, mine, other, m, v)


def _adamw_many(ws, gs, ms, vs, name):
    n = len(ws)

    def body(*refs):
        outs = refs[4 * n:]
        for k in range(n):
            d, nm, nv = _adam_math(refs[k][...], refs[n + k][...], refs[2 * n + k][...], refs[3 * n + k][...])
            outs[k][...] = d
            outs[n + k][...] = nm
            outs[2 * n + k][...] = nv

    res = _pcall(body, name=name, out_shape=[SDS(w.shape, F32) for w in ws] * 3)(*ws, *gs, *ms, *vs)
    return res[:n], res[n:2 * n], res[2 * n:]


def _pack(pieces, rows):
    flat = jnp.concatenate([p.reshape(-1) for p in pieces])
    return jnp.pad(flat, (0, rows * 128 - flat.shape[0])).reshape(rows, 128)


def _unpack(buf, shapes):
    flat = buf.reshape(-1)
    out, off = [], 0
    for shp in shapes:
        size = 1
        for s in shp:
            size *= s
        out.append(flat[off:off + size].reshape(shp))
        off += size
    return out


def _perm_cols(w):
    perm = jnp.concatenate([w[..., 3104:5152], w[..., 0:1024], w[..., 1056:1568], w[..., 1568:2080], w[..., 2592:3104],
                            w[..., 2080:2592]], axis=-1)
    return perm, w[..., 1024:1056]


def _unperm_cols(perm, lr32):
    return jnp.concatenate([perm[..., 2048:3072], lr32, perm[..., 3072:3584], perm[..., 3584:4096], perm[..., 4608:5120],
                            perm[..., 4096:4608], perm[..., 0:2048]], axis=-1)


SMALL_ROWS = 672
HALF_ROWS = 7200


def kernel(x, c, ctx, c_ctx, w_mod, b_mod, norm_g, w_in, a_ln_g, a_ln_b, a_ws, a_bs, b_gate_w2, b_gate_b, b_norm_g, w_proj_a, w_proj_b, w_out, final_norm_g, loss_target, m_c_ctx, m_w_mod, m_b_mod, m_norm_g, m_w_in, m_a_ln_g, m_a_ln_b, m_a_ws, m_a_bs, m_b_gate_w2, m_b_gate_b, m_b_norm_g, m_w_proj_a, m_w_proj_b, m_w_out, m_final_norm_g, v_c_ctx, v_w_mod, v_b_mod, v_norm_g, v_w_in, v_a_ln_g, v_a_ln_b, v_a_ws, v_a_bs, v_b_gate_w2, v_b_gate_b, v_b_norm_g, v_w_proj_a, v_w_proj_b, v_w_out, v_final_norm_g):
    xi, yi, ci = _coords()
    me_xy = 2 * xi + yi

    gate_pack = _pack([b_gate_w2[0], b_gate_b[0]], 24)
    w_in_t, m_w_in_t, v_w_in_t = (jnp.swapaxes(a[0], 0, 1) for a in (w_in, m_w_in, v_w_in))
    g_wit, g_wm, g_gate = _gather_weights([(w_in_t.astype(BF16), 1), (w_mod[0].astype(BF16), 0)], [gate_pack], "gather_weights")
    late_shards = (w_proj_a[0].astype(BF16), w_proj_b[0].astype(BF16), w_out[0].astype(BF16))
    wit_u = g_wit.reshape(4 * 1288, D)
    wit_g = wit_u[3104:5152]
    wit_r = jnp.concatenate([wit_u[1056:1568], wit_u[1568:2080], wit_u[2592:3104], wit_u[2080:2592], wit_u[0:1024]], axis=0)
    wlrt = jnp.pad(wit_u[1024:1056], ((0, LRW - 32), (0, 0)))
    wm = jnp.swapaxes(g_wm, 0, 1).reshape(D, 3 * D)
    gflat = g_gate.reshape(4, 24 * 128)
    w2 = jnp.swapaxes(gflat[:, 0:2048].reshape(4, 2, 16, 64), 0, 2)
    w2 = jnp.swapaxes(w2, 0, 1).reshape(2, 16, 256)
    gb2 = jnp.swapaxes(gflat[:, 2048:2176].reshape(4, 2, 64), 0, 1).reshape(2, 256)

    tags = ["wi", "wpa", "wpb", "wo"]
    half_axes = [2, 1, 1, 1]
    sent = []

    def exchange(g):
        dwr = g["dwit_r"]
        dwit_u = jnp.concatenate([g["dwit_qkv"], g["dwlrt"][0:32], dwr[0:512], dwr[512:1024], dwr[1536:2048], dwr[1024:1536],
                                  g["dwit_g"]], axis=0)
        big = [dwit_u.reshape(4, 1288, D), jnp.swapaxes(g["dwpa"].reshape(512, 4, 256), 0, 1),
               jnp.swapaxes(g["dwpb"].reshape(512, 4, 256), 0, 1), g["dwo"].reshape(4, 256, D)]
        other = _swap_half_c(big, half_axes, "swap_half_in")
        cidx = jnp.reshape(ci, (1,)).astype(jnp.int32)
        sent.extend(_pair_sum(a, o, cidx, ax, "sum_pair_" + t) for a, o, ax, t in zip(big, other, half_axes, tags))
        return sent

    r = _device_step(x[0], c, ctx[0], c_ctx[None], loss_target[0], wm, b_mod, norm_g, wit_g, wit_r, wlrt, a_ln_g, a_ln_b,
                     a_ws[0], a_bs[0], w2, gb2, b_norm_g, None, None, None, final_norm_g[None], exchange, late_shards)

    parts = [_own(g, lax.dynamic_index_in_dim(s_, me_xy, axis=0, keepdims=False), me_xy) for g, s_ in zip(r["got"], sent)]
    halves = [_sum_chips(p_, "sum_chips_" + t) for p_, t in zip(parts, tags)]

    small = _pack([r["dmod"], c, r["dmodc"], r["dscc"], r["dng"], r["dlng"], r["dlnb"], r["dws"], r["dbs"], r["dgbn"], r["dgf"],
                   r["dw2"], r["dgb2"], jnp.broadcast_to(r["loss"], (128,))], SMALL_ROWS)
    small_all, others = _gather_all(small, halves, "gather_small")
    small_sum = _sum_slots(small_all, "sum_small", SMALL_ROWS // 4)
    (s_dmod, _, s_dmodc, s_dscc, s_dng, s_dlng, s_dlnb, s_dws, s_dbs, s_dgbn, s_dgf, s_dw2, s_dgb2, s_loss) = _unpack(
        small_sum, [(1, 3 * D), (1, D), (1, 2 * D), (D,), (1, D), (1, 512), (1, 512), (1, 4, 128, 128), (1, 4, 128), (1, 512),
                    (D,), (2, 16, 256), (2, 256), (128,)])
    loss = s_loss[0]
    s_dmodc_p = jnp.pad(s_dmodc, ((0, 0), (0, D)))
    g_b_mod = s_dmod + s_dmodc_p
    sg = jax.nn.sigmoid(c_ctx)
    g_c_ctx = s_dscc * (sg * (1.0 + c_ctx * (1.0 - sg)))
    g_w2 = lax.dynamic_slice_in_dim(s_dw2, 64 * me_xy, 64, axis=2)[None]
    g_gb2 = lax.dynamic_slice_in_dim(s_dgb2, 64 * me_xy, 64, axis=1)[None]

    flat_all = small_all.reshape(8, SMALL_ROWS * 128)
    dmod_all = flat_all[:, 0:3 * D]
    c_all = flat_all[:, 3 * D:4 * D]
    lhs = jnp.concatenate([_silu(c_all), _silu(c_ctx)[None], jnp.zeros((7, D), F32)], axis=0)
    rhs = jnp.concatenate([dmod_all, s_dmodc_p, jnp.zeros((7, 3 * D), F32)], axis=0)
    rhs = lax.dynamic_slice_in_dim(rhs, 768 * me_xy, 768, axis=1)
    g_w_mod = _mm(lhs.T.astype(BF16), rhs.astype(BF16), tm=D, tn=768, tk=16, out_dtype=F32, name="mm_dwm")

    cidx = jnp.reshape(ci, (1,)).astype(jnp.int32)
    g_w_in_t, d_w_in_t, nm_w_in_t, nv_w_in_t = _adamw_joined(w_in_t, halves[0], others[0], m_w_in_t, v_w_in_t, cidx, 1,
                                                             "adamw_w_in", 184)
    g_w_in, d_w_in, nm_w_in, nv_w_in = (jnp.swapaxes(a, 0, 1) for a in (g_w_in_t, d_w_in_t, nm_w_in_t, nv_w_in_t))
    g_wpa, d_wpa, nm_wpa, nv_wpa = _adamw_joined(w_proj_a[0], halves[1], others[1], m_w_proj_a[0], v_w_proj_a[0], cidx, 0,
                                                 "adamw_wpa", 0)
    g_wpb, d_wpb, nm_wpb, nv_wpb = _adamw_joined(w_proj_b[0], halves[2], others[2], m_w_proj_b[0], v_w_proj_b[0], cidx, 0,
                                                 "adamw_wpb", 0)
    g_wo, d_wo, nm_wo, nv_wo = _adamw_joined(w_out[0], halves[3], others[3], m_w_out[0], v_w_out[0], cidx, 0, "adamw_wo", 0)
    d_w_mod, nm_w_mod, nv_w_mod = _adamw(w_mod[0], g_w_mod, m_w_mod[0], v_w_mod[0], "adamw_w_mod", 256)

    names = ["c_ctx", "b_mod", "norm_g", "a_ln_g", "a_ln_b", "a_ws", "a_bs", "b_gate_w2", "b_gate_b", "b_norm_g", "final_norm_g"]
    ws_ = [c_ctx, b_mod, norm_g, a_ln_g, a_ln_b, a_ws, a_bs, b_gate_w2, b_gate_b, b_norm_g, final_norm_g]
    gs_ = [g_c_ctx, g_b_mod, s_dng, s_dlng, s_dlnb, s_dws, s_dbs, g_w2, g_gb2, s_dgbn, s_dgf]
    ms_ = [m_c_ctx, m_b_mod, m_norm_g, m_a_ln_g, m_a_ln_b, m_a_ws, m_a_bs, m_b_gate_w2, m_b_gate_b, m_b_norm_g, m_final_norm_g]
    vs_ = [v_c_ctx, v_b_mod, v_norm_g, v_a_ln_g, v_a_ln_b, v_a_ws, v_a_bs, v_b_gate_w2, v_b_gate_b, v_b_norm_g, v_final_norm_g]
    shapes = [w.shape for w in ws_]
    flat2 = [(1, 1024), (1, 3072), (1, 1024), (1, 512), (1, 512), (512, 128), (4, 128), (32, 64), (2, 64), (1, 512), (1, 1024)]
    as2d = lambda arrs: [a.reshape(s) for a, s in zip(arrs, flat2)]
    d_s, nm_s, nv_s = _adamw_many(as2d(ws_), as2d(gs_), as2d(ms_), as2d(vs_), "adamw_small")
    d_small = {n: a.reshape(s) for n, a, s in zip(names, d_s, shapes)}
    nm_small = {n: a.reshape(s) for n, a, s in zip(names, nm_s, shapes)}
    nv_small = {n: a.reshape(s) for n, a, s in zip(names, nv_s, shapes)}
    g_small = {n: g.reshape(s) for n, g, s in zip(names, gs_, shapes)}

    order = ["c_ctx", "w_mod", "b_mod", "norm_g", "w_in", "a_ln_g", "a_ln_b", "a_ws", "a_bs", "b_gate_w2", "b_gate_b", "b_norm_g",
             "w_proj_a", "w_proj_b", "w_out", "final_norm_g"]
    big_g = dict(w_mod=g_w_mod[None], w_in=g_w_in[None], w_proj_a=g_wpa[None], w_proj_b=g_wpb[None], w_out=g_wo[None])
    big_d = dict(w_mod=d_w_mod[None], w_in=d_w_in[None], w_proj_a=d_wpa[None], w_proj_b=d_wpb[None], w_out=d_wo[None])
    big_m = dict(w_mod=nm_w_mod[None], w_in=nm_w_in[None], w_proj_a=nm_wpa[None], w_proj_b=nm_wpb[None], w_out=nm_wo[None])
    big_v = dict(w_mod=nv_w_mod[None], w_in=nv_w_in[None], w_proj_a=nv_wpa[None], w_proj_b=nv_wpb[None], w_out=nv_wo[None])
    grads = [big_g[n] if n in big_g else g_small[n] for n in order]
    deltas = [big_d[n] if n in big_d else d_small[n] for n in order]
    new_m = [big_m[n] if n in big_m else nm_small[n] for n in order]
    new_v = [big_v[n] if n in big_v else nv_small[n] for n in order]
    return (loss, r["dx"][None], *grads, *deltas, *new_m, *new_v)
```

```python
import functools

import jax
import jax.numpy as jnp
from jax import lax
from jax.experimental import pallas as pl
from jax.experimental.pallas import tpu as pltpu

F32 = jnp.float32
BF16 = jnp.bfloat16
SDS = jax.ShapeDtypeStruct

D = 1024
NP = 5120
LRW = 128
CH = 64
AC = 128
EPS = 1e-6
TOK = 512
GLA_TB = 1024
VMEM_BIG = 48 * 1024 * 1024

ADAM_LR, ADAM_B1, ADAM_B2, ADAM_EPS, ADAM_WD, ADAM_STEP = 0.001, 0.9, 0.999, 1e-08, 0.01, 10

_pcall = pl.pallas_call
MESH = pl.DeviceIdType.MESH


def _cp(sem=None, vmem=None):
    kw = {}
    if sem is not None:
        kw["dimension_semantics"] = sem
    if vmem is not None:
        kw["vmem_limit_bytes"] = vmem
    return pltpu.CompilerParams(**kw)


def _silu(x):
    return x * jax.nn.sigmoid(x)


def _dsilu(x):
    s = jax.nn.sigmoid(x)
    return s * (1.0 + x * (1.0 - s))


def _silu_and_grad(x):
    s = jax.nn.sigmoid(x)
    return x * s, s * (1.0 + x * (1.0 - s))


def _logsig(x):
    return jnp.minimum(x, 0.0) - jnp.log1p(jnp.exp(-jnp.abs(x)))


def _nt(a, b):
    return lax.dot_general(a, b, (((1,), (1,)), ((), ())), preferred_element_type=F32)


def _tn(a, b):
    return lax.dot_general(a, b, (((0,), (0,)), ((), ())), preferred_element_type=F32)


def _nn(a, b):
    return jnp.dot(a, b, preferred_element_type=F32)


def _full(shape):
    return pl.BlockSpec(shape, lambda *_: (0,) * len(shape))


def _mm(a, b, *, tm, tn, tk, out_dtype, name, acc=None, n_outer=False, b_t=False):
    m, k = a.shape
    n, k2 = (b.shape if b_t else b.shape[::-1])
    assert k == k2 and m % tm == 0 and n % tn == 0 and k % tk == 0, (a.shape, b.shape, tm, tn, tk)
    nk = k // tk
    has_acc = acc is not None

    def body(*refs):
        if has_acc:
            a_ref, b_ref, c_ref, o_ref = refs[:4]
        else:
            a_ref, b_ref, o_ref = refs[:3]
        part = (_nt if b_t else _nn)(a_ref[...].astype(BF16), b_ref[...].astype(BF16))
        if nk == 1:
            o_ref[...] = ((c_ref[...] + part) if has_acc else part).astype(out_dtype)
            return
        acc_ref = refs[-1]
        kk = pl.program_id(2)

        @pl.when(kk == 0)
        def _():
            if has_acc:
                acc_ref[...] = c_ref[...] + part
            else:
                acc_ref[...] = part

        @pl.when(kk > 0)
        def _():
            acc_ref[...] += part

        @pl.when(kk == nk - 1)
        def _():
            o_ref[...] = acc_ref[...].astype(out_dtype)

    if n_outer:
        ij = lambda g0, g1: (g1, g0)
        grid = (n // tn, m // tm, nk)
    else:
        ij = lambda g0, g1: (g0, g1)
        grid = (m // tm, n // tn, nk)
    b_spec = (pl.BlockSpec((tn, tk), lambda g0, g1, kk: (ij(g0, g1)[1], kk)) if b_t
              else pl.BlockSpec((tk, tn), lambda g0, g1, kk: (kk, ij(g0, g1)[1])))
    in_specs = [pl.BlockSpec((tm, tk), lambda g0, g1, kk: (ij(g0, g1)[0], kk)), b_spec]
    args = [a, b]
    if has_acc:
        in_specs.append(pl.BlockSpec((tm, tn), lambda g0, g1, kk: ij(g0, g1)))
        args.append(acc)
    return _pcall(
        body, name=name, grid=grid, in_specs=in_specs,
        out_specs=pl.BlockSpec((tm, tn), lambda g0, g1, kk: ij(g0, g1)),
        out_shape=SDS((m, n), out_dtype), scratch_shapes=([pltpu.VMEM((tm, tn), F32)] if nk > 1 else []),
        compiler_params=_cp(("parallel", "parallel", "arbitrary"), VMEM_BIG),
    )(*args)


def _mm_tn(a, b, *, ta, tn, tk, name, acc=None):
    m, ka = a.shape
    m2, n = b.shape
    assert m == m2 and ka % ta == 0 and n % tn == 0 and m % tk == 0, (a.shape, b.shape, ta, tn, tk)
    nk = m // tk
    has_acc = acc is not None

    def body(*refs):
        if has_acc:
            a_ref, b_ref, c_ref, o_ref = refs
        else:
            a_ref, b_ref, o_ref = refs
        kk = pl.program_id(2)
        part = _tn(a_ref[...].astype(BF16), b_ref[...].astype(BF16))

        @pl.when(kk == 0)
        def _():
            if has_acc:
                o_ref[...] = c_ref[...] + part
            else:
                o_ref[...] = part

        @pl.when(kk > 0)
        def _():
            o_ref[...] += part

    in_specs = [pl.BlockSpec((tk, ta), lambda i, j, kk: (kk, i)), pl.BlockSpec((tk, tn), lambda i, j, kk: (kk, j))]
    args = [a, b]
    if has_acc:
        in_specs.append(pl.BlockSpec((ta, tn), lambda i, j, kk: (i, j)))
        args.append(acc)
    return _pcall(
        body, name=name, grid=(ka // ta, n // tn, nk), in_specs=in_specs,
        out_specs=pl.BlockSpec((ta, tn), lambda i, j, kk: (i, j)), out_shape=SDS((ka, n), F32),
        compiler_params=_cp(("parallel", "parallel", "arbitrary"), VMEM_BIG),
    )(*args)


def _modvec(cc, wm, bm):
    def body(c_ref, w_ref, b_ref, o_ref):
        o_ref[...] = _nn(_silu(c_ref[...]).astype(BF16), w_ref[...]) + b_ref[...]

    return _pcall(body, name="modvec", out_shape=SDS((8, 3 * D), F32), compiler_params=_cp(None, VMEM_BIG))(cc, wm, bm)


def _dcctx(dmodc, wm):
    def body(d_ref, w_ref, o_ref):
        o_ref[...] = _nt(d_ref[...].astype(BF16), w_ref[...])

    return _pcall(
        body, name="dcctx", grid=(1,), in_specs=[_full((8, 2 * D)), pl.BlockSpec((D, 2 * D), lambda i: (0, 0))],
        out_specs=_full((8, D)), out_shape=SDS((8, D), F32), compiler_params=_cp(("arbitrary",), VMEM_BIG),
    )(dmodc, wm)


def _prep_h(x, ng, scale, shift, name):
    m = x.shape[0]

    def body(x_ref, g_ref, sc_ref, sh_ref, h_ref):
        xf = x_ref[...]
        r = lax.rsqrt(jnp.mean(xf * xf, axis=-1, keepdims=True) + EPS)
        y = (xf * r) * g_ref[...]
        h_ref[...] = (y * (1.0 + sc_ref[...]) + sh_ref[...]).astype(BF16)

    tok = min(TOK, m)
    row = pl.BlockSpec((tok, D), lambda i: (i, 0))
    return _pcall(
        body, name=name, grid=(m // tok,), in_specs=[row, _full((1, D)), _full((1, D)), _full((1, D))],
        out_specs=row, out_shape=SDS((m, D), BF16), compiler_params=_cp(("parallel",)),
    )(x, ng, scale, shift)


def _resident(shape):
    return pl.BlockSpec(shape, lambda *_: (0,) * len(shape), pipeline_mode=pl.Buffered(1))


PROJ_TM = 512


def _proj_fwd(x, ng, scale, shift, wit_g, wit_r, wlrt, ln_g, ln_b, share=()):
    m = x.shape[0]
    ns = len(share)
    steps = m // PROJ_TM
    src = [(0, 0), (0, D), (1, 2 * D), (1, 0), (1, D)]

    def body(*refs):
        x_ref, g_ref, sc_ref, sh_ref, wg_ref, wr_ref, wl_ref, lg_ref, lb_ref = refs[:9]
        share_refs = refs[9:9 + ns]
        h_ref, p_ref, plr_ref, vr_ref, vc_ref = refs[9 + ns:14 + ns]
        got_refs = refs[14 + ns:14 + 2 * ns]
        sems = refs[14 + 2 * ns:]

        def copies():
            cx, cy, cc = _coords()
            me = 2 * cx + cy
            peers = [(1 - cx, cy), (cx, 1 - cy), (1 - cx, 1 - cy)]
            out, back = [], []
            for k in range(ns):
                for j, (px, py) in enumerate(peers):
                    out.append(_remote(share_refs[k], got_refs[k].at[me], sems[0].at[3 * k + j], sems[1].at[3 * k + j], (px, py, cc)))
                    landed = got_refs[k].at[2 * px + py]
                    back.append(_remote(landed, landed, sems[0].at[3 * k + j], sems[1].at[3 * k + j], (px, py, cc)))
            return out, back

        if ns:
            @pl.when(pl.program_id(0) == 0)
            def _():
                for rc in copies()[0]:
                    rc.start()

            @pl.when(pl.program_id(0) == steps - 1)
            def _():
                out, back = copies()
                for rc in back:
                    rc.wait_recv()
                for rc in out:
                    rc.wait_send()

        xf = x_ref[...]
        r = lax.rsqrt(jnp.mean(xf * xf, axis=-1, keepdims=True) + EPS)
        y = (xf * r) * g_ref[...]
        h = (y * (1.0 + sc_ref[...]) + sh_ref[...]).astype(BF16)
        h_ref[...] = h
        for j, (which, r0) in enumerate(src):
            w_ref = wr_ref if which else wg_ref
            blk = _nt(h, w_ref[r0:r0 + D, :]).astype(BF16)
            p_ref[:, D * j:D * j + D] = blk
            if j == 4:
                xf = blk[:, 512:1024].astype(F32)
                xc = xf - jnp.mean(xf, axis=-1, keepdims=True)
                vn = (xc * lax.rsqrt(jnp.mean(xc * xc, axis=-1, keepdims=True) + EPS)) * lg_ref[...] + lb_ref[...]
                vr_ref[...] = vn[:, 0:256].astype(BF16)
                vc_ref[0] = vn[:, 256:384].astype(BF16)
                vc_ref[1] = vn[:, 384:512].astype(BF16)
        plr_ref[...] = _nt(h, wl_ref[...])

    row = pl.BlockSpec((PROJ_TM, D), lambda i: (i, 0))
    vec = _full((1, D))
    res = _pcall(
        body, name="proj_fwd", grid=(steps,),
        in_specs=[row, vec, vec, vec, _resident((2 * D, D)), _resident((3 * D, D)), _resident((LRW, D)), _full((1, 512)),
                  _full((1, 512))] + [ANY] * ns,
        out_specs=[row, pl.BlockSpec((PROJ_TM, NP), lambda i: (i, 0)), pl.BlockSpec((PROJ_TM, LRW), lambda i: (i, 0)),
                   pl.BlockSpec((PROJ_TM, 256), lambda i: (i, 0)), pl.BlockSpec((2, PROJ_TM, 128), lambda i: (0, i, 0))] + [ANY] * ns,
        out_shape=[SDS((m, D), BF16), SDS((m, NP), BF16), SDS((m, LRW), F32), SDS((m, 256), BF16), SDS((2, m, 128), BF16)]
        + [SDS((4,) + a.shape, a.dtype) for a in share],
        scratch_shapes=([pltpu.SemaphoreType.DMA((3 * ns,)), pltpu.SemaphoreType.DMA((3 * ns,))] if ns else []),
        compiler_params=_cp(("arbitrary",), VMEM_BIG),
    )(x, ng, scale, shift, wit_g, wit_r, wlrt, ln_g, ln_b, *share)
    return res[0], res[1], res[2], res[3], res[4], list(res[5:])


def _proj_bwd(dp_g, dp_r, dlr, wit_g, wit_r, wlrt, x, dx1, ng, scale, send=(), share8=None):
    m = x.shape[0]
    ns = len(send)
    n8 = 0 if share8 is None else 1
    steps = m // PROJ_TM
    masks = [(mx, my, mc) for mx in range(2) for my in range(2) for mc in range(2)][1:]

    def body(*refs):
        (dpg_ref, dpr_ref, dlr_ref, wg_ref, wr_ref, wl_ref, x_ref, r_ref, g_ref, sc_ref) = refs[:10]
        send_refs = refs[10:10 + ns]
        n_in = 10 + ns + n8
        dx_ref, dg_ref, dsc_ref, dsh_ref = refs[n_in:n_in + 4]
        got_refs = refs[n_in + 4:n_in + 4 + ns]
        sems = refs[n_in + 4 + ns + n8:]
        i = pl.program_id(0)

        def copies():
            cx, cy, cc = _coords()
            me = 2 * cx + cy
            peers = [(1 - cx, cy), (cx, 1 - cy), (1 - cx, 1 - cy)]
            out, back = [], []
            for k in range(ns):
                for j, (px, py) in enumerate(peers):
                    out.append(_remote(send_refs[k].at[2 * px + py], got_refs[k].at[me], sems[0].at[3 * k + j],
                                       sems[1].at[3 * k + j], (px, py, cc)))
                    landed = got_refs[k].at[2 * px + py]
                    back.append(_remote(landed, landed, sems[0].at[3 * k + j], sems[1].at[3 * k + j], (px, py, cc)))
            if n8:
                src8, all8 = refs[10 + ns], refs[n_in + 4 + ns]
                s8, r8 = sems[-2], sems[-1]
                for j, (mx, my, mc) in enumerate(masks):
                    px, py, pc = _flip(cx, mx), _flip(cy, my), _flip(cc, mc)
                    out.append(_remote(src8, all8.at[4 * cx + 2 * cy + cc], s8.at[j], r8.at[j], (px, py, pc)))
                    landed = all8.at[4 * px + 2 * py + pc]
                    back.append(_remote(landed, landed, s8.at[j], r8.at[j], (px, py, pc)))
            return out, back

        @pl.when(i == 0)
        def _():
            dg_ref[...] = jnp.zeros_like(dg_ref)
            dsc_ref[...] = jnp.zeros_like(dsc_ref)
            dsh_ref[...] = jnp.zeros_like(dsh_ref)
            if ns or n8:
                for rc in copies()[0]:
                    rc.start()

        dh_ = (_nn(dpg_ref[...], wg_ref[...]) + _nn(dpr_ref[...], wr_ref[...])
               + _nn(dlr_ref[...].astype(BF16), wl_ref[...]))
        xf = x_ref[...]
        r = lax.rsqrt(jnp.mean(xf * xf, axis=-1, keepdims=True) + EPS)
        xh = xf * r
        y = xh * g_ref[...]
        dsh_ref[...] += jnp.sum(dh_, axis=0, keepdims=True)
        dsc_ref[...] += jnp.sum(dh_ * y, axis=0, keepdims=True)
        dy = dh_ * (1.0 + sc_ref[...])
        dg_ref[...] += jnp.sum(dy * xh, axis=0, keepdims=True)
        dxh = dy * g_ref[...]
        dx_ref[...] = r * (dxh - xh * jnp.mean(dxh * xh, axis=-1, keepdims=True)) + r_ref[...]

        if ns or n8:
            @pl.when(i == steps - 1)
            def _():
                out, back = copies()
                for rc in back:
                    rc.wait_recv()
                for rc in out:
                    rc.wait_send()

    row = pl.BlockSpec((PROJ_TM, D), lambda i: (i, 0))
    vec = _full((1, D))
    kg, kr = dp_g.shape[1], dp_r.shape[1]
    extra_in = list(send) + ([share8] if n8 else [])
    extra_out = [SDS(a.shape, a.dtype) for a in send] + ([SDS((8,) + share8.shape, share8.dtype)] if n8 else [])
    res = _pcall(
        body, name="proj_bwd", grid=(steps,),
        in_specs=[pl.BlockSpec((PROJ_TM, kg), lambda i: (i, 0)), pl.BlockSpec((PROJ_TM, kr), lambda i: (i, 0)),
                  pl.BlockSpec((PROJ_TM, LRW), lambda i: (i, 0)), _resident((kg, D)), _resident((kr, D)), _resident((LRW, D)),
                  row, row, vec, vec] + [ANY] * len(extra_in),
        out_specs=[row, vec, vec, vec] + [ANY] * len(extra_out),
        out_shape=[SDS((m, D), F32), SDS((1, D), F32), SDS((1, D), F32), SDS((1, D), F32)] + extra_out,
        scratch_shapes=(([pltpu.SemaphoreType.DMA((3 * ns,)), pltpu.SemaphoreType.DMA((3 * ns,))] if ns else [])
                        + ([pltpu.SemaphoreType.DMA((7,)), pltpu.SemaphoreType.DMA((7,))] if n8 else [])),
        compiler_params=_cp(("arbitrary",), VMEM_BIG),
    )(dp_g, dp_r, dlr, wit_g, wit_r, wlrt, x, dx1, ng, scale, *extra_in)
    return tuple(res[:4]), list(res[4:4 + ns]), (res[4 + ns] if n8 else None)


def _prep_bwd(x, dh, dx1, ng, scale, name):
    m = x.shape[0]
    has_res = dx1 is not None

    def body(*refs):
        if has_res:
            x_ref, dh_ref, r_ref, g_ref, sc_ref, dx_ref, dg_ref, dsc_ref, dsh_ref = refs
        else:
            x_ref, dh_ref, g_ref, sc_ref, dx_ref, dg_ref, dsc_ref, dsh_ref = refs
        i = pl.program_id(0)

        @pl.when(i == 0)
        def _():
            dg_ref[...] = jnp.zeros_like(dg_ref)
            dsc_ref[...] = jnp.zeros_like(dsc_ref)
            dsh_ref[...] = jnp.zeros_like(dsh_ref)

        xf = x_ref[...]
        dh_ = dh_ref[...]
        r = lax.rsqrt(jnp.mean(xf * xf, axis=-1, keepdims=True) + EPS)
        xh = xf * r
        y = xh * g_ref[...]
        dsh_ref[...] += jnp.sum(dh_, axis=0, keepdims=True)
        dsc_ref[...] += jnp.sum(dh_ * y, axis=0, keepdims=True)
        dy = dh_ * (1.0 + sc_ref[...])
        dg_ref[...] += jnp.sum(dy * xh, axis=0, keepdims=True)
        dxh = dy * g_ref[...]
        dx = r * (dxh - xh * jnp.mean(dxh * xh, axis=-1, keepdims=True))
        if has_res:
            dx = dx + r_ref[...]
        dx_ref[...] = dx

    tok = min(TOK, m)
    row = pl.BlockSpec((tok, D), lambda i: (i, 0))
    vec = _full((1, D))
    in_specs = [row, row] + ([row] if has_res else []) + [vec, vec]
    args = [x, dh] + ([dx1] if has_res else []) + [ng, scale]
    return _pcall(
        body, name=name, grid=(m // tok,), in_specs=in_specs, out_specs=[row, vec, vec, vec],
        out_shape=[SDS((m, D), F32), SDS((1, D), F32), SDS((1, D), F32), SDS((1, D), F32)],
        compiler_params=_cp(("arbitrary",)),
    )(*args)


def _ln_fwd(p, ln_g, ln_b):
    m = p.shape[0]

    def body(va_ref, g_ref, b_ref, vr_ref, vc_ref):
        xf = va_ref[...].astype(F32)
        xc = xf - jnp.mean(xf, axis=-1, keepdims=True)
        y = xc * lax.rsqrt(jnp.mean(xc * xc, axis=-1, keepdims=True) + EPS)
        vn = y * g_ref[...] + b_ref[...]
        vr_ref[...] = vn[:, 0:256].astype(BF16)
        vc_ref[0] = vn[:, 256:384].astype(BF16)
        vc_ref[1] = vn[:, 384:512].astype(BF16)

    return _pcall(
        body, name="ln_fwd", grid=(m // TOK,),
        in_specs=[pl.BlockSpec((TOK, 512), lambda i: (i, 9)), _full((1, 512)), _full((1, 512))],
        out_specs=[pl.BlockSpec((TOK, 256), lambda i: (i, 0)), pl.BlockSpec((2, TOK, 128), lambda i: (0, i, 0))],
        out_shape=[SDS((m, 256), BF16), SDS((2, m, 128), BF16)], compiler_params=_cp(("parallel",)),
    )(p, ln_g, ln_b)


COLB = 2048


def _colmix_fwd(vnc, ws23, bs23):
    rows = vnc.shape[2] // COLB

    def body(v_ref, w_ref, b_ref, o_ref):
        o_ref[0] = _nn(w_ref[0], v_ref[0]) + b_ref[0]

    return _pcall(
        body, name="colmix_fwd", grid=(2, rows),
        in_specs=[pl.BlockSpec((1, AC, COLB), lambda g, j: (g, 0, j)), pl.BlockSpec((1, AC, AC), lambda g, j: (g, 0, 0)),
                  pl.BlockSpec((1, AC, 1), lambda g, j: (g, 0, 0))],
        out_specs=pl.BlockSpec((1, AC, COLB), lambda g, j: (g, 0, j)),
        out_shape=SDS(vnc.shape, F32), compiler_params=_cp(("parallel", "parallel")),
    )(vnc, ws23, bs23)


def _colmix_bwd(dsvc, vnc, ws23t):
    rows = vnc.shape[2] // COLB

    def body(d_ref, v_ref, wt_ref, dv_ref, dw_ref, db_ref):
        j = pl.program_id(1)

        @pl.when(j == 0)
        def _():
            dw_ref[...] = jnp.zeros_like(dw_ref)
            db_ref[...] = jnp.zeros_like(db_ref)

        d = d_ref[0]
        d16 = d.astype(BF16)
        dv_ref[0] = _nn(wt_ref[0], d16)
        dw_ref[0] += _nt(d16, v_ref[0])
        db_ref[0] += jnp.sum(d, axis=1, keepdims=True)

    blk = pl.BlockSpec((1, AC, COLB), lambda g, j: (g, 0, j))
    return _pcall(
        body, name="colmix_bwd", grid=(2, rows),
        in_specs=[blk, blk, pl.BlockSpec((1, AC, AC), lambda g, j: (g, 0, 0))],
        out_specs=[blk, pl.BlockSpec((1, AC, AC), lambda g, j: (g, 0, 0)), pl.BlockSpec((1, AC, 1), lambda g, j: (g, 0, 0))],
        out_shape=[SDS(vnc.shape, F32), SDS((2, AC, AC), F32), SDS((2, AC, 1), F32)],
        compiler_params=_cp(("parallel", "arbitrary")),
    )(dsvc, vnc, ws23t)


def _head_norm(o, gbn):
    out = []
    for h in range(4):
        oh = o[:, 128 * h:128 * h + 128]
        r = lax.rsqrt(jnp.mean(oh * oh, axis=-1, keepdims=True) + EPS)
        out.append((r, oh * r))
    return out


def _mid_fwd(o_f, o_b, p, vnr, svc, ws01, bs01, gbn):
    m = p.shape[0]

    def body(of_ref, ob_ref, zb_ref, ua_ref, za_ref, vnr_ref, svc_ref, w_ref, b_ref, g_ref, ya_ref, yb_ref, svr_ref):
        o = of_ref[...] + ob_ref[...]
        zb = zb_ref[...]
        parts = []
        for h, (r, xh) in enumerate(_head_norm(o, None)):
            parts.append(xh * g_ref[:, 128 * h:128 * h + 128])
        on = jnp.concatenate(parts, axis=1)
        yb_ref[...] = (on * _silu(zb)).astype(BF16)
        for j in range(TOK // AC):
            for g in range(2):
                sv = _nn(w_ref[g], vnr_ref[AC * j:AC * j + AC, AC * g:AC * g + AC]) + b_ref[g]
                svr_ref[AC * j:AC * j + AC, AC * g:AC * g + AC] = sv
        sz = _silu(za_ref[...])
        u = ua_ref[...]
        ya_ref[:, 0:256] = ((u[:, 0:256] * svr_ref[...]) * sz[:, 0:256]).astype(BF16)
        ya_ref[:, 256:384] = ((u[:, 256:384] * svc_ref[0]) * sz[:, 256:384]).astype(BF16)
        ya_ref[:, 384:512] = ((u[:, 384:512] * svc_ref[1]) * sz[:, 384:512]).astype(BF16)

    r512 = pl.BlockSpec((TOK, 512), lambda i: (i, 0))
    return _pcall(
        body, name="mid_fwd", grid=(m // TOK,),
        in_specs=[r512, r512, pl.BlockSpec((TOK, 512), lambda i: (i, 6)), pl.BlockSpec((TOK, 512), lambda i: (i, 7)),
                  pl.BlockSpec((TOK, 512), lambda i: (i, 8)), pl.BlockSpec((TOK, 256), lambda i: (i, 0)),
                  pl.BlockSpec((2, TOK, 128), lambda i: (0, i, 0)), _full((2, AC, AC)), _full((2, AC, 1)), _full((1, 512))],
        out_specs=[r512, r512, pl.BlockSpec((TOK, 256), lambda i: (i, 0))],
        out_shape=[SDS((m, 512), BF16), SDS((m, 512), BF16), SDS((m, 256), F32)],
        compiler_params=_cp(("parallel",)),
    )(o_f, o_b, p, p, p, vnr, svc, ws01, bs01, gbn)


def _merge_fwd(p, ya, yb):
    m = p.shape[0]

    def body(ga_ref, gb_ref, ya_ref, yb_ref, m_ref):
        m_ref[...] = (jax.nn.sigmoid(ga_ref[...]) * ya_ref[...] + jax.nn.sigmoid(gb_ref[...]) * yb_ref[...]).astype(BF16)

    row = pl.BlockSpec((TOK, D), lambda i: (i, 0))
    return _pcall(
        body, name="merge_fwd", grid=(m // TOK,),
        in_specs=[row, pl.BlockSpec((TOK, D), lambda i: (i, 1)), row, row], out_specs=row,
        out_shape=SDS((m, D), BF16), compiler_params=_cp(("parallel",)),
    )(p, p, ya, yb)


def _loss_head(x, out, tgt, gate, gf):
    m = x.shape[0]

    def body(x_ref, o_ref, t_ref, gate_ref, gf_ref, dx1_ref, dout_ref, loss_ref, dgate_ref, dgf_ref):
        i = pl.program_id(0)

        @pl.when(i == 0)
        def _():
            loss_ref[...] = jnp.zeros_like(loss_ref)
            dgate_ref[...] = jnp.zeros_like(dgate_ref)
            dgf_ref[...] = jnp.zeros_like(dgf_ref)

        out_ = o_ref[...]
        x1 = x_ref[...] + gate_ref[...] * out_
        r = lax.rsqrt(jnp.mean(x1 * x1, axis=-1, keepdims=True) + EPS)
        xh = x1 * r
        err = xh * gf_ref[...] - t_ref[...]
        loss_ref[...] += 0.5 * jnp.sum(jnp.mean(err * err, axis=-1, keepdims=True), axis=0, keepdims=True)
        dy = err * (1.0 / D)
        dgf_ref[...] += jnp.sum(dy * xh, axis=0, keepdims=True)
        dxh = dy * gf_ref[...]
        dx1 = r * (dxh - xh * jnp.mean(dxh * xh, axis=-1, keepdims=True))
        dx1_ref[...] = dx1
        dout_ref[...] = (gate_ref[...] * dx1).astype(BF16)
        dgate_ref[...] += jnp.sum(dx1 * out_, axis=0, keepdims=True)

    row = pl.BlockSpec((TOK, D), lambda i: (i, 0))
    vec = _full((1, D))
    return _pcall(
        body, name="loss_head", grid=(m // TOK,), in_specs=[row, row, row, vec, vec],
        out_specs=[row, row, _full((1, 128)), vec, vec],
        out_shape=[SDS((m, D), F32), SDS((m, D), BF16), SDS((1, 128), F32), SDS((1, D), F32), SDS((1, D), F32)],
        compiler_params=_cp(("arbitrary",)),
    )(x, out, tgt, gate, gf)


def _merge_bwd(dm, ya, yb, p):
    m = p.shape[0]

    def body(dm_ref, ya_ref, yb_ref, ga_ref, gb_ref, dya_ref, dyb_ref, dp_ref):
        dm_ = dm_ref[...]
        sa = jax.nn.sigmoid(ga_ref[...])
        sb = jax.nn.sigmoid(gb_ref[...])
        dya_ref[...] = (dm_ * sa).astype(BF16)
        dyb_ref[...] = (dm_ * sb).astype(BF16)
        dp_ref[:, 0:D] = (dm_ * ya_ref[...] * (sa * (1.0 - sa))).astype(BF16)
        dp_ref[:, D:2 * D] = (dm_ * yb_ref[...] * (sb * (1.0 - sb))).astype(BF16)

    row = pl.BlockSpec((TOK, D), lambda i: (i, 0))
    return _pcall(
        body, name="merge_bwd", grid=(m // TOK,),
        in_specs=[row, row, row, row, pl.BlockSpec((TOK, D), lambda i: (i, 1))],
        out_specs=[row, row, pl.BlockSpec((TOK, 2 * D), lambda i: (i, 0))],
        out_shape=[SDS((m, D), BF16), SDS((m, D), BF16), SDS((m, NP), BF16)],
        compiler_params=_cp(("parallel",)),
    )(dm, ya, yb, p, p)


def _mid_bwd(dya_in, dyb_in, p, svr, svc, o_f, o_b, gbn, dp):
    m = p.shape[0]

    def body(dya_ref, dyb_ref, zb_ref, ua_ref, za_ref, svr_ref, svc_ref, of_ref, ob_ref, g_ref, dpi_ref,
             dp_ref, dsr_ref, dsc_ref, do_ref, dg_ref):
        i = pl.program_id(0)

        @pl.when(i == 0)
        def _():
            dg_ref[...] = jnp.zeros_like(dg_ref)

        dya = dya_ref[...]
        u = ua_ref[...]
        za = za_ref[...]
        sz = _silu(za)
        sv = jnp.concatenate([svr_ref[...], svc_ref[0], svc_ref[1]], axis=1)
        dp_ref[:, 512:1024] = (dya * sv * sz).astype(BF16)
        dsv = dya * u * sz
        dsr_ref[...] = dsv[:, 0:256]
        dsc_ref[0] = dsv[:, 256:384]
        dsc_ref[1] = dsv[:, 384:512]
        dp_ref[:, 1024:1536] = (dya * u * sv * _dsilu(za)).astype(BF16)

        dyb = dyb_ref[...]
        zb = zb_ref[...]
        o = of_ref[...] + ob_ref[...]
        szb = _silu(zb)
        dszb = _dsilu(zb)
        for h, (r, xh) in enumerate(_head_norm(o, None)):
            sl = slice(128 * h, 128 * h + 128)
            gh = g_ref[:, sl]
            don = dyb[:, sl] * szb[:, sl]
            dp_ref[:, sl] = (dyb[:, sl] * (xh * gh) * dszb[:, sl]).astype(BF16)
            dg_ref[:, sl] += jnp.sum(don * xh, axis=0, keepdims=True)
            dxh = don * gh
            do_ref[:, sl] = r * (dxh - xh * jnp.mean(dxh * xh, axis=-1, keepdims=True))

    r512 = pl.BlockSpec((TOK, 512), lambda i: (i, 0))
    return _pcall(
        body, name="mid_bwd", grid=(m // TOK,),
        in_specs=[r512, r512, pl.BlockSpec((TOK, 512), lambda i: (i, 6)), pl.BlockSpec((TOK, 512), lambda i: (i, 7)),
                  pl.BlockSpec((TOK, 512), lambda i: (i, 8)), pl.BlockSpec((TOK, 256), lambda i: (i, 0)),
                  pl.BlockSpec((2, TOK, 128), lambda i: (0, i, 0)), r512, r512, _full((1, 512)),
                  pl.BlockSpec(memory_space=pl.ANY)],
        out_specs=[pl.BlockSpec((TOK, 1536), lambda i: (i, 2)), pl.BlockSpec((TOK, 256), lambda i: (i, 0)),
                   pl.BlockSpec((2, TOK, 128), lambda i: (0, i, 0)), r512, _full((1, 512))],
        out_shape=[SDS((m, NP), BF16), SDS((m, 256), F32), SDS((2, m, 128), F32), SDS((m, 512), F32), SDS((1, 512), F32)],
        input_output_aliases={10: 0}, compiler_params=_cp(("arbitrary",)),
    )(dya_in, dyb_in, p, p, p, svr, svc, o_f, o_b, gbn, dp)


def _tail_fwd(o_f, o_b, p, vnr, svc, x, tgt, ws01, bs01, gbn, wpa, wpb, wo, gate, gf):
    m = p.shape[0]

    def body(of_ref, ob_ref, zb_ref, ua_ref, za_ref, ga_ref, gb_ref, vnr_ref, svc_ref, x_ref, t_ref, w_ref, b_ref, g_ref,
             wpa_ref, wpb_ref, wo_ref, gate_ref, gf_ref,
             ya_ref, yb_ref, svr_ref, dwo_ref, dx1_ref, dout_ref, loss_ref, dgate_ref, dgf_ref):
        i = pl.program_id(0)

        @pl.when(i == 0)
        def _():
            loss_ref[...] = jnp.zeros_like(loss_ref)
            dgate_ref[...] = jnp.zeros_like(dgate_ref)
            dgf_ref[...] = jnp.zeros_like(dgf_ref)
            dwo_ref[...] = jnp.zeros_like(dwo_ref)

        o = of_ref[...] + ob_ref[...]
        zb = zb_ref[...].astype(F32)
        for h, (r, xh) in enumerate(_head_norm(o, None)):
            sl = slice(128 * h, 128 * h + 128)
            yb_ref[:, sl] = ((xh * g_ref[:, sl]) * _silu(zb[:, sl])).astype(BF16)
        for j in range(TOK // AC):
            for g in range(2):
                sv = _nn(w_ref[g], vnr_ref[AC * j:AC * j + AC, AC * g:AC * g + AC]) + b_ref[g]
                svr_ref[AC * j:AC * j + AC, AC * g:AC * g + AC] = sv
        sz = _silu(za_ref[...].astype(F32))
        u = ua_ref[...].astype(F32)
        ya_ref[:, 0:256] = ((u[:, 0:256] * svr_ref[...]) * sz[:, 0:256]).astype(BF16)
        ya_ref[:, 256:384] = ((u[:, 256:384] * svc_ref[0]) * sz[:, 256:384]).astype(BF16)
        ya_ref[:, 384:512] = ((u[:, 384:512] * svc_ref[1]) * sz[:, 384:512]).astype(BF16)
        ya = _nn(ya_ref[...], wpa_ref[...])
        yb = _nn(yb_ref[...], wpb_ref[...])
        mg = (jax.nn.sigmoid(ga_ref[...].astype(F32)) * ya + jax.nn.sigmoid(gb_ref[...].astype(F32)) * yb).astype(BF16)
        out_ = _nn(mg, wo_ref[...])
        x1 = x_ref[...] + gate_ref[...] * out_
        r = lax.rsqrt(jnp.mean(x1 * x1, axis=-1, keepdims=True) + EPS)
        xh = x1 * r
        err = xh * gf_ref[...] - t_ref[...]
        loss_ref[...] += 0.5 * jnp.sum(jnp.mean(err * err, axis=-1, keepdims=True), axis=0, keepdims=True)
        dy = err * (1.0 / D)
        dgf_ref[...] += jnp.sum(dy * xh, axis=0, keepdims=True)
        dxh = dy * gf_ref[...]
        dx1 = r * (dxh - xh * jnp.mean(dxh * xh, axis=-1, keepdims=True))
        dx1_ref[...] = dx1
        dout16 = (gate_ref[...] * dx1).astype(BF16)
        dout_ref[...] = dout16
        dgate_ref[...] += jnp.sum(dx1 * out_, axis=0, keepdims=True)
        dwo_ref[...] += _tn(mg, dout16)

    r512 = pl.BlockSpec((TOK, 512), lambda i: (i, 0))
    row = pl.BlockSpec((TOK, D), lambda i: (i, 0))
    vec = _full((1, D))
    return _pcall(
        body, name="tail_fwd", grid=(m // TOK,),
        in_specs=[r512, r512, pl.BlockSpec((TOK, 512), lambda i: (i, 6)), pl.BlockSpec((TOK, 512), lambda i: (i, 7)),
                  pl.BlockSpec((TOK, 512), lambda i: (i, 8)), row, pl.BlockSpec((TOK, D), lambda i: (i, 1)),
                  pl.BlockSpec((TOK, 256), lambda i: (i, 0)), pl.BlockSpec((2, TOK, 128), lambda i: (0, i, 0)), row, row,
                  _full((2, AC, AC)), _full((2, AC, 1)), _full((1, 512)), _resident((512, D)), _resident((512, D)),
                  _resident((D, D)), vec, vec],
        out_specs=[r512, r512, pl.BlockSpec((TOK, 256), lambda i: (i, 0)), _resident((D, D)), row, row, _full((1, 128)), vec, vec],
        out_shape=[SDS((m, 512), BF16), SDS((m, 512), BF16), SDS((m, 256), F32), SDS((D, D), F32), SDS((m, D), F32),
                   SDS((m, D), BF16), SDS((1, 128), F32), SDS((1, D), F32), SDS((1, D), F32)],
        compiler_params=_cp(("arbitrary",), VMEM_BIG),
    )(o_f, o_b, p, p, p, p, p, vnr, svc, x, tgt, ws01, bs01, gbn, wpa, wpb, wo, gate, gf)


DPR = 3072


def _tail_bwd(dout, ya_in, yb_in, p, svr, svc, o_f, o_b, gbn, wo, wpa, wpb):
    m = p.shape[0]

    def body(dout_ref, ya_ref, yb_ref, ga_ref, gb_ref, zb_ref, ua_ref, za_ref, svr_ref, svc_ref, of_ref, ob_ref, g_ref,
             wo_ref, wpa_ref, wpb_ref,
             dwpa_ref, dwpb_ref, dpg_ref, dpr_ref, dsr_ref, dsc_ref, do_ref, dg_ref):
        i = pl.program_id(0)

        @pl.when(i == 0)
        def _():
            dg_ref[...] = jnp.zeros_like(dg_ref)
            dwpa_ref[...] = jnp.zeros_like(dwpa_ref)
            dwpb_ref[...] = jnp.zeros_like(dwpb_ref)

        dm_ = _nt(dout_ref[...], wo_ref[...])
        ya_in, yb_in = ya_ref[...], yb_ref[...]
        ya = _nn(ya_in, wpa_ref[...])
        yb = _nn(yb_in, wpb_ref[...])
        sa = jax.nn.sigmoid(ga_ref[...].astype(F32))
        sb = jax.nn.sigmoid(gb_ref[...].astype(F32))
        dya16 = (dm_ * sa).astype(BF16)
        dyb16 = (dm_ * sb).astype(BF16)
        dwpa_ref[...] += _tn(ya_in, dya16)
        dwpb_ref[...] += _tn(yb_in, dyb16)
        dpg_ref[:, 0:D] = (dm_ * ya * (sa * (1.0 - sa))).astype(BF16)
        dpg_ref[:, D:2 * D] = (dm_ * yb * (sb * (1.0 - sb))).astype(BF16)
        dya = _nt(dya16, wpa_ref[...])
        dyb = _nt(dyb16, wpb_ref[...])

        u = ua_ref[...].astype(F32)
        za = za_ref[...].astype(F32)
        sz, dsz = _silu_and_grad(za)
        sv = jnp.concatenate([svr_ref[...], svc_ref[0], svc_ref[1]], axis=1)
        dpr_ref[:, 512:1024] = (dya * sv * sz).astype(BF16)
        dsv = dya * u * sz
        dsr_ref[...] = dsv[:, 0:256]
        dsc_ref[0] = dsv[:, 256:384]
        dsc_ref[1] = dsv[:, 384:512]
        dpr_ref[:, 1024:1536] = (dya * u * sv * dsz).astype(BF16)

        zb = zb_ref[...].astype(F32)
        o = of_ref[...] + ob_ref[...]
        szb, dszb = _silu_and_grad(zb)
        for h, (r, xh) in enumerate(_head_norm(o, None)):
            sl = slice(128 * h, 128 * h + 128)
            gh = g_ref[:, sl]
            don = dyb[:, sl] * szb[:, sl]
            dpr_ref[:, sl] = (dyb[:, sl] * (xh * gh) * dszb[:, sl]).astype(BF16)
            dg_ref[:, sl] += jnp.sum(don * xh, axis=0, keepdims=True)
            dxh = don * gh
            do_ref[:, sl] = r * (dxh - xh * jnp.mean(dxh * xh, axis=-1, keepdims=True))

    r512 = pl.BlockSpec((TOK, 512), lambda i: (i, 0))
    row = pl.BlockSpec((TOK, D), lambda i: (i, 0))
    return _pcall(
        body, name="tail_bwd", grid=(m // TOK,),
        in_specs=[row, r512, r512, row, pl.BlockSpec((TOK, D), lambda i: (i, 1)), pl.BlockSpec((TOK, 512), lambda i: (i, 6)),
                  pl.BlockSpec((TOK, 512), lambda i: (i, 7)), pl.BlockSpec((TOK, 512), lambda i: (i, 8)),
                  pl.BlockSpec((TOK, 256), lambda i: (i, 0)), pl.BlockSpec((2, TOK, 128), lambda i: (0, i, 0)), r512, r512,
                  _full((1, 512)), _resident((D, D)), _resident((512, D)), _resident((512, D))],
        out_specs=[_resident((512, D)), _resident((512, D)), pl.BlockSpec((TOK, 2 * D), lambda i: (i, 0)),
                   pl.BlockSpec((TOK, 1536), lambda i: (i, 0)),
                   pl.BlockSpec((TOK, 256), lambda i: (i, 0)), pl.BlockSpec((2, TOK, 128), lambda i: (0, i, 0)), r512, _full((1, 512))],
        out_shape=[SDS((512, D), F32), SDS((512, D), F32), SDS((m, 2 * D), BF16), SDS((m, DPR), BF16), SDS((m, 256), F32),
                   SDS((2, m, 128), F32), SDS((m, 512), F32), SDS((1, 512), F32)],
        compiler_params=_cp(("arbitrary",), VMEM_BIG),
    )(dout, ya_in, yb_in, p, p, p, p, p, svr, svc, o_f, o_b, gbn, wo, wpa, wpb)


def _mm_multi(pairs, *, tm, tn, out_dtype, name):
    m = pairs[0][0].shape[0]
    n = pairs[0][1].shape[1]
    nks = [a.shape[1] // tk for a, _, tk in pairs]
    starts = [sum(nks[:i]) for i in range(len(pairs))]
    total = sum(nks)

    def body(*refs):
        o_ref, acc_ref = refs[-2], refs[-1]
        kk = pl.program_id(2)
        for idx in range(len(pairs)):
            a_ref, b_ref = refs[2 * idx], refs[2 * idx + 1]

            @pl.when((kk >= starts[idx]) & (kk < starts[idx] + nks[idx]))
            def _(a_ref=a_ref, b_ref=b_ref, first=(idx == 0)):
                part = _nn(a_ref[...].astype(BF16), b_ref[...].astype(BF16))
                if first:
                    @pl.when(kk == 0)
                    def _():
                        acc_ref[...] = part

                    @pl.when(kk > 0)
                    def _():
                        acc_ref[...] += part
                else:
                    acc_ref[...] += part

        @pl.when(kk == total - 1)
        def _():
            o_ref[...] = acc_ref[...].astype(out_dtype)

    in_specs, args = [], []
    for (a, b, tk), st, nk in zip(pairs, starts, nks):
        in_specs.append(pl.BlockSpec((tm, tk), lambda i, j, kk, st=st, nk=nk: (i, jnp.clip(kk - st, 0, nk - 1))))
        in_specs.append(pl.BlockSpec((tk, tn), lambda i, j, kk, st=st, nk=nk: (jnp.clip(kk - st, 0, nk - 1), j)))
        args += [a, b]
    return _pcall(
        body, name=name, grid=(m // tm, n // tn, total), in_specs=in_specs,
        out_specs=pl.BlockSpec((tm, tn), lambda i, j, kk: (i, j)), out_shape=SDS((m, n), out_dtype),
        scratch_shapes=[pltpu.VMEM((tm, tn), F32)], compiler_params=_cp(("parallel", "parallel", "arbitrary"), VMEM_BIG),
    )(*args)


def _ln_bwd(dsr, vnr, dvnc, p, ws01t, ln_g, dp):
    m = p.shape[0]

    def body(dsr_ref, vnr_ref, dvc_ref, va_ref, wt_ref, g_ref, dpi_ref, dp_ref, dw_ref, db_ref, dlg_ref, dlb_ref, dvn_ref):
        i = pl.program_id(0)

        @pl.when(i == 0)
        def _():
            dw_ref[...] = jnp.zeros_like(dw_ref)
            db_ref[...] = jnp.zeros_like(db_ref)
            dlg_ref[...] = jnp.zeros_like(dlg_ref)
            dlb_ref[...] = jnp.zeros_like(dlb_ref)

        for j in range(TOK // AC):
            for g in range(2):
                d = dsr_ref[AC * j:AC * j + AC, AC * g:AC * g + AC]
                d16 = d.astype(BF16)
                dvn_ref[AC * j:AC * j + AC, AC * g:AC * g + AC] = _nn(wt_ref[g], d16)
                dw_ref[g] += _nt(d16, vnr_ref[AC * j:AC * j + AC, AC * g:AC * g + AC])
                db_ref[g] += jnp.sum(d, axis=1, keepdims=True)
        dvn_ref[:, 256:384] = dvc_ref[0]
        dvn_ref[:, 384:512] = dvc_ref[1]
        dvn = dvn_ref[...]
        xf = va_ref[...].astype(F32)
        xc = xf - jnp.mean(xf, axis=-1, keepdims=True)
        rs = lax.rsqrt(jnp.mean(xc * xc, axis=-1, keepdims=True) + EPS)
        xh = xc * rs
        dlg_ref[...] += jnp.sum(dvn * xh, axis=0, keepdims=True)
        dlb_ref[...] += jnp.sum(dvn, axis=0, keepdims=True)
        dxh = dvn * g_ref[...]
        dva = rs * (dxh - jnp.mean(dxh, axis=-1, keepdims=True) - xh * jnp.mean(dxh * xh, axis=-1, keepdims=True))
        dp_ref[...] = dva.astype(BF16)

    return _pcall(
        body, name="ln_bwd", grid=(m // TOK,),
        in_specs=[pl.BlockSpec((TOK, 256), lambda i: (i, 0)), pl.BlockSpec((TOK, 256), lambda i: (i, 0)),
                  pl.BlockSpec((2, TOK, 128), lambda i: (0, i, 0)), pl.BlockSpec((TOK, 512), lambda i: (i, 9)),
                  _full((2, AC, AC)), _full((1, 512)), pl.BlockSpec(memory_space=pl.ANY)],
        out_specs=[pl.BlockSpec((TOK, 512), lambda i: (i, 3)), _full((2, AC, AC)), _full((2, AC, 1)), _full((1, 512)), _full((1, 512))],
        out_shape=[SDS((m, DPR), BF16), SDS((2, AC, AC), F32), SDS((2, AC, 1), F32), SDS((1, 512), F32), SDS((1, 512), F32)],
        scratch_shapes=[pltpu.VMEM((TOK, 512), F32)],
        input_output_aliases={6: 0}, compiler_params=_cp(("arbitrary",)),
    )(dsr, vnr, dvnc, p, ws01t, ln_g, dp)


def _tri_mm(tri, a):
    a1 = a.astype(BF16)
    r1 = a - a1.astype(F32)
    a2 = r1.astype(BF16)
    a3 = (r1 - a2.astype(F32)).astype(BF16)
    n = a.shape[1]
    r = _nn(tri, jnp.concatenate([a1, a2, a3], axis=1))
    return r[:, 0:n] + r[:, n:2 * n] + r[:, 2 * n:3 * n]


def _gla_masks(reverse):
    ri = lax.broadcasted_iota(jnp.int32, (CH, CH), 0)
    ci = lax.broadcasted_iota(jnp.int32, (CH, CH), 1)
    vis = (ci >= ri) if reverse else (ci <= ri)
    vis_t = (ci <= ri) if reverse else (ci >= ri)
    r4 = lax.broadcasted_iota(jnp.int32, (4 * CH, CH), 0) & (CH - 1)
    c4 = lax.broadcasted_iota(jnp.int32, (4 * CH, CH), 1)
    vis4 = (c4 >= r4) if reverse else (c4 <= r4)
    vis4_t = (c4 <= r4) if reverse else (c4 >= r4)
    lane = lax.broadcasted_iota(jnp.int32, (1, 256), 1)
    hm = [(lane >= CH * h) & (lane < CH * h + CH) for h in range(4)]
    return vis, vis_t, vis4, vis4_t, hm


def _stack_heads(x, hm):
    return jnp.concatenate([jnp.where(hm[h], x, 0.0).astype(BF16) for h in range(4)], axis=0)


def _diag_heads(full, hm):
    r = full.shape[0] // 4
    acc = jnp.where(hm[0], full[0:r], 0.0)
    for h in range(1, 4):
        acc = acc + jnp.where(hm[h], full[r * h:r * h + r], 0.0)
    return acc


def _rows_of_heads(x):
    return jnp.concatenate([x[:, 128 * h:128 * h + 128] for h in range(4)], axis=0)


def _lane_vis(reverse, transpose):
    ri = lax.broadcasted_iota(jnp.int32, (CH, 4 * CH), 0)
    ci = lax.broadcasted_iota(jnp.int32, (CH, 4 * CH), 1) & (CH - 1)
    return (ci >= ri) if (reverse != transpose) else (ci <= ri)


def _gla_fwd(p, qkv_blk, lr, lrw, gbias, s0, *, reverse, name):
    m = p.shape[0]
    tb = min(GLA_TB, m)
    nb = m // tb
    nc = tb // CH
    rmap = (lambda i: nb - 1 - i) if reverse else (lambda i: i)

    def body(qkv_ref, lr_ref, lrw_ref, gb_ref, s0_ref, o_ref, sb_ref, sfin_ref, st_ref):
        i = pl.program_id(0)

        @pl.when(i == 0)
        def _():
            st_ref[...] = s0_ref[...]

        vis, _, vis4, _, hm = _gla_masks(reverse)
        tri = vis.astype(F32).astype(BF16)
        logits = _nn(lr_ref[...].astype(BF16), lrw_ref[...]) + gb_ref[...]
        a_all = _logsig(logits) * (1.0 / 16.0)
        st = st_ref[...]
        for c in (range(nc - 1, -1, -1) if reverse else range(nc)):
            rows = slice(CH * c, CH * c + CH)
            b = _tri_mm(tri, a_all[rows])
            bl = b[0:1] if reverse else b[CH - 1:CH]
            q = qkv_ref[rows, 0:256].astype(F32) * 0.125
            k = qkv_ref[rows, 256:512].astype(F32)
            v16 = qkv_ref[rows, 512:1024].astype(BF16)
            qd = q * jnp.exp(b)
            kd16 = (k * jnp.exp(-b)).astype(BF16)
            kdec16 = (k * jnp.exp(bl - b)).astype(BF16)
            qstack = _stack_heads(qd, hm)
            sc = jnp.where(vis4, _nt(qstack, kd16), 0.0).astype(BF16)
            inter = _nt(qstack, st.astype(BF16))
            for h in range(4):
                o_ref[rows, 128 * h:128 * h + 128] = (
                    _nn(sc[CH * h:CH * h + CH], v16[:, 128 * h:128 * h + 128]) + inter[CH * h:CH * h + CH])
            sb_ref[c] = st
            st = st * jnp.exp(bl) + _diag_heads(_tn(v16, kdec16), hm)
        st_ref[...] = st

        @pl.when(i == nb - 1)
        def _():
            sfin_ref[...] = st

    return _pcall(
        body, name=name, grid=(nb,),
        in_specs=[pl.BlockSpec((tb,1024), lambda i: (rmap(i), qkv_blk)), pl.BlockSpec((tb,LRW), lambda i: (rmap(i), 0)),
                  _full((LRW, 256)), _full((1, 256)), _full((128, 256))],
        out_specs=[pl.BlockSpec((tb,512), lambda i: (rmap(i), 0)), pl.BlockSpec((nc, 128, 256), lambda i: (rmap(i), 0, 0)),
                   _full((128, 256))],
        out_shape=[SDS((m, 512), F32), SDS((m // CH, 128, 256), F32), SDS((128, 256), F32)],
        scratch_shapes=[pltpu.VMEM((128, 256), F32)], compiler_params=_cp(("arbitrary",)),
    )(p, lr, lrw, gbias, s0)


def _gla_fwd2(p, qkv_blk, lr, lrws, gbiases, s0s, name):
    m = p.shape[0]
    tb = min(GLA_TB, m)
    nb = m // tb
    nc = tb // CH

    def body(qkv_f, lr_f, qkv_b, lr_b, lrw_f, lrw_b, gb_f, gb_b, s0_f, s0_b,
             o_f, sb_f, sfin_f, o_b, sb_b, sfin_b, st_f, st_b):
        i = pl.program_id(0)

        @pl.when(i == 0)
        def _():
            st_f[...] = s0_f[...]
            st_b[...] = s0_b[...]

        dirs = []
        for reverse, qkv_ref, lr_ref, lrw_ref, gb_ref, o_ref, sb_ref, st_ref in (
                (False, qkv_f, lr_f, lrw_f, gb_f, o_f, sb_f, st_f), (True, qkv_b, lr_b, lrw_b, gb_b, o_b, sb_b, st_b)):
            vis, _, vis4, _, hm = _gla_masks(reverse)
            logits = _nn(lr_ref[...].astype(BF16), lrw_ref[...]) + gb_ref[...]
            dirs.append(dict(reverse=reverse, qkv=qkv_ref, o=o_ref, sb=sb_ref, vis4=vis4, hm=hm,
                             tri=vis.astype(F32).astype(BF16), a=_logsig(logits) * (1.0 / 16.0), st=st_ref[...]))
        for step in range(nc):
            for d in dirs:
                c = nc - 1 - step if d["reverse"] else step
                rows = slice(CH * c, CH * c + CH)
                b = _tri_mm(d["tri"], d["a"][rows])
                bl = b[0:1] if d["reverse"] else b[CH - 1:CH]
                q = d["qkv"][rows, 0:256].astype(F32) * 0.125
                k = d["qkv"][rows, 256:512].astype(F32)
                v16 = d["qkv"][rows, 512:1024].astype(BF16)
                qd = q * jnp.exp(b)
                kd16 = (k * jnp.exp(-b)).astype(BF16)
                kdec16 = (k * jnp.exp(bl - b)).astype(BF16)
                qstack = _stack_heads(qd, d["hm"])
                sc = jnp.where(d["vis4"], _nt(qstack, kd16), 0.0).astype(BF16)
                inter = _nt(qstack, d["st"].astype(BF16))
                for h in range(4):
                    d["o"][rows, 128 * h:128 * h + 128] = (
                        _nn(sc[CH * h:CH * h + CH], v16[:, 128 * h:128 * h + 128]) + inter[CH * h:CH * h + CH])
                d["sb"][c] = d["st"]
                d["st"] = d["st"] * jnp.exp(bl) + _diag_heads(_tn(v16, kdec16), d["hm"])
        st_f[...] = dirs[0]["st"]
        st_b[...] = dirs[1]["st"]

        @pl.when(i == nb - 1)
        def _():
            sfin_f[...] = dirs[0]["st"]
            sfin_b[...] = dirs[1]["st"]

    fw = lambda i: i
    bw = lambda i: nb - 1 - i
    in_specs = []
    for rm in (fw, bw):
        in_specs += [pl.BlockSpec((tb, 1024), lambda i, rm=rm: (rm(i), qkv_blk)), pl.BlockSpec((tb, LRW), lambda i, rm=rm: (rm(i), 0))]
    in_specs += [_full((LRW, 256))] * 2 + [_full((1, 256))] * 2 + [_full((128, 256))] * 2
    out_specs, out_shape = [], []
    for rm in (fw, bw):
        out_specs += [pl.BlockSpec((tb, 512), lambda i, rm=rm: (rm(i), 0)), pl.BlockSpec((nc, 128, 256), lambda i, rm=rm: (rm(i), 0, 0)),
                      _full((128, 256))]
        out_shape += [SDS((m, 512), F32), SDS((m // CH, 128, 256), F32), SDS((128, 256), F32)]
    return _pcall(
        body, name=name, grid=(nb,), in_specs=in_specs, out_specs=out_specs, out_shape=out_shape,
        scratch_shapes=[pltpu.VMEM((128, 256), F32), pltpu.VMEM((128, 256), F32)], compiler_params=_cp(("arbitrary",), VMEM_BIG),
    )(p, lr, p, lr, lrws[0], lrws[1], gbiases[0], gbiases[1], s0s[0], s0s[1])


def _gla_bwd(p, qkv_blk, lr, lrw, lrwt, gbias, sb, dsfin, do, prev, dp, *, reverse, name):
    m = p.shape[0]
    tb = min(GLA_TB, m)
    nb = m // tb
    nc = tb // CH
    rmap = (lambda i: i) if reverse else (lambda i: nb - 1 - i)
    has_prev = prev is not None
    has_dp = dp is not None

    def body(*refs):
        refs = list(refs)
        qkv_ref, lr_ref, lrw_ref, lrwt_ref, gb_ref, sb_ref, dsfin_ref, do_ref = refs[:8]
        refs = refs[8:]
        if has_prev:
            pq_ref, plr_ref = refs[:2]
            refs = refs[2:]
        if has_dp:
            refs = refs[1:]
        dqkv_ref, dlr_ref, dw2_ref, dgb_ref, ds0_ref, dst_ref, dlog_ref = refs
        i = pl.program_id(0)

        @pl.when(i == 0)
        def _():
            dst_ref[...] = dsfin_ref[...]
            dw2_ref[...] = jnp.zeros_like(dw2_ref)
            dgb_ref[...] = jnp.zeros_like(dgb_ref)

        vis, vis_t, vis4, vis4_t, hm = _gla_masks(reverse)
        tri = vis.astype(F32).astype(BF16)
        tri_t = vis_t.astype(F32).astype(BF16)
        lane_vis = _lane_vis(reverse, False)
        lane_vis_t = _lane_vis(reverse, True)
        lr16 = lr_ref[...].astype(BF16)
        logits = _nn(lr16, lrw_ref[...]) + gb_ref[...]
        a_all = _logsig(logits) * (1.0 / 16.0)
        dsig = (1.0 - jax.nn.sigmoid(logits)) * (1.0 / 16.0)
        dst = dst_ref[...]
        for c in (range(nc) if reverse else range(nc - 1, -1, -1)):
            rows = slice(CH * c, CH * c + CH)
            b = _tri_mm(tri, a_all[rows])
            bl = b[0:1] if reverse else b[CH - 1:CH]
            eb = jnp.exp(b)
            enb = jnp.exp(-b)
            ebl = jnp.exp(bl - b)
            el = jnp.exp(bl)
            q = qkv_ref[rows, 0:256].astype(F32) * 0.125
            k = qkv_ref[rows, 256:512].astype(F32)
            v16 = qkv_ref[rows, 512:1024].astype(BF16)
            do16 = do_ref[rows, :].astype(BF16)
            qd = q * eb
            kd = k * enb
            kdec = k * ebl
            st = sb_ref[c]
            st16 = st.astype(BF16)
            dst16 = dst.astype(BF16)
            qd16 = qd.astype(BF16)
            kd16 = kd.astype(BF16)
            qstack = _stack_heads(qd, hm)
            kstack = _stack_heads(kd, hm)
            kdecstack = _stack_heads(kdec, hm)
            pt = jnp.where(vis4_t, _nt(kstack, qd16), 0.0).astype(BF16)
            dvinter = _nt(kdecstack, dst16)
            do_rows = _rows_of_heads(do16)
            v_rows = _rows_of_heads(v16)
            dp_cat = jnp.where(lane_vis, _diag_heads(_nt(do_rows, v_rows), hm), 0.0).astype(BF16)
            dpt_cat = jnp.where(lane_vis_t, _diag_heads(_nt(v_rows, do_rows), hm), 0.0).astype(BF16)
            dqd = _nn(dp_cat, kstack) + _diag_heads(_nn(do_rows, st16), hm)
            dkd = _nn(dpt_cat, qstack)
            dkdec = _diag_heads(_nn(v_rows, dst16), hm)
            for h in range(4):
                rh = slice(CH * h, CH * h + CH)
                dv_h = _nn(pt[rh], do_rows[rh]) + dvinter[rh]
                if has_prev:
                    dv_h = dv_h + pq_ref[rows, 512 + 128 * h:512 + 128 * h + 128]
                dqkv_ref[rows, 512 + 128 * h:512 + 128 * h + 128] = dv_h.astype(dqkv_ref.dtype)
            dq = dqd * eb * 0.125
            dk = dkd * enb + dkdec * ebl
            if has_prev:
                dq = dq + pq_ref[rows, 0:256]
                dk = dk + pq_ref[rows, 256:512]
            dqkv_ref[rows, 0:256] = dq.astype(dqkv_ref.dtype)
            dqkv_ref[rows, 256:512] = dk.astype(dqkv_ref.dtype)
            g_kdec = dkdec * kdec
            db = dqd * qd - dkd * kd - g_kdec
            dbl = jnp.sum(g_kdec, axis=0, keepdims=True) + jnp.sum(st * dst, axis=0, keepdims=True) * el
            da = _tri_mm(tri_t, db) + dbl
            dlog_ref[rows, :] = da * dsig[rows]
            dst = dst * el + _diag_heads(_tn(do16, qd16), hm)
        dst_ref[...] = dst
        dlog = dlog_ref[...]
        dlog16 = dlog.astype(BF16)
        dlr = _nn(dlog16, lrwt_ref[...])
        if has_prev:
            dlr = dlr + plr_ref[...]
        dlr_ref[...] = dlr
        dw2_ref[...] += _tn(lr16, dlog16)
        dgb_ref[...] += jnp.sum(dlog, axis=0, keepdims=True)

        @pl.when(i == nb - 1)
        def _():
            ds0_ref[...] = dst

    in_specs = [pl.BlockSpec((tb,1024), lambda i: (rmap(i), qkv_blk)), pl.BlockSpec((tb,LRW), lambda i: (rmap(i), 0)),
                _full((LRW, 256)), _full((256, LRW)), _full((1, 256)), pl.BlockSpec((nc, 128, 256), lambda i: (rmap(i), 0, 0)),
                _full((128, 256)), pl.BlockSpec((tb,512), lambda i: (rmap(i), 0))]
    args = [p, lr, lrw, lrwt, gbias, sb, dsfin, do]
    if has_prev:
        in_specs += [pl.BlockSpec((tb,1024), lambda i: (rmap(i), 0)), pl.BlockSpec((tb,LRW), lambda i: (rmap(i), 0))]
        args += list(prev)
    aliases = {}
    if has_dp:
        in_specs.append(pl.BlockSpec(memory_space=pl.ANY))
        aliases = {len(args): 0}
        args.append(dp)
        dq_spec = pl.BlockSpec((tb,1024), lambda i: (rmap(i), 2))
        dq_shape = SDS(dp.shape, dp.dtype)
    else:
        dq_spec = pl.BlockSpec((tb,1024), lambda i: (rmap(i), 0))
        dq_shape = SDS((m, 1024), F32)
    return _pcall(
        body, name=name, grid=(nb,), in_specs=in_specs,
        out_specs=[dq_spec, pl.BlockSpec((tb,LRW), lambda i: (rmap(i), 0)), _full((LRW, 256)), _full((1, 256)), _full((128, 256))],
        out_shape=[dq_shape, SDS((m, LRW), F32), SDS((LRW, 256), F32), SDS((1, 256), F32), SDS((128, 256), F32)],
        scratch_shapes=[pltpu.VMEM((128, 256), F32), pltpu.VMEM((tb,256), F32)],
        input_output_aliases=aliases, compiler_params=_cp(("arbitrary",)),
    )(*args)


EARLY_KEYS = ["dmodc", "dscc", "dng_c", "dlng", "dlnb", "dws", "dbs", "dgbn", "dgf", "dw2", "dgb2", "loss", "dgate"]
EARLY_SHAPES = [(1, 2 * D), (D,), (1, D), (1, 512), (1, 512), (1, 4, 128, 128), (1, 4, 128), (1, 512), (D,), (2, 16, 256), (2, 256),
                (128,), (1, D)]
EARLY_SIZE = 2 * D + D + D + 512 + 512 + 4 * 128 * 128 + 512 + 512 + D + 2 * 16 * 256 + 512 + 128 + D
EARLY_ROWS = 648


def _device_step(x, c, ctx, c_ctx, tgt, wm, bm, ng, wit_g, wit_r, wlrt, ln_g, ln_b, ws, bs, w2, gb2, gbn, wpa, wpb, wo, gf,
                 exchange=None, shards=None):
    L = x.shape[0]
    wit_qkv = wit_r[2048:3072]
    ws16 = ws.astype(BF16)
    wst16 = jnp.swapaxes(ws, 1, 2).astype(BF16)
    bscol = bs[:, :, None]
    lrw = [jnp.zeros((LRW, 256), F32).at[16 * r:16 * r + 16].set(w2[r]).astype(BF16) for r in range(2)]
    lrwt = [w.T for w in lrw]
    gbias = [gb2[r:r + 1] for r in range(2)]

    cc = jnp.zeros((8, D), F32).at[0:1].set(c).at[1:2].set(c_ctx)
    mod = _modvec(cc, wm, bm)
    shift, scale, gate = mod[0:1, 0:D], mod[0:1, D:2 * D], mod[0:1, 2 * D:3 * D]
    shift_c, scale_c = mod[1:2, 0:D], mod[1:2, D:2 * D]

    hc = _prep_h(ctx, ng, scale_c, shift_c, "prep_hc")
    pc = _mm(hc, wit_qkv, tm=256, tn=1024, tk=D, out_dtype=F32, name="mm_pc", b_t=True)
    plrc = _mm(hc, wlrt, tm=256, tn=LRW, tk=D, out_dtype=F32, name="mm_plrc", b_t=True)
    zero_s = jnp.zeros((128, 256), F32)
    _, sbc_f, sc_f, _, sbc_b, sc_b = _gla_fwd2(pc, 0, plrc, lrw, gbias, (zero_s, zero_s), "gla_fwd_c")

    h, p, plr, vnr, vnc, late = _proj_fwd(x, ng, scale, shift, wit_g, wit_r, wlrt, ln_g, ln_b,
                                          shards if shards is not None else ())
    if shards is not None:
        me_xy = 2 * lax.axis_index("x") + lax.axis_index("y")
        g_wpa, g_wpb, g_wo = (_own(g, s_, me_xy) for g, s_ in zip(late, shards))
        wpa = jnp.swapaxes(g_wpa, 0, 1).reshape(512, D)
        wpb = jnp.swapaxes(g_wpb, 0, 1).reshape(512, D)
        wo = g_wo.reshape(D, D)
    o_f, sb_f, _, o_b, sb_b, _ = _gla_fwd2(p, 2, plr, lrw, gbias, (sc_f, sc_b), "gla_fwd")
    svc = _colmix_fwd(vnc.reshape(2, AC, L), ws16[2:4], bscol[2:4]).reshape(2, L, 128)
    ya_in, yb_in, svr, dwo, dx1, dout, loss, dgate, dgf = _tail_fwd(
        o_f, o_b, p, vnr, svc, x, tgt, ws16[0:2], bscol[0:2], gbn, wpa, wpb, wo, gate, gf)

    dwpa, dwpb, dp_g, dp, dsr, dsc, do, dgbn = _tail_bwd(dout, ya_in, yb_in, p, svr, svc, o_f, o_b, gbn, wo, wpa, wpb)
    dvnc, dws23, dbs23 = _colmix_bwd(dsc.reshape(2, AC, L), vnc.reshape(2, AC, L), wst16[2:4])
    dp, dws01, dbs01, dlng, dlnb = _ln_bwd(dsr, vnr, dvnc.reshape(2, L, 128), p, wst16[0:2], ln_g, dp)
    zero_ds = jnp.zeros((128, 256), F32)
    dqkv_f, dlr_f, dw2_f, dgb_f, ds0_f = _gla_bwd(p, 2, plr, lrw[0], lrwt[0], gbias[0], sb_f, zero_ds, do, None, None,
                                                  reverse=False, name="gla_bwd_f")
    dp, dlr, dw2_b, dgb_b, ds0_b = _gla_bwd(p, 2, plr, lrw[1], lrwt[1], gbias[1], sb_b, zero_ds, do, (dqkv_f, dlr_f), dp,
                                            reverse=True, name="gla_bwd_b")
    zero_do = jnp.zeros((ctx.shape[0], 512), F32)
    dqkvc_f, dlrc_f, dw2c_f, dgbc_f, _ = _gla_bwd(pc, 0, plrc, lrw[0], lrwt[0], gbias[0], sbc_f, ds0_f, zero_do, None, None,
                                                  reverse=False, name="gla_bwd_cf")
    dqkvc, dlrc, dw2c_b, dgbc_b, _ = _gla_bwd(pc, 0, plrc, lrw[1], lrwt[1], gbias[1], sbc_b, ds0_b, zero_do,
                                              (dqkvc_f, dlrc_f), None, reverse=True, name="gla_bwd_cb")
    dhc = _mm(dqkvc, wit_qkv, tm=256, tn=D, tk=1024, out_dtype=F32, name="mm_dhc")
    dhc = _mm(dlrc, wlrt, tm=256, tn=D, tk=LRW, out_dtype=F32, name="mm_dhc_lr", acc=dhc)
    _, dng_c, dscale_c, dshift_c = _prep_bwd(ctx, dhc, None, ng, scale_c, "prep_bwd_c")

    dwit_g = _mm_tn(dp_g, h, ta=1024, tn=D, tk=2048, name="mm_dwi_g")
    dwit_r = _mm_tn(dp, h, ta=1024, tn=D, tk=2048, name="mm_dwi_r")
    dwit_qkv = _mm_tn(dqkvc, hc, ta=1024, tn=D, tk=256, name="mm_dwi_c", acc=dwit_r[2048:3072])
    dwlrt = _mm_tn(dlr, h, ta=LRW, tn=D, tk=2048, name="mm_dwlr")
    dwlrt = _mm_tn(dlrc, hc, ta=LRW, tn=D, tk=256, name="mm_dwlr_c", acc=dwlrt)
    big = dict(dwit_g=dwit_g, dwit_r=dwit_r, dwit_qkv=dwit_qkv, dwlrt=dwlrt, dwpa=dwpa, dwpb=dwpb, dwo=dwo)

    dmodc = jnp.concatenate([dshift_c, dscale_c], axis=1)
    dscc = _dcctx(jnp.zeros((8, 2 * D), F32).at[0:1].set(dmodc), wm)[0:1]
    dw2p = dw2_f + dw2c_f, dw2_b + dw2c_b
    small = dict(
        dmodc=dmodc, dscc=dscc, dng_c=dng_c, dlng=dlng, dlnb=dlnb, dws=jnp.concatenate([dws01, dws23], axis=0),
        dbs=jnp.concatenate([dbs01, dbs23], axis=0)[:, :, 0], dgbn=dgbn, dgf=dgf,
        dw2=jnp.stack([dw2p[0][0:16], dw2p[1][16:32]]), dgb2=jnp.concatenate([dgb_f + dgbc_f, dgb_b + dgbc_b], axis=0),
        loss=loss[0, 0], dgate=dgate)

    send = exchange(big) if exchange is not None else ()
    early = _pack([small[k] for k in EARLY_KEYS[:-2]] + [jnp.broadcast_to(small["loss"], (128,)), small["dgate"]], EARLY_ROWS) \
        if exchange is not None else None
    (dx, dng, dscale, dshift), got, early_all = _proj_bwd(dp_g, dp, dlr, wit_g, wit_r, wlrt, x, dx1, ng, scale, send, early)
    return dict(dx=dx, got=got, early=early, early_all=early_all, dshift=dshift, dscale=dscale, dng_lat=dng, **big, **small)


ANY = pl.BlockSpec(memory_space=pl.ANY)


def _coords():
    return lax.axis_index("x"), lax.axis_index("y"), lax.axis_index("c")


def _flip(v, bit):
    return 1 - v if bit else v


def _remote(src, dst, send_sem, recv_sem, dev):
    return pltpu.make_async_remote_copy(src_ref=src, dst_ref=dst, send_sem=send_sem, recv_sem=recv_sem,
                                        device_id=dev, device_id_type=MESH)


def _own(out, block, idx):
    return lax.dynamic_update_slice_in_dim(out, block[None], idx, axis=0)


def _half_idx(shape, axis, which, lead=()):
    idx = [pl.ds(0, d) for d in shape]
    h = shape[axis] // 2
    idx[axis] = pl.ds(which * h, h)
    return tuple(lead) + tuple(idx)


def _gather_weights(split, whole, name):
    ns, nw = len(split), len(whole)
    n = ns + nw
    arrs = [a for a, _ in split] + list(whole)

    def body(*refs):
        ins, outs = refs[:n], refs[n:2 * n]
        a_send, a_recv, b_send, b_recv = refs[2 * n:]
        x, y, c = _coords()
        me = 2 * x + y
        sib = (x, y, 1 - c)
        peers = [(1 - x, y), (x, 1 - y), (1 - x, 1 - y)]

        def half(k, slot, which):
            return outs[k].at[_half_idx(arrs[k].shape, split[k][1], which, lead=(slot,))]

        sends = []
        for k in range(n):
            for j, (px, py) in enumerate(peers):
                if k < ns:
                    rc = _remote(ins[k].at[_half_idx(arrs[k].shape, split[k][1], c)], half(k, me, c), a_send.at[3 * k + j],
                                 a_recv.at[3 * k + j], (px, py, c))
                else:
                    rc = _remote(ins[k], outs[k].at[me], a_send.at[3 * k + j], a_recv.at[3 * k + j], (px, py, c))
                rc.start()
                sends.append(rc)
        for k in range(ns):
            for j, (px, py) in enumerate(peers):
                landed = half(k, 2 * px + py, c)
                _remote(landed, landed, a_send.at[3 * k + j], a_recv.at[3 * k + j], (px, py, c)).wait_recv()
                fw = _remote(landed, landed, b_send.at[3 * k + j], b_recv.at[3 * k + j], sib)
                fw.start()
                sends.append(fw)
        for k in range(ns, n):
            for j, (px, py) in enumerate(peers):
                landed = outs[k].at[2 * px + py]
                _remote(landed, landed, a_send.at[3 * k + j], a_recv.at[3 * k + j], (px, py, c)).wait_recv()
        for k in range(ns):
            for j, (px, py) in enumerate(peers):
                passed = half(k, 2 * px + py, 1 - c)
                _remote(passed, passed, b_send.at[3 * k + j], b_recv.at[3 * k + j], sib).wait_recv()
        for rc in sends:
            rc.wait_send()

    outs = _pcall(
        body, name=name, in_specs=[ANY] * n, out_specs=[ANY] * n,
        out_shape=[SDS((4,) + a.shape, a.dtype) for a in arrs],
        scratch_shapes=[pltpu.SemaphoreType.DMA((3 * n,)), pltpu.SemaphoreType.DMA((3 * n,)), pltpu.SemaphoreType.DMA((3 * ns,)),
                        pltpu.SemaphoreType.DMA((3 * ns,))],
    )(*arrs)
    me_xy = 2 * lax.axis_index("x") + lax.axis_index("y")
    return [_own(o, a, me_xy) for o, a in zip(outs, arrs)]


def _gather_all(a, swap, name):
    masks = [(mx, my, mc) for mx in range(2) for my in range(2) for mc in range(2)][1:]
    n = len(swap)

    def body(*refs):
        in_ref, sw_in = refs[0], refs[1:1 + n]
        out_ref, sw_out = refs[1 + n], refs[2 + n:2 + 2 * n]
        send_sems, recv_sems = refs[2 + 2 * n:]
        x, y, c = _coords()
        me = 4 * x + 2 * y + c
        sends = []
        for j, (mx, my, mc) in enumerate(masks):
            rc = _remote(in_ref, out_ref.at[me], send_sems.at[j], recv_sems.at[j], (_flip(x, mx), _flip(y, my), _flip(c, mc)))
            rc.start()
            sends.append(rc)
        for k in range(n):
            rc = _remote(sw_in[k], sw_out[k], send_sems.at[7 + k], recv_sems.at[7 + k], (x, y, 1 - c))
            rc.start()
            sends.append(rc)
        for j, (mx, my, mc) in enumerate(masks):
            px, py, pc = _flip(x, mx), _flip(y, my), _flip(c, mc)
            landed = out_ref.at[4 * px + 2 * py + pc]
            _remote(landed, landed, send_sems.at[j], recv_sems.at[j], (px, py, pc)).wait_recv()
        for k in range(n):
            _remote(sw_out[k], sw_out[k], send_sems.at[7 + k], recv_sems.at[7 + k], (x, y, 1 - c)).wait_recv()
        for rc in sends:
            rc.wait_send()

    res = _pcall(
        body, name=name, in_specs=[ANY] * (1 + n), out_specs=[ANY] * (1 + n),
        out_shape=[SDS((8,) + a.shape, a.dtype)] + [SDS(s_.shape, s_.dtype) for s_ in swap],
        scratch_shapes=[pltpu.SemaphoreType.DMA((7 + n,)), pltpu.SemaphoreType.DMA((7 + n,))],
    )(a, *swap)
    return _own(res[0], a, 4 * lax.axis_index("x") + 2 * lax.axis_index("y") + lax.axis_index("c")), list(res[1:])


def _half_shape(shape, axis):
    return tuple(d // 2 if i == axis else d for i, d in enumerate(shape))


def _swap_half_c(arrs, axes, name):
    n = len(arrs)

    def body(*refs):
        ins, outs = refs[:n], refs[n:2 * n]
        send_sems, recv_sems = refs[2 * n:]
        x, y, c = _coords()
        sends = []
        for k in range(n):
            rc = _remote(ins[k].at[_half_idx(arrs[k].shape, axes[k], 1 - c)], outs[k], send_sems.at[k], recv_sems.at[k],
                         (x, y, 1 - c))
            rc.start()
            sends.append(rc)
        for rc in sends:
            rc.wait()

    return _pcall(
        body, name=name, in_specs=[ANY] * n, out_specs=[ANY] * n,
        out_shape=[SDS(_half_shape(a.shape, ax), a.dtype) for a, ax in zip(arrs, axes)],
        scratch_shapes=[pltpu.SemaphoreType.DMA((n,)), pltpu.SemaphoreType.DMA((n,))],
    )(*arrs)


def _a2a_xy(arrs, name):
    n = len(arrs)

    def body(*refs):
        ins, outs = refs[:n], refs[n:2 * n]
        send_sems, recv_sems = refs[2 * n:]
        x, y, c = _coords()
        me = 2 * x + y
        peers = [(1 - x, y), (x, 1 - y), (1 - x, 1 - y)]
        sends = []
        for k in range(n):
            for j, (px, py) in enumerate(peers):
                rc = _remote(ins[k].at[2 * px + py], outs[k].at[me], send_sems.at[3 * k + j], recv_sems.at[3 * k + j], (px, py, c))
                rc.start()
                sends.append(rc)
        for k in range(n):
            for j, (px, py) in enumerate(peers):
                landed = outs[k].at[2 * px + py]
                _remote(landed, landed, send_sems.at[3 * k + j], recv_sems.at[3 * k + j], (px, py, c)).wait_recv()
        for rc in sends:
            rc.wait_send()

    outs = _pcall(
        body, name=name, in_specs=[ANY] * n, out_specs=[ANY] * n, out_shape=[SDS(a.shape, a.dtype) for a in arrs],
        scratch_shapes=[pltpu.SemaphoreType.DMA((3 * n,)), pltpu.SemaphoreType.DMA((3 * n,))],
    )(*arrs)
    me_xy = 2 * lax.axis_index("x") + lax.axis_index("y")
    return [_own(o, lax.dynamic_index_in_dim(a, me_xy, axis=0, keepdims=False), me_xy) for o, a in zip(outs, arrs)]


def _exchange_c(arrs, name):
    n = len(arrs)

    def body(*refs):
        ins, outs = refs[:n], refs[n:2 * n]
        send_sems, recv_sems = refs[2 * n:]
        x, y, c = _coords()
        sends = []
        for k in range(n):
            rc = _remote(ins[k], outs[k], send_sems.at[k], recv_sems.at[k], (x, y, 1 - c))
            rc.start()
            sends.append(rc)
        for rc in sends:
            rc.wait()

    return _pcall(
        body, name=name, in_specs=[ANY] * n, out_specs=[ANY] * n, out_shape=[SDS(a.shape, a.dtype) for a in arrs],
        scratch_shapes=[pltpu.SemaphoreType.DMA((n,)), pltpu.SemaphoreType.DMA((n,))],
    )(*arrs)


def _join_halves(halves, axes, name):
    n = len(halves)
    full = [tuple(2 * d if i == ax else d for i, d in enumerate(a.shape)) for a, ax in zip(halves, axes)]

    def body(*refs):
        ins, outs = refs[:n], refs[n:2 * n]
        send_sems, recv_sems = refs[2 * n:]
        x, y, c = _coords()
        sends = []
        for k in range(n):
            rc = _remote(ins[k], outs[k].at[_half_idx(full[k], axes[k], c)], send_sems.at[k], recv_sems.at[k], (x, y, 1 - c))
            rc.start()
            sends.append(rc)
        for k in range(n):
            landed = outs[k].at[_half_idx(full[k], axes[k], 1 - c)]
            _remote(landed, landed, send_sems.at[k], recv_sems.at[k], (x, y, 1 - c)).wait_recv()
        for rc in sends:
            rc.wait_send()

    outs = _pcall(
        body, name=name, in_specs=[ANY] * n, out_specs=[ANY] * n,
        out_shape=[SDS(f, a.dtype) for f, a in zip(full, halves)],
        scratch_shapes=[pltpu.SemaphoreType.DMA((n,)), pltpu.SemaphoreType.DMA((n,))],
    )(*halves)
    ci = lax.axis_index("c")
    return [lax.dynamic_update_slice_in_dim(o, a, ci * a.shape[ax], axis=ax) for o, a, ax in zip(outs, halves, axes)]


def _pair_sum(a, got, cidx, axis, name):
    _, r, cdim = a.shape
    hshape = _half_shape(a.shape, axis)

    def body(c_ref, a_ref, g_ref, o_ref):
        o_ref[...] = (a_ref[...] + g_ref[...]).astype(BF16)

    if axis == 1:
        tr = min(r // 2, 256)
        nj = (r // 2) // tr
        blk = pl.BlockSpec((1, tr, cdim), lambda s, j, c: (s, j, 0))
        a_spec = pl.BlockSpec((1, tr, cdim), lambda s, j, c: (s, c[0] * nj + j, 0))
    else:
        nj = (cdim // 2) // 128
        blk = pl.BlockSpec((1, r, 128), lambda s, j, c: (s, 0, j))
        a_spec = pl.BlockSpec((1, r, 128), lambda s, j, c: (s, 0, c[0] * nj + j))
    return _pcall(
        body, name=name, out_shape=SDS(hshape, BF16),
        grid_spec=pltpu.PrefetchScalarGridSpec(num_scalar_prefetch=1, grid=(4, nj), in_specs=[a_spec, blk], out_specs=blk),
        compiler_params=_cp(("parallel", "parallel")),
    )(cidx, a, got)


def _sum_chips(parts, name):
    _, h, cdim = parts.shape

    def body(p_ref, o_ref):
        acc = p_ref[0].astype(F32)
        for k in range(1, 4):
            acc = acc + p_ref[k].astype(F32)
        o_ref[...] = acc

    if h % 256 == 0 or h in (128,):
        tr = min(h, 256)
        grid, in_spec, out_spec = (h // tr,), pl.BlockSpec((4, tr, cdim), lambda i: (0, i, 0)), pl.BlockSpec((tr, cdim), lambda i: (i, 0))
    else:
        grid, in_spec, out_spec = (cdim // 128,), pl.BlockSpec((4, h, 128), lambda i: (0, 0, i)), pl.BlockSpec((h, 128), lambda i: (0, i))
    return _pcall(
        body, name=name, grid=grid, in_specs=[in_spec], out_specs=out_spec, out_shape=SDS((h, cdim), F32),
        compiler_params=_cp(("parallel",)),
    )(parts)


def _sum_slots(a, name, rows):
    s, n, _ = a.shape

    def body(a_ref, o_ref):
        acc = a_ref[0]
        for k in range(1, s):
            acc = acc + a_ref[k]
        o_ref[...] = acc

    return _pcall(
        body, name=name, grid=(n // rows,), in_specs=[pl.BlockSpec((s, rows, 128), lambda i: (0, i, 0))],
        out_specs=pl.BlockSpec((rows, 128), lambda i: (i, 0)), out_shape=SDS((n, 128), F32),
        compiler_params=_cp(("parallel",)),
    )(a)


def _adam_math(w, g, m, v):
    nm = ADAM_B1 * m + (1.0 - ADAM_B1) * g
    nv = ADAM_B2 * v + (1.0 - ADAM_B2) * (g * g)
    m_hat = nm / (1.0 - ADAM_B1 ** ADAM_STEP)
    v_hat = nv / (1.0 - ADAM_B2 ** ADAM_STEP)
    return -ADAM_LR * (m_hat / (jnp.sqrt(v_hat) + ADAM_EPS) + ADAM_WD * w), nm, nv


def _adamw(w, g, m, v, name, rows):
    r, cdim = w.shape

    def body(w_ref, g_ref, m_ref, v_ref, d_ref, nm_ref, nv_ref):
        d_ref[...], nm_ref[...], nv_ref[...] = _adam_math(w_ref[...], g_ref[...], m_ref[...], v_ref[...])

    blk = pl.BlockSpec((rows, cdim), lambda i: (i, 0))
    return _pcall(
        body, name=name, grid=(r // rows,), in_specs=[blk] * 4, out_specs=[blk] * 3,
        out_shape=[SDS(w.shape, F32)] * 3, compiler_params=_cp(("parallel",)),
    )(w, g, m, v)


def _adamw_joined(w, mine, other, m, v, cidx, axis, name, rows):
    r, cdim = w.shape
    if axis == 0:
        rows = r

    def body(c_ref, w_ref, a_ref, b_ref, m_ref, v_ref, g_ref, d_ref, nm_ref, nv_ref):
        a, b = a_ref[...], b_ref[...]
        g = jnp.where(c_ref[0] == 0, jnp.concatenate([a, b], axis=axis), jnp.concatenate([b, a], axis=axis))
        g_ref[...] = g
        d_ref[...], nm_ref[...], nv_ref[...] = _adam_math(w_ref[...], g, m_ref[...], v_ref[...])

    blk = pl.BlockSpec((rows, cdim), lambda i, c: (i, 0))
    hshape = (rows // 2, cdim) if axis == 0 else (rows, cdim // 2)
    hblk = pl.BlockSpec(hshape, lambda i, c: (i, 0))
    return _pcall(
        body, name=name, out_shape=[SDS(w.shape, F32)] * 4,
        grid_spec=pltpu.PrefetchScalarGridSpec(num_scalar_prefetch=1, grid=(r // rows,), in_specs=[blk, hblk, hblk, blk, blk],
                                               out_specs=[blk] * 4),
        compiler_params=_cp(("parallel",)),
    )(cidx, w, mine, other, m, v)


def _adamw_many(ws, gs, ms, vs, name):
    n = len(ws)

    def body(*refs):
        outs = refs[4 * n:]
        for k in range(n):
            d, nm, nv = _adam_math(refs[k][...], refs[n + k][...], refs[2 * n + k][...], refs[3 * n + k][...])
            outs[k][...] = d
            outs[n + k][...] = nm
            outs[2 * n + k][...] = nv

    res = _pcall(body, name=name, out_shape=[SDS(w.shape, F32) for w in ws] * 3)(*ws, *gs, *ms, *vs)
    return res[:n], res[n:2 * n], res[2 * n:]


def _pack(pieces, rows):
    flat = jnp.concatenate([p.reshape(-1) for p in pieces])
    return jnp.pad(flat, (0, rows * 128 - flat.shape[0])).reshape(rows, 128)


def _unpack(buf, shapes):
    flat = buf.reshape(-1)
    out, off = [], 0
    for shp in shapes:
        size = 1
        for s in shp:
            size *= s
        out.append(flat[off:off + size].reshape(shp))
        off += size
    return out


def _perm_cols(w):
    perm = jnp.concatenate([w[..., 3104:5152], w[..., 0:1024], w[..., 1056:1568], w[..., 1568:2080], w[..., 2592:3104],
                            w[..., 2080:2592]], axis=-1)
    return perm, w[..., 1024:1056]


def _unperm_cols(perm, lr32):
    return jnp.concatenate([perm[..., 2048:3072], lr32, perm[..., 3072:3584], perm[..., 3584:4096], perm[..., 4608:5120],
                            perm[..., 4096:4608], perm[..., 0:2048]], axis=-1)


LATE_ROWS = 32


def kernel(x, c, ctx, c_ctx, w_mod, b_mod, norm_g, w_in, a_ln_g, a_ln_b, a_ws, a_bs, b_gate_w2, b_gate_b, b_norm_g, w_proj_a, w_proj_b, w_out, final_norm_g, loss_target, m_c_ctx, m_w_mod, m_b_mod, m_norm_g, m_w_in, m_a_ln_g, m_a_ln_b, m_a_ws, m_a_bs, m_b_gate_w2, m_b_gate_b, m_b_norm_g, m_w_proj_a, m_w_proj_b, m_w_out, m_final_norm_g, v_c_ctx, v_w_mod, v_b_mod, v_norm_g, v_w_in, v_a_ln_g, v_a_ln_b, v_a_ws, v_a_bs, v_b_gate_w2, v_b_gate_b, v_b_norm_g, v_w_proj_a, v_w_proj_b, v_w_out, v_final_norm_g):
    xi, yi, ci = _coords()
    me_xy = 2 * xi + yi

    gate_pack = _pack([b_gate_w2[0], b_gate_b[0]], 24)
    w_in_t, m_w_in_t, v_w_in_t = (jnp.swapaxes(a[0], 0, 1) for a in (w_in, m_w_in, v_w_in))
    g_wit, g_wm, g_gate = _gather_weights([(w_in_t.astype(BF16), 1), (w_mod[0].astype(BF16), 0)], [gate_pack], "gather_weights")
    late_shards = (w_proj_a[0].astype(BF16), w_proj_b[0].astype(BF16), w_out[0].astype(BF16))
    wit_u = g_wit.reshape(4 * 1288, D)
    wit_g = wit_u[3104:5152]
    wit_r = jnp.concatenate([wit_u[1056:1568], wit_u[1568:2080], wit_u[2592:3104], wit_u[2080:2592], wit_u[0:1024]], axis=0)
    wlrt = jnp.pad(wit_u[1024:1056], ((0, LRW - 32), (0, 0)))
    wm = jnp.swapaxes(g_wm, 0, 1).reshape(D, 3 * D)
    gflat = g_gate.reshape(4, 24 * 128)
    w2 = jnp.swapaxes(gflat[:, 0:2048].reshape(4, 2, 16, 64), 0, 2)
    w2 = jnp.swapaxes(w2, 0, 1).reshape(2, 16, 256)
    gb2 = jnp.swapaxes(gflat[:, 2048:2176].reshape(4, 2, 64), 0, 1).reshape(2, 256)

    tags = ["wi", "wpa", "wpb", "wo"]
    half_axes = [2, 1, 1, 1]
    sent = []

    def exchange(g):
        dwr = g["dwit_r"]
        dwit_u = jnp.concatenate([g["dwit_qkv"], g["dwlrt"][0:32], dwr[0:512], dwr[512:1024], dwr[1536:2048], dwr[1024:1536],
                                  g["dwit_g"]], axis=0)
        big = [dwit_u.reshape(4, 1288, D), jnp.swapaxes(g["dwpa"].reshape(512, 4, 256), 0, 1),
               jnp.swapaxes(g["dwpb"].reshape(512, 4, 256), 0, 1), g["dwo"].reshape(4, 256, D)]
        other = _swap_half_c(big, half_axes, "swap_half_in")
        cidx = jnp.reshape(ci, (1,)).astype(jnp.int32)
        sent.extend(_pair_sum(a, o, cidx, ax, "sum_pair_" + t) for a, o, ax, t in zip(big, other, half_axes, tags))
        return sent

    r = _device_step(x[0], c, ctx[0], c_ctx[None], loss_target[0], wm, b_mod, norm_g, wit_g, wit_r, wlrt, a_ln_g, a_ln_b,
                     a_ws[0], a_bs[0], w2, gb2, b_norm_g, None, None, None, final_norm_g[None], exchange, late_shards)

    parts = [_own(g, lax.dynamic_index_in_dim(s_, me_xy, axis=0, keepdims=False), me_xy) for g, s_ in zip(r["got"], sent)]
    halves = [_sum_chips(p_, "sum_chips_" + t) for p_, t in zip(parts, tags)]

    me8 = 4 * xi + 2 * yi + ci
    early_all = _own(r["early_all"], r["early"], me8)
    late = _pack([r["dshift"], r["dscale"], r["dng_lat"], c], LATE_ROWS)
    late_all, others = _gather_all(late, halves, "gather_small")
    s_early = _sum_slots(early_all, "sum_early", EARLY_ROWS // 3)
    s_late = _sum_slots(late_all, "sum_late", LATE_ROWS)
    (s_dmodc, s_dscc, s_dng_c, s_dlng, s_dlnb, s_dws, s_dbs, s_dgbn, s_dgf, s_dw2, s_dgb2, s_loss, s_dgate) = _unpack(
        s_early, EARLY_SHAPES)
    s_dshift, s_dscale, s_dng_lat, _ = _unpack(s_late, [(1, D)] * 4)
    s_dng = s_dng_lat + s_dng_c
    s_dmod = jnp.concatenate([s_dshift, s_dscale, s_dgate], axis=1)
    loss = s_loss[0]
    s_dmodc_p = jnp.pad(s_dmodc, ((0, 0), (0, D)))
    g_b_mod = s_dmod + s_dmodc_p
    sg = jax.nn.sigmoid(c_ctx)
    g_c_ctx = s_dscc * (sg * (1.0 + c_ctx * (1.0 - sg)))
    g_w2 = lax.dynamic_slice_in_dim(s_dw2, 64 * me_xy, 64, axis=2)[None]
    g_gb2 = lax.dynamic_slice_in_dim(s_dgb2, 64 * me_xy, 64, axis=1)[None]

    flat_l = late_all.reshape(8, LATE_ROWS * 128)
    dgate_all = early_all.reshape(8, EARLY_ROWS * 128)[:, EARLY_SIZE - D:EARLY_SIZE]
    dmod_all = jnp.concatenate([flat_l[:, 0:2 * D], dgate_all], axis=1)
    c_all = flat_l[:, 3 * D:4 * D]
    lhs = jnp.concatenate([_silu(c_all), _silu(c_ctx)[None], jnp.zeros((7, D), F32)], axis=0)
    rhs = jnp.concatenate([dmod_all, s_dmodc_p, jnp.zeros((7, 3 * D), F32)], axis=0)
    rhs = lax.dynamic_slice_in_dim(rhs, 768 * me_xy, 768, axis=1)
    g_w_mod = _mm(lhs.T.astype(BF16), rhs.astype(BF16), tm=D, tn=768, tk=16, out_dtype=F32, name="mm_dwm")

    cidx = jnp.reshape(ci, (1,)).astype(jnp.int32)
    g_w_in_t, d_w_in_t, nm_w_in_t, nv_w_in_t = _adamw_joined(w_in_t, halves[0], others[0], m_w_in_t, v_w_in_t, cidx, 1,
                                                             "adamw_w_in", 184)
    g_w_in, d_w_in, nm_w_in, nv_w_in = (jnp.swapaxes(a, 0, 1) for a in (g_w_in_t, d_w_in_t, nm_w_in_t, nv_w_in_t))
    g_wpa, d_wpa, nm_wpa, nv_wpa = _adamw_joined(w_proj_a[0], halves[1], others[1], m_w_proj_a[0], v_w_proj_a[0], cidx, 0,
                                                 "adamw_wpa", 0)
    g_wpb, d_wpb, nm_wpb, nv_wpb = _adamw_joined(w_proj_b[0], halves[2], others[2], m_w_proj_b[0], v_w_proj_b[0], cidx, 0,
                                                 "adamw_wpb", 0)
    g_wo, d_wo, nm_wo, nv_wo = _adamw_joined(w_out[0], halves[3], others[3], m_w_out[0], v_w_out[0], cidx, 0, "adamw_wo", 0)
    d_w_mod, nm_w_mod, nv_w_mod = _adamw(w_mod[0], g_w_mod, m_w_mod[0], v_w_mod[0], "adamw_w_mod", 256)

    names = ["c_ctx", "b_mod", "norm_g", "a_ln_g", "a_ln_b", "a_ws", "a_bs", "b_gate_w2", "b_gate_b", "b_norm_g", "final_norm_g"]
    ws_ = [c_ctx, b_mod, norm_g, a_ln_g, a_ln_b, a_ws, a_bs, b_gate_w2, b_gate_b, b_norm_g, final_norm_g]
    gs_ = [g_c_ctx, g_b_mod, s_dng, s_dlng, s_dlnb, s_dws, s_dbs, g_w2, g_gb2, s_dgbn, s_dgf]
    ms_ = [m_c_ctx, m_b_mod, m_norm_g, m_a_ln_g, m_a_ln_b, m_a_ws, m_a_bs, m_b_gate_w2, m_b_gate_b, m_b_norm_g, m_final_norm_g]
    vs_ = [v_c_ctx, v_b_mod, v_norm_g, v_a_ln_g, v_a_ln_b, v_a_ws, v_a_bs, v_b_gate_w2, v_b_gate_b, v_b_norm_g, v_final_norm_g]
    shapes = [w.shape for w in ws_]
    flat2 = [(1, 1024), (1, 3072), (1, 1024), (1, 512), (1, 512), (512, 128), (4, 128), (32, 64), (2, 64), (1, 512), (1, 1024)]
    as2d = lambda arrs: [a.reshape(s) for a, s in zip(arrs, flat2)]
    d_s, nm_s, nv_s = _adamw_many(as2d(ws_), as2d(gs_), as2d(ms_), as2d(vs_), "adamw_small")
    d_small = {n: a.reshape(s) for n, a, s in zip(names, d_s, shapes)}
    nm_small = {n: a.reshape(s) for n, a, s in zip(names, nm_s, shapes)}
    nv_small = {n: a.reshape(s) for n, a, s in zip(names, nv_s, shapes)}
    g_small = {n: g.reshape(s) for n, g, s in zip(names, gs_, shapes)}

    order = ["c_ctx", "w_mod", "b_mod", "norm_g", "w_in", "a_ln_g", "a_ln_b", "a_ws", "a_bs", "b_gate_w2", "b_gate_b", "b_norm_g",
             "w_proj_a", "w_proj_b", "w_out", "final_norm_g"]
    big_g = dict(w_mod=g_w_mod[None], w_in=g_w_in[None], w_proj_a=g_wpa[None], w_proj_b=g_wpb[None], w_out=g_wo[None])
    big_d = dict(w_mod=d_w_mod[None], w_in=d_w_in[None], w_proj_a=d_wpa[None], w_proj_b=d_wpb[None], w_out=d_wo[None])
    big_m = dict(w_mod=nm_w_mod[None], w_in=nm_w_in[None], w_proj_a=nm_wpa[None], w_proj_b=nm_wpb[None], w_out=nm_wo[None])
    big_v = dict(w_mod=nv_w_mod[None], w_in=nv_w_in[None], w_proj_a=nv_wpa[None], w_proj_b=nv_wpb[None], w_out=nv_wo[None])
    grads = [big_g[n] if n in big_g else g_small[n] for n in order]
    deltas = [big_d[n] if n in big_d else d_small[n] for n in order]
    new_m = [big_m[n] if n in big_m else nm_small[n] for n in order]
    new_v = [big_v[n] if n in big_v else nv_small[n] for n in order]
    return (loss, r["dx"][None], *grads, *deltas, *new_m, *new_v)
```

```python
import jax
import jax.numpy as jnp
from jax import lax
from jax.experimental import pallas as pl
from jax.experimental.pallas import tpu as pltpu

F32 = jnp.float32
BF16 = jnp.bfloat16
SDS = jax.ShapeDtypeStruct

D = 1024
NP = 5120
LRW = 128
CH = 64
AC = 128
EPS = 1e-6
TOK = 512
GLA_TB = 1024
VMEM_BIG = 48 * 1024 * 1024

ADAM_LR, ADAM_B1, ADAM_B2, ADAM_EPS, ADAM_WD, ADAM_STEP = 0.001, 0.9, 0.999, 1e-08, 0.01, 10

_pcall = pl.pallas_call
MESH = pl.DeviceIdType.MESH


def _cp(sem=None, vmem=None):
    kw = {}
    if sem is not None:
        kw["dimension_semantics"] = sem
    if vmem is not None:
        kw["vmem_limit_bytes"] = vmem
    return pltpu.CompilerParams(**kw)


def _silu(x):
    return x * jax.nn.sigmoid(x)


def _silu_and_grad(x):
    s = jax.nn.sigmoid(x)
    return x * s, s * (1.0 + x * (1.0 - s))


def _logsig(x):
    return jnp.minimum(x, 0.0) - jnp.log1p(jnp.exp(-jnp.abs(x)))


def _nt(a, b):
    return lax.dot_general(a, b, (((1,), (1,)), ((), ())), preferred_element_type=F32)


def _tn(a, b):
    return lax.dot_general(a, b, (((0,), (0,)), ((), ())), preferred_element_type=F32)


def _nn(a, b):
    return jnp.dot(a, b, preferred_element_type=F32)


def _full(shape):
    return pl.BlockSpec(shape, lambda *_: (0,) * len(shape))


def _mm(a, b, *, tm, tn, tk, out_dtype, name, acc=None, n_outer=False, b_t=False):
    m, k = a.shape
    n, k2 = (b.shape if b_t else b.shape[::-1])
    assert k == k2 and m % tm == 0 and n % tn == 0 and k % tk == 0, (a.shape, b.shape, tm, tn, tk)
    nk = k // tk
    has_acc = acc is not None

    def body(*refs):
        if has_acc:
            a_ref, b_ref, c_ref, o_ref = refs[:4]
        else:
            a_ref, b_ref, o_ref = refs[:3]
        part = (_nt if b_t else _nn)(a_ref[...].astype(BF16), b_ref[...].astype(BF16))
        if nk == 1:
            o_ref[...] = ((c_ref[...] + part) if has_acc else part).astype(out_dtype)
            return
        acc_ref = refs[-1]
        kk = pl.program_id(2)

        @pl.when(kk == 0)
        def _():
            if has_acc:
                acc_ref[...] = c_ref[...] + part
            else:
                acc_ref[...] = part

        @pl.when(kk > 0)
        def _():
            acc_ref[...] += part

        @pl.when(kk == nk - 1)
        def _():
            o_ref[...] = acc_ref[...].astype(out_dtype)

    if n_outer:
        ij = lambda g0, g1: (g1, g0)
        grid = (n // tn, m // tm, nk)
    else:
        ij = lambda g0, g1: (g0, g1)
        grid = (m // tm, n // tn, nk)
    b_spec = (pl.BlockSpec((tn, tk), lambda g0, g1, kk: (ij(g0, g1)[1], kk)) if b_t
              else pl.BlockSpec((tk, tn), lambda g0, g1, kk: (kk, ij(g0, g1)[1])))
    in_specs = [pl.BlockSpec((tm, tk), lambda g0, g1, kk: (ij(g0, g1)[0], kk)), b_spec]
    args = [a, b]
    if has_acc:
        in_specs.append(pl.BlockSpec((tm, tn), lambda g0, g1, kk: ij(g0, g1)))
        args.append(acc)
    return _pcall(
        body, name=name, grid=grid, in_specs=in_specs,
        out_specs=pl.BlockSpec((tm, tn), lambda g0, g1, kk: ij(g0, g1)),
        out_shape=SDS((m, n), out_dtype), scratch_shapes=([pltpu.VMEM((tm, tn), F32)] if nk > 1 else []),
        compiler_params=_cp(("parallel", "parallel", "arbitrary"), VMEM_BIG),
    )(*args)


def _mm_tn(a, b, *, ta, tn, tk, name, acc=None):
    m, ka = a.shape
    m2, n = b.shape
    assert m == m2 and ka % ta == 0 and n % tn == 0 and m % tk == 0, (a.shape, b.shape, ta, tn, tk)
    nk = m // tk
    has_acc = acc is not None

    def body(*refs):
        if has_acc:
            a_ref, b_ref, c_ref, o_ref = refs
        else:
            a_ref, b_ref, o_ref = refs
        kk = pl.program_id(2)
        part = _tn(a_ref[...].astype(BF16), b_ref[...].astype(BF16))

        @pl.when(kk == 0)
        def _():
            if has_acc:
                o_ref[...] = c_ref[...] + part
            else:
                o_ref[...] = part

        @pl.when(kk > 0)
        def _():
            o_ref[...] += part

    in_specs = [pl.BlockSpec((tk, ta), lambda i, j, kk: (kk, i)), pl.BlockSpec((tk, tn), lambda i, j, kk: (kk, j))]
    args = [a, b]
    if has_acc:
        in_specs.append(pl.BlockSpec((ta, tn), lambda i, j, kk: (i, j)))
        args.append(acc)
    return _pcall(
        body, name=name, grid=(ka // ta, n // tn, nk), in_specs=in_specs,
        out_specs=pl.BlockSpec((ta, tn), lambda i, j, kk: (i, j)), out_shape=SDS((ka, n), F32),
        compiler_params=_cp(("parallel", "parallel", "arbitrary"), VMEM_BIG),
    )(*args)


def _modvec(cc, wm, bm):
    def body(c_ref, w_ref, b_ref, o_ref):
        o_ref[...] = _nn(_silu(c_ref[...]).astype(BF16), w_ref[...]) + b_ref[...]

    return _pcall(body, name="modvec", out_shape=SDS((8, 3 * D), F32), compiler_params=_cp(None, VMEM_BIG))(cc, wm, bm)


def _dcctx(dmodc, wm):
    def body(d_ref, w_ref, o_ref):
        o_ref[...] = _nt(d_ref[...].astype(BF16), w_ref[...])

    return _pcall(
        body, name="dcctx", grid=(1,), in_specs=[_full((8, 2 * D)), pl.BlockSpec((D, 2 * D), lambda i: (0, 0))],
        out_specs=_full((8, D)), out_shape=SDS((8, D), F32), compiler_params=_cp(("arbitrary",), VMEM_BIG),
    )(dmodc, wm)


def _prep_h(x, ng, scale, shift, name):
    m = x.shape[0]

    def body(x_ref, g_ref, sc_ref, sh_ref, h_ref):
        xf = x_ref[...]
        r = lax.rsqrt(jnp.mean(xf * xf, axis=-1, keepdims=True) + EPS)
        y = (xf * r) * g_ref[...]
        h_ref[...] = (y * (1.0 + sc_ref[...]) + sh_ref[...]).astype(BF16)

    tok = min(TOK, m)
    row = pl.BlockSpec((tok, D), lambda i: (i, 0))
    return _pcall(
        body, name=name, grid=(m // tok,), in_specs=[row, _full((1, D)), _full((1, D)), _full((1, D))],
        out_specs=row, out_shape=SDS((m, D), BF16), compiler_params=_cp(("parallel",)),
    )(x, ng, scale, shift)


def _resident(shape):
    return pl.BlockSpec(shape, lambda *_: (0,) * len(shape), pipeline_mode=pl.Buffered(1))


PROJ_TM = 512


def _proj_fwd(x, ng, scale, shift, wit_g, wit_r, wlrt, ln_g, ln_b, share=()):
    m = x.shape[0]
    ns = len(share)
    steps = m // PROJ_TM
    src = [(0, 0), (0, D), (1, 2 * D), (1, 0), (1, D)]

    def body(*refs):
        x_ref, g_ref, sc_ref, sh_ref, wg_ref, wr_ref, wl_ref, lg_ref, lb_ref = refs[:9]
        share_refs = refs[9:9 + ns]
        h_ref, p_ref, plr_ref, vr_ref, vc_ref = refs[9 + ns:14 + ns]
        got_refs = refs[14 + ns:14 + 2 * ns]
        sems = refs[14 + 2 * ns:]

        def copies():
            cx, cy, cc = _coords()
            me = 2 * cx + cy
            peers = [(1 - cx, cy), (cx, 1 - cy), (1 - cx, 1 - cy)]
            out, back = [], []
            for k in range(ns):
                for j, (px, py) in enumerate(peers):
                    out.append(_remote(share_refs[k], got_refs[k].at[me], sems[0].at[3 * k + j], sems[1].at[3 * k + j], (px, py, cc)))
                    landed = got_refs[k].at[2 * px + py]
                    back.append(_remote(landed, landed, sems[0].at[3 * k + j], sems[1].at[3 * k + j], (px, py, cc)))
            return out, back

        if ns:
            @pl.when(pl.program_id(0) == 0)
            def _():
                for rc in copies()[0]:
                    rc.start()

            @pl.when(pl.program_id(0) == steps - 1)
            def _():
                out, back = copies()
                for rc in back:
                    rc.wait_recv()
                for rc in out:
                    rc.wait_send()

        xf = x_ref[...]
        r = lax.rsqrt(jnp.mean(xf * xf, axis=-1, keepdims=True) + EPS)
        y = (xf * r) * g_ref[...]
        h = (y * (1.0 + sc_ref[...]) + sh_ref[...]).astype(BF16)
        h_ref[...] = h
        for j, (which, r0) in enumerate(src):
            w_ref = wr_ref if which else wg_ref
            blk = _nt(h, w_ref[r0:r0 + D, :]).astype(BF16)
            p_ref[:, D * j:D * j + D] = blk
            if j == 4:
                xf = blk[:, 512:1024].astype(F32)
                xc = xf - jnp.mean(xf, axis=-1, keepdims=True)
                vn = (xc * lax.rsqrt(jnp.mean(xc * xc, axis=-1, keepdims=True) + EPS)) * lg_ref[...] + lb_ref[...]
                vr_ref[...] = vn[:, 0:256].astype(BF16)
                vc_ref[0] = vn[:, 256:384].astype(BF16)
                vc_ref[1] = vn[:, 384:512].astype(BF16)
        plr_ref[...] = _nt(h, wl_ref[...])

    row = pl.BlockSpec((PROJ_TM, D), lambda i: (i, 0))
    vec = _full((1, D))
    res = _pcall(
        body, name="proj_fwd", grid=(steps,),
        in_specs=[row, vec, vec, vec, _resident((2 * D, D)), _resident((3 * D, D)), _resident((LRW, D)), _full((1, 512)),
                  _full((1, 512))] + [ANY] * ns,
        out_specs=[row, pl.BlockSpec((PROJ_TM, NP), lambda i: (i, 0)), pl.BlockSpec((PROJ_TM, LRW), lambda i: (i, 0)),
                   pl.BlockSpec((PROJ_TM, 256), lambda i: (i, 0)), pl.BlockSpec((2, PROJ_TM, 128), lambda i: (0, i, 0))] + [ANY] * ns,
        out_shape=[SDS((m, D), BF16), SDS((m, NP), BF16), SDS((m, LRW), F32), SDS((m, 256), BF16), SDS((2, m, 128), BF16)]
        + [SDS((4,) + a.shape, a.dtype) for a in share],
        scratch_shapes=([pltpu.SemaphoreType.DMA((3 * ns,)), pltpu.SemaphoreType.DMA((3 * ns,))] if ns else []),
        compiler_params=_cp(("arbitrary",), VMEM_BIG),
    )(x, ng, scale, shift, wit_g, wit_r, wlrt, ln_g, ln_b, *share)
    return res[0], res[1], res[2], res[3], res[4], list(res[5:])


def _proj_bwd(dp_g, dp_r, dlr, wit_g, wit_r, wlrt, x, dx1, ng, scale, send=(), share8=None):
    m = x.shape[0]
    ns = len(send)
    n8 = 0 if share8 is None else 1
    steps = m // PROJ_TM
    masks = [(mx, my, mc) for mx in range(2) for my in range(2) for mc in range(2)][1:]

    def body(*refs):
        (dpg_ref, dpr_ref, dlr_ref, wg_ref, wr_ref, wl_ref, x_ref, r_ref, g_ref, sc_ref) = refs[:10]
        send_refs = refs[10:10 + ns]
        n_in = 10 + ns + n8
        dx_ref, dg_ref, dsc_ref, dsh_ref = refs[n_in:n_in + 4]
        got_refs = refs[n_in + 4:n_in + 4 + ns]
        sems = refs[n_in + 4 + ns + n8:]
        i = pl.program_id(0)

        def copies():
            cx, cy, cc = _coords()
            me = 2 * cx + cy
            peers = [(1 - cx, cy), (cx, 1 - cy), (1 - cx, 1 - cy)]
            out, back = [], []
            for k in range(ns):
                for j, (px, py) in enumerate(peers):
                    out.append(_remote(send_refs[k].at[2 * px + py], got_refs[k].at[me], sems[0].at[3 * k + j],
                                       sems[1].at[3 * k + j], (px, py, cc)))
                    landed = got_refs[k].at[2 * px + py]
                    back.append(_remote(landed, landed, sems[0].at[3 * k + j], sems[1].at[3 * k + j], (px, py, cc)))
            if n8:
                src8, all8 = refs[10 + ns], refs[n_in + 4 + ns]
                s8, r8 = sems[-2], sems[-1]
                for j, (mx, my, mc) in enumerate(masks):
                    px, py, pc = _flip(cx, mx), _flip(cy, my), _flip(cc, mc)
                    out.append(_remote(src8, all8.at[4 * cx + 2 * cy + cc], s8.at[j], r8.at[j], (px, py, pc)))
                    landed = all8.at[4 * px + 2 * py + pc]
                    back.append(_remote(landed, landed, s8.at[j], r8.at[j], (px, py, pc)))
            return out, back

        @pl.when(i == 0)
        def _():
            dg_ref[...] = jnp.zeros_like(dg_ref)
            dsc_ref[...] = jnp.zeros_like(dsc_ref)
            dsh_ref[...] = jnp.zeros_like(dsh_ref)
            if ns or n8:
                for rc in copies()[0]:
                    rc.start()

        dh_ = (_nn(dpg_ref[...], wg_ref[...]) + _nn(dpr_ref[...], wr_ref[...])
               + _nn(dlr_ref[...].astype(BF16), wl_ref[...]))
        xf = x_ref[...]
        r = lax.rsqrt(jnp.mean(xf * xf, axis=-1, keepdims=True) + EPS)
        xh = xf * r
        y = xh * g_ref[...]
        dsh_ref[...] += jnp.sum(dh_, axis=0, keepdims=True)
        dsc_ref[...] += jnp.sum(dh_ * y, axis=0, keepdims=True)
        dy = dh_ * (1.0 + sc_ref[...])
        dg_ref[...] += jnp.sum(dy * xh, axis=0, keepdims=True)
        dxh = dy * g_ref[...]
        dx_ref[...] = r * (dxh - xh * jnp.mean(dxh * xh, axis=-1, keepdims=True)) + r_ref[...]

        if ns or n8:
            @pl.when(i == steps - 1)
            def _():
                out, back = copies()
                for rc in back:
                    rc.wait_recv()
                for rc in out:
                    rc.wait_send()

    row = pl.BlockSpec((PROJ_TM, D), lambda i: (i, 0))
    vec = _full((1, D))
    kg, kr = dp_g.shape[1], dp_r.shape[1]
    extra_in = list(send) + ([share8] if n8 else [])
    extra_out = [SDS(a.shape, a.dtype) for a in send] + ([SDS((8,) + share8.shape, share8.dtype)] if n8 else [])
    res = _pcall(
        body, name="proj_bwd", grid=(steps,),
        in_specs=[pl.BlockSpec((PROJ_TM, kg), lambda i: (i, 0)), pl.BlockSpec((PROJ_TM, kr), lambda i: (i, 0)),
                  pl.BlockSpec((PROJ_TM, LRW), lambda i: (i, 0)), _resident((kg, D)), _resident((kr, D)), _resident((LRW, D)),
                  row, row, vec, vec] + [ANY] * len(extra_in),
        out_specs=[row, vec, vec, vec] + [ANY] * len(extra_out),
        out_shape=[SDS((m, D), F32), SDS((1, D), F32), SDS((1, D), F32), SDS((1, D), F32)] + extra_out,
        scratch_shapes=(([pltpu.SemaphoreType.DMA((3 * ns,)), pltpu.SemaphoreType.DMA((3 * ns,))] if ns else [])
                        + ([pltpu.SemaphoreType.DMA((7,)), pltpu.SemaphoreType.DMA((7,))] if n8 else [])),
        compiler_params=_cp(("arbitrary",), VMEM_BIG),
    )(dp_g, dp_r, dlr, wit_g, wit_r, wlrt, x, dx1, ng, scale, *extra_in)
    return tuple(res[:4]), list(res[4:4 + ns]), (res[4 + ns] if n8 else None)


def _prep_bwd(x, dh, dx1, ng, scale, name):
    m = x.shape[0]
    has_res = dx1 is not None

    def body(*refs):
        if has_res:
            x_ref, dh_ref, r_ref, g_ref, sc_ref, dx_ref, dg_ref, dsc_ref, dsh_ref = refs
        else:
            x_ref, dh_ref, g_ref, sc_ref, dx_ref, dg_ref, dsc_ref, dsh_ref = refs
        i = pl.program_id(0)

        @pl.when(i == 0)
        def _():
            dg_ref[...] = jnp.zeros_like(dg_ref)
            dsc_ref[...] = jnp.zeros_like(dsc_ref)
            dsh_ref[...] = jnp.zeros_like(dsh_ref)

        xf = x_ref[...]
        dh_ = dh_ref[...]
        r = lax.rsqrt(jnp.mean(xf * xf, axis=-1, keepdims=True) + EPS)
        xh = xf * r
        y = xh * g_ref[...]
        dsh_ref[...] += jnp.sum(dh_, axis=0, keepdims=True)
        dsc_ref[...] += jnp.sum(dh_ * y, axis=0, keepdims=True)
        dy = dh_ * (1.0 + sc_ref[...])
        dg_ref[...] += jnp.sum(dy * xh, axis=0, keepdims=True)
        dxh = dy * g_ref[...]
        dx = r * (dxh - xh * jnp.mean(dxh * xh, axis=-1, keepdims=True))
        if has_res:
            dx = dx + r_ref[...]
        dx_ref[...] = dx

    tok = min(TOK, m)
    row = pl.BlockSpec((tok, D), lambda i: (i, 0))
    vec = _full((1, D))
    in_specs = [row, row] + ([row] if has_res else []) + [vec, vec]
    args = [x, dh] + ([dx1] if has_res else []) + [ng, scale]
    return _pcall(
        body, name=name, grid=(m // tok,), in_specs=in_specs, out_specs=[row, vec, vec, vec],
        out_shape=[SDS((m, D), F32), SDS((1, D), F32), SDS((1, D), F32), SDS((1, D), F32)],
        compiler_params=_cp(("arbitrary",)),
    )(*args)


COLB = 2048


def _colmix_fwd(vnc, ws23, bs23):
    rows = vnc.shape[2] // COLB

    def body(v_ref, w_ref, b_ref, o_ref):
        o_ref[0] = _nn(w_ref[0], v_ref[0]) + b_ref[0]

    return _pcall(
        body, name="colmix_fwd", grid=(2, rows),
        in_specs=[pl.BlockSpec((1, AC, COLB), lambda g, j: (g, 0, j)), pl.BlockSpec((1, AC, AC), lambda g, j: (g, 0, 0)),
                  pl.BlockSpec((1, AC, 1), lambda g, j: (g, 0, 0))],
        out_specs=pl.BlockSpec((1, AC, COLB), lambda g, j: (g, 0, j)),
        out_shape=SDS(vnc.shape, F32), compiler_params=_cp(("parallel", "parallel")),
    )(vnc, ws23, bs23)


def _colmix_bwd(dsvc, vnc, ws23t):
    rows = vnc.shape[2] // COLB

    def body(d_ref, v_ref, wt_ref, dv_ref, dw_ref, db_ref):
        j = pl.program_id(1)

        @pl.when(j == 0)
        def _():
            dw_ref[...] = jnp.zeros_like(dw_ref)
            db_ref[...] = jnp.zeros_like(db_ref)

        d = d_ref[0]
        d16 = d.astype(BF16)
        dv_ref[0] = _nn(wt_ref[0], d16)
        dw_ref[0] += _nt(d16, v_ref[0])
        db_ref[0] += jnp.sum(d, axis=1, keepdims=True)

    blk = pl.BlockSpec((1, AC, COLB), lambda g, j: (g, 0, j))
    return _pcall(
        body, name="colmix_bwd", grid=(2, rows),
        in_specs=[blk, blk, pl.BlockSpec((1, AC, AC), lambda g, j: (g, 0, 0))],
        out_specs=[blk, pl.BlockSpec((1, AC, AC), lambda g, j: (g, 0, 0)), pl.BlockSpec((1, AC, 1), lambda g, j: (g, 0, 0))],
        out_shape=[SDS(vnc.shape, F32), SDS((2, AC, AC), F32), SDS((2, AC, 1), F32)],
        compiler_params=_cp(("parallel", "arbitrary")),
    )(dsvc, vnc, ws23t)


def _head_norm(o, gbn):
    out = []
    for h in range(4):
        oh = o[:, 128 * h:128 * h + 128]
        r = lax.rsqrt(jnp.mean(oh * oh, axis=-1, keepdims=True) + EPS)
        out.append((r, oh * r))
    return out


def _tail_fwd(o_f, o_b, p, vnr, svc, x, tgt, ws01, bs01, gbn, wpa, wpb, wo, gate, gf):
    m = p.shape[0]

    def body(of_ref, ob_ref, zb_ref, ua_ref, za_ref, ga_ref, gb_ref, vnr_ref, svc_ref, x_ref, t_ref, w_ref, b_ref, g_ref,
             wpa_ref, wpb_ref, wo_ref, gate_ref, gf_ref,
             ya_ref, yb_ref, svr_ref, dwo_ref, dx1_ref, dout_ref, loss_ref, dgate_ref, dgf_ref):
        i = pl.program_id(0)

        @pl.when(i == 0)
        def _():
            loss_ref[...] = jnp.zeros_like(loss_ref)
            dgate_ref[...] = jnp.zeros_like(dgate_ref)
            dgf_ref[...] = jnp.zeros_like(dgf_ref)
            dwo_ref[...] = jnp.zeros_like(dwo_ref)

        o = of_ref[...] + ob_ref[...]
        zb = zb_ref[...].astype(F32)
        for h, (r, xh) in enumerate(_head_norm(o, None)):
            sl = slice(128 * h, 128 * h + 128)
            yb_ref[:, sl] = ((xh * g_ref[:, sl]) * _silu(zb[:, sl])).astype(BF16)
        for j in range(TOK // AC):
            for g in range(2):
                sv = _nn(w_ref[g], vnr_ref[AC * j:AC * j + AC, AC * g:AC * g + AC]) + b_ref[g]
                svr_ref[AC * j:AC * j + AC, AC * g:AC * g + AC] = sv
        sz = _silu(za_ref[...].astype(F32))
        u = ua_ref[...].astype(F32)
        ya_ref[:, 0:256] = ((u[:, 0:256] * svr_ref[...]) * sz[:, 0:256]).astype(BF16)
        ya_ref[:, 256:384] = ((u[:, 256:384] * svc_ref[0]) * sz[:, 256:384]).astype(BF16)
        ya_ref[:, 384:512] = ((u[:, 384:512] * svc_ref[1]) * sz[:, 384:512]).astype(BF16)
        ya = _nn(ya_ref[...], wpa_ref[...])
        yb = _nn(yb_ref[...], wpb_ref[...])
        mg = (jax.nn.sigmoid(ga_ref[...].astype(F32)) * ya + jax.nn.sigmoid(gb_ref[...].astype(F32)) * yb).astype(BF16)
        out_ = _nn(mg, wo_ref[...])
        x1 = x_ref[...] + gate_ref[...] * out_
        r = lax.rsqrt(jnp.mean(x1 * x1, axis=-1, keepdims=True) + EPS)
        xh = x1 * r
        err = xh * gf_ref[...] - t_ref[...]
        loss_ref[...] += 0.5 * jnp.sum(jnp.mean(err * err, axis=-1, keepdims=True), axis=0, keepdims=True)
        dy = err * (1.0 / D)
        dgf_ref[...] += jnp.sum(dy * xh, axis=0, keepdims=True)
        dxh = dy * gf_ref[...]
        dx1 = r * (dxh - xh * jnp.mean(dxh * xh, axis=-1, keepdims=True))
        dx1_ref[...] = dx1
        dout16 = (gate_ref[...] * dx1).astype(BF16)
        dout_ref[...] = dout16
        dgate_ref[...] += jnp.sum(dx1 * out_, axis=0, keepdims=True)
        dwo_ref[...] += _tn(mg, dout16)

    r512 = pl.BlockSpec((TOK, 512), lambda i: (i, 0))
    row = pl.BlockSpec((TOK, D), lambda i: (i, 0))
    vec = _full((1, D))
    return _pcall(
        body, name="tail_fwd", grid=(m // TOK,),
        in_specs=[r512, r512, pl.BlockSpec((TOK, 512), lambda i: (i, 6)), pl.BlockSpec((TOK, 512), lambda i: (i, 7)),
                  pl.BlockSpec((TOK, 512), lambda i: (i, 8)), row, pl.BlockSpec((TOK, D), lambda i: (i, 1)),
                  pl.BlockSpec((TOK, 256), lambda i: (i, 0)), pl.BlockSpec((2, TOK, 128), lambda i: (0, i, 0)), row, row,
                  _full((2, AC, AC)), _full((2, AC, 1)), _full((1, 512)), _resident((512, D)), _resident((512, D)),
                  _resident((D, D)), vec, vec],
        out_specs=[r512, r512, pl.BlockSpec((TOK, 256), lambda i: (i, 0)), _resident((D, D)), row, row, _full((1, 128)), vec, vec],
        out_shape=[SDS((m, 512), BF16), SDS((m, 512), BF16), SDS((m, 256), F32), SDS((D, D), F32), SDS((m, D), F32),
                   SDS((m, D), BF16), SDS((1, 128), F32), SDS((1, D), F32), SDS((1, D), F32)],
        compiler_params=_cp(("arbitrary",), VMEM_BIG),
    )(o_f, o_b, p, p, p, p, p, vnr, svc, x, tgt, ws01, bs01, gbn, wpa, wpb, wo, gate, gf)


DPR = 3072


def _tail_bwd(dout, ya_in, yb_in, p, svr, svc, o_f, o_b, gbn, wo, wpa, wpb):
    m = p.shape[0]

    def body(dout_ref, ya_ref, yb_ref, ga_ref, gb_ref, zb_ref, ua_ref, za_ref, svr_ref, svc_ref, of_ref, ob_ref, g_ref,
             wo_ref, wpa_ref, wpb_ref,
             dwpa_ref, dwpb_ref, dpg_ref, dpr_ref, dsr_ref, dsc_ref, do_ref, dg_ref):
        i = pl.program_id(0)

        @pl.when(i == 0)
        def _():
            dg_ref[...] = jnp.zeros_like(dg_ref)
            dwpa_ref[...] = jnp.zeros_like(dwpa_ref)
            dwpb_ref[...] = jnp.zeros_like(dwpb_ref)

        dm_ = _nt(dout_ref[...], wo_ref[...])
        ya_in, yb_in = ya_ref[...], yb_ref[...]
        ya = _nn(ya_in, wpa_ref[...])
        yb = _nn(yb_in, wpb_ref[...])
        sa = jax.nn.sigmoid(ga_ref[...].astype(F32))
        sb = jax.nn.sigmoid(gb_ref[...].astype(F32))
        dya16 = (dm_ * sa).astype(BF16)
        dyb16 = (dm_ * sb).astype(BF16)
        dwpa_ref[...] += _tn(ya_in, dya16)
        dwpb_ref[...] += _tn(yb_in, dyb16)
        dpg_ref[:, 0:D] = (dm_ * ya * (sa * (1.0 - sa))).astype(BF16)
        dpg_ref[:, D:2 * D] = (dm_ * yb * (sb * (1.0 - sb))).astype(BF16)
        dya = _nt(dya16, wpa_ref[...])
        dyb = _nt(dyb16, wpb_ref[...])

        u = ua_ref[...].astype(F32)
        za = za_ref[...].astype(F32)
        sz, dsz = _silu_and_grad(za)
        sv = jnp.concatenate([svr_ref[...], svc_ref[0], svc_ref[1]], axis=1)
        dpr_ref[:, 512:1024] = (dya * sv * sz).astype(BF16)
        dsv = dya * u * sz
        dsr_ref[...] = dsv[:, 0:256]
        dsc_ref[0] = dsv[:, 256:384]
        dsc_ref[1] = dsv[:, 384:512]
        dpr_ref[:, 1024:1536] = (dya * u * sv * dsz).astype(BF16)

        zb = zb_ref[...].astype(F32)
        o = of_ref[...] + ob_ref[...]
        szb, dszb = _silu_and_grad(zb)
        for h, (r, xh) in enumerate(_head_norm(o, None)):
            sl = slice(128 * h, 128 * h + 128)
            gh = g_ref[:, sl]
            don = dyb[:, sl] * szb[:, sl]
            dpr_ref[:, sl] = (dyb[:, sl] * (xh * gh) * dszb[:, sl]).astype(BF16)
            dg_ref[:, sl] += jnp.sum(don * xh, axis=0, keepdims=True)
            dxh = don * gh
            do_ref[:, sl] = (r * (dxh - xh * jnp.mean(dxh * xh, axis=-1, keepdims=True))).astype(BF16)

    r512 = pl.BlockSpec((TOK, 512), lambda i: (i, 0))
    row = pl.BlockSpec((TOK, D), lambda i: (i, 0))
    return _pcall(
        body, name="tail_bwd", grid=(m // TOK,),
        in_specs=[row, r512, r512, row, pl.BlockSpec((TOK, D), lambda i: (i, 1)), pl.BlockSpec((TOK, 512), lambda i: (i, 6)),
                  pl.BlockSpec((TOK, 512), lambda i: (i, 7)), pl.BlockSpec((TOK, 512), lambda i: (i, 8)),
                  pl.BlockSpec((TOK, 256), lambda i: (i, 0)), pl.BlockSpec((2, TOK, 128), lambda i: (0, i, 0)), r512, r512,
                  _full((1, 512)), _resident((D, D)), _resident((512, D)), _resident((512, D))],
        out_specs=[_resident((512, D)), _resident((512, D)), pl.BlockSpec((TOK, 2 * D), lambda i: (i, 0)),
                   pl.BlockSpec((TOK, 1536), lambda i: (i, 0)),
                   pl.BlockSpec((TOK, 256), lambda i: (i, 0)), pl.BlockSpec((2, TOK, 128), lambda i: (0, i, 0)), r512, _full((1, 512))],
        out_shape=[SDS((512, D), F32), SDS((512, D), F32), SDS((m, 2 * D), BF16), SDS((m, DPR), BF16), SDS((m, 256), F32),
                   SDS((2, m, 128), F32), SDS((m, 512), BF16), SDS((1, 512), F32)],
        compiler_params=_cp(("arbitrary",), VMEM_BIG),
    )(dout, ya_in, yb_in, p, p, p, p, p, svr, svc, o_f, o_b, gbn, wo, wpa, wpb)


def _ln_bwd(dsr, vnr, dvnc, p, ws01t, ln_g, dp):
    m = p.shape[0]

    def body(dsr_ref, vnr_ref, dvc_ref, va_ref, wt_ref, g_ref, dpi_ref, dp_ref, dw_ref, db_ref, dlg_ref, dlb_ref, dvn_ref):
        i = pl.program_id(0)

        @pl.when(i == 0)
        def _():
            dw_ref[...] = jnp.zeros_like(dw_ref)
            db_ref[...] = jnp.zeros_like(db_ref)
            dlg_ref[...] = jnp.zeros_like(dlg_ref)
            dlb_ref[...] = jnp.zeros_like(dlb_ref)

        for j in range(TOK // AC):
            for g in range(2):
                d = dsr_ref[AC * j:AC * j + AC, AC * g:AC * g + AC]
                d16 = d.astype(BF16)
                dvn_ref[AC * j:AC * j + AC, AC * g:AC * g + AC] = _nn(wt_ref[g], d16)
                dw_ref[g] += _nt(d16, vnr_ref[AC * j:AC * j + AC, AC * g:AC * g + AC])
                db_ref[g] += jnp.sum(d, axis=1, keepdims=True)
        dvn_ref[:, 256:384] = dvc_ref[0]
        dvn_ref[:, 384:512] = dvc_ref[1]
        dvn = dvn_ref[...]
        xf = va_ref[...].astype(F32)
        xc = xf - jnp.mean(xf, axis=-1, keepdims=True)
        rs = lax.rsqrt(jnp.mean(xc * xc, axis=-1, keepdims=True) + EPS)
        xh = xc * rs
        dlg_ref[...] += jnp.sum(dvn * xh, axis=0, keepdims=True)
        dlb_ref[...] += jnp.sum(dvn, axis=0, keepdims=True)
        dxh = dvn * g_ref[...]
        dva = rs * (dxh - jnp.mean(dxh, axis=-1, keepdims=True) - xh * jnp.mean(dxh * xh, axis=-1, keepdims=True))
        dp_ref[...] = dva.astype(BF16)

    return _pcall(
        body, name="ln_bwd", grid=(m // TOK,),
        in_specs=[pl.BlockSpec((TOK, 256), lambda i: (i, 0)), pl.BlockSpec((TOK, 256), lambda i: (i, 0)),
                  pl.BlockSpec((2, TOK, 128), lambda i: (0, i, 0)), pl.BlockSpec((TOK, 512), lambda i: (i, 9)),
                  _full((2, AC, AC)), _full((1, 512)), pl.BlockSpec(memory_space=pl.ANY)],
        out_specs=[pl.BlockSpec((TOK, 512), lambda i: (i, 3)), _full((2, AC, AC)), _full((2, AC, 1)), _full((1, 512)), _full((1, 512))],
        out_shape=[SDS((m, DPR), BF16), SDS((2, AC, AC), F32), SDS((2, AC, 1), F32), SDS((1, 512), F32), SDS((1, 512), F32)],
        scratch_shapes=[pltpu.VMEM((TOK, 512), F32)],
        input_output_aliases={6: 0}, compiler_params=_cp(("arbitrary",)),
    )(dsr, vnr, dvnc, p, ws01t, ln_g, dp)


def _tri_mm(tri, a):
    a1 = a.astype(BF16)
    r1 = a - a1.astype(F32)
    a2 = r1.astype(BF16)
    a3 = (r1 - a2.astype(F32)).astype(BF16)
    n = a.shape[1]
    r = _nn(tri, jnp.concatenate([a1, a2, a3], axis=1))
    return r[:, 0:n] + r[:, n:2 * n] + r[:, 2 * n:3 * n]


def _gla_masks(reverse):
    ri = lax.broadcasted_iota(jnp.int32, (CH, CH), 0)
    ci = lax.broadcasted_iota(jnp.int32, (CH, CH), 1)
    vis = (ci >= ri) if reverse else (ci <= ri)
    vis_t = (ci <= ri) if reverse else (ci >= ri)
    r4 = lax.broadcasted_iota(jnp.int32, (4 * CH, CH), 0) & (CH - 1)
    c4 = lax.broadcasted_iota(jnp.int32, (4 * CH, CH), 1)
    vis4 = (c4 >= r4) if reverse else (c4 <= r4)
    vis4_t = (c4 <= r4) if reverse else (c4 >= r4)
    lane = lax.broadcasted_iota(jnp.int32, (1, 256), 1)
    hm = [(lane >= CH * h) & (lane < CH * h + CH) for h in range(4)]
    return vis, vis_t, vis4, vis4_t, hm


def _stack_heads(x, hm):
    return jnp.concatenate([jnp.where(hm[h], x, 0.0).astype(BF16) for h in range(4)], axis=0)


def _diag_heads(full, hm):
    r = full.shape[0] // 4
    acc = jnp.where(hm[0], full[0:r], 0.0)
    for h in range(1, 4):
        acc = acc + jnp.where(hm[h], full[r * h:r * h + r], 0.0)
    return acc


def _rows_of_heads(x):
    return jnp.concatenate([x[:, 128 * h:128 * h + 128] for h in range(4)], axis=0)


def _lane_vis(reverse, transpose):
    ri = lax.broadcasted_iota(jnp.int32, (CH, 4 * CH), 0)
    ci = lax.broadcasted_iota(jnp.int32, (CH, 4 * CH), 1) & (CH - 1)
    return (ci >= ri) if (reverse != transpose) else (ci <= ri)


def _gla_fwd2(p, qkv_blk, lr, lrws, gbiases, s0s, name):
    m = p.shape[0]
    tb = min(GLA_TB, m)
    nb = m // tb
    nc = tb // CH

    def body(qkv_f, lr_f, qkv_b, lr_b, lrw_f, lrw_b, gb_f, gb_b, s0_f, s0_b,
             o_f, sb_f, sfin_f, o_b, sb_b, sfin_b, st_f, st_b):
        i = pl.program_id(0)

        @pl.when(i == 0)
        def _():
            st_f[...] = s0_f[...]
            st_b[...] = s0_b[...]

        dirs = []
        for reverse, qkv_ref, lr_ref, lrw_ref, gb_ref, o_ref, sb_ref, st_ref in (
                (False, qkv_f, lr_f, lrw_f, gb_f, o_f, sb_f, st_f), (True, qkv_b, lr_b, lrw_b, gb_b, o_b, sb_b, st_b)):
            vis, _, vis4, _, hm = _gla_masks(reverse)
            logits = _nn(lr_ref[...].astype(BF16), lrw_ref[...]) + gb_ref[...]
            dirs.append(dict(reverse=reverse, qkv=qkv_ref, o=o_ref, sb=sb_ref, vis4=vis4, hm=hm,
                             tri=vis.astype(F32).astype(BF16), a=_logsig(logits) * (1.0 / 16.0), st=st_ref[...]))
        for step in range(nc):
            for d in dirs:
                c = nc - 1 - step if d["reverse"] else step
                rows = slice(CH * c, CH * c + CH)
                b = _tri_mm(d["tri"], d["a"][rows])
                bl = b[0:1] if d["reverse"] else b[CH - 1:CH]
                q = d["qkv"][rows, 0:256].astype(F32) * 0.125
                k = d["qkv"][rows, 256:512].astype(F32)
                v16 = d["qkv"][rows, 512:1024].astype(BF16)
                qd = q * jnp.exp(b)
                kd16 = (k * jnp.exp(-b)).astype(BF16)
                kdec16 = (k * jnp.exp(bl - b)).astype(BF16)
                qstack = _stack_heads(qd, d["hm"])
                sc = jnp.where(d["vis4"], _nt(qstack, kd16), 0.0).astype(BF16)
                inter = _nt(qstack, d["st"].astype(BF16))
                for h in range(4):
                    d["o"][rows, 128 * h:128 * h + 128] = (
                        _nn(sc[CH * h:CH * h + CH], v16[:, 128 * h:128 * h + 128]) + inter[CH * h:CH * h + CH])
                d["sb"][c] = d["st"]
                d["st"] = d["st"] * jnp.exp(bl) + _diag_heads(_tn(v16, kdec16), d["hm"])
        st_f[...] = dirs[0]["st"]
        st_b[...] = dirs[1]["st"]

        @pl.when(i == nb - 1)
        def _():
            sfin_f[...] = dirs[0]["st"]
            sfin_b[...] = dirs[1]["st"]

    fw = lambda i: i
    bw = lambda i: nb - 1 - i
    in_specs = []
    for rm in (fw, bw):
        in_specs += [pl.BlockSpec((tb, 1024), lambda i, rm=rm: (rm(i), qkv_blk)), pl.BlockSpec((tb, LRW), lambda i, rm=rm: (rm(i), 0))]
    in_specs += [_full((LRW, 256))] * 2 + [_full((1, 256))] * 2 + [_full((128, 256))] * 2
    out_specs, out_shape = [], []
    for rm in (fw, bw):
        out_specs += [pl.BlockSpec((tb, 512), lambda i, rm=rm: (rm(i), 0)), pl.BlockSpec((nc, 128, 256), lambda i, rm=rm: (rm(i), 0, 0)),
                      _full((128, 256))]
        out_shape += [SDS((m, 512), F32), SDS((m // CH, 128, 256), F32), SDS((128, 256), F32)]
    return _pcall(
        body, name=name, grid=(nb,), in_specs=in_specs, out_specs=out_specs, out_shape=out_shape,
        scratch_shapes=[pltpu.VMEM((128, 256), F32), pltpu.VMEM((128, 256), F32)], compiler_params=_cp(("arbitrary",), VMEM_BIG),
    )(p, lr, p, lr, lrws[0], lrws[1], gbiases[0], gbiases[1], s0s[0], s0s[1])


def _gla_bwd(p, qkv_blk, lr, lrw, lrwt, gbias, sb, dsfin, do, prev, dp, *, reverse, name):
    m = p.shape[0]
    tb = min(GLA_TB, m)
    nb = m // tb
    nc = tb // CH
    rmap = (lambda i: i) if reverse else (lambda i: nb - 1 - i)
    has_prev = prev is not None
    has_dp = dp is not None

    def body(*refs):
        refs = list(refs)
        qkv_ref, lr_ref, lrw_ref, lrwt_ref, gb_ref, sb_ref, dsfin_ref, do_ref = refs[:8]
        refs = refs[8:]
        if has_prev:
            pq_ref, plr_ref = refs[:2]
            refs = refs[2:]
        if has_dp:
            refs = refs[1:]
        dqkv_ref, dlr_ref, dw2_ref, dgb_ref, ds0_ref, dst_ref, dlog_ref = refs
        i = pl.program_id(0)

        @pl.when(i == 0)
        def _():
            dst_ref[...] = dsfin_ref[...]
            dw2_ref[...] = jnp.zeros_like(dw2_ref)
            dgb_ref[...] = jnp.zeros_like(dgb_ref)

        vis, vis_t, vis4, vis4_t, hm = _gla_masks(reverse)
        tri = vis.astype(F32).astype(BF16)
        tri_t = vis_t.astype(F32).astype(BF16)
        lane_vis = _lane_vis(reverse, False)
        lane_vis_t = _lane_vis(reverse, True)
        lr16 = lr_ref[...].astype(BF16)
        logits = _nn(lr16, lrw_ref[...]) + gb_ref[...]
        a_all = _logsig(logits) * (1.0 / 16.0)
        dsig = (1.0 - jax.nn.sigmoid(logits)) * (1.0 / 16.0)
        dst = dst_ref[...]
        for c in (range(nc) if reverse else range(nc - 1, -1, -1)):
            rows = slice(CH * c, CH * c + CH)
            b = _tri_mm(tri, a_all[rows])
            bl = b[0:1] if reverse else b[CH - 1:CH]
            eb = jnp.exp(b)
            enb = jnp.exp(-b)
            ebl = jnp.exp(bl - b)
            el = jnp.exp(bl)
            q = qkv_ref[rows, 0:256].astype(F32) * 0.125
            k = qkv_ref[rows, 256:512].astype(F32)
            v16 = qkv_ref[rows, 512:1024].astype(BF16)
            do16 = do_ref[rows, :].astype(BF16)
            qd = q * eb
            kd = k * enb
            kdec = k * ebl
            st = sb_ref[c]
            st16 = st.astype(BF16)
            dst16 = dst.astype(BF16)
            qd16 = qd.astype(BF16)
            kd16 = kd.astype(BF16)
            qstack = _stack_heads(qd, hm)
            kstack = _stack_heads(kd, hm)
            kdecstack = _stack_heads(kdec, hm)
            pt = jnp.where(vis4_t, _nt(kstack, qd16), 0.0).astype(BF16)
            dvinter = _nt(kdecstack, dst16)
            do_rows = _rows_of_heads(do16)
            v_rows = _rows_of_heads(v16)
            dp_cat = jnp.where(lane_vis, _diag_heads(_nt(do_rows, v_rows), hm), 0.0).astype(BF16)
            dpt_cat = jnp.where(lane_vis_t, _diag_heads(_nt(v_rows, do_rows), hm), 0.0).astype(BF16)
            dqd = _nn(dp_cat, kstack) + _diag_heads(_nn(do_rows, st16), hm)
            dkd = _nn(dpt_cat, qstack)
            dkdec = _diag_heads(_nn(v_rows, dst16), hm)
            for h in range(4):
                rh = slice(CH * h, CH * h + CH)
                dv_h = _nn(pt[rh], do_rows[rh]) + dvinter[rh]
                if has_prev:
                    dv_h = dv_h + pq_ref[rows, 512 + 128 * h:512 + 128 * h + 128]
                dqkv_ref[rows, 512 + 128 * h:512 + 128 * h + 128] = dv_h.astype(dqkv_ref.dtype)
            dq = dqd * eb * 0.125
            dk = dkd * enb + dkdec * ebl
            if has_prev:
                dq = dq + pq_ref[rows, 0:256]
                dk = dk + pq_ref[rows, 256:512]
            dqkv_ref[rows, 0:256] = dq.astype(dqkv_ref.dtype)
            dqkv_ref[rows, 256:512] = dk.astype(dqkv_ref.dtype)
            g_kdec = dkdec * kdec
            db = dqd * qd - dkd * kd - g_kdec
            dbl = jnp.sum(g_kdec, axis=0, keepdims=True) + jnp.sum(st * dst, axis=0, keepdims=True) * el
            da = _tri_mm(tri_t, db) + dbl
            dlog_ref[rows, :] = da * dsig[rows]
            dst = dst * el + _diag_heads(_tn(do16, qd16), hm)
        dst_ref[...] = dst
        dlog = dlog_ref[...]
        dlog16 = dlog.astype(BF16)
        dlr = _nn(dlog16, lrwt_ref[...])
        if has_prev:
            dlr = dlr + plr_ref[...]
        dlr_ref[...] = dlr
        dw2_ref[...] += _tn(lr16, dlog16)
        dgb_ref[...] += jnp.sum(dlog, axis=0, keepdims=True)

        @pl.when(i == nb - 1)
        def _():
            ds0_ref[...] = dst

    in_specs = [pl.BlockSpec((tb,1024), lambda i: (rmap(i), qkv_blk)), pl.BlockSpec((tb,LRW), lambda i: (rmap(i), 0)),
                _full((LRW, 256)), _full((256, LRW)), _full((1, 256)), pl.BlockSpec((nc, 128, 256), lambda i: (rmap(i), 0, 0)),
                _full((128, 256)), pl.BlockSpec((tb,512), lambda i: (rmap(i), 0))]
    args = [p, lr, lrw, lrwt, gbias, sb, dsfin, do]
    if has_prev:
        in_specs += [pl.BlockSpec((tb,1024), lambda i: (rmap(i), 0)), pl.BlockSpec((tb,LRW), lambda i: (rmap(i), 0))]
        args += list(prev)
    aliases = {}
    if has_dp:
        in_specs.append(pl.BlockSpec(memory_space=pl.ANY))
        aliases = {len(args): 0}
        args.append(dp)
        dq_spec = pl.BlockSpec((tb,1024), lambda i: (rmap(i), 2))
        dq_shape = SDS(dp.shape, dp.dtype)
    else:
        dq_spec = pl.BlockSpec((tb,1024), lambda i: (rmap(i), 0))
        dq_shape = SDS((m, 1024), F32)
    return _pcall(
        body, name=name, grid=(nb,), in_specs=in_specs,
        out_specs=[dq_spec, pl.BlockSpec((tb,LRW), lambda i: (rmap(i), 0)), _full((LRW, 256)), _full((1, 256)), _full((128, 256))],
        out_shape=[dq_shape, SDS((m, LRW), F32), SDS((LRW, 256), F32), SDS((1, 256), F32), SDS((128, 256), F32)],
        scratch_shapes=[pltpu.VMEM((128, 256), F32), pltpu.VMEM((tb,256), F32)],
        input_output_aliases=aliases, compiler_params=_cp(("arbitrary",)),
    )(*args)


EARLY_KEYS = ["dmodc", "dscc", "dng_c", "dlng", "dlnb", "dws", "dbs", "dgbn", "dgf", "dw2", "dgb2", "loss", "dgate"]
EARLY_SHAPES = [(1, 2 * D), (D,), (1, D), (1, 512), (1, 512), (1, 4, 128, 128), (1, 4, 128), (1, 512), (D,), (2, 16, 256), (2, 256),
                (128,), (1, D)]
EARLY_SIZE = 2 * D + D + D + 512 + 512 + 4 * 128 * 128 + 512 + 512 + D + 2 * 16 * 256 + 512 + 128 + D
EARLY_ROWS = 648


def _device_step(x, c, ctx, c_ctx, tgt, wm, bm, ng, wit_g, wit_r, wlrt, ln_g, ln_b, ws, bs, w2, gb2, gbn, wpa, wpb, wo, gf,
                 exchange=None, shards=None):
    L = x.shape[0]
    wit_qkv = wit_r[2048:3072]
    ws16 = ws.astype(BF16)
    wst16 = jnp.swapaxes(ws, 1, 2).astype(BF16)
    bscol = bs[:, :, None]
    lrw = [jnp.zeros((LRW, 256), F32).at[16 * r:16 * r + 16].set(w2[r]).astype(BF16) for r in range(2)]
    lrwt = [w.T for w in lrw]
    gbias = [gb2[r:r + 1] for r in range(2)]

    cc = jnp.zeros((8, D), F32).at[0:1].set(c).at[1:2].set(c_ctx)
    mod = _modvec(cc, wm, bm)
    shift, scale, gate = mod[0:1, 0:D], mod[0:1, D:2 * D], mod[0:1, 2 * D:3 * D]
    shift_c, scale_c = mod[1:2, 0:D], mod[1:2, D:2 * D]

    hc = _prep_h(ctx, ng, scale_c, shift_c, "prep_hc")
    pc = _mm(hc, wit_qkv, tm=256, tn=1024, tk=D, out_dtype=F32, name="mm_pc", b_t=True)
    plrc = _mm(hc, wlrt, tm=256, tn=LRW, tk=D, out_dtype=F32, name="mm_plrc", b_t=True)
    zero_s = jnp.zeros((128, 256), F32)
    _, sbc_f, sc_f, _, sbc_b, sc_b = _gla_fwd2(pc, 0, plrc, lrw, gbias, (zero_s, zero_s), "gla_fwd_c")

    h, p, plr, vnr, vnc, late = _proj_fwd(x, ng, scale, shift, wit_g, wit_r, wlrt, ln_g, ln_b,
                                          shards if shards is not None else ())
    if shards is not None:
        me_xy = 2 * lax.axis_index("x") + lax.axis_index("y")
        g_wpa, g_wpb, g_wo = (_own(g, s_, me_xy) for g, s_ in zip(late, shards))
        wpa = jnp.swapaxes(g_wpa, 0, 1).reshape(512, D)
        wpb = jnp.swapaxes(g_wpb, 0, 1).reshape(512, D)
        wo = g_wo.reshape(D, D)
    o_f, sb_f, _, o_b, sb_b, _ = _gla_fwd2(p, 2, plr, lrw, gbias, (sc_f, sc_b), "gla_fwd")
    svc = _colmix_fwd(vnc.reshape(2, AC, L), ws16[2:4], bscol[2:4]).reshape(2, L, 128)
    ya_in, yb_in, svr, dwo, dx1, dout, loss, dgate, dgf = _tail_fwd(
        o_f, o_b, p, vnr, svc, x, tgt, ws16[0:2], bscol[0:2], gbn, wpa, wpb, wo, gate, gf)

    dwpa, dwpb, dp_g, dp, dsr, dsc, do, dgbn = _tail_bwd(dout, ya_in, yb_in, p, svr, svc, o_f, o_b, gbn, wo, wpa, wpb)
    dvnc, dws23, dbs23 = _colmix_bwd(dsc.reshape(2, AC, L), vnc.reshape(2, AC, L), wst16[2:4])
    dp, dws01, dbs01, dlng, dlnb = _ln_bwd(dsr, vnr, dvnc.reshape(2, L, 128), p, wst16[0:2], ln_g, dp)
    zero_ds = jnp.zeros((128, 256), F32)
    dqkv_f, dlr_f, dw2_f, dgb_f, ds0_f = _gla_bwd(p, 2, plr, lrw[0], lrwt[0], gbias[0], sb_f, zero_ds, do, None, None,
                                                  reverse=False, name="gla_bwd_f")
    dp, dlr, dw2_b, dgb_b, ds0_b = _gla_bwd(p, 2, plr, lrw[1], lrwt[1], gbias[1], sb_b, zero_ds, do, (dqkv_f, dlr_f), dp,
                                            reverse=True, name="gla_bwd_b")
    zero_do = jnp.zeros((ctx.shape[0], 512), BF16)
    dqkvc_f, dlrc_f, dw2c_f, dgbc_f, _ = _gla_bwd(pc, 0, plrc, lrw[0], lrwt[0], gbias[0], sbc_f, ds0_f, zero_do, None, None,
                                                  reverse=False, name="gla_bwd_cf")
    dqkvc, dlrc, dw2c_b, dgbc_b, _ = _gla_bwd(pc, 0, plrc, lrw[1], lrwt[1], gbias[1], sbc_b, ds0_b, zero_do,
                                              (dqkvc_f, dlrc_f), None, reverse=True, name="gla_bwd_cb")
    dhc = _mm(dqkvc, wit_qkv, tm=256, tn=D, tk=1024, out_dtype=F32, name="mm_dhc")
    dhc = _mm(dlrc, wlrt, tm=256, tn=D, tk=LRW, out_dtype=F32, name="mm_dhc_lr", acc=dhc)
    _, dng_c, dscale_c, dshift_c = _prep_bwd(ctx, dhc, None, ng, scale_c, "prep_bwd_c")

    dwit_g = _mm_tn(dp_g, h, ta=1024, tn=D, tk=2048, name="mm_dwi_g")
    dwit_r = _mm_tn(dp, h, ta=1024, tn=D, tk=2048, name="mm_dwi_r")
    dwit_qkv = _mm_tn(dqkvc, hc, ta=1024, tn=D, tk=256, name="mm_dwi_c", acc=dwit_r[2048:3072])
    dwlrt = _mm_tn(dlr, h, ta=LRW, tn=D, tk=2048, name="mm_dwlr")
    dwlrt = _mm_tn(dlrc, hc, ta=LRW, tn=D, tk=256, name="mm_dwlr_c", acc=dwlrt)
    big = dict(dwit_g=dwit_g, dwit_r=dwit_r, dwit_qkv=dwit_qkv, dwlrt=dwlrt, dwpa=dwpa, dwpb=dwpb, dwo=dwo)

    dmodc = jnp.concatenate([dshift_c, dscale_c], axis=1)
    dscc = _dcctx(jnp.zeros((8, 2 * D), F32).at[0:1].set(dmodc), wm)[0:1]
    dw2p = dw2_f + dw2c_f, dw2_b + dw2c_b
    small = dict(
        dmodc=dmodc, dscc=dscc, dng_c=dng_c, dlng=dlng, dlnb=dlnb, dws=jnp.concatenate([dws01, dws23], axis=0),
        dbs=jnp.concatenate([dbs01, dbs23], axis=0)[:, :, 0], dgbn=dgbn, dgf=dgf,
        dw2=jnp.stack([dw2p[0][0:16], dw2p[1][16:32]]), dgb2=jnp.concatenate([dgb_f + dgbc_f, dgb_b + dgbc_b], axis=0),
        loss=loss[0, 0], dgate=dgate)

    send = exchange(big) if exchange is not None else ()
    early = _pack([small[k] for k in EARLY_KEYS[:-2]] + [jnp.broadcast_to(small["loss"], (128,)), small["dgate"]], EARLY_ROWS) \
        if exchange is not None else None
    (dx, dng, dscale, dshift), got, early_all = _proj_bwd(dp_g, dp, dlr, wit_g, wit_r, wlrt, x, dx1, ng, scale, send, early)
    return dict(dx=dx, got=got, early=early, early_all=early_all, dshift=dshift, dscale=dscale, dng_lat=dng, **big, **small)


ANY = pl.BlockSpec(memory_space=pl.ANY)


def _coords():
    return lax.axis_index("x"), lax.axis_index("y"), lax.axis_index("c")


def _flip(v, bit):
    return 1 - v if bit else v


def _remote(src, dst, send_sem, recv_sem, dev):
    return pltpu.make_async_remote_copy(src_ref=src, dst_ref=dst, send_sem=send_sem, recv_sem=recv_sem,
                                        device_id=dev, device_id_type=MESH)


def _own(out, block, idx):
    return lax.dynamic_update_slice_in_dim(out, block[None], idx, axis=0)


def _half_idx(shape, axis, which, lead=()):
    idx = [pl.ds(0, d) for d in shape]
    h = shape[axis] // 2
    idx[axis] = pl.ds(which * h, h)
    return tuple(lead) + tuple(idx)


def _gather_weights(split, whole, name):
    ns, nw = len(split), len(whole)
    n = ns + nw
    arrs = [a for a, _ in split] + list(whole)

    def body(*refs):
        ins, outs = refs[:n], refs[n:2 * n]
        a_send, a_recv, b_send, b_recv = refs[2 * n:]
        x, y, c = _coords()
        me = 2 * x + y
        sib = (x, y, 1 - c)
        peers = [(1 - x, y), (x, 1 - y), (1 - x, 1 - y)]

        def half(k, slot, which):
            return outs[k].at[_half_idx(arrs[k].shape, split[k][1], which, lead=(slot,))]

        sends = []
        for k in range(n):
            for j, (px, py) in enumerate(peers):
                if k < ns:
                    rc = _remote(ins[k].at[_half_idx(arrs[k].shape, split[k][1], c)], half(k, me, c), a_send.at[3 * k + j],
                                 a_recv.at[3 * k + j], (px, py, c))
                else:
                    rc = _remote(ins[k], outs[k].at[me], a_send.at[3 * k + j], a_recv.at[3 * k + j], (px, py, c))
                rc.start()
                sends.append(rc)
        for k in range(ns):
            for j, (px, py) in enumerate(peers):
                landed = half(k, 2 * px + py, c)
                _remote(landed, landed, a_send.at[3 * k + j], a_recv.at[3 * k + j], (px, py, c)).wait_recv()
                fw = _remote(landed, landed, b_send.at[3 * k + j], b_recv.at[3 * k + j], sib)
                fw.start()
                sends.append(fw)
        for k in range(ns, n):
            for j, (px, py) in enumerate(peers):
                landed = outs[k].at[2 * px + py]
                _remote(landed, landed, a_send.at[3 * k + j], a_recv.at[3 * k + j], (px, py, c)).wait_recv()
        for k in range(ns):
            for j, (px, py) in enumerate(peers):
                passed = half(k, 2 * px + py, 1 - c)
                _remote(passed, passed, b_send.at[3 * k + j], b_recv.at[3 * k + j], sib).wait_recv()
        for rc in sends:
            rc.wait_send()

    outs = _pcall(
        body, name=name, in_specs=[ANY] * n, out_specs=[ANY] * n,
        out_shape=[SDS((4,) + a.shape, a.dtype) for a in arrs],
        scratch_shapes=[pltpu.SemaphoreType.DMA((3 * n,)), pltpu.SemaphoreType.DMA((3 * n,)), pltpu.SemaphoreType.DMA((3 * ns,)),
                        pltpu.SemaphoreType.DMA((3 * ns,))],
    )(*arrs)
    me_xy = 2 * lax.axis_index("x") + lax.axis_index("y")
    return [_own(o, a, me_xy) for o, a in zip(outs, arrs)]


def _gather_all(a, swap, name):
    masks = [(mx, my, mc) for mx in range(2) for my in range(2) for mc in range(2)][1:]
    n = len(swap)

    def body(*refs):
        in_ref, sw_in = refs[0], refs[1:1 + n]
        out_ref, sw_out = refs[1 + n], refs[2 + n:2 + 2 * n]
        send_sems, recv_sems = refs[2 + 2 * n:]
        x, y, c = _coords()
        me = 4 * x + 2 * y + c
        sends = []
        for j, (mx, my, mc) in enumerate(masks):
            rc = _remote(in_ref, out_ref.at[me], send_sems.at[j], recv_sems.at[j], (_flip(x, mx), _flip(y, my), _flip(c, mc)))
            rc.start()
            sends.append(rc)
        for k in range(n):
            rc = _remote(sw_in[k], sw_out[k], send_sems.at[7 + k], recv_sems.at[7 + k], (x, y, 1 - c))
            rc.start()
            sends.append(rc)
        for j, (mx, my, mc) in enumerate(masks):
            px, py, pc = _flip(x, mx), _flip(y, my), _flip(c, mc)
            landed = out_ref.at[4 * px + 2 * py + pc]
            _remote(landed, landed, send_sems.at[j], recv_sems.at[j], (px, py, pc)).wait_recv()
        for k in range(n):
            _remote(sw_out[k], sw_out[k], send_sems.at[7 + k], recv_sems.at[7 + k], (x, y, 1 - c)).wait_recv()
        for rc in sends:
            rc.wait_send()

    res = _pcall(
        body, name=name, in_specs=[ANY] * (1 + n), out_specs=[ANY] * (1 + n),
        out_shape=[SDS((8,) + a.shape, a.dtype)] + [SDS(s_.shape, s_.dtype) for s_ in swap],
        scratch_shapes=[pltpu.SemaphoreType.DMA((7 + n,)), pltpu.SemaphoreType.DMA((7 + n,))],
    )(a, *swap)
    return _own(res[0], a, 4 * lax.axis_index("x") + 2 * lax.axis_index("y") + lax.axis_index("c")), list(res[1:])


def _half_shape(shape, axis):
    return tuple(d // 2 if i == axis else d for i, d in enumerate(shape))


def _swap_half_c(arrs, axes, name):
    n = len(arrs)

    def body(*refs):
        ins, outs = refs[:n], refs[n:2 * n]
        send_sems, recv_sems = refs[2 * n:]
        x, y, c = _coords()
        sends = []
        for k in range(n):
            rc = _remote(ins[k].at[_half_idx(arrs[k].shape, axes[k], 1 - c)], outs[k], send_sems.at[k], recv_sems.at[k],
                         (x, y, 1 - c))
            rc.start()
            sends.append(rc)
        for rc in sends:
            rc.wait()

    return _pcall(
        body, name=name, in_specs=[ANY] * n, out_specs=[ANY] * n,
        out_shape=[SDS(_half_shape(a.shape, ax), a.dtype) for a, ax in zip(arrs, axes)],
        scratch_shapes=[pltpu.SemaphoreType.DMA((n,)), pltpu.SemaphoreType.DMA((n,))],
    )(*arrs)


def _pair_sum(a, got, cidx, axis, name):
    _, r, cdim = a.shape
    hshape = _half_shape(a.shape, axis)

    def body(c_ref, a_ref, g_ref, o_ref):
        o_ref[...] = (a_ref[...] + g_ref[...]).astype(BF16)

    if axis == 1:
        tr = min(r // 2, 256)
        nj = (r // 2) // tr
        blk = pl.BlockSpec((1, tr, cdim), lambda s, j, c: (s, j, 0))
        a_spec = pl.BlockSpec((1, tr, cdim), lambda s, j, c: (s, c[0] * nj + j, 0))
    else:
        nj = (cdim // 2) // 128
        blk = pl.BlockSpec((1, r, 128), lambda s, j, c: (s, 0, j))
        a_spec = pl.BlockSpec((1, r, 128), lambda s, j, c: (s, 0, c[0] * nj + j))
    return _pcall(
        body, name=name, out_shape=SDS(hshape, BF16),
        grid_spec=pltpu.PrefetchScalarGridSpec(num_scalar_prefetch=1, grid=(4, nj), in_specs=[a_spec, blk], out_specs=blk),
        compiler_params=_cp(("parallel", "parallel")),
    )(cidx, a, got)


def _sum_chips(parts, name):
    _, h, cdim = parts.shape

    def body(p_ref, o_ref):
        acc = p_ref[0].astype(F32)
        for k in range(1, 4):
            acc = acc + p_ref[k].astype(F32)
        o_ref[...] = acc

    if h % 256 == 0 or h in (128,):
        tr = min(h, 256)
        grid, in_spec, out_spec = (h // tr,), pl.BlockSpec((4, tr, cdim), lambda i: (0, i, 0)), pl.BlockSpec((tr, cdim), lambda i: (i, 0))
    else:
        grid, in_spec, out_spec = (cdim // 128,), pl.BlockSpec((4, h, 128), lambda i: (0, 0, i)), pl.BlockSpec((h, 128), lambda i: (0, i))
    return _pcall(
        body, name=name, grid=grid, in_specs=[in_spec], out_specs=out_spec, out_shape=SDS((h, cdim), F32),
        compiler_params=_cp(("parallel",)),
    )(parts)


def _sum_slots(a, name, rows):
    s, n, _ = a.shape

    def body(a_ref, o_ref):
        acc = a_ref[0]
        for k in range(1, s):
            acc = acc + a_ref[k]
        o_ref[...] = acc

    return _pcall(
        body, name=name, grid=(n // rows,), in_specs=[pl.BlockSpec((s, rows, 128), lambda i: (0, i, 0))],
        out_specs=pl.BlockSpec((rows, 128), lambda i: (i, 0)), out_shape=SDS((n, 128), F32),
        compiler_params=_cp(("parallel",)),
    )(a)


def _adam_math(w, g, m, v):
    nm = ADAM_B1 * m + (1.0 - ADAM_B1) * g
    nv = ADAM_B2 * v + (1.0 - ADAM_B2) * (g * g)
    m_hat = nm / (1.0 - ADAM_B1 ** ADAM_STEP)
    v_hat = nv / (1.0 - ADAM_B2 ** ADAM_STEP)
    return -ADAM_LR * (m_hat / (jnp.sqrt(v_hat) + ADAM_EPS) + ADAM_WD * w), nm, nv


def _adamw(w, g, m, v, name, rows):
    r, cdim = w.shape

    def body(w_ref, g_ref, m_ref, v_ref, d_ref, nm_ref, nv_ref):
        d_ref[...], nm_ref[...], nv_ref[...] = _adam_math(w_ref[...], g_ref[...], m_ref[...], v_ref[...])

    blk = pl.BlockSpec((rows, cdim), lambda i: (i, 0))
    return _pcall(
        body, name=name, grid=(r // rows,), in_specs=[blk] * 4, out_specs=[blk] * 3,
        out_shape=[SDS(w.shape, F32)] * 3, compiler_params=_cp(("parallel",)),
    )(w, g, m, v)


def _adamw_joined(w, mine, other, m, v, cidx, axis, name, rows):
    r, cdim = w.shape
    if axis == 0:
        rows = r

    def body(c_ref, w_ref, a_ref, b_ref, m_ref, v_ref, g_ref, d_ref, nm_ref, nv_ref):
        a, b = a_ref[...], b_ref[...]
        g = jnp.where(c_ref[0] == 0, jnp.concatenate([a, b], axis=axis), jnp.concatenate([b, a], axis=axis))
        g_ref[...] = g
        d_ref[...], nm_ref[...], nv_ref[...] = _adam_math(w_ref[...], g, m_ref[...], v_ref[...])

    blk = pl.BlockSpec((rows, cdim), lambda i, c: (i, 0))
    hshape = (rows // 2, cdim) if axis == 0 else (rows, cdim // 2)
    hblk = pl.BlockSpec(hshape, lambda i, c: (i, 0))
    return _pcall(
        body, name=name, out_shape=[SDS(w.shape, F32)] * 4,
        grid_spec=pltpu.PrefetchScalarGridSpec(num_scalar_prefetch=1, grid=(r // rows,), in_specs=[blk, hblk, hblk, blk, blk],
                                               out_specs=[blk] * 4),
        compiler_params=_cp(("parallel",)),
    )(cidx, w, mine, other, m, v)


def _adamw_many(ws, gs, ms, vs, name):
    n = len(ws)

    def body(*refs):
        outs = refs[4 * n:]
        for k in range(n):
            d, nm, nv = _adam_math(refs[k][...], refs[n + k][...], refs[2 * n + k][...], refs[3 * n + k][...])
            outs[k][...] = d
            outs[n + k][...] = nm
            outs[2 * n + k][...] = nv

    res = _pcall(body, name=name, out_shape=[SDS(w.shape, F32) for w in ws] * 3)(*ws, *gs, *ms, *vs)
    return res[:n], res[n:2 * n], res[2 * n:]


def _pack(pieces, rows):
    flat = jnp.concatenate([p.reshape(-1) for p in pieces])
    return jnp.pad(flat, (0, rows * 128 - flat.shape[0])).reshape(rows, 128)


def _unpack(buf, shapes):
    flat = buf.reshape(-1)
    out, off = [], 0
    for shp in shapes:
        size = 1
        for s in shp:
            size *= s
        out.append(flat[off:off + size].reshape(shp))
        off += size
    return out


LATE_ROWS = 32


def kernel(x, c, ctx, c_ctx, w_mod, b_mod, norm_g, w_in, a_ln_g, a_ln_b, a_ws, a_bs, b_gate_w2, b_gate_b, b_norm_g, w_proj_a, w_proj_b, w_out, final_norm_g, loss_target, m_c_ctx, m_w_mod, m_b_mod, m_norm_g, m_w_in, m_a_ln_g, m_a_ln_b, m_a_ws, m_a_bs, m_b_gate_w2, m_b_gate_b, m_b_norm_g, m_w_proj_a, m_w_proj_b, m_w_out, m_final_norm_g, v_c_ctx, v_w_mod, v_b_mod, v_norm_g, v_w_in, v_a_ln_g, v_a_ln_b, v_a_ws, v_a_bs, v_b_gate_w2, v_b_gate_b, v_b_norm_g, v_w_proj_a, v_w_proj_b, v_w_out, v_final_norm_g):
    xi, yi, ci = _coords()
    me_xy = 2 * xi + yi

    gate_pack = _pack([b_gate_w2[0], b_gate_b[0]], 24)
    w_in_t, m_w_in_t, v_w_in_t = (jnp.swapaxes(a[0], 0, 1) for a in (w_in, m_w_in, v_w_in))
    g_wit, g_wm, g_gate = _gather_weights([(w_in_t.astype(BF16), 1), (w_mod[0].astype(BF16), 0)], [gate_pack], "gather_weights")
    late_shards = (w_proj_a[0].astype(BF16), w_proj_b[0].astype(BF16), w_out[0].astype(BF16))
    wit_u = g_wit.reshape(4 * 1288, D)
    wit_g = wit_u[3104:5152]
    wit_r = jnp.concatenate([wit_u[1056:1568], wit_u[1568:2080], wit_u[2592:3104], wit_u[2080:2592], wit_u[0:1024]], axis=0)
    wlrt = jnp.pad(wit_u[1024:1056], ((0, LRW - 32), (0, 0)))
    wm = jnp.swapaxes(g_wm, 0, 1).reshape(D, 3 * D)
    gflat = g_gate.reshape(4, 24 * 128)
    w2 = jnp.swapaxes(gflat[:, 0:2048].reshape(4, 2, 16, 64), 0, 2)
    w2 = jnp.swapaxes(w2, 0, 1).reshape(2, 16, 256)
    gb2 = jnp.swapaxes(gflat[:, 2048:2176].reshape(4, 2, 64), 0, 1).reshape(2, 256)

    tags = ["wi", "wpa", "wpb", "wo"]
    half_axes = [2, 1, 1, 1]
    sent = []

    def exchange(g):
        dwr = g["dwit_r"]
        dwit_u = jnp.concatenate([g["dwit_qkv"], g["dwlrt"][0:32], dwr[0:512], dwr[512:1024], dwr[1536:2048], dwr[1024:1536],
                                  g["dwit_g"]], axis=0)
        big = [dwit_u.reshape(4, 1288, D), jnp.swapaxes(g["dwpa"].reshape(512, 4, 256), 0, 1),
               jnp.swapaxes(g["dwpb"].reshape(512, 4, 256), 0, 1), g["dwo"].reshape(4, 256, D)]
        other = _swap_half_c(big, half_axes, "swap_half_in")
        cidx = jnp.reshape(ci, (1,)).astype(jnp.int32)
        sent.extend(_pair_sum(a, o, cidx, ax, "sum_pair_" + t) for a, o, ax, t in zip(big, other, half_axes, tags))
        return sent

    r = _device_step(x[0], c, ctx[0], c_ctx[None], loss_target[0], wm, b_mod, norm_g, wit_g, wit_r, wlrt, a_ln_g, a_ln_b,
                     a_ws[0], a_bs[0], w2, gb2, b_norm_g, None, None, None, final_norm_g[None], exchange, late_shards)

    parts = [_own(g, lax.dynamic_index_in_dim(s_, me_xy, axis=0, keepdims=False), me_xy) for g, s_ in zip(r["got"], sent)]
    halves = [_sum_chips(p_, "sum_chips_" + t) for p_, t in zip(parts, tags)]

    me8 = 4 * xi + 2 * yi + ci
    early_all = _own(r["early_all"], r["early"], me8)
    late = _pack([r["dshift"], r["dscale"], r["dng_lat"], c], LATE_ROWS)
    late_all, others = _gather_all(late, halves, "gather_small")
    s_early = _sum_slots(early_all, "sum_early", EARLY_ROWS // 3)
    s_late = _sum_slots(late_all, "sum_late", LATE_ROWS)
    (s_dmodc, s_dscc, s_dng_c, s_dlng, s_dlnb, s_dws, s_dbs, s_dgbn, s_dgf, s_dw2, s_dgb2, s_loss, s_dgate) = _unpack(
        s_early, EARLY_SHAPES)
    s_dshift, s_dscale, s_dng_lat, _ = _unpack(s_late, [(1, D)] * 4)
    s_dng = s_dng_lat + s_dng_c
    s_dmod = jnp.concatenate([s_dshift, s_dscale, s_dgate], axis=1)
    loss = s_loss[0]
    s_dmodc_p = jnp.pad(s_dmodc, ((0, 0), (0, D)))
    g_b_mod = s_dmod + s_dmodc_p
    sg = jax.nn.sigmoid(c_ctx)
    g_c_ctx = s_dscc * (sg * (1.0 + c_ctx * (1.0 - sg)))
    g_w2 = lax.dynamic_slice_in_dim(s_dw2, 64 * me_xy, 64, axis=2)[None]
    g_gb2 = lax.dynamic_slice_in_dim(s_dgb2, 64 * me_xy, 64, axis=1)[None]

    flat_l = late_all.reshape(8, LATE_ROWS * 128)
    dgate_all = early_all.reshape(8, EARLY_ROWS * 128)[:, EARLY_SIZE - D:EARLY_SIZE]
    dmod_all = jnp.concatenate([flat_l[:, 0:2 * D], dgate_all], axis=1)
    c_all = flat_l[:, 3 * D:4 * D]
    lhs = jnp.concatenate([_silu(c_all), _silu(c_ctx)[None], jnp.zeros((7, D), F32)], axis=0)
    rhs = jnp.concatenate([dmod_all, s_dmodc_p, jnp.zeros((7, 3 * D), F32)], axis=0)
    rhs = lax.dynamic_slice_in_dim(rhs, 768 * me_xy, 768, axis=1)
    g_w_mod = _mm(lhs.T.astype(BF16), rhs.astype(BF16), tm=D, tn=768, tk=16, out_dtype=F32, name="mm_dwm")

    cidx = jnp.reshape(ci, (1,)).astype(jnp.int32)
    g_w_in_t, d_w_in_t, nm_w_in_t, nv_w_in_t = _adamw_joined(w_in_t, halves[0], others[0], m_w_in_t, v_w_in_t, cidx, 1,
                                                             "adamw_w_in", 184)
    g_w_in, d_w_in, nm_w_in, nv_w_in = (jnp.swapaxes(a, 0, 1) for a in (g_w_in_t, d_w_in_t, nm_w_in_t, nv_w_in_t))
    g_wpa, d_wpa, nm_wpa, nv_wpa = _adamw_joined(w_proj_a[0], halves[1], others[1], m_w_proj_a[0], v_w_proj_a[0], cidx, 0,
                                                 "adamw_wpa", 0)
    g_wpb, d_wpb, nm_wpb, nv_wpb = _adamw_joined(w_proj_b[0], halves[2], others[2], m_w_proj_b[0], v_w_proj_b[0], cidx, 0,
                                                 "adamw_wpb", 0)
    g_wo, d_wo, nm_wo, nv_wo = _adamw_joined(w_out[0], halves[3], others[3], m_w_out[0], v_w_out[0], cidx, 0, "adamw_wo", 0)
    d_w_mod, nm_w_mod, nv_w_mod = _adamw(w_mod[0], g_w_mod, m_w_mod[0], v_w_mod[0], "adamw_w_mod", 256)

    names = ["c_ctx", "b_mod", "norm_g", "a_ln_g", "a_ln_b", "a_ws", "a_bs", "b_gate_w2", "b_gate_b", "b_norm_g", "final_norm_g"]
    ws_ = [c_ctx, b_mod, norm_g, a_ln_g, a_ln_b, a_ws, a_bs, b_gate_w2, b_gate_b, b_norm_g, final_norm_g]
    gs_ = [g_c_ctx, g_b_mod, s_dng, s_dlng, s_dlnb, s_dws, s_dbs, g_w2, g_gb2, s_dgbn, s_dgf]
    ms_ = [m_c_ctx, m_b_mod, m_norm_g, m_a_ln_g, m_a_ln_b, m_a_ws, m_a_bs, m_b_gate_w2, m_b_gate_b, m_b_norm_g, m_final_norm_g]
    vs_ = [v_c_ctx, v_b_mod, v_norm_g, v_a_ln_g, v_a_ln_b, v_a_ws, v_a_bs, v_b_gate_w2, v_b_gate_b, v_b_norm_g, v_final_norm_g]
    shapes = [w.shape for w in ws_]
    flat2 = [(1, 1024), (1, 3072), (1, 1024), (1, 512), (1, 512), (512, 128), (4, 128), (32, 64), (2, 64), (1, 512), (1, 1024)]
    as2d = lambda arrs: [a.reshape(s) for a, s in zip(arrs, flat2)]
    d_s, nm_s, nv_s = _adamw_many(as2d(ws_), as2d(gs_), as2d(ms_), as2d(vs_), "adamw_small")
    d_small = {n: a.reshape(s) for n, a, s in zip(names, d_s, shapes)}
    nm_small = {n: a.reshape(s) for n, a, s in zip(names, nm_s, shapes)}
    nv_small = {n: a.reshape(s) for n, a, s in zip(names, nv_s, shapes)}
    g_small = {n: g.reshape(s) for n, g, s in zip(names, gs_, shapes)}

    order = ["c_ctx", "w_mod", "b_mod", "norm_g", "w_in", "a_ln_g", "a_ln_b", "a_ws", "a_bs", "b_gate_w2", "b_gate_b", "b_norm_g",
             "w_proj_a", "w_proj_b", "w_out", "final_norm_g"]
    big_g = dict(w_mod=g_w_mod[None], w_in=g_w_in[None], w_proj_a=g_wpa[None], w_proj_b=g_wpb[None], w_out=g_wo[None])
    big_d = dict(w_mod=d_w_mod[None], w_in=d_w_in[None], w_proj_a=d_wpa[None], w_proj_b=d_wpb[None], w_out=d_wo[None])
    big_m = dict(w_mod=nm_w_mod[None], w_in=nm_w_in[None], w_proj_a=nm_wpa[None], w_proj_b=nm_wpb[None], w_out=nm_wo[None])
    big_v = dict(w_mod=nv_w_mod[None], w_in=nv_w_in[None], w_proj_a=nv_wpa[None], w_proj_b=nv_wpb[None], w_out=nv_wo[None])
    grads = [big_g[n] if n in big_g else g_small[n] for n in order]
    deltas = [big_d[n] if n in big_d else d_small[n] for n in order]
    new_m = [big_m[n] if n in big_m else nm_small[n] for n in order]
    new_v = [big_v[n] if n in big_v else nv_small[n] for n in order]
    return (loss, r["dx"][None], *grads, *deltas, *new_m, *new_v)
```

```python
import jax
import jax.numpy as jnp
from jax import lax
from jax.experimental import pallas as pl
from jax.experimental.pallas import tpu as pltpu

F32 = jnp.float32
BF16 = jnp.bfloat16
SDS = jax.ShapeDtypeStruct

D = 1024
NP = 5120
LRW = 128
CH = 64
AC = 128
EPS = 1e-6
TOK = 512
GLA_TB = 1024
VMEM_BIG = 48 * 1024 * 1024

ADAM_LR, ADAM_B1, ADAM_B2, ADAM_EPS, ADAM_WD, ADAM_STEP = 0.001, 0.9, 0.999, 1e-08, 0.01, 10

_pcall = pl.pallas_call
MESH = pl.DeviceIdType.MESH


def _cp(sem=None, vmem=None):
    kw = {}
    if sem is not None:
        kw["dimension_semantics"] = sem
    if vmem is not None:
        kw["vmem_limit_bytes"] = vmem
    return pltpu.CompilerParams(**kw)


def _silu(x):
    return x * jax.nn.sigmoid(x)


def _silu_and_grad(x):
    s = jax.nn.sigmoid(x)
    return x * s, s * (1.0 + x * (1.0 - s))


def _logsig(x):
    return jnp.minimum(x, 0.0) - jnp.log1p(jnp.exp(-jnp.abs(x)))


def _nt(a, b):
    return lax.dot_general(a, b, (((1,), (1,)), ((), ())), preferred_element_type=F32)


def _tn(a, b):
    return lax.dot_general(a, b, (((0,), (0,)), ((), ())), preferred_element_type=F32)


def _nn(a, b):
    return jnp.dot(a, b, preferred_element_type=F32)


def _full(shape):
    return pl.BlockSpec(shape, lambda *_: (0,) * len(shape))


def _mm(a, b, *, tm, tn, tk, out_dtype, name, acc=None, n_outer=False, b_t=False):
    m, k = a.shape
    n, k2 = (b.shape if b_t else b.shape[::-1])
    assert k == k2 and m % tm == 0 and n % tn == 0 and k % tk == 0, (a.shape, b.shape, tm, tn, tk)
    nk = k // tk
    has_acc = acc is not None

    def body(*refs):
        if has_acc:
            a_ref, b_ref, c_ref, o_ref = refs[:4]
        else:
            a_ref, b_ref, o_ref = refs[:3]
        part = (_nt if b_t else _nn)(a_ref[...].astype(BF16), b_ref[...].astype(BF16))
        if nk == 1:
            o_ref[...] = ((c_ref[...] + part) if has_acc else part).astype(out_dtype)
            return
        acc_ref = refs[-1]
        kk = pl.program_id(2)

        @pl.when(kk == 0)
        def _():
            if has_acc:
                acc_ref[...] = c_ref[...] + part
            else:
                acc_ref[...] = part

        @pl.when(kk > 0)
        def _():
            acc_ref[...] += part

        @pl.when(kk == nk - 1)
        def _():
            o_ref[...] = acc_ref[...].astype(out_dtype)

    if n_outer:
        ij = lambda g0, g1: (g1, g0)
        grid = (n // tn, m // tm, nk)
    else:
        ij = lambda g0, g1: (g0, g1)
        grid = (m // tm, n // tn, nk)
    b_spec = (pl.BlockSpec((tn, tk), lambda g0, g1, kk: (ij(g0, g1)[1], kk)) if b_t
              else pl.BlockSpec((tk, tn), lambda g0, g1, kk: (kk, ij(g0, g1)[1])))
    in_specs = [pl.BlockSpec((tm, tk), lambda g0, g1, kk: (ij(g0, g1)[0], kk)), b_spec]
    args = [a, b]
    if has_acc:
        in_specs.append(pl.BlockSpec((tm, tn), lambda g0, g1, kk: ij(g0, g1)))
        args.append(acc)
    return _pcall(
        body, name=name, grid=grid, in_specs=in_specs,
        out_specs=pl.BlockSpec((tm, tn), lambda g0, g1, kk: ij(g0, g1)),
        out_shape=SDS((m, n), out_dtype), scratch_shapes=([pltpu.VMEM((tm, tn), F32)] if nk > 1 else []),
        compiler_params=_cp(("parallel", "parallel", "arbitrary"), VMEM_BIG),
    )(*args)


def _mm_tn(a, b, *, ta, tn, tk, name, acc=None):
    m, ka = a.shape
    m2, n = b.shape
    assert m == m2 and ka % ta == 0 and n % tn == 0 and m % tk == 0, (a.shape, b.shape, ta, tn, tk)
    nk = m // tk
    has_acc = acc is not None

    def body(*refs):
        if has_acc:
            a_ref, b_ref, c_ref, o_ref = refs
        else:
            a_ref, b_ref, o_ref = refs
        kk = pl.program_id(2)
        part = _tn(a_ref[...].astype(BF16), b_ref[...].astype(BF16))

        @pl.when(kk == 0)
        def _():
            if has_acc:
                o_ref[...] = c_ref[...] + part
            else:
                o_ref[...] = part

        @pl.when(kk > 0)
        def _():
            o_ref[...] += part

    in_specs = [pl.BlockSpec((tk, ta), lambda i, j, kk: (kk, i)), pl.BlockSpec((tk, tn), lambda i, j, kk: (kk, j))]
    args = [a, b]
    if has_acc:
        in_specs.append(pl.BlockSpec((ta, tn), lambda i, j, kk: (i, j)))
        args.append(acc)
    return _pcall(
        body, name=name, grid=(ka // ta, n // tn, nk), in_specs=in_specs,
        out_specs=pl.BlockSpec((ta, tn), lambda i, j, kk: (i, j)), out_shape=SDS((ka, n), F32),
        compiler_params=_cp(("parallel", "parallel", "arbitrary"), VMEM_BIG),
    )(*args)


def _modvec(cc, wm, bm):
    def body(c_ref, w_ref, b_ref, o_ref):
        o_ref[...] = _nn(_silu(c_ref[...]).astype(BF16), w_ref[...]) + b_ref[...]

    return _pcall(body, name="modvec", out_shape=SDS((8, 3 * D), F32), compiler_params=_cp(None, VMEM_BIG))(cc, wm, bm)


def _dcctx(dmodc, wm):
    def body(d_ref, w_ref, o_ref):
        o_ref[...] = _nt(d_ref[...].astype(BF16), w_ref[...])

    return _pcall(
        body, name="dcctx", grid=(1,), in_specs=[_full((8, 2 * D)), pl.BlockSpec((D, 2 * D), lambda i: (0, 0))],
        out_specs=_full((8, D)), out_shape=SDS((8, D), F32), compiler_params=_cp(("arbitrary",), VMEM_BIG),
    )(dmodc, wm)


def _prep_h(x, ng, scale, shift, name):
    m = x.shape[0]

    def body(x_ref, g_ref, sc_ref, sh_ref, h_ref):
        xf = x_ref[...]
        r = lax.rsqrt(jnp.mean(xf * xf, axis=-1, keepdims=True) + EPS)
        y = (xf * r) * g_ref[...]
        h_ref[...] = (y * (1.0 + sc_ref[...]) + sh_ref[...]).astype(BF16)

    tok = min(TOK, m)
    row = pl.BlockSpec((tok, D), lambda i: (i, 0))
    return _pcall(
        body, name=name, grid=(m // tok,), in_specs=[row, _full((1, D)), _full((1, D)), _full((1, D))],
        out_specs=row, out_shape=SDS((m, D), BF16), compiler_params=_cp(("parallel",)),
    )(x, ng, scale, shift)


def _resident(shape):
    return pl.BlockSpec(shape, lambda *_: (0,) * len(shape), pipeline_mode=pl.Buffered(1))


PROJ_TM = 512


def _proj_fwd(x, ng, scale, shift, wit_g, wit_r, wlrt, ln_g, ln_b, share=()):
    m = x.shape[0]
    ns = len(share)
    steps = m // PROJ_TM
    src = [(0, 0), (0, D), (1, 2 * D), (1, 0), (1, D)]

    def body(*refs):
        x_ref, g_ref, sc_ref, sh_ref, wg_ref, wr_ref, wl_ref, lg_ref, lb_ref = refs[:9]
        share_refs = refs[9:9 + ns]
        h_ref, p_ref, plr_ref, vr_ref, vc_ref = refs[9 + ns:14 + ns]
        got_refs = refs[14 + ns:14 + 2 * ns]
        sems = refs[14 + 2 * ns:]

        def copies():
            cx, cy, cc = _coords()
            me = 2 * cx + cy
            peers = [(1 - cx, cy), (cx, 1 - cy), (1 - cx, 1 - cy)]
            out, back = [], []
            for k in range(ns):
                for j, (px, py) in enumerate(peers):
                    out.append(_remote(share_refs[k], got_refs[k].at[me], sems[0].at[3 * k + j], sems[1].at[3 * k + j], (px, py, cc)))
                    landed = got_refs[k].at[2 * px + py]
                    back.append(_remote(landed, landed, sems[0].at[3 * k + j], sems[1].at[3 * k + j], (px, py, cc)))
            return out, back

        if ns:
            @pl.when(pl.program_id(0) == 0)
            def _():
                for rc in copies()[0]:
                    rc.start()

            @pl.when(pl.program_id(0) == steps - 1)
            def _():
                out, back = copies()
                for rc in back:
                    rc.wait_recv()
                for rc in out:
                    rc.wait_send()

        xf = x_ref[...]
        r = lax.rsqrt(jnp.mean(xf * xf, axis=-1, keepdims=True) + EPS)
        y = (xf * r) * g_ref[...]
        h = (y * (1.0 + sc_ref[...]) + sh_ref[...]).astype(BF16)
        h_ref[...] = h
        for j, (which, r0) in enumerate(src):
            w_ref = wr_ref if which else wg_ref
            blk = _nt(h, w_ref[r0:r0 + D, :]).astype(BF16)
            p_ref[:, D * j:D * j + D] = blk
            if j == 4:
                xf = blk[:, 512:1024].astype(F32)
                xc = xf - jnp.mean(xf, axis=-1, keepdims=True)
                vn = (xc * lax.rsqrt(jnp.mean(xc * xc, axis=-1, keepdims=True) + EPS)) * lg_ref[...] + lb_ref[...]
                vr_ref[...] = vn[:, 0:256].astype(BF16)
                vc_ref[0] = vn[:, 256:384].astype(BF16)
                vc_ref[1] = vn[:, 384:512].astype(BF16)
        plr_ref[...] = _nt(h, wl_ref[...])

    row = pl.BlockSpec((PROJ_TM, D), lambda i: (i, 0))
    vec = _full((1, D))
    res = _pcall(
        body, name="proj_fwd", grid=(steps,),
        in_specs=[row, vec, vec, vec, _resident((2 * D, D)), _resident((3 * D, D)), _resident((LRW, D)), _full((1, 512)),
                  _full((1, 512))] + [ANY] * ns,
        out_specs=[row, pl.BlockSpec((PROJ_TM, NP), lambda i: (i, 0)), pl.BlockSpec((PROJ_TM, LRW), lambda i: (i, 0)),
                   pl.BlockSpec((PROJ_TM, 256), lambda i: (i, 0)), pl.BlockSpec((2, PROJ_TM, 128), lambda i: (0, i, 0))] + [ANY] * ns,
        out_shape=[SDS((m, D), BF16), SDS((m, NP), BF16), SDS((m, LRW), F32), SDS((m, 256), BF16), SDS((2, m, 128), BF16)]
        + [SDS((4,) + a.shape, a.dtype) for a in share],
        scratch_shapes=([pltpu.SemaphoreType.DMA((3 * ns,)), pltpu.SemaphoreType.DMA((3 * ns,))] if ns else []),
        compiler_params=_cp(("arbitrary",), VMEM_BIG),
    )(x, ng, scale, shift, wit_g, wit_r, wlrt, ln_g, ln_b, *share)
    return res[0], res[1], res[2], res[3], res[4], list(res[5:])


def _proj_bwd(dp_g, dp_r, dlr, wit_g, wit_r, wlrt, x, dx1, ng, scale, send=(), share8=None):
    m = x.shape[0]
    ns = len(send)
    n8 = 0 if share8 is None else 1
    steps = m // PROJ_TM
    masks = [(mx, my, mc) for mx in range(2) for my in range(2) for mc in range(2)][1:]

    def body(*refs):
        (dpg_ref, dpr_ref, dlr_ref, wg_ref, wr_ref, wl_ref, x_ref, r_ref, g_ref, sc_ref) = refs[:10]
        send_refs = refs[10:10 + ns]
        n_in = 10 + ns + n8
        dx_ref, dg_ref, dsc_ref, dsh_ref = refs[n_in:n_in + 4]
        got_refs = refs[n_in + 4:n_in + 4 + ns]
        sems = refs[n_in + 4 + ns + n8:]
        i = pl.program_id(0)

        def copies():
            cx, cy, cc = _coords()
            me = 2 * cx + cy
            peers = [(1 - cx, cy), (cx, 1 - cy), (1 - cx, 1 - cy)]
            out, back = [], []
            for k in range(ns):
                for j, (px, py) in enumerate(peers):
                    out.append(_remote(send_refs[k].at[2 * px + py], got_refs[k].at[me], sems[0].at[3 * k + j],
                                       sems[1].at[3 * k + j], (px, py, cc)))
                    landed = got_refs[k].at[2 * px + py]
                    back.append(_remote(landed, landed, sems[0].at[3 * k + j], sems[1].at[3 * k + j], (px, py, cc)))
            if n8:
                src8, all8 = refs[10 + ns], refs[n_in + 4 + ns]
                s8, r8 = sems[-2], sems[-1]
                for j, (mx, my, mc) in enumerate(masks):
                    px, py, pc = _flip(cx, mx), _flip(cy, my), _flip(cc, mc)
                    out.append(_remote(src8, all8.at[4 * cx + 2 * cy + cc], s8.at[j], r8.at[j], (px, py, pc)))
                    landed = all8.at[4 * px + 2 * py + pc]
                    back.append(_remote(landed, landed, s8.at[j], r8.at[j], (px, py, pc)))
            return out, back

        @pl.when(i == 0)
        def _():
            dg_ref[...] = jnp.zeros_like(dg_ref)
            dsc_ref[...] = jnp.zeros_like(dsc_ref)
            dsh_ref[...] = jnp.zeros_like(dsh_ref)
            if ns or n8:
                for rc in copies()[0]:
                    rc.start()

        dh_ = (_nn(dpg_ref[...], wg_ref[...]) + _nn(dpr_ref[...], wr_ref[...])
               + _nn(dlr_ref[...].astype(BF16), wl_ref[...]))
        xf = x_ref[...]
        r = lax.rsqrt(jnp.mean(xf * xf, axis=-1, keepdims=True) + EPS)
        xh = xf * r
        y = xh * g_ref[...]
        dsh_ref[...] += jnp.sum(dh_, axis=0, keepdims=True)
        dsc_ref[...] += jnp.sum(dh_ * y, axis=0, keepdims=True)
        dy = dh_ * (1.0 + sc_ref[...])
        dg_ref[...] += jnp.sum(dy * xh, axis=0, keepdims=True)
        dxh = dy * g_ref[...]
        dx_ref[...] = r * (dxh - xh * jnp.mean(dxh * xh, axis=-1, keepdims=True)) + r_ref[...]

        if ns or n8:
            @pl.when(i == steps - 1)
            def _():
                out, back = copies()
                for rc in back:
                    rc.wait_recv()
                for rc in out:
                    rc.wait_send()

    row = pl.BlockSpec((PROJ_TM, D), lambda i: (i, 0))
    vec = _full((1, D))
    kg, kr = dp_g.shape[1], dp_r.shape[1]
    extra_in = list(send) + ([share8] if n8 else [])
    extra_out = [SDS(a.shape, a.dtype) for a in send] + ([SDS((8,) + share8.shape, share8.dtype)] if n8 else [])
    res = _pcall(
        body, name="proj_bwd", grid=(steps,),
        in_specs=[pl.BlockSpec((PROJ_TM, kg), lambda i: (i, 0)), pl.BlockSpec((PROJ_TM, kr), lambda i: (i, 0)),
                  pl.BlockSpec((PROJ_TM, LRW), lambda i: (i, 0)), _resident((kg, D)), _resident((kr, D)), _resident((LRW, D)),
                  row, row, vec, vec] + [ANY] * len(extra_in),
        out_specs=[row, vec, vec, vec] + [ANY] * len(extra_out),
        out_shape=[SDS((m, D), F32), SDS((1, D), F32), SDS((1, D), F32), SDS((1, D), F32)] + extra_out,
        scratch_shapes=(([pltpu.SemaphoreType.DMA((3 * ns,)), pltpu.SemaphoreType.DMA((3 * ns,))] if ns else [])
                        + ([pltpu.SemaphoreType.DMA((7,)), pltpu.SemaphoreType.DMA((7,))] if n8 else [])),
        compiler_params=_cp(("arbitrary",), VMEM_BIG),
    )(dp_g, dp_r, dlr, wit_g, wit_r, wlrt, x, dx1, ng, scale, *extra_in)
    return tuple(res[:4]), list(res[4:4 + ns]), (res[4 + ns] if n8 else None)


def _prep_bwd(x, dh, dx1, ng, scale, name):
    m = x.shape[0]
    has_res = dx1 is not None

    def body(*refs):
        if has_res:
            x_ref, dh_ref, r_ref, g_ref, sc_ref, dx_ref, dg_ref, dsc_ref, dsh_ref = refs
        else:
            x_ref, dh_ref, g_ref, sc_ref, dx_ref, dg_ref, dsc_ref, dsh_ref = refs
        i = pl.program_id(0)

        @pl.when(i == 0)
        def _():
            dg_ref[...] = jnp.zeros_like(dg_ref)
            dsc_ref[...] = jnp.zeros_like(dsc_ref)
            dsh_ref[...] = jnp.zeros_like(dsh_ref)

        xf = x_ref[...]
        dh_ = dh_ref[...]
        r = lax.rsqrt(jnp.mean(xf * xf, axis=-1, keepdims=True) + EPS)
        xh = xf * r
        y = xh * g_ref[...]
        dsh_ref[...] += jnp.sum(dh_, axis=0, keepdims=True)
        dsc_ref[...] += jnp.sum(dh_ * y, axis=0, keepdims=True)
        dy = dh_ * (1.0 + sc_ref[...])
        dg_ref[...] += jnp.sum(dy * xh, axis=0, keepdims=True)
        dxh = dy * g_ref[...]
        dx = r * (dxh - xh * jnp.mean(dxh * xh, axis=-1, keepdims=True))
        if has_res:
            dx = dx + r_ref[...]
        dx_ref[...] = dx

    tok = min(TOK, m)
    row = pl.BlockSpec((tok, D), lambda i: (i, 0))
    vec = _full((1, D))
    in_specs = [row, row] + ([row] if has_res else []) + [vec, vec]
    args = [x, dh] + ([dx1] if has_res else []) + [ng, scale]
    return _pcall(
        body, name=name, grid=(m // tok,), in_specs=in_specs, out_specs=[row, vec, vec, vec],
        out_shape=[SDS((m, D), F32), SDS((1, D), F32), SDS((1, D), F32), SDS((1, D), F32)],
        compiler_params=_cp(("arbitrary",)),
    )(*args)


COLB = 2048


def _colmix_fwd(vnc, ws23, bs23):
    rows = vnc.shape[2] // COLB

    def body(v_ref, w_ref, b_ref, o_ref):
        o_ref[0] = _nn(w_ref[0], v_ref[0]) + b_ref[0]

    return _pcall(
        body, name="colmix_fwd", grid=(2, rows),
        in_specs=[pl.BlockSpec((1, AC, COLB), lambda g, j: (g, 0, j)), pl.BlockSpec((1, AC, AC), lambda g, j: (g, 0, 0)),
                  pl.BlockSpec((1, AC, 1), lambda g, j: (g, 0, 0))],
        out_specs=pl.BlockSpec((1, AC, COLB), lambda g, j: (g, 0, j)),
        out_shape=SDS(vnc.shape, F32), compiler_params=_cp(("parallel", "parallel")),
    )(vnc, ws23, bs23)


def _colmix_bwd(dsvc, vnc, ws23t):
    rows = vnc.shape[2] // COLB

    def body(d_ref, v_ref, wt_ref, dv_ref, dw_ref, db_ref):
        j = pl.program_id(1)

        @pl.when(j == 0)
        def _():
            dw_ref[...] = jnp.zeros_like(dw_ref)
            db_ref[...] = jnp.zeros_like(db_ref)

        d = d_ref[0]
        d16 = d.astype(BF16)
        dv_ref[0] = _nn(wt_ref[0], d16)
        dw_ref[0] += _nt(d16, v_ref[0])
        db_ref[0] += jnp.sum(d, axis=1, keepdims=True)

    blk = pl.BlockSpec((1, AC, COLB), lambda g, j: (g, 0, j))
    return _pcall(
        body, name="colmix_bwd", grid=(2, rows),
        in_specs=[blk, blk, pl.BlockSpec((1, AC, AC), lambda g, j: (g, 0, 0))],
        out_specs=[blk, pl.BlockSpec((1, AC, AC), lambda g, j: (g, 0, 0)), pl.BlockSpec((1, AC, 1), lambda g, j: (g, 0, 0))],
        out_shape=[SDS(vnc.shape, F32), SDS((2, AC, AC), F32), SDS((2, AC, 1), F32)],
        compiler_params=_cp(("parallel", "arbitrary")),
    )(dsvc, vnc, ws23t)


def _head_norm(o, gbn):
    out = []
    for h in range(4):
        oh = o[:, 128 * h:128 * h + 128]
        r = lax.rsqrt(jnp.mean(oh * oh, axis=-1, keepdims=True) + EPS)
        out.append((r, oh * r))
    return out


def _tail_fwd(o_f, o_b, p, vnr, svc, x, tgt, ws01, bs01, gbn, wpa, wpb, wo, gate, gf):
    m = p.shape[0]

    def body(of_ref, ob_ref, zb_ref, ua_ref, za_ref, ga_ref, gb_ref, vnr_ref, svc_ref, x_ref, t_ref, w_ref, b_ref, g_ref,
             wpa_ref, wpb_ref, wo_ref, gate_ref, gf_ref,
             ya_ref, yb_ref, svr_ref, dwo_ref, dx1_ref, dout_ref, loss_ref, dgate_ref, dgf_ref):
        i = pl.program_id(0)

        @pl.when(i == 0)
        def _():
            loss_ref[...] = jnp.zeros_like(loss_ref)
            dgate_ref[...] = jnp.zeros_like(dgate_ref)
            dgf_ref[...] = jnp.zeros_like(dgf_ref)
            dwo_ref[...] = jnp.zeros_like(dwo_ref)

        o = of_ref[...] + ob_ref[...]
        zb = zb_ref[...].astype(F32)
        for h, (r, xh) in enumerate(_head_norm(o, None)):
            sl = slice(128 * h, 128 * h + 128)
            yb_ref[:, sl] = ((xh * g_ref[:, sl]) * _silu(zb[:, sl])).astype(BF16)
        for j in range(TOK // AC):
            for g in range(2):
                sv = _nn(w_ref[g], vnr_ref[AC * j:AC * j + AC, AC * g:AC * g + AC]) + b_ref[g]
                svr_ref[AC * j:AC * j + AC, AC * g:AC * g + AC] = sv
        sz = _silu(za_ref[...].astype(F32))
        u = ua_ref[...].astype(F32)
        ya_ref[:, 0:256] = ((u[:, 0:256] * svr_ref[...]) * sz[:, 0:256]).astype(BF16)
        ya_ref[:, 256:384] = ((u[:, 256:384] * svc_ref[0]) * sz[:, 256:384]).astype(BF16)
        ya_ref[:, 384:512] = ((u[:, 384:512] * svc_ref[1]) * sz[:, 384:512]).astype(BF16)
        ya = _nn(ya_ref[...], wpa_ref[...])
        yb = _nn(yb_ref[...], wpb_ref[...])
        mg = (jax.nn.sigmoid(ga_ref[...].astype(F32)) * ya + jax.nn.sigmoid(gb_ref[...].astype(F32)) * yb).astype(BF16)
        out_ = _nn(mg, wo_ref[...])
        x1 = x_ref[...] + gate_ref[...] * out_
        r = lax.rsqrt(jnp.mean(x1 * x1, axis=-1, keepdims=True) + EPS)
        xh = x1 * r
        err = xh * gf_ref[...] - t_ref[...]
        loss_ref[...] += 0.5 * jnp.sum(jnp.mean(err * err, axis=-1, keepdims=True), axis=0, keepdims=True)
        dy = err * (1.0 / D)
        dgf_ref[...] += jnp.sum(dy * xh, axis=0, keepdims=True)
        dxh = dy * gf_ref[...]
        dx1 = r * (dxh - xh * jnp.mean(dxh * xh, axis=-1, keepdims=True))
        dx1_ref[...] = dx1
        dout16 = (gate_ref[...] * dx1).astype(BF16)
        dout_ref[...] = dout16
        dgate_ref[...] += jnp.sum(dx1 * out_, axis=0, keepdims=True)
        dwo_ref[...] += _tn(mg, dout16)

    r512 = pl.BlockSpec((TOK, 512), lambda i: (i, 0))
    row = pl.BlockSpec((TOK, D), lambda i: (i, 0))
    vec = _full((1, D))
    return _pcall(
        body, name="tail_fwd", grid=(m // TOK,),
        in_specs=[r512, r512, pl.BlockSpec((TOK, 512), lambda i: (i, 6)), pl.BlockSpec((TOK, 512), lambda i: (i, 7)),
                  pl.BlockSpec((TOK, 512), lambda i: (i, 8)), row, pl.BlockSpec((TOK, D), lambda i: (i, 1)),
                  pl.BlockSpec((TOK, 256), lambda i: (i, 0)), pl.BlockSpec((2, TOK, 128), lambda i: (0, i, 0)), row, row,
                  _full((2, AC, AC)), _full((2, AC, 1)), _full((1, 512)), _resident((512, D)), _resident((512, D)),
                  _resident((D, D)), vec, vec],
        out_specs=[r512, r512, pl.BlockSpec((TOK, 256), lambda i: (i, 0)), _resident((D, D)), row, row, _full((1, 128)), vec, vec],
        out_shape=[SDS((m, 512), BF16), SDS((m, 512), BF16), SDS((m, 256), F32), SDS((D, D), F32), SDS((m, D), F32),
                   SDS((m, D), BF16), SDS((1, 128), F32), SDS((1, D), F32), SDS((1, D), F32)],
        compiler_params=_cp(("arbitrary",), VMEM_BIG),
    )(o_f, o_b, p, p, p, p, p, vnr, svc, x, tgt, ws01, bs01, gbn, wpa, wpb, wo, gate, gf)


DPR = 3072


def _tail_bwd(dout, ya_in, yb_in, p, svr, svc, o_f, o_b, gbn, wo, wpa, wpb):
    m = p.shape[0]

    def body(dout_ref, ya_ref, yb_ref, ga_ref, gb_ref, zb_ref, ua_ref, za_ref, svr_ref, svc_ref, of_ref, ob_ref, g_ref,
             wo_ref, wpa_ref, wpb_ref,
             dwpa_ref, dwpb_ref, dpg_ref, dpr_ref, dsr_ref, dsc_ref, do_ref, dg_ref):
        i = pl.program_id(0)

        @pl.when(i == 0)
        def _():
            dg_ref[...] = jnp.zeros_like(dg_ref)
            dwpa_ref[...] = jnp.zeros_like(dwpa_ref)
            dwpb_ref[...] = jnp.zeros_like(dwpb_ref)

        dm_ = _nt(dout_ref[...], wo_ref[...])
        ya_in, yb_in = ya_ref[...], yb_ref[...]
        ya = _nn(ya_in, wpa_ref[...])
        yb = _nn(yb_in, wpb_ref[...])
        sa = jax.nn.sigmoid(ga_ref[...].astype(F32))
        sb = jax.nn.sigmoid(gb_ref[...].astype(F32))
        dya16 = (dm_ * sa).astype(BF16)
        dyb16 = (dm_ * sb).astype(BF16)
        dwpa_ref[...] += _tn(ya_in, dya16)
        dwpb_ref[...] += _tn(yb_in, dyb16)
        dpg_ref[:, 0:D] = (dm_ * ya * (sa * (1.0 - sa))).astype(BF16)
        dpg_ref[:, D:2 * D] = (dm_ * yb * (sb * (1.0 - sb))).astype(BF16)
        dya = _nt(dya16, wpa_ref[...])
        dyb = _nt(dyb16, wpb_ref[...])

        u = ua_ref[...].astype(F32)
        za = za_ref[...].astype(F32)
        sz, dsz = _silu_and_grad(za)
        sv = jnp.concatenate([svr_ref[...], svc_ref[0], svc_ref[1]], axis=1)
        dpr_ref[:, 512:1024] = (dya * sv * sz).astype(BF16)
        dsv = dya * u * sz
        dsr_ref[...] = dsv[:, 0:256]
        dsc_ref[0] = dsv[:, 256:384]
        dsc_ref[1] = dsv[:, 384:512]
        dpr_ref[:, 1024:1536] = (dya * u * sv * dsz).astype(BF16)

        zb = zb_ref[...].astype(F32)
        o = of_ref[...] + ob_ref[...]
        szb, dszb = _silu_and_grad(zb)
        for h, (r, xh) in enumerate(_head_norm(o, None)):
            sl = slice(128 * h, 128 * h + 128)
            gh = g_ref[:, sl]
            don = dyb[:, sl] * szb[:, sl]
            dpr_ref[:, sl] = (dyb[:, sl] * (xh * gh) * dszb[:, sl]).astype(BF16)
            dg_ref[:, sl] += jnp.sum(don * xh, axis=0, keepdims=True)
            dxh = don * gh
            do_ref[:, sl] = (r * (dxh - xh * jnp.mean(dxh * xh, axis=-1, keepdims=True))).astype(BF16)

    r512 = pl.BlockSpec((TOK, 512), lambda i: (i, 0))
    row = pl.BlockSpec((TOK, D), lambda i: (i, 0))
    return _pcall(
        body, name="tail_bwd", grid=(m // TOK,),
        in_specs=[row, r512, r512, row, pl.BlockSpec((TOK, D), lambda i: (i, 1)), pl.BlockSpec((TOK, 512), lambda i: (i, 6)),
                  pl.BlockSpec((TOK, 512), lambda i: (i, 7)), pl.BlockSpec((TOK, 512), lambda i: (i, 8)),
                  pl.BlockSpec((TOK, 256), lambda i: (i, 0)), pl.BlockSpec((2, TOK, 128), lambda i: (0, i, 0)), r512, r512,
                  _full((1, 512)), _resident((D, D)), _resident((512, D)), _resident((512, D))],
        out_specs=[_resident((512, D)), _resident((512, D)), pl.BlockSpec((TOK, 2 * D), lambda i: (i, 0)),
                   pl.BlockSpec((TOK, 1536), lambda i: (i, 0)),
                   pl.BlockSpec((TOK, 256), lambda i: (i, 0)), pl.BlockSpec((2, TOK, 128), lambda i: (0, i, 0)), r512, _full((1, 512))],
        out_shape=[SDS((512, D), F32), SDS((512, D), F32), SDS((m, 2 * D), BF16), SDS((m, DPR), BF16), SDS((m, 256), F32),
                   SDS((2, m, 128), F32), SDS((m, 512), BF16), SDS((1, 512), F32)],
        compiler_params=_cp(("arbitrary",), VMEM_BIG),
    )(dout, ya_in, yb_in, p, p, p, p, p, svr, svc, o_f, o_b, gbn, wo, wpa, wpb)


def _ln_bwd(dsr, vnr, dvnc, p, ws01t, ln_g, dp):
    m = p.shape[0]

    def body(dsr_ref, vnr_ref, dvc_ref, va_ref, wt_ref, g_ref, dpi_ref, dp_ref, dw_ref, db_ref, dlg_ref, dlb_ref, dvn_ref):
        i = pl.program_id(0)

        @pl.when(i == 0)
        def _():
            dw_ref[...] = jnp.zeros_like(dw_ref)
            db_ref[...] = jnp.zeros_like(db_ref)
            dlg_ref[...] = jnp.zeros_like(dlg_ref)
            dlb_ref[...] = jnp.zeros_like(dlb_ref)

        for j in range(TOK // AC):
            for g in range(2):
                d = dsr_ref[AC * j:AC * j + AC, AC * g:AC * g + AC]
                d16 = d.astype(BF16)
                dvn_ref[AC * j:AC * j + AC, AC * g:AC * g + AC] = _nn(wt_ref[g], d16)
                dw_ref[g] += _nt(d16, vnr_ref[AC * j:AC * j + AC, AC * g:AC * g + AC])
                db_ref[g] += jnp.sum(d, axis=1, keepdims=True)
        dvn_ref[:, 256:384] = dvc_ref[0]
        dvn_ref[:, 384:512] = dvc_ref[1]
        dvn = dvn_ref[...]
        xf = va_ref[...].astype(F32)
        xc = xf - jnp.mean(xf, axis=-1, keepdims=True)
        rs = lax.rsqrt(jnp.mean(xc * xc, axis=-1, keepdims=True) + EPS)
        xh = xc * rs
        dlg_ref[...] += jnp.sum(dvn * xh, axis=0, keepdims=True)
        dlb_ref[...] += jnp.sum(dvn, axis=0, keepdims=True)
        dxh = dvn * g_ref[...]
        dva = rs * (dxh - jnp.mean(dxh, axis=-1, keepdims=True) - xh * jnp.mean(dxh * xh, axis=-1, keepdims=True))
        dp_ref[...] = dva.astype(BF16)

    return _pcall(
        body, name="ln_bwd", grid=(m // TOK,),
        in_specs=[pl.BlockSpec((TOK, 256), lambda i: (i, 0)), pl.BlockSpec((TOK, 256), lambda i: (i, 0)),
                  pl.BlockSpec((2, TOK, 128), lambda i: (0, i, 0)), pl.BlockSpec((TOK, 512), lambda i: (i, 9)),
                  _full((2, AC, AC)), _full((1, 512)), pl.BlockSpec(memory_space=pl.ANY)],
        out_specs=[pl.BlockSpec((TOK, 512), lambda i: (i, 3)), _full((2, AC, AC)), _full((2, AC, 1)), _full((1, 512)), _full((1, 512))],
        out_shape=[SDS((m, DPR), BF16), SDS((2, AC, AC), F32), SDS((2, AC, 1), F32), SDS((1, 512), F32), SDS((1, 512), F32)],
        scratch_shapes=[pltpu.VMEM((TOK, 512), F32)],
        input_output_aliases={6: 0}, compiler_params=_cp(("arbitrary",)),
    )(dsr, vnr, dvnc, p, ws01t, ln_g, dp)


def _tri_mm(tri, a):
    a1 = a.astype(BF16)
    r1 = a - a1.astype(F32)
    a2 = r1.astype(BF16)
    a3 = (r1 - a2.astype(F32)).astype(BF16)
    n = a.shape[1]
    r = _nn(tri, jnp.concatenate([a1, a2, a3], axis=1))
    return r[:, 0:n] + r[:, n:2 * n] + r[:, 2 * n:3 * n]


def _gla_masks(reverse):
    ri = lax.broadcasted_iota(jnp.int32, (CH, CH), 0)
    ci = lax.broadcasted_iota(jnp.int32, (CH, CH), 1)
    vis = (ci >= ri) if reverse else (ci <= ri)
    vis_t = (ci <= ri) if reverse else (ci >= ri)
    r4 = lax.broadcasted_iota(jnp.int32, (4 * CH, CH), 0) & (CH - 1)
    c4 = lax.broadcasted_iota(jnp.int32, (4 * CH, CH), 1)
    vis4 = (c4 >= r4) if reverse else (c4 <= r4)
    vis4_t = (c4 <= r4) if reverse else (c4 >= r4)
    lane = lax.broadcasted_iota(jnp.int32, (1, 256), 1)
    hm = [(lane >= CH * h) & (lane < CH * h + CH) for h in range(4)]
    return vis, vis_t, vis4, vis4_t, hm


def _stack_heads(x, hm):
    return jnp.concatenate([jnp.where(hm[h], x, 0.0).astype(BF16) for h in range(4)], axis=0)


def _diag_heads(full, hm):
    r = full.shape[0] // 4
    acc = jnp.where(hm[0], full[0:r], 0.0)
    for h in range(1, 4):
        acc = acc + jnp.where(hm[h], full[r * h:r * h + r], 0.0)
    return acc


def _rows_of_heads(x):
    return jnp.concatenate([x[:, 128 * h:128 * h + 128] for h in range(4)], axis=0)


def _lane_vis(reverse, transpose):
    ri = lax.broadcasted_iota(jnp.int32, (CH, 4 * CH), 0)
    ci = lax.broadcasted_iota(jnp.int32, (CH, 4 * CH), 1) & (CH - 1)
    return (ci >= ri) if (reverse != transpose) else (ci <= ri)


def _gla_fwd2(p, qkv_blk, lr, lrws, gbiases, s0s, name):
    m = p.shape[0]
    tb = min(GLA_TB, m)
    nb = m // tb
    nc = tb // CH

    def body(qkv_f, lr_f, qkv_b, lr_b, lrw_f, lrw_b, gb_f, gb_b, s0_f, s0_b,
             o_f, sb_f, sfin_f, o_b, sb_b, sfin_b, st_f, st_b):
        i = pl.program_id(0)

        @pl.when(i == 0)
        def _():
            st_f[...] = s0_f[...]
            st_b[...] = s0_b[...]

        dirs = []
        for reverse, qkv_ref, lr_ref, lrw_ref, gb_ref, o_ref, sb_ref, st_ref in (
                (False, qkv_f, lr_f, lrw_f, gb_f, o_f, sb_f, st_f), (True, qkv_b, lr_b, lrw_b, gb_b, o_b, sb_b, st_b)):
            vis, _, vis4, _, hm = _gla_masks(reverse)
            logits = _nn(lr_ref[...].astype(BF16), lrw_ref[...]) + gb_ref[...]
            dirs.append(dict(reverse=reverse, qkv=qkv_ref, o=o_ref, sb=sb_ref, vis4=vis4, hm=hm,
                             tri=vis.astype(F32).astype(BF16), a=_logsig(logits) * (1.0 / 16.0), st=st_ref[...]))
        for step in range(nc):
            for d in dirs:
                c = nc - 1 - step if d["reverse"] else step
                rows = slice(CH * c, CH * c + CH)
                b = _tri_mm(d["tri"], d["a"][rows])
                bl = b[0:1] if d["reverse"] else b[CH - 1:CH]
                q = d["qkv"][rows, 0:256].astype(F32) * 0.125
                k = d["qkv"][rows, 256:512].astype(F32)
                v16 = d["qkv"][rows, 512:1024].astype(BF16)
                qd = q * jnp.exp(b)
                kd16 = (k * jnp.exp(-b)).astype(BF16)
                kdec16 = (k * jnp.exp(bl - b)).astype(BF16)
                qstack = _stack_heads(qd, d["hm"])
                sc = jnp.where(d["vis4"], _nt(qstack, kd16), 0.0).astype(BF16)
                inter = _nt(qstack, d["st"].astype(BF16))
                for h in range(4):
                    d["o"][rows, 128 * h:128 * h + 128] = (
                        _nn(sc[CH * h:CH * h + CH], v16[:, 128 * h:128 * h + 128]) + inter[CH * h:CH * h + CH])
                d["sb"][c] = d["st"]
                d["st"] = d["st"] * jnp.exp(bl) + _diag_heads(_tn(v16, kdec16), d["hm"])
        st_f[...] = dirs[0]["st"]
        st_b[...] = dirs[1]["st"]

        @pl.when(i == nb - 1)
        def _():
            sfin_f[...] = dirs[0]["st"]
            sfin_b[...] = dirs[1]["st"]

    fw = lambda i: i
    bw = lambda i: nb - 1 - i
    in_specs = []
    for rm in (fw, bw):
        in_specs += [pl.BlockSpec((tb, 1024), lambda i, rm=rm: (rm(i), qkv_blk)), pl.BlockSpec((tb, LRW), lambda i, rm=rm: (rm(i), 0))]
    in_specs += [_full((LRW, 256))] * 2 + [_full((1, 256))] * 2 + [_full((128, 256))] * 2
    out_specs, out_shape = [], []
    for rm in (fw, bw):
        out_specs += [pl.BlockSpec((tb, 512), lambda i, rm=rm: (rm(i), 0)), pl.BlockSpec((nc, 128, 256), lambda i, rm=rm: (rm(i), 0, 0)),
                      _full((128, 256))]
        out_shape += [SDS((m, 512), F32), SDS((m // CH, 128, 256), F32), SDS((128, 256), F32)]
    return _pcall(
        body, name=name, grid=(nb,), in_specs=in_specs, out_specs=out_specs, out_shape=out_shape,
        scratch_shapes=[pltpu.VMEM((128, 256), F32), pltpu.VMEM((128, 256), F32)], compiler_params=_cp(("arbitrary",), VMEM_BIG),
    )(p, lr, p, lr, lrws[0], lrws[1], gbiases[0], gbiases[1], s0s[0], s0s[1])


def _gla_bwd(p, qkv_blk, lr, lrw, lrwt, gbias, sb, dsfin, do, prev, dp, *, reverse, name):
    m = p.shape[0]
    tb = min(GLA_TB, m)
    nb = m // tb
    nc = tb // CH
    rmap = (lambda i: i) if reverse else (lambda i: nb - 1 - i)
    has_prev = prev is not None
    has_dp = dp is not None

    def body(*refs):
        refs = list(refs)
        qkv_ref, lr_ref, lrw_ref, lrwt_ref, gb_ref, sb_ref, dsfin_ref, do_ref = refs[:8]
        refs = refs[8:]
        if has_prev:
            pq_ref, plr_ref = refs[:2]
            refs = refs[2:]
        if has_dp:
            refs = refs[1:]
        dqkv_ref, dlr_ref, dw2_ref, dgb_ref, ds0_ref, dst_ref, dlog_ref = refs
        i = pl.program_id(0)

        @pl.when(i == 0)
        def _():
            dst_ref[...] = dsfin_ref[...]
            dw2_ref[...] = jnp.zeros_like(dw2_ref)
            dgb_ref[...] = jnp.zeros_like(dgb_ref)

        vis, vis_t, vis4, vis4_t, hm = _gla_masks(reverse)
        tri = vis.astype(F32).astype(BF16)
        tri_t = vis_t.astype(F32).astype(BF16)
        lane_vis = _lane_vis(reverse, False)
        lane_vis_t = _lane_vis(reverse, True)
        lr16 = lr_ref[...].astype(BF16)
        logits = _nn(lr16, lrw_ref[...]) + gb_ref[...]
        a_all = _logsig(logits) * (1.0 / 16.0)
        dsig = (1.0 - jax.nn.sigmoid(logits)) * (1.0 / 16.0)
        dst = dst_ref[...]
        for c in (range(nc) if reverse else range(nc - 1, -1, -1)):
            rows = slice(CH * c, CH * c + CH)
            b = _tri_mm(tri, a_all[rows])
            bl = b[0:1] if reverse else b[CH - 1:CH]
            eb = jnp.exp(b)
            enb = jnp.exp(-b)
            ebl = jnp.exp(bl - b)
            el = jnp.exp(bl)
            q = qkv_ref[rows, 0:256].astype(F32) * 0.125
            k = qkv_ref[rows, 256:512].astype(F32)
            v16 = qkv_ref[rows, 512:1024].astype(BF16)
            do16 = do_ref[rows, :].astype(BF16)
            qd = q * eb
            kd = k * enb
            kdec = k * ebl
            st = sb_ref[c]
            st16 = st.astype(BF16)
            dst16 = dst.astype(BF16)
            qd16 = qd.astype(BF16)
            kd16 = kd.astype(BF16)
            qstack = _stack_heads(qd, hm)
            kstack = _stack_heads(kd, hm)
            kdecstack = _stack_heads(kdec, hm)
            pt = jnp.where(vis4_t, _nt(kstack, qd16), 0.0).astype(BF16)
            dvinter = _nt(kdecstack, dst16)
            do_rows = _rows_of_heads(do16)
            v_rows = _rows_of_heads(v16)
            dp_cat = jnp.where(lane_vis, _diag_heads(_nt(do_rows, v_rows), hm), 0.0).astype(BF16)
            dpt_cat = jnp.where(lane_vis_t, _diag_heads(_nt(v_rows, do_rows), hm), 0.0).astype(BF16)
            dqd = _nn(dp_cat, kstack) + _diag_heads(_nn(do_rows, st16), hm)
            dkd = _nn(dpt_cat, qstack)
            dkdec = _diag_heads(_nn(v_rows, dst16), hm)
            for h in range(4):
                rh = slice(CH * h, CH * h + CH)
                dv_h = _nn(pt[rh], do_rows[rh]) + dvinter[rh]
                if has_prev:
                    dv_h = dv_h + pq_ref[rows, 512 + 128 * h:512 + 128 * h + 128]
                dqkv_ref[rows, 512 + 128 * h:512 + 128 * h + 128] = dv_h.astype(dqkv_ref.dtype)
            dq = dqd * eb * 0.125
            dk = dkd * enb + dkdec * ebl
            if has_prev:
                dq = dq + pq_ref[rows, 0:256]
                dk = dk + pq_ref[rows, 256:512]
            dqkv_ref[rows, 0:256] = dq.astype(dqkv_ref.dtype)
            dqkv_ref[rows, 256:512] = dk.astype(dqkv_ref.dtype)
            g_kdec = dkdec * kdec
            db = dqd * qd - dkd * kd - g_kdec
            dbl = jnp.sum(g_kdec, axis=0, keepdims=True) + jnp.sum(st * dst, axis=0, keepdims=True) * el
            da = _tri_mm(tri_t, db) + dbl
            dlog_ref[rows, :] = da * dsig[rows]
            dst = dst * el + _diag_heads(_tn(do16, qd16), hm)
        dst_ref[...] = dst
        dlog = dlog_ref[...]
        dlog16 = dlog.astype(BF16)
        dlr = _nn(dlog16, lrwt_ref[...])
        if has_prev:
            dlr = dlr + plr_ref[...]
        dlr_ref[...] = dlr
        dw2_ref[...] += _tn(lr16, dlog16)
        dgb_ref[...] += jnp.sum(dlog, axis=0, keepdims=True)

        @pl.when(i == nb - 1)
        def _():
            ds0_ref[...] = dst

    in_specs = [pl.BlockSpec((tb,1024), lambda i: (rmap(i), qkv_blk)), pl.BlockSpec((tb,LRW), lambda i: (rmap(i), 0)),
                _full((LRW, 256)), _full((256, LRW)), _full((1, 256)), pl.BlockSpec((nc, 128, 256), lambda i: (rmap(i), 0, 0)),
                _full((128, 256)), pl.BlockSpec((tb,512), lambda i: (rmap(i), 0))]
    args = [p, lr, lrw, lrwt, gbias, sb, dsfin, do]
    if has_prev:
        in_specs += [pl.BlockSpec((tb,1024), lambda i: (rmap(i), 0)), pl.BlockSpec((tb,LRW), lambda i: (rmap(i), 0))]
        args += list(prev)
    aliases = {}
    if has_dp:
        in_specs.append(pl.BlockSpec(memory_space=pl.ANY))
        aliases = {len(args): 0}
        args.append(dp)
        dq_spec = pl.BlockSpec((tb,1024), lambda i: (rmap(i), 2))
        dq_shape = SDS(dp.shape, dp.dtype)
    else:
        dq_spec = pl.BlockSpec((tb,1024), lambda i: (rmap(i), 0))
        dq_shape = SDS((m, 1024), F32)
    return _pcall(
        body, name=name, grid=(nb,), in_specs=in_specs,
        out_specs=[dq_spec, pl.BlockSpec((tb,LRW), lambda i: (rmap(i), 0)), _full((LRW, 256)), _full((1, 256)), _full((128, 256))],
        out_shape=[dq_shape, SDS((m, LRW), F32), SDS((LRW, 256), F32), SDS((1, 256), F32), SDS((128, 256), F32)],
        scratch_shapes=[pltpu.VMEM((128, 256), F32), pltpu.VMEM((tb,256), F32)],
        input_output_aliases=aliases, compiler_params=_cp(("arbitrary",)),
    )(*args)


EARLY_KEYS = ["dmodc", "dscc", "dng_c", "dlng", "dlnb", "dws", "dbs", "dgbn", "dgf", "dw2", "dgb2", "loss", "dgate"]
EARLY_SHAPES = [(1, 2 * D), (D,), (1, D), (1, 512), (1, 512), (1, 4, 128, 128), (1, 4, 128), (1, 512), (D,), (2, 16, 256), (2, 256),
                (128,), (1, D)]
EARLY_SIZE = 2 * D + D + D + 512 + 512 + 4 * 128 * 128 + 512 + 512 + D + 2 * 16 * 256 + 512 + 128 + D
EARLY_ROWS = 648


def _device_step(x, c, ctx, c_ctx, tgt, wm, bm, ng, wit_g, wit_r, wlrt, ln_g, ln_b, ws, bs, w2, gb2, gbn, wpa, wpb, wo, gf,
                 exchange=None, shards=None):
    L = x.shape[0]
    wit_qkv = wit_r[2048:3072]
    ws16 = ws.astype(BF16)
    wst16 = jnp.swapaxes(ws, 1, 2).astype(BF16)
    bscol = bs[:, :, None]
    lrw = [jnp.zeros((LRW, 256), F32).at[16 * r:16 * r + 16].set(w2[r]).astype(BF16) for r in range(2)]
    lrwt = [w.T for w in lrw]
    gbias = [gb2[r:r + 1] for r in range(2)]

    cc = jnp.zeros((8, D), F32).at[0:1].set(c).at[1:2].set(c_ctx)
    mod = _modvec(cc, wm, bm)
    shift, scale, gate = mod[0:1, 0:D], mod[0:1, D:2 * D], mod[0:1, 2 * D:3 * D]
    shift_c, scale_c = mod[1:2, 0:D], mod[1:2, D:2 * D]

    hc = _prep_h(ctx, ng, scale_c, shift_c, "prep_hc")
    pc = _mm(hc, wit_qkv, tm=256, tn=1024, tk=D, out_dtype=F32, name="mm_pc", b_t=True)
    plrc = _mm(hc, wlrt, tm=256, tn=LRW, tk=D, out_dtype=F32, name="mm_plrc", b_t=True)
    zero_s = jnp.zeros((128, 256), F32)
    _, sbc_f, sc_f, _, sbc_b, sc_b = _gla_fwd2(pc, 0, plrc, lrw, gbias, (zero_s, zero_s), "gla_fwd_c")

    h, p, plr, vnr, vnc, late = _proj_fwd(x, ng, scale, shift, wit_g, wit_r, wlrt, ln_g, ln_b,
                                          shards if shards is not None else ())
    if shards is not None:
        me_xy = 2 * lax.axis_index("x") + lax.axis_index("y")
        g_wpa, g_wpb, g_wo = (_own(g, s_, me_xy) for g, s_ in zip(late, shards))
        wpa = jnp.swapaxes(g_wpa, 0, 1).reshape(512, D)
        wpb = jnp.swapaxes(g_wpb, 0, 1).reshape(512, D)
        wo = g_wo.reshape(D, D)
    o_f, sb_f, _, o_b, sb_b, _ = _gla_fwd2(p, 2, plr, lrw, gbias, (sc_f, sc_b), "gla_fwd")
    svc = _colmix_fwd(vnc.reshape(2, AC, L), ws16[2:4], bscol[2:4]).reshape(2, L, 128)
    ya_in, yb_in, svr, dwo, dx1, dout, loss, dgate, dgf = _tail_fwd(
        o_f, o_b, p, vnr, svc, x, tgt, ws16[0:2], bscol[0:2], gbn, wpa, wpb, wo, gate, gf)

    dwpa, dwpb, dp_g, dp, dsr, dsc, do, dgbn = _tail_bwd(dout, ya_in, yb_in, p, svr, svc, o_f, o_b, gbn, wo, wpa, wpb)
    dvnc, dws23, dbs23 = _colmix_bwd(dsc.reshape(2, AC, L), vnc.reshape(2, AC, L), wst16[2:4])
    dp, dws01, dbs01, dlng, dlnb = _ln_bwd(dsr, vnr, dvnc.reshape(2, L, 128), p, wst16[0:2], ln_g, dp)
    zero_ds = jnp.zeros((128, 256), F32)
    dqkv_f, dlr_f, dw2_f, dgb_f, ds0_f = _gla_bwd(p, 2, plr, lrw[0], lrwt[0], gbias[0], sb_f, zero_ds, do, None, None,
                                                  reverse=False, name="gla_bwd_f")
    dp, dlr, dw2_b, dgb_b, ds0_b = _gla_bwd(p, 2, plr, lrw[1], lrwt[1], gbias[1], sb_b, zero_ds, do, (dqkv_f, dlr_f), dp,
                                            reverse=True, name="gla_bwd_b")
    zero_do = jnp.zeros((ctx.shape[0], 512), BF16)
    dqkvc_f, dlrc_f, dw2c_f, dgbc_f, _ = _gla_bwd(pc, 0, plrc, lrw[0], lrwt[0], gbias[0], sbc_f, ds0_f, zero_do, None, None,
                                                  reverse=False, name="gla_bwd_cf")
    dqkvc, dlrc, dw2c_b, dgbc_b, _ = _gla_bwd(pc, 0, plrc, lrw[1], lrwt[1], gbias[1], sbc_b, ds0_b, zero_do,
                                              (dqkvc_f, dlrc_f), None, reverse=True, name="gla_bwd_cb")
    dhc = _mm(dqkvc, wit_qkv, tm=256, tn=D, tk=1024, out_dtype=F32, name="mm_dhc")
    dhc = _mm(dlrc, wlrt, tm=256, tn=D, tk=LRW, out_dtype=F32, name="mm_dhc_lr", acc=dhc)
    _, dng_c, dscale_c, dshift_c = _prep_bwd(ctx, dhc, None, ng, scale_c, "prep_bwd_c")

    dwit_g = _mm_tn(dp_g, h, ta=1024, tn=D, tk=2048, name="mm_dwi_g")
    dwit_r = _mm_tn(dp, h, ta=1024, tn=D, tk=2048, name="mm_dwi_r")
    dwit_qkv = _mm_tn(dqkvc, hc, ta=1024, tn=D, tk=256, name="mm_dwi_c", acc=dwit_r[2048:3072])
    dwlrt = _mm_tn(dlr, h, ta=LRW, tn=D, tk=2048, name="mm_dwlr")
    dwlrt = _mm_tn(dlrc, hc, ta=LRW, tn=D, tk=256, name="mm_dwlr_c", acc=dwlrt)
    big = dict(dwit_g=dwit_g, dwit_r=dwit_r, dwit_qkv=dwit_qkv, dwlrt=dwlrt, dwpa=dwpa, dwpb=dwpb, dwo=dwo)

    dmodc = jnp.concatenate([dshift_c, dscale_c], axis=1)
    dscc = _dcctx(jnp.zeros((8, 2 * D), F32).at[0:1].set(dmodc), wm)[0:1]
    dw2p = dw2_f + dw2c_f, dw2_b + dw2c_b
    small = dict(
        dmodc=dmodc, dscc=dscc, dng_c=dng_c, dlng=dlng, dlnb=dlnb, dws=jnp.concatenate([dws01, dws23], axis=0),
        dbs=jnp.concatenate([dbs01, dbs23], axis=0)[:, :, 0], dgbn=dgbn, dgf=dgf,
        dw2=jnp.stack([dw2p[0][0:16], dw2p[1][16:32]]), dgb2=jnp.concatenate([dgb_f + dgbc_f, dgb_b + dgbc_b], axis=0),
        loss=loss[0, 0], dgate=dgate)

    send = exchange(big) if exchange is not None else ()
    early = _pack([small[k] for k in EARLY_KEYS[:-2]] + [jnp.broadcast_to(small["loss"], (128,)), small["dgate"]], EARLY_ROWS) \
        if exchange is not None else None
    (dx, dng, dscale, dshift), got, early_all = _proj_bwd(dp_g, dp, dlr, wit_g, wit_r, wlrt, x, dx1, ng, scale, send, early)
    return dict(dx=dx, got=got, early=early, early_all=early_all, dshift=dshift, dscale=dscale, dng_lat=dng, **big, **small)


ANY = pl.BlockSpec(memory_space=pl.ANY)


def _coords():
    return lax.axis_index("x"), lax.axis_index("y"), lax.axis_index("c")


def _flip(v, bit):
    return 1 - v if bit else v


def _remote(src, dst, send_sem, recv_sem, dev):
    return pltpu.make_async_remote_copy(src_ref=src, dst_ref=dst, send_sem=send_sem, recv_sem=recv_sem,
                                        device_id=dev, device_id_type=MESH)


def _own(out, block, idx):
    return lax.dynamic_update_slice_in_dim(out, block[None], idx, axis=0)


def _half_idx(shape, axis, which, lead=()):
    idx = [pl.ds(0, d) for d in shape]
    h = shape[axis] // 2
    idx[axis] = pl.ds(which * h, h)
    return tuple(lead) + tuple(idx)


def _gather_weights(split, whole, name):
    ns, nw = len(split), len(whole)
    n = ns + nw
    arrs = [a for a, _ in split] + list(whole)

    def body(*refs):
        ins, outs = refs[:n], refs[n:2 * n]
        a_send, a_recv, b_send, b_recv = refs[2 * n:]
        x, y, c = _coords()
        me = 2 * x + y
        sib = (x, y, 1 - c)
        peers = [(1 - x, y), (x, 1 - y), (1 - x, 1 - y)]

        def half(k, slot, which):
            return outs[k].at[_half_idx(arrs[k].shape, split[k][1], which, lead=(slot,))]

        sends = []
        for k in range(n):
            for j, (px, py) in enumerate(peers):
                if k < ns:
                    rc = _remote(ins[k].at[_half_idx(arrs[k].shape, split[k][1], c)], half(k, me, c), a_send.at[3 * k + j],
                                 a_recv.at[3 * k + j], (px, py, c))
                else:
                    rc = _remote(ins[k], outs[k].at[me], a_send.at[3 * k + j], a_recv.at[3 * k + j], (px, py, c))
                rc.start()
                sends.append(rc)
        for k in range(ns):
            for j, (px, py) in enumerate(peers):
                landed = half(k, 2 * px + py, c)
                _remote(landed, landed, a_send.at[3 * k + j], a_recv.at[3 * k + j], (px, py, c)).wait_recv()
                fw = _remote(landed, landed, b_send.at[3 * k + j], b_recv.at[3 * k + j], sib)
                fw.start()
                sends.append(fw)
        for k in range(ns, n):
            for j, (px, py) in enumerate(peers):
                landed = outs[k].at[2 * px + py]
                _remote(landed, landed, a_send.at[3 * k + j], a_recv.at[3 * k + j], (px, py, c)).wait_recv()
        for k in range(ns):
            for j, (px, py) in enumerate(peers):
                passed = half(k, 2 * px + py, 1 - c)
                _remote(passed, passed, b_send.at[3 * k + j], b_recv.at[3 * k + j], sib).wait_recv()
        for rc in sends:
            rc.wait_send()

    outs = _pcall(
        body, name=name, in_specs=[ANY] * n, out_specs=[ANY] * n,
        out_shape=[SDS((4,) + a.shape, a.dtype) for a in arrs],
        scratch_shapes=[pltpu.SemaphoreType.DMA((3 * n,)), pltpu.SemaphoreType.DMA((3 * n,)), pltpu.SemaphoreType.DMA((3 * ns,)),
                        pltpu.SemaphoreType.DMA((3 * ns,))],
    )(*arrs)
    me_xy = 2 * lax.axis_index("x") + lax.axis_index("y")
    return [_own(o, a, me_xy) for o, a in zip(outs, arrs)]


def _gather_all(a, swap, name):
    masks = [(mx, my, mc) for mx in range(2) for my in range(2) for mc in range(2)][1:]
    n = len(swap)

    def body(*refs):
        in_ref, sw_in = refs[0], refs[1:1 + n]
        out_ref, sw_out = refs[1 + n], refs[2 + n:2 + 2 * n]
        send_sems, recv_sems = refs[2 + 2 * n:]
        x, y, c = _coords()
        me = 4 * x + 2 * y + c
        sends = []
        for j, (mx, my, mc) in enumerate(masks):
            rc = _remote(in_ref, out_ref.at[me], send_sems.at[j], recv_sems.at[j], (_flip(x, mx), _flip(y, my), _flip(c, mc)))
            rc.start()
            sends.append(rc)
        for k in range(n):
            rc = _remote(sw_in[k], sw_out[k], send_sems.at[7 + k], recv_sems.at[7 + k], (x, y, 1 - c))
            rc.start()
            sends.append(rc)
        for j, (mx, my, mc) in enumerate(masks):
            px, py, pc = _flip(x, mx), _flip(y, my), _flip(c, mc)
            landed = out_ref.at[4 * px + 2 * py + pc]
            _remote(landed, landed, send_sems.at[j], recv_sems.at[j], (px, py, pc)).wait_recv()
        for k in range(n):
            _remote(sw_out[k], sw_out[k], send_sems.at[7 + k], recv_sems.at[7 + k], (x, y, 1 - c)).wait_recv()
        for rc in sends:
            rc.wait_send()

    res = _pcall(
        body, name=name, in_specs=[ANY] * (1 + n), out_specs=[ANY] * (1 + n),
        out_shape=[SDS((8,) + a.shape, a.dtype)] + [SDS(s_.shape, s_.dtype) for s_ in swap],
        scratch_shapes=[pltpu.SemaphoreType.DMA((7 + n,)), pltpu.SemaphoreType.DMA((7 + n,))],
    )(a, *swap)
    return _own(res[0], a, 4 * lax.axis_index("x") + 2 * lax.axis_index("y") + lax.axis_index("c")), list(res[1:])


def _half_shape(shape, axis):
    return tuple(d // 2 if i == axis else d for i, d in enumerate(shape))


def _swap_half_c(arrs, axes, name):
    n = len(arrs)

    def body(*refs):
        ins, outs = refs[:n], refs[n:2 * n]
        send_sems, recv_sems = refs[2 * n:]
        x, y, c = _coords()
        sends = []
        for k in range(n):
            rc = _remote(ins[k].at[_half_idx(arrs[k].shape, axes[k], 1 - c)], outs[k], send_sems.at[k], recv_sems.at[k],
                         (x, y, 1 - c))
            rc.start()
            sends.append(rc)
        for rc in sends:
            rc.wait()

    return _pcall(
        body, name=name, in_specs=[ANY] * n, out_specs=[ANY] * n,
        out_shape=[SDS(_half_shape(a.shape, ax), a.dtype) for a, ax in zip(arrs, axes)],
        scratch_shapes=[pltpu.SemaphoreType.DMA((n,)), pltpu.SemaphoreType.DMA((n,))],
    )(*arrs)


def _pair_sum(a, got, cidx, axis, name):
    _, r, cdim = a.shape
    hshape = _half_shape(a.shape, axis)

    def body(c_ref, a_ref, g_ref, o_ref):
        o_ref[...] = (a_ref[...] + g_ref[...]).astype(BF16)

    if axis == 1:
        tr = min(r // 2, 256)
        nj = (r // 2) // tr
        blk = pl.BlockSpec((1, tr, cdim), lambda s, j, c: (s, j, 0))
        a_spec = pl.BlockSpec((1, tr, cdim), lambda s, j, c: (s, c[0] * nj + j, 0))
    else:
        nj, hw = 1, cdim // 2
        blk = pl.BlockSpec((1, r, hw), lambda s, j, c: (s, 0, 0))
        a_spec = pl.BlockSpec((1, r, hw), lambda s, j, c: (s, 0, c[0]))
    return _pcall(
        body, name=name, out_shape=SDS(hshape, BF16),
        grid_spec=pltpu.PrefetchScalarGridSpec(num_scalar_prefetch=1, grid=(4, nj), in_specs=[a_spec, blk], out_specs=blk),
        compiler_params=_cp(("parallel", "parallel"), VMEM_BIG),
    )(cidx, a, got)


def _sum_chips(parts, name):
    _, h, cdim = parts.shape

    def body(p_ref, o_ref):
        acc = p_ref[0].astype(F32)
        for k in range(1, 4):
            acc = acc + p_ref[k].astype(F32)
        o_ref[...] = acc

    if h % 256 == 0 or h in (128,):
        tr = min(h, 256)
        grid, in_spec, out_spec = (h // tr,), pl.BlockSpec((4, tr, cdim), lambda i: (0, i, 0)), pl.BlockSpec((tr, cdim), lambda i: (i, 0))
    else:
        lw = 256
        grid, in_spec, out_spec = (cdim // lw,), pl.BlockSpec((4, h, lw), lambda i: (0, 0, i)), pl.BlockSpec((h, lw), lambda i: (0, i))
    return _pcall(
        body, name=name, grid=grid, in_specs=[in_spec], out_specs=out_spec, out_shape=SDS((h, cdim), F32),
        compiler_params=_cp(("parallel",), VMEM_BIG),
    )(parts)


def _sum_slots(a, name, rows):
    s, n, _ = a.shape

    def body(a_ref, o_ref):
        acc = a_ref[0]
        for k in range(1, s):
            acc = acc + a_ref[k]
        o_ref[...] = acc

    return _pcall(
        body, name=name, grid=(n // rows,), in_specs=[pl.BlockSpec((s, rows, 128), lambda i: (0, i, 0))],
        out_specs=pl.BlockSpec((rows, 128), lambda i: (i, 0)), out_shape=SDS((n, 128), F32),
        compiler_params=_cp(("parallel",)),
    )(a)


def _adam_math(w, g, m, v):
    nm = ADAM_B1 * m + (1.0 - ADAM_B1) * g
    nv = ADAM_B2 * v + (1.0 - ADAM_B2) * (g * g)
    m_hat = nm / (1.0 - ADAM_B1 ** ADAM_STEP)
    v_hat = nv / (1.0 - ADAM_B2 ** ADAM_STEP)
    return -ADAM_LR * (m_hat / (jnp.sqrt(v_hat) + ADAM_EPS) + ADAM_WD * w), nm, nv


def _adamw(w, g, m, v, name, rows):
    r, cdim = w.shape

    def body(w_ref, g_ref, m_ref, v_ref, d_ref, nm_ref, nv_ref):
        d_ref[...], nm_ref[...], nv_ref[...] = _adam_math(w_ref[...], g_ref[...], m_ref[...], v_ref[...])

    blk = pl.BlockSpec((rows, cdim), lambda i: (i, 0))
    return _pcall(
        body, name=name, grid=(r // rows,), in_specs=[blk] * 4, out_specs=[blk] * 3,
        out_shape=[SDS(w.shape, F32)] * 3, compiler_params=_cp(("parallel",)),
    )(w, g, m, v)


def _adamw_joined(w, mine, other, m, v, cidx, axis, name, rows):
    r, cdim = w.shape
    if axis == 0:
        rows = r

    def body(c_ref, w_ref, a_ref, b_ref, m_ref, v_ref, g_ref, d_ref, nm_ref, nv_ref):
        a, b = a_ref[...], b_ref[...]
        g = jnp.where(c_ref[0] == 0, jnp.concatenate([a, b], axis=axis), jnp.concatenate([b, a], axis=axis))
        g_ref[...] = g
        d_ref[...], nm_ref[...], nv_ref[...] = _adam_math(w_ref[...], g, m_ref[...], v_ref[...])

    blk = pl.BlockSpec((rows, cdim), lambda i, c: (i, 0))
    hshape = (rows // 2, cdim) if axis == 0 else (rows, cdim // 2)
    hblk = pl.BlockSpec(hshape, lambda i, c: (i, 0))
    return _pcall(
        body, name=name, out_shape=[SDS(w.shape, F32)] * 4,
        grid_spec=pltpu.PrefetchScalarGridSpec(num_scalar_prefetch=1, grid=(r // rows,), in_specs=[blk, hblk, hblk, blk, blk],
                                               out_specs=[blk] * 4),
        compiler_params=_cp(("parallel",)),
    )(cidx, w, mine, other, m, v)


def _adamw_many(ws, gs, ms, vs, name):
    n = len(ws)

    def body(*refs):
        outs = refs[4 * n:]
        for k in range(n):
            d, nm, nv = _adam_math(refs[k][...], refs[n + k][...], refs[2 * n + k][...], refs[3 * n + k][...])
            outs[k][...] = d
            outs[n + k][...] = nm
            outs[2 * n + k][...] = nv

    res = _pcall(body, name=name, out_shape=[SDS(w.shape, F32) for w in ws] * 3)(*ws, *gs, *ms, *vs)
    return res[:n], res[n:2 * n], res[2 * n:]


def _pack(pieces, rows):
    flat = jnp.concatenate([p.reshape(-1) for p in pieces])
    return jnp.pad(flat, (0, rows * 128 - flat.shape[0])).reshape(rows, 128)


def _unpack(buf, shapes):
    flat = buf.reshape(-1)
    out, off = [], 0
    for shp in shapes:
        size = 1
        for s in shp:
            size *= s
        out.append(flat[off:off + size].reshape(shp))
        off += size
    return out


LATE_ROWS = 32


def kernel(x, c, ctx, c_ctx, w_mod, b_mod, norm_g, w_in, a_ln_g, a_ln_b, a_ws, a_bs, b_gate_w2, b_gate_b, b_norm_g, w_proj_a, w_proj_b, w_out, final_norm_g, loss_target, m_c_ctx, m_w_mod, m_b_mod, m_norm_g, m_w_in, m_a_ln_g, m_a_ln_b, m_a_ws, m_a_bs, m_b_gate_w2, m_b_gate_b, m_b_norm_g, m_w_proj_a, m_w_proj_b, m_w_out, m_final_norm_g, v_c_ctx, v_w_mod, v_b_mod, v_norm_g, v_w_in, v_a_ln_g, v_a_ln_b, v_a_ws, v_a_bs, v_b_gate_w2, v_b_gate_b, v_b_norm_g, v_w_proj_a, v_w_proj_b, v_w_out, v_final_norm_g):
    xi, yi, ci = _coords()
    me_xy = 2 * xi + yi

    gate_pack = _pack([b_gate_w2[0], b_gate_b[0]], 24)
    w_in_t, m_w_in_t, v_w_in_t = (jnp.swapaxes(a[0], 0, 1) for a in (w_in, m_w_in, v_w_in))
    g_wit, g_wm, g_gate = _gather_weights([(w_in_t.astype(BF16), 1), (w_mod[0].astype(BF16), 0)], [gate_pack], "gather_weights")
    late_shards = (w_proj_a[0].astype(BF16), w_proj_b[0].astype(BF16), w_out[0].astype(BF16))
    wit_u = g_wit.reshape(4 * 1288, D)
    wit_g = wit_u[3104:5152]
    wit_r = jnp.concatenate([wit_u[1056:1568], wit_u[1568:2080], wit_u[2592:3104], wit_u[2080:2592], wit_u[0:1024]], axis=0)
    wlrt = jnp.pad(wit_u[1024:1056], ((0, LRW - 32), (0, 0)))
    wm = jnp.swapaxes(g_wm, 0, 1).reshape(D, 3 * D)
    gflat = g_gate.reshape(4, 24 * 128)
    w2 = jnp.swapaxes(gflat[:, 0:2048].reshape(4, 2, 16, 64), 0, 2)
    w2 = jnp.swapaxes(w2, 0, 1).reshape(2, 16, 256)
    gb2 = jnp.swapaxes(gflat[:, 2048:2176].reshape(4, 2, 64), 0, 1).reshape(2, 256)

    tags = ["wi", "wpa", "wpb", "wo"]
    half_axes = [2, 1, 1, 1]
    sent = []

    def exchange(g):
        dwr = g["dwit_r"]
        dwit_u = jnp.concatenate([g["dwit_qkv"], g["dwlrt"][0:32], dwr[0:512], dwr[512:1024], dwr[1536:2048], dwr[1024:1536],
                                  g["dwit_g"]], axis=0)
        big = [dwit_u.reshape(4, 1288, D), jnp.swapaxes(g["dwpa"].reshape(512, 4, 256), 0, 1),
               jnp.swapaxes(g["dwpb"].reshape(512, 4, 256), 0, 1), g["dwo"].reshape(4, 256, D)]
        other = _swap_half_c(big, half_axes, "swap_half_in")
        cidx = jnp.reshape(ci, (1,)).astype(jnp.int32)
        sent.extend(_pair_sum(a, o, cidx, ax, "sum_pair_" + t) for a, o, ax, t in zip(big, other, half_axes, tags))
        return sent

    r = _device_step(x[0], c, ctx[0], c_ctx[None], loss_target[0], wm, b_mod, norm_g, wit_g, wit_r, wlrt, a_ln_g, a_ln_b,
                     a_ws[0], a_bs[0], w2, gb2, b_norm_g, None, None, None, final_norm_g[None], exchange, late_shards)

    parts = [_own(g, lax.dynamic_index_in_dim(s_, me_xy, axis=0, keepdims=False), me_xy) for g, s_ in zip(r["got"], sent)]
    halves = [_sum_chips(p_, "sum_chips_" + t) for p_, t in zip(parts, tags)]

    me8 = 4 * xi + 2 * yi + ci
    early_all = _own(r["early_all"], r["early"], me8)
    late = _pack([r["dshift"], r["dscale"], r["dng_lat"], c], LATE_ROWS)
    late_all, others = _gather_all(late, halves, "gather_small")
    s_early = _sum_slots(early_all, "sum_early", EARLY_ROWS // 3)
    s_late = _sum_slots(late_all, "sum_late", LATE_ROWS)
    (s_dmodc, s_dscc, s_dng_c, s_dlng, s_dlnb, s_dws, s_dbs, s_dgbn, s_dgf, s_dw2, s_dgb2, s_loss, s_dgate) = _unpack(
        s_early, EARLY_SHAPES)
    s_dshift, s_dscale, s_dng_lat, _ = _unpack(s_late, [(1, D)] * 4)
    s_dng = s_dng_lat + s_dng_c
    s_dmod = jnp.concatenate([s_dshift, s_dscale, s_dgate], axis=1)
    loss = s_loss[0]
    s_dmodc_p = jnp.pad(s_dmodc, ((0, 0), (0, D)))
    g_b_mod = s_dmod + s_dmodc_p
    sg = jax.nn.sigmoid(c_ctx)
    g_c_ctx = s_dscc * (sg * (1.0 + c_ctx * (1.0 - sg)))
    g_w2 = lax.dynamic_slice_in_dim(s_dw2, 64 * me_xy, 64, axis=2)[None]
    g_gb2 = lax.dynamic_slice_in_dim(s_dgb2, 64 * me_xy, 64, axis=1)[None]

    flat_l = late_all.reshape(8, LATE_ROWS * 128)
    dgate_all = early_all.reshape(8, EARLY_ROWS * 128)[:, EARLY_SIZE - D:EARLY_SIZE]
    dmod_all = jnp.concatenate([flat_l[:, 0:2 * D], dgate_all], axis=1)
    c_all = flat_l[:, 3 * D:4 * D]
    lhs = jnp.concatenate([_silu(c_all), _silu(c_ctx)[None], jnp.zeros((7, D), F32)], axis=0)
    rhs = jnp.concatenate([dmod_all, s_dmodc_p, jnp.zeros((7, 3 * D), F32)], axis=0)
    rhs = lax.dynamic_slice_in_dim(rhs, 768 * me_xy, 768, axis=1)
    g_w_mod = _mm(lhs.T.astype(BF16), rhs.astype(BF16), tm=D, tn=768, tk=16, out_dtype=F32, name="mm_dwm")

    cidx = jnp.reshape(ci, (1,)).astype(jnp.int32)
    g_w_in_t, d_w_in_t, nm_w_in_t, nv_w_in_t = _adamw_joined(w_in_t, halves[0], others[0], m_w_in_t, v_w_in_t, cidx, 1,
                                                             "adamw_w_in", 184)
    g_w_in, d_w_in, nm_w_in, nv_w_in = (jnp.swapaxes(a, 0, 1) for a in (g_w_in_t, d_w_in_t, nm_w_in_t, nv_w_in_t))
    g_wpa, d_wpa, nm_wpa, nv_wpa = _adamw_joined(w_proj_a[0], halves[1], others[1], m_w_proj_a[0], v_w_proj_a[0], cidx, 0,
                                                 "adamw_wpa", 0)
    g_wpb, d_wpb, nm_wpb, nv_wpb = _adamw_joined(w_proj_b[0], halves[2], others[2], m_w_proj_b[0], v_w_proj_b[0], cidx, 0,
                                                 "adamw_wpb", 0)
    g_wo, d_wo, nm_wo, nv_wo = _adamw_joined(w_out[0], halves[3], others[3], m_w_out[0], v_w_out[0], cidx, 0, "adamw_wo", 0)
    d_w_mod, nm_w_mod, nv_w_mod = _adamw(w_mod[0], g_w_mod, m_w_mod[0], v_w_mod[0], "adamw_w_mod", 256)

    names = ["c_ctx", "b_mod", "norm_g", "a_ln_g", "a_ln_b", "a_ws", "a_bs", "b_gate_w2", "b_gate_b", "b_norm_g", "final_norm_g"]
    ws_ = [c_ctx, b_mod, norm_g, a_ln_g, a_ln_b, a_ws, a_bs, b_gate_w2, b_gate_b, b_norm_g, final_norm_g]
    gs_ = [g_c_ctx, g_b_mod, s_dng, s_dlng, s_dlnb, s_dws, s_dbs, g_w2, g_gb2, s_dgbn, s_dgf]
    ms_ = [m_c_ctx, m_b_mod, m_norm_g, m_a_ln_g, m_a_ln_b, m_a_ws, m_a_bs, m_b_gate_w2, m_b_gate_b, m_b_norm_g, m_final_norm_g]
    vs_ = [v_c_ctx, v_b_mod, v_norm_g, v_a_ln_g, v_a_ln_b, v_a_ws, v_a_bs, v_b_gate_w2, v_b_gate_b, v_b_norm_g, v_final_norm_g]
    shapes = [w.shape for w in ws_]
    flat2 = [(1, 1024), (1, 3072), (1, 1024), (1, 512), (1, 512), (512, 128), (4, 128), (32, 64), (2, 64), (1, 512), (1, 1024)]
    as2d = lambda arrs: [a.reshape(s) for a, s in zip(arrs, flat2)]
    d_s, nm_s, nv_s = _adamw_many(as2d(ws_), as2d(gs_), as2d(ms_), as2d(vs_), "adamw_small")
    d_small = {n: a.reshape(s) for n, a, s in zip(names, d_s, shapes)}
    nm_small = {n: a.reshape(s) for n, a, s in zip(names, nm_s, shapes)}
    nv_small = {n: a.reshape(s) for n, a, s in zip(names, nv_s, shapes)}
    g_small = {n: g.reshape(s) for n, g, s in zip(names, gs_, shapes)}

    order = ["c_ctx", "w_mod", "b_mod", "norm_g", "w_in", "a_ln_g", "a_ln_b", "a_ws", "a_bs", "b_gate_w2", "b_gate_b", "b_norm_g",
             "w_proj_a", "w_proj_b", "w_out", "final_norm_g"]
    big_g = dict(w_mod=g_w_mod[None], w_in=g_w_in[None], w_proj_a=g_wpa[None], w_proj_b=g_wpb[None], w_out=g_wo[None])
    big_d = dict(w_mod=d_w_mod[None], w_in=d_w_in[None], w_proj_a=d_wpa[None], w_proj_b=d_wpb[None], w_out=d_wo[None])
    big_m = dict(w_mod=nm_w_mod[None], w_in=nm_w_in[None], w_proj_a=nm_wpa[None], w_proj_b=nm_wpb[None], w_out=nm_wo[None])
    big_v = dict(w_mod=nv_w_mod[None], w_in=nv_w_in[None], w_proj_a=nv_wpa[None], w_proj_b=nv_wpb[None], w_out=nv_wo[None])
    grads = [big_g[n] if n in big_g else g_small[n] for n in order]
    deltas = [big_d[n] if n in big_d else d_small[n] for n in order]
    new_m = [big_m[n] if n in big_m else nm_small[n] for n in order]
    new_v = [big_v[n] if n in big_v else nv_small[n] for n in order]
    return (loss, r["dx"][None], *grads, *deltas, *new_m, *new_v)
```

```python
import jax
import jax.numpy as jnp
from jax import lax
from jax.experimental import pallas as pl
from jax.experimental.pallas import tpu as pltpu

F32 = jnp.float32
BF16 = jnp.bfloat16
SDS = jax.ShapeDtypeStruct

D = 1024
NP = 5120
LRW = 128
CH = 64
AC = 128
EPS = 1e-6
TOK = 512
GLA_TB = 1024
VMEM_BIG = 48 * 1024 * 1024

ADAM_LR, ADAM_B1, ADAM_B2, ADAM_EPS, ADAM_WD, ADAM_STEP = 0.001, 0.9, 0.999, 1e-08, 0.01, 10

_pcall = pl.pallas_call
MESH = pl.DeviceIdType.MESH


def _cp(sem=None, vmem=None):
    kw = {}
    if sem is not None:
        kw["dimension_semantics"] = sem
    if vmem is not None:
        kw["vmem_limit_bytes"] = vmem
    return pltpu.CompilerParams(**kw)


def _silu(x):
    return x * jax.nn.sigmoid(x)


def _silu_and_grad(x):
    s = jax.nn.sigmoid(x)
    return x * s, s * (1.0 + x * (1.0 - s))


def _logsig(x):
    return jnp.minimum(x, 0.0) - jnp.log1p(jnp.exp(-jnp.abs(x)))


def _nt(a, b):
    return lax.dot_general(a, b, (((1,), (1,)), ((), ())), preferred_element_type=F32)


def _tn(a, b):
    return lax.dot_general(a, b, (((0,), (0,)), ((), ())), preferred_element_type=F32)


def _nn(a, b):
    return jnp.dot(a, b, preferred_element_type=F32)


def _full(shape):
    return pl.BlockSpec(shape, lambda *_: (0,) * len(shape))


def _mm(a, b, *, tm, tn, tk, out_dtype, name, acc=None, n_outer=False, b_t=False):
    m, k = a.shape
    n, k2 = (b.shape if b_t else b.shape[::-1])
    assert k == k2 and m % tm == 0 and n % tn == 0 and k % tk == 0, (a.shape, b.shape, tm, tn, tk)
    nk = k // tk
    has_acc = acc is not None

    def body(*refs):
        if has_acc:
            a_ref, b_ref, c_ref, o_ref = refs[:4]
        else:
            a_ref, b_ref, o_ref = refs[:3]
        part = (_nt if b_t else _nn)(a_ref[...].astype(BF16), b_ref[...].astype(BF16))
        if nk == 1:
            o_ref[...] = ((c_ref[...] + part) if has_acc else part).astype(out_dtype)
            return
        acc_ref = refs[-1]
        kk = pl.program_id(2)

        @pl.when(kk == 0)
        def _():
            if has_acc:
                acc_ref[...] = c_ref[...] + part
            else:
                acc_ref[...] = part

        @pl.when(kk > 0)
        def _():
            acc_ref[...] += part

        @pl.when(kk == nk - 1)
        def _():
            o_ref[...] = acc_ref[...].astype(out_dtype)

    if n_outer:
        ij = lambda g0, g1: (g1, g0)
        grid = (n // tn, m // tm, nk)
    else:
        ij = lambda g0, g1: (g0, g1)
        grid = (m // tm, n // tn, nk)
    b_spec = (pl.BlockSpec((tn, tk), lambda g0, g1, kk: (ij(g0, g1)[1], kk)) if b_t
              else pl.BlockSpec((tk, tn), lambda g0, g1, kk: (kk, ij(g0, g1)[1])))
    in_specs = [pl.BlockSpec((tm, tk), lambda g0, g1, kk: (ij(g0, g1)[0], kk)), b_spec]
    args = [a, b]
    if has_acc:
        in_specs.append(pl.BlockSpec((tm, tn), lambda g0, g1, kk: ij(g0, g1)))
        args.append(acc)
    return _pcall(
        body, name=name, grid=grid, in_specs=in_specs,
        out_specs=pl.BlockSpec((tm, tn), lambda g0, g1, kk: ij(g0, g1)),
        out_shape=SDS((m, n), out_dtype), scratch_shapes=([pltpu.VMEM((tm, tn), F32)] if nk > 1 else []),
        compiler_params=_cp(("parallel", "parallel", "arbitrary"), VMEM_BIG),
    )(*args)


def _mm_tn(a, b, *, ta, tn, tk, name, acc=None):
    m, ka = a.shape
    m2, n = b.shape
    assert m == m2 and ka % ta == 0 and n % tn == 0 and m % tk == 0, (a.shape, b.shape, ta, tn, tk)
    nk = m // tk
    has_acc = acc is not None

    def body(*refs):
        if has_acc:
            a_ref, b_ref, c_ref, o_ref = refs
        else:
            a_ref, b_ref, o_ref = refs
        kk = pl.program_id(2)
        part = _tn(a_ref[...].astype(BF16), b_ref[...].astype(BF16))

        @pl.when(kk == 0)
        def _():
            if has_acc:
                o_ref[...] = c_ref[...] + part
            else:
                o_ref[...] = part

        @pl.when(kk > 0)
        def _():
            o_ref[...] += part

    in_specs = [pl.BlockSpec((tk, ta), lambda i, j, kk: (kk, i)), pl.BlockSpec((tk, tn), lambda i, j, kk: (kk, j))]
    args = [a, b]
    if has_acc:
        in_specs.append(pl.BlockSpec((ta, tn), lambda i, j, kk: (i, j)))
        args.append(acc)
    return _pcall(
        body, name=name, grid=(ka // ta, n // tn, nk), in_specs=in_specs,
        out_specs=pl.BlockSpec((ta, tn), lambda i, j, kk: (i, j)), out_shape=SDS((ka, n), F32),
        compiler_params=_cp(("parallel", "parallel", "arbitrary"), VMEM_BIG),
    )(*args)


def _modvec(cc, wm, bm):
    def body(c_ref, w_ref, b_ref, o_ref):
        o_ref[...] = _nn(_silu(c_ref[...]).astype(BF16), w_ref[...]) + b_ref[...]

    return _pcall(body, name="modvec", out_shape=SDS((8, 3 * D), F32), compiler_params=_cp(None, VMEM_BIG))(cc, wm, bm)


def _dcctx(dmodc, wm):
    def body(d_ref, w_ref, o_ref):
        o_ref[...] = _nt(d_ref[...].astype(BF16), w_ref[...])

    return _pcall(
        body, name="dcctx", grid=(1,), in_specs=[_full((8, 2 * D)), pl.BlockSpec((D, 2 * D), lambda i: (0, 0))],
        out_specs=_full((8, D)), out_shape=SDS((8, D), F32), compiler_params=_cp(("arbitrary",), VMEM_BIG),
    )(dmodc, wm)


def _prep_h(x, ng, scale, shift, name):
    m = x.shape[0]

    def body(x_ref, g_ref, sc_ref, sh_ref, h_ref):
        xf = x_ref[...]
        r = lax.rsqrt(jnp.mean(xf * xf, axis=-1, keepdims=True) + EPS)
        y = (xf * r) * g_ref[...]
        h_ref[...] = (y * (1.0 + sc_ref[...]) + sh_ref[...]).astype(BF16)

    tok = min(TOK, m)
    row = pl.BlockSpec((tok, D), lambda i: (i, 0))
    return _pcall(
        body, name=name, grid=(m // tok,), in_specs=[row, _full((1, D)), _full((1, D)), _full((1, D))],
        out_specs=row, out_shape=SDS((m, D), BF16), compiler_params=_cp(("parallel",)),
    )(x, ng, scale, shift)


def _resident(shape):
    return pl.BlockSpec(shape, lambda *_: (0,) * len(shape), pipeline_mode=pl.Buffered(1))


PROJ_TM = 512


def _proj_fwd(x, ng, scale, shift, wit_g, wit_r, wlrt, ln_g, ln_b, share=()):
    m = x.shape[0]
    ns = len(share)
    steps = m // PROJ_TM
    src = [(0, 0), (0, D), (1, 2 * D), (1, 0), (1, D)]

    def body(*refs):
        x_ref, g_ref, sc_ref, sh_ref, wg_ref, wr_ref, wl_ref, lg_ref, lb_ref = refs[:9]
        share_refs = refs[9:9 + ns]
        h_ref, p_ref, plr_ref, vr_ref, vc_ref = refs[9 + ns:14 + ns]
        got_refs = refs[14 + ns:14 + 2 * ns]
        sems = refs[14 + 2 * ns:]

        def copies():
            cx, cy, cc = _coords()
            me = 2 * cx + cy
            peers = [(1 - cx, cy), (cx, 1 - cy), (1 - cx, 1 - cy)]
            out, back = [], []
            for k in range(ns):
                for j, (px, py) in enumerate(peers):
                    out.append(_remote(share_refs[k], got_refs[k].at[me], sems[0].at[3 * k + j], sems[1].at[3 * k + j], (px, py, cc)))
                    landed = got_refs[k].at[2 * px + py]
                    back.append(_remote(landed, landed, sems[0].at[3 * k + j], sems[1].at[3 * k + j], (px, py, cc)))
            return out, back

        if ns:
            @pl.when(pl.program_id(0) == 0)
            def _():
                for rc in copies()[0]:
                    rc.start()

            @pl.when(pl.program_id(0) == steps - 1)
            def _():
                out, back = copies()
                for rc in back:
                    rc.wait_recv()
                for rc in out:
                    rc.wait_send()

        xf = x_ref[...]
        r = lax.rsqrt(jnp.mean(xf * xf, axis=-1, keepdims=True) + EPS)
        y = (xf * r) * g_ref[...]
        h = (y * (1.0 + sc_ref[...]) + sh_ref[...]).astype(BF16)
        h_ref[...] = h
        for j, (which, r0) in enumerate(src):
            w_ref = wr_ref if which else wg_ref
            blk = _nt(h, w_ref[r0:r0 + D, :]).astype(BF16)
            p_ref[:, D * j:D * j + D] = blk
            if j == 4:
                xf = blk[:, 512:1024].astype(F32)
                xc = xf - jnp.mean(xf, axis=-1, keepdims=True)
                vn = (xc * lax.rsqrt(jnp.mean(xc * xc, axis=-1, keepdims=True) + EPS)) * lg_ref[...] + lb_ref[...]
                vr_ref[...] = vn[:, 0:256].astype(BF16)
                vc_ref[0] = vn[:, 256:384].astype(BF16)
                vc_ref[1] = vn[:, 384:512].astype(BF16)
        plr_ref[...] = _nt(h, wl_ref[...])

    row = pl.BlockSpec((PROJ_TM, D), lambda i: (i, 0))
    vec = _full((1, D))
    res = _pcall(
        body, name="proj_fwd", grid=(steps,),
        in_specs=[row, vec, vec, vec, _resident((2 * D, D)), _resident((3 * D, D)), _resident((LRW, D)), _full((1, 512)),
                  _full((1, 512))] + [ANY] * ns,
        out_specs=[row, pl.BlockSpec((PROJ_TM, NP), lambda i: (i, 0)), pl.BlockSpec((PROJ_TM, LRW), lambda i: (i, 0)),
                   pl.BlockSpec((PROJ_TM, 256), lambda i: (i, 0)), pl.BlockSpec((2, PROJ_TM, 128), lambda i: (0, i, 0))] + [ANY] * ns,
        out_shape=[SDS((m, D), BF16), SDS((m, NP), BF16), SDS((m, LRW), F32), SDS((m, 256), BF16), SDS((2, m, 128), BF16)]
        + [SDS((4,) + a.shape, a.dtype) for a in share],
        scratch_shapes=([pltpu.SemaphoreType.DMA((3 * ns,)), pltpu.SemaphoreType.DMA((3 * ns,))] if ns else []),
        compiler_params=_cp(("arbitrary",), VMEM_BIG),
    )(x, ng, scale, shift, wit_g, wit_r, wlrt, ln_g, ln_b, *share)
    return res[0], res[1], res[2], res[3], res[4], list(res[5:])


def _proj_bwd(dp_g, dp_r, dlr, wit_g, wit_r, wlrt, x, dx1, ng, scale, send=(), share8=None, *, name, span, dx_buf=None):
    m = x.shape[0]
    ns = len(send)
    n8 = 0 if share8 is None else 1
    nbuf = 0 if dx_buf is None else 1
    first, steps = span
    masks = [(mx, my, mc) for mx in range(2) for my in range(2) for mc in range(2)][1:]

    def body(*refs):
        (dpg_ref, dpr_ref, dlr_ref, wg_ref, wr_ref, wl_ref, x_ref, r_ref, g_ref, sc_ref) = refs[:10]
        send_refs = refs[10:10 + ns]
        n_in = 10 + ns + n8 + nbuf
        dx_ref, dg_ref, dsc_ref, dsh_ref = refs[n_in:n_in + 4]
        got_refs = refs[n_in + 4:n_in + 4 + ns]
        sems = refs[n_in + 4 + ns + n8:]
        i = pl.program_id(0)

        def copies():
            cx, cy, cc = _coords()
            me = 2 * cx + cy
            peers = [(1 - cx, cy), (cx, 1 - cy), (1 - cx, 1 - cy)]
            out, back = [], []
            for k in range(ns):
                for j, (px, py) in enumerate(peers):
                    out.append(_remote(send_refs[k].at[2 * px + py], got_refs[k].at[me], sems[0].at[3 * k + j],
                                       sems[1].at[3 * k + j], (px, py, cc)))
                    landed = got_refs[k].at[2 * px + py]
                    back.append(_remote(landed, landed, sems[0].at[3 * k + j], sems[1].at[3 * k + j], (px, py, cc)))
            if n8:
                src8, all8 = refs[10 + ns], refs[n_in + 4 + ns]
                s8, r8 = sems[-2], sems[-1]
                for j, (mx, my, mc) in enumerate(masks):
                    px, py, pc = _flip(cx, mx), _flip(cy, my), _flip(cc, mc)
                    out.append(_remote(src8, all8.at[4 * cx + 2 * cy + cc], s8.at[j], r8.at[j], (px, py, pc)))
                    landed = all8.at[4 * px + 2 * py + pc]
                    back.append(_remote(landed, landed, s8.at[j], r8.at[j], (px, py, pc)))
            return out, back

        @pl.when(i == 0)
        def _():
            dg_ref[...] = jnp.zeros_like(dg_ref)
            dsc_ref[...] = jnp.zeros_like(dsc_ref)
            dsh_ref[...] = jnp.zeros_like(dsh_ref)
            if ns or n8:
                for rc in copies()[0]:
                    rc.start()

        dh_ = (_nn(dpg_ref[...], wg_ref[...]) + _nn(dpr_ref[...], wr_ref[...])
               + _nn(dlr_ref[...].astype(BF16), wl_ref[...]))
        xf = x_ref[...]
        r = lax.rsqrt(jnp.mean(xf * xf, axis=-1, keepdims=True) + EPS)
        xh = xf * r
        y = xh * g_ref[...]
        dsh_ref[...] += jnp.sum(dh_, axis=0, keepdims=True)
        dsc_ref[...] += jnp.sum(dh_ * y, axis=0, keepdims=True)
        dy = dh_ * (1.0 + sc_ref[...])
        dg_ref[...] += jnp.sum(dy * xh, axis=0, keepdims=True)
        dxh = dy * g_ref[...]
        dx_ref[...] = r * (dxh - xh * jnp.mean(dxh * xh, axis=-1, keepdims=True)) + r_ref[...]

        if ns or n8:
            @pl.when(i == steps - 1)
            def _():
                out, back = copies()
                for rc in back:
                    rc.wait_recv()
                for rc in out:
                    rc.wait_send()

    row = pl.BlockSpec((PROJ_TM, D), lambda i: (i + first, 0))
    vec = _full((1, D))
    kg, kr = dp_g.shape[1], dp_r.shape[1]
    extra_in = list(send) + ([share8] if n8 else []) + ([dx_buf] if nbuf else [])
    extra_out = [SDS(a.shape, a.dtype) for a in send] + ([SDS((8,) + share8.shape, share8.dtype)] if n8 else [])
    res = _pcall(
        body, name=name, grid=(steps,),
        in_specs=[pl.BlockSpec((PROJ_TM, kg), lambda i: (i + first, 0)), pl.BlockSpec((PROJ_TM, kr), lambda i: (i + first, 0)),
                  pl.BlockSpec((PROJ_TM, LRW), lambda i: (i + first, 0)), _resident((kg, D)), _resident((kr, D)),
                  _resident((LRW, D)), row, row, vec, vec] + [ANY] * len(extra_in),
        out_specs=[row, vec, vec, vec] + [ANY] * len(extra_out),
        out_shape=[SDS((m, D), F32), SDS((1, D), F32), SDS((1, D), F32), SDS((1, D), F32)] + extra_out,
        scratch_shapes=(([pltpu.SemaphoreType.DMA((3 * ns,)), pltpu.SemaphoreType.DMA((3 * ns,))] if ns else [])
                        + ([pltpu.SemaphoreType.DMA((7,)), pltpu.SemaphoreType.DMA((7,))] if n8 else [])),
        input_output_aliases=({10 + ns + n8: 0} if nbuf else {}),
        compiler_params=_cp(("arbitrary",), VMEM_BIG),
    )(dp_g, dp_r, dlr, wit_g, wit_r, wlrt, x, dx1, ng, scale, *extra_in)
    return tuple(res[:4]), list(res[4:4 + ns]), (res[4 + ns] if n8 else None)


def _prep_bwd(x, dh, dx1, ng, scale, name):
    m = x.shape[0]
    has_res = dx1 is not None

    def body(*refs):
        if has_res:
            x_ref, dh_ref, r_ref, g_ref, sc_ref, dx_ref, dg_ref, dsc_ref, dsh_ref = refs
        else:
            x_ref, dh_ref, g_ref, sc_ref, dx_ref, dg_ref, dsc_ref, dsh_ref = refs
        i = pl.program_id(0)

        @pl.when(i == 0)
        def _():
            dg_ref[...] = jnp.zeros_like(dg_ref)
            dsc_ref[...] = jnp.zeros_like(dsc_ref)
            dsh_ref[...] = jnp.zeros_like(dsh_ref)

        xf = x_ref[...]
        dh_ = dh_ref[...]
        r = lax.rsqrt(jnp.mean(xf * xf, axis=-1, keepdims=True) + EPS)
        xh = xf * r
        y = xh * g_ref[...]
        dsh_ref[...] += jnp.sum(dh_, axis=0, keepdims=True)
        dsc_ref[...] += jnp.sum(dh_ * y, axis=0, keepdims=True)
        dy = dh_ * (1.0 + sc_ref[...])
        dg_ref[...] += jnp.sum(dy * xh, axis=0, keepdims=True)
        dxh = dy * g_ref[...]
        dx = r * (dxh - xh * jnp.mean(dxh * xh, axis=-1, keepdims=True))
        if has_res:
            dx = dx + r_ref[...]
        dx_ref[...] = dx

    tok = min(TOK, m)
    row = pl.BlockSpec((tok, D), lambda i: (i, 0))
    vec = _full((1, D))
    in_specs = [row, row] + ([row] if has_res else []) + [vec, vec]
    args = [x, dh] + ([dx1] if has_res else []) + [ng, scale]
    return _pcall(
        body, name=name, grid=(m // tok,), in_specs=in_specs, out_specs=[row, vec, vec, vec],
        out_shape=[SDS((m, D), F32), SDS((1, D), F32), SDS((1, D), F32), SDS((1, D), F32)],
        compiler_params=_cp(("arbitrary",)),
    )(*args)


COLB = 2048


def _colmix_fwd(vnc, ws23, bs23):
    rows = vnc.shape[2] // COLB

    def body(v_ref, w_ref, b_ref, o_ref):
        o_ref[0] = _nn(w_ref[0], v_ref[0]) + b_ref[0]

    return _pcall(
        body, name="colmix_fwd", grid=(2, rows),
        in_specs=[pl.BlockSpec((1, AC, COLB), lambda g, j: (g, 0, j)), pl.BlockSpec((1, AC, AC), lambda g, j: (g, 0, 0)),
                  pl.BlockSpec((1, AC, 1), lambda g, j: (g, 0, 0))],
        out_specs=pl.BlockSpec((1, AC, COLB), lambda g, j: (g, 0, j)),
        out_shape=SDS(vnc.shape, F32), compiler_params=_cp(("parallel", "parallel")),
    )(vnc, ws23, bs23)


def _colmix_bwd(dsvc, vnc, ws23t):
    rows = vnc.shape[2] // COLB

    def body(d_ref, v_ref, wt_ref, dv_ref, dw_ref, db_ref):
        j = pl.program_id(1)

        @pl.when(j == 0)
        def _():
            dw_ref[...] = jnp.zeros_like(dw_ref)
            db_ref[...] = jnp.zeros_like(db_ref)

        d = d_ref[0]
        d16 = d.astype(BF16)
        dv_ref[0] = _nn(wt_ref[0], d16)
        dw_ref[0] += _nt(d16, v_ref[0])
        db_ref[0] += jnp.sum(d, axis=1, keepdims=True)

    blk = pl.BlockSpec((1, AC, COLB), lambda g, j: (g, 0, j))
    return _pcall(
        body, name="colmix_bwd", grid=(2, rows),
        in_specs=[blk, blk, pl.BlockSpec((1, AC, AC), lambda g, j: (g, 0, 0))],
        out_specs=[blk, pl.BlockSpec((1, AC, AC), lambda g, j: (g, 0, 0)), pl.BlockSpec((1, AC, 1), lambda g, j: (g, 0, 0))],
        out_shape=[SDS(vnc.shape, F32), SDS((2, AC, AC), F32), SDS((2, AC, 1), F32)],
        compiler_params=_cp(("parallel", "arbitrary")),
    )(dsvc, vnc, ws23t)


def _head_norm(o, gbn):
    out = []
    for h in range(4):
        oh = o[:, 128 * h:128 * h + 128]
        r = lax.rsqrt(jnp.mean(oh * oh, axis=-1, keepdims=True) + EPS)
        out.append((r, oh * r))
    return out


def _tail_fwd(o_f, o_b, p, vnr, svc, x, tgt, ws01, bs01, gbn, wpa, wpb, wo, gate, gf):
    m = p.shape[0]

    def body(of_ref, ob_ref, zb_ref, ua_ref, za_ref, ga_ref, gb_ref, vnr_ref, svc_ref, x_ref, t_ref, w_ref, b_ref, g_ref,
             wpa_ref, wpb_ref, wo_ref, gate_ref, gf_ref,
             ya_ref, yb_ref, svr_ref, dwo_ref, dx1_ref, dout_ref, loss_ref, dgate_ref, dgf_ref):
        i = pl.program_id(0)

        @pl.when(i == 0)
        def _():
            loss_ref[...] = jnp.zeros_like(loss_ref)
            dgate_ref[...] = jnp.zeros_like(dgate_ref)
            dgf_ref[...] = jnp.zeros_like(dgf_ref)
            dwo_ref[...] = jnp.zeros_like(dwo_ref)

        o = of_ref[...] + ob_ref[...]
        zb = zb_ref[...].astype(F32)
        for h, (r, xh) in enumerate(_head_norm(o, None)):
            sl = slice(128 * h, 128 * h + 128)
            yb_ref[:, sl] = ((xh * g_ref[:, sl]) * _silu(zb[:, sl])).astype(BF16)
        for j in range(TOK // AC):
            for g in range(2):
                sv = _nn(w_ref[g], vnr_ref[AC * j:AC * j + AC, AC * g:AC * g + AC]) + b_ref[g]
                svr_ref[AC * j:AC * j + AC, AC * g:AC * g + AC] = sv
        sz = _silu(za_ref[...].astype(F32))
        u = ua_ref[...].astype(F32)
        ya_ref[:, 0:256] = ((u[:, 0:256] * svr_ref[...]) * sz[:, 0:256]).astype(BF16)
        ya_ref[:, 256:384] = ((u[:, 256:384] * svc_ref[0]) * sz[:, 256:384]).astype(BF16)
        ya_ref[:, 384:512] = ((u[:, 384:512] * svc_ref[1]) * sz[:, 384:512]).astype(BF16)
        ya = _nn(ya_ref[...], wpa_ref[...])
        yb = _nn(yb_ref[...], wpb_ref[...])
        mg = (jax.nn.sigmoid(ga_ref[...].astype(F32)) * ya + jax.nn.sigmoid(gb_ref[...].astype(F32)) * yb).astype(BF16)
        out_ = _nn(mg, wo_ref[...])
        x1 = x_ref[...] + gate_ref[...] * out_
        r = lax.rsqrt(jnp.mean(x1 * x1, axis=-1, keepdims=True) + EPS)
        xh = x1 * r
        err = xh * gf_ref[...] - t_ref[...]
        loss_ref[...] += 0.5 * jnp.sum(jnp.mean(err * err, axis=-1, keepdims=True), axis=0, keepdims=True)
        dy = err * (1.0 / D)
        dgf_ref[...] += jnp.sum(dy * xh, axis=0, keepdims=True)
        dxh = dy * gf_ref[...]
        dx1 = r * (dxh - xh * jnp.mean(dxh * xh, axis=-1, keepdims=True))
        dx1_ref[...] = dx1
        dout16 = (gate_ref[...] * dx1).astype(BF16)
        dout_ref[...] = dout16
        dgate_ref[...] += jnp.sum(dx1 * out_, axis=0, keepdims=True)
        dwo_ref[...] += _tn(mg, dout16)

    r512 = pl.BlockSpec((TOK, 512), lambda i: (i, 0))
    row = pl.BlockSpec((TOK, D), lambda i: (i, 0))
    vec = _full((1, D))
    return _pcall(
        body, name="tail_fwd", grid=(m // TOK,),
        in_specs=[r512, r512, pl.BlockSpec((TOK, 512), lambda i: (i, 6)), pl.BlockSpec((TOK, 512), lambda i: (i, 7)),
                  pl.BlockSpec((TOK, 512), lambda i: (i, 8)), row, pl.BlockSpec((TOK, D), lambda i: (i, 1)),
                  pl.BlockSpec((TOK, 256), lambda i: (i, 0)), pl.BlockSpec((2, TOK, 128), lambda i: (0, i, 0)), row, row,
                  _full((2, AC, AC)), _full((2, AC, 1)), _full((1, 512)), _resident((512, D)), _resident((512, D)),
                  _resident((D, D)), vec, vec],
        out_specs=[r512, r512, pl.BlockSpec((TOK, 256), lambda i: (i, 0)), _resident((D, D)), row, row, _full((1, 128)), vec, vec],
        out_shape=[SDS((m, 512), BF16), SDS((m, 512), BF16), SDS((m, 256), F32), SDS((D, D), F32), SDS((m, D), F32),
                   SDS((m, D), BF16), SDS((1, 128), F32), SDS((1, D), F32), SDS((1, D), F32)],
        compiler_params=_cp(("arbitrary",), VMEM_BIG),
    )(o_f, o_b, p, p, p, p, p, vnr, svc, x, tgt, ws01, bs01, gbn, wpa, wpb, wo, gate, gf)


DPR = 3072


def _tail_bwd(dout, ya_in, yb_in, p, svr, svc, o_f, o_b, gbn, wo, wpa, wpb):
    m = p.shape[0]

    def body(dout_ref, ya_ref, yb_ref, ga_ref, gb_ref, zb_ref, ua_ref, za_ref, svr_ref, svc_ref, of_ref, ob_ref, g_ref,
             wo_ref, wpa_ref, wpb_ref,
             dwpa_ref, dwpb_ref, dpg_ref, dpr_ref, dsr_ref, dsc_ref, do_ref, dg_ref):
        i = pl.program_id(0)

        @pl.when(i == 0)
        def _():
            dg_ref[...] = jnp.zeros_like(dg_ref)
            dwpa_ref[...] = jnp.zeros_like(dwpa_ref)
            dwpb_ref[...] = jnp.zeros_like(dwpb_ref)

        dm_ = _nt(dout_ref[...], wo_ref[...])
        ya_in, yb_in = ya_ref[...], yb_ref[...]
        ya = _nn(ya_in, wpa_ref[...])
        yb = _nn(yb_in, wpb_ref[...])
        sa = jax.nn.sigmoid(ga_ref[...].astype(F32))
        sb = jax.nn.sigmoid(gb_ref[...].astype(F32))
        dya16 = (dm_ * sa).astype(BF16)
        dyb16 = (dm_ * sb).astype(BF16)
        dwpa_ref[...] += _tn(ya_in, dya16)
        dwpb_ref[...] += _tn(yb_in, dyb16)
        dpg_ref[:, 0:D] = (dm_ * ya * (sa * (1.0 - sa))).astype(BF16)
        dpg_ref[:, D:2 * D] = (dm_ * yb * (sb * (1.0 - sb))).astype(BF16)
        dya = _nt(dya16, wpa_ref[...])
        dyb = _nt(dyb16, wpb_ref[...])

        u = ua_ref[...].astype(F32)
        za = za_ref[...].astype(F32)
        sz, dsz = _silu_and_grad(za)
        sv = jnp.concatenate([svr_ref[...], svc_ref[0], svc_ref[1]], axis=1)
        dpr_ref[:, 512:1024] = (dya * sv * sz).astype(BF16)
        dsv = dya * u * sz
        dsr_ref[...] = dsv[:, 0:256]
        dsc_ref[0] = dsv[:, 256:384]
        dsc_ref[1] = dsv[:, 384:512]
        dpr_ref[:, 1024:1536] = (dya * u * sv * dsz).astype(BF16)

        zb = zb_ref[...].astype(F32)
        o = of_ref[...] + ob_ref[...]
        szb, dszb = _silu_and_grad(zb)
        for h, (r, xh) in enumerate(_head_norm(o, None)):
            sl = slice(128 * h, 128 * h + 128)
            gh = g_ref[:, sl]
            don = dyb[:, sl] * szb[:, sl]
            dpr_ref[:, sl] = (dyb[:, sl] * (xh * gh) * dszb[:, sl]).astype(BF16)
            dg_ref[:, sl] += jnp.sum(don * xh, axis=0, keepdims=True)
            dxh = don * gh
            do_ref[:, sl] = (r * (dxh - xh * jnp.mean(dxh * xh, axis=-1, keepdims=True))).astype(BF16)

    r512 = pl.BlockSpec((TOK, 512), lambda i: (i, 0))
    row = pl.BlockSpec((TOK, D), lambda i: (i, 0))
    return _pcall(
        body, name="tail_bwd", grid=(m // TOK,),
        in_specs=[row, r512, r512, row, pl.BlockSpec((TOK, D), lambda i: (i, 1)), pl.BlockSpec((TOK, 512), lambda i: (i, 6)),
                  pl.BlockSpec((TOK, 512), lambda i: (i, 7)), pl.BlockSpec((TOK, 512), lambda i: (i, 8)),
                  pl.BlockSpec((TOK, 256), lambda i: (i, 0)), pl.BlockSpec((2, TOK, 128), lambda i: (0, i, 0)), r512, r512,
                  _full((1, 512)), _resident((D, D)), _resident((512, D)), _resident((512, D))],
        out_specs=[_resident((512, D)), _resident((512, D)), pl.BlockSpec((TOK, 2 * D), lambda i: (i, 0)),
                   pl.BlockSpec((TOK, 1536), lambda i: (i, 0)),
                   pl.BlockSpec((TOK, 256), lambda i: (i, 0)), pl.BlockSpec((2, TOK, 128), lambda i: (0, i, 0)), r512, _full((1, 512))],
        out_shape=[SDS((512, D), F32), SDS((512, D), F32), SDS((m, 2 * D), BF16), SDS((m, DPR), BF16), SDS((m, 256), F32),
                   SDS((2, m, 128), F32), SDS((m, 512), BF16), SDS((1, 512), F32)],
        compiler_params=_cp(("arbitrary",), VMEM_BIG),
    )(dout, ya_in, yb_in, p, p, p, p, p, svr, svc, o_f, o_b, gbn, wo, wpa, wpb)


def _ln_bwd(dsr, vnr, dvnc, p, ws01t, ln_g, dp):
    m = p.shape[0]

    def body(dsr_ref, vnr_ref, dvc_ref, va_ref, wt_ref, g_ref, dpi_ref, dp_ref, dw_ref, db_ref, dlg_ref, dlb_ref, dvn_ref):
        i = pl.program_id(0)

        @pl.when(i == 0)
        def _():
            dw_ref[...] = jnp.zeros_like(dw_ref)
            db_ref[...] = jnp.zeros_like(db_ref)
            dlg_ref[...] = jnp.zeros_like(dlg_ref)
            dlb_ref[...] = jnp.zeros_like(dlb_ref)

        for j in range(TOK // AC):
            for g in range(2):
                d = dsr_ref[AC * j:AC * j + AC, AC * g:AC * g + AC]
                d16 = d.astype(BF16)
                dvn_ref[AC * j:AC * j + AC, AC * g:AC * g + AC] = _nn(wt_ref[g], d16)
                dw_ref[g] += _nt(d16, vnr_ref[AC * j:AC * j + AC, AC * g:AC * g + AC])
                db_ref[g] += jnp.sum(d, axis=1, keepdims=True)
        dvn_ref[:, 256:384] = dvc_ref[0]
        dvn_ref[:, 384:512] = dvc_ref[1]
        dvn = dvn_ref[...]
        xf = va_ref[...].astype(F32)
        xc = xf - jnp.mean(xf, axis=-1, keepdims=True)
        rs = lax.rsqrt(jnp.mean(xc * xc, axis=-1, keepdims=True) + EPS)
        xh = xc * rs
        dlg_ref[...] += jnp.sum(dvn * xh, axis=0, keepdims=True)
        dlb_ref[...] += jnp.sum(dvn, axis=0, keepdims=True)
        dxh = dvn * g_ref[...]
        dva = rs * (dxh - jnp.mean(dxh, axis=-1, keepdims=True) - xh * jnp.mean(dxh * xh, axis=-1, keepdims=True))
        dp_ref[...] = dva.astype(BF16)

    return _pcall(
        body, name="ln_bwd", grid=(m // TOK,),
        in_specs=[pl.BlockSpec((TOK, 256), lambda i: (i, 0)), pl.BlockSpec((TOK, 256), lambda i: (i, 0)),
                  pl.BlockSpec((2, TOK, 128), lambda i: (0, i, 0)), pl.BlockSpec((TOK, 512), lambda i: (i, 9)),
                  _full((2, AC, AC)), _full((1, 512)), pl.BlockSpec(memory_space=pl.ANY)],
        out_specs=[pl.BlockSpec((TOK, 512), lambda i: (i, 3)), _full((2, AC, AC)), _full((2, AC, 1)), _full((1, 512)), _full((1, 512))],
        out_shape=[SDS((m, DPR), BF16), SDS((2, AC, AC), F32), SDS((2, AC, 1), F32), SDS((1, 512), F32), SDS((1, 512), F32)],
        scratch_shapes=[pltpu.VMEM((TOK, 512), F32)],
        input_output_aliases={6: 0}, compiler_params=_cp(("arbitrary",)),
    )(dsr, vnr, dvnc, p, ws01t, ln_g, dp)


def _tri_mm(tri, a):
    a1 = a.astype(BF16)
    r1 = a - a1.astype(F32)
    a2 = r1.astype(BF16)
    a3 = (r1 - a2.astype(F32)).astype(BF16)
    n = a.shape[1]
    r = _nn(tri, jnp.concatenate([a1, a2, a3], axis=1))
    return r[:, 0:n] + r[:, n:2 * n] + r[:, 2 * n:3 * n]


def _gla_masks(reverse):
    ri = lax.broadcasted_iota(jnp.int32, (CH, CH), 0)
    ci = lax.broadcasted_iota(jnp.int32, (CH, CH), 1)
    vis = (ci >= ri) if reverse else (ci <= ri)
    vis_t = (ci <= ri) if reverse else (ci >= ri)
    r4 = lax.broadcasted_iota(jnp.int32, (4 * CH, CH), 0) & (CH - 1)
    c4 = lax.broadcasted_iota(jnp.int32, (4 * CH, CH), 1)
    vis4 = (c4 >= r4) if reverse else (c4 <= r4)
    vis4_t = (c4 <= r4) if reverse else (c4 >= r4)
    lane = lax.broadcasted_iota(jnp.int32, (1, 256), 1)
    hm = [(lane >= CH * h) & (lane < CH * h + CH) for h in range(4)]
    return vis, vis_t, vis4, vis4_t, hm


def _stack_heads(x, hm):
    return jnp.concatenate([jnp.where(hm[h], x, 0.0).astype(BF16) for h in range(4)], axis=0)


def _diag_heads(full, hm):
    r = full.shape[0] // 4
    acc = jnp.where(hm[0], full[0:r], 0.0)
    for h in range(1, 4):
        acc = acc + jnp.where(hm[h], full[r * h:r * h + r], 0.0)
    return acc


def _rows_of_heads(x):
    return jnp.concatenate([x[:, 128 * h:128 * h + 128] for h in range(4)], axis=0)


def _lane_vis(reverse, transpose):
    ri = lax.broadcasted_iota(jnp.int32, (CH, 4 * CH), 0)
    ci = lax.broadcasted_iota(jnp.int32, (CH, 4 * CH), 1) & (CH - 1)
    return (ci >= ri) if (reverse != transpose) else (ci <= ri)


def _gla_fwd2(p, qkv_blk, lr, lrws, gbiases, s0s, name):
    m = p.shape[0]
    tb = min(GLA_TB, m)
    nb = m // tb
    nc = tb // CH

    def body(qkv_f, lr_f, qkv_b, lr_b, lrw_f, lrw_b, gb_f, gb_b, s0_f, s0_b,
             o_f, sb_f, sfin_f, o_b, sb_b, sfin_b, st_f, st_b):
        i = pl.program_id(0)

        @pl.when(i == 0)
        def _():
            st_f[...] = s0_f[...]
            st_b[...] = s0_b[...]

        dirs = []
        for reverse, qkv_ref, lr_ref, lrw_ref, gb_ref, o_ref, sb_ref, st_ref in (
                (False, qkv_f, lr_f, lrw_f, gb_f, o_f, sb_f, st_f), (True, qkv_b, lr_b, lrw_b, gb_b, o_b, sb_b, st_b)):
            vis, _, vis4, _, hm = _gla_masks(reverse)
            logits = _nn(lr_ref[...].astype(BF16), lrw_ref[...]) + gb_ref[...]
            dirs.append(dict(reverse=reverse, qkv=qkv_ref, o=o_ref, sb=sb_ref, vis4=vis4, hm=hm,
                             tri=vis.astype(F32).astype(BF16), a=_logsig(logits) * (1.0 / 16.0), st=st_ref[...]))
        for step in range(nc):
            for d in dirs:
                c = nc - 1 - step if d["reverse"] else step
                rows = slice(CH * c, CH * c + CH)
                b = _tri_mm(d["tri"], d["a"][rows])
                bl = b[0:1] if d["reverse"] else b[CH - 1:CH]
                q = d["qkv"][rows, 0:256].astype(F32) * 0.125
                k = d["qkv"][rows, 256:512].astype(F32)
                v16 = d["qkv"][rows, 512:1024].astype(BF16)
                qd = q * jnp.exp(b)
                kd16 = (k * jnp.exp(-b)).astype(BF16)
                kdec16 = (k * jnp.exp(bl - b)).astype(BF16)
                qstack = _stack_heads(qd, d["hm"])
                sc = jnp.where(d["vis4"], _nt(qstack, kd16), 0.0).astype(BF16)
                inter = _nt(qstack, d["st"].astype(BF16))
                for h in range(4):
                    d["o"][rows, 128 * h:128 * h + 128] = (
                        _nn(sc[CH * h:CH * h + CH], v16[:, 128 * h:128 * h + 128]) + inter[CH * h:CH * h + CH])
                d["sb"][c] = d["st"]
                d["st"] = d["st"] * jnp.exp(bl) + _diag_heads(_tn(v16, kdec16), d["hm"])
        st_f[...] = dirs[0]["st"]
        st_b[...] = dirs[1]["st"]

        @pl.when(i == nb - 1)
        def _():
            sfin_f[...] = dirs[0]["st"]
            sfin_b[...] = dirs[1]["st"]

    fw = lambda i: i
    bw = lambda i: nb - 1 - i
    in_specs = []
    for rm in (fw, bw):
        in_specs += [pl.BlockSpec((tb, 1024), lambda i, rm=rm: (rm(i), qkv_blk)), pl.BlockSpec((tb, LRW), lambda i, rm=rm: (rm(i), 0))]
    in_specs += [_full((LRW, 256))] * 2 + [_full((1, 256))] * 2 + [_full((128, 256))] * 2
    out_specs, out_shape = [], []
    for rm in (fw, bw):
        out_specs += [pl.BlockSpec((tb, 512), lambda i, rm=rm: (rm(i), 0)), pl.BlockSpec((nc, 128, 256), lambda i, rm=rm: (rm(i), 0, 0)),
                      _full((128, 256))]
        out_shape += [SDS((m, 512), F32), SDS((m // CH, 128, 256), F32), SDS((128, 256), F32)]
    return _pcall(
        body, name=name, grid=(nb,), in_specs=in_specs, out_specs=out_specs, out_shape=out_shape,
        scratch_shapes=[pltpu.VMEM((128, 256), F32), pltpu.VMEM((128, 256), F32)], compiler_params=_cp(("arbitrary",), VMEM_BIG),
    )(p, lr, p, lr, lrws[0], lrws[1], gbiases[0], gbiases[1], s0s[0], s0s[1])


def _gla_bwd(p, qkv_blk, lr, lrw, lrwt, gbias, sb, dsfin, do, prev, dp, *, reverse, name):
    m = p.shape[0]
    tb = min(GLA_TB, m)
    nb = m // tb
    nc = tb // CH
    rmap = (lambda i: i) if reverse else (lambda i: nb - 1 - i)
    has_prev = prev is not None
    has_dp = dp is not None

    def body(*refs):
        refs = list(refs)
        qkv_ref, lr_ref, lrw_ref, lrwt_ref, gb_ref, sb_ref, dsfin_ref, do_ref = refs[:8]
        refs = refs[8:]
        if has_prev:
            pq_ref, plr_ref = refs[:2]
            refs = refs[2:]
        if has_dp:
            refs = refs[1:]
        dqkv_ref, dlr_ref, dw2_ref, dgb_ref, ds0_ref, dst_ref, dlog_ref = refs
        i = pl.program_id(0)

        @pl.when(i == 0)
        def _():
            dst_ref[...] = dsfin_ref[...]
            dw2_ref[...] = jnp.zeros_like(dw2_ref)
            dgb_ref[...] = jnp.zeros_like(dgb_ref)

        vis, vis_t, vis4, vis4_t, hm = _gla_masks(reverse)
        tri = vis.astype(F32).astype(BF16)
        tri_t = vis_t.astype(F32).astype(BF16)
        lane_vis = _lane_vis(reverse, False)
        lane_vis_t = _lane_vis(reverse, True)
        lr16 = lr_ref[...].astype(BF16)
        logits = _nn(lr16, lrw_ref[...]) + gb_ref[...]
        a_all = _logsig(logits) * (1.0 / 16.0)
        dsig = (1.0 - jax.nn.sigmoid(logits)) * (1.0 / 16.0)
        dst = dst_ref[...]
        for c in (range(nc) if reverse else range(nc - 1, -1, -1)):
            rows = slice(CH * c, CH * c + CH)
            b = _tri_mm(tri, a_all[rows])
            bl = b[0:1] if reverse else b[CH - 1:CH]
            eb = jnp.exp(b)
            enb = jnp.exp(-b)
            ebl = jnp.exp(bl - b)
            el = jnp.exp(bl)
            q = qkv_ref[rows, 0:256].astype(F32) * 0.125
            k = qkv_ref[rows, 256:512].astype(F32)
            v16 = qkv_ref[rows, 512:1024].astype(BF16)
            do16 = do_ref[rows, :].astype(BF16)
            qd = q * eb
            kd = k * enb
            kdec = k * ebl
            st = sb_ref[c]
            st16 = st.astype(BF16)
            dst16 = dst.astype(BF16)
            qd16 = qd.astype(BF16)
            kd16 = kd.astype(BF16)
            qstack = _stack_heads(qd, hm)
            kstack = _stack_heads(kd, hm)
            kdecstack = _stack_heads(kdec, hm)
            pt = jnp.where(vis4_t, _nt(kstack, qd16), 0.0).astype(BF16)
            dvinter = _nt(kdecstack, dst16)
            do_rows = _rows_of_heads(do16)
            v_rows = _rows_of_heads(v16)
            dp_cat = jnp.where(lane_vis, _diag_heads(_nt(do_rows, v_rows), hm), 0.0).astype(BF16)
            dpt_cat = jnp.where(lane_vis_t, _diag_heads(_nt(v_rows, do_rows), hm), 0.0).astype(BF16)
            dqd = _nn(dp_cat, kstack) + _diag_heads(_nn(do_rows, st16), hm)
            dkd = _nn(dpt_cat, qstack)
            dkdec = _diag_heads(_nn(v_rows, dst16), hm)
            for h in range(4):
                rh = slice(CH * h, CH * h + CH)
                dv_h = _nn(pt[rh], do_rows[rh]) + dvinter[rh]
                if has_prev:
                    dv_h = dv_h + pq_ref[rows, 512 + 128 * h:512 + 128 * h + 128]
                dqkv_ref[rows, 512 + 128 * h:512 + 128 * h + 128] = dv_h.astype(dqkv_ref.dtype)
            dq = dqd * eb * 0.125
            dk = dkd * enb + dkdec * ebl
            if has_prev:
                dq = dq + pq_ref[rows, 0:256]
                dk = dk + pq_ref[rows, 256:512]
            dqkv_ref[rows, 0:256] = dq.astype(dqkv_ref.dtype)
            dqkv_ref[rows, 256:512] = dk.astype(dqkv_ref.dtype)
            g_kdec = dkdec * kdec
            db = dqd * qd - dkd * kd - g_kdec
            dbl = jnp.sum(g_kdec, axis=0, keepdims=True) + jnp.sum(st * dst, axis=0, keepdims=True) * el
            da = _tri_mm(tri_t, db) + dbl
            dlog_ref[rows, :] = da * dsig[rows]
            dst = dst * el + _diag_heads(_tn(do16, qd16), hm)
        dst_ref[...] = dst
        dlog = dlog_ref[...]
        dlog16 = dlog.astype(BF16)
        dlr = _nn(dlog16, lrwt_ref[...])
        if has_prev:
            dlr = dlr + plr_ref[...]
        dlr_ref[...] = dlr
        dw2_ref[...] += _tn(lr16, dlog16)
        dgb_ref[...] += jnp.sum(dlog, axis=0, keepdims=True)

        @pl.when(i == nb - 1)
        def _():
            ds0_ref[...] = dst

    in_specs = [pl.BlockSpec((tb,1024), lambda i: (rmap(i), qkv_blk)), pl.BlockSpec((tb,LRW), lambda i: (rmap(i), 0)),
                _full((LRW, 256)), _full((256, LRW)), _full((1, 256)), pl.BlockSpec((nc, 128, 256), lambda i: (rmap(i), 0, 0)),
                _full((128, 256)), pl.BlockSpec((tb,512), lambda i: (rmap(i), 0))]
    args = [p, lr, lrw, lrwt, gbias, sb, dsfin, do]
    if has_prev:
        in_specs += [pl.BlockSpec((tb,1024), lambda i: (rmap(i), 0)), pl.BlockSpec((tb,LRW), lambda i: (rmap(i), 0))]
        args += list(prev)
    aliases = {}
    if has_dp:
        in_specs.append(pl.BlockSpec(memory_space=pl.ANY))
        aliases = {len(args): 0}
        args.append(dp)
        dq_spec = pl.BlockSpec((tb,1024), lambda i: (rmap(i), 2))
        dq_shape = SDS(dp.shape, dp.dtype)
    else:
        dq_spec = pl.BlockSpec((tb,1024), lambda i: (rmap(i), 0))
        dq_shape = SDS((m, 1024), F32)
    return _pcall(
        body, name=name, grid=(nb,), in_specs=in_specs,
        out_specs=[dq_spec, pl.BlockSpec((tb,LRW), lambda i: (rmap(i), 0)), _full((LRW, 256)), _full((1, 256)), _full((128, 256))],
        out_shape=[dq_shape, SDS((m, LRW), F32), SDS((LRW, 256), F32), SDS((1, 256), F32), SDS((128, 256), F32)],
        scratch_shapes=[pltpu.VMEM((128, 256), F32), pltpu.VMEM((tb,256), F32)],
        input_output_aliases=aliases, compiler_params=_cp(("arbitrary",)),
    )(*args)


EARLY_KEYS = ["dmodc", "dscc", "dng_c", "dlng", "dlnb", "dws", "dbs", "dgbn", "dgf", "dw2", "dgb2", "loss", "dgate"]
EARLY_SHAPES = [(1, 2 * D), (D,), (1, D), (1, 512), (1, 512), (1, 4, 128, 128), (1, 4, 128), (1, 512), (D,), (2, 16, 256), (2, 256),
                (128,), (1, D)]
EARLY_SIZE = 2 * D + D + D + 512 + 512 + 4 * 128 * 128 + 512 + 512 + D + 2 * 16 * 256 + 512 + 128 + D
EARLY_ROWS = 648


def _device_step(x, c, ctx, c_ctx, tgt, wm, bm, ng, wit_g, wit_r, wlrt, ln_g, ln_b, ws, bs, w2, gb2, gbn, wpa, wpb, wo, gf,
                 exchange=None, shards=None):
    L = x.shape[0]
    wit_qkv = wit_r[2048:3072]
    ws16 = ws.astype(BF16)
    wst16 = jnp.swapaxes(ws, 1, 2).astype(BF16)
    bscol = bs[:, :, None]
    lrw = [jnp.zeros((LRW, 256), F32).at[16 * r:16 * r + 16].set(w2[r]).astype(BF16) for r in range(2)]
    lrwt = [w.T for w in lrw]
    gbias = [gb2[r:r + 1] for r in range(2)]

    cc = jnp.zeros((8, D), F32).at[0:1].set(c).at[1:2].set(c_ctx)
    mod = _modvec(cc, wm, bm)
    shift, scale, gate = mod[0:1, 0:D], mod[0:1, D:2 * D], mod[0:1, 2 * D:3 * D]
    shift_c, scale_c = mod[1:2, 0:D], mod[1:2, D:2 * D]

    hc = _prep_h(ctx, ng, scale_c, shift_c, "prep_hc")
    pc = _mm(hc, wit_qkv, tm=256, tn=1024, tk=D, out_dtype=F32, name="mm_pc", b_t=True)
    plrc = _mm(hc, wlrt, tm=256, tn=LRW, tk=D, out_dtype=F32, name="mm_plrc", b_t=True)
    zero_s = jnp.zeros((128, 256), F32)
    _, sbc_f, sc_f, _, sbc_b, sc_b = _gla_fwd2(pc, 0, plrc, lrw, gbias, (zero_s, zero_s), "gla_fwd_c")

    h, p, plr, vnr, vnc, late = _proj_fwd(x, ng, scale, shift, wit_g, wit_r, wlrt, ln_g, ln_b,
                                          shards if shards is not None else ())
    if shards is not None:
        me_xy = 2 * lax.axis_index("x") + lax.axis_index("y")
        g_wpa, g_wpb, g_wo = (_own(g, s_, me_xy) for g, s_ in zip(late, shards))
        wpa = jnp.swapaxes(g_wpa, 0, 1).reshape(512, D)
        wpb = jnp.swapaxes(g_wpb, 0, 1).reshape(512, D)
        wo = g_wo.reshape(D, D)
    o_f, sb_f, _, o_b, sb_b, _ = _gla_fwd2(p, 2, plr, lrw, gbias, (sc_f, sc_b), "gla_fwd")
    svc = _colmix_fwd(vnc.reshape(2, AC, L), ws16[2:4], bscol[2:4]).reshape(2, L, 128)
    ya_in, yb_in, svr, dwo, dx1, dout, loss, dgate, dgf = _tail_fwd(
        o_f, o_b, p, vnr, svc, x, tgt, ws16[0:2], bscol[0:2], gbn, wpa, wpb, wo, gate, gf)

    dwpa, dwpb, dp_g, dp, dsr, dsc, do, dgbn = _tail_bwd(dout, ya_in, yb_in, p, svr, svc, o_f, o_b, gbn, wo, wpa, wpb)
    dvnc, dws23, dbs23 = _colmix_bwd(dsc.reshape(2, AC, L), vnc.reshape(2, AC, L), wst16[2:4])
    dp, dws01, dbs01, dlng, dlnb = _ln_bwd(dsr, vnr, dvnc.reshape(2, L, 128), p, wst16[0:2], ln_g, dp)
    zero_ds = jnp.zeros((128, 256), F32)
    dqkv_f, dlr_f, dw2_f, dgb_f, ds0_f = _gla_bwd(p, 2, plr, lrw[0], lrwt[0], gbias[0], sb_f, zero_ds, do, None, None,
                                                  reverse=False, name="gla_bwd_f")
    dp, dlr, dw2_b, dgb_b, ds0_b = _gla_bwd(p, 2, plr, lrw[1], lrwt[1], gbias[1], sb_b, zero_ds, do, (dqkv_f, dlr_f), dp,
                                            reverse=True, name="gla_bwd_b")
    zero_do = jnp.zeros((ctx.shape[0], 512), BF16)
    dqkvc_f, dlrc_f, dw2c_f, dgbc_f, _ = _gla_bwd(pc, 0, plrc, lrw[0], lrwt[0], gbias[0], sbc_f, ds0_f, zero_do, None, None,
                                                  reverse=False, name="gla_bwd_cf")
    dqkvc, dlrc, dw2c_b, dgbc_b, _ = _gla_bwd(pc, 0, plrc, lrw[1], lrwt[1], gbias[1], sbc_b, ds0_b, zero_do,
                                              (dqkvc_f, dlrc_f), None, reverse=True, name="gla_bwd_cb")
    dhc = _mm(dqkvc, wit_qkv, tm=256, tn=D, tk=1024, out_dtype=F32, name="mm_dhc")
    dhc = _mm(dlrc, wlrt, tm=256, tn=D, tk=LRW, out_dtype=F32, name="mm_dhc_lr", acc=dhc)
    _, dng_c, dscale_c, dshift_c = _prep_bwd(ctx, dhc, None, ng, scale_c, "prep_bwd_c")

    dwit_g = _mm_tn(dp_g, h, ta=1024, tn=D, tk=2048, name="mm_dwi_g")
    dwit_r = _mm_tn(dp, h, ta=1024, tn=D, tk=2048, name="mm_dwi_r")
    dwit_qkv = _mm_tn(dqkvc, hc, ta=1024, tn=D, tk=256, name="mm_dwi_c", acc=dwit_r[2048:3072])
    dwlrt = _mm_tn(dlr, h, ta=LRW, tn=D, tk=2048, name="mm_dwlr")
    dwlrt = _mm_tn(dlrc, hc, ta=LRW, tn=D, tk=256, name="mm_dwlr_c", acc=dwlrt)
    big = dict(dwit_g=dwit_g, dwit_r=dwit_r, dwit_qkv=dwit_qkv, dwlrt=dwlrt, dwpa=dwpa, dwpb=dwpb, dwo=dwo)

    dmodc = jnp.concatenate([dshift_c, dscale_c], axis=1)
    dscc = _dcctx(jnp.zeros((8, 2 * D), F32).at[0:1].set(dmodc), wm)[0:1]
    dw2p = dw2_f + dw2c_f, dw2_b + dw2c_b
    small = dict(
        dmodc=dmodc, dscc=dscc, dng_c=dng_c, dlng=dlng, dlnb=dlnb, dws=jnp.concatenate([dws01, dws23], axis=0),
        dbs=jnp.concatenate([dbs01, dbs23], axis=0)[:, :, 0], dgbn=dgbn, dgf=dgf,
        dw2=jnp.stack([dw2p[0][0:16], dw2p[1][16:32]]), dgb2=jnp.concatenate([dgb_f + dgbc_f, dgb_b + dgbc_b], axis=0),
        loss=loss[0, 0], dgate=dgate)

    send = exchange(big) if exchange is not None else ()
    early = _pack([small[k] for k in EARLY_KEYS[:-2]] + [jnp.broadcast_to(small["loss"], (128,)), small["dgate"]], EARLY_ROWS) \
        if exchange is not None else None
    tiles = L // PROJ_TM
    head = (3 * tiles) // 4
    (dx, dng, dscale, dshift), got, early_all = _proj_bwd(dp_g, dp, dlr, wit_g, wit_r, wlrt, x, dx1, ng, scale, send, early,
                                                          name="proj_bwd", span=(0, head))
    (dx, dng2, dscale2, dshift2), _, _ = _proj_bwd(dp_g, dp, dlr, wit_g, wit_r, wlrt, x, dx1, ng, scale,
                                                   name="proj_bwd_rest", span=(head, tiles - head), dx_buf=dx)
    dng, dscale, dshift = dng + dng2, dscale + dscale2, dshift + dshift2
    return dict(dx=dx, got=got, early=early, early_all=early_all, dshift=dshift, dscale=dscale, dng_lat=dng, **big, **small)


ANY = pl.BlockSpec(memory_space=pl.ANY)


def _coords():
    return lax.axis_index("x"), lax.axis_index("y"), lax.axis_index("c")


def _flip(v, bit):
    return 1 - v if bit else v


def _remote(src, dst, send_sem, recv_sem, dev):
    return pltpu.make_async_remote_copy(src_ref=src, dst_ref=dst, send_sem=send_sem, recv_sem=recv_sem,
                                        device_id=dev, device_id_type=MESH)


def _own(out, block, idx):
    return lax.dynamic_update_slice_in_dim(out, block[None], idx, axis=0)


def _half_idx(shape, axis, which, lead=()):
    idx = [pl.ds(0, d) for d in shape]
    h = shape[axis] // 2
    idx[axis] = pl.ds(which * h, h)
    return tuple(lead) + tuple(idx)


def _gather_weights(split, whole, name):
    ns, nw = len(split), len(whole)
    n = ns + nw
    arrs = [a for a, _ in split] + list(whole)

    def body(*refs):
        ins, outs = refs[:n], refs[n:2 * n]
        a_send, a_recv, b_send, b_recv = refs[2 * n:]
        x, y, c = _coords()
        me = 2 * x + y
        sib = (x, y, 1 - c)
        peers = [(1 - x, y), (x, 1 - y), (1 - x, 1 - y)]

        def half(k, slot, which):
            return outs[k].at[_half_idx(arrs[k].shape, split[k][1], which, lead=(slot,))]

        sends = []
        for k in range(n):
            for j, (px, py) in enumerate(peers):
                if k < ns:
                    rc = _remote(ins[k].at[_half_idx(arrs[k].shape, split[k][1], c)], half(k, me, c), a_send.at[3 * k + j],
                                 a_recv.at[3 * k + j], (px, py, c))
                else:
                    rc = _remote(ins[k], outs[k].at[me], a_send.at[3 * k + j], a_recv.at[3 * k + j], (px, py, c))
                rc.start()
                sends.append(rc)
        for k in range(ns):
            for j, (px, py) in enumerate(peers):
                landed = half(k, 2 * px + py, c)
                _remote(landed, landed, a_send.at[3 * k + j], a_recv.at[3 * k + j], (px, py, c)).wait_recv()
                fw = _remote(landed, landed, b_send.at[3 * k + j], b_recv.at[3 * k + j], sib)
                fw.start()
                sends.append(fw)
        for k in range(ns, n):
            for j, (px, py) in enumerate(peers):
                landed = outs[k].at[2 * px + py]
                _remote(landed, landed, a_send.at[3 * k + j], a_recv.at[3 * k + j], (px, py, c)).wait_recv()
        for k in range(ns):
            for j, (px, py) in enumerate(peers):
                passed = half(k, 2 * px + py, 1 - c)
                _remote(passed, passed, b_send.at[3 * k + j], b_recv.at[3 * k + j], sib).wait_recv()
        for rc in sends:
            rc.wait_send()

    outs = _pcall(
        body, name=name, in_specs=[ANY] * n, out_specs=[ANY] * n,
        out_shape=[SDS((4,) + a.shape, a.dtype) for a in arrs],
        scratch_shapes=[pltpu.SemaphoreType.DMA((3 * n,)), pltpu.SemaphoreType.DMA((3 * n,)), pltpu.SemaphoreType.DMA((3 * ns,)),
                        pltpu.SemaphoreType.DMA((3 * ns,))],
    )(*arrs)
    me_xy = 2 * lax.axis_index("x") + lax.axis_index("y")
    return [_own(o, a, me_xy) for o, a in zip(outs, arrs)]


def _gather_all(a, swap, name):
    masks = [(mx, my, mc) for mx in range(2) for my in range(2) for mc in range(2)][1:]
    n = len(swap)

    def body(*refs):
        in_ref, sw_in = refs[0], refs[1:1 + n]
        out_ref, sw_out = refs[1 + n], refs[2 + n:2 + 2 * n]
        send_sems, recv_sems = refs[2 + 2 * n:]
        x, y, c = _coords()
        me = 4 * x + 2 * y + c
        sends = []
        for j, (mx, my, mc) in enumerate(masks):
            rc = _remote(in_ref, out_ref.at[me], send_sems.at[j], recv_sems.at[j], (_flip(x, mx), _flip(y, my), _flip(c, mc)))
            rc.start()
            sends.append(rc)
        for k in range(n):
            rc = _remote(sw_in[k], sw_out[k], send_sems.at[7 + k], recv_sems.at[7 + k], (x, y, 1 - c))
            rc.start()
            sends.append(rc)
        for j, (mx, my, mc) in enumerate(masks):
            px, py, pc = _flip(x, mx), _flip(y, my), _flip(c, mc)
            landed = out_ref.at[4 * px + 2 * py + pc]
            _remote(landed, landed, send_sems.at[j], recv_sems.at[j], (px, py, pc)).wait_recv()
        for k in range(n):
            _remote(sw_out[k], sw_out[k], send_sems.at[7 + k], recv_sems.at[7 + k], (x, y, 1 - c)).wait_recv()
        for rc in sends:
            rc.wait_send()

    res = _pcall(
        body, name=name, in_specs=[ANY] * (1 + n), out_specs=[ANY] * (1 + n),
        out_shape=[SDS((8,) + a.shape, a.dtype)] + [SDS(s_.shape, s_.dtype) for s_ in swap],
        scratch_shapes=[pltpu.SemaphoreType.DMA((7 + n,)), pltpu.SemaphoreType.DMA((7 + n,))],
    )(a, *swap)
    return _own(res[0], a, 4 * lax.axis_index("x") + 2 * lax.axis_index("y") + lax.axis_index("c")), list(res[1:])


def _half_shape(shape, axis):
    return tuple(d // 2 if i == axis else d for i, d in enumerate(shape))


def _swap_half_c(arrs, axes, name):
    n = len(arrs)

    def body(*refs):
        ins, outs = refs[:n], refs[n:2 * n]
        send_sems, recv_sems = refs[2 * n:]
        x, y, c = _coords()
        sends = []
        for k in range(n):
            rc = _remote(ins[k].at[_half_idx(arrs[k].shape, axes[k], 1 - c)], outs[k], send_sems.at[k], recv_sems.at[k],
                         (x, y, 1 - c))
            rc.start()
            sends.append(rc)
        for rc in sends:
            rc.wait()

    return _pcall(
        body, name=name, in_specs=[ANY] * n, out_specs=[ANY] * n,
        out_shape=[SDS(_half_shape(a.shape, ax), a.dtype) for a, ax in zip(arrs, axes)],
        scratch_shapes=[pltpu.SemaphoreType.DMA((n,)), pltpu.SemaphoreType.DMA((n,))],
    )(*arrs)


def _pair_sum(a, got, cidx, axis, name):
    _, r, cdim = a.shape
    hshape = _half_shape(a.shape, axis)

    def body(c_ref, a_ref, g_ref, o_ref):
        o_ref[...] = (a_ref[...] + g_ref[...]).astype(BF16)

    if axis == 1:
        tr = min(r // 2, 256)
        nj = (r // 2) // tr
        blk = pl.BlockSpec((1, tr, cdim), lambda s, j, c: (s, j, 0))
        a_spec = pl.BlockSpec((1, tr, cdim), lambda s, j, c: (s, c[0] * nj + j, 0))
    else:
        nj, hw = 1, cdim // 2
        blk = pl.BlockSpec((1, r, hw), lambda s, j, c: (s, 0, 0))
        a_spec = pl.BlockSpec((1, r, hw), lambda s, j, c: (s, 0, c[0]))
    return _pcall(
        body, name=name, out_shape=SDS(hshape, BF16),
        grid_spec=pltpu.PrefetchScalarGridSpec(num_scalar_prefetch=1, grid=(4, nj), in_specs=[a_spec, blk], out_specs=blk),
        compiler_params=_cp(("parallel", "parallel"), VMEM_BIG),
    )(cidx, a, got)


def _sum_chips(parts, name):
    _, h, cdim = parts.shape

    def body(p_ref, o_ref):
        acc = p_ref[0].astype(F32)
        for k in range(1, 4):
            acc = acc + p_ref[k].astype(F32)
        o_ref[...] = acc

    if h % 256 == 0 or h in (128,):
        tr = min(h, 256)
        grid, in_spec, out_spec = (h // tr,), pl.BlockSpec((4, tr, cdim), lambda i: (0, i, 0)), pl.BlockSpec((tr, cdim), lambda i: (i, 0))
    else:
        lw = 256
        grid, in_spec, out_spec = (cdim // lw,), pl.BlockSpec((4, h, lw), lambda i: (0, 0, i)), pl.BlockSpec((h, lw), lambda i: (0, i))
    return _pcall(
        body, name=name, grid=grid, in_specs=[in_spec], out_specs=out_spec, out_shape=SDS((h, cdim), F32),
        compiler_params=_cp(("parallel",), VMEM_BIG),
    )(parts)


def _sum_slots(a, name, rows):
    s, n, _ = a.shape

    def body(a_ref, o_ref):
        acc = a_ref[0]
        for k in range(1, s):
            acc = acc + a_ref[k]
        o_ref[...] = acc

    return _pcall(
        body, name=name, grid=(n // rows,), in_specs=[pl.BlockSpec((s, rows, 128), lambda i: (0, i, 0))],
        out_specs=pl.BlockSpec((rows, 128), lambda i: (i, 0)), out_shape=SDS((n, 128), F32),
        compiler_params=_cp(("parallel",)),
    )(a)


def _adam_math(w, g, m, v):
    nm = ADAM_B1 * m + (1.0 - ADAM_B1) * g
    nv = ADAM_B2 * v + (1.0 - ADAM_B2) * (g * g)
    m_hat = nm / (1.0 - ADAM_B1 ** ADAM_STEP)
    v_hat = nv / (1.0 - ADAM_B2 ** ADAM_STEP)
    return -ADAM_LR * (m_hat / (jnp.sqrt(v_hat) + ADAM_EPS) + ADAM_WD * w), nm, nv


def _adamw(w, g, m, v, name, rows):
    r, cdim = w.shape

    def body(w_ref, g_ref, m_ref, v_ref, d_ref, nm_ref, nv_ref):
        d_ref[...], nm_ref[...], nv_ref[...] = _adam_math(w_ref[...], g_ref[...], m_ref[...], v_ref[...])

    blk = pl.BlockSpec((rows, cdim), lambda i: (i, 0))
    return _pcall(
        body, name=name, grid=(r // rows,), in_specs=[blk] * 4, out_specs=[blk] * 3,
        out_shape=[SDS(w.shape, F32)] * 3, compiler_params=_cp(("parallel",)),
    )(w, g, m, v)


def _adamw_joined(w, mine, other, m, v, cidx, axis, name, rows):
    r, cdim = w.shape
    if axis == 0:
        rows = r

    def body(c_ref, w_ref, a_ref, b_ref, m_ref, v_ref, g_ref, d_ref, nm_ref, nv_ref):
        a, b = a_ref[...], b_ref[...]
        g = jnp.where(c_ref[0] == 0, jnp.concatenate([a, b], axis=axis), jnp.concatenate([b, a], axis=axis))
        g_ref[...] = g
        d_ref[...], nm_ref[...], nv_ref[...] = _adam_math(w_ref[...], g, m_ref[...], v_ref[...])

    blk = pl.BlockSpec((rows, cdim), lambda i, c: (i, 0))
    hshape = (rows // 2, cdim) if axis == 0 else (rows, cdim // 2)
    hblk = pl.BlockSpec(hshape, lambda i, c: (i, 0))
    return _pcall(
        body, name=name, out_shape=[SDS(w.shape, F32)] * 4,
        grid_spec=pltpu.PrefetchScalarGridSpec(num_scalar_prefetch=1, grid=(r // rows,), in_specs=[blk, hblk, hblk, blk, blk],
                                               out_specs=[blk] * 4),
        compiler_params=_cp(("parallel",)),
    )(cidx, w, mine, other, m, v)


def _adamw_many(ws, gs, ms, vs, name):
    n = len(ws)

    def body(*refs):
        outs = refs[4 * n:]
        for k in range(n):
            d, nm, nv = _adam_math(refs[k][...], refs[n + k][...], refs[2 * n + k][...], refs[3 * n + k][...])
            outs[k][...] = d
            outs[n + k][...] = nm
            outs[2 * n + k][...] = nv

    res = _pcall(body, name=name, out_shape=[SDS(w.shape, F32) for w in ws] * 3)(*ws, *gs, *ms, *vs)
    return res[:n], res[n:2 * n], res[2 * n:]


def _pack(pieces, rows):
    flat = jnp.concatenate([p.reshape(-1) for p in pieces])
    return jnp.pad(flat, (0, rows * 128 - flat.shape[0])).reshape(rows, 128)


def _unpack(buf, shapes):
    flat = buf.reshape(-1)
    out, off = [], 0
    for shp in shapes:
        size = 1
        for s in shp:
            size *= s
        out.append(flat[off:off + size].reshape(shp))
        off += size
    return out


LATE_ROWS = 32


def kernel(x, c, ctx, c_ctx, w_mod, b_mod, norm_g, w_in, a_ln_g, a_ln_b, a_ws, a_bs, b_gate_w2, b_gate_b, b_norm_g, w_proj_a, w_proj_b, w_out, final_norm_g, loss_target, m_c_ctx, m_w_mod, m_b_mod, m_norm_g, m_w_in, m_a_ln_g, m_a_ln_b, m_a_ws, m_a_bs, m_b_gate_w2, m_b_gate_b, m_b_norm_g, m_w_proj_a, m_w_proj_b, m_w_out, m_final_norm_g, v_c_ctx, v_w_mod, v_b_mod, v_norm_g, v_w_in, v_a_ln_g, v_a_ln_b, v_a_ws, v_a_bs, v_b_gate_w2, v_b_gate_b, v_b_norm_g, v_w_proj_a, v_w_proj_b, v_w_out, v_final_norm_g):
    xi, yi, ci = _coords()
    me_xy = 2 * xi + yi

    gate_pack = _pack([b_gate_w2[0], b_gate_b[0]], 24)
    w_in_t, m_w_in_t, v_w_in_t = (jnp.swapaxes(a[0], 0, 1) for a in (w_in, m_w_in, v_w_in))
    g_wit, g_wm, g_gate = _gather_weights([(w_in_t.astype(BF16), 1), (w_mod[0].astype(BF16), 0)], [gate_pack], "gather_weights")
    late_shards = (w_proj_a[0].astype(BF16), w_proj_b[0].astype(BF16), w_out[0].astype(BF16))
    wit_u = g_wit.reshape(4 * 1288, D)
    wit_g = wit_u[3104:5152]
    wit_r = jnp.concatenate([wit_u[1056:1568], wit_u[1568:2080], wit_u[2592:3104], wit_u[2080:2592], wit_u[0:1024]], axis=0)
    wlrt = jnp.pad(wit_u[1024:1056], ((0, LRW - 32), (0, 0)))
    wm = jnp.swapaxes(g_wm, 0, 1).reshape(D, 3 * D)
    gflat = g_gate.reshape(4, 24 * 128)
    w2 = jnp.swapaxes(gflat[:, 0:2048].reshape(4, 2, 16, 64), 0, 2)
    w2 = jnp.swapaxes(w2, 0, 1).reshape(2, 16, 256)
    gb2 = jnp.swapaxes(gflat[:, 2048:2176].reshape(4, 2, 64), 0, 1).reshape(2, 256)

    tags = ["wi", "wpa", "wpb", "wo"]
    half_axes = [2, 1, 1, 1]
    sent = []

    def exchange(g):
        dwr = g["dwit_r"]
        dwit_u = jnp.concatenate([g["dwit_qkv"], g["dwlrt"][0:32], dwr[0:512], dwr[512:1024], dwr[1536:2048], dwr[1024:1536],
                                  g["dwit_g"]], axis=0)
        big = [dwit_u.reshape(4, 1288, D), jnp.swapaxes(g["dwpa"].reshape(512, 4, 256), 0, 1),
               jnp.swapaxes(g["dwpb"].reshape(512, 4, 256), 0, 1), g["dwo"].reshape(4, 256, D)]
        other = _swap_half_c(big, half_axes, "swap_half_in")
        cidx = jnp.reshape(ci, (1,)).astype(jnp.int32)
        sent.extend(_pair_sum(a, o, cidx, ax, "sum_pair_" + t) for a, o, ax, t in zip(big, other, half_axes, tags))
        return sent

    r = _device_step(x[0], c, ctx[0], c_ctx[None], loss_target[0], wm, b_mod, norm_g, wit_g, wit_r, wlrt, a_ln_g, a_ln_b,
                     a_ws[0], a_bs[0], w2, gb2, b_norm_g, None, None, None, final_norm_g[None], exchange, late_shards)

    parts = [_own(g, lax.dynamic_index_in_dim(s_, me_xy, axis=0, keepdims=False), me_xy) for g, s_ in zip(r["got"], sent)]
    halves = [_sum_chips(p_, "sum_chips_" + t) for p_, t in zip(parts, tags)]

    me8 = 4 * xi + 2 * yi + ci
    early_all = _own(r["early_all"], r["early"], me8)
    late = _pack([r["dshift"], r["dscale"], r["dng_lat"], c], LATE_ROWS)
    late_all, others = _gather_all(late, halves, "gather_small")
    s_early = _sum_slots(early_all, "sum_early", EARLY_ROWS // 3)
    s_late = _sum_slots(late_all, "sum_late", LATE_ROWS)
    (s_dmodc, s_dscc, s_dng_c, s_dlng, s_dlnb, s_dws, s_dbs, s_dgbn, s_dgf, s_dw2, s_dgb2, s_loss, s_dgate) = _unpack(
        s_early, EARLY_SHAPES)
    s_dshift, s_dscale, s_dng_lat, _ = _unpack(s_late, [(1, D)] * 4)
    s_dng = s_dng_lat + s_dng_c
    s_dmod = jnp.concatenate([s_dshift, s_dscale, s_dgate], axis=1)
    loss = s_loss[0]
    s_dmodc_p = jnp.pad(s_dmodc, ((0, 0), (0, D)))
    g_b_mod = s_dmod + s_dmodc_p
    sg = jax.nn.sigmoid(c_ctx)
    g_c_ctx = s_dscc * (sg * (1.0 + c_ctx * (1.0 - sg)))
    g_w2 = lax.dynamic_slice_in_dim(s_dw2, 64 * me_xy, 64, axis=2)[None]
    g_gb2 = lax.dynamic_slice_in_dim(s_dgb2, 64 * me_xy, 64, axis=1)[None]

    flat_l = late_all.reshape(8, LATE_ROWS * 128)
    dgate_all = early_all.reshape(8, EARLY_ROWS * 128)[:, EARLY_SIZE - D:EARLY_SIZE]
    dmod_all = jnp.concatenate([flat_l[:, 0:2 * D], dgate_all], axis=1)
    c_all = flat_l[:, 3 * D:4 * D]
    lhs = jnp.concatenate([_silu(c_all), _silu(c_ctx)[None], jnp.zeros((7, D), F32)], axis=0)
    rhs = jnp.concatenate([dmod_all, s_dmodc_p, jnp.zeros((7, 3 * D), F32)], axis=0)
    rhs = lax.dynamic_slice_in_dim(rhs, 768 * me_xy, 768, axis=1)
    g_w_mod = _mm(lhs.T.astype(BF16), rhs.astype(BF16), tm=D, tn=768, tk=16, out_dtype=F32, name="mm_dwm")

    cidx = jnp.reshape(ci, (1,)).astype(jnp.int32)
    g_w_in_t, d_w_in_t, nm_w_in_t, nv_w_in_t = _adamw_joined(w_in_t, halves[0], others[0], m_w_in_t, v_w_in_t, cidx, 1,
                                                             "adamw_w_in", 184)
    g_w_in, d_w_in, nm_w_in, nv_w_in = (jnp.swapaxes(a, 0, 1) for a in (g_w_in_t, d_w_in_t, nm_w_in_t, nv_w_in_t))
    g_wpa, d_wpa, nm_wpa, nv_wpa = _adamw_joined(w_proj_a[0], halves[1], others[1], m_w_proj_a[0], v_w_proj_a[0], cidx, 0,
                                                 "adamw_wpa", 0)
    g_wpb, d_wpb, nm_wpb, nv_wpb = _adamw_joined(w_proj_b[0], halves[2], others[2], m_w_proj_b[0], v_w_proj_b[0], cidx, 0,
                                                 "adamw_wpb", 0)
    g_wo, d_wo, nm_wo, nv_wo = _adamw_joined(w_out[0], halves[3], others[3], m_w_out[0], v_w_out[0], cidx, 0, "adamw_wo", 0)
    d_w_mod, nm_w_mod, nv_w_mod = _adamw(w_mod[0], g_w_mod, m_w_mod[0], v_w_mod[0], "adamw_w_mod", 256)

    names = ["c_ctx", "b_mod", "norm_g", "a_ln_g", "a_ln_b", "a_ws", "a_bs", "b_gate_w2", "b_gate_b", "b_norm_g", "final_norm_g"]
    ws_ = [c_ctx, b_mod, norm_g, a_ln_g, a_ln_b, a_ws, a_bs, b_gate_w2, b_gate_b, b_norm_g, final_norm_g]
    gs_ = [g_c_ctx, g_b_mod, s_dng, s_dlng, s_dlnb, s_dws, s_dbs, g_w2, g_gb2, s_dgbn, s_dgf]
    ms_ = [m_c_ctx, m_b_mod, m_norm_g, m_a_ln_g, m_a_ln_b, m_a_ws, m_a_bs, m_b_gate_w2, m_b_gate_b, m_b_norm_g, m_final_norm_g]
    vs_ = [v_c_ctx, v_b_mod, v_norm_g, v_a_ln_g, v_a_ln_b, v_a_ws, v_a_bs, v_b_gate_w2, v_b_gate_b, v_b_norm_g, v_final_norm_g]
    shapes = [w.shape for w in ws_]
    flat2 = [(1, 1024), (1, 3072), (1, 1024), (1, 512), (1, 512), (512, 128), (4, 128), (32, 64), (2, 64), (1, 512), (1, 1024)]
    as2d = lambda arrs: [a.reshape(s) for a, s in zip(arrs, flat2)]
    d_s, nm_s, nv_s = _adamw_many(as2d(ws_), as2d(gs_), as2d(ms_), as2d(vs_), "adamw_small")
    d_small = {n: a.reshape(s) for n, a, s in zip(names, d_s, shapes)}
    nm_small = {n: a.reshape(s) for n, a, s in zip(names, nm_s, shapes)}
    nv_small = {n: a.reshape(s) for n, a, s in zip(names, nv_s, shapes)}
    g_small = {n: g.reshape(s) for n, g, s in zip(names, gs_, shapes)}

    order = ["c_ctx", "w_mod", "b_mod", "norm_g", "w_in", "a_ln_g", "a_ln_b", "a_ws", "a_bs", "b_gate_w2", "b_gate_b", "b_norm_g",
             "w_proj_a", "w_proj_b", "w_out", "final_norm_g"]
    big_g = dict(w_mod=g_w_mod[None], w_in=g_w_in[None], w_proj_a=g_wpa[None], w_proj_b=g_wpb[None], w_out=g_wo[None])
    big_d = dict(w_mod=d_w_mod[None], w_in=d_w_in[None], w_proj_a=d_wpa[None], w_proj_b=d_wpb[None], w_out=d_wo[None])
    big_m = dict(w_mod=nm_w_mod[None], w_in=nm_w_in[None], w_proj_a=nm_wpa[None], w_proj_b=nm_wpb[None], w_out=nm_wo[None])
    big_v = dict(w_mod=nv_w_mod[None], w_in=nv_w_in[None], w_proj_a=nv_wpa[None], w_proj_b=nv_wpb[None], w_out=nv_wo[None])
    grads = [big_g[n] if n in big_g else g_small[n] for n in order]
    deltas = [big_d[n] if n in big_d else d_small[n] for n in order]
    new_m = [big_m[n] if n in big_m else nm_small[n] for n in order]
    new_v = [big_v[n] if n in big_v else nv_small[n] for n in order]
    return (loss, r["dx"][None], *grads, *deltas, *new_m, *new_v)
```

```python
import jax
import jax.numpy as jnp
from jax import lax
from jax.experimental import pallas as pl
from jax.experimental.pallas import tpu as pltpu

F32 = jnp.float32
BF16 = jnp.bfloat16
SDS = jax.ShapeDtypeStruct

D = 1024
NP = 5120
LRW = 128
CH = 64
AC = 128
EPS = 1e-6
TOK = 512
GLA_TB = 1024
VMEM_BIG = 48 * 1024 * 1024

ADAM_LR, ADAM_B1, ADAM_B2, ADAM_EPS, ADAM_WD, ADAM_STEP = 0.001, 0.9, 0.999, 1e-08, 0.01, 10

_pcall = pl.pallas_call
MESH = pl.DeviceIdType.MESH


def _cp(sem=None, vmem=None):
    kw = {}
    if sem is not None:
        kw["dimension_semantics"] = sem
    if vmem is not None:
        kw["vmem_limit_bytes"] = vmem
    return pltpu.CompilerParams(**kw)


def _silu(x):
    return x * jax.nn.sigmoid(x)


def _silu_and_grad(x):
    s = jax.nn.sigmoid(x)
    return x * s, s * (1.0 + x * (1.0 - s))


def _logsig(x):
    return jnp.minimum(x, 0.0) - jnp.log1p(jnp.exp(-jnp.abs(x)))


def _nt(a, b):
    return lax.dot_general(a, b, (((1,), (1,)), ((), ())), preferred_element_type=F32)


def _tn(a, b):
    return lax.dot_general(a, b, (((0,), (0,)), ((), ())), preferred_element_type=F32)


def _nn(a, b):
    return jnp.dot(a, b, preferred_element_type=F32)


def _full(shape):
    return pl.BlockSpec(shape, lambda *_: (0,) * len(shape))


def _mm(a, b, *, tm, tn, tk, out_dtype, name, acc=None, n_outer=False, b_t=False):
    m, k = a.shape
    n, k2 = (b.shape if b_t else b.shape[::-1])
    assert k == k2 and m % tm == 0 and n % tn == 0 and k % tk == 0, (a.shape, b.shape, tm, tn, tk)
    nk = k // tk
    has_acc = acc is not None

    def body(*refs):
        if has_acc:
            a_ref, b_ref, c_ref, o_ref = refs[:4]
        else:
            a_ref, b_ref, o_ref = refs[:3]
        part = (_nt if b_t else _nn)(a_ref[...].astype(BF16), b_ref[...].astype(BF16))
        if nk == 1:
            o_ref[...] = ((c_ref[...] + part) if has_acc else part).astype(out_dtype)
            return
        acc_ref = refs[-1]
        kk = pl.program_id(2)

        @pl.when(kk == 0)
        def _():
            if has_acc:
                acc_ref[...] = c_ref[...] + part
            else:
                acc_ref[...] = part

        @pl.when(kk > 0)
        def _():
            acc_ref[...] += part

        @pl.when(kk == nk - 1)
        def _():
            o_ref[...] = acc_ref[...].astype(out_dtype)

    if n_outer:
        ij = lambda g0, g1: (g1, g0)
        grid = (n // tn, m // tm, nk)
    else:
        ij = lambda g0, g1: (g0, g1)
        grid = (m // tm, n // tn, nk)
    b_spec = (pl.BlockSpec((tn, tk), lambda g0, g1, kk: (ij(g0, g1)[1], kk)) if b_t
              else pl.BlockSpec((tk, tn), lambda g0, g1, kk: (kk, ij(g0, g1)[1])))
    in_specs = [pl.BlockSpec((tm, tk), lambda g0, g1, kk: (ij(g0, g1)[0], kk)), b_spec]
    args = [a, b]
    if has_acc:
        in_specs.append(pl.BlockSpec((tm, tn), lambda g0, g1, kk: ij(g0, g1)))
        args.append(acc)
    return _pcall(
        body, name=name, grid=grid, in_specs=in_specs,
        out_specs=pl.BlockSpec((tm, tn), lambda g0, g1, kk: ij(g0, g1)),
        out_shape=SDS((m, n), out_dtype), scratch_shapes=([pltpu.VMEM((tm, tn), F32)] if nk > 1 else []),
        compiler_params=_cp(("parallel", "parallel", "arbitrary"), VMEM_BIG),
    )(*args)


def _mm_tn(a, b, *, ta, tn, tk, name, acc=None):
    m, ka = a.shape
    m2, n = b.shape
    assert m == m2 and ka % ta == 0 and n % tn == 0 and m % tk == 0, (a.shape, b.shape, ta, tn, tk)
    nk = m // tk
    has_acc = acc is not None

    def body(*refs):
        if has_acc:
            a_ref, b_ref, c_ref, o_ref = refs
        else:
            a_ref, b_ref, o_ref = refs
        kk = pl.program_id(2)
        part = _tn(a_ref[...].astype(BF16), b_ref[...].astype(BF16))

        @pl.when(kk == 0)
        def _():
            if has_acc:
                o_ref[...] = c_ref[...] + part
            else:
                o_ref[...] = part

        @pl.when(kk > 0)
        def _():
            o_ref[...] += part

    in_specs = [pl.BlockSpec((tk, ta), lambda i, j, kk: (kk, i)), pl.BlockSpec((tk, tn), lambda i, j, kk: (kk, j))]
    args = [a, b]
    if has_acc:
        in_specs.append(pl.BlockSpec((ta, tn), lambda i, j, kk: (i, j)))
        args.append(acc)
    return _pcall(
        body, name=name, grid=(ka // ta, n // tn, nk), in_specs=in_specs,
        out_specs=pl.BlockSpec((ta, tn), lambda i, j, kk: (i, j)), out_shape=SDS((ka, n), F32),
        compiler_params=_cp(("parallel", "parallel", "arbitrary"), VMEM_BIG),
    )(*args)


def _modvec(cc, wm, bm):
    def body(c_ref, w_ref, b_ref, o_ref):
        o_ref[...] = _nn(_silu(c_ref[...]).astype(BF16), w_ref[...]) + b_ref[...]

    return _pcall(body, name="modvec", out_shape=SDS((8, 3 * D), F32), compiler_params=_cp(None, VMEM_BIG))(cc, wm, bm)


def _dcctx(dmodc, wm):
    def body(d_ref, w_ref, o_ref):
        o_ref[...] = _nt(d_ref[...].astype(BF16), w_ref[...])

    return _pcall(
        body, name="dcctx", grid=(1,), in_specs=[_full((8, 2 * D)), pl.BlockSpec((D, 2 * D), lambda i: (0, 0))],
        out_specs=_full((8, D)), out_shape=SDS((8, D), F32), compiler_params=_cp(("arbitrary",), VMEM_BIG),
    )(dmodc, wm)


def _prep_h(x, ng, scale, shift, name):
    m = x.shape[0]

    def body(x_ref, g_ref, sc_ref, sh_ref, h_ref):
        xf = x_ref[...]
        r = lax.rsqrt(jnp.mean(xf * xf, axis=-1, keepdims=True) + EPS)
        y = (xf * r) * g_ref[...]
        h_ref[...] = (y * (1.0 + sc_ref[...]) + sh_ref[...]).astype(BF16)

    tok = min(TOK, m)
    row = pl.BlockSpec((tok, D), lambda i: (i, 0))
    return _pcall(
        body, name=name, grid=(m // tok,), in_specs=[row, _full((1, D)), _full((1, D)), _full((1, D))],
        out_specs=row, out_shape=SDS((m, D), BF16), compiler_params=_cp(("parallel",)),
    )(x, ng, scale, shift)


def _resident(shape):
    return pl.BlockSpec(shape, lambda *_: (0,) * len(shape), pipeline_mode=pl.Buffered(1))


PROJ_TM = 512


def _proj_fwd(x, ng, scale, shift, wit_g, wit_r, wlrt, ln_g, ln_b, share=()):
    m = x.shape[0]
    ns = len(share)
    steps = m // PROJ_TM
    src = [(0, 0), (0, D), (1, 2 * D), (1, 0), (1, D)]

    def body(*refs):
        x_ref, g_ref, sc_ref, sh_ref, wg_ref, wr_ref, wl_ref, lg_ref, lb_ref = refs[:9]
        share_refs = refs[9:9 + ns]
        h_ref, p_ref, plr_ref, vr_ref, vc_ref = refs[9 + ns:14 + ns]
        got_refs = refs[14 + ns:14 + 2 * ns]
        sems = refs[14 + 2 * ns:]

        def copies():
            cx, cy, cc = _coords()
            me = 2 * cx + cy
            peers = [(1 - cx, cy), (cx, 1 - cy), (1 - cx, 1 - cy)]
            out, back = [], []
            for k in range(ns):
                for j, (px, py) in enumerate(peers):
                    out.append(_remote(share_refs[k], got_refs[k].at[me], sems[0].at[3 * k + j], sems[1].at[3 * k + j], (px, py, cc)))
                    landed = got_refs[k].at[2 * px + py]
                    back.append(_remote(landed, landed, sems[0].at[3 * k + j], sems[1].at[3 * k + j], (px, py, cc)))
            return out, back

        if ns:
            @pl.when(pl.program_id(0) == 0)
            def _():
                for rc in copies()[0]:
                    rc.start()

            @pl.when(pl.program_id(0) == steps - 1)
            def _():
                out, back = copies()
                for rc in back:
                    rc.wait_recv()
                for rc in out:
                    rc.wait_send()

        xf = x_ref[...]
        r = lax.rsqrt(jnp.mean(xf * xf, axis=-1, keepdims=True) + EPS)
        y = (xf * r) * g_ref[...]
        h = (y * (1.0 + sc_ref[...]) + sh_ref[...]).astype(BF16)
        h_ref[...] = h
        for j, (which, r0) in enumerate(src):
            w_ref = wr_ref if which else wg_ref
            blk = _nt(h, w_ref[r0:r0 + D, :]).astype(BF16)
            p_ref[:, D * j:D * j + D] = blk
            if j == 4:
                xf = blk[:, 512:1024].astype(F32)
                xc = xf - jnp.mean(xf, axis=-1, keepdims=True)
                vn = (xc * lax.rsqrt(jnp.mean(xc * xc, axis=-1, keepdims=True) + EPS)) * lg_ref[...] + lb_ref[...]
                vr_ref[...] = vn[:, 0:256].astype(BF16)
                vc_ref[0] = vn[:, 256:384].astype(BF16)
                vc_ref[1] = vn[:, 384:512].astype(BF16)
        plr_ref[...] = _nt(h, wl_ref[...])

    row = pl.BlockSpec((PROJ_TM, D), lambda i: (i, 0))
    vec = _full((1, D))
    res = _pcall(
        body, name="proj_fwd", grid=(steps,),
        in_specs=[row, vec, vec, vec, _resident((2 * D, D)), _resident((3 * D, D)), _resident((LRW, D)), _full((1, 512)),
                  _full((1, 512))] + [ANY] * ns,
        out_specs=[row, pl.BlockSpec((PROJ_TM, NP), lambda i: (i, 0)), pl.BlockSpec((PROJ_TM, LRW), lambda i: (i, 0)),
                   pl.BlockSpec((PROJ_TM, 256), lambda i: (i, 0)), pl.BlockSpec((2, PROJ_TM, 128), lambda i: (0, i, 0))] + [ANY] * ns,
        out_shape=[SDS((m, D), BF16), SDS((m, NP), BF16), SDS((m, LRW), F32), SDS((m, 256), BF16), SDS((2, m, 128), BF16)]
        + [SDS((4,) + a.shape, a.dtype) for a in share],
        scratch_shapes=([pltpu.SemaphoreType.DMA((3 * ns,)), pltpu.SemaphoreType.DMA((3 * ns,))] if ns else []),
        compiler_params=_cp(("arbitrary",), VMEM_BIG),
    )(x, ng, scale, shift, wit_g, wit_r, wlrt, ln_g, ln_b, *share)
    return res[0], res[1], res[2], res[3], res[4], list(res[5:])


def _proj_bwd(dp_g, dp_r, dlr, wit_g, wit_r, wlrt, x, dx1, ng, scale, send=(), share8=None):
    m = x.shape[0]
    ns = len(send)
    n8 = 0 if share8 is None else 1
    steps = m // PROJ_TM
    masks = [(mx, my, mc) for mx in range(2) for my in range(2) for mc in range(2)][1:]

    def body(*refs):
        (dpg_ref, dpr_ref, dlr_ref, wg_ref, wr_ref, wl_ref, x_ref, r_ref, g_ref, sc_ref) = refs[:10]
        send_refs = refs[10:10 + ns]
        n_in = 10 + ns + n8
        dx_ref, dg_ref, dsc_ref, dsh_ref = refs[n_in:n_in + 4]
        got_refs = refs[n_in + 4:n_in + 4 + ns]
        sems = refs[n_in + 4 + ns + n8:]
        i = pl.program_id(0)

        def copies():
            cx, cy, cc = _coords()
            me = 2 * cx + cy
            peers = [(1 - cx, cy), (cx, 1 - cy), (1 - cx, 1 - cy)]
            out, back = [], []
            for k in range(ns):
                for j, (px, py) in enumerate(peers):
                    out.append(_remote(send_refs[k].at[2 * px + py], got_refs[k].at[me], sems[0].at[3 * k + j],
                                       sems[1].at[3 * k + j], (px, py, cc)))
                    landed = got_refs[k].at[2 * px + py]
                    back.append(_remote(landed, landed, sems[0].at[3 * k + j], sems[1].at[3 * k + j], (px, py, cc)))
            if n8:
                src8, all8 = refs[10 + ns], refs[n_in + 4 + ns]
                s8, r8 = sems[-2], sems[-1]
                for j, (mx, my, mc) in enumerate(masks):
                    px, py, pc = _flip(cx, mx), _flip(cy, my), _flip(cc, mc)
                    out.append(_remote(src8, all8.at[4 * cx + 2 * cy + cc], s8.at[j], r8.at[j], (px, py, pc)))
                    landed = all8.at[4 * px + 2 * py + pc]
                    back.append(_remote(landed, landed, s8.at[j], r8.at[j], (px, py, pc)))
            return out, back

        @pl.when(i == 0)
        def _():
            dg_ref[...] = jnp.zeros_like(dg_ref)
            dsc_ref[...] = jnp.zeros_like(dsc_ref)
            dsh_ref[...] = jnp.zeros_like(dsh_ref)
            if ns or n8:
                for rc in copies()[0]:
                    rc.start()

        dh_ = (_nn(dpg_ref[...], wg_ref[...]) + _nn(dpr_ref[...], wr_ref[...])
               + _nn(dlr_ref[...].astype(BF16), wl_ref[...]))
        xf = x_ref[...]
        r = lax.rsqrt(jnp.mean(xf * xf, axis=-1, keepdims=True) + EPS)
        xh = xf * r
        y = xh * g_ref[...]
        dsh_ref[...] += jnp.sum(dh_, axis=0, keepdims=True)
        dsc_ref[...] += jnp.sum(dh_ * y, axis=0, keepdims=True)
        dy = dh_ * (1.0 + sc_ref[...])
        dg_ref[...] += jnp.sum(dy * xh, axis=0, keepdims=True)
        dxh = dy * g_ref[...]
        dx_ref[...] = r * (dxh - xh * jnp.mean(dxh * xh, axis=-1, keepdims=True)) + r_ref[...]

        if ns or n8:
            @pl.when(i == steps - 1)
            def _():
                out, back = copies()
                for rc in back:
                    rc.wait_recv()
                for rc in out:
                    rc.wait_send()

    row = pl.BlockSpec((PROJ_TM, D), lambda i: (i, 0))
    vec = _full((1, D))
    kg, kr = dp_g.shape[1], dp_r.shape[1]
    extra_in = list(send) + ([share8] if n8 else [])
    extra_out = [SDS(a.shape, a.dtype) for a in send] + ([SDS((8,) + share8.shape, share8.dtype)] if n8 else [])
    res = _pcall(
        body, name="proj_bwd", grid=(steps,),
        in_specs=[pl.BlockSpec((PROJ_TM, kg), lambda i: (i, 0)), pl.BlockSpec((PROJ_TM, kr), lambda i: (i, 0)),
                  pl.BlockSpec((PROJ_TM, LRW), lambda i: (i, 0)), _resident((kg, D)), _resident((kr, D)), _resident((LRW, D)),
                  row, row, vec, vec] + [ANY] * len(extra_in),
        out_specs=[row, vec, vec, vec] + [ANY] * len(extra_out),
        out_shape=[SDS((m, D), F32), SDS((1, D), F32), SDS((1, D), F32), SDS((1, D), F32)] + extra_out,
        scratch_shapes=(([pltpu.SemaphoreType.DMA((3 * ns,)), pltpu.SemaphoreType.DMA((3 * ns,))] if ns else [])
                        + ([pltpu.SemaphoreType.DMA((7,)), pltpu.SemaphoreType.DMA((7,))] if n8 else [])),
        compiler_params=_cp(("arbitrary",), VMEM_BIG),
    )(dp_g, dp_r, dlr, wit_g, wit_r, wlrt, x, dx1, ng, scale, *extra_in)
    return tuple(res[:4]), list(res[4:4 + ns]), (res[4 + ns] if n8 else None)


def _prep_bwd(x, dh, dx1, ng, scale, name):
    m = x.shape[0]
    has_res = dx1 is not None

    def body(*refs):
        if has_res:
            x_ref, dh_ref, r_ref, g_ref, sc_ref, dx_ref, dg_ref, dsc_ref, dsh_ref = refs
        else:
            x_ref, dh_ref, g_ref, sc_ref, dx_ref, dg_ref, dsc_ref, dsh_ref = refs
        i = pl.program_id(0)

        @pl.when(i == 0)
        def _():
            dg_ref[...] = jnp.zeros_like(dg_ref)
            dsc_ref[...] = jnp.zeros_like(dsc_ref)
            dsh_ref[...] = jnp.zeros_like(dsh_ref)

        xf = x_ref[...]
        dh_ = dh_ref[...]
        r = lax.rsqrt(jnp.mean(xf * xf, axis=-1, keepdims=True) + EPS)
        xh = xf * r
        y = xh * g_ref[...]
        dsh_ref[...] += jnp.sum(dh_, axis=0, keepdims=True)
        dsc_ref[...] += jnp.sum(dh_ * y, axis=0, keepdims=True)
        dy = dh_ * (1.0 + sc_ref[...])
        dg_ref[...] += jnp.sum(dy * xh, axis=0, keepdims=True)
        dxh = dy * g_ref[...]
        dx = r * (dxh - xh * jnp.mean(dxh * xh, axis=-1, keepdims=True))
        if has_res:
            dx = dx + r_ref[...]
        dx_ref[...] = dx

    tok = min(TOK, m)
    row = pl.BlockSpec((tok, D), lambda i: (i, 0))
    vec = _full((1, D))
    in_specs = [row, row] + ([row] if has_res else []) + [vec, vec]
    args = [x, dh] + ([dx1] if has_res else []) + [ng, scale]
    return _pcall(
        body, name=name, grid=(m // tok,), in_specs=in_specs, out_specs=[row, vec, vec, vec],
        out_shape=[SDS((m, D), F32), SDS((1, D), F32), SDS((1, D), F32), SDS((1, D), F32)],
        compiler_params=_cp(("arbitrary",)),
    )(*args)


COLB = 2048


def _colmix_fwd(vnc, ws23, bs23):
    rows = vnc.shape[2] // COLB

    def body(v_ref, w_ref, b_ref, o_ref):
        o_ref[0] = _nn(w_ref[0], v_ref[0]) + b_ref[0]

    return _pcall(
        body, name="colmix_fwd", grid=(2, rows),
        in_specs=[pl.BlockSpec((1, AC, COLB), lambda g, j: (g, 0, j)), pl.BlockSpec((1, AC, AC), lambda g, j: (g, 0, 0)),
                  pl.BlockSpec((1, AC, 1), lambda g, j: (g, 0, 0))],
        out_specs=pl.BlockSpec((1, AC, COLB), lambda g, j: (g, 0, j)),
        out_shape=SDS(vnc.shape, F32), compiler_params=_cp(("parallel", "parallel")),
    )(vnc, ws23, bs23)


def _colmix_bwd(dsvc, vnc, ws23t):
    rows = vnc.shape[2] // COLB

    def body(d_ref, v_ref, wt_ref, dv_ref, dw_ref, db_ref):
        j = pl.program_id(1)

        @pl.when(j == 0)
        def _():
            dw_ref[...] = jnp.zeros_like(dw_ref)
            db_ref[...] = jnp.zeros_like(db_ref)

        d = d_ref[0]
        d16 = d.astype(BF16)
        dv_ref[0] = _nn(wt_ref[0], d16)
        dw_ref[0] += _nt(d16, v_ref[0])
        db_ref[0] += jnp.sum(d, axis=1, keepdims=True)

    blk = pl.BlockSpec((1, AC, COLB), lambda g, j: (g, 0, j))
    return _pcall(
        body, name="colmix_bwd", grid=(2, rows),
        in_specs=[blk, blk, pl.BlockSpec((1, AC, AC), lambda g, j: (g, 0, 0))],
        out_specs=[blk, pl.BlockSpec((1, AC, AC), lambda g, j: (g, 0, 0)), pl.BlockSpec((1, AC, 1), lambda g, j: (g, 0, 0))],
        out_shape=[SDS(vnc.shape, F32), SDS((2, AC, AC), F32), SDS((2, AC, 1), F32)],
        compiler_params=_cp(("parallel", "arbitrary")),
    )(dsvc, vnc, ws23t)


def _head_norm(o, gbn):
    out = []
    for h in range(4):
        oh = o[:, 128 * h:128 * h + 128]
        r = lax.rsqrt(jnp.mean(oh * oh, axis=-1, keepdims=True) + EPS)
        out.append((r, oh * r))
    return out


def _tail_fwd(o_f, o_b, p, vnr, svc, x, tgt, ws01, bs01, gbn, wpa, wpb, wo, gate, gf):
    m = p.shape[0]

    def body(of_ref, ob_ref, zb_ref, ua_ref, za_ref, ga_ref, gb_ref, vnr_ref, svc_ref, x_ref, t_ref, w_ref, b_ref, g_ref,
             wpa_ref, wpb_ref, wo_ref, gate_ref, gf_ref,
             ya_ref, yb_ref, svr_ref, dwo_ref, dx1_ref, dout_ref, loss_ref, dgate_ref, dgf_ref):
        i = pl.program_id(0)

        @pl.when(i == 0)
        def _():
            loss_ref[...] = jnp.zeros_like(loss_ref)
            dgate_ref[...] = jnp.zeros_like(dgate_ref)
            dgf_ref[...] = jnp.zeros_like(dgf_ref)
            dwo_ref[...] = jnp.zeros_like(dwo_ref)

        o = of_ref[...].astype(F32) + ob_ref[...].astype(F32)
        zb = zb_ref[...].astype(F32)
        for h, (r, xh) in enumerate(_head_norm(o, None)):
            sl = slice(128 * h, 128 * h + 128)
            yb_ref[:, sl] = ((xh * g_ref[:, sl]) * _silu(zb[:, sl])).astype(BF16)
        for j in range(TOK // AC):
            for g in range(2):
                sv = _nn(w_ref[g], vnr_ref[AC * j:AC * j + AC, AC * g:AC * g + AC]) + b_ref[g]
                svr_ref[AC * j:AC * j + AC, AC * g:AC * g + AC] = sv
        sz = _silu(za_ref[...].astype(F32))
        u = ua_ref[...].astype(F32)
        ya_ref[:, 0:256] = ((u[:, 0:256] * svr_ref[...]) * sz[:, 0:256]).astype(BF16)
        ya_ref[:, 256:384] = ((u[:, 256:384] * svc_ref[0]) * sz[:, 256:384]).astype(BF16)
        ya_ref[:, 384:512] = ((u[:, 384:512] * svc_ref[1]) * sz[:, 384:512]).astype(BF16)
        ya = _nn(ya_ref[...], wpa_ref[...])
        yb = _nn(yb_ref[...], wpb_ref[...])
        mg = (jax.nn.sigmoid(ga_ref[...].astype(F32)) * ya + jax.nn.sigmoid(gb_ref[...].astype(F32)) * yb).astype(BF16)
        out_ = _nn(mg, wo_ref[...])
        x1 = x_ref[...] + gate_ref[...] * out_
        r = lax.rsqrt(jnp.mean(x1 * x1, axis=-1, keepdims=True) + EPS)
        xh = x1 * r
        err = xh * gf_ref[...] - t_ref[...]
        loss_ref[...] += 0.5 * jnp.sum(jnp.mean(err * err, axis=-1, keepdims=True), axis=0, keepdims=True)
        dy = err * (1.0 / D)
        dgf_ref[...] += jnp.sum(dy * xh, axis=0, keepdims=True)
        dxh = dy * gf_ref[...]
        dx1 = r * (dxh - xh * jnp.mean(dxh * xh, axis=-1, keepdims=True))
        dx1_ref[...] = dx1
        dout16 = (gate_ref[...] * dx1).astype(BF16)
        dout_ref[...] = dout16
        dgate_ref[...] += jnp.sum(dx1 * out_, axis=0, keepdims=True)
        dwo_ref[...] += _tn(mg, dout16)

    r512 = pl.BlockSpec((TOK, 512), lambda i: (i, 0))
    row = pl.BlockSpec((TOK, D), lambda i: (i, 0))
    vec = _full((1, D))
    return _pcall(
        body, name="tail_fwd", grid=(m // TOK,),
        in_specs=[r512, r512, pl.BlockSpec((TOK, 512), lambda i: (i, 6)), pl.BlockSpec((TOK, 512), lambda i: (i, 7)),
                  pl.BlockSpec((TOK, 512), lambda i: (i, 8)), row, pl.BlockSpec((TOK, D), lambda i: (i, 1)),
                  pl.BlockSpec((TOK, 256), lambda i: (i, 0)), pl.BlockSpec((2, TOK, 128), lambda i: (0, i, 0)), row, row,
                  _full((2, AC, AC)), _full((2, AC, 1)), _full((1, 512)), _resident((512, D)), _resident((512, D)),
                  _resident((D, D)), vec, vec],
        out_specs=[r512, r512, pl.BlockSpec((TOK, 256), lambda i: (i, 0)), _resident((D, D)), row, row, _full((1, 128)), vec, vec],
        out_shape=[SDS((m, 512), BF16), SDS((m, 512), BF16), SDS((m, 256), F32), SDS((D, D), F32), SDS((m, D), F32),
                   SDS((m, D), BF16), SDS((1, 128), F32), SDS((1, D), F32), SDS((1, D), F32)],
        compiler_params=_cp(("arbitrary",), VMEM_BIG),
    )(o_f, o_b, p, p, p, p, p, vnr, svc, x, tgt, ws01, bs01, gbn, wpa, wpb, wo, gate, gf)


DPR = 3072


def _tail_bwd(dout, ya_in, yb_in, p, svr, svc, o_f, o_b, gbn, wo, wpa, wpb):
    m = p.shape[0]

    def body(dout_ref, ya_ref, yb_ref, ga_ref, gb_ref, zb_ref, ua_ref, za_ref, svr_ref, svc_ref, of_ref, ob_ref, g_ref,
             wo_ref, wpa_ref, wpb_ref,
             dwpa_ref, dwpb_ref, dpg_ref, dpr_ref, dsr_ref, dsc_ref, do_ref, dg_ref):
        i = pl.program_id(0)

        @pl.when(i == 0)
        def _():
            dg_ref[...] = jnp.zeros_like(dg_ref)
            dwpa_ref[...] = jnp.zeros_like(dwpa_ref)
            dwpb_ref[...] = jnp.zeros_like(dwpb_ref)

        dm_ = _nt(dout_ref[...], wo_ref[...])
        ya_in, yb_in = ya_ref[...], yb_ref[...]
        ya = _nn(ya_in, wpa_ref[...])
        yb = _nn(yb_in, wpb_ref[...])
        sa = jax.nn.sigmoid(ga_ref[...].astype(F32))
        sb = jax.nn.sigmoid(gb_ref[...].astype(F32))
        dya16 = (dm_ * sa).astype(BF16)
        dyb16 = (dm_ * sb).astype(BF16)
        dwpa_ref[...] += _tn(ya_in, dya16)
        dwpb_ref[...] += _tn(yb_in, dyb16)
        dpg_ref[:, 0:D] = (dm_ * ya * (sa * (1.0 - sa))).astype(BF16)
        dpg_ref[:, D:2 * D] = (dm_ * yb * (sb * (1.0 - sb))).astype(BF16)
        dya = _nt(dya16, wpa_ref[...])
        dyb = _nt(dyb16, wpb_ref[...])

        u = ua_ref[...].astype(F32)
        za = za_ref[...].astype(F32)
        sz, dsz = _silu_and_grad(za)
        sv = jnp.concatenate([svr_ref[...], svc_ref[0], svc_ref[1]], axis=1)
        dpr_ref[:, 512:1024] = (dya * sv * sz).astype(BF16)
        dsv = dya * u * sz
        dsr_ref[...] = dsv[:, 0:256]
        dsc_ref[0] = dsv[:, 256:384]
        dsc_ref[1] = dsv[:, 384:512]
        dpr_ref[:, 1024:1536] = (dya * u * sv * dsz).astype(BF16)

        zb = zb_ref[...].astype(F32)
        o = of_ref[...].astype(F32) + ob_ref[...].astype(F32)
        szb, dszb = _silu_and_grad(zb)
        for h, (r, xh) in enumerate(_head_norm(o, None)):
            sl = slice(128 * h, 128 * h + 128)
            gh = g_ref[:, sl]
            don = dyb[:, sl] * szb[:, sl]
            dpr_ref[:, sl] = (dyb[:, sl] * (xh * gh) * dszb[:, sl]).astype(BF16)
            dg_ref[:, sl] += jnp.sum(don * xh, axis=0, keepdims=True)
            dxh = don * gh
            do_ref[:, sl] = (r * (dxh - xh * jnp.mean(dxh * xh, axis=-1, keepdims=True))).astype(BF16)

    r512 = pl.BlockSpec((TOK, 512), lambda i: (i, 0))
    row = pl.BlockSpec((TOK, D), lambda i: (i, 0))
    return _pcall(
        body, name="tail_bwd", grid=(m // TOK,),
        in_specs=[row, r512, r512, row, pl.BlockSpec((TOK, D), lambda i: (i, 1)), pl.BlockSpec((TOK, 512), lambda i: (i, 6)),
                  pl.BlockSpec((TOK, 512), lambda i: (i, 7)), pl.BlockSpec((TOK, 512), lambda i: (i, 8)),
                  pl.BlockSpec((TOK, 256), lambda i: (i, 0)), pl.BlockSpec((2, TOK, 128), lambda i: (0, i, 0)), r512, r512,
                  _full((1, 512)), _resident((D, D)), _resident((512, D)), _resident((512, D))],
        out_specs=[_resident((512, D)), _resident((512, D)), pl.BlockSpec((TOK, 2 * D), lambda i: (i, 0)),
                   pl.BlockSpec((TOK, 1536), lambda i: (i, 0)),
                   pl.BlockSpec((TOK, 256), lambda i: (i, 0)), pl.BlockSpec((2, TOK, 128), lambda i: (0, i, 0)), r512, _full((1, 512))],
        out_shape=[SDS((512, D), F32), SDS((512, D), F32), SDS((m, 2 * D), BF16), SDS((m, DPR), BF16), SDS((m, 256), F32),
                   SDS((2, m, 128), F32), SDS((m, 512), BF16), SDS((1, 512), F32)],
        compiler_params=_cp(("arbitrary",), VMEM_BIG),
    )(dout, ya_in, yb_in, p, p, p, p, p, svr, svc, o_f, o_b, gbn, wo, wpa, wpb)


def _ln_bwd(dsr, vnr, dvnc, p, ws01t, ln_g, dp):
    m = p.shape[0]

    def body(dsr_ref, vnr_ref, dvc_ref, va_ref, wt_ref, g_ref, dpi_ref, dp_ref, dw_ref, db_ref, dlg_ref, dlb_ref, dvn_ref):
        i = pl.program_id(0)

        @pl.when(i == 0)
        def _():
            dw_ref[...] = jnp.zeros_like(dw_ref)
            db_ref[...] = jnp.zeros_like(db_ref)
            dlg_ref[...] = jnp.zeros_like(dlg_ref)
            dlb_ref[...] = jnp.zeros_like(dlb_ref)

        for j in range(TOK // AC):
            for g in range(2):
                d = dsr_ref[AC * j:AC * j + AC, AC * g:AC * g + AC]
                d16 = d.astype(BF16)
                dvn_ref[AC * j:AC * j + AC, AC * g:AC * g + AC] = _nn(wt_ref[g], d16)
                dw_ref[g] += _nt(d16, vnr_ref[AC * j:AC * j + AC, AC * g:AC * g + AC])
                db_ref[g] += jnp.sum(d, axis=1, keepdims=True)
        dvn_ref[:, 256:384] = dvc_ref[0]
        dvn_ref[:, 384:512] = dvc_ref[1]
        dvn = dvn_ref[...]
        xf = va_ref[...].astype(F32)
        xc = xf - jnp.mean(xf, axis=-1, keepdims=True)
        rs = lax.rsqrt(jnp.mean(xc * xc, axis=-1, keepdims=True) + EPS)
        xh = xc * rs
        dlg_ref[...] += jnp.sum(dvn * xh, axis=0, keepdims=True)
        dlb_ref[...] += jnp.sum(dvn, axis=0, keepdims=True)
        dxh = dvn * g_ref[...]
        dva = rs * (dxh - jnp.mean(dxh, axis=-1, keepdims=True) - xh * jnp.mean(dxh * xh, axis=-1, keepdims=True))
        dp_ref[...] = dva.astype(BF16)

    return _pcall(
        body, name="ln_bwd", grid=(m // TOK,),
        in_specs=[pl.BlockSpec((TOK, 256), lambda i: (i, 0)), pl.BlockSpec((TOK, 256), lambda i: (i, 0)),
                  pl.BlockSpec((2, TOK, 128), lambda i: (0, i, 0)), pl.BlockSpec((TOK, 512), lambda i: (i, 9)),
                  _full((2, AC, AC)), _full((1, 512)), pl.BlockSpec(memory_space=pl.ANY)],
        out_specs=[pl.BlockSpec((TOK, 512), lambda i: (i, 3)), _full((2, AC, AC)), _full((2, AC, 1)), _full((1, 512)), _full((1, 512))],
        out_shape=[SDS((m, DPR), BF16), SDS((2, AC, AC), F32), SDS((2, AC, 1), F32), SDS((1, 512), F32), SDS((1, 512), F32)],
        scratch_shapes=[pltpu.VMEM((TOK, 512), F32)],
        input_output_aliases={6: 0}, compiler_params=_cp(("arbitrary",)),
    )(dsr, vnr, dvnc, p, ws01t, ln_g, dp)


def _tri_mm(tri, a):
    a1 = a.astype(BF16)
    r1 = a - a1.astype(F32)
    a2 = r1.astype(BF16)
    a3 = (r1 - a2.astype(F32)).astype(BF16)
    n = a.shape[1]
    r = _nn(tri, jnp.concatenate([a1, a2, a3], axis=1))
    return r[:, 0:n] + r[:, n:2 * n] + r[:, 2 * n:3 * n]


def _gla_masks(reverse):
    ri = lax.broadcasted_iota(jnp.int32, (CH, CH), 0)
    ci = lax.broadcasted_iota(jnp.int32, (CH, CH), 1)
    vis = (ci >= ri) if reverse else (ci <= ri)
    vis_t = (ci <= ri) if reverse else (ci >= ri)
    r4 = lax.broadcasted_iota(jnp.int32, (4 * CH, CH), 0) & (CH - 1)
    c4 = lax.broadcasted_iota(jnp.int32, (4 * CH, CH), 1)
    vis4 = (c4 >= r4) if reverse else (c4 <= r4)
    vis4_t = (c4 <= r4) if reverse else (c4 >= r4)
    lane = lax.broadcasted_iota(jnp.int32, (1, 256), 1)
    hm = [(lane >= CH * h) & (lane < CH * h + CH) for h in range(4)]
    return vis, vis_t, vis4, vis4_t, hm


def _stack_heads(x, hm):
    return jnp.concatenate([jnp.where(hm[h], x, 0.0).astype(BF16) for h in range(4)], axis=0)


def _diag_heads(full, hm):
    r = full.shape[0] // 4
    acc = jnp.where(hm[0], full[0:r], 0.0)
    for h in range(1, 4):
        acc = acc + jnp.where(hm[h], full[r * h:r * h + r], 0.0)
    return acc


def _rows_of_heads(x):
    return jnp.concatenate([x[:, 128 * h:128 * h + 128] for h in range(4)], axis=0)


def _lane_vis(reverse, transpose):
    ri = lax.broadcasted_iota(jnp.int32, (CH, 4 * CH), 0)
    ci = lax.broadcasted_iota(jnp.int32, (CH, 4 * CH), 1) & (CH - 1)
    return (ci >= ri) if (reverse != transpose) else (ci <= ri)


def _gla_fwd2(p, qkv_blk, lr, lrws, gbiases, s0s, name):
    m = p.shape[0]
    tb = min(GLA_TB, m)
    nb = m // tb
    nc = tb // CH

    def body(qkv_f, lr_f, qkv_b, lr_b, lrw_f, lrw_b, gb_f, gb_b, s0_f, s0_b,
             o_f, sb_f, sfin_f, o_b, sb_b, sfin_b, st_f, st_b):
        i = pl.program_id(0)

        @pl.when(i == 0)
        def _():
            st_f[...] = s0_f[...]
            st_b[...] = s0_b[...]

        dirs = []
        for reverse, qkv_ref, lr_ref, lrw_ref, gb_ref, o_ref, sb_ref, st_ref in (
                (False, qkv_f, lr_f, lrw_f, gb_f, o_f, sb_f, st_f), (True, qkv_b, lr_b, lrw_b, gb_b, o_b, sb_b, st_b)):
            vis, _, vis4, _, hm = _gla_masks(reverse)
            logits = _nn(lr_ref[...].astype(BF16), lrw_ref[...]) + gb_ref[...]
            dirs.append(dict(reverse=reverse, qkv=qkv_ref, o=o_ref, sb=sb_ref, vis4=vis4, hm=hm,
                             tri=vis.astype(F32).astype(BF16), a=_logsig(logits) * (1.0 / 16.0), st=st_ref[...]))
        for step in range(nc):
            for d in dirs:
                c = nc - 1 - step if d["reverse"] else step
                rows = slice(CH * c, CH * c + CH)
                b = _tri_mm(d["tri"], d["a"][rows])
                bl = b[0:1] if d["reverse"] else b[CH - 1:CH]
                q = d["qkv"][rows, 0:256].astype(F32) * 0.125
                k = d["qkv"][rows, 256:512].astype(F32)
                v16 = d["qkv"][rows, 512:1024].astype(BF16)
                qd = q * jnp.exp(b)
                kd16 = (k * jnp.exp(-b)).astype(BF16)
                kdec16 = (k * jnp.exp(bl - b)).astype(BF16)
                qstack = _stack_heads(qd, d["hm"])
                sc = jnp.where(d["vis4"], _nt(qstack, kd16), 0.0).astype(BF16)
                inter = _nt(qstack, d["st"].astype(BF16))
                for h in range(4):
                    d["o"][rows, 128 * h:128 * h + 128] = (
                        _nn(sc[CH * h:CH * h + CH], v16[:, 128 * h:128 * h + 128]) + inter[CH * h:CH * h + CH]).astype(BF16)
                d["sb"][c] = d["st"]
                d["st"] = d["st"] * jnp.exp(bl) + _diag_heads(_tn(v16, kdec16), d["hm"])
        st_f[...] = dirs[0]["st"]
        st_b[...] = dirs[1]["st"]

        @pl.when(i == nb - 1)
        def _():
            sfin_f[...] = dirs[0]["st"]
            sfin_b[...] = dirs[1]["st"]

    fw = lambda i: i
    bw = lambda i: nb - 1 - i
    in_specs = []
    for rm in (fw, bw):
        in_specs += [pl.BlockSpec((tb, 1024), lambda i, rm=rm: (rm(i), qkv_blk)), pl.BlockSpec((tb, LRW), lambda i, rm=rm: (rm(i), 0))]
    in_specs += [_full((LRW, 256))] * 2 + [_full((1, 256))] * 2 + [_full((128, 256))] * 2
    out_specs, out_shape = [], []
    for rm in (fw, bw):
        out_specs += [pl.BlockSpec((tb, 512), lambda i, rm=rm: (rm(i), 0)), pl.BlockSpec((nc, 128, 256), lambda i, rm=rm: (rm(i), 0, 0)),
                      _full((128, 256))]
        out_shape += [SDS((m, 512), BF16), SDS((m // CH, 128, 256), F32), SDS((128, 256), F32)]
    return _pcall(
        body, name=name, grid=(nb,), in_specs=in_specs, out_specs=out_specs, out_shape=out_shape,
        scratch_shapes=[pltpu.VMEM((128, 256), F32), pltpu.VMEM((128, 256), F32)], compiler_params=_cp(("arbitrary",), VMEM_BIG),
    )(p, lr, p, lr, lrws[0], lrws[1], gbiases[0], gbiases[1], s0s[0], s0s[1])


def _gla_bwd(p, qkv_blk, lr, lrw, lrwt, gbias, sb, dsfin, do, prev, dp, *, reverse, name):
    m = p.shape[0]
    tb = min(GLA_TB, m)
    nb = m // tb
    nc = tb // CH
    rmap = (lambda i: i) if reverse else (lambda i: nb - 1 - i)
    has_prev = prev is not None
    has_dp = dp is not None

    def body(*refs):
        refs = list(refs)
        qkv_ref, lr_ref, lrw_ref, lrwt_ref, gb_ref, sb_ref, dsfin_ref, do_ref = refs[:8]
        refs = refs[8:]
        if has_prev:
            pq_ref, plr_ref = refs[:2]
            refs = refs[2:]
        if has_dp:
            refs = refs[1:]
        dqkv_ref, dlr_ref, dw2_ref, dgb_ref, ds0_ref, dst_ref, dlog_ref = refs
        i = pl.program_id(0)

        @pl.when(i == 0)
        def _():
            dst_ref[...] = dsfin_ref[...]
            dw2_ref[...] = jnp.zeros_like(dw2_ref)
            dgb_ref[...] = jnp.zeros_like(dgb_ref)

        vis, vis_t, vis4, vis4_t, hm = _gla_masks(reverse)
        tri = vis.astype(F32).astype(BF16)
        tri_t = vis_t.astype(F32).astype(BF16)
        lane_vis = _lane_vis(reverse, False)
        lane_vis_t = _lane_vis(reverse, True)
        lr16 = lr_ref[...].astype(BF16)
        logits = _nn(lr16, lrw_ref[...]) + gb_ref[...]
        a_all = _logsig(logits) * (1.0 / 16.0)
        dsig = (1.0 - jax.nn.sigmoid(logits)) * (1.0 / 16.0)
        dst = dst_ref[...]
        for c in (range(nc) if reverse else range(nc - 1, -1, -1)):
            rows = slice(CH * c, CH * c + CH)
            b = _tri_mm(tri, a_all[rows])
            bl = b[0:1] if reverse else b[CH - 1:CH]
            eb = jnp.exp(b)
            enb = jnp.exp(-b)
            ebl = jnp.exp(bl - b)
            el = jnp.exp(bl)
            q = qkv_ref[rows, 0:256].astype(F32) * 0.125
            k = qkv_ref[rows, 256:512].astype(F32)
            v16 = qkv_ref[rows, 512:1024].astype(BF16)
            do16 = do_ref[rows, :].astype(BF16)
            qd = q * eb
            kd = k * enb
            kdec = k * ebl
            st = sb_ref[c]
            st16 = st.astype(BF16)
            dst16 = dst.astype(BF16)
            qd16 = qd.astype(BF16)
            kd16 = kd.astype(BF16)
            qstack = _stack_heads(qd, hm)
            kstack = _stack_heads(kd, hm)
            kdecstack = _stack_heads(kdec, hm)
            pt = jnp.where(vis4_t, _nt(kstack, qd16), 0.0).astype(BF16)
            dvinter = _nt(kdecstack, dst16)
            do_rows = _rows_of_heads(do16)
            v_rows = _rows_of_heads(v16)
            dp_cat = jnp.where(lane_vis, _diag_heads(_nt(do_rows, v_rows), hm), 0.0).astype(BF16)
            dpt_cat = jnp.where(lane_vis_t, _diag_heads(_nt(v_rows, do_rows), hm), 0.0).astype(BF16)
            dqd = _nn(dp_cat, kstack) + _diag_heads(_nn(do_rows, st16), hm)
            dkd = _nn(dpt_cat, qstack)
            dkdec = _diag_heads(_nn(v_rows, dst16), hm)
            for h in range(4):
                rh = slice(CH * h, CH * h + CH)
                dv_h = _nn(pt[rh], do_rows[rh]) + dvinter[rh]
                if has_prev:
                    dv_h = dv_h + pq_ref[rows, 512 + 128 * h:512 + 128 * h + 128]
                dqkv_ref[rows, 512 + 128 * h:512 + 128 * h + 128] = dv_h.astype(dqkv_ref.dtype)
            dq = dqd * eb * 0.125
            dk = dkd * enb + dkdec * ebl
            if has_prev:
                dq = dq + pq_ref[rows, 0:256]
                dk = dk + pq_ref[rows, 256:512]
            dqkv_ref[rows, 0:256] = dq.astype(dqkv_ref.dtype)
            dqkv_ref[rows, 256:512] = dk.astype(dqkv_ref.dtype)
            g_kdec = dkdec * kdec
            db = dqd * qd - dkd * kd - g_kdec
            dbl = jnp.sum(g_kdec, axis=0, keepdims=True) + jnp.sum(st * dst, axis=0, keepdims=True) * el
            da = _tri_mm(tri_t, db) + dbl
            dlog_ref[rows, :] = da * dsig[rows]
            dst = dst * el + _diag_heads(_tn(do16, qd16), hm)
        dst_ref[...] = dst
        dlog = dlog_ref[...]
        dlog16 = dlog.astype(BF16)
        dlr = _nn(dlog16, lrwt_ref[...])
        if has_prev:
            dlr = dlr + plr_ref[...]
        dlr_ref[...] = dlr
        dw2_ref[...] += _tn(lr16, dlog16)
        dgb_ref[...] += jnp.sum(dlog, axis=0, keepdims=True)

        @pl.when(i == nb - 1)
        def _():
            ds0_ref[...] = dst

    in_specs = [pl.BlockSpec((tb,1024), lambda i: (rmap(i), qkv_blk)), pl.BlockSpec((tb,LRW), lambda i: (rmap(i), 0)),
                _full((LRW, 256)), _full((256, LRW)), _full((1, 256)), pl.BlockSpec((nc, 128, 256), lambda i: (rmap(i), 0, 0)),
                _full((128, 256)), pl.BlockSpec((tb,512), lambda i: (rmap(i), 0))]
    args = [p, lr, lrw, lrwt, gbias, sb, dsfin, do]
    if has_prev:
        in_specs += [pl.BlockSpec((tb,1024), lambda i: (rmap(i), 0)), pl.BlockSpec((tb,LRW), lambda i: (rmap(i), 0))]
        args += list(prev)
    aliases = {}
    if has_dp:
        in_specs.append(pl.BlockSpec(memory_space=pl.ANY))
        aliases = {len(args): 0}
        args.append(dp)
        dq_spec = pl.BlockSpec((tb,1024), lambda i: (rmap(i), 2))
        dq_shape = SDS(dp.shape, dp.dtype)
    else:
        dq_spec = pl.BlockSpec((tb,1024), lambda i: (rmap(i), 0))
        dq_shape = SDS((m, 1024), F32)
    return _pcall(
        body, name=name, grid=(nb,), in_specs=in_specs,
        out_specs=[dq_spec, pl.BlockSpec((tb,LRW), lambda i: (rmap(i), 0)), _full((LRW, 256)), _full((1, 256)), _full((128, 256))],
        out_shape=[dq_shape, SDS((m, LRW), F32), SDS((LRW, 256), F32), SDS((1, 256), F32), SDS((128, 256), F32)],
        scratch_shapes=[pltpu.VMEM((128, 256), F32), pltpu.VMEM((tb,256), F32)],
        input_output_aliases=aliases, compiler_params=_cp(("arbitrary",)),
    )(*args)


EARLY_KEYS = ["dmodc", "dscc", "dng_c", "dlng", "dlnb", "dws", "dbs", "dgbn", "dgf", "dw2", "dgb2", "loss", "dgate"]
EARLY_SHAPES = [(1, 2 * D), (D,), (1, D), (1, 512), (1, 512), (1, 4, 128, 128), (1, 4, 128), (1, 512), (D,), (2, 16, 256), (2, 256),
                (128,), (1, D)]
EARLY_SIZE = 2 * D + D + D + 512 + 512 + 4 * 128 * 128 + 512 + 512 + D + 2 * 16 * 256 + 512 + 128 + D
EARLY_ROWS = 648


def _device_step(x, c, ctx, c_ctx, tgt, wm, bm, ng, wit_g, wit_r, wlrt, ln_g, ln_b, ws, bs, w2, gb2, gbn, wpa, wpb, wo, gf,
                 exchange=None, shards=None):
    L = x.shape[0]
    wit_qkv = wit_r[2048:3072]
    ws16 = ws.astype(BF16)
    wst16 = jnp.swapaxes(ws, 1, 2).astype(BF16)
    bscol = bs[:, :, None]
    lrw = [jnp.zeros((LRW, 256), F32).at[16 * r:16 * r + 16].set(w2[r]).astype(BF16) for r in range(2)]
    lrwt = [w.T for w in lrw]
    gbias = [gb2[r:r + 1] for r in range(2)]

    cc = jnp.zeros((8, D), F32).at[0:1].set(c).at[1:2].set(c_ctx)
    mod = _modvec(cc, wm, bm)
    shift, scale, gate = mod[0:1, 0:D], mod[0:1, D:2 * D], mod[0:1, 2 * D:3 * D]
    shift_c, scale_c = mod[1:2, 0:D], mod[1:2, D:2 * D]

    hc = _prep_h(ctx, ng, scale_c, shift_c, "prep_hc")
    pc = _mm(hc, wit_qkv, tm=256, tn=1024, tk=D, out_dtype=F32, name="mm_pc", b_t=True)
    plrc = _mm(hc, wlrt, tm=256, tn=LRW, tk=D, out_dtype=F32, name="mm_plrc", b_t=True)
    zero_s = jnp.zeros((128, 256), F32)
    _, sbc_f, sc_f, _, sbc_b, sc_b = _gla_fwd2(pc, 0, plrc, lrw, gbias, (zero_s, zero_s), "gla_fwd_c")

    h, p, plr, vnr, vnc, late = _proj_fwd(x, ng, scale, shift, wit_g, wit_r, wlrt, ln_g, ln_b,
                                          shards if shards is not None else ())
    if shards is not None:
        me_xy = 2 * lax.axis_index("x") + lax.axis_index("y")
        g_wpa, g_wpb, g_wo = (_own(g, s_, me_xy) for g, s_ in zip(late, shards))
        wpa = jnp.swapaxes(g_wpa, 0, 1).reshape(512, D)
        wpb = jnp.swapaxes(g_wpb, 0, 1).reshape(512, D)
        wo = g_wo.reshape(D, D)
    o_f, sb_f, _, o_b, sb_b, _ = _gla_fwd2(p, 2, plr, lrw, gbias, (sc_f, sc_b), "gla_fwd")
    svc = _colmix_fwd(vnc.reshape(2, AC, L), ws16[2:4], bscol[2:4]).reshape(2, L, 128)
    ya_in, yb_in, svr, dwo, dx1, dout, loss, dgate, dgf = _tail_fwd(
        o_f, o_b, p, vnr, svc, x, tgt, ws16[0:2], bscol[0:2], gbn, wpa, wpb, wo, gate, gf)

    dwpa, dwpb, dp_g, dp, dsr, dsc, do, dgbn = _tail_bwd(dout, ya_in, yb_in, p, svr, svc, o_f, o_b, gbn, wo, wpa, wpb)
    dvnc, dws23, dbs23 = _colmix_bwd(dsc.reshape(2, AC, L), vnc.reshape(2, AC, L), wst16[2:4])
    dp, dws01, dbs01, dlng, dlnb = _ln_bwd(dsr, vnr, dvnc.reshape(2, L, 128), p, wst16[0:2], ln_g, dp)
    zero_ds = jnp.zeros((128, 256), F32)
    dqkv_f, dlr_f, dw2_f, dgb_f, ds0_f = _gla_bwd(p, 2, plr, lrw[0], lrwt[0], gbias[0], sb_f, zero_ds, do, None, None,
                                                  reverse=False, name="gla_bwd_f")
    dp, dlr, dw2_b, dgb_b, ds0_b = _gla_bwd(p, 2, plr, lrw[1], lrwt[1], gbias[1], sb_b, zero_ds, do, (dqkv_f, dlr_f), dp,
                                            reverse=True, name="gla_bwd_b")
    zero_do = jnp.zeros((ctx.shape[0], 512), BF16)
    dqkvc_f, dlrc_f, dw2c_f, dgbc_f, _ = _gla_bwd(pc, 0, plrc, lrw[0], lrwt[0], gbias[0], sbc_f, ds0_f, zero_do, None, None,
                                                  reverse=False, name="gla_bwd_cf")
    dqkvc, dlrc, dw2c_b, dgbc_b, _ = _gla_bwd(pc, 0, plrc, lrw[1], lrwt[1], gbias[1], sbc_b, ds0_b, zero_do,
                                              (dqkvc_f, dlrc_f), None, reverse=True, name="gla_bwd_cb")
    dhc = _mm(dqkvc, wit_qkv, tm=256, tn=D, tk=1024, out_dtype=F32, name="mm_dhc")
    dhc = _mm(dlrc, wlrt, tm=256, tn=D, tk=LRW, out_dtype=F32, name="mm_dhc_lr", acc=dhc)
    _, dng_c, dscale_c, dshift_c = _prep_bwd(ctx, dhc, None, ng, scale_c, "prep_bwd_c")

    dwit_g = _mm_tn(dp_g, h, ta=1024, tn=D, tk=2048, name="mm_dwi_g")
    dwit_r = _mm_tn(dp, h, ta=1024, tn=D, tk=2048, name="mm_dwi_r")
    dwit_qkv = _mm_tn(dqkvc, hc, ta=1024, tn=D, tk=256, name="mm_dwi_c", acc=dwit_r[2048:3072])
    dwlrt = _mm_tn(dlr, h, ta=LRW, tn=D, tk=2048, name="mm_dwlr")
    dwlrt = _mm_tn(dlrc, hc, ta=LRW, tn=D, tk=256, name="mm_dwlr_c", acc=dwlrt)
    big = dict(dwit_g=dwit_g, dwit_r=dwit_r, dwit_qkv=dwit_qkv, dwlrt=dwlrt, dwpa=dwpa, dwpb=dwpb, dwo=dwo)

    dmodc = jnp.concatenate([dshift_c, dscale_c], axis=1)
    dscc = _dcctx(jnp.zeros((8, 2 * D), F32).at[0:1].set(dmodc), wm)[0:1]
    dw2p = dw2_f + dw2c_f, dw2_b + dw2c_b
    small = dict(
        dmodc=dmodc, dscc=dscc, dng_c=dng_c, dlng=dlng, dlnb=dlnb, dws=jnp.concatenate([dws01, dws23], axis=0),
        dbs=jnp.concatenate([dbs01, dbs23], axis=0)[:, :, 0], dgbn=dgbn, dgf=dgf,
        dw2=jnp.stack([dw2p[0][0:16], dw2p[1][16:32]]), dgb2=jnp.concatenate([dgb_f + dgbc_f, dgb_b + dgbc_b], axis=0),
        loss=loss[0, 0], dgate=dgate)

    send = exchange(big) if exchange is not None else ()
    early = _pack([small[k] for k in EARLY_KEYS[:-2]] + [jnp.broadcast_to(small["loss"], (128,)), small["dgate"]], EARLY_ROWS) \
        if exchange is not None else None
    (dx, dng, dscale, dshift), got, early_all = _proj_bwd(dp_g, dp, dlr, wit_g, wit_r, wlrt, x, dx1, ng, scale, send, early)
    return dict(dx=dx, got=got, early=early, early_all=early_all, dshift=dshift, dscale=dscale, dng_lat=dng, **big, **small)


ANY = pl.BlockSpec(memory_space=pl.ANY)


def _coords():
    return lax.axis_index("x"), lax.axis_index("y"), lax.axis_index("c")


def _flip(v, bit):
    return 1 - v if bit else v


def _remote(src, dst, send_sem, recv_sem, dev):
    return pltpu.make_async_remote_copy(src_ref=src, dst_ref=dst, send_sem=send_sem, recv_sem=recv_sem,
                                        device_id=dev, device_id_type=MESH)


def _own(out, block, idx):
    return lax.dynamic_update_slice_in_dim(out, block[None], idx, axis=0)


def _half_idx(shape, axis, which, lead=()):
    idx = [pl.ds(0, d) for d in shape]
    h = shape[axis] // 2
    idx[axis] = pl.ds(which * h, h)
    return tuple(lead) + tuple(idx)


def _gather_weights(split, whole, name):
    ns, nw = len(split), len(whole)
    n = ns + nw
    arrs = [a for a, _ in split] + list(whole)

    def body(*refs):
        ins, outs = refs[:n], refs[n:2 * n]
        a_send, a_recv, b_send, b_recv = refs[2 * n:]
        x, y, c = _coords()
        me = 2 * x + y
        sib = (x, y, 1 - c)
        peers = [(1 - x, y), (x, 1 - y), (1 - x, 1 - y)]

        def half(k, slot, which):
            return outs[k].at[_half_idx(arrs[k].shape, split[k][1], which, lead=(slot,))]

        sends = []
        for k in range(n):
            for j, (px, py) in enumerate(peers):
                if k < ns:
                    rc = _remote(ins[k].at[_half_idx(arrs[k].shape, split[k][1], c)], half(k, me, c), a_send.at[3 * k + j],
                                 a_recv.at[3 * k + j], (px, py, c))
                else:
                    rc = _remote(ins[k], outs[k].at[me], a_send.at[3 * k + j], a_recv.at[3 * k + j], (px, py, c))
                rc.start()
                sends.append(rc)
        for k in range(ns):
            for j, (px, py) in enumerate(peers):
                landed = half(k, 2 * px + py, c)
                _remote(landed, landed, a_send.at[3 * k + j], a_recv.at[3 * k + j], (px, py, c)).wait_recv()
                fw = _remote(landed, landed, b_send.at[3 * k + j], b_recv.at[3 * k + j], sib)
                fw.start()
                sends.append(fw)
        for k in range(ns, n):
            for j, (px, py) in enumerate(peers):
                landed = outs[k].at[2 * px + py]
                _remote(landed, landed, a_send.at[3 * k + j], a_recv.at[3 * k + j], (px, py, c)).wait_recv()
        for k in range(ns):
            for j, (px, py) in enumerate(peers):
                passed = half(k, 2 * px + py, 1 - c)
                _remote(passed, passed, b_send.at[3 * k + j], b_recv.at[3 * k + j], sib).wait_recv()
        for rc in sends:
            rc.wait_send()

    outs = _pcall(
        body, name=name, in_specs=[ANY] * n, out_specs=[ANY] * n,
        out_shape=[SDS((4,) + a.shape, a.dtype) for a in arrs],
        scratch_shapes=[pltpu.SemaphoreType.DMA((3 * n,)), pltpu.SemaphoreType.DMA((3 * n,)), pltpu.SemaphoreType.DMA((3 * ns,)),
                        pltpu.SemaphoreType.DMA((3 * ns,))],
    )(*arrs)
    me_xy = 2 * lax.axis_index("x") + lax.axis_index("y")
    return [_own(o, a, me_xy) for o, a in zip(outs, arrs)]


def _gather_all(a, swap, name):
    masks = [(mx, my, mc) for mx in range(2) for my in range(2) for mc in range(2)][1:]
    n = len(swap)

    def body(*refs):
        in_ref, sw_in = refs[0], refs[1:1 + n]
        out_ref, sw_out = refs[1 + n], refs[2 + n:2 + 2 * n]
        send_sems, recv_sems = refs[2 + 2 * n:]
        x, y, c = _coords()
        me = 4 * x + 2 * y + c
        sends = []
        for j, (mx, my, mc) in enumerate(masks):
            rc = _remote(in_ref, out_ref.at[me], send_sems.at[j], recv_sems.at[j], (_flip(x, mx), _flip(y, my), _flip(c, mc)))
            rc.start()
            sends.append(rc)
        for k in range(n):
            rc = _remote(sw_in[k], sw_out[k], send_sems.at[7 + k], recv_sems.at[7 + k], (x, y, 1 - c))
            rc.start()
            sends.append(rc)
        for j, (mx, my, mc) in enumerate(masks):
            px, py, pc = _flip(x, mx), _flip(y, my), _flip(c, mc)
            landed = out_ref.at[4 * px + 2 * py + pc]
            _remote(landed, landed, send_sems.at[j], recv_sems.at[j], (px, py, pc)).wait_recv()
        for k in range(n):
            _remote(sw_out[k], sw_out[k], send_sems.at[7 + k], recv_sems.at[7 + k], (x, y, 1 - c)).wait_recv()
        for rc in sends:
            rc.wait_send()

    res = _pcall(
        body, name=name, in_specs=[ANY] * (1 + n), out_specs=[ANY] * (1 + n),
        out_shape=[SDS((8,) + a.shape, a.dtype)] + [SDS(s_.shape, s_.dtype) for s_ in swap],
        scratch_shapes=[pltpu.SemaphoreType.DMA((7 + n,)), pltpu.SemaphoreType.DMA((7 + n,))],
    )(a, *swap)
    return _own(res[0], a, 4 * lax.axis_index("x") + 2 * lax.axis_index("y") + lax.axis_index("c")), list(res[1:])


def _half_shape(shape, axis):
    return tuple(d // 2 if i == axis else d for i, d in enumerate(shape))


def _swap_half_c(arrs, axes, name):
    n = len(arrs)

    def body(*refs):
        ins, outs = refs[:n], refs[n:2 * n]
        send_sems, recv_sems = refs[2 * n:]
        x, y, c = _coords()
        sends = []
        for k in range(n):
            rc = _remote(ins[k].at[_half_idx(arrs[k].shape, axes[k], 1 - c)], outs[k], send_sems.at[k], recv_sems.at[k],
                         (x, y, 1 - c))
            rc.start()
            sends.append(rc)
        for rc in sends:
            rc.wait()

    return _pcall(
        body, name=name, in_specs=[ANY] * n, out_specs=[ANY] * n,
        out_shape=[SDS(_half_shape(a.shape, ax), a.dtype) for a, ax in zip(arrs, axes)],
        scratch_shapes=[pltpu.SemaphoreType.DMA((n,)), pltpu.SemaphoreType.DMA((n,))],
    )(*arrs)


def _pair_sum(a, got, cidx, axis, name):
    _, r, cdim = a.shape
    hshape = _half_shape(a.shape, axis)

    def body(c_ref, a_ref, g_ref, o_ref):
        o_ref[...] = (a_ref[...] + g_ref[...]).astype(BF16)

    if axis == 1:
        tr = min(r // 2, 256)
        nj = (r // 2) // tr
        blk = pl.BlockSpec((1, tr, cdim), lambda s, j, c: (s, j, 0))
        a_spec = pl.BlockSpec((1, tr, cdim), lambda s, j, c: (s, c[0] * nj + j, 0))
    else:
        nj, hw = 1, cdim // 2
        blk = pl.BlockSpec((1, r, hw), lambda s, j, c: (s, 0, 0))
        a_spec = pl.BlockSpec((1, r, hw), lambda s, j, c: (s, 0, c[0]))
    return _pcall(
        body, name=name, out_shape=SDS(hshape, BF16),
        grid_spec=pltpu.PrefetchScalarGridSpec(num_scalar_prefetch=1, grid=(4, nj), in_specs=[a_spec, blk], out_specs=blk),
        compiler_params=_cp(("parallel", "parallel"), VMEM_BIG),
    )(cidx, a, got)


def _sum_chips(parts, name):
    _, h, cdim = parts.shape

    def body(p_ref, o_ref):
        acc = p_ref[0].astype(F32)
        for k in range(1, 4):
            acc = acc + p_ref[k].astype(F32)
        o_ref[...] = acc

    if h % 256 == 0 or h in (128,):
        tr = min(h, 256)
        grid, in_spec, out_spec = (h // tr,), pl.BlockSpec((4, tr, cdim), lambda i: (0, i, 0)), pl.BlockSpec((tr, cdim), lambda i: (i, 0))
    else:
        lw = 256
        grid, in_spec, out_spec = (cdim // lw,), pl.BlockSpec((4, h, lw), lambda i: (0, 0, i)), pl.BlockSpec((h, lw), lambda i: (0, i))
    return _pcall(
        body, name=name, grid=grid, in_specs=[in_spec], out_specs=out_spec, out_shape=SDS((h, cdim), F32),
        compiler_params=_cp(("parallel",), VMEM_BIG),
    )(parts)


def _sum_slots(a, name, rows):
    s, n, _ = a.shape

    def body(a_ref, o_ref):
        acc = a_ref[0]
        for k in range(1, s):
            acc = acc + a_ref[k]
        o_ref[...] = acc

    return _pcall(
        body, name=name, grid=(n // rows,), in_specs=[pl.BlockSpec((s, rows, 128), lambda i: (0, i, 0))],
        out_specs=pl.BlockSpec((rows, 128), lambda i: (i, 0)), out_shape=SDS((n, 128), F32),
        compiler_params=_cp(("parallel",)),
    )(a)


def _adam_math(w, g, m, v):
    nm = ADAM_B1 * m + (1.0 - ADAM_B1) * g
    nv = ADAM_B2 * v + (1.0 - ADAM_B2) * (g * g)
    m_hat = nm / (1.0 - ADAM_B1 ** ADAM_STEP)
    v_hat = nv / (1.0 - ADAM_B2 ** ADAM_STEP)
    return -ADAM_LR * (m_hat / (jnp.sqrt(v_hat) + ADAM_EPS) + ADAM_WD * w), nm, nv


def _adamw(w, g, m, v, name, rows):
    r, cdim = w.shape

    def body(w_ref, g_ref, m_ref, v_ref, d_ref, nm_ref, nv_ref):
        d_ref[...], nm_ref[...], nv_ref[...] = _adam_math(w_ref[...], g_ref[...], m_ref[...], v_ref[...])

    blk = pl.BlockSpec((rows, cdim), lambda i: (i, 0))
    return _pcall(
        body, name=name, grid=(r // rows,), in_specs=[blk] * 4, out_specs=[blk] * 3,
        out_shape=[SDS(w.shape, F32)] * 3, compiler_params=_cp(("parallel",)),
    )(w, g, m, v)


def _adamw_joined(w, mine, other, m, v, cidx, axis, name, rows):
    r, cdim = w.shape
    if axis == 0:
        rows = r

    def body(c_ref, w_ref, a_ref, b_ref, m_ref, v_ref, g_ref, d_ref, nm_ref, nv_ref):
        a, b = a_ref[...], b_ref[...]
        g = jnp.where(c_ref[0] == 0, jnp.concatenate([a, b], axis=axis), jnp.concatenate([b, a], axis=axis))
        g_ref[...] = g
        d_ref[...], nm_ref[...], nv_ref[...] = _adam_math(w_ref[...], g, m_ref[...], v_ref[...])

    blk = pl.BlockSpec((rows, cdim), lambda i, c: (i, 0))
    hshape = (rows // 2, cdim) if axis == 0 else (rows, cdim // 2)
    hblk = pl.BlockSpec(hshape, lambda i, c: (i, 0))
    return _pcall(
        body, name=name, out_shape=[SDS(w.shape, F32)] * 4,
        grid_spec=pltpu.PrefetchScalarGridSpec(num_scalar_prefetch=1, grid=(r // rows,), in_specs=[blk, hblk, hblk, blk, blk],
                                               out_specs=[blk] * 4),
        compiler_params=_cp(("parallel",)),
    )(cidx, w, mine, other, m, v)


def _adamw_many(ws, gs, ms, vs, name):
    n = len(ws)

    def body(*refs):
        outs = refs[4 * n:]
        for k in range(n):
            d, nm, nv = _adam_math(refs[k][...], refs[n + k][...], refs[2 * n + k][...], refs[3 * n + k][...])
            outs[k][...] = d
            outs[n + k][...] = nm
            outs[2 * n + k][...] = nv

    res = _pcall(body, name=name, out_shape=[SDS(w.shape, F32) for w in ws] * 3)(*ws, *gs, *ms, *vs)
    return res[:n], res[n:2 * n], res[2 * n:]


def _pack(pieces, rows):
    flat = jnp.concatenate([p.reshape(-1) for p in pieces])
    return jnp.pad(flat, (0, rows * 128 - flat.shape[0])).reshape(rows, 128)


def _unpack(buf, shapes):
    flat = buf.reshape(-1)
    out, off = [], 0
    for shp in shapes:
        size = 1
        for s in shp:
            size *= s
        out.append(flat[off:off + size].reshape(shp))
        off += size
    return out


LATE_ROWS = 32


def kernel(x, c, ctx, c_ctx, w_mod, b_mod, norm_g, w_in, a_ln_g, a_ln_b, a_ws, a_bs, b_gate_w2, b_gate_b, b_norm_g, w_proj_a, w_proj_b, w_out, final_norm_g, loss_target, m_c_ctx, m_w_mod, m_b_mod, m_norm_g, m_w_in, m_a_ln_g, m_a_ln_b, m_a_ws, m_a_bs, m_b_gate_w2, m_b_gate_b, m_b_norm_g, m_w_proj_a, m_w_proj_b, m_w_out, m_final_norm_g, v_c_ctx, v_w_mod, v_b_mod, v_norm_g, v_w_in, v_a_ln_g, v_a_ln_b, v_a_ws, v_a_bs, v_b_gate_w2, v_b_gate_b, v_b_norm_g, v_w_proj_a, v_w_proj_b, v_w_out, v_final_norm_g):
    xi, yi, ci = _coords()
    me_xy = 2 * xi + yi

    gate_pack = _pack([b_gate_w2[0], b_gate_b[0]], 24)
    w_in_t, m_w_in_t, v_w_in_t = (jnp.swapaxes(a[0], 0, 1) for a in (w_in, m_w_in, v_w_in))
    g_wit, g_wm, g_gate = _gather_weights([(w_in_t.astype(BF16), 1), (w_mod[0].astype(BF16), 0)], [gate_pack], "gather_weights")
    late_shards = (w_proj_a[0].astype(BF16), w_proj_b[0].astype(BF16), w_out[0].astype(BF16))
    wit_u = g_wit.reshape(4 * 1288, D)
    wit_g = wit_u[3104:5152]
    wit_r = jnp.concatenate([wit_u[1056:1568], wit_u[1568:2080], wit_u[2592:3104], wit_u[2080:2592], wit_u[0:1024]], axis=0)
    wlrt = jnp.pad(wit_u[1024:1056], ((0, LRW - 32), (0, 0)))
    wm = jnp.swapaxes(g_wm, 0, 1).reshape(D, 3 * D)
    gflat = g_gate.reshape(4, 24 * 128)
    w2 = jnp.swapaxes(gflat[:, 0:2048].reshape(4, 2, 16, 64), 0, 2)
    w2 = jnp.swapaxes(w2, 0, 1).reshape(2, 16, 256)
    gb2 = jnp.swapaxes(gflat[:, 2048:2176].reshape(4, 2, 64), 0, 1).reshape(2, 256)

    tags = ["wi", "wpa", "wpb", "wo"]
    half_axes = [2, 1, 1, 1]
    sent = []

    def exchange(g):
        dwr = g["dwit_r"]
        dwit_u = jnp.concatenate([g["dwit_qkv"], g["dwlrt"][0:32], dwr[0:512], dwr[512:1024], dwr[1536:2048], dwr[1024:1536],
                                  g["dwit_g"]], axis=0)
        big = [dwit_u.reshape(4, 1288, D), jnp.swapaxes(g["dwpa"].reshape(512, 4, 256), 0, 1),
               jnp.swapaxes(g["dwpb"].reshape(512, 4, 256), 0, 1), g["dwo"].reshape(4, 256, D)]
        other = _swap_half_c(big, half_axes, "swap_half_in")
        cidx = jnp.reshape(ci, (1,)).astype(jnp.int32)
        sent.extend(_pair_sum(a, o, cidx, ax, "sum_pair_" + t) for a, o, ax, t in zip(big, other, half_axes, tags))
        return sent

    r = _device_step(x[0], c, ctx[0], c_ctx[None], loss_target[0], wm, b_mod, norm_g, wit_g, wit_r, wlrt, a_ln_g, a_ln_b,
                     a_ws[0], a_bs[0], w2, gb2, b_norm_g, None, None, None, final_norm_g[None], exchange, late_shards)

    parts = [_own(g, lax.dynamic_index_in_dim(s_, me_xy, axis=0, keepdims=False), me_xy) for g, s_ in zip(r["got"], sent)]
    halves = [_sum_chips(p_, "sum_chips_" + t) for p_, t in zip(parts, tags)]

    me8 = 4 * xi + 2 * yi + ci
    early_all = _own(r["early_all"], r["early"], me8)
    late = _pack([r["dshift"], r["dscale"], r["dng_lat"], c], LATE_ROWS)
    late_all, others = _gather_all(late, halves, "gather_small")
    s_early = _sum_slots(early_all, "sum_early", EARLY_ROWS // 3)
    s_late = _sum_slots(late_all, "sum_late", LATE_ROWS)
    (s_dmodc, s_dscc, s_dng_c, s_dlng, s_dlnb, s_dws, s_dbs, s_dgbn, s_dgf, s_dw2, s_dgb2, s_loss, s_dgate) = _unpack(
        s_early, EARLY_SHAPES)
    s_dshift, s_dscale, s_dng_lat, _ = _unpack(s_late, [(1, D)] * 4)
    s_dng = s_dng_lat + s_dng_c
    s_dmod = jnp.concatenate([s_dshift, s_dscale, s_dgate], axis=1)
    loss = s_loss[0]
    s_dmodc_p = jnp.pad(s_dmodc, ((0, 0), (0, D)))
    g_b_mod = s_dmod + s_dmodc_p
    sg = jax.nn.sigmoid(c_ctx)
    g_c_ctx = s_dscc * (sg * (1.0 + c_ctx * (1.0 - sg)))
    g_w2 = lax.dynamic_slice_in_dim(s_dw2, 64 * me_xy, 64, axis=2)[None]
    g_gb2 = lax.dynamic_slice_in_dim(s_dgb2, 64 * me_xy, 64, axis=1)[None]

    flat_l = late_all.reshape(8, LATE_ROWS * 128)
    dgate_all = early_all.reshape(8, EARLY_ROWS * 128)[:, EARLY_SIZE - D:EARLY_SIZE]
    dmod_all = jnp.concatenate([flat_l[:, 0:2 * D], dgate_all], axis=1)
    c_all = flat_l[:, 3 * D:4 * D]
    lhs = jnp.concatenate([_silu(c_all), _silu(c_ctx)[None], jnp.zeros((7, D), F32)], axis=0)
    rhs = jnp.concatenate([dmod_all, s_dmodc_p, jnp.zeros((7, 3 * D), F32)], axis=0)
    rhs = lax.dynamic_slice_in_dim(rhs, 768 * me_xy, 768, axis=1)
    g_w_mod = _mm(lhs.T.astype(BF16), rhs.astype(BF16), tm=D, tn=768, tk=16, out_dtype=F32, name="mm_dwm")

    cidx = jnp.reshape(ci, (1,)).astype(jnp.int32)
    g_w_in_t, d_w_in_t, nm_w_in_t, nv_w_in_t = _adamw_joined(w_in_t, halves[0], others[0], m_w_in_t, v_w_in_t, cidx, 1,
                                                             "adamw_w_in", 184)
    g_w_in, d_w_in, nm_w_in, nv_w_in = (jnp.swapaxes(a, 0, 1) for a in (g_w_in_t, d_w_in_t, nm_w_in_t, nv_w_in_t))
    g_wpa, d_wpa, nm_wpa, nv_wpa = _adamw_joined(w_proj_a[0], halves[1], others[1], m_w_proj_a[0], v_w_proj_a[0], cidx, 0,
                                                 "adamw_wpa", 0)
    g_wpb, d_wpb, nm_wpb, nv_wpb = _adamw_joined(w_proj_b[0], halves[2], others[2], m_w_proj_b[0], v_w_proj_b[0], cidx, 0,
                                                 "adamw_wpb", 0)
    g_wo, d_wo, nm_wo, nv_wo = _adamw_joined(w_out[0], halves[3], others[3], m_w_out[0], v_w_out[0], cidx, 0, "adamw_wo", 0)
    d_w_mod, nm_w_mod, nv_w_mod = _adamw(w_mod[0], g_w_mod, m_w_mod[0], v_w_mod[0], "adamw_w_mod", 256)

    names = ["c_ctx", "b_mod", "norm_g", "a_ln_g", "a_ln_b", "a_ws", "a_bs", "b_gate_w2", "b_gate_b", "b_norm_g", "final_norm_g"]
    ws_ = [c_ctx, b_mod, norm_g, a_ln_g, a_ln_b, a_ws, a_bs, b_gate_w2, b_gate_b, b_norm_g, final_norm_g]
    gs_ = [g_c_ctx, g_b_mod, s_dng, s_dlng, s_dlnb, s_dws, s_dbs, g_w2, g_gb2, s_dgbn, s_dgf]
    ms_ = [m_c_ctx, m_b_mod, m_norm_g, m_a_ln_g, m_a_ln_b, m_a_ws, m_a_bs, m_b_gate_w2, m_b_gate_b, m_b_norm_g, m_final_norm_g]
    vs_ = [v_c_ctx, v_b_mod, v_norm_g, v_a_ln_g, v_a_ln_b, v_a_ws, v_a_bs, v_b_gate_w2, v_b_gate_b, v_b_norm_g, v_final_norm_g]
    shapes = [w.shape for w in ws_]
    flat2 = [(1, 1024), (1, 3072), (1, 1024), (1, 512), (1, 512), (512, 128), (4, 128), (32, 64), (2, 64), (1, 512), (1, 1024)]
    as2d = lambda arrs: [a.reshape(s) for a, s in zip(arrs, flat2)]
    d_s, nm_s, nv_s = _adamw_many(as2d(ws_), as2d(gs_), as2d(ms_), as2d(vs_), "adamw_small")
    d_small = {n: a.reshape(s) for n, a, s in zip(names, d_s, shapes)}
    nm_small = {n: a.reshape(s) for n, a, s in zip(names, nm_s, shapes)}
    nv_small = {n: a.reshape(s) for n, a, s in zip(names, nv_s, shapes)}
    g_small = {n: g.reshape(s) for n, g, s in zip(names, gs_, shapes)}

    order = ["c_ctx", "w_mod", "b_mod", "norm_g", "w_in", "a_ln_g", "a_ln_b", "a_ws", "a_bs", "b_gate_w2", "b_gate_b", "b_norm_g",
             "w_proj_a", "w_proj_b", "w_out", "final_norm_g"]
    big_g = dict(w_mod=g_w_mod[None], w_in=g_w_in[None], w_proj_a=g_wpa[None], w_proj_b=g_wpb[None], w_out=g_wo[None])
    big_d = dict(w_mod=d_w_mod[None], w_in=d_w_in[None], w_proj_a=d_wpa[None], w_proj_b=d_wpb[None], w_out=d_wo[None])
    big_m = dict(w_mod=nm_w_mod[None], w_in=nm_w_in[None], w_proj_a=nm_wpa[None], w_proj_b=nm_wpb[None], w_out=nm_wo[None])
    big_v = dict(w_mod=nv_w_mod[None], w_in=nv_w_in[None], w_proj_a=nv_wpa[None], w_proj_b=nv_wpb[None], w_out=nv_wo[None])
    grads = [big_g[n] if n in big_g else g_small[n] for n in order]
    deltas = [big_d[n] if n in big_d else d_small[n] for n in order]
    new_m = [big_m[n] if n in big_m else nm_small[n] for n in order]
    new_v = [big_v[n] if n in big_v else nv_small[n] for n in order]
    return (loss, r["dx"][None], *grads, *deltas, *new_m, *new_v)
```

```python
import jax
import jax.numpy as jnp
from jax import lax
from jax.experimental import pallas as pl
from jax.experimental.pallas import tpu as pltpu

F32 = jnp.float32
BF16 = jnp.bfloat16
SDS = jax.ShapeDtypeStruct

D = 1024
NP = 5120
LRW = 128
CH = 64
AC = 128
EPS = 1e-6
TOK = 512
GLA_TB = 1024
VMEM_BIG = 48 * 1024 * 1024

ADAM_LR, ADAM_B1, ADAM_B2, ADAM_EPS, ADAM_WD, ADAM_STEP = 0.001, 0.9, 0.999, 1e-08, 0.01, 10

_pcall = pl.pallas_call
MESH = pl.DeviceIdType.MESH


def _cp(sem=None, vmem=None):
    kw = {}
    if sem is not None:
        kw["dimension_semantics"] = sem
    if vmem is not None:
        kw["vmem_limit_bytes"] = vmem
    return pltpu.CompilerParams(**kw)


def _silu(x):
    return x * jax.nn.sigmoid(x)


def _silu_and_grad(x):
    s = jax.nn.sigmoid(x)
    return x * s, s * (1.0 + x * (1.0 - s))


def _logsig(x):
    return jnp.minimum(x, 0.0) - jnp.log1p(jnp.exp(-jnp.abs(x)))


def _nt(a, b):
    return lax.dot_general(a, b, (((1,), (1,)), ((), ())), preferred_element_type=F32)


def _tn(a, b):
    return lax.dot_general(a, b, (((0,), (0,)), ((), ())), preferred_element_type=F32)


def _nn(a, b):
    return jnp.dot(a, b, preferred_element_type=F32)


def _full(shape):
    return pl.BlockSpec(shape, lambda *_: (0,) * len(shape))


def _mm(a, b, *, tm, tn, tk, out_dtype, name, acc=None, b_t=False):
    m, k = a.shape
    n, k2 = (b.shape if b_t else b.shape[::-1])
    assert k == k2 == tk and m % tm == 0 and n % tn == 0, (a.shape, b.shape, tm, tn, tk)
    has_acc = acc is not None

    def body(*refs):
        if has_acc:
            a_ref, b_ref, c_ref, o_ref = refs
        else:
            a_ref, b_ref, o_ref = refs
        part = (_nt if b_t else _nn)(a_ref[...].astype(BF16), b_ref[...].astype(BF16))
        o_ref[...] = ((c_ref[...] + part) if has_acc else part).astype(out_dtype)

    b_spec = pl.BlockSpec((tn, tk), lambda i, j: (j, 0)) if b_t else pl.BlockSpec((tk, tn), lambda i, j: (0, j))
    in_specs = [pl.BlockSpec((tm, tk), lambda i, j: (i, 0)), b_spec]
    args = [a, b]
    if has_acc:
        in_specs.append(pl.BlockSpec((tm, tn), lambda i, j: (i, j)))
        args.append(acc)
    return _pcall(
        body, name=name, grid=(m // tm, n // tn), in_specs=in_specs, out_specs=pl.BlockSpec((tm, tn), lambda i, j: (i, j)),
        out_shape=SDS((m, n), out_dtype), compiler_params=_cp(("parallel", "parallel"), VMEM_BIG),
    )(*args)


def _mm_tn(a, b, *, ta, tn, tk, name, acc=None):
    m, ka = a.shape
    m2, n = b.shape
    assert m == m2 and ka % ta == 0 and n % tn == 0 and m % tk == 0, (a.shape, b.shape, ta, tn, tk)
    nk = m // tk
    has_acc = acc is not None

    def body(*refs):
        if has_acc:
            a_ref, b_ref, c_ref, o_ref = refs
        else:
            a_ref, b_ref, o_ref = refs
        kk = pl.program_id(2)
        part = _tn(a_ref[...].astype(BF16), b_ref[...].astype(BF16))

        @pl.when(kk == 0)
        def _():
            if has_acc:
                o_ref[...] = c_ref[...] + part
            else:
                o_ref[...] = part

        @pl.when(kk > 0)
        def _():
            o_ref[...] += part

    in_specs = [pl.BlockSpec((tk, ta), lambda i, j, kk: (kk, i)), pl.BlockSpec((tk, tn), lambda i, j, kk: (kk, j))]
    args = [a, b]
    if has_acc:
        in_specs.append(pl.BlockSpec((ta, tn), lambda i, j, kk: (i, j)))
        args.append(acc)
    return _pcall(
        body, name=name, grid=(ka // ta, n // tn, nk), in_specs=in_specs,
        out_specs=pl.BlockSpec((ta, tn), lambda i, j, kk: (i, j)), out_shape=SDS((ka, n), F32),
        compiler_params=_cp(("parallel", "parallel", "arbitrary"), VMEM_BIG),
    )(*args)


def _modvec(cc, wm, bm):
    def body(c_ref, w_ref, b_ref, o_ref):
        o_ref[...] = _nn(_silu(c_ref[...]).astype(BF16), w_ref[...]) + b_ref[...]

    return _pcall(body, name="modvec", out_shape=SDS((8, 3 * D), F32), compiler_params=_cp(None, VMEM_BIG))(cc, wm, bm)


def _dcctx(dmodc, wm):
    def body(d_ref, w_ref, o_ref):
        o_ref[...] = _nt(d_ref[...].astype(BF16), w_ref[...])

    return _pcall(
        body, name="dcctx", grid=(1,), in_specs=[_full((8, 2 * D)), pl.BlockSpec((D, 2 * D), lambda i: (0, 0))],
        out_specs=_full((8, D)), out_shape=SDS((8, D), F32), compiler_params=_cp(("arbitrary",), VMEM_BIG),
    )(dmodc, wm)


def _prep_h(x, ng, scale, shift, name):
    m = x.shape[0]

    def body(x_ref, g_ref, sc_ref, sh_ref, h_ref):
        xf = x_ref[...]
        r = lax.rsqrt(jnp.mean(xf * xf, axis=-1, keepdims=True) + EPS)
        y = (xf * r) * g_ref[...]
        h_ref[...] = (y * (1.0 + sc_ref[...]) + sh_ref[...]).astype(BF16)

    tok = min(TOK, m)
    row = pl.BlockSpec((tok, D), lambda i: (i, 0))
    return _pcall(
        body, name=name, grid=(m // tok,), in_specs=[row, _full((1, D)), _full((1, D)), _full((1, D))],
        out_specs=row, out_shape=SDS((m, D), BF16), compiler_params=_cp(("parallel",)),
    )(x, ng, scale, shift)


def _resident(shape):
    return pl.BlockSpec(shape, lambda *_: (0,) * len(shape), pipeline_mode=pl.Buffered(1))


PROJ_TM = 512


def _proj_fwd(x, ng, scale, shift, wit_g, wit_r, wlrt, ln_g, ln_b, share=()):
    m = x.shape[0]
    ns = len(share)
    steps = m // PROJ_TM
    src = [(0, 0), (0, D), (1, 2 * D), (1, 0), (1, D)]

    def body(*refs):
        x_ref, g_ref, sc_ref, sh_ref, wg_ref, wr_ref, wl_ref, lg_ref, lb_ref = refs[:9]
        share_refs = refs[9:9 + ns]
        h_ref, p_ref, plr_ref, vr_ref, vc_ref = refs[9 + ns:14 + ns]
        got_refs = refs[14 + ns:14 + 2 * ns]
        sems = refs[14 + 2 * ns:]

        def copies():
            cx, cy, cc = _coords()
            me = 2 * cx + cy
            peers = [(1 - cx, cy), (cx, 1 - cy), (1 - cx, 1 - cy)]
            out, back = [], []
            for k in range(ns):
                for j, (px, py) in enumerate(peers):
                    out.append(_remote(share_refs[k], got_refs[k].at[me], sems[0].at[3 * k + j], sems[1].at[3 * k + j], (px, py, cc)))
                    landed = got_refs[k].at[2 * px + py]
                    back.append(_remote(landed, landed, sems[0].at[3 * k + j], sems[1].at[3 * k + j], (px, py, cc)))
            return out, back

        if ns:
            @pl.when(pl.program_id(0) == 0)
            def _():
                for rc in copies()[0]:
                    rc.start()

            @pl.when(pl.program_id(0) == steps - 1)
            def _():
                out, back = copies()
                for rc in back:
                    rc.wait_recv()
                for rc in out:
                    rc.wait_send()

        xf = x_ref[...]
        r = lax.rsqrt(jnp.mean(xf * xf, axis=-1, keepdims=True) + EPS)
        y = (xf * r) * g_ref[...]
        h = (y * (1.0 + sc_ref[...]) + sh_ref[...]).astype(BF16)
        h_ref[...] = h
        for j, (which, r0) in enumerate(src):
            w_ref = wr_ref if which else wg_ref
            blk = _nt(h, w_ref[r0:r0 + D, :]).astype(BF16)
            p_ref[:, D * j:D * j + D] = blk
            if j == 4:
                xf = blk[:, 512:1024].astype(F32)
                xc = xf - jnp.mean(xf, axis=-1, keepdims=True)
                vn = (xc * lax.rsqrt(jnp.mean(xc * xc, axis=-1, keepdims=True) + EPS)) * lg_ref[...] + lb_ref[...]
                vr_ref[...] = vn[:, 0:256].astype(BF16)
                vc_ref[0] = vn[:, 256:384].astype(BF16)
                vc_ref[1] = vn[:, 384:512].astype(BF16)
        plr_ref[...] = _nt(h, wl_ref[...])

    row = pl.BlockSpec((PROJ_TM, D), lambda i: (i, 0))
    vec = _full((1, D))
    res = _pcall(
        body, name="proj_fwd", grid=(steps,),
        in_specs=[row, vec, vec, vec, _resident((2 * D, D)), _resident((3 * D, D)), _resident((LRW, D)), _full((1, 512)),
                  _full((1, 512))] + [ANY] * ns,
        out_specs=[row, pl.BlockSpec((PROJ_TM, NP), lambda i: (i, 0)), pl.BlockSpec((PROJ_TM, LRW), lambda i: (i, 0)),
                   pl.BlockSpec((PROJ_TM, 256), lambda i: (i, 0)), pl.BlockSpec((2, PROJ_TM, 128), lambda i: (0, i, 0))] + [ANY] * ns,
        out_shape=[SDS((m, D), BF16), SDS((m, NP), BF16), SDS((m, LRW), F32), SDS((m, 256), BF16), SDS((2, m, 128), BF16)]
        + [SDS((4,) + a.shape, a.dtype) for a in share],
        scratch_shapes=([pltpu.SemaphoreType.DMA((3 * ns,)), pltpu.SemaphoreType.DMA((3 * ns,))] if ns else []),
        compiler_params=_cp(("arbitrary",), VMEM_BIG),
    )(x, ng, scale, shift, wit_g, wit_r, wlrt, ln_g, ln_b, *share)
    return res[0], res[1], res[2], res[3], res[4], list(res[5:])


def _proj_bwd(dp_g, dp_r, dlr, wit_g, wit_r, wlrt, x, dx1, ng, scale, send=(), share8=None):
    m = x.shape[0]
    ns = len(send)
    n8 = 0 if share8 is None else 1
    steps = m // PROJ_TM
    masks = [(mx, my, mc) for mx in range(2) for my in range(2) for mc in range(2)][1:]

    def body(*refs):
        (dpg_ref, dpr_ref, dlr_ref, wg_ref, wr_ref, wl_ref, x_ref, r_ref, g_ref, sc_ref) = refs[:10]
        send_refs = refs[10:10 + ns]
        n_in = 10 + ns + n8
        dx_ref, dg_ref, dsc_ref, dsh_ref = refs[n_in:n_in + 4]
        got_refs = refs[n_in + 4:n_in + 4 + ns]
        sems = refs[n_in + 4 + ns + n8:]
        i = pl.program_id(0)

        def copies():
            cx, cy, cc = _coords()
            me = 2 * cx + cy
            peers = [(1 - cx, cy), (cx, 1 - cy), (1 - cx, 1 - cy)]
            out, back = [], []
            for k in range(ns):
                for j, (px, py) in enumerate(peers):
                    out.append(_remote(send_refs[k].at[2 * px + py], got_refs[k].at[me], sems[0].at[3 * k + j],
                                       sems[1].at[3 * k + j], (px, py, cc)))
                    landed = got_refs[k].at[2 * px + py]
                    back.append(_remote(landed, landed, sems[0].at[3 * k + j], sems[1].at[3 * k + j], (px, py, cc)))
            if n8:
                src8, all8 = refs[10 + ns], refs[n_in + 4 + ns]
                s8, r8 = sems[-2], sems[-1]
                for j, (mx, my, mc) in enumerate(masks):
                    px, py, pc = _flip(cx, mx), _flip(cy, my), _flip(cc, mc)
                    out.append(_remote(src8, all8.at[4 * cx + 2 * cy + cc], s8.at[j], r8.at[j], (px, py, pc)))
                    landed = all8.at[4 * px + 2 * py + pc]
                    back.append(_remote(landed, landed, s8.at[j], r8.at[j], (px, py, pc)))
            return out, back

        @pl.when(i == 0)
        def _():
            dg_ref[...] = jnp.zeros_like(dg_ref)
            dsc_ref[...] = jnp.zeros_like(dsc_ref)
            dsh_ref[...] = jnp.zeros_like(dsh_ref)
            if ns or n8:
                for rc in copies()[0]:
                    rc.start()

        dh_ = (_nn(dpg_ref[...], wg_ref[...]) + _nn(dpr_ref[...], wr_ref[...])
               + _nn(dlr_ref[...].astype(BF16), wl_ref[...]))
        xf = x_ref[...]
        r = lax.rsqrt(jnp.mean(xf * xf, axis=-1, keepdims=True) + EPS)
        xh = xf * r
        y = xh * g_ref[...]
        dsh_ref[...] += jnp.sum(dh_, axis=0, keepdims=True)
        dsc_ref[...] += jnp.sum(dh_ * y, axis=0, keepdims=True)
        dy = dh_ * (1.0 + sc_ref[...])
        dg_ref[...] += jnp.sum(dy * xh, axis=0, keepdims=True)
        dxh = dy * g_ref[...]
        dx_ref[...] = r * (dxh - xh * jnp.mean(dxh * xh, axis=-1, keepdims=True)) + r_ref[...]

        if ns or n8:
            @pl.when(i == steps - 1)
            def _():
                out, back = copies()
                for rc in back:
                    rc.wait_recv()
                for rc in out:
                    rc.wait_send()

    row = pl.BlockSpec((PROJ_TM, D), lambda i: (i, 0))
    vec = _full((1, D))
    kg, kr = dp_g.shape[1], dp_r.shape[1]
    extra_in = list(send) + ([share8] if n8 else [])
    extra_out = [SDS(a.shape, a.dtype) for a in send] + ([SDS((8,) + share8.shape, share8.dtype)] if n8 else [])
    res = _pcall(
        body, name="proj_bwd", grid=(steps,),
        in_specs=[pl.BlockSpec((PROJ_TM, kg), lambda i: (i, 0)), pl.BlockSpec((PROJ_TM, kr), lambda i: (i, 0)),
                  pl.BlockSpec((PROJ_TM, LRW), lambda i: (i, 0)), _resident((kg, D)), _resident((kr, D)), _resident((LRW, D)),
                  row, row, vec, vec] + [ANY] * len(extra_in),
        out_specs=[row, vec, vec, vec] + [ANY] * len(extra_out),
        out_shape=[SDS((m, D), F32), SDS((1, D), F32), SDS((1, D), F32), SDS((1, D), F32)] + extra_out,
        scratch_shapes=(([pltpu.SemaphoreType.DMA((3 * ns,)), pltpu.SemaphoreType.DMA((3 * ns,))] if ns else [])
                        + ([pltpu.SemaphoreType.DMA((7,)), pltpu.SemaphoreType.DMA((7,))] if n8 else [])),
        compiler_params=_cp(("arbitrary",), VMEM_BIG),
    )(dp_g, dp_r, dlr, wit_g, wit_r, wlrt, x, dx1, ng, scale, *extra_in)
    return tuple(res[:4]), list(res[4:4 + ns]), (res[4 + ns] if n8 else None)


def _prep_bwd(x, dh, dx1, ng, scale, name):
    m = x.shape[0]
    has_res = dx1 is not None

    def body(*refs):
        if has_res:
            x_ref, dh_ref, r_ref, g_ref, sc_ref, dx_ref, dg_ref, dsc_ref, dsh_ref = refs
        else:
            x_ref, dh_ref, g_ref, sc_ref, dx_ref, dg_ref, dsc_ref, dsh_ref = refs
        i = pl.program_id(0)

        @pl.when(i == 0)
        def _():
            dg_ref[...] = jnp.zeros_like(dg_ref)
            dsc_ref[...] = jnp.zeros_like(dsc_ref)
            dsh_ref[...] = jnp.zeros_like(dsh_ref)

        xf = x_ref[...]
        dh_ = dh_ref[...]
        r = lax.rsqrt(jnp.mean(xf * xf, axis=-1, keepdims=True) + EPS)
        xh = xf * r
        y = xh * g_ref[...]
        dsh_ref[...] += jnp.sum(dh_, axis=0, keepdims=True)
        dsc_ref[...] += jnp.sum(dh_ * y, axis=0, keepdims=True)
        dy = dh_ * (1.0 + sc_ref[...])
        dg_ref[...] += jnp.sum(dy * xh, axis=0, keepdims=True)
        dxh = dy * g_ref[...]
        dx = r * (dxh - xh * jnp.mean(dxh * xh, axis=-1, keepdims=True))
        if has_res:
            dx = dx + r_ref[...]
        dx_ref[...] = dx

    tok = min(TOK, m)
    row = pl.BlockSpec((tok, D), lambda i: (i, 0))
    vec = _full((1, D))
    in_specs = [row, row] + ([row] if has_res else []) + [vec, vec]
    args = [x, dh] + ([dx1] if has_res else []) + [ng, scale]
    return _pcall(
        body, name=name, grid=(m // tok,), in_specs=in_specs, out_specs=[row, vec, vec, vec],
        out_shape=[SDS((m, D), F32), SDS((1, D), F32), SDS((1, D), F32), SDS((1, D), F32)],
        compiler_params=_cp(("arbitrary",)),
    )(*args)


COLB = 2048


def _colmix_fwd(vnc, ws23, bs23):
    rows = vnc.shape[2] // COLB

    def body(v_ref, w_ref, b_ref, o_ref):
        o_ref[0] = _nn(w_ref[0], v_ref[0]) + b_ref[0]

    return _pcall(
        body, name="colmix_fwd", grid=(2, rows),
        in_specs=[pl.BlockSpec((1, AC, COLB), lambda g, j: (g, 0, j)), pl.BlockSpec((1, AC, AC), lambda g, j: (g, 0, 0)),
                  pl.BlockSpec((1, AC, 1), lambda g, j: (g, 0, 0))],
        out_specs=pl.BlockSpec((1, AC, COLB), lambda g, j: (g, 0, j)),
        out_shape=SDS(vnc.shape, F32), compiler_params=_cp(("parallel", "parallel")),
    )(vnc, ws23, bs23)


def _colmix_bwd(dsvc, vnc, ws23t):
    rows = vnc.shape[2] // COLB

    def body(d_ref, v_ref, wt_ref, dv_ref, dw_ref, db_ref):
        j = pl.program_id(1)

        @pl.when(j == 0)
        def _():
            dw_ref[...] = jnp.zeros_like(dw_ref)
            db_ref[...] = jnp.zeros_like(db_ref)

        d = d_ref[0]
        d16 = d.astype(BF16)
        dv_ref[0] = _nn(wt_ref[0], d16)
        dw_ref[0] += _nt(d16, v_ref[0])
        db_ref[0] += jnp.sum(d, axis=1, keepdims=True)

    blk = pl.BlockSpec((1, AC, COLB), lambda g, j: (g, 0, j))
    return _pcall(
        body, name="colmix_bwd", grid=(2, rows),
        in_specs=[blk, blk, pl.BlockSpec((1, AC, AC), lambda g, j: (g, 0, 0))],
        out_specs=[blk, pl.BlockSpec((1, AC, AC), lambda g, j: (g, 0, 0)), pl.BlockSpec((1, AC, 1), lambda g, j: (g, 0, 0))],
        out_shape=[SDS(vnc.shape, F32), SDS((2, AC, AC), F32), SDS((2, AC, 1), F32)],
        compiler_params=_cp(("parallel", "arbitrary")),
    )(dsvc, vnc, ws23t)


def _head_norm(o):
    out = []
    for h in range(4):
        oh = o[:, 128 * h:128 * h + 128]
        r = lax.rsqrt(jnp.mean(oh * oh, axis=-1, keepdims=True) + EPS)
        out.append((r, oh * r))
    return out


def _tail_fwd(o_f, o_b, p, vnr, svc, x, tgt, ws01, bs01, gbn, wpa, wpb, wo, gate, gf):
    m = p.shape[0]

    def body(of_ref, ob_ref, zb_ref, ua_ref, za_ref, ga_ref, gb_ref, vnr_ref, svc_ref, x_ref, t_ref, w_ref, b_ref, g_ref,
             wpa_ref, wpb_ref, wo_ref, gate_ref, gf_ref,
             ya_ref, yb_ref, svr_ref, dwo_ref, dx1_ref, dout_ref, loss_ref, dgate_ref, dgf_ref):
        i = pl.program_id(0)

        @pl.when(i == 0)
        def _():
            loss_ref[...] = jnp.zeros_like(loss_ref)
            dgate_ref[...] = jnp.zeros_like(dgate_ref)
            dgf_ref[...] = jnp.zeros_like(dgf_ref)
            dwo_ref[...] = jnp.zeros_like(dwo_ref)

        o = of_ref[...] + ob_ref[...]
        zb = zb_ref[...].astype(F32)
        for h, (r, xh) in enumerate(_head_norm(o)):
            sl = slice(128 * h, 128 * h + 128)
            yb_ref[:, sl] = ((xh * g_ref[:, sl]) * _silu(zb[:, sl])).astype(BF16)
        for j in range(TOK // AC):
            for g in range(2):
                sv = _nn(w_ref[g], vnr_ref[AC * j:AC * j + AC, AC * g:AC * g + AC]) + b_ref[g]
                svr_ref[AC * j:AC * j + AC, AC * g:AC * g + AC] = sv
        sz = _silu(za_ref[...].astype(F32))
        u = ua_ref[...].astype(F32)
        ya_ref[:, 0:256] = ((u[:, 0:256] * svr_ref[...]) * sz[:, 0:256]).astype(BF16)
        ya_ref[:, 256:384] = ((u[:, 256:384] * svc_ref[0]) * sz[:, 256:384]).astype(BF16)
        ya_ref[:, 384:512] = ((u[:, 384:512] * svc_ref[1]) * sz[:, 384:512]).astype(BF16)
        ya = _nn(ya_ref[...], wpa_ref[...])
        yb = _nn(yb_ref[...], wpb_ref[...])
        mg = (jax.nn.sigmoid(ga_ref[...].astype(F32)) * ya + jax.nn.sigmoid(gb_ref[...].astype(F32)) * yb).astype(BF16)
        out_ = _nn(mg, wo_ref[...])
        x1 = x_ref[...] + gate_ref[...] * out_
        r = lax.rsqrt(jnp.mean(x1 * x1, axis=-1, keepdims=True) + EPS)
        xh = x1 * r
        err = xh * gf_ref[...] - t_ref[...]
        loss_ref[...] += 0.5 * jnp.sum(jnp.mean(err * err, axis=-1, keepdims=True), axis=0, keepdims=True)
        dy = err * (1.0 / D)
        dgf_ref[...] += jnp.sum(dy * xh, axis=0, keepdims=True)
        dxh = dy * gf_ref[...]
        dx1 = r * (dxh - xh * jnp.mean(dxh * xh, axis=-1, keepdims=True))
        dx1_ref[...] = dx1
        dout16 = (gate_ref[...] * dx1).astype(BF16)
        dout_ref[...] = dout16
        dgate_ref[...] += jnp.sum(dx1 * out_, axis=0, keepdims=True)
        dwo_ref[...] += _tn(mg, dout16)

    r512 = pl.BlockSpec((TOK, 512), lambda i: (i, 0))
    row = pl.BlockSpec((TOK, D), lambda i: (i, 0))
    vec = _full((1, D))
    return _pcall(
        body, name="tail_fwd", grid=(m // TOK,),
        in_specs=[r512, r512, pl.BlockSpec((TOK, 512), lambda i: (i, 6)), pl.BlockSpec((TOK, 512), lambda i: (i, 7)),
                  pl.BlockSpec((TOK, 512), lambda i: (i, 8)), row, pl.BlockSpec((TOK, D), lambda i: (i, 1)),
                  pl.BlockSpec((TOK, 256), lambda i: (i, 0)), pl.BlockSpec((2, TOK, 128), lambda i: (0, i, 0)), row, row,
                  _full((2, AC, AC)), _full((2, AC, 1)), _full((1, 512)), _resident((512, D)), _resident((512, D)),
                  _resident((D, D)), vec, vec],
        out_specs=[r512, r512, pl.BlockSpec((TOK, 256), lambda i: (i, 0)), _resident((D, D)), row, row, _full((1, 128)), vec, vec],
        out_shape=[SDS((m, 512), BF16), SDS((m, 512), BF16), SDS((m, 256), F32), SDS((D, D), F32), SDS((m, D), F32),
                   SDS((m, D), BF16), SDS((1, 128), F32), SDS((1, D), F32), SDS((1, D), F32)],
        compiler_params=_cp(("arbitrary",), VMEM_BIG),
    )(o_f, o_b, p, p, p, p, p, vnr, svc, x, tgt, ws01, bs01, gbn, wpa, wpb, wo, gate, gf)


DPR = 3072


def _tail_bwd(dout, ya_in, yb_in, p, svr, svc, o_f, o_b, gbn, wo, wpa, wpb):
    m = p.shape[0]

    def body(dout_ref, ya_ref, yb_ref, ga_ref, gb_ref, zb_ref, ua_ref, za_ref, svr_ref, svc_ref, of_ref, ob_ref, g_ref,
             wo_ref, wpa_ref, wpb_ref,
             dwpa_ref, dwpb_ref, dpg_ref, dpr_ref, dsr_ref, dsc_ref, do_ref, dg_ref):
        i = pl.program_id(0)

        @pl.when(i == 0)
        def _():
            dg_ref[...] = jnp.zeros_like(dg_ref)
            dwpa_ref[...] = jnp.zeros_like(dwpa_ref)
            dwpb_ref[...] = jnp.zeros_like(dwpb_ref)

        dm_ = _nt(dout_ref[...], wo_ref[...])
        ya_in, yb_in = ya_ref[...], yb_ref[...]
        ya = _nn(ya_in, wpa_ref[...])
        yb = _nn(yb_in, wpb_ref[...])
        sa = jax.nn.sigmoid(ga_ref[...].astype(F32))
        sb = jax.nn.sigmoid(gb_ref[...].astype(F32))
        dya16 = (dm_ * sa).astype(BF16)
        dyb16 = (dm_ * sb).astype(BF16)
        dwpa_ref[...] += _tn(ya_in, dya16)
        dwpb_ref[...] += _tn(yb_in, dyb16)
        dpg_ref[:, 0:D] = (dm_ * ya * (sa * (1.0 - sa))).astype(BF16)
        dpg_ref[:, D:2 * D] = (dm_ * yb * (sb * (1.0 - sb))).astype(BF16)
        dya = _nt(dya16, wpa_ref[...])
        dyb = _nt(dyb16, wpb_ref[...])

        u = ua_ref[...].astype(F32)
        za = za_ref[...].astype(F32)
        sz, dsz = _silu_and_grad(za)
        sv = jnp.concatenate([svr_ref[...], svc_ref[0], svc_ref[1]], axis=1)
        dpr_ref[:, 512:1024] = (dya * sv * sz).astype(BF16)
        dsv = dya * u * sz
        dsr_ref[...] = dsv[:, 0:256]
        dsc_ref[0] = dsv[:, 256:384]
        dsc_ref[1] = dsv[:, 384:512]
        dpr_ref[:, 1024:1536] = (dya * u * sv * dsz).astype(BF16)

        zb = zb_ref[...].astype(F32)
        o = of_ref[...] + ob_ref[...]
        szb, dszb = _silu_and_grad(zb)
        for h, (r, xh) in enumerate(_head_norm(o)):
            sl = slice(128 * h, 128 * h + 128)
            gh = g_ref[:, sl]
            don = dyb[:, sl] * szb[:, sl]
            dpr_ref[:, sl] = (dyb[:, sl] * (xh * gh) * dszb[:, sl]).astype(BF16)
            dg_ref[:, sl] += jnp.sum(don * xh, axis=0, keepdims=True)
            dxh = don * gh
            do_ref[:, sl] = (r * (dxh - xh * jnp.mean(dxh * xh, axis=-1, keepdims=True))).astype(BF16)

    r512 = pl.BlockSpec((TOK, 512), lambda i: (i, 0))
    row = pl.BlockSpec((TOK, D), lambda i: (i, 0))
    return _pcall(
        body, name="tail_bwd", grid=(m // TOK,),
        in_specs=[row, r512, r512, row, pl.BlockSpec((TOK, D), lambda i: (i, 1)), pl.BlockSpec((TOK, 512), lambda i: (i, 6)),
                  pl.BlockSpec((TOK, 512), lambda i: (i, 7)), pl.BlockSpec((TOK, 512), lambda i: (i, 8)),
                  pl.BlockSpec((TOK, 256), lambda i: (i, 0)), pl.BlockSpec((2, TOK, 128), lambda i: (0, i, 0)), r512, r512,
                  _full((1, 512)), _resident((D, D)), _resident((512, D)), _resident((512, D))],
        out_specs=[_resident((512, D)), _resident((512, D)), pl.BlockSpec((TOK, 2 * D), lambda i: (i, 0)),
                   pl.BlockSpec((TOK, 1536), lambda i: (i, 0)),
                   pl.BlockSpec((TOK, 256), lambda i: (i, 0)), pl.BlockSpec((2, TOK, 128), lambda i: (0, i, 0)), r512, _full((1, 512))],
        out_shape=[SDS((512, D), F32), SDS((512, D), F32), SDS((m, 2 * D), BF16), SDS((m, DPR), BF16), SDS((m, 256), F32),
                   SDS((2, m, 128), F32), SDS((m, 512), BF16), SDS((1, 512), F32)],
        compiler_params=_cp(("arbitrary",), VMEM_BIG),
    )(dout, ya_in, yb_in, p, p, p, p, p, svr, svc, o_f, o_b, gbn, wo, wpa, wpb)


def _ln_bwd(dsr, vnr, dvnc, p, ws01t, ln_g, dp):
    m = p.shape[0]

    def body(dsr_ref, vnr_ref, dvc_ref, va_ref, wt_ref, g_ref, dpi_ref, dp_ref, dw_ref, db_ref, dlg_ref, dlb_ref, dvn_ref):
        i = pl.program_id(0)

        @pl.when(i == 0)
        def _():
            dw_ref[...] = jnp.zeros_like(dw_ref)
            db_ref[...] = jnp.zeros_like(db_ref)
            dlg_ref[...] = jnp.zeros_like(dlg_ref)
            dlb_ref[...] = jnp.zeros_like(dlb_ref)

        for j in range(TOK // AC):
            for g in range(2):
                d = dsr_ref[AC * j:AC * j + AC, AC * g:AC * g + AC]
                d16 = d.astype(BF16)
                dvn_ref[AC * j:AC * j + AC, AC * g:AC * g + AC] = _nn(wt_ref[g], d16)
                dw_ref[g] += _nt(d16, vnr_ref[AC * j:AC * j + AC, AC * g:AC * g + AC])
                db_ref[g] += jnp.sum(d, axis=1, keepdims=True)
        dvn_ref[:, 256:384] = dvc_ref[0]
        dvn_ref[:, 384:512] = dvc_ref[1]
        dvn = dvn_ref[...]
        xf = va_ref[...].astype(F32)
        xc = xf - jnp.mean(xf, axis=-1, keepdims=True)
        rs = lax.rsqrt(jnp.mean(xc * xc, axis=-1, keepdims=True) + EPS)
        xh = xc * rs
        dlg_ref[...] += jnp.sum(dvn * xh, axis=0, keepdims=True)
        dlb_ref[...] += jnp.sum(dvn, axis=0, keepdims=True)
        dxh = dvn * g_ref[...]
        dva = rs * (dxh - jnp.mean(dxh, axis=-1, keepdims=True) - xh * jnp.mean(dxh * xh, axis=-1, keepdims=True))
        dp_ref[...] = dva.astype(BF16)

    return _pcall(
        body, name="ln_bwd", grid=(m // TOK,),
        in_specs=[pl.BlockSpec((TOK, 256), lambda i: (i, 0)), pl.BlockSpec((TOK, 256), lambda i: (i, 0)),
                  pl.BlockSpec((2, TOK, 128), lambda i: (0, i, 0)), pl.BlockSpec((TOK, 512), lambda i: (i, 9)),
                  _full((2, AC, AC)), _full((1, 512)), pl.BlockSpec(memory_space=pl.ANY)],
        out_specs=[pl.BlockSpec((TOK, 512), lambda i: (i, 3)), _full((2, AC, AC)), _full((2, AC, 1)), _full((1, 512)), _full((1, 512))],
        out_shape=[SDS((m, DPR), BF16), SDS((2, AC, AC), F32), SDS((2, AC, 1), F32), SDS((1, 512), F32), SDS((1, 512), F32)],
        scratch_shapes=[pltpu.VMEM((TOK, 512), F32)],
        input_output_aliases={6: 0}, compiler_params=_cp(("arbitrary",)),
    )(dsr, vnr, dvnc, p, ws01t, ln_g, dp)


def _tri_mm(tri, a):
    a1 = a.astype(BF16)
    r1 = a - a1.astype(F32)
    a2 = r1.astype(BF16)
    a3 = (r1 - a2.astype(F32)).astype(BF16)
    n = a.shape[1]
    r = _nn(tri, jnp.concatenate([a1, a2, a3], axis=1))
    return r[:, 0:n] + r[:, n:2 * n] + r[:, 2 * n:3 * n]


def _gla_masks(reverse):
    ri = lax.broadcasted_iota(jnp.int32, (CH, CH), 0)
    ci = lax.broadcasted_iota(jnp.int32, (CH, CH), 1)
    vis = (ci >= ri) if reverse else (ci <= ri)
    vis_t = (ci <= ri) if reverse else (ci >= ri)
    r4 = lax.broadcasted_iota(jnp.int32, (4 * CH, CH), 0) & (CH - 1)
    c4 = lax.broadcasted_iota(jnp.int32, (4 * CH, CH), 1)
    vis4 = (c4 >= r4) if reverse else (c4 <= r4)
    vis4_t = (c4 <= r4) if reverse else (c4 >= r4)
    lane = lax.broadcasted_iota(jnp.int32, (1, 256), 1)
    hm = [(lane >= CH * h) & (lane < CH * h + CH) for h in range(4)]
    return vis, vis_t, vis4, vis4_t, hm


def _stack_heads(x, hm):
    return jnp.concatenate([jnp.where(hm[h], x, 0.0).astype(BF16) for h in range(4)], axis=0)


def _diag_heads(full, hm):
    r = full.shape[0] // 4
    acc = jnp.where(hm[0], full[0:r], 0.0)
    for h in range(1, 4):
        acc = acc + jnp.where(hm[h], full[r * h:r * h + r], 0.0)
    return acc


def _rows_of_heads(x):
    return jnp.concatenate([x[:, 128 * h:128 * h + 128] for h in range(4)], axis=0)


def _lane_vis(reverse, transpose):
    ri = lax.broadcasted_iota(jnp.int32, (CH, 4 * CH), 0)
    ci = lax.broadcasted_iota(jnp.int32, (CH, 4 * CH), 1) & (CH - 1)
    return (ci >= ri) if (reverse != transpose) else (ci <= ri)


def _gla_fwd2(p, qkv_blk, lr, lrws, gbiases, s0s, name):
    m = p.shape[0]
    tb = min(GLA_TB, m)
    nb = m // tb
    nc = tb // CH

    def body(qkv_f, lr_f, qkv_b, lr_b, lrw_f, lrw_b, gb_f, gb_b, s0_f, s0_b,
             o_f, sb_f, sfin_f, o_b, sb_b, sfin_b, st_f, st_b):
        i = pl.program_id(0)

        @pl.when(i == 0)
        def _():
            st_f[...] = s0_f[...]
            st_b[...] = s0_b[...]

        dirs = []
        for reverse, qkv_ref, lr_ref, lrw_ref, gb_ref, o_ref, sb_ref, st_ref in (
                (False, qkv_f, lr_f, lrw_f, gb_f, o_f, sb_f, st_f), (True, qkv_b, lr_b, lrw_b, gb_b, o_b, sb_b, st_b)):
            vis, _, vis4, _, hm = _gla_masks(reverse)
            logits = _nn(lr_ref[...].astype(BF16), lrw_ref[...]) + gb_ref[...]
            dirs.append(dict(reverse=reverse, qkv=qkv_ref, o=o_ref, sb=sb_ref, vis4=vis4, hm=hm,
                             tri=vis.astype(F32).astype(BF16), a=_logsig(logits) * (1.0 / 16.0), st=st_ref[...]))
        for step in range(nc):
            for d in dirs:
                c = nc - 1 - step if d["reverse"] else step
                rows = slice(CH * c, CH * c + CH)
                b = _tri_mm(d["tri"], d["a"][rows])
                bl = b[0:1] if d["reverse"] else b[CH - 1:CH]
                q = d["qkv"][rows, 0:256].astype(F32) * 0.125
                k = d["qkv"][rows, 256:512].astype(F32)
                v16 = d["qkv"][rows, 512:1024].astype(BF16)
                qd = q * jnp.exp(b)
                kd16 = (k * jnp.exp(-b)).astype(BF16)
                kdec16 = (k * jnp.exp(bl - b)).astype(BF16)
                qstack = _stack_heads(qd, d["hm"])
                sc = jnp.where(d["vis4"], _nt(qstack, kd16), 0.0).astype(BF16)
                inter = _nt(qstack, d["st"].astype(BF16))
                for h in range(4):
                    d["o"][rows, 128 * h:128 * h + 128] = (
                        _nn(sc[CH * h:CH * h + CH], v16[:, 128 * h:128 * h + 128]) + inter[CH * h:CH * h + CH])
                d["sb"][c] = d["st"]
                d["st"] = d["st"] * jnp.exp(bl) + _diag_heads(_tn(v16, kdec16), d["hm"])
        st_f[...] = dirs[0]["st"]
        st_b[...] = dirs[1]["st"]

        @pl.when(i == nb - 1)
        def _():
            sfin_f[...] = dirs[0]["st"]
            sfin_b[...] = dirs[1]["st"]

    fw = lambda i: i
    bw = lambda i: nb - 1 - i
    in_specs = []
    for rm in (fw, bw):
        in_specs += [pl.BlockSpec((tb, 1024), lambda i, rm=rm: (rm(i), qkv_blk)), pl.BlockSpec((tb, LRW), lambda i, rm=rm: (rm(i), 0))]
    in_specs += [_full((LRW, 256))] * 2 + [_full((1, 256))] * 2 + [_full((128, 256))] * 2
    out_specs, out_shape = [], []
    for rm in (fw, bw):
        out_specs += [pl.BlockSpec((tb, 512), lambda i, rm=rm: (rm(i), 0)), pl.BlockSpec((nc, 128, 256), lambda i, rm=rm: (rm(i), 0, 0)),
                      _full((128, 256))]
        out_shape += [SDS((m, 512), F32), SDS((m // CH, 128, 256), F32), SDS((128, 256), F32)]
    return _pcall(
        body, name=name, grid=(nb,), in_specs=in_specs, out_specs=out_specs, out_shape=out_shape,
        scratch_shapes=[pltpu.VMEM((128, 256), F32), pltpu.VMEM((128, 256), F32)], compiler_params=_cp(("arbitrary",), VMEM_BIG),
    )(p, lr, p, lr, lrws[0], lrws[1], gbiases[0], gbiases[1], s0s[0], s0s[1])


def _gla_bwd(p, qkv_blk, lr, lrw, lrwt, gbias, sb, dsfin, do, prev, dp, *, reverse, name):
    m = p.shape[0]
    tb = min(GLA_TB, m)
    nb = m // tb
    nc = tb // CH
    rmap = (lambda i: i) if reverse else (lambda i: nb - 1 - i)
    has_prev = prev is not None
    has_dp = dp is not None

    def body(*refs):
        refs = list(refs)
        qkv_ref, lr_ref, lrw_ref, lrwt_ref, gb_ref, sb_ref, dsfin_ref, do_ref = refs[:8]
        refs = refs[8:]
        if has_prev:
            pq_ref, plr_ref = refs[:2]
            refs = refs[2:]
        if has_dp:
            refs = refs[1:]
        dqkv_ref, dlr_ref, dw2_ref, dgb_ref, ds0_ref, dst_ref, dlog_ref = refs
        i = pl.program_id(0)

        @pl.when(i == 0)
        def _():
            dst_ref[...] = dsfin_ref[...]
            dw2_ref[...] = jnp.zeros_like(dw2_ref)
            dgb_ref[...] = jnp.zeros_like(dgb_ref)

        vis, vis_t, vis4, vis4_t, hm = _gla_masks(reverse)
        tri = vis.astype(F32).astype(BF16)
        tri_t = vis_t.astype(F32).astype(BF16)
        lane_vis = _lane_vis(reverse, False)
        lane_vis_t = _lane_vis(reverse, True)
        lr16 = lr_ref[...].astype(BF16)
        logits = _nn(lr16, lrw_ref[...]) + gb_ref[...]
        a_all = _logsig(logits) * (1.0 / 16.0)
        dsig = (1.0 - jax.nn.sigmoid(logits)) * (1.0 / 16.0)
        dst = dst_ref[...]
        for c in (range(nc) if reverse else range(nc - 1, -1, -1)):
            rows = slice(CH * c, CH * c + CH)
            b = _tri_mm(tri, a_all[rows])
            bl = b[0:1] if reverse else b[CH - 1:CH]
            eb = jnp.exp(b)
            enb = jnp.exp(-b)
            ebl = jnp.exp(bl - b)
            el = jnp.exp(bl)
            q = qkv_ref[rows, 0:256].astype(F32) * 0.125
            k = qkv_ref[rows, 256:512].astype(F32)
            v16 = qkv_ref[rows, 512:1024].astype(BF16)
            do16 = do_ref[rows, :].astype(BF16)
            qd = q * eb
            kd = k * enb
            kdec = k * ebl
            st = sb_ref[c]
            st16 = st.astype(BF16)
            dst16 = dst.astype(BF16)
            qd16 = qd.astype(BF16)
            kd16 = kd.astype(BF16)
            qstack = _stack_heads(qd, hm)
            kstack = _stack_heads(kd, hm)
            kdecstack = _stack_heads(kdec, hm)
            pt = jnp.where(vis4_t, _nt(kstack, qd16), 0.0).astype(BF16)
            dvinter = _nt(kdecstack, dst16)
            do_rows = _rows_of_heads(do16)
            v_rows = _rows_of_heads(v16)
            dp_cat = jnp.where(lane_vis, _diag_heads(_nt(do_rows, v_rows), hm), 0.0).astype(BF16)
            dpt_cat = jnp.where(lane_vis_t, _diag_heads(_nt(v_rows, do_rows), hm), 0.0).astype(BF16)
            dqd = _nn(dp_cat, kstack) + _diag_heads(_nn(do_rows, st16), hm)
            dkd = _nn(dpt_cat, qstack)
            dkdec = _diag_heads(_nn(v_rows, dst16), hm)
            for h in range(4):
                rh = slice(CH * h, CH * h + CH)
                dv_h = _nn(pt[rh], do_rows[rh]) + dvinter[rh]
                if has_prev:
                    dv_h = dv_h + pq_ref[rows, 512 + 128 * h:512 + 128 * h + 128]
                dqkv_ref[rows, 512 + 128 * h:512 + 128 * h + 128] = dv_h.astype(dqkv_ref.dtype)
            dq = dqd * eb * 0.125
            dk = dkd * enb + dkdec * ebl
            if has_prev:
                dq = dq + pq_ref[rows, 0:256]
                dk = dk + pq_ref[rows, 256:512]
            dqkv_ref[rows, 0:256] = dq.astype(dqkv_ref.dtype)
            dqkv_ref[rows, 256:512] = dk.astype(dqkv_ref.dtype)
            g_kdec = dkdec * kdec
            db = dqd * qd - dkd * kd - g_kdec
            dbl = jnp.sum(g_kdec, axis=0, keepdims=True) + jnp.sum(st * dst, axis=0, keepdims=True) * el
            da = _tri_mm(tri_t, db) + dbl
            dlog_ref[rows, :] = da * dsig[rows]
            dst = dst * el + _diag_heads(_tn(do16, qd16), hm)
        dst_ref[...] = dst
        dlog = dlog_ref[...]
        dlog16 = dlog.astype(BF16)
        dlr = _nn(dlog16, lrwt_ref[...])
        if has_prev:
            dlr = dlr + plr_ref[...]
        dlr_ref[...] = dlr
        dw2_ref[...] += _tn(lr16, dlog16)
        dgb_ref[...] += jnp.sum(dlog, axis=0, keepdims=True)

        @pl.when(i == nb - 1)
        def _():
            ds0_ref[...] = dst

    in_specs = [pl.BlockSpec((tb,1024), lambda i: (rmap(i), qkv_blk)), pl.BlockSpec((tb,LRW), lambda i: (rmap(i), 0)),
                _full((LRW, 256)), _full((256, LRW)), _full((1, 256)), pl.BlockSpec((nc, 128, 256), lambda i: (rmap(i), 0, 0)),
                _full((128, 256)), pl.BlockSpec((tb,512), lambda i: (rmap(i), 0))]
    args = [p, lr, lrw, lrwt, gbias, sb, dsfin, do]
    if has_prev:
        in_specs += [pl.BlockSpec((tb,1024), lambda i: (rmap(i), 0)), pl.BlockSpec((tb,LRW), lambda i: (rmap(i), 0))]
        args += list(prev)
    aliases = {}
    if has_dp:
        in_specs.append(pl.BlockSpec(memory_space=pl.ANY))
        aliases = {len(args): 0}
        args.append(dp)
        dq_spec = pl.BlockSpec((tb,1024), lambda i: (rmap(i), 2))
        dq_shape = SDS(dp.shape, dp.dtype)
    else:
        dq_spec = pl.BlockSpec((tb,1024), lambda i: (rmap(i), 0))
        dq_shape = SDS((m, 1024), F32)
    return _pcall(
        body, name=name, grid=(nb,), in_specs=in_specs,
        out_specs=[dq_spec, pl.BlockSpec((tb,LRW), lambda i: (rmap(i), 0)), _full((LRW, 256)), _full((1, 256)), _full((128, 256))],
        out_shape=[dq_shape, SDS((m, LRW), F32), SDS((LRW, 256), F32), SDS((1, 256), F32), SDS((128, 256), F32)],
        scratch_shapes=[pltpu.VMEM((128, 256), F32), pltpu.VMEM((tb,256), F32)],
        input_output_aliases=aliases, compiler_params=_cp(("arbitrary",)),
    )(*args)


EARLY_KEYS = ["dmodc", "dscc", "dng_c", "dlng", "dlnb", "dws", "dbs", "dgbn", "dgf", "dw2", "dgb2", "loss", "dgate"]
EARLY_SHAPES = [(1, 2 * D), (D,), (1, D), (1, 512), (1, 512), (1, 4, 128, 128), (1, 4, 128), (1, 512), (D,), (2, 16, 256), (2, 256),
                (128,), (1, D)]
EARLY_SIZE = 2 * D + D + D + 512 + 512 + 4 * 128 * 128 + 512 + 512 + D + 2 * 16 * 256 + 512 + 128 + D
EARLY_ROWS = 648


def _device_step(x, c, ctx, c_ctx, tgt, wm, bm, ng, wit_g, wit_r, wlrt, ln_g, ln_b, ws, bs, w2, gb2, gbn, wpa, wpb, wo, gf,
                 exchange=None, shards=None):
    L = x.shape[0]
    wit_qkv = wit_r[2048:3072]
    ws16 = ws.astype(BF16)
    wst16 = jnp.swapaxes(ws, 1, 2).astype(BF16)
    bscol = bs[:, :, None]
    lrw = [jnp.zeros((LRW, 256), F32).at[16 * r:16 * r + 16].set(w2[r]).astype(BF16) for r in range(2)]
    lrwt = [w.T for w in lrw]
    gbias = [gb2[r:r + 1] for r in range(2)]

    cc = jnp.zeros((8, D), F32).at[0:1].set(c).at[1:2].set(c_ctx)
    mod = _modvec(cc, wm, bm)
    shift, scale, gate = mod[0:1, 0:D], mod[0:1, D:2 * D], mod[0:1, 2 * D:3 * D]
    shift_c, scale_c = mod[1:2, 0:D], mod[1:2, D:2 * D]

    hc = _prep_h(ctx, ng, scale_c, shift_c, "prep_hc")
    pc = _mm(hc, wit_qkv, tm=256, tn=1024, tk=D, out_dtype=F32, name="mm_pc", b_t=True)
    plrc = _mm(hc, wlrt, tm=256, tn=LRW, tk=D, out_dtype=F32, name="mm_plrc", b_t=True)
    zero_s = jnp.zeros((128, 256), F32)
    _, sbc_f, sc_f, _, sbc_b, sc_b = _gla_fwd2(pc, 0, plrc, lrw, gbias, (zero_s, zero_s), "gla_fwd_c")

    h, p, plr, vnr, vnc, late = _proj_fwd(x, ng, scale, shift, wit_g, wit_r, wlrt, ln_g, ln_b,
                                          shards if shards is not None else ())
    if shards is not None:
        me_xy = 2 * lax.axis_index("x") + lax.axis_index("y")
        g_wpa, g_wpb, g_wo = (_own(g, s_, me_xy) for g, s_ in zip(late, shards))
        wpa = jnp.swapaxes(g_wpa, 0, 1).reshape(512, D)
        wpb = jnp.swapaxes(g_wpb, 0, 1).reshape(512, D)
        wo = g_wo.reshape(D, D)
    o_f, sb_f, _, o_b, sb_b, _ = _gla_fwd2(p, 2, plr, lrw, gbias, (sc_f, sc_b), "gla_fwd")
    svc = _colmix_fwd(vnc.reshape(2, AC, L), ws16[2:4], bscol[2:4]).reshape(2, L, 128)
    ya_in, yb_in, svr, dwo, dx1, dout, loss, dgate, dgf = _tail_fwd(
        o_f, o_b, p, vnr, svc, x, tgt, ws16[0:2], bscol[0:2], gbn, wpa, wpb, wo, gate, gf)

    dwpa, dwpb, dp_g, dp, dsr, dsc, do, dgbn = _tail_bwd(dout, ya_in, yb_in, p, svr, svc, o_f, o_b, gbn, wo, wpa, wpb)
    dvnc, dws23, dbs23 = _colmix_bwd(dsc.reshape(2, AC, L), vnc.reshape(2, AC, L), wst16[2:4])
    dp, dws01, dbs01, dlng, dlnb = _ln_bwd(dsr, vnr, dvnc.reshape(2, L, 128), p, wst16[0:2], ln_g, dp)
    zero_ds = jnp.zeros((128, 256), F32)
    dqkv_f, dlr_f, dw2_f, dgb_f, ds0_f = _gla_bwd(p, 2, plr, lrw[0], lrwt[0], gbias[0], sb_f, zero_ds, do, None, None,
                                                  reverse=False, name="gla_bwd_f")
    dp, dlr, dw2_b, dgb_b, ds0_b = _gla_bwd(p, 2, plr, lrw[1], lrwt[1], gbias[1], sb_b, zero_ds, do, (dqkv_f, dlr_f), dp,
                                            reverse=True, name="gla_bwd_b")
    zero_do = jnp.zeros((ctx.shape[0], 512), BF16)
    dqkvc_f, dlrc_f, dw2c_f, dgbc_f, _ = _gla_bwd(pc, 0, plrc, lrw[0], lrwt[0], gbias[0], sbc_f, ds0_f, zero_do, None, None,
                                                  reverse=False, name="gla_bwd_cf")
    dqkvc, dlrc, dw2c_b, dgbc_b, _ = _gla_bwd(pc, 0, plrc, lrw[1], lrwt[1], gbias[1], sbc_b, ds0_b, zero_do,
                                              (dqkvc_f, dlrc_f), None, reverse=True, name="gla_bwd_cb")
    dhc = _mm(dqkvc, wit_qkv, tm=256, tn=D, tk=1024, out_dtype=F32, name="mm_dhc")
    dhc = _mm(dlrc, wlrt, tm=256, tn=D, tk=LRW, out_dtype=F32, name="mm_dhc_lr", acc=dhc)
    _, dng_c, dscale_c, dshift_c = _prep_bwd(ctx, dhc, None, ng, scale_c, "prep_bwd_c")

    dwit_g = _mm_tn(dp_g, h, ta=1024, tn=D, tk=2048, name="mm_dwi_g")
    dwit_r = _mm_tn(dp, h, ta=1024, tn=D, tk=2048, name="mm_dwi_r")
    dwit_qkv = _mm_tn(dqkvc, hc, ta=1024, tn=D, tk=256, name="mm_dwi_c", acc=dwit_r[2048:3072])
    dwlrt = _mm_tn(dlr, h, ta=LRW, tn=D, tk=2048, name="mm_dwlr")
    dwlrt = _mm_tn(dlrc, hc, ta=LRW, tn=D, tk=256, name="mm_dwlr_c", acc=dwlrt)
    big = dict(dwit_g=dwit_g, dwit_r=dwit_r, dwit_qkv=dwit_qkv, dwlrt=dwlrt, dwpa=dwpa, dwpb=dwpb, dwo=dwo)

    dmodc = jnp.concatenate([dshift_c, dscale_c], axis=1)
    dscc = _dcctx(jnp.zeros((8, 2 * D), F32).at[0:1].set(dmodc), wm)[0:1]
    dw2p = dw2_f + dw2c_f, dw2_b + dw2c_b
    small = dict(
        dmodc=dmodc, dscc=dscc, dng_c=dng_c, dlng=dlng, dlnb=dlnb, dws=jnp.concatenate([dws01, dws23], axis=0),
        dbs=jnp.concatenate([dbs01, dbs23], axis=0)[:, :, 0], dgbn=dgbn, dgf=dgf,
        dw2=jnp.stack([dw2p[0][0:16], dw2p[1][16:32]]), dgb2=jnp.concatenate([dgb_f + dgbc_f, dgb_b + dgbc_b], axis=0),
        loss=loss[0, 0], dgate=dgate)

    send = exchange(big) if exchange is not None else ()
    early = _pack([small[k] for k in EARLY_KEYS[:-2]] + [jnp.broadcast_to(small["loss"], (128,)), small["dgate"]], EARLY_ROWS) \
        if exchange is not None else None
    (dx, dng, dscale, dshift), got, early_all = _proj_bwd(dp_g, dp, dlr, wit_g, wit_r, wlrt, x, dx1, ng, scale, send, early)
    return dict(dx=dx, got=got, early=early, early_all=early_all, dshift=dshift, dscale=dscale, dng_lat=dng, **big, **small)


ANY = pl.BlockSpec(memory_space=pl.ANY)


def _coords():
    return lax.axis_index("x"), lax.axis_index("y"), lax.axis_index("c")


def _flip(v, bit):
    return 1 - v if bit else v


def _remote(src, dst, send_sem, recv_sem, dev):
    return pltpu.make_async_remote_copy(src_ref=src, dst_ref=dst, send_sem=send_sem, recv_sem=recv_sem,
                                        device_id=dev, device_id_type=MESH)


def _own(out, block, idx):
    return lax.dynamic_update_slice_in_dim(out, block[None], idx, axis=0)


def _half_idx(shape, axis, which, lead=()):
    idx = [pl.ds(0, d) for d in shape]
    h = shape[axis] // 2
    idx[axis] = pl.ds(which * h, h)
    return tuple(lead) + tuple(idx)


def _gather_weights(split, whole, name):
    ns, nw = len(split), len(whole)
    n = ns + nw
    arrs = [a for a, _ in split] + list(whole)

    def body(*refs):
        ins, outs = refs[:n], refs[n:2 * n]
        a_send, a_recv, b_send, b_recv = refs[2 * n:]
        x, y, c = _coords()
        me = 2 * x + y
        sib = (x, y, 1 - c)
        peers = [(1 - x, y), (x, 1 - y), (1 - x, 1 - y)]

        def half(k, slot, which):
            return outs[k].at[_half_idx(arrs[k].shape, split[k][1], which, lead=(slot,))]

        sends = []
        for k in range(n):
            for j, (px, py) in enumerate(peers):
                if k < ns:
                    rc = _remote(ins[k].at[_half_idx(arrs[k].shape, split[k][1], c)], half(k, me, c), a_send.at[3 * k + j],
                                 a_recv.at[3 * k + j], (px, py, c))
                else:
                    rc = _remote(ins[k], outs[k].at[me], a_send.at[3 * k + j], a_recv.at[3 * k + j], (px, py, c))
                rc.start()
                sends.append(rc)
        for k in range(ns):
            for j, (px, py) in enumerate(peers):
                landed = half(k, 2 * px + py, c)
                _remote(landed, landed, a_send.at[3 * k + j], a_recv.at[3 * k + j], (px, py, c)).wait_recv()
                fw = _remote(landed, landed, b_send.at[3 * k + j], b_recv.at[3 * k + j], sib)
                fw.start()
                sends.append(fw)
        for k in range(ns, n):
            for j, (px, py) in enumerate(peers):
                landed = outs[k].at[2 * px + py]
                _remote(landed, landed, a_send.at[3 * k + j], a_recv.at[3 * k + j], (px, py, c)).wait_recv()
        for k in range(ns):
            for j, (px, py) in enumerate(peers):
                passed = half(k, 2 * px + py, 1 - c)
                _remote(passed, passed, b_send.at[3 * k + j], b_recv.at[3 * k + j], sib).wait_recv()
        for rc in sends:
            rc.wait_send()

    outs = _pcall(
        body, name=name, in_specs=[ANY] * n, out_specs=[ANY] * n,
        out_shape=[SDS((4,) + a.shape, a.dtype) for a in arrs],
        scratch_shapes=[pltpu.SemaphoreType.DMA((3 * n,)), pltpu.SemaphoreType.DMA((3 * n,)), pltpu.SemaphoreType.DMA((3 * ns,)),
                        pltpu.SemaphoreType.DMA((3 * ns,))],
    )(*arrs)
    me_xy = 2 * lax.axis_index("x") + lax.axis_index("y")
    return [_own(o, a, me_xy) for o, a in zip(outs, arrs)]


def _gather_all(a, swap, name):
    masks = [(mx, my, mc) for mx in range(2) for my in range(2) for mc in range(2)][1:]
    n = len(swap)

    def body(*refs):
        in_ref, sw_in = refs[0], refs[1:1 + n]
        out_ref, sw_out = refs[1 + n], refs[2 + n:2 + 2 * n]
        send_sems, recv_sems = refs[2 + 2 * n:]
        x, y, c = _coords()
        me = 4 * x + 2 * y + c
        sends = []
        for j, (mx, my, mc) in enumerate(masks):
            rc = _remote(in_ref, out_ref.at[me], send_sems.at[j], recv_sems.at[j], (_flip(x, mx), _flip(y, my), _flip(c, mc)))
            rc.start()
            sends.append(rc)
        for k in range(n):
            rc = _remote(sw_in[k], sw_out[k], send_sems.at[7 + k], recv_sems.at[7 + k], (x, y, 1 - c))
            rc.start()
            sends.append(rc)
        for j, (mx, my, mc) in enumerate(masks):
            px, py, pc = _flip(x, mx), _flip(y, my), _flip(c, mc)
            landed = out_ref.at[4 * px + 2 * py + pc]
            _remote(landed, landed, send_sems.at[j], recv_sems.at[j], (px, py, pc)).wait_recv()
        for k in range(n):
            _remote(sw_out[k], sw_out[k], send_sems.at[7 + k], recv_sems.at[7 + k], (x, y, 1 - c)).wait_recv()
        for rc in sends:
            rc.wait_send()

    res = _pcall(
        body, name=name, in_specs=[ANY] * (1 + n), out_specs=[ANY] * (1 + n),
        out_shape=[SDS((8,) + a.shape, a.dtype)] + [SDS(s_.shape, s_.dtype) for s_ in swap],
        scratch_shapes=[pltpu.SemaphoreType.DMA((7 + n,)), pltpu.SemaphoreType.DMA((7 + n,))],
    )(a, *swap)
    return _own(res[0], a, 4 * lax.axis_index("x") + 2 * lax.axis_index("y") + lax.axis_index("c")), list(res[1:])


def _half_shape(shape, axis):
    return tuple(d // 2 if i == axis else d for i, d in enumerate(shape))


def _swap_half_c(arrs, axes, name):
    n = len(arrs)

    def body(*refs):
        ins, outs = refs[:n], refs[n:2 * n]
        send_sems, recv_sems = refs[2 * n:]
        x, y, c = _coords()
        sends = []
        for k in range(n):
            rc = _remote(ins[k].at[_half_idx(arrs[k].shape, axes[k], 1 - c)], outs[k], send_sems.at[k], recv_sems.at[k],
                         (x, y, 1 - c))
            rc.start()
            sends.append(rc)
        for rc in sends:
            rc.wait()

    return _pcall(
        body, name=name, in_specs=[ANY] * n, out_specs=[ANY] * n,
        out_shape=[SDS(_half_shape(a.shape, ax), a.dtype) for a, ax in zip(arrs, axes)],
        scratch_shapes=[pltpu.SemaphoreType.DMA((n,)), pltpu.SemaphoreType.DMA((n,))],
    )(*arrs)


def _pair_sum(a, got, cidx, axis, name):
    _, r, cdim = a.shape
    hshape = _half_shape(a.shape, axis)

    def body(c_ref, a_ref, g_ref, o_ref):
        o_ref[...] = (a_ref[...] + g_ref[...]).astype(BF16)

    if axis == 1:
        tr = min(r // 2, 256)
        nj = (r // 2) // tr
        blk = pl.BlockSpec((1, tr, cdim), lambda s, j, c: (s, j, 0))
        a_spec = pl.BlockSpec((1, tr, cdim), lambda s, j, c: (s, c[0] * nj + j, 0))
    else:
        nj, hw = 1, cdim // 2
        blk = pl.BlockSpec((1, r, hw), lambda s, j, c: (s, 0, 0))
        a_spec = pl.BlockSpec((1, r, hw), lambda s, j, c: (s, 0, c[0]))
    return _pcall(
        body, name=name, out_shape=SDS(hshape, BF16),
        grid_spec=pltpu.PrefetchScalarGridSpec(num_scalar_prefetch=1, grid=(4, nj), in_specs=[a_spec, blk], out_specs=blk),
        compiler_params=_cp(("parallel", "parallel"), VMEM_BIG),
    )(cidx, a, got)


def _sum_chips(parts, name):
    _, h, cdim = parts.shape

    def body(p_ref, o_ref):
        acc = p_ref[0].astype(F32)
        for k in range(1, 4):
            acc = acc + p_ref[k].astype(F32)
        o_ref[...] = acc

    if h % 256 == 0 or h in (128,):
        tr = min(h, 256)
        grid, in_spec, out_spec = (h // tr,), pl.BlockSpec((4, tr, cdim), lambda i: (0, i, 0)), pl.BlockSpec((tr, cdim), lambda i: (i, 0))
    else:
        lw = 256
        grid, in_spec, out_spec = (cdim // lw,), pl.BlockSpec((4, h, lw), lambda i: (0, 0, i)), pl.BlockSpec((h, lw), lambda i: (0, i))
    return _pcall(
        body, name=name, grid=grid, in_specs=[in_spec], out_specs=out_spec, out_shape=SDS((h, cdim), F32),
        compiler_params=_cp(("parallel",), VMEM_BIG),
    )(parts)


def _sum_slots(a, name, rows):
    s, n, _ = a.shape

    def body(a_ref, o_ref):
        acc = a_ref[0]
        for k in range(1, s):
            acc = acc + a_ref[k]
        o_ref[...] = acc

    return _pcall(
        body, name=name, grid=(n // rows,), in_specs=[pl.BlockSpec((s, rows, 128), lambda i: (0, i, 0))],
        out_specs=pl.BlockSpec((rows, 128), lambda i: (i, 0)), out_shape=SDS((n, 128), F32),
        compiler_params=_cp(("parallel",)),
    )(a)


def _adam_math(w, g, m, v):
    nm = ADAM_B1 * m + (1.0 - ADAM_B1) * g
    nv = ADAM_B2 * v + (1.0 - ADAM_B2) * (g * g)
    m_hat = nm / (1.0 - ADAM_B1 ** ADAM_STEP)
    v_hat = nv / (1.0 - ADAM_B2 ** ADAM_STEP)
    return -ADAM_LR * (m_hat / (jnp.sqrt(v_hat) + ADAM_EPS) + ADAM_WD * w), nm, nv


def _adamw(w, g, m, v, name, rows):
    r, cdim = w.shape

    def body(w_ref, g_ref, m_ref, v_ref, d_ref, nm_ref, nv_ref):
        d_ref[...], nm_ref[...], nv_ref[...] = _adam_math(w_ref[...], g_ref[...], m_ref[...], v_ref[...])

    blk = pl.BlockSpec((rows, cdim), lambda i: (i, 0))
    return _pcall(
        body, name=name, grid=(r // rows,), in_specs=[blk] * 4, out_specs=[blk] * 3,
        out_shape=[SDS(w.shape, F32)] * 3, compiler_params=_cp(("parallel",)),
    )(w, g, m, v)


def _adamw_joined(w, mine, other, m, v, cidx, axis, name, rows):
    r, cdim = w.shape
    if axis == 0:
        rows = r

    def body(c_ref, w_ref, a_ref, b_ref, m_ref, v_ref, g_ref, d_ref, nm_ref, nv_ref):
        a, b = a_ref[...], b_ref[...]
        g = jnp.where(c_ref[0] == 0, jnp.concatenate([a, b], axis=axis), jnp.concatenate([b, a], axis=axis))
        g_ref[...] = g
        d_ref[...], nm_ref[...], nv_ref[...] = _adam_math(w_ref[...], g, m_ref[...], v_ref[...])

    blk = pl.BlockSpec((rows, cdim), lambda i, c: (i, 0))
    hshape = (rows // 2, cdim) if axis == 0 else (rows, cdim // 2)
    hblk = pl.BlockSpec(hshape, lambda i, c: (i, 0))
    return _pcall(
        body, name=name, out_shape=[SDS(w.shape, F32)] * 4,
        grid_spec=pltpu.PrefetchScalarGridSpec(num_scalar_prefetch=1, grid=(r // rows,), in_specs=[blk, hblk, hblk, blk, blk],
                                               out_specs=[blk] * 4),
        compiler_params=_cp(("parallel",)),
    )(cidx, w, mine, other, m, v)


def _adamw_many(ws, gs, ms, vs, name):
    n = len(ws)

    def body(*refs):
        outs = refs[4 * n:]
        for k in range(n):
            d, nm, nv = _adam_math(refs[k][...], refs[n + k][...], refs[2 * n + k][...], refs[3 * n + k][...])
            outs[k][...] = d
            outs[n + k][...] = nm
            outs[2 * n + k][...] = nv

    res = _pcall(body, name=name, out_shape=[SDS(w.shape, F32) for w in ws] * 3)(*ws, *gs, *ms, *vs)
    return res[:n], res[n:2 * n], res[2 * n:]


def _pack(pieces, rows):
    flat = jnp.concatenate([p.reshape(-1) for p in pieces])
    return jnp.pad(flat, (0, rows * 128 - flat.shape[0])).reshape(rows, 128)


def _unpack(buf, shapes):
    flat = buf.reshape(-1)
    out, off = [], 0
    for shp in shapes:
        size = 1
        for s in shp:
            size *= s
        out.append(flat[off:off + size].reshape(shp))
        off += size
    return out


LATE_ROWS = 32


def kernel(x, c, ctx, c_ctx, w_mod, b_mod, norm_g, w_in, a_ln_g, a_ln_b, a_ws, a_bs, b_gate_w2, b_gate_b, b_norm_g, w_proj_a, w_proj_b, w_out, final_norm_g, loss_target, m_c_ctx, m_w_mod, m_b_mod, m_norm_g, m_w_in, m_a_ln_g, m_a_ln_b, m_a_ws, m_a_bs, m_b_gate_w2, m_b_gate_b, m_b_norm_g, m_w_proj_a, m_w_proj_b, m_w_out, m_final_norm_g, v_c_ctx, v_w_mod, v_b_mod, v_norm_g, v_w_in, v_a_ln_g, v_a_ln_b, v_a_ws, v_a_bs, v_b_gate_w2, v_b_gate_b, v_b_norm_g, v_w_proj_a, v_w_proj_b, v_w_out, v_final_norm_g):
    xi, yi, ci = _coords()
    me_xy = 2 * xi + yi

    gate_pack = _pack([b_gate_w2[0], b_gate_b[0]], 24)
    w_in_t, m_w_in_t, v_w_in_t = (jnp.swapaxes(a[0], 0, 1) for a in (w_in, m_w_in, v_w_in))
    g_wit, g_wm, g_gate = _gather_weights([(w_in_t.astype(BF16), 1), (w_mod[0].astype(BF16), 0)], [gate_pack], "gather_weights")
    late_shards = (w_proj_a[0].astype(BF16), w_proj_b[0].astype(BF16), w_out[0].astype(BF16))
    wit_u = g_wit.reshape(4 * 1288, D)
    wit_g = wit_u[3104:5152]
    wit_r = jnp.concatenate([wit_u[1056:1568], wit_u[1568:2080], wit_u[2592:3104], wit_u[2080:2592], wit_u[0:1024]], axis=0)
    wlrt = jnp.pad(wit_u[1024:1056], ((0, LRW - 32), (0, 0)))
    wm = jnp.swapaxes(g_wm, 0, 1).reshape(D, 3 * D)
    gflat = g_gate.reshape(4, 24 * 128)
    w2 = jnp.swapaxes(gflat[:, 0:2048].reshape(4, 2, 16, 64), 0, 2)
    w2 = jnp.swapaxes(w2, 0, 1).reshape(2, 16, 256)
    gb2 = jnp.swapaxes(gflat[:, 2048:2176].reshape(4, 2, 64), 0, 1).reshape(2, 256)

    tags = ["wi", "wpa", "wpb", "wo"]
    half_axes = [2, 1, 1, 1]
    sent = []

    def exchange(g):
        dwr = g["dwit_r"]
        dwit_u = jnp.concatenate([g["dwit_qkv"], g["dwlrt"][0:32], dwr[0:512], dwr[512:1024], dwr[1536:2048], dwr[1024:1536],
                                  g["dwit_g"]], axis=0)
        big = [dwit_u.reshape(4, 1288, D), jnp.swapaxes(g["dwpa"].reshape(512, 4, 256), 0, 1),
               jnp.swapaxes(g["dwpb"].reshape(512, 4, 256), 0, 1), g["dwo"].reshape(4, 256, D)]
        other = _swap_half_c(big, half_axes, "swap_half_in")
        cidx = jnp.reshape(ci, (1,)).astype(jnp.int32)
        sent.extend(_pair_sum(a, o, cidx, ax, "sum_pair_" + t) for a, o, ax, t in zip(big, other, half_axes, tags))
        return sent

    r = _device_step(x[0], c, ctx[0], c_ctx[None], loss_target[0], wm, b_mod, norm_g, wit_g, wit_r, wlrt, a_ln_g, a_ln_b,
                     a_ws[0], a_bs[0], w2, gb2, b_norm_g, None, None, None, final_norm_g[None], exchange, late_shards)

    parts = [_own(g, lax.dynamic_index_in_dim(s_, me_xy, axis=0, keepdims=False), me_xy) for g, s_ in zip(r["got"], sent)]
    halves = [_sum_chips(p_, "sum_chips_" + t) for p_, t in zip(parts, tags)]

    me8 = 4 * xi + 2 * yi + ci
    early_all = _own(r["early_all"], r["early"], me8)
    late = _pack([r["dshift"], r["dscale"], r["dng_lat"], c], LATE_ROWS)
    late_all, others = _gather_all(late, halves, "gather_small")
    s_early = _sum_slots(early_all, "sum_early", EARLY_ROWS // 3)
    s_late = _sum_slots(late_all, "sum_late", LATE_ROWS)
    (s_dmodc, s_dscc, s_dng_c, s_dlng, s_dlnb, s_dws, s_dbs, s_dgbn, s_dgf, s_dw2, s_dgb2, s_loss, s_dgate) = _unpack(
        s_early, EARLY_SHAPES)
    s_dshift, s_dscale, s_dng_lat, _ = _unpack(s_late, [(1, D)] * 4)
    s_dng = s_dng_lat + s_dng_c
    s_dmod = jnp.concatenate([s_dshift, s_dscale, s_dgate], axis=1)
    loss = s_loss[0]
    s_dmodc_p = jnp.pad(s_dmodc, ((0, 0), (0, D)))
    g_b_mod = s_dmod + s_dmodc_p
    sg = jax.nn.sigmoid(c_ctx)
    g_c_ctx = s_dscc * (sg * (1.0 + c_ctx * (1.0 - sg)))
    g_w2 = lax.dynamic_slice_in_dim(s_dw2, 64 * me_xy, 64, axis=2)[None]
    g_gb2 = lax.dynamic_slice_in_dim(s_dgb2, 64 * me_xy, 64, axis=1)[None]

    flat_l = late_all.reshape(8, LATE_ROWS * 128)
    dgate_all = early_all.reshape(8, EARLY_ROWS * 128)[:, EARLY_SIZE - D:EARLY_SIZE]
    dmod_all = jnp.concatenate([flat_l[:, 0:2 * D], dgate_all], axis=1)
    c_all = flat_l[:, 3 * D:4 * D]
    lhs = jnp.concatenate([_silu(c_all), _silu(c_ctx)[None], jnp.zeros((7, D), F32)], axis=0)
    rhs = jnp.concatenate([dmod_all, s_dmodc_p, jnp.zeros((7, 3 * D), F32)], axis=0)
    rhs = lax.dynamic_slice_in_dim(rhs, 768 * me_xy, 768, axis=1)
    g_w_mod = _mm(lhs.T.astype(BF16), rhs.astype(BF16), tm=D, tn=768, tk=16, out_dtype=F32, name="mm_dwm")

    cidx = jnp.reshape(ci, (1,)).astype(jnp.int32)
    g_w_in_t, d_w_in_t, nm_w_in_t, nv_w_in_t = _adamw_joined(w_in_t, halves[0], others[0], m_w_in_t, v_w_in_t, cidx, 1,
                                                             "adamw_w_in", 184)
    g_w_in, d_w_in, nm_w_in, nv_w_in = (jnp.swapaxes(a, 0, 1) for a in (g_w_in_t, d_w_in_t, nm_w_in_t, nv_w_in_t))
    g_wpa, d_wpa, nm_wpa, nv_wpa = _adamw_joined(w_proj_a[0], halves[1], others[1], m_w_proj_a[0], v_w_proj_a[0], cidx, 0,
                                                 "adamw_wpa", 0)
    g_wpb, d_wpb, nm_wpb, nv_wpb = _adamw_joined(w_proj_b[0], halves[2], others[2], m_w_proj_b[0], v_w_proj_b[0], cidx, 0,
                                                 "adamw_wpb", 0)
    g_wo, d_wo, nm_wo, nv_wo = _adamw_joined(w_out[0], halves[3], others[3], m_w_out[0], v_w_out[0], cidx, 0, "adamw_wo", 0)
    d_w_mod, nm_w_mod, nv_w_mod = _adamw(w_mod[0], g_w_mod, m_w_mod[0], v_w_mod[0], "adamw_w_mod", 256)

    names = ["c_ctx", "b_mod", "norm_g", "a_ln_g", "a_ln_b", "a_ws", "a_bs", "b_gate_w2", "b_gate_b", "b_norm_g", "final_norm_g"]
    ws_ = [c_ctx, b_mod, norm_g, a_ln_g, a_ln_b, a_ws, a_bs, b_gate_w2, b_gate_b, b_norm_g, final_norm_g]
    gs_ = [g_c_ctx, g_b_mod, s_dng, s_dlng, s_dlnb, s_dws, s_dbs, g_w2, g_gb2, s_dgbn, s_dgf]
    ms_ = [m_c_ctx, m_b_mod, m_norm_g, m_a_ln_g, m_a_ln_b, m_a_ws, m_a_bs, m_b_gate_w2, m_b_gate_b, m_b_norm_g, m_final_norm_g]
    vs_ = [v_c_ctx, v_b_mod, v_norm_g, v_a_ln_g, v_a_ln_b, v_a_ws, v_a_bs, v_b_gate_w2, v_b_gate_b, v_b_norm_g, v_final_norm_g]
    shapes = [w.shape for w in ws_]
    flat2 = [(1, 1024), (1, 3072), (1, 1024), (1, 512), (1, 512), (512, 128), (4, 128), (32, 64), (2, 64), (1, 512), (1, 1024)]
    as2d = lambda arrs: [a.reshape(s) for a, s in zip(arrs, flat2)]
    d_s, nm_s, nv_s = _adamw_many(as2d(ws_), as2d(gs_), as2d(ms_), as2d(vs_), "adamw_small")
    d_small = {n: a.reshape(s) for n, a, s in zip(names, d_s, shapes)}
    nm_small = {n: a.reshape(s) for n, a, s in zip(names, nm_s, shapes)}
    nv_small = {n: a.reshape(s) for n, a, s in zip(names, nv_s, shapes)}
    g_small = {n: g.reshape(s) for n, g, s in zip(names, gs_, shapes)}

    order = ["c_ctx", "w_mod", "b_mod", "norm_g", "w_in", "a_ln_g", "a_ln_b", "a_ws", "a_bs", "b_gate_w2", "b_gate_b", "b_norm_g",
             "w_proj_a", "w_proj_b", "w_out", "final_norm_g"]
    big_g = dict(w_mod=g_w_mod[None], w_in=g_w_in[None], w_proj_a=g_wpa[None], w_proj_b=g_wpb[None], w_out=g_wo[None])
    big_d = dict(w_mod=d_w_mod[None], w_in=d_w_in[None], w_proj_a=d_wpa[None], w_proj_b=d_wpb[None], w_out=d_wo[None])
    big_m = dict(w_mod=nm_w_mod[None], w_in=nm_w_in[None], w_proj_a=nm_wpa[None], w_proj_b=nm_wpb[None], w_out=nm_wo[None])
    big_v = dict(w_mod=nv_w_mod[None], w_in=nv_w_in[None], w_proj_a=nv_wpa[None], w_proj_b=nv_wpb[None], w_out=nv_wo[None])
    grads = [big_g[n] if n in big_g else g_small[n] for n in order]
    deltas = [big_d[n] if n in big_d else d_small[n] for n in order]
    new_m = [big_m[n] if n in big_m else nm_small[n] for n in order]
    new_v = [big_v[n] if n in big_v else nv_small[n] for n in order]
    return (loss, r["dx"][None], *grads, *deltas, *new_m, *new_v)
```

```python
import jax
import jax.numpy as jnp
from jax import lax
from jax.experimental import pallas as pl
from jax.experimental.pallas import tpu as pltpu

F32 = jnp.float32
BF16 = jnp.bfloat16
SDS = jax.ShapeDtypeStruct

D = 1024
NP = 5120
LRW = 128
CH = 64
AC = 128
EPS = 1e-6
TOK = 512
GLA_TB = 1024
VMEM_BIG = 48 * 1024 * 1024

ADAM_LR, ADAM_B1, ADAM_B2, ADAM_EPS, ADAM_WD, ADAM_STEP = 0.001, 0.9, 0.999, 1e-08, 0.01, 10

_pcall = pl.pallas_call
MESH = pl.DeviceIdType.MESH


def _cp(sem=None, vmem=None):
    kw = {}
    if sem is not None:
        kw["dimension_semantics"] = sem
    if vmem is not None:
        kw["vmem_limit_bytes"] = vmem
    return pltpu.CompilerParams(**kw)


def _silu(x):
    return x * jax.nn.sigmoid(x)


def _silu_and_grad(x):
    s = jax.nn.sigmoid(x)
    return x * s, s * (1.0 + x * (1.0 - s))


def _logsig(x):
    return jnp.minimum(x, 0.0) - jnp.log1p(jnp.exp(-jnp.abs(x)))


def _nt(a, b):
    return lax.dot_general(a, b, (((1,), (1,)), ((), ())), preferred_element_type=F32)


def _tn(a, b):
    return lax.dot_general(a, b, (((0,), (0,)), ((), ())), preferred_element_type=F32)


def _nn(a, b):
    return jnp.dot(a, b, preferred_element_type=F32)


def _full(shape):
    return pl.BlockSpec(shape, lambda *_: (0,) * len(shape))


def _mm(a, b, *, tm, tn, tk, out_dtype, name, acc=None, b_t=False):
    m, k = a.shape
    n, k2 = (b.shape if b_t else b.shape[::-1])
    assert k == k2 == tk and m % tm == 0 and n % tn == 0, (a.shape, b.shape, tm, tn, tk)
    has_acc = acc is not None

    def body(*refs):
        if has_acc:
            a_ref, b_ref, c_ref, o_ref = refs
        else:
            a_ref, b_ref, o_ref = refs
        part = (_nt if b_t else _nn)(a_ref[...].astype(BF16), b_ref[...].astype(BF16))
        o_ref[...] = ((c_ref[...] + part) if has_acc else part).astype(out_dtype)

    b_spec = pl.BlockSpec((tn, tk), lambda i, j: (j, 0)) if b_t else pl.BlockSpec((tk, tn), lambda i, j: (0, j))
    in_specs = [pl.BlockSpec((tm, tk), lambda i, j: (i, 0)), b_spec]
    args = [a, b]
    if has_acc:
        in_specs.append(pl.BlockSpec((tm, tn), lambda i, j: (i, j)))
        args.append(acc)
    return _pcall(
        body, name=name, grid=(m // tm, n // tn), in_specs=in_specs, out_specs=pl.BlockSpec((tm, tn), lambda i, j: (i, j)),
        out_shape=SDS((m, n), out_dtype), compiler_params=_cp(("parallel", "parallel"), VMEM_BIG),
    )(*args)


def _mm_tn(a, b, *, ta, tn, tk, name, acc=None, out_dtype=F32):
    m, ka = a.shape
    m2, n = b.shape
    assert m == m2 and ka % ta == 0 and n % tn == 0 and m % tk == 0, (a.shape, b.shape, ta, tn, tk)
    nk = m // tk
    has_acc = acc is not None

    def body(*refs):
        if has_acc:
            a_ref, b_ref, c_ref, o_ref, acc_ref = refs
        else:
            a_ref, b_ref, o_ref, acc_ref = refs
        kk = pl.program_id(2)
        part = _tn(a_ref[...].astype(BF16), b_ref[...].astype(BF16))

        @pl.when(kk == 0)
        def _():
            if has_acc:
                acc_ref[...] = c_ref[...].astype(F32) + part
            else:
                acc_ref[...] = part

        @pl.when(kk > 0)
        def _():
            acc_ref[...] += part

        @pl.when(kk == nk - 1)
        def _():
            o_ref[...] = acc_ref[...].astype(out_dtype)

    in_specs = [pl.BlockSpec((tk, ta), lambda i, j, kk: (kk, i)), pl.BlockSpec((tk, tn), lambda i, j, kk: (kk, j))]
    args = [a, b]
    if has_acc:
        in_specs.append(pl.BlockSpec((ta, tn), lambda i, j, kk: (i, j)))
        args.append(acc)
    return _pcall(
        body, name=name, grid=(ka // ta, n // tn, nk), in_specs=in_specs,
        out_specs=pl.BlockSpec((ta, tn), lambda i, j, kk: (i, j)), out_shape=SDS((ka, n), out_dtype),
        scratch_shapes=[pltpu.VMEM((ta, tn), F32)], compiler_params=_cp(("parallel", "parallel", "arbitrary"), VMEM_BIG),
    )(*args)


def _modvec(cc, wm, bm):
    def body(c_ref, w_ref, b_ref, o_ref):
        o_ref[...] = _nn(_silu(c_ref[...]).astype(BF16), w_ref[...]) + b_ref[...]

    return _pcall(body, name="modvec", out_shape=SDS((8, 3 * D), F32), compiler_params=_cp(None, VMEM_BIG))(cc, wm, bm)


def _dcctx(dmodc, wm):
    def body(d_ref, w_ref, o_ref):
        o_ref[...] = _nt(d_ref[...].astype(BF16), w_ref[...])

    return _pcall(
        body, name="dcctx", grid=(1,), in_specs=[_full((8, 2 * D)), pl.BlockSpec((D, 2 * D), lambda i: (0, 0))],
        out_specs=_full((8, D)), out_shape=SDS((8, D), F32), compiler_params=_cp(("arbitrary",), VMEM_BIG),
    )(dmodc, wm)


def _prep_h(x, ng, scale, shift, name):
    m = x.shape[0]

    def body(x_ref, g_ref, sc_ref, sh_ref, h_ref):
        xf = x_ref[...]
        r = lax.rsqrt(jnp.mean(xf * xf, axis=-1, keepdims=True) + EPS)
        y = (xf * r) * g_ref[...]
        h_ref[...] = (y * (1.0 + sc_ref[...]) + sh_ref[...]).astype(BF16)

    tok = min(TOK, m)
    row = pl.BlockSpec((tok, D), lambda i: (i, 0))
    return _pcall(
        body, name=name, grid=(m // tok,), in_specs=[row, _full((1, D)), _full((1, D)), _full((1, D))],
        out_specs=row, out_shape=SDS((m, D), BF16), compiler_params=_cp(("parallel",)),
    )(x, ng, scale, shift)


def _resident(shape):
    return pl.BlockSpec(shape, lambda *_: (0,) * len(shape), pipeline_mode=pl.Buffered(1))


PROJ_TM = 512


def _proj_fwd(x, ng, scale, shift, wit_g, wit_r, wlrt, ln_g, ln_b, share=()):
    m = x.shape[0]
    ns = len(share)
    steps = m // PROJ_TM
    src = [(0, 0), (0, D), (1, 2 * D), (1, 0), (1, D)]

    def body(*refs):
        x_ref, g_ref, sc_ref, sh_ref, wg_ref, wr_ref, wl_ref, lg_ref, lb_ref = refs[:9]
        share_refs = refs[9:9 + ns]
        h_ref, p_ref, plr_ref, vr_ref, vc_ref = refs[9 + ns:14 + ns]
        got_refs = refs[14 + ns:14 + 2 * ns]
        sems = refs[14 + 2 * ns:]

        def copies():
            cx, cy, cc = _coords()
            me = 2 * cx + cy
            peers = [(1 - cx, cy), (cx, 1 - cy), (1 - cx, 1 - cy)]
            out, back = [], []
            for k in range(ns):
                for j, (px, py) in enumerate(peers):
                    out.append(_remote(share_refs[k], got_refs[k].at[me], sems[0].at[3 * k + j], sems[1].at[3 * k + j], (px, py, cc)))
                    landed = got_refs[k].at[2 * px + py]
                    back.append(_remote(landed, landed, sems[0].at[3 * k + j], sems[1].at[3 * k + j], (px, py, cc)))
            return out, back

        if ns:
            @pl.when(pl.program_id(0) == 0)
            def _():
                for rc in copies()[0]:
                    rc.start()

            @pl.when(pl.program_id(0) == steps - 1)
            def _():
                out, back = copies()
                for rc in back:
                    rc.wait_recv()
                for rc in out:
                    rc.wait_send()

        xf = x_ref[...]
        r = lax.rsqrt(jnp.mean(xf * xf, axis=-1, keepdims=True) + EPS)
        y = (xf * r) * g_ref[...]
        h = (y * (1.0 + sc_ref[...]) + sh_ref[...]).astype(BF16)
        h_ref[...] = h
        for j, (which, r0) in enumerate(src):
            w_ref = wr_ref if which else wg_ref
            blk = _nt(h, w_ref[r0:r0 + D, :]).astype(BF16)
            p_ref[:, D * j:D * j + D] = blk
            if j == 4:
                xf = blk[:, 512:1024].astype(F32)
                xc = xf - jnp.mean(xf, axis=-1, keepdims=True)
                vn = (xc * lax.rsqrt(jnp.mean(xc * xc, axis=-1, keepdims=True) + EPS)) * lg_ref[...] + lb_ref[...]
                vr_ref[...] = vn[:, 0:256].astype(BF16)
                vc_ref[0] = vn[:, 256:384].astype(BF16)
                vc_ref[1] = vn[:, 384:512].astype(BF16)
        plr_ref[...] = _nt(h, wl_ref[...])

    row = pl.BlockSpec((PROJ_TM, D), lambda i: (i, 0))
    vec = _full((1, D))
    res = _pcall(
        body, name="proj_fwd", grid=(steps,),
        in_specs=[row, vec, vec, vec, _resident((2 * D, D)), _resident((3 * D, D)), _resident((LRW, D)), _full((1, 512)),
                  _full((1, 512))] + [ANY] * ns,
        out_specs=[row, pl.BlockSpec((PROJ_TM, NP), lambda i: (i, 0)), pl.BlockSpec((PROJ_TM, LRW), lambda i: (i, 0)),
                   pl.BlockSpec((PROJ_TM, 256), lambda i: (i, 0)), pl.BlockSpec((2, PROJ_TM, 128), lambda i: (0, i, 0))] + [ANY] * ns,
        out_shape=[SDS((m, D), BF16), SDS((m, NP), BF16), SDS((m, LRW), F32), SDS((m, 256), BF16), SDS((2, m, 128), BF16)]
        + [SDS((4,) + a.shape, a.dtype) for a in share],
        scratch_shapes=([pltpu.SemaphoreType.DMA((3 * ns,)), pltpu.SemaphoreType.DMA((3 * ns,))] if ns else []),
        compiler_params=_cp(("arbitrary",), VMEM_BIG),
    )(x, ng, scale, shift, wit_g, wit_r, wlrt, ln_g, ln_b, *share)
    return res[0], res[1], res[2], res[3], res[4], list(res[5:])


def _proj_bwd(dp_g, dp_r, dlr, wit_g, wit_r, wlrt, x, dx1, ng, scale, send=(), share8=None):
    m = x.shape[0]
    ns = len(send)
    n8 = 0 if share8 is None else 1
    steps = m // PROJ_TM
    masks = [(mx, my, mc) for mx in range(2) for my in range(2) for mc in range(2)][1:]

    def body(*refs):
        (dpg_ref, dpr_ref, dlr_ref, wg_ref, wr_ref, wl_ref, x_ref, r_ref, g_ref, sc_ref) = refs[:10]
        send_refs = refs[10:10 + ns]
        n_in = 10 + ns + n8
        dx_ref, dg_ref, dsc_ref, dsh_ref = refs[n_in:n_in + 4]
        got_refs = refs[n_in + 4:n_in + 4 + ns]
        sems = refs[n_in + 4 + ns + n8:]
        i = pl.program_id(0)

        def copies():
            cx, cy, cc = _coords()
            me = 2 * cx + cy
            peers = [(1 - cx, cy), (cx, 1 - cy), (1 - cx, 1 - cy)]
            out, back = [], []
            for k in range(ns):
                for j, (px, py) in enumerate(peers):
                    out.append(_remote(send_refs[k].at[2 * px + py], got_refs[k].at[me], sems[0].at[3 * k + j],
                                       sems[1].at[3 * k + j], (px, py, cc)))
                    landed = got_refs[k].at[2 * px + py]
                    back.append(_remote(landed, landed, sems[0].at[3 * k + j], sems[1].at[3 * k + j], (px, py, cc)))
            if n8:
                src8, all8 = refs[10 + ns], refs[n_in + 4 + ns]
                s8, r8 = sems[-2], sems[-1]
                for j, (mx, my, mc) in enumerate(masks):
                    px, py, pc = _flip(cx, mx), _flip(cy, my), _flip(cc, mc)
                    out.append(_remote(src8, all8.at[4 * cx + 2 * cy + cc], s8.at[j], r8.at[j], (px, py, pc)))
                    landed = all8.at[4 * px + 2 * py + pc]
                    back.append(_remote(landed, landed, s8.at[j], r8.at[j], (px, py, pc)))
            return out, back

        @pl.when(i == 0)
        def _():
            dg_ref[...] = jnp.zeros_like(dg_ref)
            dsc_ref[...] = jnp.zeros_like(dsc_ref)
            dsh_ref[...] = jnp.zeros_like(dsh_ref)
            if ns or n8:
                for rc in copies()[0]:
                    rc.start()

        dh_ = (_nn(dpg_ref[...], wg_ref[...]) + _nn(dpr_ref[...], wr_ref[...])
               + _nn(dlr_ref[...].astype(BF16), wl_ref[...]))
        xf = x_ref[...]
        r = lax.rsqrt(jnp.mean(xf * xf, axis=-1, keepdims=True) + EPS)
        xh = xf * r
        y = xh * g_ref[...]
        dsh_ref[...] += jnp.sum(dh_, axis=0, keepdims=True)
        dsc_ref[...] += jnp.sum(dh_ * y, axis=0, keepdims=True)
        dy = dh_ * (1.0 + sc_ref[...])
        dg_ref[...] += jnp.sum(dy * xh, axis=0, keepdims=True)
        dxh = dy * g_ref[...]
        dx_ref[...] = r * (dxh - xh * jnp.mean(dxh * xh, axis=-1, keepdims=True)) + r_ref[...]

        if ns or n8:
            @pl.when(i == steps - 1)
            def _():
                out, back = copies()
                for rc in back:
                    rc.wait_recv()
                for rc in out:
                    rc.wait_send()

    row = pl.BlockSpec((PROJ_TM, D), lambda i: (i, 0))
    vec = _full((1, D))
    kg, kr = dp_g.shape[1], dp_r.shape[1]
    extra_in = list(send) + ([share8] if n8 else [])
    extra_out = [SDS(a.shape, a.dtype) for a in send] + ([SDS((8,) + share8.shape, share8.dtype)] if n8 else [])
    res = _pcall(
        body, name="proj_bwd", grid=(steps,),
        in_specs=[pl.BlockSpec((PROJ_TM, kg), lambda i: (i, 0)), pl.BlockSpec((PROJ_TM, kr), lambda i: (i, 0)),
                  pl.BlockSpec((PROJ_TM, LRW), lambda i: (i, 0)), _resident((kg, D)), _resident((kr, D)), _resident((LRW, D)),
                  row, row, vec, vec] + [ANY] * len(extra_in),
        out_specs=[row, vec, vec, vec] + [ANY] * len(extra_out),
        out_shape=[SDS((m, D), F32), SDS((1, D), F32), SDS((1, D), F32), SDS((1, D), F32)] + extra_out,
        scratch_shapes=(([pltpu.SemaphoreType.DMA((3 * ns,)), pltpu.SemaphoreType.DMA((3 * ns,))] if ns else [])
                        + ([pltpu.SemaphoreType.DMA((7,)), pltpu.SemaphoreType.DMA((7,))] if n8 else [])),
        compiler_params=_cp(("arbitrary",), VMEM_BIG),
    )(dp_g, dp_r, dlr, wit_g, wit_r, wlrt, x, dx1, ng, scale, *extra_in)
    return tuple(res[:4]), list(res[4:4 + ns]), (res[4 + ns] if n8 else None)


def _prep_bwd(x, dh, dx1, ng, scale, name):
    m = x.shape[0]
    has_res = dx1 is not None

    def body(*refs):
        if has_res:
            x_ref, dh_ref, r_ref, g_ref, sc_ref, dx_ref, dg_ref, dsc_ref, dsh_ref = refs
        else:
            x_ref, dh_ref, g_ref, sc_ref, dx_ref, dg_ref, dsc_ref, dsh_ref = refs
        i = pl.program_id(0)

        @pl.when(i == 0)
        def _():
            dg_ref[...] = jnp.zeros_like(dg_ref)
            dsc_ref[...] = jnp.zeros_like(dsc_ref)
            dsh_ref[...] = jnp.zeros_like(dsh_ref)

        xf = x_ref[...]
        dh_ = dh_ref[...]
        r = lax.rsqrt(jnp.mean(xf * xf, axis=-1, keepdims=True) + EPS)
        xh = xf * r
        y = xh * g_ref[...]
        dsh_ref[...] += jnp.sum(dh_, axis=0, keepdims=True)
        dsc_ref[...] += jnp.sum(dh_ * y, axis=0, keepdims=True)
        dy = dh_ * (1.0 + sc_ref[...])
        dg_ref[...] += jnp.sum(dy * xh, axis=0, keepdims=True)
        dxh = dy * g_ref[...]
        dx = r * (dxh - xh * jnp.mean(dxh * xh, axis=-1, keepdims=True))
        if has_res:
            dx = dx + r_ref[...]
        dx_ref[...] = dx

    tok = min(TOK, m)
    row = pl.BlockSpec((tok, D), lambda i: (i, 0))
    vec = _full((1, D))
    in_specs = [row, row] + ([row] if has_res else []) + [vec, vec]
    args = [x, dh] + ([dx1] if has_res else []) + [ng, scale]
    return _pcall(
        body, name=name, grid=(m // tok,), in_specs=in_specs, out_specs=[row, vec, vec, vec],
        out_shape=[SDS((m, D), F32), SDS((1, D), F32), SDS((1, D), F32), SDS((1, D), F32)],
        compiler_params=_cp(("arbitrary",)),
    )(*args)


COLB = 2048


def _colmix_fwd(vnc, ws23, bs23):
    rows = vnc.shape[2] // COLB

    def body(v_ref, w_ref, b_ref, o_ref):
        o_ref[0] = _nn(w_ref[0], v_ref[0]) + b_ref[0]

    return _pcall(
        body, name="colmix_fwd", grid=(2, rows),
        in_specs=[pl.BlockSpec((1, AC, COLB), lambda g, j: (g, 0, j)), pl.BlockSpec((1, AC, AC), lambda g, j: (g, 0, 0)),
                  pl.BlockSpec((1, AC, 1), lambda g, j: (g, 0, 0))],
        out_specs=pl.BlockSpec((1, AC, COLB), lambda g, j: (g, 0, j)),
        out_shape=SDS(vnc.shape, F32), compiler_params=_cp(("parallel", "parallel")),
    )(vnc, ws23, bs23)


def _colmix_bwd(dsvc, vnc, ws23t):
    rows = vnc.shape[2] // COLB

    def body(d_ref, v_ref, wt_ref, dv_ref, dw_ref, db_ref):
        j = pl.program_id(1)

        @pl.when(j == 0)
        def _():
            dw_ref[...] = jnp.zeros_like(dw_ref)
            db_ref[...] = jnp.zeros_like(db_ref)

        d = d_ref[0]
        d16 = d.astype(BF16)
        dv_ref[0] = _nn(wt_ref[0], d16)
        dw_ref[0] += _nt(d16, v_ref[0])
        db_ref[0] += jnp.sum(d, axis=1, keepdims=True)

    blk = pl.BlockSpec((1, AC, COLB), lambda g, j: (g, 0, j))
    return _pcall(
        body, name="colmix_bwd", grid=(2, rows),
        in_specs=[blk, blk, pl.BlockSpec((1, AC, AC), lambda g, j: (g, 0, 0))],
        out_specs=[blk, pl.BlockSpec((1, AC, AC), lambda g, j: (g, 0, 0)), pl.BlockSpec((1, AC, 1), lambda g, j: (g, 0, 0))],
        out_shape=[SDS(vnc.shape, F32), SDS((2, AC, AC), F32), SDS((2, AC, 1), F32)],
        compiler_params=_cp(("parallel", "arbitrary")),
    )(dsvc, vnc, ws23t)


def _head_norm(o):
    out = []
    for h in range(4):
        oh = o[:, 128 * h:128 * h + 128]
        r = lax.rsqrt(jnp.mean(oh * oh, axis=-1, keepdims=True) + EPS)
        out.append((r, oh * r))
    return out


def _tail_fwd(o_f, o_b, p, vnr, svc, x, tgt, ws01, bs01, gbn, wpa, wpb, wo, gate, gf):
    m = p.shape[0]

    def body(of_ref, ob_ref, zb_ref, ua_ref, za_ref, ga_ref, gb_ref, vnr_ref, svc_ref, x_ref, t_ref, w_ref, b_ref, g_ref,
             wpa_ref, wpb_ref, wo_ref, gate_ref, gf_ref,
             ya_ref, yb_ref, svr_ref, dwo_ref, dx1_ref, dout_ref, loss_ref, dgate_ref, dgf_ref):
        i = pl.program_id(0)

        @pl.when(i == 0)
        def _():
            loss_ref[...] = jnp.zeros_like(loss_ref)
            dgate_ref[...] = jnp.zeros_like(dgate_ref)
            dgf_ref[...] = jnp.zeros_like(dgf_ref)
            dwo_ref[...] = jnp.zeros_like(dwo_ref)

        o = of_ref[...] + ob_ref[...]
        zb = zb_ref[...].astype(F32)
        for h, (r, xh) in enumerate(_head_norm(o)):
            sl = slice(128 * h, 128 * h + 128)
            yb_ref[:, sl] = ((xh * g_ref[:, sl]) * _silu(zb[:, sl])).astype(BF16)
        for j in range(TOK // AC):
            for g in range(2):
                sv = _nn(w_ref[g], vnr_ref[AC * j:AC * j + AC, AC * g:AC * g + AC]) + b_ref[g]
                svr_ref[AC * j:AC * j + AC, AC * g:AC * g + AC] = sv
        sz = _silu(za_ref[...].astype(F32))
        u = ua_ref[...].astype(F32)
        ya_ref[:, 0:256] = ((u[:, 0:256] * svr_ref[...]) * sz[:, 0:256]).astype(BF16)
        ya_ref[:, 256:384] = ((u[:, 256:384] * svc_ref[0]) * sz[:, 256:384]).astype(BF16)
        ya_ref[:, 384:512] = ((u[:, 384:512] * svc_ref[1]) * sz[:, 384:512]).astype(BF16)
        ya = _nn(ya_ref[...], wpa_ref[...])
        yb = _nn(yb_ref[...], wpb_ref[...])
        mg = (jax.nn.sigmoid(ga_ref[...].astype(F32)) * ya + jax.nn.sigmoid(gb_ref[...].astype(F32)) * yb).astype(BF16)
        out_ = _nn(mg, wo_ref[...])
        x1 = x_ref[...] + gate_ref[...] * out_
        r = lax.rsqrt(jnp.mean(x1 * x1, axis=-1, keepdims=True) + EPS)
        xh = x1 * r
        err = xh * gf_ref[...] - t_ref[...]
        loss_ref[...] += 0.5 * jnp.sum(jnp.mean(err * err, axis=-1, keepdims=True), axis=0, keepdims=True)
        dy = err * (1.0 / D)
        dgf_ref[...] += jnp.sum(dy * xh, axis=0, keepdims=True)
        dxh = dy * gf_ref[...]
        dx1 = r * (dxh - xh * jnp.mean(dxh * xh, axis=-1, keepdims=True))
        dx1_ref[...] = dx1
        dout16 = (gate_ref[...] * dx1).astype(BF16)
        dout_ref[...] = dout16
        dgate_ref[...] += jnp.sum(dx1 * out_, axis=0, keepdims=True)
        dwo_ref[...] += _tn(mg, dout16)

    r512 = pl.BlockSpec((TOK, 512), lambda i: (i, 0))
    row = pl.BlockSpec((TOK, D), lambda i: (i, 0))
    vec = _full((1, D))
    return _pcall(
        body, name="tail_fwd", grid=(m // TOK,),
        in_specs=[r512, r512, pl.BlockSpec((TOK, 512), lambda i: (i, 6)), pl.BlockSpec((TOK, 512), lambda i: (i, 7)),
                  pl.BlockSpec((TOK, 512), lambda i: (i, 8)), row, pl.BlockSpec((TOK, D), lambda i: (i, 1)),
                  pl.BlockSpec((TOK, 256), lambda i: (i, 0)), pl.BlockSpec((2, TOK, 128), lambda i: (0, i, 0)), row, row,
                  _full((2, AC, AC)), _full((2, AC, 1)), _full((1, 512)), _resident((512, D)), _resident((512, D)),
                  _resident((D, D)), vec, vec],
        out_specs=[r512, r512, pl.BlockSpec((TOK, 256), lambda i: (i, 0)), _resident((D, D)), row, row, _full((1, 128)), vec, vec],
        out_shape=[SDS((m, 512), BF16), SDS((m, 512), BF16), SDS((m, 256), F32), SDS((D, D), F32), SDS((m, D), F32),
                   SDS((m, D), BF16), SDS((1, 128), F32), SDS((1, D), F32), SDS((1, D), F32)],
        compiler_params=_cp(("arbitrary",), VMEM_BIG),
    )(o_f, o_b, p, p, p, p, p, vnr, svc, x, tgt, ws01, bs01, gbn, wpa, wpb, wo, gate, gf)


DPR = 3072


def _tail_bwd(dout, ya_in, yb_in, p, svr, svc, o_f, o_b, gbn, wo, wpa, wpb):
    m = p.shape[0]

    def body(dout_ref, ya_ref, yb_ref, ga_ref, gb_ref, zb_ref, ua_ref, za_ref, svr_ref, svc_ref, of_ref, ob_ref, g_ref,
             wo_ref, wpa_ref, wpb_ref,
             dwpa_ref, dwpb_ref, dpg_ref, dpr_ref, dsr_ref, dsc_ref, do_ref, dg_ref):
        i = pl.program_id(0)

        @pl.when(i == 0)
        def _():
            dg_ref[...] = jnp.zeros_like(dg_ref)
            dwpa_ref[...] = jnp.zeros_like(dwpa_ref)
            dwpb_ref[...] = jnp.zeros_like(dwpb_ref)

        dm_ = _nt(dout_ref[...], wo_ref[...])
        ya_in, yb_in = ya_ref[...], yb_ref[...]
        ya = _nn(ya_in, wpa_ref[...])
        yb = _nn(yb_in, wpb_ref[...])
        sa = jax.nn.sigmoid(ga_ref[...].astype(F32))
        sb = jax.nn.sigmoid(gb_ref[...].astype(F32))
        dya16 = (dm_ * sa).astype(BF16)
        dyb16 = (dm_ * sb).astype(BF16)
        dwpa_ref[...] += _tn(ya_in, dya16)
        dwpb_ref[...] += _tn(yb_in, dyb16)
        dpg_ref[:, 0:D] = (dm_ * ya * (sa * (1.0 - sa))).astype(BF16)
        dpg_ref[:, D:2 * D] = (dm_ * yb * (sb * (1.0 - sb))).astype(BF16)
        dya = _nt(dya16, wpa_ref[...])
        dyb = _nt(dyb16, wpb_ref[...])

        u = ua_ref[...].astype(F32)
        za = za_ref[...].astype(F32)
        sz, dsz = _silu_and_grad(za)
        sv = jnp.concatenate([svr_ref[...], svc_ref[0], svc_ref[1]], axis=1)
        dpr_ref[:, 512:1024] = (dya * sv * sz).astype(BF16)
        dsv = dya * u * sz
        dsr_ref[...] = dsv[:, 0:256]
        dsc_ref[0] = dsv[:, 256:384]
        dsc_ref[1] = dsv[:, 384:512]
        dpr_ref[:, 1024:1536] = (dya * u * sv * dsz).astype(BF16)

        zb = zb_ref[...].astype(F32)
        o = of_ref[...] + ob_ref[...]
        szb, dszb = _silu_and_grad(zb)
        for h, (r, xh) in enumerate(_head_norm(o)):
            sl = slice(128 * h, 128 * h + 128)
            gh = g_ref[:, sl]
            don = dyb[:, sl] * szb[:, sl]
            dpr_ref[:, sl] = (dyb[:, sl] * (xh * gh) * dszb[:, sl]).astype(BF16)
            dg_ref[:, sl] += jnp.sum(don * xh, axis=0, keepdims=True)
            dxh = don * gh
            do_ref[:, sl] = (r * (dxh - xh * jnp.mean(dxh * xh, axis=-1, keepdims=True))).astype(BF16)

    r512 = pl.BlockSpec((TOK, 512), lambda i: (i, 0))
    row = pl.BlockSpec((TOK, D), lambda i: (i, 0))
    return _pcall(
        body, name="tail_bwd", grid=(m // TOK,),
        in_specs=[row, r512, r512, row, pl.BlockSpec((TOK, D), lambda i: (i, 1)), pl.BlockSpec((TOK, 512), lambda i: (i, 6)),
                  pl.BlockSpec((TOK, 512), lambda i: (i, 7)), pl.BlockSpec((TOK, 512), lambda i: (i, 8)),
                  pl.BlockSpec((TOK, 256), lambda i: (i, 0)), pl.BlockSpec((2, TOK, 128), lambda i: (0, i, 0)), r512, r512,
                  _full((1, 512)), _resident((D, D)), _resident((512, D)), _resident((512, D))],
        out_specs=[_resident((512, D)), _resident((512, D)), pl.BlockSpec((TOK, 2 * D), lambda i: (i, 0)),
                   pl.BlockSpec((TOK, 1536), lambda i: (i, 0)),
                   pl.BlockSpec((TOK, 256), lambda i: (i, 0)), pl.BlockSpec((2, TOK, 128), lambda i: (0, i, 0)), r512, _full((1, 512))],
        out_shape=[SDS((512, D), F32), SDS((512, D), F32), SDS((m, 2 * D), BF16), SDS((m, DPR), BF16), SDS((m, 256), F32),
                   SDS((2, m, 128), F32), SDS((m, 512), BF16), SDS((1, 512), F32)],
        compiler_params=_cp(("arbitrary",), VMEM_BIG),
    )(dout, ya_in, yb_in, p, p, p, p, p, svr, svc, o_f, o_b, gbn, wo, wpa, wpb)


def _ln_bwd(dsr, vnr, dvnc, p, ws01t, ln_g, dp):
    m = p.shape[0]

    def body(dsr_ref, vnr_ref, dvc_ref, va_ref, wt_ref, g_ref, dpi_ref, dp_ref, dw_ref, db_ref, dlg_ref, dlb_ref, dvn_ref):
        i = pl.program_id(0)

        @pl.when(i == 0)
        def _():
            dw_ref[...] = jnp.zeros_like(dw_ref)
            db_ref[...] = jnp.zeros_like(db_ref)
            dlg_ref[...] = jnp.zeros_like(dlg_ref)
            dlb_ref[...] = jnp.zeros_like(dlb_ref)

        for j in range(TOK // AC):
            for g in range(2):
                d = dsr_ref[AC * j:AC * j + AC, AC * g:AC * g + AC]
                d16 = d.astype(BF16)
                dvn_ref[AC * j:AC * j + AC, AC * g:AC * g + AC] = _nn(wt_ref[g], d16)
                dw_ref[g] += _nt(d16, vnr_ref[AC * j:AC * j + AC, AC * g:AC * g + AC])
                db_ref[g] += jnp.sum(d, axis=1, keepdims=True)
        dvn_ref[:, 256:384] = dvc_ref[0]
        dvn_ref[:, 384:512] = dvc_ref[1]
        dvn = dvn_ref[...]
        xf = va_ref[...].astype(F32)
        xc = xf - jnp.mean(xf, axis=-1, keepdims=True)
        rs = lax.rsqrt(jnp.mean(xc * xc, axis=-1, keepdims=True) + EPS)
        xh = xc * rs
        dlg_ref[...] += jnp.sum(dvn * xh, axis=0, keepdims=True)
        dlb_ref[...] += jnp.sum(dvn, axis=0, keepdims=True)
        dxh = dvn * g_ref[...]
        dva = rs * (dxh - jnp.mean(dxh, axis=-1, keepdims=True) - xh * jnp.mean(dxh * xh, axis=-1, keepdims=True))
        dp_ref[...] = dva.astype(BF16)

    return _pcall(
        body, name="ln_bwd", grid=(m // TOK,),
        in_specs=[pl.BlockSpec((TOK, 256), lambda i: (i, 0)), pl.BlockSpec((TOK, 256), lambda i: (i, 0)),
                  pl.BlockSpec((2, TOK, 128), lambda i: (0, i, 0)), pl.BlockSpec((TOK, 512), lambda i: (i, 9)),
                  _full((2, AC, AC)), _full((1, 512)), pl.BlockSpec(memory_space=pl.ANY)],
        out_specs=[pl.BlockSpec((TOK, 512), lambda i: (i, 3)), _full((2, AC, AC)), _full((2, AC, 1)), _full((1, 512)), _full((1, 512))],
        out_shape=[SDS((m, DPR), BF16), SDS((2, AC, AC), F32), SDS((2, AC, 1), F32), SDS((1, 512), F32), SDS((1, 512), F32)],
        scratch_shapes=[pltpu.VMEM((TOK, 512), F32)],
        input_output_aliases={6: 0}, compiler_params=_cp(("arbitrary",)),
    )(dsr, vnr, dvnc, p, ws01t, ln_g, dp)


def _tri_mm(tri, a):
    a1 = a.astype(BF16)
    r1 = a - a1.astype(F32)
    a2 = r1.astype(BF16)
    a3 = (r1 - a2.astype(F32)).astype(BF16)
    n = a.shape[1]
    r = _nn(tri, jnp.concatenate([a1, a2, a3], axis=1))
    return r[:, 0:n] + r[:, n:2 * n] + r[:, 2 * n:3 * n]


def _gla_masks(reverse):
    ri = lax.broadcasted_iota(jnp.int32, (CH, CH), 0)
    ci = lax.broadcasted_iota(jnp.int32, (CH, CH), 1)
    vis = (ci >= ri) if reverse else (ci <= ri)
    vis_t = (ci <= ri) if reverse else (ci >= ri)
    r4 = lax.broadcasted_iota(jnp.int32, (4 * CH, CH), 0) & (CH - 1)
    c4 = lax.broadcasted_iota(jnp.int32, (4 * CH, CH), 1)
    vis4 = (c4 >= r4) if reverse else (c4 <= r4)
    vis4_t = (c4 <= r4) if reverse else (c4 >= r4)
    lane = lax.broadcasted_iota(jnp.int32, (1, 256), 1)
    hm = [(lane >= CH * h) & (lane < CH * h + CH) for h in range(4)]
    return vis, vis_t, vis4, vis4_t, hm


def _stack_heads(x, hm):
    return jnp.concatenate([jnp.where(hm[h], x, 0.0).astype(BF16) for h in range(4)], axis=0)


def _diag_heads(full, hm):
    r = full.shape[0] // 4
    acc = jnp.where(hm[0], full[0:r], 0.0)
    for h in range(1, 4):
        acc = acc + jnp.where(hm[h], full[r * h:r * h + r], 0.0)
    return acc


def _rows_of_heads(x):
    return jnp.concatenate([x[:, 128 * h:128 * h + 128] for h in range(4)], axis=0)


def _lane_vis(reverse, transpose):
    ri = lax.broadcasted_iota(jnp.int32, (CH, 4 * CH), 0)
    ci = lax.broadcasted_iota(jnp.int32, (CH, 4 * CH), 1) & (CH - 1)
    return (ci >= ri) if (reverse != transpose) else (ci <= ri)


def _gla_fwd2(p, qkv_blk, lr, lrws, gbiases, s0s, name):
    m = p.shape[0]
    tb = min(GLA_TB, m)
    nb = m // tb
    nc = tb // CH

    def body(qkv_f, lr_f, qkv_b, lr_b, lrw_f, lrw_b, gb_f, gb_b, s0_f, s0_b,
             o_f, sb_f, sfin_f, o_b, sb_b, sfin_b, st_f, st_b):
        i = pl.program_id(0)

        @pl.when(i == 0)
        def _():
            st_f[...] = s0_f[...]
            st_b[...] = s0_b[...]

        dirs = []
        for reverse, qkv_ref, lr_ref, lrw_ref, gb_ref, o_ref, sb_ref, st_ref in (
                (False, qkv_f, lr_f, lrw_f, gb_f, o_f, sb_f, st_f), (True, qkv_b, lr_b, lrw_b, gb_b, o_b, sb_b, st_b)):
            vis, _, vis4, _, hm = _gla_masks(reverse)
            logits = _nn(lr_ref[...].astype(BF16), lrw_ref[...]) + gb_ref[...]
            dirs.append(dict(reverse=reverse, qkv=qkv_ref, o=o_ref, sb=sb_ref, vis4=vis4, hm=hm,
                             tri=vis.astype(F32).astype(BF16), a=_logsig(logits) * (1.0 / 16.0), st=st_ref[...]))
        for step in range(nc):
            for d in dirs:
                c = nc - 1 - step if d["reverse"] else step
                rows = slice(CH * c, CH * c + CH)
                b = _tri_mm(d["tri"], d["a"][rows])
                bl = b[0:1] if d["reverse"] else b[CH - 1:CH]
                q = d["qkv"][rows, 0:256].astype(F32) * 0.125
                k = d["qkv"][rows, 256:512].astype(F32)
                v16 = d["qkv"][rows, 512:1024].astype(BF16)
                qd = q * jnp.exp(b)
                kd16 = (k * jnp.exp(-b)).astype(BF16)
                kdec16 = (k * jnp.exp(bl - b)).astype(BF16)
                qstack = _stack_heads(qd, d["hm"])
                sc = jnp.where(d["vis4"], _nt(qstack, kd16), 0.0).astype(BF16)
                inter = _nt(qstack, d["st"].astype(BF16))
                for h in range(4):
                    d["o"][rows, 128 * h:128 * h + 128] = (
                        _nn(sc[CH * h:CH * h + CH], v16[:, 128 * h:128 * h + 128]) + inter[CH * h:CH * h + CH])
                d["sb"][c] = d["st"]
                d["st"] = d["st"] * jnp.exp(bl) + _diag_heads(_tn(v16, kdec16), d["hm"])
        st_f[...] = dirs[0]["st"]
        st_b[...] = dirs[1]["st"]

        @pl.when(i == nb - 1)
        def _():
            sfin_f[...] = dirs[0]["st"]
            sfin_b[...] = dirs[1]["st"]

    fw = lambda i: i
    bw = lambda i: nb - 1 - i
    in_specs = []
    for rm in (fw, bw):
        in_specs += [pl.BlockSpec((tb, 1024), lambda i, rm=rm: (rm(i), qkv_blk)), pl.BlockSpec((tb, LRW), lambda i, rm=rm: (rm(i), 0))]
    in_specs += [_full((LRW, 256))] * 2 + [_full((1, 256))] * 2 + [_full((128, 256))] * 2
    out_specs, out_shape = [], []
    for rm in (fw, bw):
        out_specs += [pl.BlockSpec((tb, 512), lambda i, rm=rm: (rm(i), 0)), pl.BlockSpec((nc, 128, 256), lambda i, rm=rm: (rm(i), 0, 0)),
                      _full((128, 256))]
        out_shape += [SDS((m, 512), F32), SDS((m // CH, 128, 256), F32), SDS((128, 256), F32)]
    return _pcall(
        body, name=name, grid=(nb,), in_specs=in_specs, out_specs=out_specs, out_shape=out_shape,
        scratch_shapes=[pltpu.VMEM((128, 256), F32), pltpu.VMEM((128, 256), F32)], compiler_params=_cp(("arbitrary",), VMEM_BIG),
    )(p, lr, p, lr, lrws[0], lrws[1], gbiases[0], gbiases[1], s0s[0], s0s[1])


def _gla_bwd(p, qkv_blk, lr, lrw, lrwt, gbias, sb, dsfin, do, prev, dp, *, reverse, name):
    m = p.shape[0]
    tb = min(GLA_TB, m)
    nb = m // tb
    nc = tb // CH
    rmap = (lambda i: i) if reverse else (lambda i: nb - 1 - i)
    has_prev = prev is not None
    has_dp = dp is not None

    def body(*refs):
        refs = list(refs)
        qkv_ref, lr_ref, lrw_ref, lrwt_ref, gb_ref, sb_ref, dsfin_ref, do_ref = refs[:8]
        refs = refs[8:]
        if has_prev:
            pq_ref, plr_ref = refs[:2]
            refs = refs[2:]
        if has_dp:
            refs = refs[1:]
        dqkv_ref, dlr_ref, dw2_ref, dgb_ref, ds0_ref, dst_ref, dlog_ref = refs
        i = pl.program_id(0)

        @pl.when(i == 0)
        def _():
            dst_ref[...] = dsfin_ref[...]
            dw2_ref[...] = jnp.zeros_like(dw2_ref)
            dgb_ref[...] = jnp.zeros_like(dgb_ref)

        vis, vis_t, vis4, vis4_t, hm = _gla_masks(reverse)
        tri = vis.astype(F32).astype(BF16)
        tri_t = vis_t.astype(F32).astype(BF16)
        lane_vis = _lane_vis(reverse, False)
        lane_vis_t = _lane_vis(reverse, True)
        lr16 = lr_ref[...].astype(BF16)
        logits = _nn(lr16, lrw_ref[...]) + gb_ref[...]
        a_all = _logsig(logits) * (1.0 / 16.0)
        dsig = (1.0 - jax.nn.sigmoid(logits)) * (1.0 / 16.0)
        dst = dst_ref[...]
        for c in (range(nc) if reverse else range(nc - 1, -1, -1)):
            rows = slice(CH * c, CH * c + CH)
            b = _tri_mm(tri, a_all[rows])
            bl = b[0:1] if reverse else b[CH - 1:CH]
            eb = jnp.exp(b)
            enb = jnp.exp(-b)
            ebl = jnp.exp(bl - b)
            el = jnp.exp(bl)
            q = qkv_ref[rows, 0:256].astype(F32) * 0.125
            k = qkv_ref[rows, 256:512].astype(F32)
            v16 = qkv_ref[rows, 512:1024].astype(BF16)
            do16 = do_ref[rows, :].astype(BF16)
            qd = q * eb
            kd = k * enb
            kdec = k * ebl
            st = sb_ref[c]
            st16 = st.astype(BF16)
            dst16 = dst.astype(BF16)
            qd16 = qd.astype(BF16)
            kd16 = kd.astype(BF16)
            qstack = _stack_heads(qd, hm)
            kstack = _stack_heads(kd, hm)
            kdecstack = _stack_heads(kdec, hm)
            pt = jnp.where(vis4_t, _nt(kstack, qd16), 0.0).astype(BF16)
            dvinter = _nt(kdecstack, dst16)
            do_rows = _rows_of_heads(do16)
            v_rows = _rows_of_heads(v16)
            dp_cat = jnp.where(lane_vis, _diag_heads(_nt(do_rows, v_rows), hm), 0.0).astype(BF16)
            dpt_cat = jnp.where(lane_vis_t, _diag_heads(_nt(v_rows, do_rows), hm), 0.0).astype(BF16)
            dqd = _nn(dp_cat, kstack) + _diag_heads(_nn(do_rows, st16), hm)
            dkd = _nn(dpt_cat, qstack)
            dkdec = _diag_heads(_nn(v_rows, dst16), hm)
            for h in range(4):
                rh = slice(CH * h, CH * h + CH)
                dv_h = _nn(pt[rh], do_rows[rh]) + dvinter[rh]
                if has_prev:
                    dv_h = dv_h + pq_ref[rows, 512 + 128 * h:512 + 128 * h + 128]
                dqkv_ref[rows, 512 + 128 * h:512 + 128 * h + 128] = dv_h.astype(dqkv_ref.dtype)
            dq = dqd * eb * 0.125
            dk = dkd * enb + dkdec * ebl
            if has_prev:
                dq = dq + pq_ref[rows, 0:256]
                dk = dk + pq_ref[rows, 256:512]
            dqkv_ref[rows, 0:256] = dq.astype(dqkv_ref.dtype)
            dqkv_ref[rows, 256:512] = dk.astype(dqkv_ref.dtype)
            g_kdec = dkdec * kdec
            db = dqd * qd - dkd * kd - g_kdec
            dbl = jnp.sum(g_kdec, axis=0, keepdims=True) + jnp.sum(st * dst, axis=0, keepdims=True) * el
            da = _tri_mm(tri_t, db) + dbl
            dlog_ref[rows, :] = da * dsig[rows]
            dst = dst * el + _diag_heads(_tn(do16, qd16), hm)
        dst_ref[...] = dst
        dlog = dlog_ref[...]
        dlog16 = dlog.astype(BF16)
        dlr = _nn(dlog16, lrwt_ref[...])
        if has_prev:
            dlr = dlr + plr_ref[...]
        dlr_ref[...] = dlr
        dw2_ref[...] += _tn(lr16, dlog16)
        dgb_ref[...] += jnp.sum(dlog, axis=0, keepdims=True)

        @pl.when(i == nb - 1)
        def _():
            ds0_ref[...] = dst

    in_specs = [pl.BlockSpec((tb,1024), lambda i: (rmap(i), qkv_blk)), pl.BlockSpec((tb,LRW), lambda i: (rmap(i), 0)),
                _full((LRW, 256)), _full((256, LRW)), _full((1, 256)), pl.BlockSpec((nc, 128, 256), lambda i: (rmap(i), 0, 0)),
                _full((128, 256)), pl.BlockSpec((tb,512), lambda i: (rmap(i), 0))]
    args = [p, lr, lrw, lrwt, gbias, sb, dsfin, do]
    if has_prev:
        in_specs += [pl.BlockSpec((tb,1024), lambda i: (rmap(i), 0)), pl.BlockSpec((tb,LRW), lambda i: (rmap(i), 0))]
        args += list(prev)
    aliases = {}
    if has_dp:
        in_specs.append(pl.BlockSpec(memory_space=pl.ANY))
        aliases = {len(args): 0}
        args.append(dp)
        dq_spec = pl.BlockSpec((tb,1024), lambda i: (rmap(i), 2))
        dq_shape = SDS(dp.shape, dp.dtype)
    else:
        dq_spec = pl.BlockSpec((tb,1024), lambda i: (rmap(i), 0))
        dq_shape = SDS((m, 1024), F32)
    return _pcall(
        body, name=name, grid=(nb,), in_specs=in_specs,
        out_specs=[dq_spec, pl.BlockSpec((tb,LRW), lambda i: (rmap(i), 0)), _full((LRW, 256)), _full((1, 256)), _full((128, 256))],
        out_shape=[dq_shape, SDS((m, LRW), F32), SDS((LRW, 256), F32), SDS((1, 256), F32), SDS((128, 256), F32)],
        scratch_shapes=[pltpu.VMEM((128, 256), F32), pltpu.VMEM((tb,256), F32)],
        input_output_aliases=aliases, compiler_params=_cp(("arbitrary",)),
    )(*args)


EARLY_KEYS = ["dmodc", "dscc", "dng_c", "dlng", "dlnb", "dws", "dbs", "dgbn", "dgf", "dw2", "dgb2", "loss", "dgate"]
EARLY_SHAPES = [(1, 2 * D), (D,), (1, D), (1, 512), (1, 512), (1, 4, 128, 128), (1, 4, 128), (1, 512), (D,), (2, 16, 256), (2, 256),
                (128,), (1, D)]
EARLY_SIZE = 2 * D + D + D + 512 + 512 + 4 * 128 * 128 + 512 + 512 + D + 2 * 16 * 256 + 512 + 128 + D
EARLY_ROWS = 648


def _device_step(x, c, ctx, c_ctx, tgt, wm, bm, ng, wit_g, wit_r, wlrt, ln_g, ln_b, ws, bs, w2, gb2, gbn, wpa, wpb, wo, gf,
                 exchange=None, shards=None):
    L = x.shape[0]
    wit_qkv = wit_r[2048:3072]
    ws16 = ws.astype(BF16)
    wst16 = jnp.swapaxes(ws, 1, 2).astype(BF16)
    bscol = bs[:, :, None]
    lrw = [jnp.zeros((LRW, 256), F32).at[16 * r:16 * r + 16].set(w2[r]).astype(BF16) for r in range(2)]
    lrwt = [w.T for w in lrw]
    gbias = [gb2[r:r + 1] for r in range(2)]

    cc = jnp.zeros((8, D), F32).at[0:1].set(c).at[1:2].set(c_ctx)
    mod = _modvec(cc, wm, bm)
    shift, scale, gate = mod[0:1, 0:D], mod[0:1, D:2 * D], mod[0:1, 2 * D:3 * D]
    shift_c, scale_c = mod[1:2, 0:D], mod[1:2, D:2 * D]

    hc = _prep_h(ctx, ng, scale_c, shift_c, "prep_hc")
    pc = _mm(hc, wit_qkv, tm=256, tn=1024, tk=D, out_dtype=F32, name="mm_pc", b_t=True)
    plrc = _mm(hc, wlrt, tm=256, tn=LRW, tk=D, out_dtype=F32, name="mm_plrc", b_t=True)
    zero_s = jnp.zeros((128, 256), F32)
    _, sbc_f, sc_f, _, sbc_b, sc_b = _gla_fwd2(pc, 0, plrc, lrw, gbias, (zero_s, zero_s), "gla_fwd_c")

    h, p, plr, vnr, vnc, late = _proj_fwd(x, ng, scale, shift, wit_g, wit_r, wlrt, ln_g, ln_b,
                                          shards if shards is not None else ())
    if shards is not None:
        me_xy = 2 * lax.axis_index("x") + lax.axis_index("y")
        g_wpa, g_wpb, g_wo = (_own(g, s_, me_xy) for g, s_ in zip(late, shards))
        wpa = jnp.swapaxes(g_wpa, 0, 1).reshape(512, D)
        wpb = jnp.swapaxes(g_wpb, 0, 1).reshape(512, D)
        wo = g_wo.reshape(D, D)
    o_f, sb_f, _, o_b, sb_b, _ = _gla_fwd2(p, 2, plr, lrw, gbias, (sc_f, sc_b), "gla_fwd")
    svc = _colmix_fwd(vnc.reshape(2, AC, L), ws16[2:4], bscol[2:4]).reshape(2, L, 128)
    ya_in, yb_in, svr, dwo, dx1, dout, loss, dgate, dgf = _tail_fwd(
        o_f, o_b, p, vnr, svc, x, tgt, ws16[0:2], bscol[0:2], gbn, wpa, wpb, wo, gate, gf)

    dwpa, dwpb, dp_g, dp, dsr, dsc, do, dgbn = _tail_bwd(dout, ya_in, yb_in, p, svr, svc, o_f, o_b, gbn, wo, wpa, wpb)
    dvnc, dws23, dbs23 = _colmix_bwd(dsc.reshape(2, AC, L), vnc.reshape(2, AC, L), wst16[2:4])
    dp, dws01, dbs01, dlng, dlnb = _ln_bwd(dsr, vnr, dvnc.reshape(2, L, 128), p, wst16[0:2], ln_g, dp)
    zero_ds = jnp.zeros((128, 256), F32)
    dqkv_f, dlr_f, dw2_f, dgb_f, ds0_f = _gla_bwd(p, 2, plr, lrw[0], lrwt[0], gbias[0], sb_f, zero_ds, do, None, None,
                                                  reverse=False, name="gla_bwd_f")
    dp, dlr, dw2_b, dgb_b, ds0_b = _gla_bwd(p, 2, plr, lrw[1], lrwt[1], gbias[1], sb_b, zero_ds, do, (dqkv_f, dlr_f), dp,
                                            reverse=True, name="gla_bwd_b")
    zero_do = jnp.zeros((ctx.shape[0], 512), BF16)
    dqkvc_f, dlrc_f, dw2c_f, dgbc_f, _ = _gla_bwd(pc, 0, plrc, lrw[0], lrwt[0], gbias[0], sbc_f, ds0_f, zero_do, None, None,
                                                  reverse=False, name="gla_bwd_cf")
    dqkvc, dlrc, dw2c_b, dgbc_b, _ = _gla_bwd(pc, 0, plrc, lrw[1], lrwt[1], gbias[1], sbc_b, ds0_b, zero_do,
                                              (dqkvc_f, dlrc_f), None, reverse=True, name="gla_bwd_cb")
    dhc = _mm(dqkvc, wit_qkv, tm=256, tn=D, tk=1024, out_dtype=F32, name="mm_dhc")
    dhc = _mm(dlrc, wlrt, tm=256, tn=D, tk=LRW, out_dtype=F32, name="mm_dhc_lr", acc=dhc)
    _, dng_c, dscale_c, dshift_c = _prep_bwd(ctx, dhc, None, ng, scale_c, "prep_bwd_c")

    dwit_g = _mm_tn(dp_g, h, ta=1024, tn=D, tk=2048, name="mm_dwi_g", out_dtype=BF16)
    dwit_r = _mm_tn(dp, h, ta=1024, tn=D, tk=2048, name="mm_dwi_r", out_dtype=BF16)
    dwit_qkv = _mm_tn(dqkvc, hc, ta=1024, tn=D, tk=256, name="mm_dwi_c", acc=dwit_r[2048:3072], out_dtype=BF16)
    dwlrt = _mm_tn(dlr, h, ta=LRW, tn=D, tk=2048, name="mm_dwlr")
    dwlrt = _mm_tn(dlrc, hc, ta=LRW, tn=D, tk=256, name="mm_dwlr_c", acc=dwlrt, out_dtype=BF16)
    big = dict(dwit_g=dwit_g, dwit_r=dwit_r, dwit_qkv=dwit_qkv, dwlrt=dwlrt, dwpa=dwpa, dwpb=dwpb, dwo=dwo)

    dmodc = jnp.concatenate([dshift_c, dscale_c], axis=1)
    dscc = _dcctx(jnp.zeros((8, 2 * D), F32).at[0:1].set(dmodc), wm)[0:1]
    dw2p = dw2_f + dw2c_f, dw2_b + dw2c_b
    small = dict(
        dmodc=dmodc, dscc=dscc, dng_c=dng_c, dlng=dlng, dlnb=dlnb, dws=jnp.concatenate([dws01, dws23], axis=0),
        dbs=jnp.concatenate([dbs01, dbs23], axis=0)[:, :, 0], dgbn=dgbn, dgf=dgf,
        dw2=jnp.stack([dw2p[0][0:16], dw2p[1][16:32]]), dgb2=jnp.concatenate([dgb_f + dgbc_f, dgb_b + dgbc_b], axis=0),
        loss=loss[0, 0], dgate=dgate)

    send = exchange(big) if exchange is not None else ()
    early = _pack([small[k] for k in EARLY_KEYS[:-2]] + [jnp.broadcast_to(small["loss"], (128,)), small["dgate"]], EARLY_ROWS) \
        if exchange is not None else None
    (dx, dng, dscale, dshift), got, early_all = _proj_bwd(dp_g, dp, dlr, wit_g, wit_r, wlrt, x, dx1, ng, scale, send, early)
    return dict(dx=dx, got=got, early=early, early_all=early_all, dshift=dshift, dscale=dscale, dng_lat=dng, **big, **small)


ANY = pl.BlockSpec(memory_space=pl.ANY)


def _coords():
    return lax.axis_index("x"), lax.axis_index("y"), lax.axis_index("c")


def _flip(v, bit):
    return 1 - v if bit else v


def _remote(src, dst, send_sem, recv_sem, dev):
    return pltpu.make_async_remote_copy(src_ref=src, dst_ref=dst, send_sem=send_sem, recv_sem=recv_sem,
                                        device_id=dev, device_id_type=MESH)


def _own(out, block, idx):
    return lax.dynamic_update_slice_in_dim(out, block[None], idx, axis=0)


def _half_idx(shape, axis, which, lead=()):
    idx = [pl.ds(0, d) for d in shape]
    h = shape[axis] // 2
    idx[axis] = pl.ds(which * h, h)
    return tuple(lead) + tuple(idx)


def _gather_weights(split, whole, name):
    ns, nw = len(split), len(whole)
    n = ns + nw
    arrs = [a for a, _ in split] + list(whole)

    def body(*refs):
        ins, outs = refs[:n], refs[n:2 * n]
        a_send, a_recv, b_send, b_recv = refs[2 * n:]
        x, y, c = _coords()
        me = 2 * x + y
        sib = (x, y, 1 - c)
        peers = [(1 - x, y), (x, 1 - y), (1 - x, 1 - y)]

        def half(k, slot, which):
            return outs[k].at[_half_idx(arrs[k].shape, split[k][1], which, lead=(slot,))]

        sends = []
        for k in range(n):
            for j, (px, py) in enumerate(peers):
                if k < ns:
                    rc = _remote(ins[k].at[_half_idx(arrs[k].shape, split[k][1], c)], half(k, me, c), a_send.at[3 * k + j],
                                 a_recv.at[3 * k + j], (px, py, c))
                else:
                    rc = _remote(ins[k], outs[k].at[me], a_send.at[3 * k + j], a_recv.at[3 * k + j], (px, py, c))
                rc.start()
                sends.append(rc)
        for k in range(ns):
            for j, (px, py) in enumerate(peers):
                landed = half(k, 2 * px + py, c)
                _remote(landed, landed, a_send.at[3 * k + j], a_recv.at[3 * k + j], (px, py, c)).wait_recv()
                fw = _remote(landed, landed, b_send.at[3 * k + j], b_recv.at[3 * k + j], sib)
                fw.start()
                sends.append(fw)
        for k in range(ns, n):
            for j, (px, py) in enumerate(peers):
                landed = outs[k].at[2 * px + py]
                _remote(landed, landed, a_send.at[3 * k + j], a_recv.at[3 * k + j], (px, py, c)).wait_recv()
        for k in range(ns):
            for j, (px, py) in enumerate(peers):
                passed = half(k, 2 * px + py, 1 - c)
                _remote(passed, passed, b_send.at[3 * k + j], b_recv.at[3 * k + j], sib).wait_recv()
        for rc in sends:
            rc.wait_send()

    outs = _pcall(
        body, name=name, in_specs=[ANY] * n, out_specs=[ANY] * n,
        out_shape=[SDS((4,) + a.shape, a.dtype) for a in arrs],
        scratch_shapes=[pltpu.SemaphoreType.DMA((3 * n,)), pltpu.SemaphoreType.DMA((3 * n,)), pltpu.SemaphoreType.DMA((3 * ns,)),
                        pltpu.SemaphoreType.DMA((3 * ns,))],
    )(*arrs)
    me_xy = 2 * lax.axis_index("x") + lax.axis_index("y")
    return [_own(o, a, me_xy) for o, a in zip(outs, arrs)]


def _gather_all(a, swap, name):
    masks = [(mx, my, mc) for mx in range(2) for my in range(2) for mc in range(2)][1:]
    n = len(swap)

    def body(*refs):
        in_ref, sw_in = refs[0], refs[1:1 + n]
        out_ref, sw_out = refs[1 + n], refs[2 + n:2 + 2 * n]
        send_sems, recv_sems = refs[2 + 2 * n:]
        x, y, c = _coords()
        me = 4 * x + 2 * y + c
        sends = []
        for j, (mx, my, mc) in enumerate(masks):
            rc = _remote(in_ref, out_ref.at[me], send_sems.at[j], recv_sems.at[j], (_flip(x, mx), _flip(y, my), _flip(c, mc)))
            rc.start()
            sends.append(rc)
        for k in range(n):
            rc = _remote(sw_in[k], sw_out[k], send_sems.at[7 + k], recv_sems.at[7 + k], (x, y, 1 - c))
            rc.start()
            sends.append(rc)
        for j, (mx, my, mc) in enumerate(masks):
            px, py, pc = _flip(x, mx), _flip(y, my), _flip(c, mc)
            landed = out_ref.at[4 * px + 2 * py + pc]
            _remote(landed, landed, send_sems.at[j], recv_sems.at[j], (px, py, pc)).wait_recv()
        for k in range(n):
            _remote(sw_out[k], sw_out[k], send_sems.at[7 + k], recv_sems.at[7 + k], (x, y, 1 - c)).wait_recv()
        for rc in sends:
            rc.wait_send()

    res = _pcall(
        body, name=name, in_specs=[ANY] * (1 + n), out_specs=[ANY] * (1 + n),
        out_shape=[SDS((8,) + a.shape, a.dtype)] + [SDS(s_.shape, s_.dtype) for s_ in swap],
        scratch_shapes=[pltpu.SemaphoreType.DMA((7 + n,)), pltpu.SemaphoreType.DMA((7 + n,))],
    )(a, *swap)
    return _own(res[0], a, 4 * lax.axis_index("x") + 2 * lax.axis_index("y") + lax.axis_index("c")), list(res[1:])


def _half_shape(shape, axis):
    return tuple(d // 2 if i == axis else d for i, d in enumerate(shape))


def _swap_half_c(arrs, axes, name):
    n = len(arrs)

    def body(*refs):
        ins, outs = refs[:n], refs[n:2 * n]
        send_sems, recv_sems = refs[2 * n:]
        x, y, c = _coords()
        sends = []
        for k in range(n):
            rc = _remote(ins[k].at[_half_idx(arrs[k].shape, axes[k], 1 - c)], outs[k], send_sems.at[k], recv_sems.at[k],
                         (x, y, 1 - c))
            rc.start()
            sends.append(rc)
        for rc in sends:
            rc.wait()

    return _pcall(
        body, name=name, in_specs=[ANY] * n, out_specs=[ANY] * n,
        out_shape=[SDS(_half_shape(a.shape, ax), a.dtype) for a, ax in zip(arrs, axes)],
        scratch_shapes=[pltpu.SemaphoreType.DMA((n,)), pltpu.SemaphoreType.DMA((n,))],
    )(*arrs)


def _pair_sum(a, got, cidx, axis, name):
    _, r, cdim = a.shape
    hshape = _half_shape(a.shape, axis)

    def body(c_ref, a_ref, g_ref, o_ref):
        o_ref[...] = (a_ref[...].astype(F32) + g_ref[...].astype(F32)).astype(BF16)

    if axis == 1:
        tr = min(r // 2, 256)
        nj = (r // 2) // tr
        blk = pl.BlockSpec((1, tr, cdim), lambda s, j, c: (s, j, 0))
        a_spec = pl.BlockSpec((1, tr, cdim), lambda s, j, c: (s, c[0] * nj + j, 0))
    else:
        nj, hw = 1, cdim // 2
        blk = pl.BlockSpec((1, r, hw), lambda s, j, c: (s, 0, 0))
        a_spec = pl.BlockSpec((1, r, hw), lambda s, j, c: (s, 0, c[0]))
    return _pcall(
        body, name=name, out_shape=SDS(hshape, BF16),
        grid_spec=pltpu.PrefetchScalarGridSpec(num_scalar_prefetch=1, grid=(4, nj), in_specs=[a_spec, blk], out_specs=blk),
        compiler_params=_cp(("parallel", "parallel"), VMEM_BIG),
    )(cidx, a, got)


def _sum_chips(parts, name):
    _, h, cdim = parts.shape

    def body(p_ref, o_ref):
        acc = p_ref[0].astype(F32)
        for k in range(1, 4):
            acc = acc + p_ref[k].astype(F32)
        o_ref[...] = acc

    if h % 256 == 0 or h in (128,):
        tr = min(h, 256)
        grid, in_spec, out_spec = (h // tr,), pl.BlockSpec((4, tr, cdim), lambda i: (0, i, 0)), pl.BlockSpec((tr, cdim), lambda i: (i, 0))
    else:
        lw = 256
        grid, in_spec, out_spec = (cdim // lw,), pl.BlockSpec((4, h, lw), lambda i: (0, 0, i)), pl.BlockSpec((h, lw), lambda i: (0, i))
    return _pcall(
        body, name=name, grid=grid, in_specs=[in_spec], out_specs=out_spec, out_shape=SDS((h, cdim), F32),
        compiler_params=_cp(("parallel",), VMEM_BIG),
    )(parts)


def _sum_slots(a, name, rows):
    s, n, _ = a.shape

    def body(a_ref, o_ref):
        acc = a_ref[0]
        for k in range(1, s):
            acc = acc + a_ref[k]
        o_ref[...] = acc

    return _pcall(
        body, name=name, grid=(n // rows,), in_specs=[pl.BlockSpec((s, rows, 128), lambda i: (0, i, 0))],
        out_specs=pl.BlockSpec((rows, 128), lambda i: (i, 0)), out_shape=SDS((n, 128), F32),
        compiler_params=_cp(("parallel",)),
    )(a)


def _adam_math(w, g, m, v):
    nm = ADAM_B1 * m + (1.0 - ADAM_B1) * g
    nv = ADAM_B2 * v + (1.0 - ADAM_B2) * (g * g)
    m_hat = nm / (1.0 - ADAM_B1 ** ADAM_STEP)
    v_hat = nv / (1.0 - ADAM_B2 ** ADAM_STEP)
    return -ADAM_LR * (m_hat / (jnp.sqrt(v_hat) + ADAM_EPS) + ADAM_WD * w), nm, nv


def _adamw(w, g, m, v, name, rows):
    r, cdim = w.shape

    def body(w_ref, g_ref, m_ref, v_ref, d_ref, nm_ref, nv_ref):
        d_ref[...], nm_ref[...], nv_ref[...] = _adam_math(w_ref[...], g_ref[...], m_ref[...], v_ref[...])

    blk = pl.BlockSpec((rows, cdim), lambda i: (i, 0))
    return _pcall(
        body, name=name, grid=(r // rows,), in_specs=[blk] * 4, out_specs=[blk] * 3,
        out_shape=[SDS(w.shape, F32)] * 3, compiler_params=_cp(("parallel",)),
    )(w, g, m, v)


def _adamw_joined(w, mine, other, m, v, cidx, axis, name, rows):
    r, cdim = w.shape
    if axis == 0:
        rows = r

    def body(c_ref, w_ref, a_ref, b_ref, m_ref, v_ref, g_ref, d_ref, nm_ref, nv_ref):
        a, b = a_ref[...], b_ref[...]
        g = jnp.where(c_ref[0] == 0, jnp.concatenate([a, b], axis=axis), jnp.concatenate([b, a], axis=axis))
        g_ref[...] = g
        d_ref[...], nm_ref[...], nv_ref[...] = _adam_math(w_ref[...], g, m_ref[...], v_ref[...])

    blk = pl.BlockSpec((rows, cdim), lambda i, c: (i, 0))
    hshape = (rows // 2, cdim) if axis == 0 else (rows, cdim // 2)
    hblk = pl.BlockSpec(hshape, lambda i, c: (i, 0))
    return _pcall(
        body, name=name, out_shape=[SDS(w.shape, F32)] * 4,
        grid_spec=pltpu.PrefetchScalarGridSpec(num_scalar_prefetch=1, grid=(r // rows,), in_specs=[blk, hblk, hblk, blk, blk],
                                               out_specs=[blk] * 4),
        compiler_params=_cp(("parallel",)),
    )(cidx, w, mine, other, m, v)


def _adamw_many(ws, gs, ms, vs, name):
    n = len(ws)

    def body(*refs):
        outs = refs[4 * n:]
        for k in range(n):
            d, nm, nv = _adam_math(refs[k][...], refs[n + k][...], refs[2 * n + k][...], refs[3 * n + k][...])
            outs[k][...] = d
            outs[n + k][...] = nm
            outs[2 * n + k][...] = nv

    res = _pcall(body, name=name, out_shape=[SDS(w.shape, F32) for w in ws] * 3)(*ws, *gs, *ms, *vs)
    return res[:n], res[n:2 * n], res[2 * n:]


def _pack(pieces, rows):
    flat = jnp.concatenate([p.reshape(-1) for p in pieces])
    return jnp.pad(flat, (0, rows * 128 - flat.shape[0])).reshape(rows, 128)


def _unpack(buf, shapes):
    flat = buf.reshape(-1)
    out, off = [], 0
    for shp in shapes:
        size = 1
        for s in shp:
            size *= s
        out.append(flat[off:off + size].reshape(shp))
        off += size
    return out


LATE_ROWS = 32


def kernel(x, c, ctx, c_ctx, w_mod, b_mod, norm_g, w_in, a_ln_g, a_ln_b, a_ws, a_bs, b_gate_w2, b_gate_b, b_norm_g, w_proj_a, w_proj_b, w_out, final_norm_g, loss_target, m_c_ctx, m_w_mod, m_b_mod, m_norm_g, m_w_in, m_a_ln_g, m_a_ln_b, m_a_ws, m_a_bs, m_b_gate_w2, m_b_gate_b, m_b_norm_g, m_w_proj_a, m_w_proj_b, m_w_out, m_final_norm_g, v_c_ctx, v_w_mod, v_b_mod, v_norm_g, v_w_in, v_a_ln_g, v_a_ln_b, v_a_ws, v_a_bs, v_b_gate_w2, v_b_gate_b, v_b_norm_g, v_w_proj_a, v_w_proj_b, v_w_out, v_final_norm_g):
    xi, yi, ci = _coords()
    me_xy = 2 * xi + yi

    gate_pack = _pack([b_gate_w2[0], b_gate_b[0]], 24)
    w_in_t, m_w_in_t, v_w_in_t = (jnp.swapaxes(a[0], 0, 1) for a in (w_in, m_w_in, v_w_in))
    g_wit, g_wm, g_gate = _gather_weights([(w_in_t.astype(BF16), 1), (w_mod[0].astype(BF16), 0)], [gate_pack], "gather_weights")
    late_shards = (w_proj_a[0].astype(BF16), w_proj_b[0].astype(BF16), w_out[0].astype(BF16))
    wit_u = g_wit.reshape(4 * 1288, D)
    wit_g = wit_u[3104:5152]
    wit_r = jnp.concatenate([wit_u[1056:1568], wit_u[1568:2080], wit_u[2592:3104], wit_u[2080:2592], wit_u[0:1024]], axis=0)
    wlrt = jnp.pad(wit_u[1024:1056], ((0, LRW - 32), (0, 0)))
    wm = jnp.swapaxes(g_wm, 0, 1).reshape(D, 3 * D)
    gflat = g_gate.reshape(4, 24 * 128)
    w2 = jnp.swapaxes(gflat[:, 0:2048].reshape(4, 2, 16, 64), 0, 2)
    w2 = jnp.swapaxes(w2, 0, 1).reshape(2, 16, 256)
    gb2 = jnp.swapaxes(gflat[:, 2048:2176].reshape(4, 2, 64), 0, 1).reshape(2, 256)

    tags = ["wi", "wpa", "wpb", "wo"]
    half_axes = [2, 1, 1, 1]
    sent = []

    def exchange(g):
        dwr = g["dwit_r"]
        dwit_u = jnp.concatenate([g["dwit_qkv"], g["dwlrt"][0:32], dwr[0:512], dwr[512:1024], dwr[1536:2048], dwr[1024:1536],
                                  g["dwit_g"]], axis=0)
        big = [dwit_u.reshape(4, 1288, D), jnp.swapaxes(g["dwpa"].reshape(512, 4, 256), 0, 1),
               jnp.swapaxes(g["dwpb"].reshape(512, 4, 256), 0, 1), g["dwo"].reshape(4, 256, D)]
        other = _swap_half_c(big, half_axes, "swap_half_in")
        cidx = jnp.reshape(ci, (1,)).astype(jnp.int32)
        sent.extend(_pair_sum(a, o, cidx, ax, "sum_pair_" + t) for a, o, ax, t in zip(big, other, half_axes, tags))
        return sent

    r = _device_step(x[0], c, ctx[0], c_ctx[None], loss_target[0], wm, b_mod, norm_g, wit_g, wit_r, wlrt, a_ln_g, a_ln_b,
                     a_ws[0], a_bs[0], w2, gb2, b_norm_g, None, None, None, final_norm_g[None], exchange, late_shards)

    parts = [_own(g, lax.dynamic_index_in_dim(s_, me_xy, axis=0, keepdims=False), me_xy) for g, s_ in zip(r["got"], sent)]
    halves = [_sum_chips(p_, "sum_chips_" + t) for p_, t in zip(parts, tags)]

    me8 = 4 * xi + 2 * yi + ci
    early_all = _own(r["early_all"], r["early"], me8)
    late = _pack([r["dshift"], r["dscale"], r["dng_lat"], c], LATE_ROWS)
    late_all, others = _gather_all(late, halves, "gather_small")
    s_early = _sum_slots(early_all, "sum_early", EARLY_ROWS // 3)
    s_late = _sum_slots(late_all, "sum_late", LATE_ROWS)
    (s_dmodc, s_dscc, s_dng_c, s_dlng, s_dlnb, s_dws, s_dbs, s_dgbn, s_dgf, s_dw2, s_dgb2, s_loss, s_dgate) = _unpack(
        s_early, EARLY_SHAPES)
    s_dshift, s_dscale, s_dng_lat, _ = _unpack(s_late, [(1, D)] * 4)
    s_dng = s_dng_lat + s_dng_c
    s_dmod = jnp.concatenate([s_dshift, s_dscale, s_dgate], axis=1)
    loss = s_loss[0]
    s_dmodc_p = jnp.pad(s_dmodc, ((0, 0), (0, D)))
    g_b_mod = s_dmod + s_dmodc_p
    sg = jax.nn.sigmoid(c_ctx)
    g_c_ctx = s_dscc * (sg * (1.0 + c_ctx * (1.0 - sg)))
    g_w2 = lax.dynamic_slice_in_dim(s_dw2, 64 * me_xy, 64, axis=2)[None]
    g_gb2 = lax.dynamic_slice_in_dim(s_dgb2, 64 * me_xy, 64, axis=1)[None]

    flat_l = late_all.reshape(8, LATE_ROWS * 128)
    dgate_all = early_all.reshape(8, EARLY_ROWS * 128)[:, EARLY_SIZE - D:EARLY_SIZE]
    dmod_all = jnp.concatenate([flat_l[:, 0:2 * D], dgate_all], axis=1)
    c_all = flat_l[:, 3 * D:4 * D]
    lhs = jnp.concatenate([_silu(c_all), _silu(c_ctx)[None], jnp.zeros((7, D), F32)], axis=0)
    rhs = jnp.concatenate([dmod_all, s_dmodc_p, jnp.zeros((7, 3 * D), F32)], axis=0)
    rhs = lax.dynamic_slice_in_dim(rhs, 768 * me_xy, 768, axis=1)
    g_w_mod = _mm(lhs.T.astype(BF16), rhs.astype(BF16), tm=D, tn=768, tk=16, out_dtype=F32, name="mm_dwm")

    cidx = jnp.reshape(ci, (1,)).astype(jnp.int32)
    g_w_in_t, d_w_in_t, nm_w_in_t, nv_w_in_t = _adamw_joined(w_in_t, halves[0], others[0], m_w_in_t, v_w_in_t, cidx, 1,
                                                             "adamw_w_in", 184)
    g_w_in, d_w_in, nm_w_in, nv_w_in = (jnp.swapaxes(a, 0, 1) for a in (g_w_in_t, d_w_in_t, nm_w_in_t, nv_w_in_t))
    g_wpa, d_wpa, nm_wpa, nv_wpa = _adamw_joined(w_proj_a[0], halves[1], others[1], m_w_proj_a[0], v_w_proj_a[0], cidx, 0,
                                                 "adamw_wpa", 0)
    g_wpb, d_wpb, nm_wpb, nv_wpb = _adamw_joined(w_proj_b[0], halves[2], others[2], m_w_proj_b[0], v_w_proj_b[0], cidx, 0,
                                                 "adamw_wpb", 0)
    g_wo, d_wo, nm_wo, nv_wo = _adamw_joined(w_out[0], halves[3], others[3], m_w_out[0], v_w_out[0], cidx, 0, "adamw_wo", 0)
    d_w_mod, nm_w_mod, nv_w_mod = _adamw(w_mod[0], g_w_mod, m_w_mod[0], v_w_mod[0], "adamw_w_mod", 256)

    names = ["c_ctx", "b_mod", "norm_g", "a_ln_g", "a_ln_b", "a_ws", "a_bs", "b_gate_w2", "b_gate_b", "b_norm_g", "final_norm_g"]
    ws_ = [c_ctx, b_mod, norm_g, a_ln_g, a_ln_b, a_ws, a_bs, b_gate_w2, b_gate_b, b_norm_g, final_norm_g]
    gs_ = [g_c_ctx, g_b_mod, s_dng, s_dlng, s_dlnb, s_dws, s_dbs, g_w2, g_gb2, s_dgbn, s_dgf]
    ms_ = [m_c_ctx, m_b_mod, m_norm_g, m_a_ln_g, m_a_ln_b, m_a_ws, m_a_bs, m_b_gate_w2, m_b_gate_b, m_b_norm_g, m_final_norm_g]
    vs_ = [v_c_ctx, v_b_mod, v_norm_g, v_a_ln_g, v_a_ln_b, v_a_ws, v_a_bs, v_b_gate_w2, v_b_gate_b, v_b_norm_g, v_final_norm_g]
    shapes = [w.shape for w in ws_]
    flat2 = [(1, 1024), (1, 3072), (1, 1024), (1, 512), (1, 512), (512, 128), (4, 128), (32, 64), (2, 64), (1, 512), (1, 1024)]
    as2d = lambda arrs: [a.reshape(s) for a, s in zip(arrs, flat2)]
    d_s, nm_s, nv_s = _adamw_many(as2d(ws_), as2d(gs_), as2d(ms_), as2d(vs_), "adamw_small")
    d_small = {n: a.reshape(s) for n, a, s in zip(names, d_s, shapes)}
    nm_small = {n: a.reshape(s) for n, a, s in zip(names, nm_s, shapes)}
    nv_small = {n: a.reshape(s) for n, a, s in zip(names, nv_s, shapes)}
    g_small = {n: g.reshape(s) for n, g, s in zip(names, gs_, shapes)}

    order = ["c_ctx", "w_mod", "b_mod", "norm_g", "w_in", "a_ln_g", "a_ln_b", "a_ws", "a_bs", "b_gate_w2", "b_gate_b", "b_norm_g",
             "w_proj_a", "w_proj_b", "w_out", "final_norm_g"]
    big_g = dict(w_mod=g_w_mod[None], w_in=g_w_in[None], w_proj_a=g_wpa[None], w_proj_b=g_wpb[None], w_out=g_wo[None])
    big_d = dict(w_mod=d_w_mod[None], w_in=d_w_in[None], w_proj_a=d_wpa[None], w_proj_b=d_wpb[None], w_out=d_wo[None])
    big_m = dict(w_mod=nm_w_mod[None], w_in=nm_w_in[None], w_proj_a=nm_wpa[None], w_proj_b=nm_wpb[None], w_out=nm_wo[None])
    big_v = dict(w_mod=nv_w_mod[None], w_in=nv_w_in[None], w_proj_a=nv_wpa[None], w_proj_b=nv_wpb[None], w_out=nv_wo[None])
    grads = [big_g[n] if n in big_g else g_small[n] for n in order]
    deltas = [big_d[n] if n in big_d else d_small[n] for n in order]
    new_m = [big_m[n] if n in big_m else nm_small[n] for n in order]
    new_v = [big_v[n] if n in big_v else nv_small[n] for n in order]
    return (loss, r["dx"][None], *grads, *deltas, *new_m, *new_v)
```

```python
import jax
import jax.numpy as jnp
from jax import lax
from jax.experimental import pallas as pl
from jax.experimental.pallas import tpu as pltpu

F32 = jnp.float32
BF16 = jnp.bfloat16
SDS = jax.ShapeDtypeStruct

D = 1024
NP = 5120
LRW = 128
CH = 64
AC = 128
EPS = 1e-6
TOK = 512
GLA_TB = 1024
VMEM_BIG = 48 * 1024 * 1024

ADAM_LR, ADAM_B1, ADAM_B2, ADAM_EPS, ADAM_WD, ADAM_STEP = 0.001, 0.9, 0.999, 1e-08, 0.01, 10

_pcall = pl.pallas_call
MESH = pl.DeviceIdType.MESH


def _cp(sem=None, vmem=None):
    kw = {}
    if sem is not None:
        kw["dimension_semantics"] = sem
    if vmem is not None:
        kw["vmem_limit_bytes"] = vmem
    return pltpu.CompilerParams(**kw)


def _silu(x):
    return x * jax.nn.sigmoid(x)


def _silu_and_grad(x):
    s = jax.nn.sigmoid(x)
    return x * s, s * (1.0 + x * (1.0 - s))


def _logsig(x):
    return jnp.minimum(x, 0.0) - jnp.log1p(jnp.exp(-jnp.abs(x)))


def _nt(a, b):
    return lax.dot_general(a, b, (((1,), (1,)), ((), ())), preferred_element_type=F32)


def _tn(a, b):
    return lax.dot_general(a, b, (((0,), (0,)), ((), ())), preferred_element_type=F32)


def _nn(a, b):
    return jnp.dot(a, b, preferred_element_type=F32)


def _full(shape):
    return pl.BlockSpec(shape, lambda *_: (0,) * len(shape))


def _mm(a, b, *, tm, tn, tk, out_dtype, name, acc=None, b_t=False):
    m, k = a.shape
    n, k2 = (b.shape if b_t else b.shape[::-1])
    assert k == k2 == tk and m % tm == 0 and n % tn == 0, (a.shape, b.shape, tm, tn, tk)
    has_acc = acc is not None

    def body(*refs):
        if has_acc:
            a_ref, b_ref, c_ref, o_ref = refs
        else:
            a_ref, b_ref, o_ref = refs
        part = (_nt if b_t else _nn)(a_ref[...].astype(BF16), b_ref[...].astype(BF16))
        o_ref[...] = ((c_ref[...] + part) if has_acc else part).astype(out_dtype)

    b_spec = pl.BlockSpec((tn, tk), lambda i, j: (j, 0)) if b_t else pl.BlockSpec((tk, tn), lambda i, j: (0, j))
    in_specs = [pl.BlockSpec((tm, tk), lambda i, j: (i, 0)), b_spec]
    args = [a, b]
    if has_acc:
        in_specs.append(pl.BlockSpec((tm, tn), lambda i, j: (i, j)))
        args.append(acc)
    return _pcall(
        body, name=name, grid=(m // tm, n // tn), in_specs=in_specs, out_specs=pl.BlockSpec((tm, tn), lambda i, j: (i, j)),
        out_shape=SDS((m, n), out_dtype), compiler_params=_cp(("parallel", "parallel"), VMEM_BIG),
    )(*args)


def _mm_tn(a, b, *, ta, tn, tk, name, acc=None, out_dtype=F32, swap=None):
    m, ka = a.shape
    m2, n = b.shape
    assert m == m2 and ka % ta == 0 and n % tn == 0 and m % tk == 0, (a.shape, b.shape, ta, tn, tk)
    nk = m // tk
    has_acc = acc is not None
    sw_arrs, sw_axes = swap if swap is not None else ((), ())
    ns = len(sw_arrs)
    n_in = 2 + has_acc
    grid = (ka // ta, n // tn, nk)

    def body(*refs):
        a_ref, b_ref = refs[:2]
        c_ref = refs[2] if has_acc else None
        sw_in = refs[n_in:n_in + ns]
        o_ref = refs[n_in + ns]
        sw_out = refs[n_in + ns + 1:n_in + 2 * ns + 1]
        acc_ref = refs[n_in + 2 * ns + 1]
        sems = refs[n_in + 2 * ns + 2:]
        kk = pl.program_id(2)

        def copies():
            x, y, c = _coords()
            return [_remote(sw_in[k].at[_half_idx(sw_arrs[k].shape, sw_axes[k], 1 - c)], sw_out[k], sems[0].at[k],
                            sems[1].at[k], (x, y, 1 - c)) for k in range(ns)]

        step = (pl.program_id(0) * grid[1] + pl.program_id(1)) * nk + kk
        if ns:
            @pl.when(step == 0)
            def _():
                for rc in copies():
                    rc.start()

        part = _tn(a_ref[...].astype(BF16), b_ref[...].astype(BF16))

        @pl.when(kk == 0)
        def _():
            if has_acc:
                acc_ref[...] = c_ref[...].astype(F32) + part
            else:
                acc_ref[...] = part

        @pl.when(kk > 0)
        def _():
            acc_ref[...] += part

        @pl.when(kk == nk - 1)
        def _():
            o_ref[...] = acc_ref[...].astype(out_dtype)

        if ns:
            @pl.when(step == grid[0] * grid[1] * nk - 1)
            def _():
                for rc in copies():
                    rc.wait()

    in_specs = [pl.BlockSpec((tk, ta), lambda i, j, kk: (kk, i)), pl.BlockSpec((tk, tn), lambda i, j, kk: (kk, j))]
    args = [a, b]
    if has_acc:
        in_specs.append(pl.BlockSpec((ta, tn), lambda i, j, kk: (i, j)))
        args.append(acc)
    out_spec, out_shape = pl.BlockSpec((ta, tn), lambda i, j, kk: (i, j)), SDS((ka, n), out_dtype)
    scratch = [pltpu.VMEM((ta, tn), F32)]
    if not ns:
        return _pcall(body, name=name, grid=grid, in_specs=in_specs, out_specs=out_spec, out_shape=out_shape,
                      scratch_shapes=scratch, compiler_params=_cp(("parallel", "parallel", "arbitrary"), VMEM_BIG))(*args)
    res = _pcall(
        body, name=name, grid=grid, in_specs=in_specs + [ANY] * ns, out_specs=[out_spec] + [ANY] * ns,
        out_shape=[out_shape] + [SDS(_half_shape(s.shape, ax), s.dtype) for s, ax in zip(sw_arrs, sw_axes)],
        scratch_shapes=scratch + [pltpu.SemaphoreType.DMA((ns,)), pltpu.SemaphoreType.DMA((ns,))],
        compiler_params=_cp(("arbitrary", "arbitrary", "arbitrary"), VMEM_BIG),
    )(*args, *sw_arrs)
    return res[0], list(res[1:])


def _modvec(cc, wm, bm):
    def body(c_ref, w_ref, b_ref, o_ref):
        o_ref[...] = _nn(_silu(c_ref[...]).astype(BF16), w_ref[...]) + b_ref[...]

    return _pcall(body, name="modvec", out_shape=SDS((8, 3 * D), F32), compiler_params=_cp(None, VMEM_BIG))(cc, wm, bm)


def _dcctx(dmodc, wm):
    def body(d_ref, w_ref, o_ref):
        o_ref[...] = _nt(d_ref[...].astype(BF16), w_ref[...])

    return _pcall(
        body, name="dcctx", grid=(1,), in_specs=[_full((8, 2 * D)), pl.BlockSpec((D, 2 * D), lambda i: (0, 0))],
        out_specs=_full((8, D)), out_shape=SDS((8, D), F32), compiler_params=_cp(("arbitrary",), VMEM_BIG),
    )(dmodc, wm)


def _prep_h(x, ng, scale, shift, name):
    m = x.shape[0]

    def body(x_ref, g_ref, sc_ref, sh_ref, h_ref):
        xf = x_ref[...]
        r = lax.rsqrt(jnp.mean(xf * xf, axis=-1, keepdims=True) + EPS)
        y = (xf * r) * g_ref[...]
        h_ref[...] = (y * (1.0 + sc_ref[...]) + sh_ref[...]).astype(BF16)

    tok = min(TOK, m)
    row = pl.BlockSpec((tok, D), lambda i: (i, 0))
    return _pcall(
        body, name=name, grid=(m // tok,), in_specs=[row, _full((1, D)), _full((1, D)), _full((1, D))],
        out_specs=row, out_shape=SDS((m, D), BF16), compiler_params=_cp(("parallel",)),
    )(x, ng, scale, shift)


def _resident(shape):
    return pl.BlockSpec(shape, lambda *_: (0,) * len(shape), pipeline_mode=pl.Buffered(1))


PROJ_TM = 512


def _proj_fwd(x, ng, scale, shift, wit_g, wit_r, wlrt, ln_g, ln_b, share=()):
    m = x.shape[0]
    ns = len(share)
    steps = m // PROJ_TM
    src = [(0, 0), (0, D), (1, 2 * D), (1, 0), (1, D)]

    def body(*refs):
        x_ref, g_ref, sc_ref, sh_ref, wg_ref, wr_ref, wl_ref, lg_ref, lb_ref = refs[:9]
        share_refs = refs[9:9 + ns]
        h_ref, p_ref, plr_ref, vr_ref, vc_ref = refs[9 + ns:14 + ns]
        got_refs = refs[14 + ns:14 + 2 * ns]
        sems = refs[14 + 2 * ns:]

        def copies():
            cx, cy, cc = _coords()
            me = 2 * cx + cy
            peers = [(1 - cx, cy), (cx, 1 - cy), (1 - cx, 1 - cy)]
            out, back = [], []
            for k in range(ns):
                for j, (px, py) in enumerate(peers):
                    out.append(_remote(share_refs[k], got_refs[k].at[me], sems[0].at[3 * k + j], sems[1].at[3 * k + j], (px, py, cc)))
                    landed = got_refs[k].at[2 * px + py]
                    back.append(_remote(landed, landed, sems[0].at[3 * k + j], sems[1].at[3 * k + j], (px, py, cc)))
            return out, back

        if ns:
            @pl.when(pl.program_id(0) == 0)
            def _():
                for rc in copies()[0]:
                    rc.start()

            @pl.when(pl.program_id(0) == steps - 1)
            def _():
                out, back = copies()
                for rc in back:
                    rc.wait_recv()
                for rc in out:
                    rc.wait_send()

        xf = x_ref[...]
        r = lax.rsqrt(jnp.mean(xf * xf, axis=-1, keepdims=True) + EPS)
        y = (xf * r) * g_ref[...]
        h = (y * (1.0 + sc_ref[...]) + sh_ref[...]).astype(BF16)
        h_ref[...] = h
        for j, (which, r0) in enumerate(src):
            w_ref = wr_ref if which else wg_ref
            blk = _nt(h, w_ref[r0:r0 + D, :]).astype(BF16)
            p_ref[:, D * j:D * j + D] = blk
            if j == 4:
                xf = blk[:, 512:1024].astype(F32)
                xc = xf - jnp.mean(xf, axis=-1, keepdims=True)
                vn = (xc * lax.rsqrt(jnp.mean(xc * xc, axis=-1, keepdims=True) + EPS)) * lg_ref[...] + lb_ref[...]
                vr_ref[...] = vn[:, 0:256].astype(BF16)
                vc_ref[0] = vn[:, 256:384].astype(BF16)
                vc_ref[1] = vn[:, 384:512].astype(BF16)
        plr_ref[...] = _nt(h, wl_ref[...])

    row = pl.BlockSpec((PROJ_TM, D), lambda i: (i, 0))
    vec = _full((1, D))
    res = _pcall(
        body, name="proj_fwd", grid=(steps,),
        in_specs=[row, vec, vec, vec, _resident((2 * D, D)), _resident((3 * D, D)), _resident((LRW, D)), _full((1, 512)),
                  _full((1, 512))] + [ANY] * ns,
        out_specs=[row, pl.BlockSpec((PROJ_TM, NP), lambda i: (i, 0)), pl.BlockSpec((PROJ_TM, LRW), lambda i: (i, 0)),
                   pl.BlockSpec((PROJ_TM, 256), lambda i: (i, 0)), pl.BlockSpec((2, PROJ_TM, 128), lambda i: (0, i, 0))] + [ANY] * ns,
        out_shape=[SDS((m, D), BF16), SDS((m, NP), BF16), SDS((m, LRW), F32), SDS((m, 256), BF16), SDS((2, m, 128), BF16)]
        + [SDS((4,) + a.shape, a.dtype) for a in share],
        scratch_shapes=([pltpu.SemaphoreType.DMA((3 * ns,)), pltpu.SemaphoreType.DMA((3 * ns,))] if ns else []),
        compiler_params=_cp(("arbitrary",), VMEM_BIG),
    )(x, ng, scale, shift, wit_g, wit_r, wlrt, ln_g, ln_b, *share)
    return res[0], res[1], res[2], res[3], res[4], list(res[5:])


def _proj_bwd(dp_g, dp_r, dlr, wit_g, wit_r, wlrt, x, dx1, ng, scale, send=(), share8=None):
    m = x.shape[0]
    ns = len(send)
    n8 = 0 if share8 is None else 1
    steps = m // PROJ_TM
    masks = [(mx, my, mc) for mx in range(2) for my in range(2) for mc in range(2)][1:]

    def body(*refs):
        (dpg_ref, dpr_ref, dlr_ref, wg_ref, wr_ref, wl_ref, x_ref, r_ref, g_ref, sc_ref) = refs[:10]
        send_refs = refs[10:10 + ns]
        n_in = 10 + ns + n8
        dx_ref, dg_ref, dsc_ref, dsh_ref = refs[n_in:n_in + 4]
        got_refs = refs[n_in + 4:n_in + 4 + ns]
        sems = refs[n_in + 4 + ns + n8:]
        i = pl.program_id(0)

        def copies():
            cx, cy, cc = _coords()
            me = 2 * cx + cy
            peers = [(1 - cx, cy), (cx, 1 - cy), (1 - cx, 1 - cy)]
            out, back = [], []
            for k in range(ns):
                for j, (px, py) in enumerate(peers):
                    out.append(_remote(send_refs[k].at[2 * px + py], got_refs[k].at[me], sems[0].at[3 * k + j],
                                       sems[1].at[3 * k + j], (px, py, cc)))
                    landed = got_refs[k].at[2 * px + py]
                    back.append(_remote(landed, landed, sems[0].at[3 * k + j], sems[1].at[3 * k + j], (px, py, cc)))
            if n8:
                src8, all8 = refs[10 + ns], refs[n_in + 4 + ns]
                s8, r8 = sems[-2], sems[-1]
                for j, (mx, my, mc) in enumerate(masks):
                    px, py, pc = _flip(cx, mx), _flip(cy, my), _flip(cc, mc)
                    out.append(_remote(src8, all8.at[4 * cx + 2 * cy + cc], s8.at[j], r8.at[j], (px, py, pc)))
                    landed = all8.at[4 * px + 2 * py + pc]
                    back.append(_remote(landed, landed, s8.at[j], r8.at[j], (px, py, pc)))
            return out, back

        @pl.when(i == 0)
        def _():
            dg_ref[...] = jnp.zeros_like(dg_ref)
            dsc_ref[...] = jnp.zeros_like(dsc_ref)
            dsh_ref[...] = jnp.zeros_like(dsh_ref)
            if ns or n8:
                for rc in copies()[0]:
                    rc.start()

        dh_ = (_nn(dpg_ref[...], wg_ref[...]) + _nn(dpr_ref[...], wr_ref[...])
               + _nn(dlr_ref[...].astype(BF16), wl_ref[...]))
        xf = x_ref[...]
        r = lax.rsqrt(jnp.mean(xf * xf, axis=-1, keepdims=True) + EPS)
        xh = xf * r
        y = xh * g_ref[...]
        dsh_ref[...] += jnp.sum(dh_, axis=0, keepdims=True)
        dsc_ref[...] += jnp.sum(dh_ * y, axis=0, keepdims=True)
        dy = dh_ * (1.0 + sc_ref[...])
        dg_ref[...] += jnp.sum(dy * xh, axis=0, keepdims=True)
        dxh = dy * g_ref[...]
        dx_ref[...] = r * (dxh - xh * jnp.mean(dxh * xh, axis=-1, keepdims=True)) + r_ref[...]

        if ns or n8:
            @pl.when(i == steps - 1)
            def _():
                out, back = copies()
                for rc in back:
                    rc.wait_recv()
                for rc in out:
                    rc.wait_send()

    row = pl.BlockSpec((PROJ_TM, D), lambda i: (i, 0))
    vec = _full((1, D))
    kg, kr = dp_g.shape[1], dp_r.shape[1]
    extra_in = list(send) + ([share8] if n8 else [])
    extra_out = [SDS(a.shape, a.dtype) for a in send] + ([SDS((8,) + share8.shape, share8.dtype)] if n8 else [])
    res = _pcall(
        body, name="proj_bwd", grid=(steps,),
        in_specs=[pl.BlockSpec((PROJ_TM, kg), lambda i: (i, 0)), pl.BlockSpec((PROJ_TM, kr), lambda i: (i, 0)),
                  pl.BlockSpec((PROJ_TM, LRW), lambda i: (i, 0)), _resident((kg, D)), _resident((kr, D)), _resident((LRW, D)),
                  row, row, vec, vec] + [ANY] * len(extra_in),
        out_specs=[row, vec, vec, vec] + [ANY] * len(extra_out),
        out_shape=[SDS((m, D), F32), SDS((1, D), F32), SDS((1, D), F32), SDS((1, D), F32)] + extra_out,
        scratch_shapes=(([pltpu.SemaphoreType.DMA((3 * ns,)), pltpu.SemaphoreType.DMA((3 * ns,))] if ns else [])
                        + ([pltpu.SemaphoreType.DMA((7,)), pltpu.SemaphoreType.DMA((7,))] if n8 else [])),
        compiler_params=_cp(("arbitrary",), VMEM_BIG),
    )(dp_g, dp_r, dlr, wit_g, wit_r, wlrt, x, dx1, ng, scale, *extra_in)
    return tuple(res[:4]), list(res[4:4 + ns]), (res[4 + ns] if n8 else None)


def _prep_bwd(x, dh, dx1, ng, scale, name):
    m = x.shape[0]
    has_res = dx1 is not None

    def body(*refs):
        if has_res:
            x_ref, dh_ref, r_ref, g_ref, sc_ref, dx_ref, dg_ref, dsc_ref, dsh_ref = refs
        else:
            x_ref, dh_ref, g_ref, sc_ref, dx_ref, dg_ref, dsc_ref, dsh_ref = refs
        i = pl.program_id(0)

        @pl.when(i == 0)
        def _():
            dg_ref[...] = jnp.zeros_like(dg_ref)
            dsc_ref[...] = jnp.zeros_like(dsc_ref)
            dsh_ref[...] = jnp.zeros_like(dsh_ref)

        xf = x_ref[...]
        dh_ = dh_ref[...]
        r = lax.rsqrt(jnp.mean(xf * xf, axis=-1, keepdims=True) + EPS)
        xh = xf * r
        y = xh * g_ref[...]
        dsh_ref[...] += jnp.sum(dh_, axis=0, keepdims=True)
        dsc_ref[...] += jnp.sum(dh_ * y, axis=0, keepdims=True)
        dy = dh_ * (1.0 + sc_ref[...])
        dg_ref[...] += jnp.sum(dy * xh, axis=0, keepdims=True)
        dxh = dy * g_ref[...]
        dx = r * (dxh - xh * jnp.mean(dxh * xh, axis=-1, keepdims=True))
        if has_res:
            dx = dx + r_ref[...]
        dx_ref[...] = dx

    tok = min(TOK, m)
    row = pl.BlockSpec((tok, D), lambda i: (i, 0))
    vec = _full((1, D))
    in_specs = [row, row] + ([row] if has_res else []) + [vec, vec]
    args = [x, dh] + ([dx1] if has_res else []) + [ng, scale]
    return _pcall(
        body, name=name, grid=(m // tok,), in_specs=in_specs, out_specs=[row, vec, vec, vec],
        out_shape=[SDS((m, D), F32), SDS((1, D), F32), SDS((1, D), F32), SDS((1, D), F32)],
        compiler_params=_cp(("arbitrary",)),
    )(*args)


COLB = 2048


def _colmix_fwd(vnc, ws23, bs23):
    rows = vnc.shape[2] // COLB

    def body(v_ref, w_ref, b_ref, o_ref):
        o_ref[0] = _nn(w_ref[0], v_ref[0]) + b_ref[0]

    return _pcall(
        body, name="colmix_fwd", grid=(2, rows),
        in_specs=[pl.BlockSpec((1, AC, COLB), lambda g, j: (g, 0, j)), pl.BlockSpec((1, AC, AC), lambda g, j: (g, 0, 0)),
                  pl.BlockSpec((1, AC, 1), lambda g, j: (g, 0, 0))],
        out_specs=pl.BlockSpec((1, AC, COLB), lambda g, j: (g, 0, j)),
        out_shape=SDS(vnc.shape, F32), compiler_params=_cp(("parallel", "parallel")),
    )(vnc, ws23, bs23)


def _colmix_bwd(dsvc, vnc, ws23t):
    rows = vnc.shape[2] // COLB

    def body(d_ref, v_ref, wt_ref, dv_ref, dw_ref, db_ref):
        j = pl.program_id(1)

        @pl.when(j == 0)
        def _():
            dw_ref[...] = jnp.zeros_like(dw_ref)
            db_ref[...] = jnp.zeros_like(db_ref)

        d = d_ref[0]
        d16 = d.astype(BF16)
        dv_ref[0] = _nn(wt_ref[0], d16)
        dw_ref[0] += _nt(d16, v_ref[0])
        db_ref[0] += jnp.sum(d, axis=1, keepdims=True)

    blk = pl.BlockSpec((1, AC, COLB), lambda g, j: (g, 0, j))
    return _pcall(
        body, name="colmix_bwd", grid=(2, rows),
        in_specs=[blk, blk, pl.BlockSpec((1, AC, AC), lambda g, j: (g, 0, 0))],
        out_specs=[blk, pl.BlockSpec((1, AC, AC), lambda g, j: (g, 0, 0)), pl.BlockSpec((1, AC, 1), lambda g, j: (g, 0, 0))],
        out_shape=[SDS(vnc.shape, F32), SDS((2, AC, AC), F32), SDS((2, AC, 1), F32)],
        compiler_params=_cp(("parallel", "arbitrary")),
    )(dsvc, vnc, ws23t)


def _head_norm(o):
    out = []
    for h in range(4):
        oh = o[:, 128 * h:128 * h + 128]
        r = lax.rsqrt(jnp.mean(oh * oh, axis=-1, keepdims=True) + EPS)
        out.append((r, oh * r))
    return out


def _tail_fwd(o_f, o_b, p, vnr, svc, x, tgt, ws01, bs01, gbn, wpa, wpb, wo, gate, gf):
    m = p.shape[0]

    def body(of_ref, ob_ref, zb_ref, ua_ref, za_ref, ga_ref, gb_ref, vnr_ref, svc_ref, x_ref, t_ref, w_ref, b_ref, g_ref,
             wpa_ref, wpb_ref, wo_ref, gate_ref, gf_ref,
             ya_ref, yb_ref, svr_ref, dwo_ref, dx1_ref, dout_ref, loss_ref, dgate_ref, dgf_ref):
        i = pl.program_id(0)

        @pl.when(i == 0)
        def _():
            loss_ref[...] = jnp.zeros_like(loss_ref)
            dgate_ref[...] = jnp.zeros_like(dgate_ref)
            dgf_ref[...] = jnp.zeros_like(dgf_ref)
            dwo_ref[...] = jnp.zeros_like(dwo_ref)

        o = of_ref[...] + ob_ref[...]
        zb = zb_ref[...].astype(F32)
        for h, (r, xh) in enumerate(_head_norm(o)):
            sl = slice(128 * h, 128 * h + 128)
            yb_ref[:, sl] = ((xh * g_ref[:, sl]) * _silu(zb[:, sl])).astype(BF16)
        for j in range(TOK // AC):
            for g in range(2):
                sv = _nn(w_ref[g], vnr_ref[AC * j:AC * j + AC, AC * g:AC * g + AC]) + b_ref[g]
                svr_ref[AC * j:AC * j + AC, AC * g:AC * g + AC] = sv
        sz = _silu(za_ref[...].astype(F32))
        u = ua_ref[...].astype(F32)
        ya_ref[:, 0:256] = ((u[:, 0:256] * svr_ref[...]) * sz[:, 0:256]).astype(BF16)
        ya_ref[:, 256:384] = ((u[:, 256:384] * svc_ref[0]) * sz[:, 256:384]).astype(BF16)
        ya_ref[:, 384:512] = ((u[:, 384:512] * svc_ref[1]) * sz[:, 384:512]).astype(BF16)
        ya = _nn(ya_ref[...], wpa_ref[...])
        yb = _nn(yb_ref[...], wpb_ref[...])
        mg = (jax.nn.sigmoid(ga_ref[...].astype(F32)) * ya + jax.nn.sigmoid(gb_ref[...].astype(F32)) * yb).astype(BF16)
        out_ = _nn(mg, wo_ref[...])
        x1 = x_ref[...] + gate_ref[...] * out_
        r = lax.rsqrt(jnp.mean(x1 * x1, axis=-1, keepdims=True) + EPS)
        xh = x1 * r
        err = xh * gf_ref[...] - t_ref[...]
        loss_ref[...] += 0.5 * jnp.sum(jnp.mean(err * err, axis=-1, keepdims=True), axis=0, keepdims=True)
        dy = err * (1.0 / D)
        dgf_ref[...] += jnp.sum(dy * xh, axis=0, keepdims=True)
        dxh = dy * gf_ref[...]
        dx1 = r * (dxh - xh * jnp.mean(dxh * xh, axis=-1, keepdims=True))
        dx1_ref[...] = dx1
        dout16 = (gate_ref[...] * dx1).astype(BF16)
        dout_ref[...] = dout16
        dgate_ref[...] += jnp.sum(dx1 * out_, axis=0, keepdims=True)
        dwo_ref[...] += _tn(mg, dout16)

    r512 = pl.BlockSpec((TOK, 512), lambda i: (i, 0))
    row = pl.BlockSpec((TOK, D), lambda i: (i, 0))
    vec = _full((1, D))
    return _pcall(
        body, name="tail_fwd", grid=(m // TOK,),
        in_specs=[r512, r512, pl.BlockSpec((TOK, 512), lambda i: (i, 6)), pl.BlockSpec((TOK, 512), lambda i: (i, 7)),
                  pl.BlockSpec((TOK, 512), lambda i: (i, 8)), row, pl.BlockSpec((TOK, D), lambda i: (i, 1)),
                  pl.BlockSpec((TOK, 256), lambda i: (i, 0)), pl.BlockSpec((2, TOK, 128), lambda i: (0, i, 0)), row, row,
                  _full((2, AC, AC)), _full((2, AC, 1)), _full((1, 512)), _resident((512, D)), _resident((512, D)),
                  _resident((D, D)), vec, vec],
        out_specs=[r512, r512, pl.BlockSpec((TOK, 256), lambda i: (i, 0)), _resident((D, D)), row, row, _full((1, 128)), vec, vec],
        out_shape=[SDS((m, 512), BF16), SDS((m, 512), BF16), SDS((m, 256), F32), SDS((D, D), F32), SDS((m, D), F32),
                   SDS((m, D), BF16), SDS((1, 128), F32), SDS((1, D), F32), SDS((1, D), F32)],
        compiler_params=_cp(("arbitrary",), VMEM_BIG),
    )(o_f, o_b, p, p, p, p, p, vnr, svc, x, tgt, ws01, bs01, gbn, wpa, wpb, wo, gate, gf)


DPR = 3072


def _tail_bwd(dout, ya_in, yb_in, p, svr, svc, o_f, o_b, gbn, wo, wpa, wpb):
    m = p.shape[0]

    def body(dout_ref, ya_ref, yb_ref, ga_ref, gb_ref, zb_ref, ua_ref, za_ref, svr_ref, svc_ref, of_ref, ob_ref, g_ref,
             wo_ref, wpa_ref, wpb_ref,
             dwpa_ref, dwpb_ref, dpg_ref, dpr_ref, dsr_ref, dsc_ref, do_ref, dg_ref):
        i = pl.program_id(0)

        @pl.when(i == 0)
        def _():
            dg_ref[...] = jnp.zeros_like(dg_ref)
            dwpa_ref[...] = jnp.zeros_like(dwpa_ref)
            dwpb_ref[...] = jnp.zeros_like(dwpb_ref)

        dm_ = _nt(dout_ref[...], wo_ref[...])
        ya_in, yb_in = ya_ref[...], yb_ref[...]
        ya = _nn(ya_in, wpa_ref[...])
        yb = _nn(yb_in, wpb_ref[...])
        sa = jax.nn.sigmoid(ga_ref[...].astype(F32))
        sb = jax.nn.sigmoid(gb_ref[...].astype(F32))
        dya16 = (dm_ * sa).astype(BF16)
        dyb16 = (dm_ * sb).astype(BF16)
        dwpa_ref[...] += _tn(ya_in, dya16)
        dwpb_ref[...] += _tn(yb_in, dyb16)
        dpg_ref[:, 0:D] = (dm_ * ya * (sa * (1.0 - sa))).astype(BF16)
        dpg_ref[:, D:2 * D] = (dm_ * yb * (sb * (1.0 - sb))).astype(BF16)
        dya = _nt(dya16, wpa_ref[...])
        dyb = _nt(dyb16, wpb_ref[...])

        u = ua_ref[...].astype(F32)
        za = za_ref[...].astype(F32)
        sz, dsz = _silu_and_grad(za)
        sv = jnp.concatenate([svr_ref[...], svc_ref[0], svc_ref[1]], axis=1)
        dpr_ref[:, 512:1024] = (dya * sv * sz).astype(BF16)
        dsv = dya * u * sz
        dsr_ref[...] = dsv[:, 0:256]
        dsc_ref[0] = dsv[:, 256:384]
        dsc_ref[1] = dsv[:, 384:512]
        dpr_ref[:, 1024:1536] = (dya * u * sv * dsz).astype(BF16)

        zb = zb_ref[...].astype(F32)
        o = of_ref[...] + ob_ref[...]
        szb, dszb = _silu_and_grad(zb)
        for h, (r, xh) in enumerate(_head_norm(o)):
            sl = slice(128 * h, 128 * h + 128)
            gh = g_ref[:, sl]
            don = dyb[:, sl] * szb[:, sl]
            dpr_ref[:, sl] = (dyb[:, sl] * (xh * gh) * dszb[:, sl]).astype(BF16)
            dg_ref[:, sl] += jnp.sum(don * xh, axis=0, keepdims=True)
            dxh = don * gh
            do_ref[:, sl] = (r * (dxh - xh * jnp.mean(dxh * xh, axis=-1, keepdims=True))).astype(BF16)

    r512 = pl.BlockSpec((TOK, 512), lambda i: (i, 0))
    row = pl.BlockSpec((TOK, D), lambda i: (i, 0))
    return _pcall(
        body, name="tail_bwd", grid=(m // TOK,),
        in_specs=[row, r512, r512, row, pl.BlockSpec((TOK, D), lambda i: (i, 1)), pl.BlockSpec((TOK, 512), lambda i: (i, 6)),
                  pl.BlockSpec((TOK, 512), lambda i: (i, 7)), pl.BlockSpec((TOK, 512), lambda i: (i, 8)),
                  pl.BlockSpec((TOK, 256), lambda i: (i, 0)), pl.BlockSpec((2, TOK, 128), lambda i: (0, i, 0)), r512, r512,
                  _full((1, 512)), _resident((D, D)), _resident((512, D)), _resident((512, D))],
        out_specs=[_resident((512, D)), _resident((512, D)), pl.BlockSpec((TOK, 2 * D), lambda i: (i, 0)),
                   pl.BlockSpec((TOK, 1536), lambda i: (i, 0)),
                   pl.BlockSpec((TOK, 256), lambda i: (i, 0)), pl.BlockSpec((2, TOK, 128), lambda i: (0, i, 0)), r512, _full((1, 512))],
        out_shape=[SDS((512, D), F32), SDS((512, D), F32), SDS((m, 2 * D), BF16), SDS((m, DPR), BF16), SDS((m, 256), F32),
                   SDS((2, m, 128), F32), SDS((m, 512), BF16), SDS((1, 512), F32)],
        compiler_params=_cp(("arbitrary",), VMEM_BIG),
    )(dout, ya_in, yb_in, p, p, p, p, p, svr, svc, o_f, o_b, gbn, wo, wpa, wpb)


def _ln_bwd(dsr, vnr, dvnc, p, ws01t, ln_g, dp):
    m = p.shape[0]

    def body(dsr_ref, vnr_ref, dvc_ref, va_ref, wt_ref, g_ref, dpi_ref, dp_ref, dw_ref, db_ref, dlg_ref, dlb_ref, dvn_ref):
        i = pl.program_id(0)

        @pl.when(i == 0)
        def _():
            dw_ref[...] = jnp.zeros_like(dw_ref)
            db_ref[...] = jnp.zeros_like(db_ref)
            dlg_ref[...] = jnp.zeros_like(dlg_ref)
            dlb_ref[...] = jnp.zeros_like(dlb_ref)

        for j in range(TOK // AC):
            for g in range(2):
                d = dsr_ref[AC * j:AC * j + AC, AC * g:AC * g + AC]
                d16 = d.astype(BF16)
                dvn_ref[AC * j:AC * j + AC, AC * g:AC * g + AC] = _nn(wt_ref[g], d16)
                dw_ref[g] += _nt(d16, vnr_ref[AC * j:AC * j + AC, AC * g:AC * g + AC])
                db_ref[g] += jnp.sum(d, axis=1, keepdims=True)
        dvn_ref[:, 256:384] = dvc_ref[0]
        dvn_ref[:, 384:512] = dvc_ref[1]
        dvn = dvn_ref[...]
        xf = va_ref[...].astype(F32)
        xc = xf - jnp.mean(xf, axis=-1, keepdims=True)
        rs = lax.rsqrt(jnp.mean(xc * xc, axis=-1, keepdims=True) + EPS)
        xh = xc * rs
        dlg_ref[...] += jnp.sum(dvn * xh, axis=0, keepdims=True)
        dlb_ref[...] += jnp.sum(dvn, axis=0, keepdims=True)
        dxh = dvn * g_ref[...]
        dva = rs * (dxh - jnp.mean(dxh, axis=-1, keepdims=True) - xh * jnp.mean(dxh * xh, axis=-1, keepdims=True))
        dp_ref[...] = dva.astype(BF16)

    return _pcall(
        body, name="ln_bwd", grid=(m // TOK,),
        in_specs=[pl.BlockSpec((TOK, 256), lambda i: (i, 0)), pl.BlockSpec((TOK, 256), lambda i: (i, 0)),
                  pl.BlockSpec((2, TOK, 128), lambda i: (0, i, 0)), pl.BlockSpec((TOK, 512), lambda i: (i, 9)),
                  _full((2, AC, AC)), _full((1, 512)), pl.BlockSpec(memory_space=pl.ANY)],
        out_specs=[pl.BlockSpec((TOK, 512), lambda i: (i, 3)), _full((2, AC, AC)), _full((2, AC, 1)), _full((1, 512)), _full((1, 512))],
        out_shape=[SDS((m, DPR), BF16), SDS((2, AC, AC), F32), SDS((2, AC, 1), F32), SDS((1, 512), F32), SDS((1, 512), F32)],
        scratch_shapes=[pltpu.VMEM((TOK, 512), F32)],
        input_output_aliases={6: 0}, compiler_params=_cp(("arbitrary",)),
    )(dsr, vnr, dvnc, p, ws01t, ln_g, dp)


def _tri_mm(tri, a):
    a1 = a.astype(BF16)
    r1 = a - a1.astype(F32)
    a2 = r1.astype(BF16)
    a3 = (r1 - a2.astype(F32)).astype(BF16)
    n = a.shape[1]
    r = _nn(tri, jnp.concatenate([a1, a2, a3], axis=1))
    return r[:, 0:n] + r[:, n:2 * n] + r[:, 2 * n:3 * n]


def _gla_masks(reverse):
    ri = lax.broadcasted_iota(jnp.int32, (CH, CH), 0)
    ci = lax.broadcasted_iota(jnp.int32, (CH, CH), 1)
    vis = (ci >= ri) if reverse else (ci <= ri)
    vis_t = (ci <= ri) if reverse else (ci >= ri)
    r4 = lax.broadcasted_iota(jnp.int32, (4 * CH, CH), 0) & (CH - 1)
    c4 = lax.broadcasted_iota(jnp.int32, (4 * CH, CH), 1)
    vis4 = (c4 >= r4) if reverse else (c4 <= r4)
    vis4_t = (c4 <= r4) if reverse else (c4 >= r4)
    lane = lax.broadcasted_iota(jnp.int32, (1, 256), 1)
    hm = [(lane >= CH * h) & (lane < CH * h + CH) for h in range(4)]
    return vis, vis_t, vis4, vis4_t, hm


def _stack_heads(x, hm):
    return jnp.concatenate([jnp.where(hm[h], x, 0.0).astype(BF16) for h in range(4)], axis=0)


def _diag_heads(full, hm):
    r = full.shape[0] // 4
    acc = jnp.where(hm[0], full[0:r], 0.0)
    for h in range(1, 4):
        acc = acc + jnp.where(hm[h], full[r * h:r * h + r], 0.0)
    return acc


def _rows_of_heads(x):
    return jnp.concatenate([x[:, 128 * h:128 * h + 128] for h in range(4)], axis=0)


def _lane_vis(reverse, transpose):
    ri = lax.broadcasted_iota(jnp.int32, (CH, 4 * CH), 0)
    ci = lax.broadcasted_iota(jnp.int32, (CH, 4 * CH), 1) & (CH - 1)
    return (ci >= ri) if (reverse != transpose) else (ci <= ri)


def _gla_fwd2(p, qkv_blk, lr, lrws, gbiases, s0s, name):
    m = p.shape[0]
    tb = min(GLA_TB, m)
    nb = m // tb
    nc = tb // CH

    def body(qkv_f, lr_f, qkv_b, lr_b, lrw_f, lrw_b, gb_f, gb_b, s0_f, s0_b,
             o_f, sb_f, sfin_f, o_b, sb_b, sfin_b, st_f, st_b):
        i = pl.program_id(0)

        @pl.when(i == 0)
        def _():
            st_f[...] = s0_f[...]
            st_b[...] = s0_b[...]

        dirs = []
        for reverse, qkv_ref, lr_ref, lrw_ref, gb_ref, o_ref, sb_ref, st_ref in (
                (False, qkv_f, lr_f, lrw_f, gb_f, o_f, sb_f, st_f), (True, qkv_b, lr_b, lrw_b, gb_b, o_b, sb_b, st_b)):
            vis, _, vis4, _, hm = _gla_masks(reverse)
            logits = _nn(lr_ref[...].astype(BF16), lrw_ref[...]) + gb_ref[...]
            dirs.append(dict(reverse=reverse, qkv=qkv_ref, o=o_ref, sb=sb_ref, vis4=vis4, hm=hm,
                             tri=vis.astype(F32).astype(BF16), a=_logsig(logits) * (1.0 / 16.0), st=st_ref[...]))
        for step in range(nc):
            for d in dirs:
                c = nc - 1 - step if d["reverse"] else step
                rows = slice(CH * c, CH * c + CH)
                b = _tri_mm(d["tri"], d["a"][rows])
                bl = b[0:1] if d["reverse"] else b[CH - 1:CH]
                q = d["qkv"][rows, 0:256].astype(F32) * 0.125
                k = d["qkv"][rows, 256:512].astype(F32)
                v16 = d["qkv"][rows, 512:1024].astype(BF16)
                qd = q * jnp.exp(b)
                kd16 = (k * jnp.exp(-b)).astype(BF16)
                kdec16 = (k * jnp.exp(bl - b)).astype(BF16)
                qstack = _stack_heads(qd, d["hm"])
                sc = jnp.where(d["vis4"], _nt(qstack, kd16), 0.0).astype(BF16)
                inter = _nt(qstack, d["st"].astype(BF16))
                for h in range(4):
                    d["o"][rows, 128 * h:128 * h + 128] = (
                        _nn(sc[CH * h:CH * h + CH], v16[:, 128 * h:128 * h + 128]) + inter[CH * h:CH * h + CH])
                d["sb"][c] = d["st"]
                d["st"] = d["st"] * jnp.exp(bl) + _diag_heads(_tn(v16, kdec16), d["hm"])
        st_f[...] = dirs[0]["st"]
        st_b[...] = dirs[1]["st"]

        @pl.when(i == nb - 1)
        def _():
            sfin_f[...] = dirs[0]["st"]
            sfin_b[...] = dirs[1]["st"]

    fw = lambda i: i
    bw = lambda i: nb - 1 - i
    in_specs = []
    for rm in (fw, bw):
        in_specs += [pl.BlockSpec((tb, 1024), lambda i, rm=rm: (rm(i), qkv_blk)), pl.BlockSpec((tb, LRW), lambda i, rm=rm: (rm(i), 0))]
    in_specs += [_full((LRW, 256))] * 2 + [_full((1, 256))] * 2 + [_full((128, 256))] * 2
    out_specs, out_shape = [], []
    for rm in (fw, bw):
        out_specs += [pl.BlockSpec((tb, 512), lambda i, rm=rm: (rm(i), 0)), pl.BlockSpec((nc, 128, 256), lambda i, rm=rm: (rm(i), 0, 0)),
                      _full((128, 256))]
        out_shape += [SDS((m, 512), F32), SDS((m // CH, 128, 256), F32), SDS((128, 256), F32)]
    return _pcall(
        body, name=name, grid=(nb,), in_specs=in_specs, out_specs=out_specs, out_shape=out_shape,
        scratch_shapes=[pltpu.VMEM((128, 256), F32), pltpu.VMEM((128, 256), F32)], compiler_params=_cp(("arbitrary",), VMEM_BIG),
    )(p, lr, p, lr, lrws[0], lrws[1], gbiases[0], gbiases[1], s0s[0], s0s[1])


def _gla_bwd(p, qkv_blk, lr, lrw, lrwt, gbias, sb, dsfin, do, prev, dp, *, reverse, name):
    m = p.shape[0]
    tb = min(GLA_TB, m)
    nb = m // tb
    nc = tb // CH
    rmap = (lambda i: i) if reverse else (lambda i: nb - 1 - i)
    has_prev = prev is not None
    has_dp = dp is not None

    def body(*refs):
        refs = list(refs)
        qkv_ref, lr_ref, lrw_ref, lrwt_ref, gb_ref, sb_ref, dsfin_ref, do_ref = refs[:8]
        refs = refs[8:]
        if has_prev:
            pq_ref, plr_ref = refs[:2]
            refs = refs[2:]
        if has_dp:
            refs = refs[1:]
        dqkv_ref, dlr_ref, dw2_ref, dgb_ref, ds0_ref, dst_ref, dlog_ref = refs
        i = pl.program_id(0)

        @pl.when(i == 0)
        def _():
            dst_ref[...] = dsfin_ref[...]
            dw2_ref[...] = jnp.zeros_like(dw2_ref)
            dgb_ref[...] = jnp.zeros_like(dgb_ref)

        vis, vis_t, vis4, vis4_t, hm = _gla_masks(reverse)
        tri = vis.astype(F32).astype(BF16)
        tri_t = vis_t.astype(F32).astype(BF16)
        lane_vis = _lane_vis(reverse, False)
        lane_vis_t = _lane_vis(reverse, True)
        lr16 = lr_ref[...].astype(BF16)
        logits = _nn(lr16, lrw_ref[...]) + gb_ref[...]
        a_all = _logsig(logits) * (1.0 / 16.0)
        dsig = (1.0 - jax.nn.sigmoid(logits)) * (1.0 / 16.0)
        dst = dst_ref[...]
        for c in (range(nc) if reverse else range(nc - 1, -1, -1)):
            rows = slice(CH * c, CH * c + CH)
            b = _tri_mm(tri, a_all[rows])
            bl = b[0:1] if reverse else b[CH - 1:CH]
            eb = jnp.exp(b)
            enb = jnp.exp(-b)
            ebl = jnp.exp(bl - b)
            el = jnp.exp(bl)
            q = qkv_ref[rows, 0:256].astype(F32) * 0.125
            k = qkv_ref[rows, 256:512].astype(F32)
            v16 = qkv_ref[rows, 512:1024].astype(BF16)
            do16 = do_ref[rows, :].astype(BF16)
            qd = q * eb
            kd = k * enb
            kdec = k * ebl
            st = sb_ref[c]
            st16 = st.astype(BF16)
            dst16 = dst.astype(BF16)
            qd16 = qd.astype(BF16)
            kd16 = kd.astype(BF16)
            qstack = _stack_heads(qd, hm)
            kstack = _stack_heads(kd, hm)
            kdecstack = _stack_heads(kdec, hm)
            pt = jnp.where(vis4_t, _nt(kstack, qd16), 0.0).astype(BF16)
            dvinter = _nt(kdecstack, dst16)
            do_rows = _rows_of_heads(do16)
            v_rows = _rows_of_heads(v16)
            dp_cat = jnp.where(lane_vis, _diag_heads(_nt(do_rows, v_rows), hm), 0.0).astype(BF16)
            dpt_cat = jnp.where(lane_vis_t, _diag_heads(_nt(v_rows, do_rows), hm), 0.0).astype(BF16)
            dqd = _nn(dp_cat, kstack) + _diag_heads(_nn(do_rows, st16), hm)
            dkd = _nn(dpt_cat, qstack)
            dkdec = _diag_heads(_nn(v_rows, dst16), hm)
            for h in range(4):
                rh = slice(CH * h, CH * h + CH)
                dv_h = _nn(pt[rh], do_rows[rh]) + dvinter[rh]
                if has_prev:
                    dv_h = dv_h + pq_ref[rows, 512 + 128 * h:512 + 128 * h + 128]
                dqkv_ref[rows, 512 + 128 * h:512 + 128 * h + 128] = dv_h.astype(dqkv_ref.dtype)
            dq = dqd * eb * 0.125
            dk = dkd * enb + dkdec * ebl
            if has_prev:
                dq = dq + pq_ref[rows, 0:256]
                dk = dk + pq_ref[rows, 256:512]
            dqkv_ref[rows, 0:256] = dq.astype(dqkv_ref.dtype)
            dqkv_ref[rows, 256:512] = dk.astype(dqkv_ref.dtype)
            g_kdec = dkdec * kdec
            db = dqd * qd - dkd * kd - g_kdec
            dbl = jnp.sum(g_kdec, axis=0, keepdims=True) + jnp.sum(st * dst, axis=0, keepdims=True) * el
            da = _tri_mm(tri_t, db) + dbl
            dlog_ref[rows, :] = da * dsig[rows]
            dst = dst * el + _diag_heads(_tn(do16, qd16), hm)
        dst_ref[...] = dst
        dlog = dlog_ref[...]
        dlog16 = dlog.astype(BF16)
        dlr = _nn(dlog16, lrwt_ref[...])
        if has_prev:
            dlr = dlr + plr_ref[...]
        dlr_ref[...] = dlr
        dw2_ref[...] += _tn(lr16, dlog16)
        dgb_ref[...] += jnp.sum(dlog, axis=0, keepdims=True)

        @pl.when(i == nb - 1)
        def _():
            ds0_ref[...] = dst

    in_specs = [pl.BlockSpec((tb,1024), lambda i: (rmap(i), qkv_blk)), pl.BlockSpec((tb,LRW), lambda i: (rmap(i), 0)),
                _full((LRW, 256)), _full((256, LRW)), _full((1, 256)), pl.BlockSpec((nc, 128, 256), lambda i: (rmap(i), 0, 0)),
                _full((128, 256)), pl.BlockSpec((tb,512), lambda i: (rmap(i), 0))]
    args = [p, lr, lrw, lrwt, gbias, sb, dsfin, do]
    if has_prev:
        in_specs += [pl.BlockSpec((tb,1024), lambda i: (rmap(i), 0)), pl.BlockSpec((tb,LRW), lambda i: (rmap(i), 0))]
        args += list(prev)
    aliases = {}
    if has_dp:
        in_specs.append(pl.BlockSpec(memory_space=pl.ANY))
        aliases = {len(args): 0}
        args.append(dp)
        dq_spec = pl.BlockSpec((tb,1024), lambda i: (rmap(i), 2))
        dq_shape = SDS(dp.shape, dp.dtype)
    else:
        dq_spec = pl.BlockSpec((tb,1024), lambda i: (rmap(i), 0))
        dq_shape = SDS((m, 1024), F32)
    return _pcall(
        body, name=name, grid=(nb,), in_specs=in_specs,
        out_specs=[dq_spec, pl.BlockSpec((tb,LRW), lambda i: (rmap(i), 0)), _full((LRW, 256)), _full((1, 256)), _full((128, 256))],
        out_shape=[dq_shape, SDS((m, LRW), F32), SDS((LRW, 256), F32), SDS((1, 256), F32), SDS((128, 256), F32)],
        scratch_shapes=[pltpu.VMEM((128, 256), F32), pltpu.VMEM((tb,256), F32)],
        input_output_aliases=aliases, compiler_params=_cp(("arbitrary",)),
    )(*args)


EARLY_KEYS = ["dmodc", "dscc", "dng_c", "dlng", "dlnb", "dws", "dbs", "dgbn", "dgf", "dw2", "dgb2", "loss", "dgate"]
EARLY_SHAPES = [(1, 2 * D), (D,), (1, D), (1, 512), (1, 512), (1, 4, 128, 128), (1, 4, 128), (1, 512), (D,), (2, 16, 256), (2, 256),
                (128,), (1, D)]
EARLY_SIZE = 2 * D + D + D + 512 + 512 + 4 * 128 * 128 + 512 + 512 + D + 2 * 16 * 256 + 512 + 128 + D
EARLY_ROWS = 648


def _device_step(x, c, ctx, c_ctx, tgt, wm, bm, ng, wit_g, wit_r, wlrt, ln_g, ln_b, ws, bs, w2, gb2, gbn, wpa, wpb, wo, gf,
                 exchange=None, shards=None):
    L = x.shape[0]
    wit_qkv = wit_r[2048:3072]
    ws16 = ws.astype(BF16)
    wst16 = jnp.swapaxes(ws, 1, 2).astype(BF16)
    bscol = bs[:, :, None]
    lrw = [jnp.zeros((LRW, 256), F32).at[16 * r:16 * r + 16].set(w2[r]).astype(BF16) for r in range(2)]
    lrwt = [w.T for w in lrw]
    gbias = [gb2[r:r + 1] for r in range(2)]

    cc = jnp.zeros((8, D), F32).at[0:1].set(c).at[1:2].set(c_ctx)
    mod = _modvec(cc, wm, bm)
    shift, scale, gate = mod[0:1, 0:D], mod[0:1, D:2 * D], mod[0:1, 2 * D:3 * D]
    shift_c, scale_c = mod[1:2, 0:D], mod[1:2, D:2 * D]

    hc = _prep_h(ctx, ng, scale_c, shift_c, "prep_hc")
    pc = _mm(hc, wit_qkv, tm=256, tn=1024, tk=D, out_dtype=F32, name="mm_pc", b_t=True)
    plrc = _mm(hc, wlrt, tm=256, tn=LRW, tk=D, out_dtype=F32, name="mm_plrc", b_t=True)
    zero_s = jnp.zeros((128, 256), F32)
    _, sbc_f, sc_f, _, sbc_b, sc_b = _gla_fwd2(pc, 0, plrc, lrw, gbias, (zero_s, zero_s), "gla_fwd_c")

    h, p, plr, vnr, vnc, late = _proj_fwd(x, ng, scale, shift, wit_g, wit_r, wlrt, ln_g, ln_b,
                                          shards if shards is not None else ())
    if shards is not None:
        me_xy = 2 * lax.axis_index("x") + lax.axis_index("y")
        g_wpa, g_wpb, g_wo = (_own(g, s_, me_xy) for g, s_ in zip(late, shards))
        wpa = jnp.swapaxes(g_wpa, 0, 1).reshape(512, D)
        wpb = jnp.swapaxes(g_wpb, 0, 1).reshape(512, D)
        wo = g_wo.reshape(D, D)
    o_f, sb_f, _, o_b, sb_b, _ = _gla_fwd2(p, 2, plr, lrw, gbias, (sc_f, sc_b), "gla_fwd")
    svc = _colmix_fwd(vnc.reshape(2, AC, L), ws16[2:4], bscol[2:4]).reshape(2, L, 128)
    ya_in, yb_in, svr, dwo, dx1, dout, loss, dgate, dgf = _tail_fwd(
        o_f, o_b, p, vnr, svc, x, tgt, ws16[0:2], bscol[0:2], gbn, wpa, wpb, wo, gate, gf)

    dwpa, dwpb, dp_g, dp, dsr, dsc, do, dgbn = _tail_bwd(dout, ya_in, yb_in, p, svr, svc, o_f, o_b, gbn, wo, wpa, wpb)
    dvnc, dws23, dbs23 = _colmix_bwd(dsc.reshape(2, AC, L), vnc.reshape(2, AC, L), wst16[2:4])
    dp, dws01, dbs01, dlng, dlnb = _ln_bwd(dsr, vnr, dvnc.reshape(2, L, 128), p, wst16[0:2], ln_g, dp)
    zero_ds = jnp.zeros((128, 256), F32)
    dqkv_f, dlr_f, dw2_f, dgb_f, ds0_f = _gla_bwd(p, 2, plr, lrw[0], lrwt[0], gbias[0], sb_f, zero_ds, do, None, None,
                                                  reverse=False, name="gla_bwd_f")
    dp, dlr, dw2_b, dgb_b, ds0_b = _gla_bwd(p, 2, plr, lrw[1], lrwt[1], gbias[1], sb_b, zero_ds, do, (dqkv_f, dlr_f), dp,
                                            reverse=True, name="gla_bwd_b")
    zero_do = jnp.zeros((ctx.shape[0], 512), BF16)
    dqkvc_f, dlrc_f, dw2c_f, dgbc_f, _ = _gla_bwd(pc, 0, plrc, lrw[0], lrwt[0], gbias[0], sbc_f, ds0_f, zero_do, None, None,
                                                  reverse=False, name="gla_bwd_cf")
    dqkvc, dlrc, dw2c_b, dgbc_b, _ = _gla_bwd(pc, 0, plrc, lrw[1], lrwt[1], gbias[1], sbc_b, ds0_b, zero_do,
                                              (dqkvc_f, dlrc_f), None, reverse=True, name="gla_bwd_cb")
    dhc = _mm(dqkvc, wit_qkv, tm=256, tn=D, tk=1024, out_dtype=F32, name="mm_dhc")
    dhc = _mm(dlrc, wlrt, tm=256, tn=D, tk=LRW, out_dtype=F32, name="mm_dhc_lr", acc=dhc)
    _, dng_c, dscale_c, dshift_c = _prep_bwd(ctx, dhc, None, ng, scale_c, "prep_bwd_c")

    if exchange is not None:
        ready = exchange[0](dwpa, dwpb, dwo)
        dwit_g, ready_other = _mm_tn(dp_g, h, ta=1024, tn=D, tk=2048, name="mm_dwi_g", out_dtype=BF16, swap=ready)
    else:
        dwit_g = _mm_tn(dp_g, h, ta=1024, tn=D, tk=2048, name="mm_dwi_g", out_dtype=BF16)
    dwit_r =_mm_tn(dp, h, ta=1024, tn=D, tk=2048, name="mm_dwi_r", out_dtype=BF16)
    dwit_qkv = _mm_tn(dqkvc, hc, ta=1024, tn=D, tk=256, name="mm_dwi_c", acc=dwit_r[2048:3072], out_dtype=BF16)
    dwlrt = _mm_tn(dlr, h, ta=LRW, tn=D, tk=2048, name="mm_dwlr")
    dwlrt = _mm_tn(dlrc, hc, ta=LRW, tn=D, tk=256, name="mm_dwlr_c", acc=dwlrt, out_dtype=BF16)
    big = dict(dwit_g=dwit_g, dwit_r=dwit_r, dwit_qkv=dwit_qkv, dwlrt=dwlrt, dwpa=dwpa, dwpb=dwpb, dwo=dwo)

    dmodc = jnp.concatenate([dshift_c, dscale_c], axis=1)
    dscc = _dcctx(jnp.zeros((8, 2 * D), F32).at[0:1].set(dmodc), wm)[0:1]
    dw2p = dw2_f + dw2c_f, dw2_b + dw2c_b
    small = dict(
        dmodc=dmodc, dscc=dscc, dng_c=dng_c, dlng=dlng, dlnb=dlnb, dws=jnp.concatenate([dws01, dws23], axis=0),
        dbs=jnp.concatenate([dbs01, dbs23], axis=0)[:, :, 0], dgbn=dgbn, dgf=dgf,
        dw2=jnp.stack([dw2p[0][0:16], dw2p[1][16:32]]), dgb2=jnp.concatenate([dgb_f + dgbc_f, dgb_b + dgbc_b], axis=0),
        loss=loss[0, 0], dgate=dgate)

    send = exchange[1](big, ready, ready_other) if exchange is not None else ()
    early = _pack([small[k] for k in EARLY_KEYS[:-2]] + [jnp.broadcast_to(small["loss"], (128,)), small["dgate"]], EARLY_ROWS) \
        if exchange is not None else None
    (dx, dng, dscale, dshift), got, early_all = _proj_bwd(dp_g, dp, dlr, wit_g, wit_r, wlrt, x, dx1, ng, scale, send, early)
    return dict(dx=dx, got=got, early=early, early_all=early_all, dshift=dshift, dscale=dscale, dng_lat=dng, **big, **small)


ANY = pl.BlockSpec(memory_space=pl.ANY)


def _coords():
    return lax.axis_index("x"), lax.axis_index("y"), lax.axis_index("c")


def _flip(v, bit):
    return 1 - v if bit else v


def _remote(src, dst, send_sem, recv_sem, dev):
    return pltpu.make_async_remote_copy(src_ref=src, dst_ref=dst, send_sem=send_sem, recv_sem=recv_sem,
                                        device_id=dev, device_id_type=MESH)


def _own(out, block, idx):
    return lax.dynamic_update_slice_in_dim(out, block[None], idx, axis=0)


def _half_idx(shape, axis, which, lead=()):
    idx = [pl.ds(0, d) for d in shape]
    h = shape[axis] // 2
    idx[axis] = pl.ds(which * h, h)
    return tuple(lead) + tuple(idx)


def _gather_weights(split, whole, name):
    ns, nw = len(split), len(whole)
    n = ns + nw
    arrs = [a for a, _ in split] + list(whole)

    def body(*refs):
        ins, outs = refs[:n], refs[n:2 * n]
        a_send, a_recv, b_send, b_recv = refs[2 * n:]
        x, y, c = _coords()
        me = 2 * x + y
        sib = (x, y, 1 - c)
        peers = [(1 - x, y), (x, 1 - y), (1 - x, 1 - y)]

        def half(k, slot, which):
            return outs[k].at[_half_idx(arrs[k].shape, split[k][1], which, lead=(slot,))]

        sends = []
        for k in range(n):
            for j, (px, py) in enumerate(peers):
                if k < ns:
                    rc = _remote(ins[k].at[_half_idx(arrs[k].shape, split[k][1], c)], half(k, me, c), a_send.at[3 * k + j],
                                 a_recv.at[3 * k + j], (px, py, c))
                else:
                    rc = _remote(ins[k], outs[k].at[me], a_send.at[3 * k + j], a_recv.at[3 * k + j], (px, py, c))
                rc.start()
                sends.append(rc)
        for k in range(ns):
            for j, (px, py) in enumerate(peers):
                landed = half(k, 2 * px + py, c)
                _remote(landed, landed, a_send.at[3 * k + j], a_recv.at[3 * k + j], (px, py, c)).wait_recv()
                fw = _remote(landed, landed, b_send.at[3 * k + j], b_recv.at[3 * k + j], sib)
                fw.start()
                sends.append(fw)
        for k in range(ns, n):
            for j, (px, py) in enumerate(peers):
                landed = outs[k].at[2 * px + py]
                _remote(landed, landed, a_send.at[3 * k + j], a_recv.at[3 * k + j], (px, py, c)).wait_recv()
        for k in range(ns):
            for j, (px, py) in enumerate(peers):
                passed = half(k, 2 * px + py, 1 - c)
                _remote(passed, passed, b_send.at[3 * k + j], b_recv.at[3 * k + j], sib).wait_recv()
        for rc in sends:
            rc.wait_send()

    outs = _pcall(
        body, name=name, in_specs=[ANY] * n, out_specs=[ANY] * n,
        out_shape=[SDS((4,) + a.shape, a.dtype) for a in arrs],
        scratch_shapes=[pltpu.SemaphoreType.DMA((3 * n,)), pltpu.SemaphoreType.DMA((3 * n,)), pltpu.SemaphoreType.DMA((3 * ns,)),
                        pltpu.SemaphoreType.DMA((3 * ns,))],
    )(*arrs)
    me_xy = 2 * lax.axis_index("x") + lax.axis_index("y")
    return [_own(o, a, me_xy) for o, a in zip(outs, arrs)]


def _gather_all(a, swap, name):
    masks = [(mx, my, mc) for mx in range(2) for my in range(2) for mc in range(2)][1:]
    n = len(swap)

    def body(*refs):
        in_ref, sw_in = refs[0], refs[1:1 + n]
        out_ref, sw_out = refs[1 + n], refs[2 + n:2 + 2 * n]
        send_sems, recv_sems = refs[2 + 2 * n:]
        x, y, c = _coords()
        me = 4 * x + 2 * y + c
        sends = []
        for j, (mx, my, mc) in enumerate(masks):
            rc = _remote(in_ref, out_ref.at[me], send_sems.at[j], recv_sems.at[j], (_flip(x, mx), _flip(y, my), _flip(c, mc)))
            rc.start()
            sends.append(rc)
        for k in range(n):
            rc = _remote(sw_in[k], sw_out[k], send_sems.at[7 + k], recv_sems.at[7 + k], (x, y, 1 - c))
            rc.start()
            sends.append(rc)
        for j, (mx, my, mc) in enumerate(masks):
            px, py, pc = _flip(x, mx), _flip(y, my), _flip(c, mc)
            landed = out_ref.at[4 * px + 2 * py + pc]
            _remote(landed, landed, send_sems.at[j], recv_sems.at[j], (px, py, pc)).wait_recv()
        for k in range(n):
            _remote(sw_out[k], sw_out[k], send_sems.at[7 + k], recv_sems.at[7 + k], (x, y, 1 - c)).wait_recv()
        for rc in sends:
            rc.wait_send()

    res = _pcall(
        body, name=name, in_specs=[ANY] * (1 + n), out_specs=[ANY] * (1 + n),
        out_shape=[SDS((8,) + a.shape, a.dtype)] + [SDS(s_.shape, s_.dtype) for s_ in swap],
        scratch_shapes=[pltpu.SemaphoreType.DMA((7 + n,)), pltpu.SemaphoreType.DMA((7 + n,))],
    )(a, *swap)
    return _own(res[0], a, 4 * lax.axis_index("x") + 2 * lax.axis_index("y") + lax.axis_index("c")), list(res[1:])


def _half_shape(shape, axis):
    return tuple(d // 2 if i == axis else d for i, d in enumerate(shape))


def _swap_half_c(arrs, axes, name):
    n = len(arrs)

    def body(*refs):
        ins, outs = refs[:n], refs[n:2 * n]
        send_sems, recv_sems = refs[2 * n:]
        x, y, c = _coords()
        sends = []
        for k in range(n):
            rc = _remote(ins[k].at[_half_idx(arrs[k].shape, axes[k], 1 - c)], outs[k], send_sems.at[k], recv_sems.at[k],
                         (x, y, 1 - c))
            rc.start()
            sends.append(rc)
        for rc in sends:
            rc.wait()

    return _pcall(
        body, name=name, in_specs=[ANY] * n, out_specs=[ANY] * n,
        out_shape=[SDS(_half_shape(a.shape, ax), a.dtype) for a, ax in zip(arrs, axes)],
        scratch_shapes=[pltpu.SemaphoreType.DMA((n,)), pltpu.SemaphoreType.DMA((n,))],
    )(*arrs)


def _pair_sum(a, got, cidx, axis, name):
    _, r, cdim = a.shape
    hshape = _half_shape(a.shape, axis)

    def body(c_ref, a_ref, g_ref, o_ref):
        o_ref[...] = (a_ref[...].astype(F32) + g_ref[...].astype(F32)).astype(BF16)

    if axis == 1:
        tr = min(r // 2, 256)
        nj = (r // 2) // tr
        blk = pl.BlockSpec((1, tr, cdim), lambda s, j, c: (s, j, 0))
        a_spec = pl.BlockSpec((1, tr, cdim), lambda s, j, c: (s, c[0] * nj + j, 0))
    else:
        nj, hw = 1, cdim // 2
        blk = pl.BlockSpec((1, r, hw), lambda s, j, c: (s, 0, 0))
        a_spec = pl.BlockSpec((1, r, hw), lambda s, j, c: (s, 0, c[0]))
    return _pcall(
        body, name=name, out_shape=SDS(hshape, BF16),
        grid_spec=pltpu.PrefetchScalarGridSpec(num_scalar_prefetch=1, grid=(4, nj), in_specs=[a_spec, blk], out_specs=blk),
        compiler_params=_cp(("parallel", "parallel"), VMEM_BIG),
    )(cidx, a, got)


def _sum_chips(parts, name):
    _, h, cdim = parts.shape

    def body(p_ref, o_ref):
        acc = p_ref[0].astype(F32)
        for k in range(1, 4):
            acc = acc + p_ref[k].astype(F32)
        o_ref[...] = acc

    if h % 256 == 0 or h in (128,):
        tr = min(h, 256)
        grid, in_spec, out_spec = (h // tr,), pl.BlockSpec((4, tr, cdim), lambda i: (0, i, 0)), pl.BlockSpec((tr, cdim), lambda i: (i, 0))
    else:
        lw = 256
        grid, in_spec, out_spec = (cdim // lw,), pl.BlockSpec((4, h, lw), lambda i: (0, 0, i)), pl.BlockSpec((h, lw), lambda i: (0, i))
    return _pcall(
        body, name=name, grid=grid, in_specs=[in_spec], out_specs=out_spec, out_shape=SDS((h, cdim), F32),
        compiler_params=_cp(("parallel",), VMEM_BIG),
    )(parts)


def _sum_slots(a, name, rows):
    s, n, _ = a.shape

    def body(a_ref, o_ref):
        acc = a_ref[0]
        for k in range(1, s):
            acc = acc + a_ref[k]
        o_ref[...] = acc

    return _pcall(
        body, name=name, grid=(n // rows,), in_specs=[pl.BlockSpec((s, rows, 128), lambda i: (0, i, 0))],
        out_specs=pl.BlockSpec((rows, 128), lambda i: (i, 0)), out_shape=SDS((n, 128), F32),
        compiler_params=_cp(("parallel",)),
    )(a)


def _adam_math(w, g, m, v):
    nm = ADAM_B1 * m + (1.0 - ADAM_B1) * g
    nv = ADAM_B2 * v + (1.0 - ADAM_B2) * (g * g)
    m_hat = nm / (1.0 - ADAM_B1 ** ADAM_STEP)
    v_hat = nv / (1.0 - ADAM_B2 ** ADAM_STEP)
    return -ADAM_LR * (m_hat / (jnp.sqrt(v_hat) + ADAM_EPS) + ADAM_WD * w), nm, nv


def _adamw(w, g, m, v, name, rows):
    r, cdim = w.shape

    def body(w_ref, g_ref, m_ref, v_ref, d_ref, nm_ref, nv_ref):
        d_ref[...], nm_ref[...], nv_ref[...] = _adam_math(w_ref[...], g_ref[...], m_ref[...], v_ref[...])

    blk = pl.BlockSpec((rows, cdim), lambda i: (i, 0))
    return _pcall(
        body, name=name, grid=(r // rows,), in_specs=[blk] * 4, out_specs=[blk] * 3,
        out_shape=[SDS(w.shape, F32)] * 3, compiler_params=_cp(("parallel",)),
    )(w, g, m, v)


def _adamw_joined(w, mine, other, m, v, cidx, axis, name, rows):
    r, cdim = w.shape
    if axis == 0:
        rows = r

    def body(c_ref, w_ref, a_ref, b_ref, m_ref, v_ref, g_ref, d_ref, nm_ref, nv_ref):
        a, b = a_ref[...], b_ref[...]
        g = jnp.where(c_ref[0] == 0, jnp.concatenate([a, b], axis=axis), jnp.concatenate([b, a], axis=axis))
        g_ref[...] = g
        d_ref[...], nm_ref[...], nv_ref[...] = _adam_math(w_ref[...], g, m_ref[...], v_ref[...])

    blk = pl.BlockSpec((rows, cdim), lambda i, c: (i, 0))
    hshape = (rows // 2, cdim) if axis == 0 else (rows, cdim // 2)
    hblk = pl.BlockSpec(hshape, lambda i, c: (i, 0))
    return _pcall(
        body, name=name, out_shape=[SDS(w.shape, F32)] * 4,
        grid_spec=pltpu.PrefetchScalarGridSpec(num_scalar_prefetch=1, grid=(r // rows,), in_specs=[blk, hblk, hblk, blk, blk],
                                               out_specs=[blk] * 4),
        compiler_params=_cp(("parallel",)),
    )(cidx, w, mine, other, m, v)


def _adamw_many(ws, gs, ms, vs, name):
    n = len(ws)

    def body(*refs):
        outs = refs[4 * n:]
        for k in range(n):
            d, nm, nv = _adam_math(refs[k][...], refs[n + k][...], refs[2 * n + k][...], refs[3 * n + k][...])
            outs[k][...] = d
            outs[n + k][...] = nm
            outs[2 * n + k][...] = nv

    res = _pcall(body, name=name, out_shape=[SDS(w.shape, F32) for w in ws] * 3)(*ws, *gs, *ms, *vs)
    return res[:n], res[n:2 * n], res[2 * n:]


def _pack(pieces, rows):
    flat = jnp.concatenate([p.reshape(-1) for p in pieces])
    return jnp.pad(flat, (0, rows * 128 - flat.shape[0])).reshape(rows, 128)


def _unpack(buf, shapes):
    flat = buf.reshape(-1)
    out, off = [], 0
    for shp in shapes:
        size = 1
        for s in shp:
            size *= s
        out.append(flat[off:off + size].reshape(shp))
        off += size
    return out


LATE_ROWS = 32


def kernel(x, c, ctx, c_ctx, w_mod, b_mod, norm_g, w_in, a_ln_g, a_ln_b, a_ws, a_bs, b_gate_w2, b_gate_b, b_norm_g, w_proj_a, w_proj_b, w_out, final_norm_g, loss_target, m_c_ctx, m_w_mod, m_b_mod, m_norm_g, m_w_in, m_a_ln_g, m_a_ln_b, m_a_ws, m_a_bs, m_b_gate_w2, m_b_gate_b, m_b_norm_g, m_w_proj_a, m_w_proj_b, m_w_out, m_final_norm_g, v_c_ctx, v_w_mod, v_b_mod, v_norm_g, v_w_in, v_a_ln_g, v_a_ln_b, v_a_ws, v_a_bs, v_b_gate_w2, v_b_gate_b, v_b_norm_g, v_w_proj_a, v_w_proj_b, v_w_out, v_final_norm_g):
    xi, yi, ci = _coords()
    me_xy = 2 * xi + yi

    gate_pack = _pack([b_gate_w2[0], b_gate_b[0]], 24)
    w_in_t, m_w_in_t, v_w_in_t = (jnp.swapaxes(a[0], 0, 1) for a in (w_in, m_w_in, v_w_in))
    g_wit, g_wm, g_gate = _gather_weights([(w_in_t.astype(BF16), 1), (w_mod[0].astype(BF16), 0)], [gate_pack], "gather_weights")
    late_shards = (w_proj_a[0].astype(BF16), w_proj_b[0].astype(BF16), w_out[0].astype(BF16))
    wit_u = g_wit.reshape(4 * 1288, D)
    wit_g = wit_u[3104:5152]
    wit_r = jnp.concatenate([wit_u[1056:1568], wit_u[1568:2080], wit_u[2592:3104], wit_u[2080:2592], wit_u[0:1024]], axis=0)
    wlrt = jnp.pad(wit_u[1024:1056], ((0, LRW - 32), (0, 0)))
    wm = jnp.swapaxes(g_wm, 0, 1).reshape(D, 3 * D)
    gflat = g_gate.reshape(4, 24 * 128)
    w2 = jnp.swapaxes(gflat[:, 0:2048].reshape(4, 2, 16, 64), 0, 2)
    w2 = jnp.swapaxes(w2, 0, 1).reshape(2, 16, 256)
    gb2 = jnp.swapaxes(gflat[:, 2048:2176].reshape(4, 2, 64), 0, 1).reshape(2, 256)

    tags = ["wi", "wpa", "wpb", "wo"]
    half_axes = [2, 1, 1, 1]
    sent = []

    def ready_blocks(dwpa, dwpb, dwo):
        return ([jnp.swapaxes(dwpa.reshape(512, 4, 256), 0, 1), jnp.swapaxes(dwpb.reshape(512, 4, 256), 0, 1),
                 dwo.reshape(4, 256, D)], half_axes[1:])

    def exchange(g, ready, ready_other):
        dwr = g["dwit_r"]
        dwit_u = jnp.concatenate([g["dwit_qkv"], g["dwlrt"][0:32], dwr[0:512], dwr[512:1024], dwr[1536:2048], dwr[1024:1536],
                                  g["dwit_g"]], axis=0)
        big = [dwit_u.reshape(4, 1288, D)] + ready[0]
        other = list(_swap_half_c(big[:1], half_axes[:1], "swap_half_in")) + ready_other
        cidx = jnp.reshape(ci, (1,)).astype(jnp.int32)
        sent.extend(_pair_sum(a, o, cidx, ax, "sum_pair_" + t) for a, o, ax, t in zip(big, other, half_axes, tags))
        return sent

    r = _device_step(x[0], c, ctx[0], c_ctx[None], loss_target[0], wm, b_mod, norm_g, wit_g, wit_r, wlrt, a_ln_g, a_ln_b,
                     a_ws[0], a_bs[0], w2, gb2, b_norm_g, None, None, None, final_norm_g[None], (ready_blocks, exchange), late_shards)

    parts = [_own(g, lax.dynamic_index_in_dim(s_, me_xy, axis=0, keepdims=False), me_xy) for g, s_ in zip(r["got"], sent)]
    halves = [_sum_chips(p_, "sum_chips_" + t) for p_, t in zip(parts, tags)]

    me8 = 4 * xi + 2 * yi + ci
    early_all = _own(r["early_all"], r["early"], me8)
    late = _pack([r["dshift"], r["dscale"], r["dng_lat"], c], LATE_ROWS)
    late_all, others = _gather_all(late, halves, "gather_small")
    s_early = _sum_slots(early_all, "sum_early", EARLY_ROWS // 3)
    s_late = _sum_slots(late_all, "sum_late", LATE_ROWS)
    (s_dmodc, s_dscc, s_dng_c, s_dlng, s_dlnb, s_dws, s_dbs, s_dgbn, s_dgf, s_dw2, s_dgb2, s_loss, s_dgate) = _unpack(
        s_early, EARLY_SHAPES)
    s_dshift, s_dscale, s_dng_lat, _ = _unpack(s_late, [(1, D)] * 4)
    s_dng = s_dng_lat + s_dng_c
    s_dmod = jnp.concatenate([s_dshift, s_dscale, s_dgate], axis=1)
    loss = s_loss[0]
    s_dmodc_p = jnp.pad(s_dmodc, ((0, 0), (0, D)))
    g_b_mod = s_dmod + s_dmodc_p
    sg = jax.nn.sigmoid(c_ctx)
    g_c_ctx = s_dscc * (sg * (1.0 + c_ctx * (1.0 - sg)))
    g_w2 = lax.dynamic_slice_in_dim(s_dw2, 64 * me_xy, 64, axis=2)[None]
    g_gb2 = lax.dynamic_slice_in_dim(s_dgb2, 64 * me_xy, 64, axis=1)[None]

    flat_l = late_all.reshape(8, LATE_ROWS * 128)
    dgate_all = early_all.reshape(8, EARLY_ROWS * 128)[:, EARLY_SIZE - D:EARLY_SIZE]
    dmod_all = jnp.concatenate([flat_l[:, 0:2 * D], dgate_all], axis=1)
    c_all = flat_l[:, 3 * D:4 * D]
    lhs = jnp.concatenate([_silu(c_all), _silu(c_ctx)[None], jnp.zeros((7, D), F32)], axis=0)
    rhs = jnp.concatenate([dmod_all, s_dmodc_p, jnp.zeros((7, 3 * D), F32)], axis=0)
    rhs = lax.dynamic_slice_in_dim(rhs, 768 * me_xy, 768, axis=1)
    g_w_mod = _mm(lhs.T.astype(BF16), rhs.astype(BF16), tm=D, tn=768, tk=16, out_dtype=F32, name="mm_dwm")

    cidx = jnp.reshape(ci, (1,)).astype(jnp.int32)
    g_w_in_t, d_w_in_t, nm_w_in_t, nv_w_in_t = _adamw_joined(w_in_t, halves[0], others[0], m_w_in_t, v_w_in_t, cidx, 1,
                                                             "adamw_w_in", 184)
    g_w_in, d_w_in, nm_w_in, nv_w_in = (jnp.swapaxes(a, 0, 1) for a in (g_w_in_t, d_w_in_t, nm_w_in_t, nv_w_in_t))
    g_wpa, d_wpa, nm_wpa, nv_wpa = _adamw_joined(w_proj_a[0], halves[1], others[1], m_w_proj_a[0], v_w_proj_a[0], cidx, 0,
                                                 "adamw_wpa", 0)
    g_wpb, d_wpb, nm_wpb, nv_wpb = _adamw_joined(w_proj_b[0], halves[2], others[2], m_w_proj_b[0], v_w_proj_b[0], cidx, 0,
                                                 "adamw_wpb", 0)
    g_wo, d_wo, nm_wo, nv_wo = _adamw_joined(w_out[0], halves[3], others[3], m_w_out[0], v_w_out[0], cidx, 0, "adamw_wo", 0)
    d_w_mod, nm_w_mod, nv_w_mod = _adamw(w_mod[0], g_w_mod, m_w_mod[0], v_w_mod[0], "adamw_w_mod", 256)

    names = ["c_ctx", "b_mod", "norm_g", "a_ln_g", "a_ln_b", "a_ws", "a_bs", "b_gate_w2", "b_gate_b", "b_norm_g", "final_norm_g"]
    ws_ = [c_ctx, b_mod, norm_g, a_ln_g, a_ln_b, a_ws, a_bs, b_gate_w2, b_gate_b, b_norm_g, final_norm_g]
    gs_ = [g_c_ctx, g_b_mod, s_dng, s_dlng, s_dlnb, s_dws, s_dbs, g_w2, g_gb2, s_dgbn, s_dgf]
    ms_ = [m_c_ctx, m_b_mod, m_norm_g, m_a_ln_g, m_a_ln_b, m_a_ws, m_a_bs, m_b_gate_w2, m_b_gate_b, m_b_norm_g, m_final_norm_g]
    vs_ = [v_c_ctx, v_b_mod, v_norm_g, v_a_ln_g, v_a_ln_b, v_a_ws, v_a_bs, v_b_gate_w2, v_b_gate_b, v_b_norm_g, v_final_norm_g]
    shapes = [w.shape for w in ws_]
    flat2 = [(1, 1024), (1, 3072), (1, 1024), (1, 512), (1, 512), (512, 128), (4, 128), (32, 64), (2, 64), (1, 512), (1, 1024)]
    as2d = lambda arrs: [a.reshape(s) for a, s in zip(arrs, flat2)]
    d_s, nm_s, nv_s = _adamw_many(as2d(ws_), as2d(gs_), as2d(ms_), as2d(vs_), "adamw_small")
    d_small = {n: a.reshape(s) for n, a, s in zip(names, d_s, shapes)}
    nm_small = {n: a.reshape(s) for n, a, s in zip(names, nm_s, shapes)}
    nv_small = {n: a.reshape(s) for n, a, s in zip(names, nv_s, shapes)}
    g_small = {n: g.reshape(s) for n, g, s in zip(names, gs_, shapes)}

    order = ["c_ctx", "w_mod", "b_mod", "norm_g", "w_in", "a_ln_g", "a_ln_b", "a_ws", "a_bs", "b_gate_w2", "b_gate_b", "b_norm_g",
             "w_proj_a", "w_proj_b", "w_out", "final_norm_g"]
    big_g = dict(w_mod=g_w_mod[None], w_in=g_w_in[None], w_proj_a=g_wpa[None], w_proj_b=g_wpb[None], w_out=g_wo[None])
    big_d = dict(w_mod=d_w_mod[None], w_in=d_w_in[None], w_proj_a=d_wpa[None], w_proj_b=d_wpb[None], w_out=d_wo[None])
    big_m = dict(w_mod=nm_w_mod[None], w_in=nm_w_in[None], w_proj_a=nm_wpa[None], w_proj_b=nm_wpb[None], w_out=nm_wo[None])
    big_v = dict(w_mod=nv_w_mod[None], w_in=nv_w_in[None], w_proj_a=nv_wpa[None], w_proj_b=nv_wpb[None], w_out=nv_wo[None])
    grads = [big_g[n] if n in big_g else g_small[n] for n in order]
    deltas = [big_d[n] if n in big_d else d_small[n] for n in order]
    new_m = [big_m[n] if n in big_m else nm_small[n] for n in order]
    new_v = [big_v[n] if n in big_v else nv_small[n] for n in order]
    return (loss, r["dx"][None], *grads, *deltas, *new_m, *new_v)
```

```python
import jax
import jax.numpy as jnp
from jax import lax
from jax.experimental import pallas as pl
from jax.experimental.pallas import tpu as pltpu

F32 = jnp.float32
BF16 = jnp.bfloat16
SDS = jax.ShapeDtypeStruct

D = 1024
NP = 5120
LRW = 128
CH = 64
AC = 128
EPS = 1e-6
TOK = 512
GLA_TB = 1024
VMEM_BIG = 48 * 1024 * 1024

ADAM_LR, ADAM_B1, ADAM_B2, ADAM_EPS, ADAM_WD, ADAM_STEP = 0.001, 0.9, 0.999, 1e-08, 0.01, 10

_pcall = pl.pallas_call
MESH = pl.DeviceIdType.MESH


def _cp(sem=None, vmem=None):
    kw = {}
    if sem is not None:
        kw["dimension_semantics"] = sem
    if vmem is not None:
        kw["vmem_limit_bytes"] = vmem
    return pltpu.CompilerParams(**kw)


def _silu(x):
    return x * jax.nn.sigmoid(x)


def _silu_and_grad(x):
    s = jax.nn.sigmoid(x)
    return x * s, s * (1.0 + x * (1.0 - s))


def _logsig(x):
    return jnp.minimum(x, 0.0) - jnp.log1p(jnp.exp(-jnp.abs(x)))


def _nt(a, b):
    return lax.dot_general(a, b, (((1,), (1,)), ((), ())), preferred_element_type=F32)


def _tn(a, b):
    return lax.dot_general(a, b, (((0,), (0,)), ((), ())), preferred_element_type=F32)


def _nn(a, b):
    return jnp.dot(a, b, preferred_element_type=F32)


def _full(shape):
    return pl.BlockSpec(shape, lambda *_: (0,) * len(shape))


def _mm(a, b, *, tm, tn, tk, out_dtype, name, acc=None, b_t=False):
    m, k = a.shape
    n, k2 = (b.shape if b_t else b.shape[::-1])
    assert k == k2 == tk and m % tm == 0 and n % tn == 0, (a.shape, b.shape, tm, tn, tk)
    has_acc = acc is not None

    def body(*refs):
        if has_acc:
            a_ref, b_ref, c_ref, o_ref = refs
        else:
            a_ref, b_ref, o_ref = refs
        part = (_nt if b_t else _nn)(a_ref[...].astype(BF16), b_ref[...].astype(BF16))
        o_ref[...] = ((c_ref[...] + part) if has_acc else part).astype(out_dtype)

    b_spec = pl.BlockSpec((tn, tk), lambda i, j: (j, 0)) if b_t else pl.BlockSpec((tk, tn), lambda i, j: (0, j))
    in_specs = [pl.BlockSpec((tm, tk), lambda i, j: (i, 0)), b_spec]
    args = [a, b]
    if has_acc:
        in_specs.append(pl.BlockSpec((tm, tn), lambda i, j: (i, j)))
        args.append(acc)
    return _pcall(
        body, name=name, grid=(m // tm, n // tn), in_specs=in_specs, out_specs=pl.BlockSpec((tm, tn), lambda i, j: (i, j)),
        out_shape=SDS((m, n), out_dtype), compiler_params=_cp(("parallel", "parallel"), VMEM_BIG),
    )(*args)


def _mm_tn(a, b, *, ta, tn, tk, name, acc=None, out_dtype=F32, swap=None):
    m, ka = a.shape
    m2, n = b.shape
    assert m == m2 and ka % ta == 0 and n % tn == 0 and m % tk == 0, (a.shape, b.shape, ta, tn, tk)
    nk = m // tk
    has_acc = acc is not None
    sw_arrs, sw_axes = swap if swap is not None else ((), ())
    ns = len(sw_arrs)
    n_in = 2 + has_acc
    grid = (ka // ta, n // tn, nk)

    def body(*refs):
        a_ref, b_ref = refs[:2]
        c_ref = refs[2] if has_acc else None
        sw_in = refs[n_in:n_in + ns]
        o_ref = refs[n_in + ns]
        sw_out = refs[n_in + ns + 1:n_in + 2 * ns + 1]
        acc_ref = refs[n_in + 2 * ns + 1]
        sems = refs[n_in + 2 * ns + 2:]
        kk = pl.program_id(2)

        def copies():
            x, y, c = _coords()
            return [_remote(sw_in[k].at[_half_idx(sw_arrs[k].shape, sw_axes[k], 1 - c)], sw_out[k], sems[0].at[k],
                            sems[1].at[k], (x, y, 1 - c)) for k in range(ns)]

        step = (pl.program_id(0) * grid[1] + pl.program_id(1)) * nk + kk
        if ns:
            @pl.when(step == 0)
            def _():
                for rc in copies():
                    rc.start()

        part = _tn(a_ref[...].astype(BF16), b_ref[...].astype(BF16))

        @pl.when(kk == 0)
        def _():
            if has_acc:
                acc_ref[...] = c_ref[...].astype(F32) + part
            else:
                acc_ref[...] = part

        @pl.when(kk > 0)
        def _():
            acc_ref[...] += part

        @pl.when(kk == nk - 1)
        def _():
            o_ref[...] = acc_ref[...].astype(out_dtype)

        if ns:
            @pl.when(step == grid[0] * grid[1] * nk - 1)
            def _():
                for rc in copies():
                    rc.wait()

    in_specs = [pl.BlockSpec((tk, ta), lambda i, j, kk: (kk, i)), pl.BlockSpec((tk, tn), lambda i, j, kk: (kk, j))]
    args = [a, b]
    if has_acc:
        in_specs.append(pl.BlockSpec((ta, tn), lambda i, j, kk: (i, j)))
        args.append(acc)
    out_spec, out_shape = pl.BlockSpec((ta, tn), lambda i, j, kk: (i, j)), SDS((ka, n), out_dtype)
    scratch = [pltpu.VMEM((ta, tn), F32)]
    if not ns:
        return _pcall(body, name=name, grid=grid, in_specs=in_specs, out_specs=out_spec, out_shape=out_shape,
                      scratch_shapes=scratch, compiler_params=_cp(("parallel", "parallel", "arbitrary"), VMEM_BIG))(*args)
    res = _pcall(
        body, name=name, grid=grid, in_specs=in_specs + [ANY] * ns, out_specs=[out_spec] + [ANY] * ns,
        out_shape=[out_shape] + [SDS(_half_shape(s.shape, ax), s.dtype) for s, ax in zip(sw_arrs, sw_axes)],
        scratch_shapes=scratch + [pltpu.SemaphoreType.DMA((ns,)), pltpu.SemaphoreType.DMA((ns,))],
        compiler_params=_cp(("arbitrary", "arbitrary", "arbitrary"), VMEM_BIG),
    )(*args, *sw_arrs)
    return res[0], list(res[1:])


def _modvec(cc, wm, bm):
    def body(c_ref, w_ref, b_ref, o_ref):
        o_ref[...] = _nn(_silu(c_ref[...]).astype(BF16), w_ref[...]) + b_ref[...]

    return _pcall(body, name="modvec", out_shape=SDS((8, 3 * D), F32), compiler_params=_cp(None, VMEM_BIG))(cc, wm, bm)


def _dcctx(dmodc, wm):
    def body(d_ref, w_ref, o_ref):
        o_ref[...] = _nt(d_ref[...].astype(BF16), w_ref[...])

    return _pcall(
        body, name="dcctx", grid=(1,), in_specs=[_full((8, 2 * D)), pl.BlockSpec((D, 2 * D), lambda i: (0, 0))],
        out_specs=_full((8, D)), out_shape=SDS((8, D), F32), compiler_params=_cp(("arbitrary",), VMEM_BIG),
    )(dmodc, wm)


def _prep_h(x, ng, scale, shift, name):
    m = x.shape[0]

    def body(x_ref, g_ref, sc_ref, sh_ref, h_ref):
        xf = x_ref[...]
        r = lax.rsqrt(jnp.mean(xf * xf, axis=-1, keepdims=True) + EPS)
        y = (xf * r) * g_ref[...]
        h_ref[...] = (y * (1.0 + sc_ref[...]) + sh_ref[...]).astype(BF16)

    tok = min(TOK, m)
    row = pl.BlockSpec((tok, D), lambda i: (i, 0))
    return _pcall(
        body, name=name, grid=(m // tok,), in_specs=[row, _full((1, D)), _full((1, D)), _full((1, D))],
        out_specs=row, out_shape=SDS((m, D), BF16), compiler_params=_cp(("parallel",)),
    )(x, ng, scale, shift)


def _resident(shape):
    return pl.BlockSpec(shape, lambda *_: (0,) * len(shape), pipeline_mode=pl.Buffered(1))


PROJ_TM = 512


def _proj_fwd(x, ng, scale, shift, wit_g, wit_r, wlrt, ln_g, ln_b, share=()):
    m = x.shape[0]
    ns = len(share)
    steps = m // PROJ_TM
    src = [(0, 0), (0, D), (1, 2 * D), (1, 0), (1, D)]

    def body(*refs):
        x_ref, g_ref, sc_ref, sh_ref, wg_ref, wr_ref, wl_ref, lg_ref, lb_ref = refs[:9]
        share_refs = refs[9:9 + ns]
        h_ref, p_ref, plr_ref, vr_ref, vc_ref = refs[9 + ns:14 + ns]
        got_refs = refs[14 + ns:14 + 2 * ns]
        sems = refs[14 + 2 * ns:]

        def copies():
            cx, cy, cc = _coords()
            me = 2 * cx + cy
            peers = [(1 - cx, cy), (cx, 1 - cy), (1 - cx, 1 - cy)]
            out, back = [], []
            for k in range(ns):
                for j, (px, py) in enumerate(peers):
                    out.append(_remote(share_refs[k], got_refs[k].at[me], sems[0].at[3 * k + j], sems[1].at[3 * k + j], (px, py, cc)))
                    landed = got_refs[k].at[2 * px + py]
                    back.append(_remote(landed, landed, sems[0].at[3 * k + j], sems[1].at[3 * k + j], (px, py, cc)))
            return out, back

        if ns:
            @pl.when(pl.program_id(0) == 0)
            def _():
                for rc in copies()[0]:
                    rc.start()

            @pl.when(pl.program_id(0) == steps - 1)
            def _():
                out, back = copies()
                for rc in back:
                    rc.wait_recv()
                for rc in out:
                    rc.wait_send()

        xf = x_ref[...]
        r = lax.rsqrt(jnp.mean(xf * xf, axis=-1, keepdims=True) + EPS)
        y = (xf * r) * g_ref[...]
        h = (y * (1.0 + sc_ref[...]) + sh_ref[...]).astype(BF16)
        h_ref[...] = h
        for j, (which, r0) in enumerate(src):
            w_ref = wr_ref if which else wg_ref
            blk = _nt(h, w_ref[r0:r0 + D, :]).astype(BF16)
            p_ref[:, D * j:D * j + D] = blk
            if j == 4:
                xf = blk[:, 512:1024].astype(F32)
                xc = xf - jnp.mean(xf, axis=-1, keepdims=True)
                vn = (xc * lax.rsqrt(jnp.mean(xc * xc, axis=-1, keepdims=True) + EPS)) * lg_ref[...] + lb_ref[...]
                vr_ref[...] = vn[:, 0:256].astype(BF16)
                vc_ref[0] = vn[:, 256:384].astype(BF16)
                vc_ref[1] = vn[:, 384:512].astype(BF16)
        plr_ref[...] = _nt(h, wl_ref[...])

    row = pl.BlockSpec((PROJ_TM, D), lambda i: (i, 0))
    vec = _full((1, D))
    res = _pcall(
        body, name="proj_fwd", grid=(steps,),
        in_specs=[row, vec, vec, vec, _resident((2 * D, D)), _resident((3 * D, D)), _resident((LRW, D)), _full((1, 512)),
                  _full((1, 512))] + [ANY] * ns,
        out_specs=[row, pl.BlockSpec((PROJ_TM, NP), lambda i: (i, 0)), pl.BlockSpec((PROJ_TM, LRW), lambda i: (i, 0)),
                   pl.BlockSpec((PROJ_TM, 256), lambda i: (i, 0)), pl.BlockSpec((2, PROJ_TM, 128), lambda i: (0, i, 0))] + [ANY] * ns,
        out_shape=[SDS((m, D), BF16), SDS((m, NP), BF16), SDS((m, LRW), F32), SDS((m, 256), BF16), SDS((2, m, 128), BF16)]
        + [SDS((4,) + a.shape, a.dtype) for a in share],
        scratch_shapes=([pltpu.SemaphoreType.DMA((3 * ns,)), pltpu.SemaphoreType.DMA((3 * ns,))] if ns else []),
        compiler_params=_cp(("arbitrary",), VMEM_BIG),
    )(x, ng, scale, shift, wit_g, wit_r, wlrt, ln_g, ln_b, *share)
    return res[0], res[1], res[2], res[3], res[4], list(res[5:])


def _proj_bwd(dp_g, dp_r, dlr, wit_g, wit_r, wlrt, x, dx1, ng, scale, send=(), share8=None):
    m = x.shape[0]
    ns = len(send)
    n8 = 0 if share8 is None else 1
    steps = m // PROJ_TM
    masks = [(mx, my, mc) for mx in range(2) for my in range(2) for mc in range(2)][1:]

    def body(*refs):
        (dpg_ref, dpr_ref, dlr_ref, wg_ref, wr_ref, wl_ref, x_ref, r_ref, g_ref, sc_ref) = refs[:10]
        send_refs = refs[10:10 + ns]
        n_in = 10 + ns + n8
        dx_ref, dg_ref, dsc_ref, dsh_ref = refs[n_in:n_in + 4]
        got_refs = refs[n_in + 4:n_in + 4 + ns]
        sems = refs[n_in + 4 + ns + n8:]
        i = pl.program_id(0)

        def copies():
            cx, cy, cc = _coords()
            me = 2 * cx + cy
            peers = [(1 - cx, cy), (cx, 1 - cy), (1 - cx, 1 - cy)]
            out, back = [], []
            for k in range(ns):
                for j, (px, py) in enumerate(peers):
                    out.append(_remote(send_refs[k].at[2 * px + py], got_refs[k].at[me], sems[0].at[3 * k + j],
                                       sems[1].at[3 * k + j], (px, py, cc)))
                    landed = got_refs[k].at[2 * px + py]
                    back.append(_remote(landed, landed, sems[0].at[3 * k + j], sems[1].at[3 * k + j], (px, py, cc)))
            if n8:
                src8, all8 = refs[10 + ns], refs[n_in + 4 + ns]
                s8, r8 = sems[-2], sems[-1]
                for j, (mx, my, mc) in enumerate(masks):
                    px, py, pc = _flip(cx, mx), _flip(cy, my), _flip(cc, mc)
                    out.append(_remote(src8, all8.at[4 * cx + 2 * cy + cc], s8.at[j], r8.at[j], (px, py, pc)))
                    landed = all8.at[4 * px + 2 * py + pc]
                    back.append(_remote(landed, landed, s8.at[j], r8.at[j], (px, py, pc)))
            return out, back

        @pl.when(i == 0)
        def _():
            dg_ref[...] = jnp.zeros_like(dg_ref)
            dsc_ref[...] = jnp.zeros_like(dsc_ref)
            dsh_ref[...] = jnp.zeros_like(dsh_ref)
            if ns or n8:
                for rc in copies()[0]:
                    rc.start()

        dh_ = (_nn(dpg_ref[...], wg_ref[...]) + _nn(dpr_ref[...], wr_ref[...])
               + _nn(dlr_ref[...].astype(BF16), wl_ref[...]))
        xf = x_ref[...]
        r = lax.rsqrt(jnp.mean(xf * xf, axis=-1, keepdims=True) + EPS)
        xh = xf * r
        y = xh * g_ref[...]
        dsh_ref[...] += jnp.sum(dh_, axis=0, keepdims=True)
        dsc_ref[...] += jnp.sum(dh_ * y, axis=0, keepdims=True)
        dy = dh_ * (1.0 + sc_ref[...])
        dg_ref[...] += jnp.sum(dy * xh, axis=0, keepdims=True)
        dxh = dy * g_ref[...]
        dx_ref[...] = r * (dxh - xh * jnp.mean(dxh * xh, axis=-1, keepdims=True)) + r_ref[...]

        if ns or n8:
            @pl.when(i == steps - 1)
            def _():
                out, back = copies()
                for rc in back:
                    rc.wait_recv()
                for rc in out:
                    rc.wait_send()

    row = pl.BlockSpec((PROJ_TM, D), lambda i: (i, 0))
    vec = _full((1, D))
    kg, kr = dp_g.shape[1], dp_r.shape[1]
    extra_in = list(send) + ([share8] if n8 else [])
    extra_out = [SDS(a.shape, a.dtype) for a in send] + ([SDS((8,) + share8.shape, share8.dtype)] if n8 else [])
    res = _pcall(
        body, name="proj_bwd", grid=(steps,),
        in_specs=[pl.BlockSpec((PROJ_TM, kg), lambda i: (i, 0)), pl.BlockSpec((PROJ_TM, kr), lambda i: (i, 0)),
                  pl.BlockSpec((PROJ_TM, LRW), lambda i: (i, 0)), _resident((kg, D)), _resident((kr, D)), _resident((LRW, D)),
                  row, row, vec, vec] + [ANY] * len(extra_in),
        out_specs=[row, vec, vec, vec] + [ANY] * len(extra_out),
        out_shape=[SDS((m, D), F32), SDS((1, D), F32), SDS((1, D), F32), SDS((1, D), F32)] + extra_out,
        scratch_shapes=(([pltpu.SemaphoreType.DMA((3 * ns,)), pltpu.SemaphoreType.DMA((3 * ns,))] if ns else [])
                        + ([pltpu.SemaphoreType.DMA((7,)), pltpu.SemaphoreType.DMA((7,))] if n8 else [])),
        compiler_params=_cp(("arbitrary",), VMEM_BIG),
    )(dp_g, dp_r, dlr, wit_g, wit_r, wlrt, x, dx1, ng, scale, *extra_in)
    return tuple(res[:4]), list(res[4:4 + ns]), (res[4 + ns] if n8 else None)


def _prep_bwd(x, dh, dx1, ng, scale, name):
    m = x.shape[0]
    has_res = dx1 is not None

    def body(*refs):
        if has_res:
            x_ref, dh_ref, r_ref, g_ref, sc_ref, dx_ref, dg_ref, dsc_ref, dsh_ref = refs
        else:
            x_ref, dh_ref, g_ref, sc_ref, dx_ref, dg_ref, dsc_ref, dsh_ref = refs
        i = pl.program_id(0)

        @pl.when(i == 0)
        def _():
            dg_ref[...] = jnp.zeros_like(dg_ref)
            dsc_ref[...] = jnp.zeros_like(dsc_ref)
            dsh_ref[...] = jnp.zeros_like(dsh_ref)

        xf = x_ref[...]
        dh_ = dh_ref[...]
        r = lax.rsqrt(jnp.mean(xf * xf, axis=-1, keepdims=True) + EPS)
        xh = xf * r
        y = xh * g_ref[...]
        dsh_ref[...] += jnp.sum(dh_, axis=0, keepdims=True)
        dsc_ref[...] += jnp.sum(dh_ * y, axis=0, keepdims=True)
        dy = dh_ * (1.0 + sc_ref[...])
        dg_ref[...] += jnp.sum(dy * xh, axis=0, keepdims=True)
        dxh = dy * g_ref[...]
        dx = r * (dxh - xh * jnp.mean(dxh * xh, axis=-1, keepdims=True))
        if has_res:
            dx = dx + r_ref[...]
        dx_ref[...] = dx

    tok = min(TOK, m)
    row = pl.BlockSpec((tok, D), lambda i: (i, 0))
    vec = _full((1, D))
    in_specs = [row, row] + ([row] if has_res else []) + [vec, vec]
    args = [x, dh] + ([dx1] if has_res else []) + [ng, scale]
    return _pcall(
        body, name=name, grid=(m // tok,), in_specs=in_specs, out_specs=[row, vec, vec, vec],
        out_shape=[SDS((m, D), F32), SDS((1, D), F32), SDS((1, D), F32), SDS((1, D), F32)],
        compiler_params=_cp(("arbitrary",)),
    )(*args)


COLB = 2048


def _colmix_fwd(vnc, ws23, bs23):
    rows = vnc.shape[2] // COLB

    def body(v_ref, w_ref, b_ref, o_ref):
        o_ref[0] = _nn(w_ref[0], v_ref[0]) + b_ref[0]

    return _pcall(
        body, name="colmix_fwd", grid=(2, rows),
        in_specs=[pl.BlockSpec((1, AC, COLB), lambda g, j: (g, 0, j)), pl.BlockSpec((1, AC, AC), lambda g, j: (g, 0, 0)),
                  pl.BlockSpec((1, AC, 1), lambda g, j: (g, 0, 0))],
        out_specs=pl.BlockSpec((1, AC, COLB), lambda g, j: (g, 0, j)),
        out_shape=SDS(vnc.shape, F32), compiler_params=_cp(("parallel", "parallel")),
    )(vnc, ws23, bs23)


def _colmix_bwd(dsvc, vnc, ws23t):
    rows = vnc.shape[2] // COLB

    def body(d_ref, v_ref, wt_ref, dv_ref, dw_ref, db_ref):
        j = pl.program_id(1)

        @pl.when(j == 0)
        def _():
            dw_ref[...] = jnp.zeros_like(dw_ref)
            db_ref[...] = jnp.zeros_like(db_ref)

        d = d_ref[0]
        d16 = d.astype(BF16)
        dv_ref[0] = _nn(wt_ref[0], d16)
        dw_ref[0] += _nt(d16, v_ref[0])
        db_ref[0] += jnp.sum(d, axis=1, keepdims=True)

    blk = pl.BlockSpec((1, AC, COLB), lambda g, j: (g, 0, j))
    return _pcall(
        body, name="colmix_bwd", grid=(2, rows),
        in_specs=[blk, blk, pl.BlockSpec((1, AC, AC), lambda g, j: (g, 0, 0))],
        out_specs=[blk, pl.BlockSpec((1, AC, AC), lambda g, j: (g, 0, 0)), pl.BlockSpec((1, AC, 1), lambda g, j: (g, 0, 0))],
        out_shape=[SDS(vnc.shape, F32), SDS((2, AC, AC), F32), SDS((2, AC, 1), F32)],
        compiler_params=_cp(("parallel", "arbitrary")),
    )(dsvc, vnc, ws23t)


def _head_norm(o):
    out = []
    for h in range(4):
        oh = o[:, 128 * h:128 * h + 128]
        r = lax.rsqrt(jnp.mean(oh * oh, axis=-1, keepdims=True) + EPS)
        out.append((r, oh * r))
    return out


def _tail_fwd(o_f, o_b, p, vnr, svc, x, tgt, ws01, bs01, gbn, wpa, wpb, wo, gate, gf):
    m = p.shape[0]

    def body(of_ref, ob_ref, zb_ref, ua_ref, za_ref, ga_ref, gb_ref, vnr_ref, svc_ref, x_ref, t_ref, w_ref, b_ref, g_ref,
             wpa_ref, wpb_ref, wo_ref, gate_ref, gf_ref,
             ya_ref, yb_ref, svr_ref, dwo_ref, dx1_ref, dout_ref, loss_ref, dgate_ref, dgf_ref):
        i = pl.program_id(0)

        @pl.when(i == 0)
        def _():
            loss_ref[...] = jnp.zeros_like(loss_ref)
            dgate_ref[...] = jnp.zeros_like(dgate_ref)
            dgf_ref[...] = jnp.zeros_like(dgf_ref)
            dwo_ref[...] = jnp.zeros_like(dwo_ref)

        o = of_ref[...] + ob_ref[...]
        zb = zb_ref[...].astype(F32)
        for h, (r, xh) in enumerate(_head_norm(o)):
            sl = slice(128 * h, 128 * h + 128)
            yb_ref[:, sl] = ((xh * g_ref[:, sl]) * _silu(zb[:, sl])).astype(BF16)
        for j in range(TOK // AC):
            for g in range(2):
                sv = _nn(w_ref[g], vnr_ref[AC * j:AC * j + AC, AC * g:AC * g + AC]) + b_ref[g]
                svr_ref[AC * j:AC * j + AC, AC * g:AC * g + AC] = sv
        sz = _silu(za_ref[...].astype(F32))
        u = ua_ref[...].astype(F32)
        ya_ref[:, 0:256] = ((u[:, 0:256] * svr_ref[...]) * sz[:, 0:256]).astype(BF16)
        ya_ref[:, 256:384] = ((u[:, 256:384] * svc_ref[0]) * sz[:, 256:384]).astype(BF16)
        ya_ref[:, 384:512] = ((u[:, 384:512] * svc_ref[1]) * sz[:, 384:512]).astype(BF16)
        ya = _nn(ya_ref[...], wpa_ref[...])
        yb = _nn(yb_ref[...], wpb_ref[...])
        mg = (jax.nn.sigmoid(ga_ref[...].astype(F32)) * ya + jax.nn.sigmoid(gb_ref[...].astype(F32)) * yb).astype(BF16)
        out_ = _nn(mg, wo_ref[...])
        x1 = x_ref[...] + gate_ref[...] * out_
        r = lax.rsqrt(jnp.mean(x1 * x1, axis=-1, keepdims=True) + EPS)
        xh = x1 * r
        err = xh * gf_ref[...] - t_ref[...]
        loss_ref[...] += 0.5 * jnp.sum(jnp.mean(err * err, axis=-1, keepdims=True), axis=0, keepdims=True)
        dy = err * (1.0 / D)
        dgf_ref[...] += jnp.sum(dy * xh, axis=0, keepdims=True)
        dxh = dy * gf_ref[...]
        dx1 = r * (dxh - xh * jnp.mean(dxh * xh, axis=-1, keepdims=True))
        dx1_ref[...] = dx1
        dout16 = (gate_ref[...] * dx1).astype(BF16)
        dout_ref[...] = dout16
        dgate_ref[...] += jnp.sum(dx1 * out_, axis=0, keepdims=True)
        dwo_ref[...] += _tn(mg, dout16)

    r512 = pl.BlockSpec((TOK, 512), lambda i: (i, 0))
    row = pl.BlockSpec((TOK, D), lambda i: (i, 0))
    vec = _full((1, D))
    return _pcall(
        body, name="tail_fwd", grid=(m // TOK,),
        in_specs=[r512, r512, pl.BlockSpec((TOK, 512), lambda i: (i, 6)), pl.BlockSpec((TOK, 512), lambda i: (i, 7)),
                  pl.BlockSpec((TOK, 512), lambda i: (i, 8)), row, pl.BlockSpec((TOK, D), lambda i: (i, 1)),
                  pl.BlockSpec((TOK, 256), lambda i: (i, 0)), pl.BlockSpec((2, TOK, 128), lambda i: (0, i, 0)), row, row,
                  _full((2, AC, AC)), _full((2, AC, 1)), _full((1, 512)), _resident((512, D)), _resident((512, D)),
                  _resident((D, D)), vec, vec],
        out_specs=[r512, r512, pl.BlockSpec((TOK, 256), lambda i: (i, 0)), _resident((D, D)), row, row, _full((1, 128)), vec, vec],
        out_shape=[SDS((m, 512), BF16), SDS((m, 512), BF16), SDS((m, 256), F32), SDS((D, D), F32), SDS((m, D), F32),
                   SDS((m, D), BF16), SDS((1, 128), F32), SDS((1, D), F32), SDS((1, D), F32)],
        compiler_params=_cp(("arbitrary",), VMEM_BIG),
    )(o_f, o_b, p, p, p, p, p, vnr, svc, x, tgt, ws01, bs01, gbn, wpa, wpb, wo, gate, gf)


DPR = 3072


def _tail_bwd(dout, ya_in, yb_in, p, svr, svc, o_f, o_b, gbn, wo, wpa, wpb):
    m = p.shape[0]

    def body(dout_ref, ya_ref, yb_ref, ga_ref, gb_ref, zb_ref, ua_ref, za_ref, svr_ref, svc_ref, of_ref, ob_ref, g_ref,
             wo_ref, wpa_ref, wpb_ref,
             dwpa_ref, dwpb_ref, dpg_ref, dpr_ref, dsr_ref, dsc_ref, do_ref, dg_ref):
        i = pl.program_id(0)

        @pl.when(i == 0)
        def _():
            dg_ref[...] = jnp.zeros_like(dg_ref)
            dwpa_ref[...] = jnp.zeros_like(dwpa_ref)
            dwpb_ref[...] = jnp.zeros_like(dwpb_ref)

        dm_ = _nt(dout_ref[...], wo_ref[...])
        ya_in, yb_in = ya_ref[...], yb_ref[...]
        ya = _nn(ya_in, wpa_ref[...])
        yb = _nn(yb_in, wpb_ref[...])
        sa = jax.nn.sigmoid(ga_ref[...].astype(F32))
        sb = jax.nn.sigmoid(gb_ref[...].astype(F32))
        dya16 = (dm_ * sa).astype(BF16)
        dyb16 = (dm_ * sb).astype(BF16)
        dwpa, dwpb = _tn(ya_in, dya16), _tn(yb_in, dyb16)
        for s in range(4):
            dwpa_ref[s] += dwpa[:, 256 * s:256 * (s + 1)]
            dwpb_ref[s] += dwpb[:, 256 * s:256 * (s + 1)]
        dpg_ref[:, 0:D] = (dm_ * ya * (sa * (1.0 - sa))).astype(BF16)
        dpg_ref[:, D:2 * D] = (dm_ * yb * (sb * (1.0 - sb))).astype(BF16)
        dya = _nt(dya16, wpa_ref[...])
        dyb = _nt(dyb16, wpb_ref[...])

        u = ua_ref[...].astype(F32)
        za = za_ref[...].astype(F32)
        sz, dsz = _silu_and_grad(za)
        sv = jnp.concatenate([svr_ref[...], svc_ref[0], svc_ref[1]], axis=1)
        dpr_ref[:, 512:1024] = (dya * sv * sz).astype(BF16)
        dsv = dya * u * sz
        dsr_ref[...] = dsv[:, 0:256]
        dsc_ref[0] = dsv[:, 256:384]
        dsc_ref[1] = dsv[:, 384:512]
        dpr_ref[:, 1024:1536] = (dya * u * sv * dsz).astype(BF16)

        zb = zb_ref[...].astype(F32)
        o = of_ref[...] + ob_ref[...]
        szb, dszb = _silu_and_grad(zb)
        for h, (r, xh) in enumerate(_head_norm(o)):
            sl = slice(128 * h, 128 * h + 128)
            gh = g_ref[:, sl]
            don = dyb[:, sl] * szb[:, sl]
            dpr_ref[:, sl] = (dyb[:, sl] * (xh * gh) * dszb[:, sl]).astype(BF16)
            dg_ref[:, sl] += jnp.sum(don * xh, axis=0, keepdims=True)
            dxh = don * gh
            do_ref[:, sl] = (r * (dxh - xh * jnp.mean(dxh * xh, axis=-1, keepdims=True))).astype(BF16)

    r512 = pl.BlockSpec((TOK, 512), lambda i: (i, 0))
    row = pl.BlockSpec((TOK, D), lambda i: (i, 0))
    return _pcall(
        body, name="tail_bwd", grid=(m // TOK,),
        in_specs=[row, r512, r512, row, pl.BlockSpec((TOK, D), lambda i: (i, 1)), pl.BlockSpec((TOK, 512), lambda i: (i, 6)),
                  pl.BlockSpec((TOK, 512), lambda i: (i, 7)), pl.BlockSpec((TOK, 512), lambda i: (i, 8)),
                  pl.BlockSpec((TOK, 256), lambda i: (i, 0)), pl.BlockSpec((2, TOK, 128), lambda i: (0, i, 0)), r512, r512,
                  _full((1, 512)), _resident((D, D)), _resident((512, D)), _resident((512, D))],
        out_specs=[_resident((4, 512, 256)), _resident((4, 512, 256)), pl.BlockSpec((TOK, 2 * D), lambda i: (i, 0)),
                   pl.BlockSpec((TOK, 1536), lambda i: (i, 0)),
                   pl.BlockSpec((TOK, 256), lambda i: (i, 0)), pl.BlockSpec((2, TOK, 128), lambda i: (0, i, 0)), r512, _full((1, 512))],
        out_shape=[SDS((4, 512, 256), F32), SDS((4, 512, 256), F32), SDS((m, 2 * D), BF16), SDS((m, DPR), BF16), SDS((m, 256), F32),
                   SDS((2, m, 128), F32), SDS((m, 512), BF16), SDS((1, 512), F32)],
        compiler_params=_cp(("arbitrary",), VMEM_BIG),
    )(dout, ya_in, yb_in, p, p, p, p, p, svr, svc, o_f, o_b, gbn, wo, wpa, wpb)


def _ln_bwd(dsr, vnr, dvnc, p, ws01t, ln_g, dp):
    m = p.shape[0]

    def body(dsr_ref, vnr_ref, dvc_ref, va_ref, wt_ref, g_ref, dpi_ref, dp_ref, dw_ref, db_ref, dlg_ref, dlb_ref, dvn_ref):
        i = pl.program_id(0)

        @pl.when(i == 0)
        def _():
            dw_ref[...] = jnp.zeros_like(dw_ref)
            db_ref[...] = jnp.zeros_like(db_ref)
            dlg_ref[...] = jnp.zeros_like(dlg_ref)
            dlb_ref[...] = jnp.zeros_like(dlb_ref)

        for j in range(TOK // AC):
            for g in range(2):
                d = dsr_ref[AC * j:AC * j + AC, AC * g:AC * g + AC]
                d16 = d.astype(BF16)
                dvn_ref[AC * j:AC * j + AC, AC * g:AC * g + AC] = _nn(wt_ref[g], d16)
                dw_ref[g] += _nt(d16, vnr_ref[AC * j:AC * j + AC, AC * g:AC * g + AC])
                db_ref[g] += jnp.sum(d, axis=1, keepdims=True)
        dvn_ref[:, 256:384] = dvc_ref[0]
        dvn_ref[:, 384:512] = dvc_ref[1]
        dvn = dvn_ref[...]
        xf = va_ref[...].astype(F32)
        xc = xf - jnp.mean(xf, axis=-1, keepdims=True)
        rs = lax.rsqrt(jnp.mean(xc * xc, axis=-1, keepdims=True) + EPS)
        xh = xc * rs
        dlg_ref[...] += jnp.sum(dvn * xh, axis=0, keepdims=True)
        dlb_ref[...] += jnp.sum(dvn, axis=0, keepdims=True)
        dxh = dvn * g_ref[...]
        dva = rs * (dxh - jnp.mean(dxh, axis=-1, keepdims=True) - xh * jnp.mean(dxh * xh, axis=-1, keepdims=True))
        dp_ref[...] = dva.astype(BF16)

    return _pcall(
        body, name="ln_bwd", grid=(m // TOK,),
        in_specs=[pl.BlockSpec((TOK, 256), lambda i: (i, 0)), pl.BlockSpec((TOK, 256), lambda i: (i, 0)),
                  pl.BlockSpec((2, TOK, 128), lambda i: (0, i, 0)), pl.BlockSpec((TOK, 512), lambda i: (i, 9)),
                  _full((2, AC, AC)), _full((1, 512)), pl.BlockSpec(memory_space=pl.ANY)],
        out_specs=[pl.BlockSpec((TOK, 512), lambda i: (i, 3)), _full((2, AC, AC)), _full((2, AC, 1)), _full((1, 512)), _full((1, 512))],
        out_shape=[SDS((m, DPR), BF16), SDS((2, AC, AC), F32), SDS((2, AC, 1), F32), SDS((1, 512), F32), SDS((1, 512), F32)],
        scratch_shapes=[pltpu.VMEM((TOK, 512), F32)],
        input_output_aliases={6: 0}, compiler_params=_cp(("arbitrary",)),
    )(dsr, vnr, dvnc, p, ws01t, ln_g, dp)


def _tri_mm(tri, a):
    a1 = a.astype(BF16)
    r1 = a - a1.astype(F32)
    a2 = r1.astype(BF16)
    a3 = (r1 - a2.astype(F32)).astype(BF16)
    n = a.shape[1]
    r = _nn(tri, jnp.concatenate([a1, a2, a3], axis=1))
    return r[:, 0:n] + r[:, n:2 * n] + r[:, 2 * n:3 * n]


def _gla_masks(reverse):
    ri = lax.broadcasted_iota(jnp.int32, (CH, CH), 0)
    ci = lax.broadcasted_iota(jnp.int32, (CH, CH), 1)
    vis = (ci >= ri) if reverse else (ci <= ri)
    vis_t = (ci <= ri) if reverse else (ci >= ri)
    r4 = lax.broadcasted_iota(jnp.int32, (4 * CH, CH), 0) & (CH - 1)
    c4 = lax.broadcasted_iota(jnp.int32, (4 * CH, CH), 1)
    vis4 = (c4 >= r4) if reverse else (c4 <= r4)
    vis4_t = (c4 <= r4) if reverse else (c4 >= r4)
    lane = lax.broadcasted_iota(jnp.int32, (1, 256), 1)
    hm = [(lane >= CH * h) & (lane < CH * h + CH) for h in range(4)]
    return vis, vis_t, vis4, vis4_t, hm


def _stack_heads(x, hm):
    return jnp.concatenate([jnp.where(hm[h], x, 0.0).astype(BF16) for h in range(4)], axis=0)


def _diag_heads(full, hm):
    r = full.shape[0] // 4
    acc = jnp.where(hm[0], full[0:r], 0.0)
    for h in range(1, 4):
        acc = acc + jnp.where(hm[h], full[r * h:r * h + r], 0.0)
    return acc


def _rows_of_heads(x):
    return jnp.concatenate([x[:, 128 * h:128 * h + 128] for h in range(4)], axis=0)


def _lane_vis(reverse, transpose):
    ri = lax.broadcasted_iota(jnp.int32, (CH, 4 * CH), 0)
    ci = lax.broadcasted_iota(jnp.int32, (CH, 4 * CH), 1) & (CH - 1)
    return (ci >= ri) if (reverse != transpose) else (ci <= ri)


def _gla_fwd2(p, qkv_blk, lr, lrws, gbiases, s0s, name):
    m = p.shape[0]
    tb = min(GLA_TB, m)
    nb = m // tb
    nc = tb // CH

    def body(qkv_f, lr_f, qkv_b, lr_b, lrw_f, lrw_b, gb_f, gb_b, s0_f, s0_b,
             o_f, sb_f, sfin_f, o_b, sb_b, sfin_b, st_f, st_b):
        i = pl.program_id(0)

        @pl.when(i == 0)
        def _():
            st_f[...] = s0_f[...]
            st_b[...] = s0_b[...]

        dirs = []
        for reverse, qkv_ref, lr_ref, lrw_ref, gb_ref, o_ref, sb_ref, st_ref in (
                (False, qkv_f, lr_f, lrw_f, gb_f, o_f, sb_f, st_f), (True, qkv_b, lr_b, lrw_b, gb_b, o_b, sb_b, st_b)):
            vis, _, vis4, _, hm = _gla_masks(reverse)
            logits = _nn(lr_ref[...].astype(BF16), lrw_ref[...]) + gb_ref[...]
            dirs.append(dict(reverse=reverse, qkv=qkv_ref, o=o_ref, sb=sb_ref, vis4=vis4, hm=hm,
                             tri=vis.astype(F32).astype(BF16), a=_logsig(logits) * (1.0 / 16.0), st=st_ref[...]))
        for step in range(nc):
            for d in dirs:
                c = nc - 1 - step if d["reverse"] else step
                rows = slice(CH * c, CH * c + CH)
                b = _tri_mm(d["tri"], d["a"][rows])
                bl = b[0:1] if d["reverse"] else b[CH - 1:CH]
                q = d["qkv"][rows, 0:256].astype(F32) * 0.125
                k = d["qkv"][rows, 256:512].astype(F32)
                v16 = d["qkv"][rows, 512:1024].astype(BF16)
                qd = q * jnp.exp(b)
                kd16 = (k * jnp.exp(-b)).astype(BF16)
                kdec16 = (k * jnp.exp(bl - b)).astype(BF16)
                qstack = _stack_heads(qd, d["hm"])
                sc = jnp.where(d["vis4"], _nt(qstack, kd16), 0.0).astype(BF16)
                inter = _nt(qstack, d["st"].astype(BF16))
                for h in range(4):
                    d["o"][rows, 128 * h:128 * h + 128] = (
                        _nn(sc[CH * h:CH * h + CH], v16[:, 128 * h:128 * h + 128]) + inter[CH * h:CH * h + CH])
                d["sb"][c] = d["st"]
                d["st"] = d["st"] * jnp.exp(bl) + _diag_heads(_tn(v16, kdec16), d["hm"])
        st_f[...] = dirs[0]["st"]
        st_b[...] = dirs[1]["st"]

        @pl.when(i == nb - 1)
        def _():
            sfin_f[...] = dirs[0]["st"]
            sfin_b[...] = dirs[1]["st"]

    fw = lambda i: i
    bw = lambda i: nb - 1 - i
    in_specs = []
    for rm in (fw, bw):
        in_specs += [pl.BlockSpec((tb, 1024), lambda i, rm=rm: (rm(i), qkv_blk)), pl.BlockSpec((tb, LRW), lambda i, rm=rm: (rm(i), 0))]
    in_specs += [_full((LRW, 256))] * 2 + [_full((1, 256))] * 2 + [_full((128, 256))] * 2
    out_specs, out_shape = [], []
    for rm in (fw, bw):
        out_specs += [pl.BlockSpec((tb, 512), lambda i, rm=rm: (rm(i), 0)), pl.BlockSpec((nc, 128, 256), lambda i, rm=rm: (rm(i), 0, 0)),
                      _full((128, 256))]
        out_shape += [SDS((m, 512), F32), SDS((m // CH, 128, 256), F32), SDS((128, 256), F32)]
    return _pcall(
        body, name=name, grid=(nb,), in_specs=in_specs, out_specs=out_specs, out_shape=out_shape,
        scratch_shapes=[pltpu.VMEM((128, 256), F32), pltpu.VMEM((128, 256), F32)], compiler_params=_cp(("arbitrary",), VMEM_BIG),
    )(p, lr, p, lr, lrws[0], lrws[1], gbiases[0], gbiases[1], s0s[0], s0s[1])


def _gla_bwd(p, qkv_blk, lr, lrw, lrwt, gbias, sb, dsfin, do, prev, dp, *, reverse, name):
    m = p.shape[0]
    tb = min(GLA_TB, m)
    nb = m // tb
    nc = tb // CH
    rmap = (lambda i: i) if reverse else (lambda i: nb - 1 - i)
    has_prev = prev is not None
    has_dp = dp is not None

    def body(*refs):
        refs = list(refs)
        qkv_ref, lr_ref, lrw_ref, lrwt_ref, gb_ref, sb_ref, dsfin_ref, do_ref = refs[:8]
        refs = refs[8:]
        if has_prev:
            pq_ref, plr_ref = refs[:2]
            refs = refs[2:]
        if has_dp:
            refs = refs[1:]
        dqkv_ref, dlr_ref, dw2_ref, dgb_ref, ds0_ref, dst_ref, dlog_ref = refs
        i = pl.program_id(0)

        @pl.when(i == 0)
        def _():
            dst_ref[...] = dsfin_ref[...]
            dw2_ref[...] = jnp.zeros_like(dw2_ref)
            dgb_ref[...] = jnp.zeros_like(dgb_ref)

        vis, vis_t, vis4, vis4_t, hm = _gla_masks(reverse)
        tri = vis.astype(F32).astype(BF16)
        tri_t = vis_t.astype(F32).astype(BF16)
        lane_vis = _lane_vis(reverse, False)
        lane_vis_t = _lane_vis(reverse, True)
        lr16 = lr_ref[...].astype(BF16)
        logits = _nn(lr16, lrw_ref[...]) + gb_ref[...]
        a_all = _logsig(logits) * (1.0 / 16.0)
        dsig = (1.0 - jax.nn.sigmoid(logits)) * (1.0 / 16.0)
        dst = dst_ref[...]
        for c in (range(nc) if reverse else range(nc - 1, -1, -1)):
            rows = slice(CH * c, CH * c + CH)
            b = _tri_mm(tri, a_all[rows])
            bl = b[0:1] if reverse else b[CH - 1:CH]
            eb = jnp.exp(b)
            enb = jnp.exp(-b)
            ebl = jnp.exp(bl - b)
            el = jnp.exp(bl)
            q = qkv_ref[rows, 0:256].astype(F32) * 0.125
            k = qkv_ref[rows, 256:512].astype(F32)
            v16 = qkv_ref[rows, 512:1024].astype(BF16)
            do16 = do_ref[rows, :].astype(BF16)
            qd = q * eb
            kd = k * enb
            kdec = k * ebl
            st = sb_ref[c]
            st16 = st.astype(BF16)
            dst16 = dst.astype(BF16)
            qd16 = qd.astype(BF16)
            kd16 = kd.astype(BF16)
            qstack = _stack_heads(qd, hm)
            kstack = _stack_heads(kd, hm)
            kdecstack = _stack_heads(kdec, hm)
            pt = jnp.where(vis4_t, _nt(kstack, qd16), 0.0).astype(BF16)
            dvinter = _nt(kdecstack, dst16)
            do_rows = _rows_of_heads(do16)
            v_rows = _rows_of_heads(v16)
            dp_cat = jnp.where(lane_vis, _diag_heads(_nt(do_rows, v_rows), hm), 0.0).astype(BF16)
            dpt_cat = jnp.where(lane_vis_t, _diag_heads(_nt(v_rows, do_rows), hm), 0.0).astype(BF16)
            dqd = _nn(dp_cat, kstack) + _diag_heads(_nn(do_rows, st16), hm)
            dkd = _nn(dpt_cat, qstack)
            dkdec = _diag_heads(_nn(v_rows, dst16), hm)
            for h in range(4):
                rh = slice(CH * h, CH * h + CH)
                dv_h = _nn(pt[rh], do_rows[rh]) + dvinter[rh]
                if has_prev:
                    dv_h = dv_h + pq_ref[rows, 512 + 128 * h:512 + 128 * h + 128]
                dqkv_ref[rows, 512 + 128 * h:512 + 128 * h + 128] = dv_h.astype(dqkv_ref.dtype)
            dq = dqd * eb * 0.125
            dk = dkd * enb + dkdec * ebl
            if has_prev:
                dq = dq + pq_ref[rows, 0:256]
                dk = dk + pq_ref[rows, 256:512]
            dqkv_ref[rows, 0:256] = dq.astype(dqkv_ref.dtype)
            dqkv_ref[rows, 256:512] = dk.astype(dqkv_ref.dtype)
            g_kdec = dkdec * kdec
            db = dqd * qd - dkd * kd - g_kdec
            dbl = jnp.sum(g_kdec, axis=0, keepdims=True) + jnp.sum(st * dst, axis=0, keepdims=True) * el
            da = _tri_mm(tri_t, db) + dbl
            dlog_ref[rows, :] = da * dsig[rows]
            dst = dst * el + _diag_heads(_tn(do16, qd16), hm)
        dst_ref[...] = dst
        dlog = dlog_ref[...]
        dlog16 = dlog.astype(BF16)
        dlr = _nn(dlog16, lrwt_ref[...])
        if has_prev:
            dlr = dlr + plr_ref[...]
        dlr_ref[...] = dlr
        dw2_ref[...] += _tn(lr16, dlog16)
        dgb_ref[...] += jnp.sum(dlog, axis=0, keepdims=True)

        @pl.when(i == nb - 1)
        def _():
            ds0_ref[...] = dst

    in_specs = [pl.BlockSpec((tb,1024), lambda i: (rmap(i), qkv_blk)), pl.BlockSpec((tb,LRW), lambda i: (rmap(i), 0)),
                _full((LRW, 256)), _full((256, LRW)), _full((1, 256)), pl.BlockSpec((nc, 128, 256), lambda i: (rmap(i), 0, 0)),
                _full((128, 256)), pl.BlockSpec((tb,512), lambda i: (rmap(i), 0))]
    args = [p, lr, lrw, lrwt, gbias, sb, dsfin, do]
    if has_prev:
        in_specs += [pl.BlockSpec((tb,1024), lambda i: (rmap(i), 0)), pl.BlockSpec((tb,LRW), lambda i: (rmap(i), 0))]
        args += list(prev)
    aliases = {}
    if has_dp:
        in_specs.append(pl.BlockSpec(memory_space=pl.ANY))
        aliases = {len(args): 0}
        args.append(dp)
        dq_spec = pl.BlockSpec((tb,1024), lambda i: (rmap(i), 2))
        dq_shape = SDS(dp.shape, dp.dtype)
    else:
        dq_spec = pl.BlockSpec((tb,1024), lambda i: (rmap(i), 0))
        dq_shape = SDS((m, 1024), F32)
    return _pcall(
        body, name=name, grid=(nb,), in_specs=in_specs,
        out_specs=[dq_spec, pl.BlockSpec((tb,LRW), lambda i: (rmap(i), 0)), _full((LRW, 256)), _full((1, 256)), _full((128, 256))],
        out_shape=[dq_shape, SDS((m, LRW), F32), SDS((LRW, 256), F32), SDS((1, 256), F32), SDS((128, 256), F32)],
        scratch_shapes=[pltpu.VMEM((128, 256), F32), pltpu.VMEM((tb,256), F32)],
        input_output_aliases=aliases, compiler_params=_cp(("arbitrary",)),
    )(*args)


EARLY_KEYS = ["dmodc", "dscc", "dng_c", "dlng", "dlnb", "dws", "dbs", "dgbn", "dgf", "dw2", "dgb2", "loss", "dgate"]
EARLY_SHAPES = [(1, 2 * D), (D,), (1, D), (1, 512), (1, 512), (1, 4, 128, 128), (1, 4, 128), (1, 512), (D,), (2, 16, 256), (2, 256),
                (128,), (1, D)]
EARLY_SIZE = 2 * D + D + D + 512 + 512 + 4 * 128 * 128 + 512 + 512 + D + 2 * 16 * 256 + 512 + 128 + D
EARLY_ROWS = 648


def _device_step(x, c, ctx, c_ctx, tgt, wm, bm, ng, wit_g, wit_r, wlrt, ln_g, ln_b, ws, bs, w2, gb2, gbn, wpa, wpb, wo, gf,
                 exchange=None, shards=None):
    L = x.shape[0]
    wit_qkv = wit_r[2048:3072]
    ws16 = ws.astype(BF16)
    wst16 = jnp.swapaxes(ws, 1, 2).astype(BF16)
    bscol = bs[:, :, None]
    lrw = [jnp.zeros((LRW, 256), F32).at[16 * r:16 * r + 16].set(w2[r]).astype(BF16) for r in range(2)]
    lrwt = [w.T for w in lrw]
    gbias = [gb2[r:r + 1] for r in range(2)]

    cc = jnp.zeros((8, D), F32).at[0:1].set(c).at[1:2].set(c_ctx)
    mod = _modvec(cc, wm, bm)
    shift, scale, gate = mod[0:1, 0:D], mod[0:1, D:2 * D], mod[0:1, 2 * D:3 * D]
    shift_c, scale_c = mod[1:2, 0:D], mod[1:2, D:2 * D]

    hc = _prep_h(ctx, ng, scale_c, shift_c, "prep_hc")
    pc = _mm(hc, wit_qkv, tm=256, tn=1024, tk=D, out_dtype=F32, name="mm_pc", b_t=True)
    plrc = _mm(hc, wlrt, tm=256, tn=LRW, tk=D, out_dtype=F32, name="mm_plrc", b_t=True)
    zero_s = jnp.zeros((128, 256), F32)
    _, sbc_f, sc_f, _, sbc_b, sc_b = _gla_fwd2(pc, 0, plrc, lrw, gbias, (zero_s, zero_s), "gla_fwd_c")

    h, p, plr, vnr, vnc, late = _proj_fwd(x, ng, scale, shift, wit_g, wit_r, wlrt, ln_g, ln_b,
                                          shards if shards is not None else ())
    if shards is not None:
        me_xy = 2 * lax.axis_index("x") + lax.axis_index("y")
        g_wpa, g_wpb, g_wo = (_own(g, s_, me_xy) for g, s_ in zip(late, shards))
        wpa = jnp.swapaxes(g_wpa, 0, 1).reshape(512, D)
        wpb = jnp.swapaxes(g_wpb, 0, 1).reshape(512, D)
        wo = g_wo.reshape(D, D)
    o_f, sb_f, _, o_b, sb_b, _ = _gla_fwd2(p, 2, plr, lrw, gbias, (sc_f, sc_b), "gla_fwd")
    svc = _colmix_fwd(vnc.reshape(2, AC, L), ws16[2:4], bscol[2:4]).reshape(2, L, 128)
    ya_in, yb_in, svr, dwo, dx1, dout, loss, dgate, dgf = _tail_fwd(
        o_f, o_b, p, vnr, svc, x, tgt, ws16[0:2], bscol[0:2], gbn, wpa, wpb, wo, gate, gf)

    dwpa, dwpb, dp_g, dp, dsr, dsc, do, dgbn = _tail_bwd(dout, ya_in, yb_in, p, svr, svc, o_f, o_b, gbn, wo, wpa, wpb)
    dvnc, dws23, dbs23 = _colmix_bwd(dsc.reshape(2, AC, L), vnc.reshape(2, AC, L), wst16[2:4])
    dp, dws01, dbs01, dlng, dlnb = _ln_bwd(dsr, vnr, dvnc.reshape(2, L, 128), p, wst16[0:2], ln_g, dp)
    zero_ds = jnp.zeros((128, 256), F32)
    dqkv_f, dlr_f, dw2_f, dgb_f, ds0_f = _gla_bwd(p, 2, plr, lrw[0], lrwt[0], gbias[0], sb_f, zero_ds, do, None, None,
                                                  reverse=False, name="gla_bwd_f")
    dp, dlr, dw2_b, dgb_b, ds0_b = _gla_bwd(p, 2, plr, lrw[1], lrwt[1], gbias[1], sb_b, zero_ds, do, (dqkv_f, dlr_f), dp,
                                            reverse=True, name="gla_bwd_b")
    zero_do = jnp.zeros((ctx.shape[0], 512), BF16)
    dqkvc_f, dlrc_f, dw2c_f, dgbc_f, _ = _gla_bwd(pc, 0, plrc, lrw[0], lrwt[0], gbias[0], sbc_f, ds0_f, zero_do, None, None,
                                                  reverse=False, name="gla_bwd_cf")
    dqkvc, dlrc, dw2c_b, dgbc_b, _ = _gla_bwd(pc, 0, plrc, lrw[1], lrwt[1], gbias[1], sbc_b, ds0_b, zero_do,
                                              (dqkvc_f, dlrc_f), None, reverse=True, name="gla_bwd_cb")
    dhc = _mm(dqkvc, wit_qkv, tm=256, tn=D, tk=1024, out_dtype=F32, name="mm_dhc")
    dhc = _mm(dlrc, wlrt, tm=256, tn=D, tk=LRW, out_dtype=F32, name="mm_dhc_lr", acc=dhc)
    _, dng_c, dscale_c, dshift_c = _prep_bwd(ctx, dhc, None, ng, scale_c, "prep_bwd_c")

    if exchange is not None:
        ready = exchange[0](dwpa, dwpb, dwo)
        dwit_g, ready_other = _mm_tn(dp_g, h, ta=1024, tn=D, tk=2048, name="mm_dwi_g", out_dtype=BF16, swap=ready)
    else:
        dwit_g = _mm_tn(dp_g, h, ta=1024, tn=D, tk=2048, name="mm_dwi_g", out_dtype=BF16)
    dwit_r =_mm_tn(dp, h, ta=1024, tn=D, tk=2048, name="mm_dwi_r", out_dtype=BF16)
    dwit_qkv = _mm_tn(dqkvc, hc, ta=1024, tn=D, tk=256, name="mm_dwi_c", acc=dwit_r[2048:3072], out_dtype=BF16)
    dwlrt = _mm_tn(dlr, h, ta=LRW, tn=D, tk=2048, name="mm_dwlr")
    dwlrt = _mm_tn(dlrc, hc, ta=LRW, tn=D, tk=256, name="mm_dwlr_c", acc=dwlrt, out_dtype=BF16)
    big = dict(dwit_g=dwit_g, dwit_r=dwit_r, dwit_qkv=dwit_qkv, dwlrt=dwlrt, dwpa=dwpa, dwpb=dwpb, dwo=dwo)

    dmodc = jnp.concatenate([dshift_c, dscale_c], axis=1)
    dscc = _dcctx(jnp.zeros((8, 2 * D), F32).at[0:1].set(dmodc), wm)[0:1]
    dw2p = dw2_f + dw2c_f, dw2_b + dw2c_b
    small = dict(
        dmodc=dmodc, dscc=dscc, dng_c=dng_c, dlng=dlng, dlnb=dlnb, dws=jnp.concatenate([dws01, dws23], axis=0),
        dbs=jnp.concatenate([dbs01, dbs23], axis=0)[:, :, 0], dgbn=dgbn, dgf=dgf,
        dw2=jnp.stack([dw2p[0][0:16], dw2p[1][16:32]]), dgb2=jnp.concatenate([dgb_f + dgbc_f, dgb_b + dgbc_b], axis=0),
        loss=loss[0, 0], dgate=dgate)

    send = exchange[1](big, ready, ready_other) if exchange is not None else ()
    early = _pack([small[k] for k in EARLY_KEYS[:-2]] + [jnp.broadcast_to(small["loss"], (128,)), small["dgate"]], EARLY_ROWS) \
        if exchange is not None else None
    (dx, dng, dscale, dshift), got, early_all = _proj_bwd(dp_g, dp, dlr, wit_g, wit_r, wlrt, x, dx1, ng, scale, send, early)
    return dict(dx=dx, got=got, early=early, early_all=early_all, dshift=dshift, dscale=dscale, dng_lat=dng, **big, **small)


ANY = pl.BlockSpec(memory_space=pl.ANY)


def _coords():
    return lax.axis_index("x"), lax.axis_index("y"), lax.axis_index("c")


def _flip(v, bit):
    return 1 - v if bit else v


def _remote(src, dst, send_sem, recv_sem, dev):
    return pltpu.make_async_remote_copy(src_ref=src, dst_ref=dst, send_sem=send_sem, recv_sem=recv_sem,
                                        device_id=dev, device_id_type=MESH)


def _own(out, block, idx):
    return lax.dynamic_update_slice_in_dim(out, block[None], idx, axis=0)


def _half_idx(shape, axis, which, lead=()):
    idx = [pl.ds(0, d) for d in shape]
    h = shape[axis] // 2
    idx[axis] = pl.ds(which * h, h)
    return tuple(lead) + tuple(idx)


def _gather_weights(split, whole, name):
    ns, nw = len(split), len(whole)
    n = ns + nw
    arrs = [a for a, _ in split] + list(whole)

    def body(*refs):
        ins, outs = refs[:n], refs[n:2 * n]
        a_send, a_recv, b_send, b_recv = refs[2 * n:]
        x, y, c = _coords()
        me = 2 * x + y
        sib = (x, y, 1 - c)
        peers = [(1 - x, y), (x, 1 - y), (1 - x, 1 - y)]

        def half(k, slot, which):
            return outs[k].at[_half_idx(arrs[k].shape, split[k][1], which, lead=(slot,))]

        sends = []
        for k in range(n):
            for j, (px, py) in enumerate(peers):
                if k < ns:
                    rc = _remote(ins[k].at[_half_idx(arrs[k].shape, split[k][1], c)], half(k, me, c), a_send.at[3 * k + j],
                                 a_recv.at[3 * k + j], (px, py, c))
                else:
                    rc = _remote(ins[k], outs[k].at[me], a_send.at[3 * k + j], a_recv.at[3 * k + j], (px, py, c))
                rc.start()
                sends.append(rc)
        for k in range(ns):
            for j, (px, py) in enumerate(peers):
                landed = half(k, 2 * px + py, c)
                _remote(landed, landed, a_send.at[3 * k + j], a_recv.at[3 * k + j], (px, py, c)).wait_recv()
                fw = _remote(landed, landed, b_send.at[3 * k + j], b_recv.at[3 * k + j], sib)
                fw.start()
                sends.append(fw)
        for k in range(ns, n):
            for j, (px, py) in enumerate(peers):
                landed = outs[k].at[2 * px + py]
                _remote(landed, landed, a_send.at[3 * k + j], a_recv.at[3 * k + j], (px, py, c)).wait_recv()
        for k in range(ns):
            for j, (px, py) in enumerate(peers):
                passed = half(k, 2 * px + py, 1 - c)
                _remote(passed, passed, b_send.at[3 * k + j], b_recv.at[3 * k + j], sib).wait_recv()
        for rc in sends:
            rc.wait_send()

    outs = _pcall(
        body, name=name, in_specs=[ANY] * n, out_specs=[ANY] * n,
        out_shape=[SDS((4,) + a.shape, a.dtype) for a in arrs],
        scratch_shapes=[pltpu.SemaphoreType.DMA((3 * n,)), pltpu.SemaphoreType.DMA((3 * n,)), pltpu.SemaphoreType.DMA((3 * ns,)),
                        pltpu.SemaphoreType.DMA((3 * ns,))],
    )(*arrs)
    me_xy = 2 * lax.axis_index("x") + lax.axis_index("y")
    return [_own(o, a, me_xy) for o, a in zip(outs, arrs)]


def _gather_all(a, swap, name):
    masks = [(mx, my, mc) for mx in range(2) for my in range(2) for mc in range(2)][1:]
    n = len(swap)

    def body(*refs):
        in_ref, sw_in = refs[0], refs[1:1 + n]
        out_ref, sw_out = refs[1 + n], refs[2 + n:2 + 2 * n]
        send_sems, recv_sems = refs[2 + 2 * n:]
        x, y, c = _coords()
        me = 4 * x + 2 * y + c
        sends = []
        for j, (mx, my, mc) in enumerate(masks):
            rc = _remote(in_ref, out_ref.at[me], send_sems.at[j], recv_sems.at[j], (_flip(x, mx), _flip(y, my), _flip(c, mc)))
            rc.start()
            sends.append(rc)
        for k in range(n):
            rc = _remote(sw_in[k], sw_out[k], send_sems.at[7 + k], recv_sems.at[7 + k], (x, y, 1 - c))
            rc.start()
            sends.append(rc)
        for j, (mx, my, mc) in enumerate(masks):
            px, py, pc = _flip(x, mx), _flip(y, my), _flip(c, mc)
            landed = out_ref.at[4 * px + 2 * py + pc]
            _remote(landed, landed, send_sems.at[j], recv_sems.at[j], (px, py, pc)).wait_recv()
        for k in range(n):
            _remote(sw_out[k], sw_out[k], send_sems.at[7 + k], recv_sems.at[7 + k], (x, y, 1 - c)).wait_recv()
        for rc in sends:
            rc.wait_send()

    res = _pcall(
        body, name=name, in_specs=[ANY] * (1 + n), out_specs=[ANY] * (1 + n),
        out_shape=[SDS((8,) + a.shape, a.dtype)] + [SDS(s_.shape, s_.dtype) for s_ in swap],
        scratch_shapes=[pltpu.SemaphoreType.DMA((7 + n,)), pltpu.SemaphoreType.DMA((7 + n,))],
    )(a, *swap)
    return _own(res[0], a, 4 * lax.axis_index("x") + 2 * lax.axis_index("y") + lax.axis_index("c")), list(res[1:])


def _half_shape(shape, axis):
    return tuple(d // 2 if i == axis else d for i, d in enumerate(shape))


def _swap_half_c(arrs, axes, name):
    n = len(arrs)

    def body(*refs):
        ins, outs = refs[:n], refs[n:2 * n]
        send_sems, recv_sems = refs[2 * n:]
        x, y, c = _coords()
        sends = []
        for k in range(n):
            rc = _remote(ins[k].at[_half_idx(arrs[k].shape, axes[k], 1 - c)], outs[k], send_sems.at[k], recv_sems.at[k],
                         (x, y, 1 - c))
            rc.start()
            sends.append(rc)
        for rc in sends:
            rc.wait()

    return _pcall(
        body, name=name, in_specs=[ANY] * n, out_specs=[ANY] * n,
        out_shape=[SDS(_half_shape(a.shape, ax), a.dtype) for a, ax in zip(arrs, axes)],
        scratch_shapes=[pltpu.SemaphoreType.DMA((n,)), pltpu.SemaphoreType.DMA((n,))],
    )(*arrs)


def _pair_sum(a, got, cidx, axis, name):
    _, r, cdim = a.shape
    hshape = _half_shape(a.shape, axis)

    def body(c_ref, a_ref, g_ref, o_ref):
        o_ref[...] = (a_ref[...].astype(F32) + g_ref[...].astype(F32)).astype(BF16)

    if axis == 1:
        tr = min(r // 2, 256)
        nj = (r // 2) // tr
        blk = pl.BlockSpec((1, tr, cdim), lambda s, j, c: (s, j, 0))
        a_spec = pl.BlockSpec((1, tr, cdim), lambda s, j, c: (s, c[0] * nj + j, 0))
    else:
        nj, hw = 1, cdim // 2
        blk = pl.BlockSpec((1, r, hw), lambda s, j, c: (s, 0, 0))
        a_spec = pl.BlockSpec((1, r, hw), lambda s, j, c: (s, 0, c[0]))
    return _pcall(
        body, name=name, out_shape=SDS(hshape, BF16),
        grid_spec=pltpu.PrefetchScalarGridSpec(num_scalar_prefetch=1, grid=(4, nj), in_specs=[a_spec, blk], out_specs=blk),
        compiler_params=_cp(("parallel", "parallel"), VMEM_BIG),
    )(cidx, a, got)


def _sum_chips(parts, name):
    _, h, cdim = parts.shape

    def body(p_ref, o_ref):
        acc = p_ref[0].astype(F32)
        for k in range(1, 4):
            acc = acc + p_ref[k].astype(F32)
        o_ref[...] = acc

    if h % 256 == 0 or h in (128,):
        tr = min(h, 256)
        grid, in_spec, out_spec = (h // tr,), pl.BlockSpec((4, tr, cdim), lambda i: (0, i, 0)), pl.BlockSpec((tr, cdim), lambda i: (i, 0))
    else:
        lw = 256
        grid, in_spec, out_spec = (cdim // lw,), pl.BlockSpec((4, h, lw), lambda i: (0, 0, i)), pl.BlockSpec((h, lw), lambda i: (0, i))
    return _pcall(
        body, name=name, grid=grid, in_specs=[in_spec], out_specs=out_spec, out_shape=SDS((h, cdim), F32),
        compiler_params=_cp(("parallel",), VMEM_BIG),
    )(parts)


def _sum_slots(a, name, rows):
    s, n, _ = a.shape

    def body(a_ref, o_ref):
        acc = a_ref[0]
        for k in range(1, s):
            acc = acc + a_ref[k]
        o_ref[...] = acc

    return _pcall(
        body, name=name, grid=(n // rows,), in_specs=[pl.BlockSpec((s, rows, 128), lambda i: (0, i, 0))],
        out_specs=pl.BlockSpec((rows, 128), lambda i: (i, 0)), out_shape=SDS((n, 128), F32),
        compiler_params=_cp(("parallel",)),
    )(a)


def _adam_math(w, g, m, v):
    nm = ADAM_B1 * m + (1.0 - ADAM_B1) * g
    nv = ADAM_B2 * v + (1.0 - ADAM_B2) * (g * g)
    m_hat = nm / (1.0 - ADAM_B1 ** ADAM_STEP)
    v_hat = nv / (1.0 - ADAM_B2 ** ADAM_STEP)
    return -ADAM_LR * (m_hat / (jnp.sqrt(v_hat) + ADAM_EPS) + ADAM_WD * w), nm, nv


def _adamw(w, g, m, v, name, rows):
    r, cdim = w.shape

    def body(w_ref, g_ref, m_ref, v_ref, d_ref, nm_ref, nv_ref):
        d_ref[...], nm_ref[...], nv_ref[...] = _adam_math(w_ref[...], g_ref[...], m_ref[...], v_ref[...])

    blk = pl.BlockSpec((rows, cdim), lambda i: (i, 0))
    return _pcall(
        body, name=name, grid=(r // rows,), in_specs=[blk] * 4, out_specs=[blk] * 3,
        out_shape=[SDS(w.shape, F32)] * 3, compiler_params=_cp(("parallel",)),
    )(w, g, m, v)


def _adamw_joined(w, mine, other, m, v, cidx, axis, name, rows):
    r, cdim = w.shape
    if axis == 0:
        rows = r

    def body(c_ref, w_ref, a_ref, b_ref, m_ref, v_ref, g_ref, d_ref, nm_ref, nv_ref):
        a, b = a_ref[...], b_ref[...]
        g = jnp.where(c_ref[0] == 0, jnp.concatenate([a, b], axis=axis), jnp.concatenate([b, a], axis=axis))
        g_ref[...] = g
        d_ref[...], nm_ref[...], nv_ref[...] = _adam_math(w_ref[...], g, m_ref[...], v_ref[...])

    blk = pl.BlockSpec((rows, cdim), lambda i, c: (i, 0))
    hshape = (rows // 2, cdim) if axis == 0 else (rows, cdim // 2)
    hblk = pl.BlockSpec(hshape, lambda i, c: (i, 0))
    return _pcall(
        body, name=name, out_shape=[SDS(w.shape, F32)] * 4,
        grid_spec=pltpu.PrefetchScalarGridSpec(num_scalar_prefetch=1, grid=(r // rows,), in_specs=[blk, hblk, hblk, blk, blk],
                                               out_specs=[blk] * 4),
        compiler_params=_cp(("parallel",)),
    )(cidx, w, mine, other, m, v)


def _adamw_many(ws, gs, ms, vs, name):
    n = len(ws)

    def body(*refs):
        outs = refs[4 * n:]
        for k in range(n):
            d, nm, nv = _adam_math(refs[k][...], refs[n + k][...], refs[2 * n + k][...], refs[3 * n + k][...])
            outs[k][...] = d
            outs[n + k][...] = nm
            outs[2 * n + k][...] = nv

    res = _pcall(body, name=name, out_shape=[SDS(w.shape, F32) for w in ws] * 3)(*ws, *gs, *ms, *vs)
    return res[:n], res[n:2 * n], res[2 * n:]


def _pack(pieces, rows):
    flat = jnp.concatenate([p.reshape(-1) for p in pieces])
    return jnp.pad(flat, (0, rows * 128 - flat.shape[0])).reshape(rows, 128)


def _unpack(buf, shapes):
    flat = buf.reshape(-1)
    out, off = [], 0
    for shp in shapes:
        size = 1
        for s in shp:
            size *= s
        out.append(flat[off:off + size].reshape(shp))
        off += size
    return out


LATE_ROWS = 32


def kernel(x, c, ctx, c_ctx, w_mod, b_mod, norm_g, w_in, a_ln_g, a_ln_b, a_ws, a_bs, b_gate_w2, b_gate_b, b_norm_g, w_proj_a, w_proj_b, w_out, final_norm_g, loss_target, m_c_ctx, m_w_mod, m_b_mod, m_norm_g, m_w_in, m_a_ln_g, m_a_ln_b, m_a_ws, m_a_bs, m_b_gate_w2, m_b_gate_b, m_b_norm_g, m_w_proj_a, m_w_proj_b, m_w_out, m_final_norm_g, v_c_ctx, v_w_mod, v_b_mod, v_norm_g, v_w_in, v_a_ln_g, v_a_ln_b, v_a_ws, v_a_bs, v_b_gate_w2, v_b_gate_b, v_b_norm_g, v_w_proj_a, v_w_proj_b, v_w_out, v_final_norm_g):
    xi, yi, ci = _coords()
    me_xy = 2 * xi + yi

    gate_pack = _pack([b_gate_w2[0], b_gate_b[0]], 24)
    w_in_t, m_w_in_t, v_w_in_t = (jnp.swapaxes(a[0], 0, 1) for a in (w_in, m_w_in, v_w_in))
    g_wit, g_wm, g_gate = _gather_weights([(w_in_t.astype(BF16), 1), (w_mod[0].astype(BF16), 0)], [gate_pack], "gather_weights")
    late_shards = (w_proj_a[0].astype(BF16), w_proj_b[0].astype(BF16), w_out[0].astype(BF16))
    wit_u = g_wit.reshape(4 * 1288, D)
    wit_g = wit_u[3104:5152]
    wit_r = jnp.concatenate([wit_u[1056:1568], wit_u[1568:2080], wit_u[2592:3104], wit_u[2080:2592], wit_u[0:1024]], axis=0)
    wlrt = jnp.pad(wit_u[1024:1056], ((0, LRW - 32), (0, 0)))
    wm = jnp.swapaxes(g_wm, 0, 1).reshape(D, 3 * D)
    gflat = g_gate.reshape(4, 24 * 128)
    w2 = jnp.swapaxes(gflat[:, 0:2048].reshape(4, 2, 16, 64), 0, 2)
    w2 = jnp.swapaxes(w2, 0, 1).reshape(2, 16, 256)
    gb2 = jnp.swapaxes(gflat[:, 2048:2176].reshape(4, 2, 64), 0, 1).reshape(2, 256)

    tags = ["wi", "wpa", "wpb", "wo"]
    half_axes = [2, 1, 1, 1]
    sent = []

    def ready_blocks(dwpa, dwpb, dwo):
        return [dwpa, dwpb, dwo.reshape(4, 256, D)], half_axes[1:]

    def exchange(g, ready, ready_other):
        dwr = g["dwit_r"]
        dwit_u = jnp.concatenate([g["dwit_qkv"], g["dwlrt"][0:32], dwr[0:512], dwr[512:1024], dwr[1536:2048], dwr[1024:1536],
                                  g["dwit_g"]], axis=0)
        big = [dwit_u.reshape(4, 1288, D)] + ready[0]
        other = list(_swap_half_c(big[:1], half_axes[:1], "swap_half_in")) + ready_other
        cidx = jnp.reshape(ci, (1,)).astype(jnp.int32)
        sent.extend(_pair_sum(a, o, cidx, ax, "sum_pair_" + t) for a, o, ax, t in zip(big, other, half_axes, tags))
        return sent

    r = _device_step(x[0], c, ctx[0], c_ctx[None], loss_target[0], wm, b_mod, norm_g, wit_g, wit_r, wlrt, a_ln_g, a_ln_b,
                     a_ws[0], a_bs[0], w2, gb2, b_norm_g, None, None, None, final_norm_g[None], (ready_blocks, exchange), late_shards)

    parts = [_own(g, lax.dynamic_index_in_dim(s_, me_xy, axis=0, keepdims=False), me_xy) for g, s_ in zip(r["got"], sent)]
    halves = [_sum_chips(p_, "sum_chips_" + t) for p_, t in zip(parts, tags)]

    me8 = 4 * xi + 2 * yi + ci
    early_all = _own(r["early_all"], r["early"], me8)
    late = _pack([r["dshift"], r["dscale"], r["dng_lat"], c], LATE_ROWS)
    late_all, others = _gather_all(late, halves, "gather_small")
    s_early = _sum_slots(early_all, "sum_early", EARLY_ROWS // 3)
    s_late = _sum_slots(late_all, "sum_late", LATE_ROWS)
    (s_dmodc, s_dscc, s_dng_c, s_dlng, s_dlnb, s_dws, s_dbs, s_dgbn, s_dgf, s_dw2, s_dgb2, s_loss, s_dgate) = _unpack(
        s_early, EARLY_SHAPES)
    s_dshift, s_dscale, s_dng_lat, _ = _unpack(s_late, [(1, D)] * 4)
    s_dng = s_dng_lat + s_dng_c
    s_dmod = jnp.concatenate([s_dshift, s_dscale, s_dgate], axis=1)
    loss = s_loss[0]
    s_dmodc_p = jnp.pad(s_dmodc, ((0, 0), (0, D)))
    g_b_mod = s_dmod + s_dmodc_p
    sg = jax.nn.sigmoid(c_ctx)
    g_c_ctx = s_dscc * (sg * (1.0 + c_ctx * (1.0 - sg)))
    g_w2 = lax.dynamic_slice_in_dim(s_dw2, 64 * me_xy, 64, axis=2)[None]
    g_gb2 = lax.dynamic_slice_in_dim(s_dgb2, 64 * me_xy, 64, axis=1)[None]

    flat_l = late_all.reshape(8, LATE_ROWS * 128)
    dgate_all = early_all.reshape(8, EARLY_ROWS * 128)[:, EARLY_SIZE - D:EARLY_SIZE]
    dmod_all = jnp.concatenate([flat_l[:, 0:2 * D], dgate_all], axis=1)
    c_all = flat_l[:, 3 * D:4 * D]
    lhs = jnp.concatenate([_silu(c_all), _silu(c_ctx)[None], jnp.zeros((7, D), F32)], axis=0)
    rhs = jnp.concatenate([dmod_all, s_dmodc_p, jnp.zeros((7, 3 * D), F32)], axis=0)
    rhs = lax.dynamic_slice_in_dim(rhs, 768 * me_xy, 768, axis=1)
    g_w_mod = _mm(lhs.T.astype(BF16), rhs.astype(BF16), tm=D, tn=768, tk=16, out_dtype=F32, name="mm_dwm")

    cidx = jnp.reshape(ci, (1,)).astype(jnp.int32)
    g_w_in_t, d_w_in_t, nm_w_in_t, nv_w_in_t = _adamw_joined(w_in_t, halves[0], others[0], m_w_in_t, v_w_in_t, cidx, 1,
                                                             "adamw_w_in", 184)
    g_w_in, d_w_in, nm_w_in, nv_w_in = (jnp.swapaxes(a, 0, 1) for a in (g_w_in_t, d_w_in_t, nm_w_in_t, nv_w_in_t))
    g_wpa, d_wpa, nm_wpa, nv_wpa = _adamw_joined(w_proj_a[0], halves[1], others[1], m_w_proj_a[0], v_w_proj_a[0], cidx, 0,
                                                 "adamw_wpa", 0)
    g_wpb, d_wpb, nm_wpb, nv_wpb = _adamw_joined(w_proj_b[0], halves[2], others[2], m_w_proj_b[0], v_w_proj_b[0], cidx, 0,
                                                 "adamw_wpb", 0)
    g_wo, d_wo, nm_wo, nv_wo = _adamw_joined(w_out[0], halves[3], others[3], m_w_out[0], v_w_out[0], cidx, 0, "adamw_wo", 0)
    d_w_mod, nm_w_mod, nv_w_mod = _adamw(w_mod[0], g_w_mod, m_w_mod[0], v_w_mod[0], "adamw_w_mod", 256)

    names = ["c_ctx", "b_mod", "norm_g", "a_ln_g", "a_ln_b", "a_ws", "a_bs", "b_gate_w2", "b_gate_b", "b_norm_g", "final_norm_g"]
    ws_ = [c_ctx, b_mod, norm_g, a_ln_g, a_ln_b, a_ws, a_bs, b_gate_w2, b_gate_b, b_norm_g, final_norm_g]
    gs_ = [g_c_ctx, g_b_mod, s_dng, s_dlng, s_dlnb, s_dws, s_dbs, g_w2, g_gb2, s_dgbn, s_dgf]
    ms_ = [m_c_ctx, m_b_mod, m_norm_g, m_a_ln_g, m_a_ln_b, m_a_ws, m_a_bs, m_b_gate_w2, m_b_gate_b, m_b_norm_g, m_final_norm_g]
    vs_ = [v_c_ctx, v_b_mod, v_norm_g, v_a_ln_g, v_a_ln_b, v_a_ws, v_a_bs, v_b_gate_w2, v_b_gate_b, v_b_norm_g, v_final_norm_g]
    shapes = [w.shape for w in ws_]
    flat2 = [(1, 1024), (1, 3072), (1, 1024), (1, 512), (1, 512), (512, 128), (4, 128), (32, 64), (2, 64), (1, 512), (1, 1024)]
    as2d = lambda arrs: [a.reshape(s) for a, s in zip(arrs, flat2)]
    d_s, nm_s, nv_s = _adamw_many(as2d(ws_), as2d(gs_), as2d(ms_), as2d(vs_), "adamw_small")
    d_small = {n: a.reshape(s) for n, a, s in zip(names, d_s, shapes)}
    nm_small = {n: a.reshape(s) for n, a, s in zip(names, nm_s, shapes)}
    nv_small = {n: a.reshape(s) for n, a, s in zip(names, nv_s, shapes)}
    g_small = {n: g.reshape(s) for n, g, s in zip(names, gs_, shapes)}

    order = ["c_ctx", "w_mod", "b_mod", "norm_g", "w_in", "a_ln_g", "a_ln_b", "a_ws", "a_bs", "b_gate_w2", "b_gate_b", "b_norm_g",
             "w_proj_a", "w_proj_b", "w_out", "final_norm_g"]
    big_g = dict(w_mod=g_w_mod[None], w_in=g_w_in[None], w_proj_a=g_wpa[None], w_proj_b=g_wpb[None], w_out=g_wo[None])
    big_d = dict(w_mod=d_w_mod[None], w_in=d_w_in[None], w_proj_a=d_wpa[None], w_proj_b=d_wpb[None], w_out=d_wo[None])
    big_m = dict(w_mod=nm_w_mod[None], w_in=nm_w_in[None], w_proj_a=nm_wpa[None], w_proj_b=nm_wpb[None], w_out=nm_wo[None])
    big_v = dict(w_mod=nv_w_mod[None], w_in=nv_w_in[None], w_proj_a=nv_wpa[None], w_proj_b=nv_wpb[None], w_out=nv_wo[None])
    grads = [big_g[n] if n in big_g else g_small[n] for n in order]
    deltas = [big_d[n] if n in big_d else d_small[n] for n in order]
    new_m = [big_m[n] if n in big_m else nm_small[n] for n in order]
    new_v = [big_v[n] if n in big_v else nv_small[n] for n in order]
    return (loss, r["dx"][None], *grads, *deltas, *new_m, *new_v)
```

```python
import jax
import jax.numpy as jnp
from jax import lax
from jax.experimental import pallas as pl
from jax.experimental.pallas import tpu as pltpu

F32 = jnp.float32
BF16 = jnp.bfloat16
SDS = jax.ShapeDtypeStruct

D = 1024
NP = 5120
LRW = 128
CH = 64
AC = 128
EPS = 1e-6
TOK = 512
GLA_TB = 1024
VMEM_BIG = 48 * 1024 * 1024

ADAM_LR, ADAM_B1, ADAM_B2, ADAM_EPS, ADAM_WD, ADAM_STEP = 0.001, 0.9, 0.999, 1e-08, 0.01, 10

_pcall = pl.pallas_call
MESH = pl.DeviceIdType.MESH


def _cp(sem=None, vmem=None):
    kw = {}
    if sem is not None:
        kw["dimension_semantics"] = sem
    if vmem is not None:
        kw["vmem_limit_bytes"] = vmem
    return pltpu.CompilerParams(**kw)


def _silu(x):
    return x * jax.nn.sigmoid(x)


def _silu_and_grad(x):
    s = jax.nn.sigmoid(x)
    return x * s, s * (1.0 + x * (1.0 - s))


def _logsig(x):
    return jnp.minimum(x, 0.0) - jnp.log1p(jnp.exp(-jnp.abs(x)))


def _nt(a, b):
    return lax.dot_general(a, b, (((1,), (1,)), ((), ())), preferred_element_type=F32)


def _tn(a, b):
    return lax.dot_general(a, b, (((0,), (0,)), ((), ())), preferred_element_type=F32)


def _nn(a, b):
    return jnp.dot(a, b, preferred_element_type=F32)


def _full(shape):
    return pl.BlockSpec(shape, lambda *_: (0,) * len(shape))


def _mm(a, b, *, tm, tn, tk, out_dtype, name, acc=None, b_t=False):
    m, k = a.shape
    n, k2 = (b.shape if b_t else b.shape[::-1])
    assert k == k2 == tk and m % tm == 0 and n % tn == 0, (a.shape, b.shape, tm, tn, tk)
    has_acc = acc is not None

    def body(*refs):
        if has_acc:
            a_ref, b_ref, c_ref, o_ref = refs
        else:
            a_ref, b_ref, o_ref = refs
        part = (_nt if b_t else _nn)(a_ref[...].astype(BF16), b_ref[...].astype(BF16))
        o_ref[...] = ((c_ref[...] + part) if has_acc else part).astype(out_dtype)

    b_spec = pl.BlockSpec((tn, tk), lambda i, j: (j, 0)) if b_t else pl.BlockSpec((tk, tn), lambda i, j: (0, j))
    in_specs = [pl.BlockSpec((tm, tk), lambda i, j: (i, 0)), b_spec]
    args = [a, b]
    if has_acc:
        in_specs.append(pl.BlockSpec((tm, tn), lambda i, j: (i, j)))
        args.append(acc)
    return _pcall(
        body, name=name, grid=(m // tm, n // tn), in_specs=in_specs, out_specs=pl.BlockSpec((tm, tn), lambda i, j: (i, j)),
        out_shape=SDS((m, n), out_dtype), compiler_params=_cp(("parallel", "parallel"), VMEM_BIG),
    )(*args)


def _mm_tn(a, b, *, ta, tn, tk, name, acc=None, out_dtype=F32, swap=None, extra=None):
    m, ka = a.shape
    m2, n = b.shape
    assert m == m2 and ka % ta == 0 and n % tn == 0 and m % tk == 0, (a.shape, b.shape, ta, tn, tk)
    nk = m // tk
    has_acc = acc is not None
    has_x = extra is not None
    sw_arrs, sw_axes = swap if swap is not None else ((), ())
    ns = len(sw_arrs)
    grid = (ka // ta, n // tn, nk)
    assert not has_x or (grid[1] == 1 and not ns)

    def body(*refs):
        a_ref, b_ref = refs[:2]
        pos = 2
        c_ref = refs[pos] if has_acc else None
        pos += has_acc
        x_ref = refs[pos] if has_x else None
        pos += has_x
        sw_in = refs[pos:pos + ns]
        o_ref = refs[pos + ns]
        pos += ns + 1
        o2_ref = refs[pos] if has_x else None
        pos += has_x
        sw_out = refs[pos:pos + ns]
        acc_ref = refs[pos + ns]
        sems = refs[pos + ns + 1:]
        kk = pl.program_id(2)

        if has_x:
            @pl.when(pl.program_id(0) == 0)
            def _():
                part2 = _tn(x_ref[...].astype(BF16), b_ref[...].astype(BF16))

                @pl.when(kk == 0)
                def _():
                    o2_ref[...] = part2

                @pl.when(kk > 0)
                def _():
                    o2_ref[...] += part2

        def copies():
            x, y, c = _coords()
            return [_remote(sw_in[k].at[_half_idx(sw_arrs[k].shape, sw_axes[k], 1 - c)], sw_out[k], sems[0].at[k],
                            sems[1].at[k], (x, y, 1 - c)) for k in range(ns)]

        step = (pl.program_id(0) * grid[1] + pl.program_id(1)) * nk + kk
        if ns:
            @pl.when(step == 0)
            def _():
                for rc in copies():
                    rc.start()

        part = _tn(a_ref[...].astype(BF16), b_ref[...].astype(BF16))

        @pl.when(kk == 0)
        def _():
            if has_acc:
                acc_ref[...] = c_ref[...].astype(F32) + part
            else:
                acc_ref[...] = part

        @pl.when(kk > 0)
        def _():
            acc_ref[...] += part

        @pl.when(kk == nk - 1)
        def _():
            o_ref[...] = acc_ref[...].astype(out_dtype)

        if ns:
            @pl.when(step == grid[0] * grid[1] * nk - 1)
            def _():
                for rc in copies():
                    rc.wait()

    in_specs = [pl.BlockSpec((tk, ta), lambda i, j, kk: (kk, i)), pl.BlockSpec((tk, tn), lambda i, j, kk: (kk, j))]
    args = [a, b]
    if has_acc:
        in_specs.append(pl.BlockSpec((ta, tn), lambda i, j, kk: (i, j)))
        args.append(acc)
    out_spec, out_shape = pl.BlockSpec((ta, tn), lambda i, j, kk: (i, j)), SDS((ka, n), out_dtype)
    scratch = [pltpu.VMEM((ta, tn), F32)]
    if has_x:
        kx = extra.shape[1]
        in_specs.append(pl.BlockSpec((tk, kx), lambda i, j, kk: (jnp.where(i == 0, kk, nk - 1), 0)))
        args.append(extra)
        return _pcall(body, name=name, grid=grid, in_specs=in_specs,
                      out_specs=[out_spec, pl.BlockSpec((kx, tn), lambda i, j, kk: (0, 0))],
                      out_shape=[out_shape, SDS((kx, n), F32)], scratch_shapes=scratch,
                      compiler_params=_cp(("arbitrary", "arbitrary", "arbitrary"), VMEM_BIG))(*args)
    if not ns:
        return _pcall(body, name=name, grid=grid, in_specs=in_specs, out_specs=out_spec, out_shape=out_shape,
                      scratch_shapes=scratch, compiler_params=_cp(("parallel", "parallel", "arbitrary"), VMEM_BIG))(*args)
    res = _pcall(
        body, name=name, grid=grid, in_specs=in_specs + [ANY] * ns, out_specs=[out_spec] + [ANY] * ns,
        out_shape=[out_shape] + [SDS(_half_shape(s.shape, ax), s.dtype) for s, ax in zip(sw_arrs, sw_axes)],
        scratch_shapes=scratch + [pltpu.SemaphoreType.DMA((ns,)), pltpu.SemaphoreType.DMA((ns,))],
        compiler_params=_cp(("arbitrary", "arbitrary", "arbitrary"), VMEM_BIG),
    )(*args, *sw_arrs)
    return res[0], list(res[1:])


def _modvec(cc, wm, bm):
    def body(c_ref, w_ref, b_ref, o_ref):
        o_ref[...] = _nn(_silu(c_ref[...]).astype(BF16), w_ref[...]) + b_ref[...]

    return _pcall(body, name="modvec", out_shape=SDS((8, 3 * D), F32), compiler_params=_cp(None, VMEM_BIG))(cc, wm, bm)


def _dcctx(dmodc, wm):
    def body(d_ref, w_ref, o_ref):
        o_ref[...] = _nt(d_ref[...].astype(BF16), w_ref[...])

    return _pcall(
        body, name="dcctx", grid=(1,), in_specs=[_full((8, 2 * D)), pl.BlockSpec((D, 2 * D), lambda i: (0, 0))],
        out_specs=_full((8, D)), out_shape=SDS((8, D), F32), compiler_params=_cp(("arbitrary",), VMEM_BIG),
    )(dmodc, wm)


def _prep_h(x, ng, scale, shift, name):
    m = x.shape[0]

    def body(x_ref, g_ref, sc_ref, sh_ref, h_ref):
        xf = x_ref[...]
        r = lax.rsqrt(jnp.mean(xf * xf, axis=-1, keepdims=True) + EPS)
        y = (xf * r) * g_ref[...]
        h_ref[...] = (y * (1.0 + sc_ref[...]) + sh_ref[...]).astype(BF16)

    tok = min(TOK, m)
    row = pl.BlockSpec((tok, D), lambda i: (i, 0))
    return _pcall(
        body, name=name, grid=(m // tok,), in_specs=[row, _full((1, D)), _full((1, D)), _full((1, D))],
        out_specs=row, out_shape=SDS((m, D), BF16), compiler_params=_cp(("parallel",)),
    )(x, ng, scale, shift)


def _resident(shape):
    return pl.BlockSpec(shape, lambda *_: (0,) * len(shape), pipeline_mode=pl.Buffered(1))


PROJ_TM = 512


def _proj_fwd(x, ng, scale, shift, wit_g, wit_r, wlrt, ln_g, ln_b, share=()):
    m = x.shape[0]
    ns = len(share)
    steps = m // PROJ_TM
    src = [(0, 0), (0, D), (1, 2 * D), (1, 0), (1, D)]

    def body(*refs):
        x_ref, g_ref, sc_ref, sh_ref, wg_ref, wr_ref, wl_ref, lg_ref, lb_ref = refs[:9]
        share_refs = refs[9:9 + ns]
        h_ref, p_ref, plr_ref, vr_ref, vc_ref = refs[9 + ns:14 + ns]
        got_refs = refs[14 + ns:14 + 2 * ns]
        sems = refs[14 + 2 * ns:]

        def copies():
            cx, cy, cc = _coords()
            me = 2 * cx + cy
            peers = [(1 - cx, cy), (cx, 1 - cy), (1 - cx, 1 - cy)]
            out, back = [], []
            for k in range(ns):
                for j, (px, py) in enumerate(peers):
                    out.append(_remote(share_refs[k], got_refs[k].at[me], sems[0].at[3 * k + j], sems[1].at[3 * k + j], (px, py, cc)))
                    landed = got_refs[k].at[2 * px + py]
                    back.append(_remote(landed, landed, sems[0].at[3 * k + j], sems[1].at[3 * k + j], (px, py, cc)))
            return out, back

        if ns:
            @pl.when(pl.program_id(0) == 0)
            def _():
                for rc in copies()[0]:
                    rc.start()

            @pl.when(pl.program_id(0) == steps - 1)
            def _():
                out, back = copies()
                for rc in back:
                    rc.wait_recv()
                for rc in out:
                    rc.wait_send()

        xf = x_ref[...]
        r = lax.rsqrt(jnp.mean(xf * xf, axis=-1, keepdims=True) + EPS)
        y = (xf * r) * g_ref[...]
        h = (y * (1.0 + sc_ref[...]) + sh_ref[...]).astype(BF16)
        h_ref[...] = h
        for j, (which, r0) in enumerate(src):
            w_ref = wr_ref if which else wg_ref
            blk = _nt(h, w_ref[r0:r0 + D, :]).astype(BF16)
            p_ref[:, D * j:D * j + D] = blk
            if j == 4:
                xf = blk[:, 512:1024].astype(F32)
                xc = xf - jnp.mean(xf, axis=-1, keepdims=True)
                vn = (xc * lax.rsqrt(jnp.mean(xc * xc, axis=-1, keepdims=True) + EPS)) * lg_ref[...] + lb_ref[...]
                vr_ref[...] = vn[:, 0:256].astype(BF16)
                vc_ref[0] = vn[:, 256:384].astype(BF16)
                vc_ref[1] = vn[:, 384:512].astype(BF16)
        plr_ref[...] = _nt(h, wl_ref[...])

    row = pl.BlockSpec((PROJ_TM, D), lambda i: (i, 0))
    vec = _full((1, D))
    res = _pcall(
        body, name="proj_fwd", grid=(steps,),
        in_specs=[row, vec, vec, vec, _resident((2 * D, D)), _resident((3 * D, D)), _resident((LRW, D)), _full((1, 512)),
                  _full((1, 512))] + [ANY] * ns,
        out_specs=[row, pl.BlockSpec((PROJ_TM, NP), lambda i: (i, 0)), pl.BlockSpec((PROJ_TM, LRW), lambda i: (i, 0)),
                   pl.BlockSpec((PROJ_TM, 256), lambda i: (i, 0)), pl.BlockSpec((2, PROJ_TM, 128), lambda i: (0, i, 0))] + [ANY] * ns,
        out_shape=[SDS((m, D), BF16), SDS((m, NP), BF16), SDS((m, LRW), F32), SDS((m, 256), BF16), SDS((2, m, 128), BF16)]
        + [SDS((4,) + a.shape, a.dtype) for a in share],
        scratch_shapes=([pltpu.SemaphoreType.DMA((3 * ns,)), pltpu.SemaphoreType.DMA((3 * ns,))] if ns else []),
        compiler_params=_cp(("arbitrary",), VMEM_BIG),
    )(x, ng, scale, shift, wit_g, wit_r, wlrt, ln_g, ln_b, *share)
    return res[0], res[1], res[2], res[3], res[4], list(res[5:])


def _proj_bwd(dp_g, dp_r, dlr, wit_g, wit_r, wlrt, x, dx1, ng, scale, send=(), share8=None):
    m = x.shape[0]
    ns = len(send)
    n8 = 0 if share8 is None else 1
    steps = m // PROJ_TM
    masks = [(mx, my, mc) for mx in range(2) for my in range(2) for mc in range(2)][1:]

    def body(*refs):
        (dpg_ref, dpr_ref, dlr_ref, wg_ref, wr_ref, wl_ref, x_ref, r_ref, g_ref, sc_ref) = refs[:10]
        send_refs = refs[10:10 + ns]
        n_in = 10 + ns + n8
        dx_ref, dg_ref, dsc_ref, dsh_ref = refs[n_in:n_in + 4]
        got_refs = refs[n_in + 4:n_in + 4 + ns]
        sems = refs[n_in + 4 + ns + n8:]
        i = pl.program_id(0)

        def copies():
            cx, cy, cc = _coords()
            me = 2 * cx + cy
            peers = [(1 - cx, cy), (cx, 1 - cy), (1 - cx, 1 - cy)]
            out, back = [], []
            for k in range(ns):
                for j, (px, py) in enumerate(peers):
                    out.append(_remote(send_refs[k].at[2 * px + py], got_refs[k].at[me], sems[0].at[3 * k + j],
                                       sems[1].at[3 * k + j], (px, py, cc)))
                    landed = got_refs[k].at[2 * px + py]
                    back.append(_remote(landed, landed, sems[0].at[3 * k + j], sems[1].at[3 * k + j], (px, py, cc)))
            if n8:
                src8, all8 = refs[10 + ns], refs[n_in + 4 + ns]
                s8, r8 = sems[-2], sems[-1]
                for j, (mx, my, mc) in enumerate(masks):
                    px, py, pc = _flip(cx, mx), _flip(cy, my), _flip(cc, mc)
                    out.append(_remote(src8, all8.at[4 * cx + 2 * cy + cc], s8.at[j], r8.at[j], (px, py, pc)))
                    landed = all8.at[4 * px + 2 * py + pc]
                    back.append(_remote(landed, landed, s8.at[j], r8.at[j], (px, py, pc)))
            return out, back

        @pl.when(i == 0)
        def _():
            dg_ref[...] = jnp.zeros_like(dg_ref)
            dsc_ref[...] = jnp.zeros_like(dsc_ref)
            dsh_ref[...] = jnp.zeros_like(dsh_ref)
            if ns or n8:
                for rc in copies()[0]:
                    rc.start()

        dh_ = (_nn(dpg_ref[...], wg_ref[...]) + _nn(dpr_ref[...], wr_ref[...])
               + _nn(dlr_ref[...].astype(BF16), wl_ref[...]))
        xf = x_ref[...]
        r = lax.rsqrt(jnp.mean(xf * xf, axis=-1, keepdims=True) + EPS)
        xh = xf * r
        y = xh * g_ref[...]
        dsh_ref[...] += jnp.sum(dh_, axis=0, keepdims=True)
        dsc_ref[...] += jnp.sum(dh_ * y, axis=0, keepdims=True)
        dy = dh_ * (1.0 + sc_ref[...])
        dg_ref[...] += jnp.sum(dy * xh, axis=0, keepdims=True)
        dxh = dy * g_ref[...]
        dx_ref[...] = r * (dxh - xh * jnp.mean(dxh * xh, axis=-1, keepdims=True)) + r_ref[...]

        if ns or n8:
            @pl.when(i == steps - 1)
            def _():
                out, back = copies()
                for rc in back:
                    rc.wait_recv()
                for rc in out:
                    rc.wait_send()

    row = pl.BlockSpec((PROJ_TM, D), lambda i: (i, 0))
    vec = _full((1, D))
    kg, kr = dp_g.shape[1], dp_r.shape[1]
    extra_in = list(send) + ([share8] if n8 else [])
    extra_out = [SDS(a.shape, a.dtype) for a in send] + ([SDS((8,) + share8.shape, share8.dtype)] if n8 else [])
    res = _pcall(
        body, name="proj_bwd", grid=(steps,),
        in_specs=[pl.BlockSpec((PROJ_TM, kg), lambda i: (i, 0)), pl.BlockSpec((PROJ_TM, kr), lambda i: (i, 0)),
                  pl.BlockSpec((PROJ_TM, LRW), lambda i: (i, 0)), _resident((kg, D)), _resident((kr, D)), _resident((LRW, D)),
                  row, row, vec, vec] + [ANY] * len(extra_in),
        out_specs=[row, vec, vec, vec] + [ANY] * len(extra_out),
        out_shape=[SDS((m, D), F32), SDS((1, D), F32), SDS((1, D), F32), SDS((1, D), F32)] + extra_out,
        scratch_shapes=(([pltpu.SemaphoreType.DMA((3 * ns,)), pltpu.SemaphoreType.DMA((3 * ns,))] if ns else [])
                        + ([pltpu.SemaphoreType.DMA((7,)), pltpu.SemaphoreType.DMA((7,))] if n8 else [])),
        compiler_params=_cp(("arbitrary",), VMEM_BIG),
    )(dp_g, dp_r, dlr, wit_g, wit_r, wlrt, x, dx1, ng, scale, *extra_in)
    return tuple(res[:4]), list(res[4:4 + ns]), (res[4 + ns] if n8 else None)


def _prep_bwd(x, dh, dx1, ng, scale, name):
    m = x.shape[0]
    has_res = dx1 is not None

    def body(*refs):
        if has_res:
            x_ref, dh_ref, r_ref, g_ref, sc_ref, dx_ref, dg_ref, dsc_ref, dsh_ref = refs
        else:
            x_ref, dh_ref, g_ref, sc_ref, dx_ref, dg_ref, dsc_ref, dsh_ref = refs
        i = pl.program_id(0)

        @pl.when(i == 0)
        def _():
            dg_ref[...] = jnp.zeros_like(dg_ref)
            dsc_ref[...] = jnp.zeros_like(dsc_ref)
            dsh_ref[...] = jnp.zeros_like(dsh_ref)

        xf = x_ref[...]
        dh_ = dh_ref[...]
        r = lax.rsqrt(jnp.mean(xf * xf, axis=-1, keepdims=True) + EPS)
        xh = xf * r
        y = xh * g_ref[...]
        dsh_ref[...] += jnp.sum(dh_, axis=0, keepdims=True)
        dsc_ref[...] += jnp.sum(dh_ * y, axis=0, keepdims=True)
        dy = dh_ * (1.0 + sc_ref[...])
        dg_ref[...] += jnp.sum(dy * xh, axis=0, keepdims=True)
        dxh = dy * g_ref[...]
        dx = r * (dxh - xh * jnp.mean(dxh * xh, axis=-1, keepdims=True))
        if has_res:
            dx = dx + r_ref[...]
        dx_ref[...] = dx

    tok = min(TOK, m)
    row = pl.BlockSpec((tok, D), lambda i: (i, 0))
    vec = _full((1, D))
    in_specs = [row, row] + ([row] if has_res else []) + [vec, vec]
    args = [x, dh] + ([dx1] if has_res else []) + [ng, scale]
    return _pcall(
        body, name=name, grid=(m // tok,), in_specs=in_specs, out_specs=[row, vec, vec, vec],
        out_shape=[SDS((m, D), F32), SDS((1, D), F32), SDS((1, D), F32), SDS((1, D), F32)],
        compiler_params=_cp(("arbitrary",)),
    )(*args)


COLB = 2048


def _colmix_fwd(vnc, ws23, bs23):
    rows = vnc.shape[2] // COLB

    def body(v_ref, w_ref, b_ref, o_ref):
        o_ref[0] = _nn(w_ref[0], v_ref[0]) + b_ref[0]

    return _pcall(
        body, name="colmix_fwd", grid=(2, rows),
        in_specs=[pl.BlockSpec((1, AC, COLB), lambda g, j: (g, 0, j)), pl.BlockSpec((1, AC, AC), lambda g, j: (g, 0, 0)),
                  pl.BlockSpec((1, AC, 1), lambda g, j: (g, 0, 0))],
        out_specs=pl.BlockSpec((1, AC, COLB), lambda g, j: (g, 0, j)),
        out_shape=SDS(vnc.shape, F32), compiler_params=_cp(("parallel", "parallel")),
    )(vnc, ws23, bs23)


def _colmix_bwd(dsvc, vnc, ws23t):
    rows = vnc.shape[2] // COLB

    def body(d_ref, v_ref, wt_ref, dv_ref, dw_ref, db_ref):
        j = pl.program_id(1)

        @pl.when(j == 0)
        def _():
            dw_ref[...] = jnp.zeros_like(dw_ref)
            db_ref[...] = jnp.zeros_like(db_ref)

        d = d_ref[0]
        d16 = d.astype(BF16)
        dv_ref[0] = _nn(wt_ref[0], d16)
        dw_ref[0] += _nt(d16, v_ref[0])
        db_ref[0] += jnp.sum(d, axis=1, keepdims=True)

    blk = pl.BlockSpec((1, AC, COLB), lambda g, j: (g, 0, j))
    return _pcall(
        body, name="colmix_bwd", grid=(2, rows),
        in_specs=[blk, blk, pl.BlockSpec((1, AC, AC), lambda g, j: (g, 0, 0))],
        out_specs=[blk, pl.BlockSpec((1, AC, AC), lambda g, j: (g, 0, 0)), pl.BlockSpec((1, AC, 1), lambda g, j: (g, 0, 0))],
        out_shape=[SDS(vnc.shape, F32), SDS((2, AC, AC), F32), SDS((2, AC, 1), F32)],
        compiler_params=_cp(("parallel", "arbitrary")),
    )(dsvc, vnc, ws23t)


def _head_norm(o):
    out = []
    for h in range(4):
        oh = o[:, 128 * h:128 * h + 128]
        r = lax.rsqrt(jnp.mean(oh * oh, axis=-1, keepdims=True) + EPS)
        out.append((r, oh * r))
    return out


def _tail_fwd(o_f, o_b, p, vnr, svc, x, tgt, ws01, bs01, gbn, wpa, wpb, wo, gate, gf):
    m = p.shape[0]

    def body(of_ref, ob_ref, zb_ref, ua_ref, za_ref, ga_ref, gb_ref, vnr_ref, svc_ref, x_ref, t_ref, w_ref, b_ref, g_ref,
             wpa_ref, wpb_ref, wo_ref, gate_ref, gf_ref,
             ya_ref, yb_ref, svr_ref, dwo_ref, dx1_ref, dout_ref, loss_ref, dgate_ref, dgf_ref):
        i = pl.program_id(0)

        @pl.when(i == 0)
        def _():
            loss_ref[...] = jnp.zeros_like(loss_ref)
            dgate_ref[...] = jnp.zeros_like(dgate_ref)
            dgf_ref[...] = jnp.zeros_like(dgf_ref)
            dwo_ref[...] = jnp.zeros_like(dwo_ref)

        o = of_ref[...] + ob_ref[...]
        zb = zb_ref[...].astype(F32)
        for h, (r, xh) in enumerate(_head_norm(o)):
            sl = slice(128 * h, 128 * h + 128)
            yb_ref[:, sl] = ((xh * g_ref[:, sl]) * _silu(zb[:, sl])).astype(BF16)
        for j in range(TOK // AC):
            for g in range(2):
                sv = _nn(w_ref[g], vnr_ref[AC * j:AC * j + AC, AC * g:AC * g + AC]) + b_ref[g]
                svr_ref[AC * j:AC * j + AC, AC * g:AC * g + AC] = sv
        sz = _silu(za_ref[...].astype(F32))
        u = ua_ref[...].astype(F32)
        ya_ref[:, 0:256] = ((u[:, 0:256] * svr_ref[...]) * sz[:, 0:256]).astype(BF16)
        ya_ref[:, 256:384] = ((u[:, 256:384] * svc_ref[0]) * sz[:, 256:384]).astype(BF16)
        ya_ref[:, 384:512] = ((u[:, 384:512] * svc_ref[1]) * sz[:, 384:512]).astype(BF16)
        ya = _nn(ya_ref[...], wpa_ref[...])
        yb = _nn(yb_ref[...], wpb_ref[...])
        mg = (jax.nn.sigmoid(ga_ref[...].astype(F32)) * ya + jax.nn.sigmoid(gb_ref[...].astype(F32)) * yb).astype(BF16)
        out_ = _nn(mg, wo_ref[...])
        x1 = x_ref[...] + gate_ref[...] * out_
        r = lax.rsqrt(jnp.mean(x1 * x1, axis=-1, keepdims=True) + EPS)
        xh = x1 * r
        err = xh * gf_ref[...] - t_ref[...]
        loss_ref[...] += 0.5 * jnp.sum(jnp.mean(err * err, axis=-1, keepdims=True), axis=0, keepdims=True)
        dy = err * (1.0 / D)
        dgf_ref[...] += jnp.sum(dy * xh, axis=0, keepdims=True)
        dxh = dy * gf_ref[...]
        dx1 = r * (dxh - xh * jnp.mean(dxh * xh, axis=-1, keepdims=True))
        dx1_ref[...] = dx1
        dout16 = (gate_ref[...] * dx1).astype(BF16)
        dout_ref[...] = dout16
        dgate_ref[...] += jnp.sum(dx1 * out_, axis=0, keepdims=True)
        dwo_ref[...] += _tn(mg, dout16)

    r512 = pl.BlockSpec((TOK, 512), lambda i: (i, 0))
    row = pl.BlockSpec((TOK, D), lambda i: (i, 0))
    vec = _full((1, D))
    return _pcall(
        body, name="tail_fwd", grid=(m // TOK,),
        in_specs=[r512, r512, pl.BlockSpec((TOK, 512), lambda i: (i, 6)), pl.BlockSpec((TOK, 512), lambda i: (i, 7)),
                  pl.BlockSpec((TOK, 512), lambda i: (i, 8)), row, pl.BlockSpec((TOK, D), lambda i: (i, 1)),
                  pl.BlockSpec((TOK, 256), lambda i: (i, 0)), pl.BlockSpec((2, TOK, 128), lambda i: (0, i, 0)), row, row,
                  _full((2, AC, AC)), _full((2, AC, 1)), _full((1, 512)), _resident((512, D)), _resident((512, D)),
                  _resident((D, D)), vec, vec],
        out_specs=[r512, r512, pl.BlockSpec((TOK, 256), lambda i: (i, 0)), _resident((D, D)), row, row, _full((1, 128)), vec, vec],
        out_shape=[SDS((m, 512), BF16), SDS((m, 512), BF16), SDS((m, 256), F32), SDS((D, D), F32), SDS((m, D), F32),
                   SDS((m, D), BF16), SDS((1, 128), F32), SDS((1, D), F32), SDS((1, D), F32)],
        compiler_params=_cp(("arbitrary",), VMEM_BIG),
    )(o_f, o_b, p, p, p, p, p, vnr, svc, x, tgt, ws01, bs01, gbn, wpa, wpb, wo, gate, gf)


DPR = 3072


def _tail_bwd(dout, ya_in, yb_in, p, svr, svc, o_f, o_b, gbn, wo, wpa, wpb):
    m = p.shape[0]

    def body(dout_ref, ya_ref, yb_ref, ga_ref, gb_ref, zb_ref, ua_ref, za_ref, svr_ref, svc_ref, of_ref, ob_ref, g_ref,
             wo_ref, wpa_ref, wpb_ref,
             dwpa_ref, dwpb_ref, dpg_ref, dpr_ref, dsr_ref, dsc_ref, do_ref, dg_ref):
        i = pl.program_id(0)

        @pl.when(i == 0)
        def _():
            dg_ref[...] = jnp.zeros_like(dg_ref)
            dwpa_ref[...] = jnp.zeros_like(dwpa_ref)
            dwpb_ref[...] = jnp.zeros_like(dwpb_ref)

        dm_ = _nt(dout_ref[...], wo_ref[...])
        ya_in, yb_in = ya_ref[...], yb_ref[...]
        ya = _nn(ya_in, wpa_ref[...])
        yb = _nn(yb_in, wpb_ref[...])
        sa = jax.nn.sigmoid(ga_ref[...].astype(F32))
        sb = jax.nn.sigmoid(gb_ref[...].astype(F32))
        dya16 = (dm_ * sa).astype(BF16)
        dyb16 = (dm_ * sb).astype(BF16)
        dwpa, dwpb = _tn(ya_in, dya16), _tn(yb_in, dyb16)
        for s in range(4):
            dwpa_ref[s] += dwpa[:, 256 * s:256 * (s + 1)]
            dwpb_ref[s] += dwpb[:, 256 * s:256 * (s + 1)]
        dpg_ref[:, 0:D] = (dm_ * ya * (sa * (1.0 - sa))).astype(BF16)
        dpg_ref[:, D:2 * D] = (dm_ * yb * (sb * (1.0 - sb))).astype(BF16)
        dya = _nt(dya16, wpa_ref[...])
        dyb = _nt(dyb16, wpb_ref[...])

        u = ua_ref[...].astype(F32)
        za = za_ref[...].astype(F32)
        sz, dsz = _silu_and_grad(za)
        sv = jnp.concatenate([svr_ref[...], svc_ref[0], svc_ref[1]], axis=1)
        dpr_ref[:, 512:1024] = (dya * sv * sz).astype(BF16)
        dsv = dya * u * sz
        dsr_ref[...] = dsv[:, 0:256]
        dsc_ref[0] = dsv[:, 256:384]
        dsc_ref[1] = dsv[:, 384:512]
        dpr_ref[:, 1024:1536] = (dya * u * sv * dsz).astype(BF16)

        zb = zb_ref[...].astype(F32)
        o = of_ref[...] + ob_ref[...]
        szb, dszb = _silu_and_grad(zb)
        for h, (r, xh) in enumerate(_head_norm(o)):
            sl = slice(128 * h, 128 * h + 128)
            gh = g_ref[:, sl]
            don = dyb[:, sl] * szb[:, sl]
            dpr_ref[:, sl] = (dyb[:, sl] * (xh * gh) * dszb[:, sl]).astype(BF16)
            dg_ref[:, sl] += jnp.sum(don * xh, axis=0, keepdims=True)
            dxh = don * gh
            do_ref[:, sl] = (r * (dxh - xh * jnp.mean(dxh * xh, axis=-1, keepdims=True))).astype(BF16)

    r512 = pl.BlockSpec((TOK, 512), lambda i: (i, 0))
    row = pl.BlockSpec((TOK, D), lambda i: (i, 0))
    return _pcall(
        body, name="tail_bwd", grid=(m // TOK,),
        in_specs=[row, r512, r512, row, pl.BlockSpec((TOK, D), lambda i: (i, 1)), pl.BlockSpec((TOK, 512), lambda i: (i, 6)),
                  pl.BlockSpec((TOK, 512), lambda i: (i, 7)), pl.BlockSpec((TOK, 512), lambda i: (i, 8)),
                  pl.BlockSpec((TOK, 256), lambda i: (i, 0)), pl.BlockSpec((2, TOK, 128), lambda i: (0, i, 0)), r512, r512,
                  _full((1, 512)), _resident((D, D)), _resident((512, D)), _resident((512, D))],
        out_specs=[_resident((4, 512, 256)), _resident((4, 512, 256)), pl.BlockSpec((TOK, 2 * D), lambda i: (i, 0)),
                   pl.BlockSpec((TOK, 1536), lambda i: (i, 0)),
                   pl.BlockSpec((TOK, 256), lambda i: (i, 0)), pl.BlockSpec((2, TOK, 128), lambda i: (0, i, 0)), r512, _full((1, 512))],
        out_shape=[SDS((4, 512, 256), F32), SDS((4, 512, 256), F32), SDS((m, 2 * D), BF16), SDS((m, DPR), BF16), SDS((m, 256), F32),
                   SDS((2, m, 128), F32), SDS((m, 512), BF16), SDS((1, 512), F32)],
        compiler_params=_cp(("arbitrary",), VMEM_BIG),
    )(dout, ya_in, yb_in, p, p, p, p, p, svr, svc, o_f, o_b, gbn, wo, wpa, wpb)


def _ln_bwd(dsr, vnr, dvnc, p, ws01t, ln_g, dp):
    m = p.shape[0]

    def body(dsr_ref, vnr_ref, dvc_ref, va_ref, wt_ref, g_ref, dpi_ref, dp_ref, dw_ref, db_ref, dlg_ref, dlb_ref, dvn_ref):
        i = pl.program_id(0)

        @pl.when(i == 0)
        def _():
            dw_ref[...] = jnp.zeros_like(dw_ref)
            db_ref[...] = jnp.zeros_like(db_ref)
            dlg_ref[...] = jnp.zeros_like(dlg_ref)
            dlb_ref[...] = jnp.zeros_like(dlb_ref)

        for j in range(TOK // AC):
            for g in range(2):
                d = dsr_ref[AC * j:AC * j + AC, AC * g:AC * g + AC]
                d16 = d.astype(BF16)
                dvn_ref[AC * j:AC * j + AC, AC * g:AC * g + AC] = _nn(wt_ref[g], d16)
                dw_ref[g] += _nt(d16, vnr_ref[AC * j:AC * j + AC, AC * g:AC * g + AC])
                db_ref[g] += jnp.sum(d, axis=1, keepdims=True)
        dvn_ref[:, 256:384] = dvc_ref[0]
        dvn_ref[:, 384:512] = dvc_ref[1]
        dvn = dvn_ref[...]
        xf = va_ref[...].astype(F32)
        xc = xf - jnp.mean(xf, axis=-1, keepdims=True)
        rs = lax.rsqrt(jnp.mean(xc * xc, axis=-1, keepdims=True) + EPS)
        xh = xc * rs
        dlg_ref[...] += jnp.sum(dvn * xh, axis=0, keepdims=True)
        dlb_ref[...] += jnp.sum(dvn, axis=0, keepdims=True)
        dxh = dvn * g_ref[...]
        dva = rs * (dxh - jnp.mean(dxh, axis=-1, keepdims=True) - xh * jnp.mean(dxh * xh, axis=-1, keepdims=True))
        dp_ref[...] = dva.astype(BF16)

    return _pcall(
        body, name="ln_bwd", grid=(m // TOK,),
        in_specs=[pl.BlockSpec((TOK, 256), lambda i: (i, 0)), pl.BlockSpec((TOK, 256), lambda i: (i, 0)),
                  pl.BlockSpec((2, TOK, 128), lambda i: (0, i, 0)), pl.BlockSpec((TOK, 512), lambda i: (i, 9)),
                  _full((2, AC, AC)), _full((1, 512)), pl.BlockSpec(memory_space=pl.ANY)],
        out_specs=[pl.BlockSpec((TOK, 512), lambda i: (i, 3)), _full((2, AC, AC)), _full((2, AC, 1)), _full((1, 512)), _full((1, 512))],
        out_shape=[SDS((m, DPR), BF16), SDS((2, AC, AC), F32), SDS((2, AC, 1), F32), SDS((1, 512), F32), SDS((1, 512), F32)],
        scratch_shapes=[pltpu.VMEM((TOK, 512), F32)],
        input_output_aliases={6: 0}, compiler_params=_cp(("arbitrary",)),
    )(dsr, vnr, dvnc, p, ws01t, ln_g, dp)


def _tri_mm(tri, a):
    a1 = a.astype(BF16)
    r1 = a - a1.astype(F32)
    a2 = r1.astype(BF16)
    a3 = (r1 - a2.astype(F32)).astype(BF16)
    n = a.shape[1]
    r = _nn(tri, jnp.concatenate([a1, a2, a3], axis=1))
    return r[:, 0:n] + r[:, n:2 * n] + r[:, 2 * n:3 * n]


def _gla_masks(reverse):
    ri = lax.broadcasted_iota(jnp.int32, (CH, CH), 0)
    ci = lax.broadcasted_iota(jnp.int32, (CH, CH), 1)
    vis = (ci >= ri) if reverse else (ci <= ri)
    vis_t = (ci <= ri) if reverse else (ci >= ri)
    r4 = lax.broadcasted_iota(jnp.int32, (4 * CH, CH), 0) & (CH - 1)
    c4 = lax.broadcasted_iota(jnp.int32, (4 * CH, CH), 1)
    vis4 = (c4 >= r4) if reverse else (c4 <= r4)
    vis4_t = (c4 <= r4) if reverse else (c4 >= r4)
    lane = lax.broadcasted_iota(jnp.int32, (1, 256), 1)
    hm = [(lane >= CH * h) & (lane < CH * h + CH) for h in range(4)]
    return vis, vis_t, vis4, vis4_t, hm


def _stack_heads(x, hm):
    return jnp.concatenate([jnp.where(hm[h], x, 0.0).astype(BF16) for h in range(4)], axis=0)


def _diag_heads(full, hm):
    r = full.shape[0] // 4
    acc = jnp.where(hm[0], full[0:r], 0.0)
    for h in range(1, 4):
        acc = acc + jnp.where(hm[h], full[r * h:r * h + r], 0.0)
    return acc


def _rows_of_heads(x):
    return jnp.concatenate([x[:, 128 * h:128 * h + 128] for h in range(4)], axis=0)


def _lane_vis(reverse, transpose):
    ri = lax.broadcasted_iota(jnp.int32, (CH, 4 * CH), 0)
    ci = lax.broadcasted_iota(jnp.int32, (CH, 4 * CH), 1) & (CH - 1)
    return (ci >= ri) if (reverse != transpose) else (ci <= ri)


def _gla_fwd2(p, qkv_blk, lr, lrws, gbiases, s0s, name):
    m = p.shape[0]
    tb = min(GLA_TB, m)
    nb = m // tb
    nc = tb // CH

    def body(qkv_f, lr_f, qkv_b, lr_b, lrw_f, lrw_b, gb_f, gb_b, s0_f, s0_b,
             o_f, sb_f, sfin_f, o_b, sb_b, sfin_b, st_f, st_b):
        i = pl.program_id(0)

        @pl.when(i == 0)
        def _():
            st_f[...] = s0_f[...]
            st_b[...] = s0_b[...]

        dirs = []
        for reverse, qkv_ref, lr_ref, lrw_ref, gb_ref, o_ref, sb_ref, st_ref in (
                (False, qkv_f, lr_f, lrw_f, gb_f, o_f, sb_f, st_f), (True, qkv_b, lr_b, lrw_b, gb_b, o_b, sb_b, st_b)):
            vis, _, vis4, _, hm = _gla_masks(reverse)
            logits = _nn(lr_ref[...].astype(BF16), lrw_ref[...]) + gb_ref[...]
            dirs.append(dict(reverse=reverse, qkv=qkv_ref, o=o_ref, sb=sb_ref, vis4=vis4, hm=hm,
                             tri=vis.astype(F32).astype(BF16), a=_logsig(logits) * (1.0 / 16.0), st=st_ref[...]))
        for step in range(nc):
            for d in dirs:
                c = nc - 1 - step if d["reverse"] else step
                rows = slice(CH * c, CH * c + CH)
                b = _tri_mm(d["tri"], d["a"][rows])
                bl = b[0:1] if d["reverse"] else b[CH - 1:CH]
                q = d["qkv"][rows, 0:256].astype(F32) * 0.125
                k = d["qkv"][rows, 256:512].astype(F32)
                v16 = d["qkv"][rows, 512:1024].astype(BF16)
                qd = q * jnp.exp(b)
                kd16 = (k * jnp.exp(-b)).astype(BF16)
                kdec16 = (k * jnp.exp(bl - b)).astype(BF16)
                qstack = _stack_heads(qd, d["hm"])
                sc = jnp.where(d["vis4"], _nt(qstack, kd16), 0.0).astype(BF16)
                inter = _nt(qstack, d["st"].astype(BF16))
                for h in range(4):
                    d["o"][rows, 128 * h:128 * h + 128] = (
                        _nn(sc[CH * h:CH * h + CH], v16[:, 128 * h:128 * h + 128]) + inter[CH * h:CH * h + CH])
                d["sb"][c] = d["st"]
                d["st"] = d["st"] * jnp.exp(bl) + _diag_heads(_tn(v16, kdec16), d["hm"])
        st_f[...] = dirs[0]["st"]
        st_b[...] = dirs[1]["st"]

        @pl.when(i == nb - 1)
        def _():
            sfin_f[...] = dirs[0]["st"]
            sfin_b[...] = dirs[1]["st"]

    fw = lambda i: i
    bw = lambda i: nb - 1 - i
    in_specs = []
    for rm in (fw, bw):
        in_specs += [pl.BlockSpec((tb, 1024), lambda i, rm=rm: (rm(i), qkv_blk)), pl.BlockSpec((tb, LRW), lambda i, rm=rm: (rm(i), 0))]
    in_specs += [_full((LRW, 256))] * 2 + [_full((1, 256))] * 2 + [_full((128, 256))] * 2
    out_specs, out_shape = [], []
    for rm in (fw, bw):
        out_specs += [pl.BlockSpec((tb, 512), lambda i, rm=rm: (rm(i), 0)), pl.BlockSpec((nc, 128, 256), lambda i, rm=rm: (rm(i), 0, 0)),
                      _full((128, 256))]
        out_shape += [SDS((m, 512), F32), SDS((m // CH, 128, 256), F32), SDS((128, 256), F32)]
    return _pcall(
        body, name=name, grid=(nb,), in_specs=in_specs, out_specs=out_specs, out_shape=out_shape,
        scratch_shapes=[pltpu.VMEM((128, 256), F32), pltpu.VMEM((128, 256), F32)], compiler_params=_cp(("arbitrary",), VMEM_BIG),
    )(p, lr, p, lr, lrws[0], lrws[1], gbiases[0], gbiases[1], s0s[0], s0s[1])


def _gla_bwd(p, qkv_blk, lr, lrw, lrwt, gbias, sb, dsfin, do, prev, dp, *, reverse, name):
    m = p.shape[0]
    tb = min(GLA_TB, m)
    nb = m // tb
    nc = tb // CH
    rmap = (lambda i: i) if reverse else (lambda i: nb - 1 - i)
    has_prev = prev is not None
    has_dp = dp is not None

    def body(*refs):
        refs = list(refs)
        qkv_ref, lr_ref, lrw_ref, lrwt_ref, gb_ref, sb_ref, dsfin_ref, do_ref = refs[:8]
        refs = refs[8:]
        if has_prev:
            pq_ref, plr_ref = refs[:2]
            refs = refs[2:]
        if has_dp:
            refs = refs[1:]
        dqkv_ref, dlr_ref, dw2_ref, dgb_ref, ds0_ref, dst_ref, dlog_ref = refs
        i = pl.program_id(0)

        @pl.when(i == 0)
        def _():
            dst_ref[...] = dsfin_ref[...]
            dw2_ref[...] = jnp.zeros_like(dw2_ref)
            dgb_ref[...] = jnp.zeros_like(dgb_ref)

        vis, vis_t, vis4, vis4_t, hm = _gla_masks(reverse)
        tri = vis.astype(F32).astype(BF16)
        tri_t = vis_t.astype(F32).astype(BF16)
        lane_vis = _lane_vis(reverse, False)
        lane_vis_t = _lane_vis(reverse, True)
        lr16 = lr_ref[...].astype(BF16)
        logits = _nn(lr16, lrw_ref[...]) + gb_ref[...]
        a_all = _logsig(logits) * (1.0 / 16.0)
        dsig = (1.0 - jax.nn.sigmoid(logits)) * (1.0 / 16.0)
        dst = dst_ref[...]
        for c in (range(nc) if reverse else range(nc - 1, -1, -1)):
            rows = slice(CH * c, CH * c + CH)
            b = _tri_mm(tri, a_all[rows])
            bl = b[0:1] if reverse else b[CH - 1:CH]
            eb = jnp.exp(b)
            enb = jnp.exp(-b)
            ebl = jnp.exp(bl - b)
            el = jnp.exp(bl)
            q = qkv_ref[rows, 0:256].astype(F32) * 0.125
            k = qkv_ref[rows, 256:512].astype(F32)
            v16 = qkv_ref[rows, 512:1024].astype(BF16)
            do16 = do_ref[rows, :].astype(BF16)
            qd = q * eb
            kd = k * enb
            kdec = k * ebl
            st = sb_ref[c]
            st16 = st.astype(BF16)
            dst16 = dst.astype(BF16)
            qd16 = qd.astype(BF16)
            kd16 = kd.astype(BF16)
            qstack = _stack_heads(qd, hm)
            kstack = _stack_heads(kd, hm)
            kdecstack = _stack_heads(kdec, hm)
            pt = jnp.where(vis4_t, _nt(kstack, qd16), 0.0).astype(BF16)
            dvinter = _nt(kdecstack, dst16)
            do_rows = _rows_of_heads(do16)
            v_rows = _rows_of_heads(v16)
            dp_cat = jnp.where(lane_vis, _diag_heads(_nt(do_rows, v_rows), hm), 0.0).astype(BF16)
            dpt_cat = jnp.where(lane_vis_t, _diag_heads(_nt(v_rows, do_rows), hm), 0.0).astype(BF16)
            dqd = _nn(dp_cat, kstack) + _diag_heads(_nn(do_rows, st16), hm)
            dkd = _nn(dpt_cat, qstack)
            dkdec = _diag_heads(_nn(v_rows, dst16), hm)
            for h in range(4):
                rh = slice(CH * h, CH * h + CH)
                dv_h = _nn(pt[rh], do_rows[rh]) + dvinter[rh]
                if has_prev:
                    dv_h = dv_h + pq_ref[rows, 512 + 128 * h:512 + 128 * h + 128]
                dqkv_ref[rows, 512 + 128 * h:512 + 128 * h + 128] = dv_h.astype(dqkv_ref.dtype)
            dq = dqd * eb * 0.125
            dk = dkd * enb + dkdec * ebl
            if has_prev:
                dq = dq + pq_ref[rows, 0:256]
                dk = dk + pq_ref[rows, 256:512]
            dqkv_ref[rows, 0:256] = dq.astype(dqkv_ref.dtype)
            dqkv_ref[rows, 256:512] = dk.astype(dqkv_ref.dtype)
            g_kdec = dkdec * kdec
            db = dqd * qd - dkd * kd - g_kdec
            dbl = jnp.sum(g_kdec, axis=0, keepdims=True) + jnp.sum(st * dst, axis=0, keepdims=True) * el
            da = _tri_mm(tri_t, db) + dbl
            dlog_ref[rows, :] = da * dsig[rows]
            dst = dst * el + _diag_heads(_tn(do16, qd16), hm)
        dst_ref[...] = dst
        dlog = dlog_ref[...]
        dlog16 = dlog.astype(BF16)
        dlr = _nn(dlog16, lrwt_ref[...])
        if has_prev:
            dlr = dlr + plr_ref[...]
        dlr_ref[...] = dlr
        dw2_ref[...] += _tn(lr16, dlog16)
        dgb_ref[...] += jnp.sum(dlog, axis=0, keepdims=True)

        @pl.when(i == nb - 1)
        def _():
            ds0_ref[...] = dst

    in_specs = [pl.BlockSpec((tb,1024), lambda i: (rmap(i), qkv_blk)), pl.BlockSpec((tb,LRW), lambda i: (rmap(i), 0)),
                _full((LRW, 256)), _full((256, LRW)), _full((1, 256)), pl.BlockSpec((nc, 128, 256), lambda i: (rmap(i), 0, 0)),
                _full((128, 256)), pl.BlockSpec((tb,512), lambda i: (rmap(i), 0))]
    args = [p, lr, lrw, lrwt, gbias, sb, dsfin, do]
    if has_prev:
        in_specs += [pl.BlockSpec((tb,1024), lambda i: (rmap(i), 0)), pl.BlockSpec((tb,LRW), lambda i: (rmap(i), 0))]
        args += list(prev)
    aliases = {}
    if has_dp:
        in_specs.append(pl.BlockSpec(memory_space=pl.ANY))
        aliases = {len(args): 0}
        args.append(dp)
        dq_spec = pl.BlockSpec((tb,1024), lambda i: (rmap(i), 2))
        dq_shape = SDS(dp.shape, dp.dtype)
    else:
        dq_spec = pl.BlockSpec((tb,1024), lambda i: (rmap(i), 0))
        dq_shape = SDS((m, 1024), F32)
    return _pcall(
        body, name=name, grid=(nb,), in_specs=in_specs,
        out_specs=[dq_spec, pl.BlockSpec((tb,LRW), lambda i: (rmap(i), 0)), _full((LRW, 256)), _full((1, 256)), _full((128, 256))],
        out_shape=[dq_shape, SDS((m, LRW), F32), SDS((LRW, 256), F32), SDS((1, 256), F32), SDS((128, 256), F32)],
        scratch_shapes=[pltpu.VMEM((128, 256), F32), pltpu.VMEM((tb,256), F32)],
        input_output_aliases=aliases, compiler_params=_cp(("arbitrary",)),
    )(*args)


EARLY_KEYS = ["dmodc", "dscc", "dng_c", "dlng", "dlnb", "dws", "dbs", "dgbn", "dgf", "dw2", "dgb2", "loss", "dgate"]
EARLY_SHAPES = [(1, 2 * D), (D,), (1, D), (1, 512), (1, 512), (1, 4, 128, 128), (1, 4, 128), (1, 512), (D,), (2, 16, 256), (2, 256),
                (128,), (1, D)]
EARLY_SIZE = 2 * D + D + D + 512 + 512 + 4 * 128 * 128 + 512 + 512 + D + 2 * 16 * 256 + 512 + 128 + D
EARLY_ROWS = 648


def _device_step(x, c, ctx, c_ctx, tgt, wm, bm, ng, wit_g, wit_r, wlrt, ln_g, ln_b, ws, bs, w2, gb2, gbn, wpa, wpb, wo, gf,
                 exchange=None, shards=None):
    L = x.shape[0]
    wit_qkv = wit_r[2048:3072]
    ws16 = ws.astype(BF16)
    wst16 = jnp.swapaxes(ws, 1, 2).astype(BF16)
    bscol = bs[:, :, None]
    lrw = [jnp.zeros((LRW, 256), F32).at[16 * r:16 * r + 16].set(w2[r]).astype(BF16) for r in range(2)]
    lrwt = [w.T for w in lrw]
    gbias = [gb2[r:r + 1] for r in range(2)]

    cc = jnp.zeros((8, D), F32).at[0:1].set(c).at[1:2].set(c_ctx)
    mod = _modvec(cc, wm, bm)
    shift, scale, gate = mod[0:1, 0:D], mod[0:1, D:2 * D], mod[0:1, 2 * D:3 * D]
    shift_c, scale_c = mod[1:2, 0:D], mod[1:2, D:2 * D]

    hc = _prep_h(ctx, ng, scale_c, shift_c, "prep_hc")
    pc = _mm(hc, wit_qkv, tm=256, tn=1024, tk=D, out_dtype=F32, name="mm_pc", b_t=True)
    plrc = _mm(hc, wlrt, tm=256, tn=LRW, tk=D, out_dtype=F32, name="mm_plrc", b_t=True)
    zero_s = jnp.zeros((128, 256), F32)
    _, sbc_f, sc_f, _, sbc_b, sc_b = _gla_fwd2(pc, 0, plrc, lrw, gbias, (zero_s, zero_s), "gla_fwd_c")

    h, p, plr, vnr, vnc, late = _proj_fwd(x, ng, scale, shift, wit_g, wit_r, wlrt, ln_g, ln_b,
                                          shards if shards is not None else ())
    if shards is not None:
        me_xy = 2 * lax.axis_index("x") + lax.axis_index("y")
        g_wpa, g_wpb, g_wo = (_own(g, s_, me_xy) for g, s_ in zip(late, shards))
        wpa = jnp.swapaxes(g_wpa, 0, 1).reshape(512, D)
        wpb = jnp.swapaxes(g_wpb, 0, 1).reshape(512, D)
        wo = g_wo.reshape(D, D)
    o_f, sb_f, _, o_b, sb_b, _ = _gla_fwd2(p, 2, plr, lrw, gbias, (sc_f, sc_b), "gla_fwd")
    svc = _colmix_fwd(vnc.reshape(2, AC, L), ws16[2:4], bscol[2:4]).reshape(2, L, 128)
    ya_in, yb_in, svr, dwo, dx1, dout, loss, dgate, dgf = _tail_fwd(
        o_f, o_b, p, vnr, svc, x, tgt, ws16[0:2], bscol[0:2], gbn, wpa, wpb, wo, gate, gf)

    dwpa, dwpb, dp_g, dp, dsr, dsc, do, dgbn = _tail_bwd(dout, ya_in, yb_in, p, svr, svc, o_f, o_b, gbn, wo, wpa, wpb)
    dvnc, dws23, dbs23 = _colmix_bwd(dsc.reshape(2, AC, L), vnc.reshape(2, AC, L), wst16[2:4])
    dp, dws01, dbs01, dlng, dlnb = _ln_bwd(dsr, vnr, dvnc.reshape(2, L, 128), p, wst16[0:2], ln_g, dp)
    zero_ds = jnp.zeros((128, 256), F32)
    dqkv_f, dlr_f, dw2_f, dgb_f, ds0_f = _gla_bwd(p, 2, plr, lrw[0], lrwt[0], gbias[0], sb_f, zero_ds, do, None, None,
                                                  reverse=False, name="gla_bwd_f")
    dp, dlr, dw2_b, dgb_b, ds0_b = _gla_bwd(p, 2, plr, lrw[1], lrwt[1], gbias[1], sb_b, zero_ds, do, (dqkv_f, dlr_f), dp,
                                            reverse=True, name="gla_bwd_b")
    zero_do = jnp.zeros((ctx.shape[0], 512), BF16)
    dqkvc_f, dlrc_f, dw2c_f, dgbc_f, _ = _gla_bwd(pc, 0, plrc, lrw[0], lrwt[0], gbias[0], sbc_f, ds0_f, zero_do, None, None,
                                                  reverse=False, name="gla_bwd_cf")
    dqkvc, dlrc, dw2c_b, dgbc_b, _ = _gla_bwd(pc, 0, plrc, lrw[1], lrwt[1], gbias[1], sbc_b, ds0_b, zero_do,
                                              (dqkvc_f, dlrc_f), None, reverse=True, name="gla_bwd_cb")
    dhc = _mm(dqkvc, wit_qkv, tm=256, tn=D, tk=1024, out_dtype=F32, name="mm_dhc")
    dhc = _mm(dlrc, wlrt, tm=256, tn=D, tk=LRW, out_dtype=F32, name="mm_dhc_lr", acc=dhc)
    _, dng_c, dscale_c, dshift_c = _prep_bwd(ctx, dhc, None, ng, scale_c, "prep_bwd_c")

    if exchange is not None:
        ready = exchange[0](dwpa, dwpb, dwo)
        dwit_g, ready_other = _mm_tn(dp_g, h, ta=1024, tn=D, tk=2048, name="mm_dwi_g", out_dtype=BF16, swap=ready)
    else:
        dwit_g = _mm_tn(dp_g, h, ta=1024, tn=D, tk=2048, name="mm_dwi_g", out_dtype=BF16)
    dwit_r, dwlrt = _mm_tn(dp, h, ta=1024, tn=D, tk=2048, name="mm_dwi_r", out_dtype=BF16, extra=dlr)
    dwit_qkv = _mm_tn(dqkvc, hc, ta=1024, tn=D, tk=256, name="mm_dwi_c", acc=dwit_r[2048:3072], out_dtype=BF16)
    dwlrt = _mm_tn(dlrc, hc, ta=LRW, tn=D, tk=256, name="mm_dwlr_c", acc=dwlrt, out_dtype=BF16)
    big = dict(dwit_g=dwit_g, dwit_r=dwit_r, dwit_qkv=dwit_qkv, dwlrt=dwlrt, dwpa=dwpa, dwpb=dwpb, dwo=dwo)

    dmodc = jnp.concatenate([dshift_c, dscale_c], axis=1)
    dscc = _dcctx(jnp.zeros((8, 2 * D), F32).at[0:1].set(dmodc), wm)[0:1]
    dw2p = dw2_f + dw2c_f, dw2_b + dw2c_b
    small = dict(
        dmodc=dmodc, dscc=dscc, dng_c=dng_c, dlng=dlng, dlnb=dlnb, dws=jnp.concatenate([dws01, dws23], axis=0),
        dbs=jnp.concatenate([dbs01, dbs23], axis=0)[:, :, 0], dgbn=dgbn, dgf=dgf,
        dw2=jnp.stack([dw2p[0][0:16], dw2p[1][16:32]]), dgb2=jnp.concatenate([dgb_f + dgbc_f, dgb_b + dgbc_b], axis=0),
        loss=loss[0, 0], dgate=dgate)

    send = exchange[1](big, ready, ready_other) if exchange is not None else ()
    early = _pack([small[k] for k in EARLY_KEYS[:-2]] + [jnp.broadcast_to(small["loss"], (128,)), small["dgate"]], EARLY_ROWS) \
        if exchange is not None else None
    (dx, dng, dscale, dshift), got, early_all = _proj_bwd(dp_g, dp, dlr, wit_g, wit_r, wlrt, x, dx1, ng, scale, send, early)
    return dict(dx=dx, got=got, early=early, early_all=early_all, dshift=dshift, dscale=dscale, dng_lat=dng, **big, **small)


ANY = pl.BlockSpec(memory_space=pl.ANY)


def _coords():
    return lax.axis_index("x"), lax.axis_index("y"), lax.axis_index("c")


def _flip(v, bit):
    return 1 - v if bit else v


def _remote(src, dst, send_sem, recv_sem, dev):
    return pltpu.make_async_remote_copy(src_ref=src, dst_ref=dst, send_sem=send_sem, recv_sem=recv_sem,
                                        device_id=dev, device_id_type=MESH)


def _own(out, block, idx):
    return lax.dynamic_update_slice_in_dim(out, block[None], idx, axis=0)


def _half_idx(shape, axis, which, lead=()):
    idx = [pl.ds(0, d) for d in shape]
    h = shape[axis] // 2
    idx[axis] = pl.ds(which * h, h)
    return tuple(lead) + tuple(idx)


def _gather_weights(split, whole, name):
    ns, nw = len(split), len(whole)
    n = ns + nw
    arrs = [a for a, _ in split] + list(whole)

    def body(*refs):
        ins, outs = refs[:n], refs[n:2 * n]
        a_send, a_recv, b_send, b_recv = refs[2 * n:]
        x, y, c = _coords()
        me = 2 * x + y
        sib = (x, y, 1 - c)
        peers = [(1 - x, y), (x, 1 - y), (1 - x, 1 - y)]

        def half(k, slot, which):
            return outs[k].at[_half_idx(arrs[k].shape, split[k][1], which, lead=(slot,))]

        sends = []
        for k in range(n):
            for j, (px, py) in enumerate(peers):
                if k < ns:
                    rc = _remote(ins[k].at[_half_idx(arrs[k].shape, split[k][1], c)], half(k, me, c), a_send.at[3 * k + j],
                                 a_recv.at[3 * k + j], (px, py, c))
                else:
                    rc = _remote(ins[k], outs[k].at[me], a_send.at[3 * k + j], a_recv.at[3 * k + j], (px, py, c))
                rc.start()
                sends.append(rc)
        for k in range(ns):
            for j, (px, py) in enumerate(peers):
                landed = half(k, 2 * px + py, c)
                _remote(landed, landed, a_send.at[3 * k + j], a_recv.at[3 * k + j], (px, py, c)).wait_recv()
                fw = _remote(landed, landed, b_send.at[3 * k + j], b_recv.at[3 * k + j], sib)
                fw.start()
                sends.append(fw)
        for k in range(ns, n):
            for j, (px, py) in enumerate(peers):
                landed = outs[k].at[2 * px + py]
                _remote(landed, landed, a_send.at[3 * k + j], a_recv.at[3 * k + j], (px, py, c)).wait_recv()
        for k in range(ns):
            for j, (px, py) in enumerate(peers):
                passed = half(k, 2 * px + py, 1 - c)
                _remote(passed, passed, b_send.at[3 * k + j], b_recv.at[3 * k + j], sib).wait_recv()
        for rc in sends:
            rc.wait_send()

    outs = _pcall(
        body, name=name, in_specs=[ANY] * n, out_specs=[ANY] * n,
        out_shape=[SDS((4,) + a.shape, a.dtype) for a in arrs],
        scratch_shapes=[pltpu.SemaphoreType.DMA((3 * n,)), pltpu.SemaphoreType.DMA((3 * n,)), pltpu.SemaphoreType.DMA((3 * ns,)),
                        pltpu.SemaphoreType.DMA((3 * ns,))],
    )(*arrs)
    me_xy = 2 * lax.axis_index("x") + lax.axis_index("y")
    return [_own(o, a, me_xy) for o, a in zip(outs, arrs)]


def _gather_all(a, swap, name):
    masks = [(mx, my, mc) for mx in range(2) for my in range(2) for mc in range(2)][1:]
    n = len(swap)

    def body(*refs):
        in_ref, sw_in = refs[0], refs[1:1 + n]
        out_ref, sw_out = refs[1 + n], refs[2 + n:2 + 2 * n]
        send_sems, recv_sems = refs[2 + 2 * n:]
        x, y, c = _coords()
        me = 4 * x + 2 * y + c
        sends = []
        for j, (mx, my, mc) in enumerate(masks):
            rc = _remote(in_ref, out_ref.at[me], send_sems.at[j], recv_sems.at[j], (_flip(x, mx), _flip(y, my), _flip(c, mc)))
            rc.start()
            sends.append(rc)
        for k in range(n):
            rc = _remote(sw_in[k], sw_out[k], send_sems.at[7 + k], recv_sems.at[7 + k], (x, y, 1 - c))
            rc.start()
            sends.append(rc)
        for j, (mx, my, mc) in enumerate(masks):
            px, py, pc = _flip(x, mx), _flip(y, my), _flip(c, mc)
            landed = out_ref.at[4 * px + 2 * py + pc]
            _remote(landed, landed, send_sems.at[j], recv_sems.at[j], (px, py, pc)).wait_recv()
        for k in range(n):
            _remote(sw_out[k], sw_out[k], send_sems.at[7 + k], recv_sems.at[7 + k], (x, y, 1 - c)).wait_recv()
        for rc in sends:
            rc.wait_send()

    res = _pcall(
        body, name=name, in_specs=[ANY] * (1 + n), out_specs=[ANY] * (1 + n),
        out_shape=[SDS((8,) + a.shape, a.dtype)] + [SDS(s_.shape, s_.dtype) for s_ in swap],
        scratch_shapes=[pltpu.SemaphoreType.DMA((7 + n,)), pltpu.SemaphoreType.DMA((7 + n,))],
    )(a, *swap)
    return _own(res[0], a, 4 * lax.axis_index("x") + 2 * lax.axis_index("y") + lax.axis_index("c")), list(res[1:])


def _half_shape(shape, axis):
    return tuple(d // 2 if i == axis else d for i, d in enumerate(shape))


def _swap_half_c(arrs, axes, name):
    n = len(arrs)

    def body(*refs):
        ins, outs = refs[:n], refs[n:2 * n]
        send_sems, recv_sems = refs[2 * n:]
        x, y, c = _coords()
        sends = []
        for k in range(n):
            rc = _remote(ins[k].at[_half_idx(arrs[k].shape, axes[k], 1 - c)], outs[k], send_sems.at[k], recv_sems.at[k],
                         (x, y, 1 - c))
            rc.start()
            sends.append(rc)
        for rc in sends:
            rc.wait()

    return _pcall(
        body, name=name, in_specs=[ANY] * n, out_specs=[ANY] * n,
        out_shape=[SDS(_half_shape(a.shape, ax), a.dtype) for a, ax in zip(arrs, axes)],
        scratch_shapes=[pltpu.SemaphoreType.DMA((n,)), pltpu.SemaphoreType.DMA((n,))],
    )(*arrs)


def _pair_sum(a, got, cidx, axis, name):
    _, r, cdim = a.shape
    hshape = _half_shape(a.shape, axis)

    def body(c_ref, a_ref, g_ref, o_ref):
        o_ref[...] = (a_ref[...].astype(F32) + g_ref[...].astype(F32)).astype(BF16)

    if axis == 1:
        tr = min(r // 2, 256)
        nj = (r // 2) // tr
        blk = pl.BlockSpec((1, tr, cdim), lambda s, j, c: (s, j, 0))
        a_spec = pl.BlockSpec((1, tr, cdim), lambda s, j, c: (s, c[0] * nj + j, 0))
    else:
        nj, hw = 1, cdim // 2
        blk = pl.BlockSpec((1, r, hw), lambda s, j, c: (s, 0, 0))
        a_spec = pl.BlockSpec((1, r, hw), lambda s, j, c: (s, 0, c[0]))
    return _pcall(
        body, name=name, out_shape=SDS(hshape, BF16),
        grid_spec=pltpu.PrefetchScalarGridSpec(num_scalar_prefetch=1, grid=(4, nj), in_specs=[a_spec, blk], out_specs=blk),
        compiler_params=_cp(("parallel", "parallel"), VMEM_BIG),
    )(cidx, a, got)


def _sum_chips(parts, name):
    _, h, cdim = parts.shape

    def body(p_ref, o_ref):
        acc = p_ref[0].astype(F32)
        for k in range(1, 4):
            acc = acc + p_ref[k].astype(F32)
        o_ref[...] = acc

    if h % 256 == 0 or h in (128,):
        tr = min(h, 256)
        grid, in_spec, out_spec = (h // tr,), pl.BlockSpec((4, tr, cdim), lambda i: (0, i, 0)), pl.BlockSpec((tr, cdim), lambda i: (i, 0))
    else:
        lw = 256
        grid, in_spec, out_spec = (cdim // lw,), pl.BlockSpec((4, h, lw), lambda i: (0, 0, i)), pl.BlockSpec((h, lw), lambda i: (0, i))
    return _pcall(
        body, name=name, grid=grid, in_specs=[in_spec], out_specs=out_spec, out_shape=SDS((h, cdim), F32),
        compiler_params=_cp(("parallel",), VMEM_BIG),
    )(parts)


def _sum_slots(a, name, rows):
    s, n, _ = a.shape

    def body(a_ref, o_ref):
        acc = a_ref[0]
        for k in range(1, s):
            acc = acc + a_ref[k]
        o_ref[...] = acc

    return _pcall(
        body, name=name, grid=(n // rows,), in_specs=[pl.BlockSpec((s, rows, 128), lambda i: (0, i, 0))],
        out_specs=pl.BlockSpec((rows, 128), lambda i: (i, 0)), out_shape=SDS((n, 128), F32),
        compiler_params=_cp(("parallel",)),
    )(a)


def _adam_math(w, g, m, v):
    nm = ADAM_B1 * m + (1.0 - ADAM_B1) * g
    nv = ADAM_B2 * v + (1.0 - ADAM_B2) * (g * g)
    m_hat = nm / (1.0 - ADAM_B1 ** ADAM_STEP)
    v_hat = nv / (1.0 - ADAM_B2 ** ADAM_STEP)
    return -ADAM_LR * (m_hat / (jnp.sqrt(v_hat) + ADAM_EPS) + ADAM_WD * w), nm, nv


def _adamw(w, g, m, v, name, rows):
    r, cdim = w.shape

    def body(w_ref, g_ref, m_ref, v_ref, d_ref, nm_ref, nv_ref):
        d_ref[...], nm_ref[...], nv_ref[...] = _adam_math(w_ref[...], g_ref[...], m_ref[...], v_ref[...])

    blk = pl.BlockSpec((rows, cdim), lambda i: (i, 0))
    return _pcall(
        body, name=name, grid=(r // rows,), in_specs=[blk] * 4, out_specs=[blk] * 3,
        out_shape=[SDS(w.shape, F32)] * 3, compiler_params=_cp(("parallel",)),
    )(w, g, m, v)


def _adamw_joined(w, mine, other, m, v, cidx, axis, name, rows):
    r, cdim = w.shape
    if axis == 0:
        rows = r

    def body(c_ref, w_ref, a_ref, b_ref, m_ref, v_ref, g_ref, d_ref, nm_ref, nv_ref):
        a, b = a_ref[...], b_ref[...]
        g = jnp.where(c_ref[0] == 0, jnp.concatenate([a, b], axis=axis), jnp.concatenate([b, a], axis=axis))
        g_ref[...] = g
        d_ref[...], nm_ref[...], nv_ref[...] = _adam_math(w_ref[...], g, m_ref[...], v_ref[...])

    blk = pl.BlockSpec((rows, cdim), lambda i, c: (i, 0))
    hshape = (rows // 2, cdim) if axis == 0 else (rows, cdim // 2)
    hblk = pl.BlockSpec(hshape, lambda i, c: (i, 0))
    return _pcall(
        body, name=name, out_shape=[SDS(w.shape, F32)] * 4,
        grid_spec=pltpu.PrefetchScalarGridSpec(num_scalar_prefetch=1, grid=(r // rows,), in_specs=[blk, hblk, hblk, blk, blk],
                                               out_specs=[blk] * 4),
        compiler_params=_cp(("parallel",)),
    )(cidx, w, mine, other, m, v)


def _adamw_many(ws, gs, ms, vs, name):
    n = len(ws)

    def body(*refs):
        outs = refs[4 * n:]
        for k in range(n):
            d, nm, nv = _adam_math(refs[k][...], refs[n + k][...], refs[2 * n + k][...], refs[3 * n + k][...])
            outs[k][...] = d
            outs[n + k][...] = nm
            outs[2 * n + k][...] = nv

    res = _pcall(body, name=name, out_shape=[SDS(w.shape, F32) for w in ws] * 3)(*ws, *gs, *ms, *vs)
    return res[:n], res[n:2 * n], res[2 * n:]


def _pack(pieces, rows):
    flat = jnp.concatenate([p.reshape(-1) for p in pieces])
    return jnp.pad(flat, (0, rows * 128 - flat.shape[0])).reshape(rows, 128)


def _unpack(buf, shapes):
    flat = buf.reshape(-1)
    out, off = [], 0
    for shp in shapes:
        size = 1
        for s in shp:
            size *= s
        out.append(flat[off:off + size].reshape(shp))
        off += size
    return out


LATE_ROWS = 32


def kernel(x, c, ctx, c_ctx, w_mod, b_mod, norm_g, w_in, a_ln_g, a_ln_b, a_ws, a_bs, b_gate_w2, b_gate_b, b_norm_g, w_proj_a, w_proj_b, w_out, final_norm_g, loss_target, m_c_ctx, m_w_mod, m_b_mod, m_norm_g, m_w_in, m_a_ln_g, m_a_ln_b, m_a_ws, m_a_bs, m_b_gate_w2, m_b_gate_b, m_b_norm_g, m_w_proj_a, m_w_proj_b, m_w_out, m_final_norm_g, v_c_ctx, v_w_mod, v_b_mod, v_norm_g, v_w_in, v_a_ln_g, v_a_ln_b, v_a_ws, v_a_bs, v_b_gate_w2, v_b_gate_b, v_b_norm_g, v_w_proj_a, v_w_proj_b, v_w_out, v_final_norm_g):
    xi, yi, ci = _coords()
    me_xy = 2 * xi + yi

    gate_pack = _pack([b_gate_w2[0], b_gate_b[0]], 24)
    w_in_t, m_w_in_t, v_w_in_t = (jnp.swapaxes(a[0], 0, 1) for a in (w_in, m_w_in, v_w_in))
    g_wit, g_wm, g_gate = _gather_weights([(w_in_t.astype(BF16), 1), (w_mod[0].astype(BF16), 0)], [gate_pack], "gather_weights")
    late_shards = (w_proj_a[0].astype(BF16), w_proj_b[0].astype(BF16), w_out[0].astype(BF16))
    wit_u = g_wit.reshape(4 * 1288, D)
    wit_g = wit_u[3104:5152]
    wit_r = jnp.concatenate([wit_u[1056:1568], wit_u[1568:2080], wit_u[2592:3104], wit_u[2080:2592], wit_u[0:1024]], axis=0)
    wlrt = jnp.pad(wit_u[1024:1056], ((0, LRW - 32), (0, 0)))
    wm = jnp.swapaxes(g_wm, 0, 1).reshape(D, 3 * D)
    gflat = g_gate.reshape(4, 24 * 128)
    w2 = jnp.swapaxes(gflat[:, 0:2048].reshape(4, 2, 16, 64), 0, 2)
    w2 = jnp.swapaxes(w2, 0, 1).reshape(2, 16, 256)
    gb2 = jnp.swapaxes(gflat[:, 2048:2176].reshape(4, 2, 64), 0, 1).reshape(2, 256)

    tags = ["wi", "wpa", "wpb", "wo"]
    half_axes = [2, 1, 1, 1]
    sent = []

    def ready_blocks(dwpa, dwpb, dwo):
        return [dwpa, dwpb, dwo.reshape(4, 256, D)], half_axes[1:]

    def exchange(g, ready, ready_other):
        dwr = g["dwit_r"]
        dwit_u = jnp.concatenate([g["dwit_qkv"], g["dwlrt"][0:32], dwr[0:512], dwr[512:1024], dwr[1536:2048], dwr[1024:1536],
                                  g["dwit_g"]], axis=0)
        big = [dwit_u.reshape(4, 1288, D)] + ready[0]
        other = list(_swap_half_c(big[:1], half_axes[:1], "swap_half_in")) + ready_other
        cidx = jnp.reshape(ci, (1,)).astype(jnp.int32)
        sent.extend(_pair_sum(a, o, cidx, ax, "sum_pair_" + t) for a, o, ax, t in zip(big, other, half_axes, tags))
        return sent

    r = _device_step(x[0], c, ctx[0], c_ctx[None], loss_target[0], wm, b_mod, norm_g, wit_g, wit_r, wlrt, a_ln_g, a_ln_b,
                     a_ws[0], a_bs[0], w2, gb2, b_norm_g, None, None, None, final_norm_g[None], (ready_blocks, exchange), late_shards)

    parts = [_own(g, lax.dynamic_index_in_dim(s_, me_xy, axis=0, keepdims=False), me_xy) for g, s_ in zip(r["got"], sent)]
    halves = [_sum_chips(p_, "sum_chips_" + t) for p_, t in zip(parts, tags)]

    me8 = 4 * xi + 2 * yi + ci
    early_all = _own(r["early_all"], r["early"], me8)
    late = _pack([r["dshift"], r["dscale"], r["dng_lat"], c], LATE_ROWS)
    late_all, others = _gather_all(late, halves, "gather_small")
    s_early = _sum_slots(early_all, "sum_early", EARLY_ROWS // 3)
    s_late = _sum_slots(late_all, "sum_late", LATE_ROWS)
    (s_dmodc, s_dscc, s_dng_c, s_dlng, s_dlnb, s_dws, s_dbs, s_dgbn, s_dgf, s_dw2, s_dgb2, s_loss, s_dgate) = _unpack(
        s_early, EARLY_SHAPES)
    s_dshift, s_dscale, s_dng_lat, _ = _unpack(s_late, [(1, D)] * 4)
    s_dng = s_dng_lat + s_dng_c
    s_dmod = jnp.concatenate([s_dshift, s_dscale, s_dgate], axis=1)
    loss = s_loss[0]
    s_dmodc_p = jnp.pad(s_dmodc, ((0, 0), (0, D)))
    g_b_mod = s_dmod + s_dmodc_p
    sg = jax.nn.sigmoid(c_ctx)
    g_c_ctx = s_dscc * (sg * (1.0 + c_ctx * (1.0 - sg)))
    g_w2 = lax.dynamic_slice_in_dim(s_dw2, 64 * me_xy, 64, axis=2)[None]
    g_gb2 = lax.dynamic_slice_in_dim(s_dgb2, 64 * me_xy, 64, axis=1)[None]

    flat_l = late_all.reshape(8, LATE_ROWS * 128)
    dgate_all = early_all.reshape(8, EARLY_ROWS * 128)[:, EARLY_SIZE - D:EARLY_SIZE]
    dmod_all = jnp.concatenate([flat_l[:, 0:2 * D], dgate_all], axis=1)
    c_all = flat_l[:, 3 * D:4 * D]
    lhs = jnp.concatenate([_silu(c_all), _silu(c_ctx)[None], jnp.zeros((7, D), F32)], axis=0)
    rhs = jnp.concatenate([dmod_all, s_dmodc_p, jnp.zeros((7, 3 * D), F32)], axis=0)
    rhs = lax.dynamic_slice_in_dim(rhs, 768 * me_xy, 768, axis=1)
    g_w_mod = _mm(lhs.T.astype(BF16), rhs.astype(BF16), tm=D, tn=768, tk=16, out_dtype=F32, name="mm_dwm")

    cidx = jnp.reshape(ci, (1,)).astype(jnp.int32)
    g_w_in_t, d_w_in_t, nm_w_in_t, nv_w_in_t = _adamw_joined(w_in_t, halves[0], others[0], m_w_in_t, v_w_in_t, cidx, 1,
                                                             "adamw_w_in", 184)
    g_w_in, d_w_in, nm_w_in, nv_w_in = (jnp.swapaxes(a, 0, 1) for a in (g_w_in_t, d_w_in_t, nm_w_in_t, nv_w_in_t))
    g_wpa, d_wpa, nm_wpa, nv_wpa = _adamw_joined(w_proj_a[0], halves[1], others[1], m_w_proj_a[0], v_w_proj_a[0], cidx, 0,
                                                 "adamw_wpa", 0)
    g_wpb, d_wpb, nm_wpb, nv_wpb = _adamw_joined(w_proj_b[0], halves[2], others[2], m_w_proj_b[0], v_w_proj_b[0], cidx, 0,
                                                 "adamw_wpb", 0)
    g_wo, d_wo, nm_wo, nv_wo = _adamw_joined(w_out[0], halves[3], others[3], m_w_out[0], v_w_out[0], cidx, 0, "adamw_wo", 0)
    d_w_mod, nm_w_mod, nv_w_mod = _adamw(w_mod[0], g_w_mod, m_w_mod[0], v_w_mod[0], "adamw_w_mod", 256)

    names = ["c_ctx", "b_mod", "norm_g", "a_ln_g", "a_ln_b", "a_ws", "a_bs", "b_gate_w2", "b_gate_b", "b_norm_g", "final_norm_g"]
    ws_ = [c_ctx, b_mod, norm_g, a_ln_g, a_ln_b, a_ws, a_bs, b_gate_w2, b_gate_b, b_norm_g, final_norm_g]
    gs_ = [g_c_ctx, g_b_mod, s_dng, s_dlng, s_dlnb, s_dws, s_dbs, g_w2, g_gb2, s_dgbn, s_dgf]
    ms_ = [m_c_ctx, m_b_mod, m_norm_g, m_a_ln_g, m_a_ln_b, m_a_ws, m_a_bs, m_b_gate_w2, m_b_gate_b, m_b_norm_g, m_final_norm_g]
    vs_ = [v_c_ctx, v_b_mod, v_norm_g, v_a_ln_g, v_a_ln_b, v_a_ws, v_a_bs, v_b_gate_w2, v_b_gate_b, v_b_norm_g, v_final_norm_g]
    shapes = [w.shape for w in ws_]
    flat2 = [(1, 1024), (1, 3072), (1, 1024), (1, 512), (1, 512), (512, 128), (4, 128), (32, 64), (2, 64), (1, 512), (1, 1024)]
    as2d = lambda arrs: [a.reshape(s) for a, s in zip(arrs, flat2)]
    d_s, nm_s, nv_s = _adamw_many(as2d(ws_), as2d(gs_), as2d(ms_), as2d(vs_), "adamw_small")
    d_small = {n: a.reshape(s) for n, a, s in zip(names, d_s, shapes)}
    nm_small = {n: a.reshape(s) for n, a, s in zip(names, nm_s, shapes)}
    nv_small = {n: a.reshape(s) for n, a, s in zip(names, nv_s, shapes)}
    g_small = {n: g.reshape(s) for n, g, s in zip(names, gs_, shapes)}

    order = ["c_ctx", "w_mod", "b_mod", "norm_g", "w_in", "a_ln_g", "a_ln_b", "a_ws", "a_bs", "b_gate_w2", "b_gate_b", "b_norm_g",
             "w_proj_a", "w_proj_b", "w_out", "final_norm_g"]
    big_g = dict(w_mod=g_w_mod[None], w_in=g_w_in[None], w_proj_a=g_wpa[None], w_proj_b=g_wpb[None], w_out=g_wo[None])
    big_d = dict(w_mod=d_w_mod[None], w_in=d_w_in[None], w_proj_a=d_wpa[None], w_proj_b=d_wpb[None], w_out=d_wo[None])
    big_m = dict(w_mod=nm_w_mod[None], w_in=nm_w_in[None], w_proj_a=nm_wpa[None], w_proj_b=nm_wpb[None], w_out=nm_wo[None])
    big_v = dict(w_mod=nv_w_mod[None], w_in=nv_w_in[None], w_proj_a=nv_wpa[None], w_proj_b=nv_wpb[None], w_out=nv_wo[None])
    grads = [big_g[n] if n in big_g else g_small[n] for n in order]
    deltas = [big_d[n] if n in big_d else d_small[n] for n in order]
    new_m = [big_m[n] if n in big_m else nm_small[n] for n in order]
    new_v = [big_v[n] if n in big_v else nv_small[n] for n in order]
    return (loss, r["dx"][None], *grads, *deltas, *new_m, *new_v)
```

```python
import jax
import jax.numpy as jnp
from jax import lax
from jax.experimental import pallas as pl
from jax.experimental.pallas import tpu as pltpu

F32 = jnp.float32
BF16 = jnp.bfloat16
SDS = jax.ShapeDtypeStruct

D = 1024
NP = 5120
LRW = 128
MODW = 3 * D // 4
CH = 64
AC = 128
EPS = 1e-6
TOK = 512
GLA_TB = 1024
VMEM_BIG = 48 * 1024 * 1024

ADAM_LR, ADAM_B1, ADAM_B2, ADAM_EPS, ADAM_WD, ADAM_STEP = 0.001, 0.9, 0.999, 1e-08, 0.01, 10

_pcall = pl.pallas_call
MESH = pl.DeviceIdType.MESH


def _cp(sem=None, vmem=None):
    kw = {}
    if sem is not None:
        kw["dimension_semantics"] = sem
    if vmem is not None:
        kw["vmem_limit_bytes"] = vmem
    return pltpu.CompilerParams(**kw)


def _silu(x):
    return x * jax.nn.sigmoid(x)


def _silu_and_grad(x):
    s = jax.nn.sigmoid(x)
    return x * s, s * (1.0 + x * (1.0 - s))


def _logsig(x):
    return jnp.minimum(x, 0.0) - jnp.log1p(jnp.exp(-jnp.abs(x)))


def _nt(a, b):
    return lax.dot_general(a, b, (((1,), (1,)), ((), ())), preferred_element_type=F32)


def _tn(a, b):
    return lax.dot_general(a, b, (((0,), (0,)), ((), ())), preferred_element_type=F32)


def _nn(a, b):
    return jnp.dot(a, b, preferred_element_type=F32)


def _full(shape):
    return pl.BlockSpec(shape, lambda *_: (0,) * len(shape))


def _mm(a, b, *, tm, tn, tk, out_dtype, name, acc=None, b_t=False):
    m, k = a.shape
    n, k2 = (b.shape if b_t else b.shape[::-1])
    assert k == k2 == tk and m % tm == 0 and n % tn == 0, (a.shape, b.shape, tm, tn, tk)
    has_acc = acc is not None

    def body(*refs):
        if has_acc:
            a_ref, b_ref, c_ref, o_ref = refs
        else:
            a_ref, b_ref, o_ref = refs
        part = (_nt if b_t else _nn)(a_ref[...].astype(BF16), b_ref[...].astype(BF16))
        o_ref[...] = ((c_ref[...] + part) if has_acc else part).astype(out_dtype)

    b_spec = pl.BlockSpec((tn, tk), lambda i, j: (j, 0)) if b_t else pl.BlockSpec((tk, tn), lambda i, j: (0, j))
    in_specs = [pl.BlockSpec((tm, tk), lambda i, j: (i, 0)), b_spec]
    args = [a, b]
    if has_acc:
        in_specs.append(pl.BlockSpec((tm, tn), lambda i, j: (i, j)))
        args.append(acc)
    return _pcall(
        body, name=name, grid=(m // tm, n // tn), in_specs=in_specs, out_specs=pl.BlockSpec((tm, tn), lambda i, j: (i, j)),
        out_shape=SDS((m, n), out_dtype), compiler_params=_cp(("parallel", "parallel"), VMEM_BIG),
    )(*args)


def _mm_tn(a, b, *, ta, tn, tk, name, acc=None, out_dtype=F32, swap=None, extra=None):
    m, ka = a.shape
    m2, n = b.shape
    assert m == m2 and ka % ta == 0 and n % tn == 0 and m % tk == 0, (a.shape, b.shape, ta, tn, tk)
    nk = m // tk
    has_acc = acc is not None
    has_x = extra is not None
    sw_arrs, sw_axes = swap if swap is not None else ((), ())
    ns = len(sw_arrs)
    grid = (ka // ta, n // tn, nk)
    assert not has_x or (grid[1] == 1 and not ns)

    def body(*refs):
        a_ref, b_ref = refs[:2]
        pos = 2
        c_ref = refs[pos] if has_acc else None
        pos += has_acc
        x_ref = refs[pos] if has_x else None
        pos += has_x
        sw_in = refs[pos:pos + ns]
        o_ref = refs[pos + ns]
        pos += ns + 1
        o2_ref = refs[pos] if has_x else None
        pos += has_x
        sw_out = refs[pos:pos + ns]
        acc_ref = refs[pos + ns]
        sems = refs[pos + ns + 1:]
        kk = pl.program_id(2)

        if has_x:
            @pl.when(pl.program_id(0) == 0)
            def _():
                part2 = _tn(x_ref[...].astype(BF16), b_ref[...].astype(BF16))

                @pl.when(kk == 0)
                def _():
                    o2_ref[...] = part2

                @pl.when(kk > 0)
                def _():
                    o2_ref[...] += part2

        def copies():
            x, y, c = _coords()
            return [_remote(sw_in[k].at[_half_idx(sw_arrs[k].shape, sw_axes[k], 1 - c)], sw_out[k], sems[0].at[k],
                            sems[1].at[k], (x, y, 1 - c)) for k in range(ns)]

        step = (pl.program_id(0) * grid[1] + pl.program_id(1)) * nk + kk
        if ns:
            @pl.when(step == 0)
            def _():
                for rc in copies():
                    rc.start()

        part = _tn(a_ref[...].astype(BF16), b_ref[...].astype(BF16))

        @pl.when(kk == 0)
        def _():
            if has_acc:
                acc_ref[...] = c_ref[...].astype(F32) + part
            else:
                acc_ref[...] = part

        @pl.when(kk > 0)
        def _():
            acc_ref[...] += part

        @pl.when(kk == nk - 1)
        def _():
            o_ref[...] = acc_ref[...].astype(out_dtype)

        if ns:
            @pl.when(step == grid[0] * grid[1] * nk - 1)
            def _():
                for rc in copies():
                    rc.wait()

    in_specs = [pl.BlockSpec((tk, ta), lambda i, j, kk: (kk, i)), pl.BlockSpec((tk, tn), lambda i, j, kk: (kk, j))]
    args = [a, b]
    if has_acc:
        in_specs.append(pl.BlockSpec((ta, tn), lambda i, j, kk: (i, j)))
        args.append(acc)
    out_spec, out_shape = pl.BlockSpec((ta, tn), lambda i, j, kk: (i, j)), SDS((ka, n), out_dtype)
    scratch = [pltpu.VMEM((ta, tn), F32)]
    if has_x:
        kx = extra.shape[1]
        in_specs.append(pl.BlockSpec((tk, kx), lambda i, j, kk: (jnp.where(i == 0, kk, nk - 1), 0)))
        args.append(extra)
        return _pcall(body, name=name, grid=grid, in_specs=in_specs,
                      out_specs=[out_spec, pl.BlockSpec((kx, tn), lambda i, j, kk: (0, 0))],
                      out_shape=[out_shape, SDS((kx, n), F32)], scratch_shapes=scratch,
                      compiler_params=_cp(("arbitrary", "arbitrary", "arbitrary"), VMEM_BIG))(*args)
    if not ns:
        return _pcall(body, name=name, grid=grid, in_specs=in_specs, out_specs=out_spec, out_shape=out_shape,
                      scratch_shapes=scratch, compiler_params=_cp(("parallel", "parallel", "arbitrary"), VMEM_BIG))(*args)
    res = _pcall(
        body, name=name, grid=grid, in_specs=in_specs + [ANY] * ns, out_specs=[out_spec] + [ANY] * ns,
        out_shape=[out_shape] + [SDS(_half_shape(s.shape, ax), s.dtype) for s, ax in zip(sw_arrs, sw_axes)],
        scratch_shapes=scratch + [pltpu.SemaphoreType.DMA((ns,)), pltpu.SemaphoreType.DMA((ns,))],
        compiler_params=_cp(("arbitrary", "arbitrary", "arbitrary"), VMEM_BIG),
    )(*args, *sw_arrs)
    return res[0], list(res[1:])


def _modvec(cc, wm, bm):
    def body(c_ref, w_ref, b_ref, o_ref):
        cs = _silu(c_ref[...]).astype(BF16)
        for s in range(4):
            cols = slice(MODW * s, MODW * (s + 1))
            o_ref[:, cols] = _nn(cs, w_ref[s]) + b_ref[:, cols]

    return _pcall(body, name="modvec", out_shape=SDS((8, 3 * D), F32), compiler_params=_cp(None, VMEM_BIG))(cc, wm, bm)


def _dcctx(dmodc, wm):
    def body(d_ref, w_ref, o_ref):
        d = d_ref[...].astype(BF16)
        last = 2 * D - 2 * MODW
        o_ref[...] = (_nt(d[:, 0:MODW], w_ref[0]) + _nt(d[:, MODW:2 * MODW], w_ref[1])
                      + _nt(d[:, 2 * MODW:2 * D], w_ref[2][:, 0:last]))

    return _pcall(
        body, name="dcctx", grid=(1,), in_specs=[_full((8, 2 * D)), _full((4, D, MODW))],
        out_specs=_full((8, D)), out_shape=SDS((8, D), F32), compiler_params=_cp(("arbitrary",), VMEM_BIG),
    )(dmodc, wm)


def _prep_h(x, ng, scale, shift, name):
    m = x.shape[0]

    def body(x_ref, g_ref, sc_ref, sh_ref, h_ref):
        xf = x_ref[...]
        r = lax.rsqrt(jnp.mean(xf * xf, axis=-1, keepdims=True) + EPS)
        y = (xf * r) * g_ref[...]
        h_ref[...] = (y * (1.0 + sc_ref[...]) + sh_ref[...]).astype(BF16)

    tok = min(TOK, m)
    row = pl.BlockSpec((tok, D), lambda i: (i, 0))
    return _pcall(
        body, name=name, grid=(m // tok,), in_specs=[row, _full((1, D)), _full((1, D)), _full((1, D))],
        out_specs=row, out_shape=SDS((m, D), BF16), compiler_params=_cp(("parallel",)),
    )(x, ng, scale, shift)


def _resident(shape):
    return pl.BlockSpec(shape, lambda *_: (0,) * len(shape), pipeline_mode=pl.Buffered(1))


PROJ_TM = 512


def _proj_fwd(x, ng, scale, shift, wit_g, wit_r, wlrt, ln_g, ln_b, share=()):
    m = x.shape[0]
    ns = len(share)
    steps = m // PROJ_TM
    src = [(0, 0), (0, D), (1, 2 * D), (1, 0), (1, D)]

    def body(*refs):
        x_ref, g_ref, sc_ref, sh_ref, wg_ref, wr_ref, wl_ref, lg_ref, lb_ref = refs[:9]
        share_refs = refs[9:9 + ns]
        h_ref, p_ref, plr_ref, vr_ref, vc_ref = refs[9 + ns:14 + ns]
        got_refs = refs[14 + ns:14 + 2 * ns]
        sems = refs[14 + 2 * ns:]

        def copies():
            cx, cy, cc = _coords()
            me = 2 * cx + cy
            peers = [(1 - cx, cy), (cx, 1 - cy), (1 - cx, 1 - cy)]
            out, back = [], []
            for k in range(ns):
                for j, (px, py) in enumerate(peers):
                    out.append(_remote(share_refs[k], got_refs[k].at[me], sems[0].at[3 * k + j], sems[1].at[3 * k + j], (px, py, cc)))
                    landed = got_refs[k].at[2 * px + py]
                    back.append(_remote(landed, landed, sems[0].at[3 * k + j], sems[1].at[3 * k + j], (px, py, cc)))
            return out, back

        if ns:
            @pl.when(pl.program_id(0) == 0)
            def _():
                for rc in copies()[0]:
                    rc.start()

            @pl.when(pl.program_id(0) == steps - 1)
            def _():
                out, back = copies()
                for rc in back:
                    rc.wait_recv()
                for rc in out:
                    rc.wait_send()

        xf = x_ref[...]
        r = lax.rsqrt(jnp.mean(xf * xf, axis=-1, keepdims=True) + EPS)
        y = (xf * r) * g_ref[...]
        h = (y * (1.0 + sc_ref[...]) + sh_ref[...]).astype(BF16)
        h_ref[...] = h
        for j, (which, r0) in enumerate(src):
            w_ref = wr_ref if which else wg_ref
            blk = _nt(h, w_ref[r0:r0 + D, :]).astype(BF16)
            p_ref[:, D * j:D * j + D] = blk
            if j == 4:
                xf = blk[:, 512:1024].astype(F32)
                xc = xf - jnp.mean(xf, axis=-1, keepdims=True)
                vn = (xc * lax.rsqrt(jnp.mean(xc * xc, axis=-1, keepdims=True) + EPS)) * lg_ref[...] + lb_ref[...]
                vr_ref[...] = vn[:, 0:256].astype(BF16)
                vc_ref[0] = vn[:, 256:384].astype(BF16)
                vc_ref[1] = vn[:, 384:512].astype(BF16)
        plr_ref[...] = _nt(h, wl_ref[...])

    row = pl.BlockSpec((PROJ_TM, D), lambda i: (i, 0))
    vec = _full((1, D))
    res = _pcall(
        body, name="proj_fwd", grid=(steps,),
        in_specs=[row, vec, vec, vec, _resident((2 * D, D)), _resident((3 * D, D)), _resident((LRW, D)), _full((1, 512)),
                  _full((1, 512))] + [ANY] * ns,
        out_specs=[row, pl.BlockSpec((PROJ_TM, NP), lambda i: (i, 0)), pl.BlockSpec((PROJ_TM, LRW), lambda i: (i, 0)),
                   pl.BlockSpec((PROJ_TM, 256), lambda i: (i, 0)), pl.BlockSpec((2, PROJ_TM, 128), lambda i: (0, i, 0))] + [ANY] * ns,
        out_shape=[SDS((m, D), BF16), SDS((m, NP), BF16), SDS((m, LRW), F32), SDS((m, 256), BF16), SDS((2, m, 128), BF16)]
        + [SDS((4,) + a.shape, a.dtype) for a in share],
        scratch_shapes=([pltpu.SemaphoreType.DMA((3 * ns,)), pltpu.SemaphoreType.DMA((3 * ns,))] if ns else []),
        compiler_params=_cp(("arbitrary",), VMEM_BIG),
    )(x, ng, scale, shift, wit_g, wit_r, wlrt, ln_g, ln_b, *share)
    return res[0], res[1], res[2], res[3], res[4], list(res[5:])


def _proj_bwd(dp_g, dp_r, dlr, wit_g, wit_r, wlrt, x, dx1, ng, scale, send=(), share8=None):
    m = x.shape[0]
    ns = len(send)
    n8 = 0 if share8 is None else 1
    steps = m // PROJ_TM
    masks = [(mx, my, mc) for mx in range(2) for my in range(2) for mc in range(2)][1:]

    def body(*refs):
        (dpg_ref, dpr_ref, dlr_ref, wg_ref, wr_ref, wl_ref, x_ref, r_ref, g_ref, sc_ref) = refs[:10]
        send_refs = refs[10:10 + ns]
        n_in = 10 + ns + n8
        dx_ref, dg_ref, dsc_ref, dsh_ref = refs[n_in:n_in + 4]
        got_refs = refs[n_in + 4:n_in + 4 + ns]
        sems = refs[n_in + 4 + ns + n8:]
        i = pl.program_id(0)

        def copies():
            cx, cy, cc = _coords()
            me = 2 * cx + cy
            peers = [(1 - cx, cy), (cx, 1 - cy), (1 - cx, 1 - cy)]
            out, back = [], []
            for k in range(ns):
                for j, (px, py) in enumerate(peers):
                    out.append(_remote(send_refs[k].at[2 * px + py], got_refs[k].at[me], sems[0].at[3 * k + j],
                                       sems[1].at[3 * k + j], (px, py, cc)))
                    landed = got_refs[k].at[2 * px + py]
                    back.append(_remote(landed, landed, sems[0].at[3 * k + j], sems[1].at[3 * k + j], (px, py, cc)))
            if n8:
                src8, all8 = refs[10 + ns], refs[n_in + 4 + ns]
                s8, r8 = sems[-2], sems[-1]
                for j, (mx, my, mc) in enumerate(masks):
                    px, py, pc = _flip(cx, mx), _flip(cy, my), _flip(cc, mc)
                    out.append(_remote(src8, all8.at[4 * cx + 2 * cy + cc], s8.at[j], r8.at[j], (px, py, pc)))
                    landed = all8.at[4 * px + 2 * py + pc]
                    back.append(_remote(landed, landed, s8.at[j], r8.at[j], (px, py, pc)))
            return out, back

        @pl.when(i == 0)
        def _():
            dg_ref[...] = jnp.zeros_like(dg_ref)
            dsc_ref[...] = jnp.zeros_like(dsc_ref)
            dsh_ref[...] = jnp.zeros_like(dsh_ref)
            if ns or n8:
                for rc in copies()[0]:
                    rc.start()

        dh_ = (_nn(dpg_ref[...], wg_ref[...]) + _nn(dpr_ref[...], wr_ref[...])
               + _nn(dlr_ref[...].astype(BF16), wl_ref[...]))
        xf = x_ref[...]
        r = lax.rsqrt(jnp.mean(xf * xf, axis=-1, keepdims=True) + EPS)
        xh = xf * r
        y = xh * g_ref[...]
        dsh_ref[...] += jnp.sum(dh_, axis=0, keepdims=True)
        dsc_ref[...] += jnp.sum(dh_ * y, axis=0, keepdims=True)
        dy = dh_ * (1.0 + sc_ref[...])
        dg_ref[...] += jnp.sum(dy * xh, axis=0, keepdims=True)
        dxh = dy * g_ref[...]
        dx_ref[...] = r * (dxh - xh * jnp.mean(dxh * xh, axis=-1, keepdims=True)) + r_ref[...]

        if ns or n8:
            @pl.when(i == steps - 1)
            def _():
                out, back = copies()
                for rc in back:
                    rc.wait_recv()
                for rc in out:
                    rc.wait_send()

    row = pl.BlockSpec((PROJ_TM, D), lambda i: (i, 0))
    vec = _full((1, D))
    kg, kr = dp_g.shape[1], dp_r.shape[1]
    extra_in = list(send) + ([share8] if n8 else [])
    extra_out = [SDS(a.shape, a.dtype) for a in send] + ([SDS((8,) + share8.shape, share8.dtype)] if n8 else [])
    res = _pcall(
        body, name="proj_bwd", grid=(steps,),
        in_specs=[pl.BlockSpec((PROJ_TM, kg), lambda i: (i, 0)), pl.BlockSpec((PROJ_TM, kr), lambda i: (i, 0)),
                  pl.BlockSpec((PROJ_TM, LRW), lambda i: (i, 0)), _resident((kg, D)), _resident((kr, D)), _resident((LRW, D)),
                  row, row, vec, vec] + [ANY] * len(extra_in),
        out_specs=[row, vec, vec, vec] + [ANY] * len(extra_out),
        out_shape=[SDS((m, D), F32), SDS((1, D), F32), SDS((1, D), F32), SDS((1, D), F32)] + extra_out,
        scratch_shapes=(([pltpu.SemaphoreType.DMA((3 * ns,)), pltpu.SemaphoreType.DMA((3 * ns,))] if ns else [])
                        + ([pltpu.SemaphoreType.DMA((7,)), pltpu.SemaphoreType.DMA((7,))] if n8 else [])),
        compiler_params=_cp(("arbitrary",), VMEM_BIG),
    )(dp_g, dp_r, dlr, wit_g, wit_r, wlrt, x, dx1, ng, scale, *extra_in)
    return tuple(res[:4]), list(res[4:4 + ns]), (res[4 + ns] if n8 else None)


def _prep_bwd(x, dh, dx1, ng, scale, name):
    m = x.shape[0]
    has_res = dx1 is not None

    def body(*refs):
        if has_res:
            x_ref, dh_ref, r_ref, g_ref, sc_ref, dx_ref, dg_ref, dsc_ref, dsh_ref = refs
        else:
            x_ref, dh_ref, g_ref, sc_ref, dx_ref, dg_ref, dsc_ref, dsh_ref = refs
        i = pl.program_id(0)

        @pl.when(i == 0)
        def _():
            dg_ref[...] = jnp.zeros_like(dg_ref)
            dsc_ref[...] = jnp.zeros_like(dsc_ref)
            dsh_ref[...] = jnp.zeros_like(dsh_ref)

        xf = x_ref[...]
        dh_ = dh_ref[...]
        r = lax.rsqrt(jnp.mean(xf * xf, axis=-1, keepdims=True) + EPS)
        xh = xf * r
        y = xh * g_ref[...]
        dsh_ref[...] += jnp.sum(dh_, axis=0, keepdims=True)
        dsc_ref[...] += jnp.sum(dh_ * y, axis=0, keepdims=True)
        dy = dh_ * (1.0 + sc_ref[...])
        dg_ref[...] += jnp.sum(dy * xh, axis=0, keepdims=True)
        dxh = dy * g_ref[...]
        dx = r * (dxh - xh * jnp.mean(dxh * xh, axis=-1, keepdims=True))
        if has_res:
            dx = dx + r_ref[...]
        dx_ref[...] = dx

    tok = min(TOK, m)
    row = pl.BlockSpec((tok, D), lambda i: (i, 0))
    vec = _full((1, D))
    in_specs = [row, row] + ([row] if has_res else []) + [vec, vec]
    args = [x, dh] + ([dx1] if has_res else []) + [ng, scale]
    return _pcall(
        body, name=name, grid=(m // tok,), in_specs=in_specs, out_specs=[row, vec, vec, vec],
        out_shape=[SDS((m, D), F32), SDS((1, D), F32), SDS((1, D), F32), SDS((1, D), F32)],
        compiler_params=_cp(("arbitrary",)),
    )(*args)


COLB = 2048


def _colmix_fwd(vnc, ws23, bs23):
    rows = vnc.shape[2] // COLB

    def body(v_ref, w_ref, b_ref, o_ref):
        o_ref[0] = _nn(w_ref[0], v_ref[0]) + b_ref[0]

    return _pcall(
        body, name="colmix_fwd", grid=(2, rows),
        in_specs=[pl.BlockSpec((1, AC, COLB), lambda g, j: (g, 0, j)), pl.BlockSpec((1, AC, AC), lambda g, j: (g, 0, 0)),
                  pl.BlockSpec((1, AC, 1), lambda g, j: (g, 0, 0))],
        out_specs=pl.BlockSpec((1, AC, COLB), lambda g, j: (g, 0, j)),
        out_shape=SDS(vnc.shape, F32), compiler_params=_cp(("parallel", "parallel")),
    )(vnc, ws23, bs23)


def _colmix_bwd(dsvc, vnc, ws23t):
    rows = vnc.shape[2] // COLB

    def body(d_ref, v_ref, wt_ref, dv_ref, dw_ref, db_ref):
        j = pl.program_id(1)

        @pl.when(j == 0)
        def _():
            dw_ref[...] = jnp.zeros_like(dw_ref)
            db_ref[...] = jnp.zeros_like(db_ref)

        d = d_ref[0]
        d16 = d.astype(BF16)
        dv_ref[0] = _nn(wt_ref[0], d16)
        dw_ref[0] += _nt(d16, v_ref[0])
        db_ref[0] += jnp.sum(d, axis=1, keepdims=True)

    blk = pl.BlockSpec((1, AC, COLB), lambda g, j: (g, 0, j))
    return _pcall(
        body, name="colmix_bwd", grid=(2, rows),
        in_specs=[blk, blk, pl.BlockSpec((1, AC, AC), lambda g, j: (g, 0, 0))],
        out_specs=[blk, pl.BlockSpec((1, AC, AC), lambda g, j: (g, 0, 0)), pl.BlockSpec((1, AC, 1), lambda g, j: (g, 0, 0))],
        out_shape=[SDS(vnc.shape, F32), SDS((2, AC, AC), F32), SDS((2, AC, 1), F32)],
        compiler_params=_cp(("parallel", "arbitrary")),
    )(dsvc, vnc, ws23t)


def _head_norm(o):
    out = []
    for h in range(4):
        oh = o[:, 128 * h:128 * h + 128]
        r = lax.rsqrt(jnp.mean(oh * oh, axis=-1, keepdims=True) + EPS)
        out.append((r, oh * r))
    return out


def _tail_fwd(o_f, o_b, p, vnr, svc, x, tgt, ws01, bs01, gbn, wpa, wpb, wo, gate, gf):
    m = p.shape[0]

    def body(of_ref, ob_ref, zb_ref, ua_ref, za_ref, ga_ref, gb_ref, vnr_ref, svc_ref, x_ref, t_ref, w_ref, b_ref, g_ref,
             wpa_ref, wpb_ref, wo_ref, gate_ref, gf_ref,
             ya_ref, yb_ref, svr_ref, dwo_ref, dx1_ref, dout_ref, loss_ref, dgate_ref, dgf_ref):
        i = pl.program_id(0)

        @pl.when(i == 0)
        def _():
            loss_ref[...] = jnp.zeros_like(loss_ref)
            dgate_ref[...] = jnp.zeros_like(dgate_ref)
            dgf_ref[...] = jnp.zeros_like(dgf_ref)
            dwo_ref[...] = jnp.zeros_like(dwo_ref)

        o = of_ref[...] + ob_ref[...]
        zb = zb_ref[...].astype(F32)
        for h, (r, xh) in enumerate(_head_norm(o)):
            sl = slice(128 * h, 128 * h + 128)
            yb_ref[:, sl] = ((xh * g_ref[:, sl]) * _silu(zb[:, sl])).astype(BF16)
        for j in range(TOK // AC):
            for g in range(2):
                sv = _nn(w_ref[g], vnr_ref[AC * j:AC * j + AC, AC * g:AC * g + AC]) + b_ref[g]
                svr_ref[AC * j:AC * j + AC, AC * g:AC * g + AC] = sv
        sz = _silu(za_ref[...].astype(F32))
        u = ua_ref[...].astype(F32)
        ya_ref[:, 0:256] = ((u[:, 0:256] * svr_ref[...]) * sz[:, 0:256]).astype(BF16)
        ya_ref[:, 256:384] = ((u[:, 256:384] * svc_ref[0]) * sz[:, 256:384]).astype(BF16)
        ya_ref[:, 384:512] = ((u[:, 384:512] * svc_ref[1]) * sz[:, 384:512]).astype(BF16)
        ya = _nn(ya_ref[...], wpa_ref[...])
        yb = _nn(yb_ref[...], wpb_ref[...])
        mg = (jax.nn.sigmoid(ga_ref[...].astype(F32)) * ya + jax.nn.sigmoid(gb_ref[...].astype(F32)) * yb).astype(BF16)
        out_ = _nn(mg, wo_ref[...])
        x1 = x_ref[...] + gate_ref[...] * out_
        r = lax.rsqrt(jnp.mean(x1 * x1, axis=-1, keepdims=True) + EPS)
        xh = x1 * r
        err = xh * gf_ref[...] - t_ref[...]
        loss_ref[...] += 0.5 * jnp.sum(jnp.mean(err * err, axis=-1, keepdims=True), axis=0, keepdims=True)
        dy = err * (1.0 / D)
        dgf_ref[...] += jnp.sum(dy * xh, axis=0, keepdims=True)
        dxh = dy * gf_ref[...]
        dx1 = r * (dxh - xh * jnp.mean(dxh * xh, axis=-1, keepdims=True))
        dx1_ref[...] = dx1
        dout16 = (gate_ref[...] * dx1).astype(BF16)
        dout_ref[...] = dout16
        dgate_ref[...] += jnp.sum(dx1 * out_, axis=0, keepdims=True)
        dwo_ref[...] += _tn(mg, dout16)

    r512 = pl.BlockSpec((TOK, 512), lambda i: (i, 0))
    row = pl.BlockSpec((TOK, D), lambda i: (i, 0))
    vec = _full((1, D))
    return _pcall(
        body, name="tail_fwd", grid=(m // TOK,),
        in_specs=[r512, r512, pl.BlockSpec((TOK, 512), lambda i: (i, 6)), pl.BlockSpec((TOK, 512), lambda i: (i, 7)),
                  pl.BlockSpec((TOK, 512), lambda i: (i, 8)), row, pl.BlockSpec((TOK, D), lambda i: (i, 1)),
                  pl.BlockSpec((TOK, 256), lambda i: (i, 0)), pl.BlockSpec((2, TOK, 128), lambda i: (0, i, 0)), row, row,
                  _full((2, AC, AC)), _full((2, AC, 1)), _full((1, 512)), _resident((512, D)), _resident((512, D)),
                  _resident((D, D)), vec, vec],
        out_specs=[r512, r512, pl.BlockSpec((TOK, 256), lambda i: (i, 0)), _resident((D, D)), row, row, _full((1, 128)), vec, vec],
        out_shape=[SDS((m, 512), BF16), SDS((m, 512), BF16), SDS((m, 256), F32), SDS((D, D), F32), SDS((m, D), F32),
                   SDS((m, D), BF16), SDS((1, 128), F32), SDS((1, D), F32), SDS((1, D), F32)],
        compiler_params=_cp(("arbitrary",), VMEM_BIG),
    )(o_f, o_b, p, p, p, p, p, vnr, svc, x, tgt, ws01, bs01, gbn, wpa, wpb, wo, gate, gf)


DPR = 3072


def _tail_bwd(dout, ya_in, yb_in, p, svr, svc, o_f, o_b, gbn, wo, wpa, wpb):
    m = p.shape[0]

    def body(dout_ref, ya_ref, yb_ref, ga_ref, gb_ref, zb_ref, ua_ref, za_ref, svr_ref, svc_ref, of_ref, ob_ref, g_ref,
             wo_ref, wpa_ref, wpb_ref,
             dwpa_ref, dwpb_ref, dpg_ref, dpr_ref, dsr_ref, dsc_ref, do_ref, dg_ref):
        i = pl.program_id(0)

        @pl.when(i == 0)
        def _():
            dg_ref[...] = jnp.zeros_like(dg_ref)
            dwpa_ref[...] = jnp.zeros_like(dwpa_ref)
            dwpb_ref[...] = jnp.zeros_like(dwpb_ref)

        dm_ = _nt(dout_ref[...], wo_ref[...])
        ya_in, yb_in = ya_ref[...], yb_ref[...]
        ya = _nn(ya_in, wpa_ref[...])
        yb = _nn(yb_in, wpb_ref[...])
        sa = jax.nn.sigmoid(ga_ref[...].astype(F32))
        sb = jax.nn.sigmoid(gb_ref[...].astype(F32))
        dya16 = (dm_ * sa).astype(BF16)
        dyb16 = (dm_ * sb).astype(BF16)
        dwpa, dwpb = _tn(ya_in, dya16), _tn(yb_in, dyb16)
        for s in range(4):
            dwpa_ref[s] += dwpa[:, 256 * s:256 * (s + 1)]
            dwpb_ref[s] += dwpb[:, 256 * s:256 * (s + 1)]
        dpg_ref[:, 0:D] = (dm_ * ya * (sa * (1.0 - sa))).astype(BF16)
        dpg_ref[:, D:2 * D] = (dm_ * yb * (sb * (1.0 - sb))).astype(BF16)
        dya = _nt(dya16, wpa_ref[...])
        dyb = _nt(dyb16, wpb_ref[...])

        u = ua_ref[...].astype(F32)
        za = za_ref[...].astype(F32)
        sz, dsz = _silu_and_grad(za)
        sv = jnp.concatenate([svr_ref[...], svc_ref[0], svc_ref[1]], axis=1)
        dpr_ref[:, 512:1024] = (dya * sv * sz).astype(BF16)
        dsv = dya * u * sz
        dsr_ref[...] = dsv[:, 0:256]
        dsc_ref[0] = dsv[:, 256:384]
        dsc_ref[1] = dsv[:, 384:512]
        dpr_ref[:, 1024:1536] = (dya * u * sv * dsz).astype(BF16)

        zb = zb_ref[...].astype(F32)
        o = of_ref[...] + ob_ref[...]
        szb, dszb = _silu_and_grad(zb)
        for h, (r, xh) in enumerate(_head_norm(o)):
            sl = slice(128 * h, 128 * h + 128)
            gh = g_ref[:, sl]
            don = dyb[:, sl] * szb[:, sl]
            dpr_ref[:, sl] = (dyb[:, sl] * (xh * gh) * dszb[:, sl]).astype(BF16)
            dg_ref[:, sl] += jnp.sum(don * xh, axis=0, keepdims=True)
            dxh = don * gh
            do_ref[:, sl] = (r * (dxh - xh * jnp.mean(dxh * xh, axis=-1, keepdims=True))).astype(BF16)

    r512 = pl.BlockSpec((TOK, 512), lambda i: (i, 0))
    row = pl.BlockSpec((TOK, D), lambda i: (i, 0))
    return _pcall(
        body, name="tail_bwd", grid=(m // TOK,),
        in_specs=[row, r512, r512, row, pl.BlockSpec((TOK, D), lambda i: (i, 1)), pl.BlockSpec((TOK, 512), lambda i: (i, 6)),
                  pl.BlockSpec((TOK, 512), lambda i: (i, 7)), pl.BlockSpec((TOK, 512), lambda i: (i, 8)),
                  pl.BlockSpec((TOK, 256), lambda i: (i, 0)), pl.BlockSpec((2, TOK, 128), lambda i: (0, i, 0)), r512, r512,
                  _full((1, 512)), _resident((D, D)), _resident((512, D)), _resident((512, D))],
        out_specs=[_resident((4, 512, 256)), _resident((4, 512, 256)), pl.BlockSpec((TOK, 2 * D), lambda i: (i, 0)),
                   pl.BlockSpec((TOK, 1536), lambda i: (i, 0)),
                   pl.BlockSpec((TOK, 256), lambda i: (i, 0)), pl.BlockSpec((2, TOK, 128), lambda i: (0, i, 0)), r512, _full((1, 512))],
        out_shape=[SDS((4, 512, 256), F32), SDS((4, 512, 256), F32), SDS((m, 2 * D), BF16), SDS((m, DPR), BF16), SDS((m, 256), F32),
                   SDS((2, m, 128), F32), SDS((m, 512), BF16), SDS((1, 512), F32)],
        compiler_params=_cp(("arbitrary",), VMEM_BIG),
    )(dout, ya_in, yb_in, p, p, p, p, p, svr, svc, o_f, o_b, gbn, wo, wpa, wpb)


def _ln_bwd(dsr, vnr, dvnc, p, ws01t, ln_g, dp):
    m = p.shape[0]

    def body(dsr_ref, vnr_ref, dvc_ref, va_ref, wt_ref, g_ref, dpi_ref, dp_ref, dw_ref, db_ref, dlg_ref, dlb_ref, dvn_ref):
        i = pl.program_id(0)

        @pl.when(i == 0)
        def _():
            dw_ref[...] = jnp.zeros_like(dw_ref)
            db_ref[...] = jnp.zeros_like(db_ref)
            dlg_ref[...] = jnp.zeros_like(dlg_ref)
            dlb_ref[...] = jnp.zeros_like(dlb_ref)

        for j in range(TOK // AC):
            for g in range(2):
                d = dsr_ref[AC * j:AC * j + AC, AC * g:AC * g + AC]
                d16 = d.astype(BF16)
                dvn_ref[AC * j:AC * j + AC, AC * g:AC * g + AC] = _nn(wt_ref[g], d16)
                dw_ref[g] += _nt(d16, vnr_ref[AC * j:AC * j + AC, AC * g:AC * g + AC])
                db_ref[g] += jnp.sum(d, axis=1, keepdims=True)
        dvn_ref[:, 256:384] = dvc_ref[0]
        dvn_ref[:, 384:512] = dvc_ref[1]
        dvn = dvn_ref[...]
        xf = va_ref[...].astype(F32)
        xc = xf - jnp.mean(xf, axis=-1, keepdims=True)
        rs = lax.rsqrt(jnp.mean(xc * xc, axis=-1, keepdims=True) + EPS)
        xh = xc * rs
        dlg_ref[...] += jnp.sum(dvn * xh, axis=0, keepdims=True)
        dlb_ref[...] += jnp.sum(dvn, axis=0, keepdims=True)
        dxh = dvn * g_ref[...]
        dva = rs * (dxh - jnp.mean(dxh, axis=-1, keepdims=True) - xh * jnp.mean(dxh * xh, axis=-1, keepdims=True))
        dp_ref[...] = dva.astype(BF16)

    return _pcall(
        body, name="ln_bwd", grid=(m // TOK,),
        in_specs=[pl.BlockSpec((TOK, 256), lambda i: (i, 0)), pl.BlockSpec((TOK, 256), lambda i: (i, 0)),
                  pl.BlockSpec((2, TOK, 128), lambda i: (0, i, 0)), pl.BlockSpec((TOK, 512), lambda i: (i, 9)),
                  _full((2, AC, AC)), _full((1, 512)), pl.BlockSpec(memory_space=pl.ANY)],
        out_specs=[pl.BlockSpec((TOK, 512), lambda i: (i, 3)), _full((2, AC, AC)), _full((2, AC, 1)), _full((1, 512)), _full((1, 512))],
        out_shape=[SDS((m, DPR), BF16), SDS((2, AC, AC), F32), SDS((2, AC, 1), F32), SDS((1, 512), F32), SDS((1, 512), F32)],
        scratch_shapes=[pltpu.VMEM((TOK, 512), F32)],
        input_output_aliases={6: 0}, compiler_params=_cp(("arbitrary",)),
    )(dsr, vnr, dvnc, p, ws01t, ln_g, dp)


def _tri_mm(tri, a):
    a1 = a.astype(BF16)
    r1 = a - a1.astype(F32)
    a2 = r1.astype(BF16)
    a3 = (r1 - a2.astype(F32)).astype(BF16)
    n = a.shape[1]
    r = _nn(tri, jnp.concatenate([a1, a2, a3], axis=1))
    return r[:, 0:n] + r[:, n:2 * n] + r[:, 2 * n:3 * n]


def _gla_masks(reverse):
    ri = lax.broadcasted_iota(jnp.int32, (CH, CH), 0)
    ci = lax.broadcasted_iota(jnp.int32, (CH, CH), 1)
    vis = (ci >= ri) if reverse else (ci <= ri)
    vis_t = (ci <= ri) if reverse else (ci >= ri)
    r4 = lax.broadcasted_iota(jnp.int32, (4 * CH, CH), 0) & (CH - 1)
    c4 = lax.broadcasted_iota(jnp.int32, (4 * CH, CH), 1)
    vis4 = (c4 >= r4) if reverse else (c4 <= r4)
    vis4_t = (c4 <= r4) if reverse else (c4 >= r4)
    lane = lax.broadcasted_iota(jnp.int32, (1, 256), 1)
    hm = [(lane >= CH * h) & (lane < CH * h + CH) for h in range(4)]
    return vis, vis_t, vis4, vis4_t, hm


def _stack_heads(x, hm):
    return jnp.concatenate([jnp.where(hm[h], x, 0.0).astype(BF16) for h in range(4)], axis=0)


def _diag_heads(full, hm):
    r = full.shape[0] // 4
    acc = jnp.where(hm[0], full[0:r], 0.0)
    for h in range(1, 4):
        acc = acc + jnp.where(hm[h], full[r * h:r * h + r], 0.0)
    return acc


def _rows_of_heads(x):
    return jnp.concatenate([x[:, 128 * h:128 * h + 128] for h in range(4)], axis=0)


def _lane_vis(reverse, transpose):
    ri = lax.broadcasted_iota(jnp.int32, (CH, 4 * CH), 0)
    ci = lax.broadcasted_iota(jnp.int32, (CH, 4 * CH), 1) & (CH - 1)
    return (ci >= ri) if (reverse != transpose) else (ci <= ri)


def _gla_fwd2(p, qkv_blk, lr, lrws, gbiases, s0s, name):
    m = p.shape[0]
    tb = min(GLA_TB, m)
    nb = m // tb
    nc = tb // CH

    def body(qkv_f, lr_f, qkv_b, lr_b, lrw_f, lrw_b, gb_f, gb_b, s0_f, s0_b,
             o_f, sb_f, sfin_f, o_b, sb_b, sfin_b, st_f, st_b):
        i = pl.program_id(0)

        @pl.when(i == 0)
        def _():
            st_f[...] = s0_f[...]
            st_b[...] = s0_b[...]

        dirs = []
        for reverse, qkv_ref, lr_ref, lrw_ref, gb_ref, o_ref, sb_ref, st_ref in (
                (False, qkv_f, lr_f, lrw_f, gb_f, o_f, sb_f, st_f), (True, qkv_b, lr_b, lrw_b, gb_b, o_b, sb_b, st_b)):
            vis, _, vis4, _, hm = _gla_masks(reverse)
            logits = _nn(lr_ref[...].astype(BF16), lrw_ref[...]) + gb_ref[...]
            dirs.append(dict(reverse=reverse, qkv=qkv_ref, o=o_ref, sb=sb_ref, vis4=vis4, hm=hm,
                             tri=vis.astype(F32).astype(BF16), a=_logsig(logits) * (1.0 / 16.0), st=st_ref[...]))
        for step in range(nc):
            for d in dirs:
                c = nc - 1 - step if d["reverse"] else step
                rows = slice(CH * c, CH * c + CH)
                b = _tri_mm(d["tri"], d["a"][rows])
                bl = b[0:1] if d["reverse"] else b[CH - 1:CH]
                q = d["qkv"][rows, 0:256].astype(F32) * 0.125
                k = d["qkv"][rows, 256:512].astype(F32)
                v16 = d["qkv"][rows, 512:1024].astype(BF16)
                qd = q * jnp.exp(b)
                kd16 = (k * jnp.exp(-b)).astype(BF16)
                kdec16 = (k * jnp.exp(bl - b)).astype(BF16)
                qstack = _stack_heads(qd, d["hm"])
                sc = jnp.where(d["vis4"], _nt(qstack, kd16), 0.0).astype(BF16)
                inter = _nt(qstack, d["st"].astype(BF16))
                for h in range(4):
                    d["o"][rows, 128 * h:128 * h + 128] = (
                        _nn(sc[CH * h:CH * h + CH], v16[:, 128 * h:128 * h + 128]) + inter[CH * h:CH * h + CH])
                d["sb"][c] = d["st"]
                d["st"] = d["st"] * jnp.exp(bl) + _diag_heads(_tn(v16, kdec16), d["hm"])
        st_f[...] = dirs[0]["st"]
        st_b[...] = dirs[1]["st"]

        @pl.when(i == nb - 1)
        def _():
            sfin_f[...] = dirs[0]["st"]
            sfin_b[...] = dirs[1]["st"]

    fw = lambda i: i
    bw = lambda i: nb - 1 - i
    in_specs = []
    for rm in (fw, bw):
        in_specs += [pl.BlockSpec((tb, 1024), lambda i, rm=rm: (rm(i), qkv_blk)), pl.BlockSpec((tb, LRW), lambda i, rm=rm: (rm(i), 0))]
    in_specs += [_full((LRW, 256))] * 2 + [_full((1, 256))] * 2 + [_full((128, 256))] * 2
    out_specs, out_shape = [], []
    for rm in (fw, bw):
        out_specs += [pl.BlockSpec((tb, 512), lambda i, rm=rm: (rm(i), 0)), pl.BlockSpec((nc, 128, 256), lambda i, rm=rm: (rm(i), 0, 0)),
                      _full((128, 256))]
        out_shape += [SDS((m, 512), F32), SDS((m // CH, 128, 256), F32), SDS((128, 256), F32)]
    return _pcall(
        body, name=name, grid=(nb,), in_specs=in_specs, out_specs=out_specs, out_shape=out_shape,
        scratch_shapes=[pltpu.VMEM((128, 256), F32), pltpu.VMEM((128, 256), F32)], compiler_params=_cp(("arbitrary",), VMEM_BIG),
    )(p, lr, p, lr, lrws[0], lrws[1], gbiases[0], gbiases[1], s0s[0], s0s[1])


def _gla_bwd(p, qkv_blk, lr, lrw, lrwt, gbias, sb, dsfin, do, prev, dp, *, reverse, name):
    m = p.shape[0]
    tb = min(GLA_TB, m)
    nb = m // tb
    nc = tb // CH
    rmap = (lambda i: i) if reverse else (lambda i: nb - 1 - i)
    has_prev = prev is not None
    has_dp = dp is not None

    def body(*refs):
        refs = list(refs)
        qkv_ref, lr_ref, lrw_ref, lrwt_ref, gb_ref, sb_ref, dsfin_ref, do_ref = refs[:8]
        refs = refs[8:]
        if has_prev:
            pq_ref, plr_ref = refs[:2]
            refs = refs[2:]
        if has_dp:
            refs = refs[1:]
        dqkv_ref, dlr_ref, dw2_ref, dgb_ref, ds0_ref, dst_ref, dlog_ref = refs
        i = pl.program_id(0)

        @pl.when(i == 0)
        def _():
            dst_ref[...] = dsfin_ref[...]
            dw2_ref[...] = jnp.zeros_like(dw2_ref)
            dgb_ref[...] = jnp.zeros_like(dgb_ref)

        vis, vis_t, vis4, vis4_t, hm = _gla_masks(reverse)
        tri = vis.astype(F32).astype(BF16)
        tri_t = vis_t.astype(F32).astype(BF16)
        lane_vis = _lane_vis(reverse, False)
        lane_vis_t = _lane_vis(reverse, True)
        lr16 = lr_ref[...].astype(BF16)
        logits = _nn(lr16, lrw_ref[...]) + gb_ref[...]
        a_all = _logsig(logits) * (1.0 / 16.0)
        dsig = (1.0 - jax.nn.sigmoid(logits)) * (1.0 / 16.0)
        dst = dst_ref[...]
        for c in (range(nc) if reverse else range(nc - 1, -1, -1)):
            rows = slice(CH * c, CH * c + CH)
            b = _tri_mm(tri, a_all[rows])
            bl = b[0:1] if reverse else b[CH - 1:CH]
            eb = jnp.exp(b)
            enb = jnp.exp(-b)
            ebl = jnp.exp(bl - b)
            el = jnp.exp(bl)
            q = qkv_ref[rows, 0:256].astype(F32) * 0.125
            k = qkv_ref[rows, 256:512].astype(F32)
            v16 = qkv_ref[rows, 512:1024].astype(BF16)
            do16 = do_ref[rows, :].astype(BF16)
            qd = q * eb
            kd = k * enb
            kdec = k * ebl
            st = sb_ref[c]
            st16 = st.astype(BF16)
            dst16 = dst.astype(BF16)
            qd16 = qd.astype(BF16)
            kd16 = kd.astype(BF16)
            qstack = _stack_heads(qd, hm)
            kstack = _stack_heads(kd, hm)
            kdecstack = _stack_heads(kdec, hm)
            pt = jnp.where(vis4_t, _nt(kstack, qd16), 0.0).astype(BF16)
            dvinter = _nt(kdecstack, dst16)
            do_rows = _rows_of_heads(do16)
            v_rows = _rows_of_heads(v16)
            dp_cat = jnp.where(lane_vis, _diag_heads(_nt(do_rows, v_rows), hm), 0.0).astype(BF16)
            dpt_cat = jnp.where(lane_vis_t, _diag_heads(_nt(v_rows, do_rows), hm), 0.0).astype(BF16)
            dqd = _nn(dp_cat, kstack) + _diag_heads(_nn(do_rows, st16), hm)
            dkd = _nn(dpt_cat, qstack)
            dkdec = _diag_heads(_nn(v_rows, dst16), hm)
            for h in range(4):
                rh = slice(CH * h, CH * h + CH)
                dv_h = _nn(pt[rh], do_rows[rh]) + dvinter[rh]
                if has_prev:
                    dv_h = dv_h + pq_ref[rows, 512 + 128 * h:512 + 128 * h + 128]
                dqkv_ref[rows, 512 + 128 * h:512 + 128 * h + 128] = dv_h.astype(dqkv_ref.dtype)
            dq = dqd * eb * 0.125
            dk = dkd * enb + dkdec * ebl
            if has_prev:
                dq = dq + pq_ref[rows, 0:256]
                dk = dk + pq_ref[rows, 256:512]
            dqkv_ref[rows, 0:256] = dq.astype(dqkv_ref.dtype)
            dqkv_ref[rows, 256:512] = dk.astype(dqkv_ref.dtype)
            g_kdec = dkdec * kdec
            db = dqd * qd - dkd * kd - g_kdec
            dbl = jnp.sum(g_kdec, axis=0, keepdims=True) + jnp.sum(st * dst, axis=0, keepdims=True) * el
            da = _tri_mm(tri_t, db) + dbl
            dlog_ref[rows, :] = da * dsig[rows]
            dst = dst * el + _diag_heads(_tn(do16, qd16), hm)
        dst_ref[...] = dst
        dlog = dlog_ref[...]
        dlog16 = dlog.astype(BF16)
        dlr = _nn(dlog16, lrwt_ref[...])
        if has_prev:
            dlr = dlr + plr_ref[...]
        dlr_ref[...] = dlr
        dw2_ref[...] += _tn(lr16, dlog16)
        dgb_ref[...] += jnp.sum(dlog, axis=0, keepdims=True)

        @pl.when(i == nb - 1)
        def _():
            ds0_ref[...] = dst

    in_specs = [pl.BlockSpec((tb,1024), lambda i: (rmap(i), qkv_blk)), pl.BlockSpec((tb,LRW), lambda i: (rmap(i), 0)),
                _full((LRW, 256)), _full((256, LRW)), _full((1, 256)), pl.BlockSpec((nc, 128, 256), lambda i: (rmap(i), 0, 0)),
                _full((128, 256)), pl.BlockSpec((tb,512), lambda i: (rmap(i), 0))]
    args = [p, lr, lrw, lrwt, gbias, sb, dsfin, do]
    if has_prev:
        in_specs += [pl.BlockSpec((tb,1024), lambda i: (rmap(i), 0)), pl.BlockSpec((tb,LRW), lambda i: (rmap(i), 0))]
        args += list(prev)
    aliases = {}
    if has_dp:
        in_specs.append(pl.BlockSpec(memory_space=pl.ANY))
        aliases = {len(args): 0}
        args.append(dp)
        dq_spec = pl.BlockSpec((tb,1024), lambda i: (rmap(i), 2))
        dq_shape = SDS(dp.shape, dp.dtype)
    else:
        dq_spec = pl.BlockSpec((tb,1024), lambda i: (rmap(i), 0))
        dq_shape = SDS((m, 1024), F32)
    return _pcall(
        body, name=name, grid=(nb,), in_specs=in_specs,
        out_specs=[dq_spec, pl.BlockSpec((tb,LRW), lambda i: (rmap(i), 0)), _full((LRW, 256)), _full((1, 256)), _full((128, 256))],
        out_shape=[dq_shape, SDS((m, LRW), F32), SDS((LRW, 256), F32), SDS((1, 256), F32), SDS((128, 256), F32)],
        scratch_shapes=[pltpu.VMEM((128, 256), F32), pltpu.VMEM((tb,256), F32)],
        input_output_aliases=aliases, compiler_params=_cp(("arbitrary",)),
    )(*args)


EARLY_KEYS = ["dmodc", "dscc", "dng_c", "dlng", "dlnb", "dws", "dbs", "dgbn", "dgf", "dw2", "dgb2", "loss", "dgate"]
EARLY_SHAPES = [(1, 2 * D), (D,), (1, D), (1, 512), (1, 512), (1, 4, 128, 128), (1, 4, 128), (1, 512), (D,), (2, 16, 256), (2, 256),
                (128,), (1, D)]
EARLY_SIZE = 2 * D + D + D + 512 + 512 + 4 * 128 * 128 + 512 + 512 + D + 2 * 16 * 256 + 512 + 128 + D
EARLY_ROWS = 648


def _device_step(x, c, ctx, c_ctx, tgt, wm, bm, ng, wit_g, wit_r, wlrt, ln_g, ln_b, ws, bs, w2, gb2, gbn, wpa, wpb, wo, gf,
                 exchange=None, shards=None):
    L = x.shape[0]
    wit_qkv = wit_r[2048:3072]
    ws16 = ws.astype(BF16)
    wst16 = jnp.swapaxes(ws, 1, 2).astype(BF16)
    bscol = bs[:, :, None]
    lrw = [jnp.zeros((LRW, 256), F32).at[16 * r:16 * r + 16].set(w2[r]).astype(BF16) for r in range(2)]
    lrwt = [w.T for w in lrw]
    gbias = [gb2[r:r + 1] for r in range(2)]

    cc = jnp.zeros((8, D), F32).at[0:1].set(c).at[1:2].set(c_ctx)
    mod = _modvec(cc, wm, bm)
    shift, scale, gate = mod[0:1, 0:D], mod[0:1, D:2 * D], mod[0:1, 2 * D:3 * D]
    shift_c, scale_c = mod[1:2, 0:D], mod[1:2, D:2 * D]

    hc = _prep_h(ctx, ng, scale_c, shift_c, "prep_hc")
    pc = _mm(hc, wit_qkv, tm=256, tn=1024, tk=D, out_dtype=F32, name="mm_pc", b_t=True)
    plrc = _mm(hc, wlrt, tm=256, tn=LRW, tk=D, out_dtype=F32, name="mm_plrc", b_t=True)
    zero_s = jnp.zeros((128, 256), F32)
    _, sbc_f, sc_f, _, sbc_b, sc_b = _gla_fwd2(pc, 0, plrc, lrw, gbias, (zero_s, zero_s), "gla_fwd_c")

    h, p, plr, vnr, vnc, late = _proj_fwd(x, ng, scale, shift, wit_g, wit_r, wlrt, ln_g, ln_b,
                                          shards if shards is not None else ())
    if shards is not None:
        me_xy = 2 * lax.axis_index("x") + lax.axis_index("y")
        g_wpa, g_wpb, g_wo = (_own(g, s_, me_xy) for g, s_ in zip(late, shards))
        wpa = jnp.swapaxes(g_wpa, 0, 1).reshape(512, D)
        wpb = jnp.swapaxes(g_wpb, 0, 1).reshape(512, D)
        wo = g_wo.reshape(D, D)
    o_f, sb_f, _, o_b, sb_b, _ = _gla_fwd2(p, 2, plr, lrw, gbias, (sc_f, sc_b), "gla_fwd")
    svc = _colmix_fwd(vnc.reshape(2, AC, L), ws16[2:4], bscol[2:4]).reshape(2, L, 128)
    ya_in, yb_in, svr, dwo, dx1, dout, loss, dgate, dgf = _tail_fwd(
        o_f, o_b, p, vnr, svc, x, tgt, ws16[0:2], bscol[0:2], gbn, wpa, wpb, wo, gate, gf)

    dwpa, dwpb, dp_g, dp, dsr, dsc, do, dgbn = _tail_bwd(dout, ya_in, yb_in, p, svr, svc, o_f, o_b, gbn, wo, wpa, wpb)
    dvnc, dws23, dbs23 = _colmix_bwd(dsc.reshape(2, AC, L), vnc.reshape(2, AC, L), wst16[2:4])
    dp, dws01, dbs01, dlng, dlnb = _ln_bwd(dsr, vnr, dvnc.reshape(2, L, 128), p, wst16[0:2], ln_g, dp)
    zero_ds = jnp.zeros((128, 256), F32)
    dqkv_f, dlr_f, dw2_f, dgb_f, ds0_f = _gla_bwd(p, 2, plr, lrw[0], lrwt[0], gbias[0], sb_f, zero_ds, do, None, None,
                                                  reverse=False, name="gla_bwd_f")
    dp, dlr, dw2_b, dgb_b, ds0_b = _gla_bwd(p, 2, plr, lrw[1], lrwt[1], gbias[1], sb_b, zero_ds, do, (dqkv_f, dlr_f), dp,
                                            reverse=True, name="gla_bwd_b")
    zero_do = jnp.zeros((ctx.shape[0], 512), BF16)
    dqkvc_f, dlrc_f, dw2c_f, dgbc_f, _ = _gla_bwd(pc, 0, plrc, lrw[0], lrwt[0], gbias[0], sbc_f, ds0_f, zero_do, None, None,
                                                  reverse=False, name="gla_bwd_cf")
    dqkvc, dlrc, dw2c_b, dgbc_b, _ = _gla_bwd(pc, 0, plrc, lrw[1], lrwt[1], gbias[1], sbc_b, ds0_b, zero_do,
                                              (dqkvc_f, dlrc_f), None, reverse=True, name="gla_bwd_cb")
    dhc = _mm(dqkvc, wit_qkv, tm=256, tn=D, tk=1024, out_dtype=F32, name="mm_dhc")
    dhc = _mm(dlrc, wlrt, tm=256, tn=D, tk=LRW, out_dtype=F32, name="mm_dhc_lr", acc=dhc)
    _, dng_c, dscale_c, dshift_c = _prep_bwd(ctx, dhc, None, ng, scale_c, "prep_bwd_c")

    if exchange is not None:
        ready = exchange[0](dwpa, dwpb, dwo)
        dwit_g, ready_other = _mm_tn(dp_g, h, ta=1024, tn=D, tk=2048, name="mm_dwi_g", out_dtype=BF16, swap=ready)
    else:
        dwit_g = _mm_tn(dp_g, h, ta=1024, tn=D, tk=2048, name="mm_dwi_g", out_dtype=BF16)
    dwit_r, dwlrt = _mm_tn(dp, h, ta=1024, tn=D, tk=2048, name="mm_dwi_r", out_dtype=BF16, extra=dlr)
    dwit_qkv = _mm_tn(dqkvc, hc, ta=1024, tn=D, tk=256, name="mm_dwi_c", acc=dwit_r[2048:3072], out_dtype=BF16)
    dwlrt = _mm_tn(dlrc, hc, ta=LRW, tn=D, tk=256, name="mm_dwlr_c", acc=dwlrt, out_dtype=BF16)
    big = dict(dwit_g=dwit_g, dwit_r=dwit_r, dwit_qkv=dwit_qkv, dwlrt=dwlrt, dwpa=dwpa, dwpb=dwpb, dwo=dwo)

    dmodc = jnp.concatenate([dshift_c, dscale_c], axis=1)
    dscc = _dcctx(jnp.zeros((8, 2 * D), F32).at[0:1].set(dmodc), wm)[0:1]
    dw2p = dw2_f + dw2c_f, dw2_b + dw2c_b
    small = dict(
        dmodc=dmodc, dscc=dscc, dng_c=dng_c, dlng=dlng, dlnb=dlnb, dws=jnp.concatenate([dws01, dws23], axis=0),
        dbs=jnp.concatenate([dbs01, dbs23], axis=0)[:, :, 0], dgbn=dgbn, dgf=dgf,
        dw2=jnp.stack([dw2p[0][0:16], dw2p[1][16:32]]), dgb2=jnp.concatenate([dgb_f + dgbc_f, dgb_b + dgbc_b], axis=0),
        loss=loss[0, 0], dgate=dgate)

    send = exchange[1](big, ready, ready_other) if exchange is not None else ()
    early = _pack([small[k] for k in EARLY_KEYS[:-2]] + [jnp.broadcast_to(small["loss"], (128,)), small["dgate"]], EARLY_ROWS) \
        if exchange is not None else None
    (dx, dng, dscale, dshift), got, early_all = _proj_bwd(dp_g, dp, dlr, wit_g, wit_r, wlrt, x, dx1, ng, scale, send, early)
    return dict(dx=dx, got=got, early=early, early_all=early_all, dshift=dshift, dscale=dscale, dng_lat=dng, **big, **small)


ANY = pl.BlockSpec(memory_space=pl.ANY)


def _coords():
    return lax.axis_index("x"), lax.axis_index("y"), lax.axis_index("c")


def _flip(v, bit):
    return 1 - v if bit else v


def _remote(src, dst, send_sem, recv_sem, dev):
    return pltpu.make_async_remote_copy(src_ref=src, dst_ref=dst, send_sem=send_sem, recv_sem=recv_sem,
                                        device_id=dev, device_id_type=MESH)


def _own(out, block, idx):
    return lax.dynamic_update_slice_in_dim(out, block[None], idx, axis=0)


def _half_idx(shape, axis, which, lead=()):
    idx = [pl.ds(0, d) for d in shape]
    h = shape[axis] // 2
    idx[axis] = pl.ds(which * h, h)
    return tuple(lead) + tuple(idx)


def _gather_weights(split, whole, name):
    ns, nw = len(split), len(whole)
    n = ns + nw
    arrs = [a for a, _ in split] + list(whole)

    def body(*refs):
        ins, outs = refs[:n], refs[n:2 * n]
        a_send, a_recv, b_send, b_recv = refs[2 * n:]
        x, y, c = _coords()
        me = 2 * x + y
        sib = (x, y, 1 - c)
        peers = [(1 - x, y), (x, 1 - y), (1 - x, 1 - y)]

        def half(k, slot, which):
            return outs[k].at[_half_idx(arrs[k].shape, split[k][1], which, lead=(slot,))]

        sends = []
        for k in range(n):
            for j, (px, py) in enumerate(peers):
                if k < ns:
                    rc = _remote(ins[k].at[_half_idx(arrs[k].shape, split[k][1], c)], half(k, me, c), a_send.at[3 * k + j],
                                 a_recv.at[3 * k + j], (px, py, c))
                else:
                    rc = _remote(ins[k], outs[k].at[me], a_send.at[3 * k + j], a_recv.at[3 * k + j], (px, py, c))
                rc.start()
                sends.append(rc)
        for k in range(ns):
            for j, (px, py) in enumerate(peers):
                landed = half(k, 2 * px + py, c)
                _remote(landed, landed, a_send.at[3 * k + j], a_recv.at[3 * k + j], (px, py, c)).wait_recv()
                fw = _remote(landed, landed, b_send.at[3 * k + j], b_recv.at[3 * k + j], sib)
                fw.start()
                sends.append(fw)
        for k in range(ns, n):
            for j, (px, py) in enumerate(peers):
                landed = outs[k].at[2 * px + py]
                _remote(landed, landed, a_send.at[3 * k + j], a_recv.at[3 * k + j], (px, py, c)).wait_recv()
        for k in range(ns):
            for j, (px, py) in enumerate(peers):
                passed = half(k, 2 * px + py, 1 - c)
                _remote(passed, passed, b_send.at[3 * k + j], b_recv.at[3 * k + j], sib).wait_recv()
        for rc in sends:
            rc.wait_send()

    outs = _pcall(
        body, name=name, in_specs=[ANY] * n, out_specs=[ANY] * n,
        out_shape=[SDS((4,) + a.shape, a.dtype) for a in arrs],
        scratch_shapes=[pltpu.SemaphoreType.DMA((3 * n,)), pltpu.SemaphoreType.DMA((3 * n,)), pltpu.SemaphoreType.DMA((3 * ns,)),
                        pltpu.SemaphoreType.DMA((3 * ns,))],
    )(*arrs)
    me_xy = 2 * lax.axis_index("x") + lax.axis_index("y")
    return [_own(o, a, me_xy) for o, a in zip(outs, arrs)]


def _gather_all(a, swap, name):
    masks = [(mx, my, mc) for mx in range(2) for my in range(2) for mc in range(2)][1:]
    n = len(swap)

    def body(*refs):
        in_ref, sw_in = refs[0], refs[1:1 + n]
        out_ref, sw_out = refs[1 + n], refs[2 + n:2 + 2 * n]
        send_sems, recv_sems = refs[2 + 2 * n:]
        x, y, c = _coords()
        me = 4 * x + 2 * y + c
        sends = []
        for j, (mx, my, mc) in enumerate(masks):
            rc = _remote(in_ref, out_ref.at[me], send_sems.at[j], recv_sems.at[j], (_flip(x, mx), _flip(y, my), _flip(c, mc)))
            rc.start()
            sends.append(rc)
        for k in range(n):
            rc = _remote(sw_in[k], sw_out[k], send_sems.at[7 + k], recv_sems.at[7 + k], (x, y, 1 - c))
            rc.start()
            sends.append(rc)
        for j, (mx, my, mc) in enumerate(masks):
            px, py, pc = _flip(x, mx), _flip(y, my), _flip(c, mc)
            landed = out_ref.at[4 * px + 2 * py + pc]
            _remote(landed, landed, send_sems.at[j], recv_sems.at[j], (px, py, pc)).wait_recv()
        for k in range(n):
            _remote(sw_out[k], sw_out[k], send_sems.at[7 + k], recv_sems.at[7 + k], (x, y, 1 - c)).wait_recv()
        for rc in sends:
            rc.wait_send()

    res = _pcall(
        body, name=name, in_specs=[ANY] * (1 + n), out_specs=[ANY] * (1 + n),
        out_shape=[SDS((8,) + a.shape, a.dtype)] + [SDS(s_.shape, s_.dtype) for s_ in swap],
        scratch_shapes=[pltpu.SemaphoreType.DMA((7 + n,)), pltpu.SemaphoreType.DMA((7 + n,))],
    )(a, *swap)
    return _own(res[0], a, 4 * lax.axis_index("x") + 2 * lax.axis_index("y") + lax.axis_index("c")), list(res[1:])


def _half_shape(shape, axis):
    return tuple(d // 2 if i == axis else d for i, d in enumerate(shape))


def _swap_half_c(arrs, axes, name):
    n = len(arrs)

    def body(*refs):
        ins, outs = refs[:n], refs[n:2 * n]
        send_sems, recv_sems = refs[2 * n:]
        x, y, c = _coords()
        sends = []
        for k in range(n):
            rc = _remote(ins[k].at[_half_idx(arrs[k].shape, axes[k], 1 - c)], outs[k], send_sems.at[k], recv_sems.at[k],
                         (x, y, 1 - c))
            rc.start()
            sends.append(rc)
        for rc in sends:
            rc.wait()

    return _pcall(
        body, name=name, in_specs=[ANY] * n, out_specs=[ANY] * n,
        out_shape=[SDS(_half_shape(a.shape, ax), a.dtype) for a, ax in zip(arrs, axes)],
        scratch_shapes=[pltpu.SemaphoreType.DMA((n,)), pltpu.SemaphoreType.DMA((n,))],
    )(*arrs)


def _pair_sum(a, got, cidx, axis, name):
    _, r, cdim = a.shape
    hshape = _half_shape(a.shape, axis)

    def body(c_ref, a_ref, g_ref, o_ref):
        o_ref[...] = (a_ref[...].astype(F32) + g_ref[...].astype(F32)).astype(BF16)

    if axis == 1:
        tr = min(r // 2, 256)
        nj = (r // 2) // tr
        blk = pl.BlockSpec((1, tr, cdim), lambda s, j, c: (s, j, 0))
        a_spec = pl.BlockSpec((1, tr, cdim), lambda s, j, c: (s, c[0] * nj + j, 0))
    else:
        nj, hw = 1, cdim // 2
        blk = pl.BlockSpec((1, r, hw), lambda s, j, c: (s, 0, 0))
        a_spec = pl.BlockSpec((1, r, hw), lambda s, j, c: (s, 0, c[0]))
    return _pcall(
        body, name=name, out_shape=SDS(hshape, BF16),
        grid_spec=pltpu.PrefetchScalarGridSpec(num_scalar_prefetch=1, grid=(4, nj), in_specs=[a_spec, blk], out_specs=blk),
        compiler_params=_cp(("parallel", "parallel"), VMEM_BIG),
    )(cidx, a, got)


def _sum_chips(parts, name):
    _, h, cdim = parts.shape

    def body(p_ref, o_ref):
        acc = p_ref[0].astype(F32)
        for k in range(1, 4):
            acc = acc + p_ref[k].astype(F32)
        o_ref[...] = acc

    if h % 256 == 0 or h in (128,):
        tr = min(h, 256)
        grid, in_spec, out_spec = (h // tr,), pl.BlockSpec((4, tr, cdim), lambda i: (0, i, 0)), pl.BlockSpec((tr, cdim), lambda i: (i, 0))
    else:
        lw = 256
        grid, in_spec, out_spec = (cdim // lw,), pl.BlockSpec((4, h, lw), lambda i: (0, 0, i)), pl.BlockSpec((h, lw), lambda i: (0, i))
    return _pcall(
        body, name=name, grid=grid, in_specs=[in_spec], out_specs=out_spec, out_shape=SDS((h, cdim), F32),
        compiler_params=_cp(("parallel",), VMEM_BIG),
    )(parts)


def _sum_slots(a, name, rows):
    s, n, _ = a.shape

    def body(a_ref, o_ref):
        acc = a_ref[0]
        for k in range(1, s):
            acc = acc + a_ref[k]
        o_ref[...] = acc

    return _pcall(
        body, name=name, grid=(n // rows,), in_specs=[pl.BlockSpec((s, rows, 128), lambda i: (0, i, 0))],
        out_specs=pl.BlockSpec((rows, 128), lambda i: (i, 0)), out_shape=SDS((n, 128), F32),
        compiler_params=_cp(("parallel",)),
    )(a)


def _adam_math(w, g, m, v):
    nm = ADAM_B1 * m + (1.0 - ADAM_B1) * g
    nv = ADAM_B2 * v + (1.0 - ADAM_B2) * (g * g)
    m_hat = nm / (1.0 - ADAM_B1 ** ADAM_STEP)
    v_hat = nv / (1.0 - ADAM_B2 ** ADAM_STEP)
    return -ADAM_LR * (m_hat / (jnp.sqrt(v_hat) + ADAM_EPS) + ADAM_WD * w), nm, nv


def _adamw(w, g, m, v, name, rows):
    r, cdim = w.shape

    def body(w_ref, g_ref, m_ref, v_ref, d_ref, nm_ref, nv_ref):
        d_ref[...], nm_ref[...], nv_ref[...] = _adam_math(w_ref[...], g_ref[...], m_ref[...], v_ref[...])

    blk = pl.BlockSpec((rows, cdim), lambda i: (i, 0))
    return _pcall(
        body, name=name, grid=(r // rows,), in_specs=[blk] * 4, out_specs=[blk] * 3,
        out_shape=[SDS(w.shape, F32)] * 3, compiler_params=_cp(("parallel",)),
    )(w, g, m, v)


def _adamw_joined(w, mine, other, m, v, cidx, axis, name, rows):
    r, cdim = w.shape
    if axis == 0:
        rows = r

    def body(c_ref, w_ref, a_ref, b_ref, m_ref, v_ref, g_ref, d_ref, nm_ref, nv_ref):
        a, b = a_ref[...], b_ref[...]
        g = jnp.where(c_ref[0] == 0, jnp.concatenate([a, b], axis=axis), jnp.concatenate([b, a], axis=axis))
        g_ref[...] = g
        d_ref[...], nm_ref[...], nv_ref[...] = _adam_math(w_ref[...], g, m_ref[...], v_ref[...])

    blk = pl.BlockSpec((rows, cdim), lambda i, c: (i, 0))
    hshape = (rows // 2, cdim) if axis == 0 else (rows, cdim // 2)
    hblk = pl.BlockSpec(hshape, lambda i, c: (i, 0))
    return _pcall(
        body, name=name, out_shape=[SDS(w.shape, F32)] * 4,
        grid_spec=pltpu.PrefetchScalarGridSpec(num_scalar_prefetch=1, grid=(r // rows,), in_specs=[blk, hblk, hblk, blk, blk],
                                               out_specs=[blk] * 4),
        compiler_params=_cp(("parallel",)),
    )(cidx, w, mine, other, m, v)


def _adamw_many(ws, gs, ms, vs, name):
    n = len(ws)

    def body(*refs):
        outs = refs[4 * n:]
        for k in range(n):
            d, nm, nv = _adam_math(refs[k][...], refs[n + k][...], refs[2 * n + k][...], refs[3 * n + k][...])
            outs[k][...] = d
            outs[n + k][...] = nm
            outs[2 * n + k][...] = nv

    res = _pcall(body, name=name, out_shape=[SDS(w.shape, F32) for w in ws] * 3)(*ws, *gs, *ms, *vs)
    return res[:n], res[n:2 * n], res[2 * n:]


def _pack(pieces, rows):
    flat = jnp.concatenate([p.reshape(-1) for p in pieces])
    return jnp.pad(flat, (0, rows * 128 - flat.shape[0])).reshape(rows, 128)


def _unpack(buf, shapes):
    flat = buf.reshape(-1)
    out, off = [], 0
    for shp in shapes:
        size = 1
        for s in shp:
            size *= s
        out.append(flat[off:off + size].reshape(shp))
        off += size
    return out


LATE_ROWS = 32


def kernel(x, c, ctx, c_ctx, w_mod, b_mod, norm_g, w_in, a_ln_g, a_ln_b, a_ws, a_bs, b_gate_w2, b_gate_b, b_norm_g, w_proj_a, w_proj_b, w_out, final_norm_g, loss_target, m_c_ctx, m_w_mod, m_b_mod, m_norm_g, m_w_in, m_a_ln_g, m_a_ln_b, m_a_ws, m_a_bs, m_b_gate_w2, m_b_gate_b, m_b_norm_g, m_w_proj_a, m_w_proj_b, m_w_out, m_final_norm_g, v_c_ctx, v_w_mod, v_b_mod, v_norm_g, v_w_in, v_a_ln_g, v_a_ln_b, v_a_ws, v_a_bs, v_b_gate_w2, v_b_gate_b, v_b_norm_g, v_w_proj_a, v_w_proj_b, v_w_out, v_final_norm_g):
    xi, yi, ci = _coords()
    me_xy = 2 * xi + yi

    gate_pack = _pack([b_gate_w2[0], b_gate_b[0]], 24)
    w_in_t, m_w_in_t, v_w_in_t = (jnp.swapaxes(a[0], 0, 1) for a in (w_in, m_w_in, v_w_in))
    g_wit, g_wm, g_gate = _gather_weights([(w_in_t.astype(BF16), 1), (w_mod[0].astype(BF16), 0)], [gate_pack], "gather_weights")
    late_shards = (w_proj_a[0].astype(BF16), w_proj_b[0].astype(BF16), w_out[0].astype(BF16))
    wit_u = g_wit.reshape(4 * 1288, D)
    wit_g = wit_u[3104:5152]
    wit_r = jnp.concatenate([wit_u[1056:1568], wit_u[1568:2080], wit_u[2592:3104], wit_u[2080:2592], wit_u[0:1024]], axis=0)
    wlrt = jnp.pad(wit_u[1024:1056], ((0, LRW - 32), (0, 0)))
    wm = g_wm
    gflat = g_gate.reshape(4, 24 * 128)
    w2 = jnp.swapaxes(gflat[:, 0:2048].reshape(4, 2, 16, 64), 0, 2)
    w2 = jnp.swapaxes(w2, 0, 1).reshape(2, 16, 256)
    gb2 = jnp.swapaxes(gflat[:, 2048:2176].reshape(4, 2, 64), 0, 1).reshape(2, 256)

    tags = ["wi", "wpa", "wpb", "wo"]
    half_axes = [2, 1, 1, 1]
    sent = []

    def ready_blocks(dwpa, dwpb, dwo):
        return [dwpa, dwpb, dwo.reshape(4, 256, D)], half_axes[1:]

    def exchange(g, ready, ready_other):
        dwr = g["dwit_r"]
        dwit_u = jnp.concatenate([g["dwit_qkv"], g["dwlrt"][0:32], dwr[0:512], dwr[512:1024], dwr[1536:2048], dwr[1024:1536],
                                  g["dwit_g"]], axis=0)
        big = [dwit_u.reshape(4, 1288, D)] + ready[0]
        other = list(_swap_half_c(big[:1], half_axes[:1], "swap_half_in")) + ready_other
        cidx = jnp.reshape(ci, (1,)).astype(jnp.int32)
        sent.extend(_pair_sum(a, o, cidx, ax, "sum_pair_" + t) for a, o, ax, t in zip(big, other, half_axes, tags))
        return sent

    r = _device_step(x[0], c, ctx[0], c_ctx[None], loss_target[0], wm, b_mod, norm_g, wit_g, wit_r, wlrt, a_ln_g, a_ln_b,
                     a_ws[0], a_bs[0], w2, gb2, b_norm_g, None, None, None, final_norm_g[None], (ready_blocks, exchange), late_shards)

    parts = [_own(g, lax.dynamic_index_in_dim(s_, me_xy, axis=0, keepdims=False), me_xy) for g, s_ in zip(r["got"], sent)]
    halves = [_sum_chips(p_, "sum_chips_" + t) for p_, t in zip(parts, tags)]

    me8 = 4 * xi + 2 * yi + ci
    early_all = _own(r["early_all"], r["early"], me8)
    late = _pack([r["dshift"], r["dscale"], r["dng_lat"], c], LATE_ROWS)
    late_all, others = _gather_all(late, halves, "gather_small")
    s_early = _sum_slots(early_all, "sum_early", EARLY_ROWS // 3)
    s_late = _sum_slots(late_all, "sum_late", LATE_ROWS)
    (s_dmodc, s_dscc, s_dng_c, s_dlng, s_dlnb, s_dws, s_dbs, s_dgbn, s_dgf, s_dw2, s_dgb2, s_loss, s_dgate) = _unpack(
        s_early, EARLY_SHAPES)
    s_dshift, s_dscale, s_dng_lat, _ = _unpack(s_late, [(1, D)] * 4)
    s_dng = s_dng_lat + s_dng_c
    s_dmod = jnp.concatenate([s_dshift, s_dscale, s_dgate], axis=1)
    loss = s_loss[0]
    s_dmodc_p = jnp.pad(s_dmodc, ((0, 0), (0, D)))
    g_b_mod = s_dmod + s_dmodc_p
    sg = jax.nn.sigmoid(c_ctx)
    g_c_ctx = s_dscc * (sg * (1.0 + c_ctx * (1.0 - sg)))
    g_w2 = lax.dynamic_slice_in_dim(s_dw2, 64 * me_xy, 64, axis=2)[None]
    g_gb2 = lax.dynamic_slice_in_dim(s_dgb2, 64 * me_xy, 64, axis=1)[None]

    flat_l = late_all.reshape(8, LATE_ROWS * 128)
    dgate_all = early_all.reshape(8, EARLY_ROWS * 128)[:, EARLY_SIZE - D:EARLY_SIZE]
    dmod_all = jnp.concatenate([flat_l[:, 0:2 * D], dgate_all], axis=1)
    c_all = flat_l[:, 3 * D:4 * D]
    lhs = jnp.concatenate([_silu(c_all), _silu(c_ctx)[None], jnp.zeros((7, D), F32)], axis=0)
    rhs = jnp.concatenate([dmod_all, s_dmodc_p, jnp.zeros((7, 3 * D), F32)], axis=0)
    rhs = lax.dynamic_slice_in_dim(rhs, 768 * me_xy, 768, axis=1)
    g_w_mod = _mm(lhs.T.astype(BF16), rhs.astype(BF16), tm=D, tn=768, tk=16, out_dtype=F32, name="mm_dwm")

    cidx = jnp.reshape(ci, (1,)).astype(jnp.int32)
    g_w_in_t, d_w_in_t, nm_w_in_t, nv_w_in_t = _adamw_joined(w_in_t, halves[0], others[0], m_w_in_t, v_w_in_t, cidx, 1,
                                                             "adamw_w_in", 184)
    g_w_in, d_w_in, nm_w_in, nv_w_in = (jnp.swapaxes(a, 0, 1) for a in (g_w_in_t, d_w_in_t, nm_w_in_t, nv_w_in_t))
    g_wpa, d_wpa, nm_wpa, nv_wpa = _adamw_joined(w_proj_a[0], halves[1], others[1], m_w_proj_a[0], v_w_proj_a[0], cidx, 0,
                                                 "adamw_wpa", 0)
    g_wpb, d_wpb, nm_wpb, nv_wpb = _adamw_joined(w_proj_b[0], halves[2], others[2], m_w_proj_b[0], v_w_proj_b[0], cidx, 0,
                                                 "adamw_wpb", 0)
    g_wo, d_wo, nm_wo, nv_wo = _adamw_joined(w_out[0], halves[3], others[3], m_w_out[0], v_w_out[0], cidx, 0, "adamw_wo", 0)
    d_w_mod, nm_w_mod, nv_w_mod = _adamw(w_mod[0], g_w_mod, m_w_mod[0], v_w_mod[0], "adamw_w_mod", 256)

    names = ["c_ctx", "b_mod", "norm_g", "a_ln_g", "a_ln_b", "a_ws", "a_bs", "b_gate_w2", "b_gate_b", "b_norm_g", "final_norm_g"]
    ws_ = [c_ctx, b_mod, norm_g, a_ln_g, a_ln_b, a_ws, a_bs, b_gate_w2, b_gate_b, b_norm_g, final_norm_g]
    gs_ = [g_c_ctx, g_b_mod, s_dng, s_dlng, s_dlnb, s_dws, s_dbs, g_w2, g_gb2, s_dgbn, s_dgf]
    ms_ = [m_c_ctx, m_b_mod, m_norm_g, m_a_ln_g, m_a_ln_b, m_a_ws, m_a_bs, m_b_gate_w2, m_b_gate_b, m_b_norm_g, m_final_norm_g]
    vs_ = [v_c_ctx, v_b_mod, v_norm_g, v_a_ln_g, v_a_ln_b, v_a_ws, v_a_bs, v_b_gate_w2, v_b_gate_b, v_b_norm_g, v_final_norm_g]
    shapes = [w.shape for w in ws_]
    flat2 = [(1, 1024), (1, 3072), (1, 1024), (1, 512), (1, 512), (512, 128), (4, 128), (32, 64), (2, 64), (1, 512), (1, 1024)]
    as2d = lambda arrs: [a.reshape(s) for a, s in zip(arrs, flat2)]
    d_s, nm_s, nv_s = _adamw_many(as2d(ws_), as2d(gs_), as2d(ms_), as2d(vs_), "adamw_small")
    d_small = {n: a.reshape(s) for n, a, s in zip(names, d_s, shapes)}
    nm_small = {n: a.reshape(s) for n, a, s in zip(names, nm_s, shapes)}
    nv_small = {n: a.reshape(s) for n, a, s in zip(names, nv_s, shapes)}
    g_small = {n: g.reshape(s) for n, g, s in zip(names, gs_, shapes)}

    order = ["c_ctx", "w_mod", "b_mod", "norm_g", "w_in", "a_ln_g", "a_ln_b", "a_ws", "a_bs", "b_gate_w2", "b_gate_b", "b_norm_g",
             "w_proj_a", "w_proj_b", "w_out", "final_norm_g"]
    big_g = dict(w_mod=g_w_mod[None], w_in=g_w_in[None], w_proj_a=g_wpa[None], w_proj_b=g_wpb[None], w_out=g_wo[None])
    big_d = dict(w_mod=d_w_mod[None], w_in=d_w_in[None], w_proj_a=d_wpa[None], w_proj_b=d_wpb[None], w_out=d_wo[None])
    big_m = dict(w_mod=nm_w_mod[None], w_in=nm_w_in[None], w_proj_a=nm_wpa[None], w_proj_b=nm_wpb[None], w_out=nm_wo[None])
    big_v = dict(w_mod=nv_w_mod[None], w_in=nv_w_in[None], w_proj_a=nv_wpa[None], w_proj_b=nv_wpb[None], w_out=nv_wo[None])
    grads = [big_g[n] if n in big_g else g_small[n] for n in order]
    deltas = [big_d[n] if n in big_d else d_small[n] for n in order]
    new_m = [big_m[n] if n in big_m else nm_small[n] for n in order]
    new_v = [big_v[n] if n in big_v else nv_small[n] for n in order]
    return (loss, r["dx"][None], *grads, *deltas, *new_m, *new_v)
```
